```python
import jax, jax.numpy as jnp
from jax import lax
import numpy as np

D_MODEL = 1024
BATCH = 8
SEQ = 4096
DEPTH = 1

D_MIX = D_MODEL
SWA_WIDTH = D_MIX // 2
SWA_HEAD_DIM = 64
SWA_Q_HEADS = SWA_WIDTH // SWA_HEAD_DIM
SWA_KV_HEADS = 2
SWA_GROUP = SWA_Q_HEADS // SWA_KV_HEADS
WINDOW = 128
BLOCK = 128
ROPE_THETA = 500000.0
ROT_DIM = SWA_HEAD_DIM // 4
GLA_WIDTH = D_MIX - SWA_WIDTH
GLA_HEADS = 4
GLA_DK = GLA_WIDTH // 2 // GLA_HEADS
GLA_DV = GLA_WIDTH // GLA_HEADS
GLA_RANK = 16
GLA_TAU = 16.0
GLA_CHUNK = 64
IN_SPLITS = (
    SWA_Q_HEADS * SWA_HEAD_DIM,
    SWA_KV_HEADS * SWA_HEAD_DIM,
    SWA_KV_HEADS * SWA_HEAD_DIM,
    SWA_WIDTH,
    GLA_HEADS * GLA_DK,
    GLA_HEADS * GLA_DK,
    GLA_HEADS * GLA_DV,
    GLA_WIDTH,
    GLA_RANK,
)
D_IN_PROJ = sum(IN_SPLITS)
EPS = 1e-5
ALPHA = (2 * DEPTH) ** 0.25
BETA = (8 * DEPTH) ** -0.25

kernel_name = 'hymba_swa_sink_gla_deepnorm'


def split_cols(t, sizes):
    out, start = [], 0
    for s in sizes:
        out.append(t[..., start:start + s])
        start += s
    return out


def partial_rope(t, pos):
    half = ROT_DIM // 2
    inv_freq = ROPE_THETA ** (-jnp.arange(half, dtype=jnp.float32) / half)
    ang = pos.astype(jnp.float32)[..., None] * inv_freq
    cos = jnp.cos(ang)[:, :, None, :].astype(t.dtype)
    sin = jnp.sin(ang)[:, :, None, :].astype(t.dtype)
    t1 = t[..., :half]
    t2 = t[..., half:ROT_DIM]
    return jnp.concatenate([t1 * cos - t2 * sin, t2 * cos + t1 * sin, t[..., ROT_DIM:]], axis=-1)


def sliding_window_attention(q, k, v, sinks):
    B, S = q.shape[0], q.shape[1]
    nb = S // BLOCK
    qb = q.reshape(B, nb, BLOCK, SWA_KV_HEADS, SWA_GROUP, SWA_HEAD_DIM)

    def with_prev(t):
        tb = t.reshape(B, nb, BLOCK, SWA_KV_HEADS, SWA_HEAD_DIM)
        prev = jnp.concatenate([jnp.zeros_like(tb[:, :1]), tb[:, :-1]], axis=1)
        return jnp.concatenate([prev, tb], axis=2)

    kb = with_prev(k)
    vb = with_prev(v)
    scale = SWA_HEAD_DIM ** -0.5
    scores = jnp.einsum('bnqhgd,bnkhd->bnhgqk', qb, kb).astype(jnp.float32) * scale
    qi = jnp.arange(BLOCK)[:, None]
    ki = jnp.arange(2 * BLOCK)[None, :]
    dist = qi + BLOCK - ki
    in_window = (dist >= 0) & (dist < WINDOW)
    has_prev = (jnp.arange(nb)[:, None, None] > 0) | (ki >= BLOCK)[None]
    mask = in_window[None] & has_prev
    scores = jnp.where(mask[None, :, None, None], scores, -jnp.inf)
    sink = sinks.astype(jnp.float32).reshape(SWA_KV_HEADS, SWA_GROUP)[None, None, :, :, None, None]
    m = jnp.maximum(scores.max(axis=-1, keepdims=True), sink)
    p = jnp.exp(scores - m)
    denom = p.sum(axis=-1, keepdims=True) + jnp.exp(sink - m)
    probs = (p / denom).astype(v.dtype)
    out = jnp.einsum('bnhgqk,bnkhd->bnqhgd', probs, vb)
    return out.reshape(B, S, SWA_Q_HEADS * SWA_HEAD_DIM)


def gla_chunked(q, k, v, log_a):
    B, S = q.shape[0], q.shape[1]
    C = GLA_CHUNK
    nc = S // C

    def chunks(t):
        return t.reshape(B, nc, C, t.shape[2], t.shape[3]).astype(jnp.float32)

    qc = chunks(q) * (GLA_DK ** -0.5)
    kc = chunks(k)
    vc = chunks(v)
    b = jnp.cumsum(chunks(log_a), axis=2)
    b_last = b[:, :, -1:]
    q_dec = qc * jnp.exp(b)
    k_inv = kc * jnp.exp(-b)
    k_to_end = kc * jnp.exp(b_last - b)
    causal = jnp.tril(jnp.ones((C, C), dtype=bool))
    attn = jnp.einsum('bnihd,bnjhd->bnhij', q_dec, k_inv)
    attn = jnp.where(causal, attn, 0.0)
    o_intra = jnp.einsum('bnhij,bnjhv->bnihv', attn, vc)
    upd = jnp.einsum('bnjhd,bnjhv->bnhdv', k_to_end, vc)
    decay = jnp.exp(b_last[:, :, 0])

    def step(state, inp):
        dec, u = inp
        return state * dec[..., None] + u, state

    init = jnp.zeros((B, GLA_HEADS, GLA_DK, GLA_DV), jnp.float32)
    _, s_prev = lax.scan(step, init, (jnp.swapaxes(decay, 0, 1), jnp.swapaxes(upd, 0, 1)))
    s_prev = jnp.swapaxes(s_prev, 0, 1)
    o_inter = jnp.einsum('bnihd,bnhdv->bnihv', q_dec, s_prev)
    return (o_intra + o_inter).reshape(B, S, GLA_HEADS, GLA_DV)


def layer_norm(t, g, b):
    tf = t.astype(jnp.float32)
    mu = tf.mean(axis=-1, keepdims=True)
    var = jnp.square(tf - mu).mean(axis=-1, keepdims=True)
    return ((tf - mu) * lax.rsqrt(var + EPS) * g.astype(jnp.float32) + b.astype(jnp.float32)).astype(t.dtype)


def _fwd_setup_inputs(seed: int = 0) -> dict:
    key = jax.random.key(seed)
    ks = jax.random.split(key, 12)
    x = jax.random.normal(ks[0], (BATCH, SEQ, D_MODEL), jnp.float32)
    offset = jax.random.randint(ks[1], (BATCH, 1), 0, 1024, dtype=jnp.int32)
    positions = offset + jnp.arange(SEQ, dtype=jnp.int32)[None, :]
    w_in = jax.random.normal(ks[2], (DEPTH, D_MODEL, D_IN_PROJ), jnp.float32) * D_MODEL ** -0.5
    starts = np.cumsum((0,) + IN_SPLITS)
    col_scale = np.ones((D_IN_PROJ,), np.float32)
    col_scale[starts[2]:starts[3]] = BETA
    col_scale[starts[6]:starts[7]] = BETA
    w_in = w_in * jnp.asarray(col_scale)
    gla_w_gate_up = jax.random.normal(ks[3], (DEPTH, GLA_RANK, GLA_HEADS * GLA_DK), jnp.float32) * GLA_RANK ** -0.5
    gla_b_gate = 0.01 * jax.random.normal(ks[4], (DEPTH, GLA_HEADS * GLA_DK), jnp.float32)
    attn_sinks = 0.5 * jax.random.normal(ks[5], (DEPTH, SWA_Q_HEADS), jnp.float32)
    gla_norm_w = 1.0 + 0.01 * jax.random.normal(ks[6], (DEPTH, GLA_DV), jnp.float32)
    w_out = jax.random.normal(ks[7], (DEPTH, D_MIX, D_MODEL), jnp.float32) * (D_MIX ** -0.5) * BETA
    ln_g = 1.0 + 0.01 * jax.random.normal(ks[8], (DEPTH, D_MODEL), jnp.float32)
    ln_b = 0.01 * jax.random.normal(ks[9], (DEPTH, D_MODEL), jnp.float32)
    return {'x': x, 'positions': positions, 'w_in': w_in, 'gla_w_gate_up': gla_w_gate_up,
            'gla_b_gate': gla_b_gate, 'attn_sinks': attn_sinks, 'gla_norm_w': gla_norm_w,
            'w_out': w_out, 'ln_g': ln_g, 'ln_b': ln_b}


def _fwd_reference(x, positions, w_in, gla_w_gate_up, gla_b_gate, attn_sinks, gla_norm_w, w_out, ln_g, ln_b):
    B, S = x.shape[0], x.shape[1]
    for layer in range(DEPTH):
        proj = jnp.einsum('bsd,de->bse', x, w_in[layer])
        q_a, k_a, v_a, g_a, q_b, k_b, v_b, g_b, r_b = split_cols(proj, IN_SPLITS)
        q_a = partial_rope(q_a.reshape(B, S, SWA_Q_HEADS, SWA_HEAD_DIM), positions)
        k_a = partial_rope(k_a.reshape(B, S, SWA_KV_HEADS, SWA_HEAD_DIM), positions)
        v_a = v_a.reshape(B, S, SWA_KV_HEADS, SWA_HEAD_DIM)
        out_a = sliding_window_attention(q_a, k_a, v_a, attn_sinks[layer]) * jax.nn.silu(g_a)
        gate_logit = jnp.einsum('bsr,re->bse', r_b, gla_w_gate_up[layer]) + gla_b_gate[layer]
        log_a = jax.nn.log_sigmoid(gate_logit.astype(jnp.float32)) / GLA_TAU
        o_b = gla_chunked(q_b.reshape(B, S, GLA_HEADS, GLA_DK),
                          k_b.reshape(B, S, GLA_HEADS, GLA_DK),
                          v_b.reshape(B, S, GLA_HEADS, GLA_DV),
                          log_a.reshape(B, S, GLA_HEADS, GLA_DK))
        o_b = o_b * lax.rsqrt(jnp.mean(jnp.square(o_b), axis=-1, keepdims=True) + EPS) * gla_norm_w[layer].astype(jnp.float32)
        out_b = o_b.reshape(B, S, GLA_WIDTH).astype(x.dtype) * jax.nn.silu(g_b)
        mix = jnp.einsum('bse,ed->bsd', jnp.concatenate([out_a, out_b], axis=-1), w_out[layer])
        x = layer_norm(ALPHA * x + mix, ln_g[layer], ln_b[layer])
    return x


import jax as _jax
import jax.numpy as _jnp

TWIN_FORMAT = 'train_step'
FWD_PARAMS = ['x', 'positions', 'w_in', 'gla_w_gate_up', 'gla_b_gate', 'attn_sinks', 'gla_norm_w', 'w_out', 'ln_g', 'ln_b']
TWIN_WEIGHTS = ['w_in', 'gla_w_gate_up', 'gla_b_gate', 'attn_sinks', 'gla_norm_w', 'w_out', 'ln_g', 'ln_b']
TWIN_DIFF_INPUT = 'x'
TWIN_INPUTS = ['x', 'positions', 'w_in', 'gla_w_gate_up', 'gla_b_gate', 'attn_sinks', 'gla_norm_w', 'w_out', 'ln_g', 'ln_b', 'loss_target', 'm_w_in', 'm_gla_w_gate_up', 'm_gla_b_gate', 'm_attn_sinks', 'm_gla_norm_w', 'm_w_out', 'm_ln_g', 'm_ln_b', 'v_w_in', 'v_gla_w_gate_up', 'v_gla_b_gate', 'v_attn_sinks', 'v_gla_norm_w', 'v_w_out', 'v_ln_g', 'v_ln_b']
TWIN_OUTPUTS = ['loss', 'grad_x', 'grad_w_in', 'grad_gla_w_gate_up', 'grad_gla_b_gate', 'grad_attn_sinks', 'grad_gla_norm_w', 'grad_w_out', 'grad_ln_g', 'grad_ln_b', 'delta_w_in', 'delta_gla_w_gate_up', 'delta_gla_b_gate', 'delta_attn_sinks', 'delta_gla_norm_w', 'delta_w_out', 'delta_ln_g', 'delta_ln_b', 'new_m_w_in', 'new_m_gla_w_gate_up', 'new_m_gla_b_gate', 'new_m_attn_sinks', 'new_m_gla_norm_w', 'new_m_w_out', 'new_m_ln_g', 'new_m_ln_b', 'new_v_w_in', 'new_v_gla_w_gate_up', 'new_v_gla_b_gate', 'new_v_attn_sinks', 'new_v_gla_norm_w', 'new_v_w_out', 'new_v_ln_g', 'new_v_ln_b']
TWIN_LEAF_KINDS = {'loss': 'loss', 'grad_x': 'grad_x', 'grad_w_in': 'grad_w', 'grad_gla_w_gate_up': 'grad_w', 'grad_gla_b_gate': 'grad_w', 'grad_attn_sinks': 'grad_w', 'grad_gla_norm_w': 'grad_w', 'grad_w_out': 'grad_w', 'grad_ln_g': 'grad_w', 'grad_ln_b': 'grad_w', 'delta_w_in': 'delta_w', 'delta_gla_w_gate_up': 'delta_w', 'delta_gla_b_gate': 'delta_w', 'delta_attn_sinks': 'delta_w', 'delta_gla_norm_w': 'delta_w', 'delta_w_out': 'delta_w', 'delta_ln_g': 'delta_w', 'delta_ln_b': 'delta_w', 'new_m_w_in': 'new_m', 'new_m_gla_w_gate_up': 'new_m', 'new_m_gla_b_gate': 'new_m', 'new_m_attn_sinks': 'new_m', 'new_m_gla_norm_w': 'new_m', 'new_m_w_out': 'new_m', 'new_m_ln_g': 'new_m', 'new_m_ln_b': 'new_m', 'new_v_w_in': 'new_v', 'new_v_gla_w_gate_up': 'new_v', 'new_v_gla_b_gate': 'new_v', 'new_v_attn_sinks': 'new_v', 'new_v_gla_norm_w': 'new_v', 'new_v_w_out': 'new_v', 'new_v_ln_g': 'new_v', 'new_v_ln_b': 'new_v'}


def _forward(args):
    return _fwd_reference(*[args[k] for k in FWD_PARAMS])


def _output_shape():
    def fwd():
        inp = _fwd_setup_inputs(0)
        return _fwd_reference(*[inp[k] for k in FWD_PARAMS])
    out = _jax.eval_shape(fwd)
    return out.shape, out.dtype

N_MICROBATCH = 1
ADAM_LR = 0.001
ADAM_B1 = 0.9
ADAM_B2 = 0.999
ADAM_EPS = 1e-08
ADAM_WD = 0.01
ADAM_STEP = 10
PER_EXAMPLE_BATCH_AXIS = {'x': 0, 'positions': 0, 'loss_target': 0}
SHARED_INPUTS = []
_WEIGHT_DTYPES = {'w_in': _jnp.float32, 'gla_w_gate_up': _jnp.float32, 'gla_b_gate': _jnp.float32, 'attn_sinks': _jnp.float32, 'gla_norm_w': _jnp.float32, 'w_out': _jnp.float32, 'ln_g': _jnp.float32, 'ln_b': _jnp.float32}
MOMENT_SCALE = {'w_in': 5.532380e-02, 'gla_w_gate_up': 8.293121e-03, 'gla_b_gate': 3.108072e-02, 'attn_sinks': 3.789244e-03, 'gla_norm_w': 9.922947e-02, 'w_out': 6.166514e-02, 'ln_g': 3.200548e+01, 'ln_b': 3.488280e-01}


def _to_microbatches(a, axis):
    t = _jnp.moveaxis(a, axis, 0)
    t = t.reshape((N_MICROBATCH, t.shape[0] // N_MICROBATCH) + t.shape[1:])
    return _jnp.moveaxis(t, 1, axis + 1)


def setup_inputs(seed: int = 0) -> dict:
    inp = _fwd_setup_inputs(seed)
    key = _jax.random.fold_in(_jax.random.key(seed), 7919)
    shape, _ = _output_shape()
    out = dict(inp)
    out["loss_target"] = _jax.random.normal(_jax.random.fold_in(key, 0), shape, _jnp.float32)
    for i, name in enumerate(TWIN_WEIGHTS):
        w = inp[name].astype(_jnp.float32)
        if MOMENT_SCALE is None:
            s = _jnp.sqrt(_jnp.mean(_jnp.square(w)) + 1e-30)
        else:
            s = MOMENT_SCALE[name]
        km, kv = _jax.random.split(_jax.random.fold_in(key, i + 1))
        out[name] = w
        out["m_" + name] = s * _jax.random.normal(km, w.shape, _jnp.float32)
        out["v_" + name] = (s * s) * _jax.random.uniform(kv, w.shape, _jnp.float32, 0.5, 1.5)
    if N_MICROBATCH > 1:
        for name, axis in PER_EXAMPLE_BATCH_AXIS.items():
            out[name] = _to_microbatches(out[name], axis)
    return {'x': out['x'], 'positions': out['positions'], 'w_in': out['w_in'], 'gla_w_gate_up': out['gla_w_gate_up'], 'gla_b_gate': out['gla_b_gate'], 'attn_sinks': out['attn_sinks'], 'gla_norm_w': out['gla_norm_w'], 'w_out': out['w_out'], 'ln_g': out['ln_g'], 'ln_b': out['ln_b'], 'loss_target': out['loss_target'], 'm_w_in': out['m_w_in'], 'm_gla_w_gate_up': out['m_gla_w_gate_up'], 'm_gla_b_gate': out['m_gla_b_gate'], 'm_attn_sinks': out['m_attn_sinks'], 'm_gla_norm_w': out['m_gla_norm_w'], 'm_w_out': out['m_w_out'], 'm_ln_g': out['m_ln_g'], 'm_ln_b': out['m_ln_b'], 'v_w_in': out['v_w_in'], 'v_gla_w_gate_up': out['v_gla_w_gate_up'], 'v_gla_b_gate': out['v_gla_b_gate'], 'v_attn_sinks': out['v_attn_sinks'], 'v_gla_norm_w': out['v_gla_norm_w'], 'v_w_out': out['v_w_out'], 'v_ln_g': out['v_ln_g'], 'v_ln_b': out['v_ln_b']}


def _loss(weights, diff, rest, loss_target):
    with _jax.named_scope("forward"):
        args = {**rest, TWIN_DIFF_INPUT: diff, **{k: w.astype(_WEIGHT_DTYPES[k]) for k, w in weights.items()}}
        y = _forward(args)
    with _jax.named_scope("loss_head"):
        err = _jnp.square(y.astype(_jnp.float32) - loss_target)
        return 0.5 * _jnp.sum(_jnp.mean(err, axis=-1)) if err.ndim else 0.5 * err


def _adamw(w, g, m, v):
    m = ADAM_B1 * m + (1.0 - ADAM_B1) * g
    v = ADAM_B2 * v + (1.0 - ADAM_B2) * _jnp.square(g)
    m_hat = m / (1.0 - ADAM_B1 ** ADAM_STEP)
    v_hat = v / (1.0 - ADAM_B2 ** ADAM_STEP)
    delta = -ADAM_LR * (m_hat / (_jnp.sqrt(v_hat) + ADAM_EPS) + ADAM_WD * w)
    return delta, m, v


def reference(x, positions, w_in, gla_w_gate_up, gla_b_gate, attn_sinks, gla_norm_w, w_out, ln_g, ln_b, loss_target, m_w_in, m_gla_w_gate_up, m_gla_b_gate, m_attn_sinks, m_gla_norm_w, m_w_out, m_ln_g, m_ln_b, v_w_in, v_gla_w_gate_up, v_gla_b_gate, v_attn_sinks, v_gla_norm_w, v_w_out, v_ln_g, v_ln_b):
    given = dict(x=x, positions=positions, w_in=w_in, gla_w_gate_up=gla_w_gate_up, gla_b_gate=gla_b_gate, attn_sinks=attn_sinks, gla_norm_w=gla_norm_w, w_out=w_out, ln_g=ln_g, ln_b=ln_b, loss_target=loss_target, m_w_in=m_w_in, m_gla_w_gate_up=m_gla_w_gate_up, m_gla_b_gate=m_gla_b_gate, m_attn_sinks=m_attn_sinks, m_gla_norm_w=m_gla_norm_w, m_w_out=m_w_out, m_ln_g=m_ln_g, m_ln_b=m_ln_b, v_w_in=v_w_in, v_gla_w_gate_up=v_gla_w_gate_up, v_gla_b_gate=v_gla_b_gate, v_attn_sinks=v_attn_sinks, v_gla_norm_w=v_gla_norm_w, v_w_out=v_w_out, v_ln_g=v_ln_g, v_ln_b=v_ln_b)
    weights = {n: given[n] for n in TWIN_WEIGHTS}
    shared = {n: given[n] for n in SHARED_INPUTS}
    per_example = {n: given[n] for n in ['x', 'positions']}
    grad_fn = _jax.value_and_grad(_loss, argnums=(0, 1))

    def one_microbatch(ex, loss_target):
        ex = dict(ex)
        diff = ex.pop(TWIN_DIFF_INPUT)
        return grad_fn(weights, diff, {**shared, **ex}, loss_target)

    if N_MICROBATCH == 1:
        loss, (grad_w, grad_x) = one_microbatch(per_example, given["loss_target"])
    else:
        def body(carry, xs):
            loss_sum, grad_sum = carry
            l_k, (gw_k, gx_k) = one_microbatch(xs[0], xs[1])
            with _jax.named_scope("update"):
                return (loss_sum + l_k, _jax.tree.map(_jnp.add, grad_sum, gw_k)), gx_k

        init = (_jnp.zeros((), _jnp.float32), _jax.tree.map(_jnp.zeros_like, weights))
        (loss, grad_w), grad_x = _jax.lax.scan(body, init, (per_example, given["loss_target"]))
    with _jax.named_scope("update"):
        delta_w, new_m, new_v = {}, {}, {}
        for n in TWIN_WEIGHTS:
            delta_w[n], new_m[n], new_v[n] = _adamw(weights[n], grad_w[n], given["m_" + n], given["v_" + n])
    return (loss, grad_x, *[grad_w[n] for n in TWIN_WEIGHTS], *[delta_w[n] for n in TWIN_WEIGHTS],
            *[new_m[n] for n in TWIN_WEIGHTS], *[new_v[n] for n in TWIN_WEIGHTS])
```

```python
import functools

import jax
import jax.numpy as jnp
from jax import lax
from jax.experimental import pallas as pl
from jax.experimental.pallas import tpu as pltpu

F32 = jnp.float32
MXU_DTYPE = jnp.bfloat16
HIGHEST = lax.Precision.HIGHEST

N_DEV = 8
D_MODEL = 1024
SWA_Q_HEADS = 8
SWA_KV_HEADS = 2
SWA_GROUP = 4
SWA_HEAD_DIM = 64
BLOCK = 128
ROPE_THETA = 500000.0
ROT_DIM = 16
GLA_HEADS = 4
GLA_DK = 64
GLA_DV = 128
GLA_RANK = 16
GLA_TAU = 16.0
GLA_CHUNK = 64
D_IN_PROJ = 2832
D_IN_SHARD = D_IN_PROJ // N_DEV
D_OUT_SHARD = D_MODEL // N_DEV
OFF = (0, 512, 640, 768, 1280, 1536, 1792, 2304, 2816, 2832)
EPS = 1e-5
ALPHA = 2.0 ** 0.25
SWA_SCALE = SWA_HEAD_DIM ** -0.5
GLA_SCALE = GLA_DK ** -0.5
ADAM_LR = 0.001
ADAM_B1 = 0.9
ADAM_B2 = 0.999
ADAM_EPS = 1e-08
ADAM_WD = 0.01
ADAM_STEP = 10
VMEM_LIMIT = 56 * 1024 * 1024
SMALL_ROWS = 40

_NT = (((1,), (1,)), ((), ()))
_TN = (((0,), (0,)), ((), ()))


def _mm(a, b):
    return jnp.dot(a, b, preferred_element_type=F32)


def _mm_nt(a, b):
    return lax.dot_general(a, b, _NT, preferred_element_type=F32)


def _mm_tn(a, b):
    return lax.dot_general(a, b, _TN, preferred_element_type=F32)


def _sigmoid(t):
    return 1.0 / (1.0 + jnp.exp(-t))


def _cparams(**kw):
    return pltpu.CompilerParams(vmem_limit_bytes=VMEM_LIMIT, **kw)


def _full(shape):
    return pl.BlockSpec(shape, lambda *_: (0,) * len(shape))


def _rows(tile, width):
    return pl.BlockSpec((tile, width), lambda i: (i, 0))


def _rope_tables(positions):
    half = ROT_DIM // 2
    inv_freq = ROPE_THETA ** (-jnp.arange(half, dtype=F32) / half)
    ang = positions.astype(F32)[:, None] * inv_freq
    cos, sin = jnp.cos(ang), jnp.sin(ang)
    s = positions.shape[0]
    rest = SWA_HEAD_DIM - ROT_DIM
    c = jnp.concatenate([cos, cos, jnp.ones((s, rest), F32)], axis=1)
    s1 = jnp.concatenate([-sin, jnp.zeros((s, half + rest), F32)], axis=1)
    s2 = jnp.concatenate([jnp.zeros((s, half), F32), sin, jnp.zeros((s, rest), F32)], axis=1)
    return tuple(jnp.concatenate([t, t], axis=1) for t in (c, s1, s2))


def _rope(t, c, s1, s2):
    return t * c + pltpu.roll(t, 120, 1) * s1 + pltpu.roll(t, 8, 1) * s2


def _rope_t(g, c, s1, s2):
    return g * c + pltpu.roll(g * s1, 8, 1) + pltpu.roll(g * s2, 120, 1)


def _in_proj(x, w_in, wg, b_gate, rope):
    s = x.shape[0]
    ts = min(512, s)
    widths = [OFF[i + 1] - OFF[i] for i in range(9)]

    def body(x_ref, w_ref, wg_ref, bg_ref, c_ref, s1_ref, s2_ref,
             qa_ref, ka_ref, va_ref, ga_ref, qb_ref, kb_ref, vb_ref, gb_ref, rb_ref, la_ref, oms_ref):
        xb = x_ref[...].astype(MXU_DTYPE)
        c, s1, s2 = c_ref[...], s1_ref[...], s2_ref[...]

        def cols(i):
            return _mm(xb, w_ref[:, OFF[i]:OFF[i + 1]])

        qa = cols(0)
        for i in range(4):
            qa_ref[:, 128 * i:128 * (i + 1)] = _rope(qa[:, 128 * i:128 * (i + 1)], c, s1, s2)
        ka_ref[...] = _rope(cols(1), c, s1, s2)
        va_ref[...] = cols(2)
        ga_ref[...] = cols(3)
        qb_ref[...] = cols(4)
        kb_ref[...] = cols(5)
        vb_ref[...] = cols(6)
        gb_ref[...] = cols(7)
        rb = cols(8)
        rb_ref[...] = rb
        logit = _mm(rb.astype(MXU_DTYPE), wg_ref[...]) + bg_ref[...]
        e = jnp.exp(-jnp.abs(logit))
        la_ref[...] = (jnp.minimum(logit, 0.0) - jnp.log(1.0 + e)) / GLA_TAU
        oms_ref[...] = jnp.where(logit >= 0.0, e, 1.0) / (1.0 + e)

    out_shape = [jax.ShapeDtypeStruct((s, w), F32) for w in widths]
    out_shape += [jax.ShapeDtypeStruct((s, 256), F32)] * 2
    return pl.pallas_call(
        body, name="in_proj", grid=(s // ts,),
        in_specs=[_rows(ts, D_MODEL), _full((D_MODEL, D_IN_PROJ)), _full((GLA_RANK, 256)), _full((1, 256)),
                  _rows(ts, 128), _rows(ts, 128), _rows(ts, 128)],
        out_specs=[_rows(ts, w) for w in widths] + [_rows(ts, 256)] * 2,
        out_shape=out_shape,
        compiler_params=_cparams(dimension_semantics=("arbitrary",)),
    )(x, w_in, wg, b_gate, *rope)


def _swa_mask(n):
    qi = lax.broadcasted_iota(jnp.int32, (BLOCK, 2 * BLOCK), 0)
    ki = lax.broadcasted_iota(jnp.int32, (BLOCK, 2 * BLOCK), 1)
    dist = qi + BLOCK - ki
    return (dist >= 0) & (dist < BLOCK) & ((n > 0) | (ki >= BLOCK))


def _swa_probs(qh, kj, mask, sink):
    sc = _mm_nt(qh, kj) * SWA_SCALE
    sc = jnp.where(mask, sc, -jnp.inf)
    m = jnp.maximum(jnp.max(sc, axis=1, keepdims=True), sink)
    p = jnp.exp(sc - m)
    ps = jnp.exp(sink - m)
    denom = jnp.sum(p, axis=1, keepdims=True) + ps
    return p / denom, ps / denom


def _swa_fwd(sinks, qa, k_pad, v_pad, ga):
    s = qa.shape[0]

    def body(sink_ref, qa_ref, ga_ref, k_ref, v_ref, attn_ref, cat_ref):
        n = pl.program_id(0)
        start = pl.multiple_of(n * BLOCK, BLOCK)
        kw = k_ref[pl.ds(start, 2 * BLOCK), :].astype(MXU_DTYPE)
        vw = v_ref[pl.ds(start, 2 * BLOCK), :].astype(MXU_DTYPE)
        mask = _swa_mask(n)
        q = qa_ref[...].astype(MXU_DTYPE)
        outs = []
        for h in range(SWA_Q_HEADS):
            j = h // SWA_GROUP
            kj = kw[:, 64 * j:64 * (j + 1)]
            vj = vw[:, 64 * j:64 * (j + 1)]
            probs, _ = _swa_probs(q[:, 64 * h:64 * (h + 1)], kj, mask, sink_ref[h])
            outs.append(_mm(probs.astype(MXU_DTYPE), vj))
        o = jnp.concatenate(outs, axis=1)
        attn_ref[...] = o
        g = ga_ref[...]
        cat_ref[...] = (o * (g * _sigmoid(g))).astype(cat_ref.dtype)

    return pl.pallas_call(
        body, name="swa_fwd", grid=(s // BLOCK,),
        in_specs=[pl.BlockSpec(memory_space=pltpu.SMEM), _rows(BLOCK, 512), _rows(BLOCK, 512),
                  _full((s + BLOCK, 128)), _full((s + BLOCK, 128))],
        out_specs=[_rows(BLOCK, 512), _rows(BLOCK, 512)],
        out_shape=[jax.ShapeDtypeStruct((s, 512), F32), jax.ShapeDtypeStruct((s, 512), MXU_DTYPE)],
        compiler_params=_cparams(dimension_semantics=("arbitrary",)),
    )(sinks, qa, ga, k_pad, v_pad)


def _tri(lower):
    r = lax.broadcasted_iota(jnp.int32, (GLA_CHUNK, GLA_CHUNK), 0)
    c = lax.broadcasted_iota(jnp.int32, (GLA_CHUNK, GLA_CHUNK), 1)
    return (r >= c) if lower else (r <= c)


def _gla_decays(la, tri_f32):
    b = jnp.dot(tri_f32, la, precision=HIGHEST, preferred_element_type=F32)
    bl = b[GLA_CHUNK - 1:GLA_CHUNK, :]
    return b, bl


def _gla_fwd(qb, kb, vb, la, gb, norm_w):
    s = qb.shape[0]
    tb = min(256, s)
    ch = tb // GLA_CHUNK

    def body(qb_ref, kb_ref, vb_ref, la_ref, gb_ref, nw_ref, o_ref, cat_ref, sp_ref, st_ref):
        @pl.when(pl.program_id(0) == 0)
        def _():
            st_ref[...] = jnp.zeros_like(st_ref)

        causal = _tri(True)
        tri = causal.astype(F32)
        nw = nw_ref[...]
        for ci in range(ch):
            rows = slice(GLA_CHUNK * ci, GLA_CHUNK * (ci + 1))
            b, bl = _gla_decays(la_ref[rows, :], tri)
            k = kb_ref[rows, :]
            qd = ((qb_ref[rows, :] * GLA_SCALE) * jnp.exp(b)).astype(MXU_DTYPE)
            ki = (k * jnp.exp(-b)).astype(MXU_DTYPE)
            ke = (k * jnp.exp(bl - b)).astype(MXU_DTYPE)
            st = st_ref[...]
            sp_ref[ci] = st
            st16 = st.astype(MXU_DTYPE)
            v = vb_ref[rows, :].astype(MXU_DTYPE)
            g = gb_ref[rows, :]
            silu = g * _sigmoid(g)
            upd, o_all, cat_all = [], [], []
            for h in range(GLA_HEADS):
                lk = slice(GLA_DK * h, GLA_DK * (h + 1))
                lv = slice(GLA_DV * h, GLA_DV * (h + 1))
                a = jnp.where(causal, _mm_nt(qd[:, lk], ki[:, lk]), 0.0)
                oh = _mm(a.astype(MXU_DTYPE), v[:, lv]) + _mm_nt(qd[:, lk], st16[:, lk])
                upd.append(_mm_tn(v[:, lv], ke[:, lk]))
                r = lax.rsqrt(jnp.mean(oh * oh, axis=1, keepdims=True) + EPS)
                o_all.append(oh)
                cat_all.append(oh * r * nw * silu[:, lv])
            st_ref[...] = st * jnp.exp(bl) + jnp.concatenate(upd, axis=1)
            o_ref[rows, :] = jnp.concatenate(o_all, axis=1)
            cat_ref[rows, :] = jnp.concatenate(cat_all, axis=1).astype(cat_ref.dtype)

    return pl.pallas_call(
        body, name="gla_fwd", grid=(s // tb,),
        in_specs=[_rows(tb, 256), _rows(tb, 256), _rows(tb, 512), _rows(tb, 256), _rows(tb, 512), _full((1, 128))],
        out_specs=[_rows(tb, 512), _rows(tb, 512), pl.BlockSpec((ch, GLA_DV, 256), lambda i: (i, 0, 0))],
        out_shape=[jax.ShapeDtypeStruct((s, 512), F32), jax.ShapeDtypeStruct((s, 512), MXU_DTYPE),
                   jax.ShapeDtypeStruct((s // GLA_CHUNK, GLA_DV, 256), F32)],
        scratch_shapes=[pltpu.VMEM((GLA_DV, 256), F32)],
        compiler_params=_cparams(dimension_semantics=("arbitrary",)),
    )(qb, kb, vb, la, gb, norm_w)


def _out_ln_loss(cat_a, cat_b, w_out, x, target, ln_g, ln_b):
    s = x.shape[0]
    ts = min(256, s)

    def body(ca_ref, cb_ref, w_ref, x_ref, t_ref, g_ref, b_ref,
             loss_ref, gx_ref, da_ref, db_ref, gw_ref, gln_ref):
        @pl.when(pl.program_id(0) == 0)
        def _():
            loss_ref[...] = jnp.zeros_like(loss_ref)
            gw_ref[...] = jnp.zeros_like(gw_ref)
            gln_ref[...] = jnp.zeros_like(gln_ref)

        ca, cb = ca_ref[...], cb_ref[...]
        mix = _mm(ca, w_ref[0:512, :]) + _mm(cb, w_ref[512:1024, :])
        h = ALPHA * x_ref[...] + mix
        mu = jnp.mean(h, axis=1, keepdims=True)
        hc = h - mu
        rstd = lax.rsqrt(jnp.mean(hc * hc, axis=1, keepdims=True) + EPS)
        xhat = hc * rstd
        g = g_ref[...]
        err = xhat * g + b_ref[...] - t_ref[...]
        loss_ref[...] += 0.5 * jnp.sum(jnp.mean(err * err, axis=1, keepdims=True))
        dy = err * (1.0 / D_MODEL)
        gln_ref[0:1, :] += jnp.sum(dy * xhat, axis=0, keepdims=True)
        gln_ref[1:2, :] += jnp.sum(dy, axis=0, keepdims=True)
        dxh = dy * g
        dh = rstd * (dxh - jnp.mean(dxh, axis=1, keepdims=True)
                     - xhat * jnp.mean(dxh * xhat, axis=1, keepdims=True))
        gx_ref[...] = ALPHA * dh
        dh16 = dh.astype(MXU_DTYPE)
        da_ref[...] = _mm_nt(dh16, w_ref[0:512, :])
        db_ref[...] = _mm_nt(dh16, w_ref[512:1024, :])
        gw_ref[0:512, :] += _mm_tn(ca, dh16)
        gw_ref[512:1024, :] += _mm_tn(cb, dh16)

    return pl.pallas_call(
        body, name="out_ln_loss", grid=(s // ts,),
        in_specs=[_rows(ts, 512), _rows(ts, 512), _full((D_MODEL, D_MODEL)), _rows(ts, D_MODEL), _rows(ts, D_MODEL),
                  _full((1, D_MODEL)), _full((1, D_MODEL))],
        out_specs=[_full((1, 128)), _rows(ts, D_MODEL), _rows(ts, 512), _rows(ts, 512),
                   _full((D_MODEL, D_MODEL)), _full((2, D_MODEL))],
        out_shape=[jax.ShapeDtypeStruct((1, 128), F32), jax.ShapeDtypeStruct((s, D_MODEL), F32),
                   jax.ShapeDtypeStruct((s, 512), F32), jax.ShapeDtypeStruct((s, 512), F32),
                   jax.ShapeDtypeStruct((D_MODEL, D_MODEL), F32), jax.ShapeDtypeStruct((2, D_MODEL), F32)],
        compiler_params=_cparams(dimension_semantics=("arbitrary",)),
    )(cat_a, cat_b, w_out, x, target, ln_g, ln_b)


def _swa_bwd(sinks, qa, k_pad, v_pad, attn, ga, d_cat_a, rope):
    s = qa.shape[0]

    def body(sink_ref, qa_ref, ga_ref, at_ref, dc_ref, c_ref, s1_ref, s2_ref, k_ref, v_ref,
             dq_ref, dg_ref, dk_ref, dv_ref, ds_ref):
        n = pl.program_id(0)

        @pl.when(n == 0)
        def _():
            dk_ref[...] = jnp.zeros_like(dk_ref)
            dv_ref[...] = jnp.zeros_like(dv_ref)
            ds_ref[...] = jnp.zeros_like(ds_ref)

        start = pl.multiple_of(n * BLOCK, BLOCK)
        kw = k_ref[pl.ds(start, 2 * BLOCK), :].astype(MXU_DTYPE)
        vw = v_ref[pl.ds(start, 2 * BLOCK), :].astype(MXU_DTYPE)
        mask = _swa_mask(n)
        q = qa_ref[...].astype(MXU_DTYPE)
        g = ga_ref[...]
        sg = _sigmoid(g)
        o = at_ref[...]
        dc = dc_ref[...]
        do = dc * (g * sg)
        dg_ref[...] = dc * o * (sg * (1.0 + g * (1.0 - sg)))
        do16 = do.astype(MXU_DTYPE)
        od = do * o
        dq, dk, dv = [], [], []
        for j in range(SWA_KV_HEADS):
            kj = kw[:, 64 * j:64 * (j + 1)]
            vj = vw[:, 64 * j:64 * (j + 1)]
            dkj = jnp.zeros((2 * BLOCK, SWA_HEAD_DIM), F32)
            dvj = jnp.zeros((2 * BLOCK, SWA_HEAD_DIM), F32)
            for h in range(SWA_GROUP * j, SWA_GROUP * (j + 1)):
                lh = slice(64 * h, 64 * (h + 1))
                probs, psink = _swa_probs(q[:, lh], kj, mask, sink_ref[h])
                delta = jnp.sum(od[:, lh], axis=1, keepdims=True)
                dp = _mm_nt(do16[:, lh], vj)
                dsc = ((probs * (dp - delta)) * SWA_SCALE).astype(MXU_DTYPE)
                ds_ref[h:h + 1, :] += jnp.zeros((1, 128), F32) - jnp.sum(psink * delta)
                dq.append(_mm(dsc, kj))
                dkj += _mm_tn(dsc, q[:, lh])
                dvj += _mm_tn(probs.astype(MXU_DTYPE), do16[:, lh])
            dk.append(dkj)
            dv.append(dvj)
        c, s1, s2 = c_ref[...], s1_ref[...], s2_ref[...]
        dqf = jnp.concatenate(dq, axis=1)
        for i in range(4):
            dq_ref[:, 128 * i:128 * (i + 1)] = _rope_t(dqf[:, 128 * i:128 * (i + 1)], c, s1, s2)
        dk_ref[pl.ds(start, 2 * BLOCK), :] += jnp.concatenate(dk, axis=1)
        dv_ref[pl.ds(start, 2 * BLOCK), :] += jnp.concatenate(dv, axis=1)

    return pl.pallas_call(
        body, name="swa_bwd", grid=(s // BLOCK,),
        in_specs=[pl.BlockSpec(memory_space=pltpu.SMEM)] + [_rows(BLOCK, 512)] * 4 + [_rows(BLOCK, 128)] * 3
        + [_full((s + BLOCK, 128))] * 2,
        out_specs=[_rows(BLOCK, 512), _rows(BLOCK, 512), _full((s + BLOCK, 128)), _full((s + BLOCK, 128)),
                   _full((SWA_Q_HEADS, 128))],
        out_shape=[jax.ShapeDtypeStruct((s, 512), F32), jax.ShapeDtypeStruct((s, 512), F32),
                   jax.ShapeDtypeStruct((s + BLOCK, 128), F32), jax.ShapeDtypeStruct((s + BLOCK, 128), F32),
                   jax.ShapeDtypeStruct((SWA_Q_HEADS, 128), F32)],
        compiler_params=_cparams(dimension_semantics=("arbitrary",)),
    )(sinks, qa, ga, attn, d_cat_a, *rope, k_pad, v_pad)


def _gla_bwd(qb, kb, vb, la, oms, gb, o, sprev, d_cat_b, rb, wg, norm_w):
    s = qb.shape[0]
    tb = min(256, s)
    ch = tb // GLA_CHUNK
    nb = s // tb

    def body(qb_ref, kb_ref, vb_ref, la_ref, oms_ref, gb_ref, o_ref, sp_ref, dc_ref, rb_ref, wg_ref, nw_ref,
             dq_ref, dk_ref, dv_ref, dg_ref, dr_ref, gwg_ref, gbg_ref, gnw_ref, dst_ref):
        @pl.when(pl.program_id(0) == 0)
        def _():
            dst_ref[...] = jnp.zeros_like(dst_ref)
            gwg_ref[...] = jnp.zeros_like(gwg_ref)
            gbg_ref[...] = jnp.zeros_like(gbg_ref)
            gnw_ref[...] = jnp.zeros_like(gnw_ref)

        causal = _tri(True)
        tri = causal.astype(F32)
        tri_u = _tri(False).astype(F32)
        nw = nw_ref[...]
        wg16 = wg_ref[...]
        for ci in reversed(range(ch)):
            rows = slice(GLA_CHUNK * ci, GLA_CHUNK * (ci + 1))
            b, bl = _gla_decays(la_ref[rows, :], tri)
            eb, enb, ee, dec = jnp.exp(b), jnp.exp(-b), jnp.exp(bl - b), jnp.exp(bl)
            k = kb_ref[rows, :]
            qd = (qb_ref[rows, :] * GLA_SCALE) * eb
            ki = k * enb
            ke = k * ee
            qd16, ki16, ke16 = qd.astype(MXU_DTYPE), ki.astype(MXU_DTYPE), ke.astype(MXU_DTYPE)
            st = sp_ref[ci]
            st16 = st.astype(MXU_DTYPE)
            dst = dst_ref[...]
            dst16 = dst.astype(MXU_DTYPE)
            v16 = vb_ref[rows, :].astype(MXU_DTYPE)
            g = gb_ref[rows, :]
            sg = _sigmoid(g)
            silu = g * sg
            dsilu = sg * (1.0 + g * (1.0 - sg))
            oc = o_ref[rows, :]
            dc = dc_ref[rows, :]
            gnw = jnp.zeros((1, GLA_DV), F32)
            dg, dv, dqd, dki, dke, dstp, ddec = [], [], [], [], [], [], []
            for h in range(GLA_HEADS):
                lk = slice(GLA_DK * h, GLA_DK * (h + 1))
                lv = slice(GLA_DV * h, GLA_DV * (h + 1))
                oh = oc[:, lv]
                r = lax.rsqrt(jnp.mean(oh * oh, axis=1, keepdims=True) + EPS)
                d_on = dc[:, lv] * silu[:, lv]
                dg.append(dc[:, lv] * (oh * r * nw) * dsilu[:, lv])
                gnw += jnp.sum(d_on * oh * r, axis=0, keepdims=True)
                u = d_on * nw
                do = r * u - oh * (r * r * r) * jnp.mean(u * oh, axis=1, keepdims=True)
                do16 = do.astype(MXU_DTYPE)
                a16 = jnp.where(causal, _mm_nt(qd16[:, lk], ki16[:, lk]), 0.0).astype(MXU_DTYPE)
                da16 = jnp.where(causal, _mm_nt(do16, v16[:, lv]), 0.0).astype(MXU_DTYPE)
                dv.append(_mm_tn(a16, do16) + _mm_nt(ke16[:, lk], dst16[:, lk]))
                dqd.append(_mm(da16, ki16[:, lk]) + _mm(do16, st16[:, lk]))
                dki.append(_mm_tn(da16, qd16[:, lk]))
                dke.append(_mm(v16[:, lv], dst16[:, lk]))
                dstp.append(_mm_tn(do16, qd16[:, lk]))
                ddec.append(jnp.sum(dst[:, lk] * st[:, lk], axis=0, keepdims=True))
            gnw_ref[...] += gnw
            dqd, dki, dke = (jnp.concatenate(t, axis=1) for t in (dqd, dki, dke))
            dst_ref[...] = jnp.concatenate(dstp, axis=1) + dst * dec
            dq_ref[rows, :] = dqd * eb * GLA_SCALE
            dk_ref[rows, :] = dki * enb + dke * ee
            dv_ref[rows, :] = jnp.concatenate(dv, axis=1)
            dg_ref[rows, :] = jnp.concatenate(dg, axis=1)
            dke_ke = dke * ke
            db = dqd * qd - dki * ki - dke_ke
            dbl = jnp.sum(dke_ke, axis=0, keepdims=True) + jnp.concatenate(ddec, axis=1) * dec
            dla = jnp.dot(tri_u, db, precision=HIGHEST, preferred_element_type=F32) + dbl
            dlogit = dla * oms_ref[rows, :] * (1.0 / GLA_TAU)
            dl16 = dlogit.astype(MXU_DTYPE)
            gbg_ref[...] += jnp.sum(dlogit, axis=0, keepdims=True)
            gwg_ref[...] += _mm_tn(rb_ref[rows, :].astype(MXU_DTYPE), dl16)
            dr_ref[rows, :] = _mm_nt(dl16, wg16)

    def rev(width):
        return pl.BlockSpec((tb, width), lambda i: (nb - 1 - i, 0))

    return pl.pallas_call(
        body, name="gla_bwd", grid=(nb,),
        in_specs=[rev(256), rev(256), rev(512), rev(256), rev(256), rev(512), rev(512),
                  pl.BlockSpec((ch, GLA_DV, 256), lambda i: (nb - 1 - i, 0, 0)), rev(512), rev(GLA_RANK),
                  _full((GLA_RANK, 256)), _full((1, 128))],
        out_specs=[rev(256), rev(256), rev(512), rev(512), rev(GLA_RANK),
                   _full((GLA_RANK, 256)), _full((1, 256)), _full((1, 128))],
        out_shape=[jax.ShapeDtypeStruct((s, 256), F32), jax.ShapeDtypeStruct((s, 256), F32),
                   jax.ShapeDtypeStruct((s, 512), F32), jax.ShapeDtypeStruct((s, 512), F32),
                   jax.ShapeDtypeStruct((s, GLA_RANK), F32), jax.ShapeDtypeStruct((GLA_RANK, 256), F32),
                   jax.ShapeDtypeStruct((1, 256), F32), jax.ShapeDtypeStruct((1, 128), F32)],
        scratch_shapes=[pltpu.VMEM((GLA_DV, 256), F32)],
        compiler_params=_cparams(dimension_semantics=("arbitrary",)),
    )(qb, kb, vb, la, oms, gb, o, sprev, d_cat_b, rb, wg, norm_w)


def _in_proj_bwd_x(gx0, pieces, w_in, rope):
    s = gx0.shape[0]
    ts = min(256, s)
    widths = [OFF[i + 1] - OFF[i] for i in range(9)]

    def body(gx0_ref, *refs):
        piece_refs = refs[:9]
        w_ref, c_ref, s1_ref, s2_ref, gx_ref, dp_ref = refs[9:]
        acc = gx0_ref[...]
        for i in range(9):
            t = piece_refs[i][...]
            if i == 1:
                t = _rope_t(t, c_ref[...], s1_ref[...], s2_ref[...])
            t16 = t.astype(MXU_DTYPE)
            dp_ref[:, OFF[i]:OFF[i + 1]] = t16
            acc += _mm_nt(t16, w_ref[:, OFF[i]:OFF[i + 1]])
        gx_ref[...] = acc

    return pl.pallas_call(
        body, name="in_proj_bwd_x", grid=(s // ts,),
        in_specs=[_rows(ts, D_MODEL)] + [_rows(ts, w) for w in widths]
        + [_full((D_MODEL, D_IN_PROJ))] + [_rows(ts, 128)] * 3,
        out_specs=[_rows(ts, D_MODEL), _rows(ts, D_IN_PROJ)],
        out_shape=[jax.ShapeDtypeStruct((s, D_MODEL), F32), jax.ShapeDtypeStruct((s, D_IN_PROJ), MXU_DTYPE)],
        compiler_params=_cparams(dimension_semantics=("arbitrary",)),
    )(gx0, *pieces, w_in, *rope)


def _in_proj_bwd_w(x, dproj):
    s = x.shape[0]
    ts = min(512, s)
    nsteps = s // ts
    col_chunks = [(OFF[i], OFF[i + 1]) for i in range(9)]

    def body(x_ref, dp_ref, gw_ref, acc_ref, sem):
        i = pl.program_id(0)

        @pl.when(i == 0)
        def _():
            acc_ref[...] = jnp.zeros_like(acc_ref)

        xb = x_ref[...].astype(MXU_DTYPE)
        for lo, hi in col_chunks:
            acc_ref[:, lo:hi] += _mm_tn(xb, dp_ref[:, lo:hi])

        @pl.when(i == nsteps - 1)
        def _():
            cp = pltpu.make_async_copy(acc_ref, gw_ref, sem)
            cp.start()
            cp.wait()

    return pl.pallas_call(
        body, name="in_proj_bwd_w", grid=(nsteps,),
        in_specs=[_rows(ts, D_MODEL), _rows(ts, D_IN_PROJ)],
        out_specs=pl.BlockSpec(memory_space=pl.ANY),
        out_shape=jax.ShapeDtypeStruct((D_MODEL, D_IN_PROJ), F32),
        scratch_shapes=[pltpu.VMEM((D_MODEL, D_IN_PROJ), F32), pltpu.SemaphoreType.DMA],
        compiler_params=_cparams(dimension_semantics=("arbitrary",)),
    )(x, dproj)


def _local_step(x, positions, w_in, wg, b_gate, sinks, norm_w, w_out, ln_g, ln_b, target):
    rope = _rope_tables(positions)
    qa, ka, va, ga, qb, kb, vb, gb, rb, la, oms = _in_proj(x, w_in, wg, b_gate, rope)
    zeros = jnp.zeros((BLOCK, 128), F32)
    k_pad = jnp.concatenate([zeros, ka], axis=0)
    v_pad = jnp.concatenate([zeros, va], axis=0)
    attn, cat_a = _swa_fwd(sinks, qa, k_pad, v_pad, ga)
    o, cat_b, sprev = _gla_fwd(qb, kb, vb, la, gb, norm_w)
    loss, gx0, d_cat_a, d_cat_b, g_w_out, g_ln = _out_ln_loss(cat_a, cat_b, w_out, x, target, ln_g, ln_b)
    dqa, dga, dk_pad, dv_pad, g_sinks = _swa_bwd(sinks, qa, k_pad, v_pad, attn, ga, d_cat_a, rope)
    dqb, dkb, dvb, dgb, drb, g_wg, g_bg, g_nw = _gla_bwd(qb, kb, vb, la, oms, gb, o, sprev, d_cat_b, rb, wg, norm_w)
    pieces = (dqa, dk_pad[BLOCK:], dv_pad[BLOCK:], dga, dqb, dkb, dvb, dgb, drb)
    grad_x, dproj = _in_proj_bwd_x(gx0, pieces, w_in, rope)
    g_w_in = _in_proj_bwd_w(x, dproj)
    return loss, grad_x, g_w_in, g_wg, g_bg, g_sinks, g_nw, g_w_out, g_ln


def _mesh_pos():
    return lax.axis_index("x"), lax.axis_index("y"), lax.axis_index("c")


def _peer(k, x, y, c):
    px = (1 - x) if k & 4 else x
    py = (1 - y) if k & 2 else y
    pc = (1 - c) if k & 1 else c
    return (px, py, pc), 4 * px + 2 * py + pc


def _all_gather_weights(w_in_s, w_out_s, wg_s):
    def body(win_ref, wout_ref, wg_ref, win_all, wout_all, wg_all, send_sems, recv_sems):
        x, y, c = _mesh_pos()
        me = 4 * x + 2 * y + c
        alls = (win_all, wout_all, wg_all)
        win_all[me] = win_ref[...].astype(win_all.dtype)
        wout_all[me] = wout_ref[...].astype(wout_all.dtype)
        wg_all[me] = wg_ref[...].astype(wg_all.dtype)
        sends = []
        for k in range(1, N_DEV):
            peer, _ = _peer(k, x, y, c)
            for a, arr in enumerate(alls):
                i = 3 * (k - 1) + a
                cp = pltpu.make_async_remote_copy(
                    src_ref=arr.at[me], dst_ref=arr.at[me], send_sem=send_sems.at[i], recv_sem=recv_sems.at[i],
                    device_id=peer, device_id_type=pl.DeviceIdType.MESH)
                cp.start()
                sends.append(cp)
        for k in range(1, N_DEV):
            peer, pidx = _peer(k, x, y, c)
            for a, arr in enumerate(alls):
                i = 3 * (k - 1) + a
                pltpu.make_async_remote_copy(
                    src_ref=arr.at[pidx], dst_ref=arr.at[pidx], send_sem=send_sems.at[i], recv_sem=recv_sems.at[i],
                    device_id=peer, device_id_type=pl.DeviceIdType.MESH).wait_recv()
        for cp in sends:
            cp.wait_send()

    vmem = pl.BlockSpec(memory_space=pltpu.VMEM)
    return pl.pallas_call(
        body, name="all_gather_weights",
        in_specs=[vmem] * 3, out_specs=[vmem] * 3,
        out_shape=[jax.ShapeDtypeStruct((N_DEV, D_MODEL, D_IN_SHARD), MXU_DTYPE),
                   jax.ShapeDtypeStruct((N_DEV, D_OUT_SHARD, D_MODEL), MXU_DTYPE),
                   jax.ShapeDtypeStruct((N_DEV, GLA_RANK, 32), MXU_DTYPE)],
        scratch_shapes=[pltpu.SemaphoreType.DMA((3 * (N_DEV - 1),)), pltpu.SemaphoreType.DMA((3 * (N_DEV - 1),))],
        compiler_params=_cparams(),
    )(w_in_s, w_out_s, wg_s)


def _exchange_grads(parts_w_in, parts_w_out, parts_wg, small):
    n_arr = 4

    def body(pin_ref, pout_ref, pwg_ref, sm_ref, rin_ref, rout_ref, rwg_ref, rsm_ref, send_sems, recv_sems, local_sems):
        x, y, c = _mesh_pos()
        me = 4 * x + 2 * y + c
        srcs = (pin_ref, pout_ref, pwg_ref)
        dsts = (rin_ref, rout_ref, rwg_ref, rsm_ref)

        def src_of(a, idx):
            return sm_ref if a == 3 else srcs[a].at[idx]

        local = [pltpu.make_async_copy(src_of(a, me), dsts[a].at[me], local_sems.at[a]) for a in range(n_arr)]
        for cp in local:
            cp.start()
        sends = []
        for k in range(1, N_DEV):
            peer, pidx = _peer(k, x, y, c)
            for a in range(n_arr):
                i = n_arr * (k - 1) + a
                cp = pltpu.make_async_remote_copy(
                    src_ref=src_of(a, pidx), dst_ref=dsts[a].at[me], send_sem=send_sems.at[i],
                    recv_sem=recv_sems.at[i], device_id=peer, device_id_type=pl.DeviceIdType.MESH)
                cp.start()
                sends.append(cp)
        for k in range(1, N_DEV):
            peer, pidx = _peer(k, x, y, c)
            for a in range(n_arr):
                i = n_arr * (k - 1) + a
                pltpu.make_async_remote_copy(
                    src_ref=src_of(a, me), dst_ref=dsts[a].at[pidx], send_sem=send_sems.at[i],
                    recv_sem=recv_sems.at[i], device_id=peer, device_id_type=pl.DeviceIdType.MESH).wait_recv()
        for cp in sends:
            cp.wait_send()
        for cp in local:
            cp.wait()

    hbm = pl.BlockSpec(memory_space=pl.ANY)
    return pl.pallas_call(
        body, name="exchange_grads",
        in_specs=[hbm] * 4, out_specs=[hbm] * 4,
        out_shape=[jax.ShapeDtypeStruct((N_DEV,) + parts_w_in.shape[1:], F32),
                   jax.ShapeDtypeStruct((N_DEV,) + parts_w_out.shape[1:], F32),
                   jax.ShapeDtypeStruct((N_DEV,) + parts_wg.shape[1:], F32),
                   jax.ShapeDtypeStruct((N_DEV,) + small.shape, F32)],
        scratch_shapes=[pltpu.SemaphoreType.DMA((n_arr * (N_DEV - 1),)), pltpu.SemaphoreType.DMA((n_arr * (N_DEV - 1),)),
                        pltpu.SemaphoreType.DMA((n_arr,))],
        compiler_params=_cparams(),
    )(parts_w_in, parts_w_out, parts_wg, small)


def _adamw(recv, w, m, v, name):
    rows, width = w.shape
    tr = 128 if rows % 128 == 0 else rows

    def body(r_ref, w_ref, m_ref, v_ref, g_ref, d_ref, nm_ref, nv_ref):
        g = r_ref[0]
        for j in range(1, N_DEV):
            g = g + r_ref[j]
        nm = ADAM_B1 * m_ref[...] + (1.0 - ADAM_B1) * g
        nv = ADAM_B2 * v_ref[...] + (1.0 - ADAM_B2) * (g * g)
        m_hat = nm / (1.0 - ADAM_B1 ** ADAM_STEP)
        v_hat = nv / (1.0 - ADAM_B2 ** ADAM_STEP)
        g_ref[...] = g
        d_ref[...] = -ADAM_LR * (m_hat / (jnp.sqrt(v_hat) + ADAM_EPS) + ADAM_WD * w_ref[...])
        nm_ref[...] = nm
        nv_ref[...] = nv

    spec = _rows(tr, width)
    return pl.pallas_call(
        body, name=name, grid=(rows // tr,),
        in_specs=[pl.BlockSpec((N_DEV, tr, width), lambda i: (0, i, 0)), spec, spec, spec],
        out_specs=[spec] * 4,
        out_shape=[jax.ShapeDtypeStruct((rows, width), F32)] * 4,
        compiler_params=_cparams(dimension_semantics=("arbitrary",)),
    )(recv, w, m, v)


def _pack_small(ln_g, ln_b, b_gate, norm_w, sinks):
    def rows8(t):
        t = t.reshape(-1)
        t = jnp.pad(t, (0, 1024 - t.shape[0]))
        return t.reshape(8, 128)

    return jnp.concatenate([rows8(t) for t in (ln_g, ln_b, b_gate, norm_w, sinks)], axis=0)


def _unpack_small(p):
    flat = [p[8 * i:8 * (i + 1)].reshape(1, 1024) for i in range(5)]
    return flat[0], flat[1], flat[2][:, :256], flat[3][:, :128], flat[4][:, :8]


def kernel(x, positions, w_in, gla_w_gate_up, gla_b_gate, attn_sinks, gla_norm_w, w_out, ln_g, ln_b, loss_target, m_w_in, m_gla_w_gate_up, m_gla_b_gate, m_attn_sinks, m_gla_norm_w, m_w_out, m_ln_g, m_ln_b, v_w_in, v_gla_w_gate_up, v_gla_b_gate, v_attn_sinks, v_gla_norm_w, v_w_out, v_ln_g, v_ln_b):
    w_in_all, w_out_all, wg_all = _all_gather_weights(w_in[0], w_out[0], gla_w_gate_up[0])
    w_in_full = jnp.transpose(w_in_all, (1, 0, 2)).reshape(D_MODEL, D_IN_PROJ)
    w_out_full = w_out_all.reshape(D_MODEL, D_MODEL)
    wg_full = jnp.transpose(wg_all, (1, 0, 2)).reshape(GLA_RANK, 256)

    loss, grad_x, g_w_in, g_wg, g_bg, g_sinks, g_nw, g_w_out, g_ln = _local_step(
        x[0], positions[0], w_in_full, wg_full, gla_b_gate, attn_sinks[0], gla_norm_w, w_out_full, ln_g, ln_b,
        loss_target[0])

    parts_w_in = jnp.transpose(g_w_in.reshape(D_MODEL, N_DEV, D_IN_SHARD), (1, 0, 2))
    parts_w_out = g_w_out.reshape(N_DEV, D_OUT_SHARD, D_MODEL)
    parts_wg = jnp.transpose(g_wg.reshape(GLA_RANK, N_DEV, 32), (1, 0, 2))
    small = _pack_small(g_ln[0:1], g_ln[1:2], g_bg, g_nw, g_sinks[:, 0].reshape(1, SWA_Q_HEADS))
    r_in, r_out, r_wg, r_small = _exchange_grads(parts_w_in, parts_w_out, parts_wg, small)

    upd_in = _adamw(r_in, w_in[0], m_w_in[0], v_w_in[0], "adamw_w_in")
    upd_out = _adamw(r_out, w_out[0], m_w_out[0], v_w_out[0], "adamw_w_out")
    upd_wg = _adamw(r_wg, gla_w_gate_up[0], m_gla_w_gate_up[0], v_gla_w_gate_up[0], "adamw_wg")
    upd_small = _adamw(
        r_small,
        _pack_small(ln_g, ln_b, gla_b_gate, gla_norm_w, attn_sinks),
        _pack_small(m_ln_g, m_ln_b, m_gla_b_gate, m_gla_norm_w, m_attn_sinks),
        _pack_small(v_ln_g, v_ln_b, v_gla_b_gate, v_gla_norm_w, v_attn_sinks), "adamw_small")

    total = lax.psum(loss[0, 0], ("x", "y", "c"))
    outs = [total, grad_x[None]]
    for kind in range(4):
        s_ln_g, s_ln_b, s_bg, s_nw, s_sinks = _unpack_small(upd_small[kind])
        outs += [upd_in[kind][None], upd_wg[kind][None], s_bg, s_sinks, s_nw, upd_out[kind][None], s_ln_g, s_ln_b]
    return tuple(outs)
```

```python
import functools

import jax
import jax.numpy as jnp
from jax import lax
from jax.experimental import pallas as pl
from jax.experimental.pallas import tpu as pltpu

F32 = jnp.float32
MXU_DTYPE = jnp.bfloat16
HIGHEST = lax.Precision.HIGHEST

N_DEV = 8
D_MODEL = 1024
SWA_Q_HEADS = 8
SWA_KV_HEADS = 2
SWA_GROUP = 4
SWA_HEAD_DIM = 64
BLOCK = 128
ROPE_THETA = 500000.0
ROT_DIM = 16
GLA_HEADS = 4
GLA_DK = 64
GLA_DV = 128
GLA_RANK = 16
GLA_TAU = 16.0
GLA_CHUNK = 64
D_IN_PROJ = 2832
D_IN_SHARD = D_IN_PROJ // N_DEV
D_OUT_SHARD = D_MODEL // N_DEV
OFF = (0, 512, 640, 768, 1280, 1536, 1792, 2304, 2816, 2832)
EPS = 1e-5
ALPHA = 2.0 ** 0.25
SWA_SCALE = SWA_HEAD_DIM ** -0.5
GLA_SCALE = GLA_DK ** -0.5
ADAM_LR = 0.001
ADAM_B1 = 0.9
ADAM_B2 = 0.999
ADAM_EPS = 1e-08
ADAM_WD = 0.01
ADAM_STEP = 10
VMEM_LIMIT = 56 * 1024 * 1024

_NT = (((1,), (1,)), ((), ()))
_TN = (((0,), (0,)), ((), ()))


def _mm(a, b):
    return jnp.dot(a, b, preferred_element_type=F32)


def _mm_nt(a, b):
    return lax.dot_general(a, b, _NT, preferred_element_type=F32)


def _mm_tn(a, b):
    return lax.dot_general(a, b, _TN, preferred_element_type=F32)


def _sigmoid(t):
    return 1.0 / (1.0 + jnp.exp(-t))


def _cparams(**kw):
    return pltpu.CompilerParams(vmem_limit_bytes=VMEM_LIMIT, **kw)


def _full(shape):
    return pl.BlockSpec(shape, lambda *_: (0,) * len(shape))


def _rows(tile, width):
    return pl.BlockSpec((tile, width), lambda i: (i, 0))


def _rope_tables(positions):
    half = ROT_DIM // 2
    inv_freq = ROPE_THETA ** (-jnp.arange(half, dtype=F32) / half)
    lane = jnp.arange(128, dtype=jnp.int32) % SWA_HEAD_DIM
    ang = positions.astype(F32)[:, None] * jnp.tile(inv_freq, 128 // half)[None, :]
    cos, sin = jnp.cos(ang), jnp.sin(ang)
    c = jnp.where(lane < ROT_DIM, cos, 1.0)
    s1 = jnp.where(lane < half, -sin, 0.0)
    s2 = jnp.where((lane >= half) & (lane < ROT_DIM), sin, 0.0)
    return c, s1, s2


def _rope(t, c, s1, s2):
    return t * c + pltpu.roll(t, 120, 1) * s1 + pltpu.roll(t, 8, 1) * s2


def _rope_t(g, c, s1, s2):
    return g * c + pltpu.roll(g * s1, 8, 1) + pltpu.roll(g * s2, 120, 1)


def _in_proj(x, w_in, wg, b_gate, rope):
    s = x.shape[0]
    ts = min(512, s)
    widths = [OFF[i + 1] - OFF[i] for i in range(9)]

    def body(x_ref, w_ref, wg_ref, bg_ref, c_ref, s1_ref, s2_ref,
             qa_ref, ka_ref, va_ref, ga_ref, qb_ref, kb_ref, vb_ref, gb_ref, rb_ref, la_ref, oms_ref):
        xb = x_ref[...].astype(MXU_DTYPE)
        c, s1, s2 = c_ref[...], s1_ref[...], s2_ref[...]
        i0 = pl.program_id(0)

        @pl.when(i0 == 0)
        def _():
            ka_ref[0:BLOCK, :] = jnp.zeros((BLOCK, 128), F32)
            va_ref[0:BLOCK, :] = jnp.zeros((BLOCK, 128), F32)

        kv_rows = pl.ds(pl.multiple_of(BLOCK + i0 * ts, BLOCK), ts)

        def cols(i):
            return _mm(xb, w_ref[:, OFF[i]:OFF[i + 1]])

        qa = cols(0)
        for i in range(4):
            qa_ref[:, 128 * i:128 * (i + 1)] = _rope(qa[:, 128 * i:128 * (i + 1)], c, s1, s2)
        ka_ref[kv_rows, :] = _rope(cols(1), c, s1, s2)
        va_ref[kv_rows, :] = cols(2)
        ga_ref[...] = cols(3)
        qb_ref[...] = cols(4)
        kb_ref[...] = cols(5)
        vb_ref[...] = cols(6)
        gb_ref[...] = cols(7)
        rb = cols(8)
        rb_ref[...] = rb
        logit = _mm(rb.astype(MXU_DTYPE), wg_ref[...]) + bg_ref[...]
        e = jnp.exp(-jnp.abs(logit))
        la_ref[...] = (jnp.minimum(logit, 0.0) - jnp.log(1.0 + e)) / GLA_TAU
        oms_ref[...] = jnp.where(logit >= 0.0, e, 1.0) / (1.0 + e)

    out_shape = [jax.ShapeDtypeStruct((s + BLOCK if i in (1, 2) else s, w), F32) for i, w in enumerate(widths)]
    out_shape += [jax.ShapeDtypeStruct((s, 256), F32)] * 2
    return pl.pallas_call(
        body, name="in_proj", grid=(s // ts,),
        in_specs=[_rows(ts, D_MODEL), _full((D_MODEL, D_IN_PROJ)), _full((GLA_RANK, 256)), _full((1, 256)),
                  _rows(ts, 128), _rows(ts, 128), _rows(ts, 128)],
        out_specs=[_full((s + BLOCK, w)) if i in (1, 2) else _rows(ts, w) for i, w in enumerate(widths)]
        + [_rows(ts, 256)] * 2,
        out_shape=out_shape,
        compiler_params=_cparams(dimension_semantics=("arbitrary",)),
    )(x, w_in, wg, b_gate, *rope)


def _swa_mask(n):
    qi = lax.broadcasted_iota(jnp.int32, (BLOCK, 2 * BLOCK), 0)
    ki = lax.broadcasted_iota(jnp.int32, (BLOCK, 2 * BLOCK), 1)
    dist = qi + BLOCK - ki
    return (dist >= 0) & (dist < BLOCK) & ((n > 0) | (ki >= BLOCK))


def _swa_probs(qh, kj, mask, sink):
    sc = _mm_nt(qh, kj) * SWA_SCALE
    sc = jnp.where(mask, sc, -jnp.inf)
    m = jnp.maximum(jnp.max(sc, axis=1, keepdims=True), sink)
    p = jnp.exp(sc - m)
    ps = jnp.exp(sink - m)
    denom = jnp.sum(p, axis=1, keepdims=True) + ps
    return p / denom, ps / denom


def _swa_fwd(sinks, qa, k_pad, v_pad, ga):
    s = qa.shape[0]

    def body(sink_ref, qa_ref, ga_ref, k_ref, v_ref, attn_ref, cat_ref):
        n = pl.program_id(0)
        start = pl.multiple_of(n * BLOCK, BLOCK)
        kw = k_ref[pl.ds(start, 2 * BLOCK), :].astype(MXU_DTYPE)
        vw = v_ref[pl.ds(start, 2 * BLOCK), :].astype(MXU_DTYPE)
        mask = _swa_mask(n)
        q = qa_ref[...].astype(MXU_DTYPE)
        outs = []
        for h in range(SWA_Q_HEADS):
            j = h // SWA_GROUP
            kj = kw[:, 64 * j:64 * (j + 1)]
            vj = vw[:, 64 * j:64 * (j + 1)]
            probs, _ = _swa_probs(q[:, 64 * h:64 * (h + 1)], kj, mask, sink_ref[h])
            outs.append(_mm(probs.astype(MXU_DTYPE), vj))
        o = jnp.concatenate(outs, axis=1)
        attn_ref[...] = o
        g = ga_ref[...]
        cat_ref[...] = (o * (g * _sigmoid(g))).astype(cat_ref.dtype)

    return pl.pallas_call(
        body, name="swa_fwd", grid=(s // BLOCK,),
        in_specs=[pl.BlockSpec(memory_space=pltpu.SMEM), _rows(BLOCK, 512), _rows(BLOCK, 512),
                  _full((s + BLOCK, 128)), _full((s + BLOCK, 128))],
        out_specs=[_rows(BLOCK, 512), _rows(BLOCK, 512)],
        out_shape=[jax.ShapeDtypeStruct((s, 512), F32), jax.ShapeDtypeStruct((s, 512), MXU_DTYPE)],
        compiler_params=_cparams(dimension_semantics=("arbitrary",)),
    )(sinks, qa, ga, k_pad, v_pad)


def _tri(lower):
    r = lax.broadcasted_iota(jnp.int32, (GLA_CHUNK, GLA_CHUNK), 0)
    c = lax.broadcasted_iota(jnp.int32, (GLA_CHUNK, GLA_CHUNK), 1)
    return (r >= c) if lower else (r <= c)


def _gla_decays(la, tri_f32):
    b = jnp.dot(tri_f32, la, precision=HIGHEST, preferred_element_type=F32)
    bl = b[GLA_CHUNK - 1:GLA_CHUNK, :]
    return b, bl


def _gla_fwd(qb, kb, vb, la, gb, norm_w):
    s = qb.shape[0]
    tb = min(256, s)
    ch = tb // GLA_CHUNK

    def body(qb_ref, kb_ref, vb_ref, la_ref, gb_ref, nw_ref, o_ref, cat_ref, sp_ref, st_ref):
        @pl.when(pl.program_id(0) == 0)
        def _():
            st_ref[...] = jnp.zeros_like(st_ref)

        causal = _tri(True)
        tri = causal.astype(F32)
        nw = nw_ref[...]
        for ci in range(ch):
            rows = slice(GLA_CHUNK * ci, GLA_CHUNK * (ci + 1))
            b, bl = _gla_decays(la_ref[rows, :], tri)
            k = kb_ref[rows, :]
            qd = ((qb_ref[rows, :] * GLA_SCALE) * jnp.exp(b)).astype(MXU_DTYPE)
            ki = (k * jnp.exp(-b)).astype(MXU_DTYPE)
            ke = (k * jnp.exp(bl - b)).astype(MXU_DTYPE)
            st = st_ref[...]
            sp_ref[ci] = st
            st16 = st.astype(MXU_DTYPE)
            v = vb_ref[rows, :].astype(MXU_DTYPE)
            g = gb_ref[rows, :]
            silu = g * _sigmoid(g)
            upd, o_all, cat_all = [], [], []
            for h in range(GLA_HEADS):
                lk = slice(GLA_DK * h, GLA_DK * (h + 1))
                lv = slice(GLA_DV * h, GLA_DV * (h + 1))
                a = jnp.where(causal, _mm_nt(qd[:, lk], ki[:, lk]), 0.0)
                oh = _mm(a.astype(MXU_DTYPE), v[:, lv]) + _mm_nt(qd[:, lk], st16[:, lk])
                upd.append(_mm_tn(v[:, lv], ke[:, lk]))
                r = lax.rsqrt(jnp.mean(oh * oh, axis=1, keepdims=True) + EPS)
                o_all.append(oh)
                cat_all.append(oh * r * nw * silu[:, lv])
            st_ref[...] = st * jnp.exp(bl) + jnp.concatenate(upd, axis=1)
            o_ref[rows, :] = jnp.concatenate(o_all, axis=1)
            cat_ref[rows, :] = jnp.concatenate(cat_all, axis=1).astype(cat_ref.dtype)

    return pl.pallas_call(
        body, name="gla_fwd", grid=(s // tb,),
        in_specs=[_rows(tb, 256), _rows(tb, 256), _rows(tb, 512), _rows(tb, 256), _rows(tb, 512), _full((1, 128))],
        out_specs=[_rows(tb, 512), _rows(tb, 512), pl.BlockSpec((ch, GLA_DV, 256), lambda i: (i, 0, 0))],
        out_shape=[jax.ShapeDtypeStruct((s, 512), F32), jax.ShapeDtypeStruct((s, 512), MXU_DTYPE),
                   jax.ShapeDtypeStruct((s // GLA_CHUNK, GLA_DV, 256), F32)],
        scratch_shapes=[pltpu.VMEM((GLA_DV, 256), F32)],
        compiler_params=_cparams(dimension_semantics=("arbitrary",)),
    )(qb, kb, vb, la, gb, norm_w)


def _out_ln_loss(cat_a, cat_b, w_out, x, target, ln_g, ln_b):
    s = x.shape[0]
    ts = min(256, s)

    def body(ca_ref, cb_ref, w_ref, x_ref, t_ref, g_ref, b_ref,
             loss_ref, gx_ref, da_ref, db_ref, gw_ref, gln_ref):
        @pl.when(pl.program_id(0) == 0)
        def _():
            loss_ref[...] = jnp.zeros_like(loss_ref)
            gw_ref[...] = jnp.zeros_like(gw_ref)
            gln_ref[...] = jnp.zeros_like(gln_ref)

        ca, cb = ca_ref[...], cb_ref[...]
        mix = _mm(ca, w_ref[0:512, :]) + _mm(cb, w_ref[512:1024, :])
        h = ALPHA * x_ref[...] + mix
        mu = jnp.mean(h, axis=1, keepdims=True)
        hc = h - mu
        rstd = lax.rsqrt(jnp.mean(hc * hc, axis=1, keepdims=True) + EPS)
        xhat = hc * rstd
        g = g_ref[...]
        err = xhat * g + b_ref[...] - t_ref[...]
        loss_ref[...] += 0.5 * jnp.sum(jnp.mean(err * err, axis=1, keepdims=True))
        dy = err * (1.0 / D_MODEL)
        gln_ref[0:1, :] += jnp.sum(dy * xhat, axis=0, keepdims=True)
        gln_ref[1:2, :] += jnp.sum(dy, axis=0, keepdims=True)
        dxh = dy * g
        dh = rstd * (dxh - jnp.mean(dxh, axis=1, keepdims=True)
                     - xhat * jnp.mean(dxh * xhat, axis=1, keepdims=True))
        gx_ref[...] = ALPHA * dh
        dh16 = dh.astype(MXU_DTYPE)
        da_ref[...] = _mm_nt(dh16, w_ref[0:512, :])
        db_ref[...] = _mm_nt(dh16, w_ref[512:1024, :])
        gw_ref[0:512, :] += _mm_tn(ca, dh16)
        gw_ref[512:1024, :] += _mm_tn(cb, dh16)

    return pl.pallas_call(
        body, name="out_ln_loss", grid=(s // ts,),
        in_specs=[_rows(ts, 512), _rows(ts, 512), _full((D_MODEL, D_MODEL)), _rows(ts, D_MODEL), _rows(ts, D_MODEL),
                  _full((1, D_MODEL)), _full((1, D_MODEL))],
        out_specs=[_full((1, 128)), _rows(ts, D_MODEL), _rows(ts, 512), _rows(ts, 512),
                   _full((D_MODEL, D_MODEL)), _full((2, D_MODEL))],
        out_shape=[jax.ShapeDtypeStruct((1, 128), F32), jax.ShapeDtypeStruct((s, D_MODEL), F32),
                   jax.ShapeDtypeStruct((s, 512), F32), jax.ShapeDtypeStruct((s, 512), F32),
                   jax.ShapeDtypeStruct((D_MODEL, D_MODEL), F32), jax.ShapeDtypeStruct((2, D_MODEL), F32)],
        compiler_params=_cparams(dimension_semantics=("arbitrary",)),
    )(cat_a, cat_b, w_out, x, target, ln_g, ln_b)


def _swa_bwd(sinks, qa, k_pad, v_pad, attn, ga, d_cat_a, rope):
    s = qa.shape[0]

    def body(sink_ref, qa_ref, ga_ref, at_ref, dc_ref, c_ref, s1_ref, s2_ref, k_ref, v_ref,
             dq_ref, dg_ref, dk_ref, dv_ref, ds_ref):
        n = pl.program_id(0)

        @pl.when(n == 0)
        def _():
            dk_ref[...] = jnp.zeros_like(dk_ref)
            dv_ref[...] = jnp.zeros_like(dv_ref)
            ds_ref[...] = jnp.zeros_like(ds_ref)

        start = pl.multiple_of(n * BLOCK, BLOCK)
        kw = k_ref[pl.ds(start, 2 * BLOCK), :].astype(MXU_DTYPE)
        vw = v_ref[pl.ds(start, 2 * BLOCK), :].astype(MXU_DTYPE)
        mask = _swa_mask(n)
        q = qa_ref[...].astype(MXU_DTYPE)
        g = ga_ref[...]
        sg = _sigmoid(g)
        o = at_ref[...]
        dc = dc_ref[...]
        do = dc * (g * sg)
        dg_ref[...] = dc * o * (sg * (1.0 + g * (1.0 - sg)))
        do16 = do.astype(MXU_DTYPE)
        od = do * o
        dq, dk, dv = [], [], []
        for j in range(SWA_KV_HEADS):
            kj = kw[:, 64 * j:64 * (j + 1)]
            vj = vw[:, 64 * j:64 * (j + 1)]
            dkj = jnp.zeros((2 * BLOCK, SWA_HEAD_DIM), F32)
            dvj = jnp.zeros((2 * BLOCK, SWA_HEAD_DIM), F32)
            for h in range(SWA_GROUP * j, SWA_GROUP * (j + 1)):
                lh = slice(64 * h, 64 * (h + 1))
                probs, psink = _swa_probs(q[:, lh], kj, mask, sink_ref[h])
                delta = jnp.sum(od[:, lh], axis=1, keepdims=True)
                dp = _mm_nt(do16[:, lh], vj)
                dsc = ((probs * (dp - delta)) * SWA_SCALE).astype(MXU_DTYPE)
                ds_ref[h:h + 1, :] += jnp.zeros((1, 128), F32) - jnp.sum(psink * delta)
                dq.append(_mm(dsc, kj))
                dkj += _mm_tn(dsc, q[:, lh])
                dvj += _mm_tn(probs.astype(MXU_DTYPE), do16[:, lh])
            dk.append(dkj)
            dv.append(dvj)
        c, s1, s2 = c_ref[...], s1_ref[...], s2_ref[...]
        dqf = jnp.concatenate(dq, axis=1)
        for i in range(4):
            dq_ref[:, 128 * i:128 * (i + 1)] = _rope_t(dqf[:, 128 * i:128 * (i + 1)], c, s1, s2)
        dk_ref[pl.ds(start, 2 * BLOCK), :] += jnp.concatenate(dk, axis=1)
        dv_ref[pl.ds(start, 2 * BLOCK), :] += jnp.concatenate(dv, axis=1)

    return pl.pallas_call(
        body, name="swa_bwd", grid=(s // BLOCK,),
        in_specs=[pl.BlockSpec(memory_space=pltpu.SMEM)] + [_rows(BLOCK, 512)] * 4 + [_rows(BLOCK, 128)] * 3
        + [_full((s + BLOCK, 128))] * 2,
        out_specs=[_rows(BLOCK, 512), _rows(BLOCK, 512), _full((s + BLOCK, 128)), _full((s + BLOCK, 128)),
                   _full((SWA_Q_HEADS, 128))],
        out_shape=[jax.ShapeDtypeStruct((s, 512), F32), jax.ShapeDtypeStruct((s, 512), F32),
                   jax.ShapeDtypeStruct((s + BLOCK, 128), F32), jax.ShapeDtypeStruct((s + BLOCK, 128), F32),
                   jax.ShapeDtypeStruct((SWA_Q_HEADS, 128), F32)],
        compiler_params=_cparams(dimension_semantics=("arbitrary",)),
    )(sinks, qa, ga, attn, d_cat_a, *rope, k_pad, v_pad)


def _gla_bwd(qb, kb, vb, la, oms, gb, o, sprev, d_cat_b, rb, wg, norm_w):
    s = qb.shape[0]
    tb = min(256, s)
    ch = tb // GLA_CHUNK
    nb = s // tb

    def body(qb_ref, kb_ref, vb_ref, la_ref, oms_ref, gb_ref, o_ref, sp_ref, dc_ref, rb_ref, wg_ref, nw_ref,
             dq_ref, dk_ref, dv_ref, dg_ref, dr_ref, gwg_ref, gbg_ref, gnw_ref, dst_ref):
        @pl.when(pl.program_id(0) == 0)
        def _():
            dst_ref[...] = jnp.zeros_like(dst_ref)
            gwg_ref[...] = jnp.zeros_like(gwg_ref)
            gbg_ref[...] = jnp.zeros_like(gbg_ref)
            gnw_ref[...] = jnp.zeros_like(gnw_ref)

        causal = _tri(True)
        tri = causal.astype(F32)
        tri_u = _tri(False).astype(F32)
        nw = nw_ref[...]
        wg16 = wg_ref[...]
        for ci in reversed(range(ch)):
            rows = slice(GLA_CHUNK * ci, GLA_CHUNK * (ci + 1))
            b, bl = _gla_decays(la_ref[rows, :], tri)
            eb, enb, ee, dec = jnp.exp(b), jnp.exp(-b), jnp.exp(bl - b), jnp.exp(bl)
            k = kb_ref[rows, :]
            qd = (qb_ref[rows, :] * GLA_SCALE) * eb
            ki = k * enb
            ke = k * ee
            qd16, ki16, ke16 = qd.astype(MXU_DTYPE), ki.astype(MXU_DTYPE), ke.astype(MXU_DTYPE)
            st = sp_ref[ci]
            st16 = st.astype(MXU_DTYPE)
            dst = dst_ref[...]
            dst16 = dst.astype(MXU_DTYPE)
            v16 = vb_ref[rows, :].astype(MXU_DTYPE)
            g = gb_ref[rows, :]
            sg = _sigmoid(g)
            silu = g * sg
            dsilu = sg * (1.0 + g * (1.0 - sg))
            oc = o_ref[rows, :]
            dc = dc_ref[rows, :]
            gnw = jnp.zeros((1, GLA_DV), F32)
            dg, dv, dqd, dki, dke, dstp, ddec = [], [], [], [], [], [], []
            for h in range(GLA_HEADS):
                lk = slice(GLA_DK * h, GLA_DK * (h + 1))
                lv = slice(GLA_DV * h, GLA_DV * (h + 1))
                oh = oc[:, lv]
                r = lax.rsqrt(jnp.mean(oh * oh, axis=1, keepdims=True) + EPS)
                d_on = dc[:, lv] * silu[:, lv]
                dg.append(dc[:, lv] * (oh * r * nw) * dsilu[:, lv])
                gnw += jnp.sum(d_on * oh * r, axis=0, keepdims=True)
                u = d_on * nw
                do = r * u - oh * (r * r * r) * jnp.mean(u * oh, axis=1, keepdims=True)
                do16 = do.astype(MXU_DTYPE)
                a16 = jnp.where(causal, _mm_nt(qd16[:, lk], ki16[:, lk]), 0.0).astype(MXU_DTYPE)
                da16 = jnp.where(causal, _mm_nt(do16, v16[:, lv]), 0.0).astype(MXU_DTYPE)
                dv.append(_mm_tn(a16, do16) + _mm_nt(ke16[:, lk], dst16[:, lk]))
                dqd.append(_mm(da16, ki16[:, lk]) + _mm(do16, st16[:, lk]))
                dki.append(_mm_tn(da16, qd16[:, lk]))
                dke.append(_mm(v16[:, lv], dst16[:, lk]))
                dstp.append(_mm_tn(do16, qd16[:, lk]))
                ddec.append(jnp.sum(dst[:, lk] * st[:, lk], axis=0, keepdims=True))
            gnw_ref[...] += gnw
            dqd, dki, dke = (jnp.concatenate(t, axis=1) for t in (dqd, dki, dke))
            dst_ref[...] = jnp.concatenate(dstp, axis=1) + dst * dec
            dq_ref[rows, :] = dqd * eb * GLA_SCALE
            dk_ref[rows, :] = dki * enb + dke * ee
            dv_ref[rows, :] = jnp.concatenate(dv, axis=1)
            dg_ref[rows, :] = jnp.concatenate(dg, axis=1)
            dke_ke = dke * ke
            db = dqd * qd - dki * ki - dke_ke
            dbl = jnp.sum(dke_ke, axis=0, keepdims=True) + jnp.concatenate(ddec, axis=1) * dec
            dla = jnp.dot(tri_u, db, precision=HIGHEST, preferred_element_type=F32) + dbl
            dlogit = dla * oms_ref[rows, :] * (1.0 / GLA_TAU)
            dl16 = dlogit.astype(MXU_DTYPE)
            gbg_ref[...] += jnp.sum(dlogit, axis=0, keepdims=True)
            gwg_ref[...] += _mm_tn(rb_ref[rows, :].astype(MXU_DTYPE), dl16)
            dr_ref[rows, :] = _mm_nt(dl16, wg16)

    def rev(width):
        return pl.BlockSpec((tb, width), lambda i: (nb - 1 - i, 0))

    return pl.pallas_call(
        body, name="gla_bwd", grid=(nb,),
        in_specs=[rev(256), rev(256), rev(512), rev(256), rev(256), rev(512), rev(512),
                  pl.BlockSpec((ch, GLA_DV, 256), lambda i: (nb - 1 - i, 0, 0)), rev(512), rev(GLA_RANK),
                  _full((GLA_RANK, 256)), _full((1, 128))],
        out_specs=[rev(256), rev(256), rev(512), rev(512), rev(GLA_RANK),
                   _full((GLA_RANK, 256)), _full((1, 256)), _full((1, 128))],
        out_shape=[jax.ShapeDtypeStruct((s, 256), F32), jax.ShapeDtypeStruct((s, 256), F32),
                   jax.ShapeDtypeStruct((s, 512), F32), jax.ShapeDtypeStruct((s, 512), F32),
                   jax.ShapeDtypeStruct((s, GLA_RANK), F32), jax.ShapeDtypeStruct((GLA_RANK, 256), F32),
                   jax.ShapeDtypeStruct((1, 256), F32), jax.ShapeDtypeStruct((1, 128), F32)],
        scratch_shapes=[pltpu.VMEM((GLA_DV, 256), F32)],
        compiler_params=_cparams(dimension_semantics=("arbitrary",)),
    )(qb, kb, vb, la, oms, gb, o, sprev, d_cat_b, rb, wg, norm_w)


def _in_proj_bwd_x(gx0, pieces, w_in, rope):
    s = gx0.shape[0]
    ts = min(256, s)
    widths = [OFF[i + 1] - OFF[i] for i in range(9)]

    def body(gx0_ref, *refs):
        piece_refs = refs[:9]
        w_ref, c_ref, s1_ref, s2_ref, gx_ref, dp_ref = refs[9:]
        kv_rows = pl.ds(pl.multiple_of(BLOCK + pl.program_id(0) * ts, BLOCK), ts)
        acc = gx0_ref[...]
        for i in range(9):
            t = piece_refs[i][kv_rows, :] if i in (1, 2) else piece_refs[i][...]
            if i == 1:
                t = _rope_t(t, c_ref[...], s1_ref[...], s2_ref[...])
            t16 = t.astype(MXU_DTYPE)
            dp_ref[:, OFF[i]:OFF[i + 1]] = t16
            acc += _mm_nt(t16, w_ref[:, OFF[i]:OFF[i + 1]])
        gx_ref[...] = acc

    return pl.pallas_call(
        body, name="in_proj_bwd_x", grid=(s // ts,),
        in_specs=[_rows(ts, D_MODEL)]
        + [_full((s + BLOCK, w)) if i in (1, 2) else _rows(ts, w) for i, w in enumerate(widths)]
        + [_full((D_MODEL, D_IN_PROJ))] + [_rows(ts, 128)] * 3,
        out_specs=[_rows(ts, D_MODEL), _rows(ts, D_IN_PROJ)],
        out_shape=[jax.ShapeDtypeStruct((s, D_MODEL), F32), jax.ShapeDtypeStruct((s, D_IN_PROJ), MXU_DTYPE)],
        compiler_params=_cparams(dimension_semantics=("arbitrary",)),
    )(gx0, *pieces, w_in, *rope)


def _in_proj_bwd_w(x, dproj):
    s = x.shape[0]
    ts = min(512, s)
    nsteps = s // ts
    col_chunks = [(OFF[i], OFF[i + 1]) for i in range(9)]

    def body(x_ref, dp_ref, gw_ref, acc_ref, stage_ref, sems):
        i = pl.program_id(0)

        @pl.when(i == 0)
        def _():
            acc_ref[...] = jnp.zeros_like(acc_ref)

        xb = x_ref[...].astype(MXU_DTYPE)
        for lo, hi in col_chunks:
            acc_ref[:, lo:hi] += _mm_tn(xb, dp_ref[:, lo:hi])

        @pl.when(i == nsteps - 1)
        def _():
            copies = []
            for j in range(N_DEV):
                slot = j % 2
                if j >= 2:
                    copies[j - 2].wait()
                stage_ref[slot] = acc_ref[:, D_IN_SHARD * j:D_IN_SHARD * (j + 1)]
                cp = pltpu.make_async_copy(stage_ref.at[slot], gw_ref.at[j], sems.at[slot])
                cp.start()
                copies.append(cp)
            copies[N_DEV - 2].wait()
            copies[N_DEV - 1].wait()

    return pl.pallas_call(
        body, name="in_proj_bwd_w", grid=(nsteps,),
        in_specs=[_rows(ts, D_MODEL), _rows(ts, D_IN_PROJ)],
        out_specs=pl.BlockSpec(memory_space=pl.ANY),
        out_shape=jax.ShapeDtypeStruct((N_DEV, D_MODEL, D_IN_SHARD), F32),
        scratch_shapes=[pltpu.VMEM((D_MODEL, D_IN_PROJ), F32), pltpu.VMEM((2, D_MODEL, D_IN_SHARD), F32),
                        pltpu.SemaphoreType.DMA((2,))],
        compiler_params=_cparams(dimension_semantics=("arbitrary",)),
    )(x, dproj)


def _local_step(x, positions, w_in, wg, b_gate, sinks, norm_w, w_out, ln_g, ln_b, target):
    rope = _rope_tables(positions)
    qa, k_pad, v_pad, ga, qb, kb, vb, gb, rb, la, oms = _in_proj(x, w_in, wg, b_gate, rope)
    attn, cat_a = _swa_fwd(sinks, qa, k_pad, v_pad, ga)
    o, cat_b, sprev = _gla_fwd(qb, kb, vb, la, gb, norm_w)
    loss, gx0, d_cat_a, d_cat_b, g_w_out, g_ln = _out_ln_loss(cat_a, cat_b, w_out, x, target, ln_g, ln_b)
    dqa, dga, dk_pad, dv_pad, g_sinks = _swa_bwd(sinks, qa, k_pad, v_pad, attn, ga, d_cat_a, rope)
    dqb, dkb, dvb, dgb, drb, g_wg, g_bg, g_nw = _gla_bwd(qb, kb, vb, la, oms, gb, o, sprev, d_cat_b, rb, wg, norm_w)
    pieces = (dqa, dk_pad, dv_pad, dga, dqb, dkb, dvb, dgb, drb)
    grad_x, dproj = _in_proj_bwd_x(gx0, pieces, w_in, rope)
    g_w_in = _in_proj_bwd_w(x, dproj)
    return loss, grad_x, g_w_in, g_wg, g_bg, g_sinks, g_nw, g_w_out, g_ln


def _mesh_pos():
    return lax.axis_index("x"), lax.axis_index("y"), lax.axis_index("c")


def _peer(k, x, y, c):
    px = (1 - x) if k & 4 else x
    py = (1 - y) if k & 2 else y
    pc = (1 - c) if k & 1 else c
    return (px, py, pc), 4 * px + 2 * py + pc


def _other_chips(x, y):
    return [(1 - x, y), (x, 1 - y), (1 - x, 1 - y)]


def _all_gather_weights(w_in_s, w_out_s, wg_s):
    n_arr = 3

    def body(win_ref, wout_ref, wg_ref, win_full, wout_full, wg_full, win_all, wout_all, wg_all, send_sems, recv_sems):
        x, y, c = _mesh_pos()
        me = 4 * x + 2 * y + c
        sibling = (x, y, 1 - c)
        chips = _other_chips(x, y)
        alls = (win_all, wout_all, wg_all)

        def copy(k, a, block, to):
            return pltpu.make_async_remote_copy(
                src_ref=alls[a].at[block], dst_ref=alls[a].at[block], send_sem=send_sems.at[n_arr * k + a],
                recv_sem=recv_sems.at[n_arr * k + a], device_id=to, device_id_type=pl.DeviceIdType.MESH)

        win_all[me] = win_ref[...].astype(win_all.dtype)
        wout_all[me] = wout_ref[...].astype(wout_all.dtype)
        wg_all[me] = wg_ref[...].astype(wg_all.dtype)
        sends = []
        for a in range(n_arr):
            for j, (cx, cy) in enumerate(chips):
                sends.append(copy(1 + j, a, me, (cx, cy, c)))
            sends.append(copy(0, a, me, sibling))
        for cp in sends:
            cp.start()
        for j, (cx, cy) in enumerate(chips):
            for a in range(n_arr):
                block = 4 * cx + 2 * cy + c
                copy(1 + j, a, block, sibling).wait_recv()
                fwd = copy(4 + j, a, block, sibling)
                fwd.start()
                sends.append(fwd)
        for a in range(n_arr):
            copy(0, a, 4 * x + 2 * y + (1 - c), sibling).wait_recv()
        for j, (cx, cy) in enumerate(chips):
            for a in range(n_arr):
                copy(4 + j, a, 4 * cx + 2 * cy + (1 - c), sibling).wait_recv()
        for cp in sends:
            cp.wait_send()
        for j in range(N_DEV):
            win_full[:, D_IN_SHARD * j:D_IN_SHARD * (j + 1)] = win_all[j]
            wout_full[D_OUT_SHARD * j:D_OUT_SHARD * (j + 1), :] = wout_all[j]
            wg_full[:, 32 * j:32 * (j + 1)] = wg_all[j]

    vmem = pl.BlockSpec(memory_space=pltpu.VMEM)
    n_copies = n_arr * (N_DEV - 1)
    return pl.pallas_call(
        body, name="all_gather_weights",
        in_specs=[vmem] * 3, out_specs=[vmem] * 3,
        out_shape=[jax.ShapeDtypeStruct((D_MODEL, D_IN_PROJ), MXU_DTYPE),
                   jax.ShapeDtypeStruct((D_MODEL, D_MODEL), MXU_DTYPE),
                   jax.ShapeDtypeStruct((GLA_RANK, 256), MXU_DTYPE)],
        scratch_shapes=[pltpu.VMEM((N_DEV, D_MODEL, D_IN_SHARD), MXU_DTYPE),
                        pltpu.VMEM((N_DEV, D_OUT_SHARD, D_MODEL), MXU_DTYPE),
                        pltpu.VMEM((N_DEV, GLA_RANK, 32), MXU_DTYPE),
                        pltpu.SemaphoreType.DMA((n_copies,)), pltpu.SemaphoreType.DMA((n_copies,))],
        compiler_params=_cparams(),
    )(w_in_s, w_out_s, wg_s)


def _reduce_grads(parts_w_in, parts_w_out, parts_wg, small):
    def body(pin_ref, pout_ref, pwg_ref, sm_ref, gin_ref, gout_ref, rwg_ref, rsm_ref,
             own_in, sib_in, snd_in, rcv_in, own_out, sib_out, snd_out, rcv_out,
             loc_sems, d2d_send, d2d_recv, ici_send, ici_recv, sm_send, sm_recv, sm_loc):
        x, y, c = _mesh_pos()
        me = 4 * x + 2 * y + c
        sibling = (x, y, 1 - c)
        chips = [(x, y)] + _other_chips(x, y)
        parts = (pin_ref, pout_ref)
        own, sib, snd, rcv, outs = (own_in, own_out), (sib_in, sib_out), (snd_in, snd_out), (rcv_in, rcv_out), (gin_ref, gout_ref)

        small_dsts = (rwg_ref, rsm_ref)

        def small_src(a, block):
            return pwg_ref.at[block] if a == 0 else sm_ref

        small_local = [pltpu.make_async_copy(small_src(a, me), small_dsts[a].at[me], sm_loc.at[a]) for a in range(2)]
        for cp in small_local:
            cp.start()
        small_sends = []
        for k in range(1, N_DEV):
            peer, pidx = _peer(k, x, y, c)
            for a in range(2):
                i = 2 * (k - 1) + a
                cp = pltpu.make_async_remote_copy(
                    src_ref=small_src(a, pidx), dst_ref=small_dsts[a].at[me], send_sem=sm_send.at[i],
                    recv_sem=sm_recv.at[i], device_id=peer, device_id_type=pl.DeviceIdType.MESH)
                cp.start()
                small_sends.append(cp)

        local, d2d = {}, {}
        for r in (1, 2, 3, 0):
            cx, cy = chips[r]
            for a in range(2):
                i = 2 * r + a
                local[r, a] = pltpu.make_async_copy(parts[a].at[4 * cx + 2 * cy + c], own[a].at[r], loc_sems.at[i])
                local[r, a].start()
                d2d[r, a] = pltpu.make_async_remote_copy(
                    src_ref=parts[a].at[4 * cx + 2 * cy + (1 - c)], dst_ref=sib[a].at[r], send_sem=d2d_send.at[i],
                    recv_sem=d2d_recv.at[i], device_id=sibling, device_id_type=pl.DeviceIdType.MESH)
                d2d[r, a].start()
        ici = {}
        for r in (1, 2, 3):
            cx, cy = chips[r]
            for a in range(2):
                i = 2 * (r - 1) + a
                local[r, a].wait()
                d2d[r, a].wait_recv()
                snd[a][r - 1] = (own[a][r] + sib[a][r]).astype(snd[a].dtype)
                ici[r, a] = pltpu.make_async_remote_copy(
                    src_ref=snd[a].at[r - 1], dst_ref=rcv[a].at[r - 1], send_sem=ici_send.at[i],
                    recv_sem=ici_recv.at[i], device_id=(cx, cy, c), device_id_type=pl.DeviceIdType.MESH)
                ici[r, a].start()
        for a in range(2):
            local[0, a].wait()
            d2d[0, a].wait_recv()
            acc = own[a][0] + sib[a][0]
            for r in (1, 2, 3):
                ici[r, a].wait_recv()
                acc = acc + rcv[a][r - 1].astype(F32)
            outs[a][...] = acc

        for k in range(1, N_DEV):
            peer, pidx = _peer(k, x, y, c)
            for a in range(2):
                i = 2 * (k - 1) + a
                pltpu.make_async_remote_copy(
                    src_ref=small_src(a, me), dst_ref=small_dsts[a].at[pidx], send_sem=sm_send.at[i],
                    recv_sem=sm_recv.at[i], device_id=peer, device_id_type=pl.DeviceIdType.MESH).wait_recv()
        for cp in small_sends + list(d2d.values()) + list(ici.values()):
            cp.wait_send()
        for cp in small_local:
            cp.wait()

    hbm = pl.BlockSpec(memory_space=pl.ANY)
    vmem = pl.BlockSpec(memory_space=pltpu.VMEM)
    in_blk, out_blk = parts_w_in.shape[1:], parts_w_out.shape[1:]
    return pl.pallas_call(
        body, name="reduce_grads",
        in_specs=[hbm] * 4, out_specs=[vmem, vmem, hbm, hbm],
        out_shape=[jax.ShapeDtypeStruct(in_blk, F32), jax.ShapeDtypeStruct(out_blk, F32),
                   jax.ShapeDtypeStruct((N_DEV,) + parts_wg.shape[1:], F32),
                   jax.ShapeDtypeStruct((N_DEV,) + small.shape, F32)],
        scratch_shapes=[pltpu.VMEM((4,) + in_blk, F32), pltpu.VMEM((4,) + in_blk, F32),
                        pltpu.VMEM((3,) + in_blk, MXU_DTYPE), pltpu.VMEM((3,) + in_blk, MXU_DTYPE),
                        pltpu.VMEM((4,) + out_blk, F32), pltpu.VMEM((4,) + out_blk, F32),
                        pltpu.VMEM((3,) + out_blk, MXU_DTYPE), pltpu.VMEM((3,) + out_blk, MXU_DTYPE),
                        pltpu.SemaphoreType.DMA((8,)), pltpu.SemaphoreType.DMA((8,)), pltpu.SemaphoreType.DMA((8,)),
                        pltpu.SemaphoreType.DMA((6,)), pltpu.SemaphoreType.DMA((6,)),
                        pltpu.SemaphoreType.DMA((2 * (N_DEV - 1),)), pltpu.SemaphoreType.DMA((2 * (N_DEV - 1),)),
                        pltpu.SemaphoreType.DMA((2,))],
        compiler_params=_cparams(),
    )(parts_w_in, parts_w_out, parts_wg, small)


def _adamw(recv, w, m, v, name):
    rows, width = w.shape
    tr = 128 if rows % 128 == 0 else rows
    n_parts = recv.shape[0]

    def body(r_ref, w_ref, m_ref, v_ref, g_ref, d_ref, nm_ref, nv_ref):
        g = r_ref[0]
        for j in range(1, n_parts):
            g = g + r_ref[j]
        nm = ADAM_B1 * m_ref[...] + (1.0 - ADAM_B1) * g
        nv = ADAM_B2 * v_ref[...] + (1.0 - ADAM_B2) * (g * g)
        m_hat = nm / (1.0 - ADAM_B1 ** ADAM_STEP)
        v_hat = nv / (1.0 - ADAM_B2 ** ADAM_STEP)
        g_ref[...] = g
        d_ref[...] = -ADAM_LR * (m_hat / (jnp.sqrt(v_hat) + ADAM_EPS) + ADAM_WD * w_ref[...])
        nm_ref[...] = nm
        nv_ref[...] = nv

    spec = _rows(tr, width)
    return pl.pallas_call(
        body, name=name, grid=(rows // tr,),
        in_specs=[pl.BlockSpec((n_parts, tr, width), lambda i: (0, i, 0)), spec, spec, spec],
        out_specs=[spec] * 4,
        out_shape=[jax.ShapeDtypeStruct((rows, width), F32)] * 4,
        compiler_params=_cparams(dimension_semantics=("arbitrary",)),
    )(recv, w, m, v)


def _pack_small(ln_g, ln_b, b_gate, norm_w, sinks, loss=None):
    def rows8(t):
        t = t.reshape(-1)
        t = jnp.pad(t, (0, 1024 - t.shape[0]))
        return t.reshape(8, 128)

    loss = jnp.zeros((1,), F32) if loss is None else loss
    return jnp.concatenate([rows8(t) for t in (ln_g, ln_b, b_gate, norm_w, sinks, loss)], axis=0)


def _unpack_small(p):
    flat = [p[8 * i:8 * (i + 1)].reshape(1, 1024) for i in range(5)]
    return flat[0], flat[1], flat[2][:, :256], flat[3][:, :128], flat[4][:, :8]


def kernel(x, positions, w_in, gla_w_gate_up, gla_b_gate, attn_sinks, gla_norm_w, w_out, ln_g, ln_b, loss_target, m_w_in, m_gla_w_gate_up, m_gla_b_gate, m_attn_sinks, m_gla_norm_w, m_w_out, m_ln_g, m_ln_b, v_w_in, v_gla_w_gate_up, v_gla_b_gate, v_attn_sinks, v_gla_norm_w, v_w_out, v_ln_g, v_ln_b):
    w_in_full, w_out_full, wg_full = _all_gather_weights(w_in[0], w_out[0], gla_w_gate_up[0])

    loss, grad_x, parts_w_in, g_wg, g_bg, g_sinks, g_nw, g_w_out, g_ln = _local_step(
        x[0], positions[0], w_in_full, wg_full, gla_b_gate, attn_sinks[0], gla_norm_w, w_out_full, ln_g, ln_b,
        loss_target[0])

    parts_w_out = g_w_out.reshape(N_DEV, D_OUT_SHARD, D_MODEL)
    parts_wg = jnp.transpose(g_wg.reshape(GLA_RANK, N_DEV, 32), (1, 0, 2))
    small = _pack_small(g_ln[0:1], g_ln[1:2], g_bg, g_nw, g_sinks[:, 0].reshape(1, SWA_Q_HEADS), loss[0, 0:1])
    g_in, g_out, r_wg, r_small = _reduce_grads(parts_w_in, parts_w_out, parts_wg, small)

    upd_in = _adamw(g_in[None], w_in[0], m_w_in[0], v_w_in[0], "adamw_w_in")
    upd_out = _adamw(g_out[None], w_out[0], m_w_out[0], v_w_out[0], "adamw_w_out")
    upd_wg = _adamw(r_wg, gla_w_gate_up[0], m_gla_w_gate_up[0], v_gla_w_gate_up[0], "adamw_wg")
    upd_small = _adamw(
        r_small,
        _pack_small(ln_g, ln_b, gla_b_gate, gla_norm_w, attn_sinks),
        _pack_small(m_ln_g, m_ln_b, m_gla_b_gate, m_gla_norm_w, m_attn_sinks),
        _pack_small(v_ln_g, v_ln_b, v_gla_b_gate, v_gla_norm_w, v_attn_sinks), "adamw_small")

    total = jnp.sum(r_small[:, 40, 0])
    outs = [total, grad_x[None]]
    for kind in range(4):
        s_ln_g, s_ln_b, s_bg, s_nw, s_sinks = _unpack_small(upd_small[kind])
        outs += [upd_in[kind][None], upd_wg[kind][None], s_bg, s_sinks, s_nw, upd_out[kind][None], s_ln_g, s_ln_b]
    return tuple(outs)
```

```python
import functools

import jax
import jax.numpy as jnp
from jax import lax
from jax.experimental import pallas as pl
from jax.experimental.pallas import tpu as pltpu

F32 = jnp.float32
MXU_DTYPE = jnp.bfloat16
HIGHEST = lax.Precision.HIGHEST

N_DEV = 8
D_MODEL = 1024
SWA_Q_HEADS = 8
SWA_KV_HEADS = 2
SWA_GROUP = 4
SWA_HEAD_DIM = 64
BLOCK = 128
ROPE_THETA = 500000.0
ROT_DIM = 16
GLA_HEADS = 4
GLA_DK = 64
GLA_DV = 128
GLA_RANK = 16
GLA_TAU = 16.0
GLA_CHUNK = 64
D_IN_PROJ = 2832
D_IN_SHARD = D_IN_PROJ // N_DEV
D_OUT_SHARD = D_MODEL // N_DEV
OFF = (0, 512, 640, 768, 1280, 1536, 1792, 2304, 2816, 2832)
EPS = 1e-5
ALPHA = 2.0 ** 0.25
SWA_SCALE = SWA_HEAD_DIM ** -0.5
GLA_SCALE = GLA_DK ** -0.5
ADAM_LR = 0.001
ADAM_B1 = 0.9
ADAM_B2 = 0.999
ADAM_EPS = 1e-08
ADAM_WD = 0.01
ADAM_STEP = 10
VMEM_LIMIT = 56 * 1024 * 1024

_NT = (((1,), (1,)), ((), ()))
_TN = (((0,), (0,)), ((), ()))


def _mm(a, b):
    return jnp.dot(a, b, preferred_element_type=F32)


def _mm_nt(a, b):
    return lax.dot_general(a, b, _NT, preferred_element_type=F32)


def _mm_tn(a, b):
    return lax.dot_general(a, b, _TN, preferred_element_type=F32)


def _sigmoid(t):
    return 1.0 / (1.0 + jnp.exp(-t))


def _cparams(**kw):
    return pltpu.CompilerParams(vmem_limit_bytes=VMEM_LIMIT, **kw)


def _full(shape):
    return pl.BlockSpec(shape, lambda *_: (0,) * len(shape))


def _rows(tile, width):
    return pl.BlockSpec((tile, width), lambda i: (i, 0))


def _rope_tables(positions):
    half = ROT_DIM // 2
    inv_freq = ROPE_THETA ** (-jnp.arange(half, dtype=F32) / half)
    lane = jnp.arange(128, dtype=jnp.int32) % SWA_HEAD_DIM
    ang = positions.astype(F32)[:, None] * jnp.tile(inv_freq, 128 // half)[None, :]
    cos, sin = jnp.cos(ang), jnp.sin(ang)
    c = jnp.where(lane < ROT_DIM, cos, 1.0)
    s1 = jnp.where(lane < half, -sin, 0.0)
    s2 = jnp.where((lane >= half) & (lane < ROT_DIM), sin, 0.0)
    return c, s1, s2


def _rope(t, c, s1, s2):
    return t * c + pltpu.roll(t, 120, 1) * s1 + pltpu.roll(t, 8, 1) * s2


def _rope_t(g, c, s1, s2):
    return g * c + pltpu.roll(g * s1, 8, 1) + pltpu.roll(g * s2, 120, 1)


def _in_proj(x, w_in, wg, b_gate, rope):
    s = x.shape[0]
    ts = min(512, s)
    widths = [OFF[i + 1] - OFF[i] for i in range(9)]

    def body(x_ref, w_ref, wg_ref, bg_ref, c_ref, s1_ref, s2_ref,
             qa_ref, ka_ref, va_ref, ga_ref, qb_ref, kb_ref, vb_ref, gb_ref, rb_ref, la_ref, oms_ref):
        xb = x_ref[...].astype(MXU_DTYPE)
        c, s1, s2 = c_ref[...], s1_ref[...], s2_ref[...]
        i0 = pl.program_id(0)

        @pl.when(i0 == 0)
        def _():
            ka_ref[0:BLOCK, :] = jnp.zeros((BLOCK, 128), F32)
            va_ref[0:BLOCK, :] = jnp.zeros((BLOCK, 128), F32)

        kv_rows = pl.ds(pl.multiple_of(BLOCK + i0 * ts, BLOCK), ts)

        def cols(i):
            return _mm(xb, w_ref[:, OFF[i]:OFF[i + 1]])

        qa = cols(0)
        for i in range(4):
            qa_ref[:, 128 * i:128 * (i + 1)] = _rope(qa[:, 128 * i:128 * (i + 1)], c, s1, s2)
        ka_ref[kv_rows, :] = _rope(cols(1), c, s1, s2)
        va_ref[kv_rows, :] = cols(2)
        ga_ref[...] = cols(3)
        qb_ref[...] = cols(4)
        kb_ref[...] = cols(5)
        vb_ref[...] = cols(6)
        gb_ref[...] = cols(7)
        rb = cols(8)
        rb_ref[...] = rb
        logit = _mm(rb.astype(MXU_DTYPE), wg_ref[...]) + bg_ref[...]
        e = jnp.exp(-jnp.abs(logit))
        la_ref[...] = (jnp.minimum(logit, 0.0) - jnp.log(1.0 + e)) / GLA_TAU
        oms_ref[...] = jnp.where(logit >= 0.0, e, 1.0) / (1.0 + e)

    out_shape = [jax.ShapeDtypeStruct((s + BLOCK if i in (1, 2) else s, w), F32) for i, w in enumerate(widths)]
    out_shape += [jax.ShapeDtypeStruct((s, 256), F32)] * 2
    return pl.pallas_call(
        body, name="in_proj", grid=(s // ts,),
        in_specs=[_rows(ts, D_MODEL), _full((D_MODEL, D_IN_PROJ)), _full((GLA_RANK, 256)), _full((1, 256)),
                  _rows(ts, 128), _rows(ts, 128), _rows(ts, 128)],
        out_specs=[_full((s + BLOCK, w)) if i in (1, 2) else _rows(ts, w) for i, w in enumerate(widths)]
        + [_rows(ts, 256)] * 2,
        out_shape=out_shape,
        compiler_params=_cparams(dimension_semantics=("arbitrary",)),
    )(x, w_in, wg, b_gate, *rope)


SWA_ROWS = SWA_GROUP * BLOCK


def _swa_bias():
    qi = lax.broadcasted_iota(jnp.int32, (2, SWA_ROWS, 2 * BLOCK), 1) & (BLOCK - 1)
    ki = lax.broadcasted_iota(jnp.int32, (2, SWA_ROWS, 2 * BLOCK), 2)
    first = lax.broadcasted_iota(jnp.int32, (2, SWA_ROWS, 2 * BLOCK), 0) == 0
    dist = qi + BLOCK - ki
    ok = (dist >= 0) & (dist < BLOCK) & (jnp.logical_not(first) | (ki >= BLOCK))
    return jnp.where(ok, 0.0, -jnp.inf).astype(F32)


def _swa_bias_spec():
    return pl.BlockSpec((1, SWA_ROWS, 2 * BLOCK), lambda n: (jnp.minimum(n, 1), 0, 0))


def _swa_dup(t, j):
    low = lax.broadcasted_iota(jnp.int32, t.shape, 1) < SWA_HEAD_DIM
    keep = low if j == 0 else jnp.logical_not(low)
    return jnp.where(keep, t, pltpu.roll(t, SWA_HEAD_DIM, 1)).astype(MXU_DTYPE)


def _swa_stack(t, j):
    low = lax.broadcasted_iota(jnp.int32, (BLOCK, 128), 1) < SWA_HEAD_DIM
    blocks = []
    for p in (2 * j, 2 * j + 1):
        tp = t[:, 128 * p:128 * (p + 1)]
        blocks += [jnp.where(low, tp, 0.0), jnp.where(low, 0.0, tp)]
    return jnp.concatenate(blocks, axis=0)


def _swa_unstack(t):
    low = lax.broadcasted_iota(jnp.int32, (BLOCK, 128), 1) < SWA_HEAD_DIM
    return [jnp.where(low, t[2 * BLOCK * i:2 * BLOCK * i + BLOCK], t[2 * BLOCK * i + BLOCK:2 * BLOCK * (i + 1)])
            for i in range(2)]


def _swa_sink_col(sink_ref, j):
    row = lax.broadcasted_iota(jnp.int32, (SWA_ROWS, 1), 0)
    col = jnp.full((SWA_ROWS, 1), sink_ref[SWA_GROUP * j], F32)
    for r in range(1, SWA_GROUP):
        col = jnp.where(row >= BLOCK * r, sink_ref[SWA_GROUP * j + r], col)
    return col


def _swa_probs(qs, kd, bias, sink):
    sc = _mm_nt(qs, kd) + bias
    m = jnp.maximum(jnp.max(sc, axis=1, keepdims=True), sink)
    p = jnp.exp(sc - m)
    ps = jnp.exp(sink - m)
    rinv = 1.0 / (jnp.sum(p, axis=1, keepdims=True) + ps)
    return p * rinv, ps * rinv


def _swa_fwd(sinks, qa, k_pad, v_pad, ga):
    s = qa.shape[0]

    def body(sink_ref, qa_ref, ga_ref, bias_ref, k_ref, v_ref, attn_ref, cat_ref):
        n = pl.program_id(0)
        start = pl.multiple_of(n * BLOCK, BLOCK)
        kw = k_ref[pl.ds(start, 2 * BLOCK), :]
        vw = v_ref[pl.ds(start, 2 * BLOCK), :]
        bias = bias_ref[0]
        q = qa_ref[...] * SWA_SCALE
        g = ga_ref[...]
        silu = g * _sigmoid(g)
        for j in range(SWA_KV_HEADS):
            qs = _swa_stack(q, j).astype(MXU_DTYPE)
            probs, _ = _swa_probs(qs, _swa_dup(kw, j), bias, _swa_sink_col(sink_ref, j))
            pairs = _swa_unstack(_mm(probs.astype(MXU_DTYPE), _swa_dup(vw, j)))
            for i in range(2):
                lanes = slice(128 * (2 * j + i), 128 * (2 * j + i + 1))
                attn_ref[:, lanes] = pairs[i]
                cat_ref[:, lanes] = (pairs[i] * silu[:, lanes]).astype(cat_ref.dtype)

    return pl.pallas_call(
        body, name="swa_fwd", grid=(s // BLOCK,),
        in_specs=[pl.BlockSpec(memory_space=pltpu.SMEM), _rows(BLOCK, 512), _rows(BLOCK, 512), _swa_bias_spec(),
                  _full((s + BLOCK, 128)), _full((s + BLOCK, 128))],
        out_specs=[_rows(BLOCK, 512), _rows(BLOCK, 512)],
        out_shape=[jax.ShapeDtypeStruct((s, 512), F32), jax.ShapeDtypeStruct((s, 512), MXU_DTYPE)],
        compiler_params=_cparams(dimension_semantics=("arbitrary",)),
    )(sinks, qa, ga, _swa_bias(), k_pad, v_pad)


def _tri(lower):
    r = lax.broadcasted_iota(jnp.int32, (GLA_CHUNK, GLA_CHUNK), 0)
    c = lax.broadcasted_iota(jnp.int32, (GLA_CHUNK, GLA_CHUNK), 1)
    return (r >= c) if lower else (r <= c)


def _gla_decays(la, tri_f32):
    b = jnp.dot(tri_f32, la, precision=HIGHEST, preferred_element_type=F32)
    bl = b[GLA_CHUNK - 1:GLA_CHUNK, :]
    return b, bl


def _gla_fwd(qb, kb, vb, la, gb, norm_w):
    s = qb.shape[0]
    tb = min(256, s)
    ch = tb // GLA_CHUNK

    def body(qb_ref, kb_ref, vb_ref, la_ref, gb_ref, nw_ref, o_ref, cat_ref, sp_ref, st_ref):
        @pl.when(pl.program_id(0) == 0)
        def _():
            st_ref[...] = jnp.zeros_like(st_ref)

        causal = _tri(True)
        tri = causal.astype(F32)
        nw = nw_ref[...]
        for ci in range(ch):
            rows = slice(GLA_CHUNK * ci, GLA_CHUNK * (ci + 1))
            b, bl = _gla_decays(la_ref[rows, :], tri)
            k = kb_ref[rows, :]
            qd = ((qb_ref[rows, :] * GLA_SCALE) * jnp.exp(b)).astype(MXU_DTYPE)
            ki = (k * jnp.exp(-b)).astype(MXU_DTYPE)
            ke = (k * jnp.exp(bl - b)).astype(MXU_DTYPE)
            st = st_ref[...]
            sp_ref[ci] = st
            st16 = st.astype(MXU_DTYPE)
            v = vb_ref[rows, :].astype(MXU_DTYPE)
            g = gb_ref[rows, :]
            silu = g * _sigmoid(g)
            upd, o_all, cat_all = [], [], []
            for h in range(GLA_HEADS):
                lk = slice(GLA_DK * h, GLA_DK * (h + 1))
                lv = slice(GLA_DV * h, GLA_DV * (h + 1))
                a = jnp.where(causal, _mm_nt(qd[:, lk], ki[:, lk]), 0.0)
                oh = _mm(a.astype(MXU_DTYPE), v[:, lv]) + _mm_nt(qd[:, lk], st16[:, lk])
                upd.append(_mm_tn(v[:, lv], ke[:, lk]))
                r = lax.rsqrt(jnp.mean(oh * oh, axis=1, keepdims=True) + EPS)
                o_all.append(oh)
                cat_all.append(oh * r * nw * silu[:, lv])
            st_ref[...] = st * jnp.exp(bl) + jnp.concatenate(upd, axis=1)
            o_ref[rows, :] = jnp.concatenate(o_all, axis=1)
            cat_ref[rows, :] = jnp.concatenate(cat_all, axis=1).astype(cat_ref.dtype)

    return pl.pallas_call(
        body, name="gla_fwd", grid=(s // tb,),
        in_specs=[_rows(tb, 256), _rows(tb, 256), _rows(tb, 512), _rows(tb, 256), _rows(tb, 512), _full((1, 128))],
        out_specs=[_rows(tb, 512), _rows(tb, 512), pl.BlockSpec((ch, GLA_DV, 256), lambda i: (i, 0, 0))],
        out_shape=[jax.ShapeDtypeStruct((s, 512), F32), jax.ShapeDtypeStruct((s, 512), MXU_DTYPE),
                   jax.ShapeDtypeStruct((s // GLA_CHUNK, GLA_DV, 256), F32)],
        scratch_shapes=[pltpu.VMEM((GLA_DV, 256), F32)],
        compiler_params=_cparams(dimension_semantics=("arbitrary",)),
    )(qb, kb, vb, la, gb, norm_w)


def _out_ln_loss(cat_a, cat_b, w_out, x, target, ln_g, ln_b):
    s = x.shape[0]
    ts = min(256, s)

    def body(ca_ref, cb_ref, w_ref, x_ref, t_ref, g_ref, b_ref,
             loss_ref, gx_ref, da_ref, db_ref, gw_ref, gln_ref):
        @pl.when(pl.program_id(0) == 0)
        def _():
            loss_ref[...] = jnp.zeros_like(loss_ref)
            gw_ref[...] = jnp.zeros_like(gw_ref)
            gln_ref[...] = jnp.zeros_like(gln_ref)

        ca, cb = ca_ref[...], cb_ref[...]
        mix = _mm(ca, w_ref[0:512, :]) + _mm(cb, w_ref[512:1024, :])
        h = ALPHA * x_ref[...] + mix
        mu = jnp.mean(h, axis=1, keepdims=True)
        hc = h - mu
        rstd = lax.rsqrt(jnp.mean(hc * hc, axis=1, keepdims=True) + EPS)
        xhat = hc * rstd
        g = g_ref[...]
        err = xhat * g + b_ref[...] - t_ref[...]
        loss_ref[...] += 0.5 * jnp.sum(jnp.mean(err * err, axis=1, keepdims=True))
        dy = err * (1.0 / D_MODEL)
        gln_ref[0:1, :] += jnp.sum(dy * xhat, axis=0, keepdims=True)
        gln_ref[1:2, :] += jnp.sum(dy, axis=0, keepdims=True)
        dxh = dy * g
        dh = rstd * (dxh - jnp.mean(dxh, axis=1, keepdims=True)
                     - xhat * jnp.mean(dxh * xhat, axis=1, keepdims=True))
        gx_ref[...] = ALPHA * dh
        dh16 = dh.astype(MXU_DTYPE)
        da_ref[...] = _mm_nt(dh16, w_ref[0:512, :])
        db_ref[...] = _mm_nt(dh16, w_ref[512:1024, :])
        gw_ref[0:512, :] += _mm_tn(ca, dh16)
        gw_ref[512:1024, :] += _mm_tn(cb, dh16)

    return pl.pallas_call(
        body, name="out_ln_loss", grid=(s // ts,),
        in_specs=[_rows(ts, 512), _rows(ts, 512), _full((D_MODEL, D_MODEL)), _rows(ts, D_MODEL), _rows(ts, D_MODEL),
                  _full((1, D_MODEL)), _full((1, D_MODEL))],
        out_specs=[_full((1, 128)), _rows(ts, D_MODEL), _rows(ts, 512), _rows(ts, 512),
                   _full((D_MODEL, D_MODEL)), _full((2, D_MODEL))],
        out_shape=[jax.ShapeDtypeStruct((1, 128), F32), jax.ShapeDtypeStruct((s, D_MODEL), F32),
                   jax.ShapeDtypeStruct((s, 512), F32), jax.ShapeDtypeStruct((s, 512), F32),
                   jax.ShapeDtypeStruct((D_MODEL, D_MODEL), F32), jax.ShapeDtypeStruct((2, D_MODEL), F32)],
        compiler_params=_cparams(dimension_semantics=("arbitrary",)),
    )(cat_a, cat_b, w_out, x, target, ln_g, ln_b)


def _swa_bwd(sinks, qa, k_pad, v_pad, attn, ga, d_cat_a, rope):
    s = qa.shape[0]

    def body(sink_ref, qa_ref, ga_ref, at_ref, dc_ref, c_ref, s1_ref, s2_ref, bias_ref, k_ref, v_ref,
             dq_ref, dg_ref, dk_ref, dv_ref, ds_ref):
        n = pl.program_id(0)

        @pl.when(n == 0)
        def _():
            dk_ref[...] = jnp.zeros_like(dk_ref)
            dv_ref[...] = jnp.zeros_like(dv_ref)
            ds_ref[...] = jnp.zeros_like(ds_ref)

        start = pl.multiple_of(n * BLOCK, BLOCK)
        kw = k_ref[pl.ds(start, 2 * BLOCK), :]
        vw = v_ref[pl.ds(start, 2 * BLOCK), :]
        bias = bias_ref[0]
        q = qa_ref[...] * SWA_SCALE
        g = ga_ref[...]
        sg = _sigmoid(g)
        o = at_ref[...]
        dc = dc_ref[...]
        do = dc * (g * sg)
        dg_ref[...] = dc * o * (sg * (1.0 + g * (1.0 - sg)))
        od = do * o
        c, s1, s2 = c_ref[...], s1_ref[...], s2_ref[...]
        dk, dv = [], []
        for j in range(SWA_KV_HEADS):
            kd, vd = _swa_dup(kw, j), _swa_dup(vw, j)
            qs = _swa_stack(q, j).astype(MXU_DTYPE)
            dos = _swa_stack(do, j).astype(MXU_DTYPE)
            probs, psink = _swa_probs(qs, kd, bias, _swa_sink_col(sink_ref, j))
            delta = jnp.sum(_swa_stack(od, j), axis=1, keepdims=True)
            dsc = (probs * (_mm_nt(dos, vd) - delta)).astype(MXU_DTYPE)
            dsink = psink * delta
            for r in range(SWA_GROUP):
                h = SWA_GROUP * j + r
                ds_ref[h:h + 1, :] += jnp.zeros((1, 128), F32) - jnp.sum(dsink[BLOCK * r:BLOCK * (r + 1)])
            dq = _swa_unstack(_mm(dsc, kd))
            for i in range(2):
                lanes = slice(128 * (2 * j + i), 128 * (2 * j + i + 1))
                dq_ref[:, lanes] = _rope_t(dq[i] * SWA_SCALE, c, s1, s2)
            dkj = _mm_tn(dsc, qs)
            dvj = _mm_tn(probs.astype(MXU_DTYPE), dos)
            dk.append(dkj + pltpu.roll(dkj, SWA_HEAD_DIM, 1))
            dv.append(dvj + pltpu.roll(dvj, SWA_HEAD_DIM, 1))
        low = lax.broadcasted_iota(jnp.int32, (2 * BLOCK, 128), 1) < SWA_HEAD_DIM
        dk_ref[pl.ds(start, 2 * BLOCK), :] += jnp.where(low, dk[0], dk[1])
        dv_ref[pl.ds(start, 2 * BLOCK), :] += jnp.where(low, dv[0], dv[1])

    return pl.pallas_call(
        body, name="swa_bwd", grid=(s // BLOCK,),
        in_specs=[pl.BlockSpec(memory_space=pltpu.SMEM)] + [_rows(BLOCK, 512)] * 4 + [_rows(BLOCK, 128)] * 3
        + [_swa_bias_spec()] + [_full((s + BLOCK, 128))] * 2,
        out_specs=[_rows(BLOCK, 512), _rows(BLOCK, 512), _full((s + BLOCK, 128)), _full((s + BLOCK, 128)),
                   _full((SWA_Q_HEADS, 128))],
        out_shape=[jax.ShapeDtypeStruct((s, 512), F32), jax.ShapeDtypeStruct((s, 512), F32),
                   jax.ShapeDtypeStruct((s + BLOCK, 128), F32), jax.ShapeDtypeStruct((s + BLOCK, 128), F32),
                   jax.ShapeDtypeStruct((SWA_Q_HEADS, 128), F32)],
        compiler_params=_cparams(dimension_semantics=("arbitrary",)),
    )(sinks, qa, ga, attn, d_cat_a, *rope, _swa_bias(), k_pad, v_pad)


def _gla_bwd(qb, kb, vb, la, oms, gb, o, sprev, d_cat_b, rb, wg, norm_w):
    s = qb.shape[0]
    tb = min(256, s)
    ch = tb // GLA_CHUNK
    nb = s // tb

    def body(qb_ref, kb_ref, vb_ref, la_ref, oms_ref, gb_ref, o_ref, sp_ref, dc_ref, rb_ref, wg_ref, nw_ref,
             dq_ref, dk_ref, dv_ref, dg_ref, dr_ref, gwg_ref, gbg_ref, gnw_ref, dst_ref):
        @pl.when(pl.program_id(0) == 0)
        def _():
            dst_ref[...] = jnp.zeros_like(dst_ref)
            gwg_ref[...] = jnp.zeros_like(gwg_ref)
            gbg_ref[...] = jnp.zeros_like(gbg_ref)
            gnw_ref[...] = jnp.zeros_like(gnw_ref)

        causal = _tri(True)
        tri = causal.astype(F32)
        tri_u = _tri(False).astype(F32)
        nw = nw_ref[...]
        wg16 = wg_ref[...]
        for ci in reversed(range(ch)):
            rows = slice(GLA_CHUNK * ci, GLA_CHUNK * (ci + 1))
            b, bl = _gla_decays(la_ref[rows, :], tri)
            eb, enb, ee, dec = jnp.exp(b), jnp.exp(-b), jnp.exp(bl - b), jnp.exp(bl)
            k = kb_ref[rows, :]
            qd = (qb_ref[rows, :] * GLA_SCALE) * eb
            ki = k * enb
            ke = k * ee
            qd16, ki16, ke16 = qd.astype(MXU_DTYPE), ki.astype(MXU_DTYPE), ke.astype(MXU_DTYPE)
            st = sp_ref[ci]
            st16 = st.astype(MXU_DTYPE)
            dst = dst_ref[...]
            dst16 = dst.astype(MXU_DTYPE)
            v16 = vb_ref[rows, :].astype(MXU_DTYPE)
            g = gb_ref[rows, :]
            sg = _sigmoid(g)
            silu = g * sg
            dsilu = sg * (1.0 + g * (1.0 - sg))
            oc = o_ref[rows, :]
            dc = dc_ref[rows, :]
            gnw = jnp.zeros((1, GLA_DV), F32)
            dg, dv, dqd, dki, dke, dstp, ddec = [], [], [], [], [], [], []
            for h in range(GLA_HEADS):
                lk = slice(GLA_DK * h, GLA_DK * (h + 1))
                lv = slice(GLA_DV * h, GLA_DV * (h + 1))
                oh = oc[:, lv]
                r = lax.rsqrt(jnp.mean(oh * oh, axis=1, keepdims=True) + EPS)
                d_on = dc[:, lv] * silu[:, lv]
                dg.append(dc[:, lv] * (oh * r * nw) * dsilu[:, lv])
                gnw += jnp.sum(d_on * oh * r, axis=0, keepdims=True)
                u = d_on * nw
                do = r * u - oh * (r * r * r) * jnp.mean(u * oh, axis=1, keepdims=True)
                do16 = do.astype(MXU_DTYPE)
                a16 = jnp.where(causal, _mm_nt(qd16[:, lk], ki16[:, lk]), 0.0).astype(MXU_DTYPE)
                da16 = jnp.where(causal, _mm_nt(do16, v16[:, lv]), 0.0).astype(MXU_DTYPE)
                dv.append(_mm_tn(a16, do16) + _mm_nt(ke16[:, lk], dst16[:, lk]))
                dqd.append(_mm(da16, ki16[:, lk]) + _mm(do16, st16[:, lk]))
                dki.append(_mm_tn(da16, qd16[:, lk]))
                dke.append(_mm(v16[:, lv], dst16[:, lk]))
                dstp.append(_mm_tn(do16, qd16[:, lk]))
                ddec.append(jnp.sum(dst[:, lk] * st[:, lk], axis=0, keepdims=True))
            gnw_ref[...] += gnw
            dqd, dki, dke = (jnp.concatenate(t, axis=1) for t in (dqd, dki, dke))
            dst_ref[...] = jnp.concatenate(dstp, axis=1) + dst * dec
            dq_ref[rows, :] = dqd * eb * GLA_SCALE
            dk_ref[rows, :] = dki * enb + dke * ee
            dv_ref[rows, :] = jnp.concatenate(dv, axis=1)
            dg_ref[rows, :] = jnp.concatenate(dg, axis=1)
            dke_ke = dke * ke
            db = dqd * qd - dki * ki - dke_ke
            dbl = jnp.sum(dke_ke, axis=0, keepdims=True) + jnp.concatenate(ddec, axis=1) * dec
            dla = jnp.dot(tri_u, db, precision=HIGHEST, preferred_element_type=F32) + dbl
            dlogit = dla * oms_ref[rows, :] * (1.0 / GLA_TAU)
            dl16 = dlogit.astype(MXU_DTYPE)
            gbg_ref[...] += jnp.sum(dlogit, axis=0, keepdims=True)
            gwg_ref[...] += _mm_tn(rb_ref[rows, :].astype(MXU_DTYPE), dl16)
            dr_ref[rows, :] = _mm_nt(dl16, wg16)

    def rev(width):
        return pl.BlockSpec((tb, width), lambda i: (nb - 1 - i, 0))

    return pl.pallas_call(
        body, name="gla_bwd", grid=(nb,),
        in_specs=[rev(256), rev(256), rev(512), rev(256), rev(256), rev(512), rev(512),
                  pl.BlockSpec((ch, GLA_DV, 256), lambda i: (nb - 1 - i, 0, 0)), rev(512), rev(GLA_RANK),
                  _full((GLA_RANK, 256)), _full((1, 128))],
        out_specs=[rev(256), rev(256), rev(512), rev(512), rev(GLA_RANK),
                   _full((GLA_RANK, 256)), _full((1, 256)), _full((1, 128))],
        out_shape=[jax.ShapeDtypeStruct((s, 256), F32), jax.ShapeDtypeStruct((s, 256), F32),
                   jax.ShapeDtypeStruct((s, 512), F32), jax.ShapeDtypeStruct((s, 512), F32),
                   jax.ShapeDtypeStruct((s, GLA_RANK), F32), jax.ShapeDtypeStruct((GLA_RANK, 256), F32),
                   jax.ShapeDtypeStruct((1, 256), F32), jax.ShapeDtypeStruct((1, 128), F32)],
        scratch_shapes=[pltpu.VMEM((GLA_DV, 256), F32)],
        compiler_params=_cparams(dimension_semantics=("arbitrary",)),
    )(qb, kb, vb, la, oms, gb, o, sprev, d_cat_b, rb, wg, norm_w)


def _in_proj_bwd_x(gx0, pieces, w_in, rope):
    s = gx0.shape[0]
    ts = min(256, s)
    widths = [OFF[i + 1] - OFF[i] for i in range(9)]

    def body(gx0_ref, *refs):
        piece_refs = refs[:9]
        w_ref, c_ref, s1_ref, s2_ref, gx_ref, dp_ref = refs[9:]
        kv_rows = pl.ds(pl.multiple_of(BLOCK + pl.program_id(0) * ts, BLOCK), ts)
        acc = gx0_ref[...]
        for i in range(9):
            t = piece_refs[i][kv_rows, :] if i in (1, 2) else piece_refs[i][...]
            if i == 1:
                t = _rope_t(t, c_ref[...], s1_ref[...], s2_ref[...])
            t16 = t.astype(MXU_DTYPE)
            dp_ref[:, OFF[i]:OFF[i + 1]] = t16
            acc += _mm_nt(t16, w_ref[:, OFF[i]:OFF[i + 1]])
        gx_ref[...] = acc

    return pl.pallas_call(
        body, name="in_proj_bwd_x", grid=(s // ts,),
        in_specs=[_rows(ts, D_MODEL)]
        + [_full((s + BLOCK, w)) if i in (1, 2) else _rows(ts, w) for i, w in enumerate(widths)]
        + [_full((D_MODEL, D_IN_PROJ))] + [_rows(ts, 128)] * 3,
        out_specs=[_rows(ts, D_MODEL), _rows(ts, D_IN_PROJ)],
        out_shape=[jax.ShapeDtypeStruct((s, D_MODEL), F32), jax.ShapeDtypeStruct((s, D_IN_PROJ), MXU_DTYPE)],
        compiler_params=_cparams(dimension_semantics=("arbitrary",)),
    )(gx0, *pieces, w_in, *rope)


def _in_proj_bwd_w(x, dproj):
    s = x.shape[0]
    ts = min(512, s)
    nsteps = s // ts
    col_chunks = [(OFF[i], OFF[i + 1]) for i in range(9)]

    def body(x_ref, dp_ref, gw_ref, acc_ref, stage_ref, sems):
        i = pl.program_id(0)

        @pl.when(i == 0)
        def _():
            acc_ref[...] = jnp.zeros_like(acc_ref)

        xb = x_ref[...].astype(MXU_DTYPE)
        for lo, hi in col_chunks:
            acc_ref[:, lo:hi] += _mm_tn(xb, dp_ref[:, lo:hi])

        @pl.when(i == nsteps - 1)
        def _():
            copies = []
            for j in range(N_DEV):
                slot = j % 2
                if j >= 2:
                    copies[j - 2].wait()
                stage_ref[slot] = acc_ref[:, D_IN_SHARD * j:D_IN_SHARD * (j + 1)]
                cp = pltpu.make_async_copy(stage_ref.at[slot], gw_ref.at[j], sems.at[slot])
                cp.start()
                copies.append(cp)
            copies[N_DEV - 2].wait()
            copies[N_DEV - 1].wait()

    return pl.pallas_call(
        body, name="in_proj_bwd_w", grid=(nsteps,),
        in_specs=[_rows(ts, D_MODEL), _rows(ts, D_IN_PROJ)],
        out_specs=pl.BlockSpec(memory_space=pl.ANY),
        out_shape=jax.ShapeDtypeStruct((N_DEV, D_MODEL, D_IN_SHARD), F32),
        scratch_shapes=[pltpu.VMEM((D_MODEL, D_IN_PROJ), F32), pltpu.VMEM((2, D_MODEL, D_IN_SHARD), F32),
                        pltpu.SemaphoreType.DMA((2,))],
        compiler_params=_cparams(dimension_semantics=("arbitrary",)),
    )(x, dproj)


def _local_step(x, positions, w_in, wg, b_gate, sinks, norm_w, w_out, ln_g, ln_b, target):
    rope = _rope_tables(positions)
    qa, k_pad, v_pad, ga, qb, kb, vb, gb, rb, la, oms = _in_proj(x, w_in, wg, b_gate, rope)
    attn, cat_a = _swa_fwd(sinks, qa, k_pad, v_pad, ga)
    o, cat_b, sprev = _gla_fwd(qb, kb, vb, la, gb, norm_w)
    loss, gx0, d_cat_a, d_cat_b, g_w_out, g_ln = _out_ln_loss(cat_a, cat_b, w_out, x, target, ln_g, ln_b)
    dqa, dga, dk_pad, dv_pad, g_sinks = _swa_bwd(sinks, qa, k_pad, v_pad, attn, ga, d_cat_a, rope)
    dqb, dkb, dvb, dgb, drb, g_wg, g_bg, g_nw = _gla_bwd(qb, kb, vb, la, oms, gb, o, sprev, d_cat_b, rb, wg, norm_w)
    pieces = (dqa, dk_pad, dv_pad, dga, dqb, dkb, dvb, dgb, drb)
    grad_x, dproj = _in_proj_bwd_x(gx0, pieces, w_in, rope)
    g_w_in = _in_proj_bwd_w(x, dproj)
    return loss, grad_x, g_w_in, g_wg, g_bg, g_sinks, g_nw, g_w_out, g_ln


def _mesh_pos():
    return lax.axis_index("x"), lax.axis_index("y"), lax.axis_index("c")


def _peer(k, x, y, c):
    px = (1 - x) if k & 4 else x
    py = (1 - y) if k & 2 else y
    pc = (1 - c) if k & 1 else c
    return (px, py, pc), 4 * px + 2 * py + pc


def _other_chips(x, y):
    return [(1 - x, y), (x, 1 - y), (1 - x, 1 - y)]


def _all_gather_weights(w_in_s, w_out_s, wg_s):
    n_arr = 3

    def body(win_ref, wout_ref, wg_ref, win_full, wout_full, wg_full, win_all, wout_all, wg_all, send_sems, recv_sems):
        x, y, c = _mesh_pos()
        me = 4 * x + 2 * y + c
        sibling = (x, y, 1 - c)
        chips = _other_chips(x, y)
        alls = (win_all, wout_all, wg_all)

        def copy(k, a, block, to):
            return pltpu.make_async_remote_copy(
                src_ref=alls[a].at[block], dst_ref=alls[a].at[block], send_sem=send_sems.at[n_arr * k + a],
                recv_sem=recv_sems.at[n_arr * k + a], device_id=to, device_id_type=pl.DeviceIdType.MESH)

        win_all[me] = win_ref[...].astype(win_all.dtype)
        wout_all[me] = wout_ref[...].astype(wout_all.dtype)
        wg_all[me] = wg_ref[...].astype(wg_all.dtype)
        sends = []
        for a in range(n_arr):
            for j, (cx, cy) in enumerate(chips):
                sends.append(copy(1 + j, a, me, (cx, cy, c)))
            sends.append(copy(0, a, me, sibling))
        for cp in sends:
            cp.start()
        for j, (cx, cy) in enumerate(chips):
            for a in range(n_arr):
                block = 4 * cx + 2 * cy + c
                copy(1 + j, a, block, sibling).wait_recv()
                fwd = copy(4 + j, a, block, sibling)
                fwd.start()
                sends.append(fwd)
        for a in range(n_arr):
            copy(0, a, 4 * x + 2 * y + (1 - c), sibling).wait_recv()
        for j, (cx, cy) in enumerate(chips):
            for a in range(n_arr):
                copy(4 + j, a, 4 * cx + 2 * cy + (1 - c), sibling).wait_recv()
        for cp in sends:
            cp.wait_send()
        for j in range(N_DEV):
            win_full[:, D_IN_SHARD * j:D_IN_SHARD * (j + 1)] = win_all[j]
            wout_full[D_OUT_SHARD * j:D_OUT_SHARD * (j + 1), :] = wout_all[j]
            wg_full[:, 32 * j:32 * (j + 1)] = wg_all[j]

    vmem = pl.BlockSpec(memory_space=pltpu.VMEM)
    n_copies = n_arr * (N_DEV - 1)
    return pl.pallas_call(
        body, name="all_gather_weights",
        in_specs=[vmem] * 3, out_specs=[vmem] * 3,
        out_shape=[jax.ShapeDtypeStruct((D_MODEL, D_IN_PROJ), MXU_DTYPE),
                   jax.ShapeDtypeStruct((D_MODEL, D_MODEL), MXU_DTYPE),
                   jax.ShapeDtypeStruct((GLA_RANK, 256), MXU_DTYPE)],
        scratch_shapes=[pltpu.VMEM((N_DEV, D_MODEL, D_IN_SHARD), MXU_DTYPE),
                        pltpu.VMEM((N_DEV, D_OUT_SHARD, D_MODEL), MXU_DTYPE),
                        pltpu.VMEM((N_DEV, GLA_RANK, 32), MXU_DTYPE),
                        pltpu.SemaphoreType.DMA((n_copies,)), pltpu.SemaphoreType.DMA((n_copies,))],
        compiler_params=_cparams(),
    )(w_in_s, w_out_s, wg_s)


def _reduce_grads(parts_w_in, parts_w_out, parts_wg, small):
    def body(pin_ref, pout_ref, pwg_ref, sm_ref, gin_ref, gout_ref, rwg_ref, rsm_ref,
             own_in, sib_in, snd_in, rcv_in, own_out, sib_out, snd_out, rcv_out,
             loc_sems, d2d_send, d2d_recv, ici_send, ici_recv, sm_send, sm_recv, sm_loc):
        x, y, c = _mesh_pos()
        me = 4 * x + 2 * y + c
        sibling = (x, y, 1 - c)
        chips = [(x, y)] + _other_chips(x, y)
        parts = (pin_ref, pout_ref)
        own, sib, snd, rcv, outs = (own_in, own_out), (sib_in, sib_out), (snd_in, snd_out), (rcv_in, rcv_out), (gin_ref, gout_ref)

        small_dsts = (rwg_ref, rsm_ref)

        def small_src(a, block):
            return pwg_ref.at[block] if a == 0 else sm_ref

        small_local = [pltpu.make_async_copy(small_src(a, me), small_dsts[a].at[me], sm_loc.at[a]) for a in range(2)]
        for cp in small_local:
            cp.start()
        small_sends = []
        for k in range(1, N_DEV):
            peer, pidx = _peer(k, x, y, c)
            for a in range(2):
                i = 2 * (k - 1) + a
                cp = pltpu.make_async_remote_copy(
                    src_ref=small_src(a, pidx), dst_ref=small_dsts[a].at[me], send_sem=sm_send.at[i],
                    recv_sem=sm_recv.at[i], device_id=peer, device_id_type=pl.DeviceIdType.MESH)
                cp.start()
                small_sends.append(cp)

        local, d2d = {}, {}
        for r in (1, 2, 3, 0):
            cx, cy = chips[r]
            for a in range(2):
                i = 2 * r + a
                local[r, a] = pltpu.make_async_copy(parts[a].at[4 * cx + 2 * cy + c], own[a].at[r], loc_sems.at[i])
                local[r, a].start()
                d2d[r, a] = pltpu.make_async_remote_copy(
                    src_ref=parts[a].at[4 * cx + 2 * cy + (1 - c)], dst_ref=sib[a].at[r], send_sem=d2d_send.at[i],
                    recv_sem=d2d_recv.at[i], device_id=sibling, device_id_type=pl.DeviceIdType.MESH)
                d2d[r, a].start()
        ici = {}
        for r in (1, 2, 3):
            cx, cy = chips[r]
            for a in range(2):
                i = 2 * (r - 1) + a
                local[r, a].wait()
                d2d[r, a].wait_recv()
                snd[a][r - 1] = (own[a][r] + sib[a][r]).astype(snd[a].dtype)
                ici[r, a] = pltpu.make_async_remote_copy(
                    src_ref=snd[a].at[r - 1], dst_ref=rcv[a].at[r - 1], send_sem=ici_send.at[i],
                    recv_sem=ici_recv.at[i], device_id=(cx, cy, c), device_id_type=pl.DeviceIdType.MESH)
                ici[r, a].start()
        for a in range(2):
            local[0, a].wait()
            d2d[0, a].wait_recv()
            acc = own[a][0] + sib[a][0]
            for r in (1, 2, 3):
                ici[r, a].wait_recv()
                acc = acc + rcv[a][r - 1].astype(F32)
            outs[a][...] = acc

        for k in range(1, N_DEV):
            peer, pidx = _peer(k, x, y, c)
            for a in range(2):
                i = 2 * (k - 1) + a
                pltpu.make_async_remote_copy(
                    src_ref=small_src(a, me), dst_ref=small_dsts[a].at[pidx], send_sem=sm_send.at[i],
                    recv_sem=sm_recv.at[i], device_id=peer, device_id_type=pl.DeviceIdType.MESH).wait_recv()
        for cp in small_sends + list(d2d.values()) + list(ici.values()):
            cp.wait_send()
        for cp in small_local:
            cp.wait()

    hbm = pl.BlockSpec(memory_space=pl.ANY)
    vmem = pl.BlockSpec(memory_space=pltpu.VMEM)
    in_blk, out_blk = parts_w_in.shape[1:], parts_w_out.shape[1:]
    return pl.pallas_call(
        body, name="reduce_grads",
        in_specs=[hbm] * 4, out_specs=[vmem, vmem, hbm, hbm],
        out_shape=[jax.ShapeDtypeStruct(in_blk, F32), jax.ShapeDtypeStruct(out_blk, F32),
                   jax.ShapeDtypeStruct((N_DEV,) + parts_wg.shape[1:], F32),
                   jax.ShapeDtypeStruct((N_DEV,) + small.shape, F32)],
        scratch_shapes=[pltpu.VMEM((4,) + in_blk, F32), pltpu.VMEM((4,) + in_blk, F32),
                        pltpu.VMEM((3,) + in_blk, MXU_DTYPE), pltpu.VMEM((3,) + in_blk, MXU_DTYPE),
                        pltpu.VMEM((4,) + out_blk, F32), pltpu.VMEM((4,) + out_blk, F32),
                        pltpu.VMEM((3,) + out_blk, MXU_DTYPE), pltpu.VMEM((3,) + out_blk, MXU_DTYPE),
                        pltpu.SemaphoreType.DMA((8,)), pltpu.SemaphoreType.DMA((8,)), pltpu.SemaphoreType.DMA((8,)),
                        pltpu.SemaphoreType.DMA((6,)), pltpu.SemaphoreType.DMA((6,)),
                        pltpu.SemaphoreType.DMA((2 * (N_DEV - 1),)), pltpu.SemaphoreType.DMA((2 * (N_DEV - 1),)),
                        pltpu.SemaphoreType.DMA((2,))],
        compiler_params=_cparams(),
    )(parts_w_in, parts_w_out, parts_wg, small)


def _adamw(recv, w, m, v, name):
    rows, width = w.shape
    tr = 128 if rows % 128 == 0 else rows
    n_parts = recv.shape[0]

    def body(r_ref, w_ref, m_ref, v_ref, g_ref, d_ref, nm_ref, nv_ref):
        g = r_ref[0]
        for j in range(1, n_parts):
            g = g + r_ref[j]
        nm = ADAM_B1 * m_ref[...] + (1.0 - ADAM_B1) * g
        nv = ADAM_B2 * v_ref[...] + (1.0 - ADAM_B2) * (g * g)
        m_hat = nm / (1.0 - ADAM_B1 ** ADAM_STEP)
        v_hat = nv / (1.0 - ADAM_B2 ** ADAM_STEP)
        g_ref[...] = g
        d_ref[...] = -ADAM_LR * (m_hat / (jnp.sqrt(v_hat) + ADAM_EPS) + ADAM_WD * w_ref[...])
        nm_ref[...] = nm
        nv_ref[...] = nv

    spec = _rows(tr, width)
    return pl.pallas_call(
        body, name=name, grid=(rows // tr,),
        in_specs=[pl.BlockSpec((n_parts, tr, width), lambda i: (0, i, 0)), spec, spec, spec],
        out_specs=[spec] * 4,
        out_shape=[jax.ShapeDtypeStruct((rows, width), F32)] * 4,
        compiler_params=_cparams(dimension_semantics=("arbitrary",)),
    )(recv, w, m, v)


def _pack_small(ln_g, ln_b, b_gate, norm_w, sinks, loss=None):
    def rows8(t):
        t = t.reshape(-1)
        t = jnp.pad(t, (0, 1024 - t.shape[0]))
        return t.reshape(8, 128)

    loss = jnp.zeros((1,), F32) if loss is None else loss
    return jnp.concatenate([rows8(t) for t in (ln_g, ln_b, b_gate, norm_w, sinks, loss)], axis=0)


def _unpack_small(p):
    flat = [p[8 * i:8 * (i + 1)].reshape(1, 1024) for i in range(5)]
    return flat[0], flat[1], flat[2][:, :256], flat[3][:, :128], flat[4][:, :8]


def kernel(x, positions, w_in, gla_w_gate_up, gla_b_gate, attn_sinks, gla_norm_w, w_out, ln_g, ln_b, loss_target, m_w_in, m_gla_w_gate_up, m_gla_b_gate, m_attn_sinks, m_gla_norm_w, m_w_out, m_ln_g, m_ln_b, v_w_in, v_gla_w_gate_up, v_gla_b_gate, v_attn_sinks, v_gla_norm_w, v_w_out, v_ln_g, v_ln_b):
    w_in_full, w_out_full, wg_full = _all_gather_weights(w_in[0], w_out[0], gla_w_gate_up[0])

    loss, grad_x, parts_w_in, g_wg, g_bg, g_sinks, g_nw, g_w_out, g_ln = _local_step(
        x[0], positions[0], w_in_full, wg_full, gla_b_gate, attn_sinks[0], gla_norm_w, w_out_full, ln_g, ln_b,
        loss_target[0])

    parts_w_out = g_w_out.reshape(N_DEV, D_OUT_SHARD, D_MODEL)
    parts_wg = jnp.transpose(g_wg.reshape(GLA_RANK, N_DEV, 32), (1, 0, 2))
    small = _pack_small(g_ln[0:1], g_ln[1:2], g_bg, g_nw, g_sinks[:, 0].reshape(1, SWA_Q_HEADS), loss[0, 0:1])
    g_in, g_out, r_wg, r_small = _reduce_grads(parts_w_in, parts_w_out, parts_wg, small)

    upd_in = _adamw(g_in[None], w_in[0], m_w_in[0], v_w_in[0], "adamw_w_in")
    upd_out = _adamw(g_out[None], w_out[0], m_w_out[0], v_w_out[0], "adamw_w_out")
    upd_wg = _adamw(r_wg, gla_w_gate_up[0], m_gla_w_gate_up[0], v_gla_w_gate_up[0], "adamw_wg")
    upd_small = _adamw(
        r_small,
        _pack_small(ln_g, ln_b, gla_b_gate, gla_norm_w, attn_sinks),
        _pack_small(m_ln_g, m_ln_b, m_gla_b_gate, m_gla_norm_w, m_attn_sinks),
        _pack_small(v_ln_g, v_ln_b, v_gla_b_gate, v_gla_norm_w, v_attn_sinks), "adamw_small")

    total = jnp.sum(r_small[:, 40, 0])
    outs = [total, grad_x[None]]
    for kind in range(4):
        s_ln_g, s_ln_b, s_bg, s_nw, s_sinks = _unpack_small(upd_small[kind])
        outs += [upd_in[kind][None], upd_wg[kind][None], s_bg, s_sinks, s_nw, upd_out[kind][None], s_ln_g, s_ln_b]
    return tuple(outs)
```

```python
import functools

import jax
import jax.numpy as jnp
from jax import lax
from jax.experimental import pallas as pl
from jax.experimental.pallas import tpu as pltpu

F32 = jnp.float32
MXU_DTYPE = jnp.bfloat16
HIGHEST = lax.Precision.HIGHEST

N_DEV = 8
D_MODEL = 1024
SWA_Q_HEADS = 8
SWA_KV_HEADS = 2
SWA_GROUP = 4
SWA_HEAD_DIM = 64
BLOCK = 128
ROPE_THETA = 500000.0
ROT_DIM = 16
GLA_HEADS = 4
GLA_DK = 64
GLA_DV = 128
GLA_RANK = 16
GLA_TAU = 16.0
GLA_CHUNK = 64
D_IN_PROJ = 2832
D_IN_SHARD = D_IN_PROJ // N_DEV
D_OUT_SHARD = D_MODEL // N_DEV
OFF = (0, 512, 640, 768, 1280, 1536, 1792, 2304, 2816, 2832)
EPS = 1e-5
ALPHA = 2.0 ** 0.25
SWA_SCALE = SWA_HEAD_DIM ** -0.5
GLA_SCALE = GLA_DK ** -0.5
ADAM_LR = 0.001
ADAM_B1 = 0.9
ADAM_B2 = 0.999
ADAM_EPS = 1e-08
ADAM_WD = 0.01
ADAM_STEP = 10
VMEM_LIMIT = 56 * 1024 * 1024

_NT = (((1,), (1,)), ((), ()))
_TN = (((0,), (0,)), ((), ()))


def _mm(a, b):
    return jnp.dot(a, b, preferred_element_type=F32)


def _mm_nt(a, b):
    return lax.dot_general(a, b, _NT, preferred_element_type=F32)


def _mm_tn(a, b):
    return lax.dot_general(a, b, _TN, preferred_element_type=F32)


def _sigmoid(t):
    return 1.0 / (1.0 + jnp.exp(-t))


def _cparams(**kw):
    return pltpu.CompilerParams(vmem_limit_bytes=VMEM_LIMIT, **kw)


def _full(shape):
    return pl.BlockSpec(shape, lambda *_: (0,) * len(shape))


def _rows(tile, width):
    return pl.BlockSpec((tile, width), lambda i: (i, 0))


def _rope_tables(positions):
    half = ROT_DIM // 2
    inv_freq = ROPE_THETA ** (-jnp.arange(half, dtype=F32) / half)
    lane = jnp.arange(128, dtype=jnp.int32) % SWA_HEAD_DIM
    ang = positions.astype(F32)[:, None] * jnp.tile(inv_freq, 128 // half)[None, :]
    cos, sin = jnp.cos(ang), jnp.sin(ang)
    c = jnp.where(lane < ROT_DIM, cos, 1.0)
    s1 = jnp.where(lane < half, -sin, 0.0)
    s2 = jnp.where((lane >= half) & (lane < ROT_DIM), sin, 0.0)
    return c, s1, s2


def _rope(t, c, s1, s2):
    return t * c + pltpu.roll(t, 120, 1) * s1 + pltpu.roll(t, 8, 1) * s2


def _rope_t(g, c, s1, s2):
    return g * c + pltpu.roll(g * s1, 8, 1) + pltpu.roll(g * s2, 120, 1)


def _in_proj(x, w_in, wg, b_gate, rope):
    s = x.shape[0]
    ts = min(512, s)
    widths = [OFF[i + 1] - OFF[i] for i in range(9)]

    def body(x_ref, w_ref, wg_ref, bg_ref, c_ref, s1_ref, s2_ref,
             qa_ref, ka_ref, va_ref, ga_ref, qb_ref, kb_ref, vb_ref, gb_ref, rb_ref, la_ref, oms_ref):
        xb = x_ref[...].astype(MXU_DTYPE)
        c, s1, s2 = c_ref[...], s1_ref[...], s2_ref[...]
        i0 = pl.program_id(0)

        @pl.when(i0 == 0)
        def _():
            ka_ref[0:BLOCK, :] = jnp.zeros((BLOCK, 128), ka_ref.dtype)
            va_ref[0:BLOCK, :] = jnp.zeros((BLOCK, 128), va_ref.dtype)

        kv_rows = pl.ds(pl.multiple_of(BLOCK + i0 * ts, BLOCK), ts)

        def cols(i):
            return _mm(xb, w_ref[:, OFF[i]:OFF[i + 1]])

        qa = cols(0)
        for i in range(4):
            qa_ref[:, 128 * i:128 * (i + 1)] = _rope(qa[:, 128 * i:128 * (i + 1)], c, s1, s2).astype(qa_ref.dtype)
        ka_ref[kv_rows, :] = _rope(cols(1), c, s1, s2).astype(ka_ref.dtype)
        va_ref[kv_rows, :] = cols(2).astype(va_ref.dtype)
        ga_ref[...] = cols(3)
        qb_ref[...] = cols(4)
        kb_ref[...] = cols(5)
        vb_ref[...] = cols(6).astype(vb_ref.dtype)
        gb_ref[...] = cols(7)
        rb = cols(8)
        rb_ref[...] = rb
        logit = _mm(rb.astype(MXU_DTYPE), wg_ref[...]) + bg_ref[...]
        e = jnp.exp(-jnp.abs(logit))
        la_ref[...] = (jnp.minimum(logit, 0.0) - jnp.log(1.0 + e)) / GLA_TAU
        oms_ref[...] = jnp.where(logit >= 0.0, e, 1.0) / (1.0 + e)

    out_shape = [jax.ShapeDtypeStruct((s + BLOCK if i in (1, 2) else s, w), MXU_DTYPE if i in (0, 1, 2, 6) else F32)
                 for i, w in enumerate(widths)]
    out_shape += [jax.ShapeDtypeStruct((s, 256), F32)] * 2
    return pl.pallas_call(
        body, name="in_proj", grid=(s // ts,),
        in_specs=[_rows(ts, D_MODEL), _full((D_MODEL, D_IN_PROJ)), _full((GLA_RANK, 256)), _full((1, 256)),
                  _rows(ts, 128), _rows(ts, 128), _rows(ts, 128)],
        out_specs=[_full((s + BLOCK, w)) if i in (1, 2) else _rows(ts, w) for i, w in enumerate(widths)]
        + [_rows(ts, 256)] * 2,
        out_shape=out_shape,
        compiler_params=_cparams(dimension_semantics=("arbitrary",)),
    )(x, w_in, wg, b_gate, *rope)


SWA_ROWS = SWA_GROUP * BLOCK


def _swa_bias():
    qi = lax.broadcasted_iota(jnp.int32, (2, SWA_ROWS, 2 * BLOCK), 1) & (BLOCK - 1)
    ki = lax.broadcasted_iota(jnp.int32, (2, SWA_ROWS, 2 * BLOCK), 2)
    first = lax.broadcasted_iota(jnp.int32, (2, SWA_ROWS, 2 * BLOCK), 0) == 0
    dist = qi + BLOCK - ki
    ok = (dist >= 0) & (dist < BLOCK) & (jnp.logical_not(first) | (ki >= BLOCK))
    return jnp.where(ok, 0.0, -jnp.inf).astype(F32)


SWA_SUB = 2


def _swa_bias_of(bias_ref, n, b):
    return bias_ref[jnp.minimum(n, 1)] if b == 0 else bias_ref[1]


def _swa_dup(t, j):
    t = t.astype(F32)
    low = lax.broadcasted_iota(jnp.int32, t.shape, 1) < SWA_HEAD_DIM
    keep = low if j == 0 else jnp.logical_not(low)
    return jnp.where(keep, t, pltpu.roll(t, SWA_HEAD_DIM, 1)).astype(MXU_DTYPE)


def _swa_stack(t, j):
    low = lax.broadcasted_iota(jnp.int32, (BLOCK, 128), 1) < SWA_HEAD_DIM
    zero = jnp.zeros((BLOCK, 128), t.dtype)
    blocks = []
    for p in (2 * j, 2 * j + 1):
        tp = t[:, 128 * p:128 * (p + 1)]
        blocks += [jnp.where(low, tp, zero), jnp.where(low, zero, tp)]
    return jnp.concatenate(blocks, axis=0)


def _swa_unstack(t):
    low = lax.broadcasted_iota(jnp.int32, (BLOCK, 128), 1) < SWA_HEAD_DIM
    return [jnp.where(low, t[2 * BLOCK * i:2 * BLOCK * i + BLOCK], t[2 * BLOCK * i + BLOCK:2 * BLOCK * (i + 1)])
            for i in range(2)]


def _swa_sink_col(sink_ref, j):
    row = lax.broadcasted_iota(jnp.int32, (SWA_ROWS, 1), 0)
    col = jnp.full((SWA_ROWS, 1), sink_ref[SWA_GROUP * j], F32)
    for r in range(1, SWA_GROUP):
        col = jnp.where(row >= BLOCK * r, sink_ref[SWA_GROUP * j + r], col)
    return col


def _swa_probs(qs, kd, bias, sink):
    sc = _mm_nt(qs, kd) + bias
    m = jnp.maximum(jnp.max(sc, axis=1, keepdims=True), sink)
    p = jnp.exp(sc - m)
    ps = jnp.exp(sink - m)
    rinv = 1.0 / (jnp.sum(p, axis=1, keepdims=True) + ps)
    return p * rinv, ps * rinv


def _swa_fwd(sinks, qa, k_pad, v_pad, ga):
    s = qa.shape[0]
    tq = SWA_SUB * BLOCK

    def body(sink_ref, qa_ref, ga_ref, bias_ref, k_ref, v_ref, attn_ref, cat_ref):
        n = pl.program_id(0)
        for b in range(SWA_SUB):
            rows = slice(BLOCK * b, BLOCK * (b + 1))
            start = pl.multiple_of((n * SWA_SUB + b) * BLOCK, BLOCK)
            kw = k_ref[pl.ds(start, 2 * BLOCK), :]
            vw = v_ref[pl.ds(start, 2 * BLOCK), :]
            bias = _swa_bias_of(bias_ref, n, b)
            q = qa_ref[rows, :] * SWA_SCALE
            g = ga_ref[rows, :]
            silu = g * _sigmoid(g)
            for j in range(SWA_KV_HEADS):
                qs = _swa_stack(q, j).astype(MXU_DTYPE)
                probs, _ = _swa_probs(qs, _swa_dup(kw, j), bias, _swa_sink_col(sink_ref, j))
                pairs = _swa_unstack(_mm(probs.astype(MXU_DTYPE), _swa_dup(vw, j)))
                for i in range(2):
                    lanes = slice(128 * (2 * j + i), 128 * (2 * j + i + 1))
                    attn_ref[rows, lanes] = pairs[i]
                    cat_ref[rows, lanes] = (pairs[i] * silu[:, lanes]).astype(cat_ref.dtype)

    return pl.pallas_call(
        body, name="swa_fwd", grid=(s // tq,),
        in_specs=[pl.BlockSpec(memory_space=pltpu.SMEM), _rows(tq, 512), _rows(tq, 512),
                  _full((2, SWA_ROWS, 2 * BLOCK)), _full((s + BLOCK, 128)), _full((s + BLOCK, 128))],
        out_specs=[_rows(tq, 512), _rows(tq, 512)],
        out_shape=[jax.ShapeDtypeStruct((s, 512), F32), jax.ShapeDtypeStruct((s, 512), MXU_DTYPE)],
        compiler_params=_cparams(dimension_semantics=("arbitrary",)),
    )(sinks, qa, ga, _swa_bias(), k_pad, v_pad)


def _tri(lower):
    r = lax.broadcasted_iota(jnp.int32, (GLA_CHUNK, GLA_CHUNK), 0)
    c = lax.broadcasted_iota(jnp.int32, (GLA_CHUNK, GLA_CHUNK), 1)
    return (r >= c) if lower else (r <= c)


def _gla_decays(la, tri_f32):
    b = jnp.dot(tri_f32, la, precision=HIGHEST, preferred_element_type=F32)
    bl = b[GLA_CHUNK - 1:GLA_CHUNK, :]
    return b, bl


def _gla_fwd(qb, kb, vb, la, gb, norm_w):
    s = qb.shape[0]
    tb = min(256, s)
    ch = tb // GLA_CHUNK

    def body(qb_ref, kb_ref, vb_ref, la_ref, gb_ref, nw_ref, o_ref, cat_ref, sp_ref, st_ref):
        @pl.when(pl.program_id(0) == 0)
        def _():
            st_ref[...] = jnp.zeros_like(st_ref)

        causal = _tri(True)
        tri = causal.astype(F32)
        nw = nw_ref[...]
        for ci in range(ch):
            rows = slice(GLA_CHUNK * ci, GLA_CHUNK * (ci + 1))
            b, bl = _gla_decays(la_ref[rows, :], tri)
            k = kb_ref[rows, :]
            qd = ((qb_ref[rows, :] * GLA_SCALE) * jnp.exp(b)).astype(MXU_DTYPE)
            ki = (k * jnp.exp(-b)).astype(MXU_DTYPE)
            ke = (k * jnp.exp(bl - b)).astype(MXU_DTYPE)
            st = st_ref[...]
            sp_ref[ci] = st
            st16 = st.astype(MXU_DTYPE)
            v = vb_ref[rows, :].astype(MXU_DTYPE)
            g = gb_ref[rows, :]
            silu = g * _sigmoid(g)
            upd, o_all, cat_all = [], [], []
            for h in range(GLA_HEADS):
                lk = slice(GLA_DK * h, GLA_DK * (h + 1))
                lv = slice(GLA_DV * h, GLA_DV * (h + 1))
                a = jnp.where(causal, _mm_nt(qd[:, lk], ki[:, lk]), 0.0)
                oh = _mm(a.astype(MXU_DTYPE), v[:, lv]) + _mm_nt(qd[:, lk], st16[:, lk])
                upd.append(_mm_tn(v[:, lv], ke[:, lk]))
                r = lax.rsqrt(jnp.mean(oh * oh, axis=1, keepdims=True) + EPS)
                o_all.append(oh)
                cat_all.append(oh * r * nw * silu[:, lv])
            st_ref[...] = st * jnp.exp(bl) + jnp.concatenate(upd, axis=1)
            o_ref[rows, :] = jnp.concatenate(o_all, axis=1)
            cat_ref[rows, :] = jnp.concatenate(cat_all, axis=1).astype(cat_ref.dtype)

    return pl.pallas_call(
        body, name="gla_fwd", grid=(s // tb,),
        in_specs=[_rows(tb, 256), _rows(tb, 256), _rows(tb, 512), _rows(tb, 256), _rows(tb, 512), _full((1, 128))],
        out_specs=[_rows(tb, 512), _rows(tb, 512), pl.BlockSpec((ch, GLA_DV, 256), lambda i: (i, 0, 0))],
        out_shape=[jax.ShapeDtypeStruct((s, 512), F32), jax.ShapeDtypeStruct((s, 512), MXU_DTYPE),
                   jax.ShapeDtypeStruct((s // GLA_CHUNK, GLA_DV, 256), F32)],
        scratch_shapes=[pltpu.VMEM((GLA_DV, 256), F32)],
        compiler_params=_cparams(dimension_semantics=("arbitrary",)),
    )(qb, kb, vb, la, gb, norm_w)


def _out_ln_loss(cat_a, cat_b, w_out, x, target, ln_g, ln_b):
    s = x.shape[0]
    ts = min(512, s)
    halves = 2 if ts % 32 == 0 else 1
    th = ts // halves

    def body(ca_ref, cb_ref, w_ref, x_ref, t_ref, g_ref, b_ref,
             loss_ref, gx_ref, da_ref, db_ref, gw_ref, gln_ref):
        @pl.when(pl.program_id(0) == 0)
        def _():
            loss_ref[...] = jnp.zeros_like(loss_ref)
            gw_ref[...] = jnp.zeros_like(gw_ref)
            gln_ref[...] = jnp.zeros_like(gln_ref)

        g = g_ref[...]
        dh16s = []
        for k in range(halves):
            rows = slice(th * k, th * (k + 1))
            mix = _mm(ca_ref[rows, :], w_ref[0:512, :]) + _mm(cb_ref[rows, :], w_ref[512:1024, :])
            h = ALPHA * x_ref[rows, :] + mix
            mu = jnp.mean(h, axis=1, keepdims=True)
            hc = h - mu
            rstd = lax.rsqrt(jnp.mean(hc * hc, axis=1, keepdims=True) + EPS)
            xhat = hc * rstd
            err = xhat * g + b_ref[...] - t_ref[rows, :]
            loss_ref[...] += 0.5 * jnp.sum(jnp.mean(err * err, axis=1, keepdims=True))
            dy = err * (1.0 / D_MODEL)
            gln_ref[0:1, :] += jnp.sum(dy * xhat, axis=0, keepdims=True)
            gln_ref[1:2, :] += jnp.sum(dy, axis=0, keepdims=True)
            dxh = dy * g
            dh = rstd * (dxh - jnp.mean(dxh, axis=1, keepdims=True)
                         - xhat * jnp.mean(dxh * xhat, axis=1, keepdims=True))
            gx_ref[rows, :] = ALPHA * dh
            dh16s.append(dh.astype(MXU_DTYPE))
        for k in range(halves):
            rows = slice(th * k, th * (k + 1))
            da_ref[rows, :] = _mm_nt(dh16s[k], w_ref[0:512, :])
            db_ref[rows, :] = _mm_nt(dh16s[k], w_ref[512:1024, :])
            gw_ref[0:512, :] += _mm_tn(ca_ref[rows, :], dh16s[k])
            gw_ref[512:1024, :] += _mm_tn(cb_ref[rows, :], dh16s[k])

    return pl.pallas_call(
        body, name="out_ln_loss", grid=(s // ts,),
        in_specs=[_rows(ts, 512), _rows(ts, 512), _full((D_MODEL, D_MODEL)), _rows(ts, D_MODEL), _rows(ts, D_MODEL),
                  _full((1, D_MODEL)), _full((1, D_MODEL))],
        out_specs=[_full((1, 128)), _rows(ts, D_MODEL), _rows(ts, 512), _rows(ts, 512),
                   _full((D_MODEL, D_MODEL)), _full((2, D_MODEL))],
        out_shape=[jax.ShapeDtypeStruct((1, 128), F32), jax.ShapeDtypeStruct((s, D_MODEL), F32),
                   jax.ShapeDtypeStruct((s, 512), F32), jax.ShapeDtypeStruct((s, 512), F32),
                   jax.ShapeDtypeStruct((D_MODEL, D_MODEL), F32), jax.ShapeDtypeStruct((2, D_MODEL), F32)],
        compiler_params=_cparams(dimension_semantics=("arbitrary",)),
    )(cat_a, cat_b, w_out, x, target, ln_g, ln_b)


def _swa_bwd(sinks, qa, k_pad, v_pad, attn, ga, d_cat_a, rope):
    s = qa.shape[0]
    tq = SWA_SUB * BLOCK

    def body(sink_ref, qa_ref, ga_ref, at_ref, dc_ref, c_ref, s1_ref, s2_ref, bias_ref, k_ref, v_ref,
             dq_ref, dg_ref, dk_ref, dv_ref, ds_ref):
        n = pl.program_id(0)

        @pl.when(n == 0)
        def _():
            dk_ref[...] = jnp.zeros_like(dk_ref)
            dv_ref[...] = jnp.zeros_like(dv_ref)
            ds_ref[...] = jnp.zeros_like(ds_ref)

        low = lax.broadcasted_iota(jnp.int32, (2 * BLOCK, 128), 1) < SWA_HEAD_DIM
        for b in range(SWA_SUB):
            rows = slice(BLOCK * b, BLOCK * (b + 1))
            start = pl.multiple_of((n * SWA_SUB + b) * BLOCK, BLOCK)
            kw = k_ref[pl.ds(start, 2 * BLOCK), :]
            vw = v_ref[pl.ds(start, 2 * BLOCK), :]
            bias = _swa_bias_of(bias_ref, n, b)
            q = qa_ref[rows, :] * SWA_SCALE
            g = ga_ref[rows, :]
            sg = _sigmoid(g)
            o = at_ref[rows, :]
            dc = dc_ref[rows, :]
            do = dc * (g * sg)
            dg_ref[rows, :] = (dc * o * (sg * (1.0 + g * (1.0 - sg)))).astype(dg_ref.dtype)
            od = do * o
            c, s1, s2 = c_ref[rows, :], s1_ref[rows, :], s2_ref[rows, :]
            dk, dv = [], []
            for j in range(SWA_KV_HEADS):
                kd, vd = _swa_dup(kw, j), _swa_dup(vw, j)
                qs = _swa_stack(q, j).astype(MXU_DTYPE)
                dos = _swa_stack(do, j).astype(MXU_DTYPE)
                probs, psink = _swa_probs(qs, kd, bias, _swa_sink_col(sink_ref, j))
                delta = jnp.sum(_swa_stack(od, j), axis=1, keepdims=True)
                dsc = (probs * (_mm_nt(dos, vd) - delta)).astype(MXU_DTYPE)
                dsink = psink * delta
                for r in range(SWA_GROUP):
                    h = SWA_GROUP * j + r
                    ds_ref[h:h + 1, :] += jnp.zeros((1, 128), F32) - jnp.sum(dsink[BLOCK * r:BLOCK * (r + 1)])
                dq = _swa_unstack(_mm(dsc, kd))
                for i in range(2):
                    lanes = slice(128 * (2 * j + i), 128 * (2 * j + i + 1))
                    dq_ref[rows, lanes] = _rope_t(dq[i] * SWA_SCALE, c, s1, s2).astype(dq_ref.dtype)
                dkj = _mm_tn(dsc, qs)
                dvj = _mm_tn(probs.astype(MXU_DTYPE), dos)
                dk.append(dkj + pltpu.roll(dkj, SWA_HEAD_DIM, 1))
                dv.append(dvj + pltpu.roll(dvj, SWA_HEAD_DIM, 1))
            dk_ref[pl.ds(start, 2 * BLOCK), :] += jnp.where(low, dk[0], dk[1])
            dv_ref[pl.ds(start, 2 * BLOCK), :] += jnp.where(low, dv[0], dv[1])

    return pl.pallas_call(
        body, name="swa_bwd", grid=(s // tq,),
        in_specs=[pl.BlockSpec(memory_space=pltpu.SMEM)] + [_rows(tq, 512)] * 4 + [_rows(tq, 128)] * 3
        + [_full((2, SWA_ROWS, 2 * BLOCK))] + [_full((s + BLOCK, 128))] * 2,
        out_specs=[_rows(tq, 512), _rows(tq, 512), _full((s + BLOCK, 128)), _full((s + BLOCK, 128)),
                   _full((SWA_Q_HEADS, 128))],
        out_shape=[jax.ShapeDtypeStruct((s, 512), MXU_DTYPE), jax.ShapeDtypeStruct((s, 512), MXU_DTYPE),
                   jax.ShapeDtypeStruct((s + BLOCK, 128), F32), jax.ShapeDtypeStruct((s + BLOCK, 128), F32),
                   jax.ShapeDtypeStruct((SWA_Q_HEADS, 128), F32)],
        compiler_params=_cparams(dimension_semantics=("arbitrary",)),
    )(sinks, qa, ga, attn, d_cat_a, *rope, _swa_bias(), k_pad, v_pad)


def _gla_bwd(qb, kb, vb, la, oms, gb, o, sprev, d_cat_b, rb, wg, norm_w):
    s = qb.shape[0]
    tb = min(256, s)
    ch = tb // GLA_CHUNK
    nb = s // tb

    def body(qb_ref, kb_ref, vb_ref, la_ref, oms_ref, gb_ref, o_ref, sp_ref, dc_ref, rb_ref, wg_ref, nw_ref,
             dq_ref, dk_ref, dv_ref, dg_ref, dr_ref, gwg_ref, gbg_ref, gnw_ref, dst_ref):
        @pl.when(pl.program_id(0) == 0)
        def _():
            dst_ref[...] = jnp.zeros_like(dst_ref)
            gwg_ref[...] = jnp.zeros_like(gwg_ref)
            gbg_ref[...] = jnp.zeros_like(gbg_ref)
            gnw_ref[...] = jnp.zeros_like(gnw_ref)

        causal = _tri(True)
        tri = causal.astype(F32)
        tri_u = _tri(False).astype(F32)
        nw = nw_ref[...]
        wg16 = wg_ref[...]
        for ci in reversed(range(ch)):
            rows = slice(GLA_CHUNK * ci, GLA_CHUNK * (ci + 1))
            b, bl = _gla_decays(la_ref[rows, :], tri)
            eb, enb, ee, dec = jnp.exp(b), jnp.exp(-b), jnp.exp(bl - b), jnp.exp(bl)
            k = kb_ref[rows, :]
            qd = (qb_ref[rows, :] * GLA_SCALE) * eb
            ki = k * enb
            ke = k * ee
            qd16, ki16, ke16 = qd.astype(MXU_DTYPE), ki.astype(MXU_DTYPE), ke.astype(MXU_DTYPE)
            st = sp_ref[ci]
            st16 = st.astype(MXU_DTYPE)
            dst = dst_ref[...]
            dst16 = dst.astype(MXU_DTYPE)
            v16 = vb_ref[rows, :].astype(MXU_DTYPE)
            g = gb_ref[rows, :]
            sg = _sigmoid(g)
            silu = g * sg
            dsilu = sg * (1.0 + g * (1.0 - sg))
            oc = o_ref[rows, :]
            dc = dc_ref[rows, :]
            gnw = jnp.zeros((1, GLA_DV), F32)
            dg, dv, dqd, dki, dke, dstp, ddec = [], [], [], [], [], [], []
            for h in range(GLA_HEADS):
                lk = slice(GLA_DK * h, GLA_DK * (h + 1))
                lv = slice(GLA_DV * h, GLA_DV * (h + 1))
                oh = oc[:, lv]
                r = lax.rsqrt(jnp.mean(oh * oh, axis=1, keepdims=True) + EPS)
                d_on = dc[:, lv] * silu[:, lv]
                dg.append(dc[:, lv] * (oh * r * nw) * dsilu[:, lv])
                gnw += jnp.sum(d_on * oh * r, axis=0, keepdims=True)
                u = d_on * nw
                do = r * u - oh * (r * r * r) * jnp.mean(u * oh, axis=1, keepdims=True)
                do16 = do.astype(MXU_DTYPE)
                a16 = jnp.where(causal, _mm_nt(qd16[:, lk], ki16[:, lk]), 0.0).astype(MXU_DTYPE)
                da16 = jnp.where(causal, _mm_nt(do16, v16[:, lv]), 0.0).astype(MXU_DTYPE)
                dv.append(_mm_tn(a16, do16) + _mm_nt(ke16[:, lk], dst16[:, lk]))
                dqd.append(_mm(da16, ki16[:, lk]) + _mm(do16, st16[:, lk]))
                dki.append(_mm_tn(da16, qd16[:, lk]))
                dke.append(_mm(v16[:, lv], dst16[:, lk]))
                dstp.append(_mm_tn(do16, qd16[:, lk]))
                ddec.append(jnp.sum(dst[:, lk] * st[:, lk], axis=0, keepdims=True))
            gnw_ref[...] += gnw
            dqd, dki, dke = (jnp.concatenate(t, axis=1) for t in (dqd, dki, dke))
            dst_ref[...] = jnp.concatenate(dstp, axis=1) + dst * dec
            dq_ref[rows, :] = (dqd * eb * GLA_SCALE).astype(dq_ref.dtype)
            dk_ref[rows, :] = (dki * enb + dke * ee).astype(dk_ref.dtype)
            dv_ref[rows, :] = jnp.concatenate(dv, axis=1).astype(dv_ref.dtype)
            dg_ref[rows, :] = jnp.concatenate(dg, axis=1).astype(dg_ref.dtype)
            dke_ke = dke * ke
            db = dqd * qd - dki * ki - dke_ke
            dbl = jnp.sum(dke_ke, axis=0, keepdims=True) + jnp.concatenate(ddec, axis=1) * dec
            dla = jnp.dot(tri_u, db, precision=HIGHEST, preferred_element_type=F32) + dbl
            dlogit = dla * oms_ref[rows, :] * (1.0 / GLA_TAU)
            dl16 = dlogit.astype(MXU_DTYPE)
            gbg_ref[...] += jnp.sum(dlogit, axis=0, keepdims=True)
            gwg_ref[...] += _mm_tn(rb_ref[rows, :].astype(MXU_DTYPE), dl16)
            dr_ref[rows, :] = _mm_nt(dl16, wg16).astype(dr_ref.dtype)

    def rev(width):
        return pl.BlockSpec((tb, width), lambda i: (nb - 1 - i, 0))

    return pl.pallas_call(
        body, name="gla_bwd", grid=(nb,),
        in_specs=[rev(256), rev(256), rev(512), rev(256), rev(256), rev(512), rev(512),
                  pl.BlockSpec((ch, GLA_DV, 256), lambda i: (nb - 1 - i, 0, 0)), rev(512), rev(GLA_RANK),
                  _full((GLA_RANK, 256)), _full((1, 128))],
        out_specs=[rev(256), rev(256), rev(512), rev(512), rev(GLA_RANK),
                   _full((GLA_RANK, 256)), _full((1, 256)), _full((1, 128))],
        out_shape=[jax.ShapeDtypeStruct((s, 256), MXU_DTYPE), jax.ShapeDtypeStruct((s, 256), MXU_DTYPE),
                   jax.ShapeDtypeStruct((s, 512), MXU_DTYPE), jax.ShapeDtypeStruct((s, 512), MXU_DTYPE),
                   jax.ShapeDtypeStruct((s, GLA_RANK), MXU_DTYPE), jax.ShapeDtypeStruct((GLA_RANK, 256), F32),
                   jax.ShapeDtypeStruct((1, 256), F32), jax.ShapeDtypeStruct((1, 128), F32)],
        scratch_shapes=[pltpu.VMEM((GLA_DV, 256), F32)],
        compiler_params=_cparams(dimension_semantics=("arbitrary",)),
    )(qb, kb, vb, la, oms, gb, o, sprev, d_cat_b, rb, wg, norm_w)


def _in_proj_bwd_x(gx0, pieces, w_in, rope):
    s = gx0.shape[0]
    ts = min(512, s)
    widths = [OFF[i + 1] - OFF[i] for i in range(9)]

    def body(gx0_ref, *refs):
        piece_refs = refs[:9]
        w_ref, c_ref, s1_ref, s2_ref, gx_ref, dp_ref = refs[9:]
        kv_rows = pl.ds(pl.multiple_of(BLOCK + pl.program_id(0) * ts, BLOCK), ts)
        acc = gx0_ref[...]
        for i in range(9):
            t = piece_refs[i][kv_rows, :] if i in (1, 2) else piece_refs[i][...]
            if i == 1:
                t = _rope_t(t, c_ref[...], s1_ref[...], s2_ref[...])
            t16 = t.astype(MXU_DTYPE)
            dp_ref[:, OFF[i]:OFF[i + 1]] = t16
            acc += _mm_nt(t16, w_ref[:, OFF[i]:OFF[i + 1]])
        gx_ref[...] = acc

    return pl.pallas_call(
        body, name="in_proj_bwd_x", grid=(s // ts,),
        in_specs=[_rows(ts, D_MODEL)]
        + [_full((s + BLOCK, w)) if i in (1, 2) else _rows(ts, w) for i, w in enumerate(widths)]
        + [_full((D_MODEL, D_IN_PROJ))] + [_rows(ts, 128)] * 3,
        out_specs=[_rows(ts, D_MODEL), _rows(ts, D_IN_PROJ)],
        out_shape=[jax.ShapeDtypeStruct((s, D_MODEL), F32), jax.ShapeDtypeStruct((s, D_IN_PROJ), MXU_DTYPE)],
        compiler_params=_cparams(dimension_semantics=("arbitrary",)),
    )(gx0, *pieces, w_in, *rope)


def _in_proj_bwd_w(x, dproj):
    s = x.shape[0]
    ts = min(512, s)
    nsteps = s // ts
    col_chunks = [(OFF[i], OFF[i + 1]) for i in range(9)]

    def body(x_ref, dp_ref, gw_ref, acc_ref, stage_ref, sems):
        i = pl.program_id(0)

        @pl.when(i == 0)
        def _():
            acc_ref[...] = jnp.zeros_like(acc_ref)

        xb = x_ref[...].astype(MXU_DTYPE)
        for lo, hi in col_chunks:
            acc_ref[:, lo:hi] += _mm_tn(xb, dp_ref[:, lo:hi])

        @pl.when(i == nsteps - 1)
        def _():
            copies = []
            for j in range(N_DEV):
                slot = j % 2
                if j >= 2:
                    copies[j - 2].wait()
                stage_ref[slot] = acc_ref[:, D_IN_SHARD * j:D_IN_SHARD * (j + 1)]
                cp = pltpu.make_async_copy(stage_ref.at[slot], gw_ref.at[j], sems.at[slot])
                cp.start()
                copies.append(cp)
            copies[N_DEV - 2].wait()
            copies[N_DEV - 1].wait()

    return pl.pallas_call(
        body, name="in_proj_bwd_w", grid=(nsteps,),
        in_specs=[_rows(ts, D_MODEL), _rows(ts, D_IN_PROJ)],
        out_specs=pl.BlockSpec(memory_space=pl.ANY),
        out_shape=jax.ShapeDtypeStruct((N_DEV, D_MODEL, D_IN_SHARD), F32),
        scratch_shapes=[pltpu.VMEM((D_MODEL, D_IN_PROJ), F32), pltpu.VMEM((2, D_MODEL, D_IN_SHARD), F32),
                        pltpu.SemaphoreType.DMA((2,))],
        compiler_params=_cparams(dimension_semantics=("arbitrary",)),
    )(x, dproj)


def _local_step(x, positions, w_in, wg, b_gate, sinks, norm_w, w_out, ln_g, ln_b, target):
    rope = _rope_tables(positions)
    qa, k_pad, v_pad, ga, qb, kb, vb, gb, rb, la, oms = _in_proj(x, w_in, wg, b_gate, rope)
    attn, cat_a = _swa_fwd(sinks, qa, k_pad, v_pad, ga)
    o, cat_b, sprev = _gla_fwd(qb, kb, vb, la, gb, norm_w)
    loss, gx0, d_cat_a, d_cat_b, g_w_out, g_ln = _out_ln_loss(cat_a, cat_b, w_out, x, target, ln_g, ln_b)
    dqa, dga, dk_pad, dv_pad, g_sinks = _swa_bwd(sinks, qa, k_pad, v_pad, attn, ga, d_cat_a, rope)
    dqb, dkb, dvb, dgb, drb, g_wg, g_bg, g_nw = _gla_bwd(qb, kb, vb, la, oms, gb, o, sprev, d_cat_b, rb, wg, norm_w)
    pieces = (dqa, dk_pad, dv_pad, dga, dqb, dkb, dvb, dgb, drb)
    grad_x, dproj = _in_proj_bwd_x(gx0, pieces, w_in, rope)
    g_w_in = _in_proj_bwd_w(x, dproj)
    return loss, grad_x, g_w_in, g_wg, g_bg, g_sinks, g_nw, g_w_out, g_ln


def _mesh_pos():
    return lax.axis_index("x"), lax.axis_index("y"), lax.axis_index("c")


def _peer(k, x, y, c):
    px = (1 - x) if k & 4 else x
    py = (1 - y) if k & 2 else y
    pc = (1 - c) if k & 1 else c
    return (px, py, pc), 4 * px + 2 * py + pc


def _other_chips(x, y):
    return [(1 - x, y), (x, 1 - y), (1 - x, 1 - y)]


def _all_gather_weights(w_in_s, w_out_s, wg_s):
    n_arr = 3

    def body(win_ref, wout_ref, wg_ref, win_full, wout_full, wg_full, win_all, wout_all, wg_all, send_sems, recv_sems):
        x, y, c = _mesh_pos()
        me = 4 * x + 2 * y + c
        sibling = (x, y, 1 - c)
        chips = _other_chips(x, y)
        alls = (win_all, wout_all, wg_all)

        def copy(k, a, block, to):
            return pltpu.make_async_remote_copy(
                src_ref=alls[a].at[block], dst_ref=alls[a].at[block], send_sem=send_sems.at[n_arr * k + a],
                recv_sem=recv_sems.at[n_arr * k + a], device_id=to, device_id_type=pl.DeviceIdType.MESH)

        win_all[me] = win_ref[...].astype(win_all.dtype)
        wout_all[me] = wout_ref[...].astype(wout_all.dtype)
        wg_all[me] = wg_ref[...].astype(wg_all.dtype)
        sends = []
        for a in range(n_arr):
            for j, (cx, cy) in enumerate(chips):
                sends.append(copy(1 + j, a, me, (cx, cy, c)))
            sends.append(copy(0, a, me, sibling))
        for cp in sends:
            cp.start()
        for j, (cx, cy) in enumerate(chips):
            for a in range(n_arr):
                block = 4 * cx + 2 * cy + c
                copy(1 + j, a, block, sibling).wait_recv()
                fwd = copy(4 + j, a, block, sibling)
                fwd.start()
                sends.append(fwd)
        for a in range(n_arr):
            copy(0, a, 4 * x + 2 * y + (1 - c), sibling).wait_recv()
        for j, (cx, cy) in enumerate(chips):
            for a in range(n_arr):
                copy(4 + j, a, 4 * cx + 2 * cy + (1 - c), sibling).wait_recv()
        for cp in sends:
            cp.wait_send()
        for j in range(N_DEV):
            win_full[:, D_IN_SHARD * j:D_IN_SHARD * (j + 1)] = win_all[j]
            wout_full[D_OUT_SHARD * j:D_OUT_SHARD * (j + 1), :] = wout_all[j]
            wg_full[:, 32 * j:32 * (j + 1)] = wg_all[j]

    vmem = pl.BlockSpec(memory_space=pltpu.VMEM)
    n_copies = n_arr * (N_DEV - 1)
    return pl.pallas_call(
        body, name="all_gather_weights",
        in_specs=[vmem] * 3, out_specs=[vmem] * 3,
        out_shape=[jax.ShapeDtypeStruct((D_MODEL, D_IN_PROJ), MXU_DTYPE),
                   jax.ShapeDtypeStruct((D_MODEL, D_MODEL), MXU_DTYPE),
                   jax.ShapeDtypeStruct((GLA_RANK, 256), MXU_DTYPE)],
        scratch_shapes=[pltpu.VMEM((N_DEV, D_MODEL, D_IN_SHARD), MXU_DTYPE),
                        pltpu.VMEM((N_DEV, D_OUT_SHARD, D_MODEL), MXU_DTYPE),
                        pltpu.VMEM((N_DEV, GLA_RANK, 32), MXU_DTYPE),
                        pltpu.SemaphoreType.DMA((n_copies,)), pltpu.SemaphoreType.DMA((n_copies,))],
        compiler_params=_cparams(),
    )(w_in_s, w_out_s, wg_s)


def _reduce_grads(parts_w_in, parts_w_out, parts_wg, small):
    def body(pin_ref, pout_ref, pwg_ref, sm_ref, gin_ref, gout_ref, rwg_ref, rsm_ref,
             own_in, sib_in, snd_in, rcv_in, own_out, sib_out, snd_out, rcv_out,
             loc_sems, d2d_send, d2d_recv, ici_send, ici_recv, sm_send, sm_recv, sm_loc):
        x, y, c = _mesh_pos()
        me = 4 * x + 2 * y + c
        sibling = (x, y, 1 - c)
        chips = [(x, y)] + _other_chips(x, y)
        parts = (pin_ref, pout_ref)
        own, sib, snd, rcv, outs = (own_in, own_out), (sib_in, sib_out), (snd_in, snd_out), (rcv_in, rcv_out), (gin_ref, gout_ref)

        small_dsts = (rwg_ref, rsm_ref)

        def small_src(a, block):
            return pwg_ref.at[block] if a == 0 else sm_ref

        small_local = [pltpu.make_async_copy(small_src(a, me), small_dsts[a].at[me], sm_loc.at[a]) for a in range(2)]
        for cp in small_local:
            cp.start()
        small_sends = []
        for k in range(1, N_DEV):
            peer, pidx = _peer(k, x, y, c)
            for a in range(2):
                i = 2 * (k - 1) + a
                cp = pltpu.make_async_remote_copy(
                    src_ref=small_src(a, pidx), dst_ref=small_dsts[a].at[me], send_sem=sm_send.at[i],
                    recv_sem=sm_recv.at[i], device_id=peer, device_id_type=pl.DeviceIdType.MESH)
                cp.start()
                small_sends.append(cp)

        local, d2d = {}, {}
        for r in (1, 2, 3, 0):
            cx, cy = chips[r]
            for a in range(2):
                i = 2 * r + a
                local[r, a] = pltpu.make_async_copy(parts[a].at[4 * cx + 2 * cy + c], own[a].at[r], loc_sems.at[i])
                local[r, a].start()
                d2d[r, a] = pltpu.make_async_remote_copy(
                    src_ref=parts[a].at[4 * cx + 2 * cy + (1 - c)], dst_ref=sib[a].at[r], send_sem=d2d_send.at[i],
                    recv_sem=d2d_recv.at[i], device_id=sibling, device_id_type=pl.DeviceIdType.MESH)
                d2d[r, a].start()
        ici = {}
        for r in (1, 2, 3):
            cx, cy = chips[r]
            for a in range(2):
                i = 2 * (r - 1) + a
                local[r, a].wait()
                d2d[r, a].wait_recv()
                snd[a][r - 1] = (own[a][r] + sib[a][r]).astype(snd[a].dtype)
                ici[r, a] = pltpu.make_async_remote_copy(
                    src_ref=snd[a].at[r - 1], dst_ref=rcv[a].at[r - 1], send_sem=ici_send.at[i],
                    recv_sem=ici_recv.at[i], device_id=(cx, cy, c), device_id_type=pl.DeviceIdType.MESH)
                ici[r, a].start()
        for a in range(2):
            local[0, a].wait()
            d2d[0, a].wait_recv()
            acc = own[a][0] + sib[a][0]
            for r in (1, 2, 3):
                ici[r, a].wait_recv()
                acc = acc + rcv[a][r - 1].astype(F32)
            outs[a][...] = acc

        for k in range(1, N_DEV):
            peer, pidx = _peer(k, x, y, c)
            for a in range(2):
                i = 2 * (k - 1) + a
                pltpu.make_async_remote_copy(
                    src_ref=small_src(a, me), dst_ref=small_dsts[a].at[pidx], send_sem=sm_send.at[i],
                    recv_sem=sm_recv.at[i], device_id=peer, device_id_type=pl.DeviceIdType.MESH).wait_recv()
        for cp in small_sends + list(d2d.values()) + list(ici.values()):
            cp.wait_send()
        for cp in small_local:
            cp.wait()

    hbm = pl.BlockSpec(memory_space=pl.ANY)
    vmem = pl.BlockSpec(memory_space=pltpu.VMEM)
    in_blk, out_blk = parts_w_in.shape[1:], parts_w_out.shape[1:]
    return pl.pallas_call(
        body, name="reduce_grads",
        in_specs=[hbm] * 4, out_specs=[vmem, vmem, hbm, hbm],
        out_shape=[jax.ShapeDtypeStruct(in_blk, F32), jax.ShapeDtypeStruct(out_blk, F32),
                   jax.ShapeDtypeStruct((N_DEV,) + parts_wg.shape[1:], F32),
                   jax.ShapeDtypeStruct((N_DEV,) + small.shape, F32)],
        scratch_shapes=[pltpu.VMEM((4,) + in_blk, F32), pltpu.VMEM((4,) + in_blk, F32),
                        pltpu.VMEM((3,) + in_blk, MXU_DTYPE), pltpu.VMEM((3,) + in_blk, MXU_DTYPE),
                        pltpu.VMEM((4,) + out_blk, F32), pltpu.VMEM((4,) + out_blk, F32),
                        pltpu.VMEM((3,) + out_blk, MXU_DTYPE), pltpu.VMEM((3,) + out_blk, MXU_DTYPE),
                        pltpu.SemaphoreType.DMA((8,)), pltpu.SemaphoreType.DMA((8,)), pltpu.SemaphoreType.DMA((8,)),
                        pltpu.SemaphoreType.DMA((6,)), pltpu.SemaphoreType.DMA((6,)),
                        pltpu.SemaphoreType.DMA((2 * (N_DEV - 1),)), pltpu.SemaphoreType.DMA((2 * (N_DEV - 1),)),
                        pltpu.SemaphoreType.DMA((2,))],
        compiler_params=_cparams(),
    )(parts_w_in, parts_w_out, parts_wg, small)


def _adamw(recv, w, m, v, name):
    rows, width = w.shape
    tr = 128 if rows % 128 == 0 else rows
    n_parts = recv.shape[0]

    def body(r_ref, w_ref, m_ref, v_ref, g_ref, d_ref, nm_ref, nv_ref):
        g = r_ref[0]
        for j in range(1, n_parts):
            g = g + r_ref[j]
        nm = ADAM_B1 * m_ref[...] + (1.0 - ADAM_B1) * g
        nv = ADAM_B2 * v_ref[...] + (1.0 - ADAM_B2) * (g * g)
        m_hat = nm / (1.0 - ADAM_B1 ** ADAM_STEP)
        v_hat = nv / (1.0 - ADAM_B2 ** ADAM_STEP)
        g_ref[...] = g
        d_ref[...] = -ADAM_LR * (m_hat / (jnp.sqrt(v_hat) + ADAM_EPS) + ADAM_WD * w_ref[...])
        nm_ref[...] = nm
        nv_ref[...] = nv

    spec = _rows(tr, width)
    return pl.pallas_call(
        body, name=name, grid=(rows // tr,),
        in_specs=[pl.BlockSpec((n_parts, tr, width), lambda i: (0, i, 0)), spec, spec, spec],
        out_specs=[spec] * 4,
        out_shape=[jax.ShapeDtypeStruct((rows, width), F32)] * 4,
        compiler_params=_cparams(dimension_semantics=("arbitrary",)),
    )(recv, w, m, v)


def _pack_small(ln_g, ln_b, b_gate, norm_w, sinks, loss=None):
    def rows8(t):
        t = t.reshape(-1)
        t = jnp.pad(t, (0, 1024 - t.shape[0]))
        return t.reshape(8, 128)

    loss = jnp.zeros((1,), F32) if loss is None else loss
    return jnp.concatenate([rows8(t) for t in (ln_g, ln_b, b_gate, norm_w, sinks, loss)], axis=0)


def _unpack_small(p):
    flat = [p[8 * i:8 * (i + 1)].reshape(1, 1024) for i in range(5)]
    return flat[0], flat[1], flat[2][:, :256], flat[3][:, :128], flat[4][:, :8]


def kernel(x, positions, w_in, gla_w_gate_up, gla_b_gate, attn_sinks, gla_norm_w, w_out, ln_g, ln_b, loss_target, m_w_in, m_gla_w_gate_up, m_gla_b_gate, m_attn_sinks, m_gla_norm_w, m_w_out, m_ln_g, m_ln_b, v_w_in, v_gla_w_gate_up, v_gla_b_gate, v_attn_sinks, v_gla_norm_w, v_w_out, v_ln_g, v_ln_b):
    w_in_full, w_out_full, wg_full = _all_gather_weights(w_in[0], w_out[0], gla_w_gate_up[0])

    loss, grad_x, parts_w_in, g_wg, g_bg, g_sinks, g_nw, g_w_out, g_ln = _local_step(
        x[0], positions[0], w_in_full, wg_full, gla_b_gate, attn_sinks[0], gla_norm_w, w_out_full, ln_g, ln_b,
        loss_target[0])

    parts_w_out = g_w_out.reshape(N_DEV, D_OUT_SHARD, D_MODEL)
    parts_wg = jnp.transpose(g_wg.reshape(GLA_RANK, N_DEV, 32), (1, 0, 2))
    small = _pack_small(g_ln[0:1], g_ln[1:2], g_bg, g_nw, g_sinks[:, 0].reshape(1, SWA_Q_HEADS), loss[0, 0:1])
    g_in, g_out, r_wg, r_small = _reduce_grads(parts_w_in, parts_w_out, parts_wg, small)

    upd_in = _adamw(g_in[None], w_in[0], m_w_in[0], v_w_in[0], "adamw_w_in")
    upd_out = _adamw(g_out[None], w_out[0], m_w_out[0], v_w_out[0], "adamw_w_out")
    upd_wg = _adamw(r_wg, gla_w_gate_up[0], m_gla_w_gate_up[0], v_gla_w_gate_up[0], "adamw_wg")
    upd_small = _adamw(
        r_small,
        _pack_small(ln_g, ln_b, gla_b_gate, gla_norm_w, attn_sinks),
        _pack_small(m_ln_g, m_ln_b, m_gla_b_gate, m_gla_norm_w, m_attn_sinks),
        _pack_small(v_ln_g, v_ln_b, v_gla_b_gate, v_gla_norm_w, v_attn_sinks), "adamw_small")

    total = jnp.sum(r_small[:, 40, 0])
    outs = [total, grad_x[None]]
    for kind in range(4):
        s_ln_g, s_ln_b, s_bg, s_nw, s_sinks = _unpack_small(upd_small[kind])
        outs += [upd_in[kind][None], upd_wg[kind][None], s_bg, s_sinks, s_nw, upd_out[kind][None], s_ln_g, s_ln_b]
    return tuple(outs)
```

```python
import functools

import jax
import jax.numpy as jnp
from jax import lax
from jax.experimental import pallas as pl
from jax.experimental.pallas import tpu as pltpu

F32 = jnp.float32
MXU_DTYPE = jnp.bfloat16

N_DEV = 8
D_MODEL = 1024
SWA_Q_HEADS = 8
SWA_KV_HEADS = 2
SWA_GROUP = 4
SWA_HEAD_DIM = 64
BLOCK = 128
ROPE_THETA = 500000.0
ROT_DIM = 16
GLA_HEADS = 4
GLA_DK = 64
GLA_DV = 128
GLA_RANK = 16
GLA_TAU = 16.0
GLA_CHUNK = 64
D_IN_PROJ = 2832
D_IN_SHARD = D_IN_PROJ // N_DEV
D_OUT_SHARD = D_MODEL // N_DEV
OFF = (0, 512, 640, 768, 1280, 1536, 1792, 2304, 2816, 2832)
EPS = 1e-5
ALPHA = 2.0 ** 0.25
SWA_SCALE = SWA_HEAD_DIM ** -0.5
GLA_SCALE = GLA_DK ** -0.5
ADAM_LR = 0.001
ADAM_B1 = 0.9
ADAM_B2 = 0.999
ADAM_EPS = 1e-08
ADAM_WD = 0.01
ADAM_STEP = 10
VMEM_LIMIT = 56 * 1024 * 1024

_NT = (((1,), (1,)), ((), ()))
_TN = (((0,), (0,)), ((), ()))


def _mm(a, b):
    return jnp.dot(a, b, preferred_element_type=F32)


def _mm_nt(a, b):
    return lax.dot_general(a, b, _NT, preferred_element_type=F32)


def _mm_tn(a, b):
    return lax.dot_general(a, b, _TN, preferred_element_type=F32)


def _sigmoid(t):
    return 1.0 / (1.0 + jnp.exp(-t))


def _cparams(**kw):
    return pltpu.CompilerParams(vmem_limit_bytes=VMEM_LIMIT, **kw)


def _full(shape):
    return pl.BlockSpec(shape, lambda *_: (0,) * len(shape))


def _rows(tile, width):
    return pl.BlockSpec((tile, width), lambda i: (i, 0))


def _rope_tables(positions):
    half = ROT_DIM // 2
    inv_freq = ROPE_THETA ** (-jnp.arange(half, dtype=F32) / half)
    lane = jnp.arange(128, dtype=jnp.int32) % SWA_HEAD_DIM
    ang = positions.astype(F32)[:, None] * jnp.tile(inv_freq, 128 // half)[None, :]
    cos, sin = jnp.cos(ang), jnp.sin(ang)
    c = jnp.where(lane < ROT_DIM, cos, 1.0)
    s1 = jnp.where(lane < half, -sin, 0.0)
    s2 = jnp.where((lane >= half) & (lane < ROT_DIM), sin, 0.0)
    return c, s1, s2


def _rope(t, c, s1, s2):
    return t * c + pltpu.roll(t, 120, 1) * s1 + pltpu.roll(t, 8, 1) * s2


def _rope_t(g, c, s1, s2):
    return g * c + pltpu.roll(g * s1, 8, 1) + pltpu.roll(g * s2, 120, 1)


def _in_proj(x, w_in, wg, b_gate, rope):
    s = x.shape[0]
    ts = min(512, s)
    widths = [OFF[i + 1] - OFF[i] for i in range(9)]

    def body(x_ref, w_ref, wg_ref, bg_ref, c_ref, s1_ref, s2_ref,
             qa_ref, ka_ref, va_ref, ga_ref, qb_ref, kb_ref, vb_ref, gb_ref, rb_ref, la_ref, oms_ref):
        xb = x_ref[...].astype(MXU_DTYPE)
        c, s1, s2 = c_ref[...], s1_ref[...], s2_ref[...]
        i0 = pl.program_id(0)

        @pl.when(i0 == 0)
        def _():
            ka_ref[0:BLOCK, :] = jnp.zeros((BLOCK, 128), ka_ref.dtype)
            va_ref[0:BLOCK, :] = jnp.zeros((BLOCK, 128), va_ref.dtype)

        kv_rows = pl.ds(pl.multiple_of(BLOCK + i0 * ts, BLOCK), ts)

        def cols(i):
            return _mm(xb, w_ref[:, OFF[i]:OFF[i + 1]])

        qa = cols(0)
        for i in range(4):
            qa_ref[:, 128 * i:128 * (i + 1)] = _rope(qa[:, 128 * i:128 * (i + 1)], c, s1, s2).astype(qa_ref.dtype)
        ka_ref[kv_rows, :] = _rope(cols(1), c, s1, s2).astype(ka_ref.dtype)
        va_ref[kv_rows, :] = cols(2).astype(va_ref.dtype)
        ga_ref[...] = cols(3)
        qb_ref[...] = cols(4)
        kb_ref[...] = cols(5)
        vb_ref[...] = cols(6).astype(vb_ref.dtype)
        gb_ref[...] = cols(7)
        rb = cols(8)
        rb_ref[...] = rb
        logit = _mm(rb.astype(MXU_DTYPE), wg_ref[...]) + bg_ref[...]
        e = jnp.exp(-jnp.abs(logit))
        la_ref[...] = (jnp.minimum(logit, 0.0) - jnp.log(1.0 + e)) / GLA_TAU
        oms_ref[...] = jnp.where(logit >= 0.0, e, 1.0) / (1.0 + e)

    out_shape = [jax.ShapeDtypeStruct((s + BLOCK if i in (1, 2) else s, w), MXU_DTYPE if i in (0, 1, 2, 6) else F32)
                 for i, w in enumerate(widths)]
    out_shape += [jax.ShapeDtypeStruct((s, 256), F32)] * 2
    return pl.pallas_call(
        body, name="in_proj", grid=(s // ts,),
        in_specs=[_rows(ts, D_MODEL), _full((D_MODEL, D_IN_PROJ)), _full((GLA_RANK, 256)), _full((1, 256)),
                  _rows(ts, 128), _rows(ts, 128), _rows(ts, 128)],
        out_specs=[_full((s + BLOCK, w)) if i in (1, 2) else _rows(ts, w) for i, w in enumerate(widths)]
        + [_rows(ts, 256)] * 2,
        out_shape=out_shape,
        compiler_params=_cparams(dimension_semantics=("arbitrary",)),
    )(x, w_in, wg, b_gate, *rope)


SWA_ROWS = SWA_GROUP * BLOCK


def _swa_bias():
    qi = lax.broadcasted_iota(jnp.int32, (2, SWA_ROWS, 2 * BLOCK), 1) & (BLOCK - 1)
    ki = lax.broadcasted_iota(jnp.int32, (2, SWA_ROWS, 2 * BLOCK), 2)
    first = lax.broadcasted_iota(jnp.int32, (2, SWA_ROWS, 2 * BLOCK), 0) == 0
    dist = qi + BLOCK - ki
    ok = (dist >= 0) & (dist < BLOCK) & (jnp.logical_not(first) | (ki >= BLOCK))
    return jnp.where(ok, 0.0, -jnp.inf).astype(F32)


SWA_SUB = 2


def _swa_bias_of(bias_ref, n, b):
    return bias_ref[jnp.minimum(n, 1)] if b == 0 else bias_ref[1]


def _swa_dup(t, j):
    t = t.astype(F32)
    low = lax.broadcasted_iota(jnp.int32, t.shape, 1) < SWA_HEAD_DIM
    keep = low if j == 0 else jnp.logical_not(low)
    return jnp.where(keep, t, pltpu.roll(t, SWA_HEAD_DIM, 1)).astype(MXU_DTYPE)


def _swa_stack(t, j):
    low = lax.broadcasted_iota(jnp.int32, (BLOCK, 128), 1) < SWA_HEAD_DIM
    zero = jnp.zeros((BLOCK, 128), t.dtype)
    blocks = []
    for p in (2 * j, 2 * j + 1):
        tp = t[:, 128 * p:128 * (p + 1)]
        blocks += [jnp.where(low, tp, zero), jnp.where(low, zero, tp)]
    return jnp.concatenate(blocks, axis=0)


def _swa_unstack(t):
    low = lax.broadcasted_iota(jnp.int32, (BLOCK, 128), 1) < SWA_HEAD_DIM
    return [jnp.where(low, t[2 * BLOCK * i:2 * BLOCK * i + BLOCK], t[2 * BLOCK * i + BLOCK:2 * BLOCK * (i + 1)])
            for i in range(2)]


def _swa_sink_col(sink_ref, j):
    row = lax.broadcasted_iota(jnp.int32, (SWA_ROWS, 1), 0)
    col = jnp.full((SWA_ROWS, 1), sink_ref[SWA_GROUP * j], F32)
    for r in range(1, SWA_GROUP):
        col = jnp.where(row >= BLOCK * r, sink_ref[SWA_GROUP * j + r], col)
    return col


def _swa_probs(qs, kd, bias, sink):
    sc = _mm_nt(qs, kd) + bias
    m = jnp.maximum(jnp.max(sc, axis=1, keepdims=True), sink)
    p = jnp.exp(sc - m)
    ps = jnp.exp(sink - m)
    rinv = 1.0 / (jnp.sum(p, axis=1, keepdims=True) + ps)
    return p * rinv, ps * rinv


def _swa_fwd(sinks, qa, k_pad, v_pad, ga):
    s = qa.shape[0]
    tq = SWA_SUB * BLOCK

    def body(sink_ref, qa_ref, ga_ref, bias_ref, k_ref, v_ref, attn_ref, cat_ref):
        n = pl.program_id(0)
        for b in range(SWA_SUB):
            rows = slice(BLOCK * b, BLOCK * (b + 1))
            start = pl.multiple_of((n * SWA_SUB + b) * BLOCK, BLOCK)
            kw = k_ref[pl.ds(start, 2 * BLOCK), :]
            vw = v_ref[pl.ds(start, 2 * BLOCK), :]
            bias = _swa_bias_of(bias_ref, n, b)
            q = qa_ref[rows, :] * SWA_SCALE
            g = ga_ref[rows, :]
            silu = g * _sigmoid(g)
            for j in range(SWA_KV_HEADS):
                qs = _swa_stack(q, j).astype(MXU_DTYPE)
                probs, _ = _swa_probs(qs, _swa_dup(kw, j), bias, _swa_sink_col(sink_ref, j))
                pairs = _swa_unstack(_mm(probs.astype(MXU_DTYPE), _swa_dup(vw, j)))
                for i in range(2):
                    lanes = slice(128 * (2 * j + i), 128 * (2 * j + i + 1))
                    attn_ref[rows, lanes] = pairs[i]
                    cat_ref[rows, lanes] = (pairs[i] * silu[:, lanes]).astype(cat_ref.dtype)

    return pl.pallas_call(
        body, name="swa_fwd", grid=(s // tq,),
        in_specs=[pl.BlockSpec(memory_space=pltpu.SMEM), _rows(tq, 512), _rows(tq, 512),
                  _full((2, SWA_ROWS, 2 * BLOCK)), _full((s + BLOCK, 128)), _full((s + BLOCK, 128))],
        out_specs=[_rows(tq, 512), _rows(tq, 512)],
        out_shape=[jax.ShapeDtypeStruct((s, 512), F32), jax.ShapeDtypeStruct((s, 512), MXU_DTYPE)],
        compiler_params=_cparams(dimension_semantics=("arbitrary",)),
    )(sinks, qa, ga, _swa_bias(), k_pad, v_pad)


GLA_KW = GLA_HEADS * GLA_DK
GLA_VW = GLA_HEADS * GLA_DV


def _idiv(t, d):
    return t >> (d.bit_length() - 1)


def _chunk_cumsum(t, lower):
    n, w = t.shape
    r = lax.broadcasted_iota(jnp.int32, (n, n), 0)
    c = lax.broadcasted_iota(jnp.int32, (n, n), 1)
    tri = ((_idiv(r, GLA_CHUNK) == _idiv(c, GLA_CHUNK)) & ((r >= c) if lower else (r <= c))).astype(MXU_DTYPE)
    hi = t.astype(MXU_DTYPE)
    r1 = t - hi.astype(F32)
    mid = r1.astype(MXU_DTYPE)
    lo = (r1 - mid.astype(F32)).astype(MXU_DTYPE)
    parts = _mm(tri, jnp.concatenate([hi, mid, lo], axis=1))
    return (parts[:, :w] + parts[:, w:2 * w]) + parts[:, 2 * w:]


def _chunk_last(t):
    n = t.shape[0]
    return jnp.concatenate(
        [jnp.broadcast_to(t[c + GLA_CHUNK - 1:c + GLA_CHUNK, :], (GLA_CHUNK, t.shape[1]))
         for c in range(0, n, GLA_CHUNK)], axis=0)


def _head_stack(t, width):
    head = _idiv(lax.broadcasted_iota(jnp.int32, t.shape, 1), width)
    zero = jnp.zeros_like(t)
    return jnp.concatenate([jnp.where(head == h, t, zero) for h in range(GLA_HEADS)], axis=0)


def _gla_masks():
    row = lax.broadcasted_iota(jnp.int32, (GLA_CHUNK, GLA_KW), 0)
    pos = lax.broadcasted_iota(jnp.int32, (GLA_CHUNK, GLA_KW), 1) & (GLA_CHUNK - 1)
    srow = _idiv(lax.broadcasted_iota(jnp.int32, (GLA_VW, GLA_KW), 0), GLA_DV)
    slane = _idiv(lax.broadcasted_iota(jnp.int32, (GLA_VW, GLA_KW), 1), GLA_DK)
    return pos <= row, pos >= row, srow == slane


def _gla_fwd(qb, kb, vb, la, gb, norm_w):
    s = qb.shape[0]
    tb = min(256, s)
    ch = tb // GLA_CHUNK

    def body(qb_ref, kb_ref, vb_ref, la_ref, gb_ref, nw_ref, o_ref, cat_ref, sp_ref, st_ref):
        @pl.when(pl.program_id(0) == 0)
        def _():
            st_ref[...] = jnp.zeros_like(st_ref)

        causal, _, same_head = _gla_masks()
        nw = nw_ref[...]
        b = _chunk_cumsum(la_ref[...], True)
        bl = _chunk_last(b)
        k = kb_ref[...]
        qd = ((qb_ref[...] * GLA_SCALE) * jnp.exp(b)).astype(MXU_DTYPE)
        ki = (k * jnp.exp(-b)).astype(MXU_DTYPE)
        ke = (k * jnp.exp(bl - b)).astype(MXU_DTYPE)
        dec = jnp.exp(bl)
        v = vb_ref[...].astype(MXU_DTYPE)
        g = gb_ref[...]
        silu = g * _sigmoid(g)
        for ci in range(ch):
            rows = slice(GLA_CHUNK * ci, GLA_CHUNK * (ci + 1))
            a = jnp.where(causal, _mm_nt(qd[rows], _head_stack(ki[rows], GLA_DK)), 0.0).astype(MXU_DTYPE)
            st = st_ref[...]
            sp_ref[ci] = (st[0:GLA_DV] + st[GLA_DV:2 * GLA_DV]) + (st[2 * GLA_DV:3 * GLA_DV] + st[3 * GLA_DV:])
            o = _mm(a, _head_stack(v[rows], GLA_DV)) + _mm_nt(qd[rows], st.astype(MXU_DTYPE))
            st_ref[...] = st * dec[rows][0:1] + jnp.where(same_head, _mm_tn(v[rows], ke[rows]), 0.0)
            o_ref[rows, :] = o
            for h in range(GLA_HEADS):
                lv = slice(GLA_DV * h, GLA_DV * (h + 1))
                oh = o[:, lv]
                r = lax.rsqrt(jnp.mean(oh * oh, axis=1, keepdims=True) + EPS)
                cat_ref[rows, lv] = (oh * r * nw * silu[rows, lv]).astype(cat_ref.dtype)

    return pl.pallas_call(
        body, name="gla_fwd", grid=(s // tb,),
        in_specs=[_rows(tb, 256), _rows(tb, 256), _rows(tb, 512), _rows(tb, 256), _rows(tb, 512), _full((1, 128))],
        out_specs=[_rows(tb, 512), _rows(tb, 512), pl.BlockSpec((ch, GLA_DV, 256), lambda i: (i, 0, 0))],
        out_shape=[jax.ShapeDtypeStruct((s, 512), F32), jax.ShapeDtypeStruct((s, 512), MXU_DTYPE),
                   jax.ShapeDtypeStruct((s // GLA_CHUNK, GLA_DV, 256), F32)],
        scratch_shapes=[pltpu.VMEM((GLA_VW, GLA_KW), F32)],
        compiler_params=_cparams(dimension_semantics=("arbitrary",)),
    )(qb, kb, vb, la, gb, norm_w)


def _out_ln_loss(cat_a, cat_b, w_out, x, target, ln_g, ln_b):
    s = x.shape[0]
    ts = min(512, s)
    halves = 2 if ts % 32 == 0 else 1
    th = ts // halves

    def body(ca_ref, cb_ref, w_ref, x_ref, t_ref, g_ref, b_ref,
             loss_ref, gx_ref, da_ref, db_ref, gw_ref, gln_ref):
        @pl.when(pl.program_id(0) == 0)
        def _():
            loss_ref[...] = jnp.zeros_like(loss_ref)
            gw_ref[...] = jnp.zeros_like(gw_ref)
            gln_ref[...] = jnp.zeros_like(gln_ref)

        g = g_ref[...]
        dh16s = []
        for k in range(halves):
            rows = slice(th * k, th * (k + 1))
            mix = _mm(ca_ref[rows, :], w_ref[0:512, :]) + _mm(cb_ref[rows, :], w_ref[512:1024, :])
            h = ALPHA * x_ref[rows, :] + mix
            mu = jnp.mean(h, axis=1, keepdims=True)
            hc = h - mu
            rstd = lax.rsqrt(jnp.mean(hc * hc, axis=1, keepdims=True) + EPS)
            xhat = hc * rstd
            err = xhat * g + b_ref[...] - t_ref[rows, :]
            loss_ref[...] += 0.5 * jnp.sum(jnp.mean(err * err, axis=1, keepdims=True))
            dy = err * (1.0 / D_MODEL)
            gln_ref[0:1, :] += jnp.sum(dy * xhat, axis=0, keepdims=True)
            gln_ref[1:2, :] += jnp.sum(dy, axis=0, keepdims=True)
            dxh = dy * g
            dh = rstd * (dxh - jnp.mean(dxh, axis=1, keepdims=True)
                         - xhat * jnp.mean(dxh * xhat, axis=1, keepdims=True))
            gx_ref[rows, :] = ALPHA * dh
            dh16s.append(dh.astype(MXU_DTYPE))
        for k in range(halves):
            rows = slice(th * k, th * (k + 1))
            da_ref[rows, :] = _mm_nt(dh16s[k], w_ref[0:512, :])
            db_ref[rows, :] = _mm_nt(dh16s[k], w_ref[512:1024, :])
            gw_ref[0:512, :] += _mm_tn(ca_ref[rows, :], dh16s[k])
            gw_ref[512:1024, :] += _mm_tn(cb_ref[rows, :], dh16s[k])

    return pl.pallas_call(
        body, name="out_ln_loss", grid=(s // ts,),
        in_specs=[_rows(ts, 512), _rows(ts, 512), _full((D_MODEL, D_MODEL)), _rows(ts, D_MODEL), _rows(ts, D_MODEL),
                  _full((1, D_MODEL)), _full((1, D_MODEL))],
        out_specs=[_full((1, 128)), _rows(ts, D_MODEL), _rows(ts, 512), _rows(ts, 512),
                   _full((D_MODEL, D_MODEL)), _full((2, D_MODEL))],
        out_shape=[jax.ShapeDtypeStruct((1, 128), F32), jax.ShapeDtypeStruct((s, D_MODEL), F32),
                   jax.ShapeDtypeStruct((s, 512), F32), jax.ShapeDtypeStruct((s, 512), F32),
                   jax.ShapeDtypeStruct((D_MODEL, D_MODEL), F32), jax.ShapeDtypeStruct((2, D_MODEL), F32)],
        compiler_params=_cparams(dimension_semantics=("arbitrary",)),
    )(cat_a, cat_b, w_out, x, target, ln_g, ln_b)


def _swa_bwd(sinks, qa, k_pad, v_pad, attn, ga, d_cat_a, rope):
    s = qa.shape[0]
    tq = SWA_SUB * BLOCK

    def body(sink_ref, qa_ref, ga_ref, at_ref, dc_ref, c_ref, s1_ref, s2_ref, bias_ref, k_ref, v_ref,
             dq_ref, dg_ref, dk_ref, dv_ref, ds_ref):
        n = pl.program_id(0)

        @pl.when(n == 0)
        def _():
            dk_ref[...] = jnp.zeros_like(dk_ref)
            dv_ref[...] = jnp.zeros_like(dv_ref)
            ds_ref[...] = jnp.zeros_like(ds_ref)

        low = lax.broadcasted_iota(jnp.int32, (2 * BLOCK, 128), 1) < SWA_HEAD_DIM
        for b in range(SWA_SUB):
            rows = slice(BLOCK * b, BLOCK * (b + 1))
            start = pl.multiple_of((n * SWA_SUB + b) * BLOCK, BLOCK)
            kw = k_ref[pl.ds(start, 2 * BLOCK), :]
            vw = v_ref[pl.ds(start, 2 * BLOCK), :]
            bias = _swa_bias_of(bias_ref, n, b)
            q = qa_ref[rows, :] * SWA_SCALE
            g = ga_ref[rows, :]
            sg = _sigmoid(g)
            o = at_ref[rows, :]
            dc = dc_ref[rows, :]
            do = dc * (g * sg)
            dg_ref[rows, :] = (dc * o * (sg * (1.0 + g * (1.0 - sg)))).astype(dg_ref.dtype)
            od = do * o
            c, s1, s2 = c_ref[rows, :], s1_ref[rows, :], s2_ref[rows, :]
            dk, dv = [], []
            for j in range(SWA_KV_HEADS):
                kd, vd = _swa_dup(kw, j), _swa_dup(vw, j)
                qs = _swa_stack(q, j).astype(MXU_DTYPE)
                dos = _swa_stack(do, j).astype(MXU_DTYPE)
                probs, psink = _swa_probs(qs, kd, bias, _swa_sink_col(sink_ref, j))
                delta = jnp.sum(_swa_stack(od, j), axis=1, keepdims=True)
                dsc = (probs * (_mm_nt(dos, vd) - delta)).astype(MXU_DTYPE)
                dsink = psink * delta
                for r in range(SWA_GROUP):
                    h = SWA_GROUP * j + r
                    ds_ref[h:h + 1, :] += jnp.zeros((1, 128), F32) - jnp.sum(dsink[BLOCK * r:BLOCK * (r + 1)])
                dq = _swa_unstack(_mm(dsc, kd))
                for i in range(2):
                    lanes = slice(128 * (2 * j + i), 128 * (2 * j + i + 1))
                    dq_ref[rows, lanes] = _rope_t(dq[i] * SWA_SCALE, c, s1, s2).astype(dq_ref.dtype)
                dkj = _mm_tn(dsc, qs)
                dvj = _mm_tn(probs.astype(MXU_DTYPE), dos)
                dk.append(dkj + pltpu.roll(dkj, SWA_HEAD_DIM, 1))
                dv.append(dvj + pltpu.roll(dvj, SWA_HEAD_DIM, 1))
            dk_ref[pl.ds(start, 2 * BLOCK), :] += jnp.where(low, dk[0], dk[1])
            dv_ref[pl.ds(start, 2 * BLOCK), :] += jnp.where(low, dv[0], dv[1])

    return pl.pallas_call(
        body, name="swa_bwd", grid=(s // tq,),
        in_specs=[pl.BlockSpec(memory_space=pltpu.SMEM)] + [_rows(tq, 512)] * 4 + [_rows(tq, 128)] * 3
        + [_full((2, SWA_ROWS, 2 * BLOCK))] + [_full((s + BLOCK, 128))] * 2,
        out_specs=[_rows(tq, 512), _rows(tq, 512), _full((s + BLOCK, 128)), _full((s + BLOCK, 128)),
                   _full((SWA_Q_HEADS, 128))],
        out_shape=[jax.ShapeDtypeStruct((s, 512), MXU_DTYPE), jax.ShapeDtypeStruct((s, 512), MXU_DTYPE),
                   jax.ShapeDtypeStruct((s + BLOCK, 128), F32), jax.ShapeDtypeStruct((s + BLOCK, 128), F32),
                   jax.ShapeDtypeStruct((SWA_Q_HEADS, 128), F32)],
        compiler_params=_cparams(dimension_semantics=("arbitrary",)),
    )(sinks, qa, ga, attn, d_cat_a, *rope, _swa_bias(), k_pad, v_pad)


def _gla_bwd(qb, kb, vb, la, oms, gb, o, sprev, d_cat_b, rb, wg, norm_w):
    s = qb.shape[0]
    tb = min(256, s)
    ch = tb // GLA_CHUNK
    nb = s // tb

    def body(qb_ref, kb_ref, vb_ref, la_ref, oms_ref, gb_ref, o_ref, sp_ref, dc_ref, rb_ref, wg_ref, nw_ref,
             dq_ref, dk_ref, dv_ref, dg_ref, dr_ref, gwg_ref, gbg_ref, gnw_ref, dst_ref):
        @pl.when(pl.program_id(0) == 0)
        def _():
            dst_ref[...] = jnp.zeros_like(dst_ref)
            gwg_ref[...] = jnp.zeros_like(gwg_ref)
            gbg_ref[...] = jnp.zeros_like(gbg_ref)
            gnw_ref[...] = jnp.zeros_like(gnw_ref)

        causal, causal_t, same_head = _gla_masks()
        nw = nw_ref[...]
        b = _chunk_cumsum(la_ref[...], True)
        bl = _chunk_last(b)
        eb, enb, ee, dec = jnp.exp(b), jnp.exp(-b), jnp.exp(bl - b), jnp.exp(bl)
        k = kb_ref[...]
        qd = (qb_ref[...] * GLA_SCALE) * eb
        ki = k * enb
        ke = k * ee
        qd16, ki16, ke16 = qd.astype(MXU_DTYPE), ki.astype(MXU_DTYPE), ke.astype(MXU_DTYPE)
        v16 = vb_ref[...].astype(MXU_DTYPE)

        g = gb_ref[...]
        sg = _sigmoid(g)
        silu = g * sg
        dsilu = sg * (1.0 + g * (1.0 - sg))
        gnw = jnp.zeros((1, GLA_DV), F32)
        do = []
        for h in range(GLA_HEADS):
            lv = slice(GLA_DV * h, GLA_DV * (h + 1))
            oh = o_ref[:, lv]
            dch = dc_ref[:, lv]
            r = lax.rsqrt(jnp.mean(oh * oh, axis=1, keepdims=True) + EPS)
            d_on = dch * silu[:, lv]
            dg_ref[:, lv] = (dch * (oh * r * nw) * dsilu[:, lv]).astype(dg_ref.dtype)
            gnw += jnp.sum(d_on * oh * r, axis=0, keepdims=True)
            u = d_on * nw
            do.append(r * u - oh * (r * r * r) * jnp.mean(u * oh, axis=1, keepdims=True))
        gnw_ref[...] += gnw
        do16 = jnp.concatenate(do, axis=1).astype(MXU_DTYPE)

        db, dbl = [None] * ch, [None] * ch
        for ci in reversed(range(ch)):
            rows = slice(GLA_CHUNK * ci, GLA_CHUNK * (ci + 1))
            qds, kis = _head_stack(qd16[rows], GLA_DK), _head_stack(ki16[rows], GLA_DK)
            vs, dos = _head_stack(v16[rows], GLA_DV), _head_stack(do16[rows], GLA_DV)
            a = jnp.where(causal, _mm_nt(qd16[rows], kis), 0.0).astype(MXU_DTYPE)
            at = jnp.where(causal_t, _mm_nt(ki16[rows], qds), 0.0).astype(MXU_DTYPE)
            da = jnp.where(causal, _mm_nt(do16[rows], vs), 0.0).astype(MXU_DTYPE)
            dat = jnp.where(causal_t, _mm_nt(v16[rows], dos), 0.0).astype(MXU_DTYPE)
            stc = sp_ref[ci]
            st = jnp.where(same_head, jnp.concatenate([stc] * GLA_HEADS, axis=0), 0.0)
            st16 = st.astype(MXU_DTYPE)
            dst = dst_ref[...]
            dst16 = dst.astype(MXU_DTYPE)
            dv = _mm(at, dos) + _mm_nt(ke16[rows], dst16)
            dqd = _mm(da, kis) + _mm(do16[rows], st16)
            dki = _mm(dat, qds)
            dke = _mm(v16[rows], dst16)
            ddec = jnp.sum(dst * st, axis=0, keepdims=True)
            decc = dec[rows][0:1]
            dst_ref[...] = jnp.where(same_head, _mm_tn(do16[rows], qd16[rows]), 0.0) + dst * decc
            dq_ref[rows, :] = (dqd * eb[rows] * GLA_SCALE).astype(dq_ref.dtype)
            dk_ref[rows, :] = (dki * enb[rows] + dke * ee[rows]).astype(dk_ref.dtype)
            dv_ref[rows, :] = dv.astype(dv_ref.dtype)
            dke_ke = dke * ke[rows]
            db[ci] = dqd * qd[rows] - dki * ki[rows] - dke_ke
            dbl[ci] = jnp.broadcast_to(jnp.sum(dke_ke, axis=0, keepdims=True) + ddec * decc, (GLA_CHUNK, GLA_KW))

        dla = _chunk_cumsum(jnp.concatenate(db, axis=0), False) + jnp.concatenate(dbl, axis=0)
        dlogit = dla * oms_ref[...] * (1.0 / GLA_TAU)
        dl16 = dlogit.astype(MXU_DTYPE)
        gbg_ref[...] += jnp.sum(dlogit, axis=0, keepdims=True)
        gwg_ref[...] += _mm_tn(rb_ref[...].astype(MXU_DTYPE), dl16)
        dr_ref[...] = _mm_nt(dl16, wg_ref[...]).astype(dr_ref.dtype)

    def rev(width):
        return pl.BlockSpec((tb, width), lambda i: (nb - 1 - i, 0))

    return pl.pallas_call(
        body, name="gla_bwd", grid=(nb,),
        in_specs=[rev(256), rev(256), rev(512), rev(256), rev(256), rev(512), rev(512),
                  pl.BlockSpec((ch, GLA_DV, 256), lambda i: (nb - 1 - i, 0, 0)), rev(512), rev(GLA_RANK),
                  _full((GLA_RANK, 256)), _full((1, 128))],
        out_specs=[rev(256), rev(256), rev(512), rev(512), rev(GLA_RANK),
                   _full((GLA_RANK, 256)), _full((1, 256)), _full((1, 128))],
        out_shape=[jax.ShapeDtypeStruct((s, 256), MXU_DTYPE), jax.ShapeDtypeStruct((s, 256), MXU_DTYPE),
                   jax.ShapeDtypeStruct((s, 512), MXU_DTYPE), jax.ShapeDtypeStruct((s, 512), MXU_DTYPE),
                   jax.ShapeDtypeStruct((s, GLA_RANK), MXU_DTYPE), jax.ShapeDtypeStruct((GLA_RANK, 256), F32),
                   jax.ShapeDtypeStruct((1, 256), F32), jax.ShapeDtypeStruct((1, 128), F32)],
        scratch_shapes=[pltpu.VMEM((GLA_VW, GLA_KW), F32)],
        compiler_params=_cparams(dimension_semantics=("arbitrary",)),
    )(qb, kb, vb, la, oms, gb, o, sprev, d_cat_b, rb, wg, norm_w)


def _in_proj_bwd_x(gx0, pieces, w_in, rope):
    s = gx0.shape[0]
    ts = min(512, s)
    widths = [OFF[i + 1] - OFF[i] for i in range(9)]

    def body(gx0_ref, *refs):
        piece_refs = refs[:9]
        w_ref, c_ref, s1_ref, s2_ref, gx_ref, dp_ref = refs[9:]
        kv_rows = pl.ds(pl.multiple_of(BLOCK + pl.program_id(0) * ts, BLOCK), ts)
        acc = gx0_ref[...]
        for i in range(9):
            t = piece_refs[i][kv_rows, :] if i in (1, 2) else piece_refs[i][...]
            if i == 1:
                t = _rope_t(t, c_ref[...], s1_ref[...], s2_ref[...])
            t16 = t.astype(MXU_DTYPE)
            dp_ref[:, OFF[i]:OFF[i + 1]] = t16
            acc += _mm_nt(t16, w_ref[:, OFF[i]:OFF[i + 1]])
        gx_ref[...] = acc

    return pl.pallas_call(
        body, name="in_proj_bwd_x", grid=(s // ts,),
        in_specs=[_rows(ts, D_MODEL)]
        + [_full((s + BLOCK, w)) if i in (1, 2) else _rows(ts, w) for i, w in enumerate(widths)]
        + [_full((D_MODEL, D_IN_PROJ))] + [_rows(ts, 128)] * 3,
        out_specs=[_rows(ts, D_MODEL), _rows(ts, D_IN_PROJ)],
        out_shape=[jax.ShapeDtypeStruct((s, D_MODEL), F32), jax.ShapeDtypeStruct((s, D_IN_PROJ), MXU_DTYPE)],
        compiler_params=_cparams(dimension_semantics=("arbitrary",)),
    )(gx0, *pieces, w_in, *rope)


def _in_proj_bwd_w(x, dproj):
    s = x.shape[0]
    ts = min(512, s)
    nsteps = s // ts
    col_chunks = [(OFF[i], OFF[i + 1]) for i in range(9)]

    def body(x_ref, dp_ref, gw_ref, acc_ref, stage_ref, sems):
        i = pl.program_id(0)

        @pl.when(i == 0)
        def _():
            acc_ref[...] = jnp.zeros_like(acc_ref)

        xb = x_ref[...].astype(MXU_DTYPE)
        for lo, hi in col_chunks:
            acc_ref[:, lo:hi] += _mm_tn(xb, dp_ref[:, lo:hi])

        @pl.when(i == nsteps - 1)
        def _():
            copies = []
            for j in range(N_DEV):
                slot = j % 2
                if j >= 2:
                    copies[j - 2].wait()
                stage_ref[slot] = acc_ref[:, D_IN_SHARD * j:D_IN_SHARD * (j + 1)]
                cp = pltpu.make_async_copy(stage_ref.at[slot], gw_ref.at[j], sems.at[slot])
                cp.start()
                copies.append(cp)
            copies[N_DEV - 2].wait()
            copies[N_DEV - 1].wait()

    return pl.pallas_call(
        body, name="in_proj_bwd_w", grid=(nsteps,),
        in_specs=[_rows(ts, D_MODEL), _rows(ts, D_IN_PROJ)],
        out_specs=pl.BlockSpec(memory_space=pl.ANY),
        out_shape=jax.ShapeDtypeStruct((N_DEV, D_MODEL, D_IN_SHARD), F32),
        scratch_shapes=[pltpu.VMEM((D_MODEL, D_IN_PROJ), F32), pltpu.VMEM((2, D_MODEL, D_IN_SHARD), F32),
                        pltpu.SemaphoreType.DMA((2,))],
        compiler_params=_cparams(dimension_semantics=("arbitrary",)),
    )(x, dproj)


def _local_step(x, positions, w_in, wg, b_gate, sinks, norm_w, w_out, ln_g, ln_b, target):
    rope = _rope_tables(positions)
    qa, k_pad, v_pad, ga, qb, kb, vb, gb, rb, la, oms = _in_proj(x, w_in, wg, b_gate, rope)
    attn, cat_a = _swa_fwd(sinks, qa, k_pad, v_pad, ga)
    o, cat_b, sprev = _gla_fwd(qb, kb, vb, la, gb, norm_w)
    loss, gx0, d_cat_a, d_cat_b, g_w_out, g_ln = _out_ln_loss(cat_a, cat_b, w_out, x, target, ln_g, ln_b)
    dqa, dga, dk_pad, dv_pad, g_sinks = _swa_bwd(sinks, qa, k_pad, v_pad, attn, ga, d_cat_a, rope)
    dqb, dkb, dvb, dgb, drb, g_wg, g_bg, g_nw = _gla_bwd(qb, kb, vb, la, oms, gb, o, sprev, d_cat_b, rb, wg, norm_w)
    pieces = (dqa, dk_pad, dv_pad, dga, dqb, dkb, dvb, dgb, drb)
    grad_x, dproj = _in_proj_bwd_x(gx0, pieces, w_in, rope)
    g_w_in = _in_proj_bwd_w(x, dproj)
    return loss, grad_x, g_w_in, g_wg, g_bg, g_sinks, g_nw, g_w_out, g_ln


def _mesh_pos():
    return lax.axis_index("x"), lax.axis_index("y"), lax.axis_index("c")


def _peer(k, x, y, c):
    px = (1 - x) if k & 4 else x
    py = (1 - y) if k & 2 else y
    pc = (1 - c) if k & 1 else c
    return (px, py, pc), 4 * px + 2 * py + pc


def _other_chips(x, y):
    return [(1 - x, y), (x, 1 - y), (1 - x, 1 - y)]


def _all_gather_weights(w_in_s, w_out_s, wg_s):
    n_arr = 3

    def body(win_ref, wout_ref, wg_ref, win_full, wout_full, wg_full, win_all, wout_all, wg_all, send_sems, recv_sems):
        x, y, c = _mesh_pos()
        me = 4 * x + 2 * y + c
        sibling = (x, y, 1 - c)
        chips = _other_chips(x, y)
        alls = (win_all, wout_all, wg_all)

        def copy(k, a, block, to):
            return pltpu.make_async_remote_copy(
                src_ref=alls[a].at[block], dst_ref=alls[a].at[block], send_sem=send_sems.at[n_arr * k + a],
                recv_sem=recv_sems.at[n_arr * k + a], device_id=to, device_id_type=pl.DeviceIdType.MESH)

        win_all[me] = win_ref[...].astype(win_all.dtype)
        wout_all[me] = wout_ref[...].astype(wout_all.dtype)
        wg_all[me] = wg_ref[...].astype(wg_all.dtype)
        sends = []
        for a in range(n_arr):
            for j, (cx, cy) in enumerate(chips):
                sends.append(copy(1 + j, a, me, (cx, cy, c)))
            sends.append(copy(0, a, me, sibling))
        for cp in sends:
            cp.start()
        for j, (cx, cy) in enumerate(chips):
            for a in range(n_arr):
                block = 4 * cx + 2 * cy + c
                copy(1 + j, a, block, sibling).wait_recv()
                fwd = copy(4 + j, a, block, sibling)
                fwd.start()
                sends.append(fwd)
        for a in range(n_arr):
            copy(0, a, 4 * x + 2 * y + (1 - c), sibling).wait_recv()
        for j, (cx, cy) in enumerate(chips):
            for a in range(n_arr):
                copy(4 + j, a, 4 * cx + 2 * cy + (1 - c), sibling).wait_recv()
        for cp in sends:
            cp.wait_send()
        for j in range(N_DEV):
            win_full[:, D_IN_SHARD * j:D_IN_SHARD * (j + 1)] = win_all[j]
            wout_full[D_OUT_SHARD * j:D_OUT_SHARD * (j + 1), :] = wout_all[j]
            wg_full[:, 32 * j:32 * (j + 1)] = wg_all[j]

    vmem = pl.BlockSpec(memory_space=pltpu.VMEM)
    n_copies = n_arr * (N_DEV - 1)
    return pl.pallas_call(
        body, name="all_gather_weights",
        in_specs=[vmem] * 3, out_specs=[vmem] * 3,
        out_shape=[jax.ShapeDtypeStruct((D_MODEL, D_IN_PROJ), MXU_DTYPE),
                   jax.ShapeDtypeStruct((D_MODEL, D_MODEL), MXU_DTYPE),
                   jax.ShapeDtypeStruct((GLA_RANK, 256), MXU_DTYPE)],
        scratch_shapes=[pltpu.VMEM((N_DEV, D_MODEL, D_IN_SHARD), MXU_DTYPE),
                        pltpu.VMEM((N_DEV, D_OUT_SHARD, D_MODEL), MXU_DTYPE),
                        pltpu.VMEM((N_DEV, GLA_RANK, 32), MXU_DTYPE),
                        pltpu.SemaphoreType.DMA((n_copies,)), pltpu.SemaphoreType.DMA((n_copies,))],
        compiler_params=_cparams(),
    )(w_in_s, w_out_s, wg_s)


def _reduce_grads(parts_w_in, parts_w_out, parts_wg, small):
    def body(pin_ref, pout_ref, pwg_ref, sm_ref, gin_ref, gout_ref, rwg_ref, rsm_ref,
             own_in, sib_in, snd_in, rcv_in, own_out, sib_out, snd_out, rcv_out,
             loc_sems, d2d_send, d2d_recv, ici_send, ici_recv, sm_send, sm_recv, sm_loc):
        x, y, c = _mesh_pos()
        me = 4 * x + 2 * y + c
        sibling = (x, y, 1 - c)
        chips = [(x, y)] + _other_chips(x, y)
        parts = (pin_ref, pout_ref)
        own, sib, snd, rcv, outs = (own_in, own_out), (sib_in, sib_out), (snd_in, snd_out), (rcv_in, rcv_out), (gin_ref, gout_ref)

        small_dsts = (rwg_ref, rsm_ref)

        def small_src(a, block):
            return pwg_ref.at[block] if a == 0 else sm_ref

        small_local = [pltpu.make_async_copy(small_src(a, me), small_dsts[a].at[me], sm_loc.at[a]) for a in range(2)]
        for cp in small_local:
            cp.start()
        small_sends = []
        for k in range(1, N_DEV):
            peer, pidx = _peer(k, x, y, c)
            for a in range(2):
                i = 2 * (k - 1) + a
                cp = pltpu.make_async_remote_copy(
                    src_ref=small_src(a, pidx), dst_ref=small_dsts[a].at[me], send_sem=sm_send.at[i],
                    recv_sem=sm_recv.at[i], device_id=peer, device_id_type=pl.DeviceIdType.MESH)
                cp.start()
                small_sends.append(cp)

        local, d2d = {}, {}
        for r in (1, 2, 3, 0):
            cx, cy = chips[r]
            for a in range(2):
                i = 2 * r + a
                local[r, a] = pltpu.make_async_copy(parts[a].at[4 * cx + 2 * cy + c], own[a].at[r], loc_sems.at[i])
                local[r, a].start()
                d2d[r, a] = pltpu.make_async_remote_copy(
                    src_ref=parts[a].at[4 * cx + 2 * cy + (1 - c)], dst_ref=sib[a].at[r], send_sem=d2d_send.at[i],
                    recv_sem=d2d_recv.at[i], device_id=sibling, device_id_type=pl.DeviceIdType.MESH)
                d2d[r, a].start()
        ici = {}
        for r in (1, 2, 3):
            cx, cy = chips[r]
            for a in range(2):
                i = 2 * (r - 1) + a
                local[r, a].wait()
                d2d[r, a].wait_recv()
                snd[a][r - 1] = (own[a][r] + sib[a][r]).astype(snd[a].dtype)
                ici[r, a] = pltpu.make_async_remote_copy(
                    src_ref=snd[a].at[r - 1], dst_ref=rcv[a].at[r - 1], send_sem=ici_send.at[i],
                    recv_sem=ici_recv.at[i], device_id=(cx, cy, c), device_id_type=pl.DeviceIdType.MESH)
                ici[r, a].start()
        for a in range(2):
            local[0, a].wait()
            d2d[0, a].wait_recv()
            acc = own[a][0] + sib[a][0]
            for r in (1, 2, 3):
                ici[r, a].wait_recv()
                acc = acc + rcv[a][r - 1].astype(F32)
            outs[a][...] = acc

        for k in range(1, N_DEV):
            peer, pidx = _peer(k, x, y, c)
            for a in range(2):
                i = 2 * (k - 1) + a
                pltpu.make_async_remote_copy(
                    src_ref=small_src(a, me), dst_ref=small_dsts[a].at[pidx], send_sem=sm_send.at[i],
                    recv_sem=sm_recv.at[i], device_id=peer, device_id_type=pl.DeviceIdType.MESH).wait_recv()
        for cp in small_sends + list(d2d.values()) + list(ici.values()):
            cp.wait_send()
        for cp in small_local:
            cp.wait()

    hbm = pl.BlockSpec(memory_space=pl.ANY)
    vmem = pl.BlockSpec(memory_space=pltpu.VMEM)
    in_blk, out_blk = parts_w_in.shape[1:], parts_w_out.shape[1:]
    return pl.pallas_call(
        body, name="reduce_grads",
        in_specs=[hbm] * 4, out_specs=[vmem, vmem, hbm, hbm],
        out_shape=[jax.ShapeDtypeStruct(in_blk, F32), jax.ShapeDtypeStruct(out_blk, F32),
                   jax.ShapeDtypeStruct((N_DEV,) + parts_wg.shape[1:], F32),
                   jax.ShapeDtypeStruct((N_DEV,) + small.shape, F32)],
        scratch_shapes=[pltpu.VMEM((4,) + in_blk, F32), pltpu.VMEM((4,) + in_blk, F32),
                        pltpu.VMEM((3,) + in_blk, MXU_DTYPE), pltpu.VMEM((3,) + in_blk, MXU_DTYPE),
                        pltpu.VMEM((4,) + out_blk, F32), pltpu.VMEM((4,) + out_blk, F32),
                        pltpu.VMEM((3,) + out_blk, MXU_DTYPE), pltpu.VMEM((3,) + out_blk, MXU_DTYPE),
                        pltpu.SemaphoreType.DMA((8,)), pltpu.SemaphoreType.DMA((8,)), pltpu.SemaphoreType.DMA((8,)),
                        pltpu.SemaphoreType.DMA((6,)), pltpu.SemaphoreType.DMA((6,)),
                        pltpu.SemaphoreType.DMA((2 * (N_DEV - 1),)), pltpu.SemaphoreType.DMA((2 * (N_DEV - 1),)),
                        pltpu.SemaphoreType.DMA((2,))],
        compiler_params=_cparams(),
    )(parts_w_in, parts_w_out, parts_wg, small)


def _adamw(recv, w, m, v, name):
    rows, width = w.shape
    tr = 128 if rows % 128 == 0 else rows
    n_parts = recv.shape[0]

    def body(r_ref, w_ref, m_ref, v_ref, g_ref, d_ref, nm_ref, nv_ref):
        g = r_ref[0]
        for j in range(1, n_parts):
            g = g + r_ref[j]
        nm = ADAM_B1 * m_ref[...] + (1.0 - ADAM_B1) * g
        nv = ADAM_B2 * v_ref[...] + (1.0 - ADAM_B2) * (g * g)
        m_hat = nm / (1.0 - ADAM_B1 ** ADAM_STEP)
        v_hat = nv / (1.0 - ADAM_B2 ** ADAM_STEP)
        g_ref[...] = g
        d_ref[...] = -ADAM_LR * (m_hat / (jnp.sqrt(v_hat) + ADAM_EPS) + ADAM_WD * w_ref[...])
        nm_ref[...] = nm
        nv_ref[...] = nv

    spec = _rows(tr, width)
    return pl.pallas_call(
        body, name=name, grid=(rows // tr,),
        in_specs=[pl.BlockSpec((n_parts, tr, width), lambda i: (0, i, 0)), spec, spec, spec],
        out_specs=[spec] * 4,
        out_shape=[jax.ShapeDtypeStruct((rows, width), F32)] * 4,
        compiler_params=_cparams(dimension_semantics=("arbitrary",)),
    )(recv, w, m, v)


def _pack_small(ln_g, ln_b, b_gate, norm_w, sinks, loss=None):
    def rows8(t):
        t = t.reshape(-1)
        t = jnp.pad(t, (0, 1024 - t.shape[0]))
        return t.reshape(8, 128)

    loss = jnp.zeros((1,), F32) if loss is None else loss
    return jnp.concatenate([rows8(t) for t in (ln_g, ln_b, b_gate, norm_w, sinks, loss)], axis=0)


def _unpack_small(p):
    flat = [p[8 * i:8 * (i + 1)].reshape(1, 1024) for i in range(5)]
    return flat[0], flat[1], flat[2][:, :256], flat[3][:, :128], flat[4][:, :8]


def kernel(x, positions, w_in, gla_w_gate_up, gla_b_gate, attn_sinks, gla_norm_w, w_out, ln_g, ln_b, loss_target, m_w_in, m_gla_w_gate_up, m_gla_b_gate, m_attn_sinks, m_gla_norm_w, m_w_out, m_ln_g, m_ln_b, v_w_in, v_gla_w_gate_up, v_gla_b_gate, v_attn_sinks, v_gla_norm_w, v_w_out, v_ln_g, v_ln_b):
    w_in_full, w_out_full, wg_full = _all_gather_weights(w_in[0], w_out[0], gla_w_gate_up[0])

    loss, grad_x, parts_w_in, g_wg, g_bg, g_sinks, g_nw, g_w_out, g_ln = _local_step(
        x[0], positions[0], w_in_full, wg_full, gla_b_gate, attn_sinks[0], gla_norm_w, w_out_full, ln_g, ln_b,
        loss_target[0])

    parts_w_out = g_w_out.reshape(N_DEV, D_OUT_SHARD, D_MODEL)
    parts_wg = jnp.transpose(g_wg.reshape(GLA_RANK, N_DEV, 32), (1, 0, 2))
    small = _pack_small(g_ln[0:1], g_ln[1:2], g_bg, g_nw, g_sinks[:, 0].reshape(1, SWA_Q_HEADS), loss[0, 0:1])
    g_in, g_out, r_wg, r_small = _reduce_grads(parts_w_in, parts_w_out, parts_wg, small)

    upd_in = _adamw(g_in[None], w_in[0], m_w_in[0], v_w_in[0], "adamw_w_in")
    upd_out = _adamw(g_out[None], w_out[0], m_w_out[0], v_w_out[0], "adamw_w_out")
    upd_wg = _adamw(r_wg, gla_w_gate_up[0], m_gla_w_gate_up[0], v_gla_w_gate_up[0], "adamw_wg")
    upd_small = _adamw(
        r_small,
        _pack_small(ln_g, ln_b, gla_b_gate, gla_norm_w, attn_sinks),
        _pack_small(m_ln_g, m_ln_b, m_gla_b_gate, m_gla_norm_w, m_attn_sinks),
        _pack_small(v_ln_g, v_ln_b, v_gla_b_gate, v_gla_norm_w, v_attn_sinks), "adamw_small")

    total = jnp.sum(r_small[:, 40, 0])
    outs = [total, grad_x[None]]
    for kind in range(4):
        s_ln_g, s_ln_b, s_bg, s_nw, s_sinks = _unpack_small(upd_small[kind])
        outs += [upd_in[kind][None], upd_wg[kind][None], s_bg, s_sinks, s_nw, upd_out[kind][None], s_ln_g, s_ln_b]
    return tuple(outs)
```

```python
import functools

import jax
import jax.numpy as jnp
from jax import lax
from jax.experimental import pallas as pl
from jax.experimental.pallas import tpu as pltpu

F32 = jnp.float32
MXU_DTYPE = jnp.bfloat16

N_DEV = 8
D_MODEL = 1024
SWA_Q_HEADS = 8
SWA_KV_HEADS = 2
SWA_GROUP = 4
SWA_HEAD_DIM = 64
BLOCK = 128
ROPE_THETA = 500000.0
ROT_DIM = 16
GLA_HEADS = 4
GLA_DK = 64
GLA_DV = 128
GLA_RANK = 16
GLA_TAU = 16.0
GLA_CHUNK = 64
D_IN_PROJ = 2832
D_IN_SHARD = D_IN_PROJ // N_DEV
D_OUT_SHARD = D_MODEL // N_DEV
OFF = (0, 512, 640, 768, 1280, 1536, 1792, 2304, 2816, 2832)
EPS = 1e-5
ALPHA = 2.0 ** 0.25
SWA_SCALE = SWA_HEAD_DIM ** -0.5
GLA_SCALE = GLA_DK ** -0.5
ADAM_LR = 0.001
ADAM_B1 = 0.9
ADAM_B2 = 0.999
ADAM_EPS = 1e-08
ADAM_WD = 0.01
ADAM_STEP = 10
VMEM_LIMIT = 56 * 1024 * 1024

_NT = (((1,), (1,)), ((), ()))
_TN = (((0,), (0,)), ((), ()))


def _mm(a, b):
    return jnp.dot(a, b, preferred_element_type=F32)


def _mm_nt(a, b):
    return lax.dot_general(a, b, _NT, preferred_element_type=F32)


def _mm_tn(a, b):
    return lax.dot_general(a, b, _TN, preferred_element_type=F32)


def _sigmoid(t):
    return 1.0 / (1.0 + jnp.exp(-t))


def _cparams(**kw):
    return pltpu.CompilerParams(vmem_limit_bytes=VMEM_LIMIT, **kw)


def _full(shape):
    return pl.BlockSpec(shape, lambda *_: (0,) * len(shape))


def _rows(tile, width):
    return pl.BlockSpec((tile, width), lambda i: (i, 0))


def _rope_tables(positions):
    half = ROT_DIM // 2
    inv_freq = ROPE_THETA ** (-jnp.arange(half, dtype=F32) / half)
    lane = jnp.arange(128, dtype=jnp.int32) % SWA_HEAD_DIM
    ang = positions.astype(F32)[:, None] * inv_freq[None, :]
    cos, sin = lax.optimization_barrier((jnp.cos(ang), jnp.sin(ang)))
    cos, sin = jnp.tile(cos, (1, 128 // half)), jnp.tile(sin, (1, 128 // half))
    c = jnp.where(lane < ROT_DIM, cos, 1.0)
    s1 = jnp.where(lane < half, -sin, 0.0)
    s2 = jnp.where((lane >= half) & (lane < ROT_DIM), sin, 0.0)
    return c, s1, s2


def _rope(t, c, s1, s2):
    return t * c + pltpu.roll(t, 120, 1) * s1 + pltpu.roll(t, 8, 1) * s2


def _rope_t(g, c, s1, s2):
    return g * c + pltpu.roll(g * s1, 8, 1) + pltpu.roll(g * s2, 120, 1)


def _in_proj(x, w_in, wg, b_gate, rope):
    s = x.shape[0]
    ts = min(512, s)
    widths = [OFF[i + 1] - OFF[i] for i in range(9)]

    def body(x_ref, w_ref, wg_ref, bg_ref, c_ref, s1_ref, s2_ref,
             qa_ref, ka_ref, va_ref, ga_ref, qb_ref, kb_ref, vb_ref, gb_ref, rb_ref, la_ref, oms_ref):
        xb = x_ref[...].astype(MXU_DTYPE)
        c, s1, s2 = c_ref[...], s1_ref[...], s2_ref[...]
        i0 = pl.program_id(0)

        @pl.when(i0 == 0)
        def _():
            ka_ref[0:BLOCK, :] = jnp.zeros((BLOCK, 128), ka_ref.dtype)
            va_ref[0:BLOCK, :] = jnp.zeros((BLOCK, 128), va_ref.dtype)

        kv_rows = pl.ds(pl.multiple_of(BLOCK + i0 * ts, BLOCK), ts)

        def cols(i):
            return _mm(xb, w_ref[:, OFF[i]:OFF[i + 1]])

        qa = cols(0)
        for i in range(4):
            qa_ref[:, 128 * i:128 * (i + 1)] = _rope(qa[:, 128 * i:128 * (i + 1)], c, s1, s2).astype(qa_ref.dtype)
        kv = _mm(xb, w_ref[:, OFF[1]:OFF[3]])
        ka_ref[kv_rows, :] = _rope(kv[:, 0:128], c, s1, s2).astype(ka_ref.dtype)
        va_ref[kv_rows, :] = kv[:, 128:256].astype(va_ref.dtype)
        ga_ref[...] = cols(3)
        qb_ref[...] = cols(4)
        kb_ref[...] = cols(5)
        vb_ref[...] = cols(6).astype(vb_ref.dtype)
        gb_ref[...] = cols(7)
        rb = cols(8)
        rb_ref[...] = rb
        logit = _mm(rb.astype(MXU_DTYPE), wg_ref[...]) + bg_ref[...]
        e = jnp.exp(-jnp.abs(logit))
        la_ref[...] = (jnp.minimum(logit, 0.0) - jnp.log(1.0 + e)) / GLA_TAU
        oms_ref[...] = jnp.where(logit >= 0.0, e, 1.0) / (1.0 + e)

    out_shape = [jax.ShapeDtypeStruct((s + BLOCK if i in (1, 2) else s, w), MXU_DTYPE if i in (0, 1, 2, 6) else F32)
                 for i, w in enumerate(widths)]
    out_shape += [jax.ShapeDtypeStruct((s, 256), F32)] * 2
    return pl.pallas_call(
        body, name="in_proj", grid=(s // ts,),
        in_specs=[_rows(ts, D_MODEL), _full((D_MODEL, D_IN_PROJ)), _full((GLA_RANK, 256)), _full((1, 256)),
                  _rows(ts, 128), _rows(ts, 128), _rows(ts, 128)],
        out_specs=[_full((s + BLOCK, w)) if i in (1, 2) else _rows(ts, w) for i, w in enumerate(widths)]
        + [_rows(ts, 256)] * 2,
        out_shape=out_shape,
        compiler_params=_cparams(dimension_semantics=("arbitrary",)),
    )(x, w_in, wg, b_gate, *rope)


SWA_ROWS = SWA_GROUP * BLOCK


def _swa_bias():
    qi = lax.broadcasted_iota(jnp.int32, (2, SWA_ROWS, 2 * BLOCK), 1) & (BLOCK - 1)
    ki = lax.broadcasted_iota(jnp.int32, (2, SWA_ROWS, 2 * BLOCK), 2)
    first = lax.broadcasted_iota(jnp.int32, (2, SWA_ROWS, 2 * BLOCK), 0) == 0
    dist = qi + BLOCK - ki
    ok = (dist >= 0) & (dist < BLOCK) & (jnp.logical_not(first) | (ki >= BLOCK))
    return jnp.where(ok, 0.0, -jnp.inf).astype(F32)


SWA_SUB = 2


def _swa_bias_of(bias_ref, n, b):
    return bias_ref[jnp.minimum(n, 1)] if b == 0 else bias_ref[1]


def _swa_dup(t, j):
    t = t.astype(F32)
    low = lax.broadcasted_iota(jnp.int32, t.shape, 1) < SWA_HEAD_DIM
    keep = low if j == 0 else jnp.logical_not(low)
    return jnp.where(keep, t, pltpu.roll(t, SWA_HEAD_DIM, 1)).astype(MXU_DTYPE)


def _swa_stack(t, j):
    low = lax.broadcasted_iota(jnp.int32, (BLOCK, 128), 1) < SWA_HEAD_DIM
    zero = jnp.zeros((BLOCK, 128), t.dtype)
    blocks = []
    for p in (2 * j, 2 * j + 1):
        tp = t[:, 128 * p:128 * (p + 1)]
        blocks += [jnp.where(low, tp, zero), jnp.where(low, zero, tp)]
    return jnp.concatenate(blocks, axis=0)


def _swa_unstack(t):
    low = lax.broadcasted_iota(jnp.int32, (BLOCK, 128), 1) < SWA_HEAD_DIM
    return [jnp.where(low, t[2 * BLOCK * i:2 * BLOCK * i + BLOCK], t[2 * BLOCK * i + BLOCK:2 * BLOCK * (i + 1)])
            for i in range(2)]


def _swa_sink_col(sink_ref, j):
    row = lax.broadcasted_iota(jnp.int32, (SWA_ROWS, 1), 0)
    col = jnp.full((SWA_ROWS, 1), sink_ref[SWA_GROUP * j], F32)
    for r in range(1, SWA_GROUP):
        col = jnp.where(row >= BLOCK * r, sink_ref[SWA_GROUP * j + r], col)
    return col


def _swa_probs(qs, kd, bias, sink):
    sc = _mm_nt(qs, kd) + bias
    m = jnp.maximum(jnp.max(sc, axis=1, keepdims=True), sink)
    p = jnp.exp(sc - m)
    ps = jnp.exp(sink - m)
    rinv = 1.0 / (jnp.sum(p, axis=1, keepdims=True) + ps)
    return p * rinv, ps * rinv


def _swa_fwd(sinks, qa, k_pad, v_pad, ga):
    s = qa.shape[0]
    tq = SWA_SUB * BLOCK

    def body(sink_ref, qa_ref, ga_ref, bias_ref, k_ref, v_ref, attn_ref, cat_ref):
        n = pl.program_id(0)
        for b in range(SWA_SUB):
            rows = slice(BLOCK * b, BLOCK * (b + 1))
            start = pl.multiple_of((n * SWA_SUB + b) * BLOCK, BLOCK)
            kw = k_ref[pl.ds(start, 2 * BLOCK), :]
            vw = v_ref[pl.ds(start, 2 * BLOCK), :]
            bias = _swa_bias_of(bias_ref, n, b)
            q = qa_ref[rows, :] * SWA_SCALE
            g = ga_ref[rows, :]
            silu = g * _sigmoid(g)
            for j in range(SWA_KV_HEADS):
                qs = _swa_stack(q, j).astype(MXU_DTYPE)
                probs, _ = _swa_probs(qs, _swa_dup(kw, j), bias, _swa_sink_col(sink_ref, j))
                pairs = _swa_unstack(_mm(probs.astype(MXU_DTYPE), _swa_dup(vw, j)))
                for i in range(2):
                    lanes = slice(128 * (2 * j + i), 128 * (2 * j + i + 1))
                    attn_ref[rows, lanes] = pairs[i]
                    cat_ref[rows, lanes] = (pairs[i] * silu[:, lanes]).astype(cat_ref.dtype)

    return pl.pallas_call(
        body, name="swa_fwd", grid=(s // tq,),
        in_specs=[pl.BlockSpec(memory_space=pltpu.SMEM), _rows(tq, 512), _rows(tq, 512),
                  _full((2, SWA_ROWS, 2 * BLOCK)), _full((s + BLOCK, 128)), _full((s + BLOCK, 128))],
        out_specs=[_rows(tq, 512), _rows(tq, 512)],
        out_shape=[jax.ShapeDtypeStruct((s, 512), F32), jax.ShapeDtypeStruct((s, 512), MXU_DTYPE)],
        compiler_params=_cparams(dimension_semantics=("arbitrary",)),
    )(sinks, qa, ga, _swa_bias(), k_pad, v_pad)


GLA_KW = GLA_HEADS * GLA_DK
GLA_VW = GLA_HEADS * GLA_DV


def _idiv(t, d):
    return t >> (d.bit_length() - 1)


def _chunk_cumsum(t, lower):
    n, w = t.shape
    r = lax.broadcasted_iota(jnp.int32, (n, n), 0)
    c = lax.broadcasted_iota(jnp.int32, (n, n), 1)
    tri = ((_idiv(r, GLA_CHUNK) == _idiv(c, GLA_CHUNK)) & ((r >= c) if lower else (r <= c))).astype(MXU_DTYPE)
    hi = t.astype(MXU_DTYPE)
    r1 = t - hi.astype(F32)
    mid = r1.astype(MXU_DTYPE)
    lo = (r1 - mid.astype(F32)).astype(MXU_DTYPE)
    parts = _mm(tri, jnp.concatenate([hi, mid, lo], axis=1))
    return (parts[:, :w] + parts[:, w:2 * w]) + parts[:, 2 * w:]


def _chunk_last(t):
    n = t.shape[0]
    return jnp.concatenate(
        [jnp.broadcast_to(t[c + GLA_CHUNK - 1:c + GLA_CHUNK, :], (GLA_CHUNK, t.shape[1]))
         for c in range(0, n, GLA_CHUNK)], axis=0)


def _head_stack(t, width):
    head = _idiv(lax.broadcasted_iota(jnp.int32, t.shape, 1), width)
    zero = jnp.zeros_like(t)
    return jnp.concatenate([jnp.where(head == h, t, zero) for h in range(GLA_HEADS)], axis=0)


def _gla_masks():
    row = lax.broadcasted_iota(jnp.int32, (GLA_CHUNK, GLA_KW), 0)
    pos = lax.broadcasted_iota(jnp.int32, (GLA_CHUNK, GLA_KW), 1) & (GLA_CHUNK - 1)
    srow = _idiv(lax.broadcasted_iota(jnp.int32, (GLA_VW, GLA_KW), 0), GLA_DV)
    slane = _idiv(lax.broadcasted_iota(jnp.int32, (GLA_VW, GLA_KW), 1), GLA_DK)
    return pos <= row, pos >= row, srow == slane


def _gla_fwd(qb, kb, vb, la, gb, norm_w):
    s = qb.shape[0]
    tb = min(256, s)
    ch = tb // GLA_CHUNK

    def body(qb_ref, kb_ref, vb_ref, la_ref, gb_ref, nw_ref, o_ref, cat_ref, sp_ref, st_ref):
        @pl.when(pl.program_id(0) == 0)
        def _():
            st_ref[...] = jnp.zeros_like(st_ref)

        causal, _, same_head = _gla_masks()
        nw = nw_ref[...]
        b = _chunk_cumsum(la_ref[...], True)
        bl = _chunk_last(b)
        k = kb_ref[...]
        qd = ((qb_ref[...] * GLA_SCALE) * jnp.exp(b)).astype(MXU_DTYPE)
        ki = (k * jnp.exp(-b)).astype(MXU_DTYPE)
        ke = (k * jnp.exp(bl - b)).astype(MXU_DTYPE)
        dec = jnp.exp(bl)
        v = vb_ref[...].astype(MXU_DTYPE)
        g = gb_ref[...]
        silu = g * _sigmoid(g)
        for ci in range(ch):
            rows = slice(GLA_CHUNK * ci, GLA_CHUNK * (ci + 1))
            a = jnp.where(causal, _mm_nt(qd[rows], _head_stack(ki[rows], GLA_DK)), 0.0).astype(MXU_DTYPE)
            st = st_ref[...]
            sp_ref[ci] = (st[0:GLA_DV] + st[GLA_DV:2 * GLA_DV]) + (st[2 * GLA_DV:3 * GLA_DV] + st[3 * GLA_DV:])
            o = _mm(a, _head_stack(v[rows], GLA_DV)) + _mm_nt(qd[rows], st.astype(MXU_DTYPE))
            st_ref[...] = st * dec[rows][0:1] + jnp.where(same_head, _mm_tn(v[rows], ke[rows]), 0.0)
            o_ref[rows, :] = o
            for h in range(GLA_HEADS):
                lv = slice(GLA_DV * h, GLA_DV * (h + 1))
                oh = o[:, lv]
                r = lax.rsqrt(jnp.mean(oh * oh, axis=1, keepdims=True) + EPS)
                cat_ref[rows, lv] = (oh * r * nw * silu[rows, lv]).astype(cat_ref.dtype)

    return pl.pallas_call(
        body, name="gla_fwd", grid=(s // tb,),
        in_specs=[_rows(tb, 256), _rows(tb, 256), _rows(tb, 512), _rows(tb, 256), _rows(tb, 512), _full((1, 128))],
        out_specs=[_rows(tb, 512), _rows(tb, 512), pl.BlockSpec((ch, GLA_DV, 256), lambda i: (i, 0, 0))],
        out_shape=[jax.ShapeDtypeStruct((s, 512), F32), jax.ShapeDtypeStruct((s, 512), MXU_DTYPE),
                   jax.ShapeDtypeStruct((s // GLA_CHUNK, GLA_DV, 256), F32)],
        scratch_shapes=[pltpu.VMEM((GLA_VW, GLA_KW), F32)],
        compiler_params=_cparams(dimension_semantics=("arbitrary",)),
    )(qb, kb, vb, la, gb, norm_w)


def _out_ln_loss(cat_a, cat_b, w_out, x, target, ln_g, ln_b):
    s = x.shape[0]
    ts = min(512, s)
    halves = 2 if ts % 32 == 0 else 1
    th = ts // halves

    def body(ca_ref, cb_ref, w_ref, x_ref, t_ref, g_ref, b_ref,
             loss_ref, gx_ref, da_ref, db_ref, gw_ref, gln_ref):
        @pl.when(pl.program_id(0) == 0)
        def _():
            loss_ref[...] = jnp.zeros_like(loss_ref)
            gw_ref[...] = jnp.zeros_like(gw_ref)
            gln_ref[...] = jnp.zeros_like(gln_ref)

        g = g_ref[...]
        dh16s = []
        for k in range(halves):
            rows = slice(th * k, th * (k + 1))
            mix = _mm(ca_ref[rows, :], w_ref[0:512, :]) + _mm(cb_ref[rows, :], w_ref[512:1024, :])
            h = ALPHA * x_ref[rows, :] + mix
            mu = jnp.mean(h, axis=1, keepdims=True)
            hc = h - mu
            rstd = lax.rsqrt(jnp.mean(hc * hc, axis=1, keepdims=True) + EPS)
            xhat = hc * rstd
            err = xhat * g + b_ref[...] - t_ref[rows, :]
            loss_ref[...] += 0.5 * jnp.sum(jnp.mean(err * err, axis=1, keepdims=True))
            dy = err * (1.0 / D_MODEL)
            gln_ref[0:1, :] += jnp.sum(dy * xhat, axis=0, keepdims=True)
            gln_ref[1:2, :] += jnp.sum(dy, axis=0, keepdims=True)
            dxh = dy * g
            dh = rstd * (dxh - jnp.mean(dxh, axis=1, keepdims=True)
                         - xhat * jnp.mean(dxh * xhat, axis=1, keepdims=True))
            gx_ref[rows, :] = ALPHA * dh
            dh16s.append(dh.astype(MXU_DTYPE))
        for k in range(halves):
            rows = slice(th * k, th * (k + 1))
            da_ref[rows, :] = _mm_nt(dh16s[k], w_ref[0:512, :])
            db_ref[rows, :] = _mm_nt(dh16s[k], w_ref[512:1024, :])
        dh16 = jnp.concatenate(dh16s, axis=0)
        gw_ref[0:512, :] += _mm_tn(ca_ref[...], dh16)
        gw_ref[512:1024, :] += _mm_tn(cb_ref[...], dh16)

    return pl.pallas_call(
        body, name="out_ln_loss", grid=(s // ts,),
        in_specs=[_rows(ts, 512), _rows(ts, 512), _full((D_MODEL, D_MODEL)), _rows(ts, D_MODEL), _rows(ts, D_MODEL),
                  _full((1, D_MODEL)), _full((1, D_MODEL))],
        out_specs=[_full((1, 128)), _rows(ts, D_MODEL), _rows(ts, 512), _rows(ts, 512),
                   _full((D_MODEL, D_MODEL)), _full((2, D_MODEL))],
        out_shape=[jax.ShapeDtypeStruct((1, 128), F32), jax.ShapeDtypeStruct((s, D_MODEL), F32),
                   jax.ShapeDtypeStruct((s, 512), F32), jax.ShapeDtypeStruct((s, 512), F32),
                   jax.ShapeDtypeStruct((D_MODEL, D_MODEL), F32), jax.ShapeDtypeStruct((2, D_MODEL), F32)],
        compiler_params=_cparams(dimension_semantics=("arbitrary",)),
    )(cat_a, cat_b, w_out, x, target, ln_g, ln_b)


def _swa_bwd(sinks, qa, k_pad, v_pad, attn, ga, d_cat_a, rope):
    s = qa.shape[0]
    tq = SWA_SUB * BLOCK

    def body(sink_ref, qa_ref, ga_ref, at_ref, dc_ref, c_ref, s1_ref, s2_ref, bias_ref, k_ref, v_ref,
             dq_ref, dg_ref, dk_ref, dv_ref, ds_ref):
        n = pl.program_id(0)

        @pl.when(n == 0)
        def _():
            dk_ref[...] = jnp.zeros_like(dk_ref)
            dv_ref[...] = jnp.zeros_like(dv_ref)
            ds_ref[...] = jnp.zeros_like(ds_ref)

        low = lax.broadcasted_iota(jnp.int32, (2 * BLOCK, 128), 1) < SWA_HEAD_DIM
        for b in range(SWA_SUB):
            rows = slice(BLOCK * b, BLOCK * (b + 1))
            start = pl.multiple_of((n * SWA_SUB + b) * BLOCK, BLOCK)
            kw = k_ref[pl.ds(start, 2 * BLOCK), :]
            vw = v_ref[pl.ds(start, 2 * BLOCK), :]
            bias = _swa_bias_of(bias_ref, n, b)
            q = qa_ref[rows, :] * SWA_SCALE
            g = ga_ref[rows, :]
            sg = _sigmoid(g)
            o = at_ref[rows, :]
            dc = dc_ref[rows, :]
            do = dc * (g * sg)
            dg_ref[rows, :] = (dc * o * (sg * (1.0 + g * (1.0 - sg)))).astype(dg_ref.dtype)
            od = do * o
            c, s1, s2 = c_ref[rows, :], s1_ref[rows, :], s2_ref[rows, :]
            dk, dv = [], []
            for j in range(SWA_KV_HEADS):
                kd, vd = _swa_dup(kw, j), _swa_dup(vw, j)
                qs = _swa_stack(q, j).astype(MXU_DTYPE)
                dos = _swa_stack(do, j).astype(MXU_DTYPE)
                probs, psink = _swa_probs(qs, kd, bias, _swa_sink_col(sink_ref, j))
                delta = jnp.sum(_swa_stack(od, j), axis=1, keepdims=True)
                dsc = (probs * (_mm_nt(dos, vd) - delta)).astype(MXU_DTYPE)
                dsink = psink * delta
                for r in range(SWA_GROUP):
                    h = SWA_GROUP * j + r
                    ds_ref[h:h + 1, :] += jnp.zeros((1, 128), F32) - jnp.sum(dsink[BLOCK * r:BLOCK * (r + 1)])
                dq = _swa_unstack(_mm(dsc, kd))
                for i in range(2):
                    lanes = slice(128 * (2 * j + i), 128 * (2 * j + i + 1))
                    dq_ref[rows, lanes] = _rope_t(dq[i] * SWA_SCALE, c, s1, s2).astype(dq_ref.dtype)
                dkj = _mm_tn(dsc, qs)
                dvj = _mm_tn(probs.astype(MXU_DTYPE), dos)
                dk.append(dkj + pltpu.roll(dkj, SWA_HEAD_DIM, 1))
                dv.append(dvj + pltpu.roll(dvj, SWA_HEAD_DIM, 1))
            dk_ref[pl.ds(start, 2 * BLOCK), :] += jnp.where(low, dk[0], dk[1])
            dv_ref[pl.ds(start, 2 * BLOCK), :] += jnp.where(low, dv[0], dv[1])

    return pl.pallas_call(
        body, name="swa_bwd", grid=(s // tq,),
        in_specs=[pl.BlockSpec(memory_space=pltpu.SMEM)] + [_rows(tq, 512)] * 4 + [_rows(tq, 128)] * 3
        + [_full((2, SWA_ROWS, 2 * BLOCK))] + [_full((s + BLOCK, 128))] * 2,
        out_specs=[_rows(tq, 512), _rows(tq, 512), _full((s + BLOCK, 128)), _full((s + BLOCK, 128)),
                   _full((SWA_Q_HEADS, 128))],
        out_shape=[jax.ShapeDtypeStruct((s, 512), MXU_DTYPE), jax.ShapeDtypeStruct((s, 512), MXU_DTYPE),
                   jax.ShapeDtypeStruct((s + BLOCK, 128), F32), jax.ShapeDtypeStruct((s + BLOCK, 128), F32),
                   jax.ShapeDtypeStruct((SWA_Q_HEADS, 128), F32)],
        compiler_params=_cparams(dimension_semantics=("arbitrary",)),
    )(sinks, qa, ga, attn, d_cat_a, *rope, _swa_bias(), k_pad, v_pad)


def _gla_bwd(qb, kb, vb, la, oms, gb, o, sprev, d_cat_b, rb, wg, norm_w):
    s = qb.shape[0]
    tb = min(512, s)
    ch = tb // GLA_CHUNK
    nb = s // tb

    def body(qb_ref, kb_ref, vb_ref, la_ref, oms_ref, gb_ref, o_ref, sp_ref, dc_ref, rb_ref, wg_ref, nw_ref,
             dq_ref, dk_ref, dv_ref, dg_ref, dr_ref, gwg_ref, gbg_ref, gnw_ref, dst_ref):
        @pl.when(pl.program_id(0) == 0)
        def _():
            dst_ref[...] = jnp.zeros_like(dst_ref)
            gwg_ref[...] = jnp.zeros_like(gwg_ref)
            gbg_ref[...] = jnp.zeros_like(gbg_ref)
            gnw_ref[...] = jnp.zeros_like(gnw_ref)

        causal, causal_t, same_head = _gla_masks()
        nw = nw_ref[...]
        b = _chunk_cumsum(la_ref[...], True)
        bl = _chunk_last(b)
        eb, enb, ee, dec = jnp.exp(b), jnp.exp(-b), jnp.exp(bl - b), jnp.exp(bl)
        k = kb_ref[...]
        qd = (qb_ref[...] * GLA_SCALE) * eb
        ki = k * enb
        ke = k * ee
        qd16, ki16, ke16 = qd.astype(MXU_DTYPE), ki.astype(MXU_DTYPE), ke.astype(MXU_DTYPE)
        v16 = vb_ref[...].astype(MXU_DTYPE)

        g = gb_ref[...]
        sg = _sigmoid(g)
        silu = g * sg
        dsilu = sg * (1.0 + g * (1.0 - sg))
        gnw = jnp.zeros((1, GLA_DV), F32)
        do = []
        for h in range(GLA_HEADS):
            lv = slice(GLA_DV * h, GLA_DV * (h + 1))
            oh = o_ref[:, lv]
            dch = dc_ref[:, lv]
            r = lax.rsqrt(jnp.mean(oh * oh, axis=1, keepdims=True) + EPS)
            d_on = dch * silu[:, lv]
            dg_ref[:, lv] = (dch * (oh * r * nw) * dsilu[:, lv]).astype(dg_ref.dtype)
            gnw += jnp.sum(d_on * oh * r, axis=0, keepdims=True)
            u = d_on * nw
            do.append(r * u - oh * (r * r * r) * jnp.mean(u * oh, axis=1, keepdims=True))
        gnw_ref[...] += gnw
        do16 = jnp.concatenate(do, axis=1).astype(MXU_DTYPE)

        db, dbl = [None] * ch, [None] * ch
        for ci in reversed(range(ch)):
            rows = slice(GLA_CHUNK * ci, GLA_CHUNK * (ci + 1))
            qds, kis = _head_stack(qd16[rows], GLA_DK), _head_stack(ki16[rows], GLA_DK)
            vs, dos = _head_stack(v16[rows], GLA_DV), _head_stack(do16[rows], GLA_DV)
            a = jnp.where(causal, _mm_nt(qd16[rows], kis), 0.0).astype(MXU_DTYPE)
            at = jnp.where(causal_t, _mm_nt(ki16[rows], qds), 0.0).astype(MXU_DTYPE)
            da = jnp.where(causal, _mm_nt(do16[rows], vs), 0.0).astype(MXU_DTYPE)
            dat = jnp.where(causal_t, _mm_nt(v16[rows], dos), 0.0).astype(MXU_DTYPE)
            stc = sp_ref[ci]
            st = jnp.where(same_head, jnp.concatenate([stc] * GLA_HEADS, axis=0), 0.0)
            st16 = st.astype(MXU_DTYPE)
            dst = dst_ref[...]
            dst16 = dst.astype(MXU_DTYPE)
            dv = _mm(at, dos) + _mm_nt(ke16[rows], dst16)
            dqd = _mm(da, kis) + _mm(do16[rows], st16)
            dki = _mm(dat, qds)
            dke = _mm(v16[rows], dst16)
            ddec = jnp.sum(dst * st, axis=0, keepdims=True)
            decc = dec[rows][0:1]
            dst_ref[...] = jnp.where(same_head, _mm_tn(do16[rows], qd16[rows]), 0.0) + dst * decc
            dq_ref[rows, :] = (dqd * eb[rows] * GLA_SCALE).astype(dq_ref.dtype)
            dk_ref[rows, :] = (dki * enb[rows] + dke * ee[rows]).astype(dk_ref.dtype)
            dv_ref[rows, :] = dv.astype(dv_ref.dtype)
            dke_ke = dke * ke[rows]
            db[ci] = dqd * qd[rows] - dki * ki[rows] - dke_ke
            dbl[ci] = jnp.broadcast_to(jnp.sum(dke_ke, axis=0, keepdims=True) + ddec * decc, (GLA_CHUNK, GLA_KW))

        dla = _chunk_cumsum(jnp.concatenate(db, axis=0), False) + jnp.concatenate(dbl, axis=0)
        dlogit = dla * oms_ref[...] * (1.0 / GLA_TAU)
        dl16 = dlogit.astype(MXU_DTYPE)
        gbg_ref[...] += jnp.sum(dlogit, axis=0, keepdims=True)
        gwg_ref[...] += _mm_tn(rb_ref[...].astype(MXU_DTYPE), dl16)
        dr_ref[...] = _mm_nt(dl16, wg_ref[...]).astype(dr_ref.dtype)

    def rev(width):
        return pl.BlockSpec((tb, width), lambda i: (nb - 1 - i, 0))

    return pl.pallas_call(
        body, name="gla_bwd", grid=(nb,),
        in_specs=[rev(256), rev(256), rev(512), rev(256), rev(256), rev(512), rev(512),
                  pl.BlockSpec((ch, GLA_DV, 256), lambda i: (nb - 1 - i, 0, 0)), rev(512), rev(GLA_RANK),
                  _full((GLA_RANK, 256)), _full((1, 128))],
        out_specs=[rev(256), rev(256), rev(512), rev(512), rev(GLA_RANK),
                   _full((GLA_RANK, 256)), _full((1, 256)), _full((1, 128))],
        out_shape=[jax.ShapeDtypeStruct((s, 256), MXU_DTYPE), jax.ShapeDtypeStruct((s, 256), MXU_DTYPE),
                   jax.ShapeDtypeStruct((s, 512), MXU_DTYPE), jax.ShapeDtypeStruct((s, 512), MXU_DTYPE),
                   jax.ShapeDtypeStruct((s, GLA_RANK), MXU_DTYPE), jax.ShapeDtypeStruct((GLA_RANK, 256), F32),
                   jax.ShapeDtypeStruct((1, 256), F32), jax.ShapeDtypeStruct((1, 128), F32)],
        scratch_shapes=[pltpu.VMEM((GLA_VW, GLA_KW), F32)],
        compiler_params=_cparams(dimension_semantics=("arbitrary",)),
    )(qb, kb, vb, la, oms, gb, o, sprev, d_cat_b, rb, wg, norm_w)


def _in_proj_bwd_x(gx0, pieces, w_in, rope):
    s = gx0.shape[0]
    ts = min(512, s)
    widths = [OFF[i + 1] - OFF[i] for i in range(9)]

    def body(gx0_ref, *refs):
        piece_refs = refs[:9]
        w_ref, c_ref, s1_ref, s2_ref, gx_ref, dp_ref = refs[9:]
        kv_rows = pl.ds(pl.multiple_of(BLOCK + pl.program_id(0) * ts, BLOCK), ts)
        acc = gx0_ref[...]
        for i in (0, 1, 3, 4, 5, 6, 7, 8):
            lo, hi = OFF[i], OFF[i + 1]
            if i == 1:
                dk = _rope_t(piece_refs[1][kv_rows, :], c_ref[...], s1_ref[...], s2_ref[...])
                t16 = jnp.concatenate([dk, piece_refs[2][kv_rows, :]], axis=1).astype(MXU_DTYPE)
                hi = OFF[3]
            else:
                t16 = piece_refs[i][...].astype(MXU_DTYPE)
            dp_ref[:, lo:hi] = t16
            acc += _mm_nt(t16, w_ref[:, lo:hi])
        gx_ref[...] = acc

    return pl.pallas_call(
        body, name="in_proj_bwd_x", grid=(s // ts,),
        in_specs=[_rows(ts, D_MODEL)]
        + [_full((s + BLOCK, w)) if i in (1, 2) else _rows(ts, w) for i, w in enumerate(widths)]
        + [_full((D_MODEL, D_IN_PROJ))] + [_rows(ts, 128)] * 3,
        out_specs=[_rows(ts, D_MODEL), _rows(ts, D_IN_PROJ)],
        out_shape=[jax.ShapeDtypeStruct((s, D_MODEL), F32), jax.ShapeDtypeStruct((s, D_IN_PROJ), MXU_DTYPE)],
        compiler_params=_cparams(dimension_semantics=("arbitrary",)),
    )(gx0, *pieces, w_in, *rope)


def _in_proj_bwd_w(x, dproj):
    s = x.shape[0]
    ts = min(1024, s)
    nsteps = s // ts
    col_chunks = [(OFF[i], OFF[i + 2] if i == 1 else OFF[i + 1]) for i in (0, 1, 3, 4, 5, 6, 7, 8)]

    def body(x_ref, dp_ref, gw_ref, acc_ref, stage_ref, sems):
        i = pl.program_id(0)

        @pl.when(i == 0)
        def _():
            acc_ref[...] = jnp.zeros_like(acc_ref)

        xb = x_ref[...].astype(MXU_DTYPE)
        for lo, hi in col_chunks:
            acc_ref[:, lo:hi] += _mm_tn(xb, dp_ref[:, lo:hi])

        @pl.when(i == nsteps - 1)
        def _():
            copies = []
            for j in range(N_DEV):
                slot = j % 2
                if j >= 2:
                    copies[j - 2].wait()
                stage_ref[slot] = acc_ref[:, D_IN_SHARD * j:D_IN_SHARD * (j + 1)]
                cp = pltpu.make_async_copy(stage_ref.at[slot], gw_ref.at[j], sems.at[slot])
                cp.start()
                copies.append(cp)
            copies[N_DEV - 2].wait()
            copies[N_DEV - 1].wait()

    return pl.pallas_call(
        body, name="in_proj_bwd_w", grid=(nsteps,),
        in_specs=[_rows(ts, D_MODEL), _rows(ts, D_IN_PROJ)],
        out_specs=pl.BlockSpec(memory_space=pl.ANY),
        out_shape=jax.ShapeDtypeStruct((N_DEV, D_MODEL, D_IN_SHARD), F32),
        scratch_shapes=[pltpu.VMEM((D_MODEL, D_IN_PROJ), F32), pltpu.VMEM((2, D_MODEL, D_IN_SHARD), F32),
                        pltpu.SemaphoreType.DMA((2,))],
        compiler_params=_cparams(dimension_semantics=("arbitrary",)),
    )(x, dproj)


def _local_step(x, positions, w_in, wg, b_gate, sinks, norm_w, w_out, ln_g, ln_b, target):
    rope = _rope_tables(positions)
    qa, k_pad, v_pad, ga, qb, kb, vb, gb, rb, la, oms = _in_proj(x, w_in, wg, b_gate, rope)
    attn, cat_a = _swa_fwd(sinks, qa, k_pad, v_pad, ga)
    o, cat_b, sprev = _gla_fwd(qb, kb, vb, la, gb, norm_w)
    loss, gx0, d_cat_a, d_cat_b, g_w_out, g_ln = _out_ln_loss(cat_a, cat_b, w_out, x, target, ln_g, ln_b)
    dqa, dga, dk_pad, dv_pad, g_sinks = _swa_bwd(sinks, qa, k_pad, v_pad, attn, ga, d_cat_a, rope)
    dqb, dkb, dvb, dgb, drb, g_wg, g_bg, g_nw = _gla_bwd(qb, kb, vb, la, oms, gb, o, sprev, d_cat_b, rb, wg, norm_w)
    pieces = (dqa, dk_pad, dv_pad, dga, dqb, dkb, dvb, dgb, drb)
    grad_x, dproj = _in_proj_bwd_x(gx0, pieces, w_in, rope)
    g_w_in = _in_proj_bwd_w(x, dproj)
    return loss, grad_x, g_w_in, g_wg, g_bg, g_sinks, g_nw, g_w_out, g_ln


def _mesh_pos():
    return lax.axis_index("x"), lax.axis_index("y"), lax.axis_index("c")


def _peer(k, x, y, c):
    px = (1 - x) if k & 4 else x
    py = (1 - y) if k & 2 else y
    pc = (1 - c) if k & 1 else c
    return (px, py, pc), 4 * px + 2 * py + pc


def _other_chips(x, y):
    return [(1 - x, y), (x, 1 - y), (1 - x, 1 - y)]


def _all_gather_weights(w_in_s, w_out_s, wg_s):
    n_arr = 3

    def body(win_ref, wout_ref, wg_ref, win_full, wout_full, wg_full, win_all, wout_all, wg_all, send_sems, recv_sems):
        x, y, c = _mesh_pos()
        me = 4 * x + 2 * y + c
        sibling = (x, y, 1 - c)
        chips = _other_chips(x, y)
        alls = (win_all, wout_all, wg_all)

        def copy(k, a, block, to):
            return pltpu.make_async_remote_copy(
                src_ref=alls[a].at[block], dst_ref=alls[a].at[block], send_sem=send_sems.at[n_arr * k + a],
                recv_sem=recv_sems.at[n_arr * k + a], device_id=to, device_id_type=pl.DeviceIdType.MESH)

        win_all[me] = win_ref[...].astype(win_all.dtype)
        wout_all[me] = wout_ref[...].astype(wout_all.dtype)
        wg_all[me] = wg_ref[...].astype(wg_all.dtype)
        sends = []
        for a in range(n_arr):
            for j, (cx, cy) in enumerate(chips):
                sends.append(copy(1 + j, a, me, (cx, cy, c)))
            sends.append(copy(0, a, me, sibling))
        for cp in sends:
            cp.start()
        for j, (cx, cy) in enumerate(chips):
            for a in range(n_arr):
                block = 4 * cx + 2 * cy + c
                copy(1 + j, a, block, sibling).wait_recv()
                fwd = copy(4 + j, a, block, sibling)
                fwd.start()
                sends.append(fwd)
        for a in range(n_arr):
            copy(0, a, 4 * x + 2 * y + (1 - c), sibling).wait_recv()
        for j, (cx, cy) in enumerate(chips):
            for a in range(n_arr):
                copy(4 + j, a, 4 * cx + 2 * cy + (1 - c), sibling).wait_recv()
        for cp in sends:
            cp.wait_send()
        for j in range(N_DEV):
            win_full[:, D_IN_SHARD * j:D_IN_SHARD * (j + 1)] = win_all[j]
            wout_full[D_OUT_SHARD * j:D_OUT_SHARD * (j + 1), :] = wout_all[j]
            wg_full[:, 32 * j:32 * (j + 1)] = wg_all[j]

    vmem = pl.BlockSpec(memory_space=pltpu.VMEM)
    n_copies = n_arr * (N_DEV - 1)
    return pl.pallas_call(
        body, name="all_gather_weights",
        in_specs=[vmem] * 3, out_specs=[vmem] * 3,
        out_shape=[jax.ShapeDtypeStruct((D_MODEL, D_IN_PROJ), MXU_DTYPE),
                   jax.ShapeDtypeStruct((D_MODEL, D_MODEL), MXU_DTYPE),
                   jax.ShapeDtypeStruct((GLA_RANK, 256), MXU_DTYPE)],
        scratch_shapes=[pltpu.VMEM((N_DEV, D_MODEL, D_IN_SHARD), MXU_DTYPE),
                        pltpu.VMEM((N_DEV, D_OUT_SHARD, D_MODEL), MXU_DTYPE),
                        pltpu.VMEM((N_DEV, GLA_RANK, 32), MXU_DTYPE),
                        pltpu.SemaphoreType.DMA((n_copies,)), pltpu.SemaphoreType.DMA((n_copies,))],
        compiler_params=_cparams(),
    )(w_in_s, w_out_s, wg_s)


def _reduce_grads(parts_w_in, parts_w_out, parts_wg, small):
    def body(pin_ref, pout_ref, pwg_ref, sm_ref, gin_ref, gout_ref, rwg_ref, rsm_ref,
             own_in, sib_in, snd_in, rcv_in, own_out, sib_out, snd_out, rcv_out,
             loc_sems, d2d_send, d2d_recv, ici_send, ici_recv, sm_send, sm_recv, sm_loc):
        x, y, c = _mesh_pos()
        me = 4 * x + 2 * y + c
        sibling = (x, y, 1 - c)
        chips = [(x, y)] + _other_chips(x, y)
        parts = (pin_ref, pout_ref)
        own, sib, snd, rcv, outs = (own_in, own_out), (sib_in, sib_out), (snd_in, snd_out), (rcv_in, rcv_out), (gin_ref, gout_ref)

        small_dsts = (rwg_ref, rsm_ref)

        def small_src(a, block):
            return pwg_ref.at[block] if a == 0 else sm_ref

        small_local = [pltpu.make_async_copy(small_src(a, me), small_dsts[a].at[me], sm_loc.at[a]) for a in range(2)]
        for cp in small_local:
            cp.start()
        small_sends = []
        for k in range(1, N_DEV):
            peer, pidx = _peer(k, x, y, c)
            for a in range(2):
                i = 2 * (k - 1) + a
                cp = pltpu.make_async_remote_copy(
                    src_ref=small_src(a, pidx), dst_ref=small_dsts[a].at[me], send_sem=sm_send.at[i],
                    recv_sem=sm_recv.at[i], device_id=peer, device_id_type=pl.DeviceIdType.MESH)
                cp.start()
                small_sends.append(cp)

        local, d2d = {}, {}
        for r in (1, 2, 3, 0):
            cx, cy = chips[r]
            for a in range(2):
                i = 2 * r + a
                local[r, a] = pltpu.make_async_copy(parts[a].at[4 * cx + 2 * cy + c], own[a].at[r], loc_sems.at[i])
                local[r, a].start()
                d2d[r, a] = pltpu.make_async_remote_copy(
                    src_ref=parts[a].at[4 * cx + 2 * cy + (1 - c)], dst_ref=sib[a].at[r], send_sem=d2d_send.at[i],
                    recv_sem=d2d_recv.at[i], device_id=sibling, device_id_type=pl.DeviceIdType.MESH)
                d2d[r, a].start()
        ici = {}
        for r in (1, 2, 3):
            cx, cy = chips[r]
            for a in range(2):
                i = 2 * (r - 1) + a
                local[r, a].wait()
                d2d[r, a].wait_recv()
                snd[a][r - 1] = (own[a][r] + sib[a][r]).astype(snd[a].dtype)
                ici[r, a] = pltpu.make_async_remote_copy(
                    src_ref=snd[a].at[r - 1], dst_ref=rcv[a].at[r - 1], send_sem=ici_send.at[i],
                    recv_sem=ici_recv.at[i], device_id=(cx, cy, c), device_id_type=pl.DeviceIdType.MESH)
                ici[r, a].start()
        for a in range(2):
            local[0, a].wait()
            d2d[0, a].wait_recv()
            acc = own[a][0] + sib[a][0]
            for r in (1, 2, 3):
                ici[r, a].wait_recv()
                acc = acc + rcv[a][r - 1].astype(F32)
            outs[a][...] = acc

        for k in range(1, N_DEV):
            peer, pidx = _peer(k, x, y, c)
            for a in range(2):
                i = 2 * (k - 1) + a
                pltpu.make_async_remote_copy(
                    src_ref=small_src(a, me), dst_ref=small_dsts[a].at[pidx], send_sem=sm_send.at[i],
                    recv_sem=sm_recv.at[i], device_id=peer, device_id_type=pl.DeviceIdType.MESH).wait_recv()
        for cp in small_sends + list(d2d.values()) + list(ici.values()):
            cp.wait_send()
        for cp in small_local:
            cp.wait()

    hbm = pl.BlockSpec(memory_space=pl.ANY)
    vmem = pl.BlockSpec(memory_space=pltpu.VMEM)
    in_blk, out_blk = parts_w_in.shape[1:], parts_w_out.shape[1:]
    return pl.pallas_call(
        body, name="reduce_grads",
        in_specs=[hbm] * 4, out_specs=[vmem, vmem, hbm, hbm],
        out_shape=[jax.ShapeDtypeStruct(in_blk, F32), jax.ShapeDtypeStruct(out_blk, F32),
                   jax.ShapeDtypeStruct((N_DEV,) + parts_wg.shape[1:], F32),
                   jax.ShapeDtypeStruct((N_DEV,) + small.shape, F32)],
        scratch_shapes=[pltpu.VMEM((4,) + in_blk, F32), pltpu.VMEM((4,) + in_blk, F32),
                        pltpu.VMEM((3,) + in_blk, MXU_DTYPE), pltpu.VMEM((3,) + in_blk, MXU_DTYPE),
                        pltpu.VMEM((4,) + out_blk, F32), pltpu.VMEM((4,) + out_blk, F32),
                        pltpu.VMEM((3,) + out_blk, MXU_DTYPE), pltpu.VMEM((3,) + out_blk, MXU_DTYPE),
                        pltpu.SemaphoreType.DMA((8,)), pltpu.SemaphoreType.DMA((8,)), pltpu.SemaphoreType.DMA((8,)),
                        pltpu.SemaphoreType.DMA((6,)), pltpu.SemaphoreType.DMA((6,)),
                        pltpu.SemaphoreType.DMA((2 * (N_DEV - 1),)), pltpu.SemaphoreType.DMA((2 * (N_DEV - 1),)),
                        pltpu.SemaphoreType.DMA((2,))],
        compiler_params=_cparams(),
    )(parts_w_in, parts_w_out, parts_wg, small)


def _adamw(recv, w, m, v, name):
    rows, width = w.shape
    tr = 128 if rows % 128 == 0 else rows
    n_parts = recv.shape[0]

    def body(r_ref, w_ref, m_ref, v_ref, g_ref, d_ref, nm_ref, nv_ref):
        g = r_ref[0]
        for j in range(1, n_parts):
            g = g + r_ref[j]
        nm = ADAM_B1 * m_ref[...] + (1.0 - ADAM_B1) * g
        nv = ADAM_B2 * v_ref[...] + (1.0 - ADAM_B2) * (g * g)
        m_hat = nm / (1.0 - ADAM_B1 ** ADAM_STEP)
        v_hat = nv / (1.0 - ADAM_B2 ** ADAM_STEP)
        g_ref[...] = g
        d_ref[...] = -ADAM_LR * (m_hat / (jnp.sqrt(v_hat) + ADAM_EPS) + ADAM_WD * w_ref[...])
        nm_ref[...] = nm
        nv_ref[...] = nv

    spec = _rows(tr, width)
    return pl.pallas_call(
        body, name=name, grid=(rows // tr,),
        in_specs=[pl.BlockSpec((n_parts, tr, width), lambda i: (0, i, 0)), spec, spec, spec],
        out_specs=[spec] * 4,
        out_shape=[jax.ShapeDtypeStruct((rows, width), F32)] * 4,
        compiler_params=_cparams(dimension_semantics=("arbitrary",)),
    )(recv, w, m, v)


def _pack_small(ln_g, ln_b, b_gate, norm_w, sinks, loss=None):
    def rows8(t):
        t = t.reshape(-1)
        t = jnp.pad(t, (0, 1024 - t.shape[0]))
        return t.reshape(8, 128)

    loss = jnp.zeros((1,), F32) if loss is None else loss
    return jnp.concatenate([rows8(t) for t in (ln_g, ln_b, b_gate, norm_w, sinks, loss)], axis=0)


def _unpack_small(p):
    flat = [p[8 * i:8 * (i + 1)].reshape(1, 1024) for i in range(5)]
    return flat[0], flat[1], flat[2][:, :256], flat[3][:, :128], flat[4][:, :8]


def kernel(x, positions, w_in, gla_w_gate_up, gla_b_gate, attn_sinks, gla_norm_w, w_out, ln_g, ln_b, loss_target, m_w_in, m_gla_w_gate_up, m_gla_b_gate, m_attn_sinks, m_gla_norm_w, m_w_out, m_ln_g, m_ln_b, v_w_in, v_gla_w_gate_up, v_gla_b_gate, v_attn_sinks, v_gla_norm_w, v_w_out, v_ln_g, v_ln_b):
    w_in_full, w_out_full, wg_full = _all_gather_weights(w_in[0], w_out[0], gla_w_gate_up[0])

    loss, grad_x, parts_w_in, g_wg, g_bg, g_sinks, g_nw, g_w_out, g_ln = _local_step(
        x[0], positions[0], w_in_full, wg_full, gla_b_gate, attn_sinks[0], gla_norm_w, w_out_full, ln_g, ln_b,
        loss_target[0])

    parts_w_out = g_w_out.reshape(N_DEV, D_OUT_SHARD, D_MODEL)
    parts_wg = jnp.transpose(g_wg.reshape(GLA_RANK, N_DEV, 32), (1, 0, 2))
    small = _pack_small(g_ln[0:1], g_ln[1:2], g_bg, g_nw, g_sinks[:, 0].reshape(1, SWA_Q_HEADS), loss[0, 0:1])
    g_in, g_out, r_wg, r_small = _reduce_grads(parts_w_in, parts_w_out, parts_wg, small)

    upd_in = _adamw(g_in[None], w_in[0], m_w_in[0], v_w_in[0], "adamw_w_in")
    upd_out = _adamw(g_out[None], w_out[0], m_w_out[0], v_w_out[0], "adamw_w_out")
    upd_wg = _adamw(r_wg, gla_w_gate_up[0], m_gla_w_gate_up[0], v_gla_w_gate_up[0], "adamw_wg")
    upd_small = _adamw(
        r_small,
        _pack_small(ln_g, ln_b, gla_b_gate, gla_norm_w, attn_sinks),
        _pack_small(m_ln_g, m_ln_b, m_gla_b_gate, m_gla_norm_w, m_attn_sinks),
        _pack_small(v_ln_g, v_ln_b, v_gla_b_gate, v_gla_norm_w, v_attn_sinks), "adamw_small")

    total = jnp.sum(r_small[:, 40, 0])
    outs = [total, grad_x[None]]
    for kind in range(4):
        s_ln_g, s_ln_b, s_bg, s_nw, s_sinks = _unpack_small(upd_small[kind])
        outs += [upd_in[kind][None], upd_wg[kind][None], s_bg, s_sinks, s_nw, upd_out[kind][None], s_ln_g, s_ln_b]
    return tuple(outs)
```

```python
import functools

import jax
import jax.numpy as jnp
from jax import lax
from jax.experimental import pallas as pl
from jax.experimental.pallas import tpu as pltpu

F32 = jnp.float32
MXU_DTYPE = jnp.bfloat16

N_DEV = 8
D_MODEL = 1024
SWA_Q_HEADS = 8
SWA_KV_HEADS = 2
SWA_GROUP = 4
SWA_HEAD_DIM = 64
BLOCK = 128
ROPE_THETA = 500000.0
ROT_DIM = 16
GLA_HEADS = 4
GLA_DK = 64
GLA_DV = 128
GLA_RANK = 16
GLA_TAU = 16.0
GLA_CHUNK = 64
D_IN_PROJ = 2832
D_IN_SHARD = D_IN_PROJ // N_DEV
D_OUT_SHARD = D_MODEL // N_DEV
OFF = (0, 512, 640, 768, 1280, 1536, 1792, 2304, 2816, 2832)
EPS = 1e-5
ALPHA = 2.0 ** 0.25
SWA_SCALE = SWA_HEAD_DIM ** -0.5
GLA_SCALE = GLA_DK ** -0.5
ADAM_LR = 0.001
ADAM_B1 = 0.9
ADAM_B2 = 0.999
ADAM_EPS = 1e-08
ADAM_WD = 0.01
ADAM_STEP = 10
VMEM_LIMIT = 56 * 1024 * 1024

_NT = (((1,), (1,)), ((), ()))
_TN = (((0,), (0,)), ((), ()))


def _mm(a, b):
    return jnp.dot(a, b, preferred_element_type=F32)


def _mm_nt(a, b):
    return lax.dot_general(a, b, _NT, preferred_element_type=F32)


def _mm_tn(a, b):
    return lax.dot_general(a, b, _TN, preferred_element_type=F32)


def _sigmoid(t):
    return 1.0 / (1.0 + jnp.exp(-t))


def _cparams(**kw):
    return pltpu.CompilerParams(vmem_limit_bytes=VMEM_LIMIT, **kw)


def _full(shape):
    return pl.BlockSpec(shape, lambda *_: (0,) * len(shape))


def _rows(tile, width):
    return pl.BlockSpec((tile, width), lambda i: (i, 0))


def _rope_tables(positions):
    half = ROT_DIM // 2
    inv_freq = ROPE_THETA ** (-jnp.arange(half, dtype=F32) / half)
    lane = jnp.arange(128, dtype=jnp.int32) % SWA_HEAD_DIM
    ang = positions.astype(F32)[:, None] * inv_freq[None, :]
    cos, sin = lax.optimization_barrier((jnp.cos(ang), jnp.sin(ang)))
    cos, sin = jnp.tile(cos, (1, 128 // half)), jnp.tile(sin, (1, 128 // half))
    c = jnp.where(lane < ROT_DIM, cos, 1.0)
    s1 = jnp.where(lane < half, -sin, 0.0)
    s2 = jnp.where((lane >= half) & (lane < ROT_DIM), sin, 0.0)
    return c, s1, s2


def _rope(t, c, s1, s2):
    return t * c + pltpu.roll(t, 120, 1) * s1 + pltpu.roll(t, 8, 1) * s2


def _rope_t(g, c, s1, s2):
    return g * c + pltpu.roll(g * s1, 8, 1) + pltpu.roll(g * s2, 120, 1)


def _in_proj(x, w_in, wg, b_gate, rope):
    s = x.shape[0]
    ts = min(512, s)
    widths = [OFF[i + 1] - OFF[i] for i in range(9)]

    def body(x_ref, w_ref, wg_ref, bg_ref, c_ref, s1_ref, s2_ref,
             qa_ref, ka_ref, va_ref, ga_ref, qb_ref, kb_ref, vb_ref, gb_ref, rb_ref, la_ref, oms_ref):
        xb = x_ref[...].astype(MXU_DTYPE)
        c, s1, s2 = c_ref[...], s1_ref[...], s2_ref[...]
        i0 = pl.program_id(0)

        @pl.when(i0 == 0)
        def _():
            ka_ref[0:BLOCK, :] = jnp.zeros((BLOCK, 128), ka_ref.dtype)
            va_ref[0:BLOCK, :] = jnp.zeros((BLOCK, 128), va_ref.dtype)

        kv_rows = pl.ds(pl.multiple_of(BLOCK + i0 * ts, BLOCK), ts)

        def cols(i):
            return _mm_nt(xb, w_ref[OFF[i]:OFF[i + 1], :])

        qa = cols(0)
        for i in range(4):
            qa_ref[:, 128 * i:128 * (i + 1)] = _rope(qa[:, 128 * i:128 * (i + 1)], c, s1, s2).astype(qa_ref.dtype)
        kv = _mm_nt(xb, w_ref[OFF[1]:OFF[3], :])
        ka_ref[kv_rows, :] = _rope(kv[:, 0:128], c, s1, s2).astype(ka_ref.dtype)
        va_ref[kv_rows, :] = kv[:, 128:256].astype(va_ref.dtype)
        ga_ref[...] = cols(3)
        qb_ref[...] = cols(4)
        kb_ref[...] = cols(5)
        vb_ref[...] = cols(6).astype(vb_ref.dtype)
        gb_ref[...] = cols(7)
        rb = cols(8)
        rb_ref[...] = rb
        logit = _mm(rb.astype(MXU_DTYPE), wg_ref[...]) + bg_ref[...]
        e = jnp.exp(-jnp.abs(logit))
        la_ref[...] = (jnp.minimum(logit, 0.0) - jnp.log(1.0 + e)) / GLA_TAU
        oms_ref[...] = jnp.where(logit >= 0.0, e, 1.0) / (1.0 + e)

    out_shape = [jax.ShapeDtypeStruct((s + BLOCK if i in (1, 2) else s, w), MXU_DTYPE if i in (0, 1, 2, 6) else F32)
                 for i, w in enumerate(widths)]
    out_shape += [jax.ShapeDtypeStruct((s, 256), F32)] * 2
    return pl.pallas_call(
        body, name="in_proj", grid=(s // ts,),
        in_specs=[_rows(ts, D_MODEL), _full((D_IN_PROJ, D_MODEL)), _full((GLA_RANK, 256)), _full((1, 256)),
                  _rows(ts, 128), _rows(ts, 128), _rows(ts, 128)],
        out_specs=[_full((s + BLOCK, w)) if i in (1, 2) else _rows(ts, w) for i, w in enumerate(widths)]
        + [_rows(ts, 256)] * 2,
        out_shape=out_shape,
        compiler_params=_cparams(dimension_semantics=("arbitrary",)),
    )(x, w_in, wg, b_gate, *rope)


SWA_ROWS = SWA_GROUP * BLOCK


def _swa_bias():
    qi = lax.broadcasted_iota(jnp.int32, (2, SWA_ROWS, 2 * BLOCK), 1) & (BLOCK - 1)
    ki = lax.broadcasted_iota(jnp.int32, (2, SWA_ROWS, 2 * BLOCK), 2)
    first = lax.broadcasted_iota(jnp.int32, (2, SWA_ROWS, 2 * BLOCK), 0) == 0
    dist = qi + BLOCK - ki
    ok = (dist >= 0) & (dist < BLOCK) & (jnp.logical_not(first) | (ki >= BLOCK))
    return jnp.where(ok, 0.0, -jnp.inf).astype(F32)


SWA_SUB = 2


def _swa_bias_of(bias_ref, n, b):
    return bias_ref[jnp.minimum(n, 1)] if b == 0 else bias_ref[1]


def _swa_dup(t, j):
    t = t.astype(F32)
    low = lax.broadcasted_iota(jnp.int32, t.shape, 1) < SWA_HEAD_DIM
    keep = low if j == 0 else jnp.logical_not(low)
    return jnp.where(keep, t, pltpu.roll(t, SWA_HEAD_DIM, 1)).astype(MXU_DTYPE)


def _swa_stack(t, j):
    low = lax.broadcasted_iota(jnp.int32, (BLOCK, 128), 1) < SWA_HEAD_DIM
    zero = jnp.zeros((BLOCK, 128), t.dtype)
    blocks = []
    for p in (2 * j, 2 * j + 1):
        tp = t[:, 128 * p:128 * (p + 1)]
        blocks += [jnp.where(low, tp, zero), jnp.where(low, zero, tp)]
    return jnp.concatenate(blocks, axis=0)


def _swa_unstack(t):
    low = lax.broadcasted_iota(jnp.int32, (BLOCK, 128), 1) < SWA_HEAD_DIM
    return [jnp.where(low, t[2 * BLOCK * i:2 * BLOCK * i + BLOCK], t[2 * BLOCK * i + BLOCK:2 * BLOCK * (i + 1)])
            for i in range(2)]


def _swa_sink_col(sink_ref, j):
    row = lax.broadcasted_iota(jnp.int32, (SWA_ROWS, 1), 0)
    col = jnp.full((SWA_ROWS, 1), sink_ref[SWA_GROUP * j], F32)
    for r in range(1, SWA_GROUP):
        col = jnp.where(row >= BLOCK * r, sink_ref[SWA_GROUP * j + r], col)
    return col


def _swa_probs(qs, kd, bias, sink):
    sc = _mm_nt(qs, kd) + bias
    m = jnp.maximum(jnp.max(sc, axis=1, keepdims=True), sink)
    p = jnp.exp(sc - m)
    ps = jnp.exp(sink - m)
    rinv = 1.0 / (jnp.sum(p, axis=1, keepdims=True) + ps)
    return p * rinv, ps * rinv


def _swa_fwd(sinks, qa, k_pad, v_pad, ga):
    s = qa.shape[0]
    tq = SWA_SUB * BLOCK

    def body(sink_ref, qa_ref, ga_ref, bias_ref, k_ref, v_ref, attn_ref, cat_ref):
        n = pl.program_id(0)
        for b in range(SWA_SUB):
            rows = slice(BLOCK * b, BLOCK * (b + 1))
            start = pl.multiple_of((n * SWA_SUB + b) * BLOCK, BLOCK)
            kw = k_ref[pl.ds(start, 2 * BLOCK), :]
            vw = v_ref[pl.ds(start, 2 * BLOCK), :]
            bias = _swa_bias_of(bias_ref, n, b)
            q = qa_ref[rows, :] * SWA_SCALE
            g = ga_ref[rows, :]
            silu = g * _sigmoid(g)
            for j in range(SWA_KV_HEADS):
                qs = _swa_stack(q, j).astype(MXU_DTYPE)
                probs, _ = _swa_probs(qs, _swa_dup(kw, j), bias, _swa_sink_col(sink_ref, j))
                pairs = _swa_unstack(_mm(probs.astype(MXU_DTYPE), _swa_dup(vw, j)))
                for i in range(2):
                    lanes = slice(128 * (2 * j + i), 128 * (2 * j + i + 1))
                    attn_ref[rows, lanes] = pairs[i]
                    cat_ref[rows, lanes] = (pairs[i] * silu[:, lanes]).astype(cat_ref.dtype)

    return pl.pallas_call(
        body, name="swa_fwd", grid=(s // tq,),
        in_specs=[pl.BlockSpec(memory_space=pltpu.SMEM), _rows(tq, 512), _rows(tq, 512),
                  _full((2, SWA_ROWS, 2 * BLOCK)), _full((s + BLOCK, 128)), _full((s + BLOCK, 128))],
        out_specs=[_rows(tq, 512), _rows(tq, 512)],
        out_shape=[jax.ShapeDtypeStruct((s, 512), F32), jax.ShapeDtypeStruct((s, 512), MXU_DTYPE)],
        compiler_params=_cparams(dimension_semantics=("arbitrary",)),
    )(sinks, qa, ga, _swa_bias(), k_pad, v_pad)


GLA_KW = GLA_HEADS * GLA_DK
GLA_VW = GLA_HEADS * GLA_DV


def _idiv(t, d):
    return t >> (d.bit_length() - 1)


def _chunk_cumsum(t, lower):
    n, w = t.shape
    r = lax.broadcasted_iota(jnp.int32, (n, n), 0)
    c = lax.broadcasted_iota(jnp.int32, (n, n), 1)
    tri = ((_idiv(r, GLA_CHUNK) == _idiv(c, GLA_CHUNK)) & ((r >= c) if lower else (r <= c))).astype(MXU_DTYPE)
    hi = t.astype(MXU_DTYPE)
    r1 = t - hi.astype(F32)
    mid = r1.astype(MXU_DTYPE)
    lo = (r1 - mid.astype(F32)).astype(MXU_DTYPE)
    parts = _mm(tri, jnp.concatenate([hi, mid, lo], axis=1))
    return (parts[:, :w] + parts[:, w:2 * w]) + parts[:, 2 * w:]


def _chunk_last(t):
    n = t.shape[0]
    return jnp.concatenate(
        [jnp.broadcast_to(t[c + GLA_CHUNK - 1:c + GLA_CHUNK, :], (GLA_CHUNK, t.shape[1]))
         for c in range(0, n, GLA_CHUNK)], axis=0)


def _head_stack(t, width):
    head = _idiv(lax.broadcasted_iota(jnp.int32, t.shape, 1), width)
    zero = jnp.zeros_like(t)
    return jnp.concatenate([jnp.where(head == h, t, zero) for h in range(GLA_HEADS)], axis=0)


def _gla_masks():
    row = lax.broadcasted_iota(jnp.int32, (GLA_CHUNK, GLA_KW), 0)
    pos = lax.broadcasted_iota(jnp.int32, (GLA_CHUNK, GLA_KW), 1) & (GLA_CHUNK - 1)
    srow = _idiv(lax.broadcasted_iota(jnp.int32, (GLA_VW, GLA_KW), 0), GLA_DV)
    slane = _idiv(lax.broadcasted_iota(jnp.int32, (GLA_VW, GLA_KW), 1), GLA_DK)
    return pos <= row, pos >= row, srow == slane


def _gla_fwd(qb, kb, vb, la, gb, norm_w):
    s = qb.shape[0]
    tb = min(256, s)
    ch = tb // GLA_CHUNK

    def body(qb_ref, kb_ref, vb_ref, la_ref, gb_ref, nw_ref, o_ref, cat_ref, sp_ref, st_ref):
        @pl.when(pl.program_id(0) == 0)
        def _():
            st_ref[...] = jnp.zeros_like(st_ref)

        causal, _, same_head = _gla_masks()
        nw = nw_ref[...]
        b = _chunk_cumsum(la_ref[...], True)
        bl = _chunk_last(b)
        k = kb_ref[...]
        qd = ((qb_ref[...] * GLA_SCALE) * jnp.exp(b)).astype(MXU_DTYPE)
        ki = (k * jnp.exp(-b)).astype(MXU_DTYPE)
        ke = (k * jnp.exp(bl - b)).astype(MXU_DTYPE)
        dec = jnp.exp(bl)
        v = vb_ref[...].astype(MXU_DTYPE)
        g = gb_ref[...]
        silu = g * _sigmoid(g)
        for ci in range(ch):
            rows = slice(GLA_CHUNK * ci, GLA_CHUNK * (ci + 1))
            a = jnp.where(causal, _mm_nt(qd[rows], _head_stack(ki[rows], GLA_DK)), 0.0).astype(MXU_DTYPE)
            st = st_ref[...]
            sp_ref[ci] = (st[0:GLA_DV] + st[GLA_DV:2 * GLA_DV]) + (st[2 * GLA_DV:3 * GLA_DV] + st[3 * GLA_DV:])
            o = _mm(a, _head_stack(v[rows], GLA_DV)) + _mm_nt(qd[rows], st.astype(MXU_DTYPE))
            st_ref[...] = st * dec[rows][0:1] + jnp.where(same_head, _mm_tn(v[rows], ke[rows]), 0.0)
            o_ref[rows, :] = o
            for h in range(GLA_HEADS):
                lv = slice(GLA_DV * h, GLA_DV * (h + 1))
                oh = o[:, lv]
                r = lax.rsqrt(jnp.mean(oh * oh, axis=1, keepdims=True) + EPS)
                cat_ref[rows, lv] = (oh * r * nw * silu[rows, lv]).astype(cat_ref.dtype)

    return pl.pallas_call(
        body, name="gla_fwd", grid=(s // tb,),
        in_specs=[_rows(tb, 256), _rows(tb, 256), _rows(tb, 512), _rows(tb, 256), _rows(tb, 512), _full((1, 128))],
        out_specs=[_rows(tb, 512), _rows(tb, 512), pl.BlockSpec((ch, GLA_DV, 256), lambda i: (i, 0, 0))],
        out_shape=[jax.ShapeDtypeStruct((s, 512), F32), jax.ShapeDtypeStruct((s, 512), MXU_DTYPE),
                   jax.ShapeDtypeStruct((s // GLA_CHUNK, GLA_DV, 256), F32)],
        scratch_shapes=[pltpu.VMEM((GLA_VW, GLA_KW), F32)],
        compiler_params=_cparams(dimension_semantics=("arbitrary",)),
    )(qb, kb, vb, la, gb, norm_w)


def _out_ln_loss(cat_a, cat_b, w_out, x, target, ln_g, ln_b):
    s = x.shape[0]
    ts = min(512, s)
    halves = 2 if ts % 32 == 0 else 1
    th = ts // halves

    def body(ca_ref, cb_ref, w_ref, x_ref, t_ref, g_ref, b_ref,
             loss_ref, gx_ref, da_ref, db_ref, gw_ref, gln_ref):
        @pl.when(pl.program_id(0) == 0)
        def _():
            loss_ref[...] = jnp.zeros_like(loss_ref)
            gw_ref[...] = jnp.zeros_like(gw_ref)
            gln_ref[...] = jnp.zeros_like(gln_ref)

        g = g_ref[...]
        dh16s = []
        for k in range(halves):
            rows = slice(th * k, th * (k + 1))
            mix = _mm(ca_ref[rows, :], w_ref[0:512, :]) + _mm(cb_ref[rows, :], w_ref[512:1024, :])
            h = ALPHA * x_ref[rows, :] + mix
            mu = jnp.mean(h, axis=1, keepdims=True)
            hc = h - mu
            rstd = lax.rsqrt(jnp.mean(hc * hc, axis=1, keepdims=True) + EPS)
            xhat = hc * rstd
            err = xhat * g + b_ref[...] - t_ref[rows, :]
            loss_ref[...] += 0.5 * jnp.sum(jnp.mean(err * err, axis=1, keepdims=True))
            dy = err * (1.0 / D_MODEL)
            gln_ref[0:1, :] += jnp.sum(dy * xhat, axis=0, keepdims=True)
            gln_ref[1:2, :] += jnp.sum(dy, axis=0, keepdims=True)
            dxh = dy * g
            dh = rstd * (dxh - jnp.mean(dxh, axis=1, keepdims=True)
                         - xhat * jnp.mean(dxh * xhat, axis=1, keepdims=True))
            gx_ref[rows, :] = ALPHA * dh
            dh16s.append(dh.astype(MXU_DTYPE))
        for k in range(halves):
            rows = slice(th * k, th * (k + 1))
            da_ref[rows, :] = _mm_nt(dh16s[k], w_ref[0:512, :])
            db_ref[rows, :] = _mm_nt(dh16s[k], w_ref[512:1024, :])
        dh16 = jnp.concatenate(dh16s, axis=0)
        gw_ref[0:512, :] += _mm_tn(ca_ref[...], dh16)
        gw_ref[512:1024, :] += _mm_tn(cb_ref[...], dh16)

    return pl.pallas_call(
        body, name="out_ln_loss", grid=(s // ts,),
        in_specs=[_rows(ts, 512), _rows(ts, 512), _full((D_MODEL, D_MODEL)), _rows(ts, D_MODEL), _rows(ts, D_MODEL),
                  _full((1, D_MODEL)), _full((1, D_MODEL))],
        out_specs=[_full((1, 128)), _rows(ts, D_MODEL), _rows(ts, 512), _rows(ts, 512),
                   _full((D_MODEL, D_MODEL)), _full((2, D_MODEL))],
        out_shape=[jax.ShapeDtypeStruct((1, 128), F32), jax.ShapeDtypeStruct((s, D_MODEL), F32),
                   jax.ShapeDtypeStruct((s, 512), F32), jax.ShapeDtypeStruct((s, 512), F32),
                   jax.ShapeDtypeStruct((D_MODEL, D_MODEL), F32), jax.ShapeDtypeStruct((2, D_MODEL), F32)],
        compiler_params=_cparams(dimension_semantics=("arbitrary",)),
    )(cat_a, cat_b, w_out, x, target, ln_g, ln_b)


def _swa_bwd(sinks, qa, k_pad, v_pad, attn, ga, d_cat_a, rope):
    s = qa.shape[0]
    tq = SWA_SUB * BLOCK

    def body(sink_ref, qa_ref, ga_ref, at_ref, dc_ref, c_ref, s1_ref, s2_ref, bias_ref, k_ref, v_ref,
             dq_ref, dg_ref, dk_ref, dv_ref, ds_ref):
        n = pl.program_id(0)

        @pl.when(n == 0)
        def _():
            dk_ref[...] = jnp.zeros_like(dk_ref)
            dv_ref[...] = jnp.zeros_like(dv_ref)
            ds_ref[...] = jnp.zeros_like(ds_ref)

        low = lax.broadcasted_iota(jnp.int32, (2 * BLOCK, 128), 1) < SWA_HEAD_DIM
        for b in range(SWA_SUB):
            rows = slice(BLOCK * b, BLOCK * (b + 1))
            start = pl.multiple_of((n * SWA_SUB + b) * BLOCK, BLOCK)
            kw = k_ref[pl.ds(start, 2 * BLOCK), :]
            vw = v_ref[pl.ds(start, 2 * BLOCK), :]
            bias = _swa_bias_of(bias_ref, n, b)
            q = qa_ref[rows, :] * SWA_SCALE
            g = ga_ref[rows, :]
            sg = _sigmoid(g)
            o = at_ref[rows, :]
            dc = dc_ref[rows, :]
            do = dc * (g * sg)
            dg_ref[rows, :] = (dc * o * (sg * (1.0 + g * (1.0 - sg)))).astype(dg_ref.dtype)
            od = do * o
            c, s1, s2 = c_ref[rows, :], s1_ref[rows, :], s2_ref[rows, :]
            dk, dv = [], []
            for j in range(SWA_KV_HEADS):
                kd, vd = _swa_dup(kw, j), _swa_dup(vw, j)
                qs = _swa_stack(q, j).astype(MXU_DTYPE)
                dos = _swa_stack(do, j).astype(MXU_DTYPE)
                probs, psink = _swa_probs(qs, kd, bias, _swa_sink_col(sink_ref, j))
                delta = jnp.sum(_swa_stack(od, j), axis=1, keepdims=True)
                dsc = (probs * (_mm_nt(dos, vd) - delta)).astype(MXU_DTYPE)
                dsink = psink * delta
                for r in range(SWA_GROUP):
                    h = SWA_GROUP * j + r
                    ds_ref[h:h + 1, :] += jnp.zeros((1, 128), F32) - jnp.sum(dsink[BLOCK * r:BLOCK * (r + 1)])
                dq = _swa_unstack(_mm(dsc, kd))
                for i in range(2):
                    lanes = slice(128 * (2 * j + i), 128 * (2 * j + i + 1))
                    dq_ref[rows, lanes] = _rope_t(dq[i] * SWA_SCALE, c, s1, s2).astype(dq_ref.dtype)
                dkj = _mm_tn(dsc, qs)
                dvj = _mm_tn(probs.astype(MXU_DTYPE), dos)
                dk.append(dkj + pltpu.roll(dkj, SWA_HEAD_DIM, 1))
                dv.append(dvj + pltpu.roll(dvj, SWA_HEAD_DIM, 1))
            dk_ref[pl.ds(start, 2 * BLOCK), :] += jnp.where(low, dk[0], dk[1])
            dv_ref[pl.ds(start, 2 * BLOCK), :] += jnp.where(low, dv[0], dv[1])

    return pl.pallas_call(
        body, name="swa_bwd", grid=(s // tq,),
        in_specs=[pl.BlockSpec(memory_space=pltpu.SMEM)] + [_rows(tq, 512)] * 4 + [_rows(tq, 128)] * 3
        + [_full((2, SWA_ROWS, 2 * BLOCK))] + [_full((s + BLOCK, 128))] * 2,
        out_specs=[_rows(tq, 512), _rows(tq, 512), _full((s + BLOCK, 128)), _full((s + BLOCK, 128)),
                   _full((SWA_Q_HEADS, 128))],
        out_shape=[jax.ShapeDtypeStruct((s, 512), MXU_DTYPE), jax.ShapeDtypeStruct((s, 512), MXU_DTYPE),
                   jax.ShapeDtypeStruct((s + BLOCK, 128), F32), jax.ShapeDtypeStruct((s + BLOCK, 128), F32),
                   jax.ShapeDtypeStruct((SWA_Q_HEADS, 128), F32)],
        compiler_params=_cparams(dimension_semantics=("arbitrary",)),
    )(sinks, qa, ga, attn, d_cat_a, *rope, _swa_bias(), k_pad, v_pad)


def _gla_bwd(qb, kb, vb, la, oms, gb, o, sprev, d_cat_b, rb, wg, norm_w):
    s = qb.shape[0]
    tb = min(512, s)
    ch = tb // GLA_CHUNK
    nb = s // tb

    def body(qb_ref, kb_ref, vb_ref, la_ref, oms_ref, gb_ref, o_ref, sp_ref, dc_ref, rb_ref, wg_ref, nw_ref,
             dq_ref, dk_ref, dv_ref, dg_ref, dr_ref, gwg_ref, gbg_ref, gnw_ref, dst_ref):
        @pl.when(pl.program_id(0) == 0)
        def _():
            dst_ref[...] = jnp.zeros_like(dst_ref)
            gwg_ref[...] = jnp.zeros_like(gwg_ref)
            gbg_ref[...] = jnp.zeros_like(gbg_ref)
            gnw_ref[...] = jnp.zeros_like(gnw_ref)

        causal, causal_t, same_head = _gla_masks()
        nw = nw_ref[...]
        b = _chunk_cumsum(la_ref[...], True)
        bl = _chunk_last(b)
        eb, enb, ee, dec = jnp.exp(b), jnp.exp(-b), jnp.exp(bl - b), jnp.exp(bl)
        k = kb_ref[...]
        qd = (qb_ref[...] * GLA_SCALE) * eb
        ki = k * enb
        ke = k * ee
        qd16, ki16, ke16 = qd.astype(MXU_DTYPE), ki.astype(MXU_DTYPE), ke.astype(MXU_DTYPE)
        v16 = vb_ref[...].astype(MXU_DTYPE)

        g = gb_ref[...]
        sg = _sigmoid(g)
        silu = g * sg
        dsilu = sg * (1.0 + g * (1.0 - sg))
        gnw = jnp.zeros((1, GLA_DV), F32)
        do = []
        for h in range(GLA_HEADS):
            lv = slice(GLA_DV * h, GLA_DV * (h + 1))
            oh = o_ref[:, lv]
            dch = dc_ref[:, lv]
            r = lax.rsqrt(jnp.mean(oh * oh, axis=1, keepdims=True) + EPS)
            d_on = dch * silu[:, lv]
            dg_ref[:, lv] = (dch * (oh * r * nw) * dsilu[:, lv]).astype(dg_ref.dtype)
            gnw += jnp.sum(d_on * oh * r, axis=0, keepdims=True)
            u = d_on * nw
            do.append(r * u - oh * (r * r * r) * jnp.mean(u * oh, axis=1, keepdims=True))
        gnw_ref[...] += gnw
        do16 = jnp.concatenate(do, axis=1).astype(MXU_DTYPE)

        db, dbl = [None] * ch, [None] * ch
        for ci in reversed(range(ch)):
            rows = slice(GLA_CHUNK * ci, GLA_CHUNK * (ci + 1))
            qds, kis = _head_stack(qd16[rows], GLA_DK), _head_stack(ki16[rows], GLA_DK)
            vs, dos = _head_stack(v16[rows], GLA_DV), _head_stack(do16[rows], GLA_DV)
            a = jnp.where(causal, _mm_nt(qd16[rows], kis), 0.0).astype(MXU_DTYPE)
            at = jnp.where(causal_t, _mm_nt(ki16[rows], qds), 0.0).astype(MXU_DTYPE)
            da = jnp.where(causal, _mm_nt(do16[rows], vs), 0.0).astype(MXU_DTYPE)
            dat = jnp.where(causal_t, _mm_nt(v16[rows], dos), 0.0).astype(MXU_DTYPE)
            stc = sp_ref[ci]
            st = jnp.where(same_head, jnp.concatenate([stc] * GLA_HEADS, axis=0), 0.0)
            st16 = st.astype(MXU_DTYPE)
            dst = dst_ref[...]
            dst16 = dst.astype(MXU_DTYPE)
            dv = _mm(at, dos) + _mm_nt(ke16[rows], dst16)
            dqd = _mm(da, kis) + _mm(do16[rows], st16)
            dki = _mm(dat, qds)
            dke = _mm(v16[rows], dst16)
            ddec = jnp.sum(dst * st, axis=0, keepdims=True)
            decc = dec[rows][0:1]
            dst_ref[...] = jnp.where(same_head, _mm_tn(do16[rows], qd16[rows]), 0.0) + dst * decc
            dq_ref[rows, :] = (dqd * eb[rows] * GLA_SCALE).astype(dq_ref.dtype)
            dk_ref[rows, :] = (dki * enb[rows] + dke * ee[rows]).astype(dk_ref.dtype)
            dv_ref[rows, :] = dv.astype(dv_ref.dtype)
            dke_ke = dke * ke[rows]
            db[ci] = dqd * qd[rows] - dki * ki[rows] - dke_ke
            dbl[ci] = jnp.broadcast_to(jnp.sum(dke_ke, axis=0, keepdims=True) + ddec * decc, (GLA_CHUNK, GLA_KW))

        dla = _chunk_cumsum(jnp.concatenate(db, axis=0), False) + jnp.concatenate(dbl, axis=0)
        dlogit = dla * oms_ref[...] * (1.0 / GLA_TAU)
        dl16 = dlogit.astype(MXU_DTYPE)
        gbg_ref[...] += jnp.sum(dlogit, axis=0, keepdims=True)
        gwg_ref[...] += _mm_tn(rb_ref[...].astype(MXU_DTYPE), dl16)
        dr_ref[...] = _mm_nt(dl16, wg_ref[...]).astype(dr_ref.dtype)

    def rev(width):
        return pl.BlockSpec((tb, width), lambda i: (nb - 1 - i, 0))

    return pl.pallas_call(
        body, name="gla_bwd", grid=(nb,),
        in_specs=[rev(256), rev(256), rev(512), rev(256), rev(256), rev(512), rev(512),
                  pl.BlockSpec((ch, GLA_DV, 256), lambda i: (nb - 1 - i, 0, 0)), rev(512), rev(GLA_RANK),
                  _full((GLA_RANK, 256)), _full((1, 128))],
        out_specs=[rev(256), rev(256), rev(512), rev(512), rev(GLA_RANK),
                   _full((GLA_RANK, 256)), _full((1, 256)), _full((1, 128))],
        out_shape=[jax.ShapeDtypeStruct((s, 256), MXU_DTYPE), jax.ShapeDtypeStruct((s, 256), MXU_DTYPE),
                   jax.ShapeDtypeStruct((s, 512), MXU_DTYPE), jax.ShapeDtypeStruct((s, 512), MXU_DTYPE),
                   jax.ShapeDtypeStruct((s, GLA_RANK), MXU_DTYPE), jax.ShapeDtypeStruct((GLA_RANK, 256), F32),
                   jax.ShapeDtypeStruct((1, 256), F32), jax.ShapeDtypeStruct((1, 128), F32)],
        scratch_shapes=[pltpu.VMEM((GLA_VW, GLA_KW), F32)],
        compiler_params=_cparams(dimension_semantics=("arbitrary",)),
    )(qb, kb, vb, la, oms, gb, o, sprev, d_cat_b, rb, wg, norm_w)


def _in_proj_bwd_x(gx0, pieces, w_in, rope):
    s = gx0.shape[0]
    ts = min(512, s)
    widths = [OFF[i + 1] - OFF[i] for i in range(9)]

    def body(gx0_ref, *refs):
        piece_refs = refs[:9]
        w_ref, c_ref, s1_ref, s2_ref, gx_ref, dp_ref = refs[9:]
        kv_rows = pl.ds(pl.multiple_of(BLOCK + pl.program_id(0) * ts, BLOCK), ts)
        acc = gx0_ref[...]
        for i in (0, 1, 3, 4, 5, 6, 7, 8):
            lo, hi = OFF[i], OFF[i + 1]
            if i == 1:
                dk = _rope_t(piece_refs[1][kv_rows, :], c_ref[...], s1_ref[...], s2_ref[...])
                t16 = jnp.concatenate([dk, piece_refs[2][kv_rows, :]], axis=1).astype(MXU_DTYPE)
                hi = OFF[3]
            else:
                t16 = piece_refs[i][...].astype(MXU_DTYPE)
            dp_ref[:, lo:hi] = t16
            acc += _mm(t16, w_ref[lo:hi, :])
        gx_ref[...] = acc

    return pl.pallas_call(
        body, name="in_proj_bwd_x", grid=(s // ts,),
        in_specs=[_rows(ts, D_MODEL)]
        + [_full((s + BLOCK, w)) if i in (1, 2) else _rows(ts, w) for i, w in enumerate(widths)]
        + [_full((D_IN_PROJ, D_MODEL))] + [_rows(ts, 128)] * 3,
        out_specs=[_rows(ts, D_MODEL), _rows(ts, D_IN_PROJ)],
        out_shape=[jax.ShapeDtypeStruct((s, D_MODEL), F32), jax.ShapeDtypeStruct((s, D_IN_PROJ), MXU_DTYPE)],
        compiler_params=_cparams(dimension_semantics=("arbitrary",)),
    )(gx0, *pieces, w_in, *rope)


def _in_proj_bwd_w(x, dproj):
    s = x.shape[0]
    ts = min(1024, s)
    nsteps = s // ts
    col_chunks = [(OFF[i], OFF[i + 2] if i == 1 else OFF[i + 1]) for i in (0, 1, 3, 4, 5, 6, 7, 8)]

    def body(x_ref, dp_ref, gw_ref, acc_ref, stage_ref, sems):
        i = pl.program_id(0)

        @pl.when(i == 0)
        def _():
            acc_ref[...] = jnp.zeros_like(acc_ref)

        xb = x_ref[...].astype(MXU_DTYPE)
        for lo, hi in col_chunks:
            acc_ref[lo:hi, :] += _mm_tn(dp_ref[:, lo:hi], xb)

        @pl.when(i == nsteps - 1)
        def _():
            copies = []
            for j in range(N_DEV):
                slot = j % 2
                if j >= 2:
                    copies[j - 2].wait()
                stage_ref[slot] = acc_ref[D_IN_SHARD * j:D_IN_SHARD * (j + 1), :]
                cp = pltpu.make_async_copy(stage_ref.at[slot], gw_ref.at[j], sems.at[slot])
                cp.start()
                copies.append(cp)
            copies[N_DEV - 2].wait()
            copies[N_DEV - 1].wait()

    return pl.pallas_call(
        body, name="in_proj_bwd_w", grid=(nsteps,),
        in_specs=[_rows(ts, D_MODEL), _rows(ts, D_IN_PROJ)],
        out_specs=pl.BlockSpec(memory_space=pl.ANY),
        out_shape=jax.ShapeDtypeStruct((N_DEV, D_IN_SHARD, D_MODEL), F32),
        scratch_shapes=[pltpu.VMEM((D_IN_PROJ, D_MODEL), F32), pltpu.VMEM((2, D_IN_SHARD, D_MODEL), F32),
                        pltpu.SemaphoreType.DMA((2,))],
        compiler_params=_cparams(dimension_semantics=("arbitrary",)),
    )(x, dproj)


def _local_step(x, positions, w_in, wg, b_gate, sinks, norm_w, w_out, ln_g, ln_b, target):
    rope = _rope_tables(positions)
    qa, k_pad, v_pad, ga, qb, kb, vb, gb, rb, la, oms = _in_proj(x, w_in, wg, b_gate, rope)
    attn, cat_a = _swa_fwd(sinks, qa, k_pad, v_pad, ga)
    o, cat_b, sprev = _gla_fwd(qb, kb, vb, la, gb, norm_w)
    loss, gx0, d_cat_a, d_cat_b, g_w_out, g_ln = _out_ln_loss(cat_a, cat_b, w_out, x, target, ln_g, ln_b)
    dqa, dga, dk_pad, dv_pad, g_sinks = _swa_bwd(sinks, qa, k_pad, v_pad, attn, ga, d_cat_a, rope)
    dqb, dkb, dvb, dgb, drb, g_wg, g_bg, g_nw = _gla_bwd(qb, kb, vb, la, oms, gb, o, sprev, d_cat_b, rb, wg, norm_w)
    pieces = (dqa, dk_pad, dv_pad, dga, dqb, dkb, dvb, dgb, drb)
    grad_x, dproj = _in_proj_bwd_x(gx0, pieces, w_in, rope)
    g_w_in = _in_proj_bwd_w(x, dproj)
    return loss, grad_x, g_w_in, g_wg, g_bg, g_sinks, g_nw, g_w_out, g_ln


def _mesh_pos():
    return lax.axis_index("x"), lax.axis_index("y"), lax.axis_index("c")


def _peer(k, x, y, c):
    px = (1 - x) if k & 4 else x
    py = (1 - y) if k & 2 else y
    pc = (1 - c) if k & 1 else c
    return (px, py, pc), 4 * px + 2 * py + pc


def _other_chips(x, y):
    return [(1 - x, y), (x, 1 - y), (1 - x, 1 - y)]


def _shard_view(t):
    return jnp.transpose(t, (2, 0, 1))


def _all_gather_weights(w_in_t, w_out_s, wg_s):
    n_arr = 3

    def body(win_hbm, wout_ref, wg_ref, win_full, wout_full, wg_full, win_all, wout_all, wg_all, stage, send_sems,
             recv_sems, stage_sem):
        x, y, c = _mesh_pos()
        me = 4 * x + 2 * y + c
        sibling = (x, y, 1 - c)
        chips = _other_chips(x, y)
        alls = (win_all, wout_all, wg_all)

        def copy(k, a, block, to):
            return pltpu.make_async_remote_copy(
                src_ref=alls[a].at[block], dst_ref=alls[a].at[block], send_sem=send_sems.at[n_arr * k + a],
                recv_sem=recv_sems.at[n_arr * k + a], device_id=to, device_id_type=pl.DeviceIdType.MESH)

        load = pltpu.make_async_copy(win_hbm.at[:, 0, :], stage, stage_sem)
        load.start()
        load.wait()
        win_all[me] = stage[...].astype(win_all.dtype)
        wout_all[me] = wout_ref[...].astype(wout_all.dtype)
        wg_all[me] = wg_ref[...].astype(wg_all.dtype)
        sends = []
        for a in range(n_arr):
            for j, (cx, cy) in enumerate(chips):
                sends.append(copy(1 + j, a, me, (cx, cy, c)))
            sends.append(copy(0, a, me, sibling))
        for cp in sends:
            cp.start()
        for j, (cx, cy) in enumerate(chips):
            for a in range(n_arr):
                block = 4 * cx + 2 * cy + c
                copy(1 + j, a, block, sibling).wait_recv()
                fwd = copy(4 + j, a, block, sibling)
                fwd.start()
                sends.append(fwd)
        for a in range(n_arr):
            copy(0, a, 4 * x + 2 * y + (1 - c), sibling).wait_recv()
        for j, (cx, cy) in enumerate(chips):
            for a in range(n_arr):
                copy(4 + j, a, 4 * cx + 2 * cy + (1 - c), sibling).wait_recv()
        for cp in sends:
            cp.wait_send()
        for j in range(N_DEV):
            win_full[D_IN_SHARD * j:D_IN_SHARD * (j + 1), :] = win_all[j]
            wout_full[D_OUT_SHARD * j:D_OUT_SHARD * (j + 1), :] = wout_all[j]
            wg_full[:, 32 * j:32 * (j + 1)] = wg_all[j]

    vmem = pl.BlockSpec(memory_space=pltpu.VMEM)
    n_copies = n_arr * (N_DEV - 1)
    return pl.pallas_call(
        body, name="all_gather_weights",
        in_specs=[pl.BlockSpec(memory_space=pl.ANY), vmem, vmem], out_specs=[vmem] * 3,
        out_shape=[jax.ShapeDtypeStruct((D_IN_PROJ, D_MODEL), MXU_DTYPE),
                   jax.ShapeDtypeStruct((D_MODEL, D_MODEL), MXU_DTYPE),
                   jax.ShapeDtypeStruct((GLA_RANK, 256), MXU_DTYPE)],
        scratch_shapes=[pltpu.VMEM((N_DEV, D_IN_SHARD, D_MODEL), MXU_DTYPE),
                        pltpu.VMEM((N_DEV, D_OUT_SHARD, D_MODEL), MXU_DTYPE),
                        pltpu.VMEM((N_DEV, GLA_RANK, 32), MXU_DTYPE),
                        pltpu.VMEM((D_IN_SHARD, D_MODEL), F32),
                        pltpu.SemaphoreType.DMA((n_copies,)), pltpu.SemaphoreType.DMA((n_copies,)),
                        pltpu.SemaphoreType.DMA],
        compiler_params=_cparams(),
    )(w_in_t, w_out_s, wg_s)


def _reduce_grads(parts_w_in, parts_w_out, parts_wg, small):
    def body(pin_ref, pout_ref, pwg_ref, sm_ref, gin_ref, gout_ref, rwg_ref, rsm_ref,
             own_in, sib_in, snd_in, rcv_in, own_out, sib_out, snd_out, rcv_out,
             loc_sems, d2d_send, d2d_recv, ici_send, ici_recv, sm_send, sm_recv, sm_loc):
        x, y, c = _mesh_pos()
        me = 4 * x + 2 * y + c
        sibling = (x, y, 1 - c)
        chips = [(x, y)] + _other_chips(x, y)
        parts = (pin_ref, pout_ref)
        own, sib, snd, rcv, outs = (own_in, own_out), (sib_in, sib_out), (snd_in, snd_out), (rcv_in, rcv_out), (gin_ref, gout_ref)

        small_dsts = (rwg_ref, rsm_ref)

        def small_src(a, block):
            return pwg_ref.at[block] if a == 0 else sm_ref

        small_local = [pltpu.make_async_copy(small_src(a, me), small_dsts[a].at[me], sm_loc.at[a]) for a in range(2)]
        for cp in small_local:
            cp.start()
        small_sends = []
        for k in range(1, N_DEV):
            peer, pidx = _peer(k, x, y, c)
            for a in range(2):
                i = 2 * (k - 1) + a
                cp = pltpu.make_async_remote_copy(
                    src_ref=small_src(a, pidx), dst_ref=small_dsts[a].at[me], send_sem=sm_send.at[i],
                    recv_sem=sm_recv.at[i], device_id=peer, device_id_type=pl.DeviceIdType.MESH)
                cp.start()
                small_sends.append(cp)

        local, d2d = {}, {}
        for r in (1, 2, 3, 0):
            cx, cy = chips[r]
            for a in range(2):
                i = 2 * r + a
                local[r, a] = pltpu.make_async_copy(parts[a].at[4 * cx + 2 * cy + c], own[a].at[r], loc_sems.at[i])
                local[r, a].start()
                d2d[r, a] = pltpu.make_async_remote_copy(
                    src_ref=parts[a].at[4 * cx + 2 * cy + (1 - c)], dst_ref=sib[a].at[r], send_sem=d2d_send.at[i],
                    recv_sem=d2d_recv.at[i], device_id=sibling, device_id_type=pl.DeviceIdType.MESH)
                d2d[r, a].start()
        ici = {}
        for r in (1, 2, 3):
            cx, cy = chips[r]
            for a in range(2):
                i = 2 * (r - 1) + a
                local[r, a].wait()
                d2d[r, a].wait_recv()
                snd[a][r - 1] = (own[a][r] + sib[a][r]).astype(snd[a].dtype)
                ici[r, a] = pltpu.make_async_remote_copy(
                    src_ref=snd[a].at[r - 1], dst_ref=rcv[a].at[r - 1], send_sem=ici_send.at[i],
                    recv_sem=ici_recv.at[i], device_id=(cx, cy, c), device_id_type=pl.DeviceIdType.MESH)
                ici[r, a].start()
        for a in range(2):
            local[0, a].wait()
            d2d[0, a].wait_recv()
            acc = own[a][0] + sib[a][0]
            for r in (1, 2, 3):
                ici[r, a].wait_recv()
                acc = acc + rcv[a][r - 1].astype(F32)
            outs[a][...] = acc

        for k in range(1, N_DEV):
            peer, pidx = _peer(k, x, y, c)
            for a in range(2):
                i = 2 * (k - 1) + a
                pltpu.make_async_remote_copy(
                    src_ref=small_src(a, me), dst_ref=small_dsts[a].at[pidx], send_sem=sm_send.at[i],
                    recv_sem=sm_recv.at[i], device_id=peer, device_id_type=pl.DeviceIdType.MESH).wait_recv()
        for cp in small_sends + list(d2d.values()) + list(ici.values()):
            cp.wait_send()
        for cp in small_local:
            cp.wait()

    hbm = pl.BlockSpec(memory_space=pl.ANY)
    vmem = pl.BlockSpec(memory_space=pltpu.VMEM)
    in_blk, out_blk = parts_w_in.shape[1:], parts_w_out.shape[1:]
    return pl.pallas_call(
        body, name="reduce_grads",
        in_specs=[hbm] * 4, out_specs=[vmem, vmem, hbm, hbm],
        out_shape=[jax.ShapeDtypeStruct(in_blk, F32), jax.ShapeDtypeStruct(out_blk, F32),
                   jax.ShapeDtypeStruct((N_DEV,) + parts_wg.shape[1:], F32),
                   jax.ShapeDtypeStruct((N_DEV,) + small.shape, F32)],
        scratch_shapes=[pltpu.VMEM((4,) + in_blk, F32), pltpu.VMEM((4,) + in_blk, F32),
                        pltpu.VMEM((3,) + in_blk, MXU_DTYPE), pltpu.VMEM((3,) + in_blk, MXU_DTYPE),
                        pltpu.VMEM((4,) + out_blk, F32), pltpu.VMEM((4,) + out_blk, F32),
                        pltpu.VMEM((3,) + out_blk, MXU_DTYPE), pltpu.VMEM((3,) + out_blk, MXU_DTYPE),
                        pltpu.SemaphoreType.DMA((8,)), pltpu.SemaphoreType.DMA((8,)), pltpu.SemaphoreType.DMA((8,)),
                        pltpu.SemaphoreType.DMA((6,)), pltpu.SemaphoreType.DMA((6,)),
                        pltpu.SemaphoreType.DMA((2 * (N_DEV - 1),)), pltpu.SemaphoreType.DMA((2 * (N_DEV - 1),)),
                        pltpu.SemaphoreType.DMA((2,))],
        compiler_params=_cparams(),
    )(parts_w_in, parts_w_out, parts_wg, small)


def _adamw(recv, w, m, v, name):
    rows, width = w.shape
    tr = 128 if rows % 128 == 0 else rows
    n_parts = recv.shape[0]

    def body(r_ref, w_ref, m_ref, v_ref, g_ref, d_ref, nm_ref, nv_ref):
        g = r_ref[0]
        for j in range(1, n_parts):
            g = g + r_ref[j]
        nm = ADAM_B1 * m_ref[...] + (1.0 - ADAM_B1) * g
        nv = ADAM_B2 * v_ref[...] + (1.0 - ADAM_B2) * (g * g)
        m_hat = nm / (1.0 - ADAM_B1 ** ADAM_STEP)
        v_hat = nv / (1.0 - ADAM_B2 ** ADAM_STEP)
        g_ref[...] = g
        d_ref[...] = -ADAM_LR * (m_hat / (jnp.sqrt(v_hat) + ADAM_EPS) + ADAM_WD * w_ref[...])
        nm_ref[...] = nm
        nv_ref[...] = nv

    spec = _rows(tr, width)
    return pl.pallas_call(
        body, name=name, grid=(rows // tr,),
        in_specs=[pl.BlockSpec((n_parts, tr, width), lambda i: (0, i, 0)), spec, spec, spec],
        out_specs=[spec] * 4,
        out_shape=[jax.ShapeDtypeStruct((rows, width), F32)] * 4,
        compiler_params=_cparams(dimension_semantics=("arbitrary",)),
    )(recv, w, m, v)


def _adamw_shard_view(g, w, m, v):
    rows, width = g.shape

    def body(g_ref, w_hbm, m_hbm, v_hbm, g_out, d_out, nm_out, nv_out, bufs, outs, sems):
        loads = [pltpu.make_async_copy(src.at[:, 0, :], bufs.at[i], sems.at[i])
                 for i, src in enumerate((w_hbm, m_hbm, v_hbm))]
        for cp in loads:
            cp.start()
        g = g_ref[...]
        for cp in loads:
            cp.wait()
        nm = ADAM_B1 * bufs[1] + (1.0 - ADAM_B1) * g
        nv = ADAM_B2 * bufs[2] + (1.0 - ADAM_B2) * (g * g)
        m_hat = nm / (1.0 - ADAM_B1 ** ADAM_STEP)
        v_hat = nv / (1.0 - ADAM_B2 ** ADAM_STEP)
        outs[0] = g
        outs[1] = -ADAM_LR * (m_hat / (jnp.sqrt(v_hat) + ADAM_EPS) + ADAM_WD * bufs[0])
        outs[2] = nm
        outs[3] = nv
        stores = [pltpu.make_async_copy(outs.at[i], dst.at[:, 0, :], sems.at[3 + i])
                  for i, dst in enumerate((g_out, d_out, nm_out, nv_out))]
        for cp in stores:
            cp.start()
        for cp in stores:
            cp.wait()

    hbm = pl.BlockSpec(memory_space=pl.ANY)
    return pl.pallas_call(
        body, name="adamw_w_in",
        in_specs=[pl.BlockSpec(memory_space=pltpu.VMEM), hbm, hbm, hbm], out_specs=[hbm] * 4,
        out_shape=[jax.ShapeDtypeStruct((rows, 1, width), F32)] * 4,
        scratch_shapes=[pltpu.VMEM((3, rows, width), F32), pltpu.VMEM((4, rows, width), F32),
                        pltpu.SemaphoreType.DMA((7,))],
        compiler_params=_cparams(),
    )(g, w, m, v)


def _pack_small(ln_g, ln_b, b_gate, norm_w, sinks, loss=None):
    def rows8(t):
        t = t.reshape(-1)
        t = jnp.pad(t, (0, 1024 - t.shape[0]))
        return t.reshape(8, 128)

    loss = jnp.zeros((1,), F32) if loss is None else loss
    return jnp.concatenate([rows8(t) for t in (ln_g, ln_b, b_gate, norm_w, sinks, loss)], axis=0)


def _unpack_small(p):
    flat = [p[8 * i:8 * (i + 1)].reshape(1, 1024) for i in range(5)]
    return flat[0], flat[1], flat[2][:, :256], flat[3][:, :128], flat[4][:, :8]


def kernel(x, positions, w_in, gla_w_gate_up, gla_b_gate, attn_sinks, gla_norm_w, w_out, ln_g, ln_b, loss_target, m_w_in, m_gla_w_gate_up, m_gla_b_gate, m_attn_sinks, m_gla_norm_w, m_w_out, m_ln_g, m_ln_b, v_w_in, v_gla_w_gate_up, v_gla_b_gate, v_attn_sinks, v_gla_norm_w, v_w_out, v_ln_g, v_ln_b):
    w_in_full, w_out_full, wg_full = _all_gather_weights(_shard_view(w_in), w_out[0], gla_w_gate_up[0])

    loss, grad_x, parts_w_in, g_wg, g_bg, g_sinks, g_nw, g_w_out, g_ln = _local_step(
        x[0], positions[0], w_in_full, wg_full, gla_b_gate, attn_sinks[0], gla_norm_w, w_out_full, ln_g, ln_b,
        loss_target[0])

    parts_w_out = g_w_out.reshape(N_DEV, D_OUT_SHARD, D_MODEL)
    parts_wg = jnp.transpose(g_wg.reshape(GLA_RANK, N_DEV, 32), (1, 0, 2))
    small = _pack_small(g_ln[0:1], g_ln[1:2], g_bg, g_nw, g_sinks[:, 0].reshape(1, SWA_Q_HEADS), loss[0, 0:1])
    g_in, g_out, r_wg, r_small = _reduce_grads(parts_w_in, parts_w_out, parts_wg, small)

    upd_in = _adamw_shard_view(g_in, _shard_view(w_in), _shard_view(m_w_in), _shard_view(v_w_in))
    upd_in = [jnp.transpose(t, (1, 2, 0)) for t in upd_in]
    upd_out = _adamw(g_out[None], w_out[0], m_w_out[0], v_w_out[0], "adamw_w_out")
    upd_wg = _adamw(r_wg, gla_w_gate_up[0], m_gla_w_gate_up[0], v_gla_w_gate_up[0], "adamw_wg")
    upd_small = _adamw(
        r_small,
        _pack_small(ln_g, ln_b, gla_b_gate, gla_norm_w, attn_sinks),
        _pack_small(m_ln_g, m_ln_b, m_gla_b_gate, m_gla_norm_w, m_attn_sinks),
        _pack_small(v_ln_g, v_ln_b, v_gla_b_gate, v_gla_norm_w, v_attn_sinks), "adamw_small")

    total = jnp.sum(r_small[:, 40, 0])
    outs = [total, grad_x[None]]
    for kind in range(4):
        s_ln_g, s_ln_b, s_bg, s_nw, s_sinks = _unpack_small(upd_small[kind])
        outs += [upd_in[kind], upd_wg[kind][None], s_bg, s_sinks, s_nw, upd_out[kind][None], s_ln_g, s_ln_b]
    return tuple(outs)
```

```python
import functools

import jax
import jax.numpy as jnp
from jax import lax
from jax.experimental import pallas as pl
from jax.experimental.pallas import tpu as pltpu

F32 = jnp.float32
MXU_DTYPE = jnp.bfloat16

N_DEV = 8
D_MODEL = 1024
SWA_Q_HEADS = 8
SWA_KV_HEADS = 2
SWA_GROUP = 4
SWA_HEAD_DIM = 64
BLOCK = 128
ROPE_THETA = 500000.0
ROT_DIM = 16
GLA_HEADS = 4
GLA_DK = 64
GLA_DV = 128
GLA_RANK = 16
GLA_TAU = 16.0
GLA_CHUNK = 64
D_IN_PROJ = 2832
D_IN_SHARD = D_IN_PROJ // N_DEV
D_OUT_SHARD = D_MODEL // N_DEV
OFF = (0, 512, 640, 768, 1280, 1536, 1792, 2304, 2816, 2832)
EPS = 1e-5
ALPHA = 2.0 ** 0.25
SWA_SCALE = SWA_HEAD_DIM ** -0.5
GLA_SCALE = GLA_DK ** -0.5
ADAM_LR = 0.001
ADAM_B1 = 0.9
ADAM_B2 = 0.999
ADAM_EPS = 1e-08
ADAM_WD = 0.01
ADAM_STEP = 10
VMEM_LIMIT = 56 * 1024 * 1024

_NT = (((1,), (1,)), ((), ()))
_TN = (((0,), (0,)), ((), ()))


def _mm(a, b):
    return jnp.dot(a, b, preferred_element_type=F32)


def _mm_nt(a, b):
    return lax.dot_general(a, b, _NT, preferred_element_type=F32)


def _mm_tn(a, b):
    return lax.dot_general(a, b, _TN, preferred_element_type=F32)


def _sigmoid(t):
    return 1.0 / (1.0 + jnp.exp(-t))


def _cparams(**kw):
    return pltpu.CompilerParams(vmem_limit_bytes=VMEM_LIMIT, **kw)


def _full(shape):
    return pl.BlockSpec(shape, lambda *_: (0,) * len(shape))


def _rows(tile, width):
    return pl.BlockSpec((tile, width), lambda i: (i, 0))


def _rope_tables(positions):
    half = ROT_DIM // 2
    inv_freq = ROPE_THETA ** (-jnp.arange(half, dtype=F32) / half)
    lane = jnp.arange(128, dtype=jnp.int32) % SWA_HEAD_DIM
    ang = positions.astype(F32)[:, None] * inv_freq[None, :]
    cos, sin = lax.optimization_barrier((jnp.cos(ang), jnp.sin(ang)))
    cos, sin = jnp.tile(cos, (1, 128 // half)), jnp.tile(sin, (1, 128 // half))
    c = jnp.where(lane < ROT_DIM, cos, 1.0)
    s1 = jnp.where(lane < half, -sin, 0.0)
    s2 = jnp.where((lane >= half) & (lane < ROT_DIM), sin, 0.0)
    return c, s1, s2


def _rope(t, c, s1, s2):
    return t * c + pltpu.roll(t, 120, 1) * s1 + pltpu.roll(t, 8, 1) * s2


def _rope_t(g, c, s1, s2):
    return g * c + pltpu.roll(g * s1, 8, 1) + pltpu.roll(g * s2, 120, 1)


def _in_proj(x, w_in, wg, b_gate, rope, w_out_s):
    s = x.shape[0]
    ts = min(512, s)
    nsteps = s // ts
    forward_step = min(3, nsteps - 1)
    widths = [OFF[i + 1] - OFF[i] for i in range(9)]

    def body(x_ref, w_ref, wg_ref, bg_ref, c_ref, s1_ref, s2_ref, wos_ref,
             qa_ref, ka_ref, va_ref, ga_ref, qb_ref, kb_ref, vb_ref, gb_ref, rb_ref, la_ref, oms_ref, wout_ref,
             wout_all, send_sems, recv_sems):
        xb = x_ref[...].astype(MXU_DTYPE)
        c, s1, s2 = c_ref[...], s1_ref[...], s2_ref[...]
        i0 = pl.program_id(0)
        gather = _BlockGather(wout_all, send_sems, recv_sems)

        @pl.when(i0 == 0)
        def _():
            ka_ref[0:BLOCK, :] = jnp.zeros((BLOCK, 128), ka_ref.dtype)
            va_ref[0:BLOCK, :] = jnp.zeros((BLOCK, 128), va_ref.dtype)
            wout_all[gather.me] = wos_ref[...].astype(wout_all.dtype)
            gather.start()

        @pl.when(i0 == forward_step)
        def _():
            gather.forward()

        @pl.when(i0 == nsteps - 1)
        def _():
            gather.finish()
            for j in range(N_DEV):
                wout_ref[D_OUT_SHARD * j:D_OUT_SHARD * (j + 1), :] = wout_all[j]

        kv_rows = pl.ds(pl.multiple_of(BLOCK + i0 * ts, BLOCK), ts)

        def cols(i):
            return _mm_nt(xb, w_ref[OFF[i]:OFF[i + 1], :])

        qa = cols(0)
        for i in range(4):
            qa_ref[:, 128 * i:128 * (i + 1)] = _rope(qa[:, 128 * i:128 * (i + 1)], c, s1, s2).astype(qa_ref.dtype)
        kv = _mm_nt(xb, w_ref[OFF[1]:OFF[3], :])
        ka_ref[kv_rows, :] = _rope(kv[:, 0:128], c, s1, s2).astype(ka_ref.dtype)
        va_ref[kv_rows, :] = kv[:, 128:256].astype(va_ref.dtype)
        ga_ref[...] = cols(3)
        qb_ref[...] = cols(4)
        kb_ref[...] = cols(5)
        vb_ref[...] = cols(6).astype(vb_ref.dtype)
        gb_ref[...] = cols(7)
        rb = cols(8)
        rb_ref[...] = rb
        logit = _mm(rb.astype(MXU_DTYPE), wg_ref[...]) + bg_ref[...]
        e = jnp.exp(-jnp.abs(logit))
        la_ref[...] = (jnp.minimum(logit, 0.0) - jnp.log(1.0 + e)) / GLA_TAU
        oms_ref[...] = jnp.where(logit >= 0.0, e, 1.0) / (1.0 + e)

    out_shape = [jax.ShapeDtypeStruct((s + BLOCK if i in (1, 2) else s, w), MXU_DTYPE if i in (0, 1, 2, 6) else F32)
                 for i, w in enumerate(widths)]
    out_shape += [jax.ShapeDtypeStruct((s, 256), F32)] * 2
    out_shape += [jax.ShapeDtypeStruct((D_MODEL, D_MODEL), MXU_DTYPE)]
    return pl.pallas_call(
        body, name="in_proj", grid=(nsteps,),
        in_specs=[_rows(ts, D_MODEL), _full((D_IN_PROJ, D_MODEL)), _full((GLA_RANK, 256)), _full((1, 256)),
                  _rows(ts, 128), _rows(ts, 128), _rows(ts, 128), _full((D_OUT_SHARD, D_MODEL))],
        out_specs=[_full((s + BLOCK, w)) if i in (1, 2) else _rows(ts, w) for i, w in enumerate(widths)]
        + [_rows(ts, 256)] * 2 + [_full((D_MODEL, D_MODEL))],
        out_shape=out_shape,
        scratch_shapes=[pltpu.VMEM((N_DEV, D_OUT_SHARD, D_MODEL), MXU_DTYPE)] + _BlockGather.scratch(),
        compiler_params=_cparams(dimension_semantics=("arbitrary",)),
    )(x, w_in, wg, b_gate, *rope, w_out_s)


SWA_ROWS = SWA_GROUP * BLOCK


def _swa_bias():
    qi = lax.broadcasted_iota(jnp.int32, (2, SWA_ROWS, 2 * BLOCK), 1) & (BLOCK - 1)
    ki = lax.broadcasted_iota(jnp.int32, (2, SWA_ROWS, 2 * BLOCK), 2)
    first = lax.broadcasted_iota(jnp.int32, (2, SWA_ROWS, 2 * BLOCK), 0) == 0
    dist = qi + BLOCK - ki
    ok = (dist >= 0) & (dist < BLOCK) & (jnp.logical_not(first) | (ki >= BLOCK))
    return jnp.where(ok, 0.0, -jnp.inf).astype(F32)


SWA_SUB = 2


def _swa_bias_of(bias_ref, n, b):
    return bias_ref[jnp.minimum(n, 1)] if b == 0 else bias_ref[1]


def _swa_dup(t, j):
    t = t.astype(F32)
    low = lax.broadcasted_iota(jnp.int32, t.shape, 1) < SWA_HEAD_DIM
    keep = low if j == 0 else jnp.logical_not(low)
    return jnp.where(keep, t, pltpu.roll(t, SWA_HEAD_DIM, 1)).astype(MXU_DTYPE)


def _swa_stack(t, j):
    low = lax.broadcasted_iota(jnp.int32, (BLOCK, 128), 1) < SWA_HEAD_DIM
    zero = jnp.zeros((BLOCK, 128), t.dtype)
    blocks = []
    for p in (2 * j, 2 * j + 1):
        tp = t[:, 128 * p:128 * (p + 1)]
        blocks += [jnp.where(low, tp, zero), jnp.where(low, zero, tp)]
    return jnp.concatenate(blocks, axis=0)


def _swa_unstack(t):
    low = lax.broadcasted_iota(jnp.int32, (BLOCK, 128), 1) < SWA_HEAD_DIM
    return [jnp.where(low, t[2 * BLOCK * i:2 * BLOCK * i + BLOCK], t[2 * BLOCK * i + BLOCK:2 * BLOCK * (i + 1)])
            for i in range(2)]


def _swa_sink_col(sink_ref, j):
    row = lax.broadcasted_iota(jnp.int32, (SWA_ROWS, 1), 0)
    col = jnp.full((SWA_ROWS, 1), sink_ref[SWA_GROUP * j], F32)
    for r in range(1, SWA_GROUP):
        col = jnp.where(row >= BLOCK * r, sink_ref[SWA_GROUP * j + r], col)
    return col


def _swa_probs(qs, kd, bias, sink):
    sc = _mm_nt(qs, kd) + bias
    m = jnp.maximum(jnp.max(sc, axis=1, keepdims=True), sink)
    p = jnp.exp(sc - m)
    ps = jnp.exp(sink - m)
    rinv = 1.0 / (jnp.sum(p, axis=1, keepdims=True) + ps)
    return p * rinv, ps * rinv


def _swa_fwd(sinks, qa, k_pad, v_pad, ga):
    s = qa.shape[0]
    tq = SWA_SUB * BLOCK

    def body(sink_ref, qa_ref, ga_ref, bias_ref, k_ref, v_ref, attn_ref, cat_ref):
        n = pl.program_id(0)
        for b in range(SWA_SUB):
            rows = slice(BLOCK * b, BLOCK * (b + 1))
            start = pl.multiple_of((n * SWA_SUB + b) * BLOCK, BLOCK)
            kw = k_ref[pl.ds(start, 2 * BLOCK), :]
            vw = v_ref[pl.ds(start, 2 * BLOCK), :]
            bias = _swa_bias_of(bias_ref, n, b)
            q = qa_ref[rows, :] * SWA_SCALE
            g = ga_ref[rows, :]
            silu = g * _sigmoid(g)
            for j in range(SWA_KV_HEADS):
                qs = _swa_stack(q, j).astype(MXU_DTYPE)
                probs, _ = _swa_probs(qs, _swa_dup(kw, j), bias, _swa_sink_col(sink_ref, j))
                pairs = _swa_unstack(_mm(probs.astype(MXU_DTYPE), _swa_dup(vw, j)))
                for i in range(2):
                    lanes = slice(128 * (2 * j + i), 128 * (2 * j + i + 1))
                    attn_ref[rows, lanes] = pairs[i]
                    cat_ref[rows, lanes] = (pairs[i] * silu[:, lanes]).astype(cat_ref.dtype)

    return pl.pallas_call(
        body, name="swa_fwd", grid=(s // tq,),
        in_specs=[pl.BlockSpec(memory_space=pltpu.SMEM), _rows(tq, 512), _rows(tq, 512),
                  _full((2, SWA_ROWS, 2 * BLOCK)), _full((s + BLOCK, 128)), _full((s + BLOCK, 128))],
        out_specs=[_rows(tq, 512), _rows(tq, 512)],
        out_shape=[jax.ShapeDtypeStruct((s, 512), F32), jax.ShapeDtypeStruct((s, 512), MXU_DTYPE)],
        compiler_params=_cparams(dimension_semantics=("arbitrary",)),
    )(sinks, qa, ga, _swa_bias(), k_pad, v_pad)


GLA_KW = GLA_HEADS * GLA_DK
GLA_VW = GLA_HEADS * GLA_DV


def _idiv(t, d):
    return t >> (d.bit_length() - 1)


def _chunk_cumsum(t, lower):
    n, w = t.shape
    r = lax.broadcasted_iota(jnp.int32, (n, n), 0)
    c = lax.broadcasted_iota(jnp.int32, (n, n), 1)
    tri = ((_idiv(r, GLA_CHUNK) == _idiv(c, GLA_CHUNK)) & ((r >= c) if lower else (r <= c))).astype(MXU_DTYPE)
    hi = t.astype(MXU_DTYPE)
    r1 = t - hi.astype(F32)
    mid = r1.astype(MXU_DTYPE)
    lo = (r1 - mid.astype(F32)).astype(MXU_DTYPE)
    parts = _mm(tri, jnp.concatenate([hi, mid, lo], axis=1))
    return (parts[:, :w] + parts[:, w:2 * w]) + parts[:, 2 * w:]


def _chunk_last(t):
    n = t.shape[0]
    return jnp.concatenate(
        [jnp.broadcast_to(t[c + GLA_CHUNK - 1:c + GLA_CHUNK, :], (GLA_CHUNK, t.shape[1]))
         for c in range(0, n, GLA_CHUNK)], axis=0)


def _head_stack(t, width):
    head = _idiv(lax.broadcasted_iota(jnp.int32, t.shape, 1), width)
    zero = jnp.zeros_like(t)
    return jnp.concatenate([jnp.where(head == h, t, zero) for h in range(GLA_HEADS)], axis=0)


def _gla_masks():
    row = lax.broadcasted_iota(jnp.int32, (GLA_CHUNK, GLA_KW), 0)
    pos = lax.broadcasted_iota(jnp.int32, (GLA_CHUNK, GLA_KW), 1) & (GLA_CHUNK - 1)
    srow = _idiv(lax.broadcasted_iota(jnp.int32, (GLA_VW, GLA_KW), 0), GLA_DV)
    slane = _idiv(lax.broadcasted_iota(jnp.int32, (GLA_VW, GLA_KW), 1), GLA_DK)
    return pos <= row, pos >= row, srow == slane


def _gla_fwd(qb, kb, vb, la, gb, norm_w):
    s = qb.shape[0]
    tb = min(256, s)
    ch = tb // GLA_CHUNK

    def body(qb_ref, kb_ref, vb_ref, la_ref, gb_ref, nw_ref, o_ref, cat_ref, sp_ref, st_ref):
        @pl.when(pl.program_id(0) == 0)
        def _():
            st_ref[...] = jnp.zeros_like(st_ref)

        causal, _, same_head = _gla_masks()
        nw = nw_ref[...]
        b = _chunk_cumsum(la_ref[...], True)
        bl = _chunk_last(b)
        k = kb_ref[...]
        qd = ((qb_ref[...] * GLA_SCALE) * jnp.exp(b)).astype(MXU_DTYPE)
        ki = (k * jnp.exp(-b)).astype(MXU_DTYPE)
        ke = (k * jnp.exp(bl - b)).astype(MXU_DTYPE)
        dec = jnp.exp(bl)
        v = vb_ref[...].astype(MXU_DTYPE)
        g = gb_ref[...]
        silu = g * _sigmoid(g)
        for ci in range(ch):
            rows = slice(GLA_CHUNK * ci, GLA_CHUNK * (ci + 1))
            a = jnp.where(causal, _mm_nt(qd[rows], _head_stack(ki[rows], GLA_DK)), 0.0).astype(MXU_DTYPE)
            st = st_ref[...]
            sp_ref[ci] = (st[0:GLA_DV] + st[GLA_DV:2 * GLA_DV]) + (st[2 * GLA_DV:3 * GLA_DV] + st[3 * GLA_DV:])
            o = _mm(a, _head_stack(v[rows], GLA_DV)) + _mm_nt(qd[rows], st.astype(MXU_DTYPE))
            st_ref[...] = st * dec[rows][0:1] + jnp.where(same_head, _mm_tn(v[rows], ke[rows]), 0.0)
            o_ref[rows, :] = o
            for h in range(GLA_HEADS):
                lv = slice(GLA_DV * h, GLA_DV * (h + 1))
                oh = o[:, lv]
                r = lax.rsqrt(jnp.mean(oh * oh, axis=1, keepdims=True) + EPS)
                cat_ref[rows, lv] = (oh * r * nw * silu[rows, lv]).astype(cat_ref.dtype)

    return pl.pallas_call(
        body, name="gla_fwd", grid=(s // tb,),
        in_specs=[_rows(tb, 256), _rows(tb, 256), _rows(tb, 512), _rows(tb, 256), _rows(tb, 512), _full((1, 128))],
        out_specs=[_rows(tb, 512), _rows(tb, 512), pl.BlockSpec((ch, GLA_DV, 256), lambda i: (i, 0, 0))],
        out_shape=[jax.ShapeDtypeStruct((s, 512), F32), jax.ShapeDtypeStruct((s, 512), MXU_DTYPE),
                   jax.ShapeDtypeStruct((s // GLA_CHUNK, GLA_DV, 256), F32)],
        scratch_shapes=[pltpu.VMEM((GLA_VW, GLA_KW), F32)],
        compiler_params=_cparams(dimension_semantics=("arbitrary",)),
    )(qb, kb, vb, la, gb, norm_w)


def _out_ln_loss(cat_a, cat_b, w_out, x, target, ln_g, ln_b):
    s = x.shape[0]
    ts = min(512, s)
    halves = 2 if ts % 32 == 0 else 1
    th = ts // halves

    def body(ca_ref, cb_ref, w_ref, x_ref, t_ref, g_ref, b_ref,
             loss_ref, gx_ref, da_ref, db_ref, gw_ref, gln_ref):
        @pl.when(pl.program_id(0) == 0)
        def _():
            loss_ref[...] = jnp.zeros_like(loss_ref)
            gw_ref[...] = jnp.zeros_like(gw_ref)
            gln_ref[...] = jnp.zeros_like(gln_ref)

        g = g_ref[...]
        dh16s = []
        for k in range(halves):
            rows = slice(th * k, th * (k + 1))
            mix = _mm(ca_ref[rows, :], w_ref[0:512, :]) + _mm(cb_ref[rows, :], w_ref[512:1024, :])
            h = ALPHA * x_ref[rows, :] + mix
            mu = jnp.mean(h, axis=1, keepdims=True)
            hc = h - mu
            rstd = lax.rsqrt(jnp.mean(hc * hc, axis=1, keepdims=True) + EPS)
            xhat = hc * rstd
            err = xhat * g + b_ref[...] - t_ref[rows, :]
            loss_ref[...] += 0.5 * jnp.sum(jnp.mean(err * err, axis=1, keepdims=True))
            dy = err * (1.0 / D_MODEL)
            gln_ref[0:1, :] += jnp.sum(dy * xhat, axis=0, keepdims=True)
            gln_ref[1:2, :] += jnp.sum(dy, axis=0, keepdims=True)
            dxh = dy * g
            dh = rstd * (dxh - jnp.mean(dxh, axis=1, keepdims=True)
                         - xhat * jnp.mean(dxh * xhat, axis=1, keepdims=True))
            gx_ref[rows, :] = ALPHA * dh
            dh16s.append(dh.astype(MXU_DTYPE))
        for k in range(halves):
            rows = slice(th * k, th * (k + 1))
            da_ref[rows, :] = _mm_nt(dh16s[k], w_ref[0:512, :])
            db_ref[rows, :] = _mm_nt(dh16s[k], w_ref[512:1024, :])
        dh16 = jnp.concatenate(dh16s, axis=0)
        gw_ref[0:512, :] += _mm_tn(ca_ref[...], dh16)
        gw_ref[512:1024, :] += _mm_tn(cb_ref[...], dh16)

    return pl.pallas_call(
        body, name="out_ln_loss", grid=(s // ts,),
        in_specs=[_rows(ts, 512), _rows(ts, 512), _full((D_MODEL, D_MODEL)), _rows(ts, D_MODEL), _rows(ts, D_MODEL),
                  _full((1, D_MODEL)), _full((1, D_MODEL))],
        out_specs=[_full((1, 128)), _rows(ts, D_MODEL), _rows(ts, 512), _rows(ts, 512),
                   _full((D_MODEL, D_MODEL)), _full((2, D_MODEL))],
        out_shape=[jax.ShapeDtypeStruct((1, 128), F32), jax.ShapeDtypeStruct((s, D_MODEL), F32),
                   jax.ShapeDtypeStruct((s, 512), F32), jax.ShapeDtypeStruct((s, 512), F32),
                   jax.ShapeDtypeStruct((D_MODEL, D_MODEL), F32), jax.ShapeDtypeStruct((2, D_MODEL), F32)],
        compiler_params=_cparams(dimension_semantics=("arbitrary",)),
    )(cat_a, cat_b, w_out, x, target, ln_g, ln_b)


def _swa_bwd(sinks, qa, k_pad, v_pad, attn, ga, d_cat_a, rope, parts_w_out):
    s = qa.shape[0]
    tq = SWA_SUB * BLOCK
    nsteps = s // tq
    forward_step = min(2, nsteps - 1)

    def body(sink_ref, qa_ref, ga_ref, at_ref, dc_ref, c_ref, s1_ref, s2_ref, bias_ref, k_ref, v_ref, pout_ref,
             dq_ref, dg_ref, dk_ref, dv_ref, ds_ref, gout_ref, *scratch):
        n = pl.program_id(0)
        owner_sum = _OwnerSum(pout_ref, *scratch)

        @pl.when(n == 0)
        def _():
            dk_ref[...] = jnp.zeros_like(dk_ref)
            dv_ref[...] = jnp.zeros_like(dv_ref)
            ds_ref[...] = jnp.zeros_like(ds_ref)
            owner_sum.start()

        @pl.when(n == forward_step)
        def _():
            owner_sum.forward()

        @pl.when(n == nsteps - 1)
        def _():
            gout_ref[...] = owner_sum.finish()

        low = lax.broadcasted_iota(jnp.int32, (2 * BLOCK, 128), 1) < SWA_HEAD_DIM
        for b in range(SWA_SUB):
            rows = slice(BLOCK * b, BLOCK * (b + 1))
            start = pl.multiple_of((n * SWA_SUB + b) * BLOCK, BLOCK)
            kw = k_ref[pl.ds(start, 2 * BLOCK), :]
            vw = v_ref[pl.ds(start, 2 * BLOCK), :]
            bias = _swa_bias_of(bias_ref, n, b)
            q = qa_ref[rows, :] * SWA_SCALE
            g = ga_ref[rows, :]
            sg = _sigmoid(g)
            o = at_ref[rows, :]
            dc = dc_ref[rows, :]
            do = dc * (g * sg)
            dg_ref[rows, :] = (dc * o * (sg * (1.0 + g * (1.0 - sg)))).astype(dg_ref.dtype)
            od = do * o
            c, s1, s2 = c_ref[rows, :], s1_ref[rows, :], s2_ref[rows, :]
            dk, dv = [], []
            for j in range(SWA_KV_HEADS):
                kd, vd = _swa_dup(kw, j), _swa_dup(vw, j)
                qs = _swa_stack(q, j).astype(MXU_DTYPE)
                dos = _swa_stack(do, j).astype(MXU_DTYPE)
                probs, psink = _swa_probs(qs, kd, bias, _swa_sink_col(sink_ref, j))
                delta = jnp.sum(_swa_stack(od, j), axis=1, keepdims=True)
                dsc = (probs * (_mm_nt(dos, vd) - delta)).astype(MXU_DTYPE)
                dsink = psink * delta
                for r in range(SWA_GROUP):
                    h = SWA_GROUP * j + r
                    ds_ref[h:h + 1, :] += jnp.zeros((1, 128), F32) - jnp.sum(dsink[BLOCK * r:BLOCK * (r + 1)])
                dq = _swa_unstack(_mm(dsc, kd))
                for i in range(2):
                    lanes = slice(128 * (2 * j + i), 128 * (2 * j + i + 1))
                    dq_ref[rows, lanes] = _rope_t(dq[i] * SWA_SCALE, c, s1, s2).astype(dq_ref.dtype)
                dkj = _mm_tn(dsc, qs)
                dvj = _mm_tn(probs.astype(MXU_DTYPE), dos)
                dk.append(dkj + pltpu.roll(dkj, SWA_HEAD_DIM, 1))
                dv.append(dvj + pltpu.roll(dvj, SWA_HEAD_DIM, 1))
            dk_ref[pl.ds(start, 2 * BLOCK), :] += jnp.where(low, dk[0], dk[1])
            dv_ref[pl.ds(start, 2 * BLOCK), :] += jnp.where(low, dv[0], dv[1])

    out_blk = parts_w_out.shape[1:]
    return pl.pallas_call(
        body, name="swa_bwd", grid=(nsteps,),
        in_specs=[pl.BlockSpec(memory_space=pltpu.SMEM)] + [_rows(tq, 512)] * 4 + [_rows(tq, 128)] * 3
        + [_full((2, SWA_ROWS, 2 * BLOCK))] + [_full((s + BLOCK, 128))] * 2 + [pl.BlockSpec(memory_space=pl.ANY)],
        out_specs=[_rows(tq, 512), _rows(tq, 512), _full((s + BLOCK, 128)), _full((s + BLOCK, 128)),
                   _full((SWA_Q_HEADS, 128)), _full(out_blk)],
        out_shape=[jax.ShapeDtypeStruct((s, 512), MXU_DTYPE), jax.ShapeDtypeStruct((s, 512), MXU_DTYPE),
                   jax.ShapeDtypeStruct((s + BLOCK, 128), F32), jax.ShapeDtypeStruct((s + BLOCK, 128), F32),
                   jax.ShapeDtypeStruct((SWA_Q_HEADS, 128), F32), jax.ShapeDtypeStruct(out_blk, F32)],
        scratch_shapes=_OwnerSum.scratch(out_blk),
        compiler_params=_cparams(dimension_semantics=("arbitrary",)),
    )(sinks, qa, ga, attn, d_cat_a, *rope, _swa_bias(), k_pad, v_pad, parts_w_out)


def _gla_bwd(qb, kb, vb, la, oms, gb, o, sprev, d_cat_b, rb, wg, norm_w):
    s = qb.shape[0]
    tb = min(512, s)
    ch = tb // GLA_CHUNK
    nb = s // tb

    def body(qb_ref, kb_ref, vb_ref, la_ref, oms_ref, gb_ref, o_ref, sp_ref, dc_ref, rb_ref, wg_ref, nw_ref,
             dq_ref, dk_ref, dv_ref, dg_ref, dr_ref, gwg_ref, gbg_ref, gnw_ref, dst_ref):
        @pl.when(pl.program_id(0) == 0)
        def _():
            dst_ref[...] = jnp.zeros_like(dst_ref)
            gwg_ref[...] = jnp.zeros_like(gwg_ref)
            gbg_ref[...] = jnp.zeros_like(gbg_ref)
            gnw_ref[...] = jnp.zeros_like(gnw_ref)

        causal, causal_t, same_head = _gla_masks()
        nw = nw_ref[...]
        b = _chunk_cumsum(la_ref[...], True)
        bl = _chunk_last(b)
        eb, enb, ee, dec = jnp.exp(b), jnp.exp(-b), jnp.exp(bl - b), jnp.exp(bl)
        k = kb_ref[...]
        qd = (qb_ref[...] * GLA_SCALE) * eb
        ki = k * enb
        ke = k * ee
        qd16, ki16, ke16 = qd.astype(MXU_DTYPE), ki.astype(MXU_DTYPE), ke.astype(MXU_DTYPE)
        v16 = vb_ref[...].astype(MXU_DTYPE)

        g = gb_ref[...]
        sg = _sigmoid(g)
        silu = g * sg
        dsilu = sg * (1.0 + g * (1.0 - sg))
        gnw = jnp.zeros((1, GLA_DV), F32)
        do = []
        for h in range(GLA_HEADS):
            lv = slice(GLA_DV * h, GLA_DV * (h + 1))
            oh = o_ref[:, lv]
            dch = dc_ref[:, lv]
            r = lax.rsqrt(jnp.mean(oh * oh, axis=1, keepdims=True) + EPS)
            d_on = dch * silu[:, lv]
            dg_ref[:, lv] = (dch * (oh * r * nw) * dsilu[:, lv]).astype(dg_ref.dtype)
            gnw += jnp.sum(d_on * oh * r, axis=0, keepdims=True)
            u = d_on * nw
            do.append(r * u - oh * (r * r * r) * jnp.mean(u * oh, axis=1, keepdims=True))
        gnw_ref[...] += gnw
        do16 = jnp.concatenate(do, axis=1).astype(MXU_DTYPE)

        db, dbl = [None] * ch, [None] * ch
        for ci in reversed(range(ch)):
            rows = slice(GLA_CHUNK * ci, GLA_CHUNK * (ci + 1))
            qds, kis = _head_stack(qd16[rows], GLA_DK), _head_stack(ki16[rows], GLA_DK)
            vs, dos = _head_stack(v16[rows], GLA_DV), _head_stack(do16[rows], GLA_DV)
            a = jnp.where(causal, _mm_nt(qd16[rows], kis), 0.0).astype(MXU_DTYPE)
            at = jnp.where(causal_t, _mm_nt(ki16[rows], qds), 0.0).astype(MXU_DTYPE)
            da = jnp.where(causal, _mm_nt(do16[rows], vs), 0.0).astype(MXU_DTYPE)
            dat = jnp.where(causal_t, _mm_nt(v16[rows], dos), 0.0).astype(MXU_DTYPE)
            stc = sp_ref[ci]
            st = jnp.where(same_head, jnp.concatenate([stc] * GLA_HEADS, axis=0), 0.0)
            st16 = st.astype(MXU_DTYPE)
            dst = dst_ref[...]
            dst16 = dst.astype(MXU_DTYPE)
            dv = _mm(at, dos) + _mm_nt(ke16[rows], dst16)
            dqd = _mm(da, kis) + _mm(do16[rows], st16)
            dki = _mm(dat, qds)
            dke = _mm(v16[rows], dst16)
            ddec = jnp.sum(dst * st, axis=0, keepdims=True)
            decc = dec[rows][0:1]
            dst_ref[...] = jnp.where(same_head, _mm_tn(do16[rows], qd16[rows]), 0.0) + dst * decc
            dq_ref[rows, :] = (dqd * eb[rows] * GLA_SCALE).astype(dq_ref.dtype)
            dk_ref[rows, :] = (dki * enb[rows] + dke * ee[rows]).astype(dk_ref.dtype)
            dv_ref[rows, :] = dv.astype(dv_ref.dtype)
            dke_ke = dke * ke[rows]
            db[ci] = dqd * qd[rows] - dki * ki[rows] - dke_ke
            dbl[ci] = jnp.broadcast_to(jnp.sum(dke_ke, axis=0, keepdims=True) + ddec * decc, (GLA_CHUNK, GLA_KW))

        dla = _chunk_cumsum(jnp.concatenate(db, axis=0), False) + jnp.concatenate(dbl, axis=0)
        dlogit = dla * oms_ref[...] * (1.0 / GLA_TAU)
        dl16 = dlogit.astype(MXU_DTYPE)
        gbg_ref[...] += jnp.sum(dlogit, axis=0, keepdims=True)
        gwg_ref[...] += _mm_tn(rb_ref[...].astype(MXU_DTYPE), dl16)
        dr_ref[...] = _mm_nt(dl16, wg_ref[...]).astype(dr_ref.dtype)

    def rev(width):
        return pl.BlockSpec((tb, width), lambda i: (nb - 1 - i, 0))

    return pl.pallas_call(
        body, name="gla_bwd", grid=(nb,),
        in_specs=[rev(256), rev(256), rev(512), rev(256), rev(256), rev(512), rev(512),
                  pl.BlockSpec((ch, GLA_DV, 256), lambda i: (nb - 1 - i, 0, 0)), rev(512), rev(GLA_RANK),
                  _full((GLA_RANK, 256)), _full((1, 128))],
        out_specs=[rev(256), rev(256), rev(512), rev(512), rev(GLA_RANK),
                   _full((GLA_RANK, 256)), _full((1, 256)), _full((1, 128))],
        out_shape=[jax.ShapeDtypeStruct((s, 256), MXU_DTYPE), jax.ShapeDtypeStruct((s, 256), MXU_DTYPE),
                   jax.ShapeDtypeStruct((s, 512), MXU_DTYPE), jax.ShapeDtypeStruct((s, 512), MXU_DTYPE),
                   jax.ShapeDtypeStruct((s, GLA_RANK), MXU_DTYPE), jax.ShapeDtypeStruct((GLA_RANK, 256), F32),
                   jax.ShapeDtypeStruct((1, 256), F32), jax.ShapeDtypeStruct((1, 128), F32)],
        scratch_shapes=[pltpu.VMEM((GLA_VW, GLA_KW), F32)],
        compiler_params=_cparams(dimension_semantics=("arbitrary",)),
    )(qb, kb, vb, la, oms, gb, o, sprev, d_cat_b, rb, wg, norm_w)


def _in_proj_bwd_x(gx0, pieces, w_in, rope):
    s = gx0.shape[0]
    ts = min(512, s)
    widths = [OFF[i + 1] - OFF[i] for i in range(9)]

    def body(gx0_ref, *refs):
        piece_refs = refs[:9]
        w_ref, c_ref, s1_ref, s2_ref, gx_ref, dp_ref = refs[9:]
        kv_rows = pl.ds(pl.multiple_of(BLOCK + pl.program_id(0) * ts, BLOCK), ts)
        acc = gx0_ref[...]
        for i in (0, 1, 3, 4, 5, 6, 7, 8):
            lo, hi = OFF[i], OFF[i + 1]
            if i == 1:
                dk = _rope_t(piece_refs[1][kv_rows, :], c_ref[...], s1_ref[...], s2_ref[...])
                t16 = jnp.concatenate([dk, piece_refs[2][kv_rows, :]], axis=1).astype(MXU_DTYPE)
                hi = OFF[3]
            else:
                t16 = piece_refs[i][...].astype(MXU_DTYPE)
            dp_ref[:, lo:hi] = t16
            acc += _mm(t16, w_ref[lo:hi, :])
        gx_ref[...] = acc

    return pl.pallas_call(
        body, name="in_proj_bwd_x", grid=(s // ts,),
        in_specs=[_rows(ts, D_MODEL)]
        + [_full((s + BLOCK, w)) if i in (1, 2) else _rows(ts, w) for i, w in enumerate(widths)]
        + [_full((D_IN_PROJ, D_MODEL))] + [_rows(ts, 128)] * 3,
        out_specs=[_rows(ts, D_MODEL), _rows(ts, D_IN_PROJ)],
        out_shape=[jax.ShapeDtypeStruct((s, D_MODEL), F32), jax.ShapeDtypeStruct((s, D_IN_PROJ), MXU_DTYPE)],
        compiler_params=_cparams(dimension_semantics=("arbitrary",)),
    )(gx0, *pieces, w_in, *rope)


def _in_proj_bwd_w(x, dproj):
    s = x.shape[0]
    ts = min(1024, s)
    nsteps = s // ts
    col_chunks = [(OFF[i], OFF[i + 2] if i == 1 else OFF[i + 1]) for i in (0, 1, 3, 4, 5, 6, 7, 8)]

    def body(x_ref, dp_ref, gw_ref, acc_ref, stage_ref, sems):
        i = pl.program_id(0)

        @pl.when(i == 0)
        def _():
            acc_ref[...] = jnp.zeros_like(acc_ref)

        xb = x_ref[...].astype(MXU_DTYPE)
        for lo, hi in col_chunks:
            acc_ref[lo:hi, :] += _mm_tn(dp_ref[:, lo:hi], xb)

        @pl.when(i == nsteps - 1)
        def _():
            copies = []
            for j in range(N_DEV):
                slot = j % 2
                if j >= 2:
                    copies[j - 2].wait()
                stage_ref[slot] = acc_ref[D_IN_SHARD * j:D_IN_SHARD * (j + 1), :]
                cp = pltpu.make_async_copy(stage_ref.at[slot], gw_ref.at[j], sems.at[slot])
                cp.start()
                copies.append(cp)
            copies[N_DEV - 2].wait()
            copies[N_DEV - 1].wait()

    return pl.pallas_call(
        body, name="in_proj_bwd_w", grid=(nsteps,),
        in_specs=[_rows(ts, D_MODEL), _rows(ts, D_IN_PROJ)],
        out_specs=pl.BlockSpec(memory_space=pl.ANY),
        out_shape=jax.ShapeDtypeStruct((N_DEV, D_IN_SHARD, D_MODEL), F32),
        scratch_shapes=[pltpu.VMEM((D_IN_PROJ, D_MODEL), F32), pltpu.VMEM((2, D_IN_SHARD, D_MODEL), F32),
                        pltpu.SemaphoreType.DMA((2,))],
        compiler_params=_cparams(dimension_semantics=("arbitrary",)),
    )(x, dproj)


def _local_step(x, positions, w_in, wg, b_gate, sinks, norm_w, w_out_s, ln_g, ln_b, target):
    rope = _rope_tables(positions)
    qa, k_pad, v_pad, ga, qb, kb, vb, gb, rb, la, oms, w_out = _in_proj(x, w_in, wg, b_gate, rope, w_out_s)
    attn, cat_a = _swa_fwd(sinks, qa, k_pad, v_pad, ga)
    o, cat_b, sprev = _gla_fwd(qb, kb, vb, la, gb, norm_w)
    loss, gx0, d_cat_a, d_cat_b, g_w_out, g_ln = _out_ln_loss(cat_a, cat_b, w_out, x, target, ln_g, ln_b)
    parts_w_out = g_w_out.reshape(N_DEV, D_OUT_SHARD, D_MODEL)
    dqa, dga, dk_pad, dv_pad, g_sinks, g_out = _swa_bwd(sinks, qa, k_pad, v_pad, attn, ga, d_cat_a, rope, parts_w_out)
    dqb, dkb, dvb, dgb, drb, g_wg, g_bg, g_nw = _gla_bwd(qb, kb, vb, la, oms, gb, o, sprev, d_cat_b, rb, wg, norm_w)
    pieces = (dqa, dk_pad, dv_pad, dga, dqb, dkb, dvb, dgb, drb)
    grad_x, dproj = _in_proj_bwd_x(gx0, pieces, w_in, rope)
    g_w_in = _in_proj_bwd_w(x, dproj)
    return loss, grad_x, g_w_in, g_wg, g_bg, g_sinks, g_nw, g_out, g_ln


def _mesh_pos():
    return lax.axis_index("x"), lax.axis_index("y"), lax.axis_index("c")


def _peer(k, x, y, c):
    px = (1 - x) if k & 4 else x
    py = (1 - y) if k & 2 else y
    pc = (1 - c) if k & 1 else c
    return (px, py, pc), 4 * px + 2 * py + pc


def _other_chips(x, y):
    return [(1 - x, y), (x, 1 - y), (1 - x, 1 - y)]


def _shard_view(t):
    return jnp.transpose(t, (2, 0, 1))


class _BlockGather:
    def __init__(self, slots, send_sems, recv_sems):
        self.slots, self.send_sems, self.recv_sems = slots, send_sems, recv_sems
        x, y, c = _mesh_pos()
        self.xy, self.c, self.me, self.sibling = (x, y), c, 4 * x + 2 * y + c, (x, y, 1 - c)
        self.chips = _other_chips(x, y)

    @staticmethod
    def scratch():
        return [pltpu.SemaphoreType.DMA((N_DEV - 1,)), pltpu.SemaphoreType.DMA((N_DEV - 1,))]

    def _copy(self, k, block, to):
        return pltpu.make_async_remote_copy(
            src_ref=self.slots.at[block], dst_ref=self.slots.at[block], send_sem=self.send_sems.at[k],
            recv_sem=self.recv_sems.at[k], device_id=to, device_id_type=pl.DeviceIdType.MESH)

    def start(self):
        for j, (cx, cy) in enumerate(self.chips):
            self._copy(1 + j, self.me, (cx, cy, self.c)).start()
        self._copy(0, self.me, self.sibling).start()

    def forward(self):
        for j, (cx, cy) in enumerate(self.chips):
            block = 4 * cx + 2 * cy + self.c
            self._copy(1 + j, block, self.sibling).wait_recv()
            self._copy(4 + j, block, self.sibling).start()

    def finish(self):
        x, y = self.xy
        self._copy(0, 4 * x + 2 * y + (1 - self.c), self.sibling).wait_recv()
        for j, (cx, cy) in enumerate(self.chips):
            self._copy(4 + j, 4 * cx + 2 * cy + (1 - self.c), self.sibling).wait_recv()
        for k in range(N_DEV - 1):
            self._copy(k, self.me, self.sibling).wait_send()


def _all_gather_weights(w_in_t, wg_s):
    def body(win_hbm, wg_ref, win_full, wg_full, win_all, wg_all, stage, stage_sem, *sems):
        gathers = (_BlockGather(win_all, *sems[0:2]), _BlockGather(wg_all, *sems[2:4]))
        me = gathers[0].me
        load = pltpu.make_async_copy(win_hbm.at[:, 0, :], stage, stage_sem)
        load.start()
        load.wait()
        win_all[me] = stage[...].astype(win_all.dtype)
        wg_all[me] = wg_ref[...].astype(wg_all.dtype)
        for stage_of in ("start", "forward", "finish"):
            for gather in gathers:
                getattr(gather, stage_of)()
        for j in range(N_DEV):
            win_full[D_IN_SHARD * j:D_IN_SHARD * (j + 1), :] = win_all[j]
            wg_full[:, 32 * j:32 * (j + 1)] = wg_all[j]

    vmem = pl.BlockSpec(memory_space=pltpu.VMEM)
    return pl.pallas_call(
        body, name="all_gather_weights",
        in_specs=[pl.BlockSpec(memory_space=pl.ANY), vmem], out_specs=[vmem] * 2,
        out_shape=[jax.ShapeDtypeStruct((D_IN_PROJ, D_MODEL), MXU_DTYPE),
                   jax.ShapeDtypeStruct((GLA_RANK, 256), MXU_DTYPE)],
        scratch_shapes=[pltpu.VMEM((N_DEV, D_IN_SHARD, D_MODEL), MXU_DTYPE),
                        pltpu.VMEM((N_DEV, GLA_RANK, 32), MXU_DTYPE),
                        pltpu.VMEM((D_IN_SHARD, D_MODEL), F32), pltpu.SemaphoreType.DMA]
        + _BlockGather.scratch() + _BlockGather.scratch(),
        compiler_params=_cparams(),
    )(w_in_t, wg_s)


class _OwnerSum:
    def __init__(self, parts, own, sib, snd, rcv, loc_sems, d2d_send, d2d_recv, ici_send, ici_recv):
        self.parts, self.own, self.sib, self.snd, self.rcv = parts, own, sib, snd, rcv
        self.sems = (loc_sems, d2d_send, d2d_recv, ici_send, ici_recv)
        x, y, c = _mesh_pos()
        self.c, self.sibling = c, (x, y, 1 - c)
        self.chips = [(x, y)] + _other_chips(x, y)

    @staticmethod
    def scratch(block):
        return [pltpu.VMEM((4,) + block, F32), pltpu.VMEM((4,) + block, F32),
                pltpu.VMEM((3,) + block, MXU_DTYPE), pltpu.VMEM((3,) + block, MXU_DTYPE),
                pltpu.SemaphoreType.DMA((4,)), pltpu.SemaphoreType.DMA((4,)), pltpu.SemaphoreType.DMA((4,)),
                pltpu.SemaphoreType.DMA((3,)), pltpu.SemaphoreType.DMA((3,))]

    def _local(self, r):
        cx, cy = self.chips[r]
        return pltpu.make_async_copy(self.parts.at[4 * cx + 2 * cy + self.c], self.own.at[r], self.sems[0].at[r])

    def _d2d(self, r):
        cx, cy = self.chips[r]
        return pltpu.make_async_remote_copy(
            src_ref=self.parts.at[4 * cx + 2 * cy + (1 - self.c)], dst_ref=self.sib.at[r], send_sem=self.sems[1].at[r],
            recv_sem=self.sems[2].at[r], device_id=self.sibling, device_id_type=pl.DeviceIdType.MESH)

    def _ici(self, r):
        cx, cy = self.chips[r]
        return pltpu.make_async_remote_copy(
            src_ref=self.snd.at[r - 1], dst_ref=self.rcv.at[r - 1], send_sem=self.sems[3].at[r - 1],
            recv_sem=self.sems[4].at[r - 1], device_id=(cx, cy, self.c), device_id_type=pl.DeviceIdType.MESH)

    def start(self):
        for r in (1, 2, 3, 0):
            self._local(r).start()
            self._d2d(r).start()

    def forward(self):
        for r in (1, 2, 3):
            self._local(r).wait()
            self._d2d(r).wait_recv()
            self.snd[r - 1] = (self.own[r] + self.sib[r]).astype(self.snd.dtype)
            self._ici(r).start()

    def finish(self):
        self._local(0).wait()
        self._d2d(0).wait_recv()
        acc = self.own[0] + self.sib[0]
        for r in (1, 2, 3):
            self._ici(r).wait_recv()
            acc = acc + self.rcv[r - 1].astype(F32)
        for r in range(4):
            self._d2d(r).wait_send()
        for r in (1, 2, 3):
            self._ici(r).wait_send()
        return acc


def _reduce_grads(parts_w_in, parts_wg, small):
    def body(pin_ref, pwg_ref, sm_ref, gin_ref, rwg_ref, rsm_ref, *scratch):
        sm_send, sm_recv, sm_loc = scratch[-3:]
        x, y, c = _mesh_pos()
        me = 4 * x + 2 * y + c
        owner_sum = _OwnerSum(pin_ref, *scratch[:-3])

        small_dsts = (rwg_ref, rsm_ref)

        def small_src(a, block):
            return pwg_ref.at[block] if a == 0 else sm_ref

        small_local = [pltpu.make_async_copy(small_src(a, me), small_dsts[a].at[me], sm_loc.at[a]) for a in range(2)]
        for cp in small_local:
            cp.start()
        small_sends = []
        for k in range(1, N_DEV):
            peer, pidx = _peer(k, x, y, c)
            for a in range(2):
                i = 2 * (k - 1) + a
                cp = pltpu.make_async_remote_copy(
                    src_ref=small_src(a, pidx), dst_ref=small_dsts[a].at[me], send_sem=sm_send.at[i],
                    recv_sem=sm_recv.at[i], device_id=peer, device_id_type=pl.DeviceIdType.MESH)
                cp.start()
                small_sends.append(cp)

        owner_sum.start()
        owner_sum.forward()
        gin_ref[...] = owner_sum.finish()

        for k in range(1, N_DEV):
            peer, pidx = _peer(k, x, y, c)
            for a in range(2):
                i = 2 * (k - 1) + a
                pltpu.make_async_remote_copy(
                    src_ref=small_src(a, me), dst_ref=small_dsts[a].at[pidx], send_sem=sm_send.at[i],
                    recv_sem=sm_recv.at[i], device_id=peer, device_id_type=pl.DeviceIdType.MESH).wait_recv()
        for cp in small_sends:
            cp.wait_send()
        for cp in small_local:
            cp.wait()

    hbm = pl.BlockSpec(memory_space=pl.ANY)
    vmem = pl.BlockSpec(memory_space=pltpu.VMEM)
    in_blk = parts_w_in.shape[1:]
    return pl.pallas_call(
        body, name="reduce_grads",
        in_specs=[hbm] * 3, out_specs=[vmem, hbm, hbm],
        out_shape=[jax.ShapeDtypeStruct(in_blk, F32),
                   jax.ShapeDtypeStruct((N_DEV,) + parts_wg.shape[1:], F32),
                   jax.ShapeDtypeStruct((N_DEV,) + small.shape, F32)],
        scratch_shapes=_OwnerSum.scratch(in_blk)
        + [pltpu.SemaphoreType.DMA((2 * (N_DEV - 1),)), pltpu.SemaphoreType.DMA((2 * (N_DEV - 1),)),
           pltpu.SemaphoreType.DMA((2,))],
        compiler_params=_cparams(),
    )(parts_w_in, parts_wg, small)


def _adamw(recv, w, m, v, name):
    rows, width = w.shape
    tr = 128 if rows % 128 == 0 else rows
    n_parts = recv.shape[0]

    def body(r_ref, w_ref, m_ref, v_ref, g_ref, d_ref, nm_ref, nv_ref):
        g = r_ref[0]
        for j in range(1, n_parts):
            g = g + r_ref[j]
        nm = ADAM_B1 * m_ref[...] + (1.0 - ADAM_B1) * g
        nv = ADAM_B2 * v_ref[...] + (1.0 - ADAM_B2) * (g * g)
        m_hat = nm / (1.0 - ADAM_B1 ** ADAM_STEP)
        v_hat = nv / (1.0 - ADAM_B2 ** ADAM_STEP)
        g_ref[...] = g
        d_ref[...] = -ADAM_LR * (m_hat / (jnp.sqrt(v_hat) + ADAM_EPS) + ADAM_WD * w_ref[...])
        nm_ref[...] = nm
        nv_ref[...] = nv

    spec = _rows(tr, width)
    return pl.pallas_call(
        body, name=name, grid=(rows // tr,),
        in_specs=[pl.BlockSpec((n_parts, tr, width), lambda i: (0, i, 0)), spec, spec, spec],
        out_specs=[spec] * 4,
        out_shape=[jax.ShapeDtypeStruct((rows, width), F32)] * 4,
        compiler_params=_cparams(dimension_semantics=("arbitrary",)),
    )(recv, w, m, v)


def _adamw_shard_view(g, w, m, v):
    rows, width = g.shape

    def body(g_ref, w_hbm, m_hbm, v_hbm, g_out, d_out, nm_out, nv_out, bufs, outs, sems):
        loads = [pltpu.make_async_copy(src.at[:, 0, :], bufs.at[i], sems.at[i])
                 for i, src in enumerate((w_hbm, m_hbm, v_hbm))]
        for cp in loads:
            cp.start()
        g = g_ref[...]
        for cp in loads:
            cp.wait()
        nm = ADAM_B1 * bufs[1] + (1.0 - ADAM_B1) * g
        nv = ADAM_B2 * bufs[2] + (1.0 - ADAM_B2) * (g * g)
        m_hat = nm / (1.0 - ADAM_B1 ** ADAM_STEP)
        v_hat = nv / (1.0 - ADAM_B2 ** ADAM_STEP)
        outs[0] = g
        outs[1] = -ADAM_LR * (m_hat / (jnp.sqrt(v_hat) + ADAM_EPS) + ADAM_WD * bufs[0])
        outs[2] = nm
        outs[3] = nv
        stores = [pltpu.make_async_copy(outs.at[i], dst.at[:, 0, :], sems.at[3 + i])
                  for i, dst in enumerate((g_out, d_out, nm_out, nv_out))]
        for cp in stores:
            cp.start()
        for cp in stores:
            cp.wait()

    hbm = pl.BlockSpec(memory_space=pl.ANY)
    return pl.pallas_call(
        body, name="adamw_w_in",
        in_specs=[pl.BlockSpec(memory_space=pltpu.VMEM), hbm, hbm, hbm], out_specs=[hbm] * 4,
        out_shape=[jax.ShapeDtypeStruct((rows, 1, width), F32)] * 4,
        scratch_shapes=[pltpu.VMEM((3, rows, width), F32), pltpu.VMEM((4, rows, width), F32),
                        pltpu.SemaphoreType.DMA((7,))],
        compiler_params=_cparams(),
    )(g, w, m, v)


def _pack_small(ln_g, ln_b, b_gate, norm_w, sinks, loss=None):
    def rows8(t):
        t = t.reshape(-1)
        t = jnp.pad(t, (0, 1024 - t.shape[0]))
        return t.reshape(8, 128)

    loss = jnp.zeros((1,), F32) if loss is None else loss
    return jnp.concatenate([rows8(t) for t in (ln_g, ln_b, b_gate, norm_w, sinks, loss)], axis=0)


def _unpack_small(p):
    flat = [p[8 * i:8 * (i + 1)].reshape(1, 1024) for i in range(5)]
    return flat[0], flat[1], flat[2][:, :256], flat[3][:, :128], flat[4][:, :8]


def kernel(x, positions, w_in, gla_w_gate_up, gla_b_gate, attn_sinks, gla_norm_w, w_out, ln_g, ln_b, loss_target, m_w_in, m_gla_w_gate_up, m_gla_b_gate, m_attn_sinks, m_gla_norm_w, m_w_out, m_ln_g, m_ln_b, v_w_in, v_gla_w_gate_up, v_gla_b_gate, v_attn_sinks, v_gla_norm_w, v_w_out, v_ln_g, v_ln_b):
    w_in_full, wg_full = _all_gather_weights(_shard_view(w_in), gla_w_gate_up[0])

    loss, grad_x, parts_w_in, g_wg, g_bg, g_sinks, g_nw, g_out, g_ln = _local_step(
        x[0], positions[0], w_in_full, wg_full, gla_b_gate, attn_sinks[0], gla_norm_w, w_out[0], ln_g, ln_b,
        loss_target[0])

    parts_wg = jnp.transpose(g_wg.reshape(GLA_RANK, N_DEV, 32), (1, 0, 2))
    small = _pack_small(g_ln[0:1], g_ln[1:2], g_bg, g_nw, g_sinks[:, 0].reshape(1, SWA_Q_HEADS), loss[0, 0:1])
    g_in, r_wg, r_small = _reduce_grads(parts_w_in, parts_wg, small)

    upd_in = _adamw_shard_view(g_in, _shard_view(w_in), _shard_view(m_w_in), _shard_view(v_w_in))
    upd_in = [jnp.transpose(t, (1, 2, 0)) for t in upd_in]
    upd_out = _adamw(g_out[None], w_out[0], m_w_out[0], v_w_out[0], "adamw_w_out")
    upd_wg = _adamw(r_wg, gla_w_gate_up[0], m_gla_w_gate_up[0], v_gla_w_gate_up[0], "adamw_wg")
    upd_small = _adamw(
        r_small,
        _pack_small(ln_g, ln_b, gla_b_gate, gla_norm_w, attn_sinks),
        _pack_small(m_ln_g, m_ln_b, m_gla_b_gate, m_gla_norm_w, m_attn_sinks),
        _pack_small(v_ln_g, v_ln_b, v_gla_b_gate, v_gla_norm_w, v_attn_sinks), "adamw_small")

    total = jnp.sum(r_small[:, 40, 0])
    outs = [total, grad_x[None]]
    for kind in range(4):
        s_ln_g, s_ln_b, s_bg, s_nw, s_sinks = _unpack_small(upd_small[kind])
        outs += [upd_in[kind], upd_wg[kind][None], s_bg, s_sinks, s_nw, upd_out[kind][None], s_ln_g, s_ln_b]
    return tuple(outs)
```

```python
import jax
import jax.numpy as jnp
from jax import lax
from jax.experimental import pallas as pl
from jax.experimental.pallas import tpu as pltpu

F32 = jnp.float32
MXU_DTYPE = jnp.bfloat16

N_DEV = 8
D_MODEL = 1024
SWA_Q_HEADS = 8
SWA_KV_HEADS = 2
SWA_GROUP = 4
SWA_HEAD_DIM = 64
BLOCK = 128
ROPE_THETA = 500000.0
ROT_DIM = 16
GLA_HEADS = 4
GLA_DK = 64
GLA_DV = 128
GLA_RANK = 16
GLA_TAU = 16.0
GLA_CHUNK = 64
D_IN_PROJ = 2832
D_IN_SHARD = D_IN_PROJ // N_DEV
D_OUT_SHARD = D_MODEL // N_DEV
OFF = (0, 512, 640, 768, 1280, 1536, 1792, 2304, 2816, 2832)
EPS = 1e-5
ALPHA = 2.0 ** 0.25
SWA_SCALE = SWA_HEAD_DIM ** -0.5
GLA_SCALE = GLA_DK ** -0.5
ADAM_LR = 0.001
ADAM_B1 = 0.9
ADAM_B2 = 0.999
ADAM_EPS = 1e-08
ADAM_WD = 0.01
ADAM_STEP = 10
VMEM_LIMIT = 56 * 1024 * 1024

_NT = (((1,), (1,)), ((), ()))
_TN = (((0,), (0,)), ((), ()))


def _mm(a, b):
    return jnp.dot(a, b, preferred_element_type=F32)


def _mm_nt(a, b):
    return lax.dot_general(a, b, _NT, preferred_element_type=F32)


def _mm_tn(a, b):
    return lax.dot_general(a, b, _TN, preferred_element_type=F32)


def _sigmoid(t):
    return 1.0 / (1.0 + jnp.exp(-t))


def _cparams(**kw):
    return pltpu.CompilerParams(vmem_limit_bytes=VMEM_LIMIT, **kw)


def _full(shape):
    return pl.BlockSpec(shape, lambda *_: (0,) * len(shape))


def _rows(tile, width):
    return pl.BlockSpec((tile, width), lambda i: (i, 0))


def _rope_tables(positions):
    half = ROT_DIM // 2
    inv_freq = ROPE_THETA ** (-jnp.arange(half, dtype=F32) / half)
    lane = jnp.arange(128, dtype=jnp.int32) % SWA_HEAD_DIM
    ang = positions.astype(F32)[:, None] * inv_freq[None, :]
    cos, sin = lax.optimization_barrier((jnp.cos(ang), jnp.sin(ang)))
    cos, sin = jnp.tile(cos, (1, 128 // half)), jnp.tile(sin, (1, 128 // half))
    c = jnp.where(lane < ROT_DIM, cos, 1.0)
    s1 = jnp.where(lane < half, -sin, 0.0)
    s2 = jnp.where((lane >= half) & (lane < ROT_DIM), sin, 0.0)
    return c, s1, s2


def _rope(t, c, s1, s2):
    return t * c + pltpu.roll(t, 120, 1) * s1 + pltpu.roll(t, 8, 1) * s2


def _rope_t(g, c, s1, s2):
    return g * c + pltpu.roll(g * s1, 8, 1) + pltpu.roll(g * s2, 120, 1)


def _in_proj(x, w_in, wg, b_gate, rope, w_out_s):
    s = x.shape[0]
    ts = min(512, s)
    nsteps = s // ts
    forward_step = min(3, nsteps - 1)
    widths = [OFF[i + 1] - OFF[i] for i in range(9)]

    def body(x_ref, w_ref, wg_ref, bg_ref, c_ref, s1_ref, s2_ref, wos_ref,
             qa_ref, ka_ref, va_ref, ga_ref, qb_ref, kb_ref, vb_ref, gb_ref, rb_ref, la_ref, oms_ref, wout_ref,
             wout_all, send_sems, recv_sems):
        xb = x_ref[...].astype(MXU_DTYPE)
        c, s1, s2 = c_ref[...], s1_ref[...], s2_ref[...]
        i0 = pl.program_id(0)
        gather = _BlockGather(wout_all, send_sems, recv_sems)

        @pl.when(i0 == 0)
        def _():
            ka_ref[0:BLOCK, :] = jnp.zeros((BLOCK, 128), ka_ref.dtype)
            va_ref[0:BLOCK, :] = jnp.zeros((BLOCK, 128), va_ref.dtype)
            wout_all[gather.me] = wos_ref[...].astype(wout_all.dtype)
            gather.start()

        @pl.when(i0 == forward_step)
        def _():
            gather.forward()

        @pl.when(i0 == nsteps - 1)
        def _():
            gather.finish()
            for j in range(N_DEV):
                wout_ref[D_OUT_SHARD * j:D_OUT_SHARD * (j + 1), :] = wout_all[j]

        kv_rows = pl.ds(pl.multiple_of(BLOCK + i0 * ts, BLOCK), ts)

        def cols(i):
            return _mm_nt(xb, w_ref[OFF[i]:OFF[i + 1], :])

        qa = cols(0)
        for i in range(4):
            qa_ref[:, 128 * i:128 * (i + 1)] = _rope(qa[:, 128 * i:128 * (i + 1)], c, s1, s2).astype(qa_ref.dtype)
        kv = _mm_nt(xb, w_ref[OFF[1]:OFF[3], :])
        ka_ref[kv_rows, :] = _rope(kv[:, 0:128], c, s1, s2).astype(ka_ref.dtype)
        va_ref[kv_rows, :] = kv[:, 128:256].astype(va_ref.dtype)
        ga_ref[...] = cols(3)
        qb_ref[...] = cols(4)
        kb_ref[...] = cols(5)
        vb_ref[...] = cols(6).astype(vb_ref.dtype)
        gb_ref[...] = cols(7)
        rb = cols(8)
        rb_ref[...] = rb
        logit = _mm(rb.astype(MXU_DTYPE), wg_ref[...]) + bg_ref[...]
        e = jnp.exp(-jnp.abs(logit))
        la_ref[...] = (jnp.minimum(logit, 0.0) - jnp.log(1.0 + e)) / GLA_TAU
        oms_ref[...] = jnp.where(logit >= 0.0, e, 1.0) / (1.0 + e)

    out_shape = [jax.ShapeDtypeStruct((s + BLOCK if i in (1, 2) else s, w), MXU_DTYPE if i in (0, 1, 2, 6) else F32)
                 for i, w in enumerate(widths)]
    out_shape += [jax.ShapeDtypeStruct((s, 256), F32)] * 2
    out_shape += [jax.ShapeDtypeStruct((D_MODEL, D_MODEL), MXU_DTYPE)]
    return pl.pallas_call(
        body, name="in_proj", grid=(nsteps,),
        in_specs=[_rows(ts, D_MODEL), _full((D_IN_PROJ, D_MODEL)), _full((GLA_RANK, 256)), _full((1, 256)),
                  _rows(ts, 128), _rows(ts, 128), _rows(ts, 128), _full((D_OUT_SHARD, D_MODEL))],
        out_specs=[_full((s + BLOCK, w)) if i in (1, 2) else _rows(ts, w) for i, w in enumerate(widths)]
        + [_rows(ts, 256)] * 2 + [_full((D_MODEL, D_MODEL))],
        out_shape=out_shape,
        scratch_shapes=[pltpu.VMEM((N_DEV, D_OUT_SHARD, D_MODEL), MXU_DTYPE)] + _BlockGather.scratch(),
        compiler_params=_cparams(dimension_semantics=("arbitrary",)),
    )(x, w_in, wg, b_gate, *rope, w_out_s)


SWA_ROWS = SWA_GROUP * BLOCK


def _swa_bias():
    shape = (2, 2 * BLOCK, SWA_ROWS)
    ki = lax.broadcasted_iota(jnp.int32, shape, 1)
    qi = lax.broadcasted_iota(jnp.int32, shape, 2) & (BLOCK - 1)
    first = lax.broadcasted_iota(jnp.int32, shape, 0) == 0
    dist = qi + BLOCK - ki
    ok = (dist >= 0) & (dist < BLOCK) & (jnp.logical_not(first) | (ki >= BLOCK))
    return jnp.where(ok, 0.0, -jnp.inf).astype(F32)


SWA_SUB = 2


def _swa_bias_of(bias_ref, n, b):
    return bias_ref[jnp.minimum(n, 1)] if b == 0 else bias_ref[1]


def _swa_dup(t, j):
    t = t.astype(F32)
    low = lax.broadcasted_iota(jnp.int32, t.shape, 1) < SWA_HEAD_DIM
    keep = low if j == 0 else jnp.logical_not(low)
    return jnp.where(keep, t, pltpu.roll(t, SWA_HEAD_DIM, 1)).astype(MXU_DTYPE)


def _swa_stack(t, j):
    low = lax.broadcasted_iota(jnp.int32, (BLOCK, 128), 1) < SWA_HEAD_DIM
    zero = jnp.zeros((BLOCK, 128), t.dtype)
    blocks = []
    for p in (2 * j, 2 * j + 1):
        tp = t[:, 128 * p:128 * (p + 1)]
        blocks += [jnp.where(low, tp, zero), jnp.where(low, zero, tp)]
    return jnp.concatenate(blocks, axis=0)


def _swa_unstack(t):
    low = lax.broadcasted_iota(jnp.int32, (BLOCK, 128), 1) < SWA_HEAD_DIM
    return [jnp.where(low, t[2 * BLOCK * i:2 * BLOCK * i + BLOCK], t[2 * BLOCK * i + BLOCK:2 * BLOCK * (i + 1)])
            for i in range(2)]


def _swa_sink_row(sink_ref, j):
    lane = lax.broadcasted_iota(jnp.int32, (1, SWA_ROWS), 1)
    row = jnp.full((1, SWA_ROWS), sink_ref[SWA_GROUP * j], F32)
    for r in range(1, SWA_GROUP):
        row = jnp.where(lane >= BLOCK * r, sink_ref[SWA_GROUP * j + r], row)
    return row


def _split3(t):
    hi = t.astype(MXU_DTYPE)
    r1 = t - hi.astype(F32)
    mid = r1.astype(MXU_DTYPE)
    lo = (r1 - mid.astype(F32)).astype(MXU_DTYPE)
    return jnp.concatenate([hi, mid, lo], axis=1)


def _row_sums_as_row(t):
    ones = jnp.ones((8, 3 * t.shape[1]), MXU_DTYPE)
    return _mm_nt(ones, _split3(t))[0:1, :]


def _swa_probs_t(qs, kd, bias_t, sink):
    sc = _mm_nt(kd, qs) + bias_t
    m = jnp.maximum(jnp.max(sc, axis=0, keepdims=True), sink)
    p = jnp.exp(sc - m)
    ps = jnp.exp(sink - m)
    rinv = 1.0 / (jnp.sum(p, axis=0, keepdims=True) + ps)
    return p * rinv, ps * rinv


def _swa_fwd(sinks, qa, k_pad, v_pad, ga):
    s = qa.shape[0]
    tq = SWA_SUB * BLOCK

    def body(sink_ref, qa_ref, ga_ref, bias_ref, k_ref, v_ref, attn_ref, cat_ref):
        n = pl.program_id(0)
        for b in range(SWA_SUB):
            rows = slice(BLOCK * b, BLOCK * (b + 1))
            start = pl.multiple_of((n * SWA_SUB + b) * BLOCK, BLOCK)
            kw = k_ref[pl.ds(start, 2 * BLOCK), :]
            vw = v_ref[pl.ds(start, 2 * BLOCK), :]
            bias_t = _swa_bias_of(bias_ref, n, b)
            q = qa_ref[rows, :] * SWA_SCALE
            g = ga_ref[rows, :]
            silu = g * _sigmoid(g)
            for j in range(SWA_KV_HEADS):
                qs = _swa_stack(q, j).astype(MXU_DTYPE)
                probs, _ = _swa_probs_t(qs, _swa_dup(kw, j), bias_t, _swa_sink_row(sink_ref, j))
                pairs = _swa_unstack(_mm_tn(probs.astype(MXU_DTYPE), _swa_dup(vw, j)))
                for i in range(2):
                    lanes = slice(128 * (2 * j + i), 128 * (2 * j + i + 1))
                    attn_ref[rows, lanes] = pairs[i]
                    cat_ref[rows, lanes] = (pairs[i] * silu[:, lanes]).astype(cat_ref.dtype)

    return pl.pallas_call(
        body, name="swa_fwd", grid=(s // tq,),
        in_specs=[pl.BlockSpec(memory_space=pltpu.SMEM), _rows(tq, 512), _rows(tq, 512),
                  _full((2, 2 * BLOCK, SWA_ROWS)), _full((s + BLOCK, 128)), _full((s + BLOCK, 128))],
        out_specs=[_rows(tq, 512), _rows(tq, 512)],
        out_shape=[jax.ShapeDtypeStruct((s, 512), F32), jax.ShapeDtypeStruct((s, 512), MXU_DTYPE)],
        compiler_params=_cparams(dimension_semantics=("arbitrary",)),
    )(sinks, qa, ga, _swa_bias(), k_pad, v_pad)


GLA_KW = GLA_HEADS * GLA_DK
GLA_VW = GLA_HEADS * GLA_DV


def _idiv(t, d):
    return t >> (d.bit_length() - 1)


def _chunk_cumsum(t, lower):
    n, w = t.shape
    r = lax.broadcasted_iota(jnp.int32, (n, n), 0)
    c = lax.broadcasted_iota(jnp.int32, (n, n), 1)
    tri = ((_idiv(r, GLA_CHUNK) == _idiv(c, GLA_CHUNK)) & ((r >= c) if lower else (r <= c))).astype(MXU_DTYPE)
    parts = _mm(tri, _split3(t))
    return (parts[:, :w] + parts[:, w:2 * w]) + parts[:, 2 * w:]


def _chunk_last(t):
    n = t.shape[0]
    return jnp.concatenate(
        [jnp.broadcast_to(t[c + GLA_CHUNK - 1:c + GLA_CHUNK, :], (GLA_CHUNK, t.shape[1]))
         for c in range(0, n, GLA_CHUNK)], axis=0)


def _head_stack(t, width):
    head = _idiv(lax.broadcasted_iota(jnp.int32, t.shape, 1), width)
    zero = jnp.zeros_like(t)
    return jnp.concatenate([jnp.where(head == h, t, zero) for h in range(GLA_HEADS)], axis=0)


def _gla_masks():
    row = lax.broadcasted_iota(jnp.int32, (GLA_CHUNK, GLA_KW), 0)
    pos = lax.broadcasted_iota(jnp.int32, (GLA_CHUNK, GLA_KW), 1) & (GLA_CHUNK - 1)
    srow = _idiv(lax.broadcasted_iota(jnp.int32, (GLA_VW, GLA_KW), 0), GLA_DV)
    slane = _idiv(lax.broadcasted_iota(jnp.int32, (GLA_VW, GLA_KW), 1), GLA_DK)
    return pos <= row, pos >= row, srow == slane


def _gla_fwd(qb, kb, vb, la, gb, norm_w):
    s = qb.shape[0]
    tb = min(256, s)
    ch = tb // GLA_CHUNK

    def body(qb_ref, kb_ref, vb_ref, la_ref, gb_ref, nw_ref, o_ref, cat_ref, sp_ref, st_ref):
        @pl.when(pl.program_id(0) == 0)
        def _():
            st_ref[...] = jnp.zeros_like(st_ref)

        causal, _, same_head = _gla_masks()
        nw = nw_ref[...]
        b = _chunk_cumsum(la_ref[...], True)
        bl = _chunk_last(b)
        k = kb_ref[...]
        qd = ((qb_ref[...] * GLA_SCALE) * jnp.exp(b)).astype(MXU_DTYPE)
        ki = (k * jnp.exp(-b)).astype(MXU_DTYPE)
        ke = (k * jnp.exp(bl - b)).astype(MXU_DTYPE)
        dec = jnp.exp(bl)
        v = vb_ref[...].astype(MXU_DTYPE)
        g = gb_ref[...]
        silu = g * _sigmoid(g)
        for ci in range(ch):
            rows = slice(GLA_CHUNK * ci, GLA_CHUNK * (ci + 1))
            a = jnp.where(causal, _mm_nt(qd[rows], _head_stack(ki[rows], GLA_DK)), 0.0).astype(MXU_DTYPE)
            st = st_ref[...]
            sp_ref[ci] = (st[0:GLA_DV] + st[GLA_DV:2 * GLA_DV]) + (st[2 * GLA_DV:3 * GLA_DV] + st[3 * GLA_DV:])
            o = _mm(a, _head_stack(v[rows], GLA_DV)) + _mm_nt(qd[rows], st.astype(MXU_DTYPE))
            st_ref[...] = st * dec[rows][0:1] + jnp.where(same_head, _mm_tn(v[rows], ke[rows]), 0.0)
            o_ref[rows, :] = o
            for h in range(GLA_HEADS):
                lv = slice(GLA_DV * h, GLA_DV * (h + 1))
                oh = o[:, lv]
                r = lax.rsqrt(jnp.mean(oh * oh, axis=1, keepdims=True) + EPS)
                cat_ref[rows, lv] = (oh * r * nw * silu[rows, lv]).astype(cat_ref.dtype)

    return pl.pallas_call(
        body, name="gla_fwd", grid=(s // tb,),
        in_specs=[_rows(tb, 256), _rows(tb, 256), _rows(tb, 512), _rows(tb, 256), _rows(tb, 512), _full((1, 128))],
        out_specs=[_rows(tb, 512), _rows(tb, 512), pl.BlockSpec((ch, GLA_DV, 256), lambda i: (i, 0, 0))],
        out_shape=[jax.ShapeDtypeStruct((s, 512), F32), jax.ShapeDtypeStruct((s, 512), MXU_DTYPE),
                   jax.ShapeDtypeStruct((s // GLA_CHUNK, GLA_DV, 256), F32)],
        scratch_shapes=[pltpu.VMEM((GLA_VW, GLA_KW), F32)],
        compiler_params=_cparams(dimension_semantics=("arbitrary",)),
    )(qb, kb, vb, la, gb, norm_w)


def _out_ln_loss(cat_a, cat_b, w_out, x, target, ln_g, ln_b):
    s = x.shape[0]
    ts = min(512, s)
    halves = 2 if ts % 32 == 0 else 1
    th = ts // halves

    def body(ca_ref, cb_ref, w_ref, x_ref, t_ref, g_ref, b_ref,
             loss_ref, gx_ref, da_ref, db_ref, gw_ref, gln_ref):
        @pl.when(pl.program_id(0) == 0)
        def _():
            loss_ref[...] = jnp.zeros_like(loss_ref)
            gw_ref[...] = jnp.zeros_like(gw_ref)
            gln_ref[...] = jnp.zeros_like(gln_ref)

        g = g_ref[...]
        dh16s = []
        for k in range(halves):
            rows = slice(th * k, th * (k + 1))
            mix = _mm(ca_ref[rows, :], w_ref[0:512, :]) + _mm(cb_ref[rows, :], w_ref[512:1024, :])
            h = ALPHA * x_ref[rows, :] + mix
            mu = jnp.mean(h, axis=1, keepdims=True)
            hc = h - mu
            rstd = lax.rsqrt(jnp.mean(hc * hc, axis=1, keepdims=True) + EPS)
            xhat = hc * rstd
            err = xhat * g + b_ref[...] - t_ref[rows, :]
            loss_ref[...] += 0.5 * jnp.sum(jnp.mean(err * err, axis=1, keepdims=True))
            dy = err * (1.0 / D_MODEL)
            gln_ref[0:1, :] += jnp.sum(dy * xhat, axis=0, keepdims=True)
            gln_ref[1:2, :] += jnp.sum(dy, axis=0, keepdims=True)
            dxh = dy * g
            dh = rstd * (dxh - jnp.mean(dxh, axis=1, keepdims=True)
                         - xhat * jnp.mean(dxh * xhat, axis=1, keepdims=True))
            gx_ref[rows, :] = ALPHA * dh
            dh16s.append(dh.astype(MXU_DTYPE))
        for k in range(halves):
            rows = slice(th * k, th * (k + 1))
            da_ref[rows, :] = _mm_nt(dh16s[k], w_ref[0:512, :])
            db_ref[rows, :] = _mm_nt(dh16s[k], w_ref[512:1024, :])
        dh16 = jnp.concatenate(dh16s, axis=0)
        gw_ref[0:512, :] += _mm_tn(ca_ref[...], dh16)
        gw_ref[512:1024, :] += _mm_tn(cb_ref[...], dh16)

    return pl.pallas_call(
        body, name="out_ln_loss", grid=(s // ts,),
        in_specs=[_rows(ts, 512), _rows(ts, 512), _full((D_MODEL, D_MODEL)), _rows(ts, D_MODEL), _rows(ts, D_MODEL),
                  _full((1, D_MODEL)), _full((1, D_MODEL))],
        out_specs=[_full((1, 128)), _rows(ts, D_MODEL), _rows(ts, 512), _rows(ts, 512),
                   _full((D_MODEL, D_MODEL)), _full((2, D_MODEL))],
        out_shape=[jax.ShapeDtypeStruct((1, 128), F32), jax.ShapeDtypeStruct((s, D_MODEL), F32),
                   jax.ShapeDtypeStruct((s, 512), F32), jax.ShapeDtypeStruct((s, 512), F32),
                   jax.ShapeDtypeStruct((D_MODEL, D_MODEL), F32), jax.ShapeDtypeStruct((2, D_MODEL), F32)],
        compiler_params=_cparams(dimension_semantics=("arbitrary",)),
    )(cat_a, cat_b, w_out, x, target, ln_g, ln_b)


def _swa_bwd(sinks, qa, k_pad, v_pad, attn, ga, d_cat_a, rope, parts_w_out):
    s = qa.shape[0]
    tq = SWA_SUB * BLOCK
    nsteps = s // tq
    forward_step = min(2, nsteps - 1)

    def body(sink_ref, qa_ref, ga_ref, at_ref, dc_ref, c_ref, s1_ref, s2_ref, bias_ref, k_ref, v_ref, pout_ref,
             dq_ref, dg_ref, dk_ref, dv_ref, ds_ref, gout_ref, *scratch):
        n = pl.program_id(0)
        owner_sum = _OwnerSum(pout_ref, *scratch)

        @pl.when(n == 0)
        def _():
            dk_ref[...] = jnp.zeros_like(dk_ref)
            dv_ref[...] = jnp.zeros_like(dv_ref)
            ds_ref[...] = jnp.zeros_like(ds_ref)
            owner_sum.start()

        @pl.when(n == forward_step)
        def _():
            owner_sum.forward()

        @pl.when(n == nsteps - 1)
        def _():
            gout_ref[...] = owner_sum.finish()

        low = lax.broadcasted_iota(jnp.int32, (2 * BLOCK, 128), 1) < SWA_HEAD_DIM
        for b in range(SWA_SUB):
            rows = slice(BLOCK * b, BLOCK * (b + 1))
            start = pl.multiple_of((n * SWA_SUB + b) * BLOCK, BLOCK)
            kw = k_ref[pl.ds(start, 2 * BLOCK), :]
            vw = v_ref[pl.ds(start, 2 * BLOCK), :]
            bias_t = _swa_bias_of(bias_ref, n, b)
            q = qa_ref[rows, :] * SWA_SCALE
            g = ga_ref[rows, :]
            sg = _sigmoid(g)
            o = at_ref[rows, :]
            dc = dc_ref[rows, :]
            do = dc * (g * sg)
            dg_ref[rows, :] = (dc * o * (sg * (1.0 + g * (1.0 - sg)))).astype(dg_ref.dtype)
            od = do * o
            c, s1, s2 = c_ref[rows, :], s1_ref[rows, :], s2_ref[rows, :]
            dk, dv = [], []
            for j in range(SWA_KV_HEADS):
                kd, vd = _swa_dup(kw, j), _swa_dup(vw, j)
                qs = _swa_stack(q, j).astype(MXU_DTYPE)
                dos = _swa_stack(do, j).astype(MXU_DTYPE)
                probs, psink = _swa_probs_t(qs, kd, bias_t, _swa_sink_row(sink_ref, j))
                delta = _row_sums_as_row(_swa_stack(od, j))
                dsc = (probs * (_mm_nt(vd, dos) - delta)).astype(MXU_DTYPE)
                dsink = psink * delta
                for r in range(SWA_GROUP):
                    h = SWA_GROUP * j + r
                    ds_ref[h:h + 1, :] += jnp.zeros((1, 128), F32) - jnp.sum(dsink[:, BLOCK * r:BLOCK * (r + 1)])
                dq = _swa_unstack(_mm_tn(dsc, kd))
                for i in range(2):
                    lanes = slice(128 * (2 * j + i), 128 * (2 * j + i + 1))
                    dq_ref[rows, lanes] = _rope_t(dq[i] * SWA_SCALE, c, s1, s2).astype(dq_ref.dtype)
                dkj = _mm(dsc, qs)
                dvj = _mm(probs.astype(MXU_DTYPE), dos)
                dk.append(dkj + pltpu.roll(dkj, SWA_HEAD_DIM, 1))
                dv.append(dvj + pltpu.roll(dvj, SWA_HEAD_DIM, 1))
            dk_ref[pl.ds(start, 2 * BLOCK), :] += jnp.where(low, dk[0], dk[1])
            dv_ref[pl.ds(start, 2 * BLOCK), :] += jnp.where(low, dv[0], dv[1])

    out_blk = parts_w_out.shape[1:]
    return pl.pallas_call(
        body, name="swa_bwd", grid=(nsteps,),
        in_specs=[pl.BlockSpec(memory_space=pltpu.SMEM)] + [_rows(tq, 512)] * 4 + [_rows(tq, 128)] * 3
        + [_full((2, 2 * BLOCK, SWA_ROWS))] + [_full((s + BLOCK, 128))] * 2 + [pl.BlockSpec(memory_space=pl.ANY)],
        out_specs=[_rows(tq, 512), _rows(tq, 512), _full((s + BLOCK, 128)), _full((s + BLOCK, 128)),
                   _full((SWA_Q_HEADS, 128)), _full(out_blk)],
        out_shape=[jax.ShapeDtypeStruct((s, 512), MXU_DTYPE), jax.ShapeDtypeStruct((s, 512), MXU_DTYPE),
                   jax.ShapeDtypeStruct((s + BLOCK, 128), F32), jax.ShapeDtypeStruct((s + BLOCK, 128), F32),
                   jax.ShapeDtypeStruct((SWA_Q_HEADS, 128), F32), jax.ShapeDtypeStruct(out_blk, F32)],
        scratch_shapes=_OwnerSum.scratch(out_blk),
        compiler_params=_cparams(dimension_semantics=("arbitrary",)),
    )(sinks, qa, ga, attn, d_cat_a, *rope, _swa_bias(), k_pad, v_pad, parts_w_out)


def _gla_bwd(qb, kb, vb, la, oms, gb, o, sprev, d_cat_b, rb, wg, norm_w):
    s = qb.shape[0]
    tb = min(512, s)
    ch = tb // GLA_CHUNK
    nb = s // tb

    def body(qb_ref, kb_ref, vb_ref, la_ref, oms_ref, gb_ref, o_ref, sp_ref, dc_ref, rb_ref, wg_ref, nw_ref,
             dq_ref, dk_ref, dv_ref, dg_ref, dr_ref, gwg_ref, gbg_ref, gnw_ref, dst_ref):
        @pl.when(pl.program_id(0) == 0)
        def _():
            dst_ref[...] = jnp.zeros_like(dst_ref)
            gwg_ref[...] = jnp.zeros_like(gwg_ref)
            gbg_ref[...] = jnp.zeros_like(gbg_ref)
            gnw_ref[...] = jnp.zeros_like(gnw_ref)

        causal, causal_t, same_head = _gla_masks()
        nw = nw_ref[...]
        b = _chunk_cumsum(la_ref[...], True)
        bl = _chunk_last(b)
        eb, enb, ee, dec = jnp.exp(b), jnp.exp(-b), jnp.exp(bl - b), jnp.exp(bl)
        k = kb_ref[...]
        qd = (qb_ref[...] * GLA_SCALE) * eb
        ki = k * enb
        ke = k * ee
        qd16, ki16, ke16 = qd.astype(MXU_DTYPE), ki.astype(MXU_DTYPE), ke.astype(MXU_DTYPE)
        v16 = vb_ref[...].astype(MXU_DTYPE)

        g = gb_ref[...]
        sg = _sigmoid(g)
        silu = g * sg
        dsilu = sg * (1.0 + g * (1.0 - sg))
        gnw = jnp.zeros((1, GLA_DV), F32)
        do = []
        for h in range(GLA_HEADS):
            lv = slice(GLA_DV * h, GLA_DV * (h + 1))
            oh = o_ref[:, lv]
            dch = dc_ref[:, lv]
            r = lax.rsqrt(jnp.mean(oh * oh, axis=1, keepdims=True) + EPS)
            d_on = dch * silu[:, lv]
            dg_ref[:, lv] = (dch * (oh * r * nw) * dsilu[:, lv]).astype(dg_ref.dtype)
            gnw += jnp.sum(d_on * oh * r, axis=0, keepdims=True)
            u = d_on * nw
            do.append(r * u - oh * (r * r * r) * jnp.mean(u * oh, axis=1, keepdims=True))
        gnw_ref[...] += gnw
        do16 = jnp.concatenate(do, axis=1).astype(MXU_DTYPE)

        db, dbl = [None] * ch, [None] * ch
        for ci in reversed(range(ch)):
            rows = slice(GLA_CHUNK * ci, GLA_CHUNK * (ci + 1))
            qds, kis = _head_stack(qd16[rows], GLA_DK), _head_stack(ki16[rows], GLA_DK)
            vs, dos = _head_stack(v16[rows], GLA_DV), _head_stack(do16[rows], GLA_DV)
            a = jnp.where(causal, _mm_nt(qd16[rows], kis), 0.0).astype(MXU_DTYPE)
            at = jnp.where(causal_t, _mm_nt(ki16[rows], qds), 0.0).astype(MXU_DTYPE)
            da = jnp.where(causal, _mm_nt(do16[rows], vs), 0.0).astype(MXU_DTYPE)
            dat = jnp.where(causal_t, _mm_nt(v16[rows], dos), 0.0).astype(MXU_DTYPE)
            stc = sp_ref[ci]
            st = jnp.where(same_head, jnp.concatenate([stc] * GLA_HEADS, axis=0), 0.0)
            st16 = st.astype(MXU_DTYPE)
            dst = dst_ref[...]
            dst16 = dst.astype(MXU_DTYPE)
            dv = _mm(at, dos) + _mm_nt(ke16[rows], dst16)
            dqd = _mm(da, kis) + _mm(do16[rows], st16)
            dki = _mm(dat, qds)
            dke = _mm(v16[rows], dst16)
            ddec = jnp.sum(dst * st, axis=0, keepdims=True)
            decc = dec[rows][0:1]
            dst_ref[...] = jnp.where(same_head, _mm_tn(do16[rows], qd16[rows]), 0.0) + dst * decc
            dq_ref[rows, :] = (dqd * eb[rows] * GLA_SCALE).astype(dq_ref.dtype)
            dk_ref[rows, :] = (dki * enb[rows] + dke * ee[rows]).astype(dk_ref.dtype)
            dv_ref[rows, :] = dv.astype(dv_ref.dtype)
            dke_ke = dke * ke[rows]
            db[ci] = dqd * qd[rows] - dki * ki[rows] - dke_ke
            dbl[ci] = jnp.broadcast_to(jnp.sum(dke_ke, axis=0, keepdims=True) + ddec * decc, (GLA_CHUNK, GLA_KW))

        dla = _chunk_cumsum(jnp.concatenate(db, axis=0), False) + jnp.concatenate(dbl, axis=0)
        dlogit = dla * oms_ref[...] * (1.0 / GLA_TAU)
        dl16 = dlogit.astype(MXU_DTYPE)
        gbg_ref[...] += jnp.sum(dlogit, axis=0, keepdims=True)
        gwg_ref[...] += _mm_tn(rb_ref[...].astype(MXU_DTYPE), dl16)
        dr_ref[...] = _mm_nt(dl16, wg_ref[...]).astype(dr_ref.dtype)

    def rev(width):
        return pl.BlockSpec((tb, width), lambda i: (nb - 1 - i, 0))

    return pl.pallas_call(
        body, name="gla_bwd", grid=(nb,),
        in_specs=[rev(256), rev(256), rev(512), rev(256), rev(256), rev(512), rev(512),
                  pl.BlockSpec((ch, GLA_DV, 256), lambda i: (nb - 1 - i, 0, 0)), rev(512), rev(GLA_RANK),
                  _full((GLA_RANK, 256)), _full((1, 128))],
        out_specs=[rev(256), rev(256), rev(512), rev(512), rev(GLA_RANK),
                   _full((GLA_RANK, 256)), _full((1, 256)), _full((1, 128))],
        out_shape=[jax.ShapeDtypeStruct((s, 256), MXU_DTYPE), jax.ShapeDtypeStruct((s, 256), MXU_DTYPE),
                   jax.ShapeDtypeStruct((s, 512), MXU_DTYPE), jax.ShapeDtypeStruct((s, 512), MXU_DTYPE),
                   jax.ShapeDtypeStruct((s, GLA_RANK), MXU_DTYPE), jax.ShapeDtypeStruct((GLA_RANK, 256), F32),
                   jax.ShapeDtypeStruct((1, 256), F32), jax.ShapeDtypeStruct((1, 128), F32)],
        scratch_shapes=[pltpu.VMEM((GLA_VW, GLA_KW), F32)],
        compiler_params=_cparams(dimension_semantics=("arbitrary",)),
    )(qb, kb, vb, la, oms, gb, o, sprev, d_cat_b, rb, wg, norm_w)


def _in_proj_bwd_x(gx0, pieces, w_in, rope):
    s = gx0.shape[0]
    ts = min(512, s)
    widths = [OFF[i + 1] - OFF[i] for i in range(9)]

    def body(gx0_ref, *refs):
        piece_refs = refs[:9]
        w_ref, c_ref, s1_ref, s2_ref, gx_ref, dp_ref = refs[9:]
        kv_rows = pl.ds(pl.multiple_of(BLOCK + pl.program_id(0) * ts, BLOCK), ts)
        acc = gx0_ref[...]
        for i in (0, 1, 3, 4, 5, 6, 7, 8):
            lo, hi = OFF[i], OFF[i + 1]
            if i == 1:
                dk = _rope_t(piece_refs[1][kv_rows, :], c_ref[...], s1_ref[...], s2_ref[...])
                t16 = jnp.concatenate([dk, piece_refs[2][kv_rows, :]], axis=1).astype(MXU_DTYPE)
                hi = OFF[3]
            else:
                t16 = piece_refs[i][...].astype(MXU_DTYPE)
            dp_ref[:, lo:hi] = t16
            acc += _mm(t16, w_ref[lo:hi, :])
        gx_ref[...] = acc

    return pl.pallas_call(
        body, name="in_proj_bwd_x", grid=(s // ts,),
        in_specs=[_rows(ts, D_MODEL)]
        + [_full((s + BLOCK, w)) if i in (1, 2) else _rows(ts, w) for i, w in enumerate(widths)]
        + [_full((D_IN_PROJ, D_MODEL))] + [_rows(ts, 128)] * 3,
        out_specs=[_rows(ts, D_MODEL), _rows(ts, D_IN_PROJ)],
        out_shape=[jax.ShapeDtypeStruct((s, D_MODEL), F32), jax.ShapeDtypeStruct((s, D_IN_PROJ), MXU_DTYPE)],
        compiler_params=_cparams(dimension_semantics=("arbitrary",)),
    )(gx0, *pieces, w_in, *rope)


def _in_proj_bwd_w(x, dproj):
    s = x.shape[0]
    ts = min(1024, s)
    nsteps = s // ts
    col_chunks = [(OFF[i], OFF[i + 2] if i == 1 else OFF[i + 1]) for i in (0, 1, 3, 4, 5, 6, 7, 8)]

    def body(x_ref, dp_ref, gw_ref, acc_ref, stage_ref, sems):
        i = pl.program_id(0)

        @pl.when(i == 0)
        def _():
            acc_ref[...] = jnp.zeros_like(acc_ref)

        xb = x_ref[...].astype(MXU_DTYPE)
        for lo, hi in col_chunks:
            acc_ref[lo:hi, :] += _mm_tn(dp_ref[:, lo:hi], xb)

        @pl.when(i == nsteps - 1)
        def _():
            copies = []
            for j in range(N_DEV):
                slot = j % 2
                if j >= 2:
                    copies[j - 2].wait()
                stage_ref[slot] = acc_ref[D_IN_SHARD * j:D_IN_SHARD * (j + 1), :]
                cp = pltpu.make_async_copy(stage_ref.at[slot], gw_ref.at[j], sems.at[slot])
                cp.start()
                copies.append(cp)
            copies[N_DEV - 2].wait()
            copies[N_DEV - 1].wait()

    return pl.pallas_call(
        body, name="in_proj_bwd_w", grid=(nsteps,),
        in_specs=[_rows(ts, D_MODEL), _rows(ts, D_IN_PROJ)],
        out_specs=pl.BlockSpec(memory_space=pl.ANY),
        out_shape=jax.ShapeDtypeStruct((N_DEV, D_IN_SHARD, D_MODEL), F32),
        scratch_shapes=[pltpu.VMEM((D_IN_PROJ, D_MODEL), F32), pltpu.VMEM((2, D_IN_SHARD, D_MODEL), F32),
                        pltpu.SemaphoreType.DMA((2,))],
        compiler_params=_cparams(dimension_semantics=("arbitrary",)),
    )(x, dproj)


def _local_step(x, positions, w_in, wg, b_gate, sinks, norm_w, w_out_s, ln_g, ln_b, target):
    rope = _rope_tables(positions)
    qa, k_pad, v_pad, ga, qb, kb, vb, gb, rb, la, oms, w_out = _in_proj(x, w_in, wg, b_gate, rope, w_out_s)
    attn, cat_a = _swa_fwd(sinks, qa, k_pad, v_pad, ga)
    o, cat_b, sprev = _gla_fwd(qb, kb, vb, la, gb, norm_w)
    loss, gx0, d_cat_a, d_cat_b, g_w_out, g_ln = _out_ln_loss(cat_a, cat_b, w_out, x, target, ln_g, ln_b)
    parts_w_out = g_w_out.reshape(N_DEV, D_OUT_SHARD, D_MODEL)
    dqa, dga, dk_pad, dv_pad, g_sinks, g_out = _swa_bwd(sinks, qa, k_pad, v_pad, attn, ga, d_cat_a, rope, parts_w_out)
    dqb, dkb, dvb, dgb, drb, g_wg, g_bg, g_nw = _gla_bwd(qb, kb, vb, la, oms, gb, o, sprev, d_cat_b, rb, wg, norm_w)
    pieces = (dqa, dk_pad, dv_pad, dga, dqb, dkb, dvb, dgb, drb)
    grad_x, dproj = _in_proj_bwd_x(gx0, pieces, w_in, rope)
    g_w_in = _in_proj_bwd_w(x, dproj)
    return loss, grad_x, g_w_in, g_wg, g_bg, g_sinks, g_nw, g_out, g_ln


def _mesh_pos():
    return lax.axis_index("x"), lax.axis_index("y"), lax.axis_index("c")


def _peer(k, x, y, c):
    px = (1 - x) if k & 4 else x
    py = (1 - y) if k & 2 else y
    pc = (1 - c) if k & 1 else c
    return (px, py, pc), 4 * px + 2 * py + pc


def _other_chips(x, y):
    return [(1 - x, y), (x, 1 - y), (1 - x, 1 - y)]


def _shard_view(t):
    return jnp.transpose(t, (2, 0, 1))


class _BlockGather:
    def __init__(self, slots, send_sems, recv_sems):
        self.slots, self.send_sems, self.recv_sems = slots, send_sems, recv_sems
        x, y, c = _mesh_pos()
        self.xy, self.c, self.me, self.sibling = (x, y), c, 4 * x + 2 * y + c, (x, y, 1 - c)
        self.chips = _other_chips(x, y)

    @staticmethod
    def scratch():
        return [pltpu.SemaphoreType.DMA((N_DEV - 1,)), pltpu.SemaphoreType.DMA((N_DEV - 1,))]

    def _copy(self, k, block, to):
        return pltpu.make_async_remote_copy(
            src_ref=self.slots.at[block], dst_ref=self.slots.at[block], send_sem=self.send_sems.at[k],
            recv_sem=self.recv_sems.at[k], device_id=to, device_id_type=pl.DeviceIdType.MESH)

    def start(self):
        for j, (cx, cy) in enumerate(self.chips):
            self._copy(1 + j, self.me, (cx, cy, self.c)).start()
        self._copy(0, self.me, self.sibling).start()

    def forward(self):
        for j, (cx, cy) in enumerate(self.chips):
            block = 4 * cx + 2 * cy + self.c
            self._copy(1 + j, block, self.sibling).wait_recv()
            self._copy(4 + j, block, self.sibling).start()

    def finish(self):
        x, y = self.xy
        self._copy(0, 4 * x + 2 * y + (1 - self.c), self.sibling).wait_recv()
        for j, (cx, cy) in enumerate(self.chips):
            self._copy(4 + j, 4 * cx + 2 * cy + (1 - self.c), self.sibling).wait_recv()
        for k in range(N_DEV - 1):
            self._copy(k, self.me, self.sibling).wait_send()


def _all_gather_weights(w_in_t, wg_s):
    def body(win_hbm, wg_ref, win_full, wg_full, win_all, wg_all, stage, stage_sem, *sems):
        gathers = (_BlockGather(win_all, *sems[0:2]), _BlockGather(wg_all, *sems[2:4]))
        me = gathers[0].me
        load = pltpu.make_async_copy(win_hbm.at[:, 0, :], stage, stage_sem)
        load.start()
        load.wait()
        win_all[me] = stage[...].astype(win_all.dtype)
        wg_all[me] = wg_ref[...].astype(wg_all.dtype)
        for stage_of in ("start", "forward", "finish"):
            for gather in gathers:
                getattr(gather, stage_of)()
        for j in range(N_DEV):
            win_full[D_IN_SHARD * j:D_IN_SHARD * (j + 1), :] = win_all[j]
            wg_full[:, 32 * j:32 * (j + 1)] = wg_all[j]

    vmem = pl.BlockSpec(memory_space=pltpu.VMEM)
    return pl.pallas_call(
        body, name="all_gather_weights",
        in_specs=[pl.BlockSpec(memory_space=pl.ANY), vmem], out_specs=[vmem] * 2,
        out_shape=[jax.ShapeDtypeStruct((D_IN_PROJ, D_MODEL), MXU_DTYPE),
                   jax.ShapeDtypeStruct((GLA_RANK, 256), MXU_DTYPE)],
        scratch_shapes=[pltpu.VMEM((N_DEV, D_IN_SHARD, D_MODEL), MXU_DTYPE),
                        pltpu.VMEM((N_DEV, GLA_RANK, 32), MXU_DTYPE),
                        pltpu.VMEM((D_IN_SHARD, D_MODEL), F32), pltpu.SemaphoreType.DMA]
        + _BlockGather.scratch() + _BlockGather.scratch(),
        compiler_params=_cparams(),
    )(w_in_t, wg_s)


class _OwnerSum:
    def __init__(self, parts, own, sib, snd, rcv, loc_sems, d2d_send, d2d_recv, ici_send, ici_recv):
        self.parts, self.own, self.sib, self.snd, self.rcv = parts, own, sib, snd, rcv
        self.sems = (loc_sems, d2d_send, d2d_recv, ici_send, ici_recv)
        x, y, c = _mesh_pos()
        self.c, self.sibling = c, (x, y, 1 - c)
        self.chips = [(x, y)] + _other_chips(x, y)

    @staticmethod
    def scratch(block):
        return [pltpu.VMEM((4,) + block, F32), pltpu.VMEM((4,) + block, F32),
                pltpu.VMEM((3,) + block, MXU_DTYPE), pltpu.VMEM((3,) + block, MXU_DTYPE),
                pltpu.SemaphoreType.DMA((4,)), pltpu.SemaphoreType.DMA((4,)), pltpu.SemaphoreType.DMA((4,)),
                pltpu.SemaphoreType.DMA((3,)), pltpu.SemaphoreType.DMA((3,))]

    def _local(self, r):
        cx, cy = self.chips[r]
        return pltpu.make_async_copy(self.parts.at[4 * cx + 2 * cy + self.c], self.own.at[r], self.sems[0].at[r])

    def _d2d(self, r):
        cx, cy = self.chips[r]
        return pltpu.make_async_remote_copy(
            src_ref=self.parts.at[4 * cx + 2 * cy + (1 - self.c)], dst_ref=self.sib.at[r], send_sem=self.sems[1].at[r],
            recv_sem=self.sems[2].at[r], device_id=self.sibling, device_id_type=pl.DeviceIdType.MESH)

    def _ici(self, r):
        cx, cy = self.chips[r]
        return pltpu.make_async_remote_copy(
            src_ref=self.snd.at[r - 1], dst_ref=self.rcv.at[r - 1], send_sem=self.sems[3].at[r - 1],
            recv_sem=self.sems[4].at[r - 1], device_id=(cx, cy, self.c), device_id_type=pl.DeviceIdType.MESH)

    def start(self):
        for r in (1, 2, 3, 0):
            self._local(r).start()
            self._d2d(r).start()

    def forward(self):
        for r in (1, 2, 3):
            self._local(r).wait()
            self._d2d(r).wait_recv()
            self.snd[r - 1] = (self.own[r] + self.sib[r]).astype(self.snd.dtype)
            self._ici(r).start()

    def finish(self):
        self._local(0).wait()
        self._d2d(0).wait_recv()
        acc = self.own[0] + self.sib[0]
        for r in (1, 2, 3):
            self._ici(r).wait_recv()
            acc = acc + self.rcv[r - 1].astype(F32)
        for r in range(4):
            self._d2d(r).wait_send()
        for r in (1, 2, 3):
            self._ici(r).wait_send()
        return acc


def _reduce_grads(parts_w_in, parts_wg, small):
    def body(pin_ref, pwg_ref, sm_ref, gin_ref, rwg_ref, rsm_ref, *scratch):
        sm_send, sm_recv, sm_loc = scratch[-3:]
        x, y, c = _mesh_pos()
        me = 4 * x + 2 * y + c
        owner_sum = _OwnerSum(pin_ref, *scratch[:-3])

        small_dsts = (rwg_ref, rsm_ref)

        def small_src(a, block):
            return pwg_ref.at[block] if a == 0 else sm_ref

        small_local = [pltpu.make_async_copy(small_src(a, me), small_dsts[a].at[me], sm_loc.at[a]) for a in range(2)]
        for cp in small_local:
            cp.start()
        small_sends = []
        for k in range(1, N_DEV):
            peer, pidx = _peer(k, x, y, c)
            for a in range(2):
                i = 2 * (k - 1) + a
                cp = pltpu.make_async_remote_copy(
                    src_ref=small_src(a, pidx), dst_ref=small_dsts[a].at[me], send_sem=sm_send.at[i],
                    recv_sem=sm_recv.at[i], device_id=peer, device_id_type=pl.DeviceIdType.MESH)
                cp.start()
                small_sends.append(cp)

        owner_sum.start()
        owner_sum.forward()
        gin_ref[...] = owner_sum.finish()

        for k in range(1, N_DEV):
            peer, pidx = _peer(k, x, y, c)
            for a in range(2):
                i = 2 * (k - 1) + a
                pltpu.make_async_remote_copy(
                    src_ref=small_src(a, me), dst_ref=small_dsts[a].at[pidx], send_sem=sm_send.at[i],
                    recv_sem=sm_recv.at[i], device_id=peer, device_id_type=pl.DeviceIdType.MESH).wait_recv()
        for cp in small_sends:
            cp.wait_send()
        for cp in small_local:
            cp.wait()

    hbm = pl.BlockSpec(memory_space=pl.ANY)
    vmem = pl.BlockSpec(memory_space=pltpu.VMEM)
    in_blk = parts_w_in.shape[1:]
    return pl.pallas_call(
        body, name="reduce_grads",
        in_specs=[hbm] * 3, out_specs=[vmem, hbm, hbm],
        out_shape=[jax.ShapeDtypeStruct(in_blk, F32),
                   jax.ShapeDtypeStruct((N_DEV,) + parts_wg.shape[1:], F32),
                   jax.ShapeDtypeStruct((N_DEV,) + small.shape, F32)],
        scratch_shapes=_OwnerSum.scratch(in_blk)
        + [pltpu.SemaphoreType.DMA((2 * (N_DEV - 1),)), pltpu.SemaphoreType.DMA((2 * (N_DEV - 1),)),
           pltpu.SemaphoreType.DMA((2,))],
        compiler_params=_cparams(),
    )(parts_w_in, parts_wg, small)


def _adamw(recv, w, m, v, name):
    rows, width = w.shape
    tr = 128 if rows % 128 == 0 else rows
    n_parts = recv.shape[0]

    def body(r_ref, w_ref, m_ref, v_ref, g_ref, d_ref, nm_ref, nv_ref):
        g = r_ref[0]
        for j in range(1, n_parts):
            g = g + r_ref[j]
        nm = ADAM_B1 * m_ref[...] + (1.0 - ADAM_B1) * g
        nv = ADAM_B2 * v_ref[...] + (1.0 - ADAM_B2) * (g * g)
        m_hat = nm / (1.0 - ADAM_B1 ** ADAM_STEP)
        v_hat = nv / (1.0 - ADAM_B2 ** ADAM_STEP)
        g_ref[...] = g
        d_ref[...] = -ADAM_LR * (m_hat / (jnp.sqrt(v_hat) + ADAM_EPS) + ADAM_WD * w_ref[...])
        nm_ref[...] = nm
        nv_ref[...] = nv

    spec = _rows(tr, width)
    return pl.pallas_call(
        body, name=name, grid=(rows // tr,),
        in_specs=[pl.BlockSpec((n_parts, tr, width), lambda i: (0, i, 0)), spec, spec, spec],
        out_specs=[spec] * 4,
        out_shape=[jax.ShapeDtypeStruct((rows, width), F32)] * 4,
        compiler_params=_cparams(dimension_semantics=("arbitrary",)),
    )(recv, w, m, v)


def _adamw_shard_view(g, w, m, v):
    rows, width = g.shape

    def body(g_ref, w_hbm, m_hbm, v_hbm, g_out, d_out, nm_out, nv_out, bufs, outs, sems):
        loads = [pltpu.make_async_copy(src.at[:, 0, :], bufs.at[i], sems.at[i])
                 for i, src in enumerate((w_hbm, m_hbm, v_hbm))]
        for cp in loads:
            cp.start()
        g = g_ref[...]
        for cp in loads:
            cp.wait()
        nm = ADAM_B1 * bufs[1] + (1.0 - ADAM_B1) * g
        nv = ADAM_B2 * bufs[2] + (1.0 - ADAM_B2) * (g * g)
        m_hat = nm / (1.0 - ADAM_B1 ** ADAM_STEP)
        v_hat = nv / (1.0 - ADAM_B2 ** ADAM_STEP)
        outs[0] = g
        outs[1] = -ADAM_LR * (m_hat / (jnp.sqrt(v_hat) + ADAM_EPS) + ADAM_WD * bufs[0])
        outs[2] = nm
        outs[3] = nv
        stores = [pltpu.make_async_copy(outs.at[i], dst.at[:, 0, :], sems.at[3 + i])
                  for i, dst in enumerate((g_out, d_out, nm_out, nv_out))]
        for cp in stores:
            cp.start()
        for cp in stores:
            cp.wait()

    hbm = pl.BlockSpec(memory_space=pl.ANY)
    return pl.pallas_call(
        body, name="adamw_w_in",
        in_specs=[pl.BlockSpec(memory_space=pltpu.VMEM), hbm, hbm, hbm], out_specs=[hbm] * 4,
        out_shape=[jax.ShapeDtypeStruct((rows, 1, width), F32)] * 4,
        scratch_shapes=[pltpu.VMEM((3, rows, width), F32), pltpu.VMEM((4, rows, width), F32),
                        pltpu.SemaphoreType.DMA((7,))],
        compiler_params=_cparams(),
    )(g, w, m, v)


def _pack_small(ln_g, ln_b, b_gate, norm_w, sinks, loss=None):
    def rows8(t):
        t = t.reshape(-1)
        t = jnp.pad(t, (0, 1024 - t.shape[0]))
        return t.reshape(8, 128)

    loss = jnp.zeros((1,), F32) if loss is None else loss
    return jnp.concatenate([rows8(t) for t in (ln_g, ln_b, b_gate, norm_w, sinks, loss)], axis=0)


def _unpack_small(p):
    flat = [p[8 * i:8 * (i + 1)].reshape(1, 1024) for i in range(5)]
    return flat[0], flat[1], flat[2][:, :256], flat[3][:, :128], flat[4][:, :8]


def kernel(x, positions, w_in, gla_w_gate_up, gla_b_gate, attn_sinks, gla_norm_w, w_out, ln_g, ln_b, loss_target, m_w_in, m_gla_w_gate_up, m_gla_b_gate, m_attn_sinks, m_gla_norm_w, m_w_out, m_ln_g, m_ln_b, v_w_in, v_gla_w_gate_up, v_gla_b_gate, v_attn_sinks, v_gla_norm_w, v_w_out, v_ln_g, v_ln_b):
    w_in_full, wg_full = _all_gather_weights(_shard_view(w_in), gla_w_gate_up[0])

    loss, grad_x, parts_w_in, g_wg, g_bg, g_sinks, g_nw, g_out, g_ln = _local_step(
        x[0], positions[0], w_in_full, wg_full, gla_b_gate, attn_sinks[0], gla_norm_w, w_out[0], ln_g, ln_b,
        loss_target[0])

    parts_wg = jnp.transpose(g_wg.reshape(GLA_RANK, N_DEV, 32), (1, 0, 2))
    small = _pack_small(g_ln[0:1], g_ln[1:2], g_bg, g_nw, g_sinks[:, 0].reshape(1, SWA_Q_HEADS), loss[0, 0:1])
    g_in, r_wg, r_small = _reduce_grads(parts_w_in, parts_wg, small)

    upd_in = _adamw_shard_view(g_in, _shard_view(w_in), _shard_view(m_w_in), _shard_view(v_w_in))
    upd_in = [jnp.transpose(t, (1, 2, 0)) for t in upd_in]
    upd_out = _adamw(g_out[None], w_out[0], m_w_out[0], v_w_out[0], "adamw_w_out")
    upd_wg = _adamw(r_wg, gla_w_gate_up[0], m_gla_w_gate_up[0], v_gla_w_gate_up[0], "adamw_wg")
    upd_small = _adamw(
        r_small,
        _pack_small(ln_g, ln_b, gla_b_gate, gla_norm_w, attn_sinks),
        _pack_small(m_ln_g, m_ln_b, m_gla_b_gate, m_gla_norm_w, m_attn_sinks),
        _pack_small(v_ln_g, v_ln_b, v_gla_b_gate, v_gla_norm_w, v_attn_sinks), "adamw_small")

    total = jnp.sum(r_small[:, 40, 0])
    outs = [total, grad_x[None]]
    for kind in range(4):
        s_ln_g, s_ln_b, s_bg, s_nw, s_sinks = _unpack_small(upd_small[kind])
        outs += [upd_in[kind], upd_wg[kind][None], s_bg, s_sinks, s_nw, upd_out[kind][None], s_ln_g, s_ln_b]
    return tuple(outs)
```

```python
import jax
import jax.numpy as jnp
from jax import lax
from jax.experimental import pallas as pl
from jax.experimental.pallas import tpu as pltpu

F32 = jnp.float32
MXU_DTYPE = jnp.bfloat16

N_DEV = 8
D_MODEL = 1024
SWA_Q_HEADS = 8
SWA_KV_HEADS = 2
SWA_GROUP = 4
SWA_HEAD_DIM = 64
BLOCK = 128
ROPE_THETA = 500000.0
ROT_DIM = 16
GLA_HEADS = 4
GLA_DK = 64
GLA_DV = 128
GLA_RANK = 16
GLA_TAU = 16.0
GLA_CHUNK = 64
D_IN_PROJ = 2832
D_IN_SHARD = D_IN_PROJ // N_DEV
D_OUT_SHARD = D_MODEL // N_DEV
OFF = (0, 512, 640, 768, 1280, 1536, 1792, 2304, 2816, 2832)
EPS = 1e-5
ALPHA = 2.0 ** 0.25
SWA_SCALE = SWA_HEAD_DIM ** -0.5
GLA_SCALE = GLA_DK ** -0.5
ADAM_LR = 0.001
ADAM_B1 = 0.9
ADAM_B2 = 0.999
ADAM_EPS = 1e-08
ADAM_WD = 0.01
ADAM_STEP = 10
VMEM_LIMIT = 56 * 1024 * 1024

_NT = (((1,), (1,)), ((), ()))
_TN = (((0,), (0,)), ((), ()))


def _mm(a, b):
    return jnp.dot(a, b, preferred_element_type=F32)


def _mm_nt(a, b):
    return lax.dot_general(a, b, _NT, preferred_element_type=F32)


def _mm_tn(a, b):
    return lax.dot_general(a, b, _TN, preferred_element_type=F32)


def _sigmoid(t):
    return 1.0 / (1.0 + jnp.exp(-t))


def _cparams(**kw):
    return pltpu.CompilerParams(vmem_limit_bytes=VMEM_LIMIT, **kw)


def _full(shape):
    return pl.BlockSpec(shape, lambda *_: (0,) * len(shape))


def _rows(tile, width):
    return pl.BlockSpec((tile, width), lambda i: (i, 0))


def _rope_tables(positions):
    half = ROT_DIM // 2
    inv_freq = ROPE_THETA ** (-jnp.arange(half, dtype=F32) / half)
    lane = jnp.arange(128, dtype=jnp.int32) % SWA_HEAD_DIM
    ang = positions.astype(F32)[:, None] * inv_freq[None, :]
    cos, sin = lax.optimization_barrier((jnp.cos(ang), jnp.sin(ang)))
    cos, sin = jnp.tile(cos, (1, 128 // half)), jnp.tile(sin, (1, 128 // half))
    c = jnp.where(lane < ROT_DIM, cos, 1.0)
    s1 = jnp.where(lane < half, -sin, 0.0)
    s2 = jnp.where((lane >= half) & (lane < ROT_DIM), sin, 0.0)
    return c, s1, s2


def _rope(t, c, s1, s2):
    return t * c + pltpu.roll(t, 120, 1) * s1 + pltpu.roll(t, 8, 1) * s2


def _rope_t(g, c, s1, s2):
    return g * c + pltpu.roll(g * s1, 8, 1) + pltpu.roll(g * s2, 120, 1)


def _in_proj(x, w_in, wg, b_gate, rope, w_out_s):
    s = x.shape[0]
    ts = min(512, s)
    nsteps = s // ts
    forward_step = min(3, nsteps - 1)
    widths = [OFF[i + 1] - OFF[i] for i in range(9)]

    def body(x_ref, w_ref, wg_ref, bg_ref, c_ref, s1_ref, s2_ref, wos_ref,
             qa_ref, ka_ref, va_ref, ga_ref, qb_ref, kb_ref, vb_ref, gb_ref, rb_ref, la_ref, oms_ref, wout_ref,
             wout_all, send_sems, recv_sems):
        xb = x_ref[...].astype(MXU_DTYPE)
        c, s1, s2 = c_ref[...], s1_ref[...], s2_ref[...]
        i0 = pl.program_id(0)
        gather = _BlockGather(wout_all, send_sems, recv_sems)

        @pl.when(i0 == 0)
        def _():
            ka_ref[0:BLOCK, :] = jnp.zeros((BLOCK, 128), ka_ref.dtype)
            va_ref[0:BLOCK, :] = jnp.zeros((BLOCK, 128), va_ref.dtype)
            wout_all[gather.me] = wos_ref[...].astype(wout_all.dtype)
            gather.start()

        @pl.when(i0 == forward_step)
        def _():
            gather.forward()

        @pl.when(i0 == nsteps - 1)
        def _():
            gather.finish()
            for j in range(N_DEV):
                wout_ref[D_OUT_SHARD * j:D_OUT_SHARD * (j + 1), :] = wout_all[j]

        kv_rows = pl.ds(pl.multiple_of(BLOCK + i0 * ts, BLOCK), ts)

        def cols(i):
            return _mm_nt(xb, w_ref[OFF[i]:OFF[i + 1], :])

        qa = cols(0)
        for i in range(4):
            qa_ref[:, 128 * i:128 * (i + 1)] = _rope(qa[:, 128 * i:128 * (i + 1)], c, s1, s2).astype(qa_ref.dtype)
        kv = _mm_nt(xb, w_ref[OFF[1]:OFF[3], :])
        ka_ref[kv_rows, :] = _rope(kv[:, 0:128], c, s1, s2).astype(ka_ref.dtype)
        va_ref[kv_rows, :] = kv[:, 128:256].astype(va_ref.dtype)
        ga_ref[...] = cols(3)
        qb_ref[...] = cols(4)
        kb_ref[...] = cols(5)
        vb_ref[...] = cols(6).astype(vb_ref.dtype)
        gb_ref[...] = cols(7)
        rb = cols(8)
        rb_ref[...] = rb
        logit = _mm(rb.astype(MXU_DTYPE), wg_ref[...]) + bg_ref[...]
        e = jnp.exp(-jnp.abs(logit))
        la_ref[...] = (jnp.minimum(logit, 0.0) - jnp.log(1.0 + e)) / GLA_TAU
        oms_ref[...] = jnp.where(logit >= 0.0, e, 1.0) / (1.0 + e)

    out_shape = [jax.ShapeDtypeStruct((s + BLOCK if i in (1, 2) else s, w), MXU_DTYPE if i in (0, 1, 2, 6) else F32)
                 for i, w in enumerate(widths)]
    out_shape += [jax.ShapeDtypeStruct((s, 256), F32)] * 2
    out_shape += [jax.ShapeDtypeStruct((D_MODEL, D_MODEL), MXU_DTYPE)]
    return pl.pallas_call(
        body, name="in_proj", grid=(nsteps,),
        in_specs=[_rows(ts, D_MODEL), _full((D_IN_PROJ, D_MODEL)), _full((GLA_RANK, 256)), _full((1, 256)),
                  _rows(ts, 128), _rows(ts, 128), _rows(ts, 128), _full((D_OUT_SHARD, D_MODEL))],
        out_specs=[_full((s + BLOCK, w)) if i in (1, 2) else _rows(ts, w) for i, w in enumerate(widths)]
        + [_rows(ts, 256)] * 2 + [_full((D_MODEL, D_MODEL))],
        out_shape=out_shape,
        scratch_shapes=[pltpu.VMEM((N_DEV, D_OUT_SHARD, D_MODEL), MXU_DTYPE)] + _BlockGather.scratch(),
        compiler_params=_cparams(dimension_semantics=("arbitrary",)),
    )(x, w_in, wg, b_gate, *rope, w_out_s)


SWA_ROWS = SWA_GROUP * BLOCK


def _swa_bias():
    shape = (2, 2 * BLOCK, SWA_ROWS)
    ki = lax.broadcasted_iota(jnp.int32, shape, 1)
    qi = lax.broadcasted_iota(jnp.int32, shape, 2) & (BLOCK - 1)
    first = lax.broadcasted_iota(jnp.int32, shape, 0) == 0
    dist = qi + BLOCK - ki
    ok = (dist >= 0) & (dist < BLOCK) & (jnp.logical_not(first) | (ki >= BLOCK))
    return jnp.where(ok, 0.0, -jnp.inf).astype(F32)


SWA_SUB = 2


def _swa_bias_of(bias_ref, n, b):
    return bias_ref[jnp.minimum(n, 1)] if b == 0 else bias_ref[1]


def _swa_dup(t, j):
    t = t.astype(F32)
    low = lax.broadcasted_iota(jnp.int32, t.shape, 1) < SWA_HEAD_DIM
    keep = low if j == 0 else jnp.logical_not(low)
    return jnp.where(keep, t, pltpu.roll(t, SWA_HEAD_DIM, 1)).astype(MXU_DTYPE)


def _swa_stack(t, j):
    low = lax.broadcasted_iota(jnp.int32, (BLOCK, 128), 1) < SWA_HEAD_DIM
    zero = jnp.zeros((BLOCK, 128), t.dtype)
    blocks = []
    for p in (2 * j, 2 * j + 1):
        tp = t[:, 128 * p:128 * (p + 1)]
        blocks += [jnp.where(low, tp, zero), jnp.where(low, zero, tp)]
    return jnp.concatenate(blocks, axis=0)


def _swa_unstack(t):
    low = lax.broadcasted_iota(jnp.int32, (BLOCK, 128), 1) < SWA_HEAD_DIM
    return [jnp.where(low, t[2 * BLOCK * i:2 * BLOCK * i + BLOCK], t[2 * BLOCK * i + BLOCK:2 * BLOCK * (i + 1)])
            for i in range(2)]


def _swa_sink_row(sink_ref, j):
    lane = lax.broadcasted_iota(jnp.int32, (1, SWA_ROWS), 1)
    row = jnp.full((1, SWA_ROWS), sink_ref[SWA_GROUP * j], F32)
    for r in range(1, SWA_GROUP):
        row = jnp.where(lane >= BLOCK * r, sink_ref[SWA_GROUP * j + r], row)
    return row


def _split3(t):
    hi = t.astype(MXU_DTYPE)
    r1 = t - hi.astype(F32)
    mid = r1.astype(MXU_DTYPE)
    lo = (r1 - mid.astype(F32)).astype(MXU_DTYPE)
    return jnp.concatenate([hi, mid, lo], axis=1)


def _row_sums_as_row(t):
    ones = jnp.ones((8, 3 * t.shape[1]), MXU_DTYPE)
    return _mm_nt(ones, _split3(t))[0:1, :]


def _swa_probs_t(qs, kd, bias_t, sink):
    sc = _mm_nt(kd, qs) + bias_t
    m = jnp.maximum(jnp.max(sc, axis=0, keepdims=True), sink)
    p = jnp.exp(sc - m)
    ps = jnp.exp(sink - m)
    rinv = 1.0 / (jnp.sum(p, axis=0, keepdims=True) + ps)
    return p * rinv, ps * rinv


def _swa_fwd(sinks, qa, k_pad, v_pad, ga):
    s = qa.shape[0]
    tq = SWA_SUB * BLOCK

    def body(sink_ref, qa_ref, ga_ref, bias_ref, k_ref, v_ref, attn_ref, cat_ref):
        n = pl.program_id(0)
        for b in range(SWA_SUB):
            rows = slice(BLOCK * b, BLOCK * (b + 1))
            start = pl.multiple_of((n * SWA_SUB + b) * BLOCK, BLOCK)
            kw = k_ref[pl.ds(start, 2 * BLOCK), :]
            vw = v_ref[pl.ds(start, 2 * BLOCK), :]
            bias_t = _swa_bias_of(bias_ref, n, b)
            q = qa_ref[rows, :] * SWA_SCALE
            g = ga_ref[rows, :]
            silu = g * _sigmoid(g)
            for j in range(SWA_KV_HEADS):
                qs = _swa_stack(q, j).astype(MXU_DTYPE)
                probs, _ = _swa_probs_t(qs, _swa_dup(kw, j), bias_t, _swa_sink_row(sink_ref, j))
                pairs = _swa_unstack(_mm_tn(probs.astype(MXU_DTYPE), _swa_dup(vw, j)))
                for i in range(2):
                    lanes = slice(128 * (2 * j + i), 128 * (2 * j + i + 1))
                    attn_ref[rows, lanes] = pairs[i]
                    cat_ref[rows, lanes] = (pairs[i] * silu[:, lanes]).astype(cat_ref.dtype)

    return pl.pallas_call(
        body, name="swa_fwd", grid=(s // tq,),
        in_specs=[pl.BlockSpec(memory_space=pltpu.SMEM), _rows(tq, 512), _rows(tq, 512),
                  _full((2, 2 * BLOCK, SWA_ROWS)), _full((s + BLOCK, 128)), _full((s + BLOCK, 128))],
        out_specs=[_rows(tq, 512), _rows(tq, 512)],
        out_shape=[jax.ShapeDtypeStruct((s, 512), F32), jax.ShapeDtypeStruct((s, 512), MXU_DTYPE)],
        compiler_params=_cparams(dimension_semantics=("arbitrary",)),
    )(sinks, qa, ga, _swa_bias(), k_pad, v_pad)


GLA_KW = GLA_HEADS * GLA_DK
GLA_VW = GLA_HEADS * GLA_DV


def _idiv(t, d):
    return t >> (d.bit_length() - 1)


def _chunk_cumsum(t, lower):
    n, w = t.shape
    r = lax.broadcasted_iota(jnp.int32, (n, n), 0)
    c = lax.broadcasted_iota(jnp.int32, (n, n), 1)
    tri = ((_idiv(r, GLA_CHUNK) == _idiv(c, GLA_CHUNK)) & ((r >= c) if lower else (r <= c))).astype(MXU_DTYPE)
    parts = _mm(tri, _split3(t))
    return (parts[:, :w] + parts[:, w:2 * w]) + parts[:, 2 * w:]


def _chunk_last(t):
    n = t.shape[0]
    return jnp.concatenate(
        [jnp.broadcast_to(t[c + GLA_CHUNK - 1:c + GLA_CHUNK, :], (GLA_CHUNK, t.shape[1]))
         for c in range(0, n, GLA_CHUNK)], axis=0)


def _head_stack(t, width):
    head = _idiv(lax.broadcasted_iota(jnp.int32, t.shape, 1), width)
    zero = jnp.zeros_like(t)
    return jnp.concatenate([jnp.where(head == h, t, zero) for h in range(GLA_HEADS)], axis=0)


def _heads_to_rows(t):
    return jnp.concatenate([t[:, GLA_DV * h:GLA_DV * (h + 1)] for h in range(GLA_HEADS)], axis=0)


def _rows_to_heads(t):
    return jnp.concatenate([t[GLA_CHUNK * h:GLA_CHUNK * (h + 1)] for h in range(GLA_HEADS)], axis=1)


def _state_by_head(t):
    srow = _idiv(lax.broadcasted_iota(jnp.int32, (GLA_VW, GLA_KW), 0), GLA_DV)
    slane = _idiv(lax.broadcasted_iota(jnp.int32, (GLA_VW, GLA_KW), 1), GLA_DK)
    return jnp.where(srow == slane, jnp.concatenate([t] * GLA_HEADS, axis=0), jnp.zeros((GLA_VW, GLA_KW), t.dtype))


def _gla_masks():
    row = lax.broadcasted_iota(jnp.int32, (GLA_CHUNK, GLA_KW), 0)
    pos = lax.broadcasted_iota(jnp.int32, (GLA_CHUNK, GLA_KW), 1) & (GLA_CHUNK - 1)
    return pos <= row, pos >= row


def _gla_fwd(qb, kb, vb, la, gb, norm_w):
    s = qb.shape[0]
    tb = min(256, s)
    ch = tb // GLA_CHUNK

    def body(qb_ref, kb_ref, vb_ref, la_ref, gb_ref, nw_ref, o_ref, cat_ref, sp_ref, st_ref):
        @pl.when(pl.program_id(0) == 0)
        def _():
            st_ref[...] = jnp.zeros_like(st_ref)

        causal, _ = _gla_masks()
        nw = nw_ref[...]
        b = _chunk_cumsum(la_ref[...], True)
        bl = _chunk_last(b)
        k = kb_ref[...]
        qd = ((qb_ref[...] * GLA_SCALE) * jnp.exp(b)).astype(MXU_DTYPE)
        ki = (k * jnp.exp(-b)).astype(MXU_DTYPE)
        ke = (k * jnp.exp(bl - b)).astype(MXU_DTYPE)
        dec = jnp.exp(bl)
        v = vb_ref[...].astype(MXU_DTYPE)
        g = gb_ref[...]
        silu = g * _sigmoid(g)
        for ci in range(ch):
            rows = slice(GLA_CHUNK * ci, GLA_CHUNK * (ci + 1))
            qds, kis, kes = (_head_stack(t[rows], GLA_DK) for t in (qd, ki, ke))
            a = jnp.where(causal, _mm_nt(qd[rows], kis), 0.0).astype(MXU_DTYPE)
            st = st_ref[...]
            sp_ref[ci] = st
            o = _mm(a, _head_stack(v[rows], GLA_DV)) + _rows_to_heads(_mm_nt(qds, st.astype(MXU_DTYPE)))
            st_ref[...] = st * dec[rows][0:1] + _mm_tn(_heads_to_rows(v[rows]), kes)
            o_ref[rows, :] = o
            for h in range(GLA_HEADS):
                lv = slice(GLA_DV * h, GLA_DV * (h + 1))
                oh = o[:, lv]
                r = lax.rsqrt(jnp.mean(oh * oh, axis=1, keepdims=True) + EPS)
                cat_ref[rows, lv] = (oh * r * nw * silu[rows, lv]).astype(cat_ref.dtype)

    return pl.pallas_call(
        body, name="gla_fwd", grid=(s // tb,),
        in_specs=[_rows(tb, 256), _rows(tb, 256), _rows(tb, 512), _rows(tb, 256), _rows(tb, 512), _full((1, 128))],
        out_specs=[_rows(tb, 512), _rows(tb, 512), pl.BlockSpec((ch, GLA_DV, 256), lambda i: (i, 0, 0))],
        out_shape=[jax.ShapeDtypeStruct((s, 512), F32), jax.ShapeDtypeStruct((s, 512), MXU_DTYPE),
                   jax.ShapeDtypeStruct((s // GLA_CHUNK, GLA_DV, 256), F32)],
        scratch_shapes=[pltpu.VMEM((GLA_DV, GLA_KW), F32)],
        compiler_params=_cparams(dimension_semantics=("arbitrary",)),
    )(qb, kb, vb, la, gb, norm_w)


def _out_ln_loss(cat_a, cat_b, w_out, x, target, ln_g, ln_b):
    s = x.shape[0]
    ts = min(512, s)
    halves = 2 if ts % 32 == 0 else 1
    th = ts // halves

    def body(ca_ref, cb_ref, w_ref, x_ref, t_ref, g_ref, b_ref,
             loss_ref, gx_ref, da_ref, db_ref, gw_ref, gln_ref):
        @pl.when(pl.program_id(0) == 0)
        def _():
            loss_ref[...] = jnp.zeros_like(loss_ref)
            gw_ref[...] = jnp.zeros_like(gw_ref)
            gln_ref[...] = jnp.zeros_like(gln_ref)

        g = g_ref[...]
        dh16s = []
        for k in range(halves):
            rows = slice(th * k, th * (k + 1))
            mix = _mm(ca_ref[rows, :], w_ref[0:512, :]) + _mm(cb_ref[rows, :], w_ref[512:1024, :])
            h = ALPHA * x_ref[rows, :] + mix
            mu = jnp.mean(h, axis=1, keepdims=True)
            hc = h - mu
            rstd = lax.rsqrt(jnp.mean(hc * hc, axis=1, keepdims=True) + EPS)
            xhat = hc * rstd
            err = xhat * g + b_ref[...] - t_ref[rows, :]
            loss_ref[...] += 0.5 * jnp.sum(jnp.mean(err * err, axis=1, keepdims=True))
            dy = err * (1.0 / D_MODEL)
            gln_ref[0:1, :] += jnp.sum(dy * xhat, axis=0, keepdims=True)
            gln_ref[1:2, :] += jnp.sum(dy, axis=0, keepdims=True)
            dxh = dy * g
            dh = rstd * (dxh - jnp.mean(dxh, axis=1, keepdims=True)
                         - xhat * jnp.mean(dxh * xhat, axis=1, keepdims=True))
            gx_ref[rows, :] = ALPHA * dh
            dh16s.append(dh.astype(MXU_DTYPE))
        for k in range(halves):
            rows = slice(th * k, th * (k + 1))
            da_ref[rows, :] = _mm_nt(dh16s[k], w_ref[0:512, :])
            db_ref[rows, :] = _mm_nt(dh16s[k], w_ref[512:1024, :])
        dh16 = jnp.concatenate(dh16s, axis=0)
        gw_ref[0:512, :] += _mm_tn(ca_ref[...], dh16)
        gw_ref[512:1024, :] += _mm_tn(cb_ref[...], dh16)

    return pl.pallas_call(
        body, name="out_ln_loss", grid=(s // ts,),
        in_specs=[_rows(ts, 512), _rows(ts, 512), _full((D_MODEL, D_MODEL)), _rows(ts, D_MODEL), _rows(ts, D_MODEL),
                  _full((1, D_MODEL)), _full((1, D_MODEL))],
        out_specs=[_full((1, 128)), _rows(ts, D_MODEL), _rows(ts, 512), _rows(ts, 512),
                   _full((D_MODEL, D_MODEL)), _full((2, D_MODEL))],
        out_shape=[jax.ShapeDtypeStruct((1, 128), F32), jax.ShapeDtypeStruct((s, D_MODEL), F32),
                   jax.ShapeDtypeStruct((s, 512), F32), jax.ShapeDtypeStruct((s, 512), F32),
                   jax.ShapeDtypeStruct((D_MODEL, D_MODEL), F32), jax.ShapeDtypeStruct((2, D_MODEL), F32)],
        compiler_params=_cparams(dimension_semantics=("arbitrary",)),
    )(cat_a, cat_b, w_out, x, target, ln_g, ln_b)


def _swa_bwd(sinks, qa, k_pad, v_pad, attn, ga, d_cat_a, rope, parts_w_out):
    s = qa.shape[0]
    tq = SWA_SUB * BLOCK
    nsteps = s // tq
    forward_step = min(2, nsteps - 1)

    def body(sink_ref, qa_ref, ga_ref, at_ref, dc_ref, c_ref, s1_ref, s2_ref, bias_ref, k_ref, v_ref, pout_ref,
             dq_ref, dg_ref, dk_ref, dv_ref, ds_ref, gout_ref, *scratch):
        n = pl.program_id(0)
        owner_sum = _OwnerSum(pout_ref, *scratch)

        @pl.when(n == 0)
        def _():
            dk_ref[...] = jnp.zeros_like(dk_ref)
            dv_ref[...] = jnp.zeros_like(dv_ref)
            ds_ref[...] = jnp.zeros_like(ds_ref)
            owner_sum.start()

        @pl.when(n == forward_step)
        def _():
            owner_sum.forward()

        @pl.when(n == nsteps - 1)
        def _():
            gout_ref[...] = owner_sum.finish()

        low = lax.broadcasted_iota(jnp.int32, (2 * BLOCK, 128), 1) < SWA_HEAD_DIM
        for b in range(SWA_SUB):
            rows = slice(BLOCK * b, BLOCK * (b + 1))
            start = pl.multiple_of((n * SWA_SUB + b) * BLOCK, BLOCK)
            kw = k_ref[pl.ds(start, 2 * BLOCK), :]
            vw = v_ref[pl.ds(start, 2 * BLOCK), :]
            bias_t = _swa_bias_of(bias_ref, n, b)
            q = qa_ref[rows, :] * SWA_SCALE
            g = ga_ref[rows, :]
            sg = _sigmoid(g)
            o = at_ref[rows, :]
            dc = dc_ref[rows, :]
            do = dc * (g * sg)
            dg_ref[rows, :] = (dc * o * (sg * (1.0 + g * (1.0 - sg)))).astype(dg_ref.dtype)
            od = do * o
            c, s1, s2 = c_ref[rows, :], s1_ref[rows, :], s2_ref[rows, :]
            dk, dv = [], []
            for j in range(SWA_KV_HEADS):
                kd, vd = _swa_dup(kw, j), _swa_dup(vw, j)
                qs = _swa_stack(q, j).astype(MXU_DTYPE)
                dos = _swa_stack(do, j).astype(MXU_DTYPE)
                probs, psink = _swa_probs_t(qs, kd, bias_t, _swa_sink_row(sink_ref, j))
                delta = _row_sums_as_row(_swa_stack(od, j))
                dsc = (probs * (_mm_nt(vd, dos) - delta)).astype(MXU_DTYPE)
                dsink = psink * delta
                for r in range(SWA_GROUP):
                    h = SWA_GROUP * j + r
                    ds_ref[h:h + 1, :] += jnp.zeros((1, 128), F32) - jnp.sum(dsink[:, BLOCK * r:BLOCK * (r + 1)])
                dq = _swa_unstack(_mm_tn(dsc, kd))
                for i in range(2):
                    lanes = slice(128 * (2 * j + i), 128 * (2 * j + i + 1))
                    dq_ref[rows, lanes] = _rope_t(dq[i] * SWA_SCALE, c, s1, s2).astype(dq_ref.dtype)
                dkj = _mm(dsc, qs)
                dvj = _mm(probs.astype(MXU_DTYPE), dos)
                dk.append(dkj + pltpu.roll(dkj, SWA_HEAD_DIM, 1))
                dv.append(dvj + pltpu.roll(dvj, SWA_HEAD_DIM, 1))
            dk_ref[pl.ds(start, 2 * BLOCK), :] += jnp.where(low, dk[0], dk[1])
            dv_ref[pl.ds(start, 2 * BLOCK), :] += jnp.where(low, dv[0], dv[1])

    out_blk = parts_w_out.shape[1:]
    return pl.pallas_call(
        body, name="swa_bwd", grid=(nsteps,),
        in_specs=[pl.BlockSpec(memory_space=pltpu.SMEM)] + [_rows(tq, 512)] * 4 + [_rows(tq, 128)] * 3
        + [_full((2, 2 * BLOCK, SWA_ROWS))] + [_full((s + BLOCK, 128))] * 2 + [pl.BlockSpec(memory_space=pl.ANY)],
        out_specs=[_rows(tq, 512), _rows(tq, 512), _full((s + BLOCK, 128)), _full((s + BLOCK, 128)),
                   _full((SWA_Q_HEADS, 128)), _full(out_blk)],
        out_shape=[jax.ShapeDtypeStruct((s, 512), MXU_DTYPE), jax.ShapeDtypeStruct((s, 512), MXU_DTYPE),
                   jax.ShapeDtypeStruct((s + BLOCK, 128), F32), jax.ShapeDtypeStruct((s + BLOCK, 128), F32),
                   jax.ShapeDtypeStruct((SWA_Q_HEADS, 128), F32), jax.ShapeDtypeStruct(out_blk, F32)],
        scratch_shapes=_OwnerSum.scratch(out_blk),
        compiler_params=_cparams(dimension_semantics=("arbitrary",)),
    )(sinks, qa, ga, attn, d_cat_a, *rope, _swa_bias(), k_pad, v_pad, parts_w_out)


def _gla_bwd(qb, kb, vb, la, oms, gb, o, sprev, d_cat_b, rb, wg, norm_w):
    s = qb.shape[0]
    tb = min(512, s)
    ch = tb // GLA_CHUNK
    nb = s // tb

    def body(qb_ref, kb_ref, vb_ref, la_ref, oms_ref, gb_ref, o_ref, sp_ref, dc_ref, rb_ref, wg_ref, nw_ref,
             dq_ref, dk_ref, dv_ref, dg_ref, dr_ref, gwg_ref, gbg_ref, gnw_ref, dst_ref):
        @pl.when(pl.program_id(0) == 0)
        def _():
            dst_ref[...] = jnp.zeros_like(dst_ref)
            gwg_ref[...] = jnp.zeros_like(gwg_ref)
            gbg_ref[...] = jnp.zeros_like(gbg_ref)
            gnw_ref[...] = jnp.zeros_like(gnw_ref)

        causal, causal_t = _gla_masks()
        nw = nw_ref[...]
        b = _chunk_cumsum(la_ref[...], True)
        bl = _chunk_last(b)
        eb, enb, ee, dec = jnp.exp(b), jnp.exp(-b), jnp.exp(bl - b), jnp.exp(bl)
        k = kb_ref[...]
        qd = (qb_ref[...] * GLA_SCALE) * eb
        ki = k * enb
        ke = k * ee
        qd16, ki16, ke16 = qd.astype(MXU_DTYPE), ki.astype(MXU_DTYPE), ke.astype(MXU_DTYPE)
        v16 = vb_ref[...].astype(MXU_DTYPE)

        g = gb_ref[...]
        sg = _sigmoid(g)
        silu = g * sg
        dsilu = sg * (1.0 + g * (1.0 - sg))
        gnw = jnp.zeros((1, GLA_DV), F32)
        do = []
        for h in range(GLA_HEADS):
            lv = slice(GLA_DV * h, GLA_DV * (h + 1))
            oh = o_ref[:, lv]
            dch = dc_ref[:, lv]
            r = lax.rsqrt(jnp.mean(oh * oh, axis=1, keepdims=True) + EPS)
            d_on = dch * silu[:, lv]
            dg_ref[:, lv] = (dch * (oh * r * nw) * dsilu[:, lv]).astype(dg_ref.dtype)
            gnw += jnp.sum(d_on * oh * r, axis=0, keepdims=True)
            u = d_on * nw
            do.append(r * u - oh * (r * r * r) * jnp.mean(u * oh, axis=1, keepdims=True))
        gnw_ref[...] += gnw
        do16 = jnp.concatenate(do, axis=1).astype(MXU_DTYPE)

        db, dbl = [None] * ch, [None] * ch
        for ci in reversed(range(ch)):
            rows = slice(GLA_CHUNK * ci, GLA_CHUNK * (ci + 1))
            qds, kis, kes = (_head_stack(t[rows], GLA_DK) for t in (qd16, ki16, ke16))
            vs, dos = _head_stack(v16[rows], GLA_DV), _head_stack(do16[rows], GLA_DV)
            a = jnp.where(causal, _mm_nt(qd16[rows], kis), 0.0).astype(MXU_DTYPE)
            at = jnp.where(causal_t, _mm_nt(ki16[rows], qds), 0.0).astype(MXU_DTYPE)
            da = jnp.where(causal, _mm_nt(do16[rows], vs), 0.0).astype(MXU_DTYPE)
            dat = jnp.where(causal_t, _mm_nt(v16[rows], dos), 0.0).astype(MXU_DTYPE)
            st = sp_ref[ci]
            dst = dst_ref[...]
            dst16 = dst.astype(MXU_DTYPE)
            dv = _mm(at, dos) + _rows_to_heads(_mm_nt(kes, dst16))
            dqd = _mm(da, kis) + _mm(do16[rows], _state_by_head(st.astype(MXU_DTYPE)))
            dki = _mm(dat, qds)
            dke = _mm(v16[rows], _state_by_head(dst16))
            ddec = jnp.sum(dst * st, axis=0, keepdims=True)
            decc = dec[rows][0:1]
            dst_ref[...] = _mm_tn(_heads_to_rows(do16[rows]), qds) + dst * decc
            dq_ref[rows, :] = (dqd * eb[rows] * GLA_SCALE).astype(dq_ref.dtype)
            dk_ref[rows, :] = (dki * enb[rows] + dke * ee[rows]).astype(dk_ref.dtype)
            dv_ref[rows, :] = dv.astype(dv_ref.dtype)
            dke_ke = dke * ke[rows]
            db[ci] = dqd * qd[rows] - dki * ki[rows] - dke_ke
            dbl[ci] = jnp.broadcast_to(jnp.sum(dke_ke, axis=0, keepdims=True) + ddec * decc, (GLA_CHUNK, GLA_KW))

        dla = _chunk_cumsum(jnp.concatenate(db, axis=0), False) + jnp.concatenate(dbl, axis=0)
        dlogit = dla * oms_ref[...] * (1.0 / GLA_TAU)
        dl16 = dlogit.astype(MXU_DTYPE)
        gbg_ref[...] += jnp.sum(dlogit, axis=0, keepdims=True)
        gwg_ref[...] += _mm_tn(rb_ref[...].astype(MXU_DTYPE), dl16)
        dr_ref[...] = _mm_nt(dl16, wg_ref[...]).astype(dr_ref.dtype)

    def rev(width):
        return pl.BlockSpec((tb, width), lambda i: (nb - 1 - i, 0))

    return pl.pallas_call(
        body, name="gla_bwd", grid=(nb,),
        in_specs=[rev(256), rev(256), rev(512), rev(256), rev(256), rev(512), rev(512),
                  pl.BlockSpec((ch, GLA_DV, 256), lambda i: (nb - 1 - i, 0, 0)), rev(512), rev(GLA_RANK),
                  _full((GLA_RANK, 256)), _full((1, 128))],
        out_specs=[rev(256), rev(256), rev(512), rev(512), rev(GLA_RANK),
                   _full((GLA_RANK, 256)), _full((1, 256)), _full((1, 128))],
        out_shape=[jax.ShapeDtypeStruct((s, 256), MXU_DTYPE), jax.ShapeDtypeStruct((s, 256), MXU_DTYPE),
                   jax.ShapeDtypeStruct((s, 512), MXU_DTYPE), jax.ShapeDtypeStruct((s, 512), MXU_DTYPE),
                   jax.ShapeDtypeStruct((s, GLA_RANK), MXU_DTYPE), jax.ShapeDtypeStruct((GLA_RANK, 256), F32),
                   jax.ShapeDtypeStruct((1, 256), F32), jax.ShapeDtypeStruct((1, 128), F32)],
        scratch_shapes=[pltpu.VMEM((GLA_DV, GLA_KW), F32)],
        compiler_params=_cparams(dimension_semantics=("arbitrary",)),
    )(qb, kb, vb, la, oms, gb, o, sprev, d_cat_b, rb, wg, norm_w)


def _in_proj_bwd_x(gx0, pieces, w_in, rope):
    s = gx0.shape[0]
    ts = min(512, s)
    widths = [OFF[i + 1] - OFF[i] for i in range(9)]

    def body(gx0_ref, *refs):
        piece_refs = refs[:9]
        w_ref, c_ref, s1_ref, s2_ref, gx_ref, dp_ref = refs[9:]
        kv_rows = pl.ds(pl.multiple_of(BLOCK + pl.program_id(0) * ts, BLOCK), ts)
        acc = gx0_ref[...]
        for i in (0, 1, 3, 4, 5, 6, 7, 8):
            lo, hi = OFF[i], OFF[i + 1]
            if i == 1:
                dk = _rope_t(piece_refs[1][kv_rows, :], c_ref[...], s1_ref[...], s2_ref[...])
                t16 = jnp.concatenate([dk, piece_refs[2][kv_rows, :]], axis=1).astype(MXU_DTYPE)
                hi = OFF[3]
            else:
                t16 = piece_refs[i][...].astype(MXU_DTYPE)
            dp_ref[:, lo:hi] = t16
            acc += _mm(t16, w_ref[lo:hi, :])
        gx_ref[...] = acc

    return pl.pallas_call(
        body, name="in_proj_bwd_x", grid=(s // ts,),
        in_specs=[_rows(ts, D_MODEL)]
        + [_full((s + BLOCK, w)) if i in (1, 2) else _rows(ts, w) for i, w in enumerate(widths)]
        + [_full((D_IN_PROJ, D_MODEL))] + [_rows(ts, 128)] * 3,
        out_specs=[_rows(ts, D_MODEL), _rows(ts, D_IN_PROJ)],
        out_shape=[jax.ShapeDtypeStruct((s, D_MODEL), F32), jax.ShapeDtypeStruct((s, D_IN_PROJ), MXU_DTYPE)],
        compiler_params=_cparams(dimension_semantics=("arbitrary",)),
    )(gx0, *pieces, w_in, *rope)


def _in_proj_bwd_w(x, dproj):
    s = x.shape[0]
    ts = min(1024, s)
    nsteps = s // ts
    col_chunks = [(OFF[i], OFF[i + 2] if i == 1 else OFF[i + 1]) for i in (0, 1, 3, 4, 5, 6, 7, 8)]

    def body(x_ref, dp_ref, gw_ref, acc_ref, stage_ref, sems):
        i = pl.program_id(0)

        @pl.when(i == 0)
        def _():
            acc_ref[...] = jnp.zeros_like(acc_ref)

        xb = x_ref[...].astype(MXU_DTYPE)
        for lo, hi in col_chunks:
            acc_ref[lo:hi, :] += _mm_tn(dp_ref[:, lo:hi], xb)

        @pl.when(i == nsteps - 1)
        def _():
            copies = []
            for j in range(N_DEV):
                slot = j % 2
                if j >= 2:
                    copies[j - 2].wait()
                stage_ref[slot] = acc_ref[D_IN_SHARD * j:D_IN_SHARD * (j + 1), :]
                cp = pltpu.make_async_copy(stage_ref.at[slot], gw_ref.at[j], sems.at[slot])
                cp.start()
                copies.append(cp)
            copies[N_DEV - 2].wait()
            copies[N_DEV - 1].wait()

    return pl.pallas_call(
        body, name="in_proj_bwd_w", grid=(nsteps,),
        in_specs=[_rows(ts, D_MODEL), _rows(ts, D_IN_PROJ)],
        out_specs=pl.BlockSpec(memory_space=pl.ANY),
        out_shape=jax.ShapeDtypeStruct((N_DEV, D_IN_SHARD, D_MODEL), F32),
        scratch_shapes=[pltpu.VMEM((D_IN_PROJ, D_MODEL), F32), pltpu.VMEM((2, D_IN_SHARD, D_MODEL), F32),
                        pltpu.SemaphoreType.DMA((2,))],
        compiler_params=_cparams(dimension_semantics=("arbitrary",)),
    )(x, dproj)


def _local_step(x, positions, w_in, wg, b_gate, sinks, norm_w, w_out_s, ln_g, ln_b, target):
    rope = _rope_tables(positions)
    qa, k_pad, v_pad, ga, qb, kb, vb, gb, rb, la, oms, w_out = _in_proj(x, w_in, wg, b_gate, rope, w_out_s)
    attn, cat_a = _swa_fwd(sinks, qa, k_pad, v_pad, ga)
    o, cat_b, sprev = _gla_fwd(qb, kb, vb, la, gb, norm_w)
    loss, gx0, d_cat_a, d_cat_b, g_w_out, g_ln = _out_ln_loss(cat_a, cat_b, w_out, x, target, ln_g, ln_b)
    parts_w_out = g_w_out.reshape(N_DEV, D_OUT_SHARD, D_MODEL)
    dqa, dga, dk_pad, dv_pad, g_sinks, g_out = _swa_bwd(sinks, qa, k_pad, v_pad, attn, ga, d_cat_a, rope, parts_w_out)
    dqb, dkb, dvb, dgb, drb, g_wg, g_bg, g_nw = _gla_bwd(qb, kb, vb, la, oms, gb, o, sprev, d_cat_b, rb, wg, norm_w)
    pieces = (dqa, dk_pad, dv_pad, dga, dqb, dkb, dvb, dgb, drb)
    grad_x, dproj = _in_proj_bwd_x(gx0, pieces, w_in, rope)
    g_w_in = _in_proj_bwd_w(x, dproj)
    return loss, grad_x, g_w_in, g_wg, g_bg, g_sinks, g_nw, g_out, g_ln


def _mesh_pos():
    return lax.axis_index("x"), lax.axis_index("y"), lax.axis_index("c")


def _peer(k, x, y, c):
    px = (1 - x) if k & 4 else x
    py = (1 - y) if k & 2 else y
    pc = (1 - c) if k & 1 else c
    return (px, py, pc), 4 * px + 2 * py + pc


def _other_chips(x, y):
    return [(1 - x, y), (x, 1 - y), (1 - x, 1 - y)]


def _shard_view(t):
    return jnp.transpose(t, (2, 0, 1))


class _BlockGather:
    def __init__(self, slots, send_sems, recv_sems):
        self.slots, self.send_sems, self.recv_sems = slots, send_sems, recv_sems
        x, y, c = _mesh_pos()
        self.xy, self.c, self.me, self.sibling = (x, y), c, 4 * x + 2 * y + c, (x, y, 1 - c)
        self.chips = _other_chips(x, y)

    @staticmethod
    def scratch():
        return [pltpu.SemaphoreType.DMA((N_DEV - 1,)), pltpu.SemaphoreType.DMA((N_DEV - 1,))]

    def _copy(self, k, block, to):
        return pltpu.make_async_remote_copy(
            src_ref=self.slots.at[block], dst_ref=self.slots.at[block], send_sem=self.send_sems.at[k],
            recv_sem=self.recv_sems.at[k], device_id=to, device_id_type=pl.DeviceIdType.MESH)

    def start(self):
        for j, (cx, cy) in enumerate(self.chips):
            self._copy(1 + j, self.me, (cx, cy, self.c)).start()
        self._copy(0, self.me, self.sibling).start()

    def forward(self):
        for j, (cx, cy) in enumerate(self.chips):
            block = 4 * cx + 2 * cy + self.c
            self._copy(1 + j, block, self.sibling).wait_recv()
            self._copy(4 + j, block, self.sibling).start()

    def finish(self):
        x, y = self.xy
        self._copy(0, 4 * x + 2 * y + (1 - self.c), self.sibling).wait_recv()
        for j, (cx, cy) in enumerate(self.chips):
            self._copy(4 + j, 4 * cx + 2 * cy + (1 - self.c), self.sibling).wait_recv()
        for k in range(N_DEV - 1):
            self._copy(k, self.me, self.sibling).wait_send()


def _all_gather_weights(w_in_t, wg_s):
    def body(win_hbm, wg_ref, win_full, wg_full, win_all, wg_all, stage, stage_sem, *sems):
        gathers = (_BlockGather(win_all, *sems[0:2]), _BlockGather(wg_all, *sems[2:4]))
        me = gathers[0].me
        load = pltpu.make_async_copy(win_hbm.at[:, 0, :], stage, stage_sem)
        load.start()
        load.wait()
        win_all[me] = stage[...].astype(win_all.dtype)
        wg_all[me] = wg_ref[...].astype(wg_all.dtype)
        for stage_of in ("start", "forward", "finish"):
            for gather in gathers:
                getattr(gather, stage_of)()
        for j in range(N_DEV):
            win_full[D_IN_SHARD * j:D_IN_SHARD * (j + 1), :] = win_all[j]
            wg_full[:, 32 * j:32 * (j + 1)] = wg_all[j]

    vmem = pl.BlockSpec(memory_space=pltpu.VMEM)
    return pl.pallas_call(
        body, name="all_gather_weights",
        in_specs=[pl.BlockSpec(memory_space=pl.ANY), vmem], out_specs=[vmem] * 2,
        out_shape=[jax.ShapeDtypeStruct((D_IN_PROJ, D_MODEL), MXU_DTYPE),
                   jax.ShapeDtypeStruct((GLA_RANK, 256), MXU_DTYPE)],
        scratch_shapes=[pltpu.VMEM((N_DEV, D_IN_SHARD, D_MODEL), MXU_DTYPE),
                        pltpu.VMEM((N_DEV, GLA_RANK, 32), MXU_DTYPE),
                        pltpu.VMEM((D_IN_SHARD, D_MODEL), F32), pltpu.SemaphoreType.DMA]
        + _BlockGather.scratch() + _BlockGather.scratch(),
        compiler_params=_cparams(),
    )(w_in_t, wg_s)


class _OwnerSum:
    def __init__(self, parts, own, sib, snd, rcv, loc_sems, d2d_send, d2d_recv, ici_send, ici_recv):
        self.parts, self.own, self.sib, self.snd, self.rcv = parts, own, sib, snd, rcv
        self.sems = (loc_sems, d2d_send, d2d_recv, ici_send, ici_recv)
        x, y, c = _mesh_pos()
        self.c, self.sibling = c, (x, y, 1 - c)
        self.chips = [(x, y)] + _other_chips(x, y)

    @staticmethod
    def scratch(block):
        return [pltpu.VMEM((4,) + block, F32), pltpu.VMEM((4,) + block, F32),
                pltpu.VMEM((3,) + block, MXU_DTYPE), pltpu.VMEM((3,) + block, MXU_DTYPE),
                pltpu.SemaphoreType.DMA((4,)), pltpu.SemaphoreType.DMA((4,)), pltpu.SemaphoreType.DMA((4,)),
                pltpu.SemaphoreType.DMA((3,)), pltpu.SemaphoreType.DMA((3,))]

    def _local(self, r):
        cx, cy = self.chips[r]
        return pltpu.make_async_copy(self.parts.at[4 * cx + 2 * cy + self.c], self.own.at[r], self.sems[0].at[r])

    def _d2d(self, r):
        cx, cy = self.chips[r]
        return pltpu.make_async_remote_copy(
            src_ref=self.parts.at[4 * cx + 2 * cy + (1 - self.c)], dst_ref=self.sib.at[r], send_sem=self.sems[1].at[r],
            recv_sem=self.sems[2].at[r], device_id=self.sibling, device_id_type=pl.DeviceIdType.MESH)

    def _ici(self, r):
        cx, cy = self.chips[r]
        return pltpu.make_async_remote_copy(
            src_ref=self.snd.at[r - 1], dst_ref=self.rcv.at[r - 1], send_sem=self.sems[3].at[r - 1],
            recv_sem=self.sems[4].at[r - 1], device_id=(cx, cy, self.c), device_id_type=pl.DeviceIdType.MESH)

    def start(self):
        for r in (1, 2, 3, 0):
            self._local(r).start()
            self._d2d(r).start()

    def forward(self):
        for r in (1, 2, 3):
            self._local(r).wait()
            self._d2d(r).wait_recv()
            self.snd[r - 1] = (self.own[r] + self.sib[r]).astype(self.snd.dtype)
            self._ici(r).start()

    def finish(self):
        self._local(0).wait()
        self._d2d(0).wait_recv()
        acc = self.own[0] + self.sib[0]
        for r in (1, 2, 3):
            self._ici(r).wait_recv()
            acc = acc + self.rcv[r - 1].astype(F32)
        for r in range(4):
            self._d2d(r).wait_send()
        for r in (1, 2, 3):
            self._ici(r).wait_send()
        return acc


def _reduce_grads(parts_w_in, parts_wg, small):
    def body(pin_ref, pwg_ref, sm_ref, gin_ref, rwg_ref, rsm_ref, *scratch):
        sm_send, sm_recv, sm_loc = scratch[-3:]
        x, y, c = _mesh_pos()
        me = 4 * x + 2 * y + c
        owner_sum = _OwnerSum(pin_ref, *scratch[:-3])

        small_dsts = (rwg_ref, rsm_ref)

        def small_src(a, block):
            return pwg_ref.at[block] if a == 0 else sm_ref

        small_local = [pltpu.make_async_copy(small_src(a, me), small_dsts[a].at[me], sm_loc.at[a]) for a in range(2)]
        for cp in small_local:
            cp.start()
        small_sends = []
        for k in range(1, N_DEV):
            peer, pidx = _peer(k, x, y, c)
            for a in range(2):
                i = 2 * (k - 1) + a
                cp = pltpu.make_async_remote_copy(
                    src_ref=small_src(a, pidx), dst_ref=small_dsts[a].at[me], send_sem=sm_send.at[i],
                    recv_sem=sm_recv.at[i], device_id=peer, device_id_type=pl.DeviceIdType.MESH)
                cp.start()
                small_sends.append(cp)

        owner_sum.start()
        owner_sum.forward()
        gin_ref[...] = owner_sum.finish()

        for k in range(1, N_DEV):
            peer, pidx = _peer(k, x, y, c)
            for a in range(2):
                i = 2 * (k - 1) + a
                pltpu.make_async_remote_copy(
                    src_ref=small_src(a, me), dst_ref=small_dsts[a].at[pidx], send_sem=sm_send.at[i],
                    recv_sem=sm_recv.at[i], device_id=peer, device_id_type=pl.DeviceIdType.MESH).wait_recv()
        for cp in small_sends:
            cp.wait_send()
        for cp in small_local:
            cp.wait()

    hbm = pl.BlockSpec(memory_space=pl.ANY)
    vmem = pl.BlockSpec(memory_space=pltpu.VMEM)
    in_blk = parts_w_in.shape[1:]
    return pl.pallas_call(
        body, name="reduce_grads",
        in_specs=[hbm] * 3, out_specs=[vmem, hbm, hbm],
        out_shape=[jax.ShapeDtypeStruct(in_blk, F32),
                   jax.ShapeDtypeStruct((N_DEV,) + parts_wg.shape[1:], F32),
                   jax.ShapeDtypeStruct((N_DEV,) + small.shape, F32)],
        scratch_shapes=_OwnerSum.scratch(in_blk)
        + [pltpu.SemaphoreType.DMA((2 * (N_DEV - 1),)), pltpu.SemaphoreType.DMA((2 * (N_DEV - 1),)),
           pltpu.SemaphoreType.DMA((2,))],
        compiler_params=_cparams(),
    )(parts_w_in, parts_wg, small)


def _adamw(recv, w, m, v, name):
    rows, width = w.shape
    tr = 128 if rows % 128 == 0 else rows
    n_parts = recv.shape[0]

    def body(r_ref, w_ref, m_ref, v_ref, g_ref, d_ref, nm_ref, nv_ref):
        g = r_ref[0]
        for j in range(1, n_parts):
            g = g + r_ref[j]
        nm = ADAM_B1 * m_ref[...] + (1.0 - ADAM_B1) * g
        nv = ADAM_B2 * v_ref[...] + (1.0 - ADAM_B2) * (g * g)
        m_hat = nm / (1.0 - ADAM_B1 ** ADAM_STEP)
        v_hat = nv / (1.0 - ADAM_B2 ** ADAM_STEP)
        g_ref[...] = g
        d_ref[...] = -ADAM_LR * (m_hat / (jnp.sqrt(v_hat) + ADAM_EPS) + ADAM_WD * w_ref[...])
        nm_ref[...] = nm
        nv_ref[...] = nv

    spec = _rows(tr, width)
    return pl.pallas_call(
        body, name=name, grid=(rows // tr,),
        in_specs=[pl.BlockSpec((n_parts, tr, width), lambda i: (0, i, 0)), spec, spec, spec],
        out_specs=[spec] * 4,
        out_shape=[jax.ShapeDtypeStruct((rows, width), F32)] * 4,
        compiler_params=_cparams(dimension_semantics=("arbitrary",)),
    )(recv, w, m, v)


def _adamw_shard_view(g, w, m, v):
    rows, width = g.shape

    def body(g_ref, w_hbm, m_hbm, v_hbm, g_out, d_out, nm_out, nv_out, bufs, outs, sems):
        loads = [pltpu.make_async_copy(src.at[:, 0, :], bufs.at[i], sems.at[i])
                 for i, src in enumerate((w_hbm, m_hbm, v_hbm))]
        for cp in loads:
            cp.start()
        g = g_ref[...]
        for cp in loads:
            cp.wait()
        nm = ADAM_B1 * bufs[1] + (1.0 - ADAM_B1) * g
        nv = ADAM_B2 * bufs[2] + (1.0 - ADAM_B2) * (g * g)
        m_hat = nm / (1.0 - ADAM_B1 ** ADAM_STEP)
        v_hat = nv / (1.0 - ADAM_B2 ** ADAM_STEP)
        outs[0] = g
        outs[1] = -ADAM_LR * (m_hat / (jnp.sqrt(v_hat) + ADAM_EPS) + ADAM_WD * bufs[0])
        outs[2] = nm
        outs[3] = nv
        stores = [pltpu.make_async_copy(outs.at[i], dst.at[:, 0, :], sems.at[3 + i])
                  for i, dst in enumerate((g_out, d_out, nm_out, nv_out))]
        for cp in stores:
            cp.start()
        for cp in stores:
            cp.wait()

    hbm = pl.BlockSpec(memory_space=pl.ANY)
    return pl.pallas_call(
        body, name="adamw_w_in",
        in_specs=[pl.BlockSpec(memory_space=pltpu.VMEM), hbm, hbm, hbm], out_specs=[hbm] * 4,
        out_shape=[jax.ShapeDtypeStruct((rows, 1, width), F32)] * 4,
        scratch_shapes=[pltpu.VMEM((3, rows, width), F32), pltpu.VMEM((4, rows, width), F32),
                        pltpu.SemaphoreType.DMA((7,))],
        compiler_params=_cparams(),
    )(g, w, m, v)


def _pack_small(ln_g, ln_b, b_gate, norm_w, sinks, loss=None):
    def rows8(t):
        t = t.reshape(-1)
        t = jnp.pad(t, (0, 1024 - t.shape[0]))
        return t.reshape(8, 128)

    loss = jnp.zeros((1,), F32) if loss is None else loss
    return jnp.concatenate([rows8(t) for t in (ln_g, ln_b, b_gate, norm_w, sinks, loss)], axis=0)


def _unpack_small(p):
    flat = [p[8 * i:8 * (i + 1)].reshape(1, 1024) for i in range(5)]
    return flat[0], flat[1], flat[2][:, :256], flat[3][:, :128], flat[4][:, :8]


def kernel(x, positions, w_in, gla_w_gate_up, gla_b_gate, attn_sinks, gla_norm_w, w_out, ln_g, ln_b, loss_target, m_w_in, m_gla_w_gate_up, m_gla_b_gate, m_attn_sinks, m_gla_norm_w, m_w_out, m_ln_g, m_ln_b, v_w_in, v_gla_w_gate_up, v_gla_b_gate, v_attn_sinks, v_gla_norm_w, v_w_out, v_ln_g, v_ln_b):
    w_in_full, wg_full = _all_gather_weights(_shard_view(w_in), gla_w_gate_up[0])

    loss, grad_x, parts_w_in, g_wg, g_bg, g_sinks, g_nw, g_out, g_ln = _local_step(
        x[0], positions[0], w_in_full, wg_full, gla_b_gate, attn_sinks[0], gla_norm_w, w_out[0], ln_g, ln_b,
        loss_target[0])

    parts_wg = jnp.transpose(g_wg.reshape(GLA_RANK, N_DEV, 32), (1, 0, 2))
    small = _pack_small(g_ln[0:1], g_ln[1:2], g_bg, g_nw, g_sinks[:, 0].reshape(1, SWA_Q_HEADS), loss[0, 0:1])
    g_in, r_wg, r_small = _reduce_grads(parts_w_in, parts_wg, small)

    upd_in = _adamw_shard_view(g_in, _shard_view(w_in), _shard_view(m_w_in), _shard_view(v_w_in))
    upd_in = [jnp.transpose(t, (1, 2, 0)) for t in upd_in]
    upd_out = _adamw(g_out[None], w_out[0], m_w_out[0], v_w_out[0], "adamw_w_out")
    upd_wg = _adamw(r_wg, gla_w_gate_up[0], m_gla_w_gate_up[0], v_gla_w_gate_up[0], "adamw_wg")
    upd_small = _adamw(
        r_small,
        _pack_small(ln_g, ln_b, gla_b_gate, gla_norm_w, attn_sinks),
        _pack_small(m_ln_g, m_ln_b, m_gla_b_gate, m_gla_norm_w, m_attn_sinks),
        _pack_small(v_ln_g, v_ln_b, v_gla_b_gate, v_gla_norm_w, v_attn_sinks), "adamw_small")

    total = jnp.sum(r_small[:, 40, 0])
    outs = [total, grad_x[None]]
    for kind in range(4):
        s_ln_g, s_ln_b, s_bg, s_nw, s_sinks = _unpack_small(upd_small[kind])
        outs += [upd_in[kind], upd_wg[kind][None], s_bg, s_sinks, s_nw, upd_out[kind][None], s_ln_g, s_ln_b]
    return tuple(outs)
```

```python
import jax
import jax.numpy as jnp
from jax import lax
from jax.experimental import pallas as pl
from jax.experimental.pallas import tpu as pltpu

F32 = jnp.float32
MXU_DTYPE = jnp.bfloat16

N_DEV = 8
D_MODEL = 1024
SWA_Q_HEADS = 8
SWA_KV_HEADS = 2
SWA_GROUP = 4
SWA_HEAD_DIM = 64
BLOCK = 128
ROPE_THETA = 500000.0
ROT_DIM = 16
GLA_HEADS = 4
GLA_DK = 64
GLA_DV = 128
GLA_RANK = 16
GLA_TAU = 16.0
GLA_CHUNK = 64
D_IN_PROJ = 2832
D_IN_SHARD = D_IN_PROJ // N_DEV
D_OUT_SHARD = D_MODEL // N_DEV
OFF = (0, 512, 640, 768, 1280, 1536, 1792, 2304, 2816, 2832)
EPS = 1e-5
ALPHA = 2.0 ** 0.25
SWA_SCALE = SWA_HEAD_DIM ** -0.5
GLA_SCALE = GLA_DK ** -0.5
ADAM_LR = 0.001
ADAM_B1 = 0.9
ADAM_B2 = 0.999
ADAM_EPS = 1e-08
ADAM_WD = 0.01
ADAM_STEP = 10
VMEM_LIMIT = 56 * 1024 * 1024

_NT = (((1,), (1,)), ((), ()))
_TN = (((0,), (0,)), ((), ()))


def _mm(a, b):
    return jnp.dot(a, b, preferred_element_type=F32)


def _mm_nt(a, b):
    return lax.dot_general(a, b, _NT, preferred_element_type=F32)


def _mm_tn(a, b):
    return lax.dot_general(a, b, _TN, preferred_element_type=F32)


def _sigmoid(t):
    return 1.0 / (1.0 + jnp.exp(-t))


def _cparams(**kw):
    return pltpu.CompilerParams(vmem_limit_bytes=VMEM_LIMIT, **kw)


def _full(shape):
    return pl.BlockSpec(shape, lambda *_: (0,) * len(shape))


def _rows(tile, width):
    return pl.BlockSpec((tile, width), lambda i: (i, 0))


def _rope_tables(positions):
    half = ROT_DIM // 2
    inv_freq = ROPE_THETA ** (-jnp.arange(half, dtype=F32) / half)
    lane = jnp.arange(128, dtype=jnp.int32) % SWA_HEAD_DIM
    ang = positions.astype(F32)[:, None] * inv_freq[None, :]
    cos, sin = lax.optimization_barrier((jnp.cos(ang), jnp.sin(ang)))
    cos, sin = jnp.tile(cos, (1, 128 // half)), jnp.tile(sin, (1, 128 // half))
    c = jnp.where(lane < ROT_DIM, cos, 1.0)
    s1 = jnp.where(lane < half, -sin, 0.0)
    s2 = jnp.where((lane >= half) & (lane < ROT_DIM), sin, 0.0)
    return c, s1, s2


def _rope(t, c, s1, s2):
    return t * c + pltpu.roll(t, 120, 1) * s1 + pltpu.roll(t, 8, 1) * s2


def _rope_t(g, c, s1, s2):
    return g * c + pltpu.roll(g * s1, 8, 1) + pltpu.roll(g * s2, 120, 1)


def _in_proj(x, w_in, wg, b_gate, rope, w_out_s):
    s = x.shape[0]
    ts = min(512, s)
    nsteps = s // ts
    forward_step = min(3, nsteps - 1)
    widths = [OFF[i + 1] - OFF[i] for i in range(9)]

    def body(x_ref, w_ref, wg_ref, bg_ref, c_ref, s1_ref, s2_ref, wos_ref,
             qa_ref, ka_ref, va_ref, ga_ref, qb_ref, kb_ref, vb_ref, gb_ref, rb_ref, la_ref, oms_ref, wout_ref,
             wout_all, send_sems, recv_sems):
        xb = x_ref[...].astype(MXU_DTYPE)
        c, s1, s2 = c_ref[...], s1_ref[...], s2_ref[...]
        i0 = pl.program_id(0)
        gather = _BlockGather(wout_all, send_sems, recv_sems)

        @pl.when(i0 == 0)
        def _():
            ka_ref[0:BLOCK, :] = jnp.zeros((BLOCK, 128), ka_ref.dtype)
            va_ref[0:BLOCK, :] = jnp.zeros((BLOCK, 128), va_ref.dtype)
            wout_all[gather.me] = wos_ref[...].astype(wout_all.dtype)
            gather.start()

        @pl.when(i0 == forward_step)
        def _():
            gather.forward()

        @pl.when(i0 == nsteps - 1)
        def _():
            gather.finish()
            for j in range(N_DEV):
                wout_ref[D_OUT_SHARD * j:D_OUT_SHARD * (j + 1), :] = wout_all[j]

        kv_rows = pl.ds(pl.multiple_of(BLOCK + i0 * ts, BLOCK), ts)

        def cols(i):
            return _mm_nt(xb, w_ref[OFF[i]:OFF[i + 1], :])

        qa = cols(0)
        for i in range(4):
            qa_ref[:, 128 * i:128 * (i + 1)] = _rope(qa[:, 128 * i:128 * (i + 1)], c, s1, s2).astype(qa_ref.dtype)
        kv = _mm_nt(xb, w_ref[OFF[1]:OFF[3], :])
        ka_ref[kv_rows, :] = _rope(kv[:, 0:128], c, s1, s2).astype(ka_ref.dtype)
        va_ref[kv_rows, :] = kv[:, 128:256].astype(va_ref.dtype)
        ga_ref[...] = cols(3)
        qb_ref[...] = cols(4)
        kb_ref[...] = cols(5)
        vb_ref[...] = cols(6).astype(vb_ref.dtype)
        gb_ref[...] = cols(7)
        rb = cols(8)
        rb_ref[...] = rb
        logit = _mm(rb.astype(MXU_DTYPE), wg_ref[...]) + bg_ref[...]
        e = jnp.exp(-jnp.abs(logit))
        la_ref[...] = (jnp.minimum(logit, 0.0) - jnp.log(1.0 + e)) / GLA_TAU
        oms_ref[...] = jnp.where(logit >= 0.0, e, 1.0) / (1.0 + e)

    out_shape = [jax.ShapeDtypeStruct((s + BLOCK if i in (1, 2) else s, w), MXU_DTYPE if i in (0, 1, 2, 6) else F32)
                 for i, w in enumerate(widths)]
    out_shape += [jax.ShapeDtypeStruct((s, 256), F32)] * 2
    out_shape += [jax.ShapeDtypeStruct((D_MODEL, D_MODEL), MXU_DTYPE)]
    return pl.pallas_call(
        body, name="in_proj", grid=(nsteps,),
        in_specs=[_rows(ts, D_MODEL), _full((D_IN_PROJ, D_MODEL)), _full((GLA_RANK, 256)), _full((1, 256)),
                  _rows(ts, 128), _rows(ts, 128), _rows(ts, 128), _full((D_OUT_SHARD, D_MODEL))],
        out_specs=[_full((s + BLOCK, w)) if i in (1, 2) else _rows(ts, w) for i, w in enumerate(widths)]
        + [_rows(ts, 256)] * 2 + [_full((D_MODEL, D_MODEL))],
        out_shape=out_shape,
        scratch_shapes=[pltpu.VMEM((N_DEV, D_OUT_SHARD, D_MODEL), MXU_DTYPE)] + _BlockGather.scratch(),
        compiler_params=_cparams(dimension_semantics=("arbitrary",)),
    )(x, w_in, wg, b_gate, *rope, w_out_s)


SWA_ROWS = SWA_GROUP * BLOCK


def _swa_bias():
    shape = (2, 2 * BLOCK, SWA_ROWS)
    ki = lax.broadcasted_iota(jnp.int32, shape, 1)
    qi = lax.broadcasted_iota(jnp.int32, shape, 2) & (BLOCK - 1)
    first = lax.broadcasted_iota(jnp.int32, shape, 0) == 0
    dist = qi + BLOCK - ki
    ok = (dist >= 0) & (dist < BLOCK) & (jnp.logical_not(first) | (ki >= BLOCK))
    return jnp.where(ok, 0.0, -jnp.inf).astype(F32)


SWA_SUB = 2


def _swa_bias_of(bias_ref, n, b):
    return bias_ref[jnp.minimum(n, 1)] if b == 0 else bias_ref[1]


def _swa_dup(t, j):
    t = t.astype(F32)
    low = lax.broadcasted_iota(jnp.int32, t.shape, 1) < SWA_HEAD_DIM
    keep = low if j == 0 else jnp.logical_not(low)
    return jnp.where(keep, t, pltpu.roll(t, SWA_HEAD_DIM, 1)).astype(MXU_DTYPE)


def _swa_stack(t, j):
    low = lax.broadcasted_iota(jnp.int32, (BLOCK, 128), 1) < SWA_HEAD_DIM
    zero = jnp.zeros((BLOCK, 128), t.dtype)
    blocks = []
    for p in (2 * j, 2 * j + 1):
        tp = t[:, 128 * p:128 * (p + 1)]
        blocks += [jnp.where(low, tp, zero), jnp.where(low, zero, tp)]
    return jnp.concatenate(blocks, axis=0)


def _swa_unstack(t):
    low = lax.broadcasted_iota(jnp.int32, (BLOCK, 128), 1) < SWA_HEAD_DIM
    return [jnp.where(low, t[2 * BLOCK * i:2 * BLOCK * i + BLOCK], t[2 * BLOCK * i + BLOCK:2 * BLOCK * (i + 1)])
            for i in range(2)]


def _swa_sink_row(sink_ref, j):
    lane = lax.broadcasted_iota(jnp.int32, (1, SWA_ROWS), 1)
    row = jnp.full((1, SWA_ROWS), sink_ref[SWA_GROUP * j], F32)
    for r in range(1, SWA_GROUP):
        row = jnp.where(lane >= BLOCK * r, sink_ref[SWA_GROUP * j + r], row)
    return row


def _split3(t):
    hi = t.astype(MXU_DTYPE)
    r1 = t - hi.astype(F32)
    mid = r1.astype(MXU_DTYPE)
    lo = (r1 - mid.astype(F32)).astype(MXU_DTYPE)
    return jnp.concatenate([hi, mid, lo], axis=1)


def _row_sums_as_row(t):
    ones = jnp.ones((8, 3 * t.shape[1]), MXU_DTYPE)
    return _mm_nt(ones, _split3(t))[0:1, :]


def _swa_probs_t(qs, kd, bias_t, sink):
    sc = _mm_nt(kd, qs) + bias_t
    m = jnp.maximum(jnp.max(sc, axis=0, keepdims=True), sink)
    p = jnp.exp(sc - m)
    ps = jnp.exp(sink - m)
    rinv = 1.0 / (jnp.sum(p, axis=0, keepdims=True) + ps)
    return p * rinv, ps * rinv


def _swa_fwd(sinks, qa, k_pad, v_pad, ga):
    s = qa.shape[0]
    tq = SWA_SUB * BLOCK

    def body(sink_ref, qa_ref, ga_ref, bias_ref, k_ref, v_ref, attn_ref, cat_ref):
        n = pl.program_id(0)
        for b in range(SWA_SUB):
            rows = slice(BLOCK * b, BLOCK * (b + 1))
            start = pl.multiple_of((n * SWA_SUB + b) * BLOCK, BLOCK)
            kw = k_ref[pl.ds(start, 2 * BLOCK), :]
            vw = v_ref[pl.ds(start, 2 * BLOCK), :]
            bias_t = _swa_bias_of(bias_ref, n, b)
            q = qa_ref[rows, :] * SWA_SCALE
            g = ga_ref[rows, :]
            silu = g * _sigmoid(g)
            for j in range(SWA_KV_HEADS):
                qs = _swa_stack(q, j).astype(MXU_DTYPE)
                probs, _ = _swa_probs_t(qs, _swa_dup(kw, j), bias_t, _swa_sink_row(sink_ref, j))
                pairs = _swa_unstack(_mm_tn(probs.astype(MXU_DTYPE), _swa_dup(vw, j)))
                for i in range(2):
                    lanes = slice(128 * (2 * j + i), 128 * (2 * j + i + 1))
                    attn_ref[rows, lanes] = pairs[i]
                    cat_ref[rows, lanes] = (pairs[i] * silu[:, lanes]).astype(cat_ref.dtype)

    return pl.pallas_call(
        body, name="swa_fwd", grid=(s // tq,),
        in_specs=[pl.BlockSpec(memory_space=pltpu.SMEM), _rows(tq, 512), _rows(tq, 512),
                  _full((2, 2 * BLOCK, SWA_ROWS)), _full((s + BLOCK, 128)), _full((s + BLOCK, 128))],
        out_specs=[_rows(tq, 512), _rows(tq, 512)],
        out_shape=[jax.ShapeDtypeStruct((s, 512), F32), jax.ShapeDtypeStruct((s, 512), MXU_DTYPE)],
        compiler_params=_cparams(dimension_semantics=("arbitrary",)),
    )(sinks, qa, ga, _swa_bias(), k_pad, v_pad)


GLA_KW = GLA_HEADS * GLA_DK
GLA_VW = GLA_HEADS * GLA_DV


def _idiv(t, d):
    return t >> (d.bit_length() - 1)


def _chunk_cumsum(t, lower):
    n, w = t.shape
    r = lax.broadcasted_iota(jnp.int32, (n, n), 0)
    c = lax.broadcasted_iota(jnp.int32, (n, n), 1)
    tri = ((_idiv(r, GLA_CHUNK) == _idiv(c, GLA_CHUNK)) & ((r >= c) if lower else (r <= c))).astype(MXU_DTYPE)
    parts = _mm(tri, _split3(t))
    return (parts[:, :w] + parts[:, w:2 * w]) + parts[:, 2 * w:]


def _chunk_last(t):
    n = t.shape[0]
    return jnp.concatenate(
        [jnp.broadcast_to(t[c + GLA_CHUNK - 1:c + GLA_CHUNK, :], (GLA_CHUNK, t.shape[1]))
         for c in range(0, n, GLA_CHUNK)], axis=0)


def _head_stack(t, width):
    head = _idiv(lax.broadcasted_iota(jnp.int32, t.shape, 1), width)
    zero = jnp.zeros_like(t)
    return jnp.concatenate([jnp.where(head == h, t, zero) for h in range(GLA_HEADS)], axis=0)


def _heads_to_rows(t):
    return jnp.concatenate([t[:, GLA_DV * h:GLA_DV * (h + 1)] for h in range(GLA_HEADS)], axis=0)


def _rows_to_heads(t):
    return jnp.concatenate([t[GLA_CHUNK * h:GLA_CHUNK * (h + 1)] for h in range(GLA_HEADS)], axis=1)


def _state_by_head(t):
    srow = _idiv(lax.broadcasted_iota(jnp.int32, (GLA_VW, GLA_KW), 0), GLA_DV)
    slane = _idiv(lax.broadcasted_iota(jnp.int32, (GLA_VW, GLA_KW), 1), GLA_DK)
    return jnp.where(srow == slane, jnp.concatenate([t] * GLA_HEADS, axis=0), jnp.zeros((GLA_VW, GLA_KW), t.dtype))


def _gla_masks():
    row = lax.broadcasted_iota(jnp.int32, (GLA_CHUNK, GLA_KW), 0)
    pos = lax.broadcasted_iota(jnp.int32, (GLA_CHUNK, GLA_KW), 1) & (GLA_CHUNK - 1)
    return pos <= row, pos >= row


def _gla_fwd(qb, kb, vb, la, gb, norm_w):
    s = qb.shape[0]
    tb = min(256, s)
    ch = tb // GLA_CHUNK

    def body(qb_ref, kb_ref, vb_ref, la_ref, gb_ref, nw_ref, o_ref, cat_ref, sp_ref, st_ref):
        @pl.when(pl.program_id(0) == 0)
        def _():
            st_ref[...] = jnp.zeros_like(st_ref)

        causal, _ = _gla_masks()
        nw = nw_ref[...]
        b = _chunk_cumsum(la_ref[...], True)
        bl = _chunk_last(b)
        k = kb_ref[...]
        qd = ((qb_ref[...] * GLA_SCALE) * jnp.exp(b)).astype(MXU_DTYPE)
        ki = (k * jnp.exp(-b)).astype(MXU_DTYPE)
        ke = (k * jnp.exp(bl - b)).astype(MXU_DTYPE)
        dec = jnp.exp(bl)
        v = vb_ref[...].astype(MXU_DTYPE)
        g = gb_ref[...]
        silu = g * _sigmoid(g)
        for ci in range(ch):
            rows = slice(GLA_CHUNK * ci, GLA_CHUNK * (ci + 1))
            qds, kis, kes = (_head_stack(t[rows], GLA_DK) for t in (qd, ki, ke))
            a = jnp.where(causal, _mm_nt(qd[rows], kis), 0.0).astype(MXU_DTYPE)
            st = st_ref[...]
            sp_ref[ci] = st
            o = _mm(a, _head_stack(v[rows], GLA_DV)) + _rows_to_heads(_mm_nt(qds, st.astype(MXU_DTYPE)))
            st_ref[...] = st * dec[rows][0:1] + _mm_tn(_heads_to_rows(v[rows]), kes)
            o_ref[rows, :] = o
            for h in range(GLA_HEADS):
                lv = slice(GLA_DV * h, GLA_DV * (h + 1))
                oh = o[:, lv]
                r = lax.rsqrt(jnp.mean(oh * oh, axis=1, keepdims=True) + EPS)
                cat_ref[rows, lv] = (oh * r * nw * silu[rows, lv]).astype(cat_ref.dtype)

    return pl.pallas_call(
        body, name="gla_fwd", grid=(s // tb,),
        in_specs=[_rows(tb, 256), _rows(tb, 256), _rows(tb, 512), _rows(tb, 256), _rows(tb, 512), _full((1, 128))],
        out_specs=[_rows(tb, 512), _rows(tb, 512), pl.BlockSpec((ch, GLA_DV, 256), lambda i: (i, 0, 0))],
        out_shape=[jax.ShapeDtypeStruct((s, 512), F32), jax.ShapeDtypeStruct((s, 512), MXU_DTYPE),
                   jax.ShapeDtypeStruct((s // GLA_CHUNK, GLA_DV, 256), F32)],
        scratch_shapes=[pltpu.VMEM((GLA_DV, GLA_KW), F32)],
        compiler_params=_cparams(dimension_semantics=("arbitrary",)),
    )(qb, kb, vb, la, gb, norm_w)


def _out_ln_loss(cat_a, cat_b, w_out, x, target, ln_g, ln_b):
    s = x.shape[0]
    ts = min(512, s)
    halves = 2 if ts % 32 == 0 else 1
    th = ts // halves

    def body(ca_ref, cb_ref, w_ref, x_ref, t_ref, g_ref, b_ref,
             loss_ref, gx_ref, da_ref, db_ref, gw_ref, gln_ref):
        @pl.when(pl.program_id(0) == 0)
        def _():
            loss_ref[...] = jnp.zeros_like(loss_ref)
            gw_ref[...] = jnp.zeros_like(gw_ref)
            gln_ref[...] = jnp.zeros_like(gln_ref)

        g = g_ref[...]
        dh16s = []
        for k in range(halves):
            rows = slice(th * k, th * (k + 1))
            mix = _mm(ca_ref[rows, :], w_ref[0:512, :]) + _mm(cb_ref[rows, :], w_ref[512:1024, :])
            h = ALPHA * x_ref[rows, :] + mix
            mu = jnp.mean(h, axis=1, keepdims=True)
            hc = h - mu
            rstd = lax.rsqrt(jnp.mean(hc * hc, axis=1, keepdims=True) + EPS)
            xhat = hc * rstd
            err = xhat * g + b_ref[...] - t_ref[rows, :]
            loss_ref[...] += 0.5 * jnp.sum(jnp.mean(err * err, axis=1, keepdims=True))
            dy = err * (1.0 / D_MODEL)
            gln_ref[0:1, :] += jnp.sum(dy * xhat, axis=0, keepdims=True)
            gln_ref[1:2, :] += jnp.sum(dy, axis=0, keepdims=True)
            dxh = dy * g
            dh = rstd * (dxh - jnp.mean(dxh, axis=1, keepdims=True)
                         - xhat * jnp.mean(dxh * xhat, axis=1, keepdims=True))
            gx_ref[rows, :] = ALPHA * dh
            dh16s.append(dh.astype(MXU_DTYPE))
        for k in range(halves):
            rows = slice(th * k, th * (k + 1))
            da_ref[rows, :] = _mm_nt(dh16s[k], w_ref[0:512, :])
            db_ref[rows, :] = _mm_nt(dh16s[k], w_ref[512:1024, :])
        dh16 = jnp.concatenate(dh16s, axis=0)
        gw_ref[0:512, :] += _mm_tn(ca_ref[...], dh16)
        gw_ref[512:1024, :] += _mm_tn(cb_ref[...], dh16)

    return pl.pallas_call(
        body, name="out_ln_loss", grid=(s // ts,),
        in_specs=[_rows(ts, 512), _rows(ts, 512), _full((D_MODEL, D_MODEL)), _rows(ts, D_MODEL), _rows(ts, D_MODEL),
                  _full((1, D_MODEL)), _full((1, D_MODEL))],
        out_specs=[_full((1, 128)), _rows(ts, D_MODEL), _rows(ts, 512), _rows(ts, 512),
                   _full((D_MODEL, D_MODEL)), _full((2, D_MODEL))],
        out_shape=[jax.ShapeDtypeStruct((1, 128), F32), jax.ShapeDtypeStruct((s, D_MODEL), F32),
                   jax.ShapeDtypeStruct((s, 512), F32), jax.ShapeDtypeStruct((s, 512), F32),
                   jax.ShapeDtypeStruct((D_MODEL, D_MODEL), F32), jax.ShapeDtypeStruct((2, D_MODEL), F32)],
        compiler_params=_cparams(dimension_semantics=("arbitrary",)),
    )(cat_a, cat_b, w_out, x, target, ln_g, ln_b)


def _swa_bwd(sinks, qa, k_pad, v_pad, attn, ga, d_cat_a, rope, parts_w_out):
    s = qa.shape[0]
    tq = SWA_SUB * BLOCK
    nsteps = s // tq
    forward_step = min(2, nsteps - 1)

    def body(sink_ref, qa_ref, ga_ref, at_ref, dc_ref, c_ref, s1_ref, s2_ref, bias_ref, k_ref, v_ref, pout_ref,
             dq_ref, dg_ref, dk_ref, dv_ref, ds_ref, gout_ref, *scratch):
        n = pl.program_id(0)
        owner_sum = _OwnerSum(pout_ref, *scratch)

        @pl.when(n == 0)
        def _():
            dk_ref[...] = jnp.zeros_like(dk_ref)
            dv_ref[...] = jnp.zeros_like(dv_ref)
            ds_ref[...] = jnp.zeros_like(ds_ref)
            owner_sum.start()

        @pl.when(n == forward_step)
        def _():
            owner_sum.forward()

        @pl.when(n == nsteps - 1)
        def _():
            gout_ref[...] = owner_sum.finish()

        low = lax.broadcasted_iota(jnp.int32, (2 * BLOCK, 128), 1) < SWA_HEAD_DIM
        for b in range(SWA_SUB):
            rows = slice(BLOCK * b, BLOCK * (b + 1))
            start = pl.multiple_of((n * SWA_SUB + b) * BLOCK, BLOCK)
            kw = k_ref[pl.ds(start, 2 * BLOCK), :]
            vw = v_ref[pl.ds(start, 2 * BLOCK), :]
            bias_t = _swa_bias_of(bias_ref, n, b)
            q = qa_ref[rows, :] * SWA_SCALE
            g = ga_ref[rows, :]
            sg = _sigmoid(g)
            o = at_ref[rows, :]
            dc = dc_ref[rows, :]
            do = dc * (g * sg)
            dg_ref[rows, :] = (dc * o * (sg * (1.0 + g * (1.0 - sg)))).astype(dg_ref.dtype)
            od = do * o
            c, s1, s2 = c_ref[rows, :], s1_ref[rows, :], s2_ref[rows, :]
            dk, dv = [], []
            for j in range(SWA_KV_HEADS):
                kd, vd = _swa_dup(kw, j), _swa_dup(vw, j)
                qs = _swa_stack(q, j).astype(MXU_DTYPE)
                dos = _swa_stack(do, j).astype(MXU_DTYPE)
                probs, psink = _swa_probs_t(qs, kd, bias_t, _swa_sink_row(sink_ref, j))
                delta = _row_sums_as_row(_swa_stack(od, j))
                dsc = (probs * (_mm_nt(vd, dos) - delta)).astype(MXU_DTYPE)
                dsink = psink * delta
                for r in range(SWA_GROUP):
                    h = SWA_GROUP * j + r
                    ds_ref[h:h + 1, :] += jnp.zeros((1, 128), F32) - jnp.sum(dsink[:, BLOCK * r:BLOCK * (r + 1)])
                dq = _swa_unstack(_mm_tn(dsc, kd))
                for i in range(2):
                    lanes = slice(128 * (2 * j + i), 128 * (2 * j + i + 1))
                    dq_ref[rows, lanes] = _rope_t(dq[i] * SWA_SCALE, c, s1, s2).astype(dq_ref.dtype)
                dkj = _mm(dsc, qs)
                dvj = _mm(probs.astype(MXU_DTYPE), dos)
                dk.append(dkj + pltpu.roll(dkj, SWA_HEAD_DIM, 1))
                dv.append(dvj + pltpu.roll(dvj, SWA_HEAD_DIM, 1))
            dk_ref[pl.ds(start, 2 * BLOCK), :] += jnp.where(low, dk[0], dk[1])
            dv_ref[pl.ds(start, 2 * BLOCK), :] += jnp.where(low, dv[0], dv[1])

    out_blk = parts_w_out.shape[1:]
    return pl.pallas_call(
        body, name="swa_bwd", grid=(nsteps,),
        in_specs=[pl.BlockSpec(memory_space=pltpu.SMEM)] + [_rows(tq, 512)] * 4 + [_rows(tq, 128)] * 3
        + [_full((2, 2 * BLOCK, SWA_ROWS))] + [_full((s + BLOCK, 128))] * 2 + [pl.BlockSpec(memory_space=pl.ANY)],
        out_specs=[_rows(tq, 512), _rows(tq, 512), _full((s + BLOCK, 128)), _full((s + BLOCK, 128)),
                   _full((SWA_Q_HEADS, 128)), _full(out_blk)],
        out_shape=[jax.ShapeDtypeStruct((s, 512), MXU_DTYPE), jax.ShapeDtypeStruct((s, 512), MXU_DTYPE),
                   jax.ShapeDtypeStruct((s + BLOCK, 128), F32), jax.ShapeDtypeStruct((s + BLOCK, 128), F32),
                   jax.ShapeDtypeStruct((SWA_Q_HEADS, 128), F32), jax.ShapeDtypeStruct(out_blk, F32)],
        scratch_shapes=_OwnerSum.scratch(out_blk),
        compiler_params=_cparams(dimension_semantics=("arbitrary",)),
    )(sinks, qa, ga, attn, d_cat_a, *rope, _swa_bias(), k_pad, v_pad, parts_w_out)


def _gla_bwd(qb, kb, vb, la, oms, gb, o, sprev, d_cat_b, rb, wg, norm_w):
    s = qb.shape[0]
    tb = min(512, s)
    ch = tb // GLA_CHUNK
    nb = s // tb

    def body(qb_ref, kb_ref, vb_ref, la_ref, oms_ref, gb_ref, o_ref, sp_ref, dc_ref, rb_ref, wg_ref, nw_ref,
             dq_ref, dk_ref, dv_ref, dg_ref, dr_ref, gwg_ref, gbg_ref, gnw_ref, dst_ref):
        @pl.when(pl.program_id(0) == 0)
        def _():
            dst_ref[...] = jnp.zeros_like(dst_ref)
            gwg_ref[...] = jnp.zeros_like(gwg_ref)
            gbg_ref[...] = jnp.zeros_like(gbg_ref)
            gnw_ref[...] = jnp.zeros_like(gnw_ref)

        causal, causal_t = _gla_masks()
        nw = nw_ref[...]
        b = _chunk_cumsum(la_ref[...], True)
        bl = _chunk_last(b)
        eb, enb, ee, dec = jnp.exp(b), jnp.exp(-b), jnp.exp(bl - b), jnp.exp(bl)
        k = kb_ref[...]
        qd = (qb_ref[...] * GLA_SCALE) * eb
        ki = k * enb
        ke = k * ee
        qd16, ki16, ke16 = qd.astype(MXU_DTYPE), ki.astype(MXU_DTYPE), ke.astype(MXU_DTYPE)
        v16 = vb_ref[...].astype(MXU_DTYPE)

        g = gb_ref[...]
        sg = _sigmoid(g)
        silu = g * sg
        dsilu = sg * (1.0 + g * (1.0 - sg))
        gnw = jnp.zeros((1, GLA_DV), F32)
        do = []
        for h in range(GLA_HEADS):
            lv = slice(GLA_DV * h, GLA_DV * (h + 1))
            oh = o_ref[:, lv]
            dch = dc_ref[:, lv]
            r = lax.rsqrt(jnp.mean(oh * oh, axis=1, keepdims=True) + EPS)
            d_on = dch * silu[:, lv]
            dg_ref[:, lv] = (dch * (oh * r * nw) * dsilu[:, lv]).astype(dg_ref.dtype)
            gnw += jnp.sum(d_on * oh * r, axis=0, keepdims=True)
            u = d_on * nw
            do.append(r * u - oh * (r * r * r) * jnp.mean(u * oh, axis=1, keepdims=True))
        gnw_ref[...] += gnw
        do16 = jnp.concatenate(do, axis=1).astype(MXU_DTYPE)

        db, dbl = [None] * ch, [None] * ch
        for ci in reversed(range(ch)):
            rows = slice(GLA_CHUNK * ci, GLA_CHUNK * (ci + 1))
            qds, kis, kes = (_head_stack(t[rows], GLA_DK) for t in (qd16, ki16, ke16))
            vs, dos = _head_stack(v16[rows], GLA_DV), _head_stack(do16[rows], GLA_DV)
            a = jnp.where(causal, _mm_nt(qd16[rows], kis), 0.0).astype(MXU_DTYPE)
            at = jnp.where(causal_t, _mm_nt(ki16[rows], qds), 0.0).astype(MXU_DTYPE)
            da = jnp.where(causal, _mm_nt(do16[rows], vs), 0.0).astype(MXU_DTYPE)
            dat = jnp.where(causal_t, _mm_nt(v16[rows], dos), 0.0).astype(MXU_DTYPE)
            st = sp_ref[ci]
            dst = dst_ref[...]
            dst16 = dst.astype(MXU_DTYPE)
            dv = _mm(at, dos) + _rows_to_heads(_mm_nt(kes, dst16))
            dqd = _mm(da, kis) + _mm(do16[rows], _state_by_head(st.astype(MXU_DTYPE)))
            dki = _mm(dat, qds)
            dke = _mm(v16[rows], _state_by_head(dst16))
            ddec = jnp.sum(dst * st, axis=0, keepdims=True)
            decc = dec[rows][0:1]
            dst_ref[...] = _mm_tn(_heads_to_rows(do16[rows]), qds) + dst * decc
            dq_ref[rows, :] = (dqd * eb[rows] * GLA_SCALE).astype(dq_ref.dtype)
            dk_ref[rows, :] = (dki * enb[rows] + dke * ee[rows]).astype(dk_ref.dtype)
            dv_ref[rows, :] = dv.astype(dv_ref.dtype)
            dke_ke = dke * ke[rows]
            db[ci] = dqd * qd[rows] - dki * ki[rows] - dke_ke
            dbl[ci] = jnp.broadcast_to(jnp.sum(dke_ke, axis=0, keepdims=True) + ddec * decc, (GLA_CHUNK, GLA_KW))

        dla = _chunk_cumsum(jnp.concatenate(db, axis=0), False) + jnp.concatenate(dbl, axis=0)
        dlogit = dla * oms_ref[...] * (1.0 / GLA_TAU)
        dl16 = dlogit.astype(MXU_DTYPE)
        gbg_ref[...] += jnp.sum(dlogit, axis=0, keepdims=True)
        gwg_ref[...] += _mm_tn(rb_ref[...].astype(MXU_DTYPE), dl16)
        dr_ref[...] = _mm_nt(dl16, wg_ref[...]).astype(dr_ref.dtype)

    def rev(width):
        return pl.BlockSpec((tb, width), lambda i: (nb - 1 - i, 0))

    return pl.pallas_call(
        body, name="gla_bwd", grid=(nb,),
        in_specs=[rev(256), rev(256), rev(512), rev(256), rev(256), rev(512), rev(512),
                  pl.BlockSpec((ch, GLA_DV, 256), lambda i: (nb - 1 - i, 0, 0)), rev(512), rev(GLA_RANK),
                  _full((GLA_RANK, 256)), _full((1, 128))],
        out_specs=[rev(256), rev(256), rev(512), rev(512), rev(GLA_RANK),
                   _full((GLA_RANK, 256)), _full((1, 256)), _full((1, 128))],
        out_shape=[jax.ShapeDtypeStruct((s, 256), MXU_DTYPE), jax.ShapeDtypeStruct((s, 256), MXU_DTYPE),
                   jax.ShapeDtypeStruct((s, 512), MXU_DTYPE), jax.ShapeDtypeStruct((s, 512), MXU_DTYPE),
                   jax.ShapeDtypeStruct((s, GLA_RANK), MXU_DTYPE), jax.ShapeDtypeStruct((GLA_RANK, 256), F32),
                   jax.ShapeDtypeStruct((1, 256), F32), jax.ShapeDtypeStruct((1, 128), F32)],
        scratch_shapes=[pltpu.VMEM((GLA_DV, GLA_KW), F32)],
        compiler_params=_cparams(dimension_semantics=("arbitrary",)),
    )(qb, kb, vb, la, oms, gb, o, sprev, d_cat_b, rb, wg, norm_w)


def _in_proj_bwd_x(gx0, pieces, w_in, rope):
    s = gx0.shape[0]
    ts = min(512, s)
    widths = [OFF[i + 1] - OFF[i] for i in range(9)]

    def body(gx0_ref, *refs):
        piece_refs = refs[:9]
        w_ref, c_ref, s1_ref, s2_ref, gx_ref, dp_ref = refs[9:]
        kv_rows = pl.ds(pl.multiple_of(BLOCK + pl.program_id(0) * ts, BLOCK), ts)
        acc = gx0_ref[...]
        for i in (0, 1, 3, 4, 5, 6, 7, 8):
            lo, hi = OFF[i], OFF[i + 1]
            if i == 1:
                dk = _rope_t(piece_refs[1][kv_rows, :], c_ref[...], s1_ref[...], s2_ref[...])
                t16 = jnp.concatenate([dk, piece_refs[2][kv_rows, :]], axis=1).astype(MXU_DTYPE)
                hi = OFF[3]
            else:
                t16 = piece_refs[i][...].astype(MXU_DTYPE)
            dp_ref[:, lo:hi] = t16
            acc += _mm(t16, w_ref[lo:hi, :])
        gx_ref[...] = acc

    return pl.pallas_call(
        body, name="in_proj_bwd_x", grid=(s // ts,),
        in_specs=[_rows(ts, D_MODEL)]
        + [_full((s + BLOCK, w)) if i in (1, 2) else _rows(ts, w) for i, w in enumerate(widths)]
        + [_full((D_IN_PROJ, D_MODEL))] + [_rows(ts, 128)] * 3,
        out_specs=[_rows(ts, D_MODEL), _rows(ts, D_IN_PROJ)],
        out_shape=[jax.ShapeDtypeStruct((s, D_MODEL), F32), jax.ShapeDtypeStruct((s, D_IN_PROJ), MXU_DTYPE)],
        compiler_params=_cparams(dimension_semantics=("arbitrary",)),
    )(gx0, *pieces, w_in, *rope)


def _in_proj_bwd_w(x, dproj):
    s = x.shape[0]
    ts = min(1024, s)
    nsteps = s // ts
    col_chunks = [(OFF[i], OFF[i + 2] if i == 1 else OFF[i + 1]) for i in (0, 1, 3, 4, 5, 6, 7, 8)]

    def body(x_ref, dp_ref, gw_ref, acc_ref, stage_ref, sems):
        i = pl.program_id(0)

        @pl.when(i == 0)
        def _():
            acc_ref[...] = jnp.zeros_like(acc_ref)

        xb = x_ref[...].astype(MXU_DTYPE)
        for lo, hi in col_chunks:
            acc_ref[lo:hi, :] += _mm_tn(dp_ref[:, lo:hi], xb)

        @pl.when(i == nsteps - 1)
        def _():
            copies = []
            for j in range(N_DEV):
                slot = j % 2
                if j >= 2:
                    copies[j - 2].wait()
                stage_ref[slot] = acc_ref[D_IN_SHARD * j:D_IN_SHARD * (j + 1), :]
                cp = pltpu.make_async_copy(stage_ref.at[slot], gw_ref.at[j], sems.at[slot])
                cp.start()
                copies.append(cp)
            copies[N_DEV - 2].wait()
            copies[N_DEV - 1].wait()

    return pl.pallas_call(
        body, name="in_proj_bwd_w", grid=(nsteps,),
        in_specs=[_rows(ts, D_MODEL), _rows(ts, D_IN_PROJ)],
        out_specs=pl.BlockSpec(memory_space=pl.ANY),
        out_shape=jax.ShapeDtypeStruct((N_DEV, D_IN_SHARD, D_MODEL), F32),
        scratch_shapes=[pltpu.VMEM((D_IN_PROJ, D_MODEL), F32), pltpu.VMEM((2, D_IN_SHARD, D_MODEL), F32),
                        pltpu.SemaphoreType.DMA((2,))],
        compiler_params=_cparams(dimension_semantics=("arbitrary",)),
    )(x, dproj)


def _local_step(x, positions, w_in, wg, b_gate, sinks, norm_w, w_out_s, ln_g, ln_b, target):
    rope = _rope_tables(positions)
    qa, k_pad, v_pad, ga, qb, kb, vb, gb, rb, la, oms, w_out = _in_proj(x, w_in, wg, b_gate, rope, w_out_s)
    attn, cat_a = _swa_fwd(sinks, qa, k_pad, v_pad, ga)
    o, cat_b, sprev = _gla_fwd(qb, kb, vb, la, gb, norm_w)
    loss, gx0, d_cat_a, d_cat_b, g_w_out, g_ln = _out_ln_loss(cat_a, cat_b, w_out, x, target, ln_g, ln_b)
    parts_w_out = g_w_out.reshape(N_DEV, D_OUT_SHARD, D_MODEL)
    dqa, dga, dk_pad, dv_pad, g_sinks, g_out = _swa_bwd(sinks, qa, k_pad, v_pad, attn, ga, d_cat_a, rope, parts_w_out)
    dqb, dkb, dvb, dgb, drb, g_wg, g_bg, g_nw = _gla_bwd(qb, kb, vb, la, oms, gb, o, sprev, d_cat_b, rb, wg, norm_w)
    pieces = (dqa, dk_pad, dv_pad, dga, dqb, dkb, dvb, dgb, drb)
    grad_x, dproj = _in_proj_bwd_x(gx0, pieces, w_in, rope)
    g_w_in = _in_proj_bwd_w(x, dproj)
    return loss, grad_x, g_w_in, g_wg, g_bg, g_sinks, g_nw, g_out, g_ln


def _mesh_pos():
    return lax.axis_index("x"), lax.axis_index("y"), lax.axis_index("c")


def _peer(k, x, y, c):
    px = (1 - x) if k & 4 else x
    py = (1 - y) if k & 2 else y
    pc = (1 - c) if k & 1 else c
    return (px, py, pc), 4 * px + 2 * py + pc


def _other_chips(x, y):
    return [(1 - x, y), (x, 1 - y), (1 - x, 1 - y)]


def _shard_view(t):
    return jnp.transpose(t, (2, 0, 1))


class _BlockGather:
    def __init__(self, slots, send_sems, recv_sems):
        self.slots, self.send_sems, self.recv_sems = slots, send_sems, recv_sems
        x, y, c = _mesh_pos()
        self.xy, self.c, self.me, self.sibling = (x, y), c, 4 * x + 2 * y + c, (x, y, 1 - c)
        self.chips = _other_chips(x, y)

    @staticmethod
    def scratch():
        return [pltpu.SemaphoreType.DMA((N_DEV - 1,)), pltpu.SemaphoreType.DMA((N_DEV - 1,))]

    def _copy(self, k, block, to):
        return pltpu.make_async_remote_copy(
            src_ref=self.slots.at[block], dst_ref=self.slots.at[block], send_sem=self.send_sems.at[k],
            recv_sem=self.recv_sems.at[k], device_id=to, device_id_type=pl.DeviceIdType.MESH)

    def start(self):
        for j, (cx, cy) in enumerate(self.chips):
            self._copy(1 + j, self.me, (cx, cy, self.c)).start()
        self._copy(0, self.me, self.sibling).start()

    def forward(self):
        for j, (cx, cy) in enumerate(self.chips):
            block = 4 * cx + 2 * cy + self.c
            self._copy(1 + j, block, self.sibling).wait_recv()
            self._copy(4 + j, block, self.sibling).start()

    def finish(self):
        x, y = self.xy
        self._copy(0, 4 * x + 2 * y + (1 - self.c), self.sibling).wait_recv()
        for j, (cx, cy) in enumerate(self.chips):
            self._copy(4 + j, 4 * cx + 2 * cy + (1 - self.c), self.sibling).wait_recv()
        for k in range(N_DEV - 1):
            self._copy(k, self.me, self.sibling).wait_send()


def _all_gather_weights(w_in_t, wg_s):
    def body(win_hbm, wg_ref, win_full, wg_full, win_all, wg_all, stage, stage_sem, *sems):
        gathers = (_BlockGather(win_all, *sems[0:2]), _BlockGather(wg_all, *sems[2:4]))
        me = gathers[0].me
        load = pltpu.make_async_copy(win_hbm.at[:, 0, :], stage, stage_sem)
        load.start()
        load.wait()
        win_all[me] = stage[...].astype(win_all.dtype)
        wg_all[me] = wg_ref[...].astype(wg_all.dtype)
        for stage_of in ("start", "forward", "finish"):
            for gather in gathers:
                getattr(gather, stage_of)()
        for j in range(N_DEV):
            win_full[D_IN_SHARD * j:D_IN_SHARD * (j + 1), :] = win_all[j]
            wg_full[:, 32 * j:32 * (j + 1)] = wg_all[j]

    vmem = pl.BlockSpec(memory_space=pltpu.VMEM)
    return pl.pallas_call(
        body, name="all_gather_weights",
        in_specs=[pl.BlockSpec(memory_space=pl.ANY), vmem], out_specs=[vmem] * 2,
        out_shape=[jax.ShapeDtypeStruct((D_IN_PROJ, D_MODEL), MXU_DTYPE),
                   jax.ShapeDtypeStruct((GLA_RANK, 256), MXU_DTYPE)],
        scratch_shapes=[pltpu.VMEM((N_DEV, D_IN_SHARD, D_MODEL), MXU_DTYPE),
                        pltpu.VMEM((N_DEV, GLA_RANK, 32), MXU_DTYPE),
                        pltpu.VMEM((D_IN_SHARD, D_MODEL), F32), pltpu.SemaphoreType.DMA]
        + _BlockGather.scratch() + _BlockGather.scratch(),
        compiler_params=_cparams(),
    )(w_in_t, wg_s)


class _OwnerSum:
    def __init__(self, parts, own, sib, snd, rcv, loc_sems, d2d_send, d2d_recv, ici_send, ici_recv):
        self.parts, self.own, self.sib, self.snd, self.rcv = parts, own, sib, snd, rcv
        self.sems = (loc_sems, d2d_send, d2d_recv, ici_send, ici_recv)
        x, y, c = _mesh_pos()
        self.c, self.sibling = c, (x, y, 1 - c)
        self.chips = [(x, y)] + _other_chips(x, y)

    @staticmethod
    def scratch(block):
        return [pltpu.VMEM((4,) + block, F32), pltpu.VMEM((4,) + block, F32),
                pltpu.VMEM((3,) + block, MXU_DTYPE), pltpu.VMEM((3,) + block, MXU_DTYPE),
                pltpu.SemaphoreType.DMA((4,)), pltpu.SemaphoreType.DMA((4,)), pltpu.SemaphoreType.DMA((4,)),
                pltpu.SemaphoreType.DMA((3,)), pltpu.SemaphoreType.DMA((3,))]

    def _local(self, r):
        cx, cy = self.chips[r]
        return pltpu.make_async_copy(self.parts.at[4 * cx + 2 * cy + self.c], self.own.at[r], self.sems[0].at[r])

    def _d2d(self, r):
        cx, cy = self.chips[r]
        return pltpu.make_async_remote_copy(
            src_ref=self.parts.at[4 * cx + 2 * cy + (1 - self.c)], dst_ref=self.sib.at[r], send_sem=self.sems[1].at[r],
            recv_sem=self.sems[2].at[r], device_id=self.sibling, device_id_type=pl.DeviceIdType.MESH)

    def _ici(self, r):
        cx, cy = self.chips[r]
        return pltpu.make_async_remote_copy(
            src_ref=self.snd.at[r - 1], dst_ref=self.rcv.at[r - 1], send_sem=self.sems[3].at[r - 1],
            recv_sem=self.sems[4].at[r - 1], device_id=(cx, cy, self.c), device_id_type=pl.DeviceIdType.MESH)

    def start(self):
        for r in (1, 2, 3, 0):
            self._local(r).start()
            self._d2d(r).start()

    def forward(self):
        for r in (1, 2, 3):
            self._local(r).wait()
            self._d2d(r).wait_recv()
            self.snd[r - 1] = (self.own[r] + self.sib[r]).astype(self.snd.dtype)
            self._ici(r).start()

    def finish(self):
        self._local(0).wait()
        self._d2d(0).wait_recv()
        acc = self.own[0] + self.sib[0]
        for r in (1, 2, 3):
            self._ici(r).wait_recv()
            acc = acc + self.rcv[r - 1].astype(F32)
        for r in range(4):
            self._d2d(r).wait_send()
        for r in (1, 2, 3):
            self._ici(r).wait_send()
        return acc


SMALL_ROWS = 48


def _reduce_grads(parts_w_in, parts_wg, g_ln, g_bg, g_nw, g_sinks, loss):
    def body(pin_ref, pwg_ref, gln_ref, gbg_ref, gnw_ref, gsk_ref, loss_ref, gin_ref, rwg_ref, rsm_ref, sm_ref,
             *scratch):
        sm_send, sm_recv, sm_loc = scratch[-3:]
        x, y, c = _mesh_pos()
        me = 4 * x + 2 * y + c
        owner_sum = _OwnerSum(pin_ref, *scratch[:-3])

        sm_ref[...] = jnp.zeros_like(sm_ref)
        for r in range(D_MODEL // 128):
            sm_ref[r:r + 1, :] = gln_ref[0:1, 128 * r:128 * (r + 1)]
            sm_ref[8 + r:9 + r, :] = gln_ref[1:2, 128 * r:128 * (r + 1)]
        for r in range(2):
            sm_ref[16 + r:17 + r, :] = gbg_ref[0:1, 128 * r:128 * (r + 1)]
        sm_ref[24:25, :] = gnw_ref[...]
        diag = lax.broadcasted_iota(jnp.int32, gsk_ref.shape, 0) == lax.broadcasted_iota(jnp.int32, gsk_ref.shape, 1)
        sm_ref[32:33, :] = jnp.sum(jnp.where(diag, gsk_ref[...], 0.0), axis=0, keepdims=True)
        sm_ref[40:41, :] = loss_ref[...]

        small_dsts = (rwg_ref, rsm_ref)

        def small_src(a, block):
            return pwg_ref.at[block] if a == 0 else sm_ref

        small_local = [pltpu.make_async_copy(small_src(a, me), small_dsts[a].at[me], sm_loc.at[a]) for a in range(2)]
        for cp in small_local:
            cp.start()
        small_sends = []
        for k in range(1, N_DEV):
            peer, pidx = _peer(k, x, y, c)
            for a in range(2):
                i = 2 * (k - 1) + a
                cp = pltpu.make_async_remote_copy(
                    src_ref=small_src(a, pidx), dst_ref=small_dsts[a].at[me], send_sem=sm_send.at[i],
                    recv_sem=sm_recv.at[i], device_id=peer, device_id_type=pl.DeviceIdType.MESH)
                cp.start()
                small_sends.append(cp)

        owner_sum.start()
        owner_sum.forward()
        gin_ref[...] = owner_sum.finish()

        for k in range(1, N_DEV):
            peer, pidx = _peer(k, x, y, c)
            for a in range(2):
                i = 2 * (k - 1) + a
                pltpu.make_async_remote_copy(
                    src_ref=small_src(a, me), dst_ref=small_dsts[a].at[pidx], send_sem=sm_send.at[i],
                    recv_sem=sm_recv.at[i], device_id=peer, device_id_type=pl.DeviceIdType.MESH).wait_recv()
        for cp in small_sends:
            cp.wait_send()
        for cp in small_local:
            cp.wait()

    hbm = pl.BlockSpec(memory_space=pl.ANY)
    vmem = pl.BlockSpec(memory_space=pltpu.VMEM)
    in_blk = parts_w_in.shape[1:]
    return pl.pallas_call(
        body, name="reduce_grads",
        in_specs=[hbm, hbm] + [vmem] * 5, out_specs=[vmem, hbm, hbm],
        out_shape=[jax.ShapeDtypeStruct(in_blk, F32),
                   jax.ShapeDtypeStruct((N_DEV,) + parts_wg.shape[1:], F32),
                   jax.ShapeDtypeStruct((N_DEV, SMALL_ROWS, 128), F32)],
        scratch_shapes=[pltpu.VMEM((SMALL_ROWS, 128), F32)] + _OwnerSum.scratch(in_blk)
        + [pltpu.SemaphoreType.DMA((2 * (N_DEV - 1),)), pltpu.SemaphoreType.DMA((2 * (N_DEV - 1),)),
           pltpu.SemaphoreType.DMA((2,))],
        compiler_params=_cparams(),
    )(parts_w_in, parts_wg, g_ln, g_bg, g_nw, g_sinks, loss)


def _adamw(recv, w, m, v, name):
    rows, width = w.shape
    tr = 128 if rows % 128 == 0 else rows
    n_parts = recv.shape[0]

    def body(r_ref, w_ref, m_ref, v_ref, g_ref, d_ref, nm_ref, nv_ref):
        g = r_ref[0]
        for j in range(1, n_parts):
            g = g + r_ref[j]
        nm = ADAM_B1 * m_ref[...] + (1.0 - ADAM_B1) * g
        nv = ADAM_B2 * v_ref[...] + (1.0 - ADAM_B2) * (g * g)
        m_hat = nm / (1.0 - ADAM_B1 ** ADAM_STEP)
        v_hat = nv / (1.0 - ADAM_B2 ** ADAM_STEP)
        g_ref[...] = g
        d_ref[...] = -ADAM_LR * (m_hat / (jnp.sqrt(v_hat) + ADAM_EPS) + ADAM_WD * w_ref[...])
        nm_ref[...] = nm
        nv_ref[...] = nv

    spec = _rows(tr, width)
    return pl.pallas_call(
        body, name=name, grid=(rows // tr,),
        in_specs=[pl.BlockSpec((n_parts, tr, width), lambda i: (0, i, 0)), spec, spec, spec],
        out_specs=[spec] * 4,
        out_shape=[jax.ShapeDtypeStruct((rows, width), F32)] * 4,
        compiler_params=_cparams(dimension_semantics=("arbitrary",)),
    )(recv, w, m, v)


def _adamw_shard_view(g, w, m, v):
    rows, width = g.shape

    def body(g_ref, w_hbm, m_hbm, v_hbm, g_out, d_out, nm_out, nv_out, bufs, outs, sems):
        loads = [pltpu.make_async_copy(src.at[:, 0, :], bufs.at[i], sems.at[i])
                 for i, src in enumerate((w_hbm, m_hbm, v_hbm))]
        for cp in loads:
            cp.start()
        g = g_ref[...]
        for cp in loads:
            cp.wait()
        nm = ADAM_B1 * bufs[1] + (1.0 - ADAM_B1) * g
        nv = ADAM_B2 * bufs[2] + (1.0 - ADAM_B2) * (g * g)
        m_hat = nm / (1.0 - ADAM_B1 ** ADAM_STEP)
        v_hat = nv / (1.0 - ADAM_B2 ** ADAM_STEP)
        outs[0] = g
        outs[1] = -ADAM_LR * (m_hat / (jnp.sqrt(v_hat) + ADAM_EPS) + ADAM_WD * bufs[0])
        outs[2] = nm
        outs[3] = nv
        stores = [pltpu.make_async_copy(outs.at[i], dst.at[:, 0, :], sems.at[3 + i])
                  for i, dst in enumerate((g_out, d_out, nm_out, nv_out))]
        for cp in stores:
            cp.start()
        for cp in stores:
            cp.wait()

    hbm = pl.BlockSpec(memory_space=pl.ANY)
    return pl.pallas_call(
        body, name="adamw_w_in",
        in_specs=[pl.BlockSpec(memory_space=pltpu.VMEM), hbm, hbm, hbm], out_specs=[hbm] * 4,
        out_shape=[jax.ShapeDtypeStruct((rows, 1, width), F32)] * 4,
        scratch_shapes=[pltpu.VMEM((3, rows, width), F32), pltpu.VMEM((4, rows, width), F32),
                        pltpu.SemaphoreType.DMA((7,))],
        compiler_params=_cparams(),
    )(g, w, m, v)


def _adamw_vectors(r_small, r_wg, params):
    n_par = len(params)

    def body(rsm_ref, rwg_ref, *refs):
        ins, outs = refs[:3 * n_par], refs[3 * n_par:]
        g = rsm_ref[0]
        gwg = rwg_ref[0]
        for j in range(1, N_DEV):
            g = g + rsm_ref[j]
            gwg = gwg + rwg_ref[j]
        grads = [gwg,
                 jnp.concatenate([g[r:r + 1] for r in range(0, 8)], axis=1),
                 jnp.concatenate([g[r:r + 1] for r in range(8, 16)], axis=1),
                 jnp.concatenate([g[16:17], g[17:18]], axis=1),
                 g[24:25],
                 g[32:33, 0:SWA_Q_HEADS]]
        for p, gp in enumerate(grads):
            w_ref, m_ref, v_ref = ins[3 * p:3 * p + 3]
            nm = ADAM_B1 * m_ref[...] + (1.0 - ADAM_B1) * gp
            nv = ADAM_B2 * v_ref[...] + (1.0 - ADAM_B2) * (gp * gp)
            m_hat = nm / (1.0 - ADAM_B1 ** ADAM_STEP)
            v_hat = nv / (1.0 - ADAM_B2 ** ADAM_STEP)
            outs[4 * p][...] = gp
            outs[4 * p + 1][...] = -ADAM_LR * (m_hat / (jnp.sqrt(v_hat) + ADAM_EPS) + ADAM_WD * w_ref[...])
            outs[4 * p + 2][...] = nm
            outs[4 * p + 3][...] = nv

    vmem = pl.BlockSpec(memory_space=pltpu.VMEM)
    flat = [t for wmv in params for t in wmv]
    return pl.pallas_call(
        body, name="adamw_vectors",
        in_specs=[vmem] * (2 + len(flat)), out_specs=[vmem] * (4 * n_par),
        out_shape=[jax.ShapeDtypeStruct(wmv[0].shape, F32) for wmv in params for _ in range(4)],
        compiler_params=_cparams(),
    )(r_small, r_wg, *flat)


def kernel(x, positions, w_in, gla_w_gate_up, gla_b_gate, attn_sinks, gla_norm_w, w_out, ln_g, ln_b, loss_target, m_w_in, m_gla_w_gate_up, m_gla_b_gate, m_attn_sinks, m_gla_norm_w, m_w_out, m_ln_g, m_ln_b, v_w_in, v_gla_w_gate_up, v_gla_b_gate, v_attn_sinks, v_gla_norm_w, v_w_out, v_ln_g, v_ln_b):
    w_in_full, wg_full = _all_gather_weights(_shard_view(w_in), gla_w_gate_up[0])

    loss, grad_x, parts_w_in, g_wg, g_bg, g_sinks, g_nw, g_out, g_ln = _local_step(
        x[0], positions[0], w_in_full, wg_full, gla_b_gate, attn_sinks[0], gla_norm_w, w_out[0], ln_g, ln_b,
        loss_target[0])

    parts_wg = jnp.transpose(g_wg.reshape(GLA_RANK, N_DEV, 32), (1, 0, 2))
    g_in, r_wg, r_small = _reduce_grads(parts_w_in, parts_wg, g_ln, g_bg, g_nw, g_sinks, loss)

    upd_in = _adamw_shard_view(g_in, _shard_view(w_in), _shard_view(m_w_in), _shard_view(v_w_in))
    upd_in = [jnp.transpose(t, (1, 2, 0)) for t in upd_in]
    upd_out = _adamw(g_out[None], w_out[0], m_w_out[0], v_w_out[0], "adamw_w_out")
    vec = _adamw_vectors(r_small, r_wg, [
        (gla_w_gate_up[0], m_gla_w_gate_up[0], v_gla_w_gate_up[0]), (ln_g, m_ln_g, v_ln_g), (ln_b, m_ln_b, v_ln_b),
        (gla_b_gate, m_gla_b_gate, v_gla_b_gate), (gla_norm_w, m_gla_norm_w, v_gla_norm_w),
        (attn_sinks, m_attn_sinks, v_attn_sinks)])

    total = jnp.sum(r_small[:, 40, 0])
    outs = [total, grad_x[None]]
    for kind in range(4):
        u_wg, u_ln_g, u_ln_b, u_bg, u_nw, u_sinks = (vec[4 * p + kind] for p in range(6))
        outs += [upd_in[kind], u_wg[None], u_bg, u_sinks, u_nw, upd_out[kind][None], u_ln_g, u_ln_b]
    return tuple(outs)
```

```python
import jax
import jax.numpy as jnp
from jax import lax
from jax.experimental import pallas as pl
from jax.experimental.pallas import tpu as pltpu

F32 = jnp.float32
MXU_DTYPE = jnp.bfloat16

N_DEV = 8
D_MODEL = 1024
SWA_Q_HEADS = 8
SWA_KV_HEADS = 2
SWA_GROUP = 4
SWA_HEAD_DIM = 64
BLOCK = 128
ROPE_THETA = 500000.0
ROT_DIM = 16
GLA_HEADS = 4
GLA_DK = 64
GLA_DV = 128
GLA_RANK = 16
GLA_TAU = 16.0
GLA_CHUNK = 64
D_IN_PROJ = 2832
D_IN_SHARD = D_IN_PROJ // N_DEV
D_OUT_SHARD = D_MODEL // N_DEV
OFF = (0, 512, 640, 768, 1280, 1536, 1792, 2304, 2816, 2832)
EPS = 1e-5
ALPHA = 2.0 ** 0.25
SWA_SCALE = SWA_HEAD_DIM ** -0.5
GLA_SCALE = GLA_DK ** -0.5
ADAM_LR = 0.001
ADAM_B1 = 0.9
ADAM_B2 = 0.999
ADAM_EPS = 1e-08
ADAM_WD = 0.01
ADAM_STEP = 10
VMEM_LIMIT = 56 * 1024 * 1024

_NT = (((1,), (1,)), ((), ()))
_TN = (((0,), (0,)), ((), ()))


def _mm(a, b):
    return jnp.dot(a, b, preferred_element_type=F32)


def _mm_nt(a, b):
    return lax.dot_general(a, b, _NT, preferred_element_type=F32)


def _mm_tn(a, b):
    return lax.dot_general(a, b, _TN, preferred_element_type=F32)


def _sigmoid(t):
    return 1.0 / (1.0 + jnp.exp(-t))


def _cparams(**kw):
    return pltpu.CompilerParams(vmem_limit_bytes=VMEM_LIMIT, **kw)


def _full(shape):
    return pl.BlockSpec(shape, lambda *_: (0,) * len(shape))


def _rows(tile, width):
    return pl.BlockSpec((tile, width), lambda i: (i, 0))


def _rope_angles(positions):
    half = ROT_DIM // 2
    inv_freq = ROPE_THETA ** (-jnp.arange(half, dtype=F32) / half)
    ang = positions.astype(F32)[:, None] * inv_freq[None, :]
    return jnp.concatenate([jnp.cos(ang), jnp.sin(ang)], axis=1)


def _split3_parts(t):
    hi = t.astype(MXU_DTYPE)
    r1 = t - hi.astype(F32)
    mid = r1.astype(MXU_DTYPE)
    return hi, mid, (r1 - mid.astype(F32)).astype(MXU_DTYPE)


def _rope_tables(cs):
    half = ROT_DIM // 2
    i = lax.broadcasted_iota(jnp.int32, (2 * half, 3 * 128), 0)
    lane = lax.broadcasted_iota(jnp.int32, (2 * half, 3 * 128), 1)
    table, pos = _idiv(lane, 128), lane & (SWA_HEAD_DIM - 1)
    is_c = (table == 0) & (pos < ROT_DIM) & ((pos & (half - 1)) == i)
    is_s1 = (table == 1) & (pos < half) & (pos + half == i)
    is_s2 = (table == 2) & (pos >= half) & (pos < ROT_DIM) & (pos == i)
    sel = jnp.where(is_c | is_s2, 1.0, jnp.where(is_s1, -1.0, 0.0)).astype(MXU_DTYPE)
    hi, mid, lo = _split3_parts(cs)
    t = (_mm(hi, sel) + _mm(mid, sel)) + _mm(lo, sel)
    pos1 = lax.broadcasted_iota(jnp.int32, (1, 128), 1) & (SWA_HEAD_DIM - 1)
    return t[:, 0:128] + jnp.where(pos1 >= ROT_DIM, 1.0, 0.0), t[:, 128:256], t[:, 256:384]


def _rope(t, c, s1, s2):
    return t * c + pltpu.roll(t, 120, 1) * s1 + pltpu.roll(t, 8, 1) * s2


def _rope_t(g, c, s1, s2):
    return g * c + pltpu.roll(g * s1, 8, 1) + pltpu.roll(g * s2, 120, 1)


def _in_proj(x, w_in, wg, b_gate, cos_sin, w_out_s):
    s = x.shape[0]
    ts = min(512, s)
    nsteps = s // ts
    forward_step = min(3, nsteps - 1)
    widths = [OFF[i + 1] - OFF[i] for i in range(9)]

    def body(x_ref, w_ref, wg_ref, bg_ref, cs_ref, wos_ref,
             qa_ref, ka_ref, va_ref, ga_ref, qb_ref, kb_ref, vb_ref, gb_ref, rb_ref, la_ref, oms_ref,
             c_ref, s1_ref, s2_ref, wout_ref, wout_all, send_sems, recv_sems):
        xb = x_ref[...].astype(MXU_DTYPE)
        c, s1, s2 = _rope_tables(cs_ref[...])
        c_ref[...], s1_ref[...], s2_ref[...] = c, s1, s2
        i0 = pl.program_id(0)
        gather = _BlockGather(wout_all, send_sems, recv_sems)

        @pl.when(i0 == 0)
        def _():
            ka_ref[0:BLOCK, :] = jnp.zeros((BLOCK, 128), ka_ref.dtype)
            va_ref[0:BLOCK, :] = jnp.zeros((BLOCK, 128), va_ref.dtype)
            wout_all[gather.me] = wos_ref[...].astype(wout_all.dtype)
            gather.start()

        @pl.when(i0 == forward_step)
        def _():
            gather.forward()

        @pl.when(i0 == nsteps - 1)
        def _():
            gather.finish()
            for j in range(N_DEV):
                wout_ref[D_OUT_SHARD * j:D_OUT_SHARD * (j + 1), :] = wout_all[j]

        kv_rows = pl.ds(pl.multiple_of(BLOCK + i0 * ts, BLOCK), ts)

        def cols(i):
            return _mm_nt(xb, w_ref[OFF[i]:OFF[i + 1], :])

        qa = cols(0)
        for i in range(4):
            qa_ref[:, 128 * i:128 * (i + 1)] = _rope(qa[:, 128 * i:128 * (i + 1)], c, s1, s2).astype(qa_ref.dtype)
        kv = _mm_nt(xb, w_ref[OFF[1]:OFF[3], :])
        ka_ref[kv_rows, :] = _rope(kv[:, 0:128], c, s1, s2).astype(ka_ref.dtype)
        va_ref[kv_rows, :] = kv[:, 128:256].astype(va_ref.dtype)
        ga_ref[...] = cols(3)
        qb_ref[...] = cols(4)
        kb_ref[...] = cols(5)
        vb_ref[...] = cols(6).astype(vb_ref.dtype)
        gb_ref[...] = cols(7)
        rb = cols(8)
        rb_ref[...] = rb
        logit = _mm(rb.astype(MXU_DTYPE), wg_ref[...]) + bg_ref[...]
        e = jnp.exp(-jnp.abs(logit))
        la_ref[...] = (jnp.minimum(logit, 0.0) - jnp.log(1.0 + e)) / GLA_TAU
        oms_ref[...] = jnp.where(logit >= 0.0, e, 1.0) / (1.0 + e)

    out_shape = [jax.ShapeDtypeStruct((s + BLOCK if i in (1, 2) else s, w), MXU_DTYPE if i in (0, 1, 2, 6) else F32)
                 for i, w in enumerate(widths)]
    out_shape += [jax.ShapeDtypeStruct((s, 256), F32)] * 2 + [jax.ShapeDtypeStruct((s, 128), F32)] * 3
    out_shape += [jax.ShapeDtypeStruct((D_MODEL, D_MODEL), MXU_DTYPE)]
    return pl.pallas_call(
        body, name="in_proj", grid=(nsteps,),
        in_specs=[_rows(ts, D_MODEL), _full((D_IN_PROJ, D_MODEL)), _full((GLA_RANK, 256)), _full((1, 256)),
                  _rows(ts, ROT_DIM), _full((D_OUT_SHARD, D_MODEL))],
        out_specs=[_full((s + BLOCK, w)) if i in (1, 2) else _rows(ts, w) for i, w in enumerate(widths)]
        + [_rows(ts, 256)] * 2 + [_rows(ts, 128)] * 3 + [_full((D_MODEL, D_MODEL))],
        out_shape=out_shape,
        scratch_shapes=[pltpu.VMEM((N_DEV, D_OUT_SHARD, D_MODEL), MXU_DTYPE)] + _BlockGather.scratch(),
        compiler_params=_cparams(dimension_semantics=("arbitrary",)),
    )(x, w_in, wg, b_gate, cos_sin, w_out_s)


SWA_ROWS = SWA_GROUP * BLOCK


def _swa_bias():
    shape = (2, 2 * BLOCK, SWA_ROWS)
    ki = lax.broadcasted_iota(jnp.int32, shape, 1)
    qi = lax.broadcasted_iota(jnp.int32, shape, 2) & (BLOCK - 1)
    first = lax.broadcasted_iota(jnp.int32, shape, 0) == 0
    dist = qi + BLOCK - ki
    ok = (dist >= 0) & (dist < BLOCK) & (jnp.logical_not(first) | (ki >= BLOCK))
    return jnp.where(ok, 0.0, -jnp.inf).astype(F32)


SWA_SUB = 2


def _swa_bias_of(bias_ref, n, b):
    return bias_ref[jnp.minimum(n, 1)] if b == 0 else bias_ref[1]


def _swa_dup(t, j):
    t = t.astype(F32)
    low = lax.broadcasted_iota(jnp.int32, t.shape, 1) < SWA_HEAD_DIM
    keep = low if j == 0 else jnp.logical_not(low)
    return jnp.where(keep, t, pltpu.roll(t, SWA_HEAD_DIM, 1)).astype(MXU_DTYPE)


def _swa_stack(t, j):
    low = lax.broadcasted_iota(jnp.int32, (BLOCK, 128), 1) < SWA_HEAD_DIM
    zero = jnp.zeros((BLOCK, 128), t.dtype)
    blocks = []
    for p in (2 * j, 2 * j + 1):
        tp = t[:, 128 * p:128 * (p + 1)]
        blocks += [jnp.where(low, tp, zero), jnp.where(low, zero, tp)]
    return jnp.concatenate(blocks, axis=0)


def _swa_unstack(t):
    low = lax.broadcasted_iota(jnp.int32, (BLOCK, 128), 1) < SWA_HEAD_DIM
    return [jnp.where(low, t[2 * BLOCK * i:2 * BLOCK * i + BLOCK], t[2 * BLOCK * i + BLOCK:2 * BLOCK * (i + 1)])
            for i in range(2)]


def _swa_sink_row(sink_ref, j):
    lane = lax.broadcasted_iota(jnp.int32, (1, SWA_ROWS), 1)
    row = jnp.full((1, SWA_ROWS), sink_ref[SWA_GROUP * j], F32)
    for r in range(1, SWA_GROUP):
        row = jnp.where(lane >= BLOCK * r, sink_ref[SWA_GROUP * j + r], row)
    return row


def _split3(t):
    return jnp.concatenate(_split3_parts(t), axis=1)


def _row_sums_as_row(t):
    ones = jnp.ones((8, 3 * t.shape[1]), MXU_DTYPE)
    return _mm_nt(ones, _split3(t))[0:1, :]


def _swa_probs_t(qs, kd, bias_t, sink):
    sc = _mm_nt(kd, qs) + bias_t
    m = jnp.maximum(jnp.max(sc, axis=0, keepdims=True), sink)
    p = jnp.exp(sc - m)
    ps = jnp.exp(sink - m)
    rinv = 1.0 / (jnp.sum(p, axis=0, keepdims=True) + ps)
    return p * rinv, ps * rinv


def _swa_fwd(sinks, qa, k_pad, v_pad, ga):
    s = qa.shape[0]
    tq = SWA_SUB * BLOCK

    def body(sink_ref, qa_ref, ga_ref, bias_ref, k_ref, v_ref, attn_ref, cat_ref):
        n = pl.program_id(0)
        for b in range(SWA_SUB):
            rows = slice(BLOCK * b, BLOCK * (b + 1))
            start = pl.multiple_of((n * SWA_SUB + b) * BLOCK, BLOCK)
            kw = k_ref[pl.ds(start, 2 * BLOCK), :]
            vw = v_ref[pl.ds(start, 2 * BLOCK), :]
            bias_t = _swa_bias_of(bias_ref, n, b)
            q = qa_ref[rows, :] * SWA_SCALE
            g = ga_ref[rows, :]
            silu = g * _sigmoid(g)
            for j in range(SWA_KV_HEADS):
                qs = _swa_stack(q, j).astype(MXU_DTYPE)
                probs, _ = _swa_probs_t(qs, _swa_dup(kw, j), bias_t, _swa_sink_row(sink_ref, j))
                pairs = _swa_unstack(_mm_tn(probs.astype(MXU_DTYPE), _swa_dup(vw, j)))
                for i in range(2):
                    lanes = slice(128 * (2 * j + i), 128 * (2 * j + i + 1))
                    attn_ref[rows, lanes] = pairs[i]
                    cat_ref[rows, lanes] = (pairs[i] * silu[:, lanes]).astype(cat_ref.dtype)

    return pl.pallas_call(
        body, name="swa_fwd", grid=(s // tq,),
        in_specs=[pl.BlockSpec(memory_space=pltpu.SMEM), _rows(tq, 512), _rows(tq, 512),
                  _full((2, 2 * BLOCK, SWA_ROWS)), _full((s + BLOCK, 128)), _full((s + BLOCK, 128))],
        out_specs=[_rows(tq, 512), _rows(tq, 512)],
        out_shape=[jax.ShapeDtypeStruct((s, 512), F32), jax.ShapeDtypeStruct((s, 512), MXU_DTYPE)],
        compiler_params=_cparams(dimension_semantics=("arbitrary",)),
    )(sinks, qa, ga, _swa_bias(), k_pad, v_pad)


GLA_KW = GLA_HEADS * GLA_DK
GLA_VW = GLA_HEADS * GLA_DV


def _idiv(t, d):
    return t >> (d.bit_length() - 1)


def _chunk_cumsum(t, lower):
    n, w = t.shape
    r = lax.broadcasted_iota(jnp.int32, (n, n), 0)
    c = lax.broadcasted_iota(jnp.int32, (n, n), 1)
    tri = ((_idiv(r, GLA_CHUNK) == _idiv(c, GLA_CHUNK)) & ((r >= c) if lower else (r <= c))).astype(MXU_DTYPE)
    parts = _mm(tri, _split3(t))
    return (parts[:, :w] + parts[:, w:2 * w]) + parts[:, 2 * w:]


def _chunk_last(t):
    n = t.shape[0]
    return jnp.concatenate(
        [jnp.broadcast_to(t[c + GLA_CHUNK - 1:c + GLA_CHUNK, :], (GLA_CHUNK, t.shape[1]))
         for c in range(0, n, GLA_CHUNK)], axis=0)


def _head_stack(t, width):
    head = _idiv(lax.broadcasted_iota(jnp.int32, t.shape, 1), width)
    zero = jnp.zeros_like(t)
    return jnp.concatenate([jnp.where(head == h, t, zero) for h in range(GLA_HEADS)], axis=0)


def _heads_to_rows(t):
    return jnp.concatenate([t[:, GLA_DV * h:GLA_DV * (h + 1)] for h in range(GLA_HEADS)], axis=0)


def _rows_to_heads(t):
    return jnp.concatenate([t[GLA_CHUNK * h:GLA_CHUNK * (h + 1)] for h in range(GLA_HEADS)], axis=1)


def _state_by_head(t):
    srow = _idiv(lax.broadcasted_iota(jnp.int32, (GLA_VW, GLA_KW), 0), GLA_DV)
    slane = _idiv(lax.broadcasted_iota(jnp.int32, (GLA_VW, GLA_KW), 1), GLA_DK)
    return jnp.where(srow == slane, jnp.concatenate([t] * GLA_HEADS, axis=0), jnp.zeros((GLA_VW, GLA_KW), t.dtype))


def _gla_masks():
    row = lax.broadcasted_iota(jnp.int32, (GLA_CHUNK, GLA_KW), 0)
    pos = lax.broadcasted_iota(jnp.int32, (GLA_CHUNK, GLA_KW), 1) & (GLA_CHUNK - 1)
    return pos <= row, pos >= row


def _gla_fwd(qb, kb, vb, la, gb, norm_w):
    s = qb.shape[0]
    tb = min(256, s)
    ch = tb // GLA_CHUNK

    def body(qb_ref, kb_ref, vb_ref, la_ref, gb_ref, nw_ref, o_ref, cat_ref, sp_ref, st_ref):
        @pl.when(pl.program_id(0) == 0)
        def _():
            st_ref[...] = jnp.zeros_like(st_ref)

        causal, _ = _gla_masks()
        nw = nw_ref[...]
        b = _chunk_cumsum(la_ref[...], True)
        bl = _chunk_last(b)
        k = kb_ref[...]
        qd = ((qb_ref[...] * GLA_SCALE) * jnp.exp(b)).astype(MXU_DTYPE)
        ki = (k * jnp.exp(-b)).astype(MXU_DTYPE)
        ke = (k * jnp.exp(bl - b)).astype(MXU_DTYPE)
        dec = jnp.exp(bl)
        v = vb_ref[...].astype(MXU_DTYPE)
        g = gb_ref[...]
        silu = g * _sigmoid(g)
        for ci in range(ch):
            rows = slice(GLA_CHUNK * ci, GLA_CHUNK * (ci + 1))
            qds, kis, kes = (_head_stack(t[rows], GLA_DK) for t in (qd, ki, ke))
            a = jnp.where(causal, _mm_nt(qd[rows], kis), 0.0).astype(MXU_DTYPE)
            st = st_ref[...]
            sp_ref[ci] = st
            o = _mm(a, _head_stack(v[rows], GLA_DV)) + _rows_to_heads(_mm_nt(qds, st.astype(MXU_DTYPE)))
            st_ref[...] = st * dec[rows][0:1] + _mm_tn(_heads_to_rows(v[rows]), kes)
            o_ref[rows, :] = o
            for h in range(GLA_HEADS):
                lv = slice(GLA_DV * h, GLA_DV * (h + 1))
                oh = o[:, lv]
                r = lax.rsqrt(jnp.mean(oh * oh, axis=1, keepdims=True) + EPS)
                cat_ref[rows, lv] = (oh * r * nw * silu[rows, lv]).astype(cat_ref.dtype)

    return pl.pallas_call(
        body, name="gla_fwd", grid=(s // tb,),
        in_specs=[_rows(tb, 256), _rows(tb, 256), _rows(tb, 512), _rows(tb, 256), _rows(tb, 512), _full((1, 128))],
        out_specs=[_rows(tb, 512), _rows(tb, 512), pl.BlockSpec((ch, GLA_DV, 256), lambda i: (i, 0, 0))],
        out_shape=[jax.ShapeDtypeStruct((s, 512), F32), jax.ShapeDtypeStruct((s, 512), MXU_DTYPE),
                   jax.ShapeDtypeStruct((s // GLA_CHUNK, GLA_DV, 256), F32)],
        scratch_shapes=[pltpu.VMEM((GLA_DV, GLA_KW), F32)],
        compiler_params=_cparams(dimension_semantics=("arbitrary",)),
    )(qb, kb, vb, la, gb, norm_w)


def _out_ln_loss(cat_a, cat_b, w_out, x, target, ln_g, ln_b):
    s = x.shape[0]
    ts = min(512, s)
    halves = 2 if ts % 32 == 0 else 1
    th = ts // halves

    def body(ca_ref, cb_ref, w_ref, x_ref, t_ref, g_ref, b_ref,
             loss_ref, gx_ref, da_ref, db_ref, gw_ref, gln_ref):
        @pl.when(pl.program_id(0) == 0)
        def _():
            loss_ref[...] = jnp.zeros_like(loss_ref)
            gw_ref[...] = jnp.zeros_like(gw_ref)
            gln_ref[...] = jnp.zeros_like(gln_ref)

        g = g_ref[...]
        dh16s = []
        for k in range(halves):
            rows = slice(th * k, th * (k + 1))
            mix = _mm(ca_ref[rows, :], w_ref[0:512, :]) + _mm(cb_ref[rows, :], w_ref[512:1024, :])
            h = ALPHA * x_ref[rows, :] + mix
            mu = jnp.mean(h, axis=1, keepdims=True)
            hc = h - mu
            rstd = lax.rsqrt(jnp.mean(hc * hc, axis=1, keepdims=True) + EPS)
            xhat = hc * rstd
            err = xhat * g + b_ref[...] - t_ref[rows, :]
            loss_ref[...] += 0.5 * jnp.sum(jnp.mean(err * err, axis=1, keepdims=True))
            dy = err * (1.0 / D_MODEL)
            gln_ref[0:1, :] += jnp.sum(dy * xhat, axis=0, keepdims=True)
            gln_ref[1:2, :] += jnp.sum(dy, axis=0, keepdims=True)
            dxh = dy * g
            dh = rstd * (dxh - jnp.mean(dxh, axis=1, keepdims=True)
                         - xhat * jnp.mean(dxh * xhat, axis=1, keepdims=True))
            gx_ref[rows, :] = ALPHA * dh
            dh16s.append(dh.astype(MXU_DTYPE))
        for k in range(halves):
            rows = slice(th * k, th * (k + 1))
            da_ref[rows, :] = _mm_nt(dh16s[k], w_ref[0:512, :])
            db_ref[rows, :] = _mm_nt(dh16s[k], w_ref[512:1024, :])
        dh16 = jnp.concatenate(dh16s, axis=0)
        gw_ref[0:512, :] += _mm_tn(ca_ref[...], dh16)
        gw_ref[512:1024, :] += _mm_tn(cb_ref[...], dh16)

    return pl.pallas_call(
        body, name="out_ln_loss", grid=(s // ts,),
        in_specs=[_rows(ts, 512), _rows(ts, 512), _full((D_MODEL, D_MODEL)), _rows(ts, D_MODEL), _rows(ts, D_MODEL),
                  _full((1, D_MODEL)), _full((1, D_MODEL))],
        out_specs=[_full((1, 128)), _rows(ts, D_MODEL), _rows(ts, 512), _rows(ts, 512),
                   _full((D_MODEL, D_MODEL)), _full((2, D_MODEL))],
        out_shape=[jax.ShapeDtypeStruct((1, 128), F32), jax.ShapeDtypeStruct((s, D_MODEL), F32),
                   jax.ShapeDtypeStruct((s, 512), F32), jax.ShapeDtypeStruct((s, 512), F32),
                   jax.ShapeDtypeStruct((D_MODEL, D_MODEL), F32), jax.ShapeDtypeStruct((2, D_MODEL), F32)],
        compiler_params=_cparams(dimension_semantics=("arbitrary",)),
    )(cat_a, cat_b, w_out, x, target, ln_g, ln_b)


def _swa_bwd(sinks, qa, k_pad, v_pad, attn, ga, d_cat_a, rope, parts_w_out):
    s = qa.shape[0]
    tq = SWA_SUB * BLOCK
    nsteps = s // tq
    forward_step = min(2, nsteps - 1)

    def body(sink_ref, qa_ref, ga_ref, at_ref, dc_ref, c_ref, s1_ref, s2_ref, bias_ref, k_ref, v_ref, pout_ref,
             dq_ref, dg_ref, dk_ref, dv_ref, ds_ref, gout_ref, *scratch):
        n = pl.program_id(0)
        owner_sum = _OwnerSum(pout_ref, *scratch)

        @pl.when(n == 0)
        def _():
            dk_ref[...] = jnp.zeros_like(dk_ref)
            dv_ref[...] = jnp.zeros_like(dv_ref)
            ds_ref[...] = jnp.zeros_like(ds_ref)
            owner_sum.start()

        @pl.when(n == forward_step)
        def _():
            owner_sum.forward()

        @pl.when(n == nsteps - 1)
        def _():
            gout_ref[...] = owner_sum.finish()

        low = lax.broadcasted_iota(jnp.int32, (2 * BLOCK, 128), 1) < SWA_HEAD_DIM
        for b in range(SWA_SUB):
            rows = slice(BLOCK * b, BLOCK * (b + 1))
            start = pl.multiple_of((n * SWA_SUB + b) * BLOCK, BLOCK)
            kw = k_ref[pl.ds(start, 2 * BLOCK), :]
            vw = v_ref[pl.ds(start, 2 * BLOCK), :]
            bias_t = _swa_bias_of(bias_ref, n, b)
            q = qa_ref[rows, :] * SWA_SCALE
            g = ga_ref[rows, :]
            sg = _sigmoid(g)
            o = at_ref[rows, :]
            dc = dc_ref[rows, :]
            do = dc * (g * sg)
            dg_ref[rows, :] = (dc * o * (sg * (1.0 + g * (1.0 - sg)))).astype(dg_ref.dtype)
            od = do * o
            c, s1, s2 = c_ref[rows, :], s1_ref[rows, :], s2_ref[rows, :]
            dk, dv = [], []
            for j in range(SWA_KV_HEADS):
                kd, vd = _swa_dup(kw, j), _swa_dup(vw, j)
                qs = _swa_stack(q, j).astype(MXU_DTYPE)
                dos = _swa_stack(do, j).astype(MXU_DTYPE)
                probs, psink = _swa_probs_t(qs, kd, bias_t, _swa_sink_row(sink_ref, j))
                delta = _row_sums_as_row(_swa_stack(od, j))
                dsc = (probs * (_mm_nt(vd, dos) - delta)).astype(MXU_DTYPE)
                dsink = psink * delta
                for r in range(SWA_GROUP):
                    h = SWA_GROUP * j + r
                    ds_ref[h:h + 1, :] += jnp.zeros((1, 128), F32) - jnp.sum(dsink[:, BLOCK * r:BLOCK * (r + 1)])
                dq = _swa_unstack(_mm_tn(dsc, kd))
                for i in range(2):
                    lanes = slice(128 * (2 * j + i), 128 * (2 * j + i + 1))
                    dq_ref[rows, lanes] = _rope_t(dq[i] * SWA_SCALE, c, s1, s2).astype(dq_ref.dtype)
                dkj = _mm(dsc, qs)
                dvj = _mm(probs.astype(MXU_DTYPE), dos)
                dk.append(dkj + pltpu.roll(dkj, SWA_HEAD_DIM, 1))
                dv.append(dvj + pltpu.roll(dvj, SWA_HEAD_DIM, 1))
            dk_ref[pl.ds(start, 2 * BLOCK), :] += jnp.where(low, dk[0], dk[1])
            dv_ref[pl.ds(start, 2 * BLOCK), :] += jnp.where(low, dv[0], dv[1])

    out_blk = parts_w_out.shape[1:]
    return pl.pallas_call(
        body, name="swa_bwd", grid=(nsteps,),
        in_specs=[pl.BlockSpec(memory_space=pltpu.SMEM)] + [_rows(tq, 512)] * 4 + [_rows(tq, 128)] * 3
        + [_full((2, 2 * BLOCK, SWA_ROWS))] + [_full((s + BLOCK, 128))] * 2 + [pl.BlockSpec(memory_space=pl.ANY)],
        out_specs=[_rows(tq, 512), _rows(tq, 512), _full((s + BLOCK, 128)), _full((s + BLOCK, 128)),
                   _full((SWA_Q_HEADS, 128)), _full(out_blk)],
        out_shape=[jax.ShapeDtypeStruct((s, 512), MXU_DTYPE), jax.ShapeDtypeStruct((s, 512), MXU_DTYPE),
                   jax.ShapeDtypeStruct((s + BLOCK, 128), F32), jax.ShapeDtypeStruct((s + BLOCK, 128), F32),
                   jax.ShapeDtypeStruct((SWA_Q_HEADS, 128), F32), jax.ShapeDtypeStruct(out_blk, F32)],
        scratch_shapes=_OwnerSum.scratch(out_blk),
        compiler_params=_cparams(dimension_semantics=("arbitrary",)),
    )(sinks, qa, ga, attn, d_cat_a, *rope, _swa_bias(), k_pad, v_pad, parts_w_out)


def _gla_bwd(qb, kb, vb, la, oms, gb, o, sprev, d_cat_b, rb, wg, norm_w):
    s = qb.shape[0]
    tb = min(512, s)
    ch = tb // GLA_CHUNK
    nb = s // tb

    def body(qb_ref, kb_ref, vb_ref, la_ref, oms_ref, gb_ref, o_ref, sp_ref, dc_ref, rb_ref, wg_ref, nw_ref,
             dq_ref, dk_ref, dv_ref, dg_ref, dr_ref, gwg_ref, gbg_ref, gnw_ref, dst_ref):
        @pl.when(pl.program_id(0) == 0)
        def _():
            dst_ref[...] = jnp.zeros_like(dst_ref)
            gwg_ref[...] = jnp.zeros_like(gwg_ref)
            gbg_ref[...] = jnp.zeros_like(gbg_ref)
            gnw_ref[...] = jnp.zeros_like(gnw_ref)

        causal, causal_t = _gla_masks()
        nw = nw_ref[...]
        b = _chunk_cumsum(la_ref[...], True)
        bl = _chunk_last(b)
        eb, enb, ee, dec = jnp.exp(b), jnp.exp(-b), jnp.exp(bl - b), jnp.exp(bl)
        k = kb_ref[...]
        qd = (qb_ref[...] * GLA_SCALE) * eb
        ki = k * enb
        ke = k * ee
        qd16, ki16, ke16 = qd.astype(MXU_DTYPE), ki.astype(MXU_DTYPE), ke.astype(MXU_DTYPE)
        v16 = vb_ref[...].astype(MXU_DTYPE)

        g = gb_ref[...]
        sg = _sigmoid(g)
        silu = g * sg
        dsilu = sg * (1.0 + g * (1.0 - sg))
        gnw = jnp.zeros((1, GLA_DV), F32)
        do = []
        for h in range(GLA_HEADS):
            lv = slice(GLA_DV * h, GLA_DV * (h + 1))
            oh = o_ref[:, lv]
            dch = dc_ref[:, lv]
            r = lax.rsqrt(jnp.mean(oh * oh, axis=1, keepdims=True) + EPS)
            d_on = dch * silu[:, lv]
            dg_ref[:, lv] = (dch * (oh * r * nw) * dsilu[:, lv]).astype(dg_ref.dtype)
            gnw += jnp.sum(d_on * oh * r, axis=0, keepdims=True)
            u = d_on * nw
            do.append(r * u - oh * (r * r * r) * jnp.mean(u * oh, axis=1, keepdims=True))
        gnw_ref[...] += gnw
        do16 = jnp.concatenate(do, axis=1).astype(MXU_DTYPE)

        db, dbl = [None] * ch, [None] * ch
        for ci in reversed(range(ch)):
            rows = slice(GLA_CHUNK * ci, GLA_CHUNK * (ci + 1))
            qds, kis, kes = (_head_stack(t[rows], GLA_DK) for t in (qd16, ki16, ke16))
            vs, dos = _head_stack(v16[rows], GLA_DV), _head_stack(do16[rows], GLA_DV)
            a = jnp.where(causal, _mm_nt(qd16[rows], kis), 0.0).astype(MXU_DTYPE)
            at = jnp.where(causal_t, _mm_nt(ki16[rows], qds), 0.0).astype(MXU_DTYPE)
            da = jnp.where(causal, _mm_nt(do16[rows], vs), 0.0).astype(MXU_DTYPE)
            dat = jnp.where(causal_t, _mm_nt(v16[rows], dos), 0.0).astype(MXU_DTYPE)
            st = sp_ref[ci]
            dst = dst_ref[...]
            dst16 = dst.astype(MXU_DTYPE)
            dv = _mm(at, dos) + _rows_to_heads(_mm_nt(kes, dst16))
            dqd = _mm(da, kis) + _mm(do16[rows], _state_by_head(st.astype(MXU_DTYPE)))
            dki = _mm(dat, qds)
            dke = _mm(v16[rows], _state_by_head(dst16))
            ddec = jnp.sum(dst * st, axis=0, keepdims=True)
            decc = dec[rows][0:1]
            dst_ref[...] = _mm_tn(_heads_to_rows(do16[rows]), qds) + dst * decc
            dq_ref[rows, :] = (dqd * eb[rows] * GLA_SCALE).astype(dq_ref.dtype)
            dk_ref[rows, :] = (dki * enb[rows] + dke * ee[rows]).astype(dk_ref.dtype)
            dv_ref[rows, :] = dv.astype(dv_ref.dtype)
            dke_ke = dke * ke[rows]
            db[ci] = dqd * qd[rows] - dki * ki[rows] - dke_ke
            dbl[ci] = jnp.broadcast_to(jnp.sum(dke_ke, axis=0, keepdims=True) + ddec * decc, (GLA_CHUNK, GLA_KW))

        dla = _chunk_cumsum(jnp.concatenate(db, axis=0), False) + jnp.concatenate(dbl, axis=0)
        dlogit = dla * oms_ref[...] * (1.0 / GLA_TAU)
        dl16 = dlogit.astype(MXU_DTYPE)
        gbg_ref[...] += jnp.sum(dlogit, axis=0, keepdims=True)
        gwg_ref[...] += _mm_tn(rb_ref[...].astype(MXU_DTYPE), dl16)
        dr_ref[...] = _mm_nt(dl16, wg_ref[...]).astype(dr_ref.dtype)

    def rev(width):
        return pl.BlockSpec((tb, width), lambda i: (nb - 1 - i, 0))

    return pl.pallas_call(
        body, name="gla_bwd", grid=(nb,),
        in_specs=[rev(256), rev(256), rev(512), rev(256), rev(256), rev(512), rev(512),
                  pl.BlockSpec((ch, GLA_DV, 256), lambda i: (nb - 1 - i, 0, 0)), rev(512), rev(GLA_RANK),
                  _full((GLA_RANK, 256)), _full((1, 128))],
        out_specs=[rev(256), rev(256), rev(512), rev(512), rev(GLA_RANK),
                   _full((GLA_RANK, 256)), _full((1, 256)), _full((1, 128))],
        out_shape=[jax.ShapeDtypeStruct((s, 256), MXU_DTYPE), jax.ShapeDtypeStruct((s, 256), MXU_DTYPE),
                   jax.ShapeDtypeStruct((s, 512), MXU_DTYPE), jax.ShapeDtypeStruct((s, 512), MXU_DTYPE),
                   jax.ShapeDtypeStruct((s, GLA_RANK), MXU_DTYPE), jax.ShapeDtypeStruct((GLA_RANK, 256), F32),
                   jax.ShapeDtypeStruct((1, 256), F32), jax.ShapeDtypeStruct((1, 128), F32)],
        scratch_shapes=[pltpu.VMEM((GLA_DV, GLA_KW), F32)],
        compiler_params=_cparams(dimension_semantics=("arbitrary",)),
    )(qb, kb, vb, la, oms, gb, o, sprev, d_cat_b, rb, wg, norm_w)


def _in_proj_bwd_x(gx0, pieces, w_in, rope):
    s = gx0.shape[0]
    ts = min(512, s)
    widths = [OFF[i + 1] - OFF[i] for i in range(9)]

    def body(gx0_ref, *refs):
        piece_refs = refs[:9]
        w_ref, c_ref, s1_ref, s2_ref, gx_ref, dp_ref = refs[9:]
        kv_rows = pl.ds(pl.multiple_of(BLOCK + pl.program_id(0) * ts, BLOCK), ts)
        acc = gx0_ref[...]
        for i in (0, 1, 3, 4, 5, 6, 7, 8):
            lo, hi = OFF[i], OFF[i + 1]
            if i == 1:
                dk = _rope_t(piece_refs[1][kv_rows, :], c_ref[...], s1_ref[...], s2_ref[...])
                t16 = jnp.concatenate([dk, piece_refs[2][kv_rows, :]], axis=1).astype(MXU_DTYPE)
                hi = OFF[3]
            else:
                t16 = piece_refs[i][...].astype(MXU_DTYPE)
            dp_ref[:, lo:hi] = t16
            acc += _mm(t16, w_ref[lo:hi, :])
        gx_ref[...] = acc

    return pl.pallas_call(
        body, name="in_proj_bwd_x", grid=(s // ts,),
        in_specs=[_rows(ts, D_MODEL)]
        + [_full((s + BLOCK, w)) if i in (1, 2) else _rows(ts, w) for i, w in enumerate(widths)]
        + [_full((D_IN_PROJ, D_MODEL))] + [_rows(ts, 128)] * 3,
        out_specs=[_rows(ts, D_MODEL), _rows(ts, D_IN_PROJ)],
        out_shape=[jax.ShapeDtypeStruct((s, D_MODEL), F32), jax.ShapeDtypeStruct((s, D_IN_PROJ), MXU_DTYPE)],
        compiler_params=_cparams(dimension_semantics=("arbitrary",)),
    )(gx0, *pieces, w_in, *rope)


def _in_proj_bwd_w(x, dproj):
    s = x.shape[0]
    ts = min(1024, s)
    nsteps = s // ts
    col_chunks = [(OFF[i], OFF[i + 2] if i == 1 else OFF[i + 1]) for i in (0, 1, 3, 4, 5, 6, 7, 8)]

    def body(x_ref, dp_ref, gw_ref, acc_ref, stage_ref, sems):
        i = pl.program_id(0)

        @pl.when(i == 0)
        def _():
            acc_ref[...] = jnp.zeros_like(acc_ref)

        xb = x_ref[...].astype(MXU_DTYPE)
        for lo, hi in col_chunks:
            acc_ref[lo:hi, :] += _mm_tn(dp_ref[:, lo:hi], xb)

        @pl.when(i == nsteps - 1)
        def _():
            copies = []
            for j in range(N_DEV):
                slot = j % 2
                if j >= 2:
                    copies[j - 2].wait()
                stage_ref[slot] = acc_ref[D_IN_SHARD * j:D_IN_SHARD * (j + 1), :]
                cp = pltpu.make_async_copy(stage_ref.at[slot], gw_ref.at[j], sems.at[slot])
                cp.start()
                copies.append(cp)
            copies[N_DEV - 2].wait()
            copies[N_DEV - 1].wait()

    return pl.pallas_call(
        body, name="in_proj_bwd_w", grid=(nsteps,),
        in_specs=[_rows(ts, D_MODEL), _rows(ts, D_IN_PROJ)],
        out_specs=pl.BlockSpec(memory_space=pl.ANY),
        out_shape=jax.ShapeDtypeStruct((N_DEV, D_IN_SHARD, D_MODEL), F32),
        scratch_shapes=[pltpu.VMEM((D_IN_PROJ, D_MODEL), F32), pltpu.VMEM((2, D_IN_SHARD, D_MODEL), F32),
                        pltpu.SemaphoreType.DMA((2,))],
        compiler_params=_cparams(dimension_semantics=("arbitrary",)),
    )(x, dproj)


def _local_step(x, positions, w_in, wg, b_gate, sinks, norm_w, w_out_s, ln_g, ln_b, target):
    qa, k_pad, v_pad, ga, qb, kb, vb, gb, rb, la, oms, *rope, w_out = _in_proj(
        x, w_in, wg, b_gate, _rope_angles(positions), w_out_s)
    attn, cat_a = _swa_fwd(sinks, qa, k_pad, v_pad, ga)
    o, cat_b, sprev = _gla_fwd(qb, kb, vb, la, gb, norm_w)
    loss, gx0, d_cat_a, d_cat_b, g_w_out, g_ln = _out_ln_loss(cat_a, cat_b, w_out, x, target, ln_g, ln_b)
    parts_w_out = g_w_out.reshape(N_DEV, D_OUT_SHARD, D_MODEL)
    dqa, dga, dk_pad, dv_pad, g_sinks, g_out = _swa_bwd(sinks, qa, k_pad, v_pad, attn, ga, d_cat_a, rope, parts_w_out)
    dqb, dkb, dvb, dgb, drb, g_wg, g_bg, g_nw = _gla_bwd(qb, kb, vb, la, oms, gb, o, sprev, d_cat_b, rb, wg, norm_w)
    pieces = (dqa, dk_pad, dv_pad, dga, dqb, dkb, dvb, dgb, drb)
    grad_x, dproj = _in_proj_bwd_x(gx0, pieces, w_in, rope)
    g_w_in = _in_proj_bwd_w(x, dproj)
    return loss, grad_x, g_w_in, g_wg, g_bg, g_sinks, g_nw, g_out, g_ln


def _mesh_pos():
    return lax.axis_index("x"), lax.axis_index("y"), lax.axis_index("c")


def _peer(k, x, y, c):
    px = (1 - x) if k & 4 else x
    py = (1 - y) if k & 2 else y
    pc = (1 - c) if k & 1 else c
    return (px, py, pc), 4 * px + 2 * py + pc


def _other_chips(x, y):
    return [(1 - x, y), (x, 1 - y), (1 - x, 1 - y)]


def _shard_view(t):
    return jnp.transpose(t, (2, 0, 1))


class _BlockGather:
    def __init__(self, slots, send_sems, recv_sems):
        self.slots, self.send_sems, self.recv_sems = slots, send_sems, recv_sems
        x, y, c = _mesh_pos()
        self.xy, self.c, self.me, self.sibling = (x, y), c, 4 * x + 2 * y + c, (x, y, 1 - c)
        self.chips = _other_chips(x, y)

    @staticmethod
    def scratch():
        return [pltpu.SemaphoreType.DMA((N_DEV - 1,)), pltpu.SemaphoreType.DMA((N_DEV - 1,))]

    def _copy(self, k, block, to):
        return pltpu.make_async_remote_copy(
            src_ref=self.slots.at[block], dst_ref=self.slots.at[block], send_sem=self.send_sems.at[k],
            recv_sem=self.recv_sems.at[k], device_id=to, device_id_type=pl.DeviceIdType.MESH)

    def start(self):
        for j, (cx, cy) in enumerate(self.chips):
            self._copy(1 + j, self.me, (cx, cy, self.c)).start()
        self._copy(0, self.me, self.sibling).start()

    def forward(self):
        for j, (cx, cy) in enumerate(self.chips):
            block = 4 * cx + 2 * cy + self.c
            self._copy(1 + j, block, self.sibling).wait_recv()
            self._copy(4 + j, block, self.sibling).start()

    def finish(self):
        x, y = self.xy
        self._copy(0, 4 * x + 2 * y + (1 - self.c), self.sibling).wait_recv()
        for j, (cx, cy) in enumerate(self.chips):
            self._copy(4 + j, 4 * cx + 2 * cy + (1 - self.c), self.sibling).wait_recv()
        for k in range(N_DEV - 1):
            self._copy(k, self.me, self.sibling).wait_send()


def _all_gather_weights(w_in_t, wg_s):
    def body(win_hbm, wg_ref, win_full, wg_full, win_all, wg_all, stage, stage_sem, *sems):
        gathers = (_BlockGather(win_all, *sems[0:2]), _BlockGather(wg_all, *sems[2:4]))
        me = gathers[0].me
        load = pltpu.make_async_copy(win_hbm.at[:, 0, :], stage, stage_sem)
        load.start()
        load.wait()
        win_all[me] = stage[...].astype(win_all.dtype)
        wg_all[me] = wg_ref[...].astype(wg_all.dtype)
        for stage_of in ("start", "forward", "finish"):
            for gather in gathers:
                getattr(gather, stage_of)()
        for j in range(N_DEV):
            win_full[D_IN_SHARD * j:D_IN_SHARD * (j + 1), :] = win_all[j]
            wg_full[:, 32 * j:32 * (j + 1)] = wg_all[j]

    vmem = pl.BlockSpec(memory_space=pltpu.VMEM)
    return pl.pallas_call(
        body, name="all_gather_weights",
        in_specs=[pl.BlockSpec(memory_space=pl.ANY), vmem], out_specs=[vmem] * 2,
        out_shape=[jax.ShapeDtypeStruct((D_IN_PROJ, D_MODEL), MXU_DTYPE),
                   jax.ShapeDtypeStruct((GLA_RANK, 256), MXU_DTYPE)],
        scratch_shapes=[pltpu.VMEM((N_DEV, D_IN_SHARD, D_MODEL), MXU_DTYPE),
                        pltpu.VMEM((N_DEV, GLA_RANK, 32), MXU_DTYPE),
                        pltpu.VMEM((D_IN_SHARD, D_MODEL), F32), pltpu.SemaphoreType.DMA]
        + _BlockGather.scratch() + _BlockGather.scratch(),
        compiler_params=_cparams(),
    )(w_in_t, wg_s)


class _OwnerSum:
    def __init__(self, parts, own, sib, snd, rcv, loc_sems, d2d_send, d2d_recv, ici_send, ici_recv):
        self.parts, self.own, self.sib, self.snd, self.rcv = parts, own, sib, snd, rcv
        self.sems = (loc_sems, d2d_send, d2d_recv, ici_send, ici_recv)
        x, y, c = _mesh_pos()
        self.c, self.sibling = c, (x, y, 1 - c)
        self.chips = [(x, y)] + _other_chips(x, y)

    @staticmethod
    def scratch(block):
        return [pltpu.VMEM((4,) + block, F32), pltpu.VMEM((4,) + block, F32),
                pltpu.VMEM((3,) + block, MXU_DTYPE), pltpu.VMEM((3,) + block, MXU_DTYPE),
                pltpu.SemaphoreType.DMA((4,)), pltpu.SemaphoreType.DMA((4,)), pltpu.SemaphoreType.DMA((4,)),
                pltpu.SemaphoreType.DMA((3,)), pltpu.SemaphoreType.DMA((3,))]

    def _local(self, r):
        cx, cy = self.chips[r]
        return pltpu.make_async_copy(self.parts.at[4 * cx + 2 * cy + self.c], self.own.at[r], self.sems[0].at[r])

    def _d2d(self, r):
        cx, cy = self.chips[r]
        return pltpu.make_async_remote_copy(
            src_ref=self.parts.at[4 * cx + 2 * cy + (1 - self.c)], dst_ref=self.sib.at[r], send_sem=self.sems[1].at[r],
            recv_sem=self.sems[2].at[r], device_id=self.sibling, device_id_type=pl.DeviceIdType.MESH)

    def _ici(self, r):
        cx, cy = self.chips[r]
        return pltpu.make_async_remote_copy(
            src_ref=self.snd.at[r - 1], dst_ref=self.rcv.at[r - 1], send_sem=self.sems[3].at[r - 1],
            recv_sem=self.sems[4].at[r - 1], device_id=(cx, cy, self.c), device_id_type=pl.DeviceIdType.MESH)

    def start(self):
        for r in (1, 2, 3, 0):
            self._local(r).start()
            self._d2d(r).start()

    def forward(self):
        for r in (1, 2, 3):
            self._local(r).wait()
            self._d2d(r).wait_recv()
            self.snd[r - 1] = (self.own[r] + self.sib[r]).astype(self.snd.dtype)
            self._ici(r).start()

    def finish(self):
        self._local(0).wait()
        self._d2d(0).wait_recv()
        acc = self.own[0] + self.sib[0]
        for r in (1, 2, 3):
            self._ici(r).wait_recv()
            acc = acc + self.rcv[r - 1].astype(F32)
        for r in range(4):
            self._d2d(r).wait_send()
        for r in (1, 2, 3):
            self._ici(r).wait_send()
        return acc


SMALL_ROWS = 48


def _reduce_grads(parts_w_in, parts_wg, g_ln, g_bg, g_nw, g_sinks, loss):
    def body(pin_ref, pwg_ref, gln_ref, gbg_ref, gnw_ref, gsk_ref, loss_ref, gin_ref, rwg_ref, rsm_ref, sm_ref,
             *scratch):
        sm_send, sm_recv, sm_loc = scratch[-3:]
        x, y, c = _mesh_pos()
        me = 4 * x + 2 * y + c
        owner_sum = _OwnerSum(pin_ref, *scratch[:-3])

        sm_ref[...] = jnp.zeros_like(sm_ref)
        for r in range(D_MODEL // 128):
            sm_ref[r:r + 1, :] = gln_ref[0:1, 128 * r:128 * (r + 1)]
            sm_ref[8 + r:9 + r, :] = gln_ref[1:2, 128 * r:128 * (r + 1)]
        for r in range(2):
            sm_ref[16 + r:17 + r, :] = gbg_ref[0:1, 128 * r:128 * (r + 1)]
        sm_ref[24:25, :] = gnw_ref[...]
        diag = lax.broadcasted_iota(jnp.int32, gsk_ref.shape, 0) == lax.broadcasted_iota(jnp.int32, gsk_ref.shape, 1)
        sm_ref[32:33, :] = jnp.sum(jnp.where(diag, gsk_ref[...], 0.0), axis=0, keepdims=True)
        sm_ref[40:41, :] = loss_ref[...]

        small_dsts = (rwg_ref, rsm_ref)

        def small_src(a, block):
            return pwg_ref.at[block] if a == 0 else sm_ref

        small_local = [pltpu.make_async_copy(small_src(a, me), small_dsts[a].at[me], sm_loc.at[a]) for a in range(2)]
        for cp in small_local:
            cp.start()
        small_sends = []
        for k in range(1, N_DEV):
            peer, pidx = _peer(k, x, y, c)
            for a in range(2):
                i = 2 * (k - 1) + a
                cp = pltpu.make_async_remote_copy(
                    src_ref=small_src(a, pidx), dst_ref=small_dsts[a].at[me], send_sem=sm_send.at[i],
                    recv_sem=sm_recv.at[i], device_id=peer, device_id_type=pl.DeviceIdType.MESH)
                cp.start()
                small_sends.append(cp)

        owner_sum.start()
        owner_sum.forward()
        gin_ref[...] = owner_sum.finish()

        for k in range(1, N_DEV):
            peer, pidx = _peer(k, x, y, c)
            for a in range(2):
                i = 2 * (k - 1) + a
                pltpu.make_async_remote_copy(
                    src_ref=small_src(a, me), dst_ref=small_dsts[a].at[pidx], send_sem=sm_send.at[i],
                    recv_sem=sm_recv.at[i], device_id=peer, device_id_type=pl.DeviceIdType.MESH).wait_recv()
        for cp in small_sends:
            cp.wait_send()
        for cp in small_local:
            cp.wait()

    hbm = pl.BlockSpec(memory_space=pl.ANY)
    vmem = pl.BlockSpec(memory_space=pltpu.VMEM)
    in_blk = parts_w_in.shape[1:]
    return pl.pallas_call(
        body, name="reduce_grads",
        in_specs=[hbm, hbm] + [vmem] * 5, out_specs=[vmem, hbm, hbm],
        out_shape=[jax.ShapeDtypeStruct(in_blk, F32),
                   jax.ShapeDtypeStruct((N_DEV,) + parts_wg.shape[1:], F32),
                   jax.ShapeDtypeStruct((N_DEV, SMALL_ROWS, 128), F32)],
        scratch_shapes=[pltpu.VMEM((SMALL_ROWS, 128), F32)] + _OwnerSum.scratch(in_blk)
        + [pltpu.SemaphoreType.DMA((2 * (N_DEV - 1),)), pltpu.SemaphoreType.DMA((2 * (N_DEV - 1),)),
           pltpu.SemaphoreType.DMA((2,))],
        compiler_params=_cparams(),
    )(parts_w_in, parts_wg, g_ln, g_bg, g_nw, g_sinks, loss)


def _adamw(recv, w, m, v, name):
    rows, width = w.shape
    tr = 128 if rows % 128 == 0 else rows
    n_parts = recv.shape[0]

    def body(r_ref, w_ref, m_ref, v_ref, g_ref, d_ref, nm_ref, nv_ref):
        g = r_ref[0]
        for j in range(1, n_parts):
            g = g + r_ref[j]
        nm = ADAM_B1 * m_ref[...] + (1.0 - ADAM_B1) * g
        nv = ADAM_B2 * v_ref[...] + (1.0 - ADAM_B2) * (g * g)
        m_hat = nm / (1.0 - ADAM_B1 ** ADAM_STEP)
        v_hat = nv / (1.0 - ADAM_B2 ** ADAM_STEP)
        g_ref[...] = g
        d_ref[...] = -ADAM_LR * (m_hat / (jnp.sqrt(v_hat) + ADAM_EPS) + ADAM_WD * w_ref[...])
        nm_ref[...] = nm
        nv_ref[...] = nv

    spec = _rows(tr, width)
    return pl.pallas_call(
        body, name=name, grid=(rows // tr,),
        in_specs=[pl.BlockSpec((n_parts, tr, width), lambda i: (0, i, 0)), spec, spec, spec],
        out_specs=[spec] * 4,
        out_shape=[jax.ShapeDtypeStruct((rows, width), F32)] * 4,
        compiler_params=_cparams(dimension_semantics=("arbitrary",)),
    )(recv, w, m, v)


def _adamw_shard_view(g, w, m, v):
    rows, width = g.shape

    def body(g_ref, w_hbm, m_hbm, v_hbm, g_out, d_out, nm_out, nv_out, bufs, outs, sems):
        loads = [pltpu.make_async_copy(src.at[:, 0, :], bufs.at[i], sems.at[i])
                 for i, src in enumerate((w_hbm, m_hbm, v_hbm))]
        for cp in loads:
            cp.start()
        g = g_ref[...]
        for cp in loads:
            cp.wait()
        nm = ADAM_B1 * bufs[1] + (1.0 - ADAM_B1) * g
        nv = ADAM_B2 * bufs[2] + (1.0 - ADAM_B2) * (g * g)
        m_hat = nm / (1.0 - ADAM_B1 ** ADAM_STEP)
        v_hat = nv / (1.0 - ADAM_B2 ** ADAM_STEP)
        outs[0] = g
        outs[1] = -ADAM_LR * (m_hat / (jnp.sqrt(v_hat) + ADAM_EPS) + ADAM_WD * bufs[0])
        outs[2] = nm
        outs[3] = nv
        stores = [pltpu.make_async_copy(outs.at[i], dst.at[:, 0, :], sems.at[3 + i])
                  for i, dst in enumerate((g_out, d_out, nm_out, nv_out))]
        for cp in stores:
            cp.start()
        for cp in stores:
            cp.wait()

    hbm = pl.BlockSpec(memory_space=pl.ANY)
    return pl.pallas_call(
        body, name="adamw_w_in",
        in_specs=[pl.BlockSpec(memory_space=pltpu.VMEM), hbm, hbm, hbm], out_specs=[hbm] * 4,
        out_shape=[jax.ShapeDtypeStruct((rows, 1, width), F32)] * 4,
        scratch_shapes=[pltpu.VMEM((3, rows, width), F32), pltpu.VMEM((4, rows, width), F32),
                        pltpu.SemaphoreType.DMA((7,))],
        compiler_params=_cparams(),
    )(g, w, m, v)


def _adamw_vectors(r_small, r_wg, params):
    n_par = len(params)

    def body(rsm_ref, rwg_ref, *refs):
        ins, outs = refs[:3 * n_par], refs[3 * n_par:]
        g = rsm_ref[0]
        gwg = rwg_ref[0]
        for j in range(1, N_DEV):
            g = g + rsm_ref[j]
            gwg = gwg + rwg_ref[j]
        outs[4 * n_par][...] = g[40:41]
        grads = [gwg,
                 jnp.concatenate([g[r:r + 1] for r in range(0, 8)], axis=1),
                 jnp.concatenate([g[r:r + 1] for r in range(8, 16)], axis=1),
                 jnp.concatenate([g[16:17], g[17:18]], axis=1),
                 g[24:25],
                 g[32:33, 0:SWA_Q_HEADS]]
        for p, gp in enumerate(grads):
            w_ref, m_ref, v_ref = ins[3 * p:3 * p + 3]
            nm = ADAM_B1 * m_ref[...] + (1.0 - ADAM_B1) * gp
            nv = ADAM_B2 * v_ref[...] + (1.0 - ADAM_B2) * (gp * gp)
            m_hat = nm / (1.0 - ADAM_B1 ** ADAM_STEP)
            v_hat = nv / (1.0 - ADAM_B2 ** ADAM_STEP)
            outs[4 * p][...] = gp
            outs[4 * p + 1][...] = -ADAM_LR * (m_hat / (jnp.sqrt(v_hat) + ADAM_EPS) + ADAM_WD * w_ref[...])
            outs[4 * p + 2][...] = nm
            outs[4 * p + 3][...] = nv

    vmem = pl.BlockSpec(memory_space=pltpu.VMEM)
    flat = [t for wmv in params for t in wmv]
    return pl.pallas_call(
        body, name="adamw_vectors",
        in_specs=[vmem] * (2 + len(flat)), out_specs=[vmem] * (4 * n_par + 1),
        out_shape=[jax.ShapeDtypeStruct(wmv[0].shape, F32) for wmv in params for _ in range(4)]
        + [jax.ShapeDtypeStruct((1, 128), F32)],
        compiler_params=_cparams(),
    )(r_small, r_wg, *flat)


def kernel(x, positions, w_in, gla_w_gate_up, gla_b_gate, attn_sinks, gla_norm_w, w_out, ln_g, ln_b, loss_target, m_w_in, m_gla_w_gate_up, m_gla_b_gate, m_attn_sinks, m_gla_norm_w, m_w_out, m_ln_g, m_ln_b, v_w_in, v_gla_w_gate_up, v_gla_b_gate, v_attn_sinks, v_gla_norm_w, v_w_out, v_ln_g, v_ln_b):
    w_in_full, wg_full = _all_gather_weights(_shard_view(w_in), gla_w_gate_up[0])

    loss, grad_x, parts_w_in, g_wg, g_bg, g_sinks, g_nw, g_out, g_ln = _local_step(
        x[0], positions[0], w_in_full, wg_full, gla_b_gate, attn_sinks[0], gla_norm_w, w_out[0], ln_g, ln_b,
        loss_target[0])

    parts_wg = jnp.transpose(g_wg.reshape(GLA_RANK, N_DEV, 32), (1, 0, 2))
    g_in, r_wg, r_small = _reduce_grads(parts_w_in, parts_wg, g_ln, g_bg, g_nw, g_sinks, loss)

    upd_in = _adamw_shard_view(g_in, _shard_view(w_in), _shard_view(m_w_in), _shard_view(v_w_in))
    upd_in = [jnp.transpose(t, (1, 2, 0)) for t in upd_in]
    upd_out = _adamw(g_out[None], w_out[0], m_w_out[0], v_w_out[0], "adamw_w_out")
    vec = _adamw_vectors(r_small, r_wg, [
        (gla_w_gate_up[0], m_gla_w_gate_up[0], v_gla_w_gate_up[0]), (ln_g, m_ln_g, v_ln_g), (ln_b, m_ln_b, v_ln_b),
        (gla_b_gate, m_gla_b_gate, v_gla_b_gate), (gla_norm_w, m_gla_norm_w, v_gla_norm_w),
        (attn_sinks, m_attn_sinks, v_attn_sinks)])

    outs = [vec[24][0, 0], grad_x[None]]
    for kind in range(4):
        u_wg, u_ln_g, u_ln_b, u_bg, u_nw, u_sinks = (vec[4 * p + kind] for p in range(6))
        outs += [upd_in[kind], u_wg[None], u_bg, u_sinks, u_nw, upd_out[kind][None], u_ln_g, u_ln_b]
    return tuple(outs)
```

```python
import jax
import jax.numpy as jnp
from jax import lax
from jax.experimental import pallas as pl
from jax.experimental.pallas import tpu as pltpu

F32 = jnp.float32
MXU_DTYPE = jnp.bfloat16

N_DEV = 8
D_MODEL = 1024
SWA_Q_HEADS = 8
SWA_KV_HEADS = 2
SWA_GROUP = 4
SWA_HEAD_DIM = 64
BLOCK = 128
ROPE_THETA = 500000.0
ROT_DIM = 16
GLA_HEADS = 4
GLA_DK = 64
GLA_DV = 128
GLA_RANK = 16
GLA_TAU = 16.0
GLA_CHUNK = 64
D_IN_PROJ = 2832
D_IN_SHARD = D_IN_PROJ // N_DEV
D_OUT_SHARD = D_MODEL // N_DEV
OFF = (0, 512, 640, 768, 1280, 1536, 1792, 2304, 2816, 2832)
EPS = 1e-5
ALPHA = 2.0 ** 0.25
SWA_SCALE = SWA_HEAD_DIM ** -0.5
GLA_SCALE = GLA_DK ** -0.5
ADAM_LR = 0.001
ADAM_B1 = 0.9
ADAM_B2 = 0.999
ADAM_EPS = 1e-08
ADAM_WD = 0.01
ADAM_STEP = 10
VMEM_LIMIT = 56 * 1024 * 1024

_NT = (((1,), (1,)), ((), ()))
_TN = (((0,), (0,)), ((), ()))


def _mm(a, b):
    return jnp.dot(a, b, preferred_element_type=F32)


def _mm_nt(a, b):
    return lax.dot_general(a, b, _NT, preferred_element_type=F32)


def _mm_tn(a, b):
    return lax.dot_general(a, b, _TN, preferred_element_type=F32)


def _sigmoid(t):
    return 1.0 / (1.0 + jnp.exp(-t))


def _cparams(**kw):
    return pltpu.CompilerParams(vmem_limit_bytes=VMEM_LIMIT, **kw)


def _full(shape):
    return pl.BlockSpec(shape, lambda *_: (0,) * len(shape))


def _rows(tile, width):
    return pl.BlockSpec((tile, width), lambda i: (i, 0))


def _rope_angles(positions):
    half = ROT_DIM // 2
    inv_freq = ROPE_THETA ** (-jnp.arange(half, dtype=F32) / half)
    ang = positions.astype(F32)[:, None] * inv_freq[None, :]
    return jnp.concatenate([jnp.cos(ang), jnp.sin(ang)], axis=1)


def _split3_parts(t):
    hi = t.astype(MXU_DTYPE)
    r1 = t - hi.astype(F32)
    mid = r1.astype(MXU_DTYPE)
    return hi, mid, (r1 - mid.astype(F32)).astype(MXU_DTYPE)


def _rope_tables(cs):
    half = ROT_DIM // 2
    i = lax.broadcasted_iota(jnp.int32, (2 * half, 3 * 128), 0)
    lane = lax.broadcasted_iota(jnp.int32, (2 * half, 3 * 128), 1)
    table, pos = _idiv(lane, 128), lane & (SWA_HEAD_DIM - 1)
    is_c = (table == 0) & (pos < ROT_DIM) & ((pos & (half - 1)) == i)
    is_s1 = (table == 1) & (pos < half) & (pos + half == i)
    is_s2 = (table == 2) & (pos >= half) & (pos < ROT_DIM) & (pos == i)
    sel = jnp.where(is_c | is_s2, 1.0, jnp.where(is_s1, -1.0, 0.0)).astype(MXU_DTYPE)
    hi, mid, lo = _split3_parts(cs)
    t = (_mm(hi, sel) + _mm(mid, sel)) + _mm(lo, sel)
    pos1 = lax.broadcasted_iota(jnp.int32, (1, 128), 1) & (SWA_HEAD_DIM - 1)
    return t[:, 0:128] + jnp.where(pos1 >= ROT_DIM, 1.0, 0.0), t[:, 128:256], t[:, 256:384]


def _rope(t, c, s1, s2):
    return t * c + pltpu.roll(t, 120, 1) * s1 + pltpu.roll(t, 8, 1) * s2


def _rope_t(g, c, s1, s2):
    return g * c + pltpu.roll(g * s1, 8, 1) + pltpu.roll(g * s2, 120, 1)


def _in_proj(x, w_in_t, wg_s, b_gate, cos_sin, w_out_s):
    s = x.shape[0]
    ts = min(512, s)
    nsteps = s // ts
    forward_step = min(3, nsteps - 1)
    widths = [OFF[i + 1] - OFF[i] for i in range(9)]

    def body(x_ref, win_hbm, wgs_ref, bg_ref, cs_ref, wos_ref,
             qa_ref, ka_ref, va_ref, ga_ref, qb_ref, kb_ref, vb_ref, gb_ref, rb_ref, la_ref, oms_ref,
             c_ref, s1_ref, s2_ref, w_ref, wg_ref, wout_ref,
             win_all, wg_all, wout_all, stage, stage_sem, *sems):
        xb = x_ref[...].astype(MXU_DTYPE)
        c, s1, s2 = _rope_tables(cs_ref[...])
        c_ref[...], s1_ref[...], s2_ref[...] = c, s1, s2
        i0 = pl.program_id(0)
        gather = _BlockGather(wout_all, *sems[0:2])

        @pl.when(i0 == 0)
        def _():
            ka_ref[0:BLOCK, :] = jnp.zeros((BLOCK, 128), ka_ref.dtype)
            va_ref[0:BLOCK, :] = jnp.zeros((BLOCK, 128), va_ref.dtype)
            first = (_BlockGather(win_all, *sems[2:4]), _BlockGather(wg_all, *sems[4:6]))
            load = pltpu.make_async_copy(win_hbm.at[:, 0, :], stage, stage_sem)
            load.start()
            wout_all[gather.me] = wos_ref[...].astype(wout_all.dtype)
            wg_all[gather.me] = wgs_ref[...].astype(wg_all.dtype)
            load.wait()
            win_all[gather.me] = stage[...].astype(win_all.dtype)
            for stage_of in ("start", "forward", "finish"):
                for g in first:
                    getattr(g, stage_of)()
                if stage_of == "start":
                    gather.start()
            for j in range(N_DEV):
                w_ref[D_IN_SHARD * j:D_IN_SHARD * (j + 1), :] = win_all[j]
                wg_ref[:, 32 * j:32 * (j + 1)] = wg_all[j]

        @pl.when(i0 == forward_step)
        def _():
            gather.forward()

        @pl.when(i0 == nsteps - 1)
        def _():
            gather.finish()
            for j in range(N_DEV):
                wout_ref[D_OUT_SHARD * j:D_OUT_SHARD * (j + 1), :] = wout_all[j]

        kv_rows = pl.ds(pl.multiple_of(BLOCK + i0 * ts, BLOCK), ts)

        def cols(i):
            return _mm_nt(xb, w_ref[OFF[i]:OFF[i + 1], :])

        qa = cols(0)
        for i in range(4):
            qa_ref[:, 128 * i:128 * (i + 1)] = _rope(qa[:, 128 * i:128 * (i + 1)], c, s1, s2).astype(qa_ref.dtype)
        kv = _mm_nt(xb, w_ref[OFF[1]:OFF[3], :])
        ka_ref[kv_rows, :] = _rope(kv[:, 0:128], c, s1, s2).astype(ka_ref.dtype)
        va_ref[kv_rows, :] = kv[:, 128:256].astype(va_ref.dtype)
        ga_ref[...] = cols(3)
        qb_ref[...] = cols(4)
        kb_ref[...] = cols(5)
        vb_ref[...] = cols(6).astype(vb_ref.dtype)
        gb_ref[...] = cols(7)
        rb = cols(8)
        rb_ref[...] = rb
        logit = _mm(rb.astype(MXU_DTYPE), wg_ref[...]) + bg_ref[...]
        e = jnp.exp(-jnp.abs(logit))
        la_ref[...] = (jnp.minimum(logit, 0.0) - jnp.log(1.0 + e)) / GLA_TAU
        oms_ref[...] = jnp.where(logit >= 0.0, e, 1.0) / (1.0 + e)

    out_shape = [jax.ShapeDtypeStruct((s + BLOCK if i in (1, 2) else s, w), MXU_DTYPE if i in (0, 1, 2, 6) else F32)
                 for i, w in enumerate(widths)]
    out_shape += [jax.ShapeDtypeStruct((s, 256), F32)] * 2 + [jax.ShapeDtypeStruct((s, 128), F32)] * 3
    out_shape += [jax.ShapeDtypeStruct((D_IN_PROJ, D_MODEL), MXU_DTYPE), jax.ShapeDtypeStruct((GLA_RANK, 256), MXU_DTYPE),
                  jax.ShapeDtypeStruct((D_MODEL, D_MODEL), MXU_DTYPE)]
    return pl.pallas_call(
        body, name="in_proj", grid=(nsteps,),
        in_specs=[_rows(ts, D_MODEL), pl.BlockSpec(memory_space=pl.ANY), _full((GLA_RANK, 32)), _full((1, 256)),
                  _rows(ts, ROT_DIM), _full((D_OUT_SHARD, D_MODEL))],
        out_specs=[_full((s + BLOCK, w)) if i in (1, 2) else _rows(ts, w) for i, w in enumerate(widths)]
        + [_rows(ts, 256)] * 2 + [_rows(ts, 128)] * 3
        + [_full((D_IN_PROJ, D_MODEL)), _full((GLA_RANK, 256)), _full((D_MODEL, D_MODEL))],
        out_shape=out_shape,
        scratch_shapes=[pltpu.VMEM((N_DEV, D_IN_SHARD, D_MODEL), MXU_DTYPE), pltpu.VMEM((N_DEV, GLA_RANK, 32), MXU_DTYPE),
                        pltpu.VMEM((N_DEV, D_OUT_SHARD, D_MODEL), MXU_DTYPE),
                        pltpu.VMEM((D_IN_SHARD, D_MODEL), F32), pltpu.SemaphoreType.DMA]
        + 3 * _BlockGather.scratch(),
        compiler_params=_cparams(dimension_semantics=("arbitrary",)),
    )(x, w_in_t, wg_s, b_gate, cos_sin, w_out_s)


SWA_ROWS = SWA_GROUP * BLOCK


def _swa_bias():
    shape = (2, 2 * BLOCK, SWA_ROWS)
    ki = lax.broadcasted_iota(jnp.int32, shape, 1)
    qi = lax.broadcasted_iota(jnp.int32, shape, 2) & (BLOCK - 1)
    first = lax.broadcasted_iota(jnp.int32, shape, 0) == 0
    dist = qi + BLOCK - ki
    ok = (dist >= 0) & (dist < BLOCK) & (jnp.logical_not(first) | (ki >= BLOCK))
    return jnp.where(ok, 0.0, -jnp.inf).astype(F32)


SWA_SUB = 2


def _swa_bias_of(bias_ref, n, b):
    return bias_ref[jnp.minimum(n, 1)] if b == 0 else bias_ref[1]


def _swa_dup(t, j):
    t = t.astype(F32)
    low = lax.broadcasted_iota(jnp.int32, t.shape, 1) < SWA_HEAD_DIM
    keep = low if j == 0 else jnp.logical_not(low)
    return jnp.where(keep, t, pltpu.roll(t, SWA_HEAD_DIM, 1)).astype(MXU_DTYPE)


def _swa_stack(t, j):
    low = lax.broadcasted_iota(jnp.int32, (BLOCK, 128), 1) < SWA_HEAD_DIM
    zero = jnp.zeros((BLOCK, 128), t.dtype)
    blocks = []
    for p in (2 * j, 2 * j + 1):
        tp = t[:, 128 * p:128 * (p + 1)]
        blocks += [jnp.where(low, tp, zero), jnp.where(low, zero, tp)]
    return jnp.concatenate(blocks, axis=0)


def _swa_unstack(t):
    low = lax.broadcasted_iota(jnp.int32, (BLOCK, 128), 1) < SWA_HEAD_DIM
    return [jnp.where(low, t[2 * BLOCK * i:2 * BLOCK * i + BLOCK], t[2 * BLOCK * i + BLOCK:2 * BLOCK * (i + 1)])
            for i in range(2)]


def _swa_sink_row(sink_ref, j):
    lane = lax.broadcasted_iota(jnp.int32, (1, SWA_ROWS), 1)
    row = jnp.full((1, SWA_ROWS), sink_ref[SWA_GROUP * j], F32)
    for r in range(1, SWA_GROUP):
        row = jnp.where(lane >= BLOCK * r, sink_ref[SWA_GROUP * j + r], row)
    return row


def _split3(t):
    return jnp.concatenate(_split3_parts(t), axis=1)


def _row_sums_as_row(t):
    ones = jnp.ones((8, 3 * t.shape[1]), MXU_DTYPE)
    return _mm_nt(ones, _split3(t))[0:1, :]


def _swa_probs_t(qs, kd, bias_t, sink):
    sc = _mm_nt(kd, qs) + bias_t
    m = jnp.maximum(jnp.max(sc, axis=0, keepdims=True), sink)
    p = jnp.exp(sc - m)
    ps = jnp.exp(sink - m)
    rinv = 1.0 / (jnp.sum(p, axis=0, keepdims=True) + ps)
    return p * rinv, ps * rinv


def _swa_fwd(sinks, qa, k_pad, v_pad, ga):
    s = qa.shape[0]
    tq = SWA_SUB * BLOCK

    def body(sink_ref, qa_ref, ga_ref, bias_ref, k_ref, v_ref, attn_ref, cat_ref):
        n = pl.program_id(0)
        for b in range(SWA_SUB):
            rows = slice(BLOCK * b, BLOCK * (b + 1))
            start = pl.multiple_of((n * SWA_SUB + b) * BLOCK, BLOCK)
            kw = k_ref[pl.ds(start, 2 * BLOCK), :]
            vw = v_ref[pl.ds(start, 2 * BLOCK), :]
            bias_t = _swa_bias_of(bias_ref, n, b)
            q = qa_ref[rows, :] * SWA_SCALE
            g = ga_ref[rows, :]
            silu = g * _sigmoid(g)
            for j in range(SWA_KV_HEADS):
                qs = _swa_stack(q, j).astype(MXU_DTYPE)
                probs, _ = _swa_probs_t(qs, _swa_dup(kw, j), bias_t, _swa_sink_row(sink_ref, j))
                pairs = _swa_unstack(_mm_tn(probs.astype(MXU_DTYPE), _swa_dup(vw, j)))
                for i in range(2):
                    lanes = slice(128 * (2 * j + i), 128 * (2 * j + i + 1))
                    attn_ref[rows, lanes] = pairs[i]
                    cat_ref[rows, lanes] = (pairs[i] * silu[:, lanes]).astype(cat_ref.dtype)

    return pl.pallas_call(
        body, name="swa_fwd", grid=(s // tq,),
        in_specs=[pl.BlockSpec(memory_space=pltpu.SMEM), _rows(tq, 512), _rows(tq, 512),
                  _full((2, 2 * BLOCK, SWA_ROWS)), _full((s + BLOCK, 128)), _full((s + BLOCK, 128))],
        out_specs=[_rows(tq, 512), _rows(tq, 512)],
        out_shape=[jax.ShapeDtypeStruct((s, 512), F32), jax.ShapeDtypeStruct((s, 512), MXU_DTYPE)],
        compiler_params=_cparams(dimension_semantics=("arbitrary",)),
    )(sinks, qa, ga, _swa_bias(), k_pad, v_pad)


GLA_KW = GLA_HEADS * GLA_DK
GLA_VW = GLA_HEADS * GLA_DV


def _idiv(t, d):
    return t >> (d.bit_length() - 1)


def _chunk_cumsum(t, lower):
    n, w = t.shape
    r = lax.broadcasted_iota(jnp.int32, (n, n), 0)
    c = lax.broadcasted_iota(jnp.int32, (n, n), 1)
    tri = ((_idiv(r, GLA_CHUNK) == _idiv(c, GLA_CHUNK)) & ((r >= c) if lower else (r <= c))).astype(MXU_DTYPE)
    parts = _mm(tri, _split3(t))
    return (parts[:, :w] + parts[:, w:2 * w]) + parts[:, 2 * w:]


def _chunk_last(t):
    n = t.shape[0]
    return jnp.concatenate(
        [jnp.broadcast_to(t[c + GLA_CHUNK - 1:c + GLA_CHUNK, :], (GLA_CHUNK, t.shape[1]))
         for c in range(0, n, GLA_CHUNK)], axis=0)


def _head_stack(t, width):
    head = _idiv(lax.broadcasted_iota(jnp.int32, t.shape, 1), width)
    zero = jnp.zeros_like(t)
    return jnp.concatenate([jnp.where(head == h, t, zero) for h in range(GLA_HEADS)], axis=0)


def _heads_to_rows(t):
    return jnp.concatenate([t[:, GLA_DV * h:GLA_DV * (h + 1)] for h in range(GLA_HEADS)], axis=0)


def _rows_to_heads(t):
    return jnp.concatenate([t[GLA_CHUNK * h:GLA_CHUNK * (h + 1)] for h in range(GLA_HEADS)], axis=1)


def _state_by_head(t):
    srow = _idiv(lax.broadcasted_iota(jnp.int32, (GLA_VW, GLA_KW), 0), GLA_DV)
    slane = _idiv(lax.broadcasted_iota(jnp.int32, (GLA_VW, GLA_KW), 1), GLA_DK)
    return jnp.where(srow == slane, jnp.concatenate([t] * GLA_HEADS, axis=0), jnp.zeros((GLA_VW, GLA_KW), t.dtype))


def _gla_masks():
    row = lax.broadcasted_iota(jnp.int32, (GLA_CHUNK, GLA_KW), 0)
    pos = lax.broadcasted_iota(jnp.int32, (GLA_CHUNK, GLA_KW), 1) & (GLA_CHUNK - 1)
    return pos <= row, pos >= row


def _gla_fwd(qb, kb, vb, la, gb, norm_w):
    s = qb.shape[0]
    tb = min(256, s)
    ch = tb // GLA_CHUNK

    def body(qb_ref, kb_ref, vb_ref, la_ref, gb_ref, nw_ref, o_ref, cat_ref, sp_ref, st_ref):
        @pl.when(pl.program_id(0) == 0)
        def _():
            st_ref[...] = jnp.zeros_like(st_ref)

        causal, _ = _gla_masks()
        nw = nw_ref[...]
        b = _chunk_cumsum(la_ref[...], True)
        bl = _chunk_last(b)
        k = kb_ref[...]
        qd = ((qb_ref[...] * GLA_SCALE) * jnp.exp(b)).astype(MXU_DTYPE)
        ki = (k * jnp.exp(-b)).astype(MXU_DTYPE)
        ke = (k * jnp.exp(bl - b)).astype(MXU_DTYPE)
        dec = jnp.exp(bl)
        v = vb_ref[...].astype(MXU_DTYPE)
        g = gb_ref[...]
        silu = g * _sigmoid(g)
        for ci in range(ch):
            rows = slice(GLA_CHUNK * ci, GLA_CHUNK * (ci + 1))
            qds, kis, kes = (_head_stack(t[rows], GLA_DK) for t in (qd, ki, ke))
            a = jnp.where(causal, _mm_nt(qd[rows], kis), 0.0).astype(MXU_DTYPE)
            st = st_ref[...]
            sp_ref[ci] = st
            o = _mm(a, _head_stack(v[rows], GLA_DV)) + _rows_to_heads(_mm_nt(qds, st.astype(MXU_DTYPE)))
            st_ref[...] = st * dec[rows][0:1] + _mm_tn(_heads_to_rows(v[rows]), kes)
            o_ref[rows, :] = o
            for h in range(GLA_HEADS):
                lv = slice(GLA_DV * h, GLA_DV * (h + 1))
                oh = o[:, lv]
                r = lax.rsqrt(jnp.mean(oh * oh, axis=1, keepdims=True) + EPS)
                cat_ref[rows, lv] = (oh * r * nw * silu[rows, lv]).astype(cat_ref.dtype)

    return pl.pallas_call(
        body, name="gla_fwd", grid=(s // tb,),
        in_specs=[_rows(tb, 256), _rows(tb, 256), _rows(tb, 512), _rows(tb, 256), _rows(tb, 512), _full((1, 128))],
        out_specs=[_rows(tb, 512), _rows(tb, 512), pl.BlockSpec((ch, GLA_DV, 256), lambda i: (i, 0, 0))],
        out_shape=[jax.ShapeDtypeStruct((s, 512), F32), jax.ShapeDtypeStruct((s, 512), MXU_DTYPE),
                   jax.ShapeDtypeStruct((s // GLA_CHUNK, GLA_DV, 256), F32)],
        scratch_shapes=[pltpu.VMEM((GLA_DV, GLA_KW), F32)],
        compiler_params=_cparams(dimension_semantics=("arbitrary",)),
    )(qb, kb, vb, la, gb, norm_w)


def _out_ln_loss(cat_a, cat_b, w_out, x, target, ln_g, ln_b):
    s = x.shape[0]
    ts = min(512, s)
    halves = 2 if ts % 32 == 0 else 1
    th = ts // halves

    def body(ca_ref, cb_ref, w_ref, x_ref, t_ref, g_ref, b_ref,
             loss_ref, gx_ref, da_ref, db_ref, gw_ref, gln_ref):
        @pl.when(pl.program_id(0) == 0)
        def _():
            loss_ref[...] = jnp.zeros_like(loss_ref)
            gw_ref[...] = jnp.zeros_like(gw_ref)
            gln_ref[...] = jnp.zeros_like(gln_ref)

        g = g_ref[...]
        dh16s = []
        for k in range(halves):
            rows = slice(th * k, th * (k + 1))
            mix = _mm(ca_ref[rows, :], w_ref[0:512, :]) + _mm(cb_ref[rows, :], w_ref[512:1024, :])
            h = ALPHA * x_ref[rows, :] + mix
            mu = jnp.mean(h, axis=1, keepdims=True)
            hc = h - mu
            rstd = lax.rsqrt(jnp.mean(hc * hc, axis=1, keepdims=True) + EPS)
            xhat = hc * rstd
            err = xhat * g + b_ref[...] - t_ref[rows, :]
            loss_ref[...] += 0.5 * jnp.sum(jnp.mean(err * err, axis=1, keepdims=True))
            dy = err * (1.0 / D_MODEL)
            gln_ref[0:1, :] += jnp.sum(dy * xhat, axis=0, keepdims=True)
            gln_ref[1:2, :] += jnp.sum(dy, axis=0, keepdims=True)
            dxh = dy * g
            dh = rstd * (dxh - jnp.mean(dxh, axis=1, keepdims=True)
                         - xhat * jnp.mean(dxh * xhat, axis=1, keepdims=True))
            gx_ref[rows, :] = ALPHA * dh
            dh16s.append(dh.astype(MXU_DTYPE))
        for k in range(halves):
            rows = slice(th * k, th * (k + 1))
            da_ref[rows, :] = _mm_nt(dh16s[k], w_ref[0:512, :])
            db_ref[rows, :] = _mm_nt(dh16s[k], w_ref[512:1024, :])
        dh16 = jnp.concatenate(dh16s, axis=0)
        gw_ref[0:512, :] += _mm_tn(ca_ref[...], dh16)
        gw_ref[512:1024, :] += _mm_tn(cb_ref[...], dh16)

    return pl.pallas_call(
        body, name="out_ln_loss", grid=(s // ts,),
        in_specs=[_rows(ts, 512), _rows(ts, 512), _full((D_MODEL, D_MODEL)), _rows(ts, D_MODEL), _rows(ts, D_MODEL),
                  _full((1, D_MODEL)), _full((1, D_MODEL))],
        out_specs=[_full((1, 128)), _rows(ts, D_MODEL), _rows(ts, 512), _rows(ts, 512),
                   _full((D_MODEL, D_MODEL)), _full((2, D_MODEL))],
        out_shape=[jax.ShapeDtypeStruct((1, 128), F32), jax.ShapeDtypeStruct((s, D_MODEL), F32),
                   jax.ShapeDtypeStruct((s, 512), F32), jax.ShapeDtypeStruct((s, 512), F32),
                   jax.ShapeDtypeStruct((D_MODEL, D_MODEL), F32), jax.ShapeDtypeStruct((2, D_MODEL), F32)],
        compiler_params=_cparams(dimension_semantics=("arbitrary",)),
    )(cat_a, cat_b, w_out, x, target, ln_g, ln_b)


def _swa_bwd(sinks, qa, k_pad, v_pad, attn, ga, d_cat_a, rope, parts_w_out):
    s = qa.shape[0]
    tq = SWA_SUB * BLOCK
    nsteps = s // tq
    forward_step = min(2, nsteps - 1)

    def body(sink_ref, qa_ref, ga_ref, at_ref, dc_ref, c_ref, s1_ref, s2_ref, bias_ref, k_ref, v_ref, pout_ref,
             dq_ref, dg_ref, dk_ref, dv_ref, ds_ref, gout_ref, *scratch):
        n = pl.program_id(0)
        owner_sum = _OwnerSum(pout_ref, *scratch)

        @pl.when(n == 0)
        def _():
            dk_ref[...] = jnp.zeros_like(dk_ref)
            dv_ref[...] = jnp.zeros_like(dv_ref)
            ds_ref[...] = jnp.zeros_like(ds_ref)
            owner_sum.start()

        @pl.when(n == forward_step)
        def _():
            owner_sum.forward()

        @pl.when(n == nsteps - 1)
        def _():
            gout_ref[...] = owner_sum.finish()

        low = lax.broadcasted_iota(jnp.int32, (2 * BLOCK, 128), 1) < SWA_HEAD_DIM
        for b in range(SWA_SUB):
            rows = slice(BLOCK * b, BLOCK * (b + 1))
            start = pl.multiple_of((n * SWA_SUB + b) * BLOCK, BLOCK)
            kw = k_ref[pl.ds(start, 2 * BLOCK), :]
            vw = v_ref[pl.ds(start, 2 * BLOCK), :]
            bias_t = _swa_bias_of(bias_ref, n, b)
            q = qa_ref[rows, :] * SWA_SCALE
            g = ga_ref[rows, :]
            sg = _sigmoid(g)
            o = at_ref[rows, :]
            dc = dc_ref[rows, :]
            do = dc * (g * sg)
            dg_ref[rows, :] = (dc * o * (sg * (1.0 + g * (1.0 - sg)))).astype(dg_ref.dtype)
            od = do * o
            c, s1, s2 = c_ref[rows, :], s1_ref[rows, :], s2_ref[rows, :]
            dk, dv = [], []
            for j in range(SWA_KV_HEADS):
                kd, vd = _swa_dup(kw, j), _swa_dup(vw, j)
                qs = _swa_stack(q, j).astype(MXU_DTYPE)
                dos = _swa_stack(do, j).astype(MXU_DTYPE)
                probs, psink = _swa_probs_t(qs, kd, bias_t, _swa_sink_row(sink_ref, j))
                delta = _row_sums_as_row(_swa_stack(od, j))
                dsc = (probs * (_mm_nt(vd, dos) - delta)).astype(MXU_DTYPE)
                dsink = psink * delta
                for r in range(SWA_GROUP):
                    h = SWA_GROUP * j + r
                    ds_ref[h:h + 1, :] += jnp.zeros((1, 128), F32) - jnp.sum(dsink[:, BLOCK * r:BLOCK * (r + 1)])
                dq = _swa_unstack(_mm_tn(dsc, kd))
                for i in range(2):
                    lanes = slice(128 * (2 * j + i), 128 * (2 * j + i + 1))
                    dq_ref[rows, lanes] = _rope_t(dq[i] * SWA_SCALE, c, s1, s2).astype(dq_ref.dtype)
                dkj = _mm(dsc, qs)
                dvj = _mm(probs.astype(MXU_DTYPE), dos)
                dk.append(dkj + pltpu.roll(dkj, SWA_HEAD_DIM, 1))
                dv.append(dvj + pltpu.roll(dvj, SWA_HEAD_DIM, 1))
            dk_ref[pl.ds(start, 2 * BLOCK), :] += jnp.where(low, dk[0], dk[1])
            dv_ref[pl.ds(start, 2 * BLOCK), :] += jnp.where(low, dv[0], dv[1])

    out_blk = parts_w_out.shape[1:]
    return pl.pallas_call(
        body, name="swa_bwd", grid=(nsteps,),
        in_specs=[pl.BlockSpec(memory_space=pltpu.SMEM)] + [_rows(tq, 512)] * 4 + [_rows(tq, 128)] * 3
        + [_full((2, 2 * BLOCK, SWA_ROWS))] + [_full((s + BLOCK, 128))] * 2 + [pl.BlockSpec(memory_space=pl.ANY)],
        out_specs=[_rows(tq, 512), _rows(tq, 512), _full((s + BLOCK, 128)), _full((s + BLOCK, 128)),
                   _full((SWA_Q_HEADS, 128)), _full(out_blk)],
        out_shape=[jax.ShapeDtypeStruct((s, 512), MXU_DTYPE), jax.ShapeDtypeStruct((s, 512), MXU_DTYPE),
                   jax.ShapeDtypeStruct((s + BLOCK, 128), F32), jax.ShapeDtypeStruct((s + BLOCK, 128), F32),
                   jax.ShapeDtypeStruct((SWA_Q_HEADS, 128), F32), jax.ShapeDtypeStruct(out_blk, F32)],
        scratch_shapes=_OwnerSum.scratch(out_blk),
        compiler_params=_cparams(dimension_semantics=("arbitrary",)),
    )(sinks, qa, ga, attn, d_cat_a, *rope, _swa_bias(), k_pad, v_pad, parts_w_out)


def _gla_bwd(qb, kb, vb, la, oms, gb, o, sprev, d_cat_b, rb, wg, norm_w):
    s = qb.shape[0]
    tb = min(512, s)
    ch = tb // GLA_CHUNK
    nb = s // tb

    def body(qb_ref, kb_ref, vb_ref, la_ref, oms_ref, gb_ref, o_ref, sp_ref, dc_ref, rb_ref, wg_ref, nw_ref,
             dq_ref, dk_ref, dv_ref, dg_ref, dr_ref, gwg_ref, gbg_ref, gnw_ref, dst_ref):
        @pl.when(pl.program_id(0) == 0)
        def _():
            dst_ref[...] = jnp.zeros_like(dst_ref)
            gwg_ref[...] = jnp.zeros_like(gwg_ref)
            gbg_ref[...] = jnp.zeros_like(gbg_ref)
            gnw_ref[...] = jnp.zeros_like(gnw_ref)

        causal, causal_t = _gla_masks()
        nw = nw_ref[...]
        b = _chunk_cumsum(la_ref[...], True)
        bl = _chunk_last(b)
        eb, enb, ee, dec = jnp.exp(b), jnp.exp(-b), jnp.exp(bl - b), jnp.exp(bl)
        k = kb_ref[...]
        qd = (qb_ref[...] * GLA_SCALE) * eb
        ki = k * enb
        ke = k * ee
        qd16, ki16, ke16 = qd.astype(MXU_DTYPE), ki.astype(MXU_DTYPE), ke.astype(MXU_DTYPE)
        v16 = vb_ref[...].astype(MXU_DTYPE)

        g = gb_ref[...]
        sg = _sigmoid(g)
        silu = g * sg
        dsilu = sg * (1.0 + g * (1.0 - sg))
        gnw = jnp.zeros((1, GLA_DV), F32)
        do = []
        for h in range(GLA_HEADS):
            lv = slice(GLA_DV * h, GLA_DV * (h + 1))
            oh = o_ref[:, lv]
            dch = dc_ref[:, lv]
            r = lax.rsqrt(jnp.mean(oh * oh, axis=1, keepdims=True) + EPS)
            d_on = dch * silu[:, lv]
            dg_ref[:, lv] = (dch * (oh * r * nw) * dsilu[:, lv]).astype(dg_ref.dtype)
            gnw += jnp.sum(d_on * oh * r, axis=0, keepdims=True)
            u = d_on * nw
            do.append(r * u - oh * (r * r * r) * jnp.mean(u * oh, axis=1, keepdims=True))
        gnw_ref[...] += gnw
        do16 = jnp.concatenate(do, axis=1).astype(MXU_DTYPE)

        db, dbl = [None] * ch, [None] * ch
        for ci in reversed(range(ch)):
            rows = slice(GLA_CHUNK * ci, GLA_CHUNK * (ci + 1))
            qds, kis, kes = (_head_stack(t[rows], GLA_DK) for t in (qd16, ki16, ke16))
            vs, dos = _head_stack(v16[rows], GLA_DV), _head_stack(do16[rows], GLA_DV)
            a = jnp.where(causal, _mm_nt(qd16[rows], kis), 0.0).astype(MXU_DTYPE)
            at = jnp.where(causal_t, _mm_nt(ki16[rows], qds), 0.0).astype(MXU_DTYPE)
            da = jnp.where(causal, _mm_nt(do16[rows], vs), 0.0).astype(MXU_DTYPE)
            dat = jnp.where(causal_t, _mm_nt(v16[rows], dos), 0.0).astype(MXU_DTYPE)
            st = sp_ref[ci]
            dst = dst_ref[...]
            dst16 = dst.astype(MXU_DTYPE)
            dv = _mm(at, dos) + _rows_to_heads(_mm_nt(kes, dst16))
            dqd = _mm(da, kis) + _mm(do16[rows], _state_by_head(st.astype(MXU_DTYPE)))
            dki = _mm(dat, qds)
            dke = _mm(v16[rows], _state_by_head(dst16))
            ddec = jnp.sum(dst * st, axis=0, keepdims=True)
            decc = dec[rows][0:1]
            dst_ref[...] = _mm_tn(_heads_to_rows(do16[rows]), qds) + dst * decc
            dq_ref[rows, :] = (dqd * eb[rows] * GLA_SCALE).astype(dq_ref.dtype)
            dk_ref[rows, :] = (dki * enb[rows] + dke * ee[rows]).astype(dk_ref.dtype)
            dv_ref[rows, :] = dv.astype(dv_ref.dtype)
            dke_ke = dke * ke[rows]
            db[ci] = dqd * qd[rows] - dki * ki[rows] - dke_ke
            dbl[ci] = jnp.broadcast_to(jnp.sum(dke_ke, axis=0, keepdims=True) + ddec * decc, (GLA_CHUNK, GLA_KW))

        dla = _chunk_cumsum(jnp.concatenate(db, axis=0), False) + jnp.concatenate(dbl, axis=0)
        dlogit = dla * oms_ref[...] * (1.0 / GLA_TAU)
        dl16 = dlogit.astype(MXU_DTYPE)
        gbg_ref[...] += jnp.sum(dlogit, axis=0, keepdims=True)
        gwg_ref[...] += _mm_tn(rb_ref[...].astype(MXU_DTYPE), dl16)
        dr_ref[...] = _mm_nt(dl16, wg_ref[...]).astype(dr_ref.dtype)

    def rev(width):
        return pl.BlockSpec((tb, width), lambda i: (nb - 1 - i, 0))

    return pl.pallas_call(
        body, name="gla_bwd", grid=(nb,),
        in_specs=[rev(256), rev(256), rev(512), rev(256), rev(256), rev(512), rev(512),
                  pl.BlockSpec((ch, GLA_DV, 256), lambda i: (nb - 1 - i, 0, 0)), rev(512), rev(GLA_RANK),
                  _full((GLA_RANK, 256)), _full((1, 128))],
        out_specs=[rev(256), rev(256), rev(512), rev(512), rev(GLA_RANK),
                   _full((GLA_RANK, 256)), _full((1, 256)), _full((1, 128))],
        out_shape=[jax.ShapeDtypeStruct((s, 256), MXU_DTYPE), jax.ShapeDtypeStruct((s, 256), MXU_DTYPE),
                   jax.ShapeDtypeStruct((s, 512), MXU_DTYPE), jax.ShapeDtypeStruct((s, 512), MXU_DTYPE),
                   jax.ShapeDtypeStruct((s, GLA_RANK), MXU_DTYPE), jax.ShapeDtypeStruct((GLA_RANK, 256), F32),
                   jax.ShapeDtypeStruct((1, 256), F32), jax.ShapeDtypeStruct((1, 128), F32)],
        scratch_shapes=[pltpu.VMEM((GLA_DV, GLA_KW), F32)],
        compiler_params=_cparams(dimension_semantics=("arbitrary",)),
    )(qb, kb, vb, la, oms, gb, o, sprev, d_cat_b, rb, wg, norm_w)


def _in_proj_bwd_x(gx0, pieces, w_in, rope):
    s = gx0.shape[0]
    ts = min(512, s)
    widths = [OFF[i + 1] - OFF[i] for i in range(9)]

    def body(gx0_ref, *refs):
        piece_refs = refs[:9]
        w_ref, c_ref, s1_ref, s2_ref, gx_ref, dp_ref = refs[9:]
        kv_rows = pl.ds(pl.multiple_of(BLOCK + pl.program_id(0) * ts, BLOCK), ts)
        acc = gx0_ref[...]
        for i in (0, 1, 3, 4, 5, 6, 7, 8):
            lo, hi = OFF[i], OFF[i + 1]
            if i == 1:
                dk = _rope_t(piece_refs[1][kv_rows, :], c_ref[...], s1_ref[...], s2_ref[...])
                t16 = jnp.concatenate([dk, piece_refs[2][kv_rows, :]], axis=1).astype(MXU_DTYPE)
                hi = OFF[3]
            else:
                t16 = piece_refs[i][...].astype(MXU_DTYPE)
            dp_ref[:, lo:hi] = t16
            acc += _mm(t16, w_ref[lo:hi, :])
        gx_ref[...] = acc

    return pl.pallas_call(
        body, name="in_proj_bwd_x", grid=(s // ts,),
        in_specs=[_rows(ts, D_MODEL)]
        + [_full((s + BLOCK, w)) if i in (1, 2) else _rows(ts, w) for i, w in enumerate(widths)]
        + [_full((D_IN_PROJ, D_MODEL))] + [_rows(ts, 128)] * 3,
        out_specs=[_rows(ts, D_MODEL), _rows(ts, D_IN_PROJ)],
        out_shape=[jax.ShapeDtypeStruct((s, D_MODEL), F32), jax.ShapeDtypeStruct((s, D_IN_PROJ), MXU_DTYPE)],
        compiler_params=_cparams(dimension_semantics=("arbitrary",)),
    )(gx0, *pieces, w_in, *rope)


def _in_proj_bwd_w(x, dproj):
    s = x.shape[0]
    ts = min(1024, s)
    nsteps = s // ts
    col_chunks = [(OFF[i], OFF[i + 2] if i == 1 else OFF[i + 1]) for i in (0, 1, 3, 4, 5, 6, 7, 8)]

    def body(x_ref, dp_ref, gw_ref, acc_ref, stage_ref, sems):
        i = pl.program_id(0)

        @pl.when(i == 0)
        def _():
            acc_ref[...] = jnp.zeros_like(acc_ref)

        xb = x_ref[...].astype(MXU_DTYPE)
        for lo, hi in col_chunks:
            acc_ref[lo:hi, :] += _mm_tn(dp_ref[:, lo:hi], xb)

        @pl.when(i == nsteps - 1)
        def _():
            copies = []
            for j in range(N_DEV):
                slot = j % 2
                if j >= 2:
                    copies[j - 2].wait()
                stage_ref[slot] = acc_ref[D_IN_SHARD * j:D_IN_SHARD * (j + 1), :]
                cp = pltpu.make_async_copy(stage_ref.at[slot], gw_ref.at[j], sems.at[slot])
                cp.start()
                copies.append(cp)
            copies[N_DEV - 2].wait()
            copies[N_DEV - 1].wait()

    return pl.pallas_call(
        body, name="in_proj_bwd_w", grid=(nsteps,),
        in_specs=[_rows(ts, D_MODEL), _rows(ts, D_IN_PROJ)],
        out_specs=pl.BlockSpec(memory_space=pl.ANY),
        out_shape=jax.ShapeDtypeStruct((N_DEV, D_IN_SHARD, D_MODEL), F32),
        scratch_shapes=[pltpu.VMEM((D_IN_PROJ, D_MODEL), F32), pltpu.VMEM((2, D_IN_SHARD, D_MODEL), F32),
                        pltpu.SemaphoreType.DMA((2,))],
        compiler_params=_cparams(dimension_semantics=("arbitrary",)),
    )(x, dproj)


def _local_step(x, positions, w_in_t, wg_s, b_gate, sinks, norm_w, w_out_s, ln_g, ln_b, target):
    qa, k_pad, v_pad, ga, qb, kb, vb, gb, rb, la, oms, *rope, w_in, wg, w_out = _in_proj(
        x, w_in_t, wg_s, b_gate, _rope_angles(positions), w_out_s)
    attn, cat_a = _swa_fwd(sinks, qa, k_pad, v_pad, ga)
    o, cat_b, sprev = _gla_fwd(qb, kb, vb, la, gb, norm_w)
    loss, gx0, d_cat_a, d_cat_b, g_w_out, g_ln = _out_ln_loss(cat_a, cat_b, w_out, x, target, ln_g, ln_b)
    parts_w_out = g_w_out.reshape(N_DEV, D_OUT_SHARD, D_MODEL)
    dqa, dga, dk_pad, dv_pad, g_sinks, g_out = _swa_bwd(sinks, qa, k_pad, v_pad, attn, ga, d_cat_a, rope, parts_w_out)
    dqb, dkb, dvb, dgb, drb, g_wg, g_bg, g_nw = _gla_bwd(qb, kb, vb, la, oms, gb, o, sprev, d_cat_b, rb, wg, norm_w)
    pieces = (dqa, dk_pad, dv_pad, dga, dqb, dkb, dvb, dgb, drb)
    grad_x, dproj = _in_proj_bwd_x(gx0, pieces, w_in, rope)
    g_w_in = _in_proj_bwd_w(x, dproj)
    return loss, grad_x, g_w_in, g_wg, g_bg, g_sinks, g_nw, g_out, g_ln


def _mesh_pos():
    return lax.axis_index("x"), lax.axis_index("y"), lax.axis_index("c")


def _peer(k, x, y, c):
    px = (1 - x) if k & 4 else x
    py = (1 - y) if k & 2 else y
    pc = (1 - c) if k & 1 else c
    return (px, py, pc), 4 * px + 2 * py + pc


def _other_chips(x, y):
    return [(1 - x, y), (x, 1 - y), (1 - x, 1 - y)]


def _shard_view(t):
    return jnp.transpose(t, (2, 0, 1))


class _BlockGather:
    def __init__(self, slots, send_sems, recv_sems):
        self.slots, self.send_sems, self.recv_sems = slots, send_sems, recv_sems
        x, y, c = _mesh_pos()
        self.xy, self.c, self.me, self.sibling = (x, y), c, 4 * x + 2 * y + c, (x, y, 1 - c)
        self.chips = _other_chips(x, y)

    @staticmethod
    def scratch():
        return [pltpu.SemaphoreType.DMA((N_DEV - 1,)), pltpu.SemaphoreType.DMA((N_DEV - 1,))]

    def _copy(self, k, block, to):
        return pltpu.make_async_remote_copy(
            src_ref=self.slots.at[block], dst_ref=self.slots.at[block], send_sem=self.send_sems.at[k],
            recv_sem=self.recv_sems.at[k], device_id=to, device_id_type=pl.DeviceIdType.MESH)

    def start(self):
        for j, (cx, cy) in enumerate(self.chips):
            self._copy(1 + j, self.me, (cx, cy, self.c)).start()
        self._copy(0, self.me, self.sibling).start()

    def forward(self):
        for j, (cx, cy) in enumerate(self.chips):
            block = 4 * cx + 2 * cy + self.c
            self._copy(1 + j, block, self.sibling).wait_recv()
            self._copy(4 + j, block, self.sibling).start()

    def finish(self):
        x, y = self.xy
        self._copy(0, 4 * x + 2 * y + (1 - self.c), self.sibling).wait_recv()
        for j, (cx, cy) in enumerate(self.chips):
            self._copy(4 + j, 4 * cx + 2 * cy + (1 - self.c), self.sibling).wait_recv()
        for k in range(N_DEV - 1):
            self._copy(k, self.me, self.sibling).wait_send()


class _OwnerSum:
    def __init__(self, parts, own, sib, snd, rcv, loc_sems, d2d_send, d2d_recv, ici_send, ici_recv):
        self.parts, self.own, self.sib, self.snd, self.rcv = parts, own, sib, snd, rcv
        self.sems = (loc_sems, d2d_send, d2d_recv, ici_send, ici_recv)
        x, y, c = _mesh_pos()
        self.c, self.sibling = c, (x, y, 1 - c)
        self.chips = [(x, y)] + _other_chips(x, y)

    @staticmethod
    def scratch(block):
        return [pltpu.VMEM((4,) + block, F32), pltpu.VMEM((4,) + block, F32),
                pltpu.VMEM((3,) + block, MXU_DTYPE), pltpu.VMEM((3,) + block, MXU_DTYPE),
                pltpu.SemaphoreType.DMA((4,)), pltpu.SemaphoreType.DMA((4,)), pltpu.SemaphoreType.DMA((4,)),
                pltpu.SemaphoreType.DMA((3,)), pltpu.SemaphoreType.DMA((3,))]

    def _local(self, r):
        cx, cy = self.chips[r]
        return pltpu.make_async_copy(self.parts.at[4 * cx + 2 * cy + self.c], self.own.at[r], self.sems[0].at[r])

    def _d2d(self, r):
        cx, cy = self.chips[r]
        return pltpu.make_async_remote_copy(
            src_ref=self.parts.at[4 * cx + 2 * cy + (1 - self.c)], dst_ref=self.sib.at[r], send_sem=self.sems[1].at[r],
            recv_sem=self.sems[2].at[r], device_id=self.sibling, device_id_type=pl.DeviceIdType.MESH)

    def _ici(self, r):
        cx, cy = self.chips[r]
        return pltpu.make_async_remote_copy(
            src_ref=self.snd.at[r - 1], dst_ref=self.rcv.at[r - 1], send_sem=self.sems[3].at[r - 1],
            recv_sem=self.sems[4].at[r - 1], device_id=(cx, cy, self.c), device_id_type=pl.DeviceIdType.MESH)

    def start(self):
        for r in (1, 2, 3, 0):
            self._local(r).start()
            self._d2d(r).start()

    def forward(self):
        for r in (1, 2, 3):
            self._local(r).wait()
            self._d2d(r).wait_recv()
            self.snd[r - 1] = (self.own[r] + self.sib[r]).astype(self.snd.dtype)
            self._ici(r).start()

    def finish(self):
        self._local(0).wait()
        self._d2d(0).wait_recv()
        acc = self.own[0] + self.sib[0]
        for r in (1, 2, 3):
            self._ici(r).wait_recv()
            acc = acc + self.rcv[r - 1].astype(F32)
        for r in range(4):
            self._d2d(r).wait_send()
        for r in (1, 2, 3):
            self._ici(r).wait_send()
        return acc


SMALL_ROWS = 48


def _reduce_grads(parts_w_in, parts_wg, g_ln, g_bg, g_nw, g_sinks, loss):
    def body(pin_ref, pwg_ref, gln_ref, gbg_ref, gnw_ref, gsk_ref, loss_ref, gin_ref, rwg_ref, rsm_ref, sm_ref,
             *scratch):
        sm_send, sm_recv, sm_loc = scratch[-3:]
        x, y, c = _mesh_pos()
        me = 4 * x + 2 * y + c
        owner_sum = _OwnerSum(pin_ref, *scratch[:-3])

        sm_ref[...] = jnp.zeros_like(sm_ref)
        for r in range(D_MODEL // 128):
            sm_ref[r:r + 1, :] = gln_ref[0:1, 128 * r:128 * (r + 1)]
            sm_ref[8 + r:9 + r, :] = gln_ref[1:2, 128 * r:128 * (r + 1)]
        for r in range(2):
            sm_ref[16 + r:17 + r, :] = gbg_ref[0:1, 128 * r:128 * (r + 1)]
        sm_ref[24:25, :] = gnw_ref[...]
        diag = lax.broadcasted_iota(jnp.int32, gsk_ref.shape, 0) == lax.broadcasted_iota(jnp.int32, gsk_ref.shape, 1)
        sm_ref[32:33, :] = jnp.sum(jnp.where(diag, gsk_ref[...], 0.0), axis=0, keepdims=True)
        sm_ref[40:41, :] = loss_ref[...]

        small_dsts = (rwg_ref, rsm_ref)

        def small_src(a, block):
            return pwg_ref.at[block] if a == 0 else sm_ref

        small_local = [pltpu.make_async_copy(small_src(a, me), small_dsts[a].at[me], sm_loc.at[a]) for a in range(2)]
        for cp in small_local:
            cp.start()
        small_sends = []
        for k in range(1, N_DEV):
            peer, pidx = _peer(k, x, y, c)
            for a in range(2):
                i = 2 * (k - 1) + a
                cp = pltpu.make_async_remote_copy(
                    src_ref=small_src(a, pidx), dst_ref=small_dsts[a].at[me], send_sem=sm_send.at[i],
                    recv_sem=sm_recv.at[i], device_id=peer, device_id_type=pl.DeviceIdType.MESH)
                cp.start()
                small_sends.append(cp)

        owner_sum.start()
        owner_sum.forward()
        gin_ref[...] = owner_sum.finish()

        for k in range(1, N_DEV):
            peer, pidx = _peer(k, x, y, c)
            for a in range(2):
                i = 2 * (k - 1) + a
                pltpu.make_async_remote_copy(
                    src_ref=small_src(a, me), dst_ref=small_dsts[a].at[pidx], send_sem=sm_send.at[i],
                    recv_sem=sm_recv.at[i], device_id=peer, device_id_type=pl.DeviceIdType.MESH).wait_recv()
        for cp in small_sends:
            cp.wait_send()
        for cp in small_local:
            cp.wait()

    hbm = pl.BlockSpec(memory_space=pl.ANY)
    vmem = pl.BlockSpec(memory_space=pltpu.VMEM)
    in_blk = parts_w_in.shape[1:]
    return pl.pallas_call(
        body, name="reduce_grads",
        in_specs=[hbm, hbm] + [vmem] * 5, out_specs=[vmem, hbm, hbm],
        out_shape=[jax.ShapeDtypeStruct(in_blk, F32),
                   jax.ShapeDtypeStruct((N_DEV,) + parts_wg.shape[1:], F32),
                   jax.ShapeDtypeStruct((N_DEV, SMALL_ROWS, 128), F32)],
        scratch_shapes=[pltpu.VMEM((SMALL_ROWS, 128), F32)] + _OwnerSum.scratch(in_blk)
        + [pltpu.SemaphoreType.DMA((2 * (N_DEV - 1),)), pltpu.SemaphoreType.DMA((2 * (N_DEV - 1),)),
           pltpu.SemaphoreType.DMA((2,))],
        compiler_params=_cparams(),
    )(parts_w_in, parts_wg, g_ln, g_bg, g_nw, g_sinks, loss)


def _adamw(recv, w, m, v, name):
    rows, width = w.shape
    tr = 128 if rows % 128 == 0 else rows
    n_parts = recv.shape[0]

    def body(r_ref, w_ref, m_ref, v_ref, g_ref, d_ref, nm_ref, nv_ref):
        g = r_ref[0]
        for j in range(1, n_parts):
            g = g + r_ref[j]
        nm = ADAM_B1 * m_ref[...] + (1.0 - ADAM_B1) * g
        nv = ADAM_B2 * v_ref[...] + (1.0 - ADAM_B2) * (g * g)
        m_hat = nm / (1.0 - ADAM_B1 ** ADAM_STEP)
        v_hat = nv / (1.0 - ADAM_B2 ** ADAM_STEP)
        g_ref[...] = g
        d_ref[...] = -ADAM_LR * (m_hat / (jnp.sqrt(v_hat) + ADAM_EPS) + ADAM_WD * w_ref[...])
        nm_ref[...] = nm
        nv_ref[...] = nv

    spec = _rows(tr, width)
    return pl.pallas_call(
        body, name=name, grid=(rows // tr,),
        in_specs=[pl.BlockSpec((n_parts, tr, width), lambda i: (0, i, 0)), spec, spec, spec],
        out_specs=[spec] * 4,
        out_shape=[jax.ShapeDtypeStruct((rows, width), F32)] * 4,
        compiler_params=_cparams(dimension_semantics=("arbitrary",)),
    )(recv, w, m, v)


def _adamw_shard_view(g, w, m, v):
    rows, width = g.shape

    def body(g_ref, w_hbm, m_hbm, v_hbm, g_out, d_out, nm_out, nv_out, bufs, outs, sems):
        loads = [pltpu.make_async_copy(src.at[:, 0, :], bufs.at[i], sems.at[i])
                 for i, src in enumerate((w_hbm, m_hbm, v_hbm))]
        for cp in loads:
            cp.start()
        g = g_ref[...]
        for cp in loads:
            cp.wait()
        nm = ADAM_B1 * bufs[1] + (1.0 - ADAM_B1) * g
        nv = ADAM_B2 * bufs[2] + (1.0 - ADAM_B2) * (g * g)
        m_hat = nm / (1.0 - ADAM_B1 ** ADAM_STEP)
        v_hat = nv / (1.0 - ADAM_B2 ** ADAM_STEP)
        outs[0] = g
        outs[1] = -ADAM_LR * (m_hat / (jnp.sqrt(v_hat) + ADAM_EPS) + ADAM_WD * bufs[0])
        outs[2] = nm
        outs[3] = nv
        stores = [pltpu.make_async_copy(outs.at[i], dst.at[:, 0, :], sems.at[3 + i])
                  for i, dst in enumerate((g_out, d_out, nm_out, nv_out))]
        for cp in stores:
            cp.start()
        for cp in stores:
            cp.wait()

    hbm = pl.BlockSpec(memory_space=pl.ANY)
    return pl.pallas_call(
        body, name="adamw_w_in",
        in_specs=[pl.BlockSpec(memory_space=pltpu.VMEM), hbm, hbm, hbm], out_specs=[hbm] * 4,
        out_shape=[jax.ShapeDtypeStruct((rows, 1, width), F32)] * 4,
        scratch_shapes=[pltpu.VMEM((3, rows, width), F32), pltpu.VMEM((4, rows, width), F32),
                        pltpu.SemaphoreType.DMA((7,))],
        compiler_params=_cparams(),
    )(g, w, m, v)


def _adamw_vectors(r_small, r_wg, params):
    n_par = len(params)

    def body(rsm_ref, rwg_ref, *refs):
        ins, outs = refs[:3 * n_par], refs[3 * n_par:]
        g = rsm_ref[0]
        gwg = rwg_ref[0]
        for j in range(1, N_DEV):
            g = g + rsm_ref[j]
            gwg = gwg + rwg_ref[j]
        outs[4 * n_par][...] = g[40:41]
        grads = [gwg,
                 jnp.concatenate([g[r:r + 1] for r in range(0, 8)], axis=1),
                 jnp.concatenate([g[r:r + 1] for r in range(8, 16)], axis=1),
                 jnp.concatenate([g[16:17], g[17:18]], axis=1),
                 g[24:25],
                 g[32:33, 0:SWA_Q_HEADS]]
        for p, gp in enumerate(grads):
            w_ref, m_ref, v_ref = ins[3 * p:3 * p + 3]
            nm = ADAM_B1 * m_ref[...] + (1.0 - ADAM_B1) * gp
            nv = ADAM_B2 * v_ref[...] + (1.0 - ADAM_B2) * (gp * gp)
            m_hat = nm / (1.0 - ADAM_B1 ** ADAM_STEP)
            v_hat = nv / (1.0 - ADAM_B2 ** ADAM_STEP)
            outs[4 * p][...] = gp
            outs[4 * p + 1][...] = -ADAM_LR * (m_hat / (jnp.sqrt(v_hat) + ADAM_EPS) + ADAM_WD * w_ref[...])
            outs[4 * p + 2][...] = nm
            outs[4 * p + 3][...] = nv

    vmem = pl.BlockSpec(memory_space=pltpu.VMEM)
    flat = [t for wmv in params for t in wmv]
    return pl.pallas_call(
        body, name="adamw_vectors",
        in_specs=[vmem] * (2 + len(flat)), out_specs=[vmem] * (4 * n_par + 1),
        out_shape=[jax.ShapeDtypeStruct(wmv[0].shape, F32) for wmv in params for _ in range(4)]
        + [jax.ShapeDtypeStruct((1, 128), F32)],
        compiler_params=_cparams(),
    )(r_small, r_wg, *flat)


def kernel(x, positions, w_in, gla_w_gate_up, gla_b_gate, attn_sinks, gla_norm_w, w_out, ln_g, ln_b, loss_target, m_w_in, m_gla_w_gate_up, m_gla_b_gate, m_attn_sinks, m_gla_norm_w, m_w_out, m_ln_g, m_ln_b, v_w_in, v_gla_w_gate_up, v_gla_b_gate, v_attn_sinks, v_gla_norm_w, v_w_out, v_ln_g, v_ln_b):
    loss, grad_x, parts_w_in, g_wg, g_bg, g_sinks, g_nw, g_out, g_ln = _local_step(
        x[0], positions[0], _shard_view(w_in), gla_w_gate_up[0], gla_b_gate, attn_sinks[0], gla_norm_w, w_out[0],
        ln_g, ln_b, loss_target[0])

    parts_wg = jnp.transpose(g_wg.reshape(GLA_RANK, N_DEV, 32), (1, 0, 2))
    g_in, r_wg, r_small = _reduce_grads(parts_w_in, parts_wg, g_ln, g_bg, g_nw, g_sinks, loss)

    upd_in = _adamw_shard_view(g_in, _shard_view(w_in), _shard_view(m_w_in), _shard_view(v_w_in))
    upd_in = [jnp.transpose(t, (1, 2, 0)) for t in upd_in]
    upd_out = _adamw(g_out[None], w_out[0], m_w_out[0], v_w_out[0], "adamw_w_out")
    vec = _adamw_vectors(r_small, r_wg, [
        (gla_w_gate_up[0], m_gla_w_gate_up[0], v_gla_w_gate_up[0]), (ln_g, m_ln_g, v_ln_g), (ln_b, m_ln_b, v_ln_b),
        (gla_b_gate, m_gla_b_gate, v_gla_b_gate), (gla_norm_w, m_gla_norm_w, v_gla_norm_w),
        (attn_sinks, m_attn_sinks, v_attn_sinks)])

    outs = [vec[24][0, 0], grad_x[None]]
    for kind in range(4):
        u_wg, u_ln_g, u_ln_b, u_bg, u_nw, u_sinks = (vec[4 * p + kind] for p in range(6))
        outs += [upd_in[kind], u_wg[None], u_bg, u_sinks, u_nw, upd_out[kind][None], u_ln_g, u_ln_b]
    return tuple(outs)
```

```python
import jax
import jax.numpy as jnp
from jax import lax
from jax.experimental import pallas as pl
from jax.experimental.pallas import tpu as pltpu

F32 = jnp.float32
MXU_DTYPE = jnp.bfloat16

N_DEV = 8
D_MODEL = 1024
SWA_Q_HEADS = 8
SWA_KV_HEADS = 2
SWA_GROUP = 4
SWA_HEAD_DIM = 64
BLOCK = 128
ROPE_THETA = 500000.0
ROT_DIM = 16
GLA_HEADS = 4
GLA_DK = 64
GLA_DV = 128
GLA_RANK = 16
GLA_TAU = 16.0
GLA_CHUNK = 64
D_IN_PROJ = 2832
D_IN_SHARD = D_IN_PROJ // N_DEV
D_OUT_SHARD = D_MODEL // N_DEV
OFF = (0, 512, 640, 768, 1280, 1536, 1792, 2304, 2816, 2832)
EPS = 1e-5
ALPHA = 2.0 ** 0.25
SWA_SCALE = SWA_HEAD_DIM ** -0.5
GLA_SCALE = GLA_DK ** -0.5
ADAM_LR = 0.001
ADAM_B1 = 0.9
ADAM_B2 = 0.999
ADAM_EPS = 1e-08
ADAM_WD = 0.01
ADAM_STEP = 10
VMEM_LIMIT = 56 * 1024 * 1024

_NT = (((1,), (1,)), ((), ()))
_TN = (((0,), (0,)), ((), ()))


def _mm(a, b):
    return jnp.dot(a, b, preferred_element_type=F32)


def _mm_nt(a, b):
    return lax.dot_general(a, b, _NT, preferred_element_type=F32)


def _mm_tn(a, b):
    return lax.dot_general(a, b, _TN, preferred_element_type=F32)


def _sigmoid(t):
    return 1.0 / (1.0 + jnp.exp(-t))


def _cparams(**kw):
    return pltpu.CompilerParams(vmem_limit_bytes=VMEM_LIMIT, **kw)


def _full(shape):
    return pl.BlockSpec(shape, lambda *_: (0,) * len(shape))


def _rows(tile, width):
    return pl.BlockSpec((tile, width), lambda i: (i, 0))


def _rope_angles(positions):
    half = ROT_DIM // 2
    inv_freq = ROPE_THETA ** (-jnp.arange(half, dtype=F32) / half)
    ang = positions.astype(F32)[:, None] * inv_freq[None, :]
    return jnp.concatenate([jnp.cos(ang), jnp.sin(ang)], axis=1)


def _split3_parts(t):
    hi = t.astype(MXU_DTYPE)
    r1 = t - hi.astype(F32)
    mid = r1.astype(MXU_DTYPE)
    return hi, mid, (r1 - mid.astype(F32)).astype(MXU_DTYPE)


def _rope_tables(cs):
    half = ROT_DIM // 2
    i = lax.broadcasted_iota(jnp.int32, (2 * half, 3 * 128), 0)
    lane = lax.broadcasted_iota(jnp.int32, (2 * half, 3 * 128), 1)
    table, pos = _idiv(lane, 128), lane & (SWA_HEAD_DIM - 1)
    is_c = (table == 0) & (pos < ROT_DIM) & ((pos & (half - 1)) == i)
    is_s1 = (table == 1) & (pos < half) & (pos + half == i)
    is_s2 = (table == 2) & (pos >= half) & (pos < ROT_DIM) & (pos == i)
    sel = jnp.where(is_c | is_s2, 1.0, jnp.where(is_s1, -1.0, 0.0)).astype(MXU_DTYPE)
    hi, mid, lo = _split3_parts(cs)
    t = (_mm(hi, sel) + _mm(mid, sel)) + _mm(lo, sel)
    pos1 = lax.broadcasted_iota(jnp.int32, (1, 128), 1) & (SWA_HEAD_DIM - 1)
    return t[:, 0:128] + jnp.where(pos1 >= ROT_DIM, 1.0, 0.0), t[:, 128:256], t[:, 256:384]


def _rope(t, c, s1, s2):
    return t * c + pltpu.roll(t, 120, 1) * s1 + pltpu.roll(t, 8, 1) * s2


def _rope_t(g, c, s1, s2):
    return g * c + pltpu.roll(g * s1, 8, 1) + pltpu.roll(g * s2, 120, 1)


def _in_proj(x, w_in_t, wg_s, b_gate, cos_sin, w_out_s):
    s = x.shape[0]
    ts = min(512, s)
    nsteps = s // ts
    forward_step = min(3, nsteps - 1)
    widths = [OFF[i + 1] - OFF[i] for i in range(9)]

    def body(x_ref, win_hbm, wgs_ref, bg_ref, cs_ref, wos_ref,
             qa_ref, ka_ref, va_ref, ga_ref, qb_ref, kb_ref, vb_ref, gb_ref, rb_ref, la_ref, oms_ref,
             c_ref, s1_ref, s2_ref, w_ref, wg_ref, wout_ref,
             win_all, wg_all, wout_all, stage, stage_sem, *sems):
        xb = x_ref[...].astype(MXU_DTYPE)
        c, s1, s2 = _rope_tables(cs_ref[...])
        c_ref[...], s1_ref[...], s2_ref[...] = c, s1, s2
        i0 = pl.program_id(0)
        gather = _BlockGather(wout_all, *sems[0:2])

        @pl.when(i0 == 0)
        def _():
            ka_ref[0:BLOCK, :] = jnp.zeros((BLOCK, 128), ka_ref.dtype)
            va_ref[0:BLOCK, :] = jnp.zeros((BLOCK, 128), va_ref.dtype)
            first = (_BlockGather(win_all, *sems[2:4]), _BlockGather(wg_all, *sems[4:6]))
            load = pltpu.make_async_copy(win_hbm.at[:, 0, :], stage, stage_sem)
            load.start()
            wout_all[gather.me] = wos_ref[...].astype(wout_all.dtype)
            wg_all[gather.me] = wgs_ref[...].astype(wg_all.dtype)
            load.wait()
            win_all[gather.me] = stage[...].astype(win_all.dtype)
            for stage_of in ("start", "forward", "finish"):
                for g in first:
                    getattr(g, stage_of)()
            gather.start()
            for j in range(N_DEV):
                w_ref[D_IN_SHARD * j:D_IN_SHARD * (j + 1), :] = win_all[j]
                wg_ref[:, 32 * j:32 * (j + 1)] = wg_all[j]

        @pl.when(i0 == forward_step)
        def _():
            gather.forward()

        @pl.when(i0 == nsteps - 1)
        def _():
            gather.finish()
            for j in range(N_DEV):
                wout_ref[D_OUT_SHARD * j:D_OUT_SHARD * (j + 1), :] = wout_all[j]

        kv_rows = pl.ds(pl.multiple_of(BLOCK + i0 * ts, BLOCK), ts)

        def cols(i):
            return _mm_nt(xb, w_ref[OFF[i]:OFF[i + 1], :])

        qa = cols(0)
        for i in range(4):
            qa_ref[:, 128 * i:128 * (i + 1)] = _rope(qa[:, 128 * i:128 * (i + 1)], c, s1, s2).astype(qa_ref.dtype)
        kv = _mm_nt(xb, w_ref[OFF[1]:OFF[3], :])
        ka_ref[kv_rows, :] = _rope(kv[:, 0:128], c, s1, s2).astype(ka_ref.dtype)
        va_ref[kv_rows, :] = kv[:, 128:256].astype(va_ref.dtype)
        ga_ref[...] = cols(3)
        qb_ref[...] = cols(4)
        kb_ref[...] = cols(5)
        vb_ref[...] = cols(6).astype(vb_ref.dtype)
        gb_ref[...] = cols(7)
        rb = cols(8)
        rb_ref[...] = rb
        logit = _mm(rb.astype(MXU_DTYPE), wg_ref[...]) + bg_ref[...]
        e = jnp.exp(-jnp.abs(logit))
        la_ref[...] = (jnp.minimum(logit, 0.0) - jnp.log(1.0 + e)) / GLA_TAU
        oms_ref[...] = jnp.where(logit >= 0.0, e, 1.0) / (1.0 + e)

    out_shape = [jax.ShapeDtypeStruct((s + BLOCK if i in (1, 2) else s, w), MXU_DTYPE if i in (0, 1, 2, 6) else F32)
                 for i, w in enumerate(widths)]
    out_shape += [jax.ShapeDtypeStruct((s, 256), F32)] * 2 + [jax.ShapeDtypeStruct((s, 128), F32)] * 3
    out_shape += [jax.ShapeDtypeStruct((D_IN_PROJ, D_MODEL), MXU_DTYPE), jax.ShapeDtypeStruct((GLA_RANK, 256), MXU_DTYPE),
                  jax.ShapeDtypeStruct((D_MODEL, D_MODEL), MXU_DTYPE)]
    return pl.pallas_call(
        body, name="in_proj", grid=(nsteps,),
        in_specs=[_rows(ts, D_MODEL), pl.BlockSpec(memory_space=pl.ANY), _full((GLA_RANK, 32)), _full((1, 256)),
                  _rows(ts, ROT_DIM), _full((D_OUT_SHARD, D_MODEL))],
        out_specs=[_full((s + BLOCK, w)) if i in (1, 2) else _rows(ts, w) for i, w in enumerate(widths)]
        + [_rows(ts, 256)] * 2 + [_rows(ts, 128)] * 3
        + [_full((D_IN_PROJ, D_MODEL)), _full((GLA_RANK, 256)), _full((D_MODEL, D_MODEL))],
        out_shape=out_shape,
        scratch_shapes=[pltpu.VMEM((N_DEV, D_IN_SHARD, D_MODEL), MXU_DTYPE), pltpu.VMEM((N_DEV, GLA_RANK, 32), MXU_DTYPE),
                        pltpu.VMEM((N_DEV, D_OUT_SHARD, D_MODEL), MXU_DTYPE),
                        pltpu.VMEM((D_IN_SHARD, D_MODEL), F32), pltpu.SemaphoreType.DMA]
        + 3 * _BlockGather.scratch(),
        compiler_params=_cparams(dimension_semantics=("arbitrary",)),
    )(x, w_in_t, wg_s, b_gate, cos_sin, w_out_s)


SWA_ROWS = SWA_GROUP * BLOCK


def _swa_bias():
    shape = (2, 2 * BLOCK, SWA_ROWS)
    ki = lax.broadcasted_iota(jnp.int32, shape, 1)
    qi = lax.broadcasted_iota(jnp.int32, shape, 2) & (BLOCK - 1)
    first = lax.broadcasted_iota(jnp.int32, shape, 0) == 0
    dist = qi + BLOCK - ki
    ok = (dist >= 0) & (dist < BLOCK) & (jnp.logical_not(first) | (ki >= BLOCK))
    return jnp.where(ok, 0.0, -jnp.inf).astype(F32)


SWA_SUB = 2


def _swa_bias_of(bias_ref, n, b):
    return bias_ref[jnp.minimum(n, 1)] if b == 0 else bias_ref[1]


def _swa_dup(t, j):
    t = t.astype(F32)
    low = lax.broadcasted_iota(jnp.int32, t.shape, 1) < SWA_HEAD_DIM
    keep = low if j == 0 else jnp.logical_not(low)
    return jnp.where(keep, t, pltpu.roll(t, SWA_HEAD_DIM, 1)).astype(MXU_DTYPE)


def _swa_stack(t, j):
    low = lax.broadcasted_iota(jnp.int32, (BLOCK, 128), 1) < SWA_HEAD_DIM
    zero = jnp.zeros((BLOCK, 128), t.dtype)
    blocks = []
    for p in (2 * j, 2 * j + 1):
        tp = t[:, 128 * p:128 * (p + 1)]
        blocks += [jnp.where(low, tp, zero), jnp.where(low, zero, tp)]
    return jnp.concatenate(blocks, axis=0)


def _swa_unstack(t):
    low = lax.broadcasted_iota(jnp.int32, (BLOCK, 128), 1) < SWA_HEAD_DIM
    return [jnp.where(low, t[2 * BLOCK * i:2 * BLOCK * i + BLOCK], t[2 * BLOCK * i + BLOCK:2 * BLOCK * (i + 1)])
            for i in range(2)]


def _swa_sink_row(sink_ref, j):
    lane = lax.broadcasted_iota(jnp.int32, (1, SWA_ROWS), 1)
    row = jnp.full((1, SWA_ROWS), sink_ref[SWA_GROUP * j], F32)
    for r in range(1, SWA_GROUP):
        row = jnp.where(lane >= BLOCK * r, sink_ref[SWA_GROUP * j + r], row)
    return row


def _split3(t):
    return jnp.concatenate(_split3_parts(t), axis=1)


def _row_sums_as_row(t):
    ones = jnp.ones((8, 3 * t.shape[1]), MXU_DTYPE)
    return _mm_nt(ones, _split3(t))[0:1, :]


def _swa_probs_t(qs, kd, bias_t, sink):
    sc = _mm_nt(kd, qs) + bias_t
    m = jnp.maximum(jnp.max(sc, axis=0, keepdims=True), sink)
    p = jnp.exp(sc - m)
    ps = jnp.exp(sink - m)
    rinv = 1.0 / (jnp.sum(p, axis=0, keepdims=True) + ps)
    return p * rinv, ps * rinv


def _swa_fwd(sinks, qa, k_pad, v_pad, ga):
    s = qa.shape[0]
    tq = SWA_SUB * BLOCK

    def body(sink_ref, qa_ref, ga_ref, bias_ref, k_ref, v_ref, attn_ref, cat_ref):
        n = pl.program_id(0)
        for b in range(SWA_SUB):
            rows = slice(BLOCK * b, BLOCK * (b + 1))
            start = pl.multiple_of((n * SWA_SUB + b) * BLOCK, BLOCK)
            kw = k_ref[pl.ds(start, 2 * BLOCK), :]
            vw = v_ref[pl.ds(start, 2 * BLOCK), :]
            bias_t = _swa_bias_of(bias_ref, n, b)
            q = qa_ref[rows, :] * SWA_SCALE
            g = ga_ref[rows, :]
            silu = g * _sigmoid(g)
            for j in range(SWA_KV_HEADS):
                qs = _swa_stack(q, j).astype(MXU_DTYPE)
                probs, _ = _swa_probs_t(qs, _swa_dup(kw, j), bias_t, _swa_sink_row(sink_ref, j))
                pairs = _swa_unstack(_mm_tn(probs.astype(MXU_DTYPE), _swa_dup(vw, j)))
                for i in range(2):
                    lanes = slice(128 * (2 * j + i), 128 * (2 * j + i + 1))
                    attn_ref[rows, lanes] = pairs[i]
                    cat_ref[rows, lanes] = (pairs[i] * silu[:, lanes]).astype(cat_ref.dtype)

    return pl.pallas_call(
        body, name="swa_fwd", grid=(s // tq,),
        in_specs=[pl.BlockSpec(memory_space=pltpu.SMEM), _rows(tq, 512), _rows(tq, 512),
                  _full((2, 2 * BLOCK, SWA_ROWS)), _full((s + BLOCK, 128)), _full((s + BLOCK, 128))],
        out_specs=[_rows(tq, 512), _rows(tq, 512)],
        out_shape=[jax.ShapeDtypeStruct((s, 512), F32), jax.ShapeDtypeStruct((s, 512), MXU_DTYPE)],
        compiler_params=_cparams(dimension_semantics=("arbitrary",)),
    )(sinks, qa, ga, _swa_bias(), k_pad, v_pad)


GLA_KW = GLA_HEADS * GLA_DK
GLA_VW = GLA_HEADS * GLA_DV


def _idiv(t, d):
    return t >> (d.bit_length() - 1)


def _chunk_cumsum(t, lower):
    n, w = t.shape
    r = lax.broadcasted_iota(jnp.int32, (n, n), 0)
    c = lax.broadcasted_iota(jnp.int32, (n, n), 1)
    tri = ((_idiv(r, GLA_CHUNK) == _idiv(c, GLA_CHUNK)) & ((r >= c) if lower else (r <= c))).astype(MXU_DTYPE)
    parts = _mm(tri, _split3(t))
    return (parts[:, :w] + parts[:, w:2 * w]) + parts[:, 2 * w:]


def _chunk_last(t):
    n = t.shape[0]
    return jnp.concatenate(
        [jnp.broadcast_to(t[c + GLA_CHUNK - 1:c + GLA_CHUNK, :], (GLA_CHUNK, t.shape[1]))
         for c in range(0, n, GLA_CHUNK)], axis=0)


def _head_stack(t, width):
    head = _idiv(lax.broadcasted_iota(jnp.int32, t.shape, 1), width)
    zero = jnp.zeros_like(t)
    return jnp.concatenate([jnp.where(head == h, t, zero) for h in range(GLA_HEADS)], axis=0)


def _heads_to_rows(t):
    return jnp.concatenate([t[:, GLA_DV * h:GLA_DV * (h + 1)] for h in range(GLA_HEADS)], axis=0)


def _rows_to_heads(t):
    return jnp.concatenate([t[GLA_CHUNK * h:GLA_CHUNK * (h + 1)] for h in range(GLA_HEADS)], axis=1)


def _state_by_head(t):
    srow = _idiv(lax.broadcasted_iota(jnp.int32, (GLA_VW, GLA_KW), 0), GLA_DV)
    slane = _idiv(lax.broadcasted_iota(jnp.int32, (GLA_VW, GLA_KW), 1), GLA_DK)
    return jnp.where(srow == slane, jnp.concatenate([t] * GLA_HEADS, axis=0), jnp.zeros((GLA_VW, GLA_KW), t.dtype))


def _gla_masks():
    row = lax.broadcasted_iota(jnp.int32, (GLA_CHUNK, GLA_KW), 0)
    pos = lax.broadcasted_iota(jnp.int32, (GLA_CHUNK, GLA_KW), 1) & (GLA_CHUNK - 1)
    return pos <= row, pos >= row


def _gla_fwd(qb, kb, vb, la, gb, norm_w):
    s = qb.shape[0]
    tb = min(256, s)
    ch = tb // GLA_CHUNK

    def body(qb_ref, kb_ref, vb_ref, la_ref, gb_ref, nw_ref, o_ref, cat_ref, sp_ref, st_ref):
        @pl.when(pl.program_id(0) == 0)
        def _():
            st_ref[...] = jnp.zeros_like(st_ref)

        causal, _ = _gla_masks()
        nw = nw_ref[...]
        b = _chunk_cumsum(la_ref[...], True)
        bl = _chunk_last(b)
        k = kb_ref[...]
        qd = ((qb_ref[...] * GLA_SCALE) * jnp.exp(b)).astype(MXU_DTYPE)
        ki = (k * jnp.exp(-b)).astype(MXU_DTYPE)
        ke = (k * jnp.exp(bl - b)).astype(MXU_DTYPE)
        dec = jnp.exp(bl)
        v = vb_ref[...].astype(MXU_DTYPE)
        g = gb_ref[...]
        silu = g * _sigmoid(g)
        for ci in range(ch):
            rows = slice(GLA_CHUNK * ci, GLA_CHUNK * (ci + 1))
            qds, kis, kes = (_head_stack(t[rows], GLA_DK) for t in (qd, ki, ke))
            a = jnp.where(causal, _mm_nt(qd[rows], kis), 0.0).astype(MXU_DTYPE)
            st = st_ref[...]
            sp_ref[ci] = st
            o = _mm(a, _head_stack(v[rows], GLA_DV)) + _rows_to_heads(_mm_nt(qds, st.astype(MXU_DTYPE)))
            st_ref[...] = st * dec[rows][0:1] + _mm_tn(_heads_to_rows(v[rows]), kes)
            o_ref[rows, :] = o
            for h in range(GLA_HEADS):
                lv = slice(GLA_DV * h, GLA_DV * (h + 1))
                oh = o[:, lv]
                r = lax.rsqrt(jnp.mean(oh * oh, axis=1, keepdims=True) + EPS)
                cat_ref[rows, lv] = (oh * r * nw * silu[rows, lv]).astype(cat_ref.dtype)

    return pl.pallas_call(
        body, name="gla_fwd", grid=(s // tb,),
        in_specs=[_rows(tb, 256), _rows(tb, 256), _rows(tb, 512), _rows(tb, 256), _rows(tb, 512), _full((1, 128))],
        out_specs=[_rows(tb, 512), _rows(tb, 512), pl.BlockSpec((ch, GLA_DV, 256), lambda i: (i, 0, 0))],
        out_shape=[jax.ShapeDtypeStruct((s, 512), F32), jax.ShapeDtypeStruct((s, 512), MXU_DTYPE),
                   jax.ShapeDtypeStruct((s // GLA_CHUNK, GLA_DV, 256), F32)],
        scratch_shapes=[pltpu.VMEM((GLA_DV, GLA_KW), F32)],
        compiler_params=_cparams(dimension_semantics=("arbitrary",)),
    )(qb, kb, vb, la, gb, norm_w)


def _out_ln_loss(cat_a, cat_b, w_out, x, target, ln_g, ln_b):
    s = x.shape[0]
    ts = min(512, s)
    halves = 2 if ts % 32 == 0 else 1
    th = ts // halves

    def body(ca_ref, cb_ref, w_ref, x_ref, t_ref, g_ref, b_ref,
             loss_ref, gx_ref, da_ref, db_ref, gw_ref, gln_ref):
        @pl.when(pl.program_id(0) == 0)
        def _():
            loss_ref[...] = jnp.zeros_like(loss_ref)
            gw_ref[...] = jnp.zeros_like(gw_ref)
            gln_ref[...] = jnp.zeros_like(gln_ref)

        g = g_ref[...]
        dh16s = []
        for k in range(halves):
            rows = slice(th * k, th * (k + 1))
            mix = _mm(ca_ref[rows, :], w_ref[0:512, :]) + _mm(cb_ref[rows, :], w_ref[512:1024, :])
            h = ALPHA * x_ref[rows, :] + mix
            mu = jnp.mean(h, axis=1, keepdims=True)
            hc = h - mu
            rstd = lax.rsqrt(jnp.mean(hc * hc, axis=1, keepdims=True) + EPS)
            xhat = hc * rstd
            err = xhat * g + b_ref[...] - t_ref[rows, :]
            loss_ref[...] += 0.5 * jnp.sum(jnp.mean(err * err, axis=1, keepdims=True))
            dy = err * (1.0 / D_MODEL)
            gln_ref[0:1, :] += jnp.sum(dy * xhat, axis=0, keepdims=True)
            gln_ref[1:2, :] += jnp.sum(dy, axis=0, keepdims=True)
            dxh = dy * g
            dh = rstd * (dxh - jnp.mean(dxh, axis=1, keepdims=True)
                         - xhat * jnp.mean(dxh * xhat, axis=1, keepdims=True))
            gx_ref[rows, :] = ALPHA * dh
            dh16s.append(dh.astype(MXU_DTYPE))
        for k in range(halves):
            rows = slice(th * k, th * (k + 1))
            da_ref[rows, :] = _mm_nt(dh16s[k], w_ref[0:512, :])
            db_ref[rows, :] = _mm_nt(dh16s[k], w_ref[512:1024, :])
        dh16 = jnp.concatenate(dh16s, axis=0)
        gw_ref[0:512, :] += _mm_tn(ca_ref[...], dh16)
        gw_ref[512:1024, :] += _mm_tn(cb_ref[...], dh16)

    return pl.pallas_call(
        body, name="out_ln_loss", grid=(s // ts,),
        in_specs=[_rows(ts, 512), _rows(ts, 512), _full((D_MODEL, D_MODEL)), _rows(ts, D_MODEL), _rows(ts, D_MODEL),
                  _full((1, D_MODEL)), _full((1, D_MODEL))],
        out_specs=[_full((1, 128)), _rows(ts, D_MODEL), _rows(ts, 512), _rows(ts, 512),
                   _full((D_MODEL, D_MODEL)), _full((2, D_MODEL))],
        out_shape=[jax.ShapeDtypeStruct((1, 128), F32), jax.ShapeDtypeStruct((s, D_MODEL), F32),
                   jax.ShapeDtypeStruct((s, 512), F32), jax.ShapeDtypeStruct((s, 512), F32),
                   jax.ShapeDtypeStruct((D_MODEL, D_MODEL), F32), jax.ShapeDtypeStruct((2, D_MODEL), F32)],
        compiler_params=_cparams(dimension_semantics=("arbitrary",)),
    )(cat_a, cat_b, w_out, x, target, ln_g, ln_b)


def _swa_bwd(sinks, qa, k_pad, v_pad, attn, ga, d_cat_a, rope, parts_w_out):
    s = qa.shape[0]
    tq = SWA_SUB * BLOCK
    nsteps = s // tq
    forward_step = min(2, nsteps - 1)

    def body(sink_ref, qa_ref, ga_ref, at_ref, dc_ref, c_ref, s1_ref, s2_ref, bias_ref, k_ref, v_ref, pout_ref,
             dq_ref, dg_ref, dk_ref, dv_ref, ds_ref, gout_ref, *scratch):
        n = pl.program_id(0)
        owner_sum = _OwnerSum(pout_ref, *scratch)

        @pl.when(n == 0)
        def _():
            dk_ref[...] = jnp.zeros_like(dk_ref)
            dv_ref[...] = jnp.zeros_like(dv_ref)
            ds_ref[...] = jnp.zeros_like(ds_ref)
            owner_sum.start()

        @pl.when(n == forward_step)
        def _():
            owner_sum.forward()

        @pl.when(n == nsteps - 1)
        def _():
            gout_ref[...] = owner_sum.finish()

        low = lax.broadcasted_iota(jnp.int32, (2 * BLOCK, 128), 1) < SWA_HEAD_DIM
        for b in range(SWA_SUB):
            rows = slice(BLOCK * b, BLOCK * (b + 1))
            start = pl.multiple_of((n * SWA_SUB + b) * BLOCK, BLOCK)
            kw = k_ref[pl.ds(start, 2 * BLOCK), :]
            vw = v_ref[pl.ds(start, 2 * BLOCK), :]
            bias_t = _swa_bias_of(bias_ref, n, b)
            q = qa_ref[rows, :] * SWA_SCALE
            g = ga_ref[rows, :]
            sg = _sigmoid(g)
            o = at_ref[rows, :]
            dc = dc_ref[rows, :]
            do = dc * (g * sg)
            dg_ref[rows, :] = (dc * o * (sg * (1.0 + g * (1.0 - sg)))).astype(dg_ref.dtype)
            od = do * o
            c, s1, s2 = c_ref[rows, :], s1_ref[rows, :], s2_ref[rows, :]
            dk, dv = [], []
            for j in range(SWA_KV_HEADS):
                kd, vd = _swa_dup(kw, j), _swa_dup(vw, j)
                qs = _swa_stack(q, j).astype(MXU_DTYPE)
                dos = _swa_stack(do, j).astype(MXU_DTYPE)
                probs, psink = _swa_probs_t(qs, kd, bias_t, _swa_sink_row(sink_ref, j))
                delta = _row_sums_as_row(_swa_stack(od, j))
                dsc = (probs * (_mm_nt(vd, dos) - delta)).astype(MXU_DTYPE)
                dsink = psink * delta
                for r in range(SWA_GROUP):
                    h = SWA_GROUP * j + r
                    ds_ref[h:h + 1, :] += jnp.zeros((1, 128), F32) - jnp.sum(dsink[:, BLOCK * r:BLOCK * (r + 1)])
                dq = _swa_unstack(_mm_tn(dsc, kd))
                for i in range(2):
                    lanes = slice(128 * (2 * j + i), 128 * (2 * j + i + 1))
                    dq_ref[rows, lanes] = _rope_t(dq[i] * SWA_SCALE, c, s1, s2).astype(dq_ref.dtype)
                dkj = _mm(dsc, qs)
                dvj = _mm(probs.astype(MXU_DTYPE), dos)
                dk.append(dkj + pltpu.roll(dkj, SWA_HEAD_DIM, 1))
                dv.append(dvj + pltpu.roll(dvj, SWA_HEAD_DIM, 1))
            dk_ref[pl.ds(start, 2 * BLOCK), :] += jnp.where(low, dk[0], dk[1])
            dv_ref[pl.ds(start, 2 * BLOCK), :] += jnp.where(low, dv[0], dv[1])

    out_blk = parts_w_out.shape[1:]
    return pl.pallas_call(
        body, name="swa_bwd", grid=(nsteps,),
        in_specs=[pl.BlockSpec(memory_space=pltpu.SMEM)] + [_rows(tq, 512)] * 4 + [_rows(tq, 128)] * 3
        + [_full((2, 2 * BLOCK, SWA_ROWS))] + [_full((s + BLOCK, 128))] * 2 + [pl.BlockSpec(memory_space=pl.ANY)],
        out_specs=[_rows(tq, 512), _rows(tq, 512), _full((s + BLOCK, 128)), _full((s + BLOCK, 128)),
                   _full((SWA_Q_HEADS, 128)), _full(out_blk)],
        out_shape=[jax.ShapeDtypeStruct((s, 512), MXU_DTYPE), jax.ShapeDtypeStruct((s, 512), MXU_DTYPE),
                   jax.ShapeDtypeStruct((s + BLOCK, 128), F32), jax.ShapeDtypeStruct((s + BLOCK, 128), F32),
                   jax.ShapeDtypeStruct((SWA_Q_HEADS, 128), F32), jax.ShapeDtypeStruct(out_blk, F32)],
        scratch_shapes=_OwnerSum.scratch(out_blk),
        compiler_params=_cparams(dimension_semantics=("arbitrary",)),
    )(sinks, qa, ga, attn, d_cat_a, *rope, _swa_bias(), k_pad, v_pad, parts_w_out)


def _gla_bwd(qb, kb, vb, la, oms, gb, o, sprev, d_cat_b, rb, wg, norm_w):
    s = qb.shape[0]
    tb = min(512, s)
    ch = tb // GLA_CHUNK
    nb = s // tb

    def body(qb_ref, kb_ref, vb_ref, la_ref, oms_ref, gb_ref, o_ref, sp_ref, dc_ref, rb_ref, wg_ref, nw_ref,
             dq_ref, dk_ref, dv_ref, dg_ref, dr_ref, gwg_ref, gbg_ref, gnw_ref, dst_ref):
        @pl.when(pl.program_id(0) == 0)
        def _():
            dst_ref[...] = jnp.zeros_like(dst_ref)
            gwg_ref[...] = jnp.zeros_like(gwg_ref)
            gbg_ref[...] = jnp.zeros_like(gbg_ref)
            gnw_ref[...] = jnp.zeros_like(gnw_ref)

        causal, causal_t = _gla_masks()
        nw = nw_ref[...]
        b = _chunk_cumsum(la_ref[...], True)
        bl = _chunk_last(b)
        eb, enb, ee, dec = jnp.exp(b), jnp.exp(-b), jnp.exp(bl - b), jnp.exp(bl)
        k = kb_ref[...]
        qd = (qb_ref[...] * GLA_SCALE) * eb
        ki = k * enb
        ke = k * ee
        qd16, ki16, ke16 = qd.astype(MXU_DTYPE), ki.astype(MXU_DTYPE), ke.astype(MXU_DTYPE)
        v16 = vb_ref[...].astype(MXU_DTYPE)

        g = gb_ref[...]
        sg = _sigmoid(g)
        silu = g * sg
        dsilu = sg * (1.0 + g * (1.0 - sg))
        gnw = jnp.zeros((1, GLA_DV), F32)
        do = []
        for h in range(GLA_HEADS):
            lv = slice(GLA_DV * h, GLA_DV * (h + 1))
            oh = o_ref[:, lv]
            dch = dc_ref[:, lv]
            r = lax.rsqrt(jnp.mean(oh * oh, axis=1, keepdims=True) + EPS)
            d_on = dch * silu[:, lv]
            dg_ref[:, lv] = (dch * (oh * r * nw) * dsilu[:, lv]).astype(dg_ref.dtype)
            gnw += jnp.sum(d_on * oh * r, axis=0, keepdims=True)
            u = d_on * nw
            do.append(r * u - oh * (r * r * r) * jnp.mean(u * oh, axis=1, keepdims=True))
        gnw_ref[...] += gnw
        do16 = jnp.concatenate(do, axis=1).astype(MXU_DTYPE)

        db, dbl = [None] * ch, [None] * ch
        for ci in reversed(range(ch)):
            rows = slice(GLA_CHUNK * ci, GLA_CHUNK * (ci + 1))
            qds, kis, kes = (_head_stack(t[rows], GLA_DK) for t in (qd16, ki16, ke16))
            vs, dos = _head_stack(v16[rows], GLA_DV), _head_stack(do16[rows], GLA_DV)
            a = jnp.where(causal, _mm_nt(qd16[rows], kis), 0.0).astype(MXU_DTYPE)
            at = jnp.where(causal_t, _mm_nt(ki16[rows], qds), 0.0).astype(MXU_DTYPE)
            da = jnp.where(causal, _mm_nt(do16[rows], vs), 0.0).astype(MXU_DTYPE)
            dat = jnp.where(causal_t, _mm_nt(v16[rows], dos), 0.0).astype(MXU_DTYPE)
            st = sp_ref[ci]
            dst = dst_ref[...]
            dst16 = dst.astype(MXU_DTYPE)
            dv = _mm(at, dos) + _rows_to_heads(_mm_nt(kes, dst16))
            dqd = _mm(da, kis) + _mm(do16[rows], _state_by_head(st.astype(MXU_DTYPE)))
            dki = _mm(dat, qds)
            dke = _mm(v16[rows], _state_by_head(dst16))
            ddec = jnp.sum(dst * st, axis=0, keepdims=True)
            decc = dec[rows][0:1]
            dst_ref[...] = _mm_tn(_heads_to_rows(do16[rows]), qds) + dst * decc
            dq_ref[rows, :] = (dqd * eb[rows] * GLA_SCALE).astype(dq_ref.dtype)
            dk_ref[rows, :] = (dki * enb[rows] + dke * ee[rows]).astype(dk_ref.dtype)
            dv_ref[rows, :] = dv.astype(dv_ref.dtype)
            dke_ke = dke * ke[rows]
            db[ci] = dqd * qd[rows] - dki * ki[rows] - dke_ke
            dbl[ci] = jnp.broadcast_to(jnp.sum(dke_ke, axis=0, keepdims=True) + ddec * decc, (GLA_CHUNK, GLA_KW))

        dla = _chunk_cumsum(jnp.concatenate(db, axis=0), False) + jnp.concatenate(dbl, axis=0)
        dlogit = dla * oms_ref[...] * (1.0 / GLA_TAU)
        dl16 = dlogit.astype(MXU_DTYPE)
        gbg_ref[...] += jnp.sum(dlogit, axis=0, keepdims=True)
        gwg_ref[...] += _mm_tn(rb_ref[...].astype(MXU_DTYPE), dl16)
        dr_ref[...] = _mm_nt(dl16, wg_ref[...]).astype(dr_ref.dtype)

    def rev(width):
        return pl.BlockSpec((tb, width), lambda i: (nb - 1 - i, 0))

    return pl.pallas_call(
        body, name="gla_bwd", grid=(nb,),
        in_specs=[rev(256), rev(256), rev(512), rev(256), rev(256), rev(512), rev(512),
                  pl.BlockSpec((ch, GLA_DV, 256), lambda i: (nb - 1 - i, 0, 0)), rev(512), rev(GLA_RANK),
                  _full((GLA_RANK, 256)), _full((1, 128))],
        out_specs=[rev(256), rev(256), rev(512), rev(512), rev(GLA_RANK),
                   _full((GLA_RANK, 256)), _full((1, 256)), _full((1, 128))],
        out_shape=[jax.ShapeDtypeStruct((s, 256), MXU_DTYPE), jax.ShapeDtypeStruct((s, 256), MXU_DTYPE),
                   jax.ShapeDtypeStruct((s, 512), MXU_DTYPE), jax.ShapeDtypeStruct((s, 512), MXU_DTYPE),
                   jax.ShapeDtypeStruct((s, GLA_RANK), MXU_DTYPE), jax.ShapeDtypeStruct((GLA_RANK, 256), F32),
                   jax.ShapeDtypeStruct((1, 256), F32), jax.ShapeDtypeStruct((1, 128), F32)],
        scratch_shapes=[pltpu.VMEM((GLA_DV, GLA_KW), F32)],
        compiler_params=_cparams(dimension_semantics=("arbitrary",)),
    )(qb, kb, vb, la, oms, gb, o, sprev, d_cat_b, rb, wg, norm_w)


def _in_proj_bwd_x(gx0, pieces, w_in, rope):
    s = gx0.shape[0]
    ts = min(512, s)
    widths = [OFF[i + 1] - OFF[i] for i in range(9)]

    def body(gx0_ref, *refs):
        piece_refs = refs[:9]
        w_ref, c_ref, s1_ref, s2_ref, gx_ref, dp_ref = refs[9:]
        kv_rows = pl.ds(pl.multiple_of(BLOCK + pl.program_id(0) * ts, BLOCK), ts)
        acc = gx0_ref[...]
        for i in (0, 1, 3, 4, 5, 6, 7, 8):
            lo, hi = OFF[i], OFF[i + 1]
            if i == 1:
                dk = _rope_t(piece_refs[1][kv_rows, :], c_ref[...], s1_ref[...], s2_ref[...])
                t16 = jnp.concatenate([dk, piece_refs[2][kv_rows, :]], axis=1).astype(MXU_DTYPE)
                hi = OFF[3]
            else:
                t16 = piece_refs[i][...].astype(MXU_DTYPE)
            dp_ref[:, lo:hi] = t16
            acc += _mm(t16, w_ref[lo:hi, :])
        gx_ref[...] = acc

    return pl.pallas_call(
        body, name="in_proj_bwd_x", grid=(s // ts,),
        in_specs=[_rows(ts, D_MODEL)]
        + [_full((s + BLOCK, w)) if i in (1, 2) else _rows(ts, w) for i, w in enumerate(widths)]
        + [_full((D_IN_PROJ, D_MODEL))] + [_rows(ts, 128)] * 3,
        out_specs=[_rows(ts, D_MODEL), _rows(ts, D_IN_PROJ)],
        out_shape=[jax.ShapeDtypeStruct((s, D_MODEL), F32), jax.ShapeDtypeStruct((s, D_IN_PROJ), MXU_DTYPE)],
        compiler_params=_cparams(dimension_semantics=("arbitrary",)),
    )(gx0, *pieces, w_in, *rope)


def _in_proj_bwd_w(x, dproj):
    s = x.shape[0]
    ts = min(1024, s)
    nsteps = s // ts
    col_chunks = [(OFF[i], OFF[i + 2] if i == 1 else OFF[i + 1]) for i in (0, 1, 3, 4, 5, 6, 7, 8)]

    def body(x_ref, dp_ref, gw_ref, acc_ref, stage_ref, sems):
        i = pl.program_id(0)

        @pl.when(i == 0)
        def _():
            acc_ref[...] = jnp.zeros_like(acc_ref)

        xb = x_ref[...].astype(MXU_DTYPE)
        for lo, hi in col_chunks:
            acc_ref[lo:hi, :] += _mm_tn(dp_ref[:, lo:hi], xb)

        @pl.when(i == nsteps - 1)
        def _():
            copies = []
            for j in range(N_DEV):
                slot = j % 2
                if j >= 2:
                    copies[j - 2].wait()
                stage_ref[slot] = acc_ref[D_IN_SHARD * j:D_IN_SHARD * (j + 1), :]
                cp = pltpu.make_async_copy(stage_ref.at[slot], gw_ref.at[j], sems.at[slot])
                cp.start()
                copies.append(cp)
            copies[N_DEV - 2].wait()
            copies[N_DEV - 1].wait()

    return pl.pallas_call(
        body, name="in_proj_bwd_w", grid=(nsteps,),
        in_specs=[_rows(ts, D_MODEL), _rows(ts, D_IN_PROJ)],
        out_specs=pl.BlockSpec(memory_space=pl.ANY),
        out_shape=jax.ShapeDtypeStruct((N_DEV, D_IN_SHARD, D_MODEL), F32),
        scratch_shapes=[pltpu.VMEM((D_IN_PROJ, D_MODEL), F32), pltpu.VMEM((2, D_IN_SHARD, D_MODEL), F32),
                        pltpu.SemaphoreType.DMA((2,))],
        compiler_params=_cparams(dimension_semantics=("arbitrary",)),
    )(x, dproj)


def _local_step(x, positions, w_in_t, wg_s, b_gate, sinks, norm_w, w_out_s, ln_g, ln_b, target):
    qa, k_pad, v_pad, ga, qb, kb, vb, gb, rb, la, oms, *rope, w_in, wg, w_out = _in_proj(
        x, w_in_t, wg_s, b_gate, _rope_angles(positions), w_out_s)
    attn, cat_a = _swa_fwd(sinks, qa, k_pad, v_pad, ga)
    o, cat_b, sprev = _gla_fwd(qb, kb, vb, la, gb, norm_w)
    loss, gx0, d_cat_a, d_cat_b, g_w_out, g_ln = _out_ln_loss(cat_a, cat_b, w_out, x, target, ln_g, ln_b)
    parts_w_out = g_w_out.reshape(N_DEV, D_OUT_SHARD, D_MODEL)
    dqa, dga, dk_pad, dv_pad, g_sinks, g_out = _swa_bwd(sinks, qa, k_pad, v_pad, attn, ga, d_cat_a, rope, parts_w_out)
    dqb, dkb, dvb, dgb, drb, g_wg, g_bg, g_nw = _gla_bwd(qb, kb, vb, la, oms, gb, o, sprev, d_cat_b, rb, wg, norm_w)
    pieces = (dqa, dk_pad, dv_pad, dga, dqb, dkb, dvb, dgb, drb)
    grad_x, dproj = _in_proj_bwd_x(gx0, pieces, w_in, rope)
    g_w_in = _in_proj_bwd_w(x, dproj)
    return loss, grad_x, g_w_in, g_wg, g_bg, g_sinks, g_nw, g_out, g_ln


def _mesh_pos():
    return lax.axis_index("x"), lax.axis_index("y"), lax.axis_index("c")


def _peer(k, x, y, c):
    px = (1 - x) if k & 4 else x
    py = (1 - y) if k & 2 else y
    pc = (1 - c) if k & 1 else c
    return (px, py, pc), 4 * px + 2 * py + pc


def _other_chips(x, y):
    return [(1 - x, y), (x, 1 - y), (1 - x, 1 - y)]


def _shard_view(t):
    return jnp.transpose(t, (2, 0, 1))


class _BlockGather:
    def __init__(self, slots, send_sems, recv_sems):
        self.slots, self.send_sems, self.recv_sems = slots, send_sems, recv_sems
        x, y, c = _mesh_pos()
        self.xy, self.c, self.me, self.sibling = (x, y), c, 4 * x + 2 * y + c, (x, y, 1 - c)
        self.chips = _other_chips(x, y)

    @staticmethod
    def scratch():
        return [pltpu.SemaphoreType.DMA((N_DEV - 1,)), pltpu.SemaphoreType.DMA((N_DEV - 1,))]

    def _copy(self, k, block, to):
        return pltpu.make_async_remote_copy(
            src_ref=self.slots.at[block], dst_ref=self.slots.at[block], send_sem=self.send_sems.at[k],
            recv_sem=self.recv_sems.at[k], device_id=to, device_id_type=pl.DeviceIdType.MESH)

    def start(self):
        for j, (cx, cy) in enumerate(self.chips):
            self._copy(1 + j, self.me, (cx, cy, self.c)).start()
        self._copy(0, self.me, self.sibling).start()

    def forward(self):
        for j, (cx, cy) in enumerate(self.chips):
            block = 4 * cx + 2 * cy + self.c
            self._copy(1 + j, block, self.sibling).wait_recv()
            self._copy(4 + j, block, self.sibling).start()

    def finish(self):
        x, y = self.xy
        self._copy(0, 4 * x + 2 * y + (1 - self.c), self.sibling).wait_recv()
        for j, (cx, cy) in enumerate(self.chips):
            self._copy(4 + j, 4 * cx + 2 * cy + (1 - self.c), self.sibling).wait_recv()
        for k in range(N_DEV - 1):
            self._copy(k, self.me, self.sibling).wait_send()


class _OwnerSum:
    def __init__(self, parts, own, sib, snd, rcv, loc_sems, d2d_send, d2d_recv, ici_send, ici_recv):
        self.parts, self.own, self.sib, self.snd, self.rcv = parts, own, sib, snd, rcv
        self.sems = (loc_sems, d2d_send, d2d_recv, ici_send, ici_recv)
        x, y, c = _mesh_pos()
        self.c, self.sibling = c, (x, y, 1 - c)
        self.chips = [(x, y)] + _other_chips(x, y)

    @staticmethod
    def scratch(block):
        return [pltpu.VMEM((4,) + block, F32), pltpu.VMEM((4,) + block, F32),
                pltpu.VMEM((3,) + block, MXU_DTYPE), pltpu.VMEM((3,) + block, MXU_DTYPE),
                pltpu.SemaphoreType.DMA((4,)), pltpu.SemaphoreType.DMA((4,)), pltpu.SemaphoreType.DMA((4,)),
                pltpu.SemaphoreType.DMA((3,)), pltpu.SemaphoreType.DMA((3,))]

    def _local(self, r):
        cx, cy = self.chips[r]
        return pltpu.make_async_copy(self.parts.at[4 * cx + 2 * cy + self.c], self.own.at[r], self.sems[0].at[r])

    def _d2d(self, r):
        cx, cy = self.chips[r]
        return pltpu.make_async_remote_copy(
            src_ref=self.parts.at[4 * cx + 2 * cy + (1 - self.c)], dst_ref=self.sib.at[r], send_sem=self.sems[1].at[r],
            recv_sem=self.sems[2].at[r], device_id=self.sibling, device_id_type=pl.DeviceIdType.MESH)

    def _ici(self, r):
        cx, cy = self.chips[r]
        return pltpu.make_async_remote_copy(
            src_ref=self.snd.at[r - 1], dst_ref=self.rcv.at[r - 1], send_sem=self.sems[3].at[r - 1],
            recv_sem=self.sems[4].at[r - 1], device_id=(cx, cy, self.c), device_id_type=pl.DeviceIdType.MESH)

    def start(self):
        for r in (1, 2, 3, 0):
            self._local(r).start()
            self._d2d(r).start()

    def forward(self):
        for r in (1, 2, 3):
            self._local(r).wait()
            self._d2d(r).wait_recv()
            self.snd[r - 1] = (self.own[r] + self.sib[r]).astype(self.snd.dtype)
            self._ici(r).start()

    def finish(self):
        self._local(0).wait()
        self._d2d(0).wait_recv()
        acc = self.own[0] + self.sib[0]
        for r in (1, 2, 3):
            self._ici(r).wait_recv()
            acc = acc + self.rcv[r - 1].astype(F32)
        for r in range(4):
            self._d2d(r).wait_send()
        for r in (1, 2, 3):
            self._ici(r).wait_send()
        return acc


SMALL_ROWS = 48


def _reduce_grads(parts_w_in, parts_wg, g_ln, g_bg, g_nw, g_sinks, loss):
    def body(pin_ref, pwg_ref, gln_ref, gbg_ref, gnw_ref, gsk_ref, loss_ref, gin_ref, rwg_ref, rsm_ref, sm_ref,
             *scratch):
        sm_send, sm_recv, sm_loc = scratch[-3:]
        x, y, c = _mesh_pos()
        me = 4 * x + 2 * y + c
        owner_sum = _OwnerSum(pin_ref, *scratch[:-3])

        sm_ref[...] = jnp.zeros_like(sm_ref)
        for r in range(D_MODEL // 128):
            sm_ref[r:r + 1, :] = gln_ref[0:1, 128 * r:128 * (r + 1)]
            sm_ref[8 + r:9 + r, :] = gln_ref[1:2, 128 * r:128 * (r + 1)]
        for r in range(2):
            sm_ref[16 + r:17 + r, :] = gbg_ref[0:1, 128 * r:128 * (r + 1)]
        sm_ref[24:25, :] = gnw_ref[...]
        diag = lax.broadcasted_iota(jnp.int32, gsk_ref.shape, 0) == lax.broadcasted_iota(jnp.int32, gsk_ref.shape, 1)
        sm_ref[32:33, :] = jnp.sum(jnp.where(diag, gsk_ref[...], 0.0), axis=0, keepdims=True)
        sm_ref[40:41, :] = loss_ref[...]

        small_dsts = (rwg_ref, rsm_ref)

        def small_src(a, block):
            return pwg_ref.at[block] if a == 0 else sm_ref

        small_local = [pltpu.make_async_copy(small_src(a, me), small_dsts[a].at[me], sm_loc.at[a]) for a in range(2)]
        for cp in small_local:
            cp.start()
        small_sends = []
        for k in range(1, N_DEV):
            peer, pidx = _peer(k, x, y, c)
            for a in range(2):
                i = 2 * (k - 1) + a
                cp = pltpu.make_async_remote_copy(
                    src_ref=small_src(a, pidx), dst_ref=small_dsts[a].at[me], send_sem=sm_send.at[i],
                    recv_sem=sm_recv.at[i], device_id=peer, device_id_type=pl.DeviceIdType.MESH)
                cp.start()
                small_sends.append(cp)

        owner_sum.start()
        owner_sum.forward()
        gin_ref[...] = owner_sum.finish()

        for k in range(1, N_DEV):
            peer, pidx = _peer(k, x, y, c)
            for a in range(2):
                i = 2 * (k - 1) + a
                pltpu.make_async_remote_copy(
                    src_ref=small_src(a, me), dst_ref=small_dsts[a].at[pidx], send_sem=sm_send.at[i],
                    recv_sem=sm_recv.at[i], device_id=peer, device_id_type=pl.DeviceIdType.MESH).wait_recv()
        for cp in small_sends:
            cp.wait_send()
        for cp in small_local:
            cp.wait()

    hbm = pl.BlockSpec(memory_space=pl.ANY)
    vmem = pl.BlockSpec(memory_space=pltpu.VMEM)
    in_blk = parts_w_in.shape[1:]
    return pl.pallas_call(
        body, name="reduce_grads",
        in_specs=[hbm, hbm] + [vmem] * 5, out_specs=[vmem, hbm, hbm],
        out_shape=[jax.ShapeDtypeStruct(in_blk, F32),
                   jax.ShapeDtypeStruct((N_DEV,) + parts_wg.shape[1:], F32),
                   jax.ShapeDtypeStruct((N_DEV, SMALL_ROWS, 128), F32)],
        scratch_shapes=[pltpu.VMEM((SMALL_ROWS, 128), F32)] + _OwnerSum.scratch(in_blk)
        + [pltpu.SemaphoreType.DMA((2 * (N_DEV - 1),)), pltpu.SemaphoreType.DMA((2 * (N_DEV - 1),)),
           pltpu.SemaphoreType.DMA((2,))],
        compiler_params=_cparams(),
    )(parts_w_in, parts_wg, g_ln, g_bg, g_nw, g_sinks, loss)


def _adamw(recv, w, m, v, name):
    rows, width = w.shape
    tr = 128 if rows % 128 == 0 else rows
    n_parts = recv.shape[0]

    def body(r_ref, w_ref, m_ref, v_ref, g_ref, d_ref, nm_ref, nv_ref):
        g = r_ref[0]
        for j in range(1, n_parts):
            g = g + r_ref[j]
        nm = ADAM_B1 * m_ref[...] + (1.0 - ADAM_B1) * g
        nv = ADAM_B2 * v_ref[...] + (1.0 - ADAM_B2) * (g * g)
        m_hat = nm / (1.0 - ADAM_B1 ** ADAM_STEP)
        v_hat = nv / (1.0 - ADAM_B2 ** ADAM_STEP)
        g_ref[...] = g
        d_ref[...] = -ADAM_LR * (m_hat / (jnp.sqrt(v_hat) + ADAM_EPS) + ADAM_WD * w_ref[...])
        nm_ref[...] = nm
        nv_ref[...] = nv

    spec = _rows(tr, width)
    return pl.pallas_call(
        body, name=name, grid=(rows // tr,),
        in_specs=[pl.BlockSpec((n_parts, tr, width), lambda i: (0, i, 0)), spec, spec, spec],
        out_specs=[spec] * 4,
        out_shape=[jax.ShapeDtypeStruct((rows, width), F32)] * 4,
        compiler_params=_cparams(dimension_semantics=("arbitrary",)),
    )(recv, w, m, v)


def _adamw_shard_view(g, w, m, v):
    rows, width = g.shape

    def body(g_ref, w_hbm, m_hbm, v_hbm, g_out, d_out, nm_out, nv_out, bufs, outs, sems):
        loads = [pltpu.make_async_copy(src.at[:, 0, :], bufs.at[i], sems.at[i])
                 for i, src in enumerate((w_hbm, m_hbm, v_hbm))]
        for cp in loads:
            cp.start()
        g = g_ref[...]
        for cp in loads:
            cp.wait()
        nm = ADAM_B1 * bufs[1] + (1.0 - ADAM_B1) * g
        nv = ADAM_B2 * bufs[2] + (1.0 - ADAM_B2) * (g * g)
        m_hat = nm / (1.0 - ADAM_B1 ** ADAM_STEP)
        v_hat = nv / (1.0 - ADAM_B2 ** ADAM_STEP)
        outs[0] = g
        outs[1] = -ADAM_LR * (m_hat / (jnp.sqrt(v_hat) + ADAM_EPS) + ADAM_WD * bufs[0])
        outs[2] = nm
        outs[3] = nv
        stores = [pltpu.make_async_copy(outs.at[i], dst.at[:, 0, :], sems.at[3 + i])
                  for i, dst in enumerate((g_out, d_out, nm_out, nv_out))]
        for cp in stores:
            cp.start()
        for cp in stores:
            cp.wait()

    hbm = pl.BlockSpec(memory_space=pl.ANY)
    return pl.pallas_call(
        body, name="adamw_w_in",
        in_specs=[pl.BlockSpec(memory_space=pltpu.VMEM), hbm, hbm, hbm], out_specs=[hbm] * 4,
        out_shape=[jax.ShapeDtypeStruct((rows, 1, width), F32)] * 4,
        scratch_shapes=[pltpu.VMEM((3, rows, width), F32), pltpu.VMEM((4, rows, width), F32),
                        pltpu.SemaphoreType.DMA((7,))],
        compiler_params=_cparams(),
    )(g, w, m, v)


def _adamw_vectors(r_small, r_wg, params):
    n_par = len(params)

    def body(rsm_ref, rwg_ref, *refs):
        ins, outs = refs[:3 * n_par], refs[3 * n_par:]
        g = rsm_ref[0]
        gwg = rwg_ref[0]
        for j in range(1, N_DEV):
            g = g + rsm_ref[j]
            gwg = gwg + rwg_ref[j]
        outs[4 * n_par][...] = g[40:41]
        grads = [gwg,
                 jnp.concatenate([g[r:r + 1] for r in range(0, 8)], axis=1),
                 jnp.concatenate([g[r:r + 1] for r in range(8, 16)], axis=1),
                 jnp.concatenate([g[16:17], g[17:18]], axis=1),
                 g[24:25],
                 g[32:33, 0:SWA_Q_HEADS]]
        for p, gp in enumerate(grads):
            w_ref, m_ref, v_ref = ins[3 * p:3 * p + 3]
            nm = ADAM_B1 * m_ref[...] + (1.0 - ADAM_B1) * gp
            nv = ADAM_B2 * v_ref[...] + (1.0 - ADAM_B2) * (gp * gp)
            m_hat = nm / (1.0 - ADAM_B1 ** ADAM_STEP)
            v_hat = nv / (1.0 - ADAM_B2 ** ADAM_STEP)
            outs[4 * p][...] = gp
            outs[4 * p + 1][...] = -ADAM_LR * (m_hat / (jnp.sqrt(v_hat) + ADAM_EPS) + ADAM_WD * w_ref[...])
            outs[4 * p + 2][...] = nm
            outs[4 * p + 3][...] = nv

    vmem = pl.BlockSpec(memory_space=pltpu.VMEM)
    flat = [t for wmv in params for t in wmv]
    return pl.pallas_call(
        body, name="adamw_vectors",
        in_specs=[vmem] * (2 + len(flat)), out_specs=[vmem] * (4 * n_par + 1),
        out_shape=[jax.ShapeDtypeStruct(wmv[0].shape, F32) for wmv in params for _ in range(4)]
        + [jax.ShapeDtypeStruct((1, 128), F32)],
        compiler_params=_cparams(),
    )(r_small, r_wg, *flat)


def kernel(x, positions, w_in, gla_w_gate_up, gla_b_gate, attn_sinks, gla_norm_w, w_out, ln_g, ln_b, loss_target, m_w_in, m_gla_w_gate_up, m_gla_b_gate, m_attn_sinks, m_gla_norm_w, m_w_out, m_ln_g, m_ln_b, v_w_in, v_gla_w_gate_up, v_gla_b_gate, v_attn_sinks, v_gla_norm_w, v_w_out, v_ln_g, v_ln_b):
    loss, grad_x, parts_w_in, g_wg, g_bg, g_sinks, g_nw, g_out, g_ln = _local_step(
        x[0], positions[0], _shard_view(w_in), gla_w_gate_up[0], gla_b_gate, attn_sinks[0], gla_norm_w, w_out[0],
        ln_g, ln_b, loss_target[0])

    parts_wg = jnp.transpose(g_wg.reshape(GLA_RANK, N_DEV, 32), (1, 0, 2))
    g_in, r_wg, r_small = _reduce_grads(parts_w_in, parts_wg, g_ln, g_bg, g_nw, g_sinks, loss)

    upd_in = _adamw_shard_view(g_in, _shard_view(w_in), _shard_view(m_w_in), _shard_view(v_w_in))
    upd_in = [jnp.transpose(t, (1, 2, 0)) for t in upd_in]
    upd_out = _adamw(g_out[None], w_out[0], m_w_out[0], v_w_out[0], "adamw_w_out")
    vec = _adamw_vectors(r_small, r_wg, [
        (gla_w_gate_up[0], m_gla_w_gate_up[0], v_gla_w_gate_up[0]), (ln_g, m_ln_g, v_ln_g), (ln_b, m_ln_b, v_ln_b),
        (gla_b_gate, m_gla_b_gate, v_gla_b_gate), (gla_norm_w, m_gla_norm_w, v_gla_norm_w),
        (attn_sinks, m_attn_sinks, v_attn_sinks)])

    outs = [vec[24][0, 0], grad_x[None]]
    for kind in range(4):
        u_wg, u_ln_g, u_ln_b, u_bg, u_nw, u_sinks = (vec[4 * p + kind] for p in range(6))
        outs += [upd_in[kind], u_wg[None], u_bg, u_sinks, u_nw, upd_out[kind][None], u_ln_g, u_ln_b]
    return tuple(outs)
```

```python
import jax
import jax.numpy as jnp
from jax import lax
from jax.experimental import pallas as pl
from jax.experimental.pallas import tpu as pltpu

F32 = jnp.float32
MXU_DTYPE = jnp.bfloat16

N_DEV = 8
D_MODEL = 1024
SWA_Q_HEADS = 8
SWA_KV_HEADS = 2
SWA_GROUP = 4
SWA_HEAD_DIM = 64
BLOCK = 128
ROPE_THETA = 500000.0
ROT_DIM = 16
GLA_HEADS = 4
GLA_DK = 64
GLA_DV = 128
GLA_RANK = 16
GLA_TAU = 16.0
GLA_CHUNK = 64
D_IN_PROJ = 2832
D_IN_SHARD = D_IN_PROJ // N_DEV
D_OUT_SHARD = D_MODEL // N_DEV
OFF = (0, 512, 640, 768, 1280, 1536, 1792, 2304, 2816, 2832)
EPS = 1e-5
ALPHA = 2.0 ** 0.25
SWA_SCALE = SWA_HEAD_DIM ** -0.5
GLA_SCALE = GLA_DK ** -0.5
ADAM_LR = 0.001
ADAM_B1 = 0.9
ADAM_B2 = 0.999
ADAM_EPS = 1e-08
ADAM_WD = 0.01
ADAM_STEP = 10
VMEM_LIMIT = 56 * 1024 * 1024

_NT = (((1,), (1,)), ((), ()))
_TN = (((0,), (0,)), ((), ()))


def _mm(a, b):
    return jnp.dot(a, b, preferred_element_type=F32)


def _mm_nt(a, b):
    return lax.dot_general(a, b, _NT, preferred_element_type=F32)


def _mm_tn(a, b):
    return lax.dot_general(a, b, _TN, preferred_element_type=F32)


def _sigmoid(t):
    return 1.0 / (1.0 + jnp.exp(-t))


def _cparams(**kw):
    return pltpu.CompilerParams(vmem_limit_bytes=VMEM_LIMIT, **kw)


def _full(shape):
    return pl.BlockSpec(shape, lambda *_: (0,) * len(shape))


def _rows(tile, width):
    return pl.BlockSpec((tile, width), lambda i: (i, 0))


def _rope_angles(positions):
    half = ROT_DIM // 2
    inv_freq = ROPE_THETA ** (-jnp.arange(half, dtype=F32) / half)
    ang = positions.astype(F32)[:, None] * inv_freq[None, :]
    return jnp.concatenate([jnp.cos(ang), jnp.sin(ang)], axis=1)


def _split3_parts(t):
    hi = t.astype(MXU_DTYPE)
    r1 = t - hi.astype(F32)
    mid = r1.astype(MXU_DTYPE)
    return hi, mid, (r1 - mid.astype(F32)).astype(MXU_DTYPE)


def _rope_tables(cs):
    half = ROT_DIM // 2
    i = lax.broadcasted_iota(jnp.int32, (2 * half, 3 * 128), 0)
    lane = lax.broadcasted_iota(jnp.int32, (2 * half, 3 * 128), 1)
    table, pos = _idiv(lane, 128), lane & (SWA_HEAD_DIM - 1)
    is_c = (table == 0) & (pos < ROT_DIM) & ((pos & (half - 1)) == i)
    is_s1 = (table == 1) & (pos < half) & (pos + half == i)
    is_s2 = (table == 2) & (pos >= half) & (pos < ROT_DIM) & (pos == i)
    sel = jnp.where(is_c | is_s2, 1.0, jnp.where(is_s1, -1.0, 0.0)).astype(MXU_DTYPE)
    hi, mid, lo = _split3_parts(cs)
    t = (_mm(hi, sel) + _mm(mid, sel)) + _mm(lo, sel)
    pos1 = lax.broadcasted_iota(jnp.int32, (1, 128), 1) & (SWA_HEAD_DIM - 1)
    return t[:, 0:128] + jnp.where(pos1 >= ROT_DIM, 1.0, 0.0), t[:, 128:256], t[:, 256:384]


def _rope(t, c, s1, s2):
    return t * c + pltpu.roll(t, 120, 1) * s1 + pltpu.roll(t, 8, 1) * s2


def _rope_t(g, c, s1, s2):
    return g * c + pltpu.roll(g * s1, 8, 1) + pltpu.roll(g * s2, 120, 1)


def _in_proj(x, w_in_t, wg_s, b_gate, cos_sin, w_out_s):
    s = x.shape[0]
    ts = min(512, s)
    nsteps = s // ts
    forward_step = min(3, nsteps - 1)
    widths = [OFF[i + 1] - OFF[i] for i in range(9)]

    def body(x_ref, win_hbm, wgs_ref, bg_ref, cs_ref, wos_ref,
             qa_ref, ka_ref, va_ref, ga_ref, qb_ref, kb_ref, vb_ref, gb_ref, rb_ref, la_ref, oms_ref,
             c_ref, s1_ref, s2_ref, w_ref, wg_ref, wout_ref,
             win_all, wg_all, wout_all, stage, stage_sem, *sems):
        xb = x_ref[...].astype(MXU_DTYPE)
        c, s1, s2 = _rope_tables(cs_ref[...])
        c_ref[...], s1_ref[...], s2_ref[...] = c, s1, s2
        i0 = pl.program_id(0)
        gather = _BlockGather(wout_all, *sems[0:2])

        @pl.when(i0 == 0)
        def _():
            ka_ref[0:BLOCK, :] = jnp.zeros((BLOCK, 128), ka_ref.dtype)
            va_ref[0:BLOCK, :] = jnp.zeros((BLOCK, 128), va_ref.dtype)
            first = (_BlockGather(win_all, *sems[2:4]), _BlockGather(wg_all, *sems[4:6]))
            load = pltpu.make_async_copy(win_hbm.at[:, 0, :], stage, stage_sem)
            load.start()
            wout_all[gather.me] = wos_ref[...].astype(wout_all.dtype)
            wg_all[gather.me] = wgs_ref[...].astype(wg_all.dtype)
            load.wait()
            win_all[gather.me] = stage[...].astype(win_all.dtype)
            for stage_of in ("start", "forward", "finish"):
                for g in first:
                    getattr(g, stage_of)()
            gather.start()
            for j in range(N_DEV):
                w_ref[D_IN_SHARD * j:D_IN_SHARD * (j + 1), :] = win_all[j]
                wg_ref[:, 32 * j:32 * (j + 1)] = wg_all[j]

        @pl.when(i0 == forward_step)
        def _():
            gather.forward()

        @pl.when(i0 == nsteps - 1)
        def _():
            gather.finish()
            for j in range(N_DEV):
                wout_ref[D_OUT_SHARD * j:D_OUT_SHARD * (j + 1), :] = wout_all[j]

        kv_rows = pl.ds(pl.multiple_of(BLOCK + i0 * ts, BLOCK), ts)

        def cols(i):
            return _mm_nt(xb, w_ref[OFF[i]:OFF[i + 1], :])

        qa = cols(0)
        for i in range(4):
            qa_ref[:, 128 * i:128 * (i + 1)] = _rope(qa[:, 128 * i:128 * (i + 1)], c, s1, s2).astype(qa_ref.dtype)
        kv = _mm_nt(xb, w_ref[OFF[1]:OFF[3], :])
        ka_ref[kv_rows, :] = _rope(kv[:, 0:128], c, s1, s2).astype(ka_ref.dtype)
        va_ref[kv_rows, :] = kv[:, 128:256].astype(va_ref.dtype)
        ga_ref[...] = cols(3)
        qb_ref[...] = cols(4)
        kb_ref[...] = cols(5)
        vb_ref[...] = cols(6).astype(vb_ref.dtype)
        gb_ref[...] = cols(7)
        rb = cols(8)
        rb_ref[...] = rb
        logit = _mm(rb.astype(MXU_DTYPE), wg_ref[...]) + bg_ref[...]
        e = jnp.exp(-jnp.abs(logit))
        la_ref[...] = (jnp.minimum(logit, 0.0) - jnp.log(1.0 + e)) / GLA_TAU
        oms_ref[...] = jnp.where(logit >= 0.0, e, 1.0) / (1.0 + e)

    out_shape = [jax.ShapeDtypeStruct((s + BLOCK if i in (1, 2) else s, w), MXU_DTYPE if i in (0, 1, 2, 6) else F32)
                 for i, w in enumerate(widths)]
    out_shape += [jax.ShapeDtypeStruct((s, 256), F32)] * 2 + [jax.ShapeDtypeStruct((s, 128), F32)] * 3
    out_shape += [jax.ShapeDtypeStruct((D_IN_PROJ, D_MODEL), MXU_DTYPE), jax.ShapeDtypeStruct((GLA_RANK, 256), MXU_DTYPE),
                  jax.ShapeDtypeStruct((D_MODEL, D_MODEL), MXU_DTYPE)]
    return pl.pallas_call(
        body, name="in_proj", grid=(nsteps,),
        in_specs=[_rows(ts, D_MODEL), pl.BlockSpec(memory_space=pl.ANY), _full((GLA_RANK, 32)), _full((1, 256)),
                  _rows(ts, ROT_DIM), _full((D_OUT_SHARD, D_MODEL))],
        out_specs=[_full((s + BLOCK, w)) if i in (1, 2) else _rows(ts, w) for i, w in enumerate(widths)]
        + [_rows(ts, 256)] * 2 + [_rows(ts, 128)] * 3
        + [_full((D_IN_PROJ, D_MODEL)), _full((GLA_RANK, 256)), _full((D_MODEL, D_MODEL))],
        out_shape=out_shape,
        scratch_shapes=[pltpu.VMEM((N_DEV, D_IN_SHARD, D_MODEL), MXU_DTYPE), pltpu.VMEM((N_DEV, GLA_RANK, 32), MXU_DTYPE),
                        pltpu.VMEM((N_DEV, D_OUT_SHARD, D_MODEL), MXU_DTYPE),
                        pltpu.VMEM((D_IN_SHARD, D_MODEL), F32), pltpu.SemaphoreType.DMA]
        + 3 * _BlockGather.scratch(),
        compiler_params=_cparams(dimension_semantics=("arbitrary",)),
    )(x, w_in_t, wg_s, b_gate, cos_sin, w_out_s)


SWA_ROWS = SWA_GROUP * BLOCK


def _swa_bias():
    shape = (2, 2 * BLOCK, SWA_ROWS)
    ki = lax.broadcasted_iota(jnp.int32, shape, 1)
    qi = lax.broadcasted_iota(jnp.int32, shape, 2) & (BLOCK - 1)
    first = lax.broadcasted_iota(jnp.int32, shape, 0) == 0
    dist = qi + BLOCK - ki
    ok = (dist >= 0) & (dist < BLOCK) & (jnp.logical_not(first) | (ki >= BLOCK))
    return jnp.where(ok, 0.0, -jnp.inf).astype(F32)


SWA_SUB = 8


def _swa_bias_of(bias_ref, n, b):
    return bias_ref[jnp.minimum(n, 1)] if b == 0 else bias_ref[1]


def _swa_dup(t, j):
    t = t.astype(F32)
    low = lax.broadcasted_iota(jnp.int32, t.shape, 1) < SWA_HEAD_DIM
    keep = low if j == 0 else jnp.logical_not(low)
    return jnp.where(keep, t, pltpu.roll(t, SWA_HEAD_DIM, 1)).astype(MXU_DTYPE)


def _swa_stack(t, j):
    low = lax.broadcasted_iota(jnp.int32, (BLOCK, 128), 1) < SWA_HEAD_DIM
    zero = jnp.zeros((BLOCK, 128), t.dtype)
    blocks = []
    for p in (2 * j, 2 * j + 1):
        tp = t[:, 128 * p:128 * (p + 1)]
        blocks += [jnp.where(low, tp, zero), jnp.where(low, zero, tp)]
    return jnp.concatenate(blocks, axis=0)


def _swa_unstack(t):
    low = lax.broadcasted_iota(jnp.int32, (BLOCK, 128), 1) < SWA_HEAD_DIM
    return [jnp.where(low, t[2 * BLOCK * i:2 * BLOCK * i + BLOCK], t[2 * BLOCK * i + BLOCK:2 * BLOCK * (i + 1)])
            for i in range(2)]


def _swa_sink_row(sink_ref, j):
    lane = lax.broadcasted_iota(jnp.int32, (1, SWA_ROWS), 1)
    row = jnp.full((1, SWA_ROWS), sink_ref[SWA_GROUP * j], F32)
    for r in range(1, SWA_GROUP):
        row = jnp.where(lane >= BLOCK * r, sink_ref[SWA_GROUP * j + r], row)
    return row


def _split3(t):
    return jnp.concatenate(_split3_parts(t), axis=1)


def _row_sums_as_row(t):
    ones = jnp.ones((8, 3 * t.shape[1]), MXU_DTYPE)
    return _mm_nt(ones, _split3(t))[0:1, :]


def _swa_probs_t(qs, kd, bias_t, sink):
    sc = _mm_nt(kd, qs) + bias_t
    m = jnp.maximum(jnp.max(sc, axis=0, keepdims=True), sink)
    p = jnp.exp(sc - m)
    ps = jnp.exp(sink - m)
    rinv = 1.0 / (jnp.sum(p, axis=0, keepdims=True) + ps)
    return p * rinv, ps * rinv


def _swa_fwd(sinks, qa, k_pad, v_pad, ga):
    s = qa.shape[0]
    sub = min(SWA_SUB, s // BLOCK)
    tq = sub * BLOCK

    def body(sink_ref, qa_ref, ga_ref, bias_ref, k_ref, v_ref, attn_ref, cat_ref):
        n = pl.program_id(0)
        for b in range(sub):
            rows = slice(BLOCK * b, BLOCK * (b + 1))
            start = pl.multiple_of((n * sub + b) * BLOCK, BLOCK)
            kw = k_ref[pl.ds(start, 2 * BLOCK), :]
            vw = v_ref[pl.ds(start, 2 * BLOCK), :]
            bias_t = _swa_bias_of(bias_ref, n, b)
            q = qa_ref[rows, :] * SWA_SCALE
            g = ga_ref[rows, :]
            silu = g * _sigmoid(g)
            for j in range(SWA_KV_HEADS):
                qs = _swa_stack(q, j).astype(MXU_DTYPE)
                probs, _ = _swa_probs_t(qs, _swa_dup(kw, j), bias_t, _swa_sink_row(sink_ref, j))
                pairs = _swa_unstack(_mm_tn(probs.astype(MXU_DTYPE), _swa_dup(vw, j)))
                for i in range(2):
                    lanes = slice(128 * (2 * j + i), 128 * (2 * j + i + 1))
                    attn_ref[rows, lanes] = pairs[i]
                    cat_ref[rows, lanes] = (pairs[i] * silu[:, lanes]).astype(cat_ref.dtype)

    return pl.pallas_call(
        body, name="swa_fwd", grid=(s // tq,),
        in_specs=[pl.BlockSpec(memory_space=pltpu.SMEM), _rows(tq, 512), _rows(tq, 512),
                  _full((2, 2 * BLOCK, SWA_ROWS)), _full((s + BLOCK, 128)), _full((s + BLOCK, 128))],
        out_specs=[_rows(tq, 512), _rows(tq, 512)],
        out_shape=[jax.ShapeDtypeStruct((s, 512), F32), jax.ShapeDtypeStruct((s, 512), MXU_DTYPE)],
        compiler_params=_cparams(dimension_semantics=("arbitrary",)),
    )(sinks, qa, ga, _swa_bias(), k_pad, v_pad)


GLA_KW = GLA_HEADS * GLA_DK
GLA_VW = GLA_HEADS * GLA_DV


def _idiv(t, d):
    return t >> (d.bit_length() - 1)


def _chunk_cumsum(t, lower):
    n, w = t.shape
    r = lax.broadcasted_iota(jnp.int32, (n, n), 0)
    c = lax.broadcasted_iota(jnp.int32, (n, n), 1)
    tri = ((_idiv(r, GLA_CHUNK) == _idiv(c, GLA_CHUNK)) & ((r >= c) if lower else (r <= c))).astype(MXU_DTYPE)
    parts = _mm(tri, _split3(t))
    return (parts[:, :w] + parts[:, w:2 * w]) + parts[:, 2 * w:]


def _chunk_last(t):
    n = t.shape[0]
    return jnp.concatenate(
        [jnp.broadcast_to(t[c + GLA_CHUNK - 1:c + GLA_CHUNK, :], (GLA_CHUNK, t.shape[1]))
         for c in range(0, n, GLA_CHUNK)], axis=0)


def _head_stack(t, width):
    head = _idiv(lax.broadcasted_iota(jnp.int32, t.shape, 1), width)
    zero = jnp.zeros_like(t)
    return jnp.concatenate([jnp.where(head == h, t, zero) for h in range(GLA_HEADS)], axis=0)


def _heads_to_rows(t):
    return jnp.concatenate([t[:, GLA_DV * h:GLA_DV * (h + 1)] for h in range(GLA_HEADS)], axis=0)


def _rows_to_heads(t):
    return jnp.concatenate([t[GLA_CHUNK * h:GLA_CHUNK * (h + 1)] for h in range(GLA_HEADS)], axis=1)


def _state_by_head(t):
    srow = _idiv(lax.broadcasted_iota(jnp.int32, (GLA_VW, GLA_KW), 0), GLA_DV)
    slane = _idiv(lax.broadcasted_iota(jnp.int32, (GLA_VW, GLA_KW), 1), GLA_DK)
    return jnp.where(srow == slane, jnp.concatenate([t] * GLA_HEADS, axis=0), jnp.zeros((GLA_VW, GLA_KW), t.dtype))


def _gla_masks():
    row = lax.broadcasted_iota(jnp.int32, (GLA_CHUNK, GLA_KW), 0)
    pos = lax.broadcasted_iota(jnp.int32, (GLA_CHUNK, GLA_KW), 1) & (GLA_CHUNK - 1)
    return pos <= row, pos >= row


def _gla_fwd(qb, kb, vb, la, gb, norm_w):
    s = qb.shape[0]
    tb = min(256, s)
    ch = tb // GLA_CHUNK

    def body(qb_ref, kb_ref, vb_ref, la_ref, gb_ref, nw_ref, o_ref, cat_ref, sp_ref, st_ref):
        @pl.when(pl.program_id(0) == 0)
        def _():
            st_ref[...] = jnp.zeros_like(st_ref)

        causal, _ = _gla_masks()
        nw = nw_ref[...]
        b = _chunk_cumsum(la_ref[...], True)
        bl = _chunk_last(b)
        k = kb_ref[...]
        qd = ((qb_ref[...] * GLA_SCALE) * jnp.exp(b)).astype(MXU_DTYPE)
        ki = (k * jnp.exp(-b)).astype(MXU_DTYPE)
        ke = (k * jnp.exp(bl - b)).astype(MXU_DTYPE)
        dec = jnp.exp(bl)
        v = vb_ref[...].astype(MXU_DTYPE)
        g = gb_ref[...]
        silu = g * _sigmoid(g)
        for ci in range(ch):
            rows = slice(GLA_CHUNK * ci, GLA_CHUNK * (ci + 1))
            qds, kis, kes = (_head_stack(t[rows], GLA_DK) for t in (qd, ki, ke))
            a = jnp.where(causal, _mm_nt(qd[rows], kis), 0.0).astype(MXU_DTYPE)
            st = st_ref[...]
            sp_ref[ci] = st
            o = _mm(a, _head_stack(v[rows], GLA_DV)) + _rows_to_heads(_mm_nt(qds, st.astype(MXU_DTYPE)))
            st_ref[...] = st * dec[rows][0:1] + _mm_tn(_heads_to_rows(v[rows]), kes)
            o_ref[rows, :] = o
            for h in range(GLA_HEADS):
                lv = slice(GLA_DV * h, GLA_DV * (h + 1))
                oh = o[:, lv]
                r = lax.rsqrt(jnp.mean(oh * oh, axis=1, keepdims=True) + EPS)
                cat_ref[rows, lv] = (oh * r * nw * silu[rows, lv]).astype(cat_ref.dtype)

    return pl.pallas_call(
        body, name="gla_fwd", grid=(s // tb,),
        in_specs=[_rows(tb, 256), _rows(tb, 256), _rows(tb, 512), _rows(tb, 256), _rows(tb, 512), _full((1, 128))],
        out_specs=[_rows(tb, 512), _rows(tb, 512), pl.BlockSpec((ch, GLA_DV, 256), lambda i: (i, 0, 0))],
        out_shape=[jax.ShapeDtypeStruct((s, 512), F32), jax.ShapeDtypeStruct((s, 512), MXU_DTYPE),
                   jax.ShapeDtypeStruct((s // GLA_CHUNK, GLA_DV, 256), F32)],
        scratch_shapes=[pltpu.VMEM((GLA_DV, GLA_KW), F32)],
        compiler_params=_cparams(dimension_semantics=("arbitrary",)),
    )(qb, kb, vb, la, gb, norm_w)


def _out_ln_loss(cat_a, cat_b, w_out, x, target, ln_g, ln_b):
    s = x.shape[0]
    ts = min(512, s)
    halves = 2 if ts % 32 == 0 else 1
    th = ts // halves

    def body(ca_ref, cb_ref, w_ref, x_ref, t_ref, g_ref, b_ref,
             loss_ref, gx_ref, da_ref, db_ref, gw_ref, gln_ref):
        @pl.when(pl.program_id(0) == 0)
        def _():
            loss_ref[...] = jnp.zeros_like(loss_ref)
            gw_ref[...] = jnp.zeros_like(gw_ref)
            gln_ref[...] = jnp.zeros_like(gln_ref)

        g = g_ref[...]
        dh16s = []
        for k in range(halves):
            rows = slice(th * k, th * (k + 1))
            mix = _mm(ca_ref[rows, :], w_ref[0:512, :]) + _mm(cb_ref[rows, :], w_ref[512:1024, :])
            h = ALPHA * x_ref[rows, :] + mix
            mu = jnp.mean(h, axis=1, keepdims=True)
            hc = h - mu
            rstd = lax.rsqrt(jnp.mean(hc * hc, axis=1, keepdims=True) + EPS)
            xhat = hc * rstd
            err = xhat * g + b_ref[...] - t_ref[rows, :]
            loss_ref[...] += 0.5 * jnp.sum(jnp.mean(err * err, axis=1, keepdims=True))
            dy = err * (1.0 / D_MODEL)
            gln_ref[0:1, :] += jnp.sum(dy * xhat, axis=0, keepdims=True)
            gln_ref[1:2, :] += jnp.sum(dy, axis=0, keepdims=True)
            dxh = dy * g
            dh = rstd * (dxh - jnp.mean(dxh, axis=1, keepdims=True)
                         - xhat * jnp.mean(dxh * xhat, axis=1, keepdims=True))
            gx_ref[rows, :] = ALPHA * dh
            dh16s.append(dh.astype(MXU_DTYPE))
        for k in range(halves):
            rows = slice(th * k, th * (k + 1))
            da_ref[rows, :] = _mm_nt(dh16s[k], w_ref[0:512, :])
            db_ref[rows, :] = _mm_nt(dh16s[k], w_ref[512:1024, :])
        dh16 = jnp.concatenate(dh16s, axis=0)
        gw_ref[0:512, :] += _mm_tn(ca_ref[...], dh16)
        gw_ref[512:1024, :] += _mm_tn(cb_ref[...], dh16)

    return pl.pallas_call(
        body, name="out_ln_loss", grid=(s // ts,),
        in_specs=[_rows(ts, 512), _rows(ts, 512), _full((D_MODEL, D_MODEL)), _rows(ts, D_MODEL), _rows(ts, D_MODEL),
                  _full((1, D_MODEL)), _full((1, D_MODEL))],
        out_specs=[_full((1, 128)), _rows(ts, D_MODEL), _rows(ts, 512), _rows(ts, 512),
                   _full((D_MODEL, D_MODEL)), _full((2, D_MODEL))],
        out_shape=[jax.ShapeDtypeStruct((1, 128), F32), jax.ShapeDtypeStruct((s, D_MODEL), F32),
                   jax.ShapeDtypeStruct((s, 512), F32), jax.ShapeDtypeStruct((s, 512), F32),
                   jax.ShapeDtypeStruct((D_MODEL, D_MODEL), F32), jax.ShapeDtypeStruct((2, D_MODEL), F32)],
        compiler_params=_cparams(dimension_semantics=("arbitrary",)),
    )(cat_a, cat_b, w_out, x, target, ln_g, ln_b)


def _swa_bwd(sinks, qa, k_pad, v_pad, attn, ga, d_cat_a, rope, parts_w_out):
    s = qa.shape[0]
    sub = min(SWA_SUB, s // BLOCK)
    tq = sub * BLOCK
    nsteps = s // tq
    forward_step = min(1, nsteps - 1)

    def body(sink_ref, qa_ref, ga_ref, at_ref, dc_ref, c_ref, s1_ref, s2_ref, bias_ref, k_ref, v_ref, pout_ref,
             dq_ref, dg_ref, dk_ref, dv_ref, ds_ref, gout_ref, *scratch):
        n = pl.program_id(0)
        owner_sum = _OwnerSum(pout_ref, *scratch)

        @pl.when(n == 0)
        def _():
            dk_ref[...] = jnp.zeros_like(dk_ref)
            dv_ref[...] = jnp.zeros_like(dv_ref)
            ds_ref[...] = jnp.zeros_like(ds_ref)
            owner_sum.start()

        @pl.when(n == forward_step)
        def _():
            owner_sum.forward()

        @pl.when(n == nsteps - 1)
        def _():
            gout_ref[...] = owner_sum.finish()

        low = lax.broadcasted_iota(jnp.int32, (2 * BLOCK, 128), 1) < SWA_HEAD_DIM
        for b in range(sub):
            rows = slice(BLOCK * b, BLOCK * (b + 1))
            start = pl.multiple_of((n * sub + b) * BLOCK, BLOCK)
            kw = k_ref[pl.ds(start, 2 * BLOCK), :]
            vw = v_ref[pl.ds(start, 2 * BLOCK), :]
            bias_t = _swa_bias_of(bias_ref, n, b)
            q = qa_ref[rows, :] * SWA_SCALE
            g = ga_ref[rows, :]
            sg = _sigmoid(g)
            o = at_ref[rows, :]
            dc = dc_ref[rows, :]
            do = dc * (g * sg)
            dg_ref[rows, :] = (dc * o * (sg * (1.0 + g * (1.0 - sg)))).astype(dg_ref.dtype)
            od = do * o
            c, s1, s2 = c_ref[rows, :], s1_ref[rows, :], s2_ref[rows, :]
            dk, dv = [], []
            for j in range(SWA_KV_HEADS):
                kd, vd = _swa_dup(kw, j), _swa_dup(vw, j)
                qs = _swa_stack(q, j).astype(MXU_DTYPE)
                dos = _swa_stack(do, j).astype(MXU_DTYPE)
                probs, psink = _swa_probs_t(qs, kd, bias_t, _swa_sink_row(sink_ref, j))
                delta = _row_sums_as_row(_swa_stack(od, j))
                dsc = (probs * (_mm_nt(vd, dos) - delta)).astype(MXU_DTYPE)
                dsink = psink * delta
                for r in range(SWA_GROUP):
                    h = SWA_GROUP * j + r
                    ds_ref[h:h + 1, :] += jnp.zeros((1, 128), F32) - jnp.sum(dsink[:, BLOCK * r:BLOCK * (r + 1)])
                dq = _swa_unstack(_mm_tn(dsc, kd))
                for i in range(2):
                    lanes = slice(128 * (2 * j + i), 128 * (2 * j + i + 1))
                    dq_ref[rows, lanes] = _rope_t(dq[i] * SWA_SCALE, c, s1, s2).astype(dq_ref.dtype)
                dkj = _mm(dsc, qs)
                dvj = _mm(probs.astype(MXU_DTYPE), dos)
                dk.append(dkj + pltpu.roll(dkj, SWA_HEAD_DIM, 1))
                dv.append(dvj + pltpu.roll(dvj, SWA_HEAD_DIM, 1))
            dk_ref[pl.ds(start, 2 * BLOCK), :] += jnp.where(low, dk[0], dk[1])
            dv_ref[pl.ds(start, 2 * BLOCK), :] += jnp.where(low, dv[0], dv[1])

    out_blk = parts_w_out.shape[1:]
    return pl.pallas_call(
        body, name="swa_bwd", grid=(nsteps,),
        in_specs=[pl.BlockSpec(memory_space=pltpu.SMEM)] + [_rows(tq, 512)] * 4 + [_rows(tq, 128)] * 3
        + [_full((2, 2 * BLOCK, SWA_ROWS))] + [_full((s + BLOCK, 128))] * 2 + [pl.BlockSpec(memory_space=pl.ANY)],
        out_specs=[_rows(tq, 512), _rows(tq, 512), _full((s + BLOCK, 128)), _full((s + BLOCK, 128)),
                   _full((SWA_Q_HEADS, 128)), _full(out_blk)],
        out_shape=[jax.ShapeDtypeStruct((s, 512), MXU_DTYPE), jax.ShapeDtypeStruct((s, 512), MXU_DTYPE),
                   jax.ShapeDtypeStruct((s + BLOCK, 128), F32), jax.ShapeDtypeStruct((s + BLOCK, 128), F32),
                   jax.ShapeDtypeStruct((SWA_Q_HEADS, 128), F32), jax.ShapeDtypeStruct(out_blk, F32)],
        scratch_shapes=_OwnerSum.scratch(out_blk),
        compiler_params=_cparams(dimension_semantics=("arbitrary",)),
    )(sinks, qa, ga, attn, d_cat_a, *rope, _swa_bias(), k_pad, v_pad, parts_w_out)


def _gla_bwd(qb, kb, vb, la, oms, gb, o, sprev, d_cat_b, rb, wg, norm_w):
    s = qb.shape[0]
    tb = min(512, s)
    ch = tb // GLA_CHUNK
    nb = s // tb

    def body(qb_ref, kb_ref, vb_ref, la_ref, oms_ref, gb_ref, o_ref, sp_ref, dc_ref, rb_ref, wg_ref, nw_ref,
             dq_ref, dk_ref, dv_ref, dg_ref, dr_ref, gwg_ref, gbg_ref, gnw_ref, dst_ref):
        @pl.when(pl.program_id(0) == 0)
        def _():
            dst_ref[...] = jnp.zeros_like(dst_ref)
            gwg_ref[...] = jnp.zeros_like(gwg_ref)
            gbg_ref[...] = jnp.zeros_like(gbg_ref)
            gnw_ref[...] = jnp.zeros_like(gnw_ref)

        causal, causal_t = _gla_masks()
        nw = nw_ref[...]
        b = _chunk_cumsum(la_ref[...], True)
        bl = _chunk_last(b)
        eb, enb, ee, dec = jnp.exp(b), jnp.exp(-b), jnp.exp(bl - b), jnp.exp(bl)
        k = kb_ref[...]
        qd = (qb_ref[...] * GLA_SCALE) * eb
        ki = k * enb
        ke = k * ee
        qd16, ki16, ke16 = qd.astype(MXU_DTYPE), ki.astype(MXU_DTYPE), ke.astype(MXU_DTYPE)
        v16 = vb_ref[...].astype(MXU_DTYPE)

        g = gb_ref[...]
        sg = _sigmoid(g)
        silu = g * sg
        dsilu = sg * (1.0 + g * (1.0 - sg))
        gnw = jnp.zeros((1, GLA_DV), F32)
        do = []
        for h in range(GLA_HEADS):
            lv = slice(GLA_DV * h, GLA_DV * (h + 1))
            oh = o_ref[:, lv]
            dch = dc_ref[:, lv]
            r = lax.rsqrt(jnp.mean(oh * oh, axis=1, keepdims=True) + EPS)
            d_on = dch * silu[:, lv]
            dg_ref[:, lv] = (dch * (oh * r * nw) * dsilu[:, lv]).astype(dg_ref.dtype)
            gnw += jnp.sum(d_on * oh * r, axis=0, keepdims=True)
            u = d_on * nw
            do.append(r * u - oh * (r * r * r) * jnp.mean(u * oh, axis=1, keepdims=True))
        gnw_ref[...] += gnw
        do16 = jnp.concatenate(do, axis=1).astype(MXU_DTYPE)

        db, dbl = [None] * ch, [None] * ch
        for ci in reversed(range(ch)):
            rows = slice(GLA_CHUNK * ci, GLA_CHUNK * (ci + 1))
            qds, kis, kes = (_head_stack(t[rows], GLA_DK) for t in (qd16, ki16, ke16))
            vs, dos = _head_stack(v16[rows], GLA_DV), _head_stack(do16[rows], GLA_DV)
            a = jnp.where(causal, _mm_nt(qd16[rows], kis), 0.0).astype(MXU_DTYPE)
            at = jnp.where(causal_t, _mm_nt(ki16[rows], qds), 0.0).astype(MXU_DTYPE)
            da = jnp.where(causal, _mm_nt(do16[rows], vs), 0.0).astype(MXU_DTYPE)
            dat = jnp.where(causal_t, _mm_nt(v16[rows], dos), 0.0).astype(MXU_DTYPE)
            st = sp_ref[ci]
            dst = dst_ref[...]
            dst16 = dst.astype(MXU_DTYPE)
            dv = _mm(at, dos) + _rows_to_heads(_mm_nt(kes, dst16))
            dqd = _mm(da, kis) + _mm(do16[rows], _state_by_head(st.astype(MXU_DTYPE)))
            dki = _mm(dat, qds)
            dke = _mm(v16[rows], _state_by_head(dst16))
            ddec = jnp.sum(dst * st, axis=0, keepdims=True)
            decc = dec[rows][0:1]
            dst_ref[...] = _mm_tn(_heads_to_rows(do16[rows]), qds) + dst * decc
            dq_ref[rows, :] = (dqd * eb[rows] * GLA_SCALE).astype(dq_ref.dtype)
            dk_ref[rows, :] = (dki * enb[rows] + dke * ee[rows]).astype(dk_ref.dtype)
            dv_ref[rows, :] = dv.astype(dv_ref.dtype)
            dke_ke = dke * ke[rows]
            db[ci] = dqd * qd[rows] - dki * ki[rows] - dke_ke
            dbl[ci] = jnp.broadcast_to(jnp.sum(dke_ke, axis=0, keepdims=True) + ddec * decc, (GLA_CHUNK, GLA_KW))

        dla = _chunk_cumsum(jnp.concatenate(db, axis=0), False) + jnp.concatenate(dbl, axis=0)
        dlogit = dla * oms_ref[...] * (1.0 / GLA_TAU)
        dl16 = dlogit.astype(MXU_DTYPE)
        gbg_ref[...] += jnp.sum(dlogit, axis=0, keepdims=True)
        gwg_ref[...] += _mm_tn(rb_ref[...].astype(MXU_DTYPE), dl16)
        dr_ref[...] = _mm_nt(dl16, wg_ref[...]).astype(dr_ref.dtype)

    def rev(width):
        return pl.BlockSpec((tb, width), lambda i: (nb - 1 - i, 0))

    return pl.pallas_call(
        body, name="gla_bwd", grid=(nb,),
        in_specs=[rev(256), rev(256), rev(512), rev(256), rev(256), rev(512), rev(512),
                  pl.BlockSpec((ch, GLA_DV, 256), lambda i: (nb - 1 - i, 0, 0)), rev(512), rev(GLA_RANK),
                  _full((GLA_RANK, 256)), _full((1, 128))],
        out_specs=[rev(256), rev(256), rev(512), rev(512), rev(GLA_RANK),
                   _full((GLA_RANK, 256)), _full((1, 256)), _full((1, 128))],
        out_shape=[jax.ShapeDtypeStruct((s, 256), MXU_DTYPE), jax.ShapeDtypeStruct((s, 256), MXU_DTYPE),
                   jax.ShapeDtypeStruct((s, 512), MXU_DTYPE), jax.ShapeDtypeStruct((s, 512), MXU_DTYPE),
                   jax.ShapeDtypeStruct((s, GLA_RANK), MXU_DTYPE), jax.ShapeDtypeStruct((GLA_RANK, 256), F32),
                   jax.ShapeDtypeStruct((1, 256), F32), jax.ShapeDtypeStruct((1, 128), F32)],
        scratch_shapes=[pltpu.VMEM((GLA_DV, GLA_KW), F32)],
        compiler_params=_cparams(dimension_semantics=("arbitrary",)),
    )(qb, kb, vb, la, oms, gb, o, sprev, d_cat_b, rb, wg, norm_w)


def _in_proj_bwd_x(gx0, pieces, w_in, rope):
    s = gx0.shape[0]
    ts = min(512, s)
    widths = [OFF[i + 1] - OFF[i] for i in range(9)]

    def body(gx0_ref, *refs):
        piece_refs = refs[:9]
        w_ref, c_ref, s1_ref, s2_ref, gx_ref, dp_ref = refs[9:]
        kv_rows = pl.ds(pl.multiple_of(BLOCK + pl.program_id(0) * ts, BLOCK), ts)
        acc = gx0_ref[...]
        for i in (0, 1, 3, 4, 5, 6, 7, 8):
            lo, hi = OFF[i], OFF[i + 1]
            if i == 1:
                dk = _rope_t(piece_refs[1][kv_rows, :], c_ref[...], s1_ref[...], s2_ref[...])
                t16 = jnp.concatenate([dk, piece_refs[2][kv_rows, :]], axis=1).astype(MXU_DTYPE)
                hi = OFF[3]
            else:
                t16 = piece_refs[i][...].astype(MXU_DTYPE)
            dp_ref[:, lo:hi] = t16
            acc += _mm(t16, w_ref[lo:hi, :])
        gx_ref[...] = acc

    return pl.pallas_call(
        body, name="in_proj_bwd_x", grid=(s // ts,),
        in_specs=[_rows(ts, D_MODEL)]
        + [_full((s + BLOCK, w)) if i in (1, 2) else _rows(ts, w) for i, w in enumerate(widths)]
        + [_full((D_IN_PROJ, D_MODEL))] + [_rows(ts, 128)] * 3,
        out_specs=[_rows(ts, D_MODEL), _rows(ts, D_IN_PROJ)],
        out_shape=[jax.ShapeDtypeStruct((s, D_MODEL), F32), jax.ShapeDtypeStruct((s, D_IN_PROJ), MXU_DTYPE)],
        compiler_params=_cparams(dimension_semantics=("arbitrary",)),
    )(gx0, *pieces, w_in, *rope)


def _in_proj_bwd_w(x, dproj):
    s = x.shape[0]
    ts = min(1024, s)
    nsteps = s // ts
    col_chunks = [(OFF[i], OFF[i + 2] if i == 1 else OFF[i + 1]) for i in (0, 1, 3, 4, 5, 6, 7, 8)]

    def body(x_ref, dp_ref, gw_ref, acc_ref, stage_ref, sems):
        i = pl.program_id(0)

        @pl.when(i == 0)
        def _():
            acc_ref[...] = jnp.zeros_like(acc_ref)

        xb = x_ref[...].astype(MXU_DTYPE)
        for lo, hi in col_chunks:
            acc_ref[lo:hi, :] += _mm_tn(dp_ref[:, lo:hi], xb)

        @pl.when(i == nsteps - 1)
        def _():
            copies = []
            for j in range(N_DEV):
                slot = j % 2
                if j >= 2:
                    copies[j - 2].wait()
                stage_ref[slot] = acc_ref[D_IN_SHARD * j:D_IN_SHARD * (j + 1), :]
                cp = pltpu.make_async_copy(stage_ref.at[slot], gw_ref.at[j], sems.at[slot])
                cp.start()
                copies.append(cp)
            copies[N_DEV - 2].wait()
            copies[N_DEV - 1].wait()

    return pl.pallas_call(
        body, name="in_proj_bwd_w", grid=(nsteps,),
        in_specs=[_rows(ts, D_MODEL), _rows(ts, D_IN_PROJ)],
        out_specs=pl.BlockSpec(memory_space=pl.ANY),
        out_shape=jax.ShapeDtypeStruct((N_DEV, D_IN_SHARD, D_MODEL), F32),
        scratch_shapes=[pltpu.VMEM((D_IN_PROJ, D_MODEL), F32), pltpu.VMEM((2, D_IN_SHARD, D_MODEL), F32),
                        pltpu.SemaphoreType.DMA((2,))],
        compiler_params=_cparams(dimension_semantics=("arbitrary",)),
    )(x, dproj)


def _local_step(x, positions, w_in_t, wg_s, b_gate, sinks, norm_w, w_out_s, ln_g, ln_b, target):
    qa, k_pad, v_pad, ga, qb, kb, vb, gb, rb, la, oms, *rope, w_in, wg, w_out = _in_proj(
        x, w_in_t, wg_s, b_gate, _rope_angles(positions), w_out_s)
    attn, cat_a = _swa_fwd(sinks, qa, k_pad, v_pad, ga)
    o, cat_b, sprev = _gla_fwd(qb, kb, vb, la, gb, norm_w)
    loss, gx0, d_cat_a, d_cat_b, g_w_out, g_ln = _out_ln_loss(cat_a, cat_b, w_out, x, target, ln_g, ln_b)
    parts_w_out = g_w_out.reshape(N_DEV, D_OUT_SHARD, D_MODEL)
    dqa, dga, dk_pad, dv_pad, g_sinks, g_out = _swa_bwd(sinks, qa, k_pad, v_pad, attn, ga, d_cat_a, rope, parts_w_out)
    dqb, dkb, dvb, dgb, drb, g_wg, g_bg, g_nw = _gla_bwd(qb, kb, vb, la, oms, gb, o, sprev, d_cat_b, rb, wg, norm_w)
    pieces = (dqa, dk_pad, dv_pad, dga, dqb, dkb, dvb, dgb, drb)
    grad_x, dproj = _in_proj_bwd_x(gx0, pieces, w_in, rope)
    g_w_in = _in_proj_bwd_w(x, dproj)
    return loss, grad_x, g_w_in, g_wg, g_bg, g_sinks, g_nw, g_out, g_ln


def _mesh_pos():
    return lax.axis_index("x"), lax.axis_index("y"), lax.axis_index("c")


def _peer(k, x, y, c):
    px = (1 - x) if k & 4 else x
    py = (1 - y) if k & 2 else y
    pc = (1 - c) if k & 1 else c
    return (px, py, pc), 4 * px + 2 * py + pc


def _other_chips(x, y):
    return [(1 - x, y), (x, 1 - y), (1 - x, 1 - y)]


def _shard_view(t):
    return jnp.transpose(t, (2, 0, 1))


class _BlockGather:
    def __init__(self, slots, send_sems, recv_sems):
        self.slots, self.send_sems, self.recv_sems = slots, send_sems, recv_sems
        x, y, c = _mesh_pos()
        self.xy, self.c, self.me, self.sibling = (x, y), c, 4 * x + 2 * y + c, (x, y, 1 - c)
        self.chips = _other_chips(x, y)

    @staticmethod
    def scratch():
        return [pltpu.SemaphoreType.DMA((N_DEV - 1,)), pltpu.SemaphoreType.DMA((N_DEV - 1,))]

    def _copy(self, k, block, to):
        return pltpu.make_async_remote_copy(
            src_ref=self.slots.at[block], dst_ref=self.slots.at[block], send_sem=self.send_sems.at[k],
            recv_sem=self.recv_sems.at[k], device_id=to, device_id_type=pl.DeviceIdType.MESH)

    def start(self):
        for j, (cx, cy) in enumerate(self.chips):
            self._copy(1 + j, self.me, (cx, cy, self.c)).start()
        self._copy(0, self.me, self.sibling).start()

    def forward(self):
        for j, (cx, cy) in enumerate(self.chips):
            block = 4 * cx + 2 * cy + self.c
            self._copy(1 + j, block, self.sibling).wait_recv()
            self._copy(4 + j, block, self.sibling).start()

    def finish(self):
        x, y = self.xy
        self._copy(0, 4 * x + 2 * y + (1 - self.c), self.sibling).wait_recv()
        for j, (cx, cy) in enumerate(self.chips):
            self._copy(4 + j, 4 * cx + 2 * cy + (1 - self.c), self.sibling).wait_recv()
        for k in range(N_DEV - 1):
            self._copy(k, self.me, self.sibling).wait_send()


class _OwnerSum:
    def __init__(self, parts, own, sib, snd, rcv, loc_sems, d2d_send, d2d_recv, ici_send, ici_recv):
        self.parts, self.own, self.sib, self.snd, self.rcv = parts, own, sib, snd, rcv
        self.sems = (loc_sems, d2d_send, d2d_recv, ici_send, ici_recv)
        x, y, c = _mesh_pos()
        self.c, self.sibling = c, (x, y, 1 - c)
        self.chips = [(x, y)] + _other_chips(x, y)

    @staticmethod
    def scratch(block):
        return [pltpu.VMEM((4,) + block, F32), pltpu.VMEM((4,) + block, F32),
                pltpu.VMEM((3,) + block, MXU_DTYPE), pltpu.VMEM((3,) + block, MXU_DTYPE),
                pltpu.SemaphoreType.DMA((4,)), pltpu.SemaphoreType.DMA((4,)), pltpu.SemaphoreType.DMA((4,)),
                pltpu.SemaphoreType.DMA((3,)), pltpu.SemaphoreType.DMA((3,))]

    def _local(self, r):
        cx, cy = self.chips[r]
        return pltpu.make_async_copy(self.parts.at[4 * cx + 2 * cy + self.c], self.own.at[r], self.sems[0].at[r])

    def _d2d(self, r):
        cx, cy = self.chips[r]
        return pltpu.make_async_remote_copy(
            src_ref=self.parts.at[4 * cx + 2 * cy + (1 - self.c)], dst_ref=self.sib.at[r], send_sem=self.sems[1].at[r],
            recv_sem=self.sems[2].at[r], device_id=self.sibling, device_id_type=pl.DeviceIdType.MESH)

    def _ici(self, r):
        cx, cy = self.chips[r]
        return pltpu.make_async_remote_copy(
            src_ref=self.snd.at[r - 1], dst_ref=self.rcv.at[r - 1], send_sem=self.sems[3].at[r - 1],
            recv_sem=self.sems[4].at[r - 1], device_id=(cx, cy, self.c), device_id_type=pl.DeviceIdType.MESH)

    def start(self):
        for r in (1, 2, 3, 0):
            self._local(r).start()
            self._d2d(r).start()

    def forward(self):
        for r in (1, 2, 3):
            self._local(r).wait()
            self._d2d(r).wait_recv()
            self.snd[r - 1] = (self.own[r] + self.sib[r]).astype(self.snd.dtype)
            self._ici(r).start()

    def finish(self):
        self._local(0).wait()
        self._d2d(0).wait_recv()
        acc = self.own[0] + self.sib[0]
        for r in (1, 2, 3):
            self._ici(r).wait_recv()
            acc = acc + self.rcv[r - 1].astype(F32)
        for r in range(4):
            self._d2d(r).wait_send()
        for r in (1, 2, 3):
            self._ici(r).wait_send()
        return acc


SMALL_ROWS = 48


def _reduce_grads(parts_w_in, parts_wg, g_ln, g_bg, g_nw, g_sinks, loss):
    def body(pin_ref, pwg_ref, gln_ref, gbg_ref, gnw_ref, gsk_ref, loss_ref, gin_ref, rwg_ref, rsm_ref, sm_ref,
             *scratch):
        sm_send, sm_recv, sm_loc = scratch[-3:]
        x, y, c = _mesh_pos()
        me = 4 * x + 2 * y + c
        owner_sum = _OwnerSum(pin_ref, *scratch[:-3])

        sm_ref[...] = jnp.zeros_like(sm_ref)
        for r in range(D_MODEL // 128):
            sm_ref[r:r + 1, :] = gln_ref[0:1, 128 * r:128 * (r + 1)]
            sm_ref[8 + r:9 + r, :] = gln_ref[1:2, 128 * r:128 * (r + 1)]
        for r in range(2):
            sm_ref[16 + r:17 + r, :] = gbg_ref[0:1, 128 * r:128 * (r + 1)]
        sm_ref[24:25, :] = gnw_ref[...]
        diag = lax.broadcasted_iota(jnp.int32, gsk_ref.shape, 0) == lax.broadcasted_iota(jnp.int32, gsk_ref.shape, 1)
        sm_ref[32:33, :] = jnp.sum(jnp.where(diag, gsk_ref[...], 0.0), axis=0, keepdims=True)
        sm_ref[40:41, :] = loss_ref[...]

        small_dsts = (rwg_ref, rsm_ref)

        def small_src(a, block):
            return pwg_ref.at[block] if a == 0 else sm_ref

        small_local = [pltpu.make_async_copy(small_src(a, me), small_dsts[a].at[me], sm_loc.at[a]) for a in range(2)]
        for cp in small_local:
            cp.start()
        small_sends = []
        for k in range(1, N_DEV):
            peer, pidx = _peer(k, x, y, c)
            for a in range(2):
                i = 2 * (k - 1) + a
                cp = pltpu.make_async_remote_copy(
                    src_ref=small_src(a, pidx), dst_ref=small_dsts[a].at[me], send_sem=sm_send.at[i],
                    recv_sem=sm_recv.at[i], device_id=peer, device_id_type=pl.DeviceIdType.MESH)
                cp.start()
                small_sends.append(cp)

        owner_sum.start()
        owner_sum.forward()
        gin_ref[...] = owner_sum.finish()

        for k in range(1, N_DEV):
            peer, pidx = _peer(k, x, y, c)
            for a in range(2):
                i = 2 * (k - 1) + a
                pltpu.make_async_remote_copy(
                    src_ref=small_src(a, me), dst_ref=small_dsts[a].at[pidx], send_sem=sm_send.at[i],
                    recv_sem=sm_recv.at[i], device_id=peer, device_id_type=pl.DeviceIdType.MESH).wait_recv()
        for cp in small_sends:
            cp.wait_send()
        for cp in small_local:
            cp.wait()

    hbm = pl.BlockSpec(memory_space=pl.ANY)
    vmem = pl.BlockSpec(memory_space=pltpu.VMEM)
    in_blk = parts_w_in.shape[1:]
    return pl.pallas_call(
        body, name="reduce_grads",
        in_specs=[hbm, hbm] + [vmem] * 5, out_specs=[vmem, hbm, hbm],
        out_shape=[jax.ShapeDtypeStruct(in_blk, F32),
                   jax.ShapeDtypeStruct((N_DEV,) + parts_wg.shape[1:], F32),
                   jax.ShapeDtypeStruct((N_DEV, SMALL_ROWS, 128), F32)],
        scratch_shapes=[pltpu.VMEM((SMALL_ROWS, 128), F32)] + _OwnerSum.scratch(in_blk)
        + [pltpu.SemaphoreType.DMA((2 * (N_DEV - 1),)), pltpu.SemaphoreType.DMA((2 * (N_DEV - 1),)),
           pltpu.SemaphoreType.DMA((2,))],
        compiler_params=_cparams(),
    )(parts_w_in, parts_wg, g_ln, g_bg, g_nw, g_sinks, loss)


def _adamw_math(g, w, m, v):
    nm = ADAM_B1 * m + (1.0 - ADAM_B1) * g
    nv = ADAM_B2 * v + (1.0 - ADAM_B2) * (g * g)
    m_hat = nm / (1.0 - ADAM_B1 ** ADAM_STEP)
    v_hat = nv / (1.0 - ADAM_B2 ** ADAM_STEP)
    return -ADAM_LR * (m_hat / (jnp.sqrt(v_hat) + ADAM_EPS) + ADAM_WD * w), nm, nv


def _adamw_shard_view(g, w, m, v):
    rows, width = g.shape

    def body(g_ref, w_hbm, m_hbm, v_hbm, g_out, d_out, nm_out, nv_out, bufs, outs, sems):
        loads = [pltpu.make_async_copy(src.at[:, 0, :], bufs.at[i], sems.at[i])
                 for i, src in enumerate((w_hbm, m_hbm, v_hbm))]
        for cp in loads:
            cp.start()
        g = g_ref[...]
        for cp in loads:
            cp.wait()
        outs[0] = g
        outs[1], outs[2], outs[3] = _adamw_math(g, bufs[0], bufs[1], bufs[2])
        stores = [pltpu.make_async_copy(outs.at[i], dst.at[:, 0, :], sems.at[3 + i])
                  for i, dst in enumerate((g_out, d_out, nm_out, nv_out))]
        for cp in stores:
            cp.start()
        for cp in stores:
            cp.wait()

    hbm = pl.BlockSpec(memory_space=pl.ANY)
    return pl.pallas_call(
        body, name="adamw_w_in",
        in_specs=[pl.BlockSpec(memory_space=pltpu.VMEM), hbm, hbm, hbm], out_specs=[hbm] * 4,
        out_shape=[jax.ShapeDtypeStruct((rows, 1, width), F32)] * 4,
        scratch_shapes=[pltpu.VMEM((3, rows, width), F32), pltpu.VMEM((4, rows, width), F32),
                        pltpu.SemaphoreType.DMA((7,))],
        compiler_params=_cparams(),
    )(g, w, m, v)


def _adamw_vectors(r_small, r_wg, g_out, params):
    n_par = len(params)

    def body(rsm_ref, rwg_ref, gout_ref, *refs):
        ins, outs = refs[:3 * n_par], refs[3 * n_par:]
        g = rsm_ref[0]
        gwg = rwg_ref[0]
        for j in range(1, N_DEV):
            g = g + rsm_ref[j]
            gwg = gwg + rwg_ref[j]
        outs[4 * n_par][...] = g[40:41]
        grads = [gwg,
                 jnp.concatenate([g[r:r + 1] for r in range(0, 8)], axis=1),
                 jnp.concatenate([g[r:r + 1] for r in range(8, 16)], axis=1),
                 jnp.concatenate([g[16:17], g[17:18]], axis=1),
                 g[24:25],
                 g[32:33, 0:SWA_Q_HEADS],
                 gout_ref[...]]
        for p, gp in enumerate(grads):
            w_ref, m_ref, v_ref = ins[3 * p:3 * p + 3]
            outs[4 * p][...] = gp
            outs[4 * p + 1][...], outs[4 * p + 2][...], outs[4 * p + 3][...] = _adamw_math(
                gp, w_ref[...], m_ref[...], v_ref[...])

    vmem = pl.BlockSpec(memory_space=pltpu.VMEM)
    flat = [t for wmv in params for t in wmv]
    return pl.pallas_call(
        body, name="adamw_vectors",
        in_specs=[vmem] * (3 + len(flat)), out_specs=[vmem] * (4 * n_par + 1),
        out_shape=[jax.ShapeDtypeStruct(wmv[0].shape, F32) for wmv in params for _ in range(4)]
        + [jax.ShapeDtypeStruct((1, 128), F32)],
        compiler_params=_cparams(),
    )(r_small, r_wg, g_out, *flat)


def kernel(x, positions, w_in, gla_w_gate_up, gla_b_gate, attn_sinks, gla_norm_w, w_out, ln_g, ln_b, loss_target, m_w_in, m_gla_w_gate_up, m_gla_b_gate, m_attn_sinks, m_gla_norm_w, m_w_out, m_ln_g, m_ln_b, v_w_in, v_gla_w_gate_up, v_gla_b_gate, v_attn_sinks, v_gla_norm_w, v_w_out, v_ln_g, v_ln_b):
    loss, grad_x, parts_w_in, g_wg, g_bg, g_sinks, g_nw, g_out, g_ln = _local_step(
        x[0], positions[0], _shard_view(w_in), gla_w_gate_up[0], gla_b_gate, attn_sinks[0], gla_norm_w, w_out[0],
        ln_g, ln_b, loss_target[0])

    parts_wg = jnp.transpose(g_wg.reshape(GLA_RANK, N_DEV, 32), (1, 0, 2))
    g_in, r_wg, r_small = _reduce_grads(parts_w_in, parts_wg, g_ln, g_bg, g_nw, g_sinks, loss)

    upd_in = _adamw_shard_view(g_in, _shard_view(w_in), _shard_view(m_w_in), _shard_view(v_w_in))
    upd_in = [jnp.transpose(t, (1, 2, 0)) for t in upd_in]
    vec = _adamw_vectors(r_small, r_wg, g_out, [
        (gla_w_gate_up[0], m_gla_w_gate_up[0], v_gla_w_gate_up[0]), (ln_g, m_ln_g, v_ln_g), (ln_b, m_ln_b, v_ln_b),
        (gla_b_gate, m_gla_b_gate, v_gla_b_gate), (gla_norm_w, m_gla_norm_w, v_gla_norm_w),
        (attn_sinks, m_attn_sinks, v_attn_sinks), (w_out[0], m_w_out[0], v_w_out[0])])

    outs = [vec[28][0, 0], grad_x[None]]
    for kind in range(4):
        u_wg, u_ln_g, u_ln_b, u_bg, u_nw, u_sinks, u_out = (vec[4 * p + kind] for p in range(7))
        outs += [upd_in[kind], u_wg[None], u_bg, u_sinks, u_nw, u_out[None], u_ln_g, u_ln_b]
    return tuple(outs)
```

```python
import jax
import jax.numpy as jnp
from jax import lax
from jax.experimental import pallas as pl
from jax.experimental.pallas import tpu as pltpu

F32 = jnp.float32
MXU_DTYPE = jnp.bfloat16

N_DEV = 8
D_MODEL = 1024
SWA_Q_HEADS = 8
SWA_KV_HEADS = 2
SWA_GROUP = 4
SWA_HEAD_DIM = 64
BLOCK = 128
ROPE_THETA = 500000.0
ROT_DIM = 16
GLA_HEADS = 4
GLA_DK = 64
GLA_DV = 128
GLA_RANK = 16
GLA_TAU = 16.0
GLA_CHUNK = 64
D_IN_PROJ = 2832
D_IN_SHARD = D_IN_PROJ // N_DEV
D_OUT_SHARD = D_MODEL // N_DEV
OFF = (0, 512, 640, 768, 1280, 1536, 1792, 2304, 2816, 2832)
EPS = 1e-5
ALPHA = 2.0 ** 0.25
SWA_SCALE = SWA_HEAD_DIM ** -0.5
GLA_SCALE = GLA_DK ** -0.5
ADAM_LR = 0.001
ADAM_B1 = 0.9
ADAM_B2 = 0.999
ADAM_EPS = 1e-08
ADAM_WD = 0.01
ADAM_STEP = 10
VMEM_LIMIT = 56 * 1024 * 1024

_NT = (((1,), (1,)), ((), ()))
_TN = (((0,), (0,)), ((), ()))


def _mm(a, b):
    return jnp.dot(a, b, preferred_element_type=F32)


def _mm_nt(a, b):
    return lax.dot_general(a, b, _NT, preferred_element_type=F32)


def _mm_tn(a, b):
    return lax.dot_general(a, b, _TN, preferred_element_type=F32)


def _sigmoid(t):
    return 1.0 / (1.0 + jnp.exp(-t))


def _cparams(**kw):
    return pltpu.CompilerParams(vmem_limit_bytes=VMEM_LIMIT, **kw)


def _full(shape):
    return pl.BlockSpec(shape, lambda *_: (0,) * len(shape))


def _rows(tile, width):
    return pl.BlockSpec((tile, width), lambda i: (i, 0))


def _rope_angles(positions):
    half = ROT_DIM // 2
    inv_freq = ROPE_THETA ** (-jnp.arange(half, dtype=F32) / half)
    ang = positions.astype(F32)[:, None] * inv_freq[None, :]
    return jnp.concatenate([jnp.cos(ang), jnp.sin(ang)], axis=1)


def _split3_parts(t):
    hi = t.astype(MXU_DTYPE)
    r1 = t - hi.astype(F32)
    mid = r1.astype(MXU_DTYPE)
    return hi, mid, (r1 - mid.astype(F32)).astype(MXU_DTYPE)


def _rope_tables(cs):
    half = ROT_DIM // 2
    i = lax.broadcasted_iota(jnp.int32, (2 * half, 3 * 128), 0)
    lane = lax.broadcasted_iota(jnp.int32, (2 * half, 3 * 128), 1)
    table, pos = _idiv(lane, 128), lane & (SWA_HEAD_DIM - 1)
    is_c = (table == 0) & (pos < ROT_DIM) & ((pos & (half - 1)) == i)
    is_s1 = (table == 1) & (pos < half) & (pos + half == i)
    is_s2 = (table == 2) & (pos >= half) & (pos < ROT_DIM) & (pos == i)
    sel = jnp.where(is_c | is_s2, 1.0, jnp.where(is_s1, -1.0, 0.0)).astype(MXU_DTYPE)
    hi, mid, lo = _split3_parts(cs)
    t = (_mm(hi, sel) + _mm(mid, sel)) + _mm(lo, sel)
    pos1 = lax.broadcasted_iota(jnp.int32, (1, 128), 1) & (SWA_HEAD_DIM - 1)
    return t[:, 0:128] + jnp.where(pos1 >= ROT_DIM, 1.0, 0.0), t[:, 128:256], t[:, 256:384]


def _rope(t, c, s1, s2):
    return t * c + pltpu.roll(t, 120, 1) * s1 + pltpu.roll(t, 8, 1) * s2


def _rope_t(g, c, s1, s2):
    return g * c + pltpu.roll(g * s1, 8, 1) + pltpu.roll(g * s2, 120, 1)


def _in_proj(x, w_in_t, wg_s, b_gate, cos_sin, w_out_s):
    s = x.shape[0]
    ts = min(512, s)
    nsteps = s // ts
    forward_step = min(3, nsteps - 1)
    widths = [OFF[i + 1] - OFF[i] for i in range(9)]

    def body(x_ref, win_hbm, wgs_ref, bg_ref, cs_ref, wos_ref,
             qa_ref, ka_ref, va_ref, ga_ref, qb_ref, kb_ref, vb_ref, gb_ref, rb_ref, la_ref, oms_ref,
             c_ref, s1_ref, s2_ref, w_ref, wg_ref, wout_ref,
             win_all, wg_all, wout_all, stage, stage_sem, *sems):
        xb = x_ref[...].astype(MXU_DTYPE)
        c, s1, s2 = _rope_tables(cs_ref[...])
        c_ref[...], s1_ref[...], s2_ref[...] = c, s1, s2
        i0 = pl.program_id(0)
        gather = _BlockGather(wout_all, *sems[0:2])

        @pl.when(i0 == 0)
        def _():
            ka_ref[0:BLOCK, :] = jnp.zeros((BLOCK, 128), ka_ref.dtype)
            va_ref[0:BLOCK, :] = jnp.zeros((BLOCK, 128), va_ref.dtype)
            first = (_BlockGather(win_all, *sems[2:4]), _BlockGather(wg_all, *sems[4:6]))
            load = pltpu.make_async_copy(win_hbm.at[:, 0, :], stage, stage_sem)
            load.start()
            wout_all[gather.me] = wos_ref[...].astype(wout_all.dtype)
            wg_all[gather.me] = wgs_ref[...].astype(wg_all.dtype)
            load.wait()
            win_all[gather.me] = stage[...].astype(win_all.dtype)
            for stage_of in ("start", "forward", "finish"):
                for g in first:
                    getattr(g, stage_of)()
            gather.start()
            for j in range(N_DEV):
                w_ref[D_IN_SHARD * j:D_IN_SHARD * (j + 1), :] = win_all[j]
                wg_ref[:, 32 * j:32 * (j + 1)] = wg_all[j]

        @pl.when(i0 == forward_step)
        def _():
            gather.forward()

        @pl.when(i0 == nsteps - 1)
        def _():
            gather.finish()
            for j in range(N_DEV):
                wout_ref[D_OUT_SHARD * j:D_OUT_SHARD * (j + 1), :] = wout_all[j]

        kv_rows = pl.ds(pl.multiple_of(BLOCK + i0 * ts, BLOCK), ts)

        def cols(i):
            return _mm_nt(xb, w_ref[OFF[i]:OFF[i + 1], :])

        qa = cols(0)
        for i in range(4):
            qa_ref[:, 128 * i:128 * (i + 1)] = _rope(qa[:, 128 * i:128 * (i + 1)], c, s1, s2).astype(qa_ref.dtype)
        kv = _mm_nt(xb, w_ref[OFF[1]:OFF[3], :])
        ka_ref[kv_rows, :] = _rope(kv[:, 0:128], c, s1, s2).astype(ka_ref.dtype)
        va_ref[kv_rows, :] = kv[:, 128:256].astype(va_ref.dtype)
        ga_ref[...] = cols(3)
        qb_ref[...] = cols(4)
        kb_ref[...] = cols(5)
        vb_ref[...] = cols(6).astype(vb_ref.dtype)
        gb_ref[...] = cols(7)
        rb = cols(8)
        rb_ref[...] = rb
        logit = _mm(rb.astype(MXU_DTYPE), wg_ref[...]) + bg_ref[...]
        e = jnp.exp(-jnp.abs(logit))
        la_ref[...] = (jnp.minimum(logit, 0.0) - jnp.log(1.0 + e)) / GLA_TAU
        oms_ref[...] = jnp.where(logit >= 0.0, e, 1.0) / (1.0 + e)

    out_shape = [jax.ShapeDtypeStruct((s + BLOCK if i in (1, 2) else s, w), MXU_DTYPE if i in (0, 1, 2, 6) else F32)
                 for i, w in enumerate(widths)]
    out_shape += [jax.ShapeDtypeStruct((s, 256), F32)] * 2 + [jax.ShapeDtypeStruct((s, 128), F32)] * 3
    out_shape += [jax.ShapeDtypeStruct((D_IN_PROJ, D_MODEL), MXU_DTYPE), jax.ShapeDtypeStruct((GLA_RANK, 256), MXU_DTYPE),
                  jax.ShapeDtypeStruct((D_MODEL, D_MODEL), MXU_DTYPE)]
    return pl.pallas_call(
        body, name="in_proj", grid=(nsteps,),
        in_specs=[_rows(ts, D_MODEL), pl.BlockSpec(memory_space=pl.ANY), _full((GLA_RANK, 32)), _full((1, 256)),
                  _rows(ts, ROT_DIM), _full((D_OUT_SHARD, D_MODEL))],
        out_specs=[_full((s + BLOCK, w)) if i in (1, 2) else _rows(ts, w) for i, w in enumerate(widths)]
        + [_rows(ts, 256)] * 2 + [_rows(ts, 128)] * 3
        + [_full((D_IN_PROJ, D_MODEL)), _full((GLA_RANK, 256)), _full((D_MODEL, D_MODEL))],
        out_shape=out_shape,
        scratch_shapes=[pltpu.VMEM((N_DEV, D_IN_SHARD, D_MODEL), MXU_DTYPE), pltpu.VMEM((N_DEV, GLA_RANK, 32), MXU_DTYPE),
                        pltpu.VMEM((N_DEV, D_OUT_SHARD, D_MODEL), MXU_DTYPE),
                        pltpu.VMEM((D_IN_SHARD, D_MODEL), F32), pltpu.SemaphoreType.DMA]
        + 3 * _BlockGather.scratch(),
        compiler_params=_cparams(dimension_semantics=("arbitrary",)),
    )(x, w_in_t, wg_s, b_gate, cos_sin, w_out_s)


SWA_ROWS = SWA_GROUP * BLOCK


def _swa_bias():
    shape = (2, 2 * BLOCK, SWA_ROWS)
    ki = lax.broadcasted_iota(jnp.int32, shape, 1)
    qi = lax.broadcasted_iota(jnp.int32, shape, 2) & (BLOCK - 1)
    first = lax.broadcasted_iota(jnp.int32, shape, 0) == 0
    dist = qi + BLOCK - ki
    ok = (dist >= 0) & (dist < BLOCK) & (jnp.logical_not(first) | (ki >= BLOCK))
    return jnp.where(ok, 0.0, -jnp.inf).astype(F32)


SWA_SUB = 8


def _swa_bias_of(bias_ref, n, b):
    return bias_ref[jnp.minimum(n, 1)] if b == 0 else bias_ref[1]


def _swa_dup(t, j):
    t = t.astype(F32)
    low = lax.broadcasted_iota(jnp.int32, t.shape, 1) < SWA_HEAD_DIM
    keep = low if j == 0 else jnp.logical_not(low)
    return jnp.where(keep, t, pltpu.roll(t, SWA_HEAD_DIM, 1)).astype(MXU_DTYPE)


def _swa_stack(t, j):
    low = lax.broadcasted_iota(jnp.int32, (BLOCK, 128), 1) < SWA_HEAD_DIM
    zero = jnp.zeros((BLOCK, 128), t.dtype)
    blocks = []
    for p in (2 * j, 2 * j + 1):
        tp = t[:, 128 * p:128 * (p + 1)]
        blocks += [jnp.where(low, tp, zero), jnp.where(low, zero, tp)]
    return jnp.concatenate(blocks, axis=0)


def _swa_unstack(t):
    low = lax.broadcasted_iota(jnp.int32, (BLOCK, 128), 1) < SWA_HEAD_DIM
    return [jnp.where(low, t[2 * BLOCK * i:2 * BLOCK * i + BLOCK], t[2 * BLOCK * i + BLOCK:2 * BLOCK * (i + 1)])
            for i in range(2)]


def _swa_sink_row(sink_ref, j):
    lane = lax.broadcasted_iota(jnp.int32, (1, SWA_ROWS), 1)
    row = jnp.full((1, SWA_ROWS), sink_ref[SWA_GROUP * j], F32)
    for r in range(1, SWA_GROUP):
        row = jnp.where(lane >= BLOCK * r, sink_ref[SWA_GROUP * j + r], row)
    return row


def _split3(t):
    return jnp.concatenate(_split3_parts(t), axis=1)


def _row_sums_as_row(t):
    ones = jnp.ones((8, 3 * t.shape[1]), MXU_DTYPE)
    return _mm_nt(ones, _split3(t))[0:1, :]


def _swa_probs_t(qs, kd, bias_t, sink):
    sc = _mm_nt(kd, qs) + bias_t
    m = jnp.maximum(jnp.max(sc, axis=0, keepdims=True), sink)
    p = jnp.exp(sc - m)
    ps = jnp.exp(sink - m)
    rinv = 1.0 / (jnp.sum(p, axis=0, keepdims=True) + ps)
    return p * rinv, ps * rinv


def _swa_fwd(sinks, qa, k_pad, v_pad, ga):
    s = qa.shape[0]
    sub = min(SWA_SUB, s // BLOCK)
    tq = sub * BLOCK

    def body(sink_ref, qa_ref, ga_ref, bias_ref, k_ref, v_ref, attn_ref, cat_ref):
        n = pl.program_id(0)
        for b in range(sub):
            rows = slice(BLOCK * b, BLOCK * (b + 1))
            start = pl.multiple_of((n * sub + b) * BLOCK, BLOCK)
            kw = k_ref[pl.ds(start, 2 * BLOCK), :]
            vw = v_ref[pl.ds(start, 2 * BLOCK), :]
            bias_t = _swa_bias_of(bias_ref, n, b)
            q = qa_ref[rows, :] * SWA_SCALE
            g = ga_ref[rows, :]
            silu = g * _sigmoid(g)
            for j in range(SWA_KV_HEADS):
                qs = _swa_stack(q, j).astype(MXU_DTYPE)
                probs, _ = _swa_probs_t(qs, _swa_dup(kw, j), bias_t, _swa_sink_row(sink_ref, j))
                pairs = _swa_unstack(_mm_tn(probs.astype(MXU_DTYPE), _swa_dup(vw, j)))
                for i in range(2):
                    lanes = slice(128 * (2 * j + i), 128 * (2 * j + i + 1))
                    attn_ref[rows, lanes] = pairs[i]
                    cat_ref[rows, lanes] = (pairs[i] * silu[:, lanes]).astype(cat_ref.dtype)

    return pl.pallas_call(
        body, name="swa_fwd", grid=(s // tq,),
        in_specs=[pl.BlockSpec(memory_space=pltpu.SMEM), _rows(tq, 512), _rows(tq, 512),
                  _full((2, 2 * BLOCK, SWA_ROWS)), _full((s + BLOCK, 128)), _full((s + BLOCK, 128))],
        out_specs=[_rows(tq, 512), _rows(tq, 512)],
        out_shape=[jax.ShapeDtypeStruct((s, 512), F32), jax.ShapeDtypeStruct((s, 512), MXU_DTYPE)],
        compiler_params=_cparams(dimension_semantics=("arbitrary",)),
    )(sinks, qa, ga, _swa_bias(), k_pad, v_pad)


GLA_KW = GLA_HEADS * GLA_DK
GLA_VW = GLA_HEADS * GLA_DV


def _idiv(t, d):
    return t >> (d.bit_length() - 1)


def _chunk_cumsum(t, lower):
    n, w = t.shape
    r = lax.broadcasted_iota(jnp.int32, (n, n), 0)
    c = lax.broadcasted_iota(jnp.int32, (n, n), 1)
    tri = ((_idiv(r, GLA_CHUNK) == _idiv(c, GLA_CHUNK)) & ((r >= c) if lower else (r <= c))).astype(MXU_DTYPE)
    parts = _mm(tri, _split3(t))
    return (parts[:, :w] + parts[:, w:2 * w]) + parts[:, 2 * w:]


def _chunk_last(t):
    n = t.shape[0]
    return jnp.concatenate(
        [jnp.broadcast_to(t[c + GLA_CHUNK - 1:c + GLA_CHUNK, :], (GLA_CHUNK, t.shape[1]))
         for c in range(0, n, GLA_CHUNK)], axis=0)


def _head_stack(t, width):
    head = _idiv(lax.broadcasted_iota(jnp.int32, t.shape, 1), width)
    zero = jnp.zeros_like(t)
    return jnp.concatenate([jnp.where(head == h, t, zero) for h in range(GLA_HEADS)], axis=0)


def _heads_to_rows(t):
    return jnp.concatenate([t[:, GLA_DV * h:GLA_DV * (h + 1)] for h in range(GLA_HEADS)], axis=0)


def _rows_to_heads(t):
    return jnp.concatenate([t[GLA_CHUNK * h:GLA_CHUNK * (h + 1)] for h in range(GLA_HEADS)], axis=1)


def _state_by_head(t):
    srow = _idiv(lax.broadcasted_iota(jnp.int32, (GLA_VW, GLA_KW), 0), GLA_DV)
    slane = _idiv(lax.broadcasted_iota(jnp.int32, (GLA_VW, GLA_KW), 1), GLA_DK)
    return jnp.where(srow == slane, jnp.concatenate([t] * GLA_HEADS, axis=0), jnp.zeros((GLA_VW, GLA_KW), t.dtype))


def _gla_masks():
    row = lax.broadcasted_iota(jnp.int32, (GLA_CHUNK, GLA_KW), 0)
    pos = lax.broadcasted_iota(jnp.int32, (GLA_CHUNK, GLA_KW), 1) & (GLA_CHUNK - 1)
    return pos <= row, pos >= row


def _gla_fwd(qb, kb, vb, la, gb, norm_w):
    s = qb.shape[0]
    tb = min(256, s)
    ch = tb // GLA_CHUNK

    def body(qb_ref, kb_ref, vb_ref, la_ref, gb_ref, nw_ref, o_ref, cat_ref, sp_ref, st_ref):
        @pl.when(pl.program_id(0) == 0)
        def _():
            st_ref[...] = jnp.zeros_like(st_ref)

        causal, _ = _gla_masks()
        nw = nw_ref[...]
        b = _chunk_cumsum(la_ref[...], True)
        bl = _chunk_last(b)
        k = kb_ref[...]
        qd = ((qb_ref[...] * GLA_SCALE) * jnp.exp(b)).astype(MXU_DTYPE)
        ki = (k * jnp.exp(-b)).astype(MXU_DTYPE)
        ke = (k * jnp.exp(bl - b)).astype(MXU_DTYPE)
        dec = jnp.exp(bl)
        v = vb_ref[...].astype(MXU_DTYPE)
        g = gb_ref[...]
        silu = g * _sigmoid(g)
        for ci in range(ch):
            rows = slice(GLA_CHUNK * ci, GLA_CHUNK * (ci + 1))
            qds, kis, kes = (_head_stack(t[rows], GLA_DK) for t in (qd, ki, ke))
            a = jnp.where(causal, _mm_nt(qd[rows], kis), 0.0).astype(MXU_DTYPE)
            st = st_ref[...]
            sp_ref[ci] = st
            o = _mm(a, _head_stack(v[rows], GLA_DV)) + _rows_to_heads(_mm_nt(qds, st.astype(MXU_DTYPE)))
            st_ref[...] = st * dec[rows][0:1] + _mm_tn(_heads_to_rows(v[rows]), kes)
            o_ref[rows, :] = o
            for h in range(GLA_HEADS):
                lv = slice(GLA_DV * h, GLA_DV * (h + 1))
                oh = o[:, lv]
                r = lax.rsqrt(jnp.mean(oh * oh, axis=1, keepdims=True) + EPS)
                cat_ref[rows, lv] = (oh * r * nw * silu[rows, lv]).astype(cat_ref.dtype)

    return pl.pallas_call(
        body, name="gla_fwd", grid=(s // tb,),
        in_specs=[_rows(tb, 256), _rows(tb, 256), _rows(tb, 512), _rows(tb, 256), _rows(tb, 512), _full((1, 128))],
        out_specs=[_rows(tb, 512), _rows(tb, 512), pl.BlockSpec((ch, GLA_DV, 256), lambda i: (i, 0, 0))],
        out_shape=[jax.ShapeDtypeStruct((s, 512), F32), jax.ShapeDtypeStruct((s, 512), MXU_DTYPE),
                   jax.ShapeDtypeStruct((s // GLA_CHUNK, GLA_DV, 256), F32)],
        scratch_shapes=[pltpu.VMEM((GLA_DV, GLA_KW), F32)],
        compiler_params=_cparams(dimension_semantics=("arbitrary",)),
    )(qb, kb, vb, la, gb, norm_w)


def _out_ln_loss(cat_a, cat_b, w_out, x, target, ln_g, ln_b):
    s = x.shape[0]
    ts = min(512, s)
    halves = 2 if ts % 32 == 0 else 1
    th = ts // halves

    def body(ca_ref, cb_ref, w_ref, x_ref, t_ref, g_ref, b_ref,
             loss_ref, gx_ref, da_ref, db_ref, gw_ref, gln_ref):
        @pl.when(pl.program_id(0) == 0)
        def _():
            loss_ref[...] = jnp.zeros_like(loss_ref)
            gw_ref[...] = jnp.zeros_like(gw_ref)
            gln_ref[...] = jnp.zeros_like(gln_ref)

        g = g_ref[...]
        dh16s = []
        for k in range(halves):
            rows = slice(th * k, th * (k + 1))
            mix = _mm(ca_ref[rows, :], w_ref[0:512, :]) + _mm(cb_ref[rows, :], w_ref[512:1024, :])
            h = ALPHA * x_ref[rows, :] + mix
            mu = jnp.mean(h, axis=1, keepdims=True)
            hc = h - mu
            rstd = lax.rsqrt(jnp.mean(hc * hc, axis=1, keepdims=True) + EPS)
            xhat = hc * rstd
            err = xhat * g + b_ref[...] - t_ref[rows, :]
            loss_ref[...] += 0.5 * jnp.sum(jnp.mean(err * err, axis=1, keepdims=True))
            dy = err * (1.0 / D_MODEL)
            gln_ref[0:1, :] += jnp.sum(dy * xhat, axis=0, keepdims=True)
            gln_ref[1:2, :] += jnp.sum(dy, axis=0, keepdims=True)
            dxh = dy * g
            dh = rstd * (dxh - jnp.mean(dxh, axis=1, keepdims=True)
                         - xhat * jnp.mean(dxh * xhat, axis=1, keepdims=True))
            gx_ref[rows, :] = ALPHA * dh
            dh16s.append(dh.astype(MXU_DTYPE))
        for k in range(halves):
            rows = slice(th * k, th * (k + 1))
            da_ref[rows, :] = _mm_nt(dh16s[k], w_ref[0:512, :])
            db_ref[rows, :] = _mm_nt(dh16s[k], w_ref[512:1024, :])
        dh16 = jnp.concatenate(dh16s, axis=0)
        gw_ref[0:512, :] += _mm_tn(ca_ref[...], dh16)
        gw_ref[512:1024, :] += _mm_tn(cb_ref[...], dh16)

    return pl.pallas_call(
        body, name="out_ln_loss", grid=(s // ts,),
        in_specs=[_rows(ts, 512), _rows(ts, 512), _full((D_MODEL, D_MODEL)), _rows(ts, D_MODEL), _rows(ts, D_MODEL),
                  _full((1, D_MODEL)), _full((1, D_MODEL))],
        out_specs=[_full((1, 128)), _rows(ts, D_MODEL), _rows(ts, 512), _rows(ts, 512),
                   _full((D_MODEL, D_MODEL)), _full((2, D_MODEL))],
        out_shape=[jax.ShapeDtypeStruct((1, 128), F32), jax.ShapeDtypeStruct((s, D_MODEL), F32),
                   jax.ShapeDtypeStruct((s, 512), F32), jax.ShapeDtypeStruct((s, 512), F32),
                   jax.ShapeDtypeStruct((D_MODEL, D_MODEL), F32), jax.ShapeDtypeStruct((2, D_MODEL), F32)],
        compiler_params=_cparams(dimension_semantics=("arbitrary",)),
    )(cat_a, cat_b, w_out, x, target, ln_g, ln_b)


def _swa_bwd(sinks, qa, k_pad, v_pad, attn, ga, d_cat_a, rope, parts_w_out):
    s = qa.shape[0]
    sub = min(SWA_SUB, s // BLOCK)
    tq = sub * BLOCK
    nsteps = s // tq
    forward_step = min(1, nsteps - 1)

    def body(sink_ref, qa_ref, ga_ref, at_ref, dc_ref, c_ref, s1_ref, s2_ref, bias_ref, k_ref, v_ref, pout_ref,
             dq_ref, dg_ref, dk_ref, dv_ref, ds_ref, gout_ref, *scratch):
        n = pl.program_id(0)
        owner_sum = _OwnerSum(pout_ref, *scratch)

        @pl.when(n == 0)
        def _():
            dk_ref[...] = jnp.zeros_like(dk_ref)
            dv_ref[...] = jnp.zeros_like(dv_ref)
            ds_ref[...] = jnp.zeros_like(ds_ref)
            owner_sum.start()

        @pl.when(n == forward_step)
        def _():
            owner_sum.forward()

        @pl.when(n == nsteps - 1)
        def _():
            gout_ref[...] = owner_sum.finish()

        low = lax.broadcasted_iota(jnp.int32, (2 * BLOCK, 128), 1) < SWA_HEAD_DIM
        for b in range(sub):
            rows = slice(BLOCK * b, BLOCK * (b + 1))
            start = pl.multiple_of((n * sub + b) * BLOCK, BLOCK)
            kw = k_ref[pl.ds(start, 2 * BLOCK), :]
            vw = v_ref[pl.ds(start, 2 * BLOCK), :]
            bias_t = _swa_bias_of(bias_ref, n, b)
            q = qa_ref[rows, :] * SWA_SCALE
            g = ga_ref[rows, :]
            sg = _sigmoid(g)
            o = at_ref[rows, :]
            dc = dc_ref[rows, :]
            do = dc * (g * sg)
            dg_ref[rows, :] = (dc * o * (sg * (1.0 + g * (1.0 - sg)))).astype(dg_ref.dtype)
            od = do * o
            c, s1, s2 = c_ref[rows, :], s1_ref[rows, :], s2_ref[rows, :]
            dk, dv = [], []
            for j in range(SWA_KV_HEADS):
                kd, vd = _swa_dup(kw, j), _swa_dup(vw, j)
                qs = _swa_stack(q, j).astype(MXU_DTYPE)
                dos = _swa_stack(do, j).astype(MXU_DTYPE)
                probs, psink = _swa_probs_t(qs, kd, bias_t, _swa_sink_row(sink_ref, j))
                delta = _row_sums_as_row(_swa_stack(od, j))
                dsc = (probs * (_mm_nt(vd, dos) - delta)).astype(MXU_DTYPE)
                dsink = psink * delta
                for r in range(SWA_GROUP):
                    h = SWA_GROUP * j + r
                    ds_ref[h:h + 1, :] += jnp.zeros((1, 128), F32) - jnp.sum(dsink[:, BLOCK * r:BLOCK * (r + 1)])
                dq = _swa_unstack(_mm_tn(dsc, kd))
                for i in range(2):
                    lanes = slice(128 * (2 * j + i), 128 * (2 * j + i + 1))
                    dq_ref[rows, lanes] = _rope_t(dq[i] * SWA_SCALE, c, s1, s2).astype(dq_ref.dtype)
                dkj = _mm(dsc, qs)
                dvj = _mm(probs.astype(MXU_DTYPE), dos)
                dk.append(dkj + pltpu.roll(dkj, SWA_HEAD_DIM, 1))
                dv.append(dvj + pltpu.roll(dvj, SWA_HEAD_DIM, 1))
            dk_ref[pl.ds(start, 2 * BLOCK), :] += jnp.where(low, dk[0], dk[1])
            dv_ref[pl.ds(start, 2 * BLOCK), :] += jnp.where(low, dv[0], dv[1])

    out_blk = parts_w_out.shape[1:]
    return pl.pallas_call(
        body, name="swa_bwd", grid=(nsteps,),
        in_specs=[pl.BlockSpec(memory_space=pltpu.SMEM)] + [_rows(tq, 512)] * 4 + [_rows(tq, 128)] * 3
        + [_full((2, 2 * BLOCK, SWA_ROWS))] + [_full((s + BLOCK, 128))] * 2 + [pl.BlockSpec(memory_space=pl.ANY)],
        out_specs=[_rows(tq, 512), _rows(tq, 512), _full((s + BLOCK, 128)), _full((s + BLOCK, 128)),
                   _full((SWA_Q_HEADS, 128)), _full(out_blk)],
        out_shape=[jax.ShapeDtypeStruct((s, 512), MXU_DTYPE), jax.ShapeDtypeStruct((s, 512), MXU_DTYPE),
                   jax.ShapeDtypeStruct((s + BLOCK, 128), F32), jax.ShapeDtypeStruct((s + BLOCK, 128), F32),
                   jax.ShapeDtypeStruct((SWA_Q_HEADS, 128), F32), jax.ShapeDtypeStruct(out_blk, F32)],
        scratch_shapes=_OwnerSum.scratch(out_blk),
        compiler_params=_cparams(dimension_semantics=("arbitrary",)),
    )(sinks, qa, ga, attn, d_cat_a, *rope, _swa_bias(), k_pad, v_pad, parts_w_out)


def _gla_bwd(qb, kb, vb, la, oms, gb, o, sprev, d_cat_b, rb, wg, norm_w):
    s = qb.shape[0]
    tb = min(512, s)
    ch = tb // GLA_CHUNK
    nb = s // tb

    def body(qb_ref, kb_ref, vb_ref, la_ref, oms_ref, gb_ref, o_ref, sp_ref, dc_ref, rb_ref, wg_ref, nw_ref,
             dq_ref, dk_ref, dv_ref, dg_ref, dr_ref, gwg_ref, gbg_ref, gnw_ref, dst_ref):
        @pl.when(pl.program_id(0) == 0)
        def _():
            dst_ref[...] = jnp.zeros_like(dst_ref)
            gwg_ref[...] = jnp.zeros_like(gwg_ref)
            gbg_ref[...] = jnp.zeros_like(gbg_ref)
            gnw_ref[...] = jnp.zeros_like(gnw_ref)

        causal, causal_t = _gla_masks()
        nw = nw_ref[...]
        b = _chunk_cumsum(la_ref[...], True)
        bl = _chunk_last(b)
        eb, enb, ee, dec = jnp.exp(b), jnp.exp(-b), jnp.exp(bl - b), jnp.exp(bl)
        k = kb_ref[...]
        qd = (qb_ref[...] * GLA_SCALE) * eb
        ki = k * enb
        ke = k * ee
        qd16, ki16, ke16 = qd.astype(MXU_DTYPE), ki.astype(MXU_DTYPE), ke.astype(MXU_DTYPE)
        v16 = vb_ref[...].astype(MXU_DTYPE)

        g = gb_ref[...]
        sg = _sigmoid(g)
        silu = g * sg
        dsilu = sg * (1.0 + g * (1.0 - sg))
        gnw = jnp.zeros((1, GLA_DV), F32)
        do = []
        for h in range(GLA_HEADS):
            lv = slice(GLA_DV * h, GLA_DV * (h + 1))
            oh = o_ref[:, lv]
            dch = dc_ref[:, lv]
            r = lax.rsqrt(jnp.mean(oh * oh, axis=1, keepdims=True) + EPS)
            d_on = dch * silu[:, lv]
            dg_ref[:, lv] = (dch * (oh * r * nw) * dsilu[:, lv]).astype(dg_ref.dtype)
            gnw += jnp.sum(d_on * oh * r, axis=0, keepdims=True)
            u = d_on * nw
            do.append(r * u - oh * (r * r * r) * jnp.mean(u * oh, axis=1, keepdims=True))
        gnw_ref[...] += gnw
        do16 = jnp.concatenate(do, axis=1).astype(MXU_DTYPE)

        db, dbl = [None] * ch, [None] * ch
        for ci in reversed(range(ch)):
            rows = slice(GLA_CHUNK * ci, GLA_CHUNK * (ci + 1))
            qds, kis, kes = (_head_stack(t[rows], GLA_DK) for t in (qd16, ki16, ke16))
            vs, dos = _head_stack(v16[rows], GLA_DV), _head_stack(do16[rows], GLA_DV)
            a = jnp.where(causal, _mm_nt(qd16[rows], kis), 0.0).astype(MXU_DTYPE)
            at = jnp.where(causal_t, _mm_nt(ki16[rows], qds), 0.0).astype(MXU_DTYPE)
            da = jnp.where(causal, _mm_nt(do16[rows], vs), 0.0).astype(MXU_DTYPE)
            dat = jnp.where(causal_t, _mm_nt(v16[rows], dos), 0.0).astype(MXU_DTYPE)
            st = sp_ref[ci]
            dst = dst_ref[...]
            dst16 = dst.astype(MXU_DTYPE)
            dv = _mm(at, dos) + _rows_to_heads(_mm_nt(kes, dst16))
            dqd = _mm(da, kis) + _mm(do16[rows], _state_by_head(st.astype(MXU_DTYPE)))
            dki = _mm(dat, qds)
            dke = _mm(v16[rows], _state_by_head(dst16))
            ddec = jnp.sum(dst * st, axis=0, keepdims=True)
            decc = dec[rows][0:1]
            dst_ref[...] = _mm_tn(_heads_to_rows(do16[rows]), qds) + dst * decc
            dq_ref[rows, :] = (dqd * eb[rows] * GLA_SCALE).astype(dq_ref.dtype)
            dk_ref[rows, :] = (dki * enb[rows] + dke * ee[rows]).astype(dk_ref.dtype)
            dv_ref[rows, :] = dv.astype(dv_ref.dtype)
            dke_ke = dke * ke[rows]
            db[ci] = dqd * qd[rows] - dki * ki[rows] - dke_ke
            dbl[ci] = jnp.broadcast_to(jnp.sum(dke_ke, axis=0, keepdims=True) + ddec * decc, (GLA_CHUNK, GLA_KW))

        dla = _chunk_cumsum(jnp.concatenate(db, axis=0), False) + jnp.concatenate(dbl, axis=0)
        dlogit = dla * oms_ref[...] * (1.0 / GLA_TAU)
        dl16 = dlogit.astype(MXU_DTYPE)
        gbg_ref[...] += jnp.sum(dlogit, axis=0, keepdims=True)
        gwg_ref[...] += _mm_tn(rb_ref[...].astype(MXU_DTYPE), dl16)
        dr_ref[...] = _mm_nt(dl16, wg_ref[...]).astype(dr_ref.dtype)

    def rev(width):
        return pl.BlockSpec((tb, width), lambda i: (nb - 1 - i, 0))

    return pl.pallas_call(
        body, name="gla_bwd", grid=(nb,),
        in_specs=[rev(256), rev(256), rev(512), rev(256), rev(256), rev(512), rev(512),
                  pl.BlockSpec((ch, GLA_DV, 256), lambda i: (nb - 1 - i, 0, 0)), rev(512), rev(GLA_RANK),
                  _full((GLA_RANK, 256)), _full((1, 128))],
        out_specs=[rev(256), rev(256), rev(512), rev(512), rev(GLA_RANK),
                   _full((GLA_RANK, 256)), _full((1, 256)), _full((1, 128))],
        out_shape=[jax.ShapeDtypeStruct((s, 256), MXU_DTYPE), jax.ShapeDtypeStruct((s, 256), MXU_DTYPE),
                   jax.ShapeDtypeStruct((s, 512), MXU_DTYPE), jax.ShapeDtypeStruct((s, 512), MXU_DTYPE),
                   jax.ShapeDtypeStruct((s, GLA_RANK), MXU_DTYPE), jax.ShapeDtypeStruct((GLA_RANK, 256), F32),
                   jax.ShapeDtypeStruct((1, 256), F32), jax.ShapeDtypeStruct((1, 128), F32)],
        scratch_shapes=[pltpu.VMEM((GLA_DV, GLA_KW), F32)],
        compiler_params=_cparams(dimension_semantics=("arbitrary",)),
    )(qb, kb, vb, la, oms, gb, o, sprev, d_cat_b, rb, wg, norm_w)


def _dproj_specs(s, ts):
    widths = [OFF[i + 1] - OFF[i] for i in range(9)]
    return ([_full((s + BLOCK, w)) if i in (1, 2) else _rows(ts, w) for i, w in enumerate(widths)]
            + [_rows(ts, 128)] * 3)


def _dproj_tiles(piece_refs, rope_refs, ts):
    kv_rows = pl.ds(pl.multiple_of(BLOCK + pl.program_id(0) * ts, BLOCK), ts)
    for i in (0, 1, 3, 4, 5, 6, 7, 8):
        if i == 1:
            dk = _rope_t(piece_refs[1][kv_rows, :], *(r[...] for r in rope_refs))
            yield OFF[1], OFF[3], jnp.concatenate([dk, piece_refs[2][kv_rows, :]], axis=1).astype(MXU_DTYPE)
        else:
            yield OFF[i], OFF[i + 1], piece_refs[i][...].astype(MXU_DTYPE)


def _in_proj_bwd_x(gx0, pieces, w_in, rope):
    s = gx0.shape[0]
    ts = min(512, s)

    def body(gx0_ref, *refs):
        w_ref, gx_ref = refs[12:]
        acc = gx0_ref[...]
        for lo, hi, t16 in _dproj_tiles(refs[:9], refs[9:12], ts):
            acc += _mm(t16, w_ref[lo:hi, :])
        gx_ref[...] = acc

    return pl.pallas_call(
        body, name="in_proj_bwd_x", grid=(s // ts,),
        in_specs=[_rows(ts, D_MODEL)] + _dproj_specs(s, ts) + [_full((D_IN_PROJ, D_MODEL))],
        out_specs=_rows(ts, D_MODEL),
        out_shape=jax.ShapeDtypeStruct((s, D_MODEL), F32),
        compiler_params=_cparams(dimension_semantics=("arbitrary",)),
    )(gx0, *pieces, *rope, w_in)


def _in_proj_bwd_w(x, pieces, rope):
    s = x.shape[0]
    ts = min(1024, s)
    nsteps = s // ts

    def body(x_ref, *refs):
        gw_ref, acc_ref, stage_ref, sems = refs[12:]
        i = pl.program_id(0)

        @pl.when(i == 0)
        def _():
            acc_ref[...] = jnp.zeros_like(acc_ref)

        xb = x_ref[...].astype(MXU_DTYPE)
        for lo, hi, t16 in _dproj_tiles(refs[:9], refs[9:12], ts):
            acc_ref[lo:hi, :] += _mm_tn(t16, xb)

        @pl.when(i == nsteps - 1)
        def _():
            copies = []
            for j in range(N_DEV):
                slot = j % 2
                if j >= 2:
                    copies[j - 2].wait()
                stage_ref[slot] = acc_ref[D_IN_SHARD * j:D_IN_SHARD * (j + 1), :]
                cp = pltpu.make_async_copy(stage_ref.at[slot], gw_ref.at[j], sems.at[slot])
                cp.start()
                copies.append(cp)
            copies[N_DEV - 2].wait()
            copies[N_DEV - 1].wait()

    return pl.pallas_call(
        body, name="in_proj_bwd_w", grid=(nsteps,),
        in_specs=[_rows(ts, D_MODEL)] + _dproj_specs(s, ts),
        out_specs=pl.BlockSpec(memory_space=pl.ANY),
        out_shape=jax.ShapeDtypeStruct((N_DEV, D_IN_SHARD, D_MODEL), F32),
        scratch_shapes=[pltpu.VMEM((D_IN_PROJ, D_MODEL), F32), pltpu.VMEM((2, D_IN_SHARD, D_MODEL), F32),
                        pltpu.SemaphoreType.DMA((2,))],
        compiler_params=_cparams(dimension_semantics=("arbitrary",)),
    )(x, *pieces, *rope)


def _local_step(x, positions, w_in_t, wg_s, b_gate, sinks, norm_w, w_out_s, ln_g, ln_b, target):
    qa, k_pad, v_pad, ga, qb, kb, vb, gb, rb, la, oms, *rope, w_in, wg, w_out = _in_proj(
        x, w_in_t, wg_s, b_gate, _rope_angles(positions), w_out_s)
    attn, cat_a = _swa_fwd(sinks, qa, k_pad, v_pad, ga)
    o, cat_b, sprev = _gla_fwd(qb, kb, vb, la, gb, norm_w)
    loss, gx0, d_cat_a, d_cat_b, g_w_out, g_ln = _out_ln_loss(cat_a, cat_b, w_out, x, target, ln_g, ln_b)
    parts_w_out = g_w_out.reshape(N_DEV, D_OUT_SHARD, D_MODEL)
    dqa, dga, dk_pad, dv_pad, g_sinks, g_out = _swa_bwd(sinks, qa, k_pad, v_pad, attn, ga, d_cat_a, rope, parts_w_out)
    dqb, dkb, dvb, dgb, drb, g_wg, g_bg, g_nw = _gla_bwd(qb, kb, vb, la, oms, gb, o, sprev, d_cat_b, rb, wg, norm_w)
    pieces = (dqa, dk_pad, dv_pad, dga, dqb, dkb, dvb, dgb, drb)
    grad_x = _in_proj_bwd_x(gx0, pieces, w_in, rope)
    g_w_in = _in_proj_bwd_w(x, pieces, rope)
    return loss, grad_x, g_w_in, g_wg, g_bg, g_sinks, g_nw, g_out, g_ln


def _mesh_pos():
    return lax.axis_index("x"), lax.axis_index("y"), lax.axis_index("c")


def _peer(k, x, y, c):
    px = (1 - x) if k & 4 else x
    py = (1 - y) if k & 2 else y
    pc = (1 - c) if k & 1 else c
    return (px, py, pc), 4 * px + 2 * py + pc


def _other_chips(x, y):
    return [(1 - x, y), (x, 1 - y), (1 - x, 1 - y)]


def _shard_view(t):
    return jnp.transpose(t, (2, 0, 1))


class _BlockGather:
    def __init__(self, slots, send_sems, recv_sems):
        self.slots, self.send_sems, self.recv_sems = slots, send_sems, recv_sems
        x, y, c = _mesh_pos()
        self.xy, self.c, self.me, self.sibling = (x, y), c, 4 * x + 2 * y + c, (x, y, 1 - c)
        self.chips = _other_chips(x, y)

    @staticmethod
    def scratch():
        return [pltpu.SemaphoreType.DMA((N_DEV - 1,)), pltpu.SemaphoreType.DMA((N_DEV - 1,))]

    def _copy(self, k, block, to):
        return pltpu.make_async_remote_copy(
            src_ref=self.slots.at[block], dst_ref=self.slots.at[block], send_sem=self.send_sems.at[k],
            recv_sem=self.recv_sems.at[k], device_id=to, device_id_type=pl.DeviceIdType.MESH)

    def start(self):
        for j, (cx, cy) in enumerate(self.chips):
            self._copy(1 + j, self.me, (cx, cy, self.c)).start()
        self._copy(0, self.me, self.sibling).start()

    def forward(self):
        for j, (cx, cy) in enumerate(self.chips):
            block = 4 * cx + 2 * cy + self.c
            self._copy(1 + j, block, self.sibling).wait_recv()
            self._copy(4 + j, block, self.sibling).start()

    def finish(self):
        x, y = self.xy
        self._copy(0, 4 * x + 2 * y + (1 - self.c), self.sibling).wait_recv()
        for j, (cx, cy) in enumerate(self.chips):
            self._copy(4 + j, 4 * cx + 2 * cy + (1 - self.c), self.sibling).wait_recv()
        for k in range(N_DEV - 1):
            self._copy(k, self.me, self.sibling).wait_send()


class _OwnerSum:
    def __init__(self, parts, own, sib, snd, rcv, loc_sems, d2d_send, d2d_recv, ici_send, ici_recv):
        self.parts, self.own, self.sib, self.snd, self.rcv = parts, own, sib, snd, rcv
        self.sems = (loc_sems, d2d_send, d2d_recv, ici_send, ici_recv)
        x, y, c = _mesh_pos()
        self.c, self.sibling = c, (x, y, 1 - c)
        self.chips = [(x, y)] + _other_chips(x, y)

    @staticmethod
    def scratch(block):
        return [pltpu.VMEM((4,) + block, F32), pltpu.VMEM((4,) + block, F32),
                pltpu.VMEM((3,) + block, MXU_DTYPE), pltpu.VMEM((3,) + block, MXU_DTYPE),
                pltpu.SemaphoreType.DMA((4,)), pltpu.SemaphoreType.DMA((4,)), pltpu.SemaphoreType.DMA((4,)),
                pltpu.SemaphoreType.DMA((3,)), pltpu.SemaphoreType.DMA((3,))]

    def _local(self, r):
        cx, cy = self.chips[r]
        return pltpu.make_async_copy(self.parts.at[4 * cx + 2 * cy + self.c], self.own.at[r], self.sems[0].at[r])

    def _d2d(self, r):
        cx, cy = self.chips[r]
        return pltpu.make_async_remote_copy(
            src_ref=self.parts.at[4 * cx + 2 * cy + (1 - self.c)], dst_ref=self.sib.at[r], send_sem=self.sems[1].at[r],
            recv_sem=self.sems[2].at[r], device_id=self.sibling, device_id_type=pl.DeviceIdType.MESH)

    def _ici(self, r):
        cx, cy = self.chips[r]
        return pltpu.make_async_remote_copy(
            src_ref=self.snd.at[r - 1], dst_ref=self.rcv.at[r - 1], send_sem=self.sems[3].at[r - 1],
            recv_sem=self.sems[4].at[r - 1], device_id=(cx, cy, self.c), device_id_type=pl.DeviceIdType.MESH)

    def start(self):
        for r in (1, 2, 3, 0):
            self._local(r).start()
            self._d2d(r).start()

    def forward(self):
        for r in (1, 2, 3):
            self._local(r).wait()
            self._d2d(r).wait_recv()
            self.snd[r - 1] = (self.own[r] + self.sib[r]).astype(self.snd.dtype)
            self._ici(r).start()

    def finish(self):
        self._local(0).wait()
        self._d2d(0).wait_recv()
        acc = self.own[0] + self.sib[0]
        for r in (1, 2, 3):
            self._ici(r).wait_recv()
            acc = acc + self.rcv[r - 1].astype(F32)
        for r in range(4):
            self._d2d(r).wait_send()
        for r in (1, 2, 3):
            self._ici(r).wait_send()
        return acc


SMALL_ROWS = 48


def _reduce_grads(parts_w_in, parts_wg, g_ln, g_bg, g_nw, g_sinks, loss):
    def body(pin_ref, pwg_ref, gln_ref, gbg_ref, gnw_ref, gsk_ref, loss_ref, gin_ref, rwg_ref, rsm_ref, sm_ref,
             *scratch):
        sm_send, sm_recv, sm_loc = scratch[-3:]
        x, y, c = _mesh_pos()
        me = 4 * x + 2 * y + c
        owner_sum = _OwnerSum(pin_ref, *scratch[:-3])

        sm_ref[...] = jnp.zeros_like(sm_ref)
        for r in range(D_MODEL // 128):
            sm_ref[r:r + 1, :] = gln_ref[0:1, 128 * r:128 * (r + 1)]
            sm_ref[8 + r:9 + r, :] = gln_ref[1:2, 128 * r:128 * (r + 1)]
        for r in range(2):
            sm_ref[16 + r:17 + r, :] = gbg_ref[0:1, 128 * r:128 * (r + 1)]
        sm_ref[24:25, :] = gnw_ref[...]
        diag = lax.broadcasted_iota(jnp.int32, gsk_ref.shape, 0) == lax.broadcasted_iota(jnp.int32, gsk_ref.shape, 1)
        sm_ref[32:33, :] = jnp.sum(jnp.where(diag, gsk_ref[...], 0.0), axis=0, keepdims=True)
        sm_ref[40:41, :] = loss_ref[...]

        small_dsts = (rwg_ref, rsm_ref)

        def small_src(a, block):
            return pwg_ref.at[block] if a == 0 else sm_ref

        small_local = [pltpu.make_async_copy(small_src(a, me), small_dsts[a].at[me], sm_loc.at[a]) for a in range(2)]
        for cp in small_local:
            cp.start()
        small_sends = []
        for k in range(1, N_DEV):
            peer, pidx = _peer(k, x, y, c)
            for a in range(2):
                i = 2 * (k - 1) + a
                cp = pltpu.make_async_remote_copy(
                    src_ref=small_src(a, pidx), dst_ref=small_dsts[a].at[me], send_sem=sm_send.at[i],
                    recv_sem=sm_recv.at[i], device_id=peer, device_id_type=pl.DeviceIdType.MESH)
                cp.start()
                small_sends.append(cp)

        owner_sum.start()
        owner_sum.forward()
        gin_ref[...] = owner_sum.finish()

        for k in range(1, N_DEV):
            peer, pidx = _peer(k, x, y, c)
            for a in range(2):
                i = 2 * (k - 1) + a
                pltpu.make_async_remote_copy(
                    src_ref=small_src(a, me), dst_ref=small_dsts[a].at[pidx], send_sem=sm_send.at[i],
                    recv_sem=sm_recv.at[i], device_id=peer, device_id_type=pl.DeviceIdType.MESH).wait_recv()
        for cp in small_sends:
            cp.wait_send()
        for cp in small_local:
            cp.wait()

    hbm = pl.BlockSpec(memory_space=pl.ANY)
    vmem = pl.BlockSpec(memory_space=pltpu.VMEM)
    in_blk = parts_w_in.shape[1:]
    return pl.pallas_call(
        body, name="reduce_grads",
        in_specs=[hbm, hbm] + [vmem] * 5, out_specs=[vmem, hbm, hbm],
        out_shape=[jax.ShapeDtypeStruct(in_blk, F32),
                   jax.ShapeDtypeStruct((N_DEV,) + parts_wg.shape[1:], F32),
                   jax.ShapeDtypeStruct((N_DEV, SMALL_ROWS, 128), F32)],
        scratch_shapes=[pltpu.VMEM((SMALL_ROWS, 128), F32)] + _OwnerSum.scratch(in_blk)
        + [pltpu.SemaphoreType.DMA((2 * (N_DEV - 1),)), pltpu.SemaphoreType.DMA((2 * (N_DEV - 1),)),
           pltpu.SemaphoreType.DMA((2,))],
        compiler_params=_cparams(),
    )(parts_w_in, parts_wg, g_ln, g_bg, g_nw, g_sinks, loss)


def _adamw_math(g, w, m, v):
    nm = ADAM_B1 * m + (1.0 - ADAM_B1) * g
    nv = ADAM_B2 * v + (1.0 - ADAM_B2) * (g * g)
    m_hat = nm / (1.0 - ADAM_B1 ** ADAM_STEP)
    v_hat = nv / (1.0 - ADAM_B2 ** ADAM_STEP)
    return -ADAM_LR * (m_hat / (jnp.sqrt(v_hat) + ADAM_EPS) + ADAM_WD * w), nm, nv


def _adamw_shard_view(g, w, m, v):
    rows, width = g.shape

    def body(g_ref, w_hbm, m_hbm, v_hbm, g_out, d_out, nm_out, nv_out, bufs, outs, sems):
        loads = [pltpu.make_async_copy(src.at[:, 0, :], bufs.at[i], sems.at[i])
                 for i, src in enumerate((w_hbm, m_hbm, v_hbm))]
        for cp in loads:
            cp.start()
        g = g_ref[...]
        for cp in loads:
            cp.wait()
        outs[0] = g
        outs[1], outs[2], outs[3] = _adamw_math(g, bufs[0], bufs[1], bufs[2])
        stores = [pltpu.make_async_copy(outs.at[i], dst.at[:, 0, :], sems.at[3 + i])
                  for i, dst in enumerate((g_out, d_out, nm_out, nv_out))]
        for cp in stores:
            cp.start()
        for cp in stores:
            cp.wait()

    hbm = pl.BlockSpec(memory_space=pl.ANY)
    return pl.pallas_call(
        body, name="adamw_w_in",
        in_specs=[pl.BlockSpec(memory_space=pltpu.VMEM), hbm, hbm, hbm], out_specs=[hbm] * 4,
        out_shape=[jax.ShapeDtypeStruct((rows, 1, width), F32)] * 4,
        scratch_shapes=[pltpu.VMEM((3, rows, width), F32), pltpu.VMEM((4, rows, width), F32),
                        pltpu.SemaphoreType.DMA((7,))],
        compiler_params=_cparams(),
    )(g, w, m, v)


def _adamw_vectors(r_small, r_wg, g_out, params):
    n_par = len(params)

    def body(rsm_ref, rwg_ref, gout_ref, *refs):
        ins, outs = refs[:3 * n_par], refs[3 * n_par:]
        g = rsm_ref[0]
        gwg = rwg_ref[0]
        for j in range(1, N_DEV):
            g = g + rsm_ref[j]
            gwg = gwg + rwg_ref[j]
        outs[4 * n_par][...] = g[40:41]
        grads = [gwg,
                 jnp.concatenate([g[r:r + 1] for r in range(0, 8)], axis=1),
                 jnp.concatenate([g[r:r + 1] for r in range(8, 16)], axis=1),
                 jnp.concatenate([g[16:17], g[17:18]], axis=1),
                 g[24:25],
                 g[32:33, 0:SWA_Q_HEADS],
                 gout_ref[...]]
        for p, gp in enumerate(grads):
            w_ref, m_ref, v_ref = ins[3 * p:3 * p + 3]
            outs[4 * p][...] = gp
            outs[4 * p + 1][...], outs[4 * p + 2][...], outs[4 * p + 3][...] = _adamw_math(
                gp, w_ref[...], m_ref[...], v_ref[...])

    vmem = pl.BlockSpec(memory_space=pltpu.VMEM)
    flat = [t for wmv in params for t in wmv]
    return pl.pallas_call(
        body, name="adamw_vectors",
        in_specs=[vmem] * (3 + len(flat)), out_specs=[vmem] * (4 * n_par + 1),
        out_shape=[jax.ShapeDtypeStruct(wmv[0].shape, F32) for wmv in params for _ in range(4)]
        + [jax.ShapeDtypeStruct((1, 128), F32)],
        compiler_params=_cparams(),
    )(r_small, r_wg, g_out, *flat)


def kernel(x, positions, w_in, gla_w_gate_up, gla_b_gate, attn_sinks, gla_norm_w, w_out, ln_g, ln_b, loss_target, m_w_in, m_gla_w_gate_up, m_gla_b_gate, m_attn_sinks, m_gla_norm_w, m_w_out, m_ln_g, m_ln_b, v_w_in, v_gla_w_gate_up, v_gla_b_gate, v_attn_sinks, v_gla_norm_w, v_w_out, v_ln_g, v_ln_b):
    loss, grad_x, parts_w_in, g_wg, g_bg, g_sinks, g_nw, g_out, g_ln = _local_step(
        x[0], positions[0], _shard_view(w_in), gla_w_gate_up[0], gla_b_gate, attn_sinks[0], gla_norm_w, w_out[0],
        ln_g, ln_b, loss_target[0])

    parts_wg = jnp.transpose(g_wg.reshape(GLA_RANK, N_DEV, 32), (1, 0, 2))
    g_in, r_wg, r_small = _reduce_grads(parts_w_in, parts_wg, g_ln, g_bg, g_nw, g_sinks, loss)

    upd_in = _adamw_shard_view(g_in, _shard_view(w_in), _shard_view(m_w_in), _shard_view(v_w_in))
    upd_in = [jnp.transpose(t, (1, 2, 0)) for t in upd_in]
    vec = _adamw_vectors(r_small, r_wg, g_out, [
        (gla_w_gate_up[0], m_gla_w_gate_up[0], v_gla_w_gate_up[0]), (ln_g, m_ln_g, v_ln_g), (ln_b, m_ln_b, v_ln_b),
        (gla_b_gate, m_gla_b_gate, v_gla_b_gate), (gla_norm_w, m_gla_norm_w, v_gla_norm_w),
        (attn_sinks, m_attn_sinks, v_attn_sinks), (w_out[0], m_w_out[0], v_w_out[0])])

    outs = [vec[28][0, 0], grad_x[None]]
    for kind in range(4):
        u_wg, u_ln_g, u_ln_b, u_bg, u_nw, u_sinks, u_out = (vec[4 * p + kind] for p in range(7))
        outs += [upd_in[kind], u_wg[None], u_bg, u_sinks, u_nw, u_out[None], u_ln_g, u_ln_b]
    return tuple(outs)
```

```python
import jax
import jax.numpy as jnp
from jax import lax
from jax.experimental import pallas as pl
from jax.experimental.pallas import tpu as pltpu

F32 = jnp.float32
MXU_DTYPE = jnp.bfloat16

N_DEV = 8
D_MODEL = 1024
SWA_Q_HEADS = 8
SWA_KV_HEADS = 2
SWA_GROUP = 4
SWA_HEAD_DIM = 64
BLOCK = 128
ROPE_THETA = 500000.0
ROT_DIM = 16
GLA_HEADS = 4
GLA_DK = 64
GLA_DV = 128
GLA_RANK = 16
GLA_TAU = 16.0
GLA_CHUNK = 64
D_IN_PROJ = 2832
D_IN_SHARD = D_IN_PROJ // N_DEV
D_OUT_SHARD = D_MODEL // N_DEV
OFF = (0, 512, 640, 768, 1280, 1536, 1792, 2304, 2816, 2832)
EPS = 1e-5
ALPHA = 2.0 ** 0.25
SWA_SCALE = SWA_HEAD_DIM ** -0.5
GLA_SCALE = GLA_DK ** -0.5
ADAM_LR = 0.001
ADAM_B1 = 0.9
ADAM_B2 = 0.999
ADAM_EPS = 1e-08
ADAM_WD = 0.01
ADAM_STEP = 10
VMEM_LIMIT = 56 * 1024 * 1024

_NT = (((1,), (1,)), ((), ()))
_TN = (((0,), (0,)), ((), ()))


def _mm(a, b):
    return jnp.dot(a, b, preferred_element_type=F32)


def _mm_nt(a, b):
    return lax.dot_general(a, b, _NT, preferred_element_type=F32)


def _mm_tn(a, b):
    return lax.dot_general(a, b, _TN, preferred_element_type=F32)


def _sigmoid(t):
    return 1.0 / (1.0 + jnp.exp(-t))


def _cparams(**kw):
    return pltpu.CompilerParams(vmem_limit_bytes=VMEM_LIMIT, **kw)


def _full(shape):
    return pl.BlockSpec(shape, lambda *_: (0,) * len(shape))


def _rows(tile, width):
    return pl.BlockSpec((tile, width), lambda i: (i, 0))


def _rope_angles(positions):
    half = ROT_DIM // 2
    inv_freq = ROPE_THETA ** (-jnp.arange(half, dtype=F32) / half)
    ang = positions.astype(F32)[:, None] * inv_freq[None, :]
    return jnp.concatenate([jnp.cos(ang), jnp.sin(ang)], axis=1)


def _split3_parts(t):
    hi = t.astype(MXU_DTYPE)
    r1 = t - hi.astype(F32)
    mid = r1.astype(MXU_DTYPE)
    return hi, mid, (r1 - mid.astype(F32)).astype(MXU_DTYPE)


def _rope_tables(cs):
    half = ROT_DIM // 2
    i = lax.broadcasted_iota(jnp.int32, (2 * half, 3 * 128), 0)
    lane = lax.broadcasted_iota(jnp.int32, (2 * half, 3 * 128), 1)
    table, pos = _idiv(lane, 128), lane & (SWA_HEAD_DIM - 1)
    is_c = (table == 0) & (pos < ROT_DIM) & ((pos & (half - 1)) == i)
    is_s1 = (table == 1) & (pos < half) & (pos + half == i)
    is_s2 = (table == 2) & (pos >= half) & (pos < ROT_DIM) & (pos == i)
    sel = jnp.where(is_c | is_s2, 1.0, jnp.where(is_s1, -1.0, 0.0)).astype(MXU_DTYPE)
    hi, mid, lo = _split3_parts(cs)
    t = (_mm(hi, sel) + _mm(mid, sel)) + _mm(lo, sel)
    pos1 = lax.broadcasted_iota(jnp.int32, (1, 128), 1) & (SWA_HEAD_DIM - 1)
    return t[:, 0:128] + jnp.where(pos1 >= ROT_DIM, 1.0, 0.0), t[:, 128:256], t[:, 256:384]


def _rope(t, c, s1, s2):
    return t * c + pltpu.roll(t, 120, 1) * s1 + pltpu.roll(t, 8, 1) * s2


def _rope_t(g, c, s1, s2):
    return g * c + pltpu.roll(g * s1, 8, 1) + pltpu.roll(g * s2, 120, 1)


def _in_proj(x, w_in_t, wg_s, b_gate, cos_sin, w_out_s):
    s = x.shape[0]
    ts = min(512, s)
    nsteps = s // ts
    forward_step = min(3, nsteps - 1)
    widths = [OFF[i + 1] - OFF[i] for i in range(9)]

    def body(x_ref, win_hbm, wgs_ref, bg_ref, cs_ref, wos_ref,
             qa_ref, ka_ref, va_ref, ga_ref, qb_ref, kb_ref, vb_ref, gb_ref, rb_ref, la_ref, oms_ref,
             c_ref, s1_ref, s2_ref, w_ref, wg_ref, wout_ref,
             win_all, wg_all, wout_all, stage, stage_sem, *sems):
        xb = x_ref[...].astype(MXU_DTYPE)
        c, s1, s2 = _rope_tables(cs_ref[...])
        c_ref[...], s1_ref[...], s2_ref[...] = c, s1, s2
        i0 = pl.program_id(0)
        gather = _BlockGather(wout_all, *sems[0:2])

        @pl.when(i0 == 0)
        def _():
            ka_ref[0:BLOCK, :] = jnp.zeros((BLOCK, 128), ka_ref.dtype)
            va_ref[0:BLOCK, :] = jnp.zeros((BLOCK, 128), va_ref.dtype)
            first = (_BlockGather(win_all, *sems[2:4]), _BlockGather(wg_all, *sems[4:6]))
            load = pltpu.make_async_copy(win_hbm.at[:, 0, :], stage, stage_sem)
            load.start()
            wout_all[gather.me] = wos_ref[...].astype(wout_all.dtype)
            wg_all[gather.me] = wgs_ref[...].astype(wg_all.dtype)
            load.wait()
            win_all[gather.me] = stage[...].astype(win_all.dtype)
            for stage_of in ("start", "forward", "finish"):
                for g in first:
                    getattr(g, stage_of)()
            gather.start()
            for j in range(N_DEV):
                w_ref[D_IN_SHARD * j:D_IN_SHARD * (j + 1), :] = win_all[j]
                wg_ref[:, 32 * j:32 * (j + 1)] = wg_all[j]

        @pl.when(i0 == forward_step)
        def _():
            gather.forward()

        @pl.when(i0 == nsteps - 1)
        def _():
            gather.finish()
            for j in range(N_DEV):
                wout_ref[D_OUT_SHARD * j:D_OUT_SHARD * (j + 1), :] = wout_all[j]

        kv_rows = pl.ds(pl.multiple_of(BLOCK + i0 * ts, BLOCK), ts)

        def cols(i):
            return _mm_nt(xb, w_ref[OFF[i]:OFF[i + 1], :])

        qa = cols(0)
        for i in range(4):
            qa_ref[:, 128 * i:128 * (i + 1)] = _rope(qa[:, 128 * i:128 * (i + 1)], c, s1, s2).astype(qa_ref.dtype)
        kv = _mm_nt(xb, w_ref[OFF[1]:OFF[3], :])
        ka_ref[kv_rows, :] = _rope(kv[:, 0:128], c, s1, s2).astype(ka_ref.dtype)
        va_ref[kv_rows, :] = kv[:, 128:256].astype(va_ref.dtype)
        ga_ref[...] = cols(3)
        qb_ref[...] = cols(4)
        kb_ref[...] = cols(5)
        vb_ref[...] = cols(6).astype(vb_ref.dtype)
        gb_ref[...] = cols(7)
        rb = cols(8)
        rb_ref[...] = rb
        logit = _mm(rb.astype(MXU_DTYPE), wg_ref[...]) + bg_ref[...]
        e = jnp.exp(-jnp.abs(logit))
        la_ref[...] = (jnp.minimum(logit, 0.0) - jnp.log(1.0 + e)) / GLA_TAU
        oms_ref[...] = jnp.where(logit >= 0.0, e, 1.0) / (1.0 + e)

    out_shape = [jax.ShapeDtypeStruct((s + BLOCK if i in (1, 2) else s, w), MXU_DTYPE if i in (0, 1, 2, 6) else F32)
                 for i, w in enumerate(widths)]
    out_shape += [jax.ShapeDtypeStruct((s, 256), F32)] * 2 + [jax.ShapeDtypeStruct((s, 128), F32)] * 3
    out_shape += [jax.ShapeDtypeStruct((D_IN_PROJ, D_MODEL), MXU_DTYPE), jax.ShapeDtypeStruct((GLA_RANK, 256), MXU_DTYPE),
                  jax.ShapeDtypeStruct((D_MODEL, D_MODEL), MXU_DTYPE)]
    return pl.pallas_call(
        body, name="in_proj", grid=(nsteps,),
        in_specs=[_rows(ts, D_MODEL), pl.BlockSpec(memory_space=pl.ANY), _full((GLA_RANK, 32)), _full((1, 256)),
                  _rows(ts, ROT_DIM), _full((D_OUT_SHARD, D_MODEL))],
        out_specs=[_full((s + BLOCK, w)) if i in (1, 2) else _rows(ts, w) for i, w in enumerate(widths)]
        + [_rows(ts, 256)] * 2 + [_rows(ts, 128)] * 3
        + [_full((D_IN_PROJ, D_MODEL)), _full((GLA_RANK, 256)), _full((D_MODEL, D_MODEL))],
        out_shape=out_shape,
        scratch_shapes=[pltpu.VMEM((N_DEV, D_IN_SHARD, D_MODEL), MXU_DTYPE), pltpu.VMEM((N_DEV, GLA_RANK, 32), MXU_DTYPE),
                        pltpu.VMEM((N_DEV, D_OUT_SHARD, D_MODEL), MXU_DTYPE),
                        pltpu.VMEM((D_IN_SHARD, D_MODEL), F32), pltpu.SemaphoreType.DMA]
        + 3 * _BlockGather.scratch(),
        compiler_params=_cparams(dimension_semantics=("arbitrary",)),
    )(x, w_in_t, wg_s, b_gate, cos_sin, w_out_s)


SWA_ROWS = SWA_GROUP * BLOCK


def _swa_bias():
    shape = (2, 2 * BLOCK, SWA_ROWS)
    ki = lax.broadcasted_iota(jnp.int32, shape, 1)
    qi = lax.broadcasted_iota(jnp.int32, shape, 2) & (BLOCK - 1)
    first = lax.broadcasted_iota(jnp.int32, shape, 0) == 0
    dist = qi + BLOCK - ki
    ok = (dist >= 0) & (dist < BLOCK) & (jnp.logical_not(first) | (ki >= BLOCK))
    return jnp.where(ok, 0.0, -jnp.inf).astype(F32)


SWA_SUB = 8


def _swa_bias_of(bias_ref, n, b):
    return bias_ref[jnp.minimum(n, 1)] if b == 0 else bias_ref[1]


def _swa_dup(t, j):
    t = t.astype(F32)
    low = lax.broadcasted_iota(jnp.int32, t.shape, 1) < SWA_HEAD_DIM
    keep = low if j == 0 else jnp.logical_not(low)
    return jnp.where(keep, t, pltpu.roll(t, SWA_HEAD_DIM, 1)).astype(MXU_DTYPE)


def _swa_stack(t, j):
    low = lax.broadcasted_iota(jnp.int32, (BLOCK, 128), 1) < SWA_HEAD_DIM
    zero = jnp.zeros((BLOCK, 128), t.dtype)
    blocks = []
    for p in (2 * j, 2 * j + 1):
        tp = t[:, 128 * p:128 * (p + 1)]
        blocks += [jnp.where(low, tp, zero), jnp.where(low, zero, tp)]
    return jnp.concatenate(blocks, axis=0)


def _swa_unstack(t):
    low = lax.broadcasted_iota(jnp.int32, (BLOCK, 128), 1) < SWA_HEAD_DIM
    return [jnp.where(low, t[2 * BLOCK * i:2 * BLOCK * i + BLOCK], t[2 * BLOCK * i + BLOCK:2 * BLOCK * (i + 1)])
            for i in range(2)]


def _swa_sink_row(sink_ref, j):
    lane = lax.broadcasted_iota(jnp.int32, (1, SWA_ROWS), 1)
    row = jnp.full((1, SWA_ROWS), sink_ref[SWA_GROUP * j], F32)
    for r in range(1, SWA_GROUP):
        row = jnp.where(lane >= BLOCK * r, sink_ref[SWA_GROUP * j + r], row)
    return row


def _split3(t):
    return jnp.concatenate(_split3_parts(t), axis=1)


def _row_sums_as_row(t):
    ones = jnp.ones((8, 3 * t.shape[1]), MXU_DTYPE)
    return _mm_nt(ones, _split3(t))[0:1, :]


def _swa_probs_t(qs, kd, bias_t, sink):
    sc = _mm_nt(kd, qs) + bias_t
    m = jnp.maximum(jnp.max(sc, axis=0, keepdims=True), sink)
    p = jnp.exp(sc - m)
    ps = jnp.exp(sink - m)
    rinv = 1.0 / (jnp.sum(p, axis=0, keepdims=True) + ps)
    return p * rinv, ps * rinv


def _swa_fwd(sinks, qa, k_pad, v_pad, ga):
    s = qa.shape[0]
    sub = min(SWA_SUB, s // BLOCK)
    tq = sub * BLOCK

    def body(sink_ref, qa_ref, ga_ref, bias_ref, k_ref, v_ref, attn_ref, cat_ref):
        n = pl.program_id(0)
        for b in range(sub):
            rows = slice(BLOCK * b, BLOCK * (b + 1))
            start = pl.multiple_of((n * sub + b) * BLOCK, BLOCK)
            kw = k_ref[pl.ds(start, 2 * BLOCK), :]
            vw = v_ref[pl.ds(start, 2 * BLOCK), :]
            bias_t = _swa_bias_of(bias_ref, n, b)
            q = qa_ref[rows, :] * SWA_SCALE
            g = ga_ref[rows, :]
            silu = g * _sigmoid(g)
            for j in range(SWA_KV_HEADS):
                qs = _swa_stack(q, j).astype(MXU_DTYPE)
                probs, _ = _swa_probs_t(qs, _swa_dup(kw, j), bias_t, _swa_sink_row(sink_ref, j))
                pairs = _swa_unstack(_mm_tn(probs.astype(MXU_DTYPE), _swa_dup(vw, j)))
                for i in range(2):
                    lanes = slice(128 * (2 * j + i), 128 * (2 * j + i + 1))
                    attn_ref[rows, lanes] = pairs[i]
                    cat_ref[rows, lanes] = (pairs[i] * silu[:, lanes]).astype(cat_ref.dtype)

    return pl.pallas_call(
        body, name="swa_fwd", grid=(s // tq,),
        in_specs=[pl.BlockSpec(memory_space=pltpu.SMEM), _rows(tq, 512), _rows(tq, 512),
                  _full((2, 2 * BLOCK, SWA_ROWS)), _full((s + BLOCK, 128)), _full((s + BLOCK, 128))],
        out_specs=[_rows(tq, 512), _rows(tq, 512)],
        out_shape=[jax.ShapeDtypeStruct((s, 512), F32), jax.ShapeDtypeStruct((s, 512), MXU_DTYPE)],
        compiler_params=_cparams(dimension_semantics=("arbitrary",)),
    )(sinks, qa, ga, _swa_bias(), k_pad, v_pad)


GLA_KW = GLA_HEADS * GLA_DK
GLA_VW = GLA_HEADS * GLA_DV


def _idiv(t, d):
    return t >> (d.bit_length() - 1)


def _chunk_cumsum(t, lower):
    n, w = t.shape
    r = lax.broadcasted_iota(jnp.int32, (n, n), 0)
    c = lax.broadcasted_iota(jnp.int32, (n, n), 1)
    tri = ((_idiv(r, GLA_CHUNK) == _idiv(c, GLA_CHUNK)) & ((r >= c) if lower else (r <= c))).astype(MXU_DTYPE)
    parts = _mm(tri, _split3(t))
    return (parts[:, :w] + parts[:, w:2 * w]) + parts[:, 2 * w:]


def _chunk_last(t):
    n = t.shape[0]
    return jnp.concatenate(
        [jnp.broadcast_to(t[c + GLA_CHUNK - 1:c + GLA_CHUNK, :], (GLA_CHUNK, t.shape[1]))
         for c in range(0, n, GLA_CHUNK)], axis=0)


def _head_stack(t, width):
    head = _idiv(lax.broadcasted_iota(jnp.int32, t.shape, 1), width)
    zero = jnp.zeros_like(t)
    return jnp.concatenate([jnp.where(head == h, t, zero) for h in range(GLA_HEADS)], axis=0)


def _heads_to_rows(t):
    return jnp.concatenate([t[:, GLA_DV * h:GLA_DV * (h + 1)] for h in range(GLA_HEADS)], axis=0)


def _rows_to_heads(t):
    return jnp.concatenate([t[GLA_CHUNK * h:GLA_CHUNK * (h + 1)] for h in range(GLA_HEADS)], axis=1)


def _state_by_head(t):
    srow = _idiv(lax.broadcasted_iota(jnp.int32, (GLA_VW, GLA_KW), 0), GLA_DV)
    slane = _idiv(lax.broadcasted_iota(jnp.int32, (GLA_VW, GLA_KW), 1), GLA_DK)
    return jnp.where(srow == slane, jnp.concatenate([t] * GLA_HEADS, axis=0), jnp.zeros((GLA_VW, GLA_KW), t.dtype))


def _gla_masks():
    row = lax.broadcasted_iota(jnp.int32, (GLA_CHUNK, GLA_KW), 0)
    pos = lax.broadcasted_iota(jnp.int32, (GLA_CHUNK, GLA_KW), 1) & (GLA_CHUNK - 1)
    return pos <= row, pos >= row


def _gla_fwd(qb, kb, vb, la, gb, norm_w):
    s = qb.shape[0]
    tb = min(256, s)
    ch = tb // GLA_CHUNK

    def body(qb_ref, kb_ref, vb_ref, la_ref, gb_ref, nw_ref, o_ref, cat_ref, sp_ref, st_ref):
        @pl.when(pl.program_id(0) == 0)
        def _():
            st_ref[...] = jnp.zeros_like(st_ref)

        causal, _ = _gla_masks()
        nw = nw_ref[...]
        b = _chunk_cumsum(la_ref[...], True)
        bl = _chunk_last(b)
        k = kb_ref[...]
        qd = ((qb_ref[...] * GLA_SCALE) * jnp.exp(b)).astype(MXU_DTYPE)
        ki = (k * jnp.exp(-b)).astype(MXU_DTYPE)
        ke = (k * jnp.exp(bl - b)).astype(MXU_DTYPE)
        dec = jnp.exp(bl)
        v = vb_ref[...].astype(MXU_DTYPE)
        g = gb_ref[...]
        silu = g * _sigmoid(g)
        for ci in range(ch):
            rows = slice(GLA_CHUNK * ci, GLA_CHUNK * (ci + 1))
            qds, kis, kes = (_head_stack(t[rows], GLA_DK) for t in (qd, ki, ke))
            a = jnp.where(causal, _mm_nt(qd[rows], kis), 0.0).astype(MXU_DTYPE)
            st = st_ref[...]
            sp_ref[ci] = st
            o = _mm(a, _head_stack(v[rows], GLA_DV)) + _rows_to_heads(_mm_nt(qds, st.astype(MXU_DTYPE)))
            st_ref[...] = st * dec[rows][0:1] + _mm_tn(_heads_to_rows(v[rows]), kes)
            o_ref[rows, :] = o
            for h in range(GLA_HEADS):
                lv = slice(GLA_DV * h, GLA_DV * (h + 1))
                oh = o[:, lv]
                r = lax.rsqrt(jnp.mean(oh * oh, axis=1, keepdims=True) + EPS)
                cat_ref[rows, lv] = (oh * r * nw * silu[rows, lv]).astype(cat_ref.dtype)

    return pl.pallas_call(
        body, name="gla_fwd", grid=(s // tb,),
        in_specs=[_rows(tb, 256), _rows(tb, 256), _rows(tb, 512), _rows(tb, 256), _rows(tb, 512), _full((1, 128))],
        out_specs=[_rows(tb, 512), _rows(tb, 512), pl.BlockSpec((ch, GLA_DV, 256), lambda i: (i, 0, 0))],
        out_shape=[jax.ShapeDtypeStruct((s, 512), F32), jax.ShapeDtypeStruct((s, 512), MXU_DTYPE),
                   jax.ShapeDtypeStruct((s // GLA_CHUNK, GLA_DV, 256), F32)],
        scratch_shapes=[pltpu.VMEM((GLA_DV, GLA_KW), F32)],
        compiler_params=_cparams(dimension_semantics=("arbitrary",)),
    )(qb, kb, vb, la, gb, norm_w)


def _out_ln_loss(cat_a, cat_b, w_out, x, target, ln_g, ln_b):
    s = x.shape[0]
    ts = min(512, s)
    halves = 2 if ts % 32 == 0 else 1
    th = ts // halves

    def body(ca_ref, cb_ref, w_ref, x_ref, t_ref, g_ref, b_ref,
             loss_ref, gx_ref, da_ref, db_ref, gw_ref, gln_ref):
        @pl.when(pl.program_id(0) == 0)
        def _():
            loss_ref[...] = jnp.zeros_like(loss_ref)
            gw_ref[...] = jnp.zeros_like(gw_ref)
            gln_ref[...] = jnp.zeros_like(gln_ref)

        g = g_ref[...]
        dh16s = []
        for k in range(halves):
            rows = slice(th * k, th * (k + 1))
            mix = _mm(ca_ref[rows, :], w_ref[0:512, :]) + _mm(cb_ref[rows, :], w_ref[512:1024, :])
            h = ALPHA * x_ref[rows, :] + mix
            mu = jnp.mean(h, axis=1, keepdims=True)
            hc = h - mu
            rstd = lax.rsqrt(jnp.mean(hc * hc, axis=1, keepdims=True) + EPS)
            xhat = hc * rstd
            err = xhat * g + b_ref[...] - t_ref[rows, :]
            loss_ref[...] += 0.5 * jnp.sum(jnp.mean(err * err, axis=1, keepdims=True))
            dy = err * (1.0 / D_MODEL)
            gln_ref[0:1, :] += jnp.sum(dy * xhat, axis=0, keepdims=True)
            gln_ref[1:2, :] += jnp.sum(dy, axis=0, keepdims=True)
            dxh = dy * g
            dh = rstd * (dxh - jnp.mean(dxh, axis=1, keepdims=True)
                         - xhat * jnp.mean(dxh * xhat, axis=1, keepdims=True))
            gx_ref[rows, :] = ALPHA * dh
            dh16s.append(dh.astype(MXU_DTYPE))
        for k in range(halves):
            rows = slice(th * k, th * (k + 1))
            da_ref[rows, :] = _mm_nt(dh16s[k], w_ref[0:512, :])
            db_ref[rows, :] = _mm_nt(dh16s[k], w_ref[512:1024, :])
        dh16 = jnp.concatenate(dh16s, axis=0)
        gw_ref[0:512, :] += _mm_tn(ca_ref[...], dh16)
        gw_ref[512:1024, :] += _mm_tn(cb_ref[...], dh16)

    return pl.pallas_call(
        body, name="out_ln_loss", grid=(s // ts,),
        in_specs=[_rows(ts, 512), _rows(ts, 512), _full((D_MODEL, D_MODEL)), _rows(ts, D_MODEL), _rows(ts, D_MODEL),
                  _full((1, D_MODEL)), _full((1, D_MODEL))],
        out_specs=[_full((1, 128)), _rows(ts, D_MODEL), _rows(ts, 512), _rows(ts, 512),
                   _full((D_MODEL, D_MODEL)), _full((2, D_MODEL))],
        out_shape=[jax.ShapeDtypeStruct((1, 128), F32), jax.ShapeDtypeStruct((s, D_MODEL), F32),
                   jax.ShapeDtypeStruct((s, 512), F32), jax.ShapeDtypeStruct((s, 512), F32),
                   jax.ShapeDtypeStruct((D_MODEL, D_MODEL), F32), jax.ShapeDtypeStruct((2, D_MODEL), F32)],
        compiler_params=_cparams(dimension_semantics=("arbitrary",)),
    )(cat_a, cat_b, w_out, x, target, ln_g, ln_b)


def _swa_bwd(sinks, qa, k_pad, v_pad, attn, ga, d_cat_a, rope, parts_w_out):
    s = qa.shape[0]
    sub = min(SWA_SUB, s // BLOCK)
    tq = sub * BLOCK
    nsteps = s // tq
    forward_step = min(1, nsteps - 1)

    def body(sink_ref, qa_ref, ga_ref, at_ref, dc_ref, c_ref, s1_ref, s2_ref, bias_ref, k_ref, v_ref, pout_ref,
             dq_ref, dg_ref, dk_ref, dv_ref, ds_ref, gout_ref, *scratch):
        n = pl.program_id(0)
        owner_sum = _OwnerSum(pout_ref, *scratch)

        @pl.when(n == 0)
        def _():
            dk_ref[...] = jnp.zeros_like(dk_ref)
            dv_ref[...] = jnp.zeros_like(dv_ref)
            ds_ref[...] = jnp.zeros_like(ds_ref)
            owner_sum.start()

        @pl.when(n == forward_step)
        def _():
            owner_sum.forward()

        @pl.when(n == nsteps - 1)
        def _():
            gout_ref[...] = owner_sum.finish()

        low = lax.broadcasted_iota(jnp.int32, (2 * BLOCK, 128), 1) < SWA_HEAD_DIM
        for b in range(sub):
            rows = slice(BLOCK * b, BLOCK * (b + 1))
            start = pl.multiple_of((n * sub + b) * BLOCK, BLOCK)
            kw = k_ref[pl.ds(start, 2 * BLOCK), :]
            vw = v_ref[pl.ds(start, 2 * BLOCK), :]
            bias_t = _swa_bias_of(bias_ref, n, b)
            q = qa_ref[rows, :] * SWA_SCALE
            g = ga_ref[rows, :]
            sg = _sigmoid(g)
            o = at_ref[rows, :]
            dc = dc_ref[rows, :]
            do = dc * (g * sg)
            dg_ref[rows, :] = (dc * o * (sg * (1.0 + g * (1.0 - sg)))).astype(dg_ref.dtype)
            od = do * o
            c, s1, s2 = c_ref[rows, :], s1_ref[rows, :], s2_ref[rows, :]
            dk, dv = [], []
            for j in range(SWA_KV_HEADS):
                kd, vd = _swa_dup(kw, j), _swa_dup(vw, j)
                qs = _swa_stack(q, j).astype(MXU_DTYPE)
                dos = _swa_stack(do, j).astype(MXU_DTYPE)
                probs, psink = _swa_probs_t(qs, kd, bias_t, _swa_sink_row(sink_ref, j))
                delta = _row_sums_as_row(_swa_stack(od, j))
                dsc = (probs * (_mm_nt(vd, dos) - delta)).astype(MXU_DTYPE)
                dsink = psink * delta
                for r in range(SWA_GROUP):
                    h = SWA_GROUP * j + r
                    ds_ref[h:h + 1, :] += jnp.zeros((1, 128), F32) - jnp.sum(dsink[:, BLOCK * r:BLOCK * (r + 1)])
                dq = _swa_unstack(_mm_tn(dsc, kd))
                for i in range(2):
                    lanes = slice(128 * (2 * j + i), 128 * (2 * j + i + 1))
                    dq_ref[rows, lanes] = _rope_t(dq[i] * SWA_SCALE, c, s1, s2).astype(dq_ref.dtype)
                dkj = _mm(dsc, qs)
                dvj = _mm(probs.astype(MXU_DTYPE), dos)
                dk.append(dkj + pltpu.roll(dkj, SWA_HEAD_DIM, 1))
                dv.append(dvj + pltpu.roll(dvj, SWA_HEAD_DIM, 1))
            dk_ref[pl.ds(start, 2 * BLOCK), :] += jnp.where(low, dk[0], dk[1])
            dv_ref[pl.ds(start, 2 * BLOCK), :] += jnp.where(low, dv[0], dv[1])

    out_blk = parts_w_out.shape[1:]
    return pl.pallas_call(
        body, name="swa_bwd", grid=(nsteps,),
        in_specs=[pl.BlockSpec(memory_space=pltpu.SMEM)] + [_rows(tq, 512)] * 4 + [_rows(tq, 128)] * 3
        + [_full((2, 2 * BLOCK, SWA_ROWS))] + [_full((s + BLOCK, 128))] * 2 + [pl.BlockSpec(memory_space=pl.ANY)],
        out_specs=[_rows(tq, 512), _rows(tq, 512), _full((s + BLOCK, 128)), _full((s + BLOCK, 128)),
                   _full((SWA_Q_HEADS, 128)), _full(out_blk)],
        out_shape=[jax.ShapeDtypeStruct((s, 512), MXU_DTYPE), jax.ShapeDtypeStruct((s, 512), MXU_DTYPE),
                   jax.ShapeDtypeStruct((s + BLOCK, 128), F32), jax.ShapeDtypeStruct((s + BLOCK, 128), F32),
                   jax.ShapeDtypeStruct((SWA_Q_HEADS, 128), F32), jax.ShapeDtypeStruct(out_blk, F32)],
        scratch_shapes=_OwnerSum.scratch(out_blk),
        compiler_params=_cparams(dimension_semantics=("arbitrary",)),
    )(sinks, qa, ga, attn, d_cat_a, *rope, _swa_bias(), k_pad, v_pad, parts_w_out)


def _gla_bwd(qb, kb, vb, la, oms, gb, o, sprev, d_cat_b, rb, wg, norm_w):
    s = qb.shape[0]
    tb = min(512, s)
    ch = tb // GLA_CHUNK
    nb = s // tb

    def body(qb_ref, kb_ref, vb_ref, la_ref, oms_ref, gb_ref, o_ref, sp_ref, dc_ref, rb_ref, wg_ref, nw_ref,
             dq_ref, dk_ref, dv_ref, dg_ref, dr_ref, gwg_ref, gbg_ref, gnw_ref, dst_ref):
        @pl.when(pl.program_id(0) == 0)
        def _():
            dst_ref[...] = jnp.zeros_like(dst_ref)
            gwg_ref[...] = jnp.zeros_like(gwg_ref)
            gbg_ref[...] = jnp.zeros_like(gbg_ref)
            gnw_ref[...] = jnp.zeros_like(gnw_ref)

        causal, causal_t = _gla_masks()
        nw = nw_ref[...]
        b = _chunk_cumsum(la_ref[...], True)
        bl = _chunk_last(b)
        eb, enb, ee, dec = jnp.exp(b), jnp.exp(-b), jnp.exp(bl - b), jnp.exp(bl)
        k = kb_ref[...]
        qd = (qb_ref[...] * GLA_SCALE) * eb
        ki = k * enb
        ke = k * ee
        qd16, ki16, ke16 = qd.astype(MXU_DTYPE), ki.astype(MXU_DTYPE), ke.astype(MXU_DTYPE)
        v16 = vb_ref[...].astype(MXU_DTYPE)

        g = gb_ref[...]
        sg = _sigmoid(g)
        silu = g * sg
        dsilu = sg * (1.0 + g * (1.0 - sg))
        gnw = jnp.zeros((1, GLA_DV), F32)
        do = []
        for h in range(GLA_HEADS):
            lv = slice(GLA_DV * h, GLA_DV * (h + 1))
            oh = o_ref[:, lv]
            dch = dc_ref[:, lv]
            r = lax.rsqrt(jnp.mean(oh * oh, axis=1, keepdims=True) + EPS)
            d_on = dch * silu[:, lv]
            dg_ref[:, lv] = (dch * (oh * r * nw) * dsilu[:, lv]).astype(dg_ref.dtype)
            gnw += jnp.sum(d_on * oh * r, axis=0, keepdims=True)
            u = d_on * nw
            do.append(r * u - oh * (r * r * r) * jnp.mean(u * oh, axis=1, keepdims=True))
        gnw_ref[...] += gnw
        do16 = jnp.concatenate(do, axis=1).astype(MXU_DTYPE)

        db, dbl = [None] * ch, [None] * ch
        for ci in reversed(range(ch)):
            rows = slice(GLA_CHUNK * ci, GLA_CHUNK * (ci + 1))
            qds, kis, kes = (_head_stack(t[rows], GLA_DK) for t in (qd16, ki16, ke16))
            vs, dos = _head_stack(v16[rows], GLA_DV), _head_stack(do16[rows], GLA_DV)
            a = jnp.where(causal, _mm_nt(qd16[rows], kis), 0.0).astype(MXU_DTYPE)
            at = jnp.where(causal_t, _mm_nt(ki16[rows], qds), 0.0).astype(MXU_DTYPE)
            da = jnp.where(causal, _mm_nt(do16[rows], vs), 0.0).astype(MXU_DTYPE)
            dat = jnp.where(causal_t, _mm_nt(v16[rows], dos), 0.0).astype(MXU_DTYPE)
            st = sp_ref[ci]
            dst = dst_ref[...]
            dst16 = dst.astype(MXU_DTYPE)
            dv = _mm(at, dos) + _rows_to_heads(_mm_nt(kes, dst16))
            dqd = _mm(da, kis) + _mm(do16[rows], _state_by_head(st.astype(MXU_DTYPE)))
            dki = _mm(dat, qds)
            dke = _mm(v16[rows], _state_by_head(dst16))
            ddec = jnp.sum(dst * st, axis=0, keepdims=True)
            decc = dec[rows][0:1]
            dst_ref[...] = _mm_tn(_heads_to_rows(do16[rows]), qds) + dst * decc
            dq_ref[rows, :] = (dqd * eb[rows] * GLA_SCALE).astype(dq_ref.dtype)
            dk_ref[rows, :] = (dki * enb[rows] + dke * ee[rows]).astype(dk_ref.dtype)
            dv_ref[rows, :] = dv.astype(dv_ref.dtype)
            dke_ke = dke * ke[rows]
            db[ci] = dqd * qd[rows] - dki * ki[rows] - dke_ke
            dbl[ci] = jnp.broadcast_to(jnp.sum(dke_ke, axis=0, keepdims=True) + ddec * decc, (GLA_CHUNK, GLA_KW))

        dla = _chunk_cumsum(jnp.concatenate(db, axis=0), False) + jnp.concatenate(dbl, axis=0)
        dlogit = dla * oms_ref[...] * (1.0 / GLA_TAU)
        dl16 = dlogit.astype(MXU_DTYPE)
        gbg_ref[...] += jnp.sum(dlogit, axis=0, keepdims=True)
        gwg_ref[...] += _mm_tn(rb_ref[...].astype(MXU_DTYPE), dl16)
        dr_ref[...] = _mm_nt(dl16, wg_ref[...]).astype(dr_ref.dtype)

    def rev(width):
        return pl.BlockSpec((tb, width), lambda i: (nb - 1 - i, 0))

    return pl.pallas_call(
        body, name="gla_bwd", grid=(nb,),
        in_specs=[rev(256), rev(256), rev(512), rev(256), rev(256), rev(512), rev(512),
                  pl.BlockSpec((ch, GLA_DV, 256), lambda i: (nb - 1 - i, 0, 0)), rev(512), rev(GLA_RANK),
                  _full((GLA_RANK, 256)), _full((1, 128))],
        out_specs=[rev(256), rev(256), rev(512), rev(512), rev(GLA_RANK),
                   _full((GLA_RANK, 256)), _full((1, 256)), _full((1, 128))],
        out_shape=[jax.ShapeDtypeStruct((s, 256), MXU_DTYPE), jax.ShapeDtypeStruct((s, 256), MXU_DTYPE),
                   jax.ShapeDtypeStruct((s, 512), MXU_DTYPE), jax.ShapeDtypeStruct((s, 512), MXU_DTYPE),
                   jax.ShapeDtypeStruct((s, GLA_RANK), MXU_DTYPE), jax.ShapeDtypeStruct((GLA_RANK, 256), F32),
                   jax.ShapeDtypeStruct((1, 256), F32), jax.ShapeDtypeStruct((1, 128), F32)],
        scratch_shapes=[pltpu.VMEM((GLA_DV, GLA_KW), F32)],
        compiler_params=_cparams(dimension_semantics=("arbitrary",)),
    )(qb, kb, vb, la, oms, gb, o, sprev, d_cat_b, rb, wg, norm_w)


def _dproj_specs(s, ts):
    widths = [OFF[i + 1] - OFF[i] for i in range(9)]
    return ([_full((s + BLOCK, w)) if i in (1, 2) else _rows(ts, w) for i, w in enumerate(widths)]
            + [_rows(ts, 128)] * 3)


def _dproj_tiles(piece_refs, rope_refs, ts, members=(0, 1, 3, 4, 5, 6, 7, 8)):
    kv_rows = slice(None) if ts is None else pl.ds(pl.multiple_of(BLOCK + pl.program_id(0) * ts, BLOCK), ts)
    for i in members:
        if i == 1:
            dk = _rope_t(piece_refs[1][kv_rows, :], *(r[...] for r in rope_refs))
            yield OFF[1], OFF[3], jnp.concatenate([dk, piece_refs[2][kv_rows, :]], axis=1).astype(MXU_DTYPE)
        else:
            yield OFF[i], OFF[i + 1], piece_refs[i][...].astype(MXU_DTYPE)


def _in_proj_bwd_x(gx0, pieces, w_in, rope):
    s = gx0.shape[0]
    ts = min(512, s)

    def body(gx0_ref, *refs):
        w_ref, gx_ref = refs[12:]
        acc = gx0_ref[...]
        for lo, hi, t16 in _dproj_tiles(refs[:9], refs[9:12], ts):
            acc += _mm(t16, w_ref[lo:hi, :])
        gx_ref[...] = acc

    return pl.pallas_call(
        body, name="in_proj_bwd_x", grid=(s // ts,),
        in_specs=[_rows(ts, D_MODEL)] + _dproj_specs(s, ts) + [_full((D_IN_PROJ, D_MODEL))],
        out_specs=_rows(ts, D_MODEL),
        out_shape=jax.ShapeDtypeStruct((s, D_MODEL), F32),
        compiler_params=_cparams(dimension_semantics=("arbitrary",)),
    )(gx0, *pieces, *rope, w_in)


GW_GROUPS = ((0, 1), (3, 4), (5, 6), (7, 8))


def _in_proj_bwd_w(x, pieces, rope, parts_wg, g_ln, g_bg, g_nw, g_sinks, loss):
    s = x.shape[0]
    ts = min(1024, s)
    nt = s // ts
    n_groups = len(GW_GROUPS)
    ends = [OFF[3] if members[-1] == 1 else OFF[members[-1] + 1] for members in GW_GROUPS]
    assert all(ends[g] >= D_IN_SHARD * 2 * (g + 1) for g in range(n_groups))
    blk = (D_IN_SHARD, D_MODEL)

    def body(x_ref, *refs):
        piece_refs, rope_refs = refs[:9], refs[9:12]
        pwg_ref, gln_ref, gbg_ref, gnw_ref, gsk_ref, loss_ref, gin_ref, rwg_ref, rsm_ref = refs[12:21]
        (acc_ref, stage_ref, sib_ref, snd_ref, rcv_ref, mine_ref, sm_ref,
         d2d_send, d2d_recv, ici_send, ici_recv, out_sem, sm_send, sm_recv, sm_loc) = refs[21:]
        g, t = pl.program_id(0), pl.program_id(1)
        x_, y_, c = _mesh_pos()
        me, mychip, sibling = 4 * x_ + 2 * y_ + c, 2 * x_ + y_, (x_, y_, 1 - c)
        small_dsts = (rwg_ref, rsm_ref)

        def small_src(a, block):
            return pwg_ref.at[block] if a == 0 else sm_ref

        def small_copy(k, a, src_block, dst_block, peer):
            i = 2 * (k - 1) + a
            return pltpu.make_async_remote_copy(
                src_ref=small_src(a, src_block), dst_ref=small_dsts[a].at[dst_block], send_sem=sm_send.at[i],
                recv_sem=sm_recv.at[i], device_id=peer, device_id_type=pl.DeviceIdType.MESH)

        def small_local(a):
            return pltpu.make_async_copy(small_src(a, me), small_dsts[a].at[me], sm_loc.at[a])

        @pl.when((g == 0) & (t == 0))
        def _():
            acc_ref[...] = jnp.zeros_like(acc_ref)
            sm_ref[...] = jnp.zeros_like(sm_ref)
            for r in range(D_MODEL // 128):
                sm_ref[r:r + 1, :] = gln_ref[0:1, 128 * r:128 * (r + 1)]
                sm_ref[8 + r:9 + r, :] = gln_ref[1:2, 128 * r:128 * (r + 1)]
            for r in range(2):
                sm_ref[16 + r:17 + r, :] = gbg_ref[0:1, 128 * r:128 * (r + 1)]
            sm_ref[24:25, :] = gnw_ref[...]
            diag = (lax.broadcasted_iota(jnp.int32, gsk_ref.shape, 0)
                    == lax.broadcasted_iota(jnp.int32, gsk_ref.shape, 1))
            sm_ref[32:33, :] = jnp.sum(jnp.where(diag, gsk_ref[...], 0.0), axis=0, keepdims=True)
            sm_ref[40:41, :] = loss_ref[...]
            for a in range(2):
                small_local(a).start()
            for k in range(1, N_DEV):
                peer, pidx = _peer(k, x_, y_, c)
                for a in range(2):
                    small_copy(k, a, pidx, me, peer).start()

        xb = x_ref[...].astype(MXU_DTYPE)
        for gi, members in enumerate(GW_GROUPS):
            @pl.when(g == gi)
            def _(members=members):
                for lo, hi, t16 in _dproj_tiles(piece_refs, rope_refs, None, members):
                    acc_ref[lo:hi, :] += _mm_tn(t16, xb)

        def block_rows(j):
            return acc_ref[D_IN_SHARD * j:D_IN_SHARD * (j + 1), :]

        def d2d(gi):
            return pltpu.make_async_remote_copy(
                src_ref=stage_ref.at[gi % 2], dst_ref=sib_ref.at[gi], send_sem=d2d_send.at[gi],
                recv_sem=d2d_recv.at[gi], device_id=sibling, device_id_type=pl.DeviceIdType.MESH)

        def ici(slot, owner):
            return pltpu.make_async_remote_copy(
                src_ref=snd_ref.at[slot], dst_ref=rcv_ref.at[slot], send_sem=ici_send.at[slot],
                recv_sem=ici_recv.at[slot], device_id=owner, device_id_type=pl.DeviceIdType.MESH)

        def to_sibling(gi):
            if gi >= 2:
                d2d(gi - 2).wait_send()
            for cc in range(2):
                @pl.when(c == cc)
                def _(cc=cc):
                    stage_ref[gi % 2] = block_rows(2 * gi + 1 - cc)
            d2d(gi).start()

        def chip_sum(gi):
            d2d(gi).wait_recv()
            gx, gy = gi // 2, gi % 2
            for cc in range(2):
                @pl.when(c == cc)
                def _(cc=cc):
                    total = block_rows(2 * gi + cc) + sib_ref[gi]

                    @pl.when(mychip == gi)
                    def _():
                        mine_ref[...] = total

                    @pl.when(mychip != gi)
                    def _():
                        slot = jnp.where(x_ == gx, 0, 1) + 2 * jnp.where(y_ == gy, 0, 1) - 1
                        snd_ref[slot] = total.astype(snd_ref.dtype)
                        ici(slot, (gx, gy, c)).start()

        for gi in range(n_groups):
            @pl.when((g == gi) & (t == nt - 1))
            def _(gi=gi):
                to_sibling(gi)
                if gi >= 1:
                    chip_sum(gi - 1)
                if gi == n_groups - 1:
                    chip_sum(gi)
                    total = mine_ref[...]
                    for slot in range(3):
                        ici(slot, sibling).wait_recv()
                        total = total + rcv_ref[slot].astype(F32)
                    mine_ref[...] = total
                    out = pltpu.make_async_copy(mine_ref, gin_ref, out_sem)
                    out.start()
                    for k in range(1, N_DEV):
                        peer, pidx = _peer(k, x_, y_, c)
                        for a in range(2):
                            small_copy(k, a, me, pidx, peer).wait_recv()
                    for k in range(1, N_DEV):
                        peer, pidx = _peer(k, x_, y_, c)
                        for a in range(2):
                            small_copy(k, a, pidx, me, peer).wait_send()
                    for a in range(2):
                        small_local(a).wait()
                    d2d(gi - 1).wait_send()
                    d2d(gi).wait_send()
                    for slot in range(3):
                        ici(slot, sibling).wait_send()
                    out.wait()

    def piece_spec(i, width):
        gi = next(k for k, members in enumerate(GW_GROUPS) if (i in members or (i == 2 and 1 in members)))
        return pl.BlockSpec((ts, width), lambda g, t: (jnp.where(g == gi, t, jnp.where(g < gi, 0, nt - 1)), 0))

    widths = [OFF[i + 1] - OFF[i] for i in range(9)]
    hbm = pl.BlockSpec(memory_space=pl.ANY)
    vmem = pl.BlockSpec(memory_space=pltpu.VMEM)
    rope_spec = pl.BlockSpec((ts, 128), lambda g, t: (jnp.where(g == 0, t, nt - 1), 0))
    return pl.pallas_call(
        body, name="in_proj_bwd_w", grid=(n_groups, nt),
        in_specs=[pl.BlockSpec((ts, D_MODEL), lambda g, t: (t, 0))] + [piece_spec(i, w) for i, w in enumerate(widths)]
        + [rope_spec] * 3 + [hbm] + [vmem] * 5,
        out_specs=[hbm, hbm, hbm],
        out_shape=[jax.ShapeDtypeStruct(blk, F32), jax.ShapeDtypeStruct((N_DEV,) + parts_wg.shape[1:], F32),
                   jax.ShapeDtypeStruct((N_DEV, SMALL_ROWS, 128), F32)],
        scratch_shapes=[pltpu.VMEM((D_IN_PROJ, D_MODEL), F32), pltpu.VMEM((2,) + blk, F32),
                        pltpu.VMEM((n_groups,) + blk, F32), pltpu.VMEM((3,) + blk, MXU_DTYPE),
                        pltpu.VMEM((3,) + blk, MXU_DTYPE), pltpu.VMEM(blk, F32), pltpu.VMEM((SMALL_ROWS, 128), F32),
                        pltpu.SemaphoreType.DMA((n_groups,)), pltpu.SemaphoreType.DMA((n_groups,)),
                        pltpu.SemaphoreType.DMA((3,)), pltpu.SemaphoreType.DMA((3,)), pltpu.SemaphoreType.DMA,
                        pltpu.SemaphoreType.DMA((2 * (N_DEV - 1),)), pltpu.SemaphoreType.DMA((2 * (N_DEV - 1),)),
                        pltpu.SemaphoreType.DMA((2,))],
        compiler_params=_cparams(dimension_semantics=("arbitrary", "arbitrary")),
    )(x, *pieces, *rope, parts_wg, g_ln, g_bg, g_nw, g_sinks, loss)


def _local_step(x, positions, w_in_t, wg_s, b_gate, sinks, norm_w, w_out_s, ln_g, ln_b, target):
    qa, k_pad, v_pad, ga, qb, kb, vb, gb, rb, la, oms, *rope, w_in, wg, w_out = _in_proj(
        x, w_in_t, wg_s, b_gate, _rope_angles(positions), w_out_s)
    attn, cat_a = _swa_fwd(sinks, qa, k_pad, v_pad, ga)
    o, cat_b, sprev = _gla_fwd(qb, kb, vb, la, gb, norm_w)
    loss, gx0, d_cat_a, d_cat_b, g_w_out, g_ln = _out_ln_loss(cat_a, cat_b, w_out, x, target, ln_g, ln_b)
    parts_w_out = g_w_out.reshape(N_DEV, D_OUT_SHARD, D_MODEL)
    dqa, dga, dk_pad, dv_pad, g_sinks, g_out = _swa_bwd(sinks, qa, k_pad, v_pad, attn, ga, d_cat_a, rope, parts_w_out)
    dqb, dkb, dvb, dgb, drb, g_wg, g_bg, g_nw = _gla_bwd(qb, kb, vb, la, oms, gb, o, sprev, d_cat_b, rb, wg, norm_w)
    pieces = (dqa, dk_pad, dv_pad, dga, dqb, dkb, dvb, dgb, drb)
    grad_x = _in_proj_bwd_x(gx0, pieces, w_in, rope)
    pieces = (dqa, dk_pad[BLOCK:], dv_pad[BLOCK:], dga, dqb, dkb, dvb, dgb, drb)
    parts_wg = jnp.transpose(g_wg.reshape(GLA_RANK, N_DEV, 32), (1, 0, 2))
    g_in, r_wg, r_small = _in_proj_bwd_w(x, pieces, rope, parts_wg, g_ln, g_bg, g_nw, g_sinks, loss)
    return grad_x, g_in, g_out, r_wg, r_small


def _mesh_pos():
    return lax.axis_index("x"), lax.axis_index("y"), lax.axis_index("c")


def _peer(k, x, y, c):
    px = (1 - x) if k & 4 else x
    py = (1 - y) if k & 2 else y
    pc = (1 - c) if k & 1 else c
    return (px, py, pc), 4 * px + 2 * py + pc


def _other_chips(x, y):
    return [(1 - x, y), (x, 1 - y), (1 - x, 1 - y)]


def _shard_view(t):
    return jnp.transpose(t, (2, 0, 1))


class _BlockGather:
    def __init__(self, slots, send_sems, recv_sems):
        self.slots, self.send_sems, self.recv_sems = slots, send_sems, recv_sems
        x, y, c = _mesh_pos()
        self.xy, self.c, self.me, self.sibling = (x, y), c, 4 * x + 2 * y + c, (x, y, 1 - c)
        self.chips = _other_chips(x, y)

    @staticmethod
    def scratch():
        return [pltpu.SemaphoreType.DMA((N_DEV - 1,)), pltpu.SemaphoreType.DMA((N_DEV - 1,))]

    def _copy(self, k, block, to):
        return pltpu.make_async_remote_copy(
            src_ref=self.slots.at[block], dst_ref=self.slots.at[block], send_sem=self.send_sems.at[k],
            recv_sem=self.recv_sems.at[k], device_id=to, device_id_type=pl.DeviceIdType.MESH)

    def start(self):
        for j, (cx, cy) in enumerate(self.chips):
            self._copy(1 + j, self.me, (cx, cy, self.c)).start()
        self._copy(0, self.me, self.sibling).start()

    def forward(self):
        for j, (cx, cy) in enumerate(self.chips):
            block = 4 * cx + 2 * cy + self.c
            self._copy(1 + j, block, self.sibling).wait_recv()
            self._copy(4 + j, block, self.sibling).start()

    def finish(self):
        x, y = self.xy
        self._copy(0, 4 * x + 2 * y + (1 - self.c), self.sibling).wait_recv()
        for j, (cx, cy) in enumerate(self.chips):
            self._copy(4 + j, 4 * cx + 2 * cy + (1 - self.c), self.sibling).wait_recv()
        for k in range(N_DEV - 1):
            self._copy(k, self.me, self.sibling).wait_send()


class _OwnerSum:
    def __init__(self, parts, own, sib, snd, rcv, loc_sems, d2d_send, d2d_recv, ici_send, ici_recv):
        self.parts, self.own, self.sib, self.snd, self.rcv = parts, own, sib, snd, rcv
        self.sems = (loc_sems, d2d_send, d2d_recv, ici_send, ici_recv)
        x, y, c = _mesh_pos()
        self.c, self.sibling = c, (x, y, 1 - c)
        self.chips = [(x, y)] + _other_chips(x, y)

    @staticmethod
    def scratch(block):
        return [pltpu.VMEM((4,) + block, F32), pltpu.VMEM((4,) + block, F32),
                pltpu.VMEM((3,) + block, MXU_DTYPE), pltpu.VMEM((3,) + block, MXU_DTYPE),
                pltpu.SemaphoreType.DMA((4,)), pltpu.SemaphoreType.DMA((4,)), pltpu.SemaphoreType.DMA((4,)),
                pltpu.SemaphoreType.DMA((3,)), pltpu.SemaphoreType.DMA((3,))]

    def _local(self, r):
        cx, cy = self.chips[r]
        return pltpu.make_async_copy(self.parts.at[4 * cx + 2 * cy + self.c], self.own.at[r], self.sems[0].at[r])

    def _d2d(self, r):
        cx, cy = self.chips[r]
        return pltpu.make_async_remote_copy(
            src_ref=self.parts.at[4 * cx + 2 * cy + (1 - self.c)], dst_ref=self.sib.at[r], send_sem=self.sems[1].at[r],
            recv_sem=self.sems[2].at[r], device_id=self.sibling, device_id_type=pl.DeviceIdType.MESH)

    def _ici(self, r):
        cx, cy = self.chips[r]
        return pltpu.make_async_remote_copy(
            src_ref=self.snd.at[r - 1], dst_ref=self.rcv.at[r - 1], send_sem=self.sems[3].at[r - 1],
            recv_sem=self.sems[4].at[r - 1], device_id=(cx, cy, self.c), device_id_type=pl.DeviceIdType.MESH)

    def start(self):
        for r in (1, 2, 3, 0):
            self._local(r).start()
            self._d2d(r).start()

    def forward(self):
        for r in (1, 2, 3):
            self._local(r).wait()
            self._d2d(r).wait_recv()
            self.snd[r - 1] = (self.own[r] + self.sib[r]).astype(self.snd.dtype)
            self._ici(r).start()

    def finish(self):
        self._local(0).wait()
        self._d2d(0).wait_recv()
        acc = self.own[0] + self.sib[0]
        for r in (1, 2, 3):
            self._ici(r).wait_recv()
            acc = acc + self.rcv[r - 1].astype(F32)
        for r in range(4):
            self._d2d(r).wait_send()
        for r in (1, 2, 3):
            self._ici(r).wait_send()
        return acc


SMALL_ROWS = 48


def _reduce_grads(parts_w_in, parts_wg, g_ln, g_bg, g_nw, g_sinks, loss):
    def body(pin_ref, pwg_ref, gln_ref, gbg_ref, gnw_ref, gsk_ref, loss_ref, gin_ref, rwg_ref, rsm_ref, sm_ref,
             *scratch):
        sm_send, sm_recv, sm_loc = scratch[-3:]
        x, y, c = _mesh_pos()
        me = 4 * x + 2 * y + c
        owner_sum = _OwnerSum(pin_ref, *scratch[:-3])

        sm_ref[...] = jnp.zeros_like(sm_ref)
        for r in range(D_MODEL // 128):
            sm_ref[r:r + 1, :] = gln_ref[0:1, 128 * r:128 * (r + 1)]
            sm_ref[8 + r:9 + r, :] = gln_ref[1:2, 128 * r:128 * (r + 1)]
        for r in range(2):
            sm_ref[16 + r:17 + r, :] = gbg_ref[0:1, 128 * r:128 * (r + 1)]
        sm_ref[24:25, :] = gnw_ref[...]
        diag = lax.broadcasted_iota(jnp.int32, gsk_ref.shape, 0) == lax.broadcasted_iota(jnp.int32, gsk_ref.shape, 1)
        sm_ref[32:33, :] = jnp.sum(jnp.where(diag, gsk_ref[...], 0.0), axis=0, keepdims=True)
        sm_ref[40:41, :] = loss_ref[...]

        small_dsts = (rwg_ref, rsm_ref)

        def small_src(a, block):
            return pwg_ref.at[block] if a == 0 else sm_ref

        small_local = [pltpu.make_async_copy(small_src(a, me), small_dsts[a].at[me], sm_loc.at[a]) for a in range(2)]
        for cp in small_local:
            cp.start()
        small_sends = []
        for k in range(1, N_DEV):
            peer, pidx = _peer(k, x, y, c)
            for a in range(2):
                i = 2 * (k - 1) + a
                cp = pltpu.make_async_remote_copy(
                    src_ref=small_src(a, pidx), dst_ref=small_dsts[a].at[me], send_sem=sm_send.at[i],
                    recv_sem=sm_recv.at[i], device_id=peer, device_id_type=pl.DeviceIdType.MESH)
                cp.start()
                small_sends.append(cp)

        owner_sum.start()
        owner_sum.forward()
        gin_ref[...] = owner_sum.finish()

        for k in range(1, N_DEV):
            peer, pidx = _peer(k, x, y, c)
            for a in range(2):
                i = 2 * (k - 1) + a
                pltpu.make_async_remote_copy(
                    src_ref=small_src(a, me), dst_ref=small_dsts[a].at[pidx], send_sem=sm_send.at[i],
                    recv_sem=sm_recv.at[i], device_id=peer, device_id_type=pl.DeviceIdType.MESH).wait_recv()
        for cp in small_sends:
            cp.wait_send()
        for cp in small_local:
            cp.wait()

    hbm = pl.BlockSpec(memory_space=pl.ANY)
    vmem = pl.BlockSpec(memory_space=pltpu.VMEM)
    in_blk = parts_w_in.shape[1:]
    return pl.pallas_call(
        body, name="reduce_grads",
        in_specs=[hbm, hbm] + [vmem] * 5, out_specs=[vmem, hbm, hbm],
        out_shape=[jax.ShapeDtypeStruct(in_blk, F32),
                   jax.ShapeDtypeStruct((N_DEV,) + parts_wg.shape[1:], F32),
                   jax.ShapeDtypeStruct((N_DEV, SMALL_ROWS, 128), F32)],
        scratch_shapes=[pltpu.VMEM((SMALL_ROWS, 128), F32)] + _OwnerSum.scratch(in_blk)
        + [pltpu.SemaphoreType.DMA((2 * (N_DEV - 1),)), pltpu.SemaphoreType.DMA((2 * (N_DEV - 1),)),
           pltpu.SemaphoreType.DMA((2,))],
        compiler_params=_cparams(),
    )(parts_w_in, parts_wg, g_ln, g_bg, g_nw, g_sinks, loss)


def _adamw_math(g, w, m, v):
    nm = ADAM_B1 * m + (1.0 - ADAM_B1) * g
    nv = ADAM_B2 * v + (1.0 - ADAM_B2) * (g * g)
    m_hat = nm / (1.0 - ADAM_B1 ** ADAM_STEP)
    v_hat = nv / (1.0 - ADAM_B2 ** ADAM_STEP)
    return -ADAM_LR * (m_hat / (jnp.sqrt(v_hat) + ADAM_EPS) + ADAM_WD * w), nm, nv


def _adamw_shard_view(g, w, m, v):
    rows, width = g.shape

    def body(g_ref, w_hbm, m_hbm, v_hbm, g_out, d_out, nm_out, nv_out, bufs, outs, sems):
        loads = [pltpu.make_async_copy(src.at[:, 0, :], bufs.at[i], sems.at[i])
                 for i, src in enumerate((w_hbm, m_hbm, v_hbm))]
        for cp in loads:
            cp.start()
        g = g_ref[...]
        for cp in loads:
            cp.wait()
        outs[0] = g
        outs[1], outs[2], outs[3] = _adamw_math(g, bufs[0], bufs[1], bufs[2])
        stores = [pltpu.make_async_copy(outs.at[i], dst.at[:, 0, :], sems.at[3 + i])
                  for i, dst in enumerate((g_out, d_out, nm_out, nv_out))]
        for cp in stores:
            cp.start()
        for cp in stores:
            cp.wait()

    hbm = pl.BlockSpec(memory_space=pl.ANY)
    return pl.pallas_call(
        body, name="adamw_w_in",
        in_specs=[pl.BlockSpec(memory_space=pltpu.VMEM), hbm, hbm, hbm], out_specs=[hbm] * 4,
        out_shape=[jax.ShapeDtypeStruct((rows, 1, width), F32)] * 4,
        scratch_shapes=[pltpu.VMEM((3, rows, width), F32), pltpu.VMEM((4, rows, width), F32),
                        pltpu.SemaphoreType.DMA((7,))],
        compiler_params=_cparams(),
    )(g, w, m, v)


def _adamw_vectors(r_small, r_wg, g_out, params):
    n_par = len(params)

    def body(rsm_ref, rwg_ref, gout_ref, *refs):
        ins, outs = refs[:3 * n_par], refs[3 * n_par:]
        g = rsm_ref[0]
        gwg = rwg_ref[0]
        for j in range(1, N_DEV):
            g = g + rsm_ref[j]
            gwg = gwg + rwg_ref[j]
        outs[4 * n_par][...] = g[40:41]
        grads = [gwg,
                 jnp.concatenate([g[r:r + 1] for r in range(0, 8)], axis=1),
                 jnp.concatenate([g[r:r + 1] for r in range(8, 16)], axis=1),
                 jnp.concatenate([g[16:17], g[17:18]], axis=1),
                 g[24:25],
                 g[32:33, 0:SWA_Q_HEADS],
                 gout_ref[...]]
        for p, gp in enumerate(grads):
            w_ref, m_ref, v_ref = ins[3 * p:3 * p + 3]
            outs[4 * p][...] = gp
            outs[4 * p + 1][...], outs[4 * p + 2][...], outs[4 * p + 3][...] = _adamw_math(
                gp, w_ref[...], m_ref[...], v_ref[...])

    vmem = pl.BlockSpec(memory_space=pltpu.VMEM)
    flat = [t for wmv in params for t in wmv]
    return pl.pallas_call(
        body, name="adamw_vectors",
        in_specs=[vmem] * (3 + len(flat)), out_specs=[vmem] * (4 * n_par + 1),
        out_shape=[jax.ShapeDtypeStruct(wmv[0].shape, F32) for wmv in params for _ in range(4)]
        + [jax.ShapeDtypeStruct((1, 128), F32)],
        compiler_params=_cparams(),
    )(r_small, r_wg, g_out, *flat)


def kernel(x, positions, w_in, gla_w_gate_up, gla_b_gate, attn_sinks, gla_norm_w, w_out, ln_g, ln_b, loss_target, m_w_in, m_gla_w_gate_up, m_gla_b_gate, m_attn_sinks, m_gla_norm_w, m_w_out, m_ln_g, m_ln_b, v_w_in, v_gla_w_gate_up, v_gla_b_gate, v_attn_sinks, v_gla_norm_w, v_w_out, v_ln_g, v_ln_b):
    grad_x, g_in, g_out, r_wg, r_small = _local_step(
        x[0], positions[0], _shard_view(w_in), gla_w_gate_up[0], gla_b_gate, attn_sinks[0], gla_norm_w, w_out[0],
        ln_g, ln_b, loss_target[0])

    upd_in = _adamw_shard_view(g_in, _shard_view(w_in), _shard_view(m_w_in), _shard_view(v_w_in))
    upd_in = [jnp.transpose(t, (1, 2, 0)) for t in upd_in]
    vec = _adamw_vectors(r_small, r_wg, g_out, [
        (gla_w_gate_up[0], m_gla_w_gate_up[0], v_gla_w_gate_up[0]), (ln_g, m_ln_g, v_ln_g), (ln_b, m_ln_b, v_ln_b),
        (gla_b_gate, m_gla_b_gate, v_gla_b_gate), (gla_norm_w, m_gla_norm_w, v_gla_norm_w),
        (attn_sinks, m_attn_sinks, v_attn_sinks), (w_out[0], m_w_out[0], v_w_out[0])])

    outs = [vec[28][0, 0], grad_x[None]]
    for kind in range(4):
        u_wg, u_ln_g, u_ln_b, u_bg, u_nw, u_sinks, u_out = (vec[4 * p + kind] for p in range(7))
        outs += [upd_in[kind], u_wg[None], u_bg, u_sinks, u_nw, u_out[None], u_ln_g, u_ln_b]
    return tuple(outs)
```

```python
import jax
import jax.numpy as jnp
from jax import lax
from jax.experimental import pallas as pl
from jax.experimental.pallas import tpu as pltpu

F32 = jnp.float32
MXU_DTYPE = jnp.bfloat16

N_DEV = 8
D_MODEL = 1024
SWA_Q_HEADS = 8
SWA_KV_HEADS = 2
SWA_GROUP = 4
SWA_HEAD_DIM = 64
BLOCK = 128
ROPE_THETA = 500000.0
ROT_DIM = 16
GLA_HEADS = 4
GLA_DK = 64
GLA_DV = 128
GLA_RANK = 16
GLA_TAU = 16.0
GLA_CHUNK = 64
D_IN_PROJ = 2832
D_IN_SHARD = D_IN_PROJ // N_DEV
D_OUT_SHARD = D_MODEL // N_DEV
OFF = (0, 512, 640, 768, 1280, 1536, 1792, 2304, 2816, 2832)
EPS = 1e-5
ALPHA = 2.0 ** 0.25
SWA_SCALE = SWA_HEAD_DIM ** -0.5
GLA_SCALE = GLA_DK ** -0.5
ADAM_LR = 0.001
ADAM_B1 = 0.9
ADAM_B2 = 0.999
ADAM_EPS = 1e-08
ADAM_WD = 0.01
ADAM_STEP = 10
VMEM_LIMIT = 56 * 1024 * 1024

_NT = (((1,), (1,)), ((), ()))
_TN = (((0,), (0,)), ((), ()))


def _mm(a, b):
    return jnp.dot(a, b, preferred_element_type=F32)


def _mm_nt(a, b):
    return lax.dot_general(a, b, _NT, preferred_element_type=F32)


def _mm_tn(a, b):
    return lax.dot_general(a, b, _TN, preferred_element_type=F32)


def _sigmoid(t):
    return 1.0 / (1.0 + jnp.exp(-t))


def _cparams(**kw):
    return pltpu.CompilerParams(vmem_limit_bytes=VMEM_LIMIT, **kw)


def _full(shape):
    return pl.BlockSpec(shape, lambda *_: (0,) * len(shape))


def _rows(tile, width):
    return pl.BlockSpec((tile, width), lambda i: (i, 0))


def _rope_angles(positions):
    half = ROT_DIM // 2
    inv_freq = ROPE_THETA ** (-jnp.arange(half, dtype=F32) / half)
    ang = positions.astype(F32)[:, None] * inv_freq[None, :]
    return jnp.concatenate([jnp.cos(ang), jnp.sin(ang)], axis=1)


def _split3_parts(t):
    hi = t.astype(MXU_DTYPE)
    r1 = t - hi.astype(F32)
    mid = r1.astype(MXU_DTYPE)
    return hi, mid, (r1 - mid.astype(F32)).astype(MXU_DTYPE)


def _rope_tables(cs):
    half = ROT_DIM // 2
    i = lax.broadcasted_iota(jnp.int32, (2 * half, 3 * 128), 0)
    lane = lax.broadcasted_iota(jnp.int32, (2 * half, 3 * 128), 1)
    table, pos = _idiv(lane, 128), lane & (SWA_HEAD_DIM - 1)
    is_c = (table == 0) & (pos < ROT_DIM) & ((pos & (half - 1)) == i)
    is_s1 = (table == 1) & (pos < half) & (pos + half == i)
    is_s2 = (table == 2) & (pos >= half) & (pos < ROT_DIM) & (pos == i)
    sel = jnp.where(is_c | is_s2, 1.0, jnp.where(is_s1, -1.0, 0.0)).astype(MXU_DTYPE)
    hi, mid, lo = _split3_parts(cs)
    t = (_mm(hi, sel) + _mm(mid, sel)) + _mm(lo, sel)
    pos1 = lax.broadcasted_iota(jnp.int32, (1, 128), 1) & (SWA_HEAD_DIM - 1)
    return t[:, 0:128] + jnp.where(pos1 >= ROT_DIM, 1.0, 0.0), t[:, 128:256], t[:, 256:384]


def _rope(t, c, s1, s2):
    return t * c + pltpu.roll(t, 120, 1) * s1 + pltpu.roll(t, 8, 1) * s2


def _rope_t(g, c, s1, s2):
    return g * c + pltpu.roll(g * s1, 8, 1) + pltpu.roll(g * s2, 120, 1)


def _in_proj(x, w_in_t, wg_s, b_gate, cos_sin, w_out_s):
    s = x.shape[0]
    ts = min(512, s)
    nsteps = s // ts
    forward_step = min(3, nsteps - 1)
    widths = [OFF[i + 1] - OFF[i] for i in range(9)]

    def body(x_ref, win_hbm, wgs_ref, bg_ref, cs_ref, wos_ref,
             qa_ref, ka_ref, va_ref, ga_ref, qb_ref, kb_ref, vb_ref, gb_ref, rb_ref, la_ref, oms_ref,
             c_ref, s1_ref, s2_ref, w_ref, wg_ref, wout_ref,
             win_all, wg_all, wout_all, stage, stage_sem, *sems):
        xb = x_ref[...].astype(MXU_DTYPE)
        c, s1, s2 = _rope_tables(cs_ref[...])
        c_ref[...], s1_ref[...], s2_ref[...] = c, s1, s2
        i0 = pl.program_id(0)
        gather = _BlockGather(wout_all, *sems[0:2])

        @pl.when(i0 == 0)
        def _():
            ka_ref[0:BLOCK, :] = jnp.zeros((BLOCK, 128), ka_ref.dtype)
            va_ref[0:BLOCK, :] = jnp.zeros((BLOCK, 128), va_ref.dtype)
            first = (_BlockGather(win_all, *sems[2:4]), _BlockGather(wg_all, *sems[4:6]))
            load = pltpu.make_async_copy(win_hbm.at[:, 0, :], stage, stage_sem)
            load.start()
            wout_all[gather.me] = wos_ref[...].astype(wout_all.dtype)
            wg_all[gather.me] = wgs_ref[...].astype(wg_all.dtype)
            load.wait()
            win_all[gather.me] = stage[...].astype(win_all.dtype)
            for stage_of in ("start", "forward", "finish"):
                for g in first:
                    getattr(g, stage_of)()
            gather.start()
            for j in range(N_DEV):
                w_ref[D_IN_SHARD * j:D_IN_SHARD * (j + 1), :] = win_all[j]
                wg_ref[:, 32 * j:32 * (j + 1)] = wg_all[j]

        @pl.when(i0 == forward_step)
        def _():
            gather.forward()

        @pl.when(i0 == nsteps - 1)
        def _():
            gather.finish()
            for j in range(N_DEV):
                wout_ref[D_OUT_SHARD * j:D_OUT_SHARD * (j + 1), :] = wout_all[j]

        kv_rows = pl.ds(pl.multiple_of(BLOCK + i0 * ts, BLOCK), ts)

        def cols(i):
            return _mm_nt(xb, w_ref[OFF[i]:OFF[i + 1], :])

        qa = cols(0)
        for i in range(4):
            qa_ref[:, 128 * i:128 * (i + 1)] = _rope(qa[:, 128 * i:128 * (i + 1)], c, s1, s2).astype(qa_ref.dtype)
        kv = _mm_nt(xb, w_ref[OFF[1]:OFF[3], :])
        ka_ref[kv_rows, :] = _rope(kv[:, 0:128], c, s1, s2).astype(ka_ref.dtype)
        va_ref[kv_rows, :] = kv[:, 128:256].astype(va_ref.dtype)
        ga_ref[...] = cols(3)
        qb_ref[...] = cols(4)
        kb_ref[...] = cols(5)
        vb_ref[...] = cols(6).astype(vb_ref.dtype)
        gb_ref[...] = cols(7)
        rb = cols(8)
        rb_ref[...] = rb
        logit = _mm(rb.astype(MXU_DTYPE), wg_ref[...]) + bg_ref[...]
        e = jnp.exp(-jnp.abs(logit))
        la_ref[...] = (jnp.minimum(logit, 0.0) - jnp.log(1.0 + e)) / GLA_TAU
        oms_ref[...] = jnp.where(logit >= 0.0, e, 1.0) / (1.0 + e)

    out_shape = [jax.ShapeDtypeStruct((s + BLOCK if i in (1, 2) else s, w), MXU_DTYPE if i in (0, 1, 2, 6) else F32)
                 for i, w in enumerate(widths)]
    out_shape += [jax.ShapeDtypeStruct((s, 256), F32)] * 2 + [jax.ShapeDtypeStruct((s, 128), F32)] * 3
    out_shape += [jax.ShapeDtypeStruct((D_IN_PROJ, D_MODEL), MXU_DTYPE), jax.ShapeDtypeStruct((GLA_RANK, 256), MXU_DTYPE),
                  jax.ShapeDtypeStruct((D_MODEL, D_MODEL), MXU_DTYPE)]
    return pl.pallas_call(
        body, name="in_proj", grid=(nsteps,),
        in_specs=[_rows(ts, D_MODEL), pl.BlockSpec(memory_space=pl.ANY), _full((GLA_RANK, 32)), _full((1, 256)),
                  _rows(ts, ROT_DIM), _full((D_OUT_SHARD, D_MODEL))],
        out_specs=[_full((s + BLOCK, w)) if i in (1, 2) else _rows(ts, w) for i, w in enumerate(widths)]
        + [_rows(ts, 256)] * 2 + [_rows(ts, 128)] * 3
        + [_full((D_IN_PROJ, D_MODEL)), _full((GLA_RANK, 256)), _full((D_MODEL, D_MODEL))],
        out_shape=out_shape,
        scratch_shapes=[pltpu.VMEM((N_DEV, D_IN_SHARD, D_MODEL), MXU_DTYPE), pltpu.VMEM((N_DEV, GLA_RANK, 32), MXU_DTYPE),
                        pltpu.VMEM((N_DEV, D_OUT_SHARD, D_MODEL), MXU_DTYPE),
                        pltpu.VMEM((D_IN_SHARD, D_MODEL), F32), pltpu.SemaphoreType.DMA]
        + 3 * _BlockGather.scratch(),
        compiler_params=_cparams(dimension_semantics=("arbitrary",)),
    )(x, w_in_t, wg_s, b_gate, cos_sin, w_out_s)


SWA_ROWS = SWA_GROUP * BLOCK


def _swa_bias():
    shape = (2, 2 * BLOCK, SWA_ROWS)
    ki = lax.broadcasted_iota(jnp.int32, shape, 1)
    qi = lax.broadcasted_iota(jnp.int32, shape, 2) & (BLOCK - 1)
    first = lax.broadcasted_iota(jnp.int32, shape, 0) == 0
    dist = qi + BLOCK - ki
    ok = (dist >= 0) & (dist < BLOCK) & (jnp.logical_not(first) | (ki >= BLOCK))
    return jnp.where(ok, 0.0, -jnp.inf).astype(F32)


SWA_SUB = 8


def _swa_bias_of(bias_ref, n, b):
    return bias_ref[jnp.minimum(n, 1)] if b == 0 else bias_ref[1]


def _swa_dup(t, j):
    t = t.astype(F32)
    low = lax.broadcasted_iota(jnp.int32, t.shape, 1) < SWA_HEAD_DIM
    keep = low if j == 0 else jnp.logical_not(low)
    return jnp.where(keep, t, pltpu.roll(t, SWA_HEAD_DIM, 1)).astype(MXU_DTYPE)


def _swa_stack(t, j):
    low = lax.broadcasted_iota(jnp.int32, (BLOCK, 128), 1) < SWA_HEAD_DIM
    zero = jnp.zeros((BLOCK, 128), t.dtype)
    blocks = []
    for p in (2 * j, 2 * j + 1):
        tp = t[:, 128 * p:128 * (p + 1)]
        blocks += [jnp.where(low, tp, zero), jnp.where(low, zero, tp)]
    return jnp.concatenate(blocks, axis=0)


def _swa_unstack(t):
    low = lax.broadcasted_iota(jnp.int32, (BLOCK, 128), 1) < SWA_HEAD_DIM
    return [jnp.where(low, t[2 * BLOCK * i:2 * BLOCK * i + BLOCK], t[2 * BLOCK * i + BLOCK:2 * BLOCK * (i + 1)])
            for i in range(2)]


def _swa_sink_row(sink_ref, j):
    lane = lax.broadcasted_iota(jnp.int32, (1, SWA_ROWS), 1)
    row = jnp.full((1, SWA_ROWS), sink_ref[SWA_GROUP * j], F32)
    for r in range(1, SWA_GROUP):
        row = jnp.where(lane >= BLOCK * r, sink_ref[SWA_GROUP * j + r], row)
    return row


def _split3(t):
    return jnp.concatenate(_split3_parts(t), axis=1)


def _row_sums_as_row(t):
    ones = jnp.ones((8, 3 * t.shape[1]), MXU_DTYPE)
    return _mm_nt(ones, _split3(t))[0:1, :]


def _swa_probs_t(qs, kd, bias_t, sink):
    sc = _mm_nt(kd, qs) + bias_t
    m = jnp.maximum(jnp.max(sc, axis=0, keepdims=True), sink)
    p = jnp.exp(sc - m)
    ps = jnp.exp(sink - m)
    rinv = 1.0 / (jnp.sum(p, axis=0, keepdims=True) + ps)
    return p * rinv, ps * rinv


def _swa_fwd(sinks, qa, k_pad, v_pad, ga):
    s = qa.shape[0]
    sub = min(SWA_SUB, s // BLOCK)
    tq = sub * BLOCK

    def body(sink_ref, qa_ref, ga_ref, bias_ref, k_ref, v_ref, attn_ref, cat_ref):
        n = pl.program_id(0)
        for b in range(sub):
            rows = slice(BLOCK * b, BLOCK * (b + 1))
            start = pl.multiple_of((n * sub + b) * BLOCK, BLOCK)
            kw = k_ref[pl.ds(start, 2 * BLOCK), :]
            vw = v_ref[pl.ds(start, 2 * BLOCK), :]
            bias_t = _swa_bias_of(bias_ref, n, b)
            q = qa_ref[rows, :] * SWA_SCALE
            g = ga_ref[rows, :]
            silu = g * _sigmoid(g)
            for j in range(SWA_KV_HEADS):
                qs = _swa_stack(q, j).astype(MXU_DTYPE)
                probs, _ = _swa_probs_t(qs, _swa_dup(kw, j), bias_t, _swa_sink_row(sink_ref, j))
                pairs = _swa_unstack(_mm_tn(probs.astype(MXU_DTYPE), _swa_dup(vw, j)))
                for i in range(2):
                    lanes = slice(128 * (2 * j + i), 128 * (2 * j + i + 1))
                    attn_ref[rows, lanes] = pairs[i]
                    cat_ref[rows, lanes] = (pairs[i] * silu[:, lanes]).astype(cat_ref.dtype)

    return pl.pallas_call(
        body, name="swa_fwd", grid=(s // tq,),
        in_specs=[pl.BlockSpec(memory_space=pltpu.SMEM), _rows(tq, 512), _rows(tq, 512),
                  _full((2, 2 * BLOCK, SWA_ROWS)), _full((s + BLOCK, 128)), _full((s + BLOCK, 128))],
        out_specs=[_rows(tq, 512), _rows(tq, 512)],
        out_shape=[jax.ShapeDtypeStruct((s, 512), F32), jax.ShapeDtypeStruct((s, 512), MXU_DTYPE)],
        compiler_params=_cparams(dimension_semantics=("arbitrary",)),
    )(sinks, qa, ga, _swa_bias(), k_pad, v_pad)


GLA_KW = GLA_HEADS * GLA_DK
GLA_VW = GLA_HEADS * GLA_DV


def _idiv(t, d):
    return t >> (d.bit_length() - 1)


def _chunk_cumsum(t, lower):
    n, w = t.shape
    r = lax.broadcasted_iota(jnp.int32, (n, n), 0)
    c = lax.broadcasted_iota(jnp.int32, (n, n), 1)
    tri = ((_idiv(r, GLA_CHUNK) == _idiv(c, GLA_CHUNK)) & ((r >= c) if lower else (r <= c))).astype(MXU_DTYPE)
    parts = _mm(tri, _split3(t))
    return (parts[:, :w] + parts[:, w:2 * w]) + parts[:, 2 * w:]


def _chunk_last(t):
    n = t.shape[0]
    return jnp.concatenate(
        [jnp.broadcast_to(t[c + GLA_CHUNK - 1:c + GLA_CHUNK, :], (GLA_CHUNK, t.shape[1]))
         for c in range(0, n, GLA_CHUNK)], axis=0)


def _head_stack(t, width):
    head = _idiv(lax.broadcasted_iota(jnp.int32, t.shape, 1), width)
    zero = jnp.zeros_like(t)
    return jnp.concatenate([jnp.where(head == h, t, zero) for h in range(GLA_HEADS)], axis=0)


def _heads_to_rows(t):
    return jnp.concatenate([t[:, GLA_DV * h:GLA_DV * (h + 1)] for h in range(GLA_HEADS)], axis=0)


def _rows_to_heads(t):
    return jnp.concatenate([t[GLA_CHUNK * h:GLA_CHUNK * (h + 1)] for h in range(GLA_HEADS)], axis=1)


def _state_by_head(t):
    srow = _idiv(lax.broadcasted_iota(jnp.int32, (GLA_VW, GLA_KW), 0), GLA_DV)
    slane = _idiv(lax.broadcasted_iota(jnp.int32, (GLA_VW, GLA_KW), 1), GLA_DK)
    return jnp.where(srow == slane, jnp.concatenate([t] * GLA_HEADS, axis=0), jnp.zeros((GLA_VW, GLA_KW), t.dtype))


def _gla_masks():
    row = lax.broadcasted_iota(jnp.int32, (GLA_CHUNK, GLA_KW), 0)
    pos = lax.broadcasted_iota(jnp.int32, (GLA_CHUNK, GLA_KW), 1) & (GLA_CHUNK - 1)
    return pos <= row, pos >= row


def _gla_fwd(qb, kb, vb, la, gb, norm_w):
    s = qb.shape[0]
    tb = min(256, s)
    ch = tb // GLA_CHUNK

    def body(qb_ref, kb_ref, vb_ref, la_ref, gb_ref, nw_ref, o_ref, cat_ref, sp_ref, st_ref):
        @pl.when(pl.program_id(0) == 0)
        def _():
            st_ref[...] = jnp.zeros_like(st_ref)

        causal, _ = _gla_masks()
        nw = nw_ref[...]
        b = _chunk_cumsum(la_ref[...], True)
        bl = _chunk_last(b)
        k = kb_ref[...]
        qd = ((qb_ref[...] * GLA_SCALE) * jnp.exp(b)).astype(MXU_DTYPE)
        ki = (k * jnp.exp(-b)).astype(MXU_DTYPE)
        ke = (k * jnp.exp(bl - b)).astype(MXU_DTYPE)
        dec = jnp.exp(bl)
        v = vb_ref[...].astype(MXU_DTYPE)
        g = gb_ref[...]
        silu = g * _sigmoid(g)
        for ci in range(ch):
            rows = slice(GLA_CHUNK * ci, GLA_CHUNK * (ci + 1))
            qds, kis, kes = (_head_stack(t[rows], GLA_DK) for t in (qd, ki, ke))
            a = jnp.where(causal, _mm_nt(qd[rows], kis), 0.0).astype(MXU_DTYPE)
            st = st_ref[...]
            sp_ref[ci] = st
            o = _mm(a, _head_stack(v[rows], GLA_DV)) + _rows_to_heads(_mm_nt(qds, st.astype(MXU_DTYPE)))
            st_ref[...] = st * dec[rows][0:1] + _mm_tn(_heads_to_rows(v[rows]), kes)
            o_ref[rows, :] = o
            for h in range(GLA_HEADS):
                lv = slice(GLA_DV * h, GLA_DV * (h + 1))
                oh = o[:, lv]
                r = lax.rsqrt(jnp.mean(oh * oh, axis=1, keepdims=True) + EPS)
                cat_ref[rows, lv] = (oh * r * nw * silu[rows, lv]).astype(cat_ref.dtype)

    return pl.pallas_call(
        body, name="gla_fwd", grid=(s // tb,),
        in_specs=[_rows(tb, 256), _rows(tb, 256), _rows(tb, 512), _rows(tb, 256), _rows(tb, 512), _full((1, 128))],
        out_specs=[_rows(tb, 512), _rows(tb, 512), pl.BlockSpec((ch, GLA_DV, 256), lambda i: (i, 0, 0))],
        out_shape=[jax.ShapeDtypeStruct((s, 512), F32), jax.ShapeDtypeStruct((s, 512), MXU_DTYPE),
                   jax.ShapeDtypeStruct((s // GLA_CHUNK, GLA_DV, 256), F32)],
        scratch_shapes=[pltpu.VMEM((GLA_DV, GLA_KW), F32)],
        compiler_params=_cparams(dimension_semantics=("arbitrary",)),
    )(qb, kb, vb, la, gb, norm_w)


def _out_ln_loss(cat_a, cat_b, w_out, x, target, ln_g, ln_b):
    s = x.shape[0]
    ts = min(512, s)
    halves = 2 if ts % 32 == 0 else 1
    th = ts // halves

    def body(ca_ref, cb_ref, w_ref, x_ref, t_ref, g_ref, b_ref,
             loss_ref, gx_ref, da_ref, db_ref, gw_ref, gln_ref):
        @pl.when(pl.program_id(0) == 0)
        def _():
            loss_ref[...] = jnp.zeros_like(loss_ref)
            gw_ref[...] = jnp.zeros_like(gw_ref)
            gln_ref[...] = jnp.zeros_like(gln_ref)

        g = g_ref[...]
        dh16s = []
        for k in range(halves):
            rows = slice(th * k, th * (k + 1))
            mix = _mm(ca_ref[rows, :], w_ref[0:512, :]) + _mm(cb_ref[rows, :], w_ref[512:1024, :])
            h = ALPHA * x_ref[rows, :] + mix
            mu = jnp.mean(h, axis=1, keepdims=True)
            hc = h - mu
            rstd = lax.rsqrt(jnp.mean(hc * hc, axis=1, keepdims=True) + EPS)
            xhat = hc * rstd
            err = xhat * g + b_ref[...] - t_ref[rows, :]
            loss_ref[...] += 0.5 * jnp.sum(jnp.mean(err * err, axis=1, keepdims=True))
            dy = err * (1.0 / D_MODEL)
            gln_ref[0:1, :] += jnp.sum(dy * xhat, axis=0, keepdims=True)
            gln_ref[1:2, :] += jnp.sum(dy, axis=0, keepdims=True)
            dxh = dy * g
            dh = rstd * (dxh - jnp.mean(dxh, axis=1, keepdims=True)
                         - xhat * jnp.mean(dxh * xhat, axis=1, keepdims=True))
            gx_ref[rows, :] = ALPHA * dh
            dh16s.append(dh.astype(MXU_DTYPE))
        for k in range(halves):
            rows = slice(th * k, th * (k + 1))
            da_ref[rows, :] = _mm_nt(dh16s[k], w_ref[0:512, :])
            db_ref[rows, :] = _mm_nt(dh16s[k], w_ref[512:1024, :])
        dh16 = jnp.concatenate(dh16s, axis=0)
        gw_ref[0:512, :] += _mm_tn(ca_ref[...], dh16)
        gw_ref[512:1024, :] += _mm_tn(cb_ref[...], dh16)

    return pl.pallas_call(
        body, name="out_ln_loss", grid=(s // ts,),
        in_specs=[_rows(ts, 512), _rows(ts, 512), _full((D_MODEL, D_MODEL)), _rows(ts, D_MODEL), _rows(ts, D_MODEL),
                  _full((1, D_MODEL)), _full((1, D_MODEL))],
        out_specs=[_full((1, 128)), _rows(ts, D_MODEL), _rows(ts, 512), _rows(ts, 512),
                   _full((D_MODEL, D_MODEL)), _full((2, D_MODEL))],
        out_shape=[jax.ShapeDtypeStruct((1, 128), F32), jax.ShapeDtypeStruct((s, D_MODEL), F32),
                   jax.ShapeDtypeStruct((s, 512), F32), jax.ShapeDtypeStruct((s, 512), F32),
                   jax.ShapeDtypeStruct((D_MODEL, D_MODEL), F32), jax.ShapeDtypeStruct((2, D_MODEL), F32)],
        compiler_params=_cparams(dimension_semantics=("arbitrary",)),
    )(cat_a, cat_b, w_out, x, target, ln_g, ln_b)


def _swa_bwd(sinks, qa, k_pad, v_pad, attn, ga, d_cat_a, rope, parts_w_out):
    s = qa.shape[0]
    sub = min(SWA_SUB, s // BLOCK)
    tq = sub * BLOCK
    nsteps = s // tq
    forward_step = min(1, nsteps - 1)

    def body(sink_ref, qa_ref, ga_ref, at_ref, dc_ref, c_ref, s1_ref, s2_ref, bias_ref, k_ref, v_ref, pout_ref,
             dq_ref, dg_ref, dk_out, dv_out, ds_ref, gout_ref, dk_ref, dv_ref, *scratch):
        n = pl.program_id(0)
        owner_sum = _OwnerSum(pout_ref, *scratch)

        @pl.when(n == 0)
        def _():
            dk_ref[...] = jnp.zeros_like(dk_ref)
            dv_ref[...] = jnp.zeros_like(dv_ref)
            ds_ref[...] = jnp.zeros_like(ds_ref)
            owner_sum.start()

        @pl.when(n == forward_step)
        def _():
            owner_sum.forward()

        @pl.when(n == nsteps - 1)
        def _():
            gout_ref[...] = owner_sum.finish()

        low = lax.broadcasted_iota(jnp.int32, (2 * BLOCK, 128), 1) < SWA_HEAD_DIM
        for b in range(sub):
            rows = slice(BLOCK * b, BLOCK * (b + 1))
            start = pl.multiple_of((n * sub + b) * BLOCK, BLOCK)
            kw = k_ref[pl.ds(start, 2 * BLOCK), :]
            vw = v_ref[pl.ds(start, 2 * BLOCK), :]
            bias_t = _swa_bias_of(bias_ref, n, b)
            q = qa_ref[rows, :] * SWA_SCALE
            g = ga_ref[rows, :]
            sg = _sigmoid(g)
            o = at_ref[rows, :]
            dc = dc_ref[rows, :]
            do = dc * (g * sg)
            dg_ref[rows, :] = (dc * o * (sg * (1.0 + g * (1.0 - sg)))).astype(dg_ref.dtype)
            od = do * o
            c, s1, s2 = c_ref[rows, :], s1_ref[rows, :], s2_ref[rows, :]
            dk, dv = [], []
            for j in range(SWA_KV_HEADS):
                kd, vd = _swa_dup(kw, j), _swa_dup(vw, j)
                qs = _swa_stack(q, j).astype(MXU_DTYPE)
                dos = _swa_stack(do, j).astype(MXU_DTYPE)
                probs, psink = _swa_probs_t(qs, kd, bias_t, _swa_sink_row(sink_ref, j))
                delta = _row_sums_as_row(_swa_stack(od, j))
                dsc = (probs * (_mm_nt(vd, dos) - delta)).astype(MXU_DTYPE)
                dsink = psink * delta
                for r in range(SWA_GROUP):
                    h = SWA_GROUP * j + r
                    ds_ref[h:h + 1, :] += jnp.zeros((1, 128), F32) - jnp.sum(dsink[:, BLOCK * r:BLOCK * (r + 1)])
                dq = _swa_unstack(_mm_tn(dsc, kd))
                for i in range(2):
                    lanes = slice(128 * (2 * j + i), 128 * (2 * j + i + 1))
                    dq_ref[rows, lanes] = _rope_t(dq[i] * SWA_SCALE, c, s1, s2).astype(dq_ref.dtype)
                dkj = _mm(dsc, qs)
                dvj = _mm(probs.astype(MXU_DTYPE), dos)
                dk.append(dkj + pltpu.roll(dkj, SWA_HEAD_DIM, 1))
                dv.append(dvj + pltpu.roll(dvj, SWA_HEAD_DIM, 1))
            dk_ref[pl.ds(start, 2 * BLOCK), :] += jnp.where(low, dk[0], dk[1])
            dv_ref[pl.ds(start, 2 * BLOCK), :] += jnp.where(low, dv[0], dv[1])

        @pl.when(n == nsteps - 1)
        def _():
            dk_out[...] = dk_ref[BLOCK:, :]
            dv_out[...] = dv_ref[BLOCK:, :]

    out_blk = parts_w_out.shape[1:]
    return pl.pallas_call(
        body, name="swa_bwd", grid=(nsteps,),
        in_specs=[pl.BlockSpec(memory_space=pltpu.SMEM)] + [_rows(tq, 512)] * 4 + [_rows(tq, 128)] * 3
        + [_full((2, 2 * BLOCK, SWA_ROWS))] + [_full((s + BLOCK, 128))] * 2 + [pl.BlockSpec(memory_space=pl.ANY)],
        out_specs=[_rows(tq, 512), _rows(tq, 512), _full((s, 128)), _full((s, 128)),
                   _full((SWA_Q_HEADS, 128)), _full(out_blk)],
        out_shape=[jax.ShapeDtypeStruct((s, 512), MXU_DTYPE), jax.ShapeDtypeStruct((s, 512), MXU_DTYPE),
                   jax.ShapeDtypeStruct((s, 128), F32), jax.ShapeDtypeStruct((s, 128), F32),
                   jax.ShapeDtypeStruct((SWA_Q_HEADS, 128), F32), jax.ShapeDtypeStruct(out_blk, F32)],
        scratch_shapes=[pltpu.VMEM((s + BLOCK, 128), F32)] * 2 + _OwnerSum.scratch(out_blk),
        compiler_params=_cparams(dimension_semantics=("arbitrary",)),
    )(sinks, qa, ga, attn, d_cat_a, *rope, _swa_bias(), k_pad, v_pad, parts_w_out)


def _gla_bwd(qb, kb, vb, la, oms, gb, o, sprev, d_cat_b, rb, wg, norm_w):
    s = qb.shape[0]
    tb = min(512, s)
    ch = tb // GLA_CHUNK
    nb = s // tb

    def body(qb_ref, kb_ref, vb_ref, la_ref, oms_ref, gb_ref, o_ref, sp_ref, dc_ref, rb_ref, wg_ref, nw_ref,
             dq_ref, dk_ref, dv_ref, dg_ref, dr_ref, gwg_ref, gbg_ref, gnw_ref, dst_ref):
        @pl.when(pl.program_id(0) == 0)
        def _():
            dst_ref[...] = jnp.zeros_like(dst_ref)
            gwg_ref[...] = jnp.zeros_like(gwg_ref)
            gbg_ref[...] = jnp.zeros_like(gbg_ref)
            gnw_ref[...] = jnp.zeros_like(gnw_ref)

        causal, causal_t = _gla_masks()
        nw = nw_ref[...]
        b = _chunk_cumsum(la_ref[...], True)
        bl = _chunk_last(b)
        eb, enb, ee, dec = jnp.exp(b), jnp.exp(-b), jnp.exp(bl - b), jnp.exp(bl)
        k = kb_ref[...]
        qd = (qb_ref[...] * GLA_SCALE) * eb
        ki = k * enb
        ke = k * ee
        qd16, ki16, ke16 = qd.astype(MXU_DTYPE), ki.astype(MXU_DTYPE), ke.astype(MXU_DTYPE)
        v16 = vb_ref[...].astype(MXU_DTYPE)

        g = gb_ref[...]
        sg = _sigmoid(g)
        silu = g * sg
        dsilu = sg * (1.0 + g * (1.0 - sg))
        gnw = jnp.zeros((1, GLA_DV), F32)
        do = []
        for h in range(GLA_HEADS):
            lv = slice(GLA_DV * h, GLA_DV * (h + 1))
            oh = o_ref[:, lv]
            dch = dc_ref[:, lv]
            r = lax.rsqrt(jnp.mean(oh * oh, axis=1, keepdims=True) + EPS)
            d_on = dch * silu[:, lv]
            dg_ref[:, lv] = (dch * (oh * r * nw) * dsilu[:, lv]).astype(dg_ref.dtype)
            gnw += jnp.sum(d_on * oh * r, axis=0, keepdims=True)
            u = d_on * nw
            do.append(r * u - oh * (r * r * r) * jnp.mean(u * oh, axis=1, keepdims=True))
        gnw_ref[...] += gnw
        do16 = jnp.concatenate(do, axis=1).astype(MXU_DTYPE)

        db, dbl = [None] * ch, [None] * ch
        for ci in reversed(range(ch)):
            rows = slice(GLA_CHUNK * ci, GLA_CHUNK * (ci + 1))
            qds, kis, kes = (_head_stack(t[rows], GLA_DK) for t in (qd16, ki16, ke16))
            vs, dos = _head_stack(v16[rows], GLA_DV), _head_stack(do16[rows], GLA_DV)
            a = jnp.where(causal, _mm_nt(qd16[rows], kis), 0.0).astype(MXU_DTYPE)
            at = jnp.where(causal_t, _mm_nt(ki16[rows], qds), 0.0).astype(MXU_DTYPE)
            da = jnp.where(causal, _mm_nt(do16[rows], vs), 0.0).astype(MXU_DTYPE)
            dat = jnp.where(causal_t, _mm_nt(v16[rows], dos), 0.0).astype(MXU_DTYPE)
            st = sp_ref[ci]
            dst = dst_ref[...]
            dst16 = dst.astype(MXU_DTYPE)
            dv = _mm(at, dos) + _rows_to_heads(_mm_nt(kes, dst16))
            dqd = _mm(da, kis) + _mm(do16[rows], _state_by_head(st.astype(MXU_DTYPE)))
            dki = _mm(dat, qds)
            dke = _mm(v16[rows], _state_by_head(dst16))
            ddec = jnp.sum(dst * st, axis=0, keepdims=True)
            decc = dec[rows][0:1]
            dst_ref[...] = _mm_tn(_heads_to_rows(do16[rows]), qds) + dst * decc
            dq_ref[rows, :] = (dqd * eb[rows] * GLA_SCALE).astype(dq_ref.dtype)
            dk_ref[rows, :] = (dki * enb[rows] + dke * ee[rows]).astype(dk_ref.dtype)
            dv_ref[rows, :] = dv.astype(dv_ref.dtype)
            dke_ke = dke * ke[rows]
            db[ci] = dqd * qd[rows] - dki * ki[rows] - dke_ke
            dbl[ci] = jnp.broadcast_to(jnp.sum(dke_ke, axis=0, keepdims=True) + ddec * decc, (GLA_CHUNK, GLA_KW))

        dla = _chunk_cumsum(jnp.concatenate(db, axis=0), False) + jnp.concatenate(dbl, axis=0)
        dlogit = dla * oms_ref[...] * (1.0 / GLA_TAU)
        dl16 = dlogit.astype(MXU_DTYPE)
        gbg_ref[...] += jnp.sum(dlogit, axis=0, keepdims=True)
        gwg_ref[...] += _mm_tn(rb_ref[...].astype(MXU_DTYPE), dl16)
        dr_ref[...] = _mm_nt(dl16, wg_ref[...]).astype(dr_ref.dtype)

    def rev(width):
        return pl.BlockSpec((tb, width), lambda i: (nb - 1 - i, 0))

    return pl.pallas_call(
        body, name="gla_bwd", grid=(nb,),
        in_specs=[rev(256), rev(256), rev(512), rev(256), rev(256), rev(512), rev(512),
                  pl.BlockSpec((ch, GLA_DV, 256), lambda i: (nb - 1 - i, 0, 0)), rev(512), rev(GLA_RANK),
                  _full((GLA_RANK, 256)), _full((1, 128))],
        out_specs=[rev(256), rev(256), rev(512), rev(512), rev(GLA_RANK),
                   _full((GLA_RANK, 256)), _full((1, 256)), _full((1, 128))],
        out_shape=[jax.ShapeDtypeStruct((s, 256), MXU_DTYPE), jax.ShapeDtypeStruct((s, 256), MXU_DTYPE),
                   jax.ShapeDtypeStruct((s, 512), MXU_DTYPE), jax.ShapeDtypeStruct((s, 512), MXU_DTYPE),
                   jax.ShapeDtypeStruct((s, GLA_RANK), MXU_DTYPE), jax.ShapeDtypeStruct((GLA_RANK, 256), F32),
                   jax.ShapeDtypeStruct((1, 256), F32), jax.ShapeDtypeStruct((1, 128), F32)],
        scratch_shapes=[pltpu.VMEM((GLA_DV, GLA_KW), F32)],
        compiler_params=_cparams(dimension_semantics=("arbitrary",)),
    )(qb, kb, vb, la, oms, gb, o, sprev, d_cat_b, rb, wg, norm_w)


def _dproj_tiles(piece_refs, rope_refs, members=(0, 1, 3, 4, 5, 6, 7, 8)):
    for i in members:
        if i == 1:
            dk = _rope_t(piece_refs[1][...], *(r[...] for r in rope_refs))
            yield OFF[1], OFF[3], jnp.concatenate([dk, piece_refs[2][...]], axis=1).astype(MXU_DTYPE)
        else:
            yield OFF[i], OFF[i + 1], piece_refs[i][...].astype(MXU_DTYPE)


def _in_proj_bwd_x(gx0, pieces, w_in, rope):
    s = gx0.shape[0]
    ts = min(512, s)
    widths = [OFF[i + 1] - OFF[i] for i in range(9)]

    def body(gx0_ref, *refs):
        w_ref, gx_ref = refs[12:]
        acc = gx0_ref[...]
        for lo, hi, t16 in _dproj_tiles(refs[:9], refs[9:12]):
            acc += _mm(t16, w_ref[lo:hi, :])
        gx_ref[...] = acc

    return pl.pallas_call(
        body, name="in_proj_bwd_x", grid=(s // ts,),
        in_specs=[_rows(ts, D_MODEL)] + [_rows(ts, w) for w in widths] + [_rows(ts, 128)] * 3
        + [_full((D_IN_PROJ, D_MODEL))],
        out_specs=_rows(ts, D_MODEL),
        out_shape=jax.ShapeDtypeStruct((s, D_MODEL), F32),
        compiler_params=_cparams(dimension_semantics=("arbitrary",)),
    )(gx0, *pieces, *rope, w_in)


GW_GROUPS = ((0, 1), (3, 4), (5, 6), (7, 8))


def _in_proj_bwd_w(x, pieces, rope, parts_wg, g_ln, g_bg, g_nw, g_sinks, loss):
    s = x.shape[0]
    ts = min(1024, s)
    nt = s // ts
    n_groups = len(GW_GROUPS)
    ends = [OFF[3] if members[-1] == 1 else OFF[members[-1] + 1] for members in GW_GROUPS]
    assert all(ends[g] >= D_IN_SHARD * 2 * (g + 1) for g in range(n_groups))
    blk = (D_IN_SHARD, D_MODEL)

    def body(x_ref, *refs):
        piece_refs, rope_refs = refs[:9], refs[9:12]
        pwg_ref, gln_ref, gbg_ref, gnw_ref, gsk_ref, loss_ref, gin_ref, rwg_ref, rsm_ref = refs[12:21]
        (acc_ref, stage_ref, sib_ref, snd_ref, rcv_ref, mine_ref, sm_ref,
         d2d_send, d2d_recv, ici_send, ici_recv, out_sem, sm_send, sm_recv, sm_loc) = refs[21:]
        g, t = pl.program_id(0), pl.program_id(1)
        x_, y_, c = _mesh_pos()
        me, mychip, sibling = 4 * x_ + 2 * y_ + c, 2 * x_ + y_, (x_, y_, 1 - c)
        small_dsts = (rwg_ref, rsm_ref)

        def small_src(a, block):
            return pwg_ref.at[block] if a == 0 else sm_ref

        def small_copy(k, a, src_block, dst_block, peer):
            i = 2 * (k - 1) + a
            return pltpu.make_async_remote_copy(
                src_ref=small_src(a, src_block), dst_ref=small_dsts[a].at[dst_block], send_sem=sm_send.at[i],
                recv_sem=sm_recv.at[i], device_id=peer, device_id_type=pl.DeviceIdType.MESH)

        def small_local(a):
            return pltpu.make_async_copy(small_src(a, me), small_dsts[a].at[me], sm_loc.at[a])

        @pl.when((g == 0) & (t == 0))
        def _():
            acc_ref[...] = jnp.zeros_like(acc_ref)
            sm_ref[...] = jnp.zeros_like(sm_ref)
            for r in range(D_MODEL // 128):
                sm_ref[r:r + 1, :] = gln_ref[0:1, 128 * r:128 * (r + 1)]
                sm_ref[8 + r:9 + r, :] = gln_ref[1:2, 128 * r:128 * (r + 1)]
            for r in range(2):
                sm_ref[16 + r:17 + r, :] = gbg_ref[0:1, 128 * r:128 * (r + 1)]
            sm_ref[24:25, :] = gnw_ref[...]
            diag = (lax.broadcasted_iota(jnp.int32, gsk_ref.shape, 0)
                    == lax.broadcasted_iota(jnp.int32, gsk_ref.shape, 1))
            sm_ref[32:33, :] = jnp.sum(jnp.where(diag, gsk_ref[...], 0.0), axis=0, keepdims=True)
            sm_ref[40:41, :] = loss_ref[...]
            for a in range(2):
                small_local(a).start()
            for k in range(1, N_DEV):
                peer, pidx = _peer(k, x_, y_, c)
                for a in range(2):
                    small_copy(k, a, pidx, me, peer).start()

        xb = x_ref[...].astype(MXU_DTYPE)
        for gi, members in enumerate(GW_GROUPS):
            @pl.when(g == gi)
            def _(members=members):
                for lo, hi, t16 in _dproj_tiles(piece_refs, rope_refs, members):
                    acc_ref[lo:hi, :] += _mm_tn(t16, xb)

        def block_rows(j):
            return acc_ref[D_IN_SHARD * j:D_IN_SHARD * (j + 1), :]

        def d2d(gi):
            return pltpu.make_async_remote_copy(
                src_ref=stage_ref.at[gi % 2], dst_ref=sib_ref.at[gi], send_sem=d2d_send.at[gi],
                recv_sem=d2d_recv.at[gi], device_id=sibling, device_id_type=pl.DeviceIdType.MESH)

        def ici(slot, owner):
            return pltpu.make_async_remote_copy(
                src_ref=snd_ref.at[slot], dst_ref=rcv_ref.at[slot], send_sem=ici_send.at[slot],
                recv_sem=ici_recv.at[slot], device_id=owner, device_id_type=pl.DeviceIdType.MESH)

        def to_sibling(gi):
            if gi >= 2:
                d2d(gi - 2).wait_send()
            for cc in range(2):
                @pl.when(c == cc)
                def _(cc=cc):
                    stage_ref[gi % 2] = block_rows(2 * gi + 1 - cc)
            d2d(gi).start()

        def chip_sum(gi):
            d2d(gi).wait_recv()
            gx, gy = gi // 2, gi % 2
            for cc in range(2):
                @pl.when(c == cc)
                def _(cc=cc):
                    total = block_rows(2 * gi + cc) + sib_ref[gi]

                    @pl.when(mychip == gi)
                    def _():
                        mine_ref[...] = total

                    @pl.when(mychip != gi)
                    def _():
                        slot = jnp.where(x_ == gx, 0, 1) + 2 * jnp.where(y_ == gy, 0, 1) - 1
                        snd_ref[slot] = total.astype(snd_ref.dtype)
                        ici(slot, (gx, gy, c)).start()

        for gi in range(n_groups):
            if gi >= 1:
                @pl.when((g == gi) & (t == min(1, nt - 1)))
                def _(gi=gi):
                    chip_sum(gi - 1)

            @pl.when((g == gi) & (t == nt - 1))
            def _(gi=gi):
                to_sibling(gi)
                if gi == n_groups - 1:
                    chip_sum(gi)
                    total = mine_ref[...]
                    for slot in range(3):
                        ici(slot, sibling).wait_recv()
                        total = total + rcv_ref[slot].astype(F32)
                    mine_ref[...] = total
                    out = pltpu.make_async_copy(mine_ref, gin_ref, out_sem)
                    out.start()
                    for k in range(1, N_DEV):
                        peer, pidx = _peer(k, x_, y_, c)
                        for a in range(2):
                            small_copy(k, a, me, pidx, peer).wait_recv()
                    for k in range(1, N_DEV):
                        peer, pidx = _peer(k, x_, y_, c)
                        for a in range(2):
                            small_copy(k, a, pidx, me, peer).wait_send()
                    for a in range(2):
                        small_local(a).wait()
                    d2d(gi - 1).wait_send()
                    d2d(gi).wait_send()
                    for slot in range(3):
                        ici(slot, sibling).wait_send()
                    out.wait()

    def piece_spec(i, width):
        gi = next(k for k, members in enumerate(GW_GROUPS) if (i in members or (i == 2 and 1 in members)))
        return pl.BlockSpec((ts, width), lambda g, t: (jnp.where(g == gi, t, jnp.where(g < gi, 0, nt - 1)), 0))

    widths = [OFF[i + 1] - OFF[i] for i in range(9)]
    hbm = pl.BlockSpec(memory_space=pl.ANY)
    vmem = pl.BlockSpec(memory_space=pltpu.VMEM)
    rope_spec = pl.BlockSpec((ts, 128), lambda g, t: (jnp.where(g == 0, t, nt - 1), 0))
    return pl.pallas_call(
        body, name="in_proj_bwd_w", grid=(n_groups, nt),
        in_specs=[pl.BlockSpec((ts, D_MODEL), lambda g, t: (t, 0))] + [piece_spec(i, w) for i, w in enumerate(widths)]
        + [rope_spec] * 3 + [hbm] + [vmem] * 5,
        out_specs=[hbm, hbm, hbm],
        out_shape=[jax.ShapeDtypeStruct(blk, F32), jax.ShapeDtypeStruct((N_DEV,) + parts_wg.shape[1:], F32),
                   jax.ShapeDtypeStruct((N_DEV, SMALL_ROWS, 128), F32)],
        scratch_shapes=[pltpu.VMEM((D_IN_PROJ, D_MODEL), F32), pltpu.VMEM((2,) + blk, F32),
                        pltpu.VMEM((n_groups,) + blk, F32), pltpu.VMEM((3,) + blk, MXU_DTYPE),
                        pltpu.VMEM((3,) + blk, MXU_DTYPE), pltpu.VMEM(blk, F32), pltpu.VMEM((SMALL_ROWS, 128), F32),
                        pltpu.SemaphoreType.DMA((n_groups,)), pltpu.SemaphoreType.DMA((n_groups,)),
                        pltpu.SemaphoreType.DMA((3,)), pltpu.SemaphoreType.DMA((3,)), pltpu.SemaphoreType.DMA,
                        pltpu.SemaphoreType.DMA((2 * (N_DEV - 1),)), pltpu.SemaphoreType.DMA((2 * (N_DEV - 1),)),
                        pltpu.SemaphoreType.DMA((2,))],
        compiler_params=_cparams(dimension_semantics=("arbitrary", "arbitrary")),
    )(x, *pieces, *rope, parts_wg, g_ln, g_bg, g_nw, g_sinks, loss)


def _local_step(x, positions, w_in_t, wg_s, b_gate, sinks, norm_w, w_out_s, ln_g, ln_b, target):
    qa, k_pad, v_pad, ga, qb, kb, vb, gb, rb, la, oms, *rope, w_in, wg, w_out = _in_proj(
        x, w_in_t, wg_s, b_gate, _rope_angles(positions), w_out_s)
    attn, cat_a = _swa_fwd(sinks, qa, k_pad, v_pad, ga)
    o, cat_b, sprev = _gla_fwd(qb, kb, vb, la, gb, norm_w)
    loss, gx0, d_cat_a, d_cat_b, g_w_out, g_ln = _out_ln_loss(cat_a, cat_b, w_out, x, target, ln_g, ln_b)
    parts_w_out = g_w_out.reshape(N_DEV, D_OUT_SHARD, D_MODEL)
    dqa, dga, dka, dva, g_sinks, g_out = _swa_bwd(sinks, qa, k_pad, v_pad, attn, ga, d_cat_a, rope, parts_w_out)
    dqb, dkb, dvb, dgb, drb, g_wg, g_bg, g_nw = _gla_bwd(qb, kb, vb, la, oms, gb, o, sprev, d_cat_b, rb, wg, norm_w)
    pieces = (dqa, dka, dva, dga, dqb, dkb, dvb, dgb, drb)
    grad_x = _in_proj_bwd_x(gx0, pieces, w_in, rope)
    parts_wg = jnp.transpose(g_wg.reshape(GLA_RANK, N_DEV, 32), (1, 0, 2))
    g_in, r_wg, r_small = _in_proj_bwd_w(x, pieces, rope, parts_wg, g_ln, g_bg, g_nw, g_sinks, loss)
    return grad_x, g_in, g_out, r_wg, r_small


def _mesh_pos():
    return lax.axis_index("x"), lax.axis_index("y"), lax.axis_index("c")


def _peer(k, x, y, c):
    px = (1 - x) if k & 4 else x
    py = (1 - y) if k & 2 else y
    pc = (1 - c) if k & 1 else c
    return (px, py, pc), 4 * px + 2 * py + pc


def _other_chips(x, y):
    return [(1 - x, y), (x, 1 - y), (1 - x, 1 - y)]


def _shard_view(t):
    return jnp.transpose(t, (2, 0, 1))


class _BlockGather:
    def __init__(self, slots, send_sems, recv_sems):
        self.slots, self.send_sems, self.recv_sems = slots, send_sems, recv_sems
        x, y, c = _mesh_pos()
        self.xy, self.c, self.me, self.sibling = (x, y), c, 4 * x + 2 * y + c, (x, y, 1 - c)
        self.chips = _other_chips(x, y)

    @staticmethod
    def scratch():
        return [pltpu.SemaphoreType.DMA((N_DEV - 1,)), pltpu.SemaphoreType.DMA((N_DEV - 1,))]

    def _copy(self, k, block, to):
        return pltpu.make_async_remote_copy(
            src_ref=self.slots.at[block], dst_ref=self.slots.at[block], send_sem=self.send_sems.at[k],
            recv_sem=self.recv_sems.at[k], device_id=to, device_id_type=pl.DeviceIdType.MESH)

    def start(self):
        for j, (cx, cy) in enumerate(self.chips):
            self._copy(1 + j, self.me, (cx, cy, self.c)).start()
        self._copy(0, self.me, self.sibling).start()

    def forward(self):
        for j, (cx, cy) in enumerate(self.chips):
            block = 4 * cx + 2 * cy + self.c
            self._copy(1 + j, block, self.sibling).wait_recv()
            self._copy(4 + j, block, self.sibling).start()

    def finish(self):
        x, y = self.xy
        self._copy(0, 4 * x + 2 * y + (1 - self.c), self.sibling).wait_recv()
        for j, (cx, cy) in enumerate(self.chips):
            self._copy(4 + j, 4 * cx + 2 * cy + (1 - self.c), self.sibling).wait_recv()
        for k in range(N_DEV - 1):
            self._copy(k, self.me, self.sibling).wait_send()


class _OwnerSum:
    def __init__(self, parts, own, sib, snd, rcv, loc_sems, d2d_send, d2d_recv, ici_send, ici_recv):
        self.parts, self.own, self.sib, self.snd, self.rcv = parts, own, sib, snd, rcv
        self.sems = (loc_sems, d2d_send, d2d_recv, ici_send, ici_recv)
        x, y, c = _mesh_pos()
        self.c, self.sibling = c, (x, y, 1 - c)
        self.chips = [(x, y)] + _other_chips(x, y)

    @staticmethod
    def scratch(block):
        return [pltpu.VMEM((4,) + block, F32), pltpu.VMEM((4,) + block, F32),
                pltpu.VMEM((3,) + block, MXU_DTYPE), pltpu.VMEM((3,) + block, MXU_DTYPE),
                pltpu.SemaphoreType.DMA((4,)), pltpu.SemaphoreType.DMA((4,)), pltpu.SemaphoreType.DMA((4,)),
                pltpu.SemaphoreType.DMA((3,)), pltpu.SemaphoreType.DMA((3,))]

    def _local(self, r):
        cx, cy = self.chips[r]
        return pltpu.make_async_copy(self.parts.at[4 * cx + 2 * cy + self.c], self.own.at[r], self.sems[0].at[r])

    def _d2d(self, r):
        cx, cy = self.chips[r]
        return pltpu.make_async_remote_copy(
            src_ref=self.parts.at[4 * cx + 2 * cy + (1 - self.c)], dst_ref=self.sib.at[r], send_sem=self.sems[1].at[r],
            recv_sem=self.sems[2].at[r], device_id=self.sibling, device_id_type=pl.DeviceIdType.MESH)

    def _ici(self, r):
        cx, cy = self.chips[r]
        return pltpu.make_async_remote_copy(
            src_ref=self.snd.at[r - 1], dst_ref=self.rcv.at[r - 1], send_sem=self.sems[3].at[r - 1],
            recv_sem=self.sems[4].at[r - 1], device_id=(cx, cy, self.c), device_id_type=pl.DeviceIdType.MESH)

    def start(self):
        for r in (1, 2, 3, 0):
            self._local(r).start()
            self._d2d(r).start()

    def forward(self):
        for r in (1, 2, 3):
            self._local(r).wait()
            self._d2d(r).wait_recv()
            self.snd[r - 1] = (self.own[r] + self.sib[r]).astype(self.snd.dtype)
            self._ici(r).start()

    def finish(self):
        self._local(0).wait()
        self._d2d(0).wait_recv()
        acc = self.own[0] + self.sib[0]
        for r in (1, 2, 3):
            self._ici(r).wait_recv()
            acc = acc + self.rcv[r - 1].astype(F32)
        for r in range(4):
            self._d2d(r).wait_send()
        for r in (1, 2, 3):
            self._ici(r).wait_send()
        return acc


SMALL_ROWS = 48


def _reduce_grads(parts_w_in, parts_wg, g_ln, g_bg, g_nw, g_sinks, loss):
    def body(pin_ref, pwg_ref, gln_ref, gbg_ref, gnw_ref, gsk_ref, loss_ref, gin_ref, rwg_ref, rsm_ref, sm_ref,
             *scratch):
        sm_send, sm_recv, sm_loc = scratch[-3:]
        x, y, c = _mesh_pos()
        me = 4 * x + 2 * y + c
        owner_sum = _OwnerSum(pin_ref, *scratch[:-3])

        sm_ref[...] = jnp.zeros_like(sm_ref)
        for r in range(D_MODEL // 128):
            sm_ref[r:r + 1, :] = gln_ref[0:1, 128 * r:128 * (r + 1)]
            sm_ref[8 + r:9 + r, :] = gln_ref[1:2, 128 * r:128 * (r + 1)]
        for r in range(2):
            sm_ref[16 + r:17 + r, :] = gbg_ref[0:1, 128 * r:128 * (r + 1)]
        sm_ref[24:25, :] = gnw_ref[...]
        diag = lax.broadcasted_iota(jnp.int32, gsk_ref.shape, 0) == lax.broadcasted_iota(jnp.int32, gsk_ref.shape, 1)
        sm_ref[32:33, :] = jnp.sum(jnp.where(diag, gsk_ref[...], 0.0), axis=0, keepdims=True)
        sm_ref[40:41, :] = loss_ref[...]

        small_dsts = (rwg_ref, rsm_ref)

        def small_src(a, block):
            return pwg_ref.at[block] if a == 0 else sm_ref

        small_local = [pltpu.make_async_copy(small_src(a, me), small_dsts[a].at[me], sm_loc.at[a]) for a in range(2)]
        for cp in small_local:
            cp.start()
        small_sends = []
        for k in range(1, N_DEV):
            peer, pidx = _peer(k, x, y, c)
            for a in range(2):
                i = 2 * (k - 1) + a
                cp = pltpu.make_async_remote_copy(
                    src_ref=small_src(a, pidx), dst_ref=small_dsts[a].at[me], send_sem=sm_send.at[i],
                    recv_sem=sm_recv.at[i], device_id=peer, device_id_type=pl.DeviceIdType.MESH)
                cp.start()
                small_sends.append(cp)

        owner_sum.start()
        owner_sum.forward()
        gin_ref[...] = owner_sum.finish()

        for k in range(1, N_DEV):
            peer, pidx = _peer(k, x, y, c)
            for a in range(2):
                i = 2 * (k - 1) + a
                pltpu.make_async_remote_copy(
                    src_ref=small_src(a, me), dst_ref=small_dsts[a].at[pidx], send_sem=sm_send.at[i],
                    recv_sem=sm_recv.at[i], device_id=peer, device_id_type=pl.DeviceIdType.MESH).wait_recv()
        for cp in small_sends:
            cp.wait_send()
        for cp in small_local:
            cp.wait()

    hbm = pl.BlockSpec(memory_space=pl.ANY)
    vmem = pl.BlockSpec(memory_space=pltpu.VMEM)
    in_blk = parts_w_in.shape[1:]
    return pl.pallas_call(
        body, name="reduce_grads",
        in_specs=[hbm, hbm] + [vmem] * 5, out_specs=[vmem, hbm, hbm],
        out_shape=[jax.ShapeDtypeStruct(in_blk, F32),
                   jax.ShapeDtypeStruct((N_DEV,) + parts_wg.shape[1:], F32),
                   jax.ShapeDtypeStruct((N_DEV, SMALL_ROWS, 128), F32)],
        scratch_shapes=[pltpu.VMEM((SMALL_ROWS, 128), F32)] + _OwnerSum.scratch(in_blk)
        + [pltpu.SemaphoreType.DMA((2 * (N_DEV - 1),)), pltpu.SemaphoreType.DMA((2 * (N_DEV - 1),)),
           pltpu.SemaphoreType.DMA((2,))],
        compiler_params=_cparams(),
    )(parts_w_in, parts_wg, g_ln, g_bg, g_nw, g_sinks, loss)


def _adamw_math(g, w, m, v):
    nm = ADAM_B1 * m + (1.0 - ADAM_B1) * g
    nv = ADAM_B2 * v + (1.0 - ADAM_B2) * (g * g)
    m_hat = nm / (1.0 - ADAM_B1 ** ADAM_STEP)
    v_hat = nv / (1.0 - ADAM_B2 ** ADAM_STEP)
    return -ADAM_LR * (m_hat / (jnp.sqrt(v_hat) + ADAM_EPS) + ADAM_WD * w), nm, nv


def _adamw_shard_view(g, w, m, v):
    rows, width = g.shape

    def body(g_ref, w_hbm, m_hbm, v_hbm, g_out, d_out, nm_out, nv_out, bufs, outs, sems):
        loads = [pltpu.make_async_copy(src.at[:, 0, :], bufs.at[i], sems.at[i])
                 for i, src in enumerate((w_hbm, m_hbm, v_hbm))]
        for cp in loads:
            cp.start()
        g = g_ref[...]
        for cp in loads:
            cp.wait()
        outs[0] = g
        outs[1], outs[2], outs[3] = _adamw_math(g, bufs[0], bufs[1], bufs[2])
        stores = [pltpu.make_async_copy(outs.at[i], dst.at[:, 0, :], sems.at[3 + i])
                  for i, dst in enumerate((g_out, d_out, nm_out, nv_out))]
        for cp in stores:
            cp.start()
        for cp in stores:
            cp.wait()

    hbm = pl.BlockSpec(memory_space=pl.ANY)
    return pl.pallas_call(
        body, name="adamw_w_in",
        in_specs=[pl.BlockSpec(memory_space=pltpu.VMEM), hbm, hbm, hbm], out_specs=[hbm] * 4,
        out_shape=[jax.ShapeDtypeStruct((rows, 1, width), F32)] * 4,
        scratch_shapes=[pltpu.VMEM((3, rows, width), F32), pltpu.VMEM((4, rows, width), F32),
                        pltpu.SemaphoreType.DMA((7,))],
        compiler_params=_cparams(),
    )(g, w, m, v)


def _adamw_vectors(r_small, r_wg, g_out, params):
    n_par = len(params)

    def body(rsm_ref, rwg_ref, gout_ref, *refs):
        ins, outs = refs[:3 * n_par], refs[3 * n_par:]
        g = rsm_ref[0]
        gwg = rwg_ref[0]
        for j in range(1, N_DEV):
            g = g + rsm_ref[j]
            gwg = gwg + rwg_ref[j]
        outs[4 * n_par][...] = g[40:41]
        grads = [gwg,
                 jnp.concatenate([g[r:r + 1] for r in range(0, 8)], axis=1),
                 jnp.concatenate([g[r:r + 1] for r in range(8, 16)], axis=1),
                 jnp.concatenate([g[16:17], g[17:18]], axis=1),
                 g[24:25],
                 g[32:33, 0:SWA_Q_HEADS],
                 gout_ref[...]]
        for p, gp in enumerate(grads):
            w_ref, m_ref, v_ref = ins[3 * p:3 * p + 3]
            outs[4 * p][...] = gp
            outs[4 * p + 1][...], outs[4 * p + 2][...], outs[4 * p + 3][...] = _adamw_math(
                gp, w_ref[...], m_ref[...], v_ref[...])

    vmem = pl.BlockSpec(memory_space=pltpu.VMEM)
    flat = [t for wmv in params for t in wmv]
    return pl.pallas_call(
        body, name="adamw_vectors",
        in_specs=[vmem] * (3 + len(flat)), out_specs=[vmem] * (4 * n_par + 1),
        out_shape=[jax.ShapeDtypeStruct(wmv[0].shape, F32) for wmv in params for _ in range(4)]
        + [jax.ShapeDtypeStruct((1, 128), F32)],
        compiler_params=_cparams(),
    )(r_small, r_wg, g_out, *flat)


def kernel(x, positions, w_in, gla_w_gate_up, gla_b_gate, attn_sinks, gla_norm_w, w_out, ln_g, ln_b, loss_target, m_w_in, m_gla_w_gate_up, m_gla_b_gate, m_attn_sinks, m_gla_norm_w, m_w_out, m_ln_g, m_ln_b, v_w_in, v_gla_w_gate_up, v_gla_b_gate, v_attn_sinks, v_gla_norm_w, v_w_out, v_ln_g, v_ln_b):
    grad_x, g_in, g_out, r_wg, r_small = _local_step(
        x[0], positions[0], _shard_view(w_in), gla_w_gate_up[0], gla_b_gate, attn_sinks[0], gla_norm_w, w_out[0],
        ln_g, ln_b, loss_target[0])

    upd_in = _adamw_shard_view(g_in, _shard_view(w_in), _shard_view(m_w_in), _shard_view(v_w_in))
    upd_in = [jnp.transpose(t, (1, 2, 0)) for t in upd_in]
    vec = _adamw_vectors(r_small, r_wg, g_out, [
        (gla_w_gate_up[0], m_gla_w_gate_up[0], v_gla_w_gate_up[0]), (ln_g, m_ln_g, v_ln_g), (ln_b, m_ln_b, v_ln_b),
        (gla_b_gate, m_gla_b_gate, v_gla_b_gate), (gla_norm_w, m_gla_norm_w, v_gla_norm_w),
        (attn_sinks, m_attn_sinks, v_attn_sinks), (w_out[0], m_w_out[0], v_w_out[0])])

    outs = [vec[28][0, 0], grad_x[None]]
    for kind in range(4):
        u_wg, u_ln_g, u_ln_b, u_bg, u_nw, u_sinks, u_out = (vec[4 * p + kind] for p in range(7))
        outs += [upd_in[kind], u_wg[None], u_bg, u_sinks, u_nw, u_out[None], u_ln_g, u_ln_b]
    return tuple(outs)
```

```python
import jax
import jax.numpy as jnp
from jax import lax
from jax.experimental import pallas as pl
from jax.experimental.pallas import tpu as pltpu

F32 = jnp.float32
MXU_DTYPE = jnp.bfloat16

N_DEV = 8
D_MODEL = 1024
SWA_Q_HEADS = 8
SWA_KV_HEADS = 2
SWA_GROUP = 4
SWA_HEAD_DIM = 64
BLOCK = 128
ROPE_THETA = 500000.0
ROT_DIM = 16
GLA_HEADS = 4
GLA_DK = 64
GLA_DV = 128
GLA_RANK = 16
GLA_TAU = 16.0
GLA_CHUNK = 64
D_IN_PROJ = 2832
D_IN_SHARD = D_IN_PROJ // N_DEV
D_OUT_SHARD = D_MODEL // N_DEV
OFF = (0, 512, 640, 768, 1280, 1536, 1792, 2304, 2816, 2832)
EPS = 1e-5
ALPHA = 2.0 ** 0.25
SWA_SCALE = SWA_HEAD_DIM ** -0.5
GLA_SCALE = GLA_DK ** -0.5
ADAM_LR = 0.001
ADAM_B1 = 0.9
ADAM_B2 = 0.999
ADAM_EPS = 1e-08
ADAM_WD = 0.01
ADAM_STEP = 10
VMEM_LIMIT = 56 * 1024 * 1024

_NT = (((1,), (1,)), ((), ()))
_TN = (((0,), (0,)), ((), ()))


def _mm(a, b):
    return jnp.dot(a, b, preferred_element_type=F32)


def _mm_nt(a, b):
    return lax.dot_general(a, b, _NT, preferred_element_type=F32)


def _mm_tn(a, b):
    return lax.dot_general(a, b, _TN, preferred_element_type=F32)


def _sigmoid(t):
    return 1.0 / (1.0 + jnp.exp(-t))


def _cparams(**kw):
    return pltpu.CompilerParams(vmem_limit_bytes=VMEM_LIMIT, **kw)


def _full(shape):
    return pl.BlockSpec(shape, lambda *_: (0,) * len(shape))


def _rows(tile, width):
    return pl.BlockSpec((tile, width), lambda i: (i, 0))


def _rope_angles(positions):
    half = ROT_DIM // 2
    inv_freq = ROPE_THETA ** (-jnp.arange(half, dtype=F32) / half)
    ang = positions.astype(F32)[:, None] * inv_freq[None, :]
    return jnp.concatenate([jnp.cos(ang), jnp.sin(ang)], axis=1)


def _split3_parts(t):
    hi = t.astype(MXU_DTYPE)
    r1 = t - hi.astype(F32)
    mid = r1.astype(MXU_DTYPE)
    return hi, mid, (r1 - mid.astype(F32)).astype(MXU_DTYPE)


def _rope_tables(cs):
    half = ROT_DIM // 2
    i = lax.broadcasted_iota(jnp.int32, (2 * half, 3 * 128), 0)
    lane = lax.broadcasted_iota(jnp.int32, (2 * half, 3 * 128), 1)
    table, pos = _idiv(lane, 128), lane & (SWA_HEAD_DIM - 1)
    is_c = (table == 0) & (pos < ROT_DIM) & ((pos & (half - 1)) == i)
    is_s1 = (table == 1) & (pos < half) & (pos + half == i)
    is_s2 = (table == 2) & (pos >= half) & (pos < ROT_DIM) & (pos == i)
    sel = jnp.where(is_c | is_s2, 1.0, jnp.where(is_s1, -1.0, 0.0)).astype(MXU_DTYPE)
    hi, mid, lo = _split3_parts(cs)
    t = (_mm(hi, sel) + _mm(mid, sel)) + _mm(lo, sel)
    pos1 = lax.broadcasted_iota(jnp.int32, (1, 128), 1) & (SWA_HEAD_DIM - 1)
    return t[:, 0:128] + jnp.where(pos1 >= ROT_DIM, 1.0, 0.0), t[:, 128:256], t[:, 256:384]


def _rope(t, c, s1, s2):
    return t * c + pltpu.roll(t, 120, 1) * s1 + pltpu.roll(t, 8, 1) * s2


def _rope_t(g, c, s1, s2):
    return g * c + pltpu.roll(g * s1, 8, 1) + pltpu.roll(g * s2, 120, 1)


def _in_proj(x, w_in_t, wg_s, b_gate, cos_sin, w_out_s):
    s = x.shape[0]
    ts = min(512, s)
    nsteps = s // ts
    forward_step = min(3, nsteps - 1)
    widths = [OFF[i + 1] - OFF[i] for i in range(9)]

    def body(x_ref, win_hbm, wgs_ref, bg_ref, cs_ref, wos_ref,
             qa_ref, ka_ref, va_ref, ga_ref, qb_ref, kb_ref, vb_ref, gb_ref, rb_ref, la_ref, oms_ref,
             c_ref, s1_ref, s2_ref, w_ref, wg_ref, wout_ref,
             win_all, wg_all, wout_all, stage, stage_sem, *sems):
        xb = x_ref[...].astype(MXU_DTYPE)
        c, s1, s2 = _rope_tables(cs_ref[...])
        c_ref[...], s1_ref[...], s2_ref[...] = c, s1, s2
        i0 = pl.program_id(0)
        gather = _BlockGather(wout_all, *sems[0:2])

        @pl.when(i0 == 0)
        def _():
            ka_ref[0:BLOCK, :] = jnp.zeros((BLOCK, 128), ka_ref.dtype)
            va_ref[0:BLOCK, :] = jnp.zeros((BLOCK, 128), va_ref.dtype)
            first = (_BlockGather(win_all, *sems[2:4]), _BlockGather(wg_all, *sems[4:6]))
            load = pltpu.make_async_copy(win_hbm.at[:, 0, :], stage, stage_sem)
            load.start()
            wout_all[gather.me] = wos_ref[...].astype(wout_all.dtype)
            wg_all[gather.me] = wgs_ref[...].astype(wg_all.dtype)
            load.wait()
            win_all[gather.me] = stage[...].astype(win_all.dtype)
            for stage_of in ("start", "forward", "finish"):
                for g in first:
                    getattr(g, stage_of)()
            gather.start()
            for j in range(N_DEV):
                w_ref[D_IN_SHARD * j:D_IN_SHARD * (j + 1), :] = win_all[j]
                wg_ref[:, 32 * j:32 * (j + 1)] = wg_all[j]

        @pl.when(i0 == forward_step)
        def _():
            gather.forward()

        @pl.when(i0 == nsteps - 1)
        def _():
            gather.finish()
            for j in range(N_DEV):
                wout_ref[D_OUT_SHARD * j:D_OUT_SHARD * (j + 1), :] = wout_all[j]

        kv_rows = pl.ds(pl.multiple_of(BLOCK + i0 * ts, BLOCK), ts)

        def cols(i):
            return _mm_nt(xb, w_ref[OFF[i]:OFF[i + 1], :])

        qa = cols(0)
        for i in range(4):
            qa_ref[:, 128 * i:128 * (i + 1)] = _rope(qa[:, 128 * i:128 * (i + 1)], c, s1, s2).astype(qa_ref.dtype)
        kv = _mm_nt(xb, w_ref[OFF[1]:OFF[3], :])
        ka_ref[kv_rows, :] = _rope(kv[:, 0:128], c, s1, s2).astype(ka_ref.dtype)
        va_ref[kv_rows, :] = kv[:, 128:256].astype(va_ref.dtype)
        ga_ref[...] = cols(3)
        qb_ref[...] = cols(4)
        kb_ref[...] = cols(5)
        vb_ref[...] = cols(6).astype(vb_ref.dtype)
        gb_ref[...] = cols(7)
        rb = cols(8)
        rb_ref[...] = rb
        logit = _mm(rb.astype(MXU_DTYPE), wg_ref[...]) + bg_ref[...]
        e = jnp.exp(-jnp.abs(logit))
        la_ref[...] = (jnp.minimum(logit, 0.0) - jnp.log(1.0 + e)) / GLA_TAU
        oms_ref[...] = jnp.where(logit >= 0.0, e, 1.0) / (1.0 + e)

    out_shape = [jax.ShapeDtypeStruct((s + BLOCK if i in (1, 2) else s, w), MXU_DTYPE if i in (0, 1, 2, 6) else F32)
                 for i, w in enumerate(widths)]
    out_shape += [jax.ShapeDtypeStruct((s, 256), F32)] * 2 + [jax.ShapeDtypeStruct((s, 128), F32)] * 3
    out_shape += [jax.ShapeDtypeStruct((D_IN_PROJ, D_MODEL), MXU_DTYPE), jax.ShapeDtypeStruct((GLA_RANK, 256), MXU_DTYPE),
                  jax.ShapeDtypeStruct((D_MODEL, D_MODEL), MXU_DTYPE)]
    return pl.pallas_call(
        body, name="in_proj", grid=(nsteps,),
        in_specs=[_rows(ts, D_MODEL), pl.BlockSpec(memory_space=pl.ANY), _full((GLA_RANK, 32)), _full((1, 256)),
                  _rows(ts, ROT_DIM), _full((D_OUT_SHARD, D_MODEL))],
        out_specs=[_full((s + BLOCK, w)) if i in (1, 2) else _rows(ts, w) for i, w in enumerate(widths)]
        + [_rows(ts, 256)] * 2 + [_rows(ts, 128)] * 3
        + [_full((D_IN_PROJ, D_MODEL)), _full((GLA_RANK, 256)), _full((D_MODEL, D_MODEL))],
        out_shape=out_shape,
        scratch_shapes=[pltpu.VMEM((N_DEV, D_IN_SHARD, D_MODEL), MXU_DTYPE), pltpu.VMEM((N_DEV, GLA_RANK, 32), MXU_DTYPE),
                        pltpu.VMEM((N_DEV, D_OUT_SHARD, D_MODEL), MXU_DTYPE),
                        pltpu.VMEM((D_IN_SHARD, D_MODEL), F32), pltpu.SemaphoreType.DMA]
        + 3 * _BlockGather.scratch(),
        compiler_params=_cparams(dimension_semantics=("arbitrary",)),
    )(x, w_in_t, wg_s, b_gate, cos_sin, w_out_s)


SWA_ROWS = SWA_GROUP * BLOCK


def _swa_bias():
    shape = (2, 2 * BLOCK, SWA_ROWS)
    ki = lax.broadcasted_iota(jnp.int32, shape, 1)
    qi = lax.broadcasted_iota(jnp.int32, shape, 2) & (BLOCK - 1)
    first = lax.broadcasted_iota(jnp.int32, shape, 0) == 0
    dist = qi + BLOCK - ki
    ok = (dist >= 0) & (dist < BLOCK) & (jnp.logical_not(first) | (ki >= BLOCK))
    return jnp.where(ok, 0.0, -jnp.inf).astype(F32)


SWA_SUB = 8


def _swa_bias_of(bias_ref, n, b):
    return bias_ref[jnp.minimum(n, 1)] if b == 0 else bias_ref[1]


def _swa_dup(t, j):
    t = t.astype(F32)
    low = lax.broadcasted_iota(jnp.int32, t.shape, 1) < SWA_HEAD_DIM
    keep = low if j == 0 else jnp.logical_not(low)
    return jnp.where(keep, t, pltpu.roll(t, SWA_HEAD_DIM, 1)).astype(MXU_DTYPE)


def _swa_stack(t, j):
    low = lax.broadcasted_iota(jnp.int32, (BLOCK, 128), 1) < SWA_HEAD_DIM
    zero = jnp.zeros((BLOCK, 128), t.dtype)
    blocks = []
    for p in (2 * j, 2 * j + 1):
        tp = t[:, 128 * p:128 * (p + 1)]
        blocks += [jnp.where(low, tp, zero), jnp.where(low, zero, tp)]
    return jnp.concatenate(blocks, axis=0)


def _swa_unstack(t):
    low = lax.broadcasted_iota(jnp.int32, (BLOCK, 128), 1) < SWA_HEAD_DIM
    return [jnp.where(low, t[2 * BLOCK * i:2 * BLOCK * i + BLOCK], t[2 * BLOCK * i + BLOCK:2 * BLOCK * (i + 1)])
            for i in range(2)]


def _swa_sink_row(sink_ref, j):
    lane = lax.broadcasted_iota(jnp.int32, (1, SWA_ROWS), 1)
    row = jnp.full((1, SWA_ROWS), sink_ref[SWA_GROUP * j], F32)
    for r in range(1, SWA_GROUP):
        row = jnp.where(lane >= BLOCK * r, sink_ref[SWA_GROUP * j + r], row)
    return row


def _split3(t):
    return jnp.concatenate(_split3_parts(t), axis=1)


def _row_sums_as_row(t):
    ones = jnp.ones((8, 3 * t.shape[1]), MXU_DTYPE)
    return _mm_nt(ones, _split3(t))[0:1, :]


def _swa_probs_t(qs, kd, bias_t, sink):
    sc = _mm_nt(kd, qs) + bias_t
    m = jnp.maximum(jnp.max(sc, axis=0, keepdims=True), sink)
    p = jnp.exp(sc - m)
    ps = jnp.exp(sink - m)
    rinv = 1.0 / (jnp.sum(p, axis=0, keepdims=True) + ps)
    return p * rinv, ps * rinv


def _swa_fwd(sinks, qa, k_pad, v_pad, ga):
    s = qa.shape[0]
    sub = min(SWA_SUB, s // BLOCK)
    tq = sub * BLOCK

    def body(sink_ref, qa_ref, ga_ref, bias_ref, k_ref, v_ref, attn_ref, cat_ref):
        n = pl.program_id(0)
        for b in range(sub):
            rows = slice(BLOCK * b, BLOCK * (b + 1))
            start = pl.multiple_of((n * sub + b) * BLOCK, BLOCK)
            kw = k_ref[pl.ds(start, 2 * BLOCK), :]
            vw = v_ref[pl.ds(start, 2 * BLOCK), :]
            bias_t = _swa_bias_of(bias_ref, n, b)
            q = qa_ref[rows, :] * SWA_SCALE
            g = ga_ref[rows, :]
            silu = g * _sigmoid(g)
            for j in range(SWA_KV_HEADS):
                qs = _swa_stack(q, j).astype(MXU_DTYPE)
                probs, _ = _swa_probs_t(qs, _swa_dup(kw, j), bias_t, _swa_sink_row(sink_ref, j))
                pairs = _swa_unstack(_mm_tn(probs.astype(MXU_DTYPE), _swa_dup(vw, j)))
                for i in range(2):
                    lanes = slice(128 * (2 * j + i), 128 * (2 * j + i + 1))
                    attn_ref[rows, lanes] = pairs[i]
                    cat_ref[rows, lanes] = (pairs[i] * silu[:, lanes]).astype(cat_ref.dtype)

    return pl.pallas_call(
        body, name="swa_fwd", grid=(s // tq,),
        in_specs=[pl.BlockSpec(memory_space=pltpu.SMEM), _rows(tq, 512), _rows(tq, 512),
                  _full((2, 2 * BLOCK, SWA_ROWS)), _full((s + BLOCK, 128)), _full((s + BLOCK, 128))],
        out_specs=[_rows(tq, 512), _rows(tq, 512)],
        out_shape=[jax.ShapeDtypeStruct((s, 512), F32), jax.ShapeDtypeStruct((s, 512), MXU_DTYPE)],
        compiler_params=_cparams(dimension_semantics=("arbitrary",)),
    )(sinks, qa, ga, _swa_bias(), k_pad, v_pad)


GLA_KW = GLA_HEADS * GLA_DK
GLA_VW = GLA_HEADS * GLA_DV


def _idiv(t, d):
    return t >> (d.bit_length() - 1)


def _chunk_cumsum(t, lower):
    n, w = t.shape
    r = lax.broadcasted_iota(jnp.int32, (n, n), 0)
    c = lax.broadcasted_iota(jnp.int32, (n, n), 1)
    tri = ((_idiv(r, GLA_CHUNK) == _idiv(c, GLA_CHUNK)) & ((r >= c) if lower else (r <= c))).astype(MXU_DTYPE)
    parts = _mm(tri, _split3(t))
    return (parts[:, :w] + parts[:, w:2 * w]) + parts[:, 2 * w:]


def _chunk_last(t):
    n = t.shape[0]
    return jnp.concatenate(
        [jnp.broadcast_to(t[c + GLA_CHUNK - 1:c + GLA_CHUNK, :], (GLA_CHUNK, t.shape[1]))
         for c in range(0, n, GLA_CHUNK)], axis=0)


def _head_stack(t, width):
    head = _idiv(lax.broadcasted_iota(jnp.int32, t.shape, 1), width)
    zero = jnp.zeros_like(t)
    return jnp.concatenate([jnp.where(head == h, t, zero) for h in range(GLA_HEADS)], axis=0)


def _heads_to_rows(t):
    return jnp.concatenate([t[:, GLA_DV * h:GLA_DV * (h + 1)] for h in range(GLA_HEADS)], axis=0)


def _rows_to_heads(t):
    return jnp.concatenate([t[GLA_CHUNK * h:GLA_CHUNK * (h + 1)] for h in range(GLA_HEADS)], axis=1)


def _state_by_head(t):
    srow = _idiv(lax.broadcasted_iota(jnp.int32, (GLA_VW, GLA_KW), 0), GLA_DV)
    slane = _idiv(lax.broadcasted_iota(jnp.int32, (GLA_VW, GLA_KW), 1), GLA_DK)
    return jnp.where(srow == slane, jnp.concatenate([t] * GLA_HEADS, axis=0), jnp.zeros((GLA_VW, GLA_KW), t.dtype))


def _gla_masks():
    row = lax.broadcasted_iota(jnp.int32, (GLA_CHUNK, GLA_KW), 0)
    pos = lax.broadcasted_iota(jnp.int32, (GLA_CHUNK, GLA_KW), 1) & (GLA_CHUNK - 1)
    return pos <= row, pos >= row


def _gla_fwd(qb, kb, vb, la, gb, norm_w):
    s = qb.shape[0]
    tb = min(256, s)
    ch = tb // GLA_CHUNK

    def body(qb_ref, kb_ref, vb_ref, la_ref, gb_ref, nw_ref, o_ref, cat_ref, sp_ref, st_ref):
        @pl.when(pl.program_id(0) == 0)
        def _():
            st_ref[...] = jnp.zeros_like(st_ref)

        causal, _ = _gla_masks()
        nw = nw_ref[...]
        b = _chunk_cumsum(la_ref[...], True)
        bl = _chunk_last(b)
        k = kb_ref[...]
        qd = ((qb_ref[...] * GLA_SCALE) * jnp.exp(b)).astype(MXU_DTYPE)
        ki = (k * jnp.exp(-b)).astype(MXU_DTYPE)
        ke = (k * jnp.exp(bl - b)).astype(MXU_DTYPE)
        dec = jnp.exp(bl)
        v = vb_ref[...].astype(MXU_DTYPE)
        g = gb_ref[...]
        silu = g * _sigmoid(g)
        for ci in range(ch):
            rows = slice(GLA_CHUNK * ci, GLA_CHUNK * (ci + 1))
            qds, kis, kes = (_head_stack(t[rows], GLA_DK) for t in (qd, ki, ke))
            a = jnp.where(causal, _mm_nt(qd[rows], kis), 0.0).astype(MXU_DTYPE)
            st = st_ref[...]
            sp_ref[ci] = st
            o = _mm(a, _head_stack(v[rows], GLA_DV)) + _rows_to_heads(_mm_nt(qds, st.astype(MXU_DTYPE)))
            st_ref[...] = st * dec[rows][0:1] + _mm_tn(_heads_to_rows(v[rows]), kes)
            o_ref[rows, :] = o
            for h in range(GLA_HEADS):
                lv = slice(GLA_DV * h, GLA_DV * (h + 1))
                oh = o[:, lv]
                r = lax.rsqrt(jnp.mean(oh * oh, axis=1, keepdims=True) + EPS)
                cat_ref[rows, lv] = (oh * r * nw * silu[rows, lv]).astype(cat_ref.dtype)

    return pl.pallas_call(
        body, name="gla_fwd", grid=(s // tb,),
        in_specs=[_rows(tb, 256), _rows(tb, 256), _rows(tb, 512), _rows(tb, 256), _rows(tb, 512), _full((1, 128))],
        out_specs=[_rows(tb, 512), _rows(tb, 512), pl.BlockSpec((ch, GLA_DV, 256), lambda i: (i, 0, 0))],
        out_shape=[jax.ShapeDtypeStruct((s, 512), F32), jax.ShapeDtypeStruct((s, 512), MXU_DTYPE),
                   jax.ShapeDtypeStruct((s // GLA_CHUNK, GLA_DV, 256), F32)],
        scratch_shapes=[pltpu.VMEM((GLA_DV, GLA_KW), F32)],
        compiler_params=_cparams(dimension_semantics=("arbitrary",)),
    )(qb, kb, vb, la, gb, norm_w)


def _out_ln_loss(cat_a, cat_b, w_out, x, target, ln_g, ln_b):
    s = x.shape[0]
    ts = min(512, s)
    halves = 2 if ts % 32 == 0 else 1
    th = ts // halves

    def body(ca_ref, cb_ref, w_ref, x_ref, t_ref, g_ref, b_ref,
             loss_ref, gx_ref, da_ref, db_ref, gw_ref, gln_ref):
        @pl.when(pl.program_id(0) == 0)
        def _():
            loss_ref[...] = jnp.zeros_like(loss_ref)
            gw_ref[...] = jnp.zeros_like(gw_ref)
            gln_ref[...] = jnp.zeros_like(gln_ref)

        g = g_ref[...]
        dh16s = []
        for k in range(halves):
            rows = slice(th * k, th * (k + 1))
            mix = _mm(ca_ref[rows, :], w_ref[0:512, :]) + _mm(cb_ref[rows, :], w_ref[512:1024, :])
            h = ALPHA * x_ref[rows, :] + mix
            mu = jnp.mean(h, axis=1, keepdims=True)
            hc = h - mu
            rstd = lax.rsqrt(jnp.mean(hc * hc, axis=1, keepdims=True) + EPS)
            xhat = hc * rstd
            err = xhat * g + b_ref[...] - t_ref[rows, :]
            loss_ref[...] += 0.5 * jnp.sum(jnp.mean(err * err, axis=1, keepdims=True))
            dy = err * (1.0 / D_MODEL)
            gln_ref[0:1, :] += jnp.sum(dy * xhat, axis=0, keepdims=True)
            gln_ref[1:2, :] += jnp.sum(dy, axis=0, keepdims=True)
            dxh = dy * g
            dh = rstd * (dxh - jnp.mean(dxh, axis=1, keepdims=True)
                         - xhat * jnp.mean(dxh * xhat, axis=1, keepdims=True))
            gx_ref[rows, :] = ALPHA * dh
            dh16s.append(dh.astype(MXU_DTYPE))
        for k in range(halves):
            rows = slice(th * k, th * (k + 1))
            da_ref[rows, :] = _mm_nt(dh16s[k], w_ref[0:512, :])
            db_ref[rows, :] = _mm_nt(dh16s[k], w_ref[512:1024, :])
        dh16 = jnp.concatenate(dh16s, axis=0)
        gw_ref[0:512, :] += _mm_tn(ca_ref[...], dh16)
        gw_ref[512:1024, :] += _mm_tn(cb_ref[...], dh16)

    return pl.pallas_call(
        body, name="out_ln_loss", grid=(s // ts,),
        in_specs=[_rows(ts, 512), _rows(ts, 512), _full((D_MODEL, D_MODEL)), _rows(ts, D_MODEL), _rows(ts, D_MODEL),
                  _full((1, D_MODEL)), _full((1, D_MODEL))],
        out_specs=[_full((1, 128)), _rows(ts, D_MODEL), _rows(ts, 512), _rows(ts, 512),
                   _full((D_MODEL, D_MODEL)), _full((2, D_MODEL))],
        out_shape=[jax.ShapeDtypeStruct((1, 128), F32), jax.ShapeDtypeStruct((s, D_MODEL), F32),
                   jax.ShapeDtypeStruct((s, 512), F32), jax.ShapeDtypeStruct((s, 512), F32),
                   jax.ShapeDtypeStruct((D_MODEL, D_MODEL), F32), jax.ShapeDtypeStruct((2, D_MODEL), F32)],
        compiler_params=_cparams(dimension_semantics=("arbitrary",)),
    )(cat_a, cat_b, w_out, x, target, ln_g, ln_b)


def _swa_bwd(sinks, qa, k_pad, v_pad, attn, ga, d_cat_a, rope, parts_w_out):
    s = qa.shape[0]
    sub = min(SWA_SUB, s // BLOCK)
    tq = sub * BLOCK
    nsteps = s // tq
    forward_step = min(1, nsteps - 1)

    def body(sink_ref, qa_ref, ga_ref, at_ref, dc_ref, c_ref, s1_ref, s2_ref, bias_ref, k_ref, v_ref, pout_ref,
             dq_ref, dg_ref, dk_out, dv_out, ds_ref, gout_ref, dk_ref, dv_ref, *scratch):
        n = pl.program_id(0)
        owner_sum = _OwnerSum(pout_ref, *scratch)

        @pl.when(n == 0)
        def _():
            dk_ref[...] = jnp.zeros_like(dk_ref)
            dv_ref[...] = jnp.zeros_like(dv_ref)
            ds_ref[...] = jnp.zeros_like(ds_ref)
            owner_sum.start()

        @pl.when(n == forward_step)
        def _():
            owner_sum.forward()

        @pl.when(n == nsteps - 1)
        def _():
            gout_ref[...] = owner_sum.finish()

        low = lax.broadcasted_iota(jnp.int32, (2 * BLOCK, 128), 1) < SWA_HEAD_DIM
        for b in range(sub):
            rows = slice(BLOCK * b, BLOCK * (b + 1))
            start = pl.multiple_of((n * sub + b) * BLOCK, BLOCK)
            kw = k_ref[pl.ds(start, 2 * BLOCK), :]
            vw = v_ref[pl.ds(start, 2 * BLOCK), :]
            bias_t = _swa_bias_of(bias_ref, n, b)
            q = qa_ref[rows, :] * SWA_SCALE
            g = ga_ref[rows, :]
            sg = _sigmoid(g)
            o = at_ref[rows, :]
            dc = dc_ref[rows, :]
            do = dc * (g * sg)
            dg_ref[rows, :] = (dc * o * (sg * (1.0 + g * (1.0 - sg)))).astype(dg_ref.dtype)
            od = do * o
            c, s1, s2 = c_ref[rows, :], s1_ref[rows, :], s2_ref[rows, :]
            dk, dv = [], []
            for j in range(SWA_KV_HEADS):
                kd, vd = _swa_dup(kw, j), _swa_dup(vw, j)
                qs = _swa_stack(q, j).astype(MXU_DTYPE)
                dos = _swa_stack(do, j).astype(MXU_DTYPE)
                probs, psink = _swa_probs_t(qs, kd, bias_t, _swa_sink_row(sink_ref, j))
                delta = _row_sums_as_row(_swa_stack(od, j))
                dsc = (probs * (_mm_nt(vd, dos) - delta)).astype(MXU_DTYPE)
                dsink = psink * delta
                for r in range(SWA_GROUP):
                    h = SWA_GROUP * j + r
                    ds_ref[h:h + 1, :] += jnp.zeros((1, 128), F32) - jnp.sum(dsink[:, BLOCK * r:BLOCK * (r + 1)])
                dq = _swa_unstack(_mm_tn(dsc, kd))
                for i in range(2):
                    lanes = slice(128 * (2 * j + i), 128 * (2 * j + i + 1))
                    dq_ref[rows, lanes] = _rope_t(dq[i] * SWA_SCALE, c, s1, s2).astype(dq_ref.dtype)
                dkj = _mm(dsc, qs)
                dvj = _mm(probs.astype(MXU_DTYPE), dos)
                dk.append(dkj + pltpu.roll(dkj, SWA_HEAD_DIM, 1))
                dv.append(dvj + pltpu.roll(dvj, SWA_HEAD_DIM, 1))
            dk_ref[pl.ds(start, 2 * BLOCK), :] += jnp.where(low, dk[0], dk[1])
            dv_ref[pl.ds(start, 2 * BLOCK), :] += jnp.where(low, dv[0], dv[1])

        @pl.when(n == nsteps - 1)
        def _():
            dk_out[...] = dk_ref[BLOCK:, :]
            dv_out[...] = dv_ref[BLOCK:, :]

    out_blk = parts_w_out.shape[1:]
    return pl.pallas_call(
        body, name="swa_bwd", grid=(nsteps,),
        in_specs=[pl.BlockSpec(memory_space=pltpu.SMEM)] + [_rows(tq, 512)] * 4 + [_rows(tq, 128)] * 3
        + [_full((2, 2 * BLOCK, SWA_ROWS))] + [_full((s + BLOCK, 128))] * 2 + [pl.BlockSpec(memory_space=pl.ANY)],
        out_specs=[_rows(tq, 512), _rows(tq, 512), _full((s, 128)), _full((s, 128)),
                   _full((SWA_Q_HEADS, 128)), _full(out_blk)],
        out_shape=[jax.ShapeDtypeStruct((s, 512), MXU_DTYPE), jax.ShapeDtypeStruct((s, 512), MXU_DTYPE),
                   jax.ShapeDtypeStruct((s, 128), F32), jax.ShapeDtypeStruct((s, 128), F32),
                   jax.ShapeDtypeStruct((SWA_Q_HEADS, 128), F32), jax.ShapeDtypeStruct(out_blk, F32)],
        scratch_shapes=[pltpu.VMEM((s + BLOCK, 128), F32)] * 2 + _OwnerSum.scratch(out_blk),
        compiler_params=_cparams(dimension_semantics=("arbitrary",)),
    )(sinks, qa, ga, attn, d_cat_a, *rope, _swa_bias(), k_pad, v_pad, parts_w_out)


def _gla_bwd(qb, kb, vb, la, oms, gb, o, sprev, d_cat_b, rb, wg, norm_w):
    s = qb.shape[0]
    tb = min(512, s)
    ch = tb // GLA_CHUNK
    nb = s // tb

    def body(qb_ref, kb_ref, vb_ref, la_ref, oms_ref, gb_ref, o_ref, sp_ref, dc_ref, rb_ref, wg_ref, nw_ref,
             dq_ref, dk_ref, dv_ref, dg_ref, dr_ref, gwg_ref, gbg_ref, gnw_ref, dst_ref):
        @pl.when(pl.program_id(0) == 0)
        def _():
            dst_ref[...] = jnp.zeros_like(dst_ref)
            gwg_ref[...] = jnp.zeros_like(gwg_ref)
            gbg_ref[...] = jnp.zeros_like(gbg_ref)
            gnw_ref[...] = jnp.zeros_like(gnw_ref)

        causal, causal_t = _gla_masks()
        nw = nw_ref[...]
        b = _chunk_cumsum(la_ref[...], True)
        bl = _chunk_last(b)
        eb, enb, ee, dec = jnp.exp(b), jnp.exp(-b), jnp.exp(bl - b), jnp.exp(bl)
        k = kb_ref[...]
        qd = (qb_ref[...] * GLA_SCALE) * eb
        ki = k * enb
        ke = k * ee
        qd16, ki16, ke16 = qd.astype(MXU_DTYPE), ki.astype(MXU_DTYPE), ke.astype(MXU_DTYPE)
        v16 = vb_ref[...].astype(MXU_DTYPE)

        g = gb_ref[...]
        sg = _sigmoid(g)
        silu = g * sg
        dsilu = sg * (1.0 + g * (1.0 - sg))
        gnw = jnp.zeros((1, GLA_DV), F32)
        do = []
        for h in range(GLA_HEADS):
            lv = slice(GLA_DV * h, GLA_DV * (h + 1))
            oh = o_ref[:, lv]
            dch = dc_ref[:, lv]
            r = lax.rsqrt(jnp.mean(oh * oh, axis=1, keepdims=True) + EPS)
            d_on = dch * silu[:, lv]
            dg_ref[:, lv] = (dch * (oh * r * nw) * dsilu[:, lv]).astype(dg_ref.dtype)
            gnw += jnp.sum(d_on * oh * r, axis=0, keepdims=True)
            u = d_on * nw
            do.append(r * u - oh * (r * r * r) * jnp.mean(u * oh, axis=1, keepdims=True))
        gnw_ref[...] += gnw
        do16 = jnp.concatenate(do, axis=1).astype(MXU_DTYPE)

        db, dbl = [None] * ch, [None] * ch
        for ci in reversed(range(ch)):
            rows = slice(GLA_CHUNK * ci, GLA_CHUNK * (ci + 1))
            qds, kis, kes = (_head_stack(t[rows], GLA_DK) for t in (qd16, ki16, ke16))
            vs, dos = _head_stack(v16[rows], GLA_DV), _head_stack(do16[rows], GLA_DV)
            a = jnp.where(causal, _mm_nt(qd16[rows], kis), 0.0).astype(MXU_DTYPE)
            at = jnp.where(causal_t, _mm_nt(ki16[rows], qds), 0.0).astype(MXU_DTYPE)
            da = jnp.where(causal, _mm_nt(do16[rows], vs), 0.0).astype(MXU_DTYPE)
            dat = jnp.where(causal_t, _mm_nt(v16[rows], dos), 0.0).astype(MXU_DTYPE)
            st = sp_ref[ci]
            dst = dst_ref[...]
            dst16 = dst.astype(MXU_DTYPE)
            dv = _mm(at, dos) + _rows_to_heads(_mm_nt(kes, dst16))
            dqd = _mm(da, kis) + _mm(do16[rows], _state_by_head(st.astype(MXU_DTYPE)))
            dki = _mm(dat, qds)
            dke = _mm(v16[rows], _state_by_head(dst16))
            ddec = jnp.sum(dst * st, axis=0, keepdims=True)
            decc = dec[rows][0:1]
            dst_ref[...] = _mm_tn(_heads_to_rows(do16[rows]), qds) + dst * decc
            dq_ref[rows, :] = (dqd * eb[rows] * GLA_SCALE).astype(dq_ref.dtype)
            dk_ref[rows, :] = (dki * enb[rows] + dke * ee[rows]).astype(dk_ref.dtype)
            dv_ref[rows, :] = dv.astype(dv_ref.dtype)
            dke_ke = dke * ke[rows]
            db[ci] = dqd * qd[rows] - dki * ki[rows] - dke_ke
            dbl[ci] = jnp.broadcast_to(jnp.sum(dke_ke, axis=0, keepdims=True) + ddec * decc, (GLA_CHUNK, GLA_KW))

        dla = _chunk_cumsum(jnp.concatenate(db, axis=0), False) + jnp.concatenate(dbl, axis=0)
        dlogit = dla * oms_ref[...] * (1.0 / GLA_TAU)
        dl16 = dlogit.astype(MXU_DTYPE)
        gbg_ref[...] += jnp.sum(dlogit, axis=0, keepdims=True)
        gwg_ref[...] += _mm_tn(rb_ref[...].astype(MXU_DTYPE), dl16)
        dr_ref[...] = _mm_nt(dl16, wg_ref[...]).astype(dr_ref.dtype)

    def rev(width):
        return pl.BlockSpec((tb, width), lambda i: (nb - 1 - i, 0))

    return pl.pallas_call(
        body, name="gla_bwd", grid=(nb,),
        in_specs=[rev(256), rev(256), rev(512), rev(256), rev(256), rev(512), rev(512),
                  pl.BlockSpec((ch, GLA_DV, 256), lambda i: (nb - 1 - i, 0, 0)), rev(512), rev(GLA_RANK),
                  _full((GLA_RANK, 256)), _full((1, 128))],
        out_specs=[rev(256), rev(256), rev(512), rev(512), rev(GLA_RANK),
                   _full((GLA_RANK, 256)), _full((1, 256)), _full((1, 128))],
        out_shape=[jax.ShapeDtypeStruct((s, 256), MXU_DTYPE), jax.ShapeDtypeStruct((s, 256), MXU_DTYPE),
                   jax.ShapeDtypeStruct((s, 512), MXU_DTYPE), jax.ShapeDtypeStruct((s, 512), MXU_DTYPE),
                   jax.ShapeDtypeStruct((s, GLA_RANK), MXU_DTYPE), jax.ShapeDtypeStruct((GLA_RANK, 256), F32),
                   jax.ShapeDtypeStruct((1, 256), F32), jax.ShapeDtypeStruct((1, 128), F32)],
        scratch_shapes=[pltpu.VMEM((GLA_DV, GLA_KW), F32)],
        compiler_params=_cparams(dimension_semantics=("arbitrary",)),
    )(qb, kb, vb, la, oms, gb, o, sprev, d_cat_b, rb, wg, norm_w)


def _dproj_tiles(piece_refs, rope_refs, members=(0, 1, 3, 4, 5, 6, 7, 8)):
    for i in members:
        if i == 1:
            dk = _rope_t(piece_refs[1][...], *(r[...] for r in rope_refs))
            yield OFF[1], OFF[3], jnp.concatenate([dk, piece_refs[2][...]], axis=1).astype(MXU_DTYPE)
        else:
            yield OFF[i], OFF[i + 1], piece_refs[i][...].astype(MXU_DTYPE)


def _in_proj_bwd_x(gx0, pieces, w_in, rope):
    s = gx0.shape[0]
    ts = min(512, s)
    widths = [OFF[i + 1] - OFF[i] for i in range(9)]

    def body(gx0_ref, *refs):
        w_ref, gx_ref = refs[12:]
        acc = gx0_ref[...]
        for lo, hi, t16 in _dproj_tiles(refs[:9], refs[9:12]):
            acc += _mm(t16, w_ref[lo:hi, :])
        gx_ref[...] = acc

    return pl.pallas_call(
        body, name="in_proj_bwd_x", grid=(s // ts,),
        in_specs=[_rows(ts, D_MODEL)] + [_rows(ts, w) for w in widths] + [_rows(ts, 128)] * 3
        + [_full((D_IN_PROJ, D_MODEL))],
        out_specs=_rows(ts, D_MODEL),
        out_shape=jax.ShapeDtypeStruct((s, D_MODEL), F32),
        compiler_params=_cparams(dimension_semantics=("arbitrary",)),
    )(gx0, *pieces, *rope, w_in)


GW_GROUPS = ((0, 1), (3, 4), (5, 6), (7, 8))


def _in_proj_bwd_w(x, pieces, rope, parts_wg, g_ln, g_bg, g_nw, g_sinks, loss):
    s = x.shape[0]
    ts = min(1024, s)
    nt = s // ts
    n_groups = len(GW_GROUPS)
    ends = [OFF[3] if members[-1] == 1 else OFF[members[-1] + 1] for members in GW_GROUPS]
    assert all(ends[g] >= D_IN_SHARD * 2 * (g + 1) for g in range(n_groups))
    blk = (D_IN_SHARD, D_MODEL)

    def body(x_ref, *refs):
        piece_refs, rope_refs = refs[:9], refs[9:12]
        pwg_ref, gln_ref, gbg_ref, gnw_ref, gsk_ref, loss_ref, gin_ref, rwg_ref, rsm_ref = refs[12:21]
        (acc_ref, stage_ref, sib_ref, snd_ref, rcv_ref, mine_ref, sm_ref,
         d2d_send, d2d_recv, ici_send, ici_recv, out_sem, sm_send, sm_recv, sm_loc) = refs[21:]
        g, t = pl.program_id(0), pl.program_id(1)
        x_, y_, c = _mesh_pos()
        me, mychip, sibling = 4 * x_ + 2 * y_ + c, 2 * x_ + y_, (x_, y_, 1 - c)
        small_dsts = (rwg_ref, rsm_ref)

        def small_src(a, block):
            return pwg_ref.at[block] if a == 0 else sm_ref

        def small_copy(k, a, src_block, dst_block, peer):
            i = 2 * (k - 1) + a
            return pltpu.make_async_remote_copy(
                src_ref=small_src(a, src_block), dst_ref=small_dsts[a].at[dst_block], send_sem=sm_send.at[i],
                recv_sem=sm_recv.at[i], device_id=peer, device_id_type=pl.DeviceIdType.MESH)

        def small_local(a):
            return pltpu.make_async_copy(small_src(a, me), small_dsts[a].at[me], sm_loc.at[a])

        @pl.when((g == 0) & (t == 0))
        def _():
            acc_ref[...] = jnp.zeros_like(acc_ref)
            sm_ref[...] = jnp.zeros_like(sm_ref)
            for r in range(D_MODEL // 128):
                sm_ref[r:r + 1, :] = gln_ref[0:1, 128 * r:128 * (r + 1)]
                sm_ref[8 + r:9 + r, :] = gln_ref[1:2, 128 * r:128 * (r + 1)]
            for r in range(2):
                sm_ref[16 + r:17 + r, :] = gbg_ref[0:1, 128 * r:128 * (r + 1)]
            sm_ref[24:25, :] = gnw_ref[...]
            diag = (lax.broadcasted_iota(jnp.int32, gsk_ref.shape, 0)
                    == lax.broadcasted_iota(jnp.int32, gsk_ref.shape, 1))
            sm_ref[32:33, :] = jnp.sum(jnp.where(diag, gsk_ref[...], 0.0), axis=0, keepdims=True)
            sm_ref[40:41, :] = loss_ref[...]
            for a in range(2):
                small_local(a).start()
            for k in range(1, N_DEV):
                peer, pidx = _peer(k, x_, y_, c)
                for a in range(2):
                    small_copy(k, a, pidx, me, peer).start()

        xb = x_ref[...].astype(MXU_DTYPE)
        for gi, members in enumerate(GW_GROUPS):
            @pl.when(g == gi)
            def _(members=members):
                for lo, hi, t16 in _dproj_tiles(piece_refs, rope_refs, members):
                    acc_ref[lo:hi, :] += _mm_tn(t16, xb)

        def block_rows(j):
            return acc_ref[D_IN_SHARD * j:D_IN_SHARD * (j + 1), :]

        def d2d(gi):
            return pltpu.make_async_remote_copy(
                src_ref=stage_ref.at[gi % 2], dst_ref=sib_ref.at[gi], send_sem=d2d_send.at[gi],
                recv_sem=d2d_recv.at[gi], device_id=sibling, device_id_type=pl.DeviceIdType.MESH)

        def ici(slot, owner):
            return pltpu.make_async_remote_copy(
                src_ref=snd_ref.at[slot], dst_ref=rcv_ref.at[slot], send_sem=ici_send.at[slot],
                recv_sem=ici_recv.at[slot], device_id=owner, device_id_type=pl.DeviceIdType.MESH)

        def to_sibling(gi):
            if gi >= 2:
                d2d(gi - 2).wait_send()
            for cc in range(2):
                @pl.when(c == cc)
                def _(cc=cc):
                    stage_ref[gi % 2] = block_rows(2 * gi + 1 - cc)
            d2d(gi).start()

        def chip_sum(gi):
            d2d(gi).wait_recv()
            gx, gy = gi // 2, gi % 2
            for cc in range(2):
                @pl.when(c == cc)
                def _(cc=cc):
                    total = block_rows(2 * gi + cc) + sib_ref[gi]

                    @pl.when(mychip == gi)
                    def _():
                        mine_ref[...] = total

                    @pl.when(mychip != gi)
                    def _():
                        slot = jnp.where(x_ == gx, 0, 1) + 2 * jnp.where(y_ == gy, 0, 1) - 1
                        snd_ref[slot] = total.astype(snd_ref.dtype)
                        ici(slot, (gx, gy, c)).start()

        for gi in range(n_groups):
            @pl.when((g == gi) & (t == nt - 1))
            def _(gi=gi):
                to_sibling(gi)
                if gi == n_groups - 1:
                    for k in range(n_groups):
                        chip_sum(k)
                    total = mine_ref[...]
                    for slot in range(3):
                        ici(slot, sibling).wait_recv()
                        total = total + rcv_ref[slot].astype(F32)
                    mine_ref[...] = total
                    out = pltpu.make_async_copy(mine_ref, gin_ref, out_sem)
                    out.start()
                    for k in range(1, N_DEV):
                        peer, pidx = _peer(k, x_, y_, c)
                        for a in range(2):
                            small_copy(k, a, me, pidx, peer).wait_recv()
                    for k in range(1, N_DEV):
                        peer, pidx = _peer(k, x_, y_, c)
                        for a in range(2):
                            small_copy(k, a, pidx, me, peer).wait_send()
                    for a in range(2):
                        small_local(a).wait()
                    d2d(gi - 1).wait_send()
                    d2d(gi).wait_send()
                    for slot in range(3):
                        ici(slot, sibling).wait_send()
                    out.wait()

    def piece_spec(i, width):
        gi = next(k for k, members in enumerate(GW_GROUPS) if (i in members or (i == 2 and 1 in members)))
        return pl.BlockSpec((ts, width), lambda g, t: (jnp.where(g == gi, t, jnp.where(g < gi, 0, nt - 1)), 0))

    widths = [OFF[i + 1] - OFF[i] for i in range(9)]
    hbm = pl.BlockSpec(memory_space=pl.ANY)
    vmem = pl.BlockSpec(memory_space=pltpu.VMEM)
    rope_spec = pl.BlockSpec((ts, 128), lambda g, t: (jnp.where(g == 0, t, nt - 1), 0))
    return pl.pallas_call(
        body, name="in_proj_bwd_w", grid=(n_groups, nt),
        in_specs=[pl.BlockSpec((ts, D_MODEL), lambda g, t: (t, 0))] + [piece_spec(i, w) for i, w in enumerate(widths)]
        + [rope_spec] * 3 + [hbm] + [vmem] * 5,
        out_specs=[hbm, hbm, hbm],
        out_shape=[jax.ShapeDtypeStruct(blk, F32), jax.ShapeDtypeStruct((N_DEV,) + parts_wg.shape[1:], F32),
                   jax.ShapeDtypeStruct((N_DEV, SMALL_ROWS, 128), F32)],
        scratch_shapes=[pltpu.VMEM((D_IN_PROJ, D_MODEL), F32), pltpu.VMEM((2,) + blk, F32),
                        pltpu.VMEM((n_groups,) + blk, F32), pltpu.VMEM((3,) + blk, MXU_DTYPE),
                        pltpu.VMEM((3,) + blk, MXU_DTYPE), pltpu.VMEM(blk, F32), pltpu.VMEM((SMALL_ROWS, 128), F32),
                        pltpu.SemaphoreType.DMA((n_groups,)), pltpu.SemaphoreType.DMA((n_groups,)),
                        pltpu.SemaphoreType.DMA((3,)), pltpu.SemaphoreType.DMA((3,)), pltpu.SemaphoreType.DMA,
                        pltpu.SemaphoreType.DMA((2 * (N_DEV - 1),)), pltpu.SemaphoreType.DMA((2 * (N_DEV - 1),)),
                        pltpu.SemaphoreType.DMA((2,))],
        compiler_params=_cparams(dimension_semantics=("arbitrary", "arbitrary")),
    )(x, *pieces, *rope, parts_wg, g_ln, g_bg, g_nw, g_sinks, loss)


def _local_step(x, positions, w_in_t, wg_s, b_gate, sinks, norm_w, w_out_s, ln_g, ln_b, target):
    qa, k_pad, v_pad, ga, qb, kb, vb, gb, rb, la, oms, *rope, w_in, wg, w_out = _in_proj(
        x, w_in_t, wg_s, b_gate, _rope_angles(positions), w_out_s)
    attn, cat_a = _swa_fwd(sinks, qa, k_pad, v_pad, ga)
    o, cat_b, sprev = _gla_fwd(qb, kb, vb, la, gb, norm_w)
    loss, gx0, d_cat_a, d_cat_b, g_w_out, g_ln = _out_ln_loss(cat_a, cat_b, w_out, x, target, ln_g, ln_b)
    parts_w_out = g_w_out.reshape(N_DEV, D_OUT_SHARD, D_MODEL)
    dqa, dga, dka, dva, g_sinks, g_out = _swa_bwd(sinks, qa, k_pad, v_pad, attn, ga, d_cat_a, rope, parts_w_out)
    dqb, dkb, dvb, dgb, drb, g_wg, g_bg, g_nw = _gla_bwd(qb, kb, vb, la, oms, gb, o, sprev, d_cat_b, rb, wg, norm_w)
    pieces = (dqa, dka, dva, dga, dqb, dkb, dvb, dgb, drb)
    grad_x = _in_proj_bwd_x(gx0, pieces, w_in, rope)
    parts_wg = jnp.transpose(g_wg.reshape(GLA_RANK, N_DEV, 32), (1, 0, 2))
    g_in, r_wg, r_small = _in_proj_bwd_w(x, pieces, rope, parts_wg, g_ln, g_bg, g_nw, g_sinks, loss)
    return grad_x, g_in, g_out, r_wg, r_small


def _mesh_pos():
    return lax.axis_index("x"), lax.axis_index("y"), lax.axis_index("c")


def _peer(k, x, y, c):
    px = (1 - x) if k & 4 else x
    py = (1 - y) if k & 2 else y
    pc = (1 - c) if k & 1 else c
    return (px, py, pc), 4 * px + 2 * py + pc


def _other_chips(x, y):
    return [(1 - x, y), (x, 1 - y), (1 - x, 1 - y)]


def _shard_view(t):
    return jnp.transpose(t, (2, 0, 1))


class _BlockGather:
    def __init__(self, slots, send_sems, recv_sems):
        self.slots, self.send_sems, self.recv_sems = slots, send_sems, recv_sems
        x, y, c = _mesh_pos()
        self.xy, self.c, self.me, self.sibling = (x, y), c, 4 * x + 2 * y + c, (x, y, 1 - c)
        self.chips = _other_chips(x, y)

    @staticmethod
    def scratch():
        return [pltpu.SemaphoreType.DMA((N_DEV - 1,)), pltpu.SemaphoreType.DMA((N_DEV - 1,))]

    def _copy(self, k, block, to):
        return pltpu.make_async_remote_copy(
            src_ref=self.slots.at[block], dst_ref=self.slots.at[block], send_sem=self.send_sems.at[k],
            recv_sem=self.recv_sems.at[k], device_id=to, device_id_type=pl.DeviceIdType.MESH)

    def start(self):
        for j, (cx, cy) in enumerate(self.chips):
            self._copy(1 + j, self.me, (cx, cy, self.c)).start()
        self._copy(0, self.me, self.sibling).start()

    def forward(self):
        for j, (cx, cy) in enumerate(self.chips):
            block = 4 * cx + 2 * cy + self.c
            self._copy(1 + j, block, self.sibling).wait_recv()
            self._copy(4 + j, block, self.sibling).start()

    def finish(self):
        x, y = self.xy
        self._copy(0, 4 * x + 2 * y + (1 - self.c), self.sibling).wait_recv()
        for j, (cx, cy) in enumerate(self.chips):
            self._copy(4 + j, 4 * cx + 2 * cy + (1 - self.c), self.sibling).wait_recv()
        for k in range(N_DEV - 1):
            self._copy(k, self.me, self.sibling).wait_send()


class _OwnerSum:
    def __init__(self, parts, own, sib, snd, rcv, loc_sems, d2d_send, d2d_recv, ici_send, ici_recv):
        self.parts, self.own, self.sib, self.snd, self.rcv = parts, own, sib, snd, rcv
        self.sems = (loc_sems, d2d_send, d2d_recv, ici_send, ici_recv)
        x, y, c = _mesh_pos()
        self.c, self.sibling = c, (x, y, 1 - c)
        self.chips = [(x, y)] + _other_chips(x, y)

    @staticmethod
    def scratch(block):
        return [pltpu.VMEM((4,) + block, F32), pltpu.VMEM((4,) + block, F32),
                pltpu.VMEM((3,) + block, MXU_DTYPE), pltpu.VMEM((3,) + block, MXU_DTYPE),
                pltpu.SemaphoreType.DMA((4,)), pltpu.SemaphoreType.DMA((4,)), pltpu.SemaphoreType.DMA((4,)),
                pltpu.SemaphoreType.DMA((3,)), pltpu.SemaphoreType.DMA((3,))]

    def _local(self, r):
        cx, cy = self.chips[r]
        return pltpu.make_async_copy(self.parts.at[4 * cx + 2 * cy + self.c], self.own.at[r], self.sems[0].at[r])

    def _d2d(self, r):
        cx, cy = self.chips[r]
        return pltpu.make_async_remote_copy(
            src_ref=self.parts.at[4 * cx + 2 * cy + (1 - self.c)], dst_ref=self.sib.at[r], send_sem=self.sems[1].at[r],
            recv_sem=self.sems[2].at[r], device_id=self.sibling, device_id_type=pl.DeviceIdType.MESH)

    def _ici(self, r):
        cx, cy = self.chips[r]
        return pltpu.make_async_remote_copy(
            src_ref=self.snd.at[r - 1], dst_ref=self.rcv.at[r - 1], send_sem=self.sems[3].at[r - 1],
            recv_sem=self.sems[4].at[r - 1], device_id=(cx, cy, self.c), device_id_type=pl.DeviceIdType.MESH)

    def start(self):
        for r in (1, 2, 3, 0):
            self._local(r).start()
            self._d2d(r).start()

    def forward(self):
        for r in (1, 2, 3):
            self._local(r).wait()
            self._d2d(r).wait_recv()
            self.snd[r - 1] = (self.own[r] + self.sib[r]).astype(self.snd.dtype)
            self._ici(r).start()

    def finish(self):
        self._local(0).wait()
        self._d2d(0).wait_recv()
        acc = self.own[0] + self.sib[0]
        for r in (1, 2, 3):
            self._ici(r).wait_recv()
            acc = acc + self.rcv[r - 1].astype(F32)
        for r in range(4):
            self._d2d(r).wait_send()
        for r in (1, 2, 3):
            self._ici(r).wait_send()
        return acc


SMALL_ROWS = 48


def _adamw_math(g, w, m, v):
    nm = ADAM_B1 * m + (1.0 - ADAM_B1) * g
    nv = ADAM_B2 * v + (1.0 - ADAM_B2) * (g * g)
    m_hat = nm / (1.0 - ADAM_B1 ** ADAM_STEP)
    v_hat = nv / (1.0 - ADAM_B2 ** ADAM_STEP)
    return -ADAM_LR * (m_hat / (jnp.sqrt(v_hat) + ADAM_EPS) + ADAM_WD * w), nm, nv


def _adamw_shard_view(g, w, m, v):
    rows, width = g.shape

    def body(g_ref, w_hbm, m_hbm, v_hbm, g_out, d_out, nm_out, nv_out, bufs, outs, sems):
        loads = [pltpu.make_async_copy(src.at[:, 0, :], bufs.at[i], sems.at[i])
                 for i, src in enumerate((w_hbm, m_hbm, v_hbm))]
        for cp in loads:
            cp.start()
        g = g_ref[...]
        for cp in loads:
            cp.wait()
        outs[0] = g
        outs[1], outs[2], outs[3] = _adamw_math(g, bufs[0], bufs[1], bufs[2])
        stores = [pltpu.make_async_copy(outs.at[i], dst.at[:, 0, :], sems.at[3 + i])
                  for i, dst in enumerate((g_out, d_out, nm_out, nv_out))]
        for cp in stores:
            cp.start()
        for cp in stores:
            cp.wait()

    hbm = pl.BlockSpec(memory_space=pl.ANY)
    return pl.pallas_call(
        body, name="adamw_w_in",
        in_specs=[pl.BlockSpec(memory_space=pltpu.VMEM), hbm, hbm, hbm], out_specs=[hbm] * 4,
        out_shape=[jax.ShapeDtypeStruct((rows, 1, width), F32)] * 4,
        scratch_shapes=[pltpu.VMEM((3, rows, width), F32), pltpu.VMEM((4, rows, width), F32),
                        pltpu.SemaphoreType.DMA((7,))],
        compiler_params=_cparams(),
    )(g, w, m, v)


def _adamw_vectors(r_small, r_wg, g_out, params):
    n_par = len(params)

    def body(rsm_ref, rwg_ref, gout_ref, *refs):
        ins, outs = refs[:3 * n_par], refs[3 * n_par:]
        g = rsm_ref[0]
        gwg = rwg_ref[0]
        for j in range(1, N_DEV):
            g = g + rsm_ref[j]
            gwg = gwg + rwg_ref[j]
        outs[4 * n_par][...] = g[40:41]
        grads = [gwg,
                 jnp.concatenate([g[r:r + 1] for r in range(0, 8)], axis=1),
                 jnp.concatenate([g[r:r + 1] for r in range(8, 16)], axis=1),
                 jnp.concatenate([g[16:17], g[17:18]], axis=1),
                 g[24:25],
                 g[32:33, 0:SWA_Q_HEADS],
                 gout_ref[...]]
        for p, gp in enumerate(grads):
            w_ref, m_ref, v_ref = ins[3 * p:3 * p + 3]
            outs[4 * p][...] = gp
            outs[4 * p + 1][...], outs[4 * p + 2][...], outs[4 * p + 3][...] = _adamw_math(
                gp, w_ref[...], m_ref[...], v_ref[...])

    vmem = pl.BlockSpec(memory_space=pltpu.VMEM)
    flat = [t for wmv in params for t in wmv]
    return pl.pallas_call(
        body, name="adamw_vectors",
        in_specs=[vmem] * (3 + len(flat)), out_specs=[vmem] * (4 * n_par + 1),
        out_shape=[jax.ShapeDtypeStruct(wmv[0].shape, F32) for wmv in params for _ in range(4)]
        + [jax.ShapeDtypeStruct((1, 128), F32)],
        compiler_params=_cparams(),
    )(r_small, r_wg, g_out, *flat)


def kernel(x, positions, w_in, gla_w_gate_up, gla_b_gate, attn_sinks, gla_norm_w, w_out, ln_g, ln_b, loss_target, m_w_in, m_gla_w_gate_up, m_gla_b_gate, m_attn_sinks, m_gla_norm_w, m_w_out, m_ln_g, m_ln_b, v_w_in, v_gla_w_gate_up, v_gla_b_gate, v_attn_sinks, v_gla_norm_w, v_w_out, v_ln_g, v_ln_b):
    grad_x, g_in, g_out, r_wg, r_small = _local_step(
        x[0], positions[0], _shard_view(w_in), gla_w_gate_up[0], gla_b_gate, attn_sinks[0], gla_norm_w, w_out[0],
        ln_g, ln_b, loss_target[0])

    upd_in = _adamw_shard_view(g_in, _shard_view(w_in), _shard_view(m_w_in), _shard_view(v_w_in))
    upd_in = [jnp.transpose(t, (1, 2, 0)) for t in upd_in]
    vec = _adamw_vectors(r_small, r_wg, g_out, [
        (gla_w_gate_up[0], m_gla_w_gate_up[0], v_gla_w_gate_up[0]), (ln_g, m_ln_g, v_ln_g), (ln_b, m_ln_b, v_ln_b),
        (gla_b_gate, m_gla_b_gate, v_gla_b_gate), (gla_norm_w, m_gla_norm_w, v_gla_norm_w),
        (attn_sinks, m_attn_sinks, v_attn_sinks), (w_out[0], m_w_out[0], v_w_out[0])])

    outs = [vec[28][0, 0], grad_x[None]]
    for kind in range(4):
        u_wg, u_ln_g, u_ln_b, u_bg, u_nw, u_sinks, u_out = (vec[4 * p + kind] for p in range(7))
        outs += [upd_in[kind], u_wg[None], u_bg, u_sinks, u_nw, u_out[None], u_ln_g, u_ln_b]
    return tuple(outs)
```

```python
import jax
import jax.numpy as jnp
from jax import lax
from jax.experimental import pallas as pl
from jax.experimental.pallas import tpu as pltpu

F32 = jnp.float32
MXU_DTYPE = jnp.bfloat16

N_DEV = 8
D_MODEL = 1024
SWA_Q_HEADS = 8
SWA_KV_HEADS = 2
SWA_GROUP = 4
SWA_HEAD_DIM = 64
BLOCK = 128
ROPE_THETA = 500000.0
ROT_DIM = 16
GLA_HEADS = 4
GLA_DK = 64
GLA_DV = 128
GLA_RANK = 16
GLA_TAU = 16.0
GLA_CHUNK = 64
D_IN_PROJ = 2832
D_IN_SHARD = D_IN_PROJ // N_DEV
D_OUT_SHARD = D_MODEL // N_DEV
OFF = (0, 512, 640, 768, 1280, 1536, 1792, 2304, 2816, 2832)
EPS = 1e-5
ALPHA = 2.0 ** 0.25
SWA_SCALE = SWA_HEAD_DIM ** -0.5
GLA_SCALE = GLA_DK ** -0.5
ADAM_LR = 0.001
ADAM_B1 = 0.9
ADAM_B2 = 0.999
ADAM_EPS = 1e-08
ADAM_WD = 0.01
ADAM_STEP = 10
VMEM_LIMIT = 56 * 1024 * 1024

_NT = (((1,), (1,)), ((), ()))
_TN = (((0,), (0,)), ((), ()))


def _mm(a, b):
    return jnp.dot(a, b, preferred_element_type=F32)


def _mm_nt(a, b):
    return lax.dot_general(a, b, _NT, preferred_element_type=F32)


def _mm_tn(a, b):
    return lax.dot_general(a, b, _TN, preferred_element_type=F32)


def _sigmoid(t):
    return 1.0 / (1.0 + jnp.exp(-t))


def _cparams(**kw):
    return pltpu.CompilerParams(vmem_limit_bytes=VMEM_LIMIT, **kw)


def _full(shape):
    return pl.BlockSpec(shape, lambda *_: (0,) * len(shape))


def _rows(tile, width):
    return pl.BlockSpec((tile, width), lambda i: (i, 0))


def _rope_angles(positions):
    half = ROT_DIM // 2
    inv_freq = ROPE_THETA ** (-jnp.arange(half, dtype=F32) / half)
    ang = positions.astype(F32)[:, None] * inv_freq[None, :]
    return jnp.concatenate([jnp.cos(ang), jnp.sin(ang)], axis=1)


def _split3_parts(t):
    hi = t.astype(MXU_DTYPE)
    r1 = t - hi.astype(F32)
    mid = r1.astype(MXU_DTYPE)
    return hi, mid, (r1 - mid.astype(F32)).astype(MXU_DTYPE)


def _rope_tables(cs):
    half = ROT_DIM // 2
    i = lax.broadcasted_iota(jnp.int32, (2 * half, 3 * 128), 0)
    lane = lax.broadcasted_iota(jnp.int32, (2 * half, 3 * 128), 1)
    table, pos = _idiv(lane, 128), lane & (SWA_HEAD_DIM - 1)
    is_c = (table == 0) & (pos < ROT_DIM) & ((pos & (half - 1)) == i)
    is_s1 = (table == 1) & (pos < half) & (pos + half == i)
    is_s2 = (table == 2) & (pos >= half) & (pos < ROT_DIM) & (pos == i)
    sel = jnp.where(is_c | is_s2, 1.0, jnp.where(is_s1, -1.0, 0.0)).astype(MXU_DTYPE)
    hi, mid, lo = _split3_parts(cs)
    t = (_mm(hi, sel) + _mm(mid, sel)) + _mm(lo, sel)
    pos1 = lax.broadcasted_iota(jnp.int32, (1, 128), 1) & (SWA_HEAD_DIM - 1)
    return t[:, 0:128] + jnp.where(pos1 >= ROT_DIM, 1.0, 0.0), t[:, 128:256], t[:, 256:384]


def _rope(t, c, s1, s2):
    return t * c + pltpu.roll(t, 120, 1) * s1 + pltpu.roll(t, 8, 1) * s2


def _rope_t(g, c, s1, s2):
    return g * c + pltpu.roll(g * s1, 8, 1) + pltpu.roll(g * s2, 120, 1)


def _in_proj(x, w_in_t, wg_s, b_gate, cos_sin, w_out_s):
    s = x.shape[0]
    ts = min(512, s)
    nsteps = s // ts
    forward_step = min(3, nsteps - 1)
    widths = [OFF[i + 1] - OFF[i] for i in range(9)]

    def body(x_ref, win_hbm, wgs_ref, bg_ref, cs_ref, wos_ref,
             qa_ref, ka_ref, va_ref, ga_ref, qb_ref, kb_ref, vb_ref, gb_ref, rb_ref, la_ref, oms_ref,
             c_ref, s1_ref, s2_ref, x16_ref, w_ref, wg_ref, wout_ref,
             win_all, wg_all, wout_all, stage, stage_sem, *sems):
        xb = x_ref[...].astype(MXU_DTYPE)
        x16_ref[...] = xb
        c, s1, s2 = _rope_tables(cs_ref[...])
        c_ref[...], s1_ref[...], s2_ref[...] = c, s1, s2
        i0 = pl.program_id(0)
        gather = _BlockGather(wout_all, *sems[0:2])

        @pl.when(i0 == 0)
        def _():
            ka_ref[0:BLOCK, :] = jnp.zeros((BLOCK, 128), ka_ref.dtype)
            va_ref[0:BLOCK, :] = jnp.zeros((BLOCK, 128), va_ref.dtype)
            first = (_BlockGather(win_all, *sems[2:4]), _BlockGather(wg_all, *sems[4:6]))
            load = pltpu.make_async_copy(win_hbm.at[:, 0, :], stage, stage_sem)
            load.start()
            wout_all[gather.me] = wos_ref[...].astype(wout_all.dtype)
            wg_all[gather.me] = wgs_ref[...].astype(wg_all.dtype)
            load.wait()
            win_all[gather.me] = stage[...].astype(win_all.dtype)
            for stage_of in ("start", "forward", "finish"):
                for g in first:
                    getattr(g, stage_of)()
            gather.start()
            for j in range(N_DEV):
                w_ref[D_IN_SHARD * j:D_IN_SHARD * (j + 1), :] = win_all[j]
                wg_ref[:, 32 * j:32 * (j + 1)] = wg_all[j]

        @pl.when(i0 == forward_step)
        def _():
            gather.forward()

        @pl.when(i0 == nsteps - 1)
        def _():
            gather.finish()
            for j in range(N_DEV):
                wout_ref[D_OUT_SHARD * j:D_OUT_SHARD * (j + 1), :] = wout_all[j]

        kv_rows = pl.ds(pl.multiple_of(BLOCK + i0 * ts, BLOCK), ts)

        def cols(i):
            return _mm_nt(xb, w_ref[OFF[i]:OFF[i + 1], :])

        qa = cols(0)
        for i in range(4):
            qa_ref[:, 128 * i:128 * (i + 1)] = _rope(qa[:, 128 * i:128 * (i + 1)], c, s1, s2).astype(qa_ref.dtype)
        kv = _mm_nt(xb, w_ref[OFF[1]:OFF[3], :])
        ka_ref[kv_rows, :] = _rope(kv[:, 0:128], c, s1, s2).astype(ka_ref.dtype)
        va_ref[kv_rows, :] = kv[:, 128:256].astype(va_ref.dtype)
        ga_ref[...] = cols(3)
        qb_ref[...] = cols(4)
        kb_ref[...] = cols(5)
        vb_ref[...] = cols(6).astype(vb_ref.dtype)
        gb_ref[...] = cols(7)
        rb = cols(8)
        rb_ref[...] = rb
        logit = _mm(rb.astype(MXU_DTYPE), wg_ref[...]) + bg_ref[...]
        e = jnp.exp(-jnp.abs(logit))
        la_ref[...] = (jnp.minimum(logit, 0.0) - jnp.log(1.0 + e)) / GLA_TAU
        oms_ref[...] = jnp.where(logit >= 0.0, e, 1.0) / (1.0 + e)

    out_shape = [jax.ShapeDtypeStruct((s + BLOCK if i in (1, 2) else s, w), MXU_DTYPE if i in (0, 1, 2, 6) else F32)
                 for i, w in enumerate(widths)]
    out_shape += [jax.ShapeDtypeStruct((s, 256), F32)] * 2 + [jax.ShapeDtypeStruct((s, 128), F32)] * 3
    out_shape += [jax.ShapeDtypeStruct((s, D_MODEL), MXU_DTYPE)]
    out_shape += [jax.ShapeDtypeStruct((D_IN_PROJ, D_MODEL), MXU_DTYPE), jax.ShapeDtypeStruct((GLA_RANK, 256), MXU_DTYPE),
                  jax.ShapeDtypeStruct((D_MODEL, D_MODEL), MXU_DTYPE)]
    return pl.pallas_call(
        body, name="in_proj", grid=(nsteps,),
        in_specs=[_rows(ts, D_MODEL), pl.BlockSpec(memory_space=pl.ANY), _full((GLA_RANK, 32)), _full((1, 256)),
                  _rows(ts, ROT_DIM), _full((D_OUT_SHARD, D_MODEL))],
        out_specs=[_full((s + BLOCK, w)) if i in (1, 2) else _rows(ts, w) for i, w in enumerate(widths)]
        + [_rows(ts, 256)] * 2 + [_rows(ts, 128)] * 3 + [_rows(ts, D_MODEL)]
        + [_full((D_IN_PROJ, D_MODEL)), _full((GLA_RANK, 256)), _full((D_MODEL, D_MODEL))],
        out_shape=out_shape,
        scratch_shapes=[pltpu.VMEM((N_DEV, D_IN_SHARD, D_MODEL), MXU_DTYPE), pltpu.VMEM((N_DEV, GLA_RANK, 32), MXU_DTYPE),
                        pltpu.VMEM((N_DEV, D_OUT_SHARD, D_MODEL), MXU_DTYPE),
                        pltpu.VMEM((D_IN_SHARD, D_MODEL), F32), pltpu.SemaphoreType.DMA]
        + 3 * _BlockGather.scratch(),
        compiler_params=_cparams(dimension_semantics=("arbitrary",)),
    )(x, w_in_t, wg_s, b_gate, cos_sin, w_out_s)


SWA_ROWS = SWA_GROUP * BLOCK


def _swa_bias():
    shape = (2, 2 * BLOCK, SWA_ROWS)
    ki = lax.broadcasted_iota(jnp.int32, shape, 1)
    qi = lax.broadcasted_iota(jnp.int32, shape, 2) & (BLOCK - 1)
    first = lax.broadcasted_iota(jnp.int32, shape, 0) == 0
    dist = qi + BLOCK - ki
    ok = (dist >= 0) & (dist < BLOCK) & (jnp.logical_not(first) | (ki >= BLOCK))
    return jnp.where(ok, 0.0, -jnp.inf).astype(F32)


SWA_SUB = 8


def _swa_bias_of(bias_ref, n, b):
    return bias_ref[jnp.minimum(n, 1)] if b == 0 else bias_ref[1]


def _swa_dup(t, j):
    t = t.astype(F32)
    low = lax.broadcasted_iota(jnp.int32, t.shape, 1) < SWA_HEAD_DIM
    keep = low if j == 0 else jnp.logical_not(low)
    return jnp.where(keep, t, pltpu.roll(t, SWA_HEAD_DIM, 1)).astype(MXU_DTYPE)


def _swa_stack(t, j):
    low = lax.broadcasted_iota(jnp.int32, (BLOCK, 128), 1) < SWA_HEAD_DIM
    zero = jnp.zeros((BLOCK, 128), t.dtype)
    blocks = []
    for p in (2 * j, 2 * j + 1):
        tp = t[:, 128 * p:128 * (p + 1)]
        blocks += [jnp.where(low, tp, zero), jnp.where(low, zero, tp)]
    return jnp.concatenate(blocks, axis=0)


def _swa_unstack(t):
    low = lax.broadcasted_iota(jnp.int32, (BLOCK, 128), 1) < SWA_HEAD_DIM
    return [jnp.where(low, t[2 * BLOCK * i:2 * BLOCK * i + BLOCK], t[2 * BLOCK * i + BLOCK:2 * BLOCK * (i + 1)])
            for i in range(2)]


def _swa_sink_row(sink_ref, j):
    lane = lax.broadcasted_iota(jnp.int32, (1, SWA_ROWS), 1)
    row = jnp.full((1, SWA_ROWS), sink_ref[SWA_GROUP * j], F32)
    for r in range(1, SWA_GROUP):
        row = jnp.where(lane >= BLOCK * r, sink_ref[SWA_GROUP * j + r], row)
    return row


def _split3(t):
    return jnp.concatenate(_split3_parts(t), axis=1)


def _row_sums_as_row(t):
    ones = jnp.ones((8, 3 * t.shape[1]), MXU_DTYPE)
    return _mm_nt(ones, _split3(t))[0:1, :]


def _swa_probs_t(qs, kd, bias_t, sink):
    sc = _mm_nt(kd, qs) + bias_t
    m = jnp.maximum(jnp.max(sc, axis=0, keepdims=True), sink)
    p = jnp.exp(sc - m)
    ps = jnp.exp(sink - m)
    rinv = 1.0 / (jnp.sum(p, axis=0, keepdims=True) + ps)
    return p * rinv, ps * rinv


def _swa_fwd(sinks, qa, k_pad, v_pad, ga):
    s = qa.shape[0]
    sub = min(SWA_SUB, s // BLOCK)
    tq = sub * BLOCK

    def body(sink_ref, qa_ref, ga_ref, bias_ref, k_ref, v_ref, attn_ref, cat_ref):
        n = pl.program_id(0)
        for b in range(sub):
            rows = slice(BLOCK * b, BLOCK * (b + 1))
            start = pl.multiple_of((n * sub + b) * BLOCK, BLOCK)
            kw = k_ref[pl.ds(start, 2 * BLOCK), :]
            vw = v_ref[pl.ds(start, 2 * BLOCK), :]
            bias_t = _swa_bias_of(bias_ref, n, b)
            q = qa_ref[rows, :] * SWA_SCALE
            g = ga_ref[rows, :]
            silu = g * _sigmoid(g)
            for j in range(SWA_KV_HEADS):
                qs = _swa_stack(q, j).astype(MXU_DTYPE)
                probs, _ = _swa_probs_t(qs, _swa_dup(kw, j), bias_t, _swa_sink_row(sink_ref, j))
                pairs = _swa_unstack(_mm_tn(probs.astype(MXU_DTYPE), _swa_dup(vw, j)))
                for i in range(2):
                    lanes = slice(128 * (2 * j + i), 128 * (2 * j + i + 1))
                    attn_ref[rows, lanes] = pairs[i]
                    cat_ref[rows, lanes] = (pairs[i] * silu[:, lanes]).astype(cat_ref.dtype)

    return pl.pallas_call(
        body, name="swa_fwd", grid=(s // tq,),
        in_specs=[pl.BlockSpec(memory_space=pltpu.SMEM), _rows(tq, 512), _rows(tq, 512),
                  _full((2, 2 * BLOCK, SWA_ROWS)), _full((s + BLOCK, 128)), _full((s + BLOCK, 128))],
        out_specs=[_rows(tq, 512), _rows(tq, 512)],
        out_shape=[jax.ShapeDtypeStruct((s, 512), F32), jax.ShapeDtypeStruct((s, 512), MXU_DTYPE)],
        compiler_params=_cparams(dimension_semantics=("arbitrary",)),
    )(sinks, qa, ga, _swa_bias(), k_pad, v_pad)


GLA_KW = GLA_HEADS * GLA_DK
GLA_VW = GLA_HEADS * GLA_DV


def _idiv(t, d):
    return t >> (d.bit_length() - 1)


def _chunk_cumsum(t, lower):
    n, w = t.shape
    r = lax.broadcasted_iota(jnp.int32, (n, n), 0)
    c = lax.broadcasted_iota(jnp.int32, (n, n), 1)
    tri = ((_idiv(r, GLA_CHUNK) == _idiv(c, GLA_CHUNK)) & ((r >= c) if lower else (r <= c))).astype(MXU_DTYPE)
    parts = _mm(tri, _split3(t))
    return (parts[:, :w] + parts[:, w:2 * w]) + parts[:, 2 * w:]


def _chunk_last(t):
    n = t.shape[0]
    return jnp.concatenate(
        [jnp.broadcast_to(t[c + GLA_CHUNK - 1:c + GLA_CHUNK, :], (GLA_CHUNK, t.shape[1]))
         for c in range(0, n, GLA_CHUNK)], axis=0)


def _head_stack(t, width):
    head = _idiv(lax.broadcasted_iota(jnp.int32, t.shape, 1), width)
    zero = jnp.zeros_like(t)
    return jnp.concatenate([jnp.where(head == h, t, zero) for h in range(GLA_HEADS)], axis=0)


def _heads_to_rows(t):
    return jnp.concatenate([t[:, GLA_DV * h:GLA_DV * (h + 1)] for h in range(GLA_HEADS)], axis=0)


def _rows_to_heads(t):
    return jnp.concatenate([t[GLA_CHUNK * h:GLA_CHUNK * (h + 1)] for h in range(GLA_HEADS)], axis=1)


def _state_by_head(t):
    srow = _idiv(lax.broadcasted_iota(jnp.int32, (GLA_VW, GLA_KW), 0), GLA_DV)
    slane = _idiv(lax.broadcasted_iota(jnp.int32, (GLA_VW, GLA_KW), 1), GLA_DK)
    return jnp.where(srow == slane, jnp.concatenate([t] * GLA_HEADS, axis=0), jnp.zeros((GLA_VW, GLA_KW), t.dtype))


def _gla_masks():
    row = lax.broadcasted_iota(jnp.int32, (GLA_CHUNK, GLA_KW), 0)
    pos = lax.broadcasted_iota(jnp.int32, (GLA_CHUNK, GLA_KW), 1) & (GLA_CHUNK - 1)
    return pos <= row, pos >= row


def _gla_fwd(qb, kb, vb, la, gb, norm_w):
    s = qb.shape[0]
    tb = min(256, s)
    ch = tb // GLA_CHUNK

    def body(qb_ref, kb_ref, vb_ref, la_ref, gb_ref, nw_ref, o_ref, cat_ref, sp_ref, st_ref):
        @pl.when(pl.program_id(0) == 0)
        def _():
            st_ref[...] = jnp.zeros_like(st_ref)

        causal, _ = _gla_masks()
        nw = nw_ref[...]
        b = _chunk_cumsum(la_ref[...], True)
        bl = _chunk_last(b)
        k = kb_ref[...]
        qd = ((qb_ref[...] * GLA_SCALE) * jnp.exp(b)).astype(MXU_DTYPE)
        ki = (k * jnp.exp(-b)).astype(MXU_DTYPE)
        ke = (k * jnp.exp(bl - b)).astype(MXU_DTYPE)
        dec = jnp.exp(bl)
        v = vb_ref[...].astype(MXU_DTYPE)
        g = gb_ref[...]
        silu = g * _sigmoid(g)
        for ci in range(ch):
            rows = slice(GLA_CHUNK * ci, GLA_CHUNK * (ci + 1))
            qds, kis, kes = (_head_stack(t[rows], GLA_DK) for t in (qd, ki, ke))
            a = jnp.where(causal, _mm_nt(qd[rows], kis), 0.0).astype(MXU_DTYPE)
            st = st_ref[...]
            sp_ref[ci] = st
            o = _mm(a, _head_stack(v[rows], GLA_DV)) + _rows_to_heads(_mm_nt(qds, st.astype(MXU_DTYPE)))
            st_ref[...] = st * dec[rows][0:1] + _mm_tn(_heads_to_rows(v[rows]), kes)
            o_ref[rows, :] = o
            for h in range(GLA_HEADS):
                lv = slice(GLA_DV * h, GLA_DV * (h + 1))
                oh = o[:, lv]
                r = lax.rsqrt(jnp.mean(oh * oh, axis=1, keepdims=True) + EPS)
                cat_ref[rows, lv] = (oh * r * nw * silu[rows, lv]).astype(cat_ref.dtype)

    return pl.pallas_call(
        body, name="gla_fwd", grid=(s // tb,),
        in_specs=[_rows(tb, 256), _rows(tb, 256), _rows(tb, 512), _rows(tb, 256), _rows(tb, 512), _full((1, 128))],
        out_specs=[_rows(tb, 512), _rows(tb, 512), pl.BlockSpec((ch, GLA_DV, 256), lambda i: (i, 0, 0))],
        out_shape=[jax.ShapeDtypeStruct((s, 512), F32), jax.ShapeDtypeStruct((s, 512), MXU_DTYPE),
                   jax.ShapeDtypeStruct((s // GLA_CHUNK, GLA_DV, 256), F32)],
        scratch_shapes=[pltpu.VMEM((GLA_DV, GLA_KW), F32)],
        compiler_params=_cparams(dimension_semantics=("arbitrary",)),
    )(qb, kb, vb, la, gb, norm_w)


def _out_ln_loss(cat_a, cat_b, w_out, x, target, ln_g, ln_b):
    s = x.shape[0]
    ts = min(512, s)
    halves = 2 if ts % 32 == 0 else 1
    th = ts // halves

    def body(ca_ref, cb_ref, w_ref, x_ref, t_ref, g_ref, b_ref,
             loss_ref, gx_ref, da_ref, db_ref, gw_ref, gln_ref):
        @pl.when(pl.program_id(0) == 0)
        def _():
            loss_ref[...] = jnp.zeros_like(loss_ref)
            gw_ref[...] = jnp.zeros_like(gw_ref)
            gln_ref[...] = jnp.zeros_like(gln_ref)

        g = g_ref[...]
        dh16s = []
        for k in range(halves):
            rows = slice(th * k, th * (k + 1))
            mix = _mm(ca_ref[rows, :], w_ref[0:512, :]) + _mm(cb_ref[rows, :], w_ref[512:1024, :])
            h = ALPHA * x_ref[rows, :] + mix
            mu = jnp.mean(h, axis=1, keepdims=True)
            hc = h - mu
            rstd = lax.rsqrt(jnp.mean(hc * hc, axis=1, keepdims=True) + EPS)
            xhat = hc * rstd
            err = xhat * g + b_ref[...] - t_ref[rows, :]
            loss_ref[...] += 0.5 * jnp.sum(jnp.mean(err * err, axis=1, keepdims=True))
            dy = err * (1.0 / D_MODEL)
            gln_ref[0:1, :] += jnp.sum(dy * xhat, axis=0, keepdims=True)
            gln_ref[1:2, :] += jnp.sum(dy, axis=0, keepdims=True)
            dxh = dy * g
            dh = rstd * (dxh - jnp.mean(dxh, axis=1, keepdims=True)
                         - xhat * jnp.mean(dxh * xhat, axis=1, keepdims=True))
            gx_ref[rows, :] = ALPHA * dh
            dh16s.append(dh.astype(MXU_DTYPE))
        for k in range(halves):
            rows = slice(th * k, th * (k + 1))
            da_ref[rows, :] = _mm_nt(dh16s[k], w_ref[0:512, :])
            db_ref[rows, :] = _mm_nt(dh16s[k], w_ref[512:1024, :])
        dh16 = jnp.concatenate(dh16s, axis=0)
        gw_ref[0:512, :] += _mm_tn(ca_ref[...], dh16)
        gw_ref[512:1024, :] += _mm_tn(cb_ref[...], dh16)

    return pl.pallas_call(
        body, name="out_ln_loss", grid=(s // ts,),
        in_specs=[_rows(ts, 512), _rows(ts, 512), _full((D_MODEL, D_MODEL)), _rows(ts, D_MODEL), _rows(ts, D_MODEL),
                  _full((1, D_MODEL)), _full((1, D_MODEL))],
        out_specs=[_full((1, 128)), _rows(ts, D_MODEL), _rows(ts, 512), _rows(ts, 512),
                   _full((D_MODEL, D_MODEL)), _full((2, D_MODEL))],
        out_shape=[jax.ShapeDtypeStruct((1, 128), F32), jax.ShapeDtypeStruct((s, D_MODEL), F32),
                   jax.ShapeDtypeStruct((s, 512), F32), jax.ShapeDtypeStruct((s, 512), F32),
                   jax.ShapeDtypeStruct((D_MODEL, D_MODEL), F32), jax.ShapeDtypeStruct((2, D_MODEL), F32)],
        compiler_params=_cparams(dimension_semantics=("arbitrary",)),
    )(cat_a, cat_b, w_out, x, target, ln_g, ln_b)


def _swa_bwd(sinks, qa, k_pad, v_pad, attn, ga, d_cat_a, rope, parts_w_out):
    s = qa.shape[0]
    sub = min(SWA_SUB, s // BLOCK)
    tq = sub * BLOCK
    nsteps = s // tq
    forward_step = min(1, nsteps - 1)

    def body(sink_ref, qa_ref, ga_ref, at_ref, dc_ref, c_ref, s1_ref, s2_ref, bias_ref, k_ref, v_ref, pout_ref,
             dq_ref, dg_ref, dk_out, dv_out, ds_ref, gout_ref, dk_ref, dv_ref, *scratch):
        n = pl.program_id(0)
        owner_sum = _OwnerSum(pout_ref, *scratch)

        @pl.when(n == 0)
        def _():
            dk_ref[...] = jnp.zeros_like(dk_ref)
            dv_ref[...] = jnp.zeros_like(dv_ref)
            ds_ref[...] = jnp.zeros_like(ds_ref)
            owner_sum.start()

        @pl.when(n == forward_step)
        def _():
            owner_sum.forward()

        @pl.when(n == nsteps - 1)
        def _():
            gout_ref[...] = owner_sum.finish()

        low = lax.broadcasted_iota(jnp.int32, (2 * BLOCK, 128), 1) < SWA_HEAD_DIM
        for b in range(sub):
            rows = slice(BLOCK * b, BLOCK * (b + 1))
            start = pl.multiple_of((n * sub + b) * BLOCK, BLOCK)
            kw = k_ref[pl.ds(start, 2 * BLOCK), :]
            vw = v_ref[pl.ds(start, 2 * BLOCK), :]
            bias_t = _swa_bias_of(bias_ref, n, b)
            q = qa_ref[rows, :] * SWA_SCALE
            g = ga_ref[rows, :]
            sg = _sigmoid(g)
            o = at_ref[rows, :]
            dc = dc_ref[rows, :]
            do = dc * (g * sg)
            dg_ref[rows, :] = (dc * o * (sg * (1.0 + g * (1.0 - sg)))).astype(dg_ref.dtype)
            od = do * o
            c, s1, s2 = c_ref[rows, :], s1_ref[rows, :], s2_ref[rows, :]
            dk, dv = [], []
            for j in range(SWA_KV_HEADS):
                kd, vd = _swa_dup(kw, j), _swa_dup(vw, j)
                qs = _swa_stack(q, j).astype(MXU_DTYPE)
                dos = _swa_stack(do, j).astype(MXU_DTYPE)
                probs, psink = _swa_probs_t(qs, kd, bias_t, _swa_sink_row(sink_ref, j))
                delta = _row_sums_as_row(_swa_stack(od, j))
                dsc = (probs * (_mm_nt(vd, dos) - delta)).astype(MXU_DTYPE)
                dsink = psink * delta
                for r in range(SWA_GROUP):
                    h = SWA_GROUP * j + r
                    ds_ref[h:h + 1, :] += jnp.zeros((1, 128), F32) - jnp.sum(dsink[:, BLOCK * r:BLOCK * (r + 1)])
                dq = _swa_unstack(_mm_tn(dsc, kd))
                for i in range(2):
                    lanes = slice(128 * (2 * j + i), 128 * (2 * j + i + 1))
                    dq_ref[rows, lanes] = _rope_t(dq[i] * SWA_SCALE, c, s1, s2).astype(dq_ref.dtype)
                dkj = _mm(dsc, qs)
                dvj = _mm(probs.astype(MXU_DTYPE), dos)
                dk.append(dkj + pltpu.roll(dkj, SWA_HEAD_DIM, 1))
                dv.append(dvj + pltpu.roll(dvj, SWA_HEAD_DIM, 1))
            dk_ref[pl.ds(start, 2 * BLOCK), :] += jnp.where(low, dk[0], dk[1])
            dv_ref[pl.ds(start, 2 * BLOCK), :] += jnp.where(low, dv[0], dv[1])

        @pl.when(n == nsteps - 1)
        def _():
            dk_out[...] = dk_ref[BLOCK:, :]
            dv_out[...] = dv_ref[BLOCK:, :]

    out_blk = parts_w_out.shape[1:]
    return pl.pallas_call(
        body, name="swa_bwd", grid=(nsteps,),
        in_specs=[pl.BlockSpec(memory_space=pltpu.SMEM)] + [_rows(tq, 512)] * 4 + [_rows(tq, 128)] * 3
        + [_full((2, 2 * BLOCK, SWA_ROWS))] + [_full((s + BLOCK, 128))] * 2 + [pl.BlockSpec(memory_space=pl.ANY)],
        out_specs=[_rows(tq, 512), _rows(tq, 512), _full((s, 128)), _full((s, 128)),
                   _full((SWA_Q_HEADS, 128)), _full(out_blk)],
        out_shape=[jax.ShapeDtypeStruct((s, 512), MXU_DTYPE), jax.ShapeDtypeStruct((s, 512), MXU_DTYPE),
                   jax.ShapeDtypeStruct((s, 128), F32), jax.ShapeDtypeStruct((s, 128), F32),
                   jax.ShapeDtypeStruct((SWA_Q_HEADS, 128), F32), jax.ShapeDtypeStruct(out_blk, F32)],
        scratch_shapes=[pltpu.VMEM((s + BLOCK, 128), F32)] * 2 + _OwnerSum.scratch(out_blk),
        compiler_params=_cparams(dimension_semantics=("arbitrary",)),
    )(sinks, qa, ga, attn, d_cat_a, *rope, _swa_bias(), k_pad, v_pad, parts_w_out)


def _gla_bwd(qb, kb, vb, la, oms, gb, o, sprev, d_cat_b, rb, wg, norm_w):
    s = qb.shape[0]
    tb = min(512, s)
    ch = tb // GLA_CHUNK
    nb = s // tb

    def body(qb_ref, kb_ref, vb_ref, la_ref, oms_ref, gb_ref, o_ref, sp_ref, dc_ref, rb_ref, wg_ref, nw_ref,
             dq_ref, dk_ref, dv_ref, dg_ref, dr_ref, gwg_ref, gbg_ref, gnw_ref, dst_ref):
        @pl.when(pl.program_id(0) == 0)
        def _():
            dst_ref[...] = jnp.zeros_like(dst_ref)
            gwg_ref[...] = jnp.zeros_like(gwg_ref)
            gbg_ref[...] = jnp.zeros_like(gbg_ref)
            gnw_ref[...] = jnp.zeros_like(gnw_ref)

        causal, causal_t = _gla_masks()
        nw = nw_ref[...]
        b = _chunk_cumsum(la_ref[...], True)
        bl = _chunk_last(b)
        eb, enb, ee, dec = jnp.exp(b), jnp.exp(-b), jnp.exp(bl - b), jnp.exp(bl)
        k = kb_ref[...]
        qd = (qb_ref[...] * GLA_SCALE) * eb
        ki = k * enb
        ke = k * ee
        qd16, ki16, ke16 = qd.astype(MXU_DTYPE), ki.astype(MXU_DTYPE), ke.astype(MXU_DTYPE)
        v16 = vb_ref[...].astype(MXU_DTYPE)

        g = gb_ref[...]
        sg = _sigmoid(g)
        silu = g * sg
        dsilu = sg * (1.0 + g * (1.0 - sg))
        gnw = jnp.zeros((1, GLA_DV), F32)
        do = []
        for h in range(GLA_HEADS):
            lv = slice(GLA_DV * h, GLA_DV * (h + 1))
            oh = o_ref[:, lv]
            dch = dc_ref[:, lv]
            r = lax.rsqrt(jnp.mean(oh * oh, axis=1, keepdims=True) + EPS)
            d_on = dch * silu[:, lv]
            dg_ref[:, lv] = (dch * (oh * r * nw) * dsilu[:, lv]).astype(dg_ref.dtype)
            gnw += jnp.sum(d_on * oh * r, axis=0, keepdims=True)
            u = d_on * nw
            do.append(r * u - oh * (r * r * r) * jnp.mean(u * oh, axis=1, keepdims=True))
        gnw_ref[...] += gnw
        do16 = jnp.concatenate(do, axis=1).astype(MXU_DTYPE)

        db, dbl = [None] * ch, [None] * ch
        for ci in reversed(range(ch)):
            rows = slice(GLA_CHUNK * ci, GLA_CHUNK * (ci + 1))
            qds, kis, kes = (_head_stack(t[rows], GLA_DK) for t in (qd16, ki16, ke16))
            vs, dos = _head_stack(v16[rows], GLA_DV), _head_stack(do16[rows], GLA_DV)
            a = jnp.where(causal, _mm_nt(qd16[rows], kis), 0.0).astype(MXU_DTYPE)
            at = jnp.where(causal_t, _mm_nt(ki16[rows], qds), 0.0).astype(MXU_DTYPE)
            da = jnp.where(causal, _mm_nt(do16[rows], vs), 0.0).astype(MXU_DTYPE)
            dat = jnp.where(causal_t, _mm_nt(v16[rows], dos), 0.0).astype(MXU_DTYPE)
            st = sp_ref[ci]
            dst = dst_ref[...]
            dst16 = dst.astype(MXU_DTYPE)
            dv = _mm(at, dos) + _rows_to_heads(_mm_nt(kes, dst16))
            dqd = _mm(da, kis) + _mm(do16[rows], _state_by_head(st.astype(MXU_DTYPE)))
            dki = _mm(dat, qds)
            dke = _mm(v16[rows], _state_by_head(dst16))
            ddec = jnp.sum(dst * st, axis=0, keepdims=True)
            decc = dec[rows][0:1]
            dst_ref[...] = _mm_tn(_heads_to_rows(do16[rows]), qds) + dst * decc
            dq_ref[rows, :] = (dqd * eb[rows] * GLA_SCALE).astype(dq_ref.dtype)
            dk_ref[rows, :] = (dki * enb[rows] + dke * ee[rows]).astype(dk_ref.dtype)
            dv_ref[rows, :] = dv.astype(dv_ref.dtype)
            dke_ke = dke * ke[rows]
            db[ci] = dqd * qd[rows] - dki * ki[rows] - dke_ke
            dbl[ci] = jnp.broadcast_to(jnp.sum(dke_ke, axis=0, keepdims=True) + ddec * decc, (GLA_CHUNK, GLA_KW))

        dla = _chunk_cumsum(jnp.concatenate(db, axis=0), False) + jnp.concatenate(dbl, axis=0)
        dlogit = dla * oms_ref[...] * (1.0 / GLA_TAU)
        dl16 = dlogit.astype(MXU_DTYPE)
        gbg_ref[...] += jnp.sum(dlogit, axis=0, keepdims=True)
        gwg_ref[...] += _mm_tn(rb_ref[...].astype(MXU_DTYPE), dl16)
        dr_ref[...] = _mm_nt(dl16, wg_ref[...]).astype(dr_ref.dtype)

    def rev(width):
        return pl.BlockSpec((tb, width), lambda i: (nb - 1 - i, 0))

    return pl.pallas_call(
        body, name="gla_bwd", grid=(nb,),
        in_specs=[rev(256), rev(256), rev(512), rev(256), rev(256), rev(512), rev(512),
                  pl.BlockSpec((ch, GLA_DV, 256), lambda i: (nb - 1 - i, 0, 0)), rev(512), rev(GLA_RANK),
                  _full((GLA_RANK, 256)), _full((1, 128))],
        out_specs=[rev(256), rev(256), rev(512), rev(512), rev(GLA_RANK),
                   _full((GLA_RANK, 256)), _full((1, 256)), _full((1, 128))],
        out_shape=[jax.ShapeDtypeStruct((s, 256), MXU_DTYPE), jax.ShapeDtypeStruct((s, 256), MXU_DTYPE),
                   jax.ShapeDtypeStruct((s, 512), MXU_DTYPE), jax.ShapeDtypeStruct((s, 512), MXU_DTYPE),
                   jax.ShapeDtypeStruct((s, GLA_RANK), MXU_DTYPE), jax.ShapeDtypeStruct((GLA_RANK, 256), F32),
                   jax.ShapeDtypeStruct((1, 256), F32), jax.ShapeDtypeStruct((1, 128), F32)],
        scratch_shapes=[pltpu.VMEM((GLA_DV, GLA_KW), F32)],
        compiler_params=_cparams(dimension_semantics=("arbitrary",)),
    )(qb, kb, vb, la, oms, gb, o, sprev, d_cat_b, rb, wg, norm_w)


def _dproj_tiles(piece_refs, rope_refs, members=(0, 1, 3, 4, 5, 6, 7, 8)):
    for i in members:
        if i == 1:
            dk = _rope_t(piece_refs[1][...], *(r[...] for r in rope_refs))
            yield OFF[1], OFF[3], jnp.concatenate([dk, piece_refs[2][...]], axis=1).astype(MXU_DTYPE)
        else:
            yield OFF[i], OFF[i + 1], piece_refs[i][...].astype(MXU_DTYPE)


def _in_proj_bwd_x(gx0, pieces, w_in, rope):
    s = gx0.shape[0]
    ts = min(512, s)
    widths = [OFF[i + 1] - OFF[i] for i in range(9)]

    def body(gx0_ref, *refs):
        w_ref, gx_ref = refs[12:]
        acc = gx0_ref[...]
        for lo, hi, t16 in _dproj_tiles(refs[:9], refs[9:12]):
            acc += _mm(t16, w_ref[lo:hi, :])
        gx_ref[...] = acc

    return pl.pallas_call(
        body, name="in_proj_bwd_x", grid=(s // ts,),
        in_specs=[_rows(ts, D_MODEL)] + [_rows(ts, w) for w in widths] + [_rows(ts, 128)] * 3
        + [_full((D_IN_PROJ, D_MODEL))],
        out_specs=_rows(ts, D_MODEL),
        out_shape=jax.ShapeDtypeStruct((s, D_MODEL), F32),
        compiler_params=_cparams(dimension_semantics=("arbitrary",)),
    )(gx0, *pieces, *rope, w_in)


GW_GROUPS = ((0, 1), (3, 4), (5, 6), (7, 8))


def _in_proj_bwd_w(x, pieces, rope, parts_wg, g_ln, g_bg, g_nw, g_sinks, loss):
    s = x.shape[0]
    ts = min(1024, s)
    nt = s // ts
    n_groups = len(GW_GROUPS)
    ends = [OFF[3] if members[-1] == 1 else OFF[members[-1] + 1] for members in GW_GROUPS]
    assert all(ends[g] >= D_IN_SHARD * 2 * (g + 1) for g in range(n_groups))
    blk = (D_IN_SHARD, D_MODEL)

    def body(x_hbm, *refs):
        piece_refs, rope_refs = refs[:9], refs[9:12]
        pwg_ref, gln_ref, gbg_ref, gnw_ref, gsk_ref, loss_ref, gin_ref, rwg_ref, rsm_ref = refs[12:21]
        (acc_ref, stage_ref, sib_ref, snd_ref, rcv_ref, mine_ref, sm_ref, xs_ref,
         d2d_send, d2d_recv, ici_send, ici_recv, out_sem, sm_send, sm_recv, sm_loc, x_sems) = refs[21:]
        g, t = pl.program_id(0), pl.program_id(1)
        tile = pl.ds(pl.multiple_of(t * ts, ts), ts)

        def x_load(k):
            rows = pl.ds(pl.multiple_of(k * ts, ts), ts)
            return pltpu.make_async_copy(x_hbm.at[rows, :], xs_ref.at[rows, :], x_sems.at[k])
        x_, y_, c = _mesh_pos()
        me, mychip, sibling = 4 * x_ + 2 * y_ + c, 2 * x_ + y_, (x_, y_, 1 - c)
        small_dsts = (rwg_ref, rsm_ref)

        def small_src(a, block):
            return pwg_ref.at[block] if a == 0 else sm_ref

        def small_copy(k, a, src_block, dst_block, peer):
            i = 2 * (k - 1) + a
            return pltpu.make_async_remote_copy(
                src_ref=small_src(a, src_block), dst_ref=small_dsts[a].at[dst_block], send_sem=sm_send.at[i],
                recv_sem=sm_recv.at[i], device_id=peer, device_id_type=pl.DeviceIdType.MESH)

        def small_local(a):
            return pltpu.make_async_copy(small_src(a, me), small_dsts[a].at[me], sm_loc.at[a])

        @pl.when((g == 0) & (t == 0))
        def _():
            for k in range(nt):
                x_load(k).start()
            acc_ref[...] = jnp.zeros_like(acc_ref)
            sm_ref[...] = jnp.zeros_like(sm_ref)
            for r in range(D_MODEL // 128):
                sm_ref[r:r + 1, :] = gln_ref[0:1, 128 * r:128 * (r + 1)]
                sm_ref[8 + r:9 + r, :] = gln_ref[1:2, 128 * r:128 * (r + 1)]
            for r in range(2):
                sm_ref[16 + r:17 + r, :] = gbg_ref[0:1, 128 * r:128 * (r + 1)]
            sm_ref[24:25, :] = gnw_ref[...]
            diag = (lax.broadcasted_iota(jnp.int32, gsk_ref.shape, 0)
                    == lax.broadcasted_iota(jnp.int32, gsk_ref.shape, 1))
            sm_ref[32:33, :] = jnp.sum(jnp.where(diag, gsk_ref[...], 0.0), axis=0, keepdims=True)
            sm_ref[40:41, :] = loss_ref[...]
            for a in range(2):
                small_local(a).start()
            for k in range(1, N_DEV):
                peer, pidx = _peer(k, x_, y_, c)
                for a in range(2):
                    small_copy(k, a, pidx, me, peer).start()

        @pl.when(g == 0)
        def _():
            x_load(t).wait()

        xb = xs_ref[tile, :]
        for gi, members in enumerate(GW_GROUPS):
            @pl.when(g == gi)
            def _(members=members):
                for lo, hi, t16 in _dproj_tiles(piece_refs, rope_refs, members):
                    acc_ref[lo:hi, :] += _mm_tn(t16, xb)

        def block_rows(j):
            return acc_ref[D_IN_SHARD * j:D_IN_SHARD * (j + 1), :]

        def d2d(gi):
            return pltpu.make_async_remote_copy(
                src_ref=stage_ref.at[gi % 2], dst_ref=sib_ref.at[gi], send_sem=d2d_send.at[gi],
                recv_sem=d2d_recv.at[gi], device_id=sibling, device_id_type=pl.DeviceIdType.MESH)

        def ici(slot, owner):
            return pltpu.make_async_remote_copy(
                src_ref=snd_ref.at[slot], dst_ref=rcv_ref.at[slot], send_sem=ici_send.at[slot],
                recv_sem=ici_recv.at[slot], device_id=owner, device_id_type=pl.DeviceIdType.MESH)

        def to_sibling(gi):
            if gi >= 2:
                d2d(gi - 2).wait_send()
            for cc in range(2):
                @pl.when(c == cc)
                def _(cc=cc):
                    stage_ref[gi % 2] = block_rows(2 * gi + 1 - cc)
            d2d(gi).start()

        def chip_sum(gi):
            d2d(gi).wait_recv()
            gx, gy = gi // 2, gi % 2
            for cc in range(2):
                @pl.when(c == cc)
                def _(cc=cc):
                    total = block_rows(2 * gi + cc) + sib_ref[gi]

                    @pl.when(mychip == gi)
                    def _():
                        mine_ref[...] = total

                    @pl.when(mychip != gi)
                    def _():
                        slot = jnp.where(x_ == gx, 0, 1) + 2 * jnp.where(y_ == gy, 0, 1) - 1
                        snd_ref[slot] = total.astype(snd_ref.dtype)
                        ici(slot, (gx, gy, c)).start()

        for gi in range(n_groups):
            @pl.when((g == gi) & (t == nt - 1))
            def _(gi=gi):
                to_sibling(gi)
                if gi == n_groups - 1:
                    for k in range(n_groups):
                        chip_sum(k)
                    total = mine_ref[...]
                    for slot in range(3):
                        ici(slot, sibling).wait_recv()
                        total = total + rcv_ref[slot].astype(F32)
                    mine_ref[...] = total
                    out = pltpu.make_async_copy(mine_ref, gin_ref, out_sem)
                    out.start()
                    for k in range(1, N_DEV):
                        peer, pidx = _peer(k, x_, y_, c)
                        for a in range(2):
                            small_copy(k, a, me, pidx, peer).wait_recv()
                    for k in range(1, N_DEV):
                        peer, pidx = _peer(k, x_, y_, c)
                        for a in range(2):
                            small_copy(k, a, pidx, me, peer).wait_send()
                    for a in range(2):
                        small_local(a).wait()
                    d2d(gi - 1).wait_send()
                    d2d(gi).wait_send()
                    for slot in range(3):
                        ici(slot, sibling).wait_send()
                    out.wait()

    def piece_spec(i, width):
        gi = next(k for k, members in enumerate(GW_GROUPS) if (i in members or (i == 2 and 1 in members)))
        return pl.BlockSpec((ts, width), lambda g, t: (jnp.where(g == gi, t, jnp.where(g < gi, 0, nt - 1)), 0))

    widths = [OFF[i + 1] - OFF[i] for i in range(9)]
    hbm = pl.BlockSpec(memory_space=pl.ANY)
    vmem = pl.BlockSpec(memory_space=pltpu.VMEM)
    rope_spec = pl.BlockSpec((ts, 128), lambda g, t: (jnp.where(g == 0, t, nt - 1), 0))
    return pl.pallas_call(
        body, name="in_proj_bwd_w", grid=(n_groups, nt),
        in_specs=[hbm] + [piece_spec(i, w) for i, w in enumerate(widths)] + [rope_spec] * 3 + [hbm] + [vmem] * 5,
        out_specs=[hbm, hbm, hbm],
        out_shape=[jax.ShapeDtypeStruct(blk, F32), jax.ShapeDtypeStruct((N_DEV,) + parts_wg.shape[1:], F32),
                   jax.ShapeDtypeStruct((N_DEV, SMALL_ROWS, 128), F32)],
        scratch_shapes=[pltpu.VMEM((D_IN_PROJ, D_MODEL), F32), pltpu.VMEM((2,) + blk, F32),
                        pltpu.VMEM((n_groups,) + blk, F32), pltpu.VMEM((3,) + blk, MXU_DTYPE),
                        pltpu.VMEM((3,) + blk, MXU_DTYPE), pltpu.VMEM(blk, F32), pltpu.VMEM((SMALL_ROWS, 128), F32),
                        pltpu.VMEM((s, D_MODEL), MXU_DTYPE),
                        pltpu.SemaphoreType.DMA((n_groups,)), pltpu.SemaphoreType.DMA((n_groups,)),
                        pltpu.SemaphoreType.DMA((3,)), pltpu.SemaphoreType.DMA((3,)), pltpu.SemaphoreType.DMA,
                        pltpu.SemaphoreType.DMA((2 * (N_DEV - 1),)), pltpu.SemaphoreType.DMA((2 * (N_DEV - 1),)),
                        pltpu.SemaphoreType.DMA((2,)), pltpu.SemaphoreType.DMA((nt,))],
        compiler_params=_cparams(dimension_semantics=("arbitrary", "arbitrary")),
    )(x, *pieces, *rope, parts_wg, g_ln, g_bg, g_nw, g_sinks, loss)


def _local_step(x, positions, w_in_t, wg_s, b_gate, sinks, norm_w, w_out_s, ln_g, ln_b, target):
    qa, k_pad, v_pad, ga, qb, kb, vb, gb, rb, la, oms, *rope, x16, w_in, wg, w_out = _in_proj(
        x, w_in_t, wg_s, b_gate, _rope_angles(positions), w_out_s)
    attn, cat_a = _swa_fwd(sinks, qa, k_pad, v_pad, ga)
    o, cat_b, sprev = _gla_fwd(qb, kb, vb, la, gb, norm_w)
    loss, gx0, d_cat_a, d_cat_b, g_w_out, g_ln = _out_ln_loss(cat_a, cat_b, w_out, x, target, ln_g, ln_b)
    parts_w_out = g_w_out.reshape(N_DEV, D_OUT_SHARD, D_MODEL)
    dqa, dga, dka, dva, g_sinks, g_out = _swa_bwd(sinks, qa, k_pad, v_pad, attn, ga, d_cat_a, rope, parts_w_out)
    dqb, dkb, dvb, dgb, drb, g_wg, g_bg, g_nw = _gla_bwd(qb, kb, vb, la, oms, gb, o, sprev, d_cat_b, rb, wg, norm_w)
    pieces = (dqa, dka, dva, dga, dqb, dkb, dvb, dgb, drb)
    grad_x = _in_proj_bwd_x(gx0, pieces, w_in, rope)
    parts_wg = jnp.transpose(g_wg.reshape(GLA_RANK, N_DEV, 32), (1, 0, 2))
    g_in, r_wg, r_small = _in_proj_bwd_w(x16, pieces, rope, parts_wg, g_ln, g_bg, g_nw, g_sinks, loss)
    return grad_x, g_in, g_out, r_wg, r_small


def _mesh_pos():
    return lax.axis_index("x"), lax.axis_index("y"), lax.axis_index("c")


def _peer(k, x, y, c):
    px = (1 - x) if k & 4 else x
    py = (1 - y) if k & 2 else y
    pc = (1 - c) if k & 1 else c
    return (px, py, pc), 4 * px + 2 * py + pc


def _other_chips(x, y):
    return [(1 - x, y), (x, 1 - y), (1 - x, 1 - y)]


def _shard_view(t):
    return jnp.transpose(t, (2, 0, 1))


class _BlockGather:
    def __init__(self, slots, send_sems, recv_sems):
        self.slots, self.send_sems, self.recv_sems = slots, send_sems, recv_sems
        x, y, c = _mesh_pos()
        self.xy, self.c, self.me, self.sibling = (x, y), c, 4 * x + 2 * y + c, (x, y, 1 - c)
        self.chips = _other_chips(x, y)

    @staticmethod
    def scratch():
        return [pltpu.SemaphoreType.DMA((N_DEV - 1,)), pltpu.SemaphoreType.DMA((N_DEV - 1,))]

    def _copy(self, k, block, to):
        return pltpu.make_async_remote_copy(
            src_ref=self.slots.at[block], dst_ref=self.slots.at[block], send_sem=self.send_sems.at[k],
            recv_sem=self.recv_sems.at[k], device_id=to, device_id_type=pl.DeviceIdType.MESH)

    def start(self):
        for j, (cx, cy) in enumerate(self.chips):
            self._copy(1 + j, self.me, (cx, cy, self.c)).start()
        self._copy(0, self.me, self.sibling).start()

    def forward(self):
        for j, (cx, cy) in enumerate(self.chips):
            block = 4 * cx + 2 * cy + self.c
            self._copy(1 + j, block, self.sibling).wait_recv()
            self._copy(4 + j, block, self.sibling).start()

    def finish(self):
        x, y = self.xy
        self._copy(0, 4 * x + 2 * y + (1 - self.c), self.sibling).wait_recv()
        for j, (cx, cy) in enumerate(self.chips):
            self._copy(4 + j, 4 * cx + 2 * cy + (1 - self.c), self.sibling).wait_recv()
        for k in range(N_DEV - 1):
            self._copy(k, self.me, self.sibling).wait_send()


class _OwnerSum:
    def __init__(self, parts, own, sib, snd, rcv, loc_sems, d2d_send, d2d_recv, ici_send, ici_recv):
        self.parts, self.own, self.sib, self.snd, self.rcv = parts, own, sib, snd, rcv
        self.sems = (loc_sems, d2d_send, d2d_recv, ici_send, ici_recv)
        x, y, c = _mesh_pos()
        self.c, self.sibling = c, (x, y, 1 - c)
        self.chips = [(x, y)] + _other_chips(x, y)

    @staticmethod
    def scratch(block):
        return [pltpu.VMEM((4,) + block, F32), pltpu.VMEM((4,) + block, F32),
                pltpu.VMEM((3,) + block, MXU_DTYPE), pltpu.VMEM((3,) + block, MXU_DTYPE),
                pltpu.SemaphoreType.DMA((4,)), pltpu.SemaphoreType.DMA((4,)), pltpu.SemaphoreType.DMA((4,)),
                pltpu.SemaphoreType.DMA((3,)), pltpu.SemaphoreType.DMA((3,))]

    def _local(self, r):
        cx, cy = self.chips[r]
        return pltpu.make_async_copy(self.parts.at[4 * cx + 2 * cy + self.c], self.own.at[r], self.sems[0].at[r])

    def _d2d(self, r):
        cx, cy = self.chips[r]
        return pltpu.make_async_remote_copy(
            src_ref=self.parts.at[4 * cx + 2 * cy + (1 - self.c)], dst_ref=self.sib.at[r], send_sem=self.sems[1].at[r],
            recv_sem=self.sems[2].at[r], device_id=self.sibling, device_id_type=pl.DeviceIdType.MESH)

    def _ici(self, r):
        cx, cy = self.chips[r]
        return pltpu.make_async_remote_copy(
            src_ref=self.snd.at[r - 1], dst_ref=self.rcv.at[r - 1], send_sem=self.sems[3].at[r - 1],
            recv_sem=self.sems[4].at[r - 1], device_id=(cx, cy, self.c), device_id_type=pl.DeviceIdType.MESH)

    def start(self):
        for r in (1, 2, 3, 0):
            self._local(r).start()
            self._d2d(r).start()

    def forward(self):
        for r in (1, 2, 3):
            self._local(r).wait()
            self._d2d(r).wait_recv()
            self.snd[r - 1] = (self.own[r] + self.sib[r]).astype(self.snd.dtype)
            self._ici(r).start()

    def finish(self):
        self._local(0).wait()
        self._d2d(0).wait_recv()
        acc = self.own[0] + self.sib[0]
        for r in (1, 2, 3):
            self._ici(r).wait_recv()
            acc = acc + self.rcv[r - 1].astype(F32)
        for r in range(4):
            self._d2d(r).wait_send()
        for r in (1, 2, 3):
            self._ici(r).wait_send()
        return acc


SMALL_ROWS = 48


def _adamw_math(g, w, m, v):
    nm = ADAM_B1 * m + (1.0 - ADAM_B1) * g
    nv = ADAM_B2 * v + (1.0 - ADAM_B2) * (g * g)
    m_hat = nm / (1.0 - ADAM_B1 ** ADAM_STEP)
    v_hat = nv / (1.0 - ADAM_B2 ** ADAM_STEP)
    return -ADAM_LR * (m_hat / (jnp.sqrt(v_hat) + ADAM_EPS) + ADAM_WD * w), nm, nv


def _adamw_shard_view(g, w, m, v):
    rows, width = g.shape

    def body(g_ref, w_hbm, m_hbm, v_hbm, g_out, d_out, nm_out, nv_out, bufs, outs, sems):
        loads = [pltpu.make_async_copy(src.at[:, 0, :], bufs.at[i], sems.at[i])
                 for i, src in enumerate((w_hbm, m_hbm, v_hbm))]
        for cp in loads:
            cp.start()
        g = g_ref[...]
        for cp in loads:
            cp.wait()
        outs[0] = g
        outs[1], outs[2], outs[3] = _adamw_math(g, bufs[0], bufs[1], bufs[2])
        stores = [pltpu.make_async_copy(outs.at[i], dst.at[:, 0, :], sems.at[3 + i])
                  for i, dst in enumerate((g_out, d_out, nm_out, nv_out))]
        for cp in stores:
            cp.start()
        for cp in stores:
            cp.wait()

    hbm = pl.BlockSpec(memory_space=pl.ANY)
    return pl.pallas_call(
        body, name="adamw_w_in",
        in_specs=[pl.BlockSpec(memory_space=pltpu.VMEM), hbm, hbm, hbm], out_specs=[hbm] * 4,
        out_shape=[jax.ShapeDtypeStruct((rows, 1, width), F32)] * 4,
        scratch_shapes=[pltpu.VMEM((3, rows, width), F32), pltpu.VMEM((4, rows, width), F32),
                        pltpu.SemaphoreType.DMA((7,))],
        compiler_params=_cparams(),
    )(g, w, m, v)


def _adamw_vectors(r_small, r_wg, g_out, params):
    n_par = len(params)

    def body(rsm_ref, rwg_ref, gout_ref, *refs):
        ins, outs = refs[:3 * n_par], refs[3 * n_par:]
        g = rsm_ref[0]
        gwg = rwg_ref[0]
        for j in range(1, N_DEV):
            g = g + rsm_ref[j]
            gwg = gwg + rwg_ref[j]
        outs[4 * n_par][...] = g[40:41]
        grads = [gwg,
                 jnp.concatenate([g[r:r + 1] for r in range(0, 8)], axis=1),
                 jnp.concatenate([g[r:r + 1] for r in range(8, 16)], axis=1),
                 jnp.concatenate([g[16:17], g[17:18]], axis=1),
                 g[24:25],
                 g[32:33, 0:SWA_Q_HEADS],
                 gout_ref[...]]
        for p, gp in enumerate(grads):
            w_ref, m_ref, v_ref = ins[3 * p:3 * p + 3]
            outs[4 * p][...] = gp
            outs[4 * p + 1][...], outs[4 * p + 2][...], outs[4 * p + 3][...] = _adamw_math(
                gp, w_ref[...], m_ref[...], v_ref[...])

    vmem = pl.BlockSpec(memory_space=pltpu.VMEM)
    flat = [t for wmv in params for t in wmv]
    return pl.pallas_call(
        body, name="adamw_vectors",
        in_specs=[vmem] * (3 + len(flat)), out_specs=[vmem] * (4 * n_par + 1),
        out_shape=[jax.ShapeDtypeStruct(wmv[0].shape, F32) for wmv in params for _ in range(4)]
        + [jax.ShapeDtypeStruct((1, 128), F32)],
        compiler_params=_cparams(),
    )(r_small, r_wg, g_out, *flat)


def kernel(x, positions, w_in, gla_w_gate_up, gla_b_gate, attn_sinks, gla_norm_w, w_out, ln_g, ln_b, loss_target, m_w_in, m_gla_w_gate_up, m_gla_b_gate, m_attn_sinks, m_gla_norm_w, m_w_out, m_ln_g, m_ln_b, v_w_in, v_gla_w_gate_up, v_gla_b_gate, v_attn_sinks, v_gla_norm_w, v_w_out, v_ln_g, v_ln_b):
    grad_x, g_in, g_out, r_wg, r_small = _local_step(
        x[0], positions[0], _shard_view(w_in), gla_w_gate_up[0], gla_b_gate, attn_sinks[0], gla_norm_w, w_out[0],
        ln_g, ln_b, loss_target[0])

    upd_in = _adamw_shard_view(g_in, _shard_view(w_in), _shard_view(m_w_in), _shard_view(v_w_in))
    upd_in = [jnp.transpose(t, (1, 2, 0)) for t in upd_in]
    vec = _adamw_vectors(r_small, r_wg, g_out, [
        (gla_w_gate_up[0], m_gla_w_gate_up[0], v_gla_w_gate_up[0]), (ln_g, m_ln_g, v_ln_g), (ln_b, m_ln_b, v_ln_b),
        (gla_b_gate, m_gla_b_gate, v_gla_b_gate), (gla_norm_w, m_gla_norm_w, v_gla_norm_w),
        (attn_sinks, m_attn_sinks, v_attn_sinks), (w_out[0], m_w_out[0], v_w_out[0])])

    outs = [vec[28][0, 0], grad_x[None]]
    for kind in range(4):
        u_wg, u_ln_g, u_ln_b, u_bg, u_nw, u_sinks, u_out = (vec[4 * p + kind] for p in range(7))
        outs += [upd_in[kind], u_wg[None], u_bg, u_sinks, u_nw, u_out[None], u_ln_g, u_ln_b]
    return tuple(outs)
```

```python
import jax
import jax.numpy as jnp
from jax import lax
from jax.experimental import pallas as pl
from jax.experimental.pallas import tpu as pltpu

F32 = jnp.float32
MXU_DTYPE = jnp.bfloat16

N_DEV = 8
D_MODEL = 1024
SWA_Q_HEADS = 8
SWA_KV_HEADS = 2
SWA_GROUP = 4
SWA_HEAD_DIM = 64
BLOCK = 128
ROPE_THETA = 500000.0
ROT_DIM = 16
GLA_HEADS = 4
GLA_DK = 64
GLA_DV = 128
GLA_RANK = 16
GLA_TAU = 16.0
GLA_CHUNK = 64
D_IN_PROJ = 2832
D_IN_SHARD = D_IN_PROJ // N_DEV
D_OUT_SHARD = D_MODEL // N_DEV
OFF = (0, 512, 640, 768, 1280, 1536, 1792, 2304, 2816, 2832)
EPS = 1e-5
ALPHA = 2.0 ** 0.25
SWA_SCALE = SWA_HEAD_DIM ** -0.5
GLA_SCALE = GLA_DK ** -0.5
ADAM_LR = 0.001
ADAM_B1 = 0.9
ADAM_B2 = 0.999
ADAM_EPS = 1e-08
ADAM_WD = 0.01
ADAM_STEP = 10
VMEM_LIMIT = 56 * 1024 * 1024

_NT = (((1,), (1,)), ((), ()))
_TN = (((0,), (0,)), ((), ()))


def _mm(a, b):
    return jnp.dot(a, b, preferred_element_type=F32)


def _mm_nt(a, b):
    return lax.dot_general(a, b, _NT, preferred_element_type=F32)


def _mm_tn(a, b):
    return lax.dot_general(a, b, _TN, preferred_element_type=F32)


def _sigmoid(t):
    return 1.0 / (1.0 + jnp.exp(-t))


def _cparams(**kw):
    return pltpu.CompilerParams(vmem_limit_bytes=VMEM_LIMIT, **kw)


def _full(shape):
    return pl.BlockSpec(shape, lambda *_: (0,) * len(shape))


def _rows(tile, width):
    return pl.BlockSpec((tile, width), lambda i: (i, 0))


def _rope_angles(positions):
    half = ROT_DIM // 2
    inv_freq = ROPE_THETA ** (-jnp.arange(half, dtype=F32) / half)
    ang = positions.astype(F32)[:, None] * inv_freq[None, :]
    return jnp.concatenate([jnp.cos(ang), jnp.sin(ang)], axis=1)


def _split3_parts(t):
    hi = t.astype(MXU_DTYPE)
    r1 = t - hi.astype(F32)
    mid = r1.astype(MXU_DTYPE)
    return hi, mid, (r1 - mid.astype(F32)).astype(MXU_DTYPE)


def _rope_tables(cs):
    half = ROT_DIM // 2
    i = lax.broadcasted_iota(jnp.int32, (2 * half, 3 * 128), 0)
    lane = lax.broadcasted_iota(jnp.int32, (2 * half, 3 * 128), 1)
    table, pos = _idiv(lane, 128), lane & (SWA_HEAD_DIM - 1)
    is_c = (table == 0) & (pos < ROT_DIM) & ((pos & (half - 1)) == i)
    is_s1 = (table == 1) & (pos < half) & (pos + half == i)
    is_s2 = (table == 2) & (pos >= half) & (pos < ROT_DIM) & (pos == i)
    sel = jnp.where(is_c | is_s2, 1.0, jnp.where(is_s1, -1.0, 0.0)).astype(MXU_DTYPE)
    hi, mid, lo = _split3_parts(cs)
    t = (_mm(hi, sel) + _mm(mid, sel)) + _mm(lo, sel)
    pos1 = lax.broadcasted_iota(jnp.int32, (1, 128), 1) & (SWA_HEAD_DIM - 1)
    return t[:, 0:128] + jnp.where(pos1 >= ROT_DIM, 1.0, 0.0), t[:, 128:256], t[:, 256:384]


def _rope(t, c, s1, s2):
    return t * c + pltpu.roll(t, 120, 1) * s1 + pltpu.roll(t, 8, 1) * s2


def _rope_t(g, c, s1, s2):
    return g * c + pltpu.roll(g * s1, 8, 1) + pltpu.roll(g * s2, 120, 1)


def _in_proj(x, w_in_t, wg_s, b_gate, cos_sin, w_out_s):
    s = x.shape[0]
    ts = min(512, s)
    nsteps = s // ts
    forward_step = min(3, nsteps - 1)
    widths = [OFF[i + 1] - OFF[i] for i in range(9)]

    def body(x_ref, win_hbm, wgs_ref, bg_ref, cs_ref, wos_ref,
             qa_ref, ka_ref, va_ref, ga_ref, qb_ref, kb_ref, vb_ref, gb_ref, rb_ref, la_ref, oms_ref,
             c_ref, s1_ref, s2_ref, x16_ref, w_ref, wg_ref, wout_ref,
             win_all, wg_all, wout_all, stage, stage_sem, *sems):
        xb = x_ref[...].astype(MXU_DTYPE)
        x16_ref[...] = xb
        c, s1, s2 = _rope_tables(cs_ref[...])
        c_ref[...], s1_ref[...], s2_ref[...] = c, s1, s2
        i0 = pl.program_id(0)
        gather = _BlockGather(wout_all, *sems[0:2])

        @pl.when(i0 == 0)
        def _():
            ka_ref[0:BLOCK, :] = jnp.zeros((BLOCK, 128), ka_ref.dtype)
            va_ref[0:BLOCK, :] = jnp.zeros((BLOCK, 128), va_ref.dtype)
            first = (_BlockGather(win_all, *sems[2:4]), _BlockGather(wg_all, *sems[4:6]))
            load = pltpu.make_async_copy(win_hbm.at[:, 0, :], stage, stage_sem)
            load.start()
            wout_all[gather.me] = wos_ref[...].astype(wout_all.dtype)
            wg_all[gather.me] = wgs_ref[...].astype(wg_all.dtype)
            load.wait()
            win_all[gather.me] = stage[...].astype(win_all.dtype)
            for stage_of in ("start", "forward", "finish"):
                for g in first:
                    getattr(g, stage_of)()
            gather.start()
            for j in range(N_DEV):
                w_ref[D_IN_SHARD * j:D_IN_SHARD * (j + 1), :] = win_all[j]
                wg_ref[:, 32 * j:32 * (j + 1)] = wg_all[j]

        @pl.when(i0 == forward_step)
        def _():
            gather.forward()

        @pl.when(i0 == nsteps - 1)
        def _():
            gather.finish()
            for j in range(N_DEV):
                wout_ref[D_OUT_SHARD * j:D_OUT_SHARD * (j + 1), :] = wout_all[j]

        kv_rows = pl.ds(pl.multiple_of(BLOCK + i0 * ts, BLOCK), ts)

        def cols(i):
            return _mm_nt(xb, w_ref[OFF[i]:OFF[i + 1], :])

        qa = cols(0)
        for i in range(4):
            qa_ref[:, 128 * i:128 * (i + 1)] = _rope(qa[:, 128 * i:128 * (i + 1)], c, s1, s2).astype(qa_ref.dtype)
        kv = _mm_nt(xb, w_ref[OFF[1]:OFF[3], :])
        ka_ref[kv_rows, :] = _rope(kv[:, 0:128], c, s1, s2).astype(ka_ref.dtype)
        va_ref[kv_rows, :] = kv[:, 128:256].astype(va_ref.dtype)
        ga_ref[...] = cols(3)
        qb_ref[...] = cols(4)
        kb_ref[...] = cols(5)
        vb_ref[...] = cols(6).astype(vb_ref.dtype)
        gb_ref[...] = cols(7)
        rb = cols(8)
        rb_ref[...] = rb
        logit = _mm(rb.astype(MXU_DTYPE), wg_ref[...]) + bg_ref[...]
        e = jnp.exp(-jnp.abs(logit))
        la_ref[...] = (jnp.minimum(logit, 0.0) - jnp.log(1.0 + e)) / GLA_TAU
        oms_ref[...] = jnp.where(logit >= 0.0, e, 1.0) / (1.0 + e)

    out_shape = [jax.ShapeDtypeStruct((s + BLOCK if i in (1, 2) else s, w), MXU_DTYPE if i in (0, 1, 2, 6) else F32)
                 for i, w in enumerate(widths)]
    out_shape += [jax.ShapeDtypeStruct((s, 256), F32)] * 2 + [jax.ShapeDtypeStruct((s, 128), F32)] * 3
    out_shape += [jax.ShapeDtypeStruct((s, D_MODEL), MXU_DTYPE)]
    out_shape += [jax.ShapeDtypeStruct((D_IN_PROJ, D_MODEL), MXU_DTYPE), jax.ShapeDtypeStruct((GLA_RANK, 256), MXU_DTYPE),
                  jax.ShapeDtypeStruct((D_MODEL, D_MODEL), MXU_DTYPE)]
    return pl.pallas_call(
        body, name="in_proj", grid=(nsteps,),
        in_specs=[_rows(ts, D_MODEL), pl.BlockSpec(memory_space=pl.ANY), _full((GLA_RANK, 32)), _full((1, 256)),
                  _rows(ts, ROT_DIM), _full((D_OUT_SHARD, D_MODEL))],
        out_specs=[_full((s + BLOCK, w)) if i in (1, 2) else _rows(ts, w) for i, w in enumerate(widths)]
        + [_rows(ts, 256)] * 2 + [_rows(ts, 128)] * 3 + [_rows(ts, D_MODEL)]
        + [_full((D_IN_PROJ, D_MODEL)), _full((GLA_RANK, 256)), _full((D_MODEL, D_MODEL))],
        out_shape=out_shape,
        scratch_shapes=[pltpu.VMEM((N_DEV, D_IN_SHARD, D_MODEL), MXU_DTYPE), pltpu.VMEM((N_DEV, GLA_RANK, 32), MXU_DTYPE),
                        pltpu.VMEM((N_DEV, D_OUT_SHARD, D_MODEL), MXU_DTYPE),
                        pltpu.VMEM((D_IN_SHARD, D_MODEL), F32), pltpu.SemaphoreType.DMA]
        + 3 * _BlockGather.scratch(),
        compiler_params=_cparams(dimension_semantics=("arbitrary",)),
    )(x, w_in_t, wg_s, b_gate, cos_sin, w_out_s)


SWA_ROWS = SWA_GROUP * BLOCK


def _swa_bias():
    shape = (2, 2 * BLOCK, SWA_ROWS)
    ki = lax.broadcasted_iota(jnp.int32, shape, 1)
    qi = lax.broadcasted_iota(jnp.int32, shape, 2) & (BLOCK - 1)
    first = lax.broadcasted_iota(jnp.int32, shape, 0) == 0
    dist = qi + BLOCK - ki
    ok = (dist >= 0) & (dist < BLOCK) & (jnp.logical_not(first) | (ki >= BLOCK))
    return jnp.where(ok, 0.0, -jnp.inf).astype(F32)


SWA_SUB = 8


def _swa_bias_of(bias_ref, n, b):
    return bias_ref[jnp.minimum(n, 1)] if b == 0 else bias_ref[1]


def _swa_dup(t, j):
    t = t.astype(F32)
    low = lax.broadcasted_iota(jnp.int32, t.shape, 1) < SWA_HEAD_DIM
    keep = low if j == 0 else jnp.logical_not(low)
    return jnp.where(keep, t, pltpu.roll(t, SWA_HEAD_DIM, 1)).astype(MXU_DTYPE)


def _swa_stack(t, j):
    low = lax.broadcasted_iota(jnp.int32, (BLOCK, 128), 1) < SWA_HEAD_DIM
    zero = jnp.zeros((BLOCK, 128), t.dtype)
    blocks = []
    for p in (2 * j, 2 * j + 1):
        tp = t[:, 128 * p:128 * (p + 1)]
        blocks += [jnp.where(low, tp, zero), jnp.where(low, zero, tp)]
    return jnp.concatenate(blocks, axis=0)


def _swa_unstack(t):
    low = lax.broadcasted_iota(jnp.int32, (BLOCK, 128), 1) < SWA_HEAD_DIM
    return [jnp.where(low, t[2 * BLOCK * i:2 * BLOCK * i + BLOCK], t[2 * BLOCK * i + BLOCK:2 * BLOCK * (i + 1)])
            for i in range(2)]


def _swa_sink_row(sink_ref, j):
    lane = lax.broadcasted_iota(jnp.int32, (1, SWA_ROWS), 1)
    row = jnp.full((1, SWA_ROWS), sink_ref[SWA_GROUP * j], F32)
    for r in range(1, SWA_GROUP):
        row = jnp.where(lane >= BLOCK * r, sink_ref[SWA_GROUP * j + r], row)
    return row


def _split3(t):
    return jnp.concatenate(_split3_parts(t), axis=1)


def _row_sums_as_row(t):
    ones = jnp.ones((8, 3 * t.shape[1]), MXU_DTYPE)
    return _mm_nt(ones, _split3(t))[0:1, :]


def _swa_probs_t(qs, kd, bias_t, sink):
    sc = _mm_nt(kd, qs) + bias_t
    m = jnp.maximum(jnp.max(sc, axis=0, keepdims=True), sink)
    p = jnp.exp(sc - m)
    ps = jnp.exp(sink - m)
    rinv = 1.0 / (jnp.sum(p, axis=0, keepdims=True) + ps)
    return p * rinv, ps * rinv


def _swa_fwd(sinks, qa, k_pad, v_pad, ga):
    s = qa.shape[0]
    sub = min(SWA_SUB, s // BLOCK)
    tq = sub * BLOCK

    def body(sink_ref, qa_ref, ga_ref, bias_ref, k_ref, v_ref, attn_ref, cat_ref):
        n = pl.program_id(0)
        for b in range(sub):
            rows = slice(BLOCK * b, BLOCK * (b + 1))
            start = pl.multiple_of((n * sub + b) * BLOCK, BLOCK)
            kw = k_ref[pl.ds(start, 2 * BLOCK), :]
            vw = v_ref[pl.ds(start, 2 * BLOCK), :]
            bias_t = _swa_bias_of(bias_ref, n, b)
            q = qa_ref[rows, :] * SWA_SCALE
            g = ga_ref[rows, :]
            silu = g * _sigmoid(g)
            for j in range(SWA_KV_HEADS):
                qs = _swa_stack(q, j).astype(MXU_DTYPE)
                probs, _ = _swa_probs_t(qs, _swa_dup(kw, j), bias_t, _swa_sink_row(sink_ref, j))
                pairs = _swa_unstack(_mm_tn(probs.astype(MXU_DTYPE), _swa_dup(vw, j)))
                for i in range(2):
                    lanes = slice(128 * (2 * j + i), 128 * (2 * j + i + 1))
                    attn_ref[rows, lanes] = pairs[i]
                    cat_ref[rows, lanes] = (pairs[i] * silu[:, lanes]).astype(cat_ref.dtype)

    return pl.pallas_call(
        body, name="swa_fwd", grid=(s // tq,),
        in_specs=[pl.BlockSpec(memory_space=pltpu.SMEM), _rows(tq, 512), _rows(tq, 512),
                  _full((2, 2 * BLOCK, SWA_ROWS)), _full((s + BLOCK, 128)), _full((s + BLOCK, 128))],
        out_specs=[_rows(tq, 512), _rows(tq, 512)],
        out_shape=[jax.ShapeDtypeStruct((s, 512), F32), jax.ShapeDtypeStruct((s, 512), MXU_DTYPE)],
        compiler_params=_cparams(dimension_semantics=("arbitrary",)),
    )(sinks, qa, ga, _swa_bias(), k_pad, v_pad)


GLA_KW = GLA_HEADS * GLA_DK
GLA_VW = GLA_HEADS * GLA_DV


def _idiv(t, d):
    return t >> (d.bit_length() - 1)


def _chunk_cumsum(t, lower):
    n, w = t.shape
    r = lax.broadcasted_iota(jnp.int32, (n, n), 0)
    c = lax.broadcasted_iota(jnp.int32, (n, n), 1)
    tri = ((_idiv(r, GLA_CHUNK) == _idiv(c, GLA_CHUNK)) & ((r >= c) if lower else (r <= c))).astype(MXU_DTYPE)
    parts = _mm(tri, _split3(t))
    return (parts[:, :w] + parts[:, w:2 * w]) + parts[:, 2 * w:]


def _chunk_last(t):
    n = t.shape[0]
    return jnp.concatenate(
        [jnp.broadcast_to(t[c + GLA_CHUNK - 1:c + GLA_CHUNK, :], (GLA_CHUNK, t.shape[1]))
         for c in range(0, n, GLA_CHUNK)], axis=0)


def _head_stack(t, width):
    head = _idiv(lax.broadcasted_iota(jnp.int32, t.shape, 1), width)
    zero = jnp.zeros_like(t)
    return jnp.concatenate([jnp.where(head == h, t, zero) for h in range(GLA_HEADS)], axis=0)


def _heads_to_rows(t):
    return jnp.concatenate([t[:, GLA_DV * h:GLA_DV * (h + 1)] for h in range(GLA_HEADS)], axis=0)


def _rows_to_heads(t):
    return jnp.concatenate([t[GLA_CHUNK * h:GLA_CHUNK * (h + 1)] for h in range(GLA_HEADS)], axis=1)


def _state_by_head(t):
    srow = _idiv(lax.broadcasted_iota(jnp.int32, (GLA_VW, GLA_KW), 0), GLA_DV)
    slane = _idiv(lax.broadcasted_iota(jnp.int32, (GLA_VW, GLA_KW), 1), GLA_DK)
    return jnp.where(srow == slane, jnp.concatenate([t] * GLA_HEADS, axis=0), jnp.zeros((GLA_VW, GLA_KW), t.dtype))


def _gla_masks():
    row = lax.broadcasted_iota(jnp.int32, (GLA_CHUNK, GLA_KW), 0)
    pos = lax.broadcasted_iota(jnp.int32, (GLA_CHUNK, GLA_KW), 1) & (GLA_CHUNK - 1)
    return pos <= row, pos >= row


def _gla_fwd(qb, kb, vb, la, gb, norm_w):
    s = qb.shape[0]
    tb = min(256, s)
    ch = tb // GLA_CHUNK

    def body(qb_ref, kb_ref, vb_ref, la_ref, gb_ref, nw_ref, o_ref, cat_ref, sp_ref, st_ref):
        @pl.when(pl.program_id(0) == 0)
        def _():
            st_ref[...] = jnp.zeros_like(st_ref)

        causal, _ = _gla_masks()
        nw = nw_ref[...]
        b = _chunk_cumsum(la_ref[...], True)
        bl = _chunk_last(b)
        k = kb_ref[...]
        qd = ((qb_ref[...] * GLA_SCALE) * jnp.exp(b)).astype(MXU_DTYPE)
        ki = (k * jnp.exp(-b)).astype(MXU_DTYPE)
        ke = (k * jnp.exp(bl - b)).astype(MXU_DTYPE)
        dec = jnp.exp(bl)
        v = vb_ref[...].astype(MXU_DTYPE)
        g = gb_ref[...]
        silu = g * _sigmoid(g)
        for ci in range(ch):
            rows = slice(GLA_CHUNK * ci, GLA_CHUNK * (ci + 1))
            qds, kis, kes = (_head_stack(t[rows], GLA_DK) for t in (qd, ki, ke))
            a = jnp.where(causal, _mm_nt(qd[rows], kis), 0.0).astype(MXU_DTYPE)
            st = st_ref[...]
            sp_ref[ci] = st
            o = _mm(a, _head_stack(v[rows], GLA_DV)) + _rows_to_heads(_mm_nt(qds, st.astype(MXU_DTYPE)))
            st_ref[...] = st * dec[rows][0:1] + _mm_tn(_heads_to_rows(v[rows]), kes)
            o_ref[rows, :] = o
            for h in range(GLA_HEADS):
                lv = slice(GLA_DV * h, GLA_DV * (h + 1))
                oh = o[:, lv]
                r = lax.rsqrt(jnp.mean(oh * oh, axis=1, keepdims=True) + EPS)
                cat_ref[rows, lv] = (oh * r * nw * silu[rows, lv]).astype(cat_ref.dtype)

    return pl.pallas_call(
        body, name="gla_fwd", grid=(s // tb,),
        in_specs=[_rows(tb, 256), _rows(tb, 256), _rows(tb, 512), _rows(tb, 256), _rows(tb, 512), _full((1, 128))],
        out_specs=[_rows(tb, 512), _rows(tb, 512), pl.BlockSpec((ch, GLA_DV, 256), lambda i: (i, 0, 0))],
        out_shape=[jax.ShapeDtypeStruct((s, 512), F32), jax.ShapeDtypeStruct((s, 512), MXU_DTYPE),
                   jax.ShapeDtypeStruct((s // GLA_CHUNK, GLA_DV, 256), F32)],
        scratch_shapes=[pltpu.VMEM((GLA_DV, GLA_KW), F32)],
        compiler_params=_cparams(dimension_semantics=("arbitrary",)),
    )(qb, kb, vb, la, gb, norm_w)


def _out_ln_loss(cat_a, cat_b, w_out, x, target, ln_g, ln_b):
    s = x.shape[0]
    ts = min(512, s)
    halves = 2 if ts % 32 == 0 else 1
    th = ts // halves

    def body(ca_ref, cb_ref, w_ref, x_ref, t_ref, g_ref, b_ref,
             loss_ref, gx_ref, da_ref, db_ref, gw_ref, gln_ref):
        @pl.when(pl.program_id(0) == 0)
        def _():
            loss_ref[...] = jnp.zeros_like(loss_ref)
            gw_ref[...] = jnp.zeros_like(gw_ref)
            gln_ref[...] = jnp.zeros_like(gln_ref)

        g = g_ref[...]
        dh16s = []
        for k in range(halves):
            rows = slice(th * k, th * (k + 1))
            mix = _mm(ca_ref[rows, :], w_ref[0:512, :]) + _mm(cb_ref[rows, :], w_ref[512:1024, :])
            h = ALPHA * x_ref[rows, :] + mix
            mu = jnp.mean(h, axis=1, keepdims=True)
            hc = h - mu
            rstd = lax.rsqrt(jnp.mean(hc * hc, axis=1, keepdims=True) + EPS)
            xhat = hc * rstd
            err = xhat * g + b_ref[...] - t_ref[rows, :]
            loss_ref[...] += 0.5 * jnp.sum(jnp.mean(err * err, axis=1, keepdims=True))
            dy = err * (1.0 / D_MODEL)
            gln_ref[0:1, :] += jnp.sum(dy * xhat, axis=0, keepdims=True)
            gln_ref[1:2, :] += jnp.sum(dy, axis=0, keepdims=True)
            dxh = dy * g
            dh = rstd * (dxh - jnp.mean(dxh, axis=1, keepdims=True)
                         - xhat * jnp.mean(dxh * xhat, axis=1, keepdims=True))
            gx_ref[rows, :] = ALPHA * dh
            dh16s.append(dh.astype(MXU_DTYPE))
        for k in range(halves):
            rows = slice(th * k, th * (k + 1))
            da_ref[rows, :] = _mm_nt(dh16s[k], w_ref[0:512, :])
            db_ref[rows, :] = _mm_nt(dh16s[k], w_ref[512:1024, :])
        dh16 = jnp.concatenate(dh16s, axis=0)
        gw_ref[0:512, :] += _mm_tn(ca_ref[...], dh16)
        gw_ref[512:1024, :] += _mm_tn(cb_ref[...], dh16)

    return pl.pallas_call(
        body, name="out_ln_loss", grid=(s // ts,),
        in_specs=[_rows(ts, 512), _rows(ts, 512), _full((D_MODEL, D_MODEL)), _rows(ts, D_MODEL), _rows(ts, D_MODEL),
                  _full((1, D_MODEL)), _full((1, D_MODEL))],
        out_specs=[_full((1, 128)), _rows(ts, D_MODEL), _rows(ts, 512), _rows(ts, 512),
                   _full((D_MODEL, D_MODEL)), _full((2, D_MODEL))],
        out_shape=[jax.ShapeDtypeStruct((1, 128), F32), jax.ShapeDtypeStruct((s, D_MODEL), F32),
                   jax.ShapeDtypeStruct((s, 512), F32), jax.ShapeDtypeStruct((s, 512), F32),
                   jax.ShapeDtypeStruct((D_MODEL, D_MODEL), F32), jax.ShapeDtypeStruct((2, D_MODEL), F32)],
        compiler_params=_cparams(dimension_semantics=("arbitrary",)),
    )(cat_a, cat_b, w_out, x, target, ln_g, ln_b)


def _swa_bwd(sinks, qa, k_pad, v_pad, attn, ga, d_cat_a, rope, parts_w_out):
    s = qa.shape[0]
    sub = min(SWA_SUB, s // BLOCK)
    tq = sub * BLOCK
    nsteps = s // tq
    forward_step = min(1, nsteps - 1)

    def body(sink_ref, qa_ref, ga_ref, at_ref, dc_ref, c_ref, s1_ref, s2_ref, bias_ref, k_ref, v_ref, pout_ref,
             dq_ref, dg_ref, dk_out, dv_out, ds_ref, gout_ref, dk_ref, dv_ref, *scratch):
        n = pl.program_id(0)
        owner_sum = _OwnerSum(pout_ref, *scratch)

        @pl.when(n == 0)
        def _():
            dk_ref[...] = jnp.zeros_like(dk_ref)
            dv_ref[...] = jnp.zeros_like(dv_ref)
            ds_ref[...] = jnp.zeros_like(ds_ref)
            owner_sum.start()

        @pl.when(n == forward_step)
        def _():
            owner_sum.forward()

        @pl.when(n == nsteps - 1)
        def _():
            gout_ref[...] = owner_sum.finish()

        low = lax.broadcasted_iota(jnp.int32, (2 * BLOCK, 128), 1) < SWA_HEAD_DIM
        for b in range(sub):
            rows = slice(BLOCK * b, BLOCK * (b + 1))
            start = pl.multiple_of((n * sub + b) * BLOCK, BLOCK)
            kw = k_ref[pl.ds(start, 2 * BLOCK), :]
            vw = v_ref[pl.ds(start, 2 * BLOCK), :]
            bias_t = _swa_bias_of(bias_ref, n, b)
            q = qa_ref[rows, :] * SWA_SCALE
            g = ga_ref[rows, :]
            sg = _sigmoid(g)
            o = at_ref[rows, :]
            dc = dc_ref[rows, :]
            do = dc * (g * sg)
            dg_ref[rows, :] = (dc * o * (sg * (1.0 + g * (1.0 - sg)))).astype(dg_ref.dtype)
            od = do * o
            c, s1, s2 = c_ref[rows, :], s1_ref[rows, :], s2_ref[rows, :]
            dk, dv = [], []
            for j in range(SWA_KV_HEADS):
                kd, vd = _swa_dup(kw, j), _swa_dup(vw, j)
                qs = _swa_stack(q, j).astype(MXU_DTYPE)
                dos = _swa_stack(do, j).astype(MXU_DTYPE)
                probs, psink = _swa_probs_t(qs, kd, bias_t, _swa_sink_row(sink_ref, j))
                delta = _row_sums_as_row(_swa_stack(od, j))
                dsc = (probs * (_mm_nt(vd, dos) - delta)).astype(MXU_DTYPE)
                dsink = psink * delta
                for r in range(SWA_GROUP):
                    h = SWA_GROUP * j + r
                    ds_ref[h:h + 1, :] += jnp.zeros((1, 128), F32) - jnp.sum(dsink[:, BLOCK * r:BLOCK * (r + 1)])
                dq = _swa_unstack(_mm_tn(dsc, kd))
                for i in range(2):
                    lanes = slice(128 * (2 * j + i), 128 * (2 * j + i + 1))
                    dq_ref[rows, lanes] = _rope_t(dq[i] * SWA_SCALE, c, s1, s2).astype(dq_ref.dtype)
                dkj = _mm(dsc, qs)
                dvj = _mm(probs.astype(MXU_DTYPE), dos)
                dk.append(dkj + pltpu.roll(dkj, SWA_HEAD_DIM, 1))
                dv.append(dvj + pltpu.roll(dvj, SWA_HEAD_DIM, 1))
            dk_ref[pl.ds(start, 2 * BLOCK), :] += jnp.where(low, dk[0], dk[1])
            dv_ref[pl.ds(start, 2 * BLOCK), :] += jnp.where(low, dv[0], dv[1])

        @pl.when(n == nsteps - 1)
        def _():
            dk_out[...] = dk_ref[BLOCK:, :]
            dv_out[...] = dv_ref[BLOCK:, :]

    out_blk = parts_w_out.shape[1:]
    return pl.pallas_call(
        body, name="swa_bwd", grid=(nsteps,),
        in_specs=[pl.BlockSpec(memory_space=pltpu.SMEM)] + [_rows(tq, 512)] * 4 + [_rows(tq, 128)] * 3
        + [_full((2, 2 * BLOCK, SWA_ROWS))] + [_full((s + BLOCK, 128))] * 2 + [pl.BlockSpec(memory_space=pl.ANY)],
        out_specs=[_rows(tq, 512), _rows(tq, 512), _full((s, 128)), _full((s, 128)),
                   _full((SWA_Q_HEADS, 128)), _full(out_blk)],
        out_shape=[jax.ShapeDtypeStruct((s, 512), MXU_DTYPE), jax.ShapeDtypeStruct((s, 512), MXU_DTYPE),
                   jax.ShapeDtypeStruct((s, 128), F32), jax.ShapeDtypeStruct((s, 128), F32),
                   jax.ShapeDtypeStruct((SWA_Q_HEADS, 128), F32), jax.ShapeDtypeStruct(out_blk, F32)],
        scratch_shapes=[pltpu.VMEM((s + BLOCK, 128), F32)] * 2 + _OwnerSum.scratch(out_blk),
        compiler_params=_cparams(dimension_semantics=("arbitrary",)),
    )(sinks, qa, ga, attn, d_cat_a, *rope, _swa_bias(), k_pad, v_pad, parts_w_out)


def _gla_bwd(qb, kb, vb, la, oms, gb, o, sprev, d_cat_b, rb, wg, norm_w):
    s = qb.shape[0]
    tb = min(512, s)
    ch = tb // GLA_CHUNK
    nb = s // tb

    def body(qb_ref, kb_ref, vb_ref, la_ref, oms_ref, gb_ref, o_ref, sp_ref, dc_ref, rb_ref, wg_ref, nw_ref,
             dq_ref, dk_ref, dv_ref, dg_ref, dr_ref, gwg_ref, gbg_ref, gnw_ref, dst_ref):
        @pl.when(pl.program_id(0) == 0)
        def _():
            dst_ref[...] = jnp.zeros_like(dst_ref)
            gwg_ref[...] = jnp.zeros_like(gwg_ref)
            gbg_ref[...] = jnp.zeros_like(gbg_ref)
            gnw_ref[...] = jnp.zeros_like(gnw_ref)

        causal, causal_t = _gla_masks()
        nw = nw_ref[...]
        b = _chunk_cumsum(la_ref[...], True)
        bl = _chunk_last(b)
        eb, enb, ee, dec = jnp.exp(b), jnp.exp(-b), jnp.exp(bl - b), jnp.exp(bl)
        k = kb_ref[...]
        qd = (qb_ref[...] * GLA_SCALE) * eb
        ki = k * enb
        ke = k * ee
        qd16, ki16, ke16 = qd.astype(MXU_DTYPE), ki.astype(MXU_DTYPE), ke.astype(MXU_DTYPE)
        v16 = vb_ref[...].astype(MXU_DTYPE)

        g = gb_ref[...]
        sg = _sigmoid(g)
        silu = g * sg
        dsilu = sg * (1.0 + g * (1.0 - sg))
        gnw = jnp.zeros((1, GLA_DV), F32)
        do = []
        for h in range(GLA_HEADS):
            lv = slice(GLA_DV * h, GLA_DV * (h + 1))
            oh = o_ref[:, lv]
            dch = dc_ref[:, lv]
            r = lax.rsqrt(jnp.mean(oh * oh, axis=1, keepdims=True) + EPS)
            d_on = dch * silu[:, lv]
            dg_ref[:, lv] = (dch * (oh * r * nw) * dsilu[:, lv]).astype(dg_ref.dtype)
            gnw += jnp.sum(d_on * oh * r, axis=0, keepdims=True)
            u = d_on * nw
            do.append(r * u - oh * (r * r * r) * jnp.mean(u * oh, axis=1, keepdims=True))
        gnw_ref[...] += gnw
        do16 = jnp.concatenate(do, axis=1).astype(MXU_DTYPE)

        db, dbl = [None] * ch, [None] * ch
        for ci in reversed(range(ch)):
            rows = slice(GLA_CHUNK * ci, GLA_CHUNK * (ci + 1))
            qds, kis, kes = (_head_stack(t[rows], GLA_DK) for t in (qd16, ki16, ke16))
            vs, dos = _head_stack(v16[rows], GLA_DV), _head_stack(do16[rows], GLA_DV)
            a = jnp.where(causal, _mm_nt(qd16[rows], kis), 0.0).astype(MXU_DTYPE)
            at = jnp.where(causal_t, _mm_nt(ki16[rows], qds), 0.0).astype(MXU_DTYPE)
            da = jnp.where(causal, _mm_nt(do16[rows], vs), 0.0).astype(MXU_DTYPE)
            dat = jnp.where(causal_t, _mm_nt(v16[rows], dos), 0.0).astype(MXU_DTYPE)
            st = sp_ref[ci]
            dst = dst_ref[...]
            dst16 = dst.astype(MXU_DTYPE)
            dv = _mm(at, dos) + _rows_to_heads(_mm_nt(kes, dst16))
            dqd = _mm(da, kis) + _mm(do16[rows], _state_by_head(st.astype(MXU_DTYPE)))
            dki = _mm(dat, qds)
            dke = _mm(v16[rows], _state_by_head(dst16))
            ddec = jnp.sum(dst * st, axis=0, keepdims=True)
            decc = dec[rows][0:1]
            dst_ref[...] = _mm_tn(_heads_to_rows(do16[rows]), qds) + dst * decc
            dq_ref[rows, :] = (dqd * eb[rows] * GLA_SCALE).astype(dq_ref.dtype)
            dk_ref[rows, :] = (dki * enb[rows] + dke * ee[rows]).astype(dk_ref.dtype)
            dv_ref[rows, :] = dv.astype(dv_ref.dtype)
            dke_ke = dke * ke[rows]
            db[ci] = dqd * qd[rows] - dki * ki[rows] - dke_ke
            dbl[ci] = jnp.broadcast_to(jnp.sum(dke_ke, axis=0, keepdims=True) + ddec * decc, (GLA_CHUNK, GLA_KW))

        dla = _chunk_cumsum(jnp.concatenate(db, axis=0), False) + jnp.concatenate(dbl, axis=0)
        dlogit = dla * oms_ref[...] * (1.0 / GLA_TAU)
        dl16 = dlogit.astype(MXU_DTYPE)
        gbg_ref[...] += jnp.sum(dlogit, axis=0, keepdims=True)
        gwg_ref[...] += _mm_tn(rb_ref[...].astype(MXU_DTYPE), dl16)
        dr_ref[...] = _mm_nt(dl16, wg_ref[...]).astype(dr_ref.dtype)

    def rev(width):
        return pl.BlockSpec((tb, width), lambda i: (nb - 1 - i, 0))

    return pl.pallas_call(
        body, name="gla_bwd", grid=(nb,),
        in_specs=[rev(256), rev(256), rev(512), rev(256), rev(256), rev(512), rev(512),
                  pl.BlockSpec((ch, GLA_DV, 256), lambda i: (nb - 1 - i, 0, 0)), rev(512), rev(GLA_RANK),
                  _full((GLA_RANK, 256)), _full((1, 128))],
        out_specs=[rev(256), rev(256), rev(512), rev(512), rev(GLA_RANK),
                   _full((GLA_RANK, 256)), _full((1, 256)), _full((1, 128))],
        out_shape=[jax.ShapeDtypeStruct((s, 256), MXU_DTYPE), jax.ShapeDtypeStruct((s, 256), MXU_DTYPE),
                   jax.ShapeDtypeStruct((s, 512), MXU_DTYPE), jax.ShapeDtypeStruct((s, 512), MXU_DTYPE),
                   jax.ShapeDtypeStruct((s, GLA_RANK), MXU_DTYPE), jax.ShapeDtypeStruct((GLA_RANK, 256), F32),
                   jax.ShapeDtypeStruct((1, 256), F32), jax.ShapeDtypeStruct((1, 128), F32)],
        scratch_shapes=[pltpu.VMEM((GLA_DV, GLA_KW), F32)],
        compiler_params=_cparams(dimension_semantics=("arbitrary",)),
    )(qb, kb, vb, la, oms, gb, o, sprev, d_cat_b, rb, wg, norm_w)


def _dproj_tiles(piece_refs, rope_refs, members=(0, 1, 3, 4, 5, 6, 7, 8)):
    for i in members:
        if i == 1:
            dk = _rope_t(piece_refs[1][...], *(r[...] for r in rope_refs))
            yield OFF[1], OFF[3], jnp.concatenate([dk, piece_refs[2][...]], axis=1).astype(MXU_DTYPE)
        else:
            yield OFF[i], OFF[i + 1], piece_refs[i][...].astype(MXU_DTYPE)


def _in_proj_bwd_x(gx0, pieces, w_in, rope):
    s = gx0.shape[0]
    ts = min(512, s)
    widths = [OFF[i + 1] - OFF[i] for i in range(9)]

    def body(gx0_ref, *refs):
        w_ref, gx_ref = refs[12:]
        acc = gx0_ref[...]
        for lo, hi, t16 in _dproj_tiles(refs[:9], refs[9:12]):
            acc += _mm(t16, w_ref[lo:hi, :])
        gx_ref[...] = acc

    return pl.pallas_call(
        body, name="in_proj_bwd_x", grid=(s // ts,),
        in_specs=[_rows(ts, D_MODEL)] + [_rows(ts, w) for w in widths] + [_rows(ts, 128)] * 3
        + [_full((D_IN_PROJ, D_MODEL))],
        out_specs=_rows(ts, D_MODEL),
        out_shape=jax.ShapeDtypeStruct((s, D_MODEL), F32),
        compiler_params=_cparams(dimension_semantics=("arbitrary",)),
    )(gx0, *pieces, *rope, w_in)


GW_GROUPS = ((0, 1), (3, 4), (5, 6), (7, 8))


def _in_proj_bwd_w(x, pieces, rope, parts_wg, g_ln, g_bg, g_nw, g_sinks, loss):
    s = x.shape[0]
    ts = min(1024, s)
    nt = s // ts
    n_groups = len(GW_GROUPS)
    ends = [OFF[3] if members[-1] == 1 else OFF[members[-1] + 1] for members in GW_GROUPS]
    assert all(ends[g] >= D_IN_SHARD * 2 * (g + 1) for g in range(n_groups))
    blk = (D_IN_SHARD, D_MODEL)

    def body(x_hbm, *refs):
        piece_refs, rope_refs = refs[:9], refs[9:12]
        pwg_ref, gln_ref, gbg_ref, gnw_ref, gsk_ref, loss_ref, gin_ref, rwg_ref, rsm_ref = refs[12:21]
        (acc_ref, stage_ref, sib_ref, snd_ref, rcv_ref, mine_ref, sm_ref, xs_ref,
         d2d_send, d2d_recv, ici_send, ici_recv, out_sem, sm_send, sm_recv, sm_loc, x_sems) = refs[21:]
        g, t = pl.program_id(0), pl.program_id(1)
        tile = pl.ds(pl.multiple_of(t * ts, ts), ts)

        def x_load(k):
            rows = pl.ds(pl.multiple_of(k * ts, ts), ts)
            return pltpu.make_async_copy(x_hbm.at[rows, :], xs_ref.at[rows, :], x_sems.at[k])
        x_, y_, c = _mesh_pos()
        me, mychip, sibling = 4 * x_ + 2 * y_ + c, 2 * x_ + y_, (x_, y_, 1 - c)
        small_dsts = (rwg_ref, rsm_ref)

        def small_src(a, block):
            return pwg_ref.at[block] if a == 0 else sm_ref

        def small_copy(k, a, src_block, dst_block, peer):
            i = 2 * (k - 1) + a
            return pltpu.make_async_remote_copy(
                src_ref=small_src(a, src_block), dst_ref=small_dsts[a].at[dst_block], send_sem=sm_send.at[i],
                recv_sem=sm_recv.at[i], device_id=peer, device_id_type=pl.DeviceIdType.MESH)

        def small_local(a):
            return pltpu.make_async_copy(small_src(a, me), small_dsts[a].at[me], sm_loc.at[a])

        @pl.when((g == 0) & (t == 0))
        def _():
            for k in range(nt):
                x_load(k).start()
            acc_ref[...] = jnp.zeros_like(acc_ref)
            sm_ref[...] = jnp.zeros_like(sm_ref)
            for r in range(D_MODEL // 128):
                sm_ref[r:r + 1, :] = gln_ref[0:1, 128 * r:128 * (r + 1)]
                sm_ref[8 + r:9 + r, :] = gln_ref[1:2, 128 * r:128 * (r + 1)]
            for r in range(2):
                sm_ref[16 + r:17 + r, :] = gbg_ref[0:1, 128 * r:128 * (r + 1)]
            sm_ref[24:25, :] = gnw_ref[...]
            diag = (lax.broadcasted_iota(jnp.int32, gsk_ref.shape, 0)
                    == lax.broadcasted_iota(jnp.int32, gsk_ref.shape, 1))
            sm_ref[32:33, :] = jnp.sum(jnp.where(diag, gsk_ref[...], 0.0), axis=0, keepdims=True)
            sm_ref[40:41, :] = loss_ref[...]
            for a in range(2):
                small_local(a).start()
            for k in range(1, N_DEV):
                peer, pidx = _peer(k, x_, y_, c)
                for a in range(2):
                    small_copy(k, a, pidx, me, peer).start()

        @pl.when(g == 0)
        def _():
            x_load(t).wait()

        xb = xs_ref[tile, :]
        for gi, members in enumerate(GW_GROUPS):
            @pl.when(g == gi)
            def _(members=members):
                for lo, hi, t16 in _dproj_tiles(piece_refs, rope_refs, members):
                    acc_ref[lo:hi, :] += _mm_tn(t16, xb)

        def block_rows(j):
            return acc_ref[D_IN_SHARD * j:D_IN_SHARD * (j + 1), :]

        def d2d(gi):
            return pltpu.make_async_remote_copy(
                src_ref=stage_ref.at[gi % 2], dst_ref=sib_ref.at[gi], send_sem=d2d_send.at[gi],
                recv_sem=d2d_recv.at[gi], device_id=sibling, device_id_type=pl.DeviceIdType.MESH)

        def ici(slot, owner):
            return pltpu.make_async_remote_copy(
                src_ref=snd_ref.at[slot], dst_ref=rcv_ref.at[slot], send_sem=ici_send.at[slot],
                recv_sem=ici_recv.at[slot], device_id=owner, device_id_type=pl.DeviceIdType.MESH)

        def to_sibling(gi):
            if gi >= 2:
                d2d(gi - 2).wait_send()
            for cc in range(2):
                @pl.when(c == cc)
                def _(cc=cc):
                    stage_ref[gi % 2] = block_rows(2 * gi + 1 - cc)
            d2d(gi).start()

        def chip_sum(gi):
            d2d(gi).wait_recv()
            gx, gy = gi // 2, gi % 2
            for cc in range(2):
                @pl.when(c == cc)
                def _(cc=cc):
                    total = block_rows(2 * gi + cc) + sib_ref[gi]

                    @pl.when(mychip == gi)
                    def _():
                        mine_ref[...] = total

                    @pl.when(mychip != gi)
                    def _():
                        slot = jnp.where(x_ == gx, 0, 1) + 2 * jnp.where(y_ == gy, 0, 1) - 1
                        snd_ref[slot] = total.astype(snd_ref.dtype)
                        ici(slot, (gx, gy, c)).start()

        for gi in range(n_groups):
            @pl.when((g == gi) & (t == nt - 1))
            def _(gi=gi):
                to_sibling(gi)
                if gi == n_groups - 1:
                    for k in range(n_groups):
                        chip_sum(k)
                    total = mine_ref[...]
                    for slot in range(3):
                        ici(slot, sibling).wait_recv()
                        total = total + rcv_ref[slot].astype(F32)
                    mine_ref[...] = total
                    out = pltpu.make_async_copy(mine_ref, gin_ref, out_sem)
                    out.start()
                    for k in range(1, N_DEV):
                        peer, pidx = _peer(k, x_, y_, c)
                        for a in range(2):
                            small_copy(k, a, me, pidx, peer).wait_recv()
                    for k in range(1, N_DEV):
                        peer, pidx = _peer(k, x_, y_, c)
                        for a in range(2):
                            small_copy(k, a, pidx, me, peer).wait_send()
                    for a in range(2):
                        small_local(a).wait()
                    d2d(gi - 1).wait_send()
                    d2d(gi).wait_send()
                    for slot in range(3):
                        ici(slot, sibling).wait_send()
                    out.wait()

    def piece_spec(i, width):
        gi = next(k for k, members in enumerate(GW_GROUPS) if (i in members or (i == 2 and 1 in members)))
        return pl.BlockSpec((ts, width), lambda g, t: (jnp.where(g == gi, t, jnp.where(g < gi, 0, nt - 1)), 0))

    widths = [OFF[i + 1] - OFF[i] for i in range(9)]
    hbm = pl.BlockSpec(memory_space=pl.ANY)
    vmem = pl.BlockSpec(memory_space=pltpu.VMEM)
    rope_spec = pl.BlockSpec((ts, 128), lambda g, t: (jnp.where(g == 0, t, nt - 1), 0))
    return pl.pallas_call(
        body, name="in_proj_bwd_w", grid=(n_groups, nt),
        in_specs=[hbm] + [piece_spec(i, w) for i, w in enumerate(widths)] + [rope_spec] * 3 + [hbm] + [vmem] * 5,
        out_specs=[hbm, hbm, hbm],
        out_shape=[jax.ShapeDtypeStruct(blk, F32), jax.ShapeDtypeStruct((N_DEV,) + parts_wg.shape[1:], F32),
                   jax.ShapeDtypeStruct((N_DEV, SMALL_ROWS, 128), F32)],
        scratch_shapes=[pltpu.VMEM((D_IN_PROJ, D_MODEL), F32), pltpu.VMEM((2,) + blk, F32),
                        pltpu.VMEM((n_groups,) + blk, F32), pltpu.VMEM((3,) + blk, MXU_DTYPE),
                        pltpu.VMEM((3,) + blk, MXU_DTYPE), pltpu.VMEM(blk, F32), pltpu.VMEM((SMALL_ROWS, 128), F32),
                        pltpu.VMEM((s, D_MODEL), MXU_DTYPE),
                        pltpu.SemaphoreType.DMA((n_groups,)), pltpu.SemaphoreType.DMA((n_groups,)),
                        pltpu.SemaphoreType.DMA((3,)), pltpu.SemaphoreType.DMA((3,)), pltpu.SemaphoreType.DMA,
                        pltpu.SemaphoreType.DMA((2 * (N_DEV - 1),)), pltpu.SemaphoreType.DMA((2 * (N_DEV - 1),)),
                        pltpu.SemaphoreType.DMA((2,)), pltpu.SemaphoreType.DMA((nt,))],
        compiler_params=_cparams(dimension_semantics=("arbitrary", "arbitrary")),
    )(x, *pieces, *rope, parts_wg, g_ln, g_bg, g_nw, g_sinks, loss)


def _local_step(x, positions, w_in_t, wg_s, b_gate, sinks, norm_w, w_out_s, ln_g, ln_b, target):
    qa, k_pad, v_pad, ga, qb, kb, vb, gb, rb, la, oms, *rope, x16, w_in, wg, w_out = _in_proj(
        x, w_in_t, wg_s, b_gate, _rope_angles(positions), w_out_s)
    attn, cat_a = _swa_fwd(sinks, qa, k_pad, v_pad, ga)
    o, cat_b, sprev = _gla_fwd(qb, kb, vb, la, gb, norm_w)
    loss, gx0, d_cat_a, d_cat_b, g_w_out, g_ln = _out_ln_loss(cat_a, cat_b, w_out, x, target, ln_g, ln_b)
    parts_w_out = g_w_out.reshape(N_DEV, D_OUT_SHARD, D_MODEL)
    dqa, dga, dka, dva, g_sinks, g_out = _swa_bwd(sinks, qa, k_pad, v_pad, attn, ga, d_cat_a, rope, parts_w_out)
    dqb, dkb, dvb, dgb, drb, g_wg, g_bg, g_nw = _gla_bwd(qb, kb, vb, la, oms, gb, o, sprev, d_cat_b, rb, wg, norm_w)
    pieces = (dqa, dka, dva, dga, dqb, dkb, dvb, dgb, drb)
    grad_x = _in_proj_bwd_x(gx0, pieces, w_in, rope)
    parts_wg = jnp.transpose(g_wg.reshape(GLA_RANK, N_DEV, 32), (1, 0, 2))
    g_in, r_wg, r_small = _in_proj_bwd_w(x16, pieces, rope, parts_wg, g_ln, g_bg, g_nw, g_sinks, loss)
    return grad_x, g_in, g_out, r_wg, r_small


def _mesh_pos():
    return lax.axis_index("x"), lax.axis_index("y"), lax.axis_index("c")


def _peer(k, x, y, c):
    px = (1 - x) if k & 4 else x
    py = (1 - y) if k & 2 else y
    pc = (1 - c) if k & 1 else c
    return (px, py, pc), 4 * px + 2 * py + pc


def _other_chips(x, y):
    return [(1 - x, y), (x, 1 - y), (1 - x, 1 - y)]


def _shard_view(t):
    return jnp.transpose(t, (2, 0, 1))


class _BlockGather:
    def __init__(self, slots, send_sems, recv_sems):
        self.slots, self.send_sems, self.recv_sems = slots, send_sems, recv_sems
        x, y, c = _mesh_pos()
        self.xy, self.c, self.me, self.sibling = (x, y), c, 4 * x + 2 * y + c, (x, y, 1 - c)
        self.chips = _other_chips(x, y)

    @staticmethod
    def scratch():
        return [pltpu.SemaphoreType.DMA((N_DEV - 1,)), pltpu.SemaphoreType.DMA((N_DEV - 1,))]

    def _copy(self, k, block, to):
        return pltpu.make_async_remote_copy(
            src_ref=self.slots.at[block], dst_ref=self.slots.at[block], send_sem=self.send_sems.at[k],
            recv_sem=self.recv_sems.at[k], device_id=to, device_id_type=pl.DeviceIdType.MESH)

    def start(self):
        for j, (cx, cy) in enumerate(self.chips):
            self._copy(1 + j, self.me, (cx, cy, self.c)).start()
        self._copy(0, self.me, self.sibling).start()

    def forward(self):
        for j, (cx, cy) in enumerate(self.chips):
            block = 4 * cx + 2 * cy + self.c
            self._copy(1 + j, block, self.sibling).wait_recv()
            self._copy(4 + j, block, self.sibling).start()

    def finish(self):
        x, y = self.xy
        self._copy(0, 4 * x + 2 * y + (1 - self.c), self.sibling).wait_recv()
        for j, (cx, cy) in enumerate(self.chips):
            self._copy(4 + j, 4 * cx + 2 * cy + (1 - self.c), self.sibling).wait_recv()
        for k in range(N_DEV - 1):
            self._copy(k, self.me, self.sibling).wait_send()


class _OwnerSum:
    def __init__(self, parts, own, sib, snd, rcv, loc_sems, d2d_send, d2d_recv, ici_send, ici_recv):
        self.parts, self.own, self.sib, self.snd, self.rcv = parts, own, sib, snd, rcv
        self.sems = (loc_sems, d2d_send, d2d_recv, ici_send, ici_recv)
        x, y, c = _mesh_pos()
        self.c, self.sibling = c, (x, y, 1 - c)
        self.chips = [(x, y)] + _other_chips(x, y)

    @staticmethod
    def scratch(block):
        return [pltpu.VMEM((4,) + block, F32), pltpu.VMEM((4,) + block, F32),
                pltpu.VMEM((3,) + block, MXU_DTYPE), pltpu.VMEM((3,) + block, MXU_DTYPE),
                pltpu.SemaphoreType.DMA((4,)), pltpu.SemaphoreType.DMA((4,)), pltpu.SemaphoreType.DMA((4,)),
                pltpu.SemaphoreType.DMA((3,)), pltpu.SemaphoreType.DMA((3,))]

    def _local(self, r):
        cx, cy = self.chips[r]
        return pltpu.make_async_copy(self.parts.at[4 * cx + 2 * cy + self.c], self.own.at[r], self.sems[0].at[r])

    def _d2d(self, r):
        cx, cy = self.chips[r]
        return pltpu.make_async_remote_copy(
            src_ref=self.parts.at[4 * cx + 2 * cy + (1 - self.c)], dst_ref=self.sib.at[r], send_sem=self.sems[1].at[r],
            recv_sem=self.sems[2].at[r], device_id=self.sibling, device_id_type=pl.DeviceIdType.MESH)

    def _ici(self, r):
        cx, cy = self.chips[r]
        return pltpu.make_async_remote_copy(
            src_ref=self.snd.at[r - 1], dst_ref=self.rcv.at[r - 1], send_sem=self.sems[3].at[r - 1],
            recv_sem=self.sems[4].at[r - 1], device_id=(cx, cy, self.c), device_id_type=pl.DeviceIdType.MESH)

    def start(self):
        for r in (1, 2, 3, 0):
            self._local(r).start()
            self._d2d(r).start()

    def forward(self):
        for r in (1, 2, 3):
            self._local(r).wait()
            self._d2d(r).wait_recv()
            self.snd[r - 1] = (self.own[r] + self.sib[r]).astype(self.snd.dtype)
            self._ici(r).start()

    def finish(self):
        self._local(0).wait()
        self._d2d(0).wait_recv()
        acc = self.own[0] + self.sib[0]
        for r in (1, 2, 3):
            self._ici(r).wait_recv()
            acc = acc + self.rcv[r - 1].astype(F32)
        for r in range(4):
            self._d2d(r).wait_send()
        for r in (1, 2, 3):
            self._ici(r).wait_send()
        return acc


SMALL_ROWS = 48


def _adamw_math(g, w, m, v):
    nm = ADAM_B1 * m + (1.0 - ADAM_B1) * g
    nv = ADAM_B2 * v + (1.0 - ADAM_B2) * (g * g)
    m_hat = nm / (1.0 - ADAM_B1 ** ADAM_STEP)
    v_hat = nv / (1.0 - ADAM_B2 ** ADAM_STEP)
    return -ADAM_LR * (m_hat / (jnp.sqrt(v_hat) + ADAM_EPS) + ADAM_WD * w), nm, nv


def _adamw_shard_view(g, w, m, v):
    rows, width = g.shape
    parts = 3
    pr = rows // parts
    assert pr * parts == rows

    def body(g_ref, w_hbm, m_hbm, v_hbm, g_out, d_out, nm_out, nv_out, bufs, outs, sems):
        def load(k, i, src):
            return pltpu.make_async_copy(src.at[pl.ds(pr * k, pr), 0, :], bufs.at[k, i], sems.at[7 * k + i])

        def store(k, i, dst):
            return pltpu.make_async_copy(outs.at[k, i], dst.at[pl.ds(pr * k, pr), 0, :], sems.at[7 * k + 3 + i])

        for k in range(parts):
            for i, src in enumerate((w_hbm, m_hbm, v_hbm)):
                load(k, i, src).start()
        for k in range(parts):
            for i, src in enumerate((w_hbm, m_hbm, v_hbm)):
                load(k, i, src).wait()
            gk = g_ref[pr * k:pr * (k + 1), :]
            outs[k, 0] = gk
            outs[k, 1], outs[k, 2], outs[k, 3] = _adamw_math(gk, bufs[k, 0], bufs[k, 1], bufs[k, 2])
            for i, dst in enumerate((g_out, d_out, nm_out, nv_out)):
                store(k, i, dst).start()
        for k in range(parts):
            for i, dst in enumerate((g_out, d_out, nm_out, nv_out)):
                store(k, i, dst).wait()

    hbm = pl.BlockSpec(memory_space=pl.ANY)
    return pl.pallas_call(
        body, name="adamw_w_in",
        in_specs=[pl.BlockSpec(memory_space=pltpu.VMEM), hbm, hbm, hbm], out_specs=[hbm] * 4,
        out_shape=[jax.ShapeDtypeStruct((rows, 1, width), F32)] * 4,
        scratch_shapes=[pltpu.VMEM((parts, 3, pr, width), F32), pltpu.VMEM((parts, 4, pr, width), F32),
                        pltpu.SemaphoreType.DMA((7 * parts,))],
        compiler_params=_cparams(),
    )(g, w, m, v)


def _adamw_vectors(r_small, r_wg, g_out, params):
    n_par = len(params)

    def body(rsm_ref, rwg_ref, gout_ref, *refs):
        ins, outs = refs[:3 * n_par], refs[3 * n_par:]
        g = rsm_ref[0]
        gwg = rwg_ref[0]
        for j in range(1, N_DEV):
            g = g + rsm_ref[j]
            gwg = gwg + rwg_ref[j]
        outs[4 * n_par][...] = g[40:41]
        grads = [gwg,
                 jnp.concatenate([g[r:r + 1] for r in range(0, 8)], axis=1),
                 jnp.concatenate([g[r:r + 1] for r in range(8, 16)], axis=1),
                 jnp.concatenate([g[16:17], g[17:18]], axis=1),
                 g[24:25],
                 g[32:33, 0:SWA_Q_HEADS],
                 gout_ref[...]]
        for p, gp in enumerate(grads):
            w_ref, m_ref, v_ref = ins[3 * p:3 * p + 3]
            outs[4 * p][...] = gp
            outs[4 * p + 1][...], outs[4 * p + 2][...], outs[4 * p + 3][...] = _adamw_math(
                gp, w_ref[...], m_ref[...], v_ref[...])

    vmem = pl.BlockSpec(memory_space=pltpu.VMEM)
    flat = [t for wmv in params for t in wmv]
    return pl.pallas_call(
        body, name="adamw_vectors",
        in_specs=[vmem] * (3 + len(flat)), out_specs=[vmem] * (4 * n_par + 1),
        out_shape=[jax.ShapeDtypeStruct(wmv[0].shape, F32) for wmv in params for _ in range(4)]
        + [jax.ShapeDtypeStruct((1, 128), F32)],
        compiler_params=_cparams(),
    )(r_small, r_wg, g_out, *flat)


def kernel(x, positions, w_in, gla_w_gate_up, gla_b_gate, attn_sinks, gla_norm_w, w_out, ln_g, ln_b, loss_target, m_w_in, m_gla_w_gate_up, m_gla_b_gate, m_attn_sinks, m_gla_norm_w, m_w_out, m_ln_g, m_ln_b, v_w_in, v_gla_w_gate_up, v_gla_b_gate, v_attn_sinks, v_gla_norm_w, v_w_out, v_ln_g, v_ln_b):
    grad_x, g_in, g_out, r_wg, r_small = _local_step(
        x[0], positions[0], _shard_view(w_in), gla_w_gate_up[0], gla_b_gate, attn_sinks[0], gla_norm_w, w_out[0],
        ln_g, ln_b, loss_target[0])

    upd_in = _adamw_shard_view(g_in, _shard_view(w_in), _shard_view(m_w_in), _shard_view(v_w_in))
    upd_in = [jnp.transpose(t, (1, 2, 0)) for t in upd_in]
    vec = _adamw_vectors(r_small, r_wg, g_out, [
        (gla_w_gate_up[0], m_gla_w_gate_up[0], v_gla_w_gate_up[0]), (ln_g, m_ln_g, v_ln_g), (ln_b, m_ln_b, v_ln_b),
        (gla_b_gate, m_gla_b_gate, v_gla_b_gate), (gla_norm_w, m_gla_norm_w, v_gla_norm_w),
        (attn_sinks, m_attn_sinks, v_attn_sinks), (w_out[0], m_w_out[0], v_w_out[0])])

    outs = [vec[28][0, 0], grad_x[None]]
    for kind in range(4):
        u_wg, u_ln_g, u_ln_b, u_bg, u_nw, u_sinks, u_out = (vec[4 * p + kind] for p in range(7))
        outs += [upd_in[kind], u_wg[None], u_bg, u_sinks, u_nw, u_out[None], u_ln_g, u_ln_b]
    return tuple(outs)
```

```python
import jax
import jax.numpy as jnp
from jax import lax
from jax.experimental import pallas as pl
from jax.experimental.pallas import tpu as pltpu

F32 = jnp.float32
MXU_DTYPE = jnp.bfloat16

N_DEV = 8
D_MODEL = 1024
SWA_Q_HEADS = 8
SWA_KV_HEADS = 2
SWA_GROUP = 4
SWA_HEAD_DIM = 64
BLOCK = 128
ROPE_THETA = 500000.0
ROT_DIM = 16
GLA_HEADS = 4
GLA_DK = 64
GLA_DV = 128
GLA_RANK = 16
GLA_TAU = 16.0
GLA_CHUNK = 64
D_IN_PROJ = 2832
D_IN_SHARD = D_IN_PROJ // N_DEV
D_OUT_SHARD = D_MODEL // N_DEV
OFF = (0, 512, 640, 768, 1280, 1536, 1792, 2304, 2816, 2832)
EPS = 1e-5
ALPHA = 2.0 ** 0.25
SWA_SCALE = SWA_HEAD_DIM ** -0.5
GLA_SCALE = GLA_DK ** -0.5
ADAM_LR = 0.001
ADAM_B1 = 0.9
ADAM_B2 = 0.999
ADAM_EPS = 1e-08
ADAM_WD = 0.01
ADAM_STEP = 10
VMEM_LIMIT = 56 * 1024 * 1024

_NT = (((1,), (1,)), ((), ()))
_TN = (((0,), (0,)), ((), ()))


def _mm(a, b):
    return jnp.dot(a, b, preferred_element_type=F32)


def _mm_nt(a, b):
    return lax.dot_general(a, b, _NT, preferred_element_type=F32)


def _mm_tn(a, b):
    return lax.dot_general(a, b, _TN, preferred_element_type=F32)


def _sigmoid(t):
    return 1.0 / (1.0 + jnp.exp(-t))


def _cparams(**kw):
    return pltpu.CompilerParams(vmem_limit_bytes=VMEM_LIMIT, **kw)


def _full(shape):
    return pl.BlockSpec(shape, lambda *_: (0,) * len(shape))


def _rows(tile, width):
    return pl.BlockSpec((tile, width), lambda i: (i, 0))


def _rope_angles(positions):
    half = ROT_DIM // 2
    inv_freq = ROPE_THETA ** (-jnp.arange(half, dtype=F32) / half)
    ang = positions.astype(F32)[:, None] * inv_freq[None, :]
    return jnp.concatenate([jnp.cos(ang), jnp.sin(ang)], axis=1)


def _split3_parts(t):
    hi = t.astype(MXU_DTYPE)
    r1 = t - hi.astype(F32)
    mid = r1.astype(MXU_DTYPE)
    return hi, mid, (r1 - mid.astype(F32)).astype(MXU_DTYPE)


def _rope_tables(cs):
    half = ROT_DIM // 2
    i = lax.broadcasted_iota(jnp.int32, (2 * half, 3 * 128), 0)
    lane = lax.broadcasted_iota(jnp.int32, (2 * half, 3 * 128), 1)
    table, pos = _idiv(lane, 128), lane & (SWA_HEAD_DIM - 1)
    is_c = (table == 0) & (pos < ROT_DIM) & ((pos & (half - 1)) == i)
    is_s1 = (table == 1) & (pos < half) & (pos + half == i)
    is_s2 = (table == 2) & (pos >= half) & (pos < ROT_DIM) & (pos == i)
    sel = jnp.where(is_c | is_s2, 1.0, jnp.where(is_s1, -1.0, 0.0)).astype(MXU_DTYPE)
    hi, mid, lo = _split3_parts(cs)
    t = (_mm(hi, sel) + _mm(mid, sel)) + _mm(lo, sel)
    pos1 = lax.broadcasted_iota(jnp.int32, (1, 128), 1) & (SWA_HEAD_DIM - 1)
    return t[:, 0:128] + jnp.where(pos1 >= ROT_DIM, 1.0, 0.0), t[:, 128:256], t[:, 256:384]


def _rope(t, c, s1, s2):
    return t * c + pltpu.roll(t, 120, 1) * s1 + pltpu.roll(t, 8, 1) * s2


def _rope_t(g, c, s1, s2):
    return g * c + pltpu.roll(g * s1, 8, 1) + pltpu.roll(g * s2, 120, 1)


def _in_proj(x, w_in_t, wg_s, b_gate, cos_sin, w_out_s):
    s = x.shape[0]
    ts = min(512, s)
    nsteps = s // ts
    forward_step, far_step = min(3, nsteps - 1), min(5, nsteps - 1)
    widths = [OFF[i + 1] - OFF[i] for i in range(9)]

    def body(x_ref, win_hbm, wgs_ref, bg_ref, cs_ref, wos_ref,
             qa_ref, ka_ref, va_ref, ga_ref, qb_ref, kb_ref, vb_ref, gb_ref, rb_ref, la_ref, oms_ref,
             c_ref, s1_ref, s2_ref, x16_ref, w_ref, wg_ref, wout_ref,
             win_all, wg_all, wout_all, stage, stage_sem, *sems):
        xb = x_ref[...].astype(MXU_DTYPE)
        x16_ref[...] = xb
        c, s1, s2 = _rope_tables(cs_ref[...])
        c_ref[...], s1_ref[...], s2_ref[...] = c, s1, s2
        i0 = pl.program_id(0)
        gather = _BlockGather(wout_all, *sems[0:2])

        @pl.when(i0 == 0)
        def _():
            ka_ref[0:BLOCK, :] = jnp.zeros((BLOCK, 128), ka_ref.dtype)
            va_ref[0:BLOCK, :] = jnp.zeros((BLOCK, 128), va_ref.dtype)
            first = (_BlockGather(win_all, *sems[2:4]), _BlockGather(wg_all, *sems[4:6]))
            load = pltpu.make_async_copy(win_hbm.at[:, 0, :], stage, stage_sem)
            load.start()
            wout_all[gather.me] = wos_ref[...].astype(wout_all.dtype)
            wg_all[gather.me] = wgs_ref[...].astype(wg_all.dtype)
            load.wait()
            win_all[gather.me] = stage[...].astype(win_all.dtype)
            for stage_of in ("start", "forward", "forward_far", "finish"):
                for g in first:
                    getattr(g, stage_of)()
            gather.start()
            for j in range(N_DEV):
                w_ref[D_IN_SHARD * j:D_IN_SHARD * (j + 1), :] = win_all[j]
                wg_ref[:, 32 * j:32 * (j + 1)] = wg_all[j]

        @pl.when(i0 == forward_step)
        def _():
            gather.forward()

        @pl.when(i0 == far_step)
        def _():
            gather.forward_far()

        @pl.when(i0 == nsteps - 1)
        def _():
            gather.finish()
            for j in range(N_DEV):
                wout_ref[D_OUT_SHARD * j:D_OUT_SHARD * (j + 1), :] = wout_all[j]

        kv_rows = pl.ds(pl.multiple_of(BLOCK + i0 * ts, BLOCK), ts)

        def cols(i):
            return _mm_nt(xb, w_ref[OFF[i]:OFF[i + 1], :])

        qa = cols(0)
        for i in range(4):
            qa_ref[:, 128 * i:128 * (i + 1)] = _rope(qa[:, 128 * i:128 * (i + 1)], c, s1, s2).astype(qa_ref.dtype)
        kv = _mm_nt(xb, w_ref[OFF[1]:OFF[3], :])
        ka_ref[kv_rows, :] = _rope(kv[:, 0:128], c, s1, s2).astype(ka_ref.dtype)
        va_ref[kv_rows, :] = kv[:, 128:256].astype(va_ref.dtype)
        ga_ref[...] = cols(3)
        qb_ref[...] = cols(4)
        kb_ref[...] = cols(5)
        vb_ref[...] = cols(6).astype(vb_ref.dtype)
        gb_ref[...] = cols(7)
        rb = cols(8)
        rb_ref[...] = rb
        logit = _mm(rb.astype(MXU_DTYPE), wg_ref[...]) + bg_ref[...]
        e = jnp.exp(-jnp.abs(logit))
        la_ref[...] = (jnp.minimum(logit, 0.0) - jnp.log(1.0 + e)) / GLA_TAU
        oms_ref[...] = jnp.where(logit >= 0.0, e, 1.0) / (1.0 + e)

    out_shape = [jax.ShapeDtypeStruct((s + BLOCK if i in (1, 2) else s, w), MXU_DTYPE if i in (0, 1, 2, 6) else F32)
                 for i, w in enumerate(widths)]
    out_shape += [jax.ShapeDtypeStruct((s, 256), F32)] * 2 + [jax.ShapeDtypeStruct((s, 128), F32)] * 3
    out_shape += [jax.ShapeDtypeStruct((s, D_MODEL), MXU_DTYPE)]
    out_shape += [jax.ShapeDtypeStruct((D_IN_PROJ, D_MODEL), MXU_DTYPE), jax.ShapeDtypeStruct((GLA_RANK, 256), MXU_DTYPE),
                  jax.ShapeDtypeStruct((D_MODEL, D_MODEL), MXU_DTYPE)]
    return pl.pallas_call(
        body, name="in_proj", grid=(nsteps,),
        in_specs=[_rows(ts, D_MODEL), pl.BlockSpec(memory_space=pl.ANY), _full((GLA_RANK, 32)), _full((1, 256)),
                  _rows(ts, ROT_DIM), _full((D_OUT_SHARD, D_MODEL))],
        out_specs=[_full((s + BLOCK, w)) if i in (1, 2) else _rows(ts, w) for i, w in enumerate(widths)]
        + [_rows(ts, 256)] * 2 + [_rows(ts, 128)] * 3 + [_rows(ts, D_MODEL)]
        + [_full((D_IN_PROJ, D_MODEL)), _full((GLA_RANK, 256)), _full((D_MODEL, D_MODEL))],
        out_shape=out_shape,
        scratch_shapes=[pltpu.VMEM((N_DEV, D_IN_SHARD, D_MODEL), MXU_DTYPE), pltpu.VMEM((N_DEV, GLA_RANK, 32), MXU_DTYPE),
                        pltpu.VMEM((N_DEV, D_OUT_SHARD, D_MODEL), MXU_DTYPE),
                        pltpu.VMEM((D_IN_SHARD, D_MODEL), F32), pltpu.SemaphoreType.DMA]
        + 3 * _BlockGather.scratch(),
        compiler_params=_cparams(dimension_semantics=("arbitrary",)),
    )(x, w_in_t, wg_s, b_gate, cos_sin, w_out_s)


SWA_ROWS = SWA_GROUP * BLOCK


def _swa_bias():
    shape = (2, 2 * BLOCK, SWA_ROWS)
    ki = lax.broadcasted_iota(jnp.int32, shape, 1)
    qi = lax.broadcasted_iota(jnp.int32, shape, 2) & (BLOCK - 1)
    first = lax.broadcasted_iota(jnp.int32, shape, 0) == 0
    dist = qi + BLOCK - ki
    ok = (dist >= 0) & (dist < BLOCK) & (jnp.logical_not(first) | (ki >= BLOCK))
    return jnp.where(ok, 0.0, -jnp.inf).astype(F32)


SWA_SUB = 8


def _swa_bias_of(bias_ref, n, b):
    return bias_ref[jnp.minimum(n, 1)] if b == 0 else bias_ref[1]


def _swa_dup(t, j):
    t = t.astype(F32)
    low = lax.broadcasted_iota(jnp.int32, t.shape, 1) < SWA_HEAD_DIM
    keep = low if j == 0 else jnp.logical_not(low)
    return jnp.where(keep, t, pltpu.roll(t, SWA_HEAD_DIM, 1)).astype(MXU_DTYPE)


def _swa_stack(t, j):
    low = lax.broadcasted_iota(jnp.int32, (BLOCK, 128), 1) < SWA_HEAD_DIM
    zero = jnp.zeros((BLOCK, 128), t.dtype)
    blocks = []
    for p in (2 * j, 2 * j + 1):
        tp = t[:, 128 * p:128 * (p + 1)]
        blocks += [jnp.where(low, tp, zero), jnp.where(low, zero, tp)]
    return jnp.concatenate(blocks, axis=0)


def _swa_unstack(t):
    low = lax.broadcasted_iota(jnp.int32, (BLOCK, 128), 1) < SWA_HEAD_DIM
    return [jnp.where(low, t[2 * BLOCK * i:2 * BLOCK * i + BLOCK], t[2 * BLOCK * i + BLOCK:2 * BLOCK * (i + 1)])
            for i in range(2)]


def _swa_sink_row(sink_ref, j):
    lane = lax.broadcasted_iota(jnp.int32, (1, SWA_ROWS), 1)
    row = jnp.full((1, SWA_ROWS), sink_ref[SWA_GROUP * j], F32)
    for r in range(1, SWA_GROUP):
        row = jnp.where(lane >= BLOCK * r, sink_ref[SWA_GROUP * j + r], row)
    return row


def _split3(t):
    return jnp.concatenate(_split3_parts(t), axis=1)


def _row_sums_as_row(t):
    ones = jnp.ones((8, 3 * t.shape[1]), MXU_DTYPE)
    return _mm_nt(ones, _split3(t))[0:1, :]


def _swa_probs_t(qs, kd, bias_t, sink):
    sc = _mm_nt(kd, qs) + bias_t
    m = jnp.maximum(jnp.max(sc, axis=0, keepdims=True), sink)
    p = jnp.exp(sc - m)
    ps = jnp.exp(sink - m)
    rinv = 1.0 / (jnp.sum(p, axis=0, keepdims=True) + ps)
    return p * rinv, ps * rinv


def _swa_fwd(sinks, qa, k_pad, v_pad, ga):
    s = qa.shape[0]
    sub = min(SWA_SUB, s // BLOCK)
    tq = sub * BLOCK

    def body(sink_ref, qa_ref, ga_ref, bias_ref, k_ref, v_ref, attn_ref, cat_ref):
        n = pl.program_id(0)
        for b in range(sub):
            rows = slice(BLOCK * b, BLOCK * (b + 1))
            start = pl.multiple_of((n * sub + b) * BLOCK, BLOCK)
            kw = k_ref[pl.ds(start, 2 * BLOCK), :]
            vw = v_ref[pl.ds(start, 2 * BLOCK), :]
            bias_t = _swa_bias_of(bias_ref, n, b)
            q = qa_ref[rows, :] * SWA_SCALE
            g = ga_ref[rows, :]
            silu = g * _sigmoid(g)
            for j in range(SWA_KV_HEADS):
                qs = _swa_stack(q, j).astype(MXU_DTYPE)
                probs, _ = _swa_probs_t(qs, _swa_dup(kw, j), bias_t, _swa_sink_row(sink_ref, j))
                pairs = _swa_unstack(_mm_tn(probs.astype(MXU_DTYPE), _swa_dup(vw, j)))
                for i in range(2):
                    lanes = slice(128 * (2 * j + i), 128 * (2 * j + i + 1))
                    attn_ref[rows, lanes] = pairs[i]
                    cat_ref[rows, lanes] = (pairs[i] * silu[:, lanes]).astype(cat_ref.dtype)

    return pl.pallas_call(
        body, name="swa_fwd", grid=(s // tq,),
        in_specs=[pl.BlockSpec(memory_space=pltpu.SMEM), _rows(tq, 512), _rows(tq, 512),
                  _full((2, 2 * BLOCK, SWA_ROWS)), _full((s + BLOCK, 128)), _full((s + BLOCK, 128))],
        out_specs=[_rows(tq, 512), _rows(tq, 512)],
        out_shape=[jax.ShapeDtypeStruct((s, 512), F32), jax.ShapeDtypeStruct((s, 512), MXU_DTYPE)],
        compiler_params=_cparams(dimension_semantics=("arbitrary",)),
    )(sinks, qa, ga, _swa_bias(), k_pad, v_pad)


GLA_KW = GLA_HEADS * GLA_DK
GLA_VW = GLA_HEADS * GLA_DV


def _idiv(t, d):
    return t >> (d.bit_length() - 1)


def _chunk_cumsum(t, lower):
    n, w = t.shape
    r = lax.broadcasted_iota(jnp.int32, (n, n), 0)
    c = lax.broadcasted_iota(jnp.int32, (n, n), 1)
    tri = ((_idiv(r, GLA_CHUNK) == _idiv(c, GLA_CHUNK)) & ((r >= c) if lower else (r <= c))).astype(MXU_DTYPE)
    parts = _mm(tri, _split3(t))
    return (parts[:, :w] + parts[:, w:2 * w]) + parts[:, 2 * w:]


def _chunk_last(t):
    n = t.shape[0]
    return jnp.concatenate(
        [jnp.broadcast_to(t[c + GLA_CHUNK - 1:c + GLA_CHUNK, :], (GLA_CHUNK, t.shape[1]))
         for c in range(0, n, GLA_CHUNK)], axis=0)


def _head_stack(t, width):
    head = _idiv(lax.broadcasted_iota(jnp.int32, t.shape, 1), width)
    zero = jnp.zeros_like(t)
    return jnp.concatenate([jnp.where(head == h, t, zero) for h in range(GLA_HEADS)], axis=0)


def _heads_to_rows(t):
    return jnp.concatenate([t[:, GLA_DV * h:GLA_DV * (h + 1)] for h in range(GLA_HEADS)], axis=0)


def _rows_to_heads(t):
    return jnp.concatenate([t[GLA_CHUNK * h:GLA_CHUNK * (h + 1)] for h in range(GLA_HEADS)], axis=1)


def _state_by_head(t):
    srow = _idiv(lax.broadcasted_iota(jnp.int32, (GLA_VW, GLA_KW), 0), GLA_DV)
    slane = _idiv(lax.broadcasted_iota(jnp.int32, (GLA_VW, GLA_KW), 1), GLA_DK)
    return jnp.where(srow == slane, jnp.concatenate([t] * GLA_HEADS, axis=0), jnp.zeros((GLA_VW, GLA_KW), t.dtype))


def _gla_masks():
    row = lax.broadcasted_iota(jnp.int32, (GLA_CHUNK, GLA_KW), 0)
    pos = lax.broadcasted_iota(jnp.int32, (GLA_CHUNK, GLA_KW), 1) & (GLA_CHUNK - 1)
    return pos <= row, pos >= row


def _gla_fwd(qb, kb, vb, la, gb, norm_w):
    s = qb.shape[0]
    tb = min(256, s)
    ch = tb // GLA_CHUNK

    def body(qb_ref, kb_ref, vb_ref, la_ref, gb_ref, nw_ref, o_ref, cat_ref, sp_ref, st_ref):
        @pl.when(pl.program_id(0) == 0)
        def _():
            st_ref[...] = jnp.zeros_like(st_ref)

        causal, _ = _gla_masks()
        nw = nw_ref[...]
        b = _chunk_cumsum(la_ref[...], True)
        bl = _chunk_last(b)
        k = kb_ref[...]
        qd = ((qb_ref[...] * GLA_SCALE) * jnp.exp(b)).astype(MXU_DTYPE)
        ki = (k * jnp.exp(-b)).astype(MXU_DTYPE)
        ke = (k * jnp.exp(bl - b)).astype(MXU_DTYPE)
        dec = jnp.exp(bl)
        v = vb_ref[...].astype(MXU_DTYPE)
        g = gb_ref[...]
        silu = g * _sigmoid(g)
        for ci in range(ch):
            rows = slice(GLA_CHUNK * ci, GLA_CHUNK * (ci + 1))
            qds, kis, kes = (_head_stack(t[rows], GLA_DK) for t in (qd, ki, ke))
            a = jnp.where(causal, _mm_nt(qd[rows], kis), 0.0).astype(MXU_DTYPE)
            st = st_ref[...]
            sp_ref[ci] = st
            o = _mm(a, _head_stack(v[rows], GLA_DV)) + _rows_to_heads(_mm_nt(qds, st.astype(MXU_DTYPE)))
            st_ref[...] = st * dec[rows][0:1] + _mm_tn(_heads_to_rows(v[rows]), kes)
            o_ref[rows, :] = o
            for h in range(GLA_HEADS):
                lv = slice(GLA_DV * h, GLA_DV * (h + 1))
                oh = o[:, lv]
                r = lax.rsqrt(jnp.mean(oh * oh, axis=1, keepdims=True) + EPS)
                cat_ref[rows, lv] = (oh * r * nw * silu[rows, lv]).astype(cat_ref.dtype)

    return pl.pallas_call(
        body, name="gla_fwd", grid=(s // tb,),
        in_specs=[_rows(tb, 256), _rows(tb, 256), _rows(tb, 512), _rows(tb, 256), _rows(tb, 512), _full((1, 128))],
        out_specs=[_rows(tb, 512), _rows(tb, 512), pl.BlockSpec((ch, GLA_DV, 256), lambda i: (i, 0, 0))],
        out_shape=[jax.ShapeDtypeStruct((s, 512), F32), jax.ShapeDtypeStruct((s, 512), MXU_DTYPE),
                   jax.ShapeDtypeStruct((s // GLA_CHUNK, GLA_DV, 256), F32)],
        scratch_shapes=[pltpu.VMEM((GLA_DV, GLA_KW), F32)],
        compiler_params=_cparams(dimension_semantics=("arbitrary",)),
    )(qb, kb, vb, la, gb, norm_w)


def _out_ln_loss(cat_a, cat_b, w_out, x, target, ln_g, ln_b):
    s = x.shape[0]
    ts = min(512, s)
    halves = 2 if ts % 32 == 0 else 1
    th = ts // halves

    def body(ca_ref, cb_ref, w_ref, x_ref, t_ref, g_ref, b_ref,
             loss_ref, gx_ref, da_ref, db_ref, gw_ref, gln_ref):
        @pl.when(pl.program_id(0) == 0)
        def _():
            loss_ref[...] = jnp.zeros_like(loss_ref)
            gw_ref[...] = jnp.zeros_like(gw_ref)
            gln_ref[...] = jnp.zeros_like(gln_ref)

        g = g_ref[...]
        dh16s = []
        for k in range(halves):
            rows = slice(th * k, th * (k + 1))
            mix = _mm(ca_ref[rows, :], w_ref[0:512, :]) + _mm(cb_ref[rows, :], w_ref[512:1024, :])
            h = ALPHA * x_ref[rows, :] + mix
            mu = jnp.mean(h, axis=1, keepdims=True)
            hc = h - mu
            rstd = lax.rsqrt(jnp.mean(hc * hc, axis=1, keepdims=True) + EPS)
            xhat = hc * rstd
            err = xhat * g + b_ref[...] - t_ref[rows, :]
            loss_ref[...] += 0.5 * jnp.sum(jnp.mean(err * err, axis=1, keepdims=True))
            dy = err * (1.0 / D_MODEL)
            gln_ref[0:1, :] += jnp.sum(dy * xhat, axis=0, keepdims=True)
            gln_ref[1:2, :] += jnp.sum(dy, axis=0, keepdims=True)
            dxh = dy * g
            dh = rstd * (dxh - jnp.mean(dxh, axis=1, keepdims=True)
                         - xhat * jnp.mean(dxh * xhat, axis=1, keepdims=True))
            gx_ref[rows, :] = ALPHA * dh
            dh16s.append(dh.astype(MXU_DTYPE))
        for k in range(halves):
            rows = slice(th * k, th * (k + 1))
            da_ref[rows, :] = _mm_nt(dh16s[k], w_ref[0:512, :])
            db_ref[rows, :] = _mm_nt(dh16s[k], w_ref[512:1024, :])
        dh16 = jnp.concatenate(dh16s, axis=0)
        gw_ref[0:512, :] += _mm_tn(ca_ref[...], dh16)
        gw_ref[512:1024, :] += _mm_tn(cb_ref[...], dh16)

    return pl.pallas_call(
        body, name="out_ln_loss", grid=(s // ts,),
        in_specs=[_rows(ts, 512), _rows(ts, 512), _full((D_MODEL, D_MODEL)), _rows(ts, D_MODEL), _rows(ts, D_MODEL),
                  _full((1, D_MODEL)), _full((1, D_MODEL))],
        out_specs=[_full((1, 128)), _rows(ts, D_MODEL), _rows(ts, 512), _rows(ts, 512),
                   _full((D_MODEL, D_MODEL)), _full((2, D_MODEL))],
        out_shape=[jax.ShapeDtypeStruct((1, 128), F32), jax.ShapeDtypeStruct((s, D_MODEL), F32),
                   jax.ShapeDtypeStruct((s, 512), F32), jax.ShapeDtypeStruct((s, 512), F32),
                   jax.ShapeDtypeStruct((D_MODEL, D_MODEL), F32), jax.ShapeDtypeStruct((2, D_MODEL), F32)],
        compiler_params=_cparams(dimension_semantics=("arbitrary",)),
    )(cat_a, cat_b, w_out, x, target, ln_g, ln_b)


def _swa_bwd(sinks, qa, k_pad, v_pad, attn, ga, d_cat_a, rope, parts_w_out):
    s = qa.shape[0]
    sub = min(SWA_SUB, s // BLOCK)
    tq = sub * BLOCK
    nsteps = s // tq
    forward_step = min(1, nsteps - 1)

    def body(sink_ref, qa_ref, ga_ref, at_ref, dc_ref, c_ref, s1_ref, s2_ref, bias_ref, k_ref, v_ref, pout_ref,
             dq_ref, dg_ref, dk_out, dv_out, ds_ref, gout_ref, dk_ref, dv_ref, *scratch):
        n = pl.program_id(0)
        owner_sum = _OwnerSum(pout_ref, *scratch)

        @pl.when(n == 0)
        def _():
            dk_ref[...] = jnp.zeros_like(dk_ref)
            dv_ref[...] = jnp.zeros_like(dv_ref)
            ds_ref[...] = jnp.zeros_like(ds_ref)
            owner_sum.start()

        @pl.when(n == forward_step)
        def _():
            owner_sum.forward()

        @pl.when(n == nsteps - 1)
        def _():
            gout_ref[...] = owner_sum.finish()

        low = lax.broadcasted_iota(jnp.int32, (2 * BLOCK, 128), 1) < SWA_HEAD_DIM
        for b in range(sub):
            rows = slice(BLOCK * b, BLOCK * (b + 1))
            start = pl.multiple_of((n * sub + b) * BLOCK, BLOCK)
            kw = k_ref[pl.ds(start, 2 * BLOCK), :]
            vw = v_ref[pl.ds(start, 2 * BLOCK), :]
            bias_t = _swa_bias_of(bias_ref, n, b)
            q = qa_ref[rows, :] * SWA_SCALE
            g = ga_ref[rows, :]
            sg = _sigmoid(g)
            o = at_ref[rows, :]
            dc = dc_ref[rows, :]
            do = dc * (g * sg)
            dg_ref[rows, :] = (dc * o * (sg * (1.0 + g * (1.0 - sg)))).astype(dg_ref.dtype)
            od = do * o
            c, s1, s2 = c_ref[rows, :], s1_ref[rows, :], s2_ref[rows, :]
            dk, dv = [], []
            for j in range(SWA_KV_HEADS):
                kd, vd = _swa_dup(kw, j), _swa_dup(vw, j)
                qs = _swa_stack(q, j).astype(MXU_DTYPE)
                dos = _swa_stack(do, j).astype(MXU_DTYPE)
                probs, psink = _swa_probs_t(qs, kd, bias_t, _swa_sink_row(sink_ref, j))
                delta = _row_sums_as_row(_swa_stack(od, j))
                dsc = (probs * (_mm_nt(vd, dos) - delta)).astype(MXU_DTYPE)
                dsink = psink * delta
                for r in range(SWA_GROUP):
                    h = SWA_GROUP * j + r
                    ds_ref[h:h + 1, :] += jnp.zeros((1, 128), F32) - jnp.sum(dsink[:, BLOCK * r:BLOCK * (r + 1)])
                dq = _swa_unstack(_mm_tn(dsc, kd))
                for i in range(2):
                    lanes = slice(128 * (2 * j + i), 128 * (2 * j + i + 1))
                    dq_ref[rows, lanes] = _rope_t(dq[i] * SWA_SCALE, c, s1, s2).astype(dq_ref.dtype)
                dkj = _mm(dsc, qs)
                dvj = _mm(probs.astype(MXU_DTYPE), dos)
                dk.append(dkj + pltpu.roll(dkj, SWA_HEAD_DIM, 1))
                dv.append(dvj + pltpu.roll(dvj, SWA_HEAD_DIM, 1))
            dk_ref[pl.ds(start, 2 * BLOCK), :] += jnp.where(low, dk[0], dk[1])
            dv_ref[pl.ds(start, 2 * BLOCK), :] += jnp.where(low, dv[0], dv[1])

        @pl.when(n == nsteps - 1)
        def _():
            dk_out[...] = dk_ref[BLOCK:, :]
            dv_out[...] = dv_ref[BLOCK:, :]

    out_blk = parts_w_out.shape[1:]
    return pl.pallas_call(
        body, name="swa_bwd", grid=(nsteps,),
        in_specs=[pl.BlockSpec(memory_space=pltpu.SMEM)] + [_rows(tq, 512)] * 4 + [_rows(tq, 128)] * 3
        + [_full((2, 2 * BLOCK, SWA_ROWS))] + [_full((s + BLOCK, 128))] * 2 + [pl.BlockSpec(memory_space=pl.ANY)],
        out_specs=[_rows(tq, 512), _rows(tq, 512), _full((s, 128)), _full((s, 128)),
                   _full((SWA_Q_HEADS, 128)), _full(out_blk)],
        out_shape=[jax.ShapeDtypeStruct((s, 512), MXU_DTYPE), jax.ShapeDtypeStruct((s, 512), MXU_DTYPE),
                   jax.ShapeDtypeStruct((s, 128), F32), jax.ShapeDtypeStruct((s, 128), F32),
                   jax.ShapeDtypeStruct((SWA_Q_HEADS, 128), F32), jax.ShapeDtypeStruct(out_blk, F32)],
        scratch_shapes=[pltpu.VMEM((s + BLOCK, 128), F32)] * 2 + _OwnerSum.scratch(out_blk),
        compiler_params=_cparams(dimension_semantics=("arbitrary",)),
    )(sinks, qa, ga, attn, d_cat_a, *rope, _swa_bias(), k_pad, v_pad, parts_w_out)


def _gla_bwd(qb, kb, vb, la, oms, gb, o, sprev, d_cat_b, rb, wg, norm_w):
    s = qb.shape[0]
    tb = min(512, s)
    ch = tb // GLA_CHUNK
    nb = s // tb

    def body(qb_ref, kb_ref, vb_ref, la_ref, oms_ref, gb_ref, o_ref, sp_ref, dc_ref, rb_ref, wg_ref, nw_ref,
             dq_ref, dk_ref, dv_ref, dg_ref, dr_ref, gwg_ref, gbg_ref, gnw_ref, dst_ref):
        @pl.when(pl.program_id(0) == 0)
        def _():
            dst_ref[...] = jnp.zeros_like(dst_ref)
            gwg_ref[...] = jnp.zeros_like(gwg_ref)
            gbg_ref[...] = jnp.zeros_like(gbg_ref)
            gnw_ref[...] = jnp.zeros_like(gnw_ref)

        causal, causal_t = _gla_masks()
        nw = nw_ref[...]
        b = _chunk_cumsum(la_ref[...], True)
        bl = _chunk_last(b)
        eb, enb, ee, dec = jnp.exp(b), jnp.exp(-b), jnp.exp(bl - b), jnp.exp(bl)
        k = kb_ref[...]
        qd = (qb_ref[...] * GLA_SCALE) * eb
        ki = k * enb
        ke = k * ee
        qd16, ki16, ke16 = qd.astype(MXU_DTYPE), ki.astype(MXU_DTYPE), ke.astype(MXU_DTYPE)
        v16 = vb_ref[...].astype(MXU_DTYPE)

        g = gb_ref[...]
        sg = _sigmoid(g)
        silu = g * sg
        dsilu = sg * (1.0 + g * (1.0 - sg))
        gnw = jnp.zeros((1, GLA_DV), F32)
        do = []
        for h in range(GLA_HEADS):
            lv = slice(GLA_DV * h, GLA_DV * (h + 1))
            oh = o_ref[:, lv]
            dch = dc_ref[:, lv]
            r = lax.rsqrt(jnp.mean(oh * oh, axis=1, keepdims=True) + EPS)
            d_on = dch * silu[:, lv]
            dg_ref[:, lv] = (dch * (oh * r * nw) * dsilu[:, lv]).astype(dg_ref.dtype)
            gnw += jnp.sum(d_on * oh * r, axis=0, keepdims=True)
            u = d_on * nw
            do.append(r * u - oh * (r * r * r) * jnp.mean(u * oh, axis=1, keepdims=True))
        gnw_ref[...] += gnw
        do16 = jnp.concatenate(do, axis=1).astype(MXU_DTYPE)

        db, dbl = [None] * ch, [None] * ch
        for ci in reversed(range(ch)):
            rows = slice(GLA_CHUNK * ci, GLA_CHUNK * (ci + 1))
            qds, kis, kes = (_head_stack(t[rows], GLA_DK) for t in (qd16, ki16, ke16))
            vs, dos = _head_stack(v16[rows], GLA_DV), _head_stack(do16[rows], GLA_DV)
            a = jnp.where(causal, _mm_nt(qd16[rows], kis), 0.0).astype(MXU_DTYPE)
            at = jnp.where(causal_t, _mm_nt(ki16[rows], qds), 0.0).astype(MXU_DTYPE)
            da = jnp.where(causal, _mm_nt(do16[rows], vs), 0.0).astype(MXU_DTYPE)
            dat = jnp.where(causal_t, _mm_nt(v16[rows], dos), 0.0).astype(MXU_DTYPE)
            st = sp_ref[ci]
            dst = dst_ref[...]
            dst16 = dst.astype(MXU_DTYPE)
            dv = _mm(at, dos) + _rows_to_heads(_mm_nt(kes, dst16))
            dqd = _mm(da, kis) + _mm(do16[rows], _state_by_head(st.astype(MXU_DTYPE)))
            dki = _mm(dat, qds)
            dke = _mm(v16[rows], _state_by_head(dst16))
            ddec = jnp.sum(dst * st, axis=0, keepdims=True)
            decc = dec[rows][0:1]
            dst_ref[...] = _mm_tn(_heads_to_rows(do16[rows]), qds) + dst * decc
            dq_ref[rows, :] = (dqd * eb[rows] * GLA_SCALE).astype(dq_ref.dtype)
            dk_ref[rows, :] = (dki * enb[rows] + dke * ee[rows]).astype(dk_ref.dtype)
            dv_ref[rows, :] = dv.astype(dv_ref.dtype)
            dke_ke = dke * ke[rows]
            db[ci] = dqd * qd[rows] - dki * ki[rows] - dke_ke
            dbl[ci] = jnp.broadcast_to(jnp.sum(dke_ke, axis=0, keepdims=True) + ddec * decc, (GLA_CHUNK, GLA_KW))

        dla = _chunk_cumsum(jnp.concatenate(db, axis=0), False) + jnp.concatenate(dbl, axis=0)
        dlogit = dla * oms_ref[...] * (1.0 / GLA_TAU)
        dl16 = dlogit.astype(MXU_DTYPE)
        gbg_ref[...] += jnp.sum(dlogit, axis=0, keepdims=True)
        gwg_ref[...] += _mm_tn(rb_ref[...].astype(MXU_DTYPE), dl16)
        dr_ref[...] = _mm_nt(dl16, wg_ref[...]).astype(dr_ref.dtype)

    def rev(width):
        return pl.BlockSpec((tb, width), lambda i: (nb - 1 - i, 0))

    return pl.pallas_call(
        body, name="gla_bwd", grid=(nb,),
        in_specs=[rev(256), rev(256), rev(512), rev(256), rev(256), rev(512), rev(512),
                  pl.BlockSpec((ch, GLA_DV, 256), lambda i: (nb - 1 - i, 0, 0)), rev(512), rev(GLA_RANK),
                  _full((GLA_RANK, 256)), _full((1, 128))],
        out_specs=[rev(256), rev(256), rev(512), rev(512), rev(GLA_RANK),
                   _full((GLA_RANK, 256)), _full((1, 256)), _full((1, 128))],
        out_shape=[jax.ShapeDtypeStruct((s, 256), MXU_DTYPE), jax.ShapeDtypeStruct((s, 256), MXU_DTYPE),
                   jax.ShapeDtypeStruct((s, 512), MXU_DTYPE), jax.ShapeDtypeStruct((s, 512), MXU_DTYPE),
                   jax.ShapeDtypeStruct((s, GLA_RANK), MXU_DTYPE), jax.ShapeDtypeStruct((GLA_RANK, 256), F32),
                   jax.ShapeDtypeStruct((1, 256), F32), jax.ShapeDtypeStruct((1, 128), F32)],
        scratch_shapes=[pltpu.VMEM((GLA_DV, GLA_KW), F32)],
        compiler_params=_cparams(dimension_semantics=("arbitrary",)),
    )(qb, kb, vb, la, oms, gb, o, sprev, d_cat_b, rb, wg, norm_w)


def _dproj_tiles(piece_refs, rope_refs, members=(0, 1, 3, 4, 5, 6, 7, 8)):
    for i in members:
        if i == 1:
            dk = _rope_t(piece_refs[1][...], *(r[...] for r in rope_refs))
            yield OFF[1], OFF[3], jnp.concatenate([dk, piece_refs[2][...]], axis=1).astype(MXU_DTYPE)
        else:
            yield OFF[i], OFF[i + 1], piece_refs[i][...].astype(MXU_DTYPE)


def _in_proj_bwd_x(gx0, pieces, w_in, rope):
    s = gx0.shape[0]
    ts = min(512, s)
    widths = [OFF[i + 1] - OFF[i] for i in range(9)]

    def body(gx0_ref, *refs):
        w_ref, gx_ref = refs[12:]
        acc = gx0_ref[...]
        for lo, hi, t16 in _dproj_tiles(refs[:9], refs[9:12]):
            acc += _mm(t16, w_ref[lo:hi, :])
        gx_ref[...] = acc

    return pl.pallas_call(
        body, name="in_proj_bwd_x", grid=(s // ts,),
        in_specs=[_rows(ts, D_MODEL)] + [_rows(ts, w) for w in widths] + [_rows(ts, 128)] * 3
        + [_full((D_IN_PROJ, D_MODEL))],
        out_specs=_rows(ts, D_MODEL),
        out_shape=jax.ShapeDtypeStruct((s, D_MODEL), F32),
        compiler_params=_cparams(dimension_semantics=("arbitrary",)),
    )(gx0, *pieces, *rope, w_in)


GW_GROUPS = ((0, 1), (3, 4), (5, 6), (7, 8))


def _in_proj_bwd_w(x, pieces, rope, parts_wg, g_ln, g_bg, g_nw, g_sinks, loss):
    s = x.shape[0]
    ts = min(1024, s)
    nt = s // ts
    n_groups = len(GW_GROUPS)
    ends = [OFF[3] if members[-1] == 1 else OFF[members[-1] + 1] for members in GW_GROUPS]
    assert all(ends[g] >= D_IN_SHARD * 2 * (g + 1) for g in range(n_groups))
    blk = (D_IN_SHARD, D_MODEL)

    def body(x_hbm, *refs):
        piece_refs, rope_refs = refs[:9], refs[9:12]
        pwg_ref, gln_ref, gbg_ref, gnw_ref, gsk_ref, loss_ref, gin_ref, rwg_ref, rsm_ref = refs[12:21]
        (acc_ref, stage_ref, sib_ref, snd_ref, rcv_ref, mine_ref, sm_ref, xs_ref,
         d2d_send, d2d_recv, ici_send, ici_recv, out_sem, sm_send, sm_recv, sm_loc, x_sems) = refs[21:]
        g, t = pl.program_id(0), pl.program_id(1)
        tile = pl.ds(pl.multiple_of(t * ts, ts), ts)

        def x_load(k):
            rows = pl.ds(pl.multiple_of(k * ts, ts), ts)
            return pltpu.make_async_copy(x_hbm.at[rows, :], xs_ref.at[rows, :], x_sems.at[k])
        x_, y_, c = _mesh_pos()
        me, mychip, sibling = 4 * x_ + 2 * y_ + c, 2 * x_ + y_, (x_, y_, 1 - c)
        small_dsts = (rwg_ref, rsm_ref)

        def small_src(a, block):
            return pwg_ref.at[block] if a == 0 else sm_ref

        def small_copy(k, a, src_block, dst_block, peer):
            i = 2 * (k - 1) + a
            return pltpu.make_async_remote_copy(
                src_ref=small_src(a, src_block), dst_ref=small_dsts[a].at[dst_block], send_sem=sm_send.at[i],
                recv_sem=sm_recv.at[i], device_id=peer, device_id_type=pl.DeviceIdType.MESH)

        def small_local(a):
            return pltpu.make_async_copy(small_src(a, me), small_dsts[a].at[me], sm_loc.at[a])

        @pl.when((g == 0) & (t == 0))
        def _():
            for k in range(nt):
                x_load(k).start()
            acc_ref[...] = jnp.zeros_like(acc_ref)
            sm_ref[...] = jnp.zeros_like(sm_ref)
            for r in range(D_MODEL // 128):
                sm_ref[r:r + 1, :] = gln_ref[0:1, 128 * r:128 * (r + 1)]
                sm_ref[8 + r:9 + r, :] = gln_ref[1:2, 128 * r:128 * (r + 1)]
            for r in range(2):
                sm_ref[16 + r:17 + r, :] = gbg_ref[0:1, 128 * r:128 * (r + 1)]
            sm_ref[24:25, :] = gnw_ref[...]
            diag = (lax.broadcasted_iota(jnp.int32, gsk_ref.shape, 0)
                    == lax.broadcasted_iota(jnp.int32, gsk_ref.shape, 1))
            sm_ref[32:33, :] = jnp.sum(jnp.where(diag, gsk_ref[...], 0.0), axis=0, keepdims=True)
            sm_ref[40:41, :] = loss_ref[...]
            for a in range(2):
                small_local(a).start()
            for k in range(1, N_DEV):
                peer, pidx = _peer(k, x_, y_, c)
                for a in range(2):
                    small_copy(k, a, pidx, me, peer).start()

        @pl.when(g == 0)
        def _():
            x_load(t).wait()

        xb = xs_ref[tile, :]
        for gi, members in enumerate(GW_GROUPS):
            @pl.when(g == gi)
            def _(members=members):
                for lo, hi, t16 in _dproj_tiles(piece_refs, rope_refs, members):
                    acc_ref[lo:hi, :] += _mm_tn(t16, xb)

        def block_rows(j):
            return acc_ref[D_IN_SHARD * j:D_IN_SHARD * (j + 1), :]

        def d2d(gi):
            return pltpu.make_async_remote_copy(
                src_ref=stage_ref.at[gi % 2], dst_ref=sib_ref.at[gi], send_sem=d2d_send.at[gi],
                recv_sem=d2d_recv.at[gi], device_id=sibling, device_id_type=pl.DeviceIdType.MESH)

        def ici(slot, owner):
            return pltpu.make_async_remote_copy(
                src_ref=snd_ref.at[slot], dst_ref=rcv_ref.at[slot], send_sem=ici_send.at[slot],
                recv_sem=ici_recv.at[slot], device_id=owner, device_id_type=pl.DeviceIdType.MESH)

        def to_sibling(gi):
            if gi >= 2:
                d2d(gi - 2).wait_send()
            for cc in range(2):
                @pl.when(c == cc)
                def _(cc=cc):
                    stage_ref[gi % 2] = block_rows(2 * gi + 1 - cc)
            d2d(gi).start()

        def chip_sum(gi):
            d2d(gi).wait_recv()
            gx, gy = gi // 2, gi % 2
            for cc in range(2):
                @pl.when(c == cc)
                def _(cc=cc):
                    total = block_rows(2 * gi + cc) + sib_ref[gi]

                    @pl.when(mychip == gi)
                    def _():
                        mine_ref[...] = total

                    @pl.when(mychip != gi)
                    def _():
                        slot = jnp.where(x_ == gx, 0, 1) + 2 * jnp.where(y_ == gy, 0, 1) - 1
                        snd_ref[slot] = total.astype(snd_ref.dtype)
                        ici(slot, (gx, gy, c)).start()

        for gi in range(n_groups):
            @pl.when((g == gi) & (t == nt - 1))
            def _(gi=gi):
                to_sibling(gi)
                if gi == n_groups - 1:
                    for k in range(n_groups):
                        chip_sum(k)
                    total = mine_ref[...]
                    for slot in range(3):
                        ici(slot, sibling).wait_recv()
                        total = total + rcv_ref[slot].astype(F32)
                    mine_ref[...] = total
                    out = pltpu.make_async_copy(mine_ref, gin_ref, out_sem)
                    out.start()
                    for k in range(1, N_DEV):
                        peer, pidx = _peer(k, x_, y_, c)
                        for a in range(2):
                            small_copy(k, a, me, pidx, peer).wait_recv()
                    for k in range(1, N_DEV):
                        peer, pidx = _peer(k, x_, y_, c)
                        for a in range(2):
                            small_copy(k, a, pidx, me, peer).wait_send()
                    for a in range(2):
                        small_local(a).wait()
                    d2d(gi - 1).wait_send()
                    d2d(gi).wait_send()
                    for slot in range(3):
                        ici(slot, sibling).wait_send()
                    out.wait()

    def piece_spec(i, width):
        gi = next(k for k, members in enumerate(GW_GROUPS) if (i in members or (i == 2 and 1 in members)))
        return pl.BlockSpec((ts, width), lambda g, t: (jnp.where(g == gi, t, jnp.where(g < gi, 0, nt - 1)), 0))

    widths = [OFF[i + 1] - OFF[i] for i in range(9)]
    hbm = pl.BlockSpec(memory_space=pl.ANY)
    vmem = pl.BlockSpec(memory_space=pltpu.VMEM)
    rope_spec = pl.BlockSpec((ts, 128), lambda g, t: (jnp.where(g == 0, t, nt - 1), 0))
    return pl.pallas_call(
        body, name="in_proj_bwd_w", grid=(n_groups, nt),
        in_specs=[hbm] + [piece_spec(i, w) for i, w in enumerate(widths)] + [rope_spec] * 3 + [hbm] + [vmem] * 5,
        out_specs=[hbm, hbm, hbm],
        out_shape=[jax.ShapeDtypeStruct(blk, F32), jax.ShapeDtypeStruct((N_DEV,) + parts_wg.shape[1:], F32),
                   jax.ShapeDtypeStruct((N_DEV, SMALL_ROWS, 128), F32)],
        scratch_shapes=[pltpu.VMEM((D_IN_PROJ, D_MODEL), F32), pltpu.VMEM((2,) + blk, F32),
                        pltpu.VMEM((n_groups,) + blk, F32), pltpu.VMEM((3,) + blk, MXU_DTYPE),
                        pltpu.VMEM((3,) + blk, MXU_DTYPE), pltpu.VMEM(blk, F32), pltpu.VMEM((SMALL_ROWS, 128), F32),
                        pltpu.VMEM((s, D_MODEL), MXU_DTYPE),
                        pltpu.SemaphoreType.DMA((n_groups,)), pltpu.SemaphoreType.DMA((n_groups,)),
                        pltpu.SemaphoreType.DMA((3,)), pltpu.SemaphoreType.DMA((3,)), pltpu.SemaphoreType.DMA,
                        pltpu.SemaphoreType.DMA((2 * (N_DEV - 1),)), pltpu.SemaphoreType.DMA((2 * (N_DEV - 1),)),
                        pltpu.SemaphoreType.DMA((2,)), pltpu.SemaphoreType.DMA((nt,))],
        compiler_params=_cparams(dimension_semantics=("arbitrary", "arbitrary")),
    )(x, *pieces, *rope, parts_wg, g_ln, g_bg, g_nw, g_sinks, loss)


def _local_step(x, positions, w_in_t, wg_s, b_gate, sinks, norm_w, w_out_s, ln_g, ln_b, target):
    qa, k_pad, v_pad, ga, qb, kb, vb, gb, rb, la, oms, *rope, x16, w_in, wg, w_out = _in_proj(
        x, w_in_t, wg_s, b_gate, _rope_angles(positions), w_out_s)
    attn, cat_a = _swa_fwd(sinks, qa, k_pad, v_pad, ga)
    o, cat_b, sprev = _gla_fwd(qb, kb, vb, la, gb, norm_w)
    loss, gx0, d_cat_a, d_cat_b, g_w_out, g_ln = _out_ln_loss(cat_a, cat_b, w_out, x, target, ln_g, ln_b)
    parts_w_out = g_w_out.reshape(N_DEV, D_OUT_SHARD, D_MODEL)
    dqa, dga, dka, dva, g_sinks, g_out = _swa_bwd(sinks, qa, k_pad, v_pad, attn, ga, d_cat_a, rope, parts_w_out)
    dqb, dkb, dvb, dgb, drb, g_wg, g_bg, g_nw = _gla_bwd(qb, kb, vb, la, oms, gb, o, sprev, d_cat_b, rb, wg, norm_w)
    pieces = (dqa, dka, dva, dga, dqb, dkb, dvb, dgb, drb)
    grad_x = _in_proj_bwd_x(gx0, pieces, w_in, rope)
    parts_wg = jnp.transpose(g_wg.reshape(GLA_RANK, N_DEV, 32), (1, 0, 2))
    g_in, r_wg, r_small = _in_proj_bwd_w(x16, pieces, rope, parts_wg, g_ln, g_bg, g_nw, g_sinks, loss)
    return grad_x, g_in, g_out, r_wg, r_small


def _mesh_pos():
    return lax.axis_index("x"), lax.axis_index("y"), lax.axis_index("c")


def _peer(k, x, y, c):
    px = (1 - x) if k & 4 else x
    py = (1 - y) if k & 2 else y
    pc = (1 - c) if k & 1 else c
    return (px, py, pc), 4 * px + 2 * py + pc


def _other_chips(x, y):
    return [(1 - x, y), (x, 1 - y), (1 - x, 1 - y)]


def _shard_view(t):
    return jnp.transpose(t, (2, 0, 1))


class _BlockGather:
    def __init__(self, slots, send_sems, recv_sems):
        self.slots, self.send_sems, self.recv_sems = slots, send_sems, recv_sems
        x, y, c = _mesh_pos()
        self.c, self.me, self.sibling = c, 4 * x + 2 * y + c, (x, y, 1 - c)
        first = (jnp.where(c == 0, 1 - x, x), jnp.where(c == 0, y, 1 - y))
        second = (jnp.where(c == 0, x, 1 - x), jnp.where(c == 0, 1 - y, y))
        self.chips = [first, second, (1 - x, 1 - y)]

    @staticmethod
    def scratch():
        return [pltpu.SemaphoreType.DMA((N_DEV - 1,)), pltpu.SemaphoreType.DMA((N_DEV - 1,))]

    def _copy(self, k, block, to):
        return pltpu.make_async_remote_copy(
            src_ref=self.slots.at[block], dst_ref=self.slots.at[block], send_sem=self.send_sems.at[k],
            recv_sem=self.recv_sems.at[k], device_id=to, device_id_type=pl.DeviceIdType.MESH)

    def _block(self, j, c):
        cx, cy = self.chips[j]
        return 4 * cx + 2 * cy + c

    def _dev(self, j):
        return (*self.chips[j], self.c)

    def start(self):
        self._copy(1, self.me, self._dev(0)).start()
        self._copy(2, self.me, self._dev(1)).start()
        self._copy(0, self.me, self.sibling).start()

    def forward(self):
        self._copy(1, self._block(0, self.c), self.sibling).wait_recv()
        self._copy(3, self._block(0, self.c), self._dev(1)).start()
        self._copy(4, self._block(0, self.c), self.sibling).start()
        self._copy(2, self._block(1, self.c), self.sibling).wait_recv()
        self._copy(5, self._block(1, self.c), self.sibling).start()

    def forward_far(self):
        self._copy(3, self._block(2, self.c), self.sibling).wait_recv()
        self._copy(6, self._block(2, self.c), self.sibling).start()

    def finish(self):
        for k in (0, 4, 5, 6):
            self._copy(k, self.me, self.sibling).wait_recv()
        for k in range(N_DEV - 1):
            self._copy(k, self.me, self.sibling).wait_send()


class _OwnerSum:
    def __init__(self, parts, own, sib, snd, rcv, loc_sems, d2d_send, d2d_recv, ici_send, ici_recv):
        self.parts, self.own, self.sib, self.snd, self.rcv = parts, own, sib, snd, rcv
        self.sems = (loc_sems, d2d_send, d2d_recv, ici_send, ici_recv)
        x, y, c = _mesh_pos()
        self.c, self.sibling = c, (x, y, 1 - c)
        self.chips = [(x, y)] + _other_chips(x, y)

    @staticmethod
    def scratch(block):
        return [pltpu.VMEM((4,) + block, F32), pltpu.VMEM((4,) + block, F32),
                pltpu.VMEM((3,) + block, MXU_DTYPE), pltpu.VMEM((3,) + block, MXU_DTYPE),
                pltpu.SemaphoreType.DMA((4,)), pltpu.SemaphoreType.DMA((4,)), pltpu.SemaphoreType.DMA((4,)),
                pltpu.SemaphoreType.DMA((3,)), pltpu.SemaphoreType.DMA((3,))]

    def _local(self, r):
        cx, cy = self.chips[r]
        return pltpu.make_async_copy(self.parts.at[4 * cx + 2 * cy + self.c], self.own.at[r], self.sems[0].at[r])

    def _d2d(self, r):
        cx, cy = self.chips[r]
        return pltpu.make_async_remote_copy(
            src_ref=self.parts.at[4 * cx + 2 * cy + (1 - self.c)], dst_ref=self.sib.at[r], send_sem=self.sems[1].at[r],
            recv_sem=self.sems[2].at[r], device_id=self.sibling, device_id_type=pl.DeviceIdType.MESH)

    def _ici(self, r):
        cx, cy = self.chips[r]
        return pltpu.make_async_remote_copy(
            src_ref=self.snd.at[r - 1], dst_ref=self.rcv.at[r - 1], send_sem=self.sems[3].at[r - 1],
            recv_sem=self.sems[4].at[r - 1], device_id=(cx, cy, self.c), device_id_type=pl.DeviceIdType.MESH)

    def start(self):
        for r in (1, 2, 3, 0):
            self._local(r).start()
            self._d2d(r).start()

    def forward(self):
        for r in (1, 2, 3):
            self._local(r).wait()
            self._d2d(r).wait_recv()
            self.snd[r - 1] = (self.own[r] + self.sib[r]).astype(self.snd.dtype)
            self._ici(r).start()

    def finish(self):
        self._local(0).wait()
        self._d2d(0).wait_recv()
        acc = self.own[0] + self.sib[0]
        for r in (1, 2, 3):
            self._ici(r).wait_recv()
            acc = acc + self.rcv[r - 1].astype(F32)
        for r in range(4):
            self._d2d(r).wait_send()
        for r in (1, 2, 3):
            self._ici(r).wait_send()
        return acc


SMALL_ROWS = 48


def _adamw_math(g, w, m, v):
    nm = ADAM_B1 * m + (1.0 - ADAM_B1) * g
    nv = ADAM_B2 * v + (1.0 - ADAM_B2) * (g * g)
    m_hat = nm / (1.0 - ADAM_B1 ** ADAM_STEP)
    v_hat = nv / (1.0 - ADAM_B2 ** ADAM_STEP)
    return -ADAM_LR * (m_hat / (jnp.sqrt(v_hat) + ADAM_EPS) + ADAM_WD * w), nm, nv


def _adamw_shard_view(g, w, m, v):
    rows, width = g.shape
    parts = 3
    pr = rows // parts
    assert pr * parts == rows

    def body(g_ref, w_hbm, m_hbm, v_hbm, g_out, d_out, nm_out, nv_out, bufs, outs, sems):
        def load(k, i, src):
            return pltpu.make_async_copy(src.at[pl.ds(pr * k, pr), 0, :], bufs.at[k, i], sems.at[7 * k + i])

        def store(k, i, dst):
            return pltpu.make_async_copy(outs.at[k, i], dst.at[pl.ds(pr * k, pr), 0, :], sems.at[7 * k + 3 + i])

        for k in range(parts):
            for i, src in enumerate((w_hbm, m_hbm, v_hbm)):
                load(k, i, src).start()
        for k in range(parts):
            for i, src in enumerate((w_hbm, m_hbm, v_hbm)):
                load(k, i, src).wait()
            gk = g_ref[pr * k:pr * (k + 1), :]
            outs[k, 0] = gk
            outs[k, 1], outs[k, 2], outs[k, 3] = _adamw_math(gk, bufs[k, 0], bufs[k, 1], bufs[k, 2])
            for i, dst in enumerate((g_out, d_out, nm_out, nv_out)):
                store(k, i, dst).start()
        for k in range(parts):
            for i, dst in enumerate((g_out, d_out, nm_out, nv_out)):
                store(k, i, dst).wait()

    hbm = pl.BlockSpec(memory_space=pl.ANY)
    return pl.pallas_call(
        body, name="adamw_w_in",
        in_specs=[pl.BlockSpec(memory_space=pltpu.VMEM), hbm, hbm, hbm], out_specs=[hbm] * 4,
        out_shape=[jax.ShapeDtypeStruct((rows, 1, width), F32)] * 4,
        scratch_shapes=[pltpu.VMEM((parts, 3, pr, width), F32), pltpu.VMEM((parts, 4, pr, width), F32),
                        pltpu.SemaphoreType.DMA((7 * parts,))],
        compiler_params=_cparams(),
    )(g, w, m, v)


def _adamw_vectors(r_small, r_wg, g_out, params):
    n_par = len(params)

    def body(rsm_ref, rwg_ref, gout_ref, *refs):
        ins, outs = refs[:3 * n_par], refs[3 * n_par:]
        g = rsm_ref[0]
        gwg = rwg_ref[0]
        for j in range(1, N_DEV):
            g = g + rsm_ref[j]
            gwg = gwg + rwg_ref[j]
        outs[4 * n_par][...] = g[40:41]
        grads = [gwg,
                 jnp.concatenate([g[r:r + 1] for r in range(0, 8)], axis=1),
                 jnp.concatenate([g[r:r + 1] for r in range(8, 16)], axis=1),
                 jnp.concatenate([g[16:17], g[17:18]], axis=1),
                 g[24:25],
                 g[32:33, 0:SWA_Q_HEADS],
                 gout_ref[...]]
        for p, gp in enumerate(grads):
            w_ref, m_ref, v_ref = ins[3 * p:3 * p + 3]
            outs[4 * p][...] = gp
            outs[4 * p + 1][...], outs[4 * p + 2][...], outs[4 * p + 3][...] = _adamw_math(
                gp, w_ref[...], m_ref[...], v_ref[...])

    vmem = pl.BlockSpec(memory_space=pltpu.VMEM)
    flat = [t for wmv in params for t in wmv]
    return pl.pallas_call(
        body, name="adamw_vectors",
        in_specs=[vmem] * (3 + len(flat)), out_specs=[vmem] * (4 * n_par + 1),
        out_shape=[jax.ShapeDtypeStruct(wmv[0].shape, F32) for wmv in params for _ in range(4)]
        + [jax.ShapeDtypeStruct((1, 128), F32)],
        compiler_params=_cparams(),
    )(r_small, r_wg, g_out, *flat)


def kernel(x, positions, w_in, gla_w_gate_up, gla_b_gate, attn_sinks, gla_norm_w, w_out, ln_g, ln_b, loss_target, m_w_in, m_gla_w_gate_up, m_gla_b_gate, m_attn_sinks, m_gla_norm_w, m_w_out, m_ln_g, m_ln_b, v_w_in, v_gla_w_gate_up, v_gla_b_gate, v_attn_sinks, v_gla_norm_w, v_w_out, v_ln_g, v_ln_b):
    grad_x, g_in, g_out, r_wg, r_small = _local_step(
        x[0], positions[0], _shard_view(w_in), gla_w_gate_up[0], gla_b_gate, attn_sinks[0], gla_norm_w, w_out[0],
        ln_g, ln_b, loss_target[0])

    upd_in = _adamw_shard_view(g_in, _shard_view(w_in), _shard_view(m_w_in), _shard_view(v_w_in))
    upd_in = [jnp.transpose(t, (1, 2, 0)) for t in upd_in]
    vec = _adamw_vectors(r_small, r_wg, g_out, [
        (gla_w_gate_up[0], m_gla_w_gate_up[0], v_gla_w_gate_up[0]), (ln_g, m_ln_g, v_ln_g), (ln_b, m_ln_b, v_ln_b),
        (gla_b_gate, m_gla_b_gate, v_gla_b_gate), (gla_norm_w, m_gla_norm_w, v_gla_norm_w),
        (attn_sinks, m_attn_sinks, v_attn_sinks), (w_out[0], m_w_out[0], v_w_out[0])])

    outs = [vec[28][0, 0], grad_x[None]]
    for kind in range(4):
        u_wg, u_ln_g, u_ln_b, u_bg, u_nw, u_sinks, u_out = (vec[4 * p + kind] for p in range(7))
        outs += [upd_in[kind], u_wg[None], u_bg, u_sinks, u_nw, u_out[None], u_ln_g, u_ln_b]
    return tuple(outs)
```

```python
import jax
import jax.numpy as jnp
from jax import lax
from jax.experimental import pallas as pl
from jax.experimental.pallas import tpu as pltpu

F32 = jnp.float32
MXU_DTYPE = jnp.bfloat16

N_DEV = 8
D_MODEL = 1024
SWA_Q_HEADS = 8
SWA_KV_HEADS = 2
SWA_GROUP = 4
SWA_HEAD_DIM = 64
BLOCK = 128
ROPE_THETA = 500000.0
ROT_DIM = 16
GLA_HEADS = 4
GLA_DK = 64
GLA_DV = 128
GLA_RANK = 16
GLA_TAU = 16.0
GLA_CHUNK = 64
D_IN_PROJ = 2832
D_IN_SHARD = D_IN_PROJ // N_DEV
D_OUT_SHARD = D_MODEL // N_DEV
OFF = (0, 512, 640, 768, 1280, 1536, 1792, 2304, 2816, 2832)
EPS = 1e-5
ALPHA = 2.0 ** 0.25
SWA_SCALE = SWA_HEAD_DIM ** -0.5
GLA_SCALE = GLA_DK ** -0.5
ADAM_LR = 0.001
ADAM_B1 = 0.9
ADAM_B2 = 0.999
ADAM_EPS = 1e-08
ADAM_WD = 0.01
ADAM_STEP = 10
VMEM_LIMIT = 56 * 1024 * 1024

_NT = (((1,), (1,)), ((), ()))
_TN = (((0,), (0,)), ((), ()))


def _mm(a, b):
    return jnp.dot(a, b, preferred_element_type=F32)


def _mm_nt(a, b):
    return lax.dot_general(a, b, _NT, preferred_element_type=F32)


def _mm_tn(a, b):
    return lax.dot_general(a, b, _TN, preferred_element_type=F32)


def _sigmoid(t):
    return 1.0 / (1.0 + jnp.exp(-t))


def _cparams(**kw):
    return pltpu.CompilerParams(vmem_limit_bytes=VMEM_LIMIT, **kw)


def _full(shape):
    return pl.BlockSpec(shape, lambda *_: (0,) * len(shape))


def _rows(tile, width):
    return pl.BlockSpec((tile, width), lambda i: (i, 0))


def _rope_angles(positions):
    half = ROT_DIM // 2
    inv_freq = ROPE_THETA ** (-jnp.arange(half, dtype=F32) / half)
    ang = positions.astype(F32)[:, None] * inv_freq[None, :]
    return jnp.concatenate([jnp.cos(ang), jnp.sin(ang)], axis=1)


def _split3_parts(t):
    hi = t.astype(MXU_DTYPE)
    r1 = t - hi.astype(F32)
    mid = r1.astype(MXU_DTYPE)
    return hi, mid, (r1 - mid.astype(F32)).astype(MXU_DTYPE)


def _rope_tables(cs):
    half = ROT_DIM // 2
    i = lax.broadcasted_iota(jnp.int32, (2 * half, 3 * 128), 0)
    lane = lax.broadcasted_iota(jnp.int32, (2 * half, 3 * 128), 1)
    table, pos = _idiv(lane, 128), lane & (SWA_HEAD_DIM - 1)
    is_c = (table == 0) & (pos < ROT_DIM) & ((pos & (half - 1)) == i)
    is_s1 = (table == 1) & (pos < half) & (pos + half == i)
    is_s2 = (table == 2) & (pos >= half) & (pos < ROT_DIM) & (pos == i)
    sel = jnp.where(is_c | is_s2, 1.0, jnp.where(is_s1, -1.0, 0.0)).astype(MXU_DTYPE)
    hi, mid, lo = _split3_parts(cs)
    t = (_mm(hi, sel) + _mm(mid, sel)) + _mm(lo, sel)
    pos1 = lax.broadcasted_iota(jnp.int32, (1, 128), 1) & (SWA_HEAD_DIM - 1)
    return t[:, 0:128] + jnp.where(pos1 >= ROT_DIM, 1.0, 0.0), t[:, 128:256], t[:, 256:384]


def _rope(t, c, s1, s2):
    return t * c + pltpu.roll(t, 120, 1) * s1 + pltpu.roll(t, 8, 1) * s2


def _rope_t(g, c, s1, s2):
    return g * c + pltpu.roll(g * s1, 8, 1) + pltpu.roll(g * s2, 120, 1)


def _in_proj(x, w_in_t, wg_s, b_gate, cos_sin, w_out_s):
    s = x.shape[0]
    ts = min(512, s)
    nsteps = s // ts
    forward_step, far_step = min(3, nsteps - 1), min(5, nsteps - 1)
    widths = [OFF[i + 1] - OFF[i] for i in range(9)]

    def body(x_ref, win_hbm, wgs_ref, bg_ref, cs_ref, wos_ref,
             qa_ref, ka_ref, va_ref, ga_ref, qb_ref, kb_ref, vb_ref, gb_ref, rb_ref, la_ref, oms_ref,
             c_ref, s1_ref, s2_ref, x16_ref, w_ref, wg_ref, wout_ref,
             win_all, wg_all, wout_all, stage, stage_sem, *sems):
        xb = x_ref[...].astype(MXU_DTYPE)
        x16_ref[...] = xb
        c, s1, s2 = _rope_tables(cs_ref[...])
        c_ref[...], s1_ref[...], s2_ref[...] = c, s1, s2
        i0 = pl.program_id(0)
        gather = _BlockGather(wout_all, *sems[0:2])

        @pl.when(i0 == 0)
        def _():
            ka_ref[0:BLOCK, :] = jnp.zeros((BLOCK, 128), ka_ref.dtype)
            va_ref[0:BLOCK, :] = jnp.zeros((BLOCK, 128), va_ref.dtype)
            first = (_BlockGather(win_all, *sems[2:4]), _BlockGather(wg_all, *sems[4:6]))
            load = pltpu.make_async_copy(win_hbm.at[:, 0, :], stage, stage_sem)
            load.start()
            wout_all[gather.me] = wos_ref[...].astype(wout_all.dtype)
            wg_all[gather.me] = wgs_ref[...].astype(wg_all.dtype)
            load.wait()
            win_all[gather.me] = stage[...].astype(win_all.dtype)
            for stage_of in ("start", "forward", "forward_far", "finish"):
                for g in first:
                    getattr(g, stage_of)()
            gather.start()
            for j in range(N_DEV):
                w_ref[D_IN_SHARD * j:D_IN_SHARD * (j + 1), :] = win_all[j]
                wg_ref[:, 32 * j:32 * (j + 1)] = wg_all[j]

        @pl.when(i0 == forward_step)
        def _():
            gather.forward()

        @pl.when(i0 == far_step)
        def _():
            gather.forward_far()

        @pl.when(i0 == nsteps - 1)
        def _():
            gather.finish()
            for j in range(N_DEV):
                wout_ref[D_OUT_SHARD * j:D_OUT_SHARD * (j + 1), :] = wout_all[j]

        kv_rows = pl.ds(pl.multiple_of(BLOCK + i0 * ts, BLOCK), ts)

        def cols(i):
            return _mm_nt(xb, w_ref[OFF[i]:OFF[i + 1], :])

        qa = cols(0)
        for i in range(4):
            qa_ref[:, 128 * i:128 * (i + 1)] = _rope(qa[:, 128 * i:128 * (i + 1)], c, s1, s2).astype(qa_ref.dtype)
        kv = _mm_nt(xb, w_ref[OFF[1]:OFF[3], :])
        ka_ref[kv_rows, :] = _rope(kv[:, 0:128], c, s1, s2).astype(ka_ref.dtype)
        va_ref[kv_rows, :] = kv[:, 128:256].astype(va_ref.dtype)
        ga_ref[...] = cols(3)
        qb_ref[...] = cols(4)
        kb_ref[...] = cols(5)
        vb_ref[...] = cols(6).astype(vb_ref.dtype)
        gb_ref[...] = cols(7)
        rb = cols(8)
        rb_ref[...] = rb
        logit = _mm(rb.astype(MXU_DTYPE), wg_ref[...]) + bg_ref[...]
        e = jnp.exp(-jnp.abs(logit))
        la_ref[...] = (jnp.minimum(logit, 0.0) - jnp.log(1.0 + e)) / GLA_TAU
        oms_ref[...] = jnp.where(logit >= 0.0, e, 1.0) / (1.0 + e)

    out_shape = [jax.ShapeDtypeStruct((s + BLOCK if i in (1, 2) else s, w), MXU_DTYPE if i in (0, 1, 2, 6) else F32)
                 for i, w in enumerate(widths)]
    out_shape += [jax.ShapeDtypeStruct((s, 256), F32)] * 2 + [jax.ShapeDtypeStruct((s, 128), F32)] * 3
    out_shape += [jax.ShapeDtypeStruct((s, D_MODEL), MXU_DTYPE)]
    out_shape += [jax.ShapeDtypeStruct((D_IN_PROJ, D_MODEL), MXU_DTYPE), jax.ShapeDtypeStruct((GLA_RANK, 256), MXU_DTYPE),
                  jax.ShapeDtypeStruct((D_MODEL, D_MODEL), MXU_DTYPE)]
    return pl.pallas_call(
        body, name="in_proj", grid=(nsteps,),
        in_specs=[_rows(ts, D_MODEL), pl.BlockSpec(memory_space=pl.ANY), _full((GLA_RANK, 32)), _full((1, 256)),
                  _rows(ts, ROT_DIM), _full((D_OUT_SHARD, D_MODEL))],
        out_specs=[_full((s + BLOCK, w)) if i in (1, 2) else _rows(ts, w) for i, w in enumerate(widths)]
        + [_rows(ts, 256)] * 2 + [_rows(ts, 128)] * 3 + [_rows(ts, D_MODEL)]
        + [_full((D_IN_PROJ, D_MODEL)), _full((GLA_RANK, 256)), _full((D_MODEL, D_MODEL))],
        out_shape=out_shape,
        scratch_shapes=[pltpu.VMEM((N_DEV, D_IN_SHARD, D_MODEL), MXU_DTYPE), pltpu.VMEM((N_DEV, GLA_RANK, 32), MXU_DTYPE),
                        pltpu.VMEM((N_DEV, D_OUT_SHARD, D_MODEL), MXU_DTYPE),
                        pltpu.VMEM((D_IN_SHARD, D_MODEL), F32), pltpu.SemaphoreType.DMA]
        + 3 * _BlockGather.scratch(),
        compiler_params=_cparams(dimension_semantics=("arbitrary",)),
    )(x, w_in_t, wg_s, b_gate, cos_sin, w_out_s)


SWA_ROWS = SWA_GROUP * BLOCK


def _swa_bias():
    shape = (2, 2 * BLOCK, SWA_ROWS)
    ki = lax.broadcasted_iota(jnp.int32, shape, 1)
    qi = lax.broadcasted_iota(jnp.int32, shape, 2) & (BLOCK - 1)
    first = lax.broadcasted_iota(jnp.int32, shape, 0) == 0
    dist = qi + BLOCK - ki
    ok = (dist >= 0) & (dist < BLOCK) & (jnp.logical_not(first) | (ki >= BLOCK))
    return jnp.where(ok, 0.0, -jnp.inf).astype(F32)


SWA_SUB = 8


def _swa_bias_of(bias_ref, n, b):
    return bias_ref[jnp.minimum(n, 1)] if b == 0 else bias_ref[1]


def _swa_dup(t, j):
    t = t.astype(F32)
    low = lax.broadcasted_iota(jnp.int32, t.shape, 1) < SWA_HEAD_DIM
    keep = low if j == 0 else jnp.logical_not(low)
    return jnp.where(keep, t, pltpu.roll(t, SWA_HEAD_DIM, 1)).astype(MXU_DTYPE)


def _swa_stack(t, j):
    low = lax.broadcasted_iota(jnp.int32, (BLOCK, 128), 1) < SWA_HEAD_DIM
    zero = jnp.zeros((BLOCK, 128), t.dtype)
    blocks = []
    for p in (2 * j, 2 * j + 1):
        tp = t[:, 128 * p:128 * (p + 1)]
        blocks += [jnp.where(low, tp, zero), jnp.where(low, zero, tp)]
    return jnp.concatenate(blocks, axis=0)


def _swa_unstack(t):
    low = lax.broadcasted_iota(jnp.int32, (BLOCK, 128), 1) < SWA_HEAD_DIM
    return [jnp.where(low, t[2 * BLOCK * i:2 * BLOCK * i + BLOCK], t[2 * BLOCK * i + BLOCK:2 * BLOCK * (i + 1)])
            for i in range(2)]


def _swa_sink_row(sink_ref, j):
    lane = lax.broadcasted_iota(jnp.int32, (1, SWA_ROWS), 1)
    row = jnp.full((1, SWA_ROWS), sink_ref[SWA_GROUP * j], F32)
    for r in range(1, SWA_GROUP):
        row = jnp.where(lane >= BLOCK * r, sink_ref[SWA_GROUP * j + r], row)
    return row


def _split3(t):
    return jnp.concatenate(_split3_parts(t), axis=1)


def _row_sums_as_row(t):
    ones = jnp.ones((8, 3 * t.shape[1]), MXU_DTYPE)
    return _mm_nt(ones, _split3(t))[0:1, :]


def _swa_probs_t(qs, kd, bias_t, sink):
    sc = _mm_nt(kd, qs) + bias_t
    m = jnp.maximum(jnp.max(sc, axis=0, keepdims=True), sink)
    p = jnp.exp(sc - m)
    ps = jnp.exp(sink - m)
    rinv = 1.0 / (jnp.sum(p, axis=0, keepdims=True) + ps)
    return p * rinv, ps * rinv


def _swa_fwd(sinks, qa, k_pad, v_pad, ga):
    s = qa.shape[0]
    sub = min(SWA_SUB, s // BLOCK)
    tq = sub * BLOCK

    def body(sink_ref, qa_ref, ga_ref, bias_ref, k_ref, v_ref, attn_ref, cat_ref):
        n = pl.program_id(0)
        for b in range(sub):
            rows = slice(BLOCK * b, BLOCK * (b + 1))
            start = pl.multiple_of((n * sub + b) * BLOCK, BLOCK)
            kw = k_ref[pl.ds(start, 2 * BLOCK), :]
            vw = v_ref[pl.ds(start, 2 * BLOCK), :]
            bias_t = _swa_bias_of(bias_ref, n, b)
            q = qa_ref[rows, :] * SWA_SCALE
            g = ga_ref[rows, :]
            silu = g * _sigmoid(g)
            for j in range(SWA_KV_HEADS):
                qs = _swa_stack(q, j).astype(MXU_DTYPE)
                probs, _ = _swa_probs_t(qs, _swa_dup(kw, j), bias_t, _swa_sink_row(sink_ref, j))
                pairs = _swa_unstack(_mm_tn(probs.astype(MXU_DTYPE), _swa_dup(vw, j)))
                for i in range(2):
                    lanes = slice(128 * (2 * j + i), 128 * (2 * j + i + 1))
                    attn_ref[rows, lanes] = pairs[i]
                    cat_ref[rows, lanes] = (pairs[i] * silu[:, lanes]).astype(cat_ref.dtype)

    return pl.pallas_call(
        body, name="swa_fwd", grid=(s // tq,),
        in_specs=[pl.BlockSpec(memory_space=pltpu.SMEM), _rows(tq, 512), _rows(tq, 512),
                  _full((2, 2 * BLOCK, SWA_ROWS)), _full((s + BLOCK, 128)), _full((s + BLOCK, 128))],
        out_specs=[_rows(tq, 512), _rows(tq, 512)],
        out_shape=[jax.ShapeDtypeStruct((s, 512), F32), jax.ShapeDtypeStruct((s, 512), MXU_DTYPE)],
        compiler_params=_cparams(dimension_semantics=("arbitrary",)),
    )(sinks, qa, ga, _swa_bias(), k_pad, v_pad)


GLA_KW = GLA_HEADS * GLA_DK
GLA_VW = GLA_HEADS * GLA_DV


def _idiv(t, d):
    return t >> (d.bit_length() - 1)


def _chunk_cumsum(t, lower):
    n, w = t.shape
    r = lax.broadcasted_iota(jnp.int32, (n, n), 0)
    c = lax.broadcasted_iota(jnp.int32, (n, n), 1)
    tri = ((_idiv(r, GLA_CHUNK) == _idiv(c, GLA_CHUNK)) & ((r >= c) if lower else (r <= c))).astype(MXU_DTYPE)
    parts = _mm(tri, _split3(t))
    return (parts[:, :w] + parts[:, w:2 * w]) + parts[:, 2 * w:]


def _chunk_last(t):
    n = t.shape[0]
    return jnp.concatenate(
        [jnp.broadcast_to(t[c + GLA_CHUNK - 1:c + GLA_CHUNK, :], (GLA_CHUNK, t.shape[1]))
         for c in range(0, n, GLA_CHUNK)], axis=0)


def _head_stack(t, width):
    head = _idiv(lax.broadcasted_iota(jnp.int32, t.shape, 1), width)
    zero = jnp.zeros_like(t)
    return jnp.concatenate([jnp.where(head == h, t, zero) for h in range(GLA_HEADS)], axis=0)


def _heads_to_rows(t):
    return jnp.concatenate([t[:, GLA_DV * h:GLA_DV * (h + 1)] for h in range(GLA_HEADS)], axis=0)


def _rows_to_heads(t):
    return jnp.concatenate([t[GLA_CHUNK * h:GLA_CHUNK * (h + 1)] for h in range(GLA_HEADS)], axis=1)


def _state_by_head(t):
    srow = _idiv(lax.broadcasted_iota(jnp.int32, (GLA_VW, GLA_KW), 0), GLA_DV)
    slane = _idiv(lax.broadcasted_iota(jnp.int32, (GLA_VW, GLA_KW), 1), GLA_DK)
    return jnp.where(srow == slane, jnp.concatenate([t] * GLA_HEADS, axis=0), jnp.zeros((GLA_VW, GLA_KW), t.dtype))


def _gla_masks():
    row = lax.broadcasted_iota(jnp.int32, (GLA_CHUNK, GLA_KW), 0)
    pos = lax.broadcasted_iota(jnp.int32, (GLA_CHUNK, GLA_KW), 1) & (GLA_CHUNK - 1)
    return pos <= row, pos >= row


def _gla_fwd(qb, kb, vb, la, gb, norm_w):
    s = qb.shape[0]
    tb = min(256, s)
    ch = tb // GLA_CHUNK

    def body(qb_ref, kb_ref, vb_ref, la_ref, gb_ref, nw_ref, o_ref, cat_ref, sp_ref, st_ref):
        @pl.when(pl.program_id(0) == 0)
        def _():
            st_ref[...] = jnp.zeros_like(st_ref)

        causal, _ = _gla_masks()
        nw = nw_ref[...]
        b = _chunk_cumsum(la_ref[...], True)
        bl = _chunk_last(b)
        k = kb_ref[...]
        qd = ((qb_ref[...] * GLA_SCALE) * jnp.exp(b)).astype(MXU_DTYPE)
        ki = (k * jnp.exp(-b)).astype(MXU_DTYPE)
        ke = (k * jnp.exp(bl - b)).astype(MXU_DTYPE)
        dec = jnp.exp(bl)
        v = vb_ref[...].astype(MXU_DTYPE)
        g = gb_ref[...]
        silu = g * _sigmoid(g)
        for ci in range(ch):
            rows = slice(GLA_CHUNK * ci, GLA_CHUNK * (ci + 1))
            qds, kis, kes = (_head_stack(t[rows], GLA_DK) for t in (qd, ki, ke))
            a = jnp.where(causal, _mm_nt(qd[rows], kis), 0.0).astype(MXU_DTYPE)
            st = st_ref[...]
            sp_ref[ci] = st
            o = _mm(a, _head_stack(v[rows], GLA_DV)) + _rows_to_heads(_mm_nt(qds, st.astype(MXU_DTYPE)))
            st_ref[...] = st * dec[rows][0:1] + _mm_tn(_heads_to_rows(v[rows]), kes)
            o_ref[rows, :] = o
            for h in range(GLA_HEADS):
                lv = slice(GLA_DV * h, GLA_DV * (h + 1))
                oh = o[:, lv]
                r = lax.rsqrt(jnp.mean(oh * oh, axis=1, keepdims=True) + EPS)
                cat_ref[rows, lv] = (oh * r * nw * silu[rows, lv]).astype(cat_ref.dtype)

    return pl.pallas_call(
        body, name="gla_fwd", grid=(s // tb,),
        in_specs=[_rows(tb, 256), _rows(tb, 256), _rows(tb, 512), _rows(tb, 256), _rows(tb, 512), _full((1, 128))],
        out_specs=[_rows(tb, 512), _rows(tb, 512), pl.BlockSpec((ch, GLA_DV, 256), lambda i: (i, 0, 0))],
        out_shape=[jax.ShapeDtypeStruct((s, 512), F32), jax.ShapeDtypeStruct((s, 512), MXU_DTYPE),
                   jax.ShapeDtypeStruct((s // GLA_CHUNK, GLA_DV, 256), F32)],
        scratch_shapes=[pltpu.VMEM((GLA_DV, GLA_KW), F32)],
        compiler_params=_cparams(dimension_semantics=("arbitrary",)),
    )(qb, kb, vb, la, gb, norm_w)


def _out_ln_loss(cat_a, cat_b, w_out, x, target, ln_g, ln_b):
    s = x.shape[0]
    ts = min(512, s)
    halves = 2 if ts % 32 == 0 else 1
    th = ts // halves

    def body(ca_ref, cb_ref, w_ref, x_ref, t_ref, g_ref, b_ref,
             loss_ref, gx_ref, da_ref, db_ref, gw_ref, gln_ref):
        @pl.when(pl.program_id(0) == 0)
        def _():
            loss_ref[...] = jnp.zeros_like(loss_ref)
            gw_ref[...] = jnp.zeros_like(gw_ref)
            gln_ref[...] = jnp.zeros_like(gln_ref)

        g = g_ref[...]
        dh16s = []
        for k in range(halves):
            rows = slice(th * k, th * (k + 1))
            mix = _mm(ca_ref[rows, :], w_ref[0:512, :]) + _mm(cb_ref[rows, :], w_ref[512:1024, :])
            h = ALPHA * x_ref[rows, :] + mix
            mu = jnp.mean(h, axis=1, keepdims=True)
            hc = h - mu
            rstd = lax.rsqrt(jnp.mean(hc * hc, axis=1, keepdims=True) + EPS)
            xhat = hc * rstd
            err = xhat * g + b_ref[...] - t_ref[rows, :]
            loss_ref[...] += 0.5 * jnp.sum(jnp.mean(err * err, axis=1, keepdims=True))
            dy = err * (1.0 / D_MODEL)
            gln_ref[0:1, :] += jnp.sum(dy * xhat, axis=0, keepdims=True)
            gln_ref[1:2, :] += jnp.sum(dy, axis=0, keepdims=True)
            dxh = dy * g
            dh = rstd * (dxh - jnp.mean(dxh, axis=1, keepdims=True)
                         - xhat * jnp.mean(dxh * xhat, axis=1, keepdims=True))
            gx_ref[rows, :] = ALPHA * dh
            dh16s.append(dh.astype(MXU_DTYPE))
        for k in range(halves):
            rows = slice(th * k, th * (k + 1))
            da_ref[rows, :] = _mm_nt(dh16s[k], w_ref[0:512, :])
            db_ref[rows, :] = _mm_nt(dh16s[k], w_ref[512:1024, :])
        dh16 = jnp.concatenate(dh16s, axis=0)
        gw_ref[0:512, :] += _mm_tn(ca_ref[...], dh16)
        gw_ref[512:1024, :] += _mm_tn(cb_ref[...], dh16)

    return pl.pallas_call(
        body, name="out_ln_loss", grid=(s // ts,),
        in_specs=[_rows(ts, 512), _rows(ts, 512), _full((D_MODEL, D_MODEL)), _rows(ts, D_MODEL), _rows(ts, D_MODEL),
                  _full((1, D_MODEL)), _full((1, D_MODEL))],
        out_specs=[_full((1, 128)), _rows(ts, D_MODEL), _rows(ts, 512), _rows(ts, 512),
                   _full((D_MODEL, D_MODEL)), _full((2, D_MODEL))],
        out_shape=[jax.ShapeDtypeStruct((1, 128), F32), jax.ShapeDtypeStruct((s, D_MODEL), F32),
                   jax.ShapeDtypeStruct((s, 512), F32), jax.ShapeDtypeStruct((s, 512), F32),
                   jax.ShapeDtypeStruct((D_MODEL, D_MODEL), F32), jax.ShapeDtypeStruct((2, D_MODEL), F32)],
        compiler_params=_cparams(dimension_semantics=("arbitrary",)),
    )(cat_a, cat_b, w_out, x, target, ln_g, ln_b)


def _swa_bwd(sinks, qa, k_pad, v_pad, attn, ga, d_cat_a, rope, parts_w_out):
    s = qa.shape[0]
    sub = min(SWA_SUB, s // BLOCK)
    tq = sub * BLOCK
    nsteps = s // tq
    forward_step = min(1, nsteps - 1)

    def body(sink_ref, qa_ref, ga_ref, at_ref, dc_ref, c_ref, s1_ref, s2_ref, bias_ref, k_ref, v_ref, pout_ref,
             dq_ref, dg_ref, dk_out, dv_out, ds_ref, gout_ref, dk_ref, dv_ref, *scratch):
        n = pl.program_id(0)
        owner_sum = _OwnerSum(pout_ref, *scratch)

        @pl.when(n == 0)
        def _():
            dk_ref[...] = jnp.zeros_like(dk_ref)
            dv_ref[...] = jnp.zeros_like(dv_ref)
            ds_ref[...] = jnp.zeros_like(ds_ref)
            owner_sum.start()

        @pl.when(n == forward_step)
        def _():
            owner_sum.forward()

        @pl.when(n == nsteps - 1)
        def _():
            gout_ref[...] = owner_sum.finish()

        low = lax.broadcasted_iota(jnp.int32, (2 * BLOCK, 128), 1) < SWA_HEAD_DIM
        for b in range(sub):
            rows = slice(BLOCK * b, BLOCK * (b + 1))
            start = pl.multiple_of((n * sub + b) * BLOCK, BLOCK)
            kw = k_ref[pl.ds(start, 2 * BLOCK), :]
            vw = v_ref[pl.ds(start, 2 * BLOCK), :]
            bias_t = _swa_bias_of(bias_ref, n, b)
            q = qa_ref[rows, :] * SWA_SCALE
            g = ga_ref[rows, :]
            sg = _sigmoid(g)
            o = at_ref[rows, :]
            dc = dc_ref[rows, :]
            do = dc * (g * sg)
            dg_ref[rows, :] = (dc * o * (sg * (1.0 + g * (1.0 - sg)))).astype(dg_ref.dtype)
            od = do * o
            c, s1, s2 = c_ref[rows, :], s1_ref[rows, :], s2_ref[rows, :]
            dk, dv = [], []
            for j in range(SWA_KV_HEADS):
                kd, vd = _swa_dup(kw, j), _swa_dup(vw, j)
                qs = _swa_stack(q, j).astype(MXU_DTYPE)
                dos = _swa_stack(do, j).astype(MXU_DTYPE)
                probs, psink = _swa_probs_t(qs, kd, bias_t, _swa_sink_row(sink_ref, j))
                delta = _row_sums_as_row(_swa_stack(od, j))
                dsc = (probs * (_mm_nt(vd, dos) - delta)).astype(MXU_DTYPE)
                dsink = psink * delta
                for r in range(SWA_GROUP):
                    h = SWA_GROUP * j + r
                    ds_ref[h:h + 1, :] += jnp.zeros((1, 128), F32) - jnp.sum(dsink[:, BLOCK * r:BLOCK * (r + 1)])
                dq = _swa_unstack(_mm_tn(dsc, kd))
                for i in range(2):
                    lanes = slice(128 * (2 * j + i), 128 * (2 * j + i + 1))
                    dq_ref[rows, lanes] = _rope_t(dq[i] * SWA_SCALE, c, s1, s2).astype(dq_ref.dtype)
                dkj = _mm(dsc, qs)
                dvj = _mm(probs.astype(MXU_DTYPE), dos)
                dk.append(dkj + pltpu.roll(dkj, SWA_HEAD_DIM, 1))
                dv.append(dvj + pltpu.roll(dvj, SWA_HEAD_DIM, 1))
            dk_ref[pl.ds(start, 2 * BLOCK), :] += jnp.where(low, dk[0], dk[1])
            dv_ref[pl.ds(start, 2 * BLOCK), :] += jnp.where(low, dv[0], dv[1])

        @pl.when(n == nsteps - 1)
        def _():
            dk_out[...] = dk_ref[BLOCK:, :]
            dv_out[...] = dv_ref[BLOCK:, :]

    out_blk = parts_w_out.shape[1:]
    return pl.pallas_call(
        body, name="swa_bwd", grid=(nsteps,),
        in_specs=[pl.BlockSpec(memory_space=pltpu.SMEM)] + [_rows(tq, 512)] * 4 + [_rows(tq, 128)] * 3
        + [_full((2, 2 * BLOCK, SWA_ROWS))] + [_full((s + BLOCK, 128))] * 2 + [pl.BlockSpec(memory_space=pl.ANY)],
        out_specs=[_rows(tq, 512), _rows(tq, 512), _full((s, 128)), _full((s, 128)),
                   _full((SWA_Q_HEADS, 128)), _full(out_blk)],
        out_shape=[jax.ShapeDtypeStruct((s, 512), MXU_DTYPE), jax.ShapeDtypeStruct((s, 512), MXU_DTYPE),
                   jax.ShapeDtypeStruct((s, 128), F32), jax.ShapeDtypeStruct((s, 128), F32),
                   jax.ShapeDtypeStruct((SWA_Q_HEADS, 128), F32), jax.ShapeDtypeStruct(out_blk, F32)],
        scratch_shapes=[pltpu.VMEM((s + BLOCK, 128), F32)] * 2 + _OwnerSum.scratch(out_blk),
        compiler_params=_cparams(dimension_semantics=("arbitrary",)),
    )(sinks, qa, ga, attn, d_cat_a, *rope, _swa_bias(), k_pad, v_pad, parts_w_out)


def _gla_bwd(qb, kb, vb, la, oms, gb, o, sprev, d_cat_b, rb, wg, norm_w):
    s = qb.shape[0]
    tb = min(512, s)
    ch = tb // GLA_CHUNK
    nb = s // tb

    def body(qb_ref, kb_ref, vb_ref, la_ref, oms_ref, gb_ref, o_ref, sp_ref, dc_ref, rb_ref, wg_ref, nw_ref,
             dq_ref, dk_ref, dv_ref, dg_ref, dr_ref, gwg_ref, gbg_ref, gnw_ref, dst_ref):
        @pl.when(pl.program_id(0) == 0)
        def _():
            dst_ref[...] = jnp.zeros_like(dst_ref)
            gwg_ref[...] = jnp.zeros_like(gwg_ref)
            gbg_ref[...] = jnp.zeros_like(gbg_ref)
            gnw_ref[...] = jnp.zeros_like(gnw_ref)

        causal, causal_t = _gla_masks()
        nw = nw_ref[...]
        b = _chunk_cumsum(la_ref[...], True)
        bl = _chunk_last(b)
        eb, enb, ee, dec = jnp.exp(b), jnp.exp(-b), jnp.exp(bl - b), jnp.exp(bl)
        k = kb_ref[...]
        qd = (qb_ref[...] * GLA_SCALE) * eb
        ki = k * enb
        ke = k * ee
        qd16, ki16, ke16 = qd.astype(MXU_DTYPE), ki.astype(MXU_DTYPE), ke.astype(MXU_DTYPE)
        v16 = vb_ref[...].astype(MXU_DTYPE)

        g = gb_ref[...]
        sg = _sigmoid(g)
        silu = g * sg
        dsilu = sg * (1.0 + g * (1.0 - sg))
        gnw = jnp.zeros((1, GLA_DV), F32)
        do = []
        for h in range(GLA_HEADS):
            lv = slice(GLA_DV * h, GLA_DV * (h + 1))
            oh = o_ref[:, lv]
            dch = dc_ref[:, lv]
            r = lax.rsqrt(jnp.mean(oh * oh, axis=1, keepdims=True) + EPS)
            d_on = dch * silu[:, lv]
            dg_ref[:, lv] = (dch * (oh * r * nw) * dsilu[:, lv]).astype(dg_ref.dtype)
            gnw += jnp.sum(d_on * oh * r, axis=0, keepdims=True)
            u = d_on * nw
            do.append(r * u - oh * (r * r * r) * jnp.mean(u * oh, axis=1, keepdims=True))
        gnw_ref[...] += gnw
        do16 = jnp.concatenate(do, axis=1).astype(MXU_DTYPE)

        db, dbl = [None] * ch, [None] * ch
        for ci in reversed(range(ch)):
            rows = slice(GLA_CHUNK * ci, GLA_CHUNK * (ci + 1))
            qds, kis, kes = (_head_stack(t[rows], GLA_DK) for t in (qd16, ki16, ke16))
            vs, dos = _head_stack(v16[rows], GLA_DV), _head_stack(do16[rows], GLA_DV)
            a = jnp.where(causal, _mm_nt(qd16[rows], kis), 0.0).astype(MXU_DTYPE)
            at = jnp.where(causal_t, _mm_nt(ki16[rows], qds), 0.0).astype(MXU_DTYPE)
            da = jnp.where(causal, _mm_nt(do16[rows], vs), 0.0).astype(MXU_DTYPE)
            dat = jnp.where(causal_t, _mm_nt(v16[rows], dos), 0.0).astype(MXU_DTYPE)
            st = sp_ref[ci]
            dst = dst_ref[...]
            dst16 = dst.astype(MXU_DTYPE)
            dv = _mm(at, dos) + _rows_to_heads(_mm_nt(kes, dst16))
            dqd = _mm(da, kis) + _mm(do16[rows], _state_by_head(st.astype(MXU_DTYPE)))
            dki = _mm(dat, qds)
            dke = _mm(v16[rows], _state_by_head(dst16))
            ddec = jnp.sum(dst * st, axis=0, keepdims=True)
            decc = dec[rows][0:1]
            dst_ref[...] = _mm_tn(_heads_to_rows(do16[rows]), qds) + dst * decc
            dq_ref[rows, :] = (dqd * eb[rows] * GLA_SCALE).astype(dq_ref.dtype)
            dk_ref[rows, :] = (dki * enb[rows] + dke * ee[rows]).astype(dk_ref.dtype)
            dv_ref[rows, :] = dv.astype(dv_ref.dtype)
            dke_ke = dke * ke[rows]
            db[ci] = dqd * qd[rows] - dki * ki[rows] - dke_ke
            dbl[ci] = jnp.broadcast_to(jnp.sum(dke_ke, axis=0, keepdims=True) + ddec * decc, (GLA_CHUNK, GLA_KW))

        dla = _chunk_cumsum(jnp.concatenate(db, axis=0), False) + jnp.concatenate(dbl, axis=0)
        dlogit = dla * oms_ref[...] * (1.0 / GLA_TAU)
        dl16 = dlogit.astype(MXU_DTYPE)
        gbg_ref[...] += jnp.sum(dlogit, axis=0, keepdims=True)
        gwg_ref[...] += _mm_tn(rb_ref[...].astype(MXU_DTYPE), dl16)
        dr_ref[...] = _mm_nt(dl16, wg_ref[...]).astype(dr_ref.dtype)

    def rev(width):
        return pl.BlockSpec((tb, width), lambda i: (nb - 1 - i, 0))

    return pl.pallas_call(
        body, name="gla_bwd", grid=(nb,),
        in_specs=[rev(256), rev(256), rev(512), rev(256), rev(256), rev(512), rev(512),
                  pl.BlockSpec((ch, GLA_DV, 256), lambda i: (nb - 1 - i, 0, 0)), rev(512), rev(GLA_RANK),
                  _full((GLA_RANK, 256)), _full((1, 128))],
        out_specs=[rev(256), rev(256), rev(512), rev(512), rev(GLA_RANK),
                   _full((GLA_RANK, 256)), _full((1, 256)), _full((1, 128))],
        out_shape=[jax.ShapeDtypeStruct((s, 256), MXU_DTYPE), jax.ShapeDtypeStruct((s, 256), MXU_DTYPE),
                   jax.ShapeDtypeStruct((s, 512), MXU_DTYPE), jax.ShapeDtypeStruct((s, 512), MXU_DTYPE),
                   jax.ShapeDtypeStruct((s, GLA_RANK), MXU_DTYPE), jax.ShapeDtypeStruct((GLA_RANK, 256), F32),
                   jax.ShapeDtypeStruct((1, 256), F32), jax.ShapeDtypeStruct((1, 128), F32)],
        scratch_shapes=[pltpu.VMEM((GLA_DV, GLA_KW), F32)],
        compiler_params=_cparams(dimension_semantics=("arbitrary",)),
    )(qb, kb, vb, la, oms, gb, o, sprev, d_cat_b, rb, wg, norm_w)


def _dproj_tiles(piece_refs, rope_refs, members=(0, 1, 3, 4, 5, 6, 7, 8)):
    for i in members:
        if i == 1:
            dk = _rope_t(piece_refs[1][...], *(r[...] for r in rope_refs))
            yield OFF[1], OFF[3], jnp.concatenate([dk, piece_refs[2][...]], axis=1).astype(MXU_DTYPE)
        else:
            yield OFF[i], OFF[i + 1], piece_refs[i][...].astype(MXU_DTYPE)


def _in_proj_bwd_x(gx0, pieces, w_in, rope):
    s = gx0.shape[0]
    ts = min(512, s)
    widths = [OFF[i + 1] - OFF[i] for i in range(9)]

    def body(gx0_ref, *refs):
        w_ref, gx_ref = refs[12:]
        acc = gx0_ref[...]
        for lo, hi, t16 in _dproj_tiles(refs[:9], refs[9:12]):
            acc += _mm(t16, w_ref[lo:hi, :])
        gx_ref[...] = acc

    return pl.pallas_call(
        body, name="in_proj_bwd_x", grid=(s // ts,),
        in_specs=[_rows(ts, D_MODEL)] + [_rows(ts, w) for w in widths] + [_rows(ts, 128)] * 3
        + [_full((D_IN_PROJ, D_MODEL))],
        out_specs=_rows(ts, D_MODEL),
        out_shape=jax.ShapeDtypeStruct((s, D_MODEL), F32),
        compiler_params=_cparams(dimension_semantics=("arbitrary",)),
    )(gx0, *pieces, *rope, w_in)


GW_GROUPS = ((0, 1), (3, 4), (5, 6), (7, 8))


def _in_proj_bwd_w(x, pieces, rope, parts_wg, g_ln, g_bg, g_nw, g_sinks, loss):
    s = x.shape[0]
    ts = min(1024, s)
    nt = s // ts
    n_groups = len(GW_GROUPS)
    ends = [OFF[3] if members[-1] == 1 else OFF[members[-1] + 1] for members in GW_GROUPS]
    assert all(ends[g] >= D_IN_SHARD * 2 * (g + 1) for g in range(n_groups))
    blk = (D_IN_SHARD, D_MODEL)

    def body(x_hbm, *refs):
        piece_refs, rope_refs = refs[:9], refs[9:12]
        pwg_ref, gln_ref, gbg_ref, gnw_ref, gsk_ref, loss_ref, gin_ref, rwg_ref, rsm_ref = refs[12:21]
        (acc_ref, stage_ref, sib_ref, snd_ref, rcv_ref, mine_ref, sm_ref, xs_ref,
         d2d_send, d2d_recv, ici_send, ici_recv, out_sem, sm_send, sm_recv, sm_loc, x_sems) = refs[21:]
        g, t = pl.program_id(0), pl.program_id(1)
        tile = pl.ds(pl.multiple_of(t * ts, ts), ts)

        def x_load(k):
            rows = pl.ds(pl.multiple_of(k * ts, ts), ts)
            return pltpu.make_async_copy(x_hbm.at[rows, :], xs_ref.at[rows, :], x_sems.at[k])
        x_, y_, c = _mesh_pos()
        me, mychip, sibling = 4 * x_ + 2 * y_ + c, 2 * x_ + y_, (x_, y_, 1 - c)
        small_dsts = (rwg_ref, rsm_ref)

        def small_src(a, block):
            return pwg_ref.at[block] if a == 0 else sm_ref

        def small_copy(k, a, src_block, dst_block, peer):
            i = 2 * (k - 1) + a
            return pltpu.make_async_remote_copy(
                src_ref=small_src(a, src_block), dst_ref=small_dsts[a].at[dst_block], send_sem=sm_send.at[i],
                recv_sem=sm_recv.at[i], device_id=peer, device_id_type=pl.DeviceIdType.MESH)

        def small_local(a):
            return pltpu.make_async_copy(small_src(a, me), small_dsts[a].at[me], sm_loc.at[a])

        @pl.when((g == 0) & (t == 0))
        def _():
            for k in range(nt):
                x_load(k).start()
            acc_ref[...] = jnp.zeros_like(acc_ref)
            sm_ref[...] = jnp.zeros_like(sm_ref)
            for r in range(D_MODEL // 128):
                sm_ref[r:r + 1, :] = gln_ref[0:1, 128 * r:128 * (r + 1)]
                sm_ref[8 + r:9 + r, :] = gln_ref[1:2, 128 * r:128 * (r + 1)]
            for r in range(2):
                sm_ref[16 + r:17 + r, :] = gbg_ref[0:1, 128 * r:128 * (r + 1)]
            sm_ref[24:25, :] = gnw_ref[...]
            diag = (lax.broadcasted_iota(jnp.int32, gsk_ref.shape, 0)
                    == lax.broadcasted_iota(jnp.int32, gsk_ref.shape, 1))
            sm_ref[32:33, :] = jnp.sum(jnp.where(diag, gsk_ref[...], 0.0), axis=0, keepdims=True)
            sm_ref[40:41, :] = loss_ref[...]
            for a in range(2):
                small_local(a).start()
            for k in range(1, N_DEV):
                peer, pidx = _peer(k, x_, y_, c)
                for a in range(2):
                    small_copy(k, a, pidx, me, peer).start()

        @pl.when(g == 0)
        def _():
            x_load(t).wait()

        xb = xs_ref[tile, :]
        for gi, members in enumerate(GW_GROUPS):
            @pl.when(g == gi)
            def _(members=members):
                for lo, hi, t16 in _dproj_tiles(piece_refs, rope_refs, members):
                    acc_ref[lo:hi, :] += _mm_tn(t16, xb)

        def block_rows(j):
            return acc_ref[D_IN_SHARD * j:D_IN_SHARD * (j + 1), :]

        def d2d(gi):
            return pltpu.make_async_remote_copy(
                src_ref=stage_ref.at[gi % 2], dst_ref=sib_ref.at[gi], send_sem=d2d_send.at[gi],
                recv_sem=d2d_recv.at[gi], device_id=sibling, device_id_type=pl.DeviceIdType.MESH)

        def ici(slot, owner):
            return pltpu.make_async_remote_copy(
                src_ref=snd_ref.at[slot], dst_ref=rcv_ref.at[slot], send_sem=ici_send.at[slot],
                recv_sem=ici_recv.at[slot], device_id=owner, device_id_type=pl.DeviceIdType.MESH)

        def to_sibling(gi):
            if gi >= 2:
                d2d(gi - 2).wait_send()
            for cc in range(2):
                @pl.when(c == cc)
                def _(cc=cc):
                    stage_ref[gi % 2] = block_rows(2 * gi + 1 - cc)
            d2d(gi).start()

        def chip_sum(gi):
            d2d(gi).wait_recv()
            for cc in range(2):
                @pl.when(c == cc)
                def _(cc=cc):
                    sib_ref[gi] = block_rows(2 * gi + cc) + sib_ref[gi]

        def chip_sums_to_owners():
            first = (jnp.where(c == 0, 1 - x_, x_), jnp.where(c == 0, y_, 1 - y_))
            second = (jnp.where(c == 0, x_, 1 - x_), jnp.where(c == 0, 1 - y_, y_))

            def chip_of(p):
                return 2 * p[0] + p[1]

            snd_ref[2] = sib_ref[chip_of((1 - x_, 1 - y_))].astype(snd_ref.dtype)
            ici(2, (*first, c)).start()
            snd_ref[0] = sib_ref[chip_of(first)].astype(snd_ref.dtype)
            ici(0, (*first, c)).start()
            ici(2, sibling).wait_recv()
            snd_ref[1] = (sib_ref[chip_of(second)] + rcv_ref[2].astype(F32)).astype(snd_ref.dtype)
            ici(1, (*second, c)).start()
            total = sib_ref[mychip]
            for slot in range(2):
                ici(slot, sibling).wait_recv()
                total = total + rcv_ref[slot].astype(F32)
            return total

        for gi in range(n_groups):
            @pl.when((g == gi) & (t == nt - 1))
            def _(gi=gi):
                to_sibling(gi)
                if gi == n_groups - 1:
                    for k in range(n_groups):
                        chip_sum(k)
                    mine_ref[...] = chip_sums_to_owners()
                    out = pltpu.make_async_copy(mine_ref, gin_ref, out_sem)
                    out.start()
                    for k in range(1, N_DEV):
                        peer, pidx = _peer(k, x_, y_, c)
                        for a in range(2):
                            small_copy(k, a, me, pidx, peer).wait_recv()
                    for k in range(1, N_DEV):
                        peer, pidx = _peer(k, x_, y_, c)
                        for a in range(2):
                            small_copy(k, a, pidx, me, peer).wait_send()
                    for a in range(2):
                        small_local(a).wait()
                    d2d(gi - 1).wait_send()
                    d2d(gi).wait_send()
                    for slot in range(3):
                        ici(slot, sibling).wait_send()
                    out.wait()

    def piece_spec(i, width):
        gi = next(k for k, members in enumerate(GW_GROUPS) if (i in members or (i == 2 and 1 in members)))
        return pl.BlockSpec((ts, width), lambda g, t: (jnp.where(g == gi, t, jnp.where(g < gi, 0, nt - 1)), 0))

    widths = [OFF[i + 1] - OFF[i] for i in range(9)]
    hbm = pl.BlockSpec(memory_space=pl.ANY)
    vmem = pl.BlockSpec(memory_space=pltpu.VMEM)
    rope_spec = pl.BlockSpec((ts, 128), lambda g, t: (jnp.where(g == 0, t, nt - 1), 0))
    return pl.pallas_call(
        body, name="in_proj_bwd_w", grid=(n_groups, nt),
        in_specs=[hbm] + [piece_spec(i, w) for i, w in enumerate(widths)] + [rope_spec] * 3 + [hbm] + [vmem] * 5,
        out_specs=[hbm, hbm, hbm],
        out_shape=[jax.ShapeDtypeStruct(blk, F32), jax.ShapeDtypeStruct((N_DEV,) + parts_wg.shape[1:], F32),
                   jax.ShapeDtypeStruct((N_DEV, SMALL_ROWS, 128), F32)],
        scratch_shapes=[pltpu.VMEM((D_IN_PROJ, D_MODEL), F32), pltpu.VMEM((2,) + blk, F32),
                        pltpu.VMEM((n_groups,) + blk, F32), pltpu.VMEM((3,) + blk, MXU_DTYPE),
                        pltpu.VMEM((3,) + blk, MXU_DTYPE), pltpu.VMEM(blk, F32), pltpu.VMEM((SMALL_ROWS, 128), F32),
                        pltpu.VMEM((s, D_MODEL), MXU_DTYPE),
                        pltpu.SemaphoreType.DMA((n_groups,)), pltpu.SemaphoreType.DMA((n_groups,)),
                        pltpu.SemaphoreType.DMA((3,)), pltpu.SemaphoreType.DMA((3,)), pltpu.SemaphoreType.DMA,
                        pltpu.SemaphoreType.DMA((2 * (N_DEV - 1),)), pltpu.SemaphoreType.DMA((2 * (N_DEV - 1),)),
                        pltpu.SemaphoreType.DMA((2,)), pltpu.SemaphoreType.DMA((nt,))],
        compiler_params=_cparams(dimension_semantics=("arbitrary", "arbitrary")),
    )(x, *pieces, *rope, parts_wg, g_ln, g_bg, g_nw, g_sinks, loss)


def _local_step(x, positions, w_in_t, wg_s, b_gate, sinks, norm_w, w_out_s, ln_g, ln_b, target):
    qa, k_pad, v_pad, ga, qb, kb, vb, gb, rb, la, oms, *rope, x16, w_in, wg, w_out = _in_proj(
        x, w_in_t, wg_s, b_gate, _rope_angles(positions), w_out_s)
    attn, cat_a = _swa_fwd(sinks, qa, k_pad, v_pad, ga)
    o, cat_b, sprev = _gla_fwd(qb, kb, vb, la, gb, norm_w)
    loss, gx0, d_cat_a, d_cat_b, g_w_out, g_ln = _out_ln_loss(cat_a, cat_b, w_out, x, target, ln_g, ln_b)
    parts_w_out = g_w_out.reshape(N_DEV, D_OUT_SHARD, D_MODEL)
    dqa, dga, dka, dva, g_sinks, g_out = _swa_bwd(sinks, qa, k_pad, v_pad, attn, ga, d_cat_a, rope, parts_w_out)
    dqb, dkb, dvb, dgb, drb, g_wg, g_bg, g_nw = _gla_bwd(qb, kb, vb, la, oms, gb, o, sprev, d_cat_b, rb, wg, norm_w)
    pieces = (dqa, dka, dva, dga, dqb, dkb, dvb, dgb, drb)
    grad_x = _in_proj_bwd_x(gx0, pieces, w_in, rope)
    parts_wg = jnp.transpose(g_wg.reshape(GLA_RANK, N_DEV, 32), (1, 0, 2))
    g_in, r_wg, r_small = _in_proj_bwd_w(x16, pieces, rope, parts_wg, g_ln, g_bg, g_nw, g_sinks, loss)
    return grad_x, g_in, g_out, r_wg, r_small


def _mesh_pos():
    return lax.axis_index("x"), lax.axis_index("y"), lax.axis_index("c")


def _peer(k, x, y, c):
    px = (1 - x) if k & 4 else x
    py = (1 - y) if k & 2 else y
    pc = (1 - c) if k & 1 else c
    return (px, py, pc), 4 * px + 2 * py + pc


def _other_chips(x, y):
    return [(1 - x, y), (x, 1 - y), (1 - x, 1 - y)]


def _shard_view(t):
    return jnp.transpose(t, (2, 0, 1))


class _BlockGather:
    def __init__(self, slots, send_sems, recv_sems):
        self.slots, self.send_sems, self.recv_sems = slots, send_sems, recv_sems
        x, y, c = _mesh_pos()
        self.c, self.me, self.sibling = c, 4 * x + 2 * y + c, (x, y, 1 - c)
        first = (jnp.where(c == 0, 1 - x, x), jnp.where(c == 0, y, 1 - y))
        second = (jnp.where(c == 0, x, 1 - x), jnp.where(c == 0, 1 - y, y))
        self.chips = [first, second, (1 - x, 1 - y)]

    @staticmethod
    def scratch():
        return [pltpu.SemaphoreType.DMA((N_DEV - 1,)), pltpu.SemaphoreType.DMA((N_DEV - 1,))]

    def _copy(self, k, block, to):
        return pltpu.make_async_remote_copy(
            src_ref=self.slots.at[block], dst_ref=self.slots.at[block], send_sem=self.send_sems.at[k],
            recv_sem=self.recv_sems.at[k], device_id=to, device_id_type=pl.DeviceIdType.MESH)

    def _block(self, j, c):
        cx, cy = self.chips[j]
        return 4 * cx + 2 * cy + c

    def _dev(self, j):
        return (*self.chips[j], self.c)

    def start(self):
        self._copy(1, self.me, self._dev(0)).start()
        self._copy(2, self.me, self._dev(1)).start()
        self._copy(0, self.me, self.sibling).start()

    def forward(self):
        self._copy(1, self._block(0, self.c), self.sibling).wait_recv()
        self._copy(3, self._block(0, self.c), self._dev(1)).start()
        self._copy(4, self._block(0, self.c), self.sibling).start()
        self._copy(2, self._block(1, self.c), self.sibling).wait_recv()
        self._copy(5, self._block(1, self.c), self.sibling).start()

    def forward_far(self):
        self._copy(3, self._block(2, self.c), self.sibling).wait_recv()
        self._copy(6, self._block(2, self.c), self.sibling).start()

    def finish(self):
        for k in (0, 4, 5, 6):
            self._copy(k, self.me, self.sibling).wait_recv()
        for k in range(N_DEV - 1):
            self._copy(k, self.me, self.sibling).wait_send()


class _OwnerSum:
    def __init__(self, parts, own, sib, snd, rcv, loc_sems, d2d_send, d2d_recv, ici_send, ici_recv):
        self.parts, self.own, self.sib, self.snd, self.rcv = parts, own, sib, snd, rcv
        self.sems = (loc_sems, d2d_send, d2d_recv, ici_send, ici_recv)
        x, y, c = _mesh_pos()
        self.c, self.sibling = c, (x, y, 1 - c)
        self.chips = [(x, y)] + _other_chips(x, y)

    @staticmethod
    def scratch(block):
        return [pltpu.VMEM((4,) + block, F32), pltpu.VMEM((4,) + block, F32),
                pltpu.VMEM((3,) + block, MXU_DTYPE), pltpu.VMEM((3,) + block, MXU_DTYPE),
                pltpu.SemaphoreType.DMA((4,)), pltpu.SemaphoreType.DMA((4,)), pltpu.SemaphoreType.DMA((4,)),
                pltpu.SemaphoreType.DMA((3,)), pltpu.SemaphoreType.DMA((3,))]

    def _local(self, r):
        cx, cy = self.chips[r]
        return pltpu.make_async_copy(self.parts.at[4 * cx + 2 * cy + self.c], self.own.at[r], self.sems[0].at[r])

    def _d2d(self, r):
        cx, cy = self.chips[r]
        return pltpu.make_async_remote_copy(
            src_ref=self.parts.at[4 * cx + 2 * cy + (1 - self.c)], dst_ref=self.sib.at[r], send_sem=self.sems[1].at[r],
            recv_sem=self.sems[2].at[r], device_id=self.sibling, device_id_type=pl.DeviceIdType.MESH)

    def _ici(self, r):
        cx, cy = self.chips[r]
        return pltpu.make_async_remote_copy(
            src_ref=self.snd.at[r - 1], dst_ref=self.rcv.at[r - 1], send_sem=self.sems[3].at[r - 1],
            recv_sem=self.sems[4].at[r - 1], device_id=(cx, cy, self.c), device_id_type=pl.DeviceIdType.MESH)

    def start(self):
        for r in (1, 2, 3, 0):
            self._local(r).start()
            self._d2d(r).start()

    def forward(self):
        for r in (1, 2, 3):
            self._local(r).wait()
            self._d2d(r).wait_recv()
            self.snd[r - 1] = (self.own[r] + self.sib[r]).astype(self.snd.dtype)
            self._ici(r).start()

    def finish(self):
        self._local(0).wait()
        self._d2d(0).wait_recv()
        acc = self.own[0] + self.sib[0]
        for r in (1, 2, 3):
            self._ici(r).wait_recv()
            acc = acc + self.rcv[r - 1].astype(F32)
        for r in range(4):
            self._d2d(r).wait_send()
        for r in (1, 2, 3):
            self._ici(r).wait_send()
        return acc


SMALL_ROWS = 48


def _adamw_math(g, w, m, v):
    nm = ADAM_B1 * m + (1.0 - ADAM_B1) * g
    nv = ADAM_B2 * v + (1.0 - ADAM_B2) * (g * g)
    m_hat = nm / (1.0 - ADAM_B1 ** ADAM_STEP)
    v_hat = nv / (1.0 - ADAM_B2 ** ADAM_STEP)
    return -ADAM_LR * (m_hat / (jnp.sqrt(v_hat) + ADAM_EPS) + ADAM_WD * w), nm, nv


def _adamw_shard_view(g, w, m, v):
    rows, width = g.shape
    parts = 3
    pr = rows // parts
    assert pr * parts == rows

    def body(g_ref, w_hbm, m_hbm, v_hbm, g_out, d_out, nm_out, nv_out, bufs, outs, sems):
        def load(k, i, src):
            return pltpu.make_async_copy(src.at[pl.ds(pr * k, pr), 0, :], bufs.at[k, i], sems.at[7 * k + i])

        def store(k, i, dst):
            return pltpu.make_async_copy(outs.at[k, i], dst.at[pl.ds(pr * k, pr), 0, :], sems.at[7 * k + 3 + i])

        for k in range(parts):
            for i, src in enumerate((w_hbm, m_hbm, v_hbm)):
                load(k, i, src).start()
        for k in range(parts):
            for i, src in enumerate((w_hbm, m_hbm, v_hbm)):
                load(k, i, src).wait()
            gk = g_ref[pr * k:pr * (k + 1), :]
            outs[k, 0] = gk
            outs[k, 1], outs[k, 2], outs[k, 3] = _adamw_math(gk, bufs[k, 0], bufs[k, 1], bufs[k, 2])
            for i, dst in enumerate((g_out, d_out, nm_out, nv_out)):
                store(k, i, dst).start()
        for k in range(parts):
            for i, dst in enumerate((g_out, d_out, nm_out, nv_out)):
                store(k, i, dst).wait()

    hbm = pl.BlockSpec(memory_space=pl.ANY)
    return pl.pallas_call(
        body, name="adamw_w_in",
        in_specs=[pl.BlockSpec(memory_space=pltpu.VMEM), hbm, hbm, hbm], out_specs=[hbm] * 4,
        out_shape=[jax.ShapeDtypeStruct((rows, 1, width), F32)] * 4,
        scratch_shapes=[pltpu.VMEM((parts, 3, pr, width), F32), pltpu.VMEM((parts, 4, pr, width), F32),
                        pltpu.SemaphoreType.DMA((7 * parts,))],
        compiler_params=_cparams(),
    )(g, w, m, v)


def _adamw_vectors(r_small, r_wg, g_out, params):
    n_par = len(params)

    def body(rsm_ref, rwg_ref, gout_ref, *refs):
        ins, outs = refs[:3 * n_par], refs[3 * n_par:]
        g = rsm_ref[0]
        gwg = rwg_ref[0]
        for j in range(1, N_DEV):
            g = g + rsm_ref[j]
            gwg = gwg + rwg_ref[j]
        outs[4 * n_par][...] = g[40:41]
        grads = [gwg,
                 jnp.concatenate([g[r:r + 1] for r in range(0, 8)], axis=1),
                 jnp.concatenate([g[r:r + 1] for r in range(8, 16)], axis=1),
                 jnp.concatenate([g[16:17], g[17:18]], axis=1),
                 g[24:25],
                 g[32:33, 0:SWA_Q_HEADS],
                 gout_ref[...]]
        for p, gp in enumerate(grads):
            w_ref, m_ref, v_ref = ins[3 * p:3 * p + 3]
            outs[4 * p][...] = gp
            outs[4 * p + 1][...], outs[4 * p + 2][...], outs[4 * p + 3][...] = _adamw_math(
                gp, w_ref[...], m_ref[...], v_ref[...])

    vmem = pl.BlockSpec(memory_space=pltpu.VMEM)
    flat = [t for wmv in params for t in wmv]
    return pl.pallas_call(
        body, name="adamw_vectors",
        in_specs=[vmem] * (3 + len(flat)), out_specs=[vmem] * (4 * n_par + 1),
        out_shape=[jax.ShapeDtypeStruct(wmv[0].shape, F32) for wmv in params for _ in range(4)]
        + [jax.ShapeDtypeStruct((1, 128), F32)],
        compiler_params=_cparams(),
    )(r_small, r_wg, g_out, *flat)


def kernel(x, positions, w_in, gla_w_gate_up, gla_b_gate, attn_sinks, gla_norm_w, w_out, ln_g, ln_b, loss_target, m_w_in, m_gla_w_gate_up, m_gla_b_gate, m_attn_sinks, m_gla_norm_w, m_w_out, m_ln_g, m_ln_b, v_w_in, v_gla_w_gate_up, v_gla_b_gate, v_attn_sinks, v_gla_norm_w, v_w_out, v_ln_g, v_ln_b):
    grad_x, g_in, g_out, r_wg, r_small = _local_step(
        x[0], positions[0], _shard_view(w_in), gla_w_gate_up[0], gla_b_gate, attn_sinks[0], gla_norm_w, w_out[0],
        ln_g, ln_b, loss_target[0])

    upd_in = _adamw_shard_view(g_in, _shard_view(w_in), _shard_view(m_w_in), _shard_view(v_w_in))
    upd_in = [jnp.transpose(t, (1, 2, 0)) for t in upd_in]
    vec = _adamw_vectors(r_small, r_wg, g_out, [
        (gla_w_gate_up[0], m_gla_w_gate_up[0], v_gla_w_gate_up[0]), (ln_g, m_ln_g, v_ln_g), (ln_b, m_ln_b, v_ln_b),
        (gla_b_gate, m_gla_b_gate, v_gla_b_gate), (gla_norm_w, m_gla_norm_w, v_gla_norm_w),
        (attn_sinks, m_attn_sinks, v_attn_sinks), (w_out[0], m_w_out[0], v_w_out[0])])

    outs = [vec[28][0, 0], grad_x[None]]
    for kind in range(4):
        u_wg, u_ln_g, u_ln_b, u_bg, u_nw, u_sinks, u_out = (vec[4 * p + kind] for p in range(7))
        outs += [upd_in[kind], u_wg[None], u_bg, u_sinks, u_nw, u_out[None], u_ln_g, u_ln_b]
    return tuple(outs)
```

```python
import jax
import jax.numpy as jnp
from jax import lax
from jax.experimental import pallas as pl
from jax.experimental.pallas import tpu as pltpu

F32 = jnp.float32
MXU_DTYPE = jnp.bfloat16

N_DEV = 8
D_MODEL = 1024
SWA_Q_HEADS = 8
SWA_KV_HEADS = 2
SWA_GROUP = 4
SWA_HEAD_DIM = 64
BLOCK = 128
ROPE_THETA = 500000.0
ROT_DIM = 16
GLA_HEADS = 4
GLA_DK = 64
GLA_DV = 128
GLA_RANK = 16
GLA_TAU = 16.0
GLA_CHUNK = 64
D_IN_PROJ = 2832
D_IN_SHARD = D_IN_PROJ // N_DEV
D_OUT_SHARD = D_MODEL // N_DEV
OFF = (0, 512, 640, 768, 1280, 1536, 1792, 2304, 2816, 2832)
EPS = 1e-5
ALPHA = 2.0 ** 0.25
SWA_SCALE = SWA_HEAD_DIM ** -0.5
GLA_SCALE = GLA_DK ** -0.5
ADAM_LR = 0.001
ADAM_B1 = 0.9
ADAM_B2 = 0.999
ADAM_EPS = 1e-08
ADAM_WD = 0.01
ADAM_STEP = 10
VMEM_LIMIT = 56 * 1024 * 1024

_NT = (((1,), (1,)), ((), ()))
_TN = (((0,), (0,)), ((), ()))


def _mm(a, b):
    return jnp.dot(a, b, preferred_element_type=F32)


def _mm_nt(a, b):
    return lax.dot_general(a, b, _NT, preferred_element_type=F32)


def _mm_tn(a, b):
    return lax.dot_general(a, b, _TN, preferred_element_type=F32)


def _sigmoid(t):
    return 1.0 / (1.0 + jnp.exp(-t))


def _cparams(**kw):
    return pltpu.CompilerParams(vmem_limit_bytes=VMEM_LIMIT, **kw)


def _full(shape):
    return pl.BlockSpec(shape, lambda *_: (0,) * len(shape))


def _rows(tile, width):
    return pl.BlockSpec((tile, width), lambda i: (i, 0))


def _rope_angles(positions):
    half = ROT_DIM // 2
    inv_freq = ROPE_THETA ** (-jnp.arange(half, dtype=F32) / half)
    ang = positions.astype(F32)[:, None] * inv_freq[None, :]
    return jnp.concatenate([jnp.cos(ang), jnp.sin(ang)], axis=1)


def _split3_parts(t):
    hi = t.astype(MXU_DTYPE)
    r1 = t - hi.astype(F32)
    mid = r1.astype(MXU_DTYPE)
    return hi, mid, (r1 - mid.astype(F32)).astype(MXU_DTYPE)


def _rope_tables(cs):
    half = ROT_DIM // 2
    i = lax.broadcasted_iota(jnp.int32, (2 * half, 3 * 128), 0)
    lane = lax.broadcasted_iota(jnp.int32, (2 * half, 3 * 128), 1)
    table, pos = _idiv(lane, 128), lane & (SWA_HEAD_DIM - 1)
    is_c = (table == 0) & (pos < ROT_DIM) & ((pos & (half - 1)) == i)
    is_s1 = (table == 1) & (pos < half) & (pos + half == i)
    is_s2 = (table == 2) & (pos >= half) & (pos < ROT_DIM) & (pos == i)
    sel = jnp.where(is_c | is_s2, 1.0, jnp.where(is_s1, -1.0, 0.0)).astype(MXU_DTYPE)
    hi, mid, lo = _split3_parts(cs)
    t = (_mm(hi, sel) + _mm(mid, sel)) + _mm(lo, sel)
    pos1 = lax.broadcasted_iota(jnp.int32, (1, 128), 1) & (SWA_HEAD_DIM - 1)
    return t[:, 0:128] + jnp.where(pos1 >= ROT_DIM, 1.0, 0.0), t[:, 128:256], t[:, 256:384]


def _rope(t, c, s1, s2):
    return t * c + pltpu.roll(t, 120, 1) * s1 + pltpu.roll(t, 8, 1) * s2


def _rope_t(g, c, s1, s2):
    return g * c + pltpu.roll(g * s1, 8, 1) + pltpu.roll(g * s2, 120, 1)


def _in_proj(x, w_in_t, wg_s, b_gate, cos_sin, w_out_s):
    s = x.shape[0]
    ts = min(512, s)
    nsteps = s // ts
    forward_step, far_step = min(3, nsteps - 1), min(5, nsteps - 1)
    widths = [OFF[i + 1] - OFF[i] for i in range(9)]

    def body(x_ref, win_hbm, wgs_ref, bg_ref, cs_ref, wos_ref,
             qa_ref, ka_ref, va_ref, ga_ref, qb_ref, kb_ref, vb_ref, gb_ref, rb_ref, la_ref, oms_ref,
             c_ref, s1_ref, s2_ref, x16_ref, w_ref, wg_ref, wout_ref,
             win_all, wg_all, wout_all, stage, stage_sem, *sems):
        xb = x_ref[...].astype(MXU_DTYPE)
        x16_ref[...] = xb
        c, s1, s2 = _rope_tables(cs_ref[...])
        c_ref[...], s1_ref[...], s2_ref[...] = c, s1, s2
        i0 = pl.program_id(0)
        gather = _BlockGather(wout_all, *sems[0:2])

        @pl.when(i0 == 0)
        def _():
            ka_ref[0:BLOCK, :] = jnp.zeros((BLOCK, 128), ka_ref.dtype)
            va_ref[0:BLOCK, :] = jnp.zeros((BLOCK, 128), va_ref.dtype)
            first = (_BlockGather(win_all, *sems[2:4]), _BlockGather(wg_all, *sems[4:6]))
            load = pltpu.make_async_copy(win_hbm.at[:, 0, :], stage, stage_sem)
            load.start()
            wout_all[gather.me] = wos_ref[...].astype(wout_all.dtype)
            wg_all[gather.me] = wgs_ref[...].astype(wg_all.dtype)
            load.wait()
            win_all[gather.me] = stage[...].astype(win_all.dtype)
            for stage_of in ("start", "forward", "forward_far", "finish"):
                for g in first:
                    getattr(g, stage_of)()
            gather.start()
            for j in range(N_DEV):
                w_ref[D_IN_SHARD * j:D_IN_SHARD * (j + 1), :] = win_all[j]
                wg_ref[:, 32 * j:32 * (j + 1)] = wg_all[j]

        @pl.when(i0 == forward_step)
        def _():
            gather.forward()

        @pl.when(i0 == far_step)
        def _():
            gather.forward_far()

        @pl.when(i0 == nsteps - 1)
        def _():
            gather.finish()
            for j in range(N_DEV):
                wout_ref[D_OUT_SHARD * j:D_OUT_SHARD * (j + 1), :] = wout_all[j]

        kv_rows = pl.ds(pl.multiple_of(BLOCK + i0 * ts, BLOCK), ts)

        def cols(i):
            return _mm_nt(xb, w_ref[OFF[i]:OFF[i + 1], :])

        qa = cols(0)
        for i in range(4):
            qa_ref[:, 128 * i:128 * (i + 1)] = _rope(qa[:, 128 * i:128 * (i + 1)], c, s1, s2).astype(qa_ref.dtype)
        kv = _mm_nt(xb, w_ref[OFF[1]:OFF[3], :])
        ka_ref[kv_rows, :] = _rope(kv[:, 0:128], c, s1, s2).astype(ka_ref.dtype)
        va_ref[kv_rows, :] = kv[:, 128:256].astype(va_ref.dtype)
        ga_ref[...] = cols(3)
        qb_ref[...] = cols(4)
        kb_ref[...] = cols(5)
        vb_ref[...] = cols(6).astype(vb_ref.dtype)
        gb_ref[...] = cols(7)
        rb = cols(8)
        rb_ref[...] = rb
        logit = _mm(rb.astype(MXU_DTYPE), wg_ref[...]) + bg_ref[...]
        e = jnp.exp(-jnp.abs(logit))
        la_ref[...] = (jnp.minimum(logit, 0.0) - jnp.log(1.0 + e)) / GLA_TAU
        oms_ref[...] = jnp.where(logit >= 0.0, e, 1.0) / (1.0 + e)

    out_shape = [jax.ShapeDtypeStruct((s + BLOCK if i in (1, 2) else s, w), MXU_DTYPE if i in (0, 1, 2, 6) else F32)
                 for i, w in enumerate(widths)]
    out_shape += [jax.ShapeDtypeStruct((s, 256), F32)] * 2 + [jax.ShapeDtypeStruct((s, 128), F32)] * 3
    out_shape += [jax.ShapeDtypeStruct((s, D_MODEL), MXU_DTYPE)]
    out_shape += [jax.ShapeDtypeStruct((D_IN_PROJ, D_MODEL), MXU_DTYPE), jax.ShapeDtypeStruct((GLA_RANK, 256), MXU_DTYPE),
                  jax.ShapeDtypeStruct((D_MODEL, D_MODEL), MXU_DTYPE)]
    return pl.pallas_call(
        body, name="in_proj", grid=(nsteps,),
        in_specs=[_rows(ts, D_MODEL), pl.BlockSpec(memory_space=pl.ANY), _full((GLA_RANK, 32)), _full((1, 256)),
                  _rows(ts, ROT_DIM), _full((D_OUT_SHARD, D_MODEL))],
        out_specs=[_full((s + BLOCK, w)) if i in (1, 2) else _rows(ts, w) for i, w in enumerate(widths)]
        + [_rows(ts, 256)] * 2 + [_rows(ts, 128)] * 3 + [_rows(ts, D_MODEL)]
        + [_full((D_IN_PROJ, D_MODEL)), _full((GLA_RANK, 256)), _full((D_MODEL, D_MODEL))],
        out_shape=out_shape,
        scratch_shapes=[pltpu.VMEM((N_DEV, D_IN_SHARD, D_MODEL), MXU_DTYPE), pltpu.VMEM((N_DEV, GLA_RANK, 32), MXU_DTYPE),
                        pltpu.VMEM((N_DEV, D_OUT_SHARD, D_MODEL), MXU_DTYPE),
                        pltpu.VMEM((D_IN_SHARD, D_MODEL), F32), pltpu.SemaphoreType.DMA]
        + 3 * _BlockGather.scratch(),
        compiler_params=_cparams(dimension_semantics=("arbitrary",)),
    )(x, w_in_t, wg_s, b_gate, cos_sin, w_out_s)


SWA_ROWS = SWA_GROUP * BLOCK


def _swa_bias():
    shape = (2, 2 * BLOCK, SWA_ROWS)
    ki = lax.broadcasted_iota(jnp.int32, shape, 1)
    qi = lax.broadcasted_iota(jnp.int32, shape, 2) & (BLOCK - 1)
    first = lax.broadcasted_iota(jnp.int32, shape, 0) == 0
    dist = qi + BLOCK - ki
    ok = (dist >= 0) & (dist < BLOCK) & (jnp.logical_not(first) | (ki >= BLOCK))
    return jnp.where(ok, 0.0, -jnp.inf).astype(F32)


SWA_SUB = 8


def _swa_bias_of(bias_ref, n, b):
    return bias_ref[jnp.minimum(n, 1)] if b == 0 else bias_ref[1]


def _swa_dup(t, j):
    t = t.astype(F32)
    low = lax.broadcasted_iota(jnp.int32, t.shape, 1) < SWA_HEAD_DIM
    keep = low if j == 0 else jnp.logical_not(low)
    return jnp.where(keep, t, pltpu.roll(t, SWA_HEAD_DIM, 1)).astype(MXU_DTYPE)


def _swa_stack(t, j):
    low = lax.broadcasted_iota(jnp.int32, (BLOCK, 128), 1) < SWA_HEAD_DIM
    zero = jnp.zeros((BLOCK, 128), t.dtype)
    blocks = []
    for p in (2 * j, 2 * j + 1):
        tp = t[:, 128 * p:128 * (p + 1)]
        blocks += [jnp.where(low, tp, zero), jnp.where(low, zero, tp)]
    return jnp.concatenate(blocks, axis=0)


def _swa_unstack(t):
    low = lax.broadcasted_iota(jnp.int32, (BLOCK, 128), 1) < SWA_HEAD_DIM
    return [jnp.where(low, t[2 * BLOCK * i:2 * BLOCK * i + BLOCK], t[2 * BLOCK * i + BLOCK:2 * BLOCK * (i + 1)])
            for i in range(2)]


def _swa_sink_row(sink_ref, j):
    lane = lax.broadcasted_iota(jnp.int32, (1, SWA_ROWS), 1)
    row = jnp.full((1, SWA_ROWS), sink_ref[SWA_GROUP * j], F32)
    for r in range(1, SWA_GROUP):
        row = jnp.where(lane >= BLOCK * r, sink_ref[SWA_GROUP * j + r], row)
    return row


def _split3(t):
    return jnp.concatenate(_split3_parts(t), axis=1)


def _row_sums_as_row(t):
    ones = jnp.ones((8, 3 * t.shape[1]), MXU_DTYPE)
    return _mm_nt(ones, _split3(t))[0:1, :]


def _swa_probs_t(qs, kd, bias_t, sink):
    sc = _mm_nt(kd, qs) + bias_t
    m = jnp.maximum(jnp.max(sc, axis=0, keepdims=True), sink)
    p = jnp.exp(sc - m)
    ps = jnp.exp(sink - m)
    rinv = 1.0 / (jnp.sum(p, axis=0, keepdims=True) + ps)
    return p * rinv, ps * rinv


def _swa_fwd(sinks, qa, k_pad, v_pad, ga):
    s = qa.shape[0]
    sub = min(SWA_SUB, s // BLOCK)
    tq = sub * BLOCK

    def body(sink_ref, qa_ref, ga_ref, bias_ref, k_ref, v_ref, attn_ref, cat_ref):
        n = pl.program_id(0)
        for b in range(sub):
            rows = slice(BLOCK * b, BLOCK * (b + 1))
            start = pl.multiple_of((n * sub + b) * BLOCK, BLOCK)
            kw = k_ref[pl.ds(start, 2 * BLOCK), :]
            vw = v_ref[pl.ds(start, 2 * BLOCK), :]
            bias_t = _swa_bias_of(bias_ref, n, b)
            q = qa_ref[rows, :] * SWA_SCALE
            g = ga_ref[rows, :]
            silu = g * _sigmoid(g)
            for j in range(SWA_KV_HEADS):
                qs = _swa_stack(q, j).astype(MXU_DTYPE)
                probs, _ = _swa_probs_t(qs, _swa_dup(kw, j), bias_t, _swa_sink_row(sink_ref, j))
                pairs = _swa_unstack(_mm_tn(probs.astype(MXU_DTYPE), _swa_dup(vw, j)))
                for i in range(2):
                    lanes = slice(128 * (2 * j + i), 128 * (2 * j + i + 1))
                    attn_ref[rows, lanes] = pairs[i]
                    cat_ref[rows, lanes] = (pairs[i] * silu[:, lanes]).astype(cat_ref.dtype)

    return pl.pallas_call(
        body, name="swa_fwd", grid=(s // tq,),
        in_specs=[pl.BlockSpec(memory_space=pltpu.SMEM), _rows(tq, 512), _rows(tq, 512),
                  _full((2, 2 * BLOCK, SWA_ROWS)), _full((s + BLOCK, 128)), _full((s + BLOCK, 128))],
        out_specs=[_rows(tq, 512), _rows(tq, 512)],
        out_shape=[jax.ShapeDtypeStruct((s, 512), F32), jax.ShapeDtypeStruct((s, 512), MXU_DTYPE)],
        compiler_params=_cparams(dimension_semantics=("arbitrary",)),
    )(sinks, qa, ga, _swa_bias(), k_pad, v_pad)


GLA_KW = GLA_HEADS * GLA_DK
GLA_VW = GLA_HEADS * GLA_DV


def _idiv(t, d):
    return t >> (d.bit_length() - 1)


def _chunk_cumsum(t, lower):
    n, w = t.shape
    r = lax.broadcasted_iota(jnp.int32, (n, n), 0)
    c = lax.broadcasted_iota(jnp.int32, (n, n), 1)
    tri = ((_idiv(r, GLA_CHUNK) == _idiv(c, GLA_CHUNK)) & ((r >= c) if lower else (r <= c))).astype(MXU_DTYPE)
    parts = _mm(tri, _split3(t))
    return (parts[:, :w] + parts[:, w:2 * w]) + parts[:, 2 * w:]


def _chunk_last(t):
    n = t.shape[0]
    return jnp.concatenate(
        [jnp.broadcast_to(t[c + GLA_CHUNK - 1:c + GLA_CHUNK, :], (GLA_CHUNK, t.shape[1]))
         for c in range(0, n, GLA_CHUNK)], axis=0)


def _head_stack(t, width):
    head = _idiv(lax.broadcasted_iota(jnp.int32, t.shape, 1), width)
    zero = jnp.zeros_like(t)
    return jnp.concatenate([jnp.where(head == h, t, zero) for h in range(GLA_HEADS)], axis=0)


def _heads_to_rows(t):
    return jnp.concatenate([t[:, GLA_DV * h:GLA_DV * (h + 1)] for h in range(GLA_HEADS)], axis=0)


def _rows_to_heads(t):
    return jnp.concatenate([t[GLA_CHUNK * h:GLA_CHUNK * (h + 1)] for h in range(GLA_HEADS)], axis=1)


def _state_by_head(t):
    srow = _idiv(lax.broadcasted_iota(jnp.int32, (GLA_VW, GLA_KW), 0), GLA_DV)
    slane = _idiv(lax.broadcasted_iota(jnp.int32, (GLA_VW, GLA_KW), 1), GLA_DK)
    return jnp.where(srow == slane, jnp.concatenate([t] * GLA_HEADS, axis=0), jnp.zeros((GLA_VW, GLA_KW), t.dtype))


def _gla_masks():
    row = lax.broadcasted_iota(jnp.int32, (GLA_CHUNK, GLA_KW), 0)
    pos = lax.broadcasted_iota(jnp.int32, (GLA_CHUNK, GLA_KW), 1) & (GLA_CHUNK - 1)
    return pos <= row, pos >= row


def _gla_fwd(qb, kb, vb, la, gb, norm_w):
    s = qb.shape[0]
    tb = min(256, s)
    ch = tb // GLA_CHUNK

    def body(qb_ref, kb_ref, vb_ref, la_ref, gb_ref, nw_ref, o_ref, cat_ref, sp_ref, st_ref):
        @pl.when(pl.program_id(0) == 0)
        def _():
            st_ref[...] = jnp.zeros_like(st_ref)

        causal, _ = _gla_masks()
        nw = nw_ref[...]
        b = _chunk_cumsum(la_ref[...], True)
        bl = _chunk_last(b)
        k = kb_ref[...]
        qd = ((qb_ref[...] * GLA_SCALE) * jnp.exp(b)).astype(MXU_DTYPE)
        ki = (k * jnp.exp(-b)).astype(MXU_DTYPE)
        ke = (k * jnp.exp(bl - b)).astype(MXU_DTYPE)
        dec = jnp.exp(bl)
        v = vb_ref[...].astype(MXU_DTYPE)
        g = gb_ref[...]
        silu = g * _sigmoid(g)
        for ci in range(ch):
            rows = slice(GLA_CHUNK * ci, GLA_CHUNK * (ci + 1))
            qds, kis, kes = (_head_stack(t[rows], GLA_DK) for t in (qd, ki, ke))
            a = jnp.where(causal, _mm_nt(qd[rows], kis), 0.0).astype(MXU_DTYPE)
            st = st_ref[...]
            sp_ref[ci] = st
            o = _mm(a, _head_stack(v[rows], GLA_DV)) + _rows_to_heads(_mm_nt(qds, st.astype(MXU_DTYPE)))
            st_ref[...] = st * dec[rows][0:1] + _mm_tn(_heads_to_rows(v[rows]), kes)
            o_ref[rows, :] = o
            for h in range(GLA_HEADS):
                lv = slice(GLA_DV * h, GLA_DV * (h + 1))
                oh = o[:, lv]
                r = lax.rsqrt(jnp.mean(oh * oh, axis=1, keepdims=True) + EPS)
                cat_ref[rows, lv] = (oh * r * nw * silu[rows, lv]).astype(cat_ref.dtype)

    return pl.pallas_call(
        body, name="gla_fwd", grid=(s // tb,),
        in_specs=[_rows(tb, 256), _rows(tb, 256), _rows(tb, 512), _rows(tb, 256), _rows(tb, 512), _full((1, 128))],
        out_specs=[_rows(tb, 512), _rows(tb, 512), pl.BlockSpec((ch, GLA_DV, 256), lambda i: (i, 0, 0))],
        out_shape=[jax.ShapeDtypeStruct((s, 512), F32), jax.ShapeDtypeStruct((s, 512), MXU_DTYPE),
                   jax.ShapeDtypeStruct((s // GLA_CHUNK, GLA_DV, 256), F32)],
        scratch_shapes=[pltpu.VMEM((GLA_DV, GLA_KW), F32)],
        compiler_params=_cparams(dimension_semantics=("arbitrary",)),
    )(qb, kb, vb, la, gb, norm_w)


def _out_ln_loss(cat_a, cat_b, w_out, x, target, ln_g, ln_b):
    s = x.shape[0]
    ts = min(512, s)
    halves = 2 if ts % 32 == 0 else 1
    th = ts // halves

    def body(ca_ref, cb_ref, w_ref, x_ref, t_ref, g_ref, b_ref,
             loss_ref, gx_ref, da_ref, db_ref, gw_ref, gln_ref):
        @pl.when(pl.program_id(0) == 0)
        def _():
            loss_ref[...] = jnp.zeros_like(loss_ref)
            gw_ref[...] = jnp.zeros_like(gw_ref)
            gln_ref[...] = jnp.zeros_like(gln_ref)

        g = g_ref[...]
        dh16s = []
        for k in range(halves):
            rows = slice(th * k, th * (k + 1))
            mix = _mm(ca_ref[rows, :], w_ref[0:512, :]) + _mm(cb_ref[rows, :], w_ref[512:1024, :])
            h = ALPHA * x_ref[rows, :] + mix
            mu = jnp.mean(h, axis=1, keepdims=True)
            hc = h - mu
            rstd = lax.rsqrt(jnp.mean(hc * hc, axis=1, keepdims=True) + EPS)
            xhat = hc * rstd
            err = xhat * g + b_ref[...] - t_ref[rows, :]
            loss_ref[...] += 0.5 * jnp.sum(jnp.mean(err * err, axis=1, keepdims=True))
            dy = err * (1.0 / D_MODEL)
            gln_ref[0:1, :] += jnp.sum(dy * xhat, axis=0, keepdims=True)
            gln_ref[1:2, :] += jnp.sum(dy, axis=0, keepdims=True)
            dxh = dy * g
            dh = rstd * (dxh - jnp.mean(dxh, axis=1, keepdims=True)
                         - xhat * jnp.mean(dxh * xhat, axis=1, keepdims=True))
            gx_ref[rows, :] = ALPHA * dh
            dh16s.append(dh.astype(MXU_DTYPE))
        for k in range(halves):
            rows = slice(th * k, th * (k + 1))
            da_ref[rows, :] = _mm_nt(dh16s[k], w_ref[0:512, :])
            db_ref[rows, :] = _mm_nt(dh16s[k], w_ref[512:1024, :])
        dh16 = jnp.concatenate(dh16s, axis=0)
        gw_ref[0:512, :] += _mm_tn(ca_ref[...], dh16)
        gw_ref[512:1024, :] += _mm_tn(cb_ref[...], dh16)

    return pl.pallas_call(
        body, name="out_ln_loss", grid=(s // ts,),
        in_specs=[_rows(ts, 512), _rows(ts, 512), _full((D_MODEL, D_MODEL)), _rows(ts, D_MODEL), _rows(ts, D_MODEL),
                  _full((1, D_MODEL)), _full((1, D_MODEL))],
        out_specs=[_full((1, 128)), _rows(ts, D_MODEL), _rows(ts, 512), _rows(ts, 512),
                   _full((D_MODEL, D_MODEL)), _full((2, D_MODEL))],
        out_shape=[jax.ShapeDtypeStruct((1, 128), F32), jax.ShapeDtypeStruct((s, D_MODEL), F32),
                   jax.ShapeDtypeStruct((s, 512), F32), jax.ShapeDtypeStruct((s, 512), F32),
                   jax.ShapeDtypeStruct((D_MODEL, D_MODEL), F32), jax.ShapeDtypeStruct((2, D_MODEL), F32)],
        compiler_params=_cparams(dimension_semantics=("arbitrary",)),
    )(cat_a, cat_b, w_out, x, target, ln_g, ln_b)


def _swa_bwd(sinks, qa, k_pad, v_pad, attn, ga, d_cat_a, rope, parts_w_out):
    s = qa.shape[0]
    sub = min(SWA_SUB, s // BLOCK)
    tq = sub * BLOCK
    nsteps = s // tq
    forward_step = min(1, nsteps - 1)

    def body(sink_ref, qa_ref, ga_ref, at_ref, dc_ref, c_ref, s1_ref, s2_ref, bias_ref, k_ref, v_ref, pout_ref,
             dq_ref, dg_ref, dk_out, dv_out, ds_ref, gout_ref, dk_ref, dv_ref, *scratch):
        n = pl.program_id(0)
        owner_sum = _OwnerSum(pout_ref, *scratch)

        @pl.when(n == 0)
        def _():
            dk_ref[...] = jnp.zeros_like(dk_ref)
            dv_ref[...] = jnp.zeros_like(dv_ref)
            ds_ref[...] = jnp.zeros_like(ds_ref)
            owner_sum.start()

        @pl.when(n == forward_step)
        def _():
            owner_sum.forward()

        @pl.when(n == nsteps - 1)
        def _():
            gout_ref[...] = owner_sum.finish()

        low = lax.broadcasted_iota(jnp.int32, (2 * BLOCK, 128), 1) < SWA_HEAD_DIM
        for b in range(sub):
            rows = slice(BLOCK * b, BLOCK * (b + 1))
            start = pl.multiple_of((n * sub + b) * BLOCK, BLOCK)
            kw = k_ref[pl.ds(start, 2 * BLOCK), :]
            vw = v_ref[pl.ds(start, 2 * BLOCK), :]
            bias_t = _swa_bias_of(bias_ref, n, b)
            q = qa_ref[rows, :] * SWA_SCALE
            g = ga_ref[rows, :]
            sg = _sigmoid(g)
            o = at_ref[rows, :]
            dc = dc_ref[rows, :]
            do = dc * (g * sg)
            dg_ref[rows, :] = (dc * o * (sg * (1.0 + g * (1.0 - sg)))).astype(dg_ref.dtype)
            od = do * o
            c, s1, s2 = c_ref[rows, :], s1_ref[rows, :], s2_ref[rows, :]
            dk, dv = [], []
            for j in range(SWA_KV_HEADS):
                kd, vd = _swa_dup(kw, j), _swa_dup(vw, j)
                qs = _swa_stack(q, j).astype(MXU_DTYPE)
                dos = _swa_stack(do, j).astype(MXU_DTYPE)
                probs, psink = _swa_probs_t(qs, kd, bias_t, _swa_sink_row(sink_ref, j))
                delta = _row_sums_as_row(_swa_stack(od, j))
                dsc = (probs * (_mm_nt(vd, dos) - delta)).astype(MXU_DTYPE)
                dsink = psink * delta
                for r in range(SWA_GROUP):
                    h = SWA_GROUP * j + r
                    ds_ref[h:h + 1, :] += jnp.zeros((1, 128), F32) - jnp.sum(dsink[:, BLOCK * r:BLOCK * (r + 1)])
                dq = _swa_unstack(_mm_tn(dsc, kd))
                for i in range(2):
                    lanes = slice(128 * (2 * j + i), 128 * (2 * j + i + 1))
                    dq_ref[rows, lanes] = _rope_t(dq[i] * SWA_SCALE, c, s1, s2).astype(dq_ref.dtype)
                dkj = _mm(dsc, qs)
                dvj = _mm(probs.astype(MXU_DTYPE), dos)
                dk.append(dkj + pltpu.roll(dkj, SWA_HEAD_DIM, 1))
                dv.append(dvj + pltpu.roll(dvj, SWA_HEAD_DIM, 1))
            dk_ref[pl.ds(start, 2 * BLOCK), :] += jnp.where(low, dk[0], dk[1])
            dv_ref[pl.ds(start, 2 * BLOCK), :] += jnp.where(low, dv[0], dv[1])

        @pl.when(n == nsteps - 1)
        def _():
            dk_out[...] = dk_ref[BLOCK:, :]
            dv_out[...] = dv_ref[BLOCK:, :]

    out_blk = parts_w_out.shape[1:]
    return pl.pallas_call(
        body, name="swa_bwd", grid=(nsteps,),
        in_specs=[pl.BlockSpec(memory_space=pltpu.SMEM)] + [_rows(tq, 512)] * 4 + [_rows(tq, 128)] * 3
        + [_full((2, 2 * BLOCK, SWA_ROWS))] + [_full((s + BLOCK, 128))] * 2 + [pl.BlockSpec(memory_space=pl.ANY)],
        out_specs=[_rows(tq, 512), _rows(tq, 512), _full((s, 128)), _full((s, 128)),
                   _full((SWA_Q_HEADS, 128)), _full(out_blk)],
        out_shape=[jax.ShapeDtypeStruct((s, 512), MXU_DTYPE), jax.ShapeDtypeStruct((s, 512), MXU_DTYPE),
                   jax.ShapeDtypeStruct((s, 128), F32), jax.ShapeDtypeStruct((s, 128), F32),
                   jax.ShapeDtypeStruct((SWA_Q_HEADS, 128), F32), jax.ShapeDtypeStruct(out_blk, F32)],
        scratch_shapes=[pltpu.VMEM((s + BLOCK, 128), F32)] * 2 + _OwnerSum.scratch(out_blk),
        compiler_params=_cparams(dimension_semantics=("arbitrary",)),
    )(sinks, qa, ga, attn, d_cat_a, *rope, _swa_bias(), k_pad, v_pad, parts_w_out)


def _gla_bwd(qb, kb, vb, la, oms, gb, o, sprev, d_cat_b, rb, wg, norm_w):
    s = qb.shape[0]
    tb = min(512, s)
    ch = tb // GLA_CHUNK
    nb = s // tb

    def body(qb_ref, kb_ref, vb_ref, la_ref, oms_ref, gb_ref, o_ref, sp_ref, dc_ref, rb_ref, wg_ref, nw_ref,
             dq_ref, dk_ref, dv_ref, dg_ref, dr_ref, gwg_ref, gbg_ref, gnw_ref, dst_ref):
        @pl.when(pl.program_id(0) == 0)
        def _():
            dst_ref[...] = jnp.zeros_like(dst_ref)
            gwg_ref[...] = jnp.zeros_like(gwg_ref)
            gbg_ref[...] = jnp.zeros_like(gbg_ref)
            gnw_ref[...] = jnp.zeros_like(gnw_ref)

        causal, causal_t = _gla_masks()
        nw = nw_ref[...]
        b = _chunk_cumsum(la_ref[...], True)
        bl = _chunk_last(b)
        eb, enb, ee, dec = jnp.exp(b), jnp.exp(-b), jnp.exp(bl - b), jnp.exp(bl)
        k = kb_ref[...]
        qd = (qb_ref[...] * GLA_SCALE) * eb
        ki = k * enb
        ke = k * ee
        qd16, ki16, ke16 = qd.astype(MXU_DTYPE), ki.astype(MXU_DTYPE), ke.astype(MXU_DTYPE)
        v16 = vb_ref[...].astype(MXU_DTYPE)

        g = gb_ref[...]
        sg = _sigmoid(g)
        silu = g * sg
        dsilu = sg * (1.0 + g * (1.0 - sg))
        gnw = jnp.zeros((1, GLA_DV), F32)
        do = []
        for h in range(GLA_HEADS):
            lv = slice(GLA_DV * h, GLA_DV * (h + 1))
            oh = o_ref[:, lv]
            dch = dc_ref[:, lv]
            r = lax.rsqrt(jnp.mean(oh * oh, axis=1, keepdims=True) + EPS)
            d_on = dch * silu[:, lv]
            dg_ref[:, lv] = (dch * (oh * r * nw) * dsilu[:, lv]).astype(dg_ref.dtype)
            gnw += jnp.sum(d_on * oh * r, axis=0, keepdims=True)
            u = d_on * nw
            do.append(r * u - oh * (r * r * r) * jnp.mean(u * oh, axis=1, keepdims=True))
        gnw_ref[...] += gnw
        do16 = jnp.concatenate(do, axis=1).astype(MXU_DTYPE)

        db, dbl = [None] * ch, [None] * ch
        for ci in reversed(range(ch)):
            rows = slice(GLA_CHUNK * ci, GLA_CHUNK * (ci + 1))
            qds, kis, kes = (_head_stack(t[rows], GLA_DK) for t in (qd16, ki16, ke16))
            vs, dos = _head_stack(v16[rows], GLA_DV), _head_stack(do16[rows], GLA_DV)
            a = jnp.where(causal, _mm_nt(qd16[rows], kis), 0.0).astype(MXU_DTYPE)
            at = jnp.where(causal_t, _mm_nt(ki16[rows], qds), 0.0).astype(MXU_DTYPE)
            da = jnp.where(causal, _mm_nt(do16[rows], vs), 0.0).astype(MXU_DTYPE)
            dat = jnp.where(causal_t, _mm_nt(v16[rows], dos), 0.0).astype(MXU_DTYPE)
            st = sp_ref[ci]
            dst = dst_ref[...]
            dst16 = dst.astype(MXU_DTYPE)
            dv = _mm(at, dos) + _rows_to_heads(_mm_nt(kes, dst16))
            dqd = _mm(da, kis) + _mm(do16[rows], _state_by_head(st.astype(MXU_DTYPE)))
            dki = _mm(dat, qds)
            dke = _mm(v16[rows], _state_by_head(dst16))
            ddec = jnp.sum(dst * st, axis=0, keepdims=True)
            decc = dec[rows][0:1]
            dst_ref[...] = _mm_tn(_heads_to_rows(do16[rows]), qds) + dst * decc
            dq_ref[rows, :] = (dqd * eb[rows] * GLA_SCALE).astype(dq_ref.dtype)
            dk_ref[rows, :] = (dki * enb[rows] + dke * ee[rows]).astype(dk_ref.dtype)
            dv_ref[rows, :] = dv.astype(dv_ref.dtype)
            dke_ke = dke * ke[rows]
            db[ci] = dqd * qd[rows] - dki * ki[rows] - dke_ke
            dbl[ci] = jnp.broadcast_to(jnp.sum(dke_ke, axis=0, keepdims=True) + ddec * decc, (GLA_CHUNK, GLA_KW))

        dla = _chunk_cumsum(jnp.concatenate(db, axis=0), False) + jnp.concatenate(dbl, axis=0)
        dlogit = dla * oms_ref[...] * (1.0 / GLA_TAU)
        dl16 = dlogit.astype(MXU_DTYPE)
        gbg_ref[...] += jnp.sum(dlogit, axis=0, keepdims=True)
        gwg_ref[...] += _mm_tn(rb_ref[...].astype(MXU_DTYPE), dl16)
        dr_ref[...] = _mm_nt(dl16, wg_ref[...]).astype(dr_ref.dtype)

    def rev(width):
        return pl.BlockSpec((tb, width), lambda i: (nb - 1 - i, 0))

    return pl.pallas_call(
        body, name="gla_bwd", grid=(nb,),
        in_specs=[rev(256), rev(256), rev(512), rev(256), rev(256), rev(512), rev(512),
                  pl.BlockSpec((ch, GLA_DV, 256), lambda i: (nb - 1 - i, 0, 0)), rev(512), rev(GLA_RANK),
                  _full((GLA_RANK, 256)), _full((1, 128))],
        out_specs=[rev(256), rev(256), rev(512), rev(512), rev(GLA_RANK),
                   _full((GLA_RANK, 256)), _full((1, 256)), _full((1, 128))],
        out_shape=[jax.ShapeDtypeStruct((s, 256), MXU_DTYPE), jax.ShapeDtypeStruct((s, 256), MXU_DTYPE),
                   jax.ShapeDtypeStruct((s, 512), MXU_DTYPE), jax.ShapeDtypeStruct((s, 512), MXU_DTYPE),
                   jax.ShapeDtypeStruct((s, GLA_RANK), MXU_DTYPE), jax.ShapeDtypeStruct((GLA_RANK, 256), F32),
                   jax.ShapeDtypeStruct((1, 256), F32), jax.ShapeDtypeStruct((1, 128), F32)],
        scratch_shapes=[pltpu.VMEM((GLA_DV, GLA_KW), F32)],
        compiler_params=_cparams(dimension_semantics=("arbitrary",)),
    )(qb, kb, vb, la, oms, gb, o, sprev, d_cat_b, rb, wg, norm_w)


def _dproj_tiles(piece_refs, rope_refs, members=(0, 1, 3, 4, 5, 6, 7, 8)):
    for i in members:
        if i == 1:
            dk = _rope_t(piece_refs[1][...], *(r[...] for r in rope_refs))
            yield OFF[1], OFF[3], jnp.concatenate([dk, piece_refs[2][...]], axis=1).astype(MXU_DTYPE)
        else:
            yield OFF[i], OFF[i + 1], piece_refs[i][...].astype(MXU_DTYPE)


def _in_proj_bwd_x(gx0, pieces, w_in, rope):
    s = gx0.shape[0]
    ts = min(512, s)
    widths = [OFF[i + 1] - OFF[i] for i in range(9)]

    def body(gx0_ref, *refs):
        w_ref, gx_ref = refs[12:]
        acc = gx0_ref[...]
        for lo, hi, t16 in _dproj_tiles(refs[:9], refs[9:12]):
            acc += _mm(t16, w_ref[lo:hi, :])
        gx_ref[...] = acc

    return pl.pallas_call(
        body, name="in_proj_bwd_x", grid=(s // ts,),
        in_specs=[_rows(ts, D_MODEL)] + [_rows(ts, w) for w in widths] + [_rows(ts, 128)] * 3
        + [_full((D_IN_PROJ, D_MODEL))],
        out_specs=_rows(ts, D_MODEL),
        out_shape=jax.ShapeDtypeStruct((s, D_MODEL), F32),
        compiler_params=_cparams(dimension_semantics=("arbitrary",)),
    )(gx0, *pieces, *rope, w_in)


GW_PASSES = 4
GW_EVENT_STEPS = (0, 2, 6, 8)


def _in_proj_bwd_w(x, pieces, rope, parts_wg, g_ln, g_bg, g_nw, g_sinks, loss):
    s = x.shape[0]
    ts = min(1024, s)
    nt = s // ts
    n_pass = GW_PASSES
    n_steps = n_pass * nt
    cw = D_MODEL // n_pass
    n_chips = N_DEV // 2
    blk = (D_IN_SHARD, cw)

    def body(x_hbm, *refs):
        piece_refs, rope_refs = refs[:9], refs[9:12]
        pwg_ref, gln_ref, gbg_ref, gnw_ref, gsk_ref, loss_ref, gin_ref, rwg_ref, rsm_ref = refs[12:21]
        (acc_ref, stage_ref, sib_ref, snd_ref, rcv_ref, mine_ref, sm_ref, xs_ref,
         d2d_send, d2d_recv, ici_send, ici_recv, out_sems, sm_send, sm_recv, sm_loc, x_sems) = refs[21:]
        p, t = pl.program_id(0), pl.program_id(1)
        step = p * nt + t
        tile = pl.ds(pl.multiple_of(t * ts, ts), ts)

        def x_load(k, q):
            rows = pl.ds(pl.multiple_of(k * ts, ts), ts)
            return pltpu.make_async_copy(x_hbm.at[rows, pl.ds(q * cw, cw)], xs_ref.at[q, rows, :],
                                         x_sems.at[q * nt + k])
        x_, y_, c = _mesh_pos()
        me, mychip, sibling = 4 * x_ + 2 * y_ + c, 2 * x_ + y_, (x_, y_, 1 - c)
        small_dsts = (rwg_ref, rsm_ref)

        def small_src(a, block):
            return pwg_ref.at[block] if a == 0 else sm_ref

        def small_copy(k, a, src_block, dst_block, peer):
            i = 2 * (k - 1) + a
            return pltpu.make_async_remote_copy(
                src_ref=small_src(a, src_block), dst_ref=small_dsts[a].at[dst_block], send_sem=sm_send.at[i],
                recv_sem=sm_recv.at[i], device_id=peer, device_id_type=pl.DeviceIdType.MESH)

        def small_local(a):
            return pltpu.make_async_copy(small_src(a, me), small_dsts[a].at[me], sm_loc.at[a])

        @pl.when(step == 0)
        def _():
            for q in range(n_pass):
                for k in range(nt):
                    x_load(k, q).start()
            acc_ref[...] = jnp.zeros_like(acc_ref)
            sm_ref[...] = jnp.zeros_like(sm_ref)
            for r in range(D_MODEL // 128):
                sm_ref[r:r + 1, :] = gln_ref[0:1, 128 * r:128 * (r + 1)]
                sm_ref[8 + r:9 + r, :] = gln_ref[1:2, 128 * r:128 * (r + 1)]
            for r in range(2):
                sm_ref[16 + r:17 + r, :] = gbg_ref[0:1, 128 * r:128 * (r + 1)]
            sm_ref[24:25, :] = gnw_ref[...]
            diag = (lax.broadcasted_iota(jnp.int32, gsk_ref.shape, 0)
                    == lax.broadcasted_iota(jnp.int32, gsk_ref.shape, 1))
            sm_ref[32:33, :] = jnp.sum(jnp.where(diag, gsk_ref[...], 0.0), axis=0, keepdims=True)
            sm_ref[40:41, :] = loss_ref[...]
            for a in range(2):
                small_local(a).start()
            for k in range(1, N_DEV):
                peer, pidx = _peer(k, x_, y_, c)
                for a in range(2):
                    small_copy(k, a, pidx, me, peer).start()

        for q in range(n_pass):
            @pl.when(p == q)
            def _(q=q):
                x_load(t, q).wait()

        xb = xs_ref[p, tile, :]
        for lo, hi, t16 in _dproj_tiles(piece_refs, rope_refs):
            acc_ref[p, lo:hi, :] += _mm_tn(t16, xb)

        def block_rows(q, j):
            return acc_ref[q, D_IN_SHARD * j:D_IN_SHARD * (j + 1), :]

        def d2d(q):
            return pltpu.make_async_remote_copy(
                src_ref=stage_ref.at[q % 2], dst_ref=sib_ref.at[q], send_sem=d2d_send.at[q],
                recv_sem=d2d_recv.at[q], device_id=sibling, device_id_type=pl.DeviceIdType.MESH)

        def ici(q, slot, owner):
            i = 3 * q + slot
            return pltpu.make_async_remote_copy(
                src_ref=snd_ref.at[q, slot], dst_ref=rcv_ref.at[q, slot], send_sem=ici_send.at[i],
                recv_sem=ici_recv.at[i], device_id=owner, device_id_type=pl.DeviceIdType.MESH)

        def out_copy(q):
            return pltpu.make_async_copy(mine_ref.at[q], gin_ref.at[:, pl.ds(q * cw, cw)], out_sems.at[q])

        first = (jnp.where(c == 0, 1 - x_, x_), jnp.where(c == 0, y_, 1 - y_))
        second = (jnp.where(c == 0, x_, 1 - x_), jnp.where(c == 0, 1 - y_, y_))

        def chip_of(pos):
            return 2 * pos[0] + pos[1]

        def to_sibling(q):
            if q >= 2:
                d2d(q - 2).wait_send()
            for cc in range(2):
                @pl.when(c == cc)
                def _(cc=cc):
                    for k in range(n_chips):
                        stage_ref[q % 2, k] = block_rows(q, 2 * k + 1 - cc)
            d2d(q).start()

        def chip_sums_leave(q):
            d2d(q).wait_recv()
            for cc in range(2):
                @pl.when(c == cc)
                def _(cc=cc):
                    for k in range(n_chips):
                        sib_ref[q, k] = block_rows(q, 2 * k + cc) + sib_ref[q, k]
            snd_ref[q, 2] = sib_ref[q, chip_of((1 - x_, 1 - y_))].astype(snd_ref.dtype)
            ici(q, 2, (*first, c)).start()
            snd_ref[q, 0] = sib_ref[q, chip_of(first)].astype(snd_ref.dtype)
            ici(q, 0, (*first, c)).start()

        def combined_sum_leaves(q):
            ici(q, 2, sibling).wait_recv()
            snd_ref[q, 1] = (sib_ref[q, chip_of(second)] + rcv_ref[q, 2].astype(F32)).astype(snd_ref.dtype)
            ici(q, 1, (*second, c)).start()

        def owner_total(q):
            total = sib_ref[q, mychip]
            for slot in range(2):
                ici(q, slot, sibling).wait_recv()
                total = total + rcv_ref[q, slot].astype(F32)
            mine_ref[q] = total
            out_copy(q).start()

        stages = (to_sibling, chip_sums_leave, combined_sum_leaves, owner_total)
        events = sorted((min((q + 1) * nt - 1 + GW_EVENT_STEPS[i], n_steps - 1), i > 0, q, i)
                        for q in range(n_pass) for i in range(len(stages)))
        for at_step, _, q, i in events:
            @pl.when(step == at_step)
            def _(q=q, i=i):
                stages[i](q)

        @pl.when(step == n_steps - 1)
        def _():
            for k in range(1, N_DEV):
                peer, pidx = _peer(k, x_, y_, c)
                for a in range(2):
                    small_copy(k, a, me, pidx, peer).wait_recv()
            for k in range(1, N_DEV):
                peer, pidx = _peer(k, x_, y_, c)
                for a in range(2):
                    small_copy(k, a, pidx, me, peer).wait_send()
            for a in range(2):
                small_local(a).wait()
            for q in range(max(n_pass - 2, 0), n_pass):
                d2d(q).wait_send()
            for q in range(n_pass):
                for slot in range(3):
                    ici(q, slot, sibling).wait_send()
            for q in range(n_pass):
                out_copy(q).wait()

    widths = [OFF[i + 1] - OFF[i] for i in range(9)]
    hbm = pl.BlockSpec(memory_space=pl.ANY)
    vmem = pl.BlockSpec(memory_space=pltpu.VMEM)

    def token_tile(width):
        return pl.BlockSpec((ts, width), lambda p, t: (t, 0))

    return pl.pallas_call(
        body, name="in_proj_bwd_w", grid=(n_pass, nt),
        in_specs=[hbm] + [token_tile(w) for w in widths] + [token_tile(128)] * 3 + [hbm] + [vmem] * 5,
        out_specs=[hbm, hbm, hbm],
        out_shape=[jax.ShapeDtypeStruct((D_IN_SHARD, D_MODEL), F32),
                   jax.ShapeDtypeStruct((N_DEV,) + parts_wg.shape[1:], F32),
                   jax.ShapeDtypeStruct((N_DEV, SMALL_ROWS, 128), F32)],
        scratch_shapes=[pltpu.VMEM((n_pass, D_IN_PROJ, cw), F32), pltpu.VMEM((2, n_chips) + blk, F32),
                        pltpu.VMEM((n_pass, n_chips) + blk, F32), pltpu.VMEM((n_pass, 3) + blk, MXU_DTYPE),
                        pltpu.VMEM((n_pass, 3) + blk, MXU_DTYPE), pltpu.VMEM((n_pass,) + blk, F32),
                        pltpu.VMEM((SMALL_ROWS, 128), F32), pltpu.VMEM((n_pass, s, cw), MXU_DTYPE),
                        pltpu.SemaphoreType.DMA((n_pass,)), pltpu.SemaphoreType.DMA((n_pass,)),
                        pltpu.SemaphoreType.DMA((3 * n_pass,)), pltpu.SemaphoreType.DMA((3 * n_pass,)),
                        pltpu.SemaphoreType.DMA((n_pass,)),
                        pltpu.SemaphoreType.DMA((2 * (N_DEV - 1),)), pltpu.SemaphoreType.DMA((2 * (N_DEV - 1),)),
                        pltpu.SemaphoreType.DMA((2,)), pltpu.SemaphoreType.DMA((n_pass * nt,))],
        compiler_params=_cparams(dimension_semantics=("arbitrary", "arbitrary")),
    )(x, *pieces, *rope, parts_wg, g_ln, g_bg, g_nw, g_sinks, loss)


def _local_step(x, positions, w_in_t, wg_s, b_gate, sinks, norm_w, w_out_s, ln_g, ln_b, target):
    qa, k_pad, v_pad, ga, qb, kb, vb, gb, rb, la, oms, *rope, x16, w_in, wg, w_out = _in_proj(
        x, w_in_t, wg_s, b_gate, _rope_angles(positions), w_out_s)
    attn, cat_a = _swa_fwd(sinks, qa, k_pad, v_pad, ga)
    o, cat_b, sprev = _gla_fwd(qb, kb, vb, la, gb, norm_w)
    loss, gx0, d_cat_a, d_cat_b, g_w_out, g_ln = _out_ln_loss(cat_a, cat_b, w_out, x, target, ln_g, ln_b)
    parts_w_out = g_w_out.reshape(N_DEV, D_OUT_SHARD, D_MODEL)
    dqa, dga, dka, dva, g_sinks, g_out = _swa_bwd(sinks, qa, k_pad, v_pad, attn, ga, d_cat_a, rope, parts_w_out)
    dqb, dkb, dvb, dgb, drb, g_wg, g_bg, g_nw = _gla_bwd(qb, kb, vb, la, oms, gb, o, sprev, d_cat_b, rb, wg, norm_w)
    pieces = (dqa, dka, dva, dga, dqb, dkb, dvb, dgb, drb)
    grad_x = _in_proj_bwd_x(gx0, pieces, w_in, rope)
    parts_wg = jnp.transpose(g_wg.reshape(GLA_RANK, N_DEV, 32), (1, 0, 2))
    g_in, r_wg, r_small = _in_proj_bwd_w(x16, pieces, rope, parts_wg, g_ln, g_bg, g_nw, g_sinks, loss)
    return grad_x, g_in, g_out, r_wg, r_small


def _mesh_pos():
    return lax.axis_index("x"), lax.axis_index("y"), lax.axis_index("c")


def _peer(k, x, y, c):
    px = (1 - x) if k & 4 else x
    py = (1 - y) if k & 2 else y
    pc = (1 - c) if k & 1 else c
    return (px, py, pc), 4 * px + 2 * py + pc


def _other_chips(x, y):
    return [(1 - x, y), (x, 1 - y), (1 - x, 1 - y)]


def _shard_view(t):
    return jnp.transpose(t, (2, 0, 1))


class _BlockGather:
    def __init__(self, slots, send_sems, recv_sems):
        self.slots, self.send_sems, self.recv_sems = slots, send_sems, recv_sems
        x, y, c = _mesh_pos()
        self.c, self.me, self.sibling = c, 4 * x + 2 * y + c, (x, y, 1 - c)
        first = (jnp.where(c == 0, 1 - x, x), jnp.where(c == 0, y, 1 - y))
        second = (jnp.where(c == 0, x, 1 - x), jnp.where(c == 0, 1 - y, y))
        self.chips = [first, second, (1 - x, 1 - y)]

    @staticmethod
    def scratch():
        return [pltpu.SemaphoreType.DMA((N_DEV - 1,)), pltpu.SemaphoreType.DMA((N_DEV - 1,))]

    def _copy(self, k, block, to):
        return pltpu.make_async_remote_copy(
            src_ref=self.slots.at[block], dst_ref=self.slots.at[block], send_sem=self.send_sems.at[k],
            recv_sem=self.recv_sems.at[k], device_id=to, device_id_type=pl.DeviceIdType.MESH)

    def _block(self, j, c):
        cx, cy = self.chips[j]
        return 4 * cx + 2 * cy + c

    def _dev(self, j):
        return (*self.chips[j], self.c)

    def start(self):
        self._copy(1, self.me, self._dev(0)).start()
        self._copy(2, self.me, self._dev(1)).start()
        self._copy(0, self.me, self.sibling).start()

    def forward(self):
        self._copy(1, self._block(0, self.c), self.sibling).wait_recv()
        self._copy(3, self._block(0, self.c), self._dev(1)).start()
        self._copy(4, self._block(0, self.c), self.sibling).start()
        self._copy(2, self._block(1, self.c), self.sibling).wait_recv()
        self._copy(5, self._block(1, self.c), self.sibling).start()

    def forward_far(self):
        self._copy(3, self._block(2, self.c), self.sibling).wait_recv()
        self._copy(6, self._block(2, self.c), self.sibling).start()

    def finish(self):
        for k in (0, 4, 5, 6):
            self._copy(k, self.me, self.sibling).wait_recv()
        for k in range(N_DEV - 1):
            self._copy(k, self.me, self.sibling).wait_send()


class _OwnerSum:
    def __init__(self, parts, own, sib, snd, rcv, loc_sems, d2d_send, d2d_recv, ici_send, ici_recv):
        self.parts, self.own, self.sib, self.snd, self.rcv = parts, own, sib, snd, rcv
        self.sems = (loc_sems, d2d_send, d2d_recv, ici_send, ici_recv)
        x, y, c = _mesh_pos()
        self.c, self.sibling = c, (x, y, 1 - c)
        self.chips = [(x, y)] + _other_chips(x, y)

    @staticmethod
    def scratch(block):
        return [pltpu.VMEM((4,) + block, F32), pltpu.VMEM((4,) + block, F32),
                pltpu.VMEM((3,) + block, MXU_DTYPE), pltpu.VMEM((3,) + block, MXU_DTYPE),
                pltpu.SemaphoreType.DMA((4,)), pltpu.SemaphoreType.DMA((4,)), pltpu.SemaphoreType.DMA((4,)),
                pltpu.SemaphoreType.DMA((3,)), pltpu.SemaphoreType.DMA((3,))]

    def _local(self, r):
        cx, cy = self.chips[r]
        return pltpu.make_async_copy(self.parts.at[4 * cx + 2 * cy + self.c], self.own.at[r], self.sems[0].at[r])

    def _d2d(self, r):
        cx, cy = self.chips[r]
        return pltpu.make_async_remote_copy(
            src_ref=self.parts.at[4 * cx + 2 * cy + (1 - self.c)], dst_ref=self.sib.at[r], send_sem=self.sems[1].at[r],
            recv_sem=self.sems[2].at[r], device_id=self.sibling, device_id_type=pl.DeviceIdType.MESH)

    def _ici(self, r):
        cx, cy = self.chips[r]
        return pltpu.make_async_remote_copy(
            src_ref=self.snd.at[r - 1], dst_ref=self.rcv.at[r - 1], send_sem=self.sems[3].at[r - 1],
            recv_sem=self.sems[4].at[r - 1], device_id=(cx, cy, self.c), device_id_type=pl.DeviceIdType.MESH)

    def start(self):
        for r in (1, 2, 3, 0):
            self._local(r).start()
            self._d2d(r).start()

    def forward(self):
        for r in (1, 2, 3):
            self._local(r).wait()
            self._d2d(r).wait_recv()
            self.snd[r - 1] = (self.own[r] + self.sib[r]).astype(self.snd.dtype)
            self._ici(r).start()

    def finish(self):
        self._local(0).wait()
        self._d2d(0).wait_recv()
        acc = self.own[0] + self.sib[0]
        for r in (1, 2, 3):
            self._ici(r).wait_recv()
            acc = acc + self.rcv[r - 1].astype(F32)
        for r in range(4):
            self._d2d(r).wait_send()
        for r in (1, 2, 3):
            self._ici(r).wait_send()
        return acc


SMALL_ROWS = 48


def _adamw_math(g, w, m, v):
    nm = ADAM_B1 * m + (1.0 - ADAM_B1) * g
    nv = ADAM_B2 * v + (1.0 - ADAM_B2) * (g * g)
    m_hat = nm / (1.0 - ADAM_B1 ** ADAM_STEP)
    v_hat = nv / (1.0 - ADAM_B2 ** ADAM_STEP)
    return -ADAM_LR * (m_hat / (jnp.sqrt(v_hat) + ADAM_EPS) + ADAM_WD * w), nm, nv


def _adamw_shard_view(g, w, m, v):
    rows, width = g.shape
    parts = 3
    pr = rows // parts
    assert pr * parts == rows

    def body(g_ref, w_hbm, m_hbm, v_hbm, g_out, d_out, nm_out, nv_out, bufs, outs, sems):
        def load(k, i, src):
            return pltpu.make_async_copy(src.at[pl.ds(pr * k, pr), 0, :], bufs.at[k, i], sems.at[7 * k + i])

        def store(k, i, dst):
            return pltpu.make_async_copy(outs.at[k, i], dst.at[pl.ds(pr * k, pr), 0, :], sems.at[7 * k + 3 + i])

        for k in range(parts):
            for i, src in enumerate((w_hbm, m_hbm, v_hbm)):
                load(k, i, src).start()
        for k in range(parts):
            for i, src in enumerate((w_hbm, m_hbm, v_hbm)):
                load(k, i, src).wait()
            gk = g_ref[pr * k:pr * (k + 1), :]
            outs[k, 0] = gk
            outs[k, 1], outs[k, 2], outs[k, 3] = _adamw_math(gk, bufs[k, 0], bufs[k, 1], bufs[k, 2])
            for i, dst in enumerate((g_out, d_out, nm_out, nv_out)):
                store(k, i, dst).start()
        for k in range(parts):
            for i, dst in enumerate((g_out, d_out, nm_out, nv_out)):
                store(k, i, dst).wait()

    hbm = pl.BlockSpec(memory_space=pl.ANY)
    return pl.pallas_call(
        body, name="adamw_w_in",
        in_specs=[pl.BlockSpec(memory_space=pltpu.VMEM), hbm, hbm, hbm], out_specs=[hbm] * 4,
        out_shape=[jax.ShapeDtypeStruct((rows, 1, width), F32)] * 4,
        scratch_shapes=[pltpu.VMEM((parts, 3, pr, width), F32), pltpu.VMEM((parts, 4, pr, width), F32),
                        pltpu.SemaphoreType.DMA((7 * parts,))],
        compiler_params=_cparams(),
    )(g, w, m, v)


def _adamw_vectors(r_small, r_wg, g_out, params):
    n_par = len(params)

    def body(rsm_ref, rwg_ref, gout_ref, *refs):
        ins, outs = refs[:3 * n_par], refs[3 * n_par:]
        g = rsm_ref[0]
        gwg = rwg_ref[0]
        for j in range(1, N_DEV):
            g = g + rsm_ref[j]
            gwg = gwg + rwg_ref[j]
        outs[4 * n_par][...] = g[40:41]
        grads = [gwg,
                 jnp.concatenate([g[r:r + 1] for r in range(0, 8)], axis=1),
                 jnp.concatenate([g[r:r + 1] for r in range(8, 16)], axis=1),
                 jnp.concatenate([g[16:17], g[17:18]], axis=1),
                 g[24:25],
                 g[32:33, 0:SWA_Q_HEADS],
                 gout_ref[...]]
        for p, gp in enumerate(grads):
            w_ref, m_ref, v_ref = ins[3 * p:3 * p + 3]
            outs[4 * p][...] = gp
            outs[4 * p + 1][...], outs[4 * p + 2][...], outs[4 * p + 3][...] = _adamw_math(
                gp, w_ref[...], m_ref[...], v_ref[...])

    vmem = pl.BlockSpec(memory_space=pltpu.VMEM)
    flat = [t for wmv in params for t in wmv]
    return pl.pallas_call(
        body, name="adamw_vectors",
        in_specs=[vmem] * (3 + len(flat)), out_specs=[vmem] * (4 * n_par + 1),
        out_shape=[jax.ShapeDtypeStruct(wmv[0].shape, F32) for wmv in params for _ in range(4)]
        + [jax.ShapeDtypeStruct((1, 128), F32)],
        compiler_params=_cparams(),
    )(r_small, r_wg, g_out, *flat)


def kernel(x, positions, w_in, gla_w_gate_up, gla_b_gate, attn_sinks, gla_norm_w, w_out, ln_g, ln_b, loss_target, m_w_in, m_gla_w_gate_up, m_gla_b_gate, m_attn_sinks, m_gla_norm_w, m_w_out, m_ln_g, m_ln_b, v_w_in, v_gla_w_gate_up, v_gla_b_gate, v_attn_sinks, v_gla_norm_w, v_w_out, v_ln_g, v_ln_b):
    grad_x, g_in, g_out, r_wg, r_small = _local_step(
        x[0], positions[0], _shard_view(w_in), gla_w_gate_up[0], gla_b_gate, attn_sinks[0], gla_norm_w, w_out[0],
        ln_g, ln_b, loss_target[0])

    upd_in = _adamw_shard_view(g_in, _shard_view(w_in), _shard_view(m_w_in), _shard_view(v_w_in))
    upd_in = [jnp.transpose(t, (1, 2, 0)) for t in upd_in]
    vec = _adamw_vectors(r_small, r_wg, g_out, [
        (gla_w_gate_up[0], m_gla_w_gate_up[0], v_gla_w_gate_up[0]), (ln_g, m_ln_g, v_ln_g), (ln_b, m_ln_b, v_ln_b),
        (gla_b_gate, m_gla_b_gate, v_gla_b_gate), (gla_norm_w, m_gla_norm_w, v_gla_norm_w),
        (attn_sinks, m_attn_sinks, v_attn_sinks), (w_out[0], m_w_out[0], v_w_out[0])])

    outs = [vec[28][0, 0], grad_x[None]]
    for kind in range(4):
        u_wg, u_ln_g, u_ln_b, u_bg, u_nw, u_sinks, u_out = (vec[4 * p + kind] for p in range(7))
        outs += [upd_in[kind], u_wg[None], u_bg, u_sinks, u_nw, u_out[None], u_ln_g, u_ln_b]
    return tuple(outs)
```

```python
import jax
import jax.numpy as jnp
from jax import lax
from jax.experimental import pallas as pl
from jax.experimental.pallas import tpu as pltpu

F32 = jnp.float32
MXU_DTYPE = jnp.bfloat16

N_DEV = 8
D_MODEL = 1024
SWA_Q_HEADS = 8
SWA_KV_HEADS = 2
SWA_GROUP = 4
SWA_HEAD_DIM = 64
BLOCK = 128
ROPE_THETA = 500000.0
ROT_DIM = 16
GLA_HEADS = 4
GLA_DK = 64
GLA_DV = 128
GLA_RANK = 16
GLA_TAU = 16.0
GLA_CHUNK = 64
D_IN_PROJ = 2832
D_IN_SHARD = D_IN_PROJ // N_DEV
D_OUT_SHARD = D_MODEL // N_DEV
OFF = (0, 512, 640, 768, 1280, 1536, 1792, 2304, 2816, 2832)
EPS = 1e-5
ALPHA = 2.0 ** 0.25
SWA_SCALE = SWA_HEAD_DIM ** -0.5
GLA_SCALE = GLA_DK ** -0.5
ADAM_LR = 0.001
ADAM_B1 = 0.9
ADAM_B2 = 0.999
ADAM_EPS = 1e-08
ADAM_WD = 0.01
ADAM_STEP = 10
VMEM_LIMIT = 56 * 1024 * 1024

_NT = (((1,), (1,)), ((), ()))
_TN = (((0,), (0,)), ((), ()))


def _mm(a, b):
    return jnp.dot(a, b, preferred_element_type=F32)


def _mm_nt(a, b):
    return lax.dot_general(a, b, _NT, preferred_element_type=F32)


def _mm_tn(a, b):
    return lax.dot_general(a, b, _TN, preferred_element_type=F32)


def _sigmoid(t):
    return 1.0 / (1.0 + jnp.exp(-t))


def _cparams(**kw):
    return pltpu.CompilerParams(vmem_limit_bytes=VMEM_LIMIT, **kw)


def _full(shape):
    return pl.BlockSpec(shape, lambda *_: (0,) * len(shape))


def _rows(tile, width):
    return pl.BlockSpec((tile, width), lambda i: (i, 0))


def _rope_angles(positions):
    half = ROT_DIM // 2
    inv_freq = ROPE_THETA ** (-jnp.arange(half, dtype=F32) / half)
    ang = positions.astype(F32)[:, None] * inv_freq[None, :]
    return jnp.concatenate([jnp.cos(ang), jnp.sin(ang)], axis=1)


def _split3_parts(t):
    hi = t.astype(MXU_DTYPE)
    r1 = t - hi.astype(F32)
    mid = r1.astype(MXU_DTYPE)
    return hi, mid, (r1 - mid.astype(F32)).astype(MXU_DTYPE)


def _rope_tables(cs):
    half = ROT_DIM // 2
    i = lax.broadcasted_iota(jnp.int32, (2 * half, 3 * 128), 0)
    lane = lax.broadcasted_iota(jnp.int32, (2 * half, 3 * 128), 1)
    table, pos = _idiv(lane, 128), lane & (SWA_HEAD_DIM - 1)
    is_c = (table == 0) & (pos < ROT_DIM) & ((pos & (half - 1)) == i)
    is_s1 = (table == 1) & (pos < half) & (pos + half == i)
    is_s2 = (table == 2) & (pos >= half) & (pos < ROT_DIM) & (pos == i)
    sel = jnp.where(is_c | is_s2, 1.0, jnp.where(is_s1, -1.0, 0.0)).astype(MXU_DTYPE)
    hi, mid, lo = _split3_parts(cs)
    t = (_mm(hi, sel) + _mm(mid, sel)) + _mm(lo, sel)
    pos1 = lax.broadcasted_iota(jnp.int32, (1, 128), 1) & (SWA_HEAD_DIM - 1)
    return t[:, 0:128] + jnp.where(pos1 >= ROT_DIM, 1.0, 0.0), t[:, 128:256], t[:, 256:384]


def _rope(t, c, s1, s2):
    return t * c + pltpu.roll(t, 120, 1) * s1 + pltpu.roll(t, 8, 1) * s2


def _rope_t(g, c, s1, s2):
    return g * c + pltpu.roll(g * s1, 8, 1) + pltpu.roll(g * s2, 120, 1)


def _in_proj(x, w_in_t, wg_s, b_gate, cos_sin, w_out_s):
    s = x.shape[0]
    ts = min(512, s)
    nsteps = s // ts
    forward_step, far_step = min(3, nsteps - 1), min(5, nsteps - 1)
    widths = [OFF[i + 1] - OFF[i] for i in range(9)]

    def body(x_ref, win_hbm, wgs_ref, bg_ref, cs_ref, wos_ref,
             qa_ref, ka_ref, va_ref, ga_ref, qb_ref, kb_ref, vb_ref, gb_ref, rb_ref, la_ref, oms_ref,
             c_ref, s1_ref, s2_ref, x16_ref, w_ref, wg_ref, wout_ref,
             win_all, wg_all, wout_all, stage, stage_sem, *sems):
        xb = x_ref[...].astype(MXU_DTYPE)
        x16_ref[...] = xb
        c, s1, s2 = _rope_tables(cs_ref[...])
        c_ref[...], s1_ref[...], s2_ref[...] = c, s1, s2
        i0 = pl.program_id(0)
        gather = _BlockGather(wout_all, *sems[0:2])

        @pl.when(i0 == 0)
        def _():
            ka_ref[0:BLOCK, :] = jnp.zeros((BLOCK, 128), ka_ref.dtype)
            va_ref[0:BLOCK, :] = jnp.zeros((BLOCK, 128), va_ref.dtype)
            first = (_BlockGather(win_all, *sems[2:4]), _BlockGather(wg_all, *sems[4:6]))
            load = pltpu.make_async_copy(win_hbm.at[:, 0, :], stage, stage_sem)
            load.start()
            wout_all[gather.me] = wos_ref[...].astype(wout_all.dtype)
            wg_all[gather.me] = wgs_ref[...].astype(wg_all.dtype)
            load.wait()
            win_all[gather.me] = stage[...].astype(win_all.dtype)
            for stage_of in ("start", "forward", "forward_far", "finish"):
                for g in first:
                    getattr(g, stage_of)()
            gather.start()
            for j in range(N_DEV):
                w_ref[D_IN_SHARD * j:D_IN_SHARD * (j + 1), :] = win_all[j]
                wg_ref[:, 32 * j:32 * (j + 1)] = wg_all[j]

        @pl.when(i0 == forward_step)
        def _():
            gather.forward()

        @pl.when(i0 == far_step)
        def _():
            gather.forward_far()

        @pl.when(i0 == nsteps - 1)
        def _():
            gather.finish()
            for j in range(N_DEV):
                wout_ref[D_OUT_SHARD * j:D_OUT_SHARD * (j + 1), :] = wout_all[j]

        kv_rows = pl.ds(pl.multiple_of(BLOCK + i0 * ts, BLOCK), ts)

        def cols(i):
            return _mm_nt(xb, w_ref[OFF[i]:OFF[i + 1], :])

        qa = cols(0)
        for i in range(4):
            qa_ref[:, 128 * i:128 * (i + 1)] = _rope(qa[:, 128 * i:128 * (i + 1)], c, s1, s2).astype(qa_ref.dtype)
        kv = _mm_nt(xb, w_ref[OFF[1]:OFF[3], :])
        ka_ref[kv_rows, :] = _rope(kv[:, 0:128], c, s1, s2).astype(ka_ref.dtype)
        va_ref[kv_rows, :] = kv[:, 128:256].astype(va_ref.dtype)
        ga_ref[...] = cols(3)
        qb_ref[...] = cols(4)
        kb_ref[...] = cols(5)
        vb_ref[...] = cols(6).astype(vb_ref.dtype)
        gb_ref[...] = cols(7)
        rb = cols(8)
        rb_ref[...] = rb
        logit = _mm(rb.astype(MXU_DTYPE), wg_ref[...]) + bg_ref[...]
        e = jnp.exp(-jnp.abs(logit))
        la_ref[...] = (jnp.minimum(logit, 0.0) - jnp.log(1.0 + e)) / GLA_TAU
        oms_ref[...] = jnp.where(logit >= 0.0, e, 1.0) / (1.0 + e)

    out_shape = [jax.ShapeDtypeStruct((s + BLOCK if i in (1, 2) else s, w), MXU_DTYPE if i in (0, 1, 2, 6) else F32)
                 for i, w in enumerate(widths)]
    out_shape += [jax.ShapeDtypeStruct((s, 256), F32)] * 2 + [jax.ShapeDtypeStruct((s, 128), F32)] * 3
    out_shape += [jax.ShapeDtypeStruct((s, D_MODEL), MXU_DTYPE)]
    out_shape += [jax.ShapeDtypeStruct((D_IN_PROJ, D_MODEL), MXU_DTYPE), jax.ShapeDtypeStruct((GLA_RANK, 256), MXU_DTYPE),
                  jax.ShapeDtypeStruct((D_MODEL, D_MODEL), MXU_DTYPE)]
    return pl.pallas_call(
        body, name="in_proj", grid=(nsteps,),
        in_specs=[_rows(ts, D_MODEL), pl.BlockSpec(memory_space=pl.ANY), _full((GLA_RANK, 32)), _full((1, 256)),
                  _rows(ts, ROT_DIM), _full((D_OUT_SHARD, D_MODEL))],
        out_specs=[_full((s + BLOCK, w)) if i in (1, 2) else _rows(ts, w) for i, w in enumerate(widths)]
        + [_rows(ts, 256)] * 2 + [_rows(ts, 128)] * 3 + [_rows(ts, D_MODEL)]
        + [_full((D_IN_PROJ, D_MODEL)), _full((GLA_RANK, 256)), _full((D_MODEL, D_MODEL))],
        out_shape=out_shape,
        scratch_shapes=[pltpu.VMEM((N_DEV, D_IN_SHARD, D_MODEL), MXU_DTYPE), pltpu.VMEM((N_DEV, GLA_RANK, 32), MXU_DTYPE),
                        pltpu.VMEM((N_DEV, D_OUT_SHARD, D_MODEL), MXU_DTYPE),
                        pltpu.VMEM((D_IN_SHARD, D_MODEL), F32), pltpu.SemaphoreType.DMA]
        + 3 * _BlockGather.scratch(),
        compiler_params=_cparams(dimension_semantics=("arbitrary",)),
    )(x, w_in_t, wg_s, b_gate, cos_sin, w_out_s)


SWA_ROWS = SWA_GROUP * BLOCK


def _swa_bias():
    shape = (2, 2 * BLOCK, SWA_ROWS)
    ki = lax.broadcasted_iota(jnp.int32, shape, 1)
    qi = lax.broadcasted_iota(jnp.int32, shape, 2) & (BLOCK - 1)
    first = lax.broadcasted_iota(jnp.int32, shape, 0) == 0
    dist = qi + BLOCK - ki
    ok = (dist >= 0) & (dist < BLOCK) & (jnp.logical_not(first) | (ki >= BLOCK))
    return jnp.where(ok, 0.0, -jnp.inf).astype(F32)


SWA_SUB = 8


def _swa_bias_of(bias_ref, n, b):
    return bias_ref[jnp.minimum(n, 1)] if b == 0 else bias_ref[1]


def _swa_dup(t, j):
    t = t.astype(F32)
    low = lax.broadcasted_iota(jnp.int32, t.shape, 1) < SWA_HEAD_DIM
    keep = low if j == 0 else jnp.logical_not(low)
    return jnp.where(keep, t, pltpu.roll(t, SWA_HEAD_DIM, 1)).astype(MXU_DTYPE)


def _swa_stack(t, j):
    low = lax.broadcasted_iota(jnp.int32, (BLOCK, 128), 1) < SWA_HEAD_DIM
    zero = jnp.zeros((BLOCK, 128), t.dtype)
    blocks = []
    for p in (2 * j, 2 * j + 1):
        tp = t[:, 128 * p:128 * (p + 1)]
        blocks += [jnp.where(low, tp, zero), jnp.where(low, zero, tp)]
    return jnp.concatenate(blocks, axis=0)


def _swa_unstack(t):
    low = lax.broadcasted_iota(jnp.int32, (BLOCK, 128), 1) < SWA_HEAD_DIM
    return [jnp.where(low, t[2 * BLOCK * i:2 * BLOCK * i + BLOCK], t[2 * BLOCK * i + BLOCK:2 * BLOCK * (i + 1)])
            for i in range(2)]


def _swa_sink_row(sink_ref, j):
    lane = lax.broadcasted_iota(jnp.int32, (1, SWA_ROWS), 1)
    row = jnp.full((1, SWA_ROWS), sink_ref[SWA_GROUP * j], F32)
    for r in range(1, SWA_GROUP):
        row = jnp.where(lane >= BLOCK * r, sink_ref[SWA_GROUP * j + r], row)
    return row


def _split3(t):
    return jnp.concatenate(_split3_parts(t), axis=1)


def _row_sums_as_row(t):
    ones = jnp.ones((8, 3 * t.shape[1]), MXU_DTYPE)
    return _mm_nt(ones, _split3(t))[0:1, :]


def _swa_probs_t(qs, kd, bias_t, sink):
    sc = _mm_nt(kd, qs) + bias_t
    m = jnp.maximum(jnp.max(sc, axis=0, keepdims=True), sink)
    p = jnp.exp(sc - m)
    ps = jnp.exp(sink - m)
    rinv = 1.0 / (jnp.sum(p, axis=0, keepdims=True) + ps)
    return p * rinv, ps * rinv


def _swa_fwd(sinks, qa, k_pad, v_pad, ga):
    s = qa.shape[0]
    sub = min(SWA_SUB, s // BLOCK)
    tq = sub * BLOCK

    def body(sink_ref, qa_ref, ga_ref, bias_ref, k_ref, v_ref, attn_ref, cat_ref):
        n = pl.program_id(0)
        for b in range(sub):
            rows = slice(BLOCK * b, BLOCK * (b + 1))
            start = pl.multiple_of((n * sub + b) * BLOCK, BLOCK)
            kw = k_ref[pl.ds(start, 2 * BLOCK), :]
            vw = v_ref[pl.ds(start, 2 * BLOCK), :]
            bias_t = _swa_bias_of(bias_ref, n, b)
            q = qa_ref[rows, :] * SWA_SCALE
            g = ga_ref[rows, :]
            silu = g * _sigmoid(g)
            for j in range(SWA_KV_HEADS):
                qs = _swa_stack(q, j).astype(MXU_DTYPE)
                probs, _ = _swa_probs_t(qs, _swa_dup(kw, j), bias_t, _swa_sink_row(sink_ref, j))
                pairs = _swa_unstack(_mm_tn(probs.astype(MXU_DTYPE), _swa_dup(vw, j)))
                for i in range(2):
                    lanes = slice(128 * (2 * j + i), 128 * (2 * j + i + 1))
                    attn_ref[rows, lanes] = pairs[i]
                    cat_ref[rows, lanes] = (pairs[i] * silu[:, lanes]).astype(cat_ref.dtype)

    return pl.pallas_call(
        body, name="swa_fwd", grid=(s // tq,),
        in_specs=[pl.BlockSpec(memory_space=pltpu.SMEM), _rows(tq, 512), _rows(tq, 512),
                  _full((2, 2 * BLOCK, SWA_ROWS)), _full((s + BLOCK, 128)), _full((s + BLOCK, 128))],
        out_specs=[_rows(tq, 512), _rows(tq, 512)],
        out_shape=[jax.ShapeDtypeStruct((s, 512), F32), jax.ShapeDtypeStruct((s, 512), MXU_DTYPE)],
        compiler_params=_cparams(dimension_semantics=("arbitrary",)),
    )(sinks, qa, ga, _swa_bias(), k_pad, v_pad)


GLA_KW = GLA_HEADS * GLA_DK
GLA_VW = GLA_HEADS * GLA_DV


def _idiv(t, d):
    return t >> (d.bit_length() - 1)


def _chunk_cumsum(t, lower):
    n, w = t.shape
    r = lax.broadcasted_iota(jnp.int32, (n, n), 0)
    c = lax.broadcasted_iota(jnp.int32, (n, n), 1)
    tri = ((_idiv(r, GLA_CHUNK) == _idiv(c, GLA_CHUNK)) & ((r >= c) if lower else (r <= c))).astype(MXU_DTYPE)
    parts = _mm(tri, _split3(t))
    return (parts[:, :w] + parts[:, w:2 * w]) + parts[:, 2 * w:]


def _chunk_last(t):
    n = t.shape[0]
    return jnp.concatenate(
        [jnp.broadcast_to(t[c + GLA_CHUNK - 1:c + GLA_CHUNK, :], (GLA_CHUNK, t.shape[1]))
         for c in range(0, n, GLA_CHUNK)], axis=0)


def _head_stack(t, width):
    head = _idiv(lax.broadcasted_iota(jnp.int32, t.shape, 1), width)
    zero = jnp.zeros_like(t)
    return jnp.concatenate([jnp.where(head == h, t, zero) for h in range(GLA_HEADS)], axis=0)


def _heads_to_rows(t):
    return jnp.concatenate([t[:, GLA_DV * h:GLA_DV * (h + 1)] for h in range(GLA_HEADS)], axis=0)


def _rows_to_heads(t):
    return jnp.concatenate([t[GLA_CHUNK * h:GLA_CHUNK * (h + 1)] for h in range(GLA_HEADS)], axis=1)


def _state_by_head(t):
    srow = _idiv(lax.broadcasted_iota(jnp.int32, (GLA_VW, GLA_KW), 0), GLA_DV)
    slane = _idiv(lax.broadcasted_iota(jnp.int32, (GLA_VW, GLA_KW), 1), GLA_DK)
    return jnp.where(srow == slane, jnp.concatenate([t] * GLA_HEADS, axis=0), jnp.zeros((GLA_VW, GLA_KW), t.dtype))


def _gla_masks():
    row = lax.broadcasted_iota(jnp.int32, (GLA_CHUNK, GLA_KW), 0)
    pos = lax.broadcasted_iota(jnp.int32, (GLA_CHUNK, GLA_KW), 1) & (GLA_CHUNK - 1)
    return pos <= row, pos >= row


def _gla_fwd(qb, kb, vb, la, gb, norm_w):
    s = qb.shape[0]
    tb = min(256, s)
    ch = tb // GLA_CHUNK

    def body(qb_ref, kb_ref, vb_ref, la_ref, gb_ref, nw_ref, o_ref, cat_ref, sp_ref, st_ref):
        @pl.when(pl.program_id(0) == 0)
        def _():
            st_ref[...] = jnp.zeros_like(st_ref)

        causal, _ = _gla_masks()
        nw = nw_ref[...]
        b = _chunk_cumsum(la_ref[...], True)
        bl = _chunk_last(b)
        k = kb_ref[...]
        qd = ((qb_ref[...] * GLA_SCALE) * jnp.exp(b)).astype(MXU_DTYPE)
        ki = (k * jnp.exp(-b)).astype(MXU_DTYPE)
        ke = (k * jnp.exp(bl - b)).astype(MXU_DTYPE)
        dec = jnp.exp(bl)
        v = vb_ref[...].astype(MXU_DTYPE)
        g = gb_ref[...]
        silu = g * _sigmoid(g)
        for ci in range(ch):
            rows = slice(GLA_CHUNK * ci, GLA_CHUNK * (ci + 1))
            qds, kis, kes = (_head_stack(t[rows], GLA_DK) for t in (qd, ki, ke))
            a = jnp.where(causal, _mm_nt(qd[rows], kis), 0.0).astype(MXU_DTYPE)
            st = st_ref[...]
            sp_ref[ci] = st
            o = _mm(a, _head_stack(v[rows], GLA_DV)) + _rows_to_heads(_mm_nt(qds, st.astype(MXU_DTYPE)))
            st_ref[...] = st * dec[rows][0:1] + _mm_tn(_heads_to_rows(v[rows]), kes)
            o_ref[rows, :] = o
            for h in range(GLA_HEADS):
                lv = slice(GLA_DV * h, GLA_DV * (h + 1))
                oh = o[:, lv]
                r = lax.rsqrt(jnp.mean(oh * oh, axis=1, keepdims=True) + EPS)
                cat_ref[rows, lv] = (oh * r * nw * silu[rows, lv]).astype(cat_ref.dtype)

    return pl.pallas_call(
        body, name="gla_fwd", grid=(s // tb,),
        in_specs=[_rows(tb, 256), _rows(tb, 256), _rows(tb, 512), _rows(tb, 256), _rows(tb, 512), _full((1, 128))],
        out_specs=[_rows(tb, 512), _rows(tb, 512), pl.BlockSpec((ch, GLA_DV, 256), lambda i: (i, 0, 0))],
        out_shape=[jax.ShapeDtypeStruct((s, 512), F32), jax.ShapeDtypeStruct((s, 512), MXU_DTYPE),
                   jax.ShapeDtypeStruct((s // GLA_CHUNK, GLA_DV, 256), F32)],
        scratch_shapes=[pltpu.VMEM((GLA_DV, GLA_KW), F32)],
        compiler_params=_cparams(dimension_semantics=("arbitrary",)),
    )(qb, kb, vb, la, gb, norm_w)


def _out_ln_loss(cat_a, cat_b, w_out, x, target, ln_g, ln_b):
    s = x.shape[0]
    ts = min(512, s)
    halves = 2 if ts % 32 == 0 else 1
    th = ts // halves

    def body(ca_ref, cb_ref, w_ref, x_ref, t_ref, g_ref, b_ref,
             loss_ref, gx_ref, da_ref, db_ref, gw_ref, gln_ref):
        @pl.when(pl.program_id(0) == 0)
        def _():
            loss_ref[...] = jnp.zeros_like(loss_ref)
            gw_ref[...] = jnp.zeros_like(gw_ref)
            gln_ref[...] = jnp.zeros_like(gln_ref)

        g = g_ref[...]
        dh16s = []
        for k in range(halves):
            rows = slice(th * k, th * (k + 1))
            mix = _mm(ca_ref[rows, :], w_ref[0:512, :]) + _mm(cb_ref[rows, :], w_ref[512:1024, :])
            h = ALPHA * x_ref[rows, :] + mix
            mu = jnp.mean(h, axis=1, keepdims=True)
            hc = h - mu
            rstd = lax.rsqrt(jnp.mean(hc * hc, axis=1, keepdims=True) + EPS)
            xhat = hc * rstd
            err = xhat * g + b_ref[...] - t_ref[rows, :]
            loss_ref[...] += 0.5 * jnp.sum(jnp.mean(err * err, axis=1, keepdims=True))
            dy = err * (1.0 / D_MODEL)
            gln_ref[0:1, :] += jnp.sum(dy * xhat, axis=0, keepdims=True)
            gln_ref[1:2, :] += jnp.sum(dy, axis=0, keepdims=True)
            dxh = dy * g
            dh = rstd * (dxh - jnp.mean(dxh, axis=1, keepdims=True)
                         - xhat * jnp.mean(dxh * xhat, axis=1, keepdims=True))
            gx_ref[rows, :] = ALPHA * dh
            dh16s.append(dh.astype(MXU_DTYPE))
        for k in range(halves):
            rows = slice(th * k, th * (k + 1))
            da_ref[rows, :] = _mm_nt(dh16s[k], w_ref[0:512, :])
            db_ref[rows, :] = _mm_nt(dh16s[k], w_ref[512:1024, :])
        dh16 = jnp.concatenate(dh16s, axis=0)
        gw_ref[0:512, :] += _mm_tn(ca_ref[...], dh16)
        gw_ref[512:1024, :] += _mm_tn(cb_ref[...], dh16)

    return pl.pallas_call(
        body, name="out_ln_loss", grid=(s // ts,),
        in_specs=[_rows(ts, 512), _rows(ts, 512), _full((D_MODEL, D_MODEL)), _rows(ts, D_MODEL), _rows(ts, D_MODEL),
                  _full((1, D_MODEL)), _full((1, D_MODEL))],
        out_specs=[_full((1, 128)), _rows(ts, D_MODEL), _rows(ts, 512), _rows(ts, 512),
                   _full((D_MODEL, D_MODEL)), _full((2, D_MODEL))],
        out_shape=[jax.ShapeDtypeStruct((1, 128), F32), jax.ShapeDtypeStruct((s, D_MODEL), F32),
                   jax.ShapeDtypeStruct((s, 512), F32), jax.ShapeDtypeStruct((s, 512), F32),
                   jax.ShapeDtypeStruct((D_MODEL, D_MODEL), F32), jax.ShapeDtypeStruct((2, D_MODEL), F32)],
        compiler_params=_cparams(dimension_semantics=("arbitrary",)),
    )(cat_a, cat_b, w_out, x, target, ln_g, ln_b)


def _swa_bwd(sinks, qa, k_pad, v_pad, attn, ga, d_cat_a, rope, parts_w_out):
    s = qa.shape[0]
    sub = min(SWA_SUB, s // BLOCK)
    tq = sub * BLOCK
    nsteps = s // tq
    forward_step = min(1, nsteps - 1)

    def body(sink_ref, qa_ref, ga_ref, at_ref, dc_ref, c_ref, s1_ref, s2_ref, bias_ref, k_ref, v_ref, pout_ref,
             dq_ref, dg_ref, dk_out, dv_out, ds_ref, gout_ref, dk_ref, dv_ref, *scratch):
        n = pl.program_id(0)
        owner_sum = _OwnerSum(pout_ref, *scratch)

        @pl.when(n == 0)
        def _():
            dk_ref[...] = jnp.zeros_like(dk_ref)
            dv_ref[...] = jnp.zeros_like(dv_ref)
            ds_ref[...] = jnp.zeros_like(ds_ref)
            owner_sum.start()

        @pl.when(n == forward_step)
        def _():
            owner_sum.forward()

        @pl.when(n == nsteps - 1)
        def _():
            gout_ref[...] = owner_sum.finish()

        low = lax.broadcasted_iota(jnp.int32, (2 * BLOCK, 128), 1) < SWA_HEAD_DIM
        for b in range(sub):
            rows = slice(BLOCK * b, BLOCK * (b + 1))
            start = pl.multiple_of((n * sub + b) * BLOCK, BLOCK)
            kw = k_ref[pl.ds(start, 2 * BLOCK), :]
            vw = v_ref[pl.ds(start, 2 * BLOCK), :]
            bias_t = _swa_bias_of(bias_ref, n, b)
            q = qa_ref[rows, :] * SWA_SCALE
            g = ga_ref[rows, :]
            sg = _sigmoid(g)
            o = at_ref[rows, :]
            dc = dc_ref[rows, :]
            do = dc * (g * sg)
            dg_ref[rows, :] = (dc * o * (sg * (1.0 + g * (1.0 - sg)))).astype(dg_ref.dtype)
            od = do * o
            c, s1, s2 = c_ref[rows, :], s1_ref[rows, :], s2_ref[rows, :]
            dk, dv = [], []
            for j in range(SWA_KV_HEADS):
                kd, vd = _swa_dup(kw, j), _swa_dup(vw, j)
                qs = _swa_stack(q, j).astype(MXU_DTYPE)
                dos = _swa_stack(do, j).astype(MXU_DTYPE)
                probs, psink = _swa_probs_t(qs, kd, bias_t, _swa_sink_row(sink_ref, j))
                delta = _row_sums_as_row(_swa_stack(od, j))
                dsc = (probs * (_mm_nt(vd, dos) - delta)).astype(MXU_DTYPE)
                dsink = psink * delta
                for r in range(SWA_GROUP):
                    h = SWA_GROUP * j + r
                    ds_ref[h:h + 1, :] += jnp.zeros((1, 128), F32) - jnp.sum(dsink[:, BLOCK * r:BLOCK * (r + 1)])
                dq = _swa_unstack(_mm_tn(dsc, kd))
                for i in range(2):
                    lanes = slice(128 * (2 * j + i), 128 * (2 * j + i + 1))
                    dq_ref[rows, lanes] = _rope_t(dq[i] * SWA_SCALE, c, s1, s2).astype(dq_ref.dtype)
                dkj = _mm(dsc, qs)
                dvj = _mm(probs.astype(MXU_DTYPE), dos)
                dk.append(dkj + pltpu.roll(dkj, SWA_HEAD_DIM, 1))
                dv.append(dvj + pltpu.roll(dvj, SWA_HEAD_DIM, 1))
            dk_ref[pl.ds(start, 2 * BLOCK), :] += jnp.where(low, dk[0], dk[1])
            dv_ref[pl.ds(start, 2 * BLOCK), :] += jnp.where(low, dv[0], dv[1])

        @pl.when(n == nsteps - 1)
        def _():
            dk_out[...] = dk_ref[BLOCK:, :]
            dv_out[...] = dv_ref[BLOCK:, :]

    out_blk = parts_w_out.shape[1:]
    return pl.pallas_call(
        body, name="swa_bwd", grid=(nsteps,),
        in_specs=[pl.BlockSpec(memory_space=pltpu.SMEM)] + [_rows(tq, 512)] * 4 + [_rows(tq, 128)] * 3
        + [_full((2, 2 * BLOCK, SWA_ROWS))] + [_full((s + BLOCK, 128))] * 2 + [pl.BlockSpec(memory_space=pl.ANY)],
        out_specs=[_rows(tq, 512), _rows(tq, 512), _full((s, 128)), _full((s, 128)),
                   _full((SWA_Q_HEADS, 128)), _full(out_blk)],
        out_shape=[jax.ShapeDtypeStruct((s, 512), MXU_DTYPE), jax.ShapeDtypeStruct((s, 512), MXU_DTYPE),
                   jax.ShapeDtypeStruct((s, 128), F32), jax.ShapeDtypeStruct((s, 128), F32),
                   jax.ShapeDtypeStruct((SWA_Q_HEADS, 128), F32), jax.ShapeDtypeStruct(out_blk, F32)],
        scratch_shapes=[pltpu.VMEM((s + BLOCK, 128), F32)] * 2 + _OwnerSum.scratch(out_blk),
        compiler_params=_cparams(dimension_semantics=("arbitrary",)),
    )(sinks, qa, ga, attn, d_cat_a, *rope, _swa_bias(), k_pad, v_pad, parts_w_out)


def _gla_bwd(qb, kb, vb, la, oms, gb, o, sprev, d_cat_b, rb, wg, norm_w):
    s = qb.shape[0]
    tb = min(512, s)
    ch = tb // GLA_CHUNK
    nb = s // tb

    def body(qb_ref, kb_ref, vb_ref, la_ref, oms_ref, gb_ref, o_ref, sp_ref, dc_ref, rb_ref, wg_ref, nw_ref,
             dq_ref, dk_ref, dv_ref, dg_ref, dr_ref, gwg_ref, gbg_ref, gnw_ref, dst_ref):
        @pl.when(pl.program_id(0) == 0)
        def _():
            dst_ref[...] = jnp.zeros_like(dst_ref)
            gwg_ref[...] = jnp.zeros_like(gwg_ref)
            gbg_ref[...] = jnp.zeros_like(gbg_ref)
            gnw_ref[...] = jnp.zeros_like(gnw_ref)

        causal, causal_t = _gla_masks()
        nw = nw_ref[...]
        b = _chunk_cumsum(la_ref[...], True)
        bl = _chunk_last(b)
        eb, enb, ee, dec = jnp.exp(b), jnp.exp(-b), jnp.exp(bl - b), jnp.exp(bl)
        k = kb_ref[...]
        qd = (qb_ref[...] * GLA_SCALE) * eb
        ki = k * enb
        ke = k * ee
        qd16, ki16, ke16 = qd.astype(MXU_DTYPE), ki.astype(MXU_DTYPE), ke.astype(MXU_DTYPE)
        v16 = vb_ref[...].astype(MXU_DTYPE)

        g = gb_ref[...]
        sg = _sigmoid(g)
        silu = g * sg
        dsilu = sg * (1.0 + g * (1.0 - sg))
        gnw = jnp.zeros((1, GLA_DV), F32)
        do = []
        for h in range(GLA_HEADS):
            lv = slice(GLA_DV * h, GLA_DV * (h + 1))
            oh = o_ref[:, lv]
            dch = dc_ref[:, lv]
            r = lax.rsqrt(jnp.mean(oh * oh, axis=1, keepdims=True) + EPS)
            d_on = dch * silu[:, lv]
            dg_ref[:, lv] = (dch * (oh * r * nw) * dsilu[:, lv]).astype(dg_ref.dtype)
            gnw += jnp.sum(d_on * oh * r, axis=0, keepdims=True)
            u = d_on * nw
            do.append(r * u - oh * (r * r * r) * jnp.mean(u * oh, axis=1, keepdims=True))
        gnw_ref[...] += gnw
        do16 = jnp.concatenate(do, axis=1).astype(MXU_DTYPE)

        db, dbl = [None] * ch, [None] * ch
        for ci in reversed(range(ch)):
            rows = slice(GLA_CHUNK * ci, GLA_CHUNK * (ci + 1))
            qds, kis, kes = (_head_stack(t[rows], GLA_DK) for t in (qd16, ki16, ke16))
            vs, dos = _head_stack(v16[rows], GLA_DV), _head_stack(do16[rows], GLA_DV)
            a = jnp.where(causal, _mm_nt(qd16[rows], kis), 0.0).astype(MXU_DTYPE)
            at = jnp.where(causal_t, _mm_nt(ki16[rows], qds), 0.0).astype(MXU_DTYPE)
            da = jnp.where(causal, _mm_nt(do16[rows], vs), 0.0).astype(MXU_DTYPE)
            dat = jnp.where(causal_t, _mm_nt(v16[rows], dos), 0.0).astype(MXU_DTYPE)
            st = sp_ref[ci]
            dst = dst_ref[...]
            dst16 = dst.astype(MXU_DTYPE)
            dv = _mm(at, dos) + _rows_to_heads(_mm_nt(kes, dst16))
            dqd = _mm(da, kis) + _mm(do16[rows], _state_by_head(st.astype(MXU_DTYPE)))
            dki = _mm(dat, qds)
            dke = _mm(v16[rows], _state_by_head(dst16))
            ddec = jnp.sum(dst * st, axis=0, keepdims=True)
            decc = dec[rows][0:1]
            dst_ref[...] = _mm_tn(_heads_to_rows(do16[rows]), qds) + dst * decc
            dq_ref[rows, :] = (dqd * eb[rows] * GLA_SCALE).astype(dq_ref.dtype)
            dk_ref[rows, :] = (dki * enb[rows] + dke * ee[rows]).astype(dk_ref.dtype)
            dv_ref[rows, :] = dv.astype(dv_ref.dtype)
            dke_ke = dke * ke[rows]
            db[ci] = dqd * qd[rows] - dki * ki[rows] - dke_ke
            dbl[ci] = jnp.broadcast_to(jnp.sum(dke_ke, axis=0, keepdims=True) + ddec * decc, (GLA_CHUNK, GLA_KW))

        dla = _chunk_cumsum(jnp.concatenate(db, axis=0), False) + jnp.concatenate(dbl, axis=0)
        dlogit = dla * oms_ref[...] * (1.0 / GLA_TAU)
        dl16 = dlogit.astype(MXU_DTYPE)
        gbg_ref[...] += jnp.sum(dlogit, axis=0, keepdims=True)
        gwg_ref[...] += _mm_tn(rb_ref[...].astype(MXU_DTYPE), dl16)
        dr_ref[...] = _mm_nt(dl16, wg_ref[...]).astype(dr_ref.dtype)

    def rev(width):
        return pl.BlockSpec((tb, width), lambda i: (nb - 1 - i, 0))

    return pl.pallas_call(
        body, name="gla_bwd", grid=(nb,),
        in_specs=[rev(256), rev(256), rev(512), rev(256), rev(256), rev(512), rev(512),
                  pl.BlockSpec((ch, GLA_DV, 256), lambda i: (nb - 1 - i, 0, 0)), rev(512), rev(GLA_RANK),
                  _full((GLA_RANK, 256)), _full((1, 128))],
        out_specs=[rev(256), rev(256), rev(512), rev(512), rev(GLA_RANK),
                   _full((GLA_RANK, 256)), _full((1, 256)), _full((1, 128))],
        out_shape=[jax.ShapeDtypeStruct((s, 256), MXU_DTYPE), jax.ShapeDtypeStruct((s, 256), MXU_DTYPE),
                   jax.ShapeDtypeStruct((s, 512), MXU_DTYPE), jax.ShapeDtypeStruct((s, 512), MXU_DTYPE),
                   jax.ShapeDtypeStruct((s, GLA_RANK), MXU_DTYPE), jax.ShapeDtypeStruct((GLA_RANK, 256), F32),
                   jax.ShapeDtypeStruct((1, 256), F32), jax.ShapeDtypeStruct((1, 128), F32)],
        scratch_shapes=[pltpu.VMEM((GLA_DV, GLA_KW), F32)],
        compiler_params=_cparams(dimension_semantics=("arbitrary",)),
    )(qb, kb, vb, la, oms, gb, o, sprev, d_cat_b, rb, wg, norm_w)


def _dproj_tiles(piece_refs, rope_refs, members=(0, 1, 3, 4, 5, 6, 7, 8)):
    for i in members:
        if i == 1:
            dk = _rope_t(piece_refs[1][...], *(r[...] for r in rope_refs))
            yield OFF[1], OFF[3], jnp.concatenate([dk, piece_refs[2][...]], axis=1).astype(MXU_DTYPE)
        else:
            yield OFF[i], OFF[i + 1], piece_refs[i][...].astype(MXU_DTYPE)


def _in_proj_bwd_x(gx0, pieces, w_in, rope):
    s = gx0.shape[0]
    ts = min(512, s)
    widths = [OFF[i + 1] - OFF[i] for i in range(9)]

    def body(gx0_ref, *refs):
        w_ref, gx_ref = refs[12:]
        acc = gx0_ref[...]
        for lo, hi, t16 in _dproj_tiles(refs[:9], refs[9:12]):
            acc += _mm(t16, w_ref[lo:hi, :])
        gx_ref[...] = acc

    return pl.pallas_call(
        body, name="in_proj_bwd_x", grid=(s // ts,),
        in_specs=[_rows(ts, D_MODEL)] + [_rows(ts, w) for w in widths] + [_rows(ts, 128)] * 3
        + [_full((D_IN_PROJ, D_MODEL))],
        out_specs=_rows(ts, D_MODEL),
        out_shape=jax.ShapeDtypeStruct((s, D_MODEL), F32),
        compiler_params=_cparams(dimension_semantics=("arbitrary",)),
    )(gx0, *pieces, *rope, w_in)


GW_PASSES = 2
GW_EVENT_STEPS = (0, 2, 4, 6)


def _in_proj_bwd_w(x, pieces, rope, parts_wg, g_ln, g_bg, g_nw, g_sinks, loss):
    s = x.shape[0]
    ts = min(1024, s)
    nt = s // ts
    n_pass = GW_PASSES
    n_steps = n_pass * nt
    cw = D_MODEL // n_pass
    n_chips = N_DEV // 2
    blk = (D_IN_SHARD, cw)

    def body(x_hbm, *refs):
        piece_refs, rope_refs = refs[:9], refs[9:12]
        pwg_ref, gln_ref, gbg_ref, gnw_ref, gsk_ref, loss_ref, gin_ref, rwg_ref, rsm_ref = refs[12:21]
        (acc_ref, stage_ref, sib_ref, snd_ref, rcv_ref, mine_ref, sm_ref, xs_ref,
         d2d_send, d2d_recv, ici_send, ici_recv, out_sems, sm_send, sm_recv, sm_loc, x_sems) = refs[21:]
        p, t = pl.program_id(0), pl.program_id(1)
        step = p * nt + t
        tile = pl.ds(pl.multiple_of(t * ts, ts), ts)

        def x_load(k, q):
            rows = pl.ds(pl.multiple_of(k * ts, ts), ts)
            return pltpu.make_async_copy(x_hbm.at[rows, pl.ds(q * cw, cw)], xs_ref.at[q, rows, :],
                                         x_sems.at[q * nt + k])
        x_, y_, c = _mesh_pos()
        me, mychip, sibling = 4 * x_ + 2 * y_ + c, 2 * x_ + y_, (x_, y_, 1 - c)
        small_dsts = (rwg_ref, rsm_ref)

        def small_src(a, block):
            return pwg_ref.at[block] if a == 0 else sm_ref

        def small_copy(k, a, src_block, dst_block, peer):
            i = 2 * (k - 1) + a
            return pltpu.make_async_remote_copy(
                src_ref=small_src(a, src_block), dst_ref=small_dsts[a].at[dst_block], send_sem=sm_send.at[i],
                recv_sem=sm_recv.at[i], device_id=peer, device_id_type=pl.DeviceIdType.MESH)

        def small_local(a):
            return pltpu.make_async_copy(small_src(a, me), small_dsts[a].at[me], sm_loc.at[a])

        @pl.when(step == 0)
        def _():
            for q in range(n_pass):
                for k in range(nt):
                    x_load(k, q).start()
            acc_ref[...] = jnp.zeros_like(acc_ref)
            sm_ref[...] = jnp.zeros_like(sm_ref)
            for r in range(D_MODEL // 128):
                sm_ref[r:r + 1, :] = gln_ref[0:1, 128 * r:128 * (r + 1)]
                sm_ref[8 + r:9 + r, :] = gln_ref[1:2, 128 * r:128 * (r + 1)]
            for r in range(2):
                sm_ref[16 + r:17 + r, :] = gbg_ref[0:1, 128 * r:128 * (r + 1)]
            sm_ref[24:25, :] = gnw_ref[...]
            diag = (lax.broadcasted_iota(jnp.int32, gsk_ref.shape, 0)
                    == lax.broadcasted_iota(jnp.int32, gsk_ref.shape, 1))
            sm_ref[32:33, :] = jnp.sum(jnp.where(diag, gsk_ref[...], 0.0), axis=0, keepdims=True)
            sm_ref[40:41, :] = loss_ref[...]
            for a in range(2):
                small_local(a).start()
            for k in range(1, N_DEV):
                peer, pidx = _peer(k, x_, y_, c)
                for a in range(2):
                    small_copy(k, a, pidx, me, peer).start()

        for q in range(n_pass):
            @pl.when(p == q)
            def _(q=q):
                x_load(t, q).wait()

        xb = xs_ref[p, tile, :]
        for lo, hi, t16 in _dproj_tiles(piece_refs, rope_refs):
            acc_ref[p, lo:hi, :] += _mm_tn(t16, xb)

        def block_rows(q, j):
            return acc_ref[q, D_IN_SHARD * j:D_IN_SHARD * (j + 1), :]

        def d2d(q):
            return pltpu.make_async_remote_copy(
                src_ref=stage_ref.at[q % 2], dst_ref=sib_ref.at[q], send_sem=d2d_send.at[q],
                recv_sem=d2d_recv.at[q], device_id=sibling, device_id_type=pl.DeviceIdType.MESH)

        def ici(q, slot, owner):
            i = 3 * q + slot
            return pltpu.make_async_remote_copy(
                src_ref=snd_ref.at[q, slot], dst_ref=rcv_ref.at[q, slot], send_sem=ici_send.at[i],
                recv_sem=ici_recv.at[i], device_id=owner, device_id_type=pl.DeviceIdType.MESH)

        def out_copy(q):
            return pltpu.make_async_copy(mine_ref.at[q], gin_ref.at[:, pl.ds(q * cw, cw)], out_sems.at[q])

        first = (jnp.where(c == 0, 1 - x_, x_), jnp.where(c == 0, y_, 1 - y_))
        second = (jnp.where(c == 0, x_, 1 - x_), jnp.where(c == 0, 1 - y_, y_))

        def chip_of(pos):
            return 2 * pos[0] + pos[1]

        def to_sibling(q):
            if q >= 2:
                d2d(q - 2).wait_send()
            for cc in range(2):
                @pl.when(c == cc)
                def _(cc=cc):
                    for k in range(n_chips):
                        stage_ref[q % 2, k] = block_rows(q, 2 * k + 1 - cc)
            d2d(q).start()

        def chip_sums_leave(q):
            d2d(q).wait_recv()
            for cc in range(2):
                @pl.when(c == cc)
                def _(cc=cc):
                    for k in range(n_chips):
                        sib_ref[q, k] = block_rows(q, 2 * k + cc) + sib_ref[q, k]
            snd_ref[q, 2] = sib_ref[q, chip_of((1 - x_, 1 - y_))].astype(snd_ref.dtype)
            ici(q, 2, (*first, c)).start()
            snd_ref[q, 0] = sib_ref[q, chip_of(first)].astype(snd_ref.dtype)
            ici(q, 0, (*first, c)).start()

        def combined_sum_leaves(q):
            ici(q, 2, sibling).wait_recv()
            snd_ref[q, 1] = (sib_ref[q, chip_of(second)] + rcv_ref[q, 2].astype(F32)).astype(snd_ref.dtype)
            ici(q, 1, (*second, c)).start()

        def owner_total(q):
            total = sib_ref[q, mychip]
            for slot in range(2):
                ici(q, slot, sibling).wait_recv()
                total = total + rcv_ref[q, slot].astype(F32)
            mine_ref[q] = total
            out_copy(q).start()

        stages = (to_sibling, chip_sums_leave, combined_sum_leaves, owner_total)
        events = sorted((min((q + 1) * nt - 1 + GW_EVENT_STEPS[i], n_steps - 1), i > 0, q, i)
                        for q in range(n_pass) for i in range(len(stages)))
        for at_step, _, q, i in events:
            @pl.when(step == at_step)
            def _(q=q, i=i):
                stages[i](q)

        @pl.when(step == n_steps - 1)
        def _():
            for k in range(1, N_DEV):
                peer, pidx = _peer(k, x_, y_, c)
                for a in range(2):
                    small_copy(k, a, me, pidx, peer).wait_recv()
            for k in range(1, N_DEV):
                peer, pidx = _peer(k, x_, y_, c)
                for a in range(2):
                    small_copy(k, a, pidx, me, peer).wait_send()
            for a in range(2):
                small_local(a).wait()
            for q in range(max(n_pass - 2, 0), n_pass):
                d2d(q).wait_send()
            for q in range(n_pass):
                for slot in range(3):
                    ici(q, slot, sibling).wait_send()
            for q in range(n_pass):
                out_copy(q).wait()

    widths = [OFF[i + 1] - OFF[i] for i in range(9)]
    hbm = pl.BlockSpec(memory_space=pl.ANY)
    vmem = pl.BlockSpec(memory_space=pltpu.VMEM)

    def token_tile(width):
        return pl.BlockSpec((ts, width), lambda p, t: (t, 0))

    return pl.pallas_call(
        body, name="in_proj_bwd_w", grid=(n_pass, nt),
        in_specs=[hbm] + [token_tile(w) for w in widths] + [token_tile(128)] * 3 + [hbm] + [vmem] * 5,
        out_specs=[hbm, hbm, hbm],
        out_shape=[jax.ShapeDtypeStruct((D_IN_SHARD, D_MODEL), F32),
                   jax.ShapeDtypeStruct((N_DEV,) + parts_wg.shape[1:], F32),
                   jax.ShapeDtypeStruct((N_DEV, SMALL_ROWS, 128), F32)],
        scratch_shapes=[pltpu.VMEM((n_pass, D_IN_PROJ, cw), F32), pltpu.VMEM((2, n_chips) + blk, F32),
                        pltpu.VMEM((n_pass, n_chips) + blk, F32), pltpu.VMEM((n_pass, 3) + blk, MXU_DTYPE),
                        pltpu.VMEM((n_pass, 3) + blk, MXU_DTYPE), pltpu.VMEM((n_pass,) + blk, F32),
                        pltpu.VMEM((SMALL_ROWS, 128), F32), pltpu.VMEM((n_pass, s, cw), MXU_DTYPE),
                        pltpu.SemaphoreType.DMA((n_pass,)), pltpu.SemaphoreType.DMA((n_pass,)),
                        pltpu.SemaphoreType.DMA((3 * n_pass,)), pltpu.SemaphoreType.DMA((3 * n_pass,)),
                        pltpu.SemaphoreType.DMA((n_pass,)),
                        pltpu.SemaphoreType.DMA((2 * (N_DEV - 1),)), pltpu.SemaphoreType.DMA((2 * (N_DEV - 1),)),
                        pltpu.SemaphoreType.DMA((2,)), pltpu.SemaphoreType.DMA((n_pass * nt,))],
        compiler_params=_cparams(dimension_semantics=("arbitrary", "arbitrary")),
    )(x, *pieces, *rope, parts_wg, g_ln, g_bg, g_nw, g_sinks, loss)


def _local_step(x, positions, w_in_t, wg_s, b_gate, sinks, norm_w, w_out_s, ln_g, ln_b, target):
    qa, k_pad, v_pad, ga, qb, kb, vb, gb, rb, la, oms, *rope, x16, w_in, wg, w_out = _in_proj(
        x, w_in_t, wg_s, b_gate, _rope_angles(positions), w_out_s)
    attn, cat_a = _swa_fwd(sinks, qa, k_pad, v_pad, ga)
    o, cat_b, sprev = _gla_fwd(qb, kb, vb, la, gb, norm_w)
    loss, gx0, d_cat_a, d_cat_b, g_w_out, g_ln = _out_ln_loss(cat_a, cat_b, w_out, x, target, ln_g, ln_b)
    parts_w_out = g_w_out.reshape(N_DEV, D_OUT_SHARD, D_MODEL)
    dqa, dga, dka, dva, g_sinks, g_out = _swa_bwd(sinks, qa, k_pad, v_pad, attn, ga, d_cat_a, rope, parts_w_out)
    dqb, dkb, dvb, dgb, drb, g_wg, g_bg, g_nw = _gla_bwd(qb, kb, vb, la, oms, gb, o, sprev, d_cat_b, rb, wg, norm_w)
    pieces = (dqa, dka, dva, dga, dqb, dkb, dvb, dgb, drb)
    grad_x = _in_proj_bwd_x(gx0, pieces, w_in, rope)
    parts_wg = jnp.transpose(g_wg.reshape(GLA_RANK, N_DEV, 32), (1, 0, 2))
    g_in, r_wg, r_small = _in_proj_bwd_w(x16, pieces, rope, parts_wg, g_ln, g_bg, g_nw, g_sinks, loss)
    return grad_x, g_in, g_out, r_wg, r_small


def _mesh_pos():
    return lax.axis_index("x"), lax.axis_index("y"), lax.axis_index("c")


def _peer(k, x, y, c):
    px = (1 - x) if k & 4 else x
    py = (1 - y) if k & 2 else y
    pc = (1 - c) if k & 1 else c
    return (px, py, pc), 4 * px + 2 * py + pc


def _other_chips(x, y):
    return [(1 - x, y), (x, 1 - y), (1 - x, 1 - y)]


def _shard_view(t):
    return jnp.transpose(t, (2, 0, 1))


class _BlockGather:
    def __init__(self, slots, send_sems, recv_sems):
        self.slots, self.send_sems, self.recv_sems = slots, send_sems, recv_sems
        x, y, c = _mesh_pos()
        self.c, self.me, self.sibling = c, 4 * x + 2 * y + c, (x, y, 1 - c)
        first = (jnp.where(c == 0, 1 - x, x), jnp.where(c == 0, y, 1 - y))
        second = (jnp.where(c == 0, x, 1 - x), jnp.where(c == 0, 1 - y, y))
        self.chips = [first, second, (1 - x, 1 - y)]

    @staticmethod
    def scratch():
        return [pltpu.SemaphoreType.DMA((N_DEV - 1,)), pltpu.SemaphoreType.DMA((N_DEV - 1,))]

    def _copy(self, k, block, to):
        return pltpu.make_async_remote_copy(
            src_ref=self.slots.at[block], dst_ref=self.slots.at[block], send_sem=self.send_sems.at[k],
            recv_sem=self.recv_sems.at[k], device_id=to, device_id_type=pl.DeviceIdType.MESH)

    def _block(self, j, c):
        cx, cy = self.chips[j]
        return 4 * cx + 2 * cy + c

    def _dev(self, j):
        return (*self.chips[j], self.c)

    def start(self):
        self._copy(1, self.me, self._dev(0)).start()
        self._copy(2, self.me, self._dev(1)).start()
        self._copy(0, self.me, self.sibling).start()

    def forward(self):
        self._copy(1, self._block(0, self.c), self.sibling).wait_recv()
        self._copy(3, self._block(0, self.c), self._dev(1)).start()
        self._copy(4, self._block(0, self.c), self.sibling).start()
        self._copy(2, self._block(1, self.c), self.sibling).wait_recv()
        self._copy(5, self._block(1, self.c), self.sibling).start()

    def forward_far(self):
        self._copy(3, self._block(2, self.c), self.sibling).wait_recv()
        self._copy(6, self._block(2, self.c), self.sibling).start()

    def finish(self):
        for k in (0, 4, 5, 6):
            self._copy(k, self.me, self.sibling).wait_recv()
        for k in range(N_DEV - 1):
            self._copy(k, self.me, self.sibling).wait_send()


class _OwnerSum:
    def __init__(self, parts, own, sib, snd, rcv, loc_sems, d2d_send, d2d_recv, ici_send, ici_recv):
        self.parts, self.own, self.sib, self.snd, self.rcv = parts, own, sib, snd, rcv
        self.sems = (loc_sems, d2d_send, d2d_recv, ici_send, ici_recv)
        x, y, c = _mesh_pos()
        self.c, self.sibling = c, (x, y, 1 - c)
        self.chips = [(x, y)] + _other_chips(x, y)

    @staticmethod
    def scratch(block):
        return [pltpu.VMEM((4,) + block, F32), pltpu.VMEM((4,) + block, F32),
                pltpu.VMEM((3,) + block, MXU_DTYPE), pltpu.VMEM((3,) + block, MXU_DTYPE),
                pltpu.SemaphoreType.DMA((4,)), pltpu.SemaphoreType.DMA((4,)), pltpu.SemaphoreType.DMA((4,)),
                pltpu.SemaphoreType.DMA((3,)), pltpu.SemaphoreType.DMA((3,))]

    def _local(self, r):
        cx, cy = self.chips[r]
        return pltpu.make_async_copy(self.parts.at[4 * cx + 2 * cy + self.c], self.own.at[r], self.sems[0].at[r])

    def _d2d(self, r):
        cx, cy = self.chips[r]
        return pltpu.make_async_remote_copy(
            src_ref=self.parts.at[4 * cx + 2 * cy + (1 - self.c)], dst_ref=self.sib.at[r], send_sem=self.sems[1].at[r],
            recv_sem=self.sems[2].at[r], device_id=self.sibling, device_id_type=pl.DeviceIdType.MESH)

    def _ici(self, r):
        cx, cy = self.chips[r]
        return pltpu.make_async_remote_copy(
            src_ref=self.snd.at[r - 1], dst_ref=self.rcv.at[r - 1], send_sem=self.sems[3].at[r - 1],
            recv_sem=self.sems[4].at[r - 1], device_id=(cx, cy, self.c), device_id_type=pl.DeviceIdType.MESH)

    def start(self):
        for r in (1, 2, 3, 0):
            self._local(r).start()
            self._d2d(r).start()

    def forward(self):
        for r in (1, 2, 3):
            self._local(r).wait()
            self._d2d(r).wait_recv()
            self.snd[r - 1] = (self.own[r] + self.sib[r]).astype(self.snd.dtype)
            self._ici(r).start()

    def finish(self):
        self._local(0).wait()
        self._d2d(0).wait_recv()
        acc = self.own[0] + self.sib[0]
        for r in (1, 2, 3):
            self._ici(r).wait_recv()
            acc = acc + self.rcv[r - 1].astype(F32)
        for r in range(4):
            self._d2d(r).wait_send()
        for r in (1, 2, 3):
            self._ici(r).wait_send()
        return acc


SMALL_ROWS = 48


def _adamw_math(g, w, m, v):
    nm = ADAM_B1 * m + (1.0 - ADAM_B1) * g
    nv = ADAM_B2 * v + (1.0 - ADAM_B2) * (g * g)
    m_hat = nm / (1.0 - ADAM_B1 ** ADAM_STEP)
    v_hat = nv / (1.0 - ADAM_B2 ** ADAM_STEP)
    return -ADAM_LR * (m_hat / (jnp.sqrt(v_hat) + ADAM_EPS) + ADAM_WD * w), nm, nv


def _adamw_shard_view(g, w, m, v):
    rows, width = g.shape
    parts = 3
    pr = rows // parts
    assert pr * parts == rows

    def body(g_ref, w_hbm, m_hbm, v_hbm, g_out, d_out, nm_out, nv_out, bufs, outs, sems):
        def load(k, i, src):
            return pltpu.make_async_copy(src.at[pl.ds(pr * k, pr), 0, :], bufs.at[k, i], sems.at[7 * k + i])

        def store(k, i, dst):
            return pltpu.make_async_copy(outs.at[k, i], dst.at[pl.ds(pr * k, pr), 0, :], sems.at[7 * k + 3 + i])

        for k in range(parts):
            for i, src in enumerate((w_hbm, m_hbm, v_hbm)):
                load(k, i, src).start()
        for k in range(parts):
            for i, src in enumerate((w_hbm, m_hbm, v_hbm)):
                load(k, i, src).wait()
            gk = g_ref[pr * k:pr * (k + 1), :]
            outs[k, 0] = gk
            outs[k, 1], outs[k, 2], outs[k, 3] = _adamw_math(gk, bufs[k, 0], bufs[k, 1], bufs[k, 2])
            for i, dst in enumerate((g_out, d_out, nm_out, nv_out)):
                store(k, i, dst).start()
        for k in range(parts):
            for i, dst in enumerate((g_out, d_out, nm_out, nv_out)):
                store(k, i, dst).wait()

    hbm = pl.BlockSpec(memory_space=pl.ANY)
    return pl.pallas_call(
        body, name="adamw_w_in",
        in_specs=[pl.BlockSpec(memory_space=pltpu.VMEM), hbm, hbm, hbm], out_specs=[hbm] * 4,
        out_shape=[jax.ShapeDtypeStruct((rows, 1, width), F32)] * 4,
        scratch_shapes=[pltpu.VMEM((parts, 3, pr, width), F32), pltpu.VMEM((parts, 4, pr, width), F32),
                        pltpu.SemaphoreType.DMA((7 * parts,))],
        compiler_params=_cparams(),
    )(g, w, m, v)


def _adamw_vectors(r_small, r_wg, g_out, params):
    n_par = len(params)

    def body(rsm_ref, rwg_ref, gout_ref, *refs):
        ins, outs = refs[:3 * n_par], refs[3 * n_par:]
        g = rsm_ref[0]
        gwg = rwg_ref[0]
        for j in range(1, N_DEV):
            g = g + rsm_ref[j]
            gwg = gwg + rwg_ref[j]
        outs[4 * n_par][...] = g[40:41]
        grads = [gwg,
                 jnp.concatenate([g[r:r + 1] for r in range(0, 8)], axis=1),
                 jnp.concatenate([g[r:r + 1] for r in range(8, 16)], axis=1),
                 jnp.concatenate([g[16:17], g[17:18]], axis=1),
                 g[24:25],
                 g[32:33, 0:SWA_Q_HEADS],
                 gout_ref[...]]
        for p, gp in enumerate(grads):
            w_ref, m_ref, v_ref = ins[3 * p:3 * p + 3]
            outs[4 * p][...] = gp
            outs[4 * p + 1][...], outs[4 * p + 2][...], outs[4 * p + 3][...] = _adamw_math(
                gp, w_ref[...], m_ref[...], v_ref[...])

    vmem = pl.BlockSpec(memory_space=pltpu.VMEM)
    flat = [t for wmv in params for t in wmv]
    return pl.pallas_call(
        body, name="adamw_vectors",
        in_specs=[vmem] * (3 + len(flat)), out_specs=[vmem] * (4 * n_par + 1),
        out_shape=[jax.ShapeDtypeStruct(wmv[0].shape, F32) for wmv in params for _ in range(4)]
        + [jax.ShapeDtypeStruct((1, 128), F32)],
        compiler_params=_cparams(),
    )(r_small, r_wg, g_out, *flat)


def kernel(x, positions, w_in, gla_w_gate_up, gla_b_gate, attn_sinks, gla_norm_w, w_out, ln_g, ln_b, loss_target, m_w_in, m_gla_w_gate_up, m_gla_b_gate, m_attn_sinks, m_gla_norm_w, m_w_out, m_ln_g, m_ln_b, v_w_in, v_gla_w_gate_up, v_gla_b_gate, v_attn_sinks, v_gla_norm_w, v_w_out, v_ln_g, v_ln_b):
    grad_x, g_in, g_out, r_wg, r_small = _local_step(
        x[0], positions[0], _shard_view(w_in), gla_w_gate_up[0], gla_b_gate, attn_sinks[0], gla_norm_w, w_out[0],
        ln_g, ln_b, loss_target[0])

    upd_in = _adamw_shard_view(g_in, _shard_view(w_in), _shard_view(m_w_in), _shard_view(v_w_in))
    upd_in = [jnp.transpose(t, (1, 2, 0)) for t in upd_in]
    vec = _adamw_vectors(r_small, r_wg, g_out, [
        (gla_w_gate_up[0], m_gla_w_gate_up[0], v_gla_w_gate_up[0]), (ln_g, m_ln_g, v_ln_g), (ln_b, m_ln_b, v_ln_b),
        (gla_b_gate, m_gla_b_gate, v_gla_b_gate), (gla_norm_w, m_gla_norm_w, v_gla_norm_w),
        (attn_sinks, m_attn_sinks, v_attn_sinks), (w_out[0], m_w_out[0], v_w_out[0])])

    outs = [vec[28][0, 0], grad_x[None]]
    for kind in range(4):
        u_wg, u_ln_g, u_ln_b, u_bg, u_nw, u_sinks, u_out = (vec[4 * p + kind] for p in range(7))
        outs += [upd_in[kind], u_wg[None], u_bg, u_sinks, u_nw, u_out[None], u_ln_g, u_ln_b]
    return tuple(outs)
```

```python
import jax
import jax.numpy as jnp
from jax import lax
from jax.experimental import pallas as pl
from jax.experimental.pallas import tpu as pltpu

F32 = jnp.float32
MXU_DTYPE = jnp.bfloat16

N_DEV = 8
D_MODEL = 1024
SWA_Q_HEADS = 8
SWA_KV_HEADS = 2
SWA_GROUP = 4
SWA_HEAD_DIM = 64
BLOCK = 128
ROPE_THETA = 500000.0
ROT_DIM = 16
GLA_HEADS = 4
GLA_DK = 64
GLA_DV = 128
GLA_RANK = 16
GLA_TAU = 16.0
GLA_CHUNK = 64
D_IN_PROJ = 2832
D_IN_SHARD = D_IN_PROJ // N_DEV
D_OUT_SHARD = D_MODEL // N_DEV
OFF = (0, 512, 640, 768, 1280, 1536, 1792, 2304, 2816, 2832)
EPS = 1e-5
ALPHA = 2.0 ** 0.25
SWA_SCALE = SWA_HEAD_DIM ** -0.5
GLA_SCALE = GLA_DK ** -0.5
ADAM_LR = 0.001
ADAM_B1 = 0.9
ADAM_B2 = 0.999
ADAM_EPS = 1e-08
ADAM_WD = 0.01
ADAM_STEP = 10
VMEM_LIMIT = 56 * 1024 * 1024

_NT = (((1,), (1,)), ((), ()))
_TN = (((0,), (0,)), ((), ()))


def _mm(a, b):
    return jnp.dot(a, b, preferred_element_type=F32)


def _mm_nt(a, b):
    return lax.dot_general(a, b, _NT, preferred_element_type=F32)


def _mm_tn(a, b):
    return lax.dot_general(a, b, _TN, preferred_element_type=F32)


def _sigmoid(t):
    return 1.0 / (1.0 + jnp.exp(-t))


def _cparams(**kw):
    return pltpu.CompilerParams(vmem_limit_bytes=VMEM_LIMIT, **kw)


def _full(shape):
    return pl.BlockSpec(shape, lambda *_: (0,) * len(shape))


def _rows(tile, width):
    return pl.BlockSpec((tile, width), lambda i: (i, 0))


def _rope_angles(positions):
    half = ROT_DIM // 2
    inv_freq = ROPE_THETA ** (-jnp.arange(half, dtype=F32) / half)
    ang = positions.astype(F32)[:, None] * inv_freq[None, :]
    return jnp.concatenate([jnp.cos(ang), jnp.sin(ang)], axis=1)


def _split3_parts(t):
    hi = t.astype(MXU_DTYPE)
    r1 = t - hi.astype(F32)
    mid = r1.astype(MXU_DTYPE)
    return hi, mid, (r1 - mid.astype(F32)).astype(MXU_DTYPE)


def _rope_tables(cs):
    half = ROT_DIM // 2
    i = lax.broadcasted_iota(jnp.int32, (2 * half, 3 * 128), 0)
    lane = lax.broadcasted_iota(jnp.int32, (2 * half, 3 * 128), 1)
    table, pos = _idiv(lane, 128), lane & (SWA_HEAD_DIM - 1)
    is_c = (table == 0) & (pos < ROT_DIM) & ((pos & (half - 1)) == i)
    is_s1 = (table == 1) & (pos < half) & (pos + half == i)
    is_s2 = (table == 2) & (pos >= half) & (pos < ROT_DIM) & (pos == i)
    sel = jnp.where(is_c | is_s2, 1.0, jnp.where(is_s1, -1.0, 0.0)).astype(MXU_DTYPE)
    hi, mid, lo = _split3_parts(cs)
    t = (_mm(hi, sel) + _mm(mid, sel)) + _mm(lo, sel)
    pos1 = lax.broadcasted_iota(jnp.int32, (1, 128), 1) & (SWA_HEAD_DIM - 1)
    return t[:, 0:128] + jnp.where(pos1 >= ROT_DIM, 1.0, 0.0), t[:, 128:256], t[:, 256:384]


def _rope(t, c, s1, s2):
    return t * c + pltpu.roll(t, 120, 1) * s1 + pltpu.roll(t, 8, 1) * s2


def _rope_t(g, c, s1, s2):
    return g * c + pltpu.roll(g * s1, 8, 1) + pltpu.roll(g * s2, 120, 1)


def _in_proj(x, w_in_t, wg_s, b_gate, cos_sin, w_out_s):
    s = x.shape[0]
    ts = min(512, s)
    nsteps = s // ts
    forward_step, far_step = min(3, nsteps - 1), min(5, nsteps - 1)
    widths = [OFF[i + 1] - OFF[i] for i in range(9)]

    def body(x_ref, win_hbm, wgs_ref, bg_ref, cs_ref, wos_ref,
             qa_ref, ka_ref, va_ref, ga_ref, qb_ref, kb_ref, vb_ref, gb_ref, rb_ref, la_ref, oms_ref,
             c_ref, s1_ref, s2_ref, x16_ref, w_ref, wg_ref, wout_ref,
             win_all, wg_all, wout_all, stage, stage_sem, *sems):
        xb = x_ref[...].astype(MXU_DTYPE)
        x16_ref[...] = xb
        c, s1, s2 = _rope_tables(cs_ref[...])
        c_ref[...], s1_ref[...], s2_ref[...] = c, s1, s2
        i0 = pl.program_id(0)
        gather = _BlockGather(wout_all, *sems[0:2])

        @pl.when(i0 == 0)
        def _():
            ka_ref[0:BLOCK, :] = jnp.zeros((BLOCK, 128), ka_ref.dtype)
            va_ref[0:BLOCK, :] = jnp.zeros((BLOCK, 128), va_ref.dtype)
            first = (_BlockGather(win_all, *sems[2:4]), _BlockGather(wg_all, *sems[4:6]))
            load = pltpu.make_async_copy(win_hbm.at[:, 0, :], stage, stage_sem)
            load.start()
            wout_all[gather.me] = wos_ref[...].astype(wout_all.dtype)
            wg_all[gather.me] = wgs_ref[...].astype(wg_all.dtype)
            load.wait()
            win_all[gather.me] = stage[...].astype(win_all.dtype)
            for stage_of in ("start", "forward", "forward_far", "finish"):
                for g in first:
                    getattr(g, stage_of)()
            gather.start()
            for j in range(N_DEV):
                w_ref[D_IN_SHARD * j:D_IN_SHARD * (j + 1), :] = win_all[j]
                wg_ref[:, 32 * j:32 * (j + 1)] = wg_all[j]

        @pl.when(i0 == forward_step)
        def _():
            gather.forward()

        @pl.when(i0 == far_step)
        def _():
            gather.forward_far()

        @pl.when(i0 == nsteps - 1)
        def _():
            gather.finish()
            for j in range(N_DEV):
                wout_ref[D_OUT_SHARD * j:D_OUT_SHARD * (j + 1), :] = wout_all[j]

        kv_rows = pl.ds(pl.multiple_of(BLOCK + i0 * ts, BLOCK), ts)

        def cols(i):
            return _mm_nt(xb, w_ref[OFF[i]:OFF[i + 1], :])

        qa = cols(0)
        for i in range(4):
            qa_ref[:, 128 * i:128 * (i + 1)] = _rope(qa[:, 128 * i:128 * (i + 1)], c, s1, s2).astype(qa_ref.dtype)
        kv = _mm_nt(xb, w_ref[OFF[1]:OFF[3], :])
        ka_ref[kv_rows, :] = _rope(kv[:, 0:128], c, s1, s2).astype(ka_ref.dtype)
        va_ref[kv_rows, :] = kv[:, 128:256].astype(va_ref.dtype)
        ga_ref[...] = cols(3)
        qb_ref[...] = cols(4)
        kb_ref[...] = cols(5)
        vb_ref[...] = cols(6).astype(vb_ref.dtype)
        gb_ref[...] = cols(7)
        rb = cols(8)
        rb_ref[...] = rb
        logit = _mm(rb.astype(MXU_DTYPE), wg_ref[...]) + bg_ref[...]
        e = jnp.exp(-jnp.abs(logit))
        la_ref[...] = (jnp.minimum(logit, 0.0) - jnp.log(1.0 + e)) / GLA_TAU
        oms_ref[...] = jnp.where(logit >= 0.0, e, 1.0) / (1.0 + e)

    out_shape = [jax.ShapeDtypeStruct((s + BLOCK if i in (1, 2) else s, w), MXU_DTYPE if i in (0, 1, 2, 6) else F32)
                 for i, w in enumerate(widths)]
    out_shape += [jax.ShapeDtypeStruct((s, 256), F32)] * 2 + [jax.ShapeDtypeStruct((s, 128), F32)] * 3
    out_shape += [jax.ShapeDtypeStruct((s, D_MODEL), MXU_DTYPE)]
    out_shape += [jax.ShapeDtypeStruct((D_IN_PROJ, D_MODEL), MXU_DTYPE), jax.ShapeDtypeStruct((GLA_RANK, 256), MXU_DTYPE),
                  jax.ShapeDtypeStruct((D_MODEL, D_MODEL), MXU_DTYPE)]
    return pl.pallas_call(
        body, name="in_proj", grid=(nsteps,),
        in_specs=[_rows(ts, D_MODEL), pl.BlockSpec(memory_space=pl.ANY), _full((GLA_RANK, 32)), _full((1, 256)),
                  _rows(ts, ROT_DIM), _full((D_OUT_SHARD, D_MODEL))],
        out_specs=[_full((s + BLOCK, w)) if i in (1, 2) else _rows(ts, w) for i, w in enumerate(widths)]
        + [_rows(ts, 256)] * 2 + [_rows(ts, 128)] * 3 + [_rows(ts, D_MODEL)]
        + [_full((D_IN_PROJ, D_MODEL)), _full((GLA_RANK, 256)), _full((D_MODEL, D_MODEL))],
        out_shape=out_shape,
        scratch_shapes=[pltpu.VMEM((N_DEV, D_IN_SHARD, D_MODEL), MXU_DTYPE), pltpu.VMEM((N_DEV, GLA_RANK, 32), MXU_DTYPE),
                        pltpu.VMEM((N_DEV, D_OUT_SHARD, D_MODEL), MXU_DTYPE),
                        pltpu.VMEM((D_IN_SHARD, D_MODEL), F32), pltpu.SemaphoreType.DMA]
        + 3 * _BlockGather.scratch(),
        compiler_params=_cparams(dimension_semantics=("arbitrary",)),
    )(x, w_in_t, wg_s, b_gate, cos_sin, w_out_s)


SWA_ROWS = SWA_GROUP * BLOCK


def _swa_bias():
    shape = (2, 2 * BLOCK, SWA_ROWS)
    ki = lax.broadcasted_iota(jnp.int32, shape, 1)
    qi = lax.broadcasted_iota(jnp.int32, shape, 2) & (BLOCK - 1)
    first = lax.broadcasted_iota(jnp.int32, shape, 0) == 0
    dist = qi + BLOCK - ki
    ok = (dist >= 0) & (dist < BLOCK) & (jnp.logical_not(first) | (ki >= BLOCK))
    return jnp.where(ok, 0.0, -jnp.inf).astype(F32)


SWA_SUB = 8


def _swa_bias_of(bias_ref, n, b):
    return bias_ref[jnp.minimum(n, 1)] if b == 0 else bias_ref[1]


def _swa_dup(t, j):
    t = t.astype(F32)
    low = lax.broadcasted_iota(jnp.int32, t.shape, 1) < SWA_HEAD_DIM
    keep = low if j == 0 else jnp.logical_not(low)
    return jnp.where(keep, t, pltpu.roll(t, SWA_HEAD_DIM, 1)).astype(MXU_DTYPE)


def _swa_stack(t, j):
    low = lax.broadcasted_iota(jnp.int32, (BLOCK, 128), 1) < SWA_HEAD_DIM
    zero = jnp.zeros((BLOCK, 128), t.dtype)
    blocks = []
    for p in (2 * j, 2 * j + 1):
        tp = t[:, 128 * p:128 * (p + 1)]
        blocks += [jnp.where(low, tp, zero), jnp.where(low, zero, tp)]
    return jnp.concatenate(blocks, axis=0)


def _swa_unstack(t):
    low = lax.broadcasted_iota(jnp.int32, (BLOCK, 128), 1) < SWA_HEAD_DIM
    return [jnp.where(low, t[2 * BLOCK * i:2 * BLOCK * i + BLOCK], t[2 * BLOCK * i + BLOCK:2 * BLOCK * (i + 1)])
            for i in range(2)]


def _swa_sink_row(sink_ref, j):
    lane = lax.broadcasted_iota(jnp.int32, (1, SWA_ROWS), 1)
    row = jnp.full((1, SWA_ROWS), sink_ref[SWA_GROUP * j], F32)
    for r in range(1, SWA_GROUP):
        row = jnp.where(lane >= BLOCK * r, sink_ref[SWA_GROUP * j + r], row)
    return row


def _split3(t):
    return jnp.concatenate(_split3_parts(t), axis=1)


def _row_sums_as_row(t):
    ones = jnp.ones((8, 3 * t.shape[1]), MXU_DTYPE)
    return _mm_nt(ones, _split3(t))[0:1, :]


def _swa_probs_t(qs, kd, bias_t, sink):
    sc = _mm_nt(kd, qs) + bias_t
    m = jnp.maximum(jnp.max(sc, axis=0, keepdims=True), sink)
    p = jnp.exp(sc - m)
    ps = jnp.exp(sink - m)
    rinv = 1.0 / (jnp.sum(p, axis=0, keepdims=True) + ps)
    return p * rinv, ps * rinv


def _swa_fwd(sinks, qa, k_pad, v_pad, ga):
    s = qa.shape[0]
    sub = min(SWA_SUB, s // BLOCK)
    tq = sub * BLOCK

    def body(sink_ref, qa_ref, ga_ref, bias_ref, k_ref, v_ref, attn_ref, cat_ref):
        n = pl.program_id(0)
        for b in range(sub):
            rows = slice(BLOCK * b, BLOCK * (b + 1))
            start = pl.multiple_of((n * sub + b) * BLOCK, BLOCK)
            kw = k_ref[pl.ds(start, 2 * BLOCK), :]
            vw = v_ref[pl.ds(start, 2 * BLOCK), :]
            bias_t = _swa_bias_of(bias_ref, n, b)
            q = qa_ref[rows, :] * SWA_SCALE
            g = ga_ref[rows, :]
            silu = g * _sigmoid(g)
            for j in range(SWA_KV_HEADS):
                qs = _swa_stack(q, j).astype(MXU_DTYPE)
                probs, _ = _swa_probs_t(qs, _swa_dup(kw, j), bias_t, _swa_sink_row(sink_ref, j))
                pairs = _swa_unstack(_mm_tn(probs.astype(MXU_DTYPE), _swa_dup(vw, j)))
                for i in range(2):
                    lanes = slice(128 * (2 * j + i), 128 * (2 * j + i + 1))
                    attn_ref[rows, lanes] = pairs[i]
                    cat_ref[rows, lanes] = (pairs[i] * silu[:, lanes]).astype(cat_ref.dtype)

    return pl.pallas_call(
        body, name="swa_fwd", grid=(s // tq,),
        in_specs=[pl.BlockSpec(memory_space=pltpu.SMEM), _rows(tq, 512), _rows(tq, 512),
                  _full((2, 2 * BLOCK, SWA_ROWS)), _full((s + BLOCK, 128)), _full((s + BLOCK, 128))],
        out_specs=[_rows(tq, 512), _rows(tq, 512)],
        out_shape=[jax.ShapeDtypeStruct((s, 512), F32), jax.ShapeDtypeStruct((s, 512), MXU_DTYPE)],
        compiler_params=_cparams(dimension_semantics=("arbitrary",)),
    )(sinks, qa, ga, _swa_bias(), k_pad, v_pad)


GLA_KW = GLA_HEADS * GLA_DK
GLA_VW = GLA_HEADS * GLA_DV


def _idiv(t, d):
    return t >> (d.bit_length() - 1)


def _chunk_cumsum(t, lower):
    n, w = t.shape
    r = lax.broadcasted_iota(jnp.int32, (n, n), 0)
    c = lax.broadcasted_iota(jnp.int32, (n, n), 1)
    tri = ((_idiv(r, GLA_CHUNK) == _idiv(c, GLA_CHUNK)) & ((r >= c) if lower else (r <= c))).astype(MXU_DTYPE)
    parts = _mm(tri, _split3(t))
    return (parts[:, :w] + parts[:, w:2 * w]) + parts[:, 2 * w:]


def _chunk_last(t):
    n = t.shape[0]
    return jnp.concatenate(
        [jnp.broadcast_to(t[c + GLA_CHUNK - 1:c + GLA_CHUNK, :], (GLA_CHUNK, t.shape[1]))
         for c in range(0, n, GLA_CHUNK)], axis=0)


def _head_stack(t, width):
    head = _idiv(lax.broadcasted_iota(jnp.int32, t.shape, 1), width)
    zero = jnp.zeros_like(t)
    return jnp.concatenate([jnp.where(head == h, t, zero) for h in range(GLA_HEADS)], axis=0)


def _heads_to_rows(t):
    return jnp.concatenate([t[:, GLA_DV * h:GLA_DV * (h + 1)] for h in range(GLA_HEADS)], axis=0)


def _rows_to_heads(t):
    return jnp.concatenate([t[GLA_CHUNK * h:GLA_CHUNK * (h + 1)] for h in range(GLA_HEADS)], axis=1)


def _state_by_head(t):
    srow = _idiv(lax.broadcasted_iota(jnp.int32, (GLA_VW, GLA_KW), 0), GLA_DV)
    slane = _idiv(lax.broadcasted_iota(jnp.int32, (GLA_VW, GLA_KW), 1), GLA_DK)
    return jnp.where(srow == slane, jnp.concatenate([t] * GLA_HEADS, axis=0), jnp.zeros((GLA_VW, GLA_KW), t.dtype))


def _gla_masks():
    row = lax.broadcasted_iota(jnp.int32, (GLA_CHUNK, GLA_KW), 0)
    pos = lax.broadcasted_iota(jnp.int32, (GLA_CHUNK, GLA_KW), 1) & (GLA_CHUNK - 1)
    return pos <= row, pos >= row


def _gla_fwd(qb, kb, vb, la, gb, norm_w):
    s = qb.shape[0]
    tb = min(256, s)
    ch = tb // GLA_CHUNK

    def body(qb_ref, kb_ref, vb_ref, la_ref, gb_ref, nw_ref, o_ref, cat_ref, sp_ref, st_ref):
        @pl.when(pl.program_id(0) == 0)
        def _():
            st_ref[...] = jnp.zeros_like(st_ref)

        causal, _ = _gla_masks()
        nw = nw_ref[...]
        b = _chunk_cumsum(la_ref[...], True)
        bl = _chunk_last(b)
        k = kb_ref[...]
        qd = ((qb_ref[...] * GLA_SCALE) * jnp.exp(b)).astype(MXU_DTYPE)
        ki = (k * jnp.exp(-b)).astype(MXU_DTYPE)
        ke = (k * jnp.exp(bl - b)).astype(MXU_DTYPE)
        dec = jnp.exp(bl)
        v = vb_ref[...].astype(MXU_DTYPE)
        g = gb_ref[...]
        silu = g * _sigmoid(g)
        for ci in range(ch):
            rows = slice(GLA_CHUNK * ci, GLA_CHUNK * (ci + 1))
            qds, kis, kes = (_head_stack(t[rows], GLA_DK) for t in (qd, ki, ke))
            a = jnp.where(causal, _mm_nt(qd[rows], kis), 0.0).astype(MXU_DTYPE)
            st = st_ref[...]
            sp_ref[ci] = st
            o = _mm(a, _head_stack(v[rows], GLA_DV)) + _rows_to_heads(_mm_nt(qds, st.astype(MXU_DTYPE)))
            st_ref[...] = st * dec[rows][0:1] + _mm_tn(_heads_to_rows(v[rows]), kes)
            o_ref[rows, :] = o
            for h in range(GLA_HEADS):
                lv = slice(GLA_DV * h, GLA_DV * (h + 1))
                oh = o[:, lv]
                r = lax.rsqrt(jnp.mean(oh * oh, axis=1, keepdims=True) + EPS)
                cat_ref[rows, lv] = (oh * r * nw * silu[rows, lv]).astype(cat_ref.dtype)

    return pl.pallas_call(
        body, name="gla_fwd", grid=(s // tb,),
        in_specs=[_rows(tb, 256), _rows(tb, 256), _rows(tb, 512), _rows(tb, 256), _rows(tb, 512), _full((1, 128))],
        out_specs=[_rows(tb, 512), _rows(tb, 512), pl.BlockSpec((ch, GLA_DV, 256), lambda i: (i, 0, 0))],
        out_shape=[jax.ShapeDtypeStruct((s, 512), F32), jax.ShapeDtypeStruct((s, 512), MXU_DTYPE),
                   jax.ShapeDtypeStruct((s // GLA_CHUNK, GLA_DV, 256), F32)],
        scratch_shapes=[pltpu.VMEM((GLA_DV, GLA_KW), F32)],
        compiler_params=_cparams(dimension_semantics=("arbitrary",)),
    )(qb, kb, vb, la, gb, norm_w)


def _out_ln_loss(cat_a, cat_b, w_out, x, target, ln_g, ln_b):
    s = x.shape[0]
    ts = min(512, s)
    halves = 2 if ts % 32 == 0 else 1
    th = ts // halves

    def body(ca_ref, cb_ref, w_ref, x_ref, t_ref, g_ref, b_ref,
             loss_ref, gx_ref, da_ref, db_ref, gw_ref, gln_ref):
        @pl.when(pl.program_id(0) == 0)
        def _():
            loss_ref[...] = jnp.zeros_like(loss_ref)
            gw_ref[...] = jnp.zeros_like(gw_ref)
            gln_ref[...] = jnp.zeros_like(gln_ref)

        g = g_ref[...]
        dh16s = []
        for k in range(halves):
            rows = slice(th * k, th * (k + 1))
            mix = _mm(ca_ref[rows, :], w_ref[0:512, :]) + _mm(cb_ref[rows, :], w_ref[512:1024, :])
            h = ALPHA * x_ref[rows, :] + mix
            mu = jnp.mean(h, axis=1, keepdims=True)
            hc = h - mu
            rstd = lax.rsqrt(jnp.mean(hc * hc, axis=1, keepdims=True) + EPS)
            xhat = hc * rstd
            err = xhat * g + b_ref[...] - t_ref[rows, :]
            loss_ref[...] += 0.5 * jnp.sum(jnp.mean(err * err, axis=1, keepdims=True))
            dy = err * (1.0 / D_MODEL)
            gln_ref[0:1, :] += jnp.sum(dy * xhat, axis=0, keepdims=True)
            gln_ref[1:2, :] += jnp.sum(dy, axis=0, keepdims=True)
            dxh = dy * g
            dh = rstd * (dxh - jnp.mean(dxh, axis=1, keepdims=True)
                         - xhat * jnp.mean(dxh * xhat, axis=1, keepdims=True))
            gx_ref[rows, :] = ALPHA * dh
            dh16s.append(dh.astype(MXU_DTYPE))
        for k in range(halves):
            rows = slice(th * k, th * (k + 1))
            da_ref[rows, :] = _mm_nt(dh16s[k], w_ref[0:512, :])
            db_ref[rows, :] = _mm_nt(dh16s[k], w_ref[512:1024, :])
        dh16 = jnp.concatenate(dh16s, axis=0)
        gw_ref[0:512, :] += _mm_tn(ca_ref[...], dh16)
        gw_ref[512:1024, :] += _mm_tn(cb_ref[...], dh16)

    return pl.pallas_call(
        body, name="out_ln_loss", grid=(s // ts,),
        in_specs=[_rows(ts, 512), _rows(ts, 512), _full((D_MODEL, D_MODEL)), _rows(ts, D_MODEL), _rows(ts, D_MODEL),
                  _full((1, D_MODEL)), _full((1, D_MODEL))],
        out_specs=[_full((1, 128)), _rows(ts, D_MODEL), _rows(ts, 512), _rows(ts, 512),
                   _full((D_MODEL, D_MODEL)), _full((2, D_MODEL))],
        out_shape=[jax.ShapeDtypeStruct((1, 128), F32), jax.ShapeDtypeStruct((s, D_MODEL), F32),
                   jax.ShapeDtypeStruct((s, 512), F32), jax.ShapeDtypeStruct((s, 512), F32),
                   jax.ShapeDtypeStruct((D_MODEL, D_MODEL), F32), jax.ShapeDtypeStruct((2, D_MODEL), F32)],
        compiler_params=_cparams(dimension_semantics=("arbitrary",)),
    )(cat_a, cat_b, w_out, x, target, ln_g, ln_b)


def _swa_bwd(sinks, qa, k_pad, v_pad, attn, ga, d_cat_a, rope, parts_w_out):
    s = qa.shape[0]
    sub = min(SWA_SUB, s // BLOCK)
    tq = sub * BLOCK
    nsteps = s // tq
    forward_step = min(1, nsteps - 1)

    def body(sink_ref, qa_ref, ga_ref, at_ref, dc_ref, c_ref, s1_ref, s2_ref, bias_ref, k_ref, v_ref, pout_ref,
             dq_ref, dg_ref, dk_out, dv_out, ds_ref, gout_ref, dk_ref, dv_ref, *scratch):
        n = pl.program_id(0)
        owner_sum = _OwnerSum(pout_ref, *scratch)

        @pl.when(n == 0)
        def _():
            dk_ref[...] = jnp.zeros_like(dk_ref)
            dv_ref[...] = jnp.zeros_like(dv_ref)
            ds_ref[...] = jnp.zeros_like(ds_ref)
            owner_sum.start()

        @pl.when(n == forward_step)
        def _():
            owner_sum.forward()

        @pl.when(n == nsteps - 1)
        def _():
            gout_ref[...] = owner_sum.finish()

        low = lax.broadcasted_iota(jnp.int32, (2 * BLOCK, 128), 1) < SWA_HEAD_DIM
        for b in range(sub):
            rows = slice(BLOCK * b, BLOCK * (b + 1))
            start = pl.multiple_of((n * sub + b) * BLOCK, BLOCK)
            kw = k_ref[pl.ds(start, 2 * BLOCK), :]
            vw = v_ref[pl.ds(start, 2 * BLOCK), :]
            bias_t = _swa_bias_of(bias_ref, n, b)
            q = qa_ref[rows, :] * SWA_SCALE
            g = ga_ref[rows, :]
            sg = _sigmoid(g)
            o = at_ref[rows, :]
            dc = dc_ref[rows, :]
            do = dc * (g * sg)
            dg_ref[rows, :] = (dc * o * (sg * (1.0 + g * (1.0 - sg)))).astype(dg_ref.dtype)
            od = do * o
            c, s1, s2 = c_ref[rows, :], s1_ref[rows, :], s2_ref[rows, :]
            dk, dv = [], []
            for j in range(SWA_KV_HEADS):
                kd, vd = _swa_dup(kw, j), _swa_dup(vw, j)
                qs = _swa_stack(q, j).astype(MXU_DTYPE)
                dos = _swa_stack(do, j).astype(MXU_DTYPE)
                probs, psink = _swa_probs_t(qs, kd, bias_t, _swa_sink_row(sink_ref, j))
                delta = _row_sums_as_row(_swa_stack(od, j))
                dsc = (probs * (_mm_nt(vd, dos) - delta)).astype(MXU_DTYPE)
                dsink = psink * delta
                for r in range(SWA_GROUP):
                    h = SWA_GROUP * j + r
                    ds_ref[h:h + 1, :] += jnp.zeros((1, 128), F32) - jnp.sum(dsink[:, BLOCK * r:BLOCK * (r + 1)])
                dq = _swa_unstack(_mm_tn(dsc, kd))
                for i in range(2):
                    lanes = slice(128 * (2 * j + i), 128 * (2 * j + i + 1))
                    dq_ref[rows, lanes] = _rope_t(dq[i] * SWA_SCALE, c, s1, s2).astype(dq_ref.dtype)
                dkj = _mm(dsc, qs)
                dvj = _mm(probs.astype(MXU_DTYPE), dos)
                dk.append(dkj + pltpu.roll(dkj, SWA_HEAD_DIM, 1))
                dv.append(dvj + pltpu.roll(dvj, SWA_HEAD_DIM, 1))
            dk_ref[pl.ds(start, 2 * BLOCK), :] += jnp.where(low, dk[0], dk[1])
            dv_ref[pl.ds(start, 2 * BLOCK), :] += jnp.where(low, dv[0], dv[1])

        @pl.when(n == nsteps - 1)
        def _():
            dk_out[...] = dk_ref[BLOCK:, :]
            dv_out[...] = dv_ref[BLOCK:, :]

    out_blk = parts_w_out.shape[1:]
    return pl.pallas_call(
        body, name="swa_bwd", grid=(nsteps,),
        in_specs=[pl.BlockSpec(memory_space=pltpu.SMEM)] + [_rows(tq, 512)] * 4 + [_rows(tq, 128)] * 3
        + [_full((2, 2 * BLOCK, SWA_ROWS))] + [_full((s + BLOCK, 128))] * 2 + [pl.BlockSpec(memory_space=pl.ANY)],
        out_specs=[_rows(tq, 512), _rows(tq, 512), _full((s, 128)), _full((s, 128)),
                   _full((SWA_Q_HEADS, 128)), _full(out_blk)],
        out_shape=[jax.ShapeDtypeStruct((s, 512), MXU_DTYPE), jax.ShapeDtypeStruct((s, 512), MXU_DTYPE),
                   jax.ShapeDtypeStruct((s, 128), F32), jax.ShapeDtypeStruct((s, 128), F32),
                   jax.ShapeDtypeStruct((SWA_Q_HEADS, 128), F32), jax.ShapeDtypeStruct(out_blk, F32)],
        scratch_shapes=[pltpu.VMEM((s + BLOCK, 128), F32)] * 2 + _OwnerSum.scratch(out_blk),
        compiler_params=_cparams(dimension_semantics=("arbitrary",)),
    )(sinks, qa, ga, attn, d_cat_a, *rope, _swa_bias(), k_pad, v_pad, parts_w_out)


def _gla_bwd(qb, kb, vb, la, oms, gb, o, sprev, d_cat_b, rb, wg, norm_w):
    s = qb.shape[0]
    tb = min(512, s)
    ch = tb // GLA_CHUNK
    nb = s // tb

    def body(qb_ref, kb_ref, vb_ref, la_ref, oms_ref, gb_ref, o_ref, sp_ref, dc_ref, rb_ref, wg_ref, nw_ref,
             dq_ref, dk_ref, dv_ref, dg_ref, dr_ref, gwg_ref, gbg_ref, gnw_ref, dst_ref):
        @pl.when(pl.program_id(0) == 0)
        def _():
            dst_ref[...] = jnp.zeros_like(dst_ref)
            gwg_ref[...] = jnp.zeros_like(gwg_ref)
            gbg_ref[...] = jnp.zeros_like(gbg_ref)
            gnw_ref[...] = jnp.zeros_like(gnw_ref)

        causal, causal_t = _gla_masks()
        nw = nw_ref[...]
        b = _chunk_cumsum(la_ref[...], True)
        bl = _chunk_last(b)
        eb, enb, ee, dec = jnp.exp(b), jnp.exp(-b), jnp.exp(bl - b), jnp.exp(bl)
        k = kb_ref[...]
        qd = (qb_ref[...] * GLA_SCALE) * eb
        ki = k * enb
        ke = k * ee
        qd16, ki16, ke16 = qd.astype(MXU_DTYPE), ki.astype(MXU_DTYPE), ke.astype(MXU_DTYPE)
        v16 = vb_ref[...].astype(MXU_DTYPE)

        g = gb_ref[...]
        sg = _sigmoid(g)
        silu = g * sg
        dsilu = sg * (1.0 + g * (1.0 - sg))
        gnw = jnp.zeros((1, GLA_DV), F32)
        do = []
        for h in range(GLA_HEADS):
            lv = slice(GLA_DV * h, GLA_DV * (h + 1))
            oh = o_ref[:, lv]
            dch = dc_ref[:, lv]
            r = lax.rsqrt(jnp.mean(oh * oh, axis=1, keepdims=True) + EPS)
            d_on = dch * silu[:, lv]
            dg_ref[:, lv] = (dch * (oh * r * nw) * dsilu[:, lv]).astype(dg_ref.dtype)
            gnw += jnp.sum(d_on * oh * r, axis=0, keepdims=True)
            u = d_on * nw
            do.append(r * u - oh * (r * r * r) * jnp.mean(u * oh, axis=1, keepdims=True))
        gnw_ref[...] += gnw
        do16 = jnp.concatenate(do, axis=1).astype(MXU_DTYPE)

        db, dbl = [None] * ch, [None] * ch
        for ci in reversed(range(ch)):
            rows = slice(GLA_CHUNK * ci, GLA_CHUNK * (ci + 1))
            qds, kis, kes = (_head_stack(t[rows], GLA_DK) for t in (qd16, ki16, ke16))
            vs, dos = _head_stack(v16[rows], GLA_DV), _head_stack(do16[rows], GLA_DV)
            a = jnp.where(causal, _mm_nt(qd16[rows], kis), 0.0).astype(MXU_DTYPE)
            at = jnp.where(causal_t, _mm_nt(ki16[rows], qds), 0.0).astype(MXU_DTYPE)
            da = jnp.where(causal, _mm_nt(do16[rows], vs), 0.0).astype(MXU_DTYPE)
            dat = jnp.where(causal_t, _mm_nt(v16[rows], dos), 0.0).astype(MXU_DTYPE)
            st = sp_ref[ci]
            dst = dst_ref[...]
            dst16 = dst.astype(MXU_DTYPE)
            dv = _mm(at, dos) + _rows_to_heads(_mm_nt(kes, dst16))
            dqd = _mm(da, kis) + _mm(do16[rows], _state_by_head(st.astype(MXU_DTYPE)))
            dki = _mm(dat, qds)
            dke = _mm(v16[rows], _state_by_head(dst16))
            ddec = jnp.sum(dst * st, axis=0, keepdims=True)
            decc = dec[rows][0:1]
            dst_ref[...] = _mm_tn(_heads_to_rows(do16[rows]), qds) + dst * decc
            dq_ref[rows, :] = (dqd * eb[rows] * GLA_SCALE).astype(dq_ref.dtype)
            dk_ref[rows, :] = (dki * enb[rows] + dke * ee[rows]).astype(dk_ref.dtype)
            dv_ref[rows, :] = dv.astype(dv_ref.dtype)
            dke_ke = dke * ke[rows]
            db[ci] = dqd * qd[rows] - dki * ki[rows] - dke_ke
            dbl[ci] = jnp.broadcast_to(jnp.sum(dke_ke, axis=0, keepdims=True) + ddec * decc, (GLA_CHUNK, GLA_KW))

        dla = _chunk_cumsum(jnp.concatenate(db, axis=0), False) + jnp.concatenate(dbl, axis=0)
        dlogit = dla * oms_ref[...] * (1.0 / GLA_TAU)
        dl16 = dlogit.astype(MXU_DTYPE)
        gbg_ref[...] += jnp.sum(dlogit, axis=0, keepdims=True)
        gwg_ref[...] += _mm_tn(rb_ref[...].astype(MXU_DTYPE), dl16)
        dr_ref[...] = _mm_nt(dl16, wg_ref[...]).astype(dr_ref.dtype)

    def rev(width):
        return pl.BlockSpec((tb, width), lambda i: (nb - 1 - i, 0))

    return pl.pallas_call(
        body, name="gla_bwd", grid=(nb,),
        in_specs=[rev(256), rev(256), rev(512), rev(256), rev(256), rev(512), rev(512),
                  pl.BlockSpec((ch, GLA_DV, 256), lambda i: (nb - 1 - i, 0, 0)), rev(512), rev(GLA_RANK),
                  _full((GLA_RANK, 256)), _full((1, 128))],
        out_specs=[rev(256), rev(256), rev(512), rev(512), rev(GLA_RANK),
                   _full((GLA_RANK, 256)), _full((1, 256)), _full((1, 128))],
        out_shape=[jax.ShapeDtypeStruct((s, 256), MXU_DTYPE), jax.ShapeDtypeStruct((s, 256), MXU_DTYPE),
                   jax.ShapeDtypeStruct((s, 512), MXU_DTYPE), jax.ShapeDtypeStruct((s, 512), MXU_DTYPE),
                   jax.ShapeDtypeStruct((s, GLA_RANK), MXU_DTYPE), jax.ShapeDtypeStruct((GLA_RANK, 256), F32),
                   jax.ShapeDtypeStruct((1, 256), F32), jax.ShapeDtypeStruct((1, 128), F32)],
        scratch_shapes=[pltpu.VMEM((GLA_DV, GLA_KW), F32)],
        compiler_params=_cparams(dimension_semantics=("arbitrary",)),
    )(qb, kb, vb, la, oms, gb, o, sprev, d_cat_b, rb, wg, norm_w)


def _dproj_tiles(piece_refs, rope_refs, members=(0, 1, 3, 4, 5, 6, 7, 8)):
    for i in members:
        if i == 1:
            dk = _rope_t(piece_refs[1][...], *(r[...] for r in rope_refs))
            yield OFF[1], OFF[3], jnp.concatenate([dk, piece_refs[2][...]], axis=1).astype(MXU_DTYPE)
        else:
            yield OFF[i], OFF[i + 1], piece_refs[i][...].astype(MXU_DTYPE)


def _in_proj_bwd_x(gx0, pieces, w_in, rope):
    s = gx0.shape[0]
    ts = min(512, s)
    widths = [OFF[i + 1] - OFF[i] for i in range(9)]

    def body(gx0_ref, *refs):
        w_ref, gx_ref = refs[12:]
        acc = gx0_ref[...]
        for lo, hi, t16 in _dproj_tiles(refs[:9], refs[9:12]):
            acc += _mm(t16, w_ref[lo:hi, :])
        gx_ref[...] = acc

    return pl.pallas_call(
        body, name="in_proj_bwd_x", grid=(s // ts,),
        in_specs=[_rows(ts, D_MODEL)] + [_rows(ts, w) for w in widths] + [_rows(ts, 128)] * 3
        + [_full((D_IN_PROJ, D_MODEL))],
        out_specs=_rows(ts, D_MODEL),
        out_shape=jax.ShapeDtypeStruct((s, D_MODEL), F32),
        compiler_params=_cparams(dimension_semantics=("arbitrary",)),
    )(gx0, *pieces, *rope, w_in)


GW_PASSES = 2
GW_EVENT_STEPS = (0, 2, 4, 6)


def _in_proj_bwd_w(x, pieces, rope, parts_wg, g_ln, g_bg, g_nw, g_sinks, loss):
    s = x.shape[0]
    ts = min(1024, s)
    nt = s // ts
    n_pass = GW_PASSES
    n_steps = n_pass * nt
    cw = D_MODEL // n_pass
    n_chips = N_DEV // 2
    blk = (D_IN_SHARD, cw)

    def body(x_hbm, *refs):
        piece_refs, rope_refs = refs[:9], refs[9:12]
        pwg_ref, gln_ref, gbg_ref, gnw_ref, gsk_ref, loss_ref, gin_ref, rwg_ref, rsm_ref = refs[12:21]
        (acc_ref, stage_ref, sib_ref, snd_ref, rcv_ref, mine_ref, sm_ref, xs_ref,
         d2d_send, d2d_recv, ici_send, ici_recv, out_sems, sm_send, sm_recv, sm_loc, x_sems) = refs[21:]
        p, t = pl.program_id(0), pl.program_id(1)
        step = p * nt + t
        tile = pl.ds(pl.multiple_of(t * ts, ts), ts)

        def x_load(k, q):
            rows = pl.ds(pl.multiple_of(k * ts, ts), ts)
            return pltpu.make_async_copy(x_hbm.at[rows, pl.ds(q * cw, cw)], xs_ref.at[q, rows, :],
                                         x_sems.at[q * nt + k])
        x_, y_, c = _mesh_pos()
        me, mychip, sibling = 4 * x_ + 2 * y_ + c, 2 * x_ + y_, (x_, y_, 1 - c)
        small_dsts = (rwg_ref, rsm_ref)

        def small_src(a, block):
            return pwg_ref.at[block] if a == 0 else sm_ref

        def small_copy(k, a, src_block, dst_block, peer):
            i = 2 * (k - 1) + a
            return pltpu.make_async_remote_copy(
                src_ref=small_src(a, src_block), dst_ref=small_dsts[a].at[dst_block], send_sem=sm_send.at[i],
                recv_sem=sm_recv.at[i], device_id=peer, device_id_type=pl.DeviceIdType.MESH)

        def small_local(a):
            return pltpu.make_async_copy(small_src(a, me), small_dsts[a].at[me], sm_loc.at[a])

        @pl.when(step == 0)
        def _():
            for q in range(n_pass):
                for k in range(nt):
                    x_load(k, q).start()
            acc_ref[...] = jnp.zeros_like(acc_ref)
            sm_ref[...] = jnp.zeros_like(sm_ref)
            for r in range(D_MODEL // 128):
                sm_ref[r:r + 1, :] = gln_ref[0:1, 128 * r:128 * (r + 1)]
                sm_ref[8 + r:9 + r, :] = gln_ref[1:2, 128 * r:128 * (r + 1)]
            for r in range(2):
                sm_ref[16 + r:17 + r, :] = gbg_ref[0:1, 128 * r:128 * (r + 1)]
            sm_ref[24:25, :] = gnw_ref[...]
            diag = (lax.broadcasted_iota(jnp.int32, gsk_ref.shape, 0)
                    == lax.broadcasted_iota(jnp.int32, gsk_ref.shape, 1))
            sm_ref[32:33, :] = jnp.sum(jnp.where(diag, gsk_ref[...], 0.0), axis=0, keepdims=True)
            sm_ref[40:41, :] = loss_ref[...]
            for a in range(2):
                small_local(a).start()
            for k in range(1, N_DEV):
                peer, pidx = _peer(k, x_, y_, c)
                for a in range(2):
                    small_copy(k, a, pidx, me, peer).start()

        for q in range(n_pass):
            @pl.when(p == q)
            def _(q=q):
                x_load(t, q).wait()

        xb = xs_ref[p, tile, :]
        for lo, hi, t16 in _dproj_tiles(piece_refs, rope_refs):
            acc_ref[p, lo:hi, :] += _mm_tn(t16, xb)

        def block_rows(q, j):
            return acc_ref[q, D_IN_SHARD * j:D_IN_SHARD * (j + 1), :]

        def d2d(q):
            return pltpu.make_async_remote_copy(
                src_ref=stage_ref.at[q % 2], dst_ref=sib_ref.at[q], send_sem=d2d_send.at[q],
                recv_sem=d2d_recv.at[q], device_id=sibling, device_id_type=pl.DeviceIdType.MESH)

        def ici(q, slot, owner):
            i = 3 * q + slot
            return pltpu.make_async_remote_copy(
                src_ref=snd_ref.at[q, slot], dst_ref=rcv_ref.at[q, slot], send_sem=ici_send.at[i],
                recv_sem=ici_recv.at[i], device_id=owner, device_id_type=pl.DeviceIdType.MESH)

        def out_copy(q):
            return pltpu.make_async_copy(mine_ref.at[q], gin_ref.at[:, pl.ds(q * cw, cw)], out_sems.at[q])

        first = (jnp.where(c == 0, 1 - x_, x_), jnp.where(c == 0, y_, 1 - y_))
        second = (jnp.where(c == 0, x_, 1 - x_), jnp.where(c == 0, 1 - y_, y_))

        def chip_of(pos):
            return 2 * pos[0] + pos[1]

        def to_sibling(q):
            if q >= 2:
                d2d(q - 2).wait_send()
            for cc in range(2):
                @pl.when(c == cc)
                def _(cc=cc):
                    for k in range(n_chips):
                        stage_ref[q % 2, k] = block_rows(q, 2 * k + 1 - cc)
            d2d(q).start()

        def chip_sums_leave(q):
            d2d(q).wait_recv()
            for cc in range(2):
                @pl.when(c == cc)
                def _(cc=cc):
                    for k in range(n_chips):
                        sib_ref[q, k] = block_rows(q, 2 * k + cc) + sib_ref[q, k]
            snd_ref[q, 2] = sib_ref[q, chip_of((1 - x_, 1 - y_))].astype(snd_ref.dtype)
            ici(q, 2, (*first, c)).start()
            snd_ref[q, 0] = sib_ref[q, chip_of(first)].astype(snd_ref.dtype)
            ici(q, 0, (*first, c)).start()

        def combined_sum_leaves(q):
            ici(q, 2, sibling).wait_recv()
            snd_ref[q, 1] = (sib_ref[q, chip_of(second)] + rcv_ref[q, 2].astype(F32)).astype(snd_ref.dtype)
            ici(q, 1, (*second, c)).start()

        def owner_total(q):
            total = sib_ref[q, mychip]
            for slot in range(2):
                ici(q, slot, sibling).wait_recv()
                total = total + rcv_ref[q, slot].astype(F32)
            mine_ref[q] = total
            out_copy(q).start()

        stages = (to_sibling, chip_sums_leave, combined_sum_leaves, owner_total)
        events = sorted((min((q + 1) * nt - 1 + GW_EVENT_STEPS[i], n_steps - 1), i > 0, 2 * q + i, -q, q, i)
                        for q in range(n_pass) for i in range(len(stages)))
        for at_step, _, _, _, q, i in events:
            @pl.when(step == at_step)
            def _(q=q, i=i):
                stages[i](q)

        @pl.when(step == n_steps - 1)
        def _():
            for k in range(1, N_DEV):
                peer, pidx = _peer(k, x_, y_, c)
                for a in range(2):
                    small_copy(k, a, me, pidx, peer).wait_recv()
            for k in range(1, N_DEV):
                peer, pidx = _peer(k, x_, y_, c)
                for a in range(2):
                    small_copy(k, a, pidx, me, peer).wait_send()
            for a in range(2):
                small_local(a).wait()
            for q in range(max(n_pass - 2, 0), n_pass):
                d2d(q).wait_send()
            for q in range(n_pass):
                for slot in range(3):
                    ici(q, slot, sibling).wait_send()
            for q in range(n_pass):
                out_copy(q).wait()

    widths = [OFF[i + 1] - OFF[i] for i in range(9)]
    hbm = pl.BlockSpec(memory_space=pl.ANY)
    vmem = pl.BlockSpec(memory_space=pltpu.VMEM)

    def token_tile(width):
        return pl.BlockSpec((ts, width), lambda p, t: (t, 0))

    return pl.pallas_call(
        body, name="in_proj_bwd_w", grid=(n_pass, nt),
        in_specs=[hbm] + [token_tile(w) for w in widths] + [token_tile(128)] * 3 + [hbm] + [vmem] * 5,
        out_specs=[hbm, hbm, hbm],
        out_shape=[jax.ShapeDtypeStruct((D_IN_SHARD, D_MODEL), F32),
                   jax.ShapeDtypeStruct((N_DEV,) + parts_wg.shape[1:], F32),
                   jax.ShapeDtypeStruct((N_DEV, SMALL_ROWS, 128), F32)],
        scratch_shapes=[pltpu.VMEM((n_pass, D_IN_PROJ, cw), F32), pltpu.VMEM((2, n_chips) + blk, F32),
                        pltpu.VMEM((n_pass, n_chips) + blk, F32), pltpu.VMEM((n_pass, 3) + blk, MXU_DTYPE),
                        pltpu.VMEM((n_pass, 3) + blk, MXU_DTYPE), pltpu.VMEM((n_pass,) + blk, F32),
                        pltpu.VMEM((SMALL_ROWS, 128), F32), pltpu.VMEM((n_pass, s, cw), MXU_DTYPE),
                        pltpu.SemaphoreType.DMA((n_pass,)), pltpu.SemaphoreType.DMA((n_pass,)),
                        pltpu.SemaphoreType.DMA((3 * n_pass,)), pltpu.SemaphoreType.DMA((3 * n_pass,)),
                        pltpu.SemaphoreType.DMA((n_pass,)),
                        pltpu.SemaphoreType.DMA((2 * (N_DEV - 1),)), pltpu.SemaphoreType.DMA((2 * (N_DEV - 1),)),
                        pltpu.SemaphoreType.DMA((2,)), pltpu.SemaphoreType.DMA((n_pass * nt,))],
        compiler_params=_cparams(dimension_semantics=("arbitrary", "arbitrary")),
    )(x, *pieces, *rope, parts_wg, g_ln, g_bg, g_nw, g_sinks, loss)


def _local_step(x, positions, w_in_t, wg_s, b_gate, sinks, norm_w, w_out_s, ln_g, ln_b, target):
    qa, k_pad, v_pad, ga, qb, kb, vb, gb, rb, la, oms, *rope, x16, w_in, wg, w_out = _in_proj(
        x, w_in_t, wg_s, b_gate, _rope_angles(positions), w_out_s)
    attn, cat_a = _swa_fwd(sinks, qa, k_pad, v_pad, ga)
    o, cat_b, sprev = _gla_fwd(qb, kb, vb, la, gb, norm_w)
    loss, gx0, d_cat_a, d_cat_b, g_w_out, g_ln = _out_ln_loss(cat_a, cat_b, w_out, x, target, ln_g, ln_b)
    parts_w_out = g_w_out.reshape(N_DEV, D_OUT_SHARD, D_MODEL)
    dqa, dga, dka, dva, g_sinks, g_out = _swa_bwd(sinks, qa, k_pad, v_pad, attn, ga, d_cat_a, rope, parts_w_out)
    dqb, dkb, dvb, dgb, drb, g_wg, g_bg, g_nw = _gla_bwd(qb, kb, vb, la, oms, gb, o, sprev, d_cat_b, rb, wg, norm_w)
    pieces = (dqa, dka, dva, dga, dqb, dkb, dvb, dgb, drb)
    grad_x = _in_proj_bwd_x(gx0, pieces, w_in, rope)
    parts_wg = jnp.transpose(g_wg.reshape(GLA_RANK, N_DEV, 32), (1, 0, 2))
    g_in, r_wg, r_small = _in_proj_bwd_w(x16, pieces, rope, parts_wg, g_ln, g_bg, g_nw, g_sinks, loss)
    return grad_x, g_in, g_out, r_wg, r_small


def _mesh_pos():
    return lax.axis_index("x"), lax.axis_index("y"), lax.axis_index("c")


def _peer(k, x, y, c):
    px = (1 - x) if k & 4 else x
    py = (1 - y) if k & 2 else y
    pc = (1 - c) if k & 1 else c
    return (px, py, pc), 4 * px + 2 * py + pc


def _other_chips(x, y):
    return [(1 - x, y), (x, 1 - y), (1 - x, 1 - y)]


def _shard_view(t):
    return jnp.transpose(t, (2, 0, 1))


class _BlockGather:
    def __init__(self, slots, send_sems, recv_sems):
        self.slots, self.send_sems, self.recv_sems = slots, send_sems, recv_sems
        x, y, c = _mesh_pos()
        self.c, self.me, self.sibling = c, 4 * x + 2 * y + c, (x, y, 1 - c)
        first = (jnp.where(c == 0, 1 - x, x), jnp.where(c == 0, y, 1 - y))
        second = (jnp.where(c == 0, x, 1 - x), jnp.where(c == 0, 1 - y, y))
        self.chips = [first, second, (1 - x, 1 - y)]

    @staticmethod
    def scratch():
        return [pltpu.SemaphoreType.DMA((N_DEV - 1,)), pltpu.SemaphoreType.DMA((N_DEV - 1,))]

    def _copy(self, k, block, to):
        return pltpu.make_async_remote_copy(
            src_ref=self.slots.at[block], dst_ref=self.slots.at[block], send_sem=self.send_sems.at[k],
            recv_sem=self.recv_sems.at[k], device_id=to, device_id_type=pl.DeviceIdType.MESH)

    def _block(self, j, c):
        cx, cy = self.chips[j]
        return 4 * cx + 2 * cy + c

    def _dev(self, j):
        return (*self.chips[j], self.c)

    def start(self):
        self._copy(1, self.me, self._dev(0)).start()
        self._copy(2, self.me, self._dev(1)).start()
        self._copy(0, self.me, self.sibling).start()

    def forward(self):
        self._copy(1, self._block(0, self.c), self.sibling).wait_recv()
        self._copy(3, self._block(0, self.c), self._dev(1)).start()
        self._copy(4, self._block(0, self.c), self.sibling).start()
        self._copy(2, self._block(1, self.c), self.sibling).wait_recv()
        self._copy(5, self._block(1, self.c), self.sibling).start()

    def forward_far(self):
        self._copy(3, self._block(2, self.c), self.sibling).wait_recv()
        self._copy(6, self._block(2, self.c), self.sibling).start()

    def finish(self):
        for k in (0, 4, 5, 6):
            self._copy(k, self.me, self.sibling).wait_recv()
        for k in range(N_DEV - 1):
            self._copy(k, self.me, self.sibling).wait_send()


class _OwnerSum:
    def __init__(self, parts, own, sib, snd, rcv, loc_sems, d2d_send, d2d_recv, ici_send, ici_recv):
        self.parts, self.own, self.sib, self.snd, self.rcv = parts, own, sib, snd, rcv
        self.sems = (loc_sems, d2d_send, d2d_recv, ici_send, ici_recv)
        x, y, c = _mesh_pos()
        self.c, self.sibling = c, (x, y, 1 - c)
        self.chips = [(x, y)] + _other_chips(x, y)

    @staticmethod
    def scratch(block):
        return [pltpu.VMEM((4,) + block, F32), pltpu.VMEM((4,) + block, F32),
                pltpu.VMEM((3,) + block, MXU_DTYPE), pltpu.VMEM((3,) + block, MXU_DTYPE),
                pltpu.SemaphoreType.DMA((4,)), pltpu.SemaphoreType.DMA((4,)), pltpu.SemaphoreType.DMA((4,)),
                pltpu.SemaphoreType.DMA((3,)), pltpu.SemaphoreType.DMA((3,))]

    def _local(self, r):
        cx, cy = self.chips[r]
        return pltpu.make_async_copy(self.parts.at[4 * cx + 2 * cy + self.c], self.own.at[r], self.sems[0].at[r])

    def _d2d(self, r):
        cx, cy = self.chips[r]
        return pltpu.make_async_remote_copy(
            src_ref=self.parts.at[4 * cx + 2 * cy + (1 - self.c)], dst_ref=self.sib.at[r], send_sem=self.sems[1].at[r],
            recv_sem=self.sems[2].at[r], device_id=self.sibling, device_id_type=pl.DeviceIdType.MESH)

    def _ici(self, r):
        cx, cy = self.chips[r]
        return pltpu.make_async_remote_copy(
            src_ref=self.snd.at[r - 1], dst_ref=self.rcv.at[r - 1], send_sem=self.sems[3].at[r - 1],
            recv_sem=self.sems[4].at[r - 1], device_id=(cx, cy, self.c), device_id_type=pl.DeviceIdType.MESH)

    def start(self):
        for r in (1, 2, 3, 0):
            self._local(r).start()
            self._d2d(r).start()

    def forward(self):
        for r in (1, 2, 3):
            self._local(r).wait()
            self._d2d(r).wait_recv()
            self.snd[r - 1] = (self.own[r] + self.sib[r]).astype(self.snd.dtype)
            self._ici(r).start()

    def finish(self):
        self._local(0).wait()
        self._d2d(0).wait_recv()
        acc = self.own[0] + self.sib[0]
        for r in (1, 2, 3):
            self._ici(r).wait_recv()
            acc = acc + self.rcv[r - 1].astype(F32)
        for r in range(4):
            self._d2d(r).wait_send()
        for r in (1, 2, 3):
            self._ici(r).wait_send()
        return acc


SMALL_ROWS = 48


def _adamw_math(g, w, m, v):
    nm = ADAM_B1 * m + (1.0 - ADAM_B1) * g
    nv = ADAM_B2 * v + (1.0 - ADAM_B2) * (g * g)
    m_hat = nm / (1.0 - ADAM_B1 ** ADAM_STEP)
    v_hat = nv / (1.0 - ADAM_B2 ** ADAM_STEP)
    return -ADAM_LR * (m_hat / (jnp.sqrt(v_hat) + ADAM_EPS) + ADAM_WD * w), nm, nv


def _adamw_shard_view(g, w, m, v):
    rows, width = g.shape
    parts = 3
    pr = rows // parts
    assert pr * parts == rows

    def body(g_ref, w_hbm, m_hbm, v_hbm, g_out, d_out, nm_out, nv_out, bufs, outs, sems):
        def load(k, i, src):
            return pltpu.make_async_copy(src.at[pl.ds(pr * k, pr), 0, :], bufs.at[k, i], sems.at[7 * k + i])

        def store(k, i, dst):
            return pltpu.make_async_copy(outs.at[k, i], dst.at[pl.ds(pr * k, pr), 0, :], sems.at[7 * k + 3 + i])

        for k in range(parts):
            for i, src in enumerate((w_hbm, m_hbm, v_hbm)):
                load(k, i, src).start()
        for k in range(parts):
            for i, src in enumerate((w_hbm, m_hbm, v_hbm)):
                load(k, i, src).wait()
            gk = g_ref[pr * k:pr * (k + 1), :]
            outs[k, 0] = gk
            outs[k, 1], outs[k, 2], outs[k, 3] = _adamw_math(gk, bufs[k, 0], bufs[k, 1], bufs[k, 2])
            for i, dst in enumerate((g_out, d_out, nm_out, nv_out)):
                store(k, i, dst).start()
        for k in range(parts):
            for i, dst in enumerate((g_out, d_out, nm_out, nv_out)):
                store(k, i, dst).wait()

    hbm = pl.BlockSpec(memory_space=pl.ANY)
    return pl.pallas_call(
        body, name="adamw_w_in",
        in_specs=[pl.BlockSpec(memory_space=pltpu.VMEM), hbm, hbm, hbm], out_specs=[hbm] * 4,
        out_shape=[jax.ShapeDtypeStruct((rows, 1, width), F32)] * 4,
        scratch_shapes=[pltpu.VMEM((parts, 3, pr, width), F32), pltpu.VMEM((parts, 4, pr, width), F32),
                        pltpu.SemaphoreType.DMA((7 * parts,))],
        compiler_params=_cparams(),
    )(g, w, m, v)


def _adamw_vectors(r_small, r_wg, g_out, params):
    n_par = len(params)

    def body(rsm_ref, rwg_ref, gout_ref, *refs):
        ins, outs = refs[:3 * n_par], refs[3 * n_par:]
        g = rsm_ref[0]
        gwg = rwg_ref[0]
        for j in range(1, N_DEV):
            g = g + rsm_ref[j]
            gwg = gwg + rwg_ref[j]
        outs[4 * n_par][...] = g[40:41]
        grads = [gwg,
                 jnp.concatenate([g[r:r + 1] for r in range(0, 8)], axis=1),
                 jnp.concatenate([g[r:r + 1] for r in range(8, 16)], axis=1),
                 jnp.concatenate([g[16:17], g[17:18]], axis=1),
                 g[24:25],
                 g[32:33, 0:SWA_Q_HEADS],
                 gout_ref[...]]
        for p, gp in enumerate(grads):
            w_ref, m_ref, v_ref = ins[3 * p:3 * p + 3]
            outs[4 * p][...] = gp
            outs[4 * p + 1][...], outs[4 * p + 2][...], outs[4 * p + 3][...] = _adamw_math(
                gp, w_ref[...], m_ref[...], v_ref[...])

    vmem = pl.BlockSpec(memory_space=pltpu.VMEM)
    flat = [t for wmv in params for t in wmv]
    return pl.pallas_call(
        body, name="adamw_vectors",
        in_specs=[vmem] * (3 + len(flat)), out_specs=[vmem] * (4 * n_par + 1),
        out_shape=[jax.ShapeDtypeStruct(wmv[0].shape, F32) for wmv in params for _ in range(4)]
        + [jax.ShapeDtypeStruct((1, 128), F32)],
        compiler_params=_cparams(),
    )(r_small, r_wg, g_out, *flat)


def kernel(x, positions, w_in, gla_w_gate_up, gla_b_gate, attn_sinks, gla_norm_w, w_out, ln_g, ln_b, loss_target, m_w_in, m_gla_w_gate_up, m_gla_b_gate, m_attn_sinks, m_gla_norm_w, m_w_out, m_ln_g, m_ln_b, v_w_in, v_gla_w_gate_up, v_gla_b_gate, v_attn_sinks, v_gla_norm_w, v_w_out, v_ln_g, v_ln_b):
    grad_x, g_in, g_out, r_wg, r_small = _local_step(
        x[0], positions[0], _shard_view(w_in), gla_w_gate_up[0], gla_b_gate, attn_sinks[0], gla_norm_w, w_out[0],
        ln_g, ln_b, loss_target[0])

    upd_in = _adamw_shard_view(g_in, _shard_view(w_in), _shard_view(m_w_in), _shard_view(v_w_in))
    upd_in = [jnp.transpose(t, (1, 2, 0)) for t in upd_in]
    vec = _adamw_vectors(r_small, r_wg, g_out, [
        (gla_w_gate_up[0], m_gla_w_gate_up[0], v_gla_w_gate_up[0]), (ln_g, m_ln_g, v_ln_g), (ln_b, m_ln_b, v_ln_b),
        (gla_b_gate, m_gla_b_gate, v_gla_b_gate), (gla_norm_w, m_gla_norm_w, v_gla_norm_w),
        (attn_sinks, m_attn_sinks, v_attn_sinks), (w_out[0], m_w_out[0], v_w_out[0])])

    outs = [vec[28][0, 0], grad_x[None]]
    for kind in range(4):
        u_wg, u_ln_g, u_ln_b, u_bg, u_nw, u_sinks, u_out = (vec[4 * p + kind] for p in range(7))
        outs += [upd_in[kind], u_wg[None], u_bg, u_sinks, u_nw, u_out[None], u_ln_g, u_ln_b]
    return tuple(outs)
```

```python
import jax
import jax.numpy as jnp
from jax import lax
from jax.experimental import pallas as pl
from jax.experimental.pallas import tpu as pltpu

F32 = jnp.float32
MXU_DTYPE = jnp.bfloat16

N_DEV = 8
D_MODEL = 1024
SWA_Q_HEADS = 8
SWA_KV_HEADS = 2
SWA_GROUP = 4
SWA_HEAD_DIM = 64
BLOCK = 128
ROPE_THETA = 500000.0
ROT_DIM = 16
GLA_HEADS = 4
GLA_DK = 64
GLA_DV = 128
GLA_RANK = 16
GLA_TAU = 16.0
GLA_CHUNK = 64
D_IN_PROJ = 2832
D_IN_SHARD = D_IN_PROJ // N_DEV
D_OUT_SHARD = D_MODEL // N_DEV
OFF = (0, 512, 640, 768, 1280, 1536, 1792, 2304, 2816, 2832)
EPS = 1e-5
ALPHA = 2.0 ** 0.25
SWA_SCALE = SWA_HEAD_DIM ** -0.5
GLA_SCALE = GLA_DK ** -0.5
ADAM_LR = 0.001
ADAM_B1 = 0.9
ADAM_B2 = 0.999
ADAM_EPS = 1e-08
ADAM_WD = 0.01
ADAM_STEP = 10
VMEM_LIMIT = 56 * 1024 * 1024

_NT = (((1,), (1,)), ((), ()))
_TN = (((0,), (0,)), ((), ()))


def _mm(a, b):
    return jnp.dot(a, b, preferred_element_type=F32)


def _mm_nt(a, b):
    return lax.dot_general(a, b, _NT, preferred_element_type=F32)


def _mm_tn(a, b):
    return lax.dot_general(a, b, _TN, preferred_element_type=F32)


def _sigmoid(t):
    return 1.0 / (1.0 + jnp.exp(-t))


def _cparams(**kw):
    return pltpu.CompilerParams(vmem_limit_bytes=VMEM_LIMIT, **kw)


def _full(shape):
    return pl.BlockSpec(shape, lambda *_: (0,) * len(shape))


def _rows(tile, width):
    return pl.BlockSpec((tile, width), lambda i: (i, 0))


def _rope_angles(positions):
    half = ROT_DIM // 2
    inv_freq = ROPE_THETA ** (-jnp.arange(half, dtype=F32) / half)
    ang = positions.astype(F32)[:, None] * inv_freq[None, :]
    return jnp.concatenate([jnp.cos(ang), jnp.sin(ang)], axis=1)


def _split3_parts(t):
    hi = t.astype(MXU_DTYPE)
    r1 = t - hi.astype(F32)
    mid = r1.astype(MXU_DTYPE)
    return hi, mid, (r1 - mid.astype(F32)).astype(MXU_DTYPE)


def _rope_tables(cs):
    half = ROT_DIM // 2
    i = lax.broadcasted_iota(jnp.int32, (2 * half, 3 * 128), 0)
    lane = lax.broadcasted_iota(jnp.int32, (2 * half, 3 * 128), 1)
    table, pos = _idiv(lane, 128), lane & (SWA_HEAD_DIM - 1)
    is_c = (table == 0) & (pos < ROT_DIM) & ((pos & (half - 1)) == i)
    is_s1 = (table == 1) & (pos < half) & (pos + half == i)
    is_s2 = (table == 2) & (pos >= half) & (pos < ROT_DIM) & (pos == i)
    sel = jnp.where(is_c | is_s2, 1.0, jnp.where(is_s1, -1.0, 0.0)).astype(MXU_DTYPE)
    hi, mid, lo = _split3_parts(cs)
    t = (_mm(hi, sel) + _mm(mid, sel)) + _mm(lo, sel)
    pos1 = lax.broadcasted_iota(jnp.int32, (1, 128), 1) & (SWA_HEAD_DIM - 1)
    return t[:, 0:128] + jnp.where(pos1 >= ROT_DIM, 1.0, 0.0), t[:, 128:256], t[:, 256:384]


def _rope(t, c, s1, s2):
    return t * c + pltpu.roll(t, 120, 1) * s1 + pltpu.roll(t, 8, 1) * s2


def _rope_t(g, c, s1, s2):
    return g * c + pltpu.roll(g * s1, 8, 1) + pltpu.roll(g * s2, 120, 1)


def _in_proj(x, w_in_t, wg_s, b_gate, cos_sin, w_out_s):
    s = x.shape[0]
    ts = min(512, s)
    nsteps = s // ts
    forward_step, far_step = min(3, nsteps - 1), min(5, nsteps - 1)
    widths = [OFF[i + 1] - OFF[i] for i in range(9)]

    def body(x_ref, win_hbm, wgs_ref, bg_ref, cs_ref, wos_ref,
             qa_ref, ka_ref, va_ref, ga_ref, qb_ref, kb_ref, vb_ref, gb_ref, rb_ref, la_ref, oms_ref,
             c_ref, s1_ref, s2_ref, x16_ref, w_ref, wg_ref, wout_ref,
             win_all, wg_all, wout_all, stage, stage_sem, *sems):
        xb = x_ref[...].astype(MXU_DTYPE)
        x16_ref[...] = xb
        c, s1, s2 = _rope_tables(cs_ref[...])
        c_ref[...], s1_ref[...], s2_ref[...] = c, s1, s2
        i0 = pl.program_id(0)
        gather = _BlockGather(wout_all, *sems[0:2])

        @pl.when(i0 == 0)
        def _():
            ka_ref[0:BLOCK, :] = jnp.zeros((BLOCK, 128), ka_ref.dtype)
            va_ref[0:BLOCK, :] = jnp.zeros((BLOCK, 128), va_ref.dtype)
            first = (_BlockGather(win_all, *sems[2:4]), _BlockGather(wg_all, *sems[4:6]))
            load = pltpu.make_async_copy(win_hbm.at[:, 0, :], stage, stage_sem)
            load.start()
            wout_all[gather.me] = wos_ref[...].astype(wout_all.dtype)
            wg_all[gather.me] = wgs_ref[...].astype(wg_all.dtype)
            load.wait()
            win_all[gather.me] = stage[...].astype(win_all.dtype)
            for stage_of in ("start", "forward", "forward_far", "finish"):
                for g in first:
                    getattr(g, stage_of)()
            gather.start()
            for j in range(N_DEV):
                w_ref[D_IN_SHARD * j:D_IN_SHARD * (j + 1), :] = win_all[j]
                wg_ref[:, 32 * j:32 * (j + 1)] = wg_all[j]

        @pl.when(i0 == forward_step)
        def _():
            gather.forward()

        @pl.when(i0 == far_step)
        def _():
            gather.forward_far()

        @pl.when(i0 == nsteps - 1)
        def _():
            gather.finish()
            for j in range(N_DEV):
                wout_ref[D_OUT_SHARD * j:D_OUT_SHARD * (j + 1), :] = wout_all[j]

        kv_rows = pl.ds(pl.multiple_of(BLOCK + i0 * ts, BLOCK), ts)

        def cols(i):
            return _mm_nt(xb, w_ref[OFF[i]:OFF[i + 1], :])

        qa = cols(0)
        for i in range(4):
            qa_ref[:, 128 * i:128 * (i + 1)] = _rope(qa[:, 128 * i:128 * (i + 1)], c, s1, s2).astype(qa_ref.dtype)
        kv = _mm_nt(xb, w_ref[OFF[1]:OFF[3], :])
        ka_ref[kv_rows, :] = _rope(kv[:, 0:128], c, s1, s2).astype(ka_ref.dtype)
        va_ref[kv_rows, :] = kv[:, 128:256].astype(va_ref.dtype)
        ga_ref[...] = cols(3)
        qb_ref[...] = cols(4)
        kb_ref[...] = cols(5)
        vb_ref[...] = cols(6).astype(vb_ref.dtype)
        gb_ref[...] = cols(7)
        rb = cols(8)
        rb_ref[...] = rb
        logit = _mm(rb.astype(MXU_DTYPE), wg_ref[...]) + bg_ref[...]
        e = jnp.exp(-jnp.abs(logit))
        la_ref[...] = (jnp.minimum(logit, 0.0) - jnp.log(1.0 + e)) / GLA_TAU
        oms_ref[...] = jnp.where(logit >= 0.0, e, 1.0) / (1.0 + e)

    out_shape = [jax.ShapeDtypeStruct((s + BLOCK if i in (1, 2) else s, w), MXU_DTYPE if i in (0, 1, 2, 6) else F32)
                 for i, w in enumerate(widths)]
    out_shape += [jax.ShapeDtypeStruct((s, 256), F32)] * 2 + [jax.ShapeDtypeStruct((s, 128), F32)] * 3
    out_shape += [jax.ShapeDtypeStruct((s, D_MODEL), MXU_DTYPE)]
    out_shape += [jax.ShapeDtypeStruct((D_IN_PROJ, D_MODEL), MXU_DTYPE), jax.ShapeDtypeStruct((GLA_RANK, 256), MXU_DTYPE),
                  jax.ShapeDtypeStruct((D_MODEL, D_MODEL), MXU_DTYPE)]
    return pl.pallas_call(
        body, name="in_proj", grid=(nsteps,),
        in_specs=[_rows(ts, D_MODEL), pl.BlockSpec(memory_space=pl.ANY), _full((GLA_RANK, 32)), _full((1, 256)),
                  _rows(ts, ROT_DIM), _full((D_OUT_SHARD, D_MODEL))],
        out_specs=[_full((s + BLOCK, w)) if i in (1, 2) else _rows(ts, w) for i, w in enumerate(widths)]
        + [_rows(ts, 256)] * 2 + [_rows(ts, 128)] * 3 + [_rows(ts, D_MODEL)]
        + [_full((D_IN_PROJ, D_MODEL)), _full((GLA_RANK, 256)), _full((D_MODEL, D_MODEL))],
        out_shape=out_shape,
        scratch_shapes=[pltpu.VMEM((N_DEV, D_IN_SHARD, D_MODEL), MXU_DTYPE), pltpu.VMEM((N_DEV, GLA_RANK, 32), MXU_DTYPE),
                        pltpu.VMEM((N_DEV, D_OUT_SHARD, D_MODEL), MXU_DTYPE),
                        pltpu.VMEM((D_IN_SHARD, D_MODEL), F32), pltpu.SemaphoreType.DMA]
        + 3 * _BlockGather.scratch(),
        compiler_params=_cparams(dimension_semantics=("arbitrary",)),
    )(x, w_in_t, wg_s, b_gate, cos_sin, w_out_s)


SWA_ROWS = SWA_GROUP * BLOCK


def _swa_bias():
    shape = (2, 2 * BLOCK, SWA_ROWS)
    ki = lax.broadcasted_iota(jnp.int32, shape, 1)
    qi = lax.broadcasted_iota(jnp.int32, shape, 2) & (BLOCK - 1)
    first = lax.broadcasted_iota(jnp.int32, shape, 0) == 0
    dist = qi + BLOCK - ki
    ok = (dist >= 0) & (dist < BLOCK) & (jnp.logical_not(first) | (ki >= BLOCK))
    return jnp.where(ok, 0.0, -jnp.inf).astype(F32)


SWA_SUB = 8


def _swa_bias_of(bias_ref, n, b):
    return bias_ref[jnp.minimum(n, 1)] if b == 0 else bias_ref[1]


def _swa_dup(t, j):
    t = t.astype(F32)
    low = lax.broadcasted_iota(jnp.int32, t.shape, 1) < SWA_HEAD_DIM
    keep = low if j == 0 else jnp.logical_not(low)
    return jnp.where(keep, t, pltpu.roll(t, SWA_HEAD_DIM, 1)).astype(MXU_DTYPE)


def _swa_stack(t, j):
    low = lax.broadcasted_iota(jnp.int32, (BLOCK, 128), 1) < SWA_HEAD_DIM
    zero = jnp.zeros((BLOCK, 128), t.dtype)
    blocks = []
    for p in (2 * j, 2 * j + 1):
        tp = t[:, 128 * p:128 * (p + 1)]
        blocks += [jnp.where(low, tp, zero), jnp.where(low, zero, tp)]
    return jnp.concatenate(blocks, axis=0)


def _swa_unstack(t):
    low = lax.broadcasted_iota(jnp.int32, (BLOCK, 128), 1) < SWA_HEAD_DIM
    return [jnp.where(low, t[2 * BLOCK * i:2 * BLOCK * i + BLOCK], t[2 * BLOCK * i + BLOCK:2 * BLOCK * (i + 1)])
            for i in range(2)]


def _swa_sink_row(sink_ref, j):
    lane = lax.broadcasted_iota(jnp.int32, (1, SWA_ROWS), 1)
    row = jnp.full((1, SWA_ROWS), sink_ref[SWA_GROUP * j], F32)
    for r in range(1, SWA_GROUP):
        row = jnp.where(lane >= BLOCK * r, sink_ref[SWA_GROUP * j + r], row)
    return row


def _split3(t):
    return jnp.concatenate(_split3_parts(t), axis=1)


def _row_sums_as_row(t):
    ones = jnp.ones((8, 3 * t.shape[1]), MXU_DTYPE)
    return _mm_nt(ones, _split3(t))[0:1, :]


def _swa_probs_t(qs, kd, bias_t, sink):
    sc = _mm_nt(kd, qs) + bias_t
    m = jnp.maximum(jnp.max(sc, axis=0, keepdims=True), sink)
    p = jnp.exp(sc - m)
    ps = jnp.exp(sink - m)
    rinv = 1.0 / (jnp.sum(p, axis=0, keepdims=True) + ps)
    return p * rinv, ps * rinv


def _swa_fwd(sinks, qa, k_pad, v_pad, ga):
    s = qa.shape[0]
    sub = min(SWA_SUB, s // BLOCK)
    tq = sub * BLOCK

    def body(sink_ref, qa_ref, ga_ref, bias_ref, k_ref, v_ref, attn_ref, cat_ref):
        n = pl.program_id(0)
        for b in range(sub):
            rows = slice(BLOCK * b, BLOCK * (b + 1))
            start = pl.multiple_of((n * sub + b) * BLOCK, BLOCK)
            kw = k_ref[pl.ds(start, 2 * BLOCK), :]
            vw = v_ref[pl.ds(start, 2 * BLOCK), :]
            bias_t = _swa_bias_of(bias_ref, n, b)
            q = qa_ref[rows, :] * SWA_SCALE
            g = ga_ref[rows, :]
            silu = g * _sigmoid(g)
            for j in range(SWA_KV_HEADS):
                qs = _swa_stack(q, j).astype(MXU_DTYPE)
                probs, _ = _swa_probs_t(qs, _swa_dup(kw, j), bias_t, _swa_sink_row(sink_ref, j))
                pairs = _swa_unstack(_mm_tn(probs.astype(MXU_DTYPE), _swa_dup(vw, j)))
                for i in range(2):
                    lanes = slice(128 * (2 * j + i), 128 * (2 * j + i + 1))
                    attn_ref[rows, lanes] = pairs[i]
                    cat_ref[rows, lanes] = (pairs[i] * silu[:, lanes]).astype(cat_ref.dtype)

    return pl.pallas_call(
        body, name="swa_fwd", grid=(s // tq,),
        in_specs=[pl.BlockSpec(memory_space=pltpu.SMEM), _rows(tq, 512), _rows(tq, 512),
                  _full((2, 2 * BLOCK, SWA_ROWS)), _full((s + BLOCK, 128)), _full((s + BLOCK, 128))],
        out_specs=[_rows(tq, 512), _rows(tq, 512)],
        out_shape=[jax.ShapeDtypeStruct((s, 512), F32), jax.ShapeDtypeStruct((s, 512), MXU_DTYPE)],
        compiler_params=_cparams(dimension_semantics=("arbitrary",)),
    )(sinks, qa, ga, _swa_bias(), k_pad, v_pad)


GLA_KW = GLA_HEADS * GLA_DK
GLA_VW = GLA_HEADS * GLA_DV


def _idiv(t, d):
    return t >> (d.bit_length() - 1)


def _chunk_cumsum(t, lower):
    n, w = t.shape
    r = lax.broadcasted_iota(jnp.int32, (n, n), 0)
    c = lax.broadcasted_iota(jnp.int32, (n, n), 1)
    tri = ((_idiv(r, GLA_CHUNK) == _idiv(c, GLA_CHUNK)) & ((r >= c) if lower else (r <= c))).astype(MXU_DTYPE)
    parts = _mm(tri, _split3(t))
    return (parts[:, :w] + parts[:, w:2 * w]) + parts[:, 2 * w:]


def _chunk_last(t):
    n = t.shape[0]
    return jnp.concatenate(
        [jnp.broadcast_to(t[c + GLA_CHUNK - 1:c + GLA_CHUNK, :], (GLA_CHUNK, t.shape[1]))
         for c in range(0, n, GLA_CHUNK)], axis=0)


def _head_stack(t, width):
    head = _idiv(lax.broadcasted_iota(jnp.int32, t.shape, 1), width)
    zero = jnp.zeros_like(t)
    return jnp.concatenate([jnp.where(head == h, t, zero) for h in range(GLA_HEADS)], axis=0)


def _heads_to_rows(t):
    return jnp.concatenate([t[:, GLA_DV * h:GLA_DV * (h + 1)] for h in range(GLA_HEADS)], axis=0)


def _rows_to_heads(t):
    return jnp.concatenate([t[GLA_CHUNK * h:GLA_CHUNK * (h + 1)] for h in range(GLA_HEADS)], axis=1)


def _state_by_head(t):
    srow = _idiv(lax.broadcasted_iota(jnp.int32, (GLA_VW, GLA_KW), 0), GLA_DV)
    slane = _idiv(lax.broadcasted_iota(jnp.int32, (GLA_VW, GLA_KW), 1), GLA_DK)
    return jnp.where(srow == slane, jnp.concatenate([t] * GLA_HEADS, axis=0), jnp.zeros((GLA_VW, GLA_KW), t.dtype))


def _gla_masks():
    row = lax.broadcasted_iota(jnp.int32, (GLA_CHUNK, GLA_KW), 0)
    pos = lax.broadcasted_iota(jnp.int32, (GLA_CHUNK, GLA_KW), 1) & (GLA_CHUNK - 1)
    return pos <= row, pos >= row


def _gla_fwd(qb, kb, vb, la, gb, norm_w):
    s = qb.shape[0]
    tb = min(256, s)
    ch = tb // GLA_CHUNK

    def body(qb_ref, kb_ref, vb_ref, la_ref, gb_ref, nw_ref, o_ref, cat_ref, sp_ref, st_ref):
        @pl.when(pl.program_id(0) == 0)
        def _():
            st_ref[...] = jnp.zeros_like(st_ref)

        causal, _ = _gla_masks()
        nw = nw_ref[...]
        b = _chunk_cumsum(la_ref[...], True)
        bl = _chunk_last(b)
        k = kb_ref[...]
        qd = ((qb_ref[...] * GLA_SCALE) * jnp.exp(b)).astype(MXU_DTYPE)
        ki = (k * jnp.exp(-b)).astype(MXU_DTYPE)
        ke = (k * jnp.exp(bl - b)).astype(MXU_DTYPE)
        dec = jnp.exp(bl)
        v = vb_ref[...].astype(MXU_DTYPE)
        g = gb_ref[...]
        silu = g * _sigmoid(g)
        for ci in range(ch):
            rows = slice(GLA_CHUNK * ci, GLA_CHUNK * (ci + 1))
            qds, kis, kes = (_head_stack(t[rows], GLA_DK) for t in (qd, ki, ke))
            a = jnp.where(causal, _mm_nt(qd[rows], kis), 0.0).astype(MXU_DTYPE)
            st = st_ref[...]
            sp_ref[ci] = st
            o = _mm(a, _head_stack(v[rows], GLA_DV)) + _rows_to_heads(_mm_nt(qds, st.astype(MXU_DTYPE)))
            st_ref[...] = st * dec[rows][0:1] + _mm_tn(_heads_to_rows(v[rows]), kes)
            o_ref[rows, :] = o
            for h in range(GLA_HEADS):
                lv = slice(GLA_DV * h, GLA_DV * (h + 1))
                oh = o[:, lv]
                r = lax.rsqrt(jnp.mean(oh * oh, axis=1, keepdims=True) + EPS)
                cat_ref[rows, lv] = (oh * r * nw * silu[rows, lv]).astype(cat_ref.dtype)

    return pl.pallas_call(
        body, name="gla_fwd", grid=(s // tb,),
        in_specs=[_rows(tb, 256), _rows(tb, 256), _rows(tb, 512), _rows(tb, 256), _rows(tb, 512), _full((1, 128))],
        out_specs=[_rows(tb, 512), _rows(tb, 512), pl.BlockSpec((ch, GLA_DV, 256), lambda i: (i, 0, 0))],
        out_shape=[jax.ShapeDtypeStruct((s, 512), F32), jax.ShapeDtypeStruct((s, 512), MXU_DTYPE),
                   jax.ShapeDtypeStruct((s // GLA_CHUNK, GLA_DV, 256), F32)],
        scratch_shapes=[pltpu.VMEM((GLA_DV, GLA_KW), F32)],
        compiler_params=_cparams(dimension_semantics=("arbitrary",)),
    )(qb, kb, vb, la, gb, norm_w)


def _out_ln_loss(cat_a, cat_b, w_out, x, target, ln_g, ln_b):
    s = x.shape[0]
    ts = min(512, s)
    halves = 2 if ts % 32 == 0 else 1
    th = ts // halves

    def body(ca_ref, cb_ref, w_ref, x_ref, t_ref, g_ref, b_ref,
             loss_ref, gx_ref, da_ref, db_ref, gw_ref, gln_ref):
        @pl.when(pl.program_id(0) == 0)
        def _():
            loss_ref[...] = jnp.zeros_like(loss_ref)
            gw_ref[...] = jnp.zeros_like(gw_ref)
            gln_ref[...] = jnp.zeros_like(gln_ref)

        g = g_ref[...]
        dh16s = []
        for k in range(halves):
            rows = slice(th * k, th * (k + 1))
            mix = _mm(ca_ref[rows, :], w_ref[0:512, :]) + _mm(cb_ref[rows, :], w_ref[512:1024, :])
            h = ALPHA * x_ref[rows, :] + mix
            mu = jnp.mean(h, axis=1, keepdims=True)
            hc = h - mu
            rstd = lax.rsqrt(jnp.mean(hc * hc, axis=1, keepdims=True) + EPS)
            xhat = hc * rstd
            err = xhat * g + b_ref[...] - t_ref[rows, :]
            loss_ref[...] += 0.5 * jnp.sum(jnp.mean(err * err, axis=1, keepdims=True))
            dy = err * (1.0 / D_MODEL)
            gln_ref[0:1, :] += jnp.sum(dy * xhat, axis=0, keepdims=True)
            gln_ref[1:2, :] += jnp.sum(dy, axis=0, keepdims=True)
            dxh = dy * g
            dh = rstd * (dxh - jnp.mean(dxh, axis=1, keepdims=True)
                         - xhat * jnp.mean(dxh * xhat, axis=1, keepdims=True))
            gx_ref[rows, :] = ALPHA * dh
            dh16s.append(dh.astype(MXU_DTYPE))
        for k in range(halves):
            rows = slice(th * k, th * (k + 1))
            da_ref[rows, :] = _mm_nt(dh16s[k], w_ref[0:512, :])
            db_ref[rows, :] = _mm_nt(dh16s[k], w_ref[512:1024, :])
        dh16 = jnp.concatenate(dh16s, axis=0)
        gw_ref[0:512, :] += _mm_tn(ca_ref[...], dh16)
        gw_ref[512:1024, :] += _mm_tn(cb_ref[...], dh16)

    return pl.pallas_call(
        body, name="out_ln_loss", grid=(s // ts,),
        in_specs=[_rows(ts, 512), _rows(ts, 512), _full((D_MODEL, D_MODEL)), _rows(ts, D_MODEL), _rows(ts, D_MODEL),
                  _full((1, D_MODEL)), _full((1, D_MODEL))],
        out_specs=[_full((1, 128)), _rows(ts, D_MODEL), _rows(ts, 512), _rows(ts, 512),
                   _full((D_MODEL, D_MODEL)), _full((2, D_MODEL))],
        out_shape=[jax.ShapeDtypeStruct((1, 128), F32), jax.ShapeDtypeStruct((s, D_MODEL), F32),
                   jax.ShapeDtypeStruct((s, 512), F32), jax.ShapeDtypeStruct((s, 512), F32),
                   jax.ShapeDtypeStruct((D_MODEL, D_MODEL), F32), jax.ShapeDtypeStruct((2, D_MODEL), F32)],
        compiler_params=_cparams(dimension_semantics=("arbitrary",)),
    )(cat_a, cat_b, w_out, x, target, ln_g, ln_b)


def _swa_bwd(sinks, qa, k_pad, v_pad, attn, ga, d_cat_a, rope):
    s = qa.shape[0]
    sub = min(SWA_SUB, s // BLOCK)
    tq = sub * BLOCK
    nsteps = s // tq

    def body(sink_ref, qa_ref, ga_ref, at_ref, dc_ref, c_ref, s1_ref, s2_ref, bias_ref, k_ref, v_ref,
             dq_ref, dg_ref, dk_out, dv_out, ds_ref, dk_ref, dv_ref):
        n = pl.program_id(0)

        @pl.when(n == 0)
        def _():
            dk_ref[...] = jnp.zeros_like(dk_ref)
            dv_ref[...] = jnp.zeros_like(dv_ref)
            ds_ref[...] = jnp.zeros_like(ds_ref)

        low = lax.broadcasted_iota(jnp.int32, (2 * BLOCK, 128), 1) < SWA_HEAD_DIM
        for b in range(sub):
            rows = slice(BLOCK * b, BLOCK * (b + 1))
            start = pl.multiple_of((n * sub + b) * BLOCK, BLOCK)
            kw = k_ref[pl.ds(start, 2 * BLOCK), :]
            vw = v_ref[pl.ds(start, 2 * BLOCK), :]
            bias_t = _swa_bias_of(bias_ref, n, b)
            q = qa_ref[rows, :] * SWA_SCALE
            g = ga_ref[rows, :]
            sg = _sigmoid(g)
            o = at_ref[rows, :]
            dc = dc_ref[rows, :]
            do = dc * (g * sg)
            dg_ref[rows, :] = (dc * o * (sg * (1.0 + g * (1.0 - sg)))).astype(dg_ref.dtype)
            od = do * o
            c, s1, s2 = c_ref[rows, :], s1_ref[rows, :], s2_ref[rows, :]
            dk, dv = [], []
            for j in range(SWA_KV_HEADS):
                kd, vd = _swa_dup(kw, j), _swa_dup(vw, j)
                qs = _swa_stack(q, j).astype(MXU_DTYPE)
                dos = _swa_stack(do, j).astype(MXU_DTYPE)
                probs, psink = _swa_probs_t(qs, kd, bias_t, _swa_sink_row(sink_ref, j))
                delta = _row_sums_as_row(_swa_stack(od, j))
                dsc = (probs * (_mm_nt(vd, dos) - delta)).astype(MXU_DTYPE)
                dsink = psink * delta
                for r in range(SWA_GROUP):
                    h = SWA_GROUP * j + r
                    ds_ref[h:h + 1, :] += jnp.zeros((1, 128), F32) - jnp.sum(dsink[:, BLOCK * r:BLOCK * (r + 1)])
                dq = _swa_unstack(_mm_tn(dsc, kd))
                for i in range(2):
                    lanes = slice(128 * (2 * j + i), 128 * (2 * j + i + 1))
                    dq_ref[rows, lanes] = _rope_t(dq[i] * SWA_SCALE, c, s1, s2).astype(dq_ref.dtype)
                dkj = _mm(dsc, qs)
                dvj = _mm(probs.astype(MXU_DTYPE), dos)
                dk.append(dkj + pltpu.roll(dkj, SWA_HEAD_DIM, 1))
                dv.append(dvj + pltpu.roll(dvj, SWA_HEAD_DIM, 1))
            dk_ref[pl.ds(start, 2 * BLOCK), :] += jnp.where(low, dk[0], dk[1])
            dv_ref[pl.ds(start, 2 * BLOCK), :] += jnp.where(low, dv[0], dv[1])

        @pl.when(n == nsteps - 1)
        def _():
            dk_out[...] = dk_ref[BLOCK:, :]
            dv_out[...] = dv_ref[BLOCK:, :]

    return pl.pallas_call(
        body, name="swa_bwd", grid=(nsteps,),
        in_specs=[pl.BlockSpec(memory_space=pltpu.SMEM)] + [_rows(tq, 512)] * 4 + [_rows(tq, 128)] * 3
        + [_full((2, 2 * BLOCK, SWA_ROWS))] + [_full((s + BLOCK, 128))] * 2,
        out_specs=[_rows(tq, 512), _rows(tq, 512), _full((s, 128)), _full((s, 128)), _full((SWA_Q_HEADS, 128))],
        out_shape=[jax.ShapeDtypeStruct((s, 512), MXU_DTYPE), jax.ShapeDtypeStruct((s, 512), MXU_DTYPE),
                   jax.ShapeDtypeStruct((s, 128), F32), jax.ShapeDtypeStruct((s, 128), F32),
                   jax.ShapeDtypeStruct((SWA_Q_HEADS, 128), F32)],
        scratch_shapes=[pltpu.VMEM((s + BLOCK, 128), F32)] * 2,
        compiler_params=_cparams(dimension_semantics=("arbitrary",)),
    )(sinks, qa, ga, attn, d_cat_a, *rope, _swa_bias(), k_pad, v_pad)


def _gla_bwd(qb, kb, vb, la, oms, gb, o, sprev, d_cat_b, rb, wg, norm_w):
    s = qb.shape[0]
    tb = min(512, s)
    ch = tb // GLA_CHUNK
    nb = s // tb

    def body(qb_ref, kb_ref, vb_ref, la_ref, oms_ref, gb_ref, o_ref, sp_ref, dc_ref, rb_ref, wg_ref, nw_ref,
             dq_ref, dk_ref, dv_ref, dg_ref, dr_ref, gwg_ref, gbg_ref, gnw_ref, dst_ref):
        @pl.when(pl.program_id(0) == 0)
        def _():
            dst_ref[...] = jnp.zeros_like(dst_ref)
            gwg_ref[...] = jnp.zeros_like(gwg_ref)
            gbg_ref[...] = jnp.zeros_like(gbg_ref)
            gnw_ref[...] = jnp.zeros_like(gnw_ref)

        causal, causal_t = _gla_masks()
        nw = nw_ref[...]
        b = _chunk_cumsum(la_ref[...], True)
        bl = _chunk_last(b)
        eb, enb, ee, dec = jnp.exp(b), jnp.exp(-b), jnp.exp(bl - b), jnp.exp(bl)
        k = kb_ref[...]
        qd = (qb_ref[...] * GLA_SCALE) * eb
        ki = k * enb
        ke = k * ee
        qd16, ki16, ke16 = qd.astype(MXU_DTYPE), ki.astype(MXU_DTYPE), ke.astype(MXU_DTYPE)
        v16 = vb_ref[...].astype(MXU_DTYPE)

        g = gb_ref[...]
        sg = _sigmoid(g)
        silu = g * sg
        dsilu = sg * (1.0 + g * (1.0 - sg))
        gnw = jnp.zeros((1, GLA_DV), F32)
        do = []
        for h in range(GLA_HEADS):
            lv = slice(GLA_DV * h, GLA_DV * (h + 1))
            oh = o_ref[:, lv]
            dch = dc_ref[:, lv]
            r = lax.rsqrt(jnp.mean(oh * oh, axis=1, keepdims=True) + EPS)
            d_on = dch * silu[:, lv]
            dg_ref[:, lv] = (dch * (oh * r * nw) * dsilu[:, lv]).astype(dg_ref.dtype)
            gnw += jnp.sum(d_on * oh * r, axis=0, keepdims=True)
            u = d_on * nw
            do.append(r * u - oh * (r * r * r) * jnp.mean(u * oh, axis=1, keepdims=True))
        gnw_ref[...] += gnw
        do16 = jnp.concatenate(do, axis=1).astype(MXU_DTYPE)

        db, dbl = [None] * ch, [None] * ch
        for ci in reversed(range(ch)):
            rows = slice(GLA_CHUNK * ci, GLA_CHUNK * (ci + 1))
            qds, kis, kes = (_head_stack(t[rows], GLA_DK) for t in (qd16, ki16, ke16))
            vs, dos = _head_stack(v16[rows], GLA_DV), _head_stack(do16[rows], GLA_DV)
            a = jnp.where(causal, _mm_nt(qd16[rows], kis), 0.0).astype(MXU_DTYPE)
            at = jnp.where(causal_t, _mm_nt(ki16[rows], qds), 0.0).astype(MXU_DTYPE)
            da = jnp.where(causal, _mm_nt(do16[rows], vs), 0.0).astype(MXU_DTYPE)
            dat = jnp.where(causal_t, _mm_nt(v16[rows], dos), 0.0).astype(MXU_DTYPE)
            st = sp_ref[ci]
            dst = dst_ref[...]
            dst16 = dst.astype(MXU_DTYPE)
            dv = _mm(at, dos) + _rows_to_heads(_mm_nt(kes, dst16))
            dqd = _mm(da, kis) + _mm(do16[rows], _state_by_head(st.astype(MXU_DTYPE)))
            dki = _mm(dat, qds)
            dke = _mm(v16[rows], _state_by_head(dst16))
            ddec = jnp.sum(dst * st, axis=0, keepdims=True)
            decc = dec[rows][0:1]
            dst_ref[...] = _mm_tn(_heads_to_rows(do16[rows]), qds) + dst * decc
            dq_ref[rows, :] = (dqd * eb[rows] * GLA_SCALE).astype(dq_ref.dtype)
            dk_ref[rows, :] = (dki * enb[rows] + dke * ee[rows]).astype(dk_ref.dtype)
            dv_ref[rows, :] = dv.astype(dv_ref.dtype)
            dke_ke = dke * ke[rows]
            db[ci] = dqd * qd[rows] - dki * ki[rows] - dke_ke
            dbl[ci] = jnp.broadcast_to(jnp.sum(dke_ke, axis=0, keepdims=True) + ddec * decc, (GLA_CHUNK, GLA_KW))

        dla = _chunk_cumsum(jnp.concatenate(db, axis=0), False) + jnp.concatenate(dbl, axis=0)
        dlogit = dla * oms_ref[...] * (1.0 / GLA_TAU)
        dl16 = dlogit.astype(MXU_DTYPE)
        gbg_ref[...] += jnp.sum(dlogit, axis=0, keepdims=True)
        gwg_ref[...] += _mm_tn(rb_ref[...].astype(MXU_DTYPE), dl16)
        dr_ref[...] = _mm_nt(dl16, wg_ref[...]).astype(dr_ref.dtype)

    def rev(width):
        return pl.BlockSpec((tb, width), lambda i: (nb - 1 - i, 0))

    return pl.pallas_call(
        body, name="gla_bwd", grid=(nb,),
        in_specs=[rev(256), rev(256), rev(512), rev(256), rev(256), rev(512), rev(512),
                  pl.BlockSpec((ch, GLA_DV, 256), lambda i: (nb - 1 - i, 0, 0)), rev(512), rev(GLA_RANK),
                  _full((GLA_RANK, 256)), _full((1, 128))],
        out_specs=[rev(256), rev(256), rev(512), rev(512), rev(GLA_RANK),
                   _full((GLA_RANK, 256)), _full((1, 256)), _full((1, 128))],
        out_shape=[jax.ShapeDtypeStruct((s, 256), MXU_DTYPE), jax.ShapeDtypeStruct((s, 256), MXU_DTYPE),
                   jax.ShapeDtypeStruct((s, 512), MXU_DTYPE), jax.ShapeDtypeStruct((s, 512), MXU_DTYPE),
                   jax.ShapeDtypeStruct((s, GLA_RANK), MXU_DTYPE), jax.ShapeDtypeStruct((GLA_RANK, 256), F32),
                   jax.ShapeDtypeStruct((1, 256), F32), jax.ShapeDtypeStruct((1, 128), F32)],
        scratch_shapes=[pltpu.VMEM((GLA_DV, GLA_KW), F32)],
        compiler_params=_cparams(dimension_semantics=("arbitrary",)),
    )(qb, kb, vb, la, oms, gb, o, sprev, d_cat_b, rb, wg, norm_w)


def _dproj_tiles(piece_refs, rope_refs, members=(0, 1, 3, 4, 5, 6, 7, 8)):
    for i in members:
        if i == 1:
            dk = _rope_t(piece_refs[1][...], *(r[...] for r in rope_refs))
            yield OFF[1], OFF[3], jnp.concatenate([dk, piece_refs[2][...]], axis=1).astype(MXU_DTYPE)
        else:
            yield OFF[i], OFF[i + 1], piece_refs[i][...].astype(MXU_DTYPE)


def _in_proj_bwd_x(gx0, pieces, w_in, rope):
    s = gx0.shape[0]
    ts = min(512, s)
    widths = [OFF[i + 1] - OFF[i] for i in range(9)]

    def body(gx0_ref, *refs):
        w_ref, gx_ref = refs[12:]
        acc = gx0_ref[...]
        for lo, hi, t16 in _dproj_tiles(refs[:9], refs[9:12]):
            acc += _mm(t16, w_ref[lo:hi, :])
        gx_ref[...] = acc

    return pl.pallas_call(
        body, name="in_proj_bwd_x", grid=(s // ts,),
        in_specs=[_rows(ts, D_MODEL)] + [_rows(ts, w) for w in widths] + [_rows(ts, 128)] * 3
        + [_full((D_IN_PROJ, D_MODEL))],
        out_specs=_rows(ts, D_MODEL),
        out_shape=jax.ShapeDtypeStruct((s, D_MODEL), F32),
        compiler_params=_cparams(dimension_semantics=("arbitrary",)),
    )(gx0, *pieces, *rope, w_in)


GW_PASSES = 2
GW_EVENT_STEPS = (0, 2, 4, 6)


def _in_proj_bwd_w(x, pieces, rope, parts_wg, g_ln, g_bg, g_nw, g_sinks, loss, parts_w_out):
    s = x.shape[0]
    ts = min(1024, s)
    nt = s // ts
    n_pass = GW_PASSES
    n_steps = n_pass * nt
    cw = D_MODEL // n_pass
    n_chips = N_DEV // 2
    blk = (D_IN_SHARD, cw)
    out_blk = parts_w_out.shape[1:]
    w_out_forward_step = min(1, n_steps - 1)
    w_out_finish_step = max(n_steps - 2, w_out_forward_step)

    def body(x_hbm, *refs):
        piece_refs, rope_refs = refs[:9], refs[9:12]
        (pwg_ref, gln_ref, gbg_ref, gnw_ref, gsk_ref, loss_ref, pout_ref,
         gin_ref, rwg_ref, rsm_ref, gout_ref) = refs[12:23]
        (acc_ref, stage_ref, sib_ref, snd_ref, rcv_ref, sm_ref, xs_ref,
         d2d_send, d2d_recv, ici_send, ici_recv, out_sems, sm_send, sm_recv, sm_loc, x_sems) = refs[23:39]
        w_out_sum = _OwnerSum(pout_ref, *refs[39:])
        p, t = pl.program_id(0), pl.program_id(1)
        step = p * nt + t
        tile = pl.ds(pl.multiple_of(t * ts, ts), ts)

        def x_load(k, q):
            rows = pl.ds(pl.multiple_of(k * ts, ts), ts)
            return pltpu.make_async_copy(x_hbm.at[rows, pl.ds(q * cw, cw)], xs_ref.at[q, rows, :],
                                         x_sems.at[q * nt + k])
        x_, y_, c = _mesh_pos()
        me, mychip, sibling = 4 * x_ + 2 * y_ + c, 2 * x_ + y_, (x_, y_, 1 - c)
        small_dsts = (rwg_ref, rsm_ref)

        def small_src(a, block):
            return pwg_ref.at[block] if a == 0 else sm_ref

        def small_copy(k, a, src_block, dst_block, peer):
            i = 2 * (k - 1) + a
            return pltpu.make_async_remote_copy(
                src_ref=small_src(a, src_block), dst_ref=small_dsts[a].at[dst_block], send_sem=sm_send.at[i],
                recv_sem=sm_recv.at[i], device_id=peer, device_id_type=pl.DeviceIdType.MESH)

        def small_local(a):
            return pltpu.make_async_copy(small_src(a, me), small_dsts[a].at[me], sm_loc.at[a])

        @pl.when(step == 0)
        def _():
            for q in range(n_pass):
                for k in range(nt):
                    x_load(k, q).start()
            w_out_sum.start()
            acc_ref[...] = jnp.zeros_like(acc_ref)
            sm_ref[...] = jnp.zeros_like(sm_ref)
            for r in range(D_MODEL // 128):
                sm_ref[r:r + 1, :] = gln_ref[0:1, 128 * r:128 * (r + 1)]
                sm_ref[8 + r:9 + r, :] = gln_ref[1:2, 128 * r:128 * (r + 1)]
            for r in range(2):
                sm_ref[16 + r:17 + r, :] = gbg_ref[0:1, 128 * r:128 * (r + 1)]
            sm_ref[24:25, :] = gnw_ref[...]
            diag = (lax.broadcasted_iota(jnp.int32, gsk_ref.shape, 0)
                    == lax.broadcasted_iota(jnp.int32, gsk_ref.shape, 1))
            sm_ref[32:33, :] = jnp.sum(jnp.where(diag, gsk_ref[...], 0.0), axis=0, keepdims=True)
            sm_ref[40:41, :] = loss_ref[...]
            for a in range(2):
                small_local(a).start()
            for k in range(1, N_DEV):
                peer, pidx = _peer(k, x_, y_, c)
                for a in range(2):
                    small_copy(k, a, pidx, me, peer).start()

        for q in range(n_pass):
            @pl.when(p == q)
            def _(q=q):
                x_load(t, q).wait()

        xb = xs_ref[p, tile, :]
        for lo, hi, t16 in _dproj_tiles(piece_refs, rope_refs):
            acc_ref[p, lo:hi, :] += _mm_tn(t16, xb)

        @pl.when(step == w_out_forward_step)
        def _():
            w_out_sum.forward()

        @pl.when(step == w_out_finish_step)
        def _():
            gout_ref[...] = w_out_sum.finish()

        def block_rows(q, j):
            return acc_ref[q, D_IN_SHARD * j:D_IN_SHARD * (j + 1), :]

        def d2d(q):
            return pltpu.make_async_remote_copy(
                src_ref=stage_ref, dst_ref=sib_ref.at[q], send_sem=d2d_send.at[q],
                recv_sem=d2d_recv.at[q], device_id=sibling, device_id_type=pl.DeviceIdType.MESH)

        def ici(q, slot, owner):
            i = 3 * q + slot
            return pltpu.make_async_remote_copy(
                src_ref=snd_ref.at[q, slot], dst_ref=rcv_ref.at[q, slot], send_sem=ici_send.at[i],
                recv_sem=ici_recv.at[i], device_id=owner, device_id_type=pl.DeviceIdType.MESH)

        def out_copy(q):
            return pltpu.make_async_copy(sib_ref.at[q, mychip], gin_ref.at[:, pl.ds(q * cw, cw)], out_sems.at[q])

        first = (jnp.where(c == 0, 1 - x_, x_), jnp.where(c == 0, y_, 1 - y_))
        second = (jnp.where(c == 0, x_, 1 - x_), jnp.where(c == 0, 1 - y_, y_))

        def chip_of(pos):
            return 2 * pos[0] + pos[1]

        def to_sibling(q):
            if q >= 1:
                d2d(q - 1).wait_send()
            for cc in range(2):
                @pl.when(c == cc)
                def _(cc=cc):
                    for k in range(n_chips):
                        stage_ref[k] = block_rows(q, 2 * k + 1 - cc)
            d2d(q).start()

        def chip_sums_leave(q):
            d2d(q).wait_recv()
            for cc in range(2):
                @pl.when(c == cc)
                def _(cc=cc):
                    for k in range(n_chips):
                        sib_ref[q, k] = block_rows(q, 2 * k + cc) + sib_ref[q, k]
            snd_ref[q, 2] = sib_ref[q, chip_of((1 - x_, 1 - y_))].astype(snd_ref.dtype)
            ici(q, 2, (*first, c)).start()
            snd_ref[q, 0] = sib_ref[q, chip_of(first)].astype(snd_ref.dtype)
            ici(q, 0, (*first, c)).start()

        def combined_sum_leaves(q):
            ici(q, 2, sibling).wait_recv()
            snd_ref[q, 1] = (sib_ref[q, chip_of(second)] + rcv_ref[q, 2].astype(F32)).astype(snd_ref.dtype)
            ici(q, 1, (*second, c)).start()

        def owner_total(q):
            total = sib_ref[q, mychip]
            for slot in range(2):
                ici(q, slot, sibling).wait_recv()
                total = total + rcv_ref[q, slot].astype(F32)
            sib_ref[q, mychip] = total
            out_copy(q).start()

        stages = (to_sibling, chip_sums_leave, combined_sum_leaves, owner_total)
        events = sorted((min((q + 1) * nt - 1 + GW_EVENT_STEPS[i], n_steps - 1), i > 0, 2 * q + i, -q, q, i)
                        for q in range(n_pass) for i in range(len(stages)))
        for at_step, _, _, _, q, i in events:
            @pl.when(step == at_step)
            def _(q=q, i=i):
                stages[i](q)

        @pl.when(step == n_steps - 1)
        def _():
            for k in range(1, N_DEV):
                peer, pidx = _peer(k, x_, y_, c)
                for a in range(2):
                    small_copy(k, a, me, pidx, peer).wait_recv()
            for k in range(1, N_DEV):
                peer, pidx = _peer(k, x_, y_, c)
                for a in range(2):
                    small_copy(k, a, pidx, me, peer).wait_send()
            for a in range(2):
                small_local(a).wait()
            d2d(n_pass - 1).wait_send()
            for q in range(n_pass):
                for slot in range(3):
                    ici(q, slot, sibling).wait_send()
            for q in range(n_pass):
                out_copy(q).wait()

    widths = [OFF[i + 1] - OFF[i] for i in range(9)]
    hbm = pl.BlockSpec(memory_space=pl.ANY)
    vmem = pl.BlockSpec(memory_space=pltpu.VMEM)

    def token_tile(width):
        return pl.BlockSpec((ts, width), lambda p, t: (t, 0))

    return pl.pallas_call(
        body, name="in_proj_bwd_w", grid=(n_pass, nt),
        in_specs=[hbm] + [token_tile(w) for w in widths] + [token_tile(128)] * 3 + [hbm] + [vmem] * 5 + [hbm],
        out_specs=[hbm, hbm, hbm, vmem],
        out_shape=[jax.ShapeDtypeStruct((D_IN_SHARD, D_MODEL), F32),
                   jax.ShapeDtypeStruct((N_DEV,) + parts_wg.shape[1:], F32),
                   jax.ShapeDtypeStruct((N_DEV, SMALL_ROWS, 128), F32), jax.ShapeDtypeStruct(out_blk, F32)],
        scratch_shapes=[pltpu.VMEM((n_pass, D_IN_PROJ, cw), F32), pltpu.VMEM((n_chips,) + blk, F32),
                        pltpu.VMEM((n_pass, n_chips) + blk, F32), pltpu.VMEM((n_pass, 3) + blk, MXU_DTYPE),
                        pltpu.VMEM((n_pass, 3) + blk, MXU_DTYPE),
                        pltpu.VMEM((SMALL_ROWS, 128), F32), pltpu.VMEM((n_pass, s, cw), MXU_DTYPE),
                        pltpu.SemaphoreType.DMA((n_pass,)), pltpu.SemaphoreType.DMA((n_pass,)),
                        pltpu.SemaphoreType.DMA((3 * n_pass,)), pltpu.SemaphoreType.DMA((3 * n_pass,)),
                        pltpu.SemaphoreType.DMA((n_pass,)),
                        pltpu.SemaphoreType.DMA((2 * (N_DEV - 1),)), pltpu.SemaphoreType.DMA((2 * (N_DEV - 1),)),
                        pltpu.SemaphoreType.DMA((2,)), pltpu.SemaphoreType.DMA((n_pass * nt,))]
        + _OwnerSum.scratch(out_blk),
        compiler_params=_cparams(dimension_semantics=("arbitrary", "arbitrary")),
    )(x, *pieces, *rope, parts_wg, g_ln, g_bg, g_nw, g_sinks, loss, parts_w_out)


def _local_step(x, positions, w_in_t, wg_s, b_gate, sinks, norm_w, w_out_s, ln_g, ln_b, target):
    qa, k_pad, v_pad, ga, qb, kb, vb, gb, rb, la, oms, *rope, x16, w_in, wg, w_out = _in_proj(
        x, w_in_t, wg_s, b_gate, _rope_angles(positions), w_out_s)
    attn, cat_a = _swa_fwd(sinks, qa, k_pad, v_pad, ga)
    o, cat_b, sprev = _gla_fwd(qb, kb, vb, la, gb, norm_w)
    loss, gx0, d_cat_a, d_cat_b, g_w_out, g_ln = _out_ln_loss(cat_a, cat_b, w_out, x, target, ln_g, ln_b)
    parts_w_out = g_w_out.reshape(N_DEV, D_OUT_SHARD, D_MODEL)
    dqa, dga, dka, dva, g_sinks = _swa_bwd(sinks, qa, k_pad, v_pad, attn, ga, d_cat_a, rope)
    dqb, dkb, dvb, dgb, drb, g_wg, g_bg, g_nw = _gla_bwd(qb, kb, vb, la, oms, gb, o, sprev, d_cat_b, rb, wg, norm_w)
    pieces = (dqa, dka, dva, dga, dqb, dkb, dvb, dgb, drb)
    grad_x = _in_proj_bwd_x(gx0, pieces, w_in, rope)
    parts_wg = jnp.transpose(g_wg.reshape(GLA_RANK, N_DEV, 32), (1, 0, 2))
    g_in, r_wg, r_small, g_out = _in_proj_bwd_w(
        x16, pieces, rope, parts_wg, g_ln, g_bg, g_nw, g_sinks, loss, parts_w_out)
    return grad_x, g_in, g_out, r_wg, r_small


def _mesh_pos():
    return lax.axis_index("x"), lax.axis_index("y"), lax.axis_index("c")


def _peer(k, x, y, c):
    px = (1 - x) if k & 4 else x
    py = (1 - y) if k & 2 else y
    pc = (1 - c) if k & 1 else c
    return (px, py, pc), 4 * px + 2 * py + pc


def _other_chips(x, y):
    return [(1 - x, y), (x, 1 - y), (1 - x, 1 - y)]


def _shard_view(t):
    return jnp.transpose(t, (2, 0, 1))


class _BlockGather:
    def __init__(self, slots, send_sems, recv_sems):
        self.slots, self.send_sems, self.recv_sems = slots, send_sems, recv_sems
        x, y, c = _mesh_pos()
        self.c, self.me, self.sibling = c, 4 * x + 2 * y + c, (x, y, 1 - c)
        first = (jnp.where(c == 0, 1 - x, x), jnp.where(c == 0, y, 1 - y))
        second = (jnp.where(c == 0, x, 1 - x), jnp.where(c == 0, 1 - y, y))
        self.chips = [first, second, (1 - x, 1 - y)]

    @staticmethod
    def scratch():
        return [pltpu.SemaphoreType.DMA((N_DEV - 1,)), pltpu.SemaphoreType.DMA((N_DEV - 1,))]

    def _copy(self, k, block, to):
        return pltpu.make_async_remote_copy(
            src_ref=self.slots.at[block], dst_ref=self.slots.at[block], send_sem=self.send_sems.at[k],
            recv_sem=self.recv_sems.at[k], device_id=to, device_id_type=pl.DeviceIdType.MESH)

    def _block(self, j, c):
        cx, cy = self.chips[j]
        return 4 * cx + 2 * cy + c

    def _dev(self, j):
        return (*self.chips[j], self.c)

    def start(self):
        self._copy(1, self.me, self._dev(0)).start()
        self._copy(2, self.me, self._dev(1)).start()
        self._copy(0, self.me, self.sibling).start()

    def forward(self):
        self._copy(1, self._block(0, self.c), self.sibling).wait_recv()
        self._copy(3, self._block(0, self.c), self._dev(1)).start()
        self._copy(4, self._block(0, self.c), self.sibling).start()
        self._copy(2, self._block(1, self.c), self.sibling).wait_recv()
        self._copy(5, self._block(1, self.c), self.sibling).start()

    def forward_far(self):
        self._copy(3, self._block(2, self.c), self.sibling).wait_recv()
        self._copy(6, self._block(2, self.c), self.sibling).start()

    def finish(self):
        for k in (0, 4, 5, 6):
            self._copy(k, self.me, self.sibling).wait_recv()
        for k in range(N_DEV - 1):
            self._copy(k, self.me, self.sibling).wait_send()


class _OwnerSum:
    def __init__(self, parts, own, sib, snd, rcv, loc_sems, d2d_send, d2d_recv, ici_send, ici_recv):
        self.parts, self.own, self.sib, self.snd, self.rcv = parts, own, sib, snd, rcv
        self.sems = (loc_sems, d2d_send, d2d_recv, ici_send, ici_recv)
        x, y, c = _mesh_pos()
        self.c, self.sibling = c, (x, y, 1 - c)
        self.chips = [(x, y)] + _other_chips(x, y)

    @staticmethod
    def scratch(block):
        return [pltpu.VMEM((4,) + block, F32), pltpu.VMEM((4,) + block, F32),
                pltpu.VMEM((3,) + block, MXU_DTYPE), pltpu.VMEM((3,) + block, MXU_DTYPE),
                pltpu.SemaphoreType.DMA((4,)), pltpu.SemaphoreType.DMA((4,)), pltpu.SemaphoreType.DMA((4,)),
                pltpu.SemaphoreType.DMA((3,)), pltpu.SemaphoreType.DMA((3,))]

    def _local(self, r):
        cx, cy = self.chips[r]
        return pltpu.make_async_copy(self.parts.at[4 * cx + 2 * cy + self.c], self.own.at[r], self.sems[0].at[r])

    def _d2d(self, r):
        cx, cy = self.chips[r]
        return pltpu.make_async_remote_copy(
            src_ref=self.parts.at[4 * cx + 2 * cy + (1 - self.c)], dst_ref=self.sib.at[r], send_sem=self.sems[1].at[r],
            recv_sem=self.sems[2].at[r], device_id=self.sibling, device_id_type=pl.DeviceIdType.MESH)

    def _ici(self, r):
        cx, cy = self.chips[r]
        return pltpu.make_async_remote_copy(
            src_ref=self.snd.at[r - 1], dst_ref=self.rcv.at[r - 1], send_sem=self.sems[3].at[r - 1],
            recv_sem=self.sems[4].at[r - 1], device_id=(cx, cy, self.c), device_id_type=pl.DeviceIdType.MESH)

    def start(self):
        for r in (1, 2, 3, 0):
            self._local(r).start()
            self._d2d(r).start()

    def forward(self):
        for r in (1, 2, 3):
            self._local(r).wait()
            self._d2d(r).wait_recv()
            self.snd[r - 1] = (self.own[r] + self.sib[r]).astype(self.snd.dtype)
            self._ici(r).start()

    def finish(self):
        self._local(0).wait()
        self._d2d(0).wait_recv()
        acc = self.own[0] + self.sib[0]
        for r in (1, 2, 3):
            self._ici(r).wait_recv()
            acc = acc + self.rcv[r - 1].astype(F32)
        for r in range(4):
            self._d2d(r).wait_send()
        for r in (1, 2, 3):
            self._ici(r).wait_send()
        return acc


SMALL_ROWS = 48


def _adamw_math(g, w, m, v):
    nm = ADAM_B1 * m + (1.0 - ADAM_B1) * g
    nv = ADAM_B2 * v + (1.0 - ADAM_B2) * (g * g)
    m_hat = nm / (1.0 - ADAM_B1 ** ADAM_STEP)
    v_hat = nv / (1.0 - ADAM_B2 ** ADAM_STEP)
    return -ADAM_LR * (m_hat / (jnp.sqrt(v_hat) + ADAM_EPS) + ADAM_WD * w), nm, nv


def _adamw_shard_view(g, w, m, v):
    rows, width = g.shape
    parts = 3
    pr = rows // parts
    assert pr * parts == rows

    def body(g_ref, w_hbm, m_hbm, v_hbm, g_out, d_out, nm_out, nv_out, bufs, outs, sems):
        def load(k, i, src):
            return pltpu.make_async_copy(src.at[pl.ds(pr * k, pr), 0, :], bufs.at[k, i], sems.at[7 * k + i])

        def store(k, i, dst):
            return pltpu.make_async_copy(outs.at[k, i], dst.at[pl.ds(pr * k, pr), 0, :], sems.at[7 * k + 3 + i])

        for k in range(parts):
            for i, src in enumerate((w_hbm, m_hbm, v_hbm)):
                load(k, i, src).start()
        for k in range(parts):
            for i, src in enumerate((w_hbm, m_hbm, v_hbm)):
                load(k, i, src).wait()
            gk = g_ref[pr * k:pr * (k + 1), :]
            outs[k, 0] = gk
            outs[k, 1], outs[k, 2], outs[k, 3] = _adamw_math(gk, bufs[k, 0], bufs[k, 1], bufs[k, 2])
            for i, dst in enumerate((g_out, d_out, nm_out, nv_out)):
                store(k, i, dst).start()
        for k in range(parts):
            for i, dst in enumerate((g_out, d_out, nm_out, nv_out)):
                store(k, i, dst).wait()

    hbm = pl.BlockSpec(memory_space=pl.ANY)
    return pl.pallas_call(
        body, name="adamw_w_in",
        in_specs=[pl.BlockSpec(memory_space=pltpu.VMEM), hbm, hbm, hbm], out_specs=[hbm] * 4,
        out_shape=[jax.ShapeDtypeStruct((rows, 1, width), F32)] * 4,
        scratch_shapes=[pltpu.VMEM((parts, 3, pr, width), F32), pltpu.VMEM((parts, 4, pr, width), F32),
                        pltpu.SemaphoreType.DMA((7 * parts,))],
        compiler_params=_cparams(),
    )(g, w, m, v)


def _adamw_vectors(r_small, r_wg, g_out, params):
    n_par = len(params)

    def body(rsm_ref, rwg_ref, gout_ref, *refs):
        ins, outs = refs[:3 * n_par], refs[3 * n_par:]
        g = rsm_ref[0]
        gwg = rwg_ref[0]
        for j in range(1, N_DEV):
            g = g + rsm_ref[j]
            gwg = gwg + rwg_ref[j]
        outs[4 * n_par][...] = g[40:41]
        grads = [gwg,
                 jnp.concatenate([g[r:r + 1] for r in range(0, 8)], axis=1),
                 jnp.concatenate([g[r:r + 1] for r in range(8, 16)], axis=1),
                 jnp.concatenate([g[16:17], g[17:18]], axis=1),
                 g[24:25],
                 g[32:33, 0:SWA_Q_HEADS],
                 gout_ref[...]]
        for p, gp in enumerate(grads):
            w_ref, m_ref, v_ref = ins[3 * p:3 * p + 3]
            outs[4 * p][...] = gp
            outs[4 * p + 1][...], outs[4 * p + 2][...], outs[4 * p + 3][...] = _adamw_math(
                gp, w_ref[...], m_ref[...], v_ref[...])

    vmem = pl.BlockSpec(memory_space=pltpu.VMEM)
    flat = [t for wmv in params for t in wmv]
    return pl.pallas_call(
        body, name="adamw_vectors",
        in_specs=[vmem] * (3 + len(flat)), out_specs=[vmem] * (4 * n_par + 1),
        out_shape=[jax.ShapeDtypeStruct(wmv[0].shape, F32) for wmv in params for _ in range(4)]
        + [jax.ShapeDtypeStruct((1, 128), F32)],
        compiler_params=_cparams(),
    )(r_small, r_wg, g_out, *flat)


def kernel(x, positions, w_in, gla_w_gate_up, gla_b_gate, attn_sinks, gla_norm_w, w_out, ln_g, ln_b, loss_target, m_w_in, m_gla_w_gate_up, m_gla_b_gate, m_attn_sinks, m_gla_norm_w, m_w_out, m_ln_g, m_ln_b, v_w_in, v_gla_w_gate_up, v_gla_b_gate, v_attn_sinks, v_gla_norm_w, v_w_out, v_ln_g, v_ln_b):
    grad_x, g_in, g_out, r_wg, r_small = _local_step(
        x[0], positions[0], _shard_view(w_in), gla_w_gate_up[0], gla_b_gate, attn_sinks[0], gla_norm_w, w_out[0],
        ln_g, ln_b, loss_target[0])

    upd_in = _adamw_shard_view(g_in, _shard_view(w_in), _shard_view(m_w_in), _shard_view(v_w_in))
    upd_in = [jnp.transpose(t, (1, 2, 0)) for t in upd_in]
    vec = _adamw_vectors(r_small, r_wg, g_out, [
        (gla_w_gate_up[0], m_gla_w_gate_up[0], v_gla_w_gate_up[0]), (ln_g, m_ln_g, v_ln_g), (ln_b, m_ln_b, v_ln_b),
        (gla_b_gate, m_gla_b_gate, v_gla_b_gate), (gla_norm_w, m_gla_norm_w, v_gla_norm_w),
        (attn_sinks, m_attn_sinks, v_attn_sinks), (w_out[0], m_w_out[0], v_w_out[0])])

    outs = [vec[28][0, 0], grad_x[None]]
    for kind in range(4):
        u_wg, u_ln_g, u_ln_b, u_bg, u_nw, u_sinks, u_out = (vec[4 * p + kind] for p in range(7))
        outs += [upd_in[kind], u_wg[None], u_bg, u_sinks, u_nw, u_out[None], u_ln_g, u_ln_b]
    return tuple(outs)
```

```python
import jax
import jax.numpy as jnp
from jax import lax
from jax.experimental import pallas as pl
from jax.experimental.pallas import tpu as pltpu

F32 = jnp.float32
MXU_DTYPE = jnp.bfloat16

N_DEV = 8
D_MODEL = 1024
SWA_Q_HEADS = 8
SWA_KV_HEADS = 2
SWA_GROUP = 4
SWA_HEAD_DIM = 64
BLOCK = 128
ROPE_THETA = 500000.0
ROT_DIM = 16
GLA_HEADS = 4
GLA_DK = 64
GLA_DV = 128
GLA_RANK = 16
GLA_TAU = 16.0
GLA_CHUNK = 64
D_IN_PROJ = 2832
D_IN_SHARD = D_IN_PROJ // N_DEV
D_OUT_SHARD = D_MODEL // N_DEV
OFF = (0, 512, 640, 768, 1280, 1536, 1792, 2304, 2816, 2832)
EPS = 1e-5
ALPHA = 2.0 ** 0.25
SWA_SCALE = SWA_HEAD_DIM ** -0.5
GLA_SCALE = GLA_DK ** -0.5
ADAM_LR = 0.001
ADAM_B1 = 0.9
ADAM_B2 = 0.999
ADAM_EPS = 1e-08
ADAM_WD = 0.01
ADAM_STEP = 10
VMEM_LIMIT = 56 * 1024 * 1024

_NT = (((1,), (1,)), ((), ()))
_TN = (((0,), (0,)), ((), ()))


def _mm(a, b):
    return jnp.dot(a, b, preferred_element_type=F32)


def _mm_nt(a, b):
    return lax.dot_general(a, b, _NT, preferred_element_type=F32)


def _mm_tn(a, b):
    return lax.dot_general(a, b, _TN, preferred_element_type=F32)


def _sigmoid(t):
    return 1.0 / (1.0 + jnp.exp(-t))


def _cparams(**kw):
    return pltpu.CompilerParams(vmem_limit_bytes=VMEM_LIMIT, **kw)


def _full(shape):
    return pl.BlockSpec(shape, lambda *_: (0,) * len(shape))


def _rows(tile, width):
    return pl.BlockSpec((tile, width), lambda i: (i, 0))


def _rope_angles(positions):
    half = ROT_DIM // 2
    inv_freq = ROPE_THETA ** (-jnp.arange(half, dtype=F32) / half)
    ang = positions.astype(F32)[:, None] * inv_freq[None, :]
    return jnp.concatenate([jnp.cos(ang), jnp.sin(ang)], axis=1)


def _split3_parts(t):
    hi = t.astype(MXU_DTYPE)
    r1 = t - hi.astype(F32)
    mid = r1.astype(MXU_DTYPE)
    return hi, mid, (r1 - mid.astype(F32)).astype(MXU_DTYPE)


def _rope_tables(cs):
    half = ROT_DIM // 2
    i = lax.broadcasted_iota(jnp.int32, (2 * half, 3 * 128), 0)
    lane = lax.broadcasted_iota(jnp.int32, (2 * half, 3 * 128), 1)
    table, pos = _idiv(lane, 128), lane & (SWA_HEAD_DIM - 1)
    is_c = (table == 0) & (pos < ROT_DIM) & ((pos & (half - 1)) == i)
    is_s1 = (table == 1) & (pos < half) & (pos + half == i)
    is_s2 = (table == 2) & (pos >= half) & (pos < ROT_DIM) & (pos == i)
    sel = jnp.where(is_c | is_s2, 1.0, jnp.where(is_s1, -1.0, 0.0)).astype(MXU_DTYPE)
    hi, mid, lo = _split3_parts(cs)
    t = (_mm(hi, sel) + _mm(mid, sel)) + _mm(lo, sel)
    pos1 = lax.broadcasted_iota(jnp.int32, (1, 128), 1) & (SWA_HEAD_DIM - 1)
    return t[:, 0:128] + jnp.where(pos1 >= ROT_DIM, 1.0, 0.0), t[:, 128:256], t[:, 256:384]


def _rope(t, c, s1, s2):
    return t * c + pltpu.roll(t, 120, 1) * s1 + pltpu.roll(t, 8, 1) * s2


def _rope_t(g, c, s1, s2):
    return g * c + pltpu.roll(g * s1, 8, 1) + pltpu.roll(g * s2, 120, 1)


def _in_proj(x, w_in_t, wg_s, b_gate, cos_sin, w_out_s):
    s = x.shape[0]
    ts = min(512, s)
    nsteps = s // ts
    forward_step, far_step = min(3, nsteps - 1), min(5, nsteps - 1)
    widths = [OFF[i + 1] - OFF[i] for i in range(9)]

    def body(x_ref, win_hbm, wgs_ref, bg_ref, cs_ref, wos_ref,
             qa_ref, ka_ref, va_ref, ga_ref, qb_ref, kb_ref, vb_ref, gb_ref, rb_ref, la_ref, oms_ref,
             c_ref, s1_ref, s2_ref, x16_ref, w_ref, wg_ref, wout_ref,
             win_all, wg_all, wout_all, stage, stage_sem, *sems):
        xb = x_ref[...].astype(MXU_DTYPE)
        x16_ref[...] = xb
        c, s1, s2 = _rope_tables(cs_ref[...])
        c_ref[...], s1_ref[...], s2_ref[...] = c, s1, s2
        i0 = pl.program_id(0)
        gather = _BlockGather(wout_all, *sems[0:2])

        @pl.when(i0 == 0)
        def _():
            ka_ref[0:BLOCK, :] = jnp.zeros((BLOCK, 128), ka_ref.dtype)
            va_ref[0:BLOCK, :] = jnp.zeros((BLOCK, 128), va_ref.dtype)
            first = (_BlockGather(win_all, *sems[2:4]), _BlockGather(wg_all, *sems[4:6]))
            load = pltpu.make_async_copy(win_hbm.at[:, 0, :], stage, stage_sem)
            load.start()
            wout_all[gather.me] = wos_ref[...].astype(wout_all.dtype)
            wg_all[gather.me] = wgs_ref[...].astype(wg_all.dtype)
            load.wait()
            win_all[gather.me] = stage[...].astype(win_all.dtype)
            for stage_of in ("start", "forward", "forward_far", "finish"):
                for g in first:
                    getattr(g, stage_of)()
            gather.start()
            for j in range(N_DEV):
                w_ref[D_IN_SHARD * j:D_IN_SHARD * (j + 1), :] = win_all[j]
                wg_ref[:, 32 * j:32 * (j + 1)] = wg_all[j]

        @pl.when(i0 == forward_step)
        def _():
            gather.forward()

        @pl.when(i0 == far_step)
        def _():
            gather.forward_far()

        @pl.when(i0 == nsteps - 1)
        def _():
            gather.finish()
            for j in range(N_DEV):
                wout_ref[D_OUT_SHARD * j:D_OUT_SHARD * (j + 1), :] = wout_all[j]

        kv_rows = pl.ds(pl.multiple_of(BLOCK + i0 * ts, BLOCK), ts)

        def cols(i):
            return _mm_nt(xb, w_ref[OFF[i]:OFF[i + 1], :])

        qa = cols(0)
        for i in range(4):
            qa_ref[:, 128 * i:128 * (i + 1)] = _rope(qa[:, 128 * i:128 * (i + 1)], c, s1, s2).astype(qa_ref.dtype)
        kv = _mm_nt(xb, w_ref[OFF[1]:OFF[3], :])
        ka_ref[kv_rows, :] = _rope(kv[:, 0:128], c, s1, s2).astype(ka_ref.dtype)
        va_ref[kv_rows, :] = kv[:, 128:256].astype(va_ref.dtype)
        ga_ref[...] = cols(3)
        qb_ref[...] = cols(4)
        kb_ref[...] = cols(5)
        vb_ref[...] = cols(6).astype(vb_ref.dtype)
        gb_ref[...] = cols(7)
        rb = cols(8)
        rb_ref[...] = rb
        logit = _mm(rb.astype(MXU_DTYPE), wg_ref[...]) + bg_ref[...]
        e = jnp.exp(-jnp.abs(logit))
        la_ref[...] = (jnp.minimum(logit, 0.0) - jnp.log(1.0 + e)) / GLA_TAU
        oms_ref[...] = jnp.where(logit >= 0.0, e, 1.0) / (1.0 + e)

    out_shape = [jax.ShapeDtypeStruct((s + BLOCK if i in (1, 2) else s, w), MXU_DTYPE if i in (0, 1, 2, 6) else F32)
                 for i, w in enumerate(widths)]
    out_shape += [jax.ShapeDtypeStruct((s, 256), F32)] * 2 + [jax.ShapeDtypeStruct((s, 128), F32)] * 3
    out_shape += [jax.ShapeDtypeStruct((s, D_MODEL), MXU_DTYPE)]
    out_shape += [jax.ShapeDtypeStruct((D_IN_PROJ, D_MODEL), MXU_DTYPE), jax.ShapeDtypeStruct((GLA_RANK, 256), MXU_DTYPE),
                  jax.ShapeDtypeStruct((D_MODEL, D_MODEL), MXU_DTYPE)]
    return pl.pallas_call(
        body, name="in_proj", grid=(nsteps,),
        in_specs=[_rows(ts, D_MODEL), pl.BlockSpec(memory_space=pl.ANY), _full((GLA_RANK, 32)), _full((1, 256)),
                  _rows(ts, ROT_DIM), _full((D_OUT_SHARD, D_MODEL))],
        out_specs=[_full((s + BLOCK, w)) if i in (1, 2) else _rows(ts, w) for i, w in enumerate(widths)]
        + [_rows(ts, 256)] * 2 + [_rows(ts, 128)] * 3 + [_rows(ts, D_MODEL)]
        + [_full((D_IN_PROJ, D_MODEL)), _full((GLA_RANK, 256)), _full((D_MODEL, D_MODEL))],
        out_shape=out_shape,
        scratch_shapes=[pltpu.VMEM((N_DEV, D_IN_SHARD, D_MODEL), MXU_DTYPE), pltpu.VMEM((N_DEV, GLA_RANK, 32), MXU_DTYPE),
                        pltpu.VMEM((N_DEV, D_OUT_SHARD, D_MODEL), MXU_DTYPE),
                        pltpu.VMEM((D_IN_SHARD, D_MODEL), F32), pltpu.SemaphoreType.DMA]
        + 3 * _BlockGather.scratch(),
        compiler_params=_cparams(dimension_semantics=("arbitrary",)),
    )(x, w_in_t, wg_s, b_gate, cos_sin, w_out_s)


SWA_ROWS = SWA_GROUP * BLOCK


def _swa_bias():
    shape = (2, 2 * BLOCK, SWA_ROWS)
    ki = lax.broadcasted_iota(jnp.int32, shape, 1)
    qi = lax.broadcasted_iota(jnp.int32, shape, 2) & (BLOCK - 1)
    first = lax.broadcasted_iota(jnp.int32, shape, 0) == 0
    dist = qi + BLOCK - ki
    ok = (dist >= 0) & (dist < BLOCK) & (jnp.logical_not(first) | (ki >= BLOCK))
    return jnp.where(ok, 0.0, -jnp.inf).astype(F32)


SWA_SUB = 8


def _swa_bias_of(bias_ref, n, b):
    return bias_ref[jnp.minimum(n, 1)] if b == 0 else bias_ref[1]


def _swa_dup(t, j):
    t = t.astype(F32)
    low = lax.broadcasted_iota(jnp.int32, t.shape, 1) < SWA_HEAD_DIM
    keep = low if j == 0 else jnp.logical_not(low)
    return jnp.where(keep, t, pltpu.roll(t, SWA_HEAD_DIM, 1)).astype(MXU_DTYPE)


def _swa_stack(t, j):
    low = lax.broadcasted_iota(jnp.int32, (BLOCK, 128), 1) < SWA_HEAD_DIM
    zero = jnp.zeros((BLOCK, 128), t.dtype)
    blocks = []
    for p in (2 * j, 2 * j + 1):
        tp = t[:, 128 * p:128 * (p + 1)]
        blocks += [jnp.where(low, tp, zero), jnp.where(low, zero, tp)]
    return jnp.concatenate(blocks, axis=0)


def _swa_unstack(t):
    low = lax.broadcasted_iota(jnp.int32, (BLOCK, 128), 1) < SWA_HEAD_DIM
    return [jnp.where(low, t[2 * BLOCK * i:2 * BLOCK * i + BLOCK], t[2 * BLOCK * i + BLOCK:2 * BLOCK * (i + 1)])
            for i in range(2)]


def _swa_sink_row(sink_ref, j):
    lane = lax.broadcasted_iota(jnp.int32, (1, SWA_ROWS), 1)
    row = jnp.full((1, SWA_ROWS), sink_ref[SWA_GROUP * j], F32)
    for r in range(1, SWA_GROUP):
        row = jnp.where(lane >= BLOCK * r, sink_ref[SWA_GROUP * j + r], row)
    return row


def _split3(t):
    return jnp.concatenate(_split3_parts(t), axis=1)


def _row_sums_as_row(t):
    ones = jnp.ones((8, 3 * t.shape[1]), MXU_DTYPE)
    return _mm_nt(ones, _split3(t))[0:1, :]


def _swa_probs_t(qs, kd, bias_t, sink):
    sc = _mm_nt(kd, qs) + bias_t
    m = jnp.maximum(jnp.max(sc, axis=0, keepdims=True), sink)
    p = jnp.exp(sc - m)
    ps = jnp.exp(sink - m)
    rinv = 1.0 / (jnp.sum(p, axis=0, keepdims=True) + ps)
    return p * rinv, ps * rinv


def _swa_fwd(sinks, qa, k_pad, v_pad, ga):
    s = qa.shape[0]
    sub = min(SWA_SUB, s // BLOCK)
    tq = sub * BLOCK

    def body(sink_ref, qa_ref, ga_ref, bias_ref, k_ref, v_ref, attn_ref, cat_ref):
        n = pl.program_id(0)
        for b in range(sub):
            rows = slice(BLOCK * b, BLOCK * (b + 1))
            start = pl.multiple_of((n * sub + b) * BLOCK, BLOCK)
            kw = k_ref[pl.ds(start, 2 * BLOCK), :]
            vw = v_ref[pl.ds(start, 2 * BLOCK), :]
            bias_t = _swa_bias_of(bias_ref, n, b)
            q = qa_ref[rows, :] * SWA_SCALE
            g = ga_ref[rows, :]
            silu = g * _sigmoid(g)
            for j in range(SWA_KV_HEADS):
                qs = _swa_stack(q, j).astype(MXU_DTYPE)
                probs, _ = _swa_probs_t(qs, _swa_dup(kw, j), bias_t, _swa_sink_row(sink_ref, j))
                pairs = _swa_unstack(_mm_tn(probs.astype(MXU_DTYPE), _swa_dup(vw, j)))
                for i in range(2):
                    lanes = slice(128 * (2 * j + i), 128 * (2 * j + i + 1))
                    attn_ref[rows, lanes] = pairs[i]
                    cat_ref[rows, lanes] = (pairs[i] * silu[:, lanes]).astype(cat_ref.dtype)

    return pl.pallas_call(
        body, name="swa_fwd", grid=(s // tq,),
        in_specs=[pl.BlockSpec(memory_space=pltpu.SMEM), _rows(tq, 512), _rows(tq, 512),
                  _full((2, 2 * BLOCK, SWA_ROWS)), _full((s + BLOCK, 128)), _full((s + BLOCK, 128))],
        out_specs=[_rows(tq, 512), _rows(tq, 512)],
        out_shape=[jax.ShapeDtypeStruct((s, 512), F32), jax.ShapeDtypeStruct((s, 512), MXU_DTYPE)],
        compiler_params=_cparams(dimension_semantics=("arbitrary",)),
    )(sinks, qa, ga, _swa_bias(), k_pad, v_pad)


GLA_KW = GLA_HEADS * GLA_DK
GLA_VW = GLA_HEADS * GLA_DV


def _idiv(t, d):
    return t >> (d.bit_length() - 1)


def _chunk_cumsum(t, lower):
    n, w = t.shape
    r = lax.broadcasted_iota(jnp.int32, (n, n), 0)
    c = lax.broadcasted_iota(jnp.int32, (n, n), 1)
    tri = ((_idiv(r, GLA_CHUNK) == _idiv(c, GLA_CHUNK)) & ((r >= c) if lower else (r <= c))).astype(MXU_DTYPE)
    parts = _mm(tri, _split3(t))
    return (parts[:, :w] + parts[:, w:2 * w]) + parts[:, 2 * w:]


def _chunk_last(t):
    n = t.shape[0]
    return jnp.concatenate(
        [jnp.broadcast_to(t[c + GLA_CHUNK - 1:c + GLA_CHUNK, :], (GLA_CHUNK, t.shape[1]))
         for c in range(0, n, GLA_CHUNK)], axis=0)


def _head_stack(t, width):
    head = _idiv(lax.broadcasted_iota(jnp.int32, t.shape, 1), width)
    zero = jnp.zeros_like(t)
    return jnp.concatenate([jnp.where(head == h, t, zero) for h in range(GLA_HEADS)], axis=0)


def _heads_to_rows(t):
    return jnp.concatenate([t[:, GLA_DV * h:GLA_DV * (h + 1)] for h in range(GLA_HEADS)], axis=0)


def _rows_to_heads(t):
    return jnp.concatenate([t[GLA_CHUNK * h:GLA_CHUNK * (h + 1)] for h in range(GLA_HEADS)], axis=1)


def _state_by_head(t):
    srow = _idiv(lax.broadcasted_iota(jnp.int32, (GLA_VW, GLA_KW), 0), GLA_DV)
    slane = _idiv(lax.broadcasted_iota(jnp.int32, (GLA_VW, GLA_KW), 1), GLA_DK)
    return jnp.where(srow == slane, jnp.concatenate([t] * GLA_HEADS, axis=0), jnp.zeros((GLA_VW, GLA_KW), t.dtype))


def _gla_masks():
    row = lax.broadcasted_iota(jnp.int32, (GLA_CHUNK, GLA_KW), 0)
    pos = lax.broadcasted_iota(jnp.int32, (GLA_CHUNK, GLA_KW), 1) & (GLA_CHUNK - 1)
    return pos <= row, pos >= row


def _gla_fwd(qb, kb, vb, la, gb, norm_w):
    s = qb.shape[0]
    tb = min(256, s)
    ch = tb // GLA_CHUNK

    def body(qb_ref, kb_ref, vb_ref, la_ref, gb_ref, nw_ref, o_ref, cat_ref, sp_ref, st_ref):
        @pl.when(pl.program_id(0) == 0)
        def _():
            st_ref[...] = jnp.zeros_like(st_ref)

        causal, _ = _gla_masks()
        nw = nw_ref[...]
        b = _chunk_cumsum(la_ref[...], True)
        bl = _chunk_last(b)
        k = kb_ref[...]
        qd = ((qb_ref[...] * GLA_SCALE) * jnp.exp(b)).astype(MXU_DTYPE)
        ki = (k * jnp.exp(-b)).astype(MXU_DTYPE)
        ke = (k * jnp.exp(bl - b)).astype(MXU_DTYPE)
        dec = jnp.exp(bl)
        v = vb_ref[...].astype(MXU_DTYPE)
        g = gb_ref[...]
        silu = g * _sigmoid(g)
        for ci in range(ch):
            rows = slice(GLA_CHUNK * ci, GLA_CHUNK * (ci + 1))
            qds, kis, kes = (_head_stack(t[rows], GLA_DK) for t in (qd, ki, ke))
            a = jnp.where(causal, _mm_nt(qd[rows], kis), 0.0).astype(MXU_DTYPE)
            st = st_ref[...]
            sp_ref[ci] = st
            o = _mm(a, _head_stack(v[rows], GLA_DV)) + _rows_to_heads(_mm_nt(qds, st.astype(MXU_DTYPE)))
            st_ref[...] = st * dec[rows][0:1] + _mm_tn(_heads_to_rows(v[rows]), kes)
            o_ref[rows, :] = o
            for h in range(GLA_HEADS):
                lv = slice(GLA_DV * h, GLA_DV * (h + 1))
                oh = o[:, lv]
                r = lax.rsqrt(jnp.mean(oh * oh, axis=1, keepdims=True) + EPS)
                cat_ref[rows, lv] = (oh * r * nw * silu[rows, lv]).astype(cat_ref.dtype)

    return pl.pallas_call(
        body, name="gla_fwd", grid=(s // tb,),
        in_specs=[_rows(tb, 256), _rows(tb, 256), _rows(tb, 512), _rows(tb, 256), _rows(tb, 512), _full((1, 128))],
        out_specs=[_rows(tb, 512), _rows(tb, 512), pl.BlockSpec((ch, GLA_DV, 256), lambda i: (i, 0, 0))],
        out_shape=[jax.ShapeDtypeStruct((s, 512), F32), jax.ShapeDtypeStruct((s, 512), MXU_DTYPE),
                   jax.ShapeDtypeStruct((s // GLA_CHUNK, GLA_DV, 256), F32)],
        scratch_shapes=[pltpu.VMEM((GLA_DV, GLA_KW), F32)],
        compiler_params=_cparams(dimension_semantics=("arbitrary",)),
    )(qb, kb, vb, la, gb, norm_w)


def _out_ln_loss(cat_a, cat_b, w_out, x, target, ln_g, ln_b):
    s = x.shape[0]
    ts = min(512, s)
    halves = 2 if ts % 32 == 0 else 1
    th = ts // halves

    def body(ca_ref, cb_ref, w_ref, x_ref, t_ref, g_ref, b_ref,
             loss_ref, gx_ref, da_ref, db_ref, gw_ref, gln_ref):
        @pl.when(pl.program_id(0) == 0)
        def _():
            loss_ref[...] = jnp.zeros_like(loss_ref)
            gw_ref[...] = jnp.zeros_like(gw_ref)
            gln_ref[...] = jnp.zeros_like(gln_ref)

        g = g_ref[...]
        dh16s = []
        for k in range(halves):
            rows = slice(th * k, th * (k + 1))
            mix = _mm(ca_ref[rows, :], w_ref[0:512, :]) + _mm(cb_ref[rows, :], w_ref[512:1024, :])
            h = ALPHA * x_ref[rows, :] + mix
            mu = jnp.mean(h, axis=1, keepdims=True)
            hc = h - mu
            rstd = lax.rsqrt(jnp.mean(hc * hc, axis=1, keepdims=True) + EPS)
            xhat = hc * rstd
            err = xhat * g + b_ref[...] - t_ref[rows, :]
            loss_ref[...] += 0.5 * jnp.sum(jnp.mean(err * err, axis=1, keepdims=True))
            dy = err * (1.0 / D_MODEL)
            gln_ref[0:1, :] += jnp.sum(dy * xhat, axis=0, keepdims=True)
            gln_ref[1:2, :] += jnp.sum(dy, axis=0, keepdims=True)
            dxh = dy * g
            dh = rstd * (dxh - jnp.mean(dxh, axis=1, keepdims=True)
                         - xhat * jnp.mean(dxh * xhat, axis=1, keepdims=True))
            gx_ref[rows, :] = ALPHA * dh
            dh16s.append(dh.astype(MXU_DTYPE))
        for k in range(halves):
            rows = slice(th * k, th * (k + 1))
            da_ref[rows, :] = _mm_nt(dh16s[k], w_ref[0:512, :])
            db_ref[rows, :] = _mm_nt(dh16s[k], w_ref[512:1024, :])
        dh16 = jnp.concatenate(dh16s, axis=0)
        gw_ref[0:512, :] += _mm_tn(ca_ref[...], dh16)
        gw_ref[512:1024, :] += _mm_tn(cb_ref[...], dh16)

    return pl.pallas_call(
        body, name="out_ln_loss", grid=(s // ts,),
        in_specs=[_rows(ts, 512), _rows(ts, 512), _full((D_MODEL, D_MODEL)), _rows(ts, D_MODEL), _rows(ts, D_MODEL),
                  _full((1, D_MODEL)), _full((1, D_MODEL))],
        out_specs=[_full((1, 128)), _rows(ts, D_MODEL), _rows(ts, 512), _rows(ts, 512),
                   _full((D_MODEL, D_MODEL)), _full((2, D_MODEL))],
        out_shape=[jax.ShapeDtypeStruct((1, 128), F32), jax.ShapeDtypeStruct((s, D_MODEL), F32),
                   jax.ShapeDtypeStruct((s, 512), F32), jax.ShapeDtypeStruct((s, 512), F32),
                   jax.ShapeDtypeStruct((D_MODEL, D_MODEL), F32), jax.ShapeDtypeStruct((2, D_MODEL), F32)],
        compiler_params=_cparams(dimension_semantics=("arbitrary",)),
    )(cat_a, cat_b, w_out, x, target, ln_g, ln_b)


def _swa_bwd(sinks, qa, k_pad, v_pad, attn, ga, d_cat_a, rope):
    s = qa.shape[0]
    sub = min(SWA_SUB, s // BLOCK)
    tq = sub * BLOCK
    nsteps = s // tq

    def body(sink_ref, qa_ref, ga_ref, at_ref, dc_ref, c_ref, s1_ref, s2_ref, bias_ref, k_ref, v_ref,
             dq_ref, dg_ref, dk_out, dv_out, ds_ref, dk_ref, dv_ref):
        n = pl.program_id(0)

        @pl.when(n == 0)
        def _():
            dk_ref[...] = jnp.zeros_like(dk_ref)
            dv_ref[...] = jnp.zeros_like(dv_ref)
            ds_ref[...] = jnp.zeros_like(ds_ref)

        low = lax.broadcasted_iota(jnp.int32, (2 * BLOCK, 128), 1) < SWA_HEAD_DIM
        for b in range(sub):
            rows = slice(BLOCK * b, BLOCK * (b + 1))
            start = pl.multiple_of((n * sub + b) * BLOCK, BLOCK)
            kw = k_ref[pl.ds(start, 2 * BLOCK), :]
            vw = v_ref[pl.ds(start, 2 * BLOCK), :]
            bias_t = _swa_bias_of(bias_ref, n, b)
            q = qa_ref[rows, :] * SWA_SCALE
            g = ga_ref[rows, :]
            sg = _sigmoid(g)
            o = at_ref[rows, :]
            dc = dc_ref[rows, :]
            do = dc * (g * sg)
            dg_ref[rows, :] = (dc * o * (sg * (1.0 + g * (1.0 - sg)))).astype(dg_ref.dtype)
            od = do * o
            c, s1, s2 = c_ref[rows, :], s1_ref[rows, :], s2_ref[rows, :]
            dk, dv = [], []
            for j in range(SWA_KV_HEADS):
                kd, vd = _swa_dup(kw, j), _swa_dup(vw, j)
                qs = _swa_stack(q, j).astype(MXU_DTYPE)
                dos = _swa_stack(do, j).astype(MXU_DTYPE)
                probs, psink = _swa_probs_t(qs, kd, bias_t, _swa_sink_row(sink_ref, j))
                delta = _row_sums_as_row(_swa_stack(od, j))
                dsc = (probs * (_mm_nt(vd, dos) - delta)).astype(MXU_DTYPE)
                dsink = psink * delta
                for r in range(SWA_GROUP):
                    h = SWA_GROUP * j + r
                    ds_ref[h:h + 1, :] += jnp.zeros((1, 128), F32) - jnp.sum(dsink[:, BLOCK * r:BLOCK * (r + 1)])
                dq = _swa_unstack(_mm_tn(dsc, kd))
                for i in range(2):
                    lanes = slice(128 * (2 * j + i), 128 * (2 * j + i + 1))
                    dq_ref[rows, lanes] = _rope_t(dq[i] * SWA_SCALE, c, s1, s2).astype(dq_ref.dtype)
                dkj = _mm(dsc, qs)
                dvj = _mm(probs.astype(MXU_DTYPE), dos)
                dk.append(dkj + pltpu.roll(dkj, SWA_HEAD_DIM, 1))
                dv.append(dvj + pltpu.roll(dvj, SWA_HEAD_DIM, 1))
            dk_ref[pl.ds(start, 2 * BLOCK), :] += jnp.where(low, dk[0], dk[1])
            dv_ref[pl.ds(start, 2 * BLOCK), :] += jnp.where(low, dv[0], dv[1])

        @pl.when(n == nsteps - 1)
        def _():
            dk_out[...] = dk_ref[BLOCK:, :]
            dv_out[...] = dv_ref[BLOCK:, :]

    return pl.pallas_call(
        body, name="swa_bwd", grid=(nsteps,),
        in_specs=[pl.BlockSpec(memory_space=pltpu.SMEM)] + [_rows(tq, 512)] * 4 + [_rows(tq, 128)] * 3
        + [_full((2, 2 * BLOCK, SWA_ROWS))] + [_full((s + BLOCK, 128))] * 2,
        out_specs=[_rows(tq, 512), _rows(tq, 512), _full((s, 128)), _full((s, 128)), _full((SWA_Q_HEADS, 128))],
        out_shape=[jax.ShapeDtypeStruct((s, 512), MXU_DTYPE), jax.ShapeDtypeStruct((s, 512), MXU_DTYPE),
                   jax.ShapeDtypeStruct((s, 128), F32), jax.ShapeDtypeStruct((s, 128), F32),
                   jax.ShapeDtypeStruct((SWA_Q_HEADS, 128), F32)],
        scratch_shapes=[pltpu.VMEM((s + BLOCK, 128), F32)] * 2,
        compiler_params=_cparams(dimension_semantics=("arbitrary",)),
    )(sinks, qa, ga, attn, d_cat_a, *rope, _swa_bias(), k_pad, v_pad)


def _gla_bwd(qb, kb, vb, la, oms, gb, o, sprev, d_cat_b, rb, wg, norm_w):
    s = qb.shape[0]
    tb = min(512, s)
    ch = tb // GLA_CHUNK
    nb = s // tb

    def body(qb_ref, kb_ref, vb_ref, la_ref, oms_ref, gb_ref, o_ref, sp_ref, dc_ref, rb_ref, wg_ref, nw_ref,
             dq_ref, dk_ref, dv_ref, dg_ref, dr_ref, gwg_ref, gbg_ref, gnw_ref, dst_ref):
        @pl.when(pl.program_id(0) == 0)
        def _():
            dst_ref[...] = jnp.zeros_like(dst_ref)
            gwg_ref[...] = jnp.zeros_like(gwg_ref)
            gbg_ref[...] = jnp.zeros_like(gbg_ref)
            gnw_ref[...] = jnp.zeros_like(gnw_ref)

        causal, causal_t = _gla_masks()
        nw = nw_ref[...]
        b = _chunk_cumsum(la_ref[...], True)
        bl = _chunk_last(b)
        eb, enb, ee, dec = jnp.exp(b), jnp.exp(-b), jnp.exp(bl - b), jnp.exp(bl)
        k = kb_ref[...]
        qd = (qb_ref[...] * GLA_SCALE) * eb
        ki = k * enb
        ke = k * ee
        qd16, ki16, ke16 = qd.astype(MXU_DTYPE), ki.astype(MXU_DTYPE), ke.astype(MXU_DTYPE)
        v16 = vb_ref[...].astype(MXU_DTYPE)

        g = gb_ref[...]
        sg = _sigmoid(g)
        silu = g * sg
        dsilu = sg * (1.0 + g * (1.0 - sg))
        gnw = jnp.zeros((1, GLA_DV), F32)
        do = []
        for h in range(GLA_HEADS):
            lv = slice(GLA_DV * h, GLA_DV * (h + 1))
            oh = o_ref[:, lv]
            dch = dc_ref[:, lv]
            r = lax.rsqrt(jnp.mean(oh * oh, axis=1, keepdims=True) + EPS)
            d_on = dch * silu[:, lv]
            dg_ref[:, lv] = (dch * (oh * r * nw) * dsilu[:, lv]).astype(dg_ref.dtype)
            gnw += jnp.sum(d_on * oh * r, axis=0, keepdims=True)
            u = d_on * nw
            do.append(r * u - oh * (r * r * r) * jnp.mean(u * oh, axis=1, keepdims=True))
        gnw_ref[...] += gnw
        do16 = jnp.concatenate(do, axis=1).astype(MXU_DTYPE)

        db, dbl = [None] * ch, [None] * ch
        for ci in reversed(range(ch)):
            rows = slice(GLA_CHUNK * ci, GLA_CHUNK * (ci + 1))
            qds, kis, kes = (_head_stack(t[rows], GLA_DK) for t in (qd16, ki16, ke16))
            vs, dos = _head_stack(v16[rows], GLA_DV), _head_stack(do16[rows], GLA_DV)
            a = jnp.where(causal, _mm_nt(qd16[rows], kis), 0.0).astype(MXU_DTYPE)
            at = jnp.where(causal_t, _mm_nt(ki16[rows], qds), 0.0).astype(MXU_DTYPE)
            da = jnp.where(causal, _mm_nt(do16[rows], vs), 0.0).astype(MXU_DTYPE)
            dat = jnp.where(causal_t, _mm_nt(v16[rows], dos), 0.0).astype(MXU_DTYPE)
            st = sp_ref[ci]
            dst = dst_ref[...]
            dst16 = dst.astype(MXU_DTYPE)
            dv = _mm(at, dos) + _rows_to_heads(_mm_nt(kes, dst16))
            dqd = _mm(da, kis) + _mm(do16[rows], _state_by_head(st.astype(MXU_DTYPE)))
            dki = _mm(dat, qds)
            dke = _mm(v16[rows], _state_by_head(dst16))
            ddec = jnp.sum(dst * st, axis=0, keepdims=True)
            decc = dec[rows][0:1]
            dst_ref[...] = _mm_tn(_heads_to_rows(do16[rows]), qds) + dst * decc
            dq_ref[rows, :] = (dqd * eb[rows] * GLA_SCALE).astype(dq_ref.dtype)
            dk_ref[rows, :] = (dki * enb[rows] + dke * ee[rows]).astype(dk_ref.dtype)
            dv_ref[rows, :] = dv.astype(dv_ref.dtype)
            dke_ke = dke * ke[rows]
            db[ci] = dqd * qd[rows] - dki * ki[rows] - dke_ke
            dbl[ci] = jnp.broadcast_to(jnp.sum(dke_ke, axis=0, keepdims=True) + ddec * decc, (GLA_CHUNK, GLA_KW))

        dla = _chunk_cumsum(jnp.concatenate(db, axis=0), False) + jnp.concatenate(dbl, axis=0)
        dlogit = dla * oms_ref[...] * (1.0 / GLA_TAU)
        dl16 = dlogit.astype(MXU_DTYPE)
        gbg_ref[...] += jnp.sum(dlogit, axis=0, keepdims=True)
        gwg_ref[...] += _mm_tn(rb_ref[...].astype(MXU_DTYPE), dl16)
        dr_ref[...] = _mm_nt(dl16, wg_ref[...]).astype(dr_ref.dtype)

    def rev(width):
        return pl.BlockSpec((tb, width), lambda i: (nb - 1 - i, 0))

    return pl.pallas_call(
        body, name="gla_bwd", grid=(nb,),
        in_specs=[rev(256), rev(256), rev(512), rev(256), rev(256), rev(512), rev(512),
                  pl.BlockSpec((ch, GLA_DV, 256), lambda i: (nb - 1 - i, 0, 0)), rev(512), rev(GLA_RANK),
                  _full((GLA_RANK, 256)), _full((1, 128))],
        out_specs=[rev(256), rev(256), rev(512), rev(512), rev(GLA_RANK),
                   _full((GLA_RANK, 256)), _full((1, 256)), _full((1, 128))],
        out_shape=[jax.ShapeDtypeStruct((s, 256), MXU_DTYPE), jax.ShapeDtypeStruct((s, 256), MXU_DTYPE),
                   jax.ShapeDtypeStruct((s, 512), MXU_DTYPE), jax.ShapeDtypeStruct((s, 512), MXU_DTYPE),
                   jax.ShapeDtypeStruct((s, GLA_RANK), MXU_DTYPE), jax.ShapeDtypeStruct((GLA_RANK, 256), F32),
                   jax.ShapeDtypeStruct((1, 256), F32), jax.ShapeDtypeStruct((1, 128), F32)],
        scratch_shapes=[pltpu.VMEM((GLA_DV, GLA_KW), F32)],
        compiler_params=_cparams(dimension_semantics=("arbitrary",)),
    )(qb, kb, vb, la, oms, gb, o, sprev, d_cat_b, rb, wg, norm_w)


def _dproj_tiles(piece_refs, rope_refs, members=(0, 1, 3, 4, 5, 6, 7, 8)):
    for i in members:
        if i == 1:
            dk = _rope_t(piece_refs[1][...], *(r[...] for r in rope_refs))
            yield OFF[1], OFF[3], jnp.concatenate([dk, piece_refs[2][...]], axis=1).astype(MXU_DTYPE)
        else:
            yield OFF[i], OFF[i + 1], piece_refs[i][...].astype(MXU_DTYPE)


def _in_proj_bwd_x(gx0, pieces, w_in, rope):
    s = gx0.shape[0]
    ts = min(512, s)
    widths = [OFF[i + 1] - OFF[i] for i in range(9)]

    def body(gx0_ref, *refs):
        w_ref, gx_ref = refs[12:]
        acc = gx0_ref[...]
        for lo, hi, t16 in _dproj_tiles(refs[:9], refs[9:12]):
            acc += _mm(t16, w_ref[lo:hi, :])
        gx_ref[...] = acc

    return pl.pallas_call(
        body, name="in_proj_bwd_x", grid=(s // ts,),
        in_specs=[_rows(ts, D_MODEL)] + [_rows(ts, w) for w in widths] + [_rows(ts, 128)] * 3
        + [_full((D_IN_PROJ, D_MODEL))],
        out_specs=_rows(ts, D_MODEL),
        out_shape=jax.ShapeDtypeStruct((s, D_MODEL), F32),
        compiler_params=_cparams(dimension_semantics=("arbitrary",)),
    )(gx0, *pieces, *rope, w_in)


GW_PASSES = 2
GW_EVENT_STEPS = (0, 2, 4, 6)


def _in_proj_bwd_w(x, pieces, rope, parts_wg, g_ln, g_bg, g_nw, g_sinks, loss, parts_w_out):
    s = x.shape[0]
    ts = min(1024, s)
    nt = s // ts
    n_pass = GW_PASSES
    n_steps = n_pass * nt
    cw = D_MODEL // n_pass
    n_chips = N_DEV // 2
    blk = (D_IN_SHARD, cw)
    out_blk = parts_w_out.shape[1:]
    w_out_forward_step = min(1, n_steps - 1)
    w_out_finish_step = max(n_steps - 2, w_out_forward_step)

    def body(x_hbm, *refs):
        piece_refs, rope_refs = refs[:9], refs[9:12]
        (pwg_ref, gln_ref, gbg_ref, gnw_ref, gsk_ref, loss_ref, pout_ref,
         gin_ref, rwg_ref, rsm_ref, gout_ref) = refs[12:23]
        (acc_ref, stage_ref, sib_ref, snd_ref, rcv_ref, sm_ref, xs_ref,
         d2d_send, d2d_recv, ici_send, ici_recv, out_sems, sm_send, sm_recv, sm_loc, x_sems) = refs[23:39]
        w_out_sum = _OwnerSum(pout_ref, *refs[39:])
        p, t = pl.program_id(0), pl.program_id(1)
        step = p * nt + t
        tile = pl.ds(pl.multiple_of(t * ts, ts), ts)

        def x_load(k, q):
            rows = pl.ds(pl.multiple_of(k * ts, ts), ts)
            return pltpu.make_async_copy(x_hbm.at[rows, pl.ds(q * cw, cw)], xs_ref.at[q, rows, :],
                                         x_sems.at[q * nt + k])
        x_, y_, c = _mesh_pos()
        me, mychip, sibling = 4 * x_ + 2 * y_ + c, 2 * x_ + y_, (x_, y_, 1 - c)
        small_dsts = (rwg_ref, rsm_ref)

        def small_src(a, block):
            return pwg_ref.at[block] if a == 0 else sm_ref

        def small_copy(k, a, src_block, dst_block, peer):
            i = 2 * (k - 1) + a
            return pltpu.make_async_remote_copy(
                src_ref=small_src(a, src_block), dst_ref=small_dsts[a].at[dst_block], send_sem=sm_send.at[i],
                recv_sem=sm_recv.at[i], device_id=peer, device_id_type=pl.DeviceIdType.MESH)

        def small_local(a):
            return pltpu.make_async_copy(small_src(a, me), small_dsts[a].at[me], sm_loc.at[a])

        @pl.when(step == 0)
        def _():
            for q in range(n_pass):
                for k in range(nt):
                    x_load(k, q).start()
            w_out_sum.start()
            acc_ref[...] = jnp.zeros_like(acc_ref)
            sm_ref[...] = jnp.zeros_like(sm_ref)
            for r in range(D_MODEL // 128):
                sm_ref[r:r + 1, :] = gln_ref[0:1, 128 * r:128 * (r + 1)]
                sm_ref[8 + r:9 + r, :] = gln_ref[1:2, 128 * r:128 * (r + 1)]
            for r in range(2):
                sm_ref[16 + r:17 + r, :] = gbg_ref[0:1, 128 * r:128 * (r + 1)]
            sm_ref[24:25, :] = gnw_ref[...]
            diag = (lax.broadcasted_iota(jnp.int32, gsk_ref.shape, 0)
                    == lax.broadcasted_iota(jnp.int32, gsk_ref.shape, 1))
            sm_ref[32:33, :] = jnp.sum(jnp.where(diag, gsk_ref[...], 0.0), axis=0, keepdims=True)
            sm_ref[40:41, :] = loss_ref[...]
            for a in range(2):
                small_local(a).start()
            for k in range(1, N_DEV):
                peer, pidx = _peer(k, x_, y_, c)
                for a in range(2):
                    small_copy(k, a, pidx, me, peer).start()

        for q in range(n_pass):
            @pl.when(p == q)
            def _(q=q):
                x_load(t, q).wait()

        xb = xs_ref[p, tile, :]
        for lo, hi, t16 in _dproj_tiles(piece_refs, rope_refs):
            acc_ref[p, lo:hi, :] += _mm_tn(t16, xb)

        @pl.when(step == w_out_forward_step)
        def _():
            w_out_sum.forward()

        @pl.when(step == w_out_finish_step)
        def _():
            gout_ref[...] = w_out_sum.finish()

        def block_rows(q, j):
            return acc_ref[q, D_IN_SHARD * j:D_IN_SHARD * (j + 1), :]

        def d2d(q):
            return pltpu.make_async_remote_copy(
                src_ref=stage_ref, dst_ref=sib_ref.at[q], send_sem=d2d_send.at[q],
                recv_sem=d2d_recv.at[q], device_id=sibling, device_id_type=pl.DeviceIdType.MESH)

        def ici(q, slot, owner):
            i = 3 * q + slot
            return pltpu.make_async_remote_copy(
                src_ref=snd_ref.at[q, slot], dst_ref=rcv_ref.at[q, slot], send_sem=ici_send.at[i],
                recv_sem=ici_recv.at[i], device_id=owner, device_id_type=pl.DeviceIdType.MESH)

        def out_copy(q):
            return pltpu.make_async_copy(sib_ref.at[q, mychip], gin_ref.at[:, pl.ds(q * cw, cw)], out_sems.at[q])

        first = (jnp.where(c == 0, 1 - x_, x_), jnp.where(c == 0, y_, 1 - y_))
        second = (jnp.where(c == 0, x_, 1 - x_), jnp.where(c == 0, 1 - y_, y_))

        def chip_of(pos):
            return 2 * pos[0] + pos[1]

        def to_sibling(q):
            if q >= 1:
                d2d(q - 1).wait_send()
            for cc in range(2):
                @pl.when(c == cc)
                def _(cc=cc):
                    for k in range(n_chips):
                        stage_ref[k] = block_rows(q, 2 * k + 1 - cc)
            d2d(q).start()

        def chip_sums_leave(q):
            d2d(q).wait_recv()
            for cc in range(2):
                @pl.when(c == cc)
                def _(cc=cc):
                    for k in range(n_chips):
                        sib_ref[q, k] = block_rows(q, 2 * k + cc) + sib_ref[q, k]
            snd_ref[q, 2] = sib_ref[q, chip_of((1 - x_, 1 - y_))].astype(snd_ref.dtype)
            ici(q, 2, (*first, c)).start()
            snd_ref[q, 0] = sib_ref[q, chip_of(first)].astype(snd_ref.dtype)
            ici(q, 0, (*first, c)).start()

        def combined_sum_leaves(q):
            ici(q, 2, sibling).wait_recv()
            snd_ref[q, 1] = (sib_ref[q, chip_of(second)] + rcv_ref[q, 2].astype(F32)).astype(snd_ref.dtype)
            ici(q, 1, (*second, c)).start()

        def owner_total(q):
            total = sib_ref[q, mychip]
            for slot in range(2):
                ici(q, slot, sibling).wait_recv()
                total = total + rcv_ref[q, slot].astype(F32)
            sib_ref[q, mychip] = total
            out_copy(q).start()

        stages = (to_sibling, chip_sums_leave, combined_sum_leaves, owner_total)
        events = sorted((min((q + 1) * nt - 1 + GW_EVENT_STEPS[i], n_steps - 1), i > 0, 2 * q + i, -q, q, i)
                        for q in range(n_pass) for i in range(len(stages)))
        for at_step, _, _, _, q, i in events:
            @pl.when(step == at_step)
            def _(q=q, i=i):
                stages[i](q)

        @pl.when(step == n_steps - 1)
        def _():
            for k in range(1, N_DEV):
                peer, pidx = _peer(k, x_, y_, c)
                for a in range(2):
                    small_copy(k, a, me, pidx, peer).wait_recv()
            for k in range(1, N_DEV):
                peer, pidx = _peer(k, x_, y_, c)
                for a in range(2):
                    small_copy(k, a, pidx, me, peer).wait_send()
            for a in range(2):
                small_local(a).wait()
            d2d(n_pass - 1).wait_send()
            for q in range(n_pass):
                for slot in range(3):
                    ici(q, slot, sibling).wait_send()
            for q in range(n_pass):
                out_copy(q).wait()

    widths = [OFF[i + 1] - OFF[i] for i in range(9)]
    hbm = pl.BlockSpec(memory_space=pl.ANY)
    vmem = pl.BlockSpec(memory_space=pltpu.VMEM)

    def token_tile(width):
        return pl.BlockSpec((ts, width), lambda p, t: (t, 0))

    return pl.pallas_call(
        body, name="in_proj_bwd_w", grid=(n_pass, nt),
        in_specs=[hbm] + [token_tile(w) for w in widths] + [token_tile(128)] * 3 + [hbm] + [vmem] * 5 + [hbm],
        out_specs=[hbm, hbm, hbm, vmem],
        out_shape=[jax.ShapeDtypeStruct((D_IN_SHARD, D_MODEL), F32),
                   jax.ShapeDtypeStruct((N_DEV,) + parts_wg.shape[1:], F32),
                   jax.ShapeDtypeStruct((N_DEV, SMALL_ROWS, 128), F32), jax.ShapeDtypeStruct(out_blk, F32)],
        scratch_shapes=[pltpu.VMEM((n_pass, D_IN_PROJ, cw), F32), pltpu.VMEM((n_chips,) + blk, F32),
                        pltpu.VMEM((n_pass, n_chips) + blk, F32), pltpu.VMEM((n_pass, 3) + blk, MXU_DTYPE),
                        pltpu.VMEM((n_pass, 3) + blk, MXU_DTYPE),
                        pltpu.VMEM((SMALL_ROWS, 128), F32), pltpu.VMEM((n_pass, s, cw), MXU_DTYPE),
                        pltpu.SemaphoreType.DMA((n_pass,)), pltpu.SemaphoreType.DMA((n_pass,)),
                        pltpu.SemaphoreType.DMA((3 * n_pass,)), pltpu.SemaphoreType.DMA((3 * n_pass,)),
                        pltpu.SemaphoreType.DMA((n_pass,)),
                        pltpu.SemaphoreType.DMA((2 * (N_DEV - 1),)), pltpu.SemaphoreType.DMA((2 * (N_DEV - 1),)),
                        pltpu.SemaphoreType.DMA((2,)), pltpu.SemaphoreType.DMA((n_pass * nt,))]
        + _OwnerSum.scratch(out_blk),
        compiler_params=_cparams(dimension_semantics=("arbitrary", "arbitrary")),
    )(x, *pieces, *rope, parts_wg, g_ln, g_bg, g_nw, g_sinks, loss, parts_w_out)


def _local_step(x, positions, w_in_t, wg_s, b_gate, sinks, norm_w, w_out_s, ln_g, ln_b, target):
    qa, k_pad, v_pad, ga, qb, kb, vb, gb, rb, la, oms, *rope, x16, w_in, wg, w_out = _in_proj(
        x, w_in_t, wg_s, b_gate, _rope_angles(positions), w_out_s)
    attn, cat_a = _swa_fwd(sinks, qa, k_pad, v_pad, ga)
    o, cat_b, sprev = _gla_fwd(qb, kb, vb, la, gb, norm_w)
    loss, gx0, d_cat_a, d_cat_b, g_w_out, g_ln = _out_ln_loss(cat_a, cat_b, w_out, x, target, ln_g, ln_b)
    parts_w_out = g_w_out.reshape(N_DEV, D_OUT_SHARD, D_MODEL)
    dqa, dga, dka, dva, g_sinks = _swa_bwd(sinks, qa, k_pad, v_pad, attn, ga, d_cat_a, rope)
    dqb, dkb, dvb, dgb, drb, g_wg, g_bg, g_nw = _gla_bwd(qb, kb, vb, la, oms, gb, o, sprev, d_cat_b, rb, wg, norm_w)
    pieces = (dqa, dka, dva, dga, dqb, dkb, dvb, dgb, drb)
    grad_x = _in_proj_bwd_x(gx0, pieces, w_in, rope)
    parts_wg = jnp.transpose(g_wg.reshape(GLA_RANK, N_DEV, 32), (1, 0, 2))
    g_in, r_wg, r_small, g_out = _in_proj_bwd_w(
        x16, pieces, rope, parts_wg, g_ln, g_bg, g_nw, g_sinks, loss, parts_w_out)
    return grad_x, g_in, g_out, r_wg, r_small


def _mesh_pos():
    return lax.axis_index("x"), lax.axis_index("y"), lax.axis_index("c")


def _peer(k, x, y, c):
    px = (1 - x) if k & 4 else x
    py = (1 - y) if k & 2 else y
    pc = (1 - c) if k & 1 else c
    return (px, py, pc), 4 * px + 2 * py + pc


def _other_chips(x, y):
    return [(1 - x, y), (x, 1 - y), (1 - x, 1 - y)]


def _shard_view(t):
    return jnp.transpose(t, (2, 0, 1))


class _BlockGather:
    def __init__(self, slots, send_sems, recv_sems):
        self.slots, self.send_sems, self.recv_sems = slots, send_sems, recv_sems
        x, y, c = _mesh_pos()
        self.c, self.me, self.sibling = c, 4 * x + 2 * y + c, (x, y, 1 - c)
        first = (jnp.where(c == 0, 1 - x, x), jnp.where(c == 0, y, 1 - y))
        second = (jnp.where(c == 0, x, 1 - x), jnp.where(c == 0, 1 - y, y))
        self.chips = [first, second, (1 - x, 1 - y)]

    @staticmethod
    def scratch():
        return [pltpu.SemaphoreType.DMA((N_DEV - 1,)), pltpu.SemaphoreType.DMA((N_DEV - 1,))]

    def _copy(self, k, block, to):
        return pltpu.make_async_remote_copy(
            src_ref=self.slots.at[block], dst_ref=self.slots.at[block], send_sem=self.send_sems.at[k],
            recv_sem=self.recv_sems.at[k], device_id=to, device_id_type=pl.DeviceIdType.MESH)

    def _block(self, j, c):
        cx, cy = self.chips[j]
        return 4 * cx + 2 * cy + c

    def _dev(self, j):
        return (*self.chips[j], self.c)

    def start(self):
        self._copy(1, self.me, self._dev(0)).start()
        self._copy(2, self.me, self._dev(1)).start()
        self._copy(0, self.me, self.sibling).start()

    def forward(self):
        self._copy(1, self._block(0, self.c), self.sibling).wait_recv()
        self._copy(3, self._block(0, self.c), self._dev(1)).start()
        self._copy(4, self._block(0, self.c), self.sibling).start()
        self._copy(2, self._block(1, self.c), self.sibling).wait_recv()
        self._copy(5, self._block(1, self.c), self.sibling).start()

    def forward_far(self):
        self._copy(3, self._block(2, self.c), self.sibling).wait_recv()
        self._copy(6, self._block(2, self.c), self.sibling).start()

    def finish(self):
        for k in (0, 4, 5, 6):
            self._copy(k, self.me, self.sibling).wait_recv()
        for k in range(N_DEV - 1):
            self._copy(k, self.me, self.sibling).wait_send()


class _OwnerSum:
    def __init__(self, parts, own, sib, snd, rcv, loc_sems, d2d_send, d2d_recv, ici_send, ici_recv):
        self.parts, self.own, self.sib, self.snd, self.rcv = parts, own, sib, snd, rcv
        self.sems = (loc_sems, d2d_send, d2d_recv, ici_send, ici_recv)
        x, y, c = _mesh_pos()
        self.c, self.sibling = c, (x, y, 1 - c)
        self.chips = [(x, y)] + _other_chips(x, y)

    @staticmethod
    def scratch(block):
        return [pltpu.VMEM((4,) + block, F32), pltpu.VMEM((4,) + block, F32),
                pltpu.VMEM((3,) + block, MXU_DTYPE), pltpu.VMEM((3,) + block, MXU_DTYPE),
                pltpu.SemaphoreType.DMA((4,)), pltpu.SemaphoreType.DMA((4,)), pltpu.SemaphoreType.DMA((4,)),
                pltpu.SemaphoreType.DMA((3,)), pltpu.SemaphoreType.DMA((3,))]

    def _local(self, r):
        cx, cy = self.chips[r]
        return pltpu.make_async_copy(self.parts.at[4 * cx + 2 * cy + self.c], self.own.at[r], self.sems[0].at[r])

    def _d2d(self, r):
        cx, cy = self.chips[r]
        return pltpu.make_async_remote_copy(
            src_ref=self.parts.at[4 * cx + 2 * cy + (1 - self.c)], dst_ref=self.sib.at[r], send_sem=self.sems[1].at[r],
            recv_sem=self.sems[2].at[r], device_id=self.sibling, device_id_type=pl.DeviceIdType.MESH)

    def _ici(self, r):
        cx, cy = self.chips[r]
        return pltpu.make_async_remote_copy(
            src_ref=self.snd.at[r - 1], dst_ref=self.rcv.at[r - 1], send_sem=self.sems[3].at[r - 1],
            recv_sem=self.sems[4].at[r - 1], device_id=(cx, cy, self.c), device_id_type=pl.DeviceIdType.MESH)

    def start(self):
        for r in (1, 2, 3, 0):
            self._local(r).start()
            self._d2d(r).start()

    def forward(self):
        for r in (1, 2, 3):
            self._local(r).wait()
            self._d2d(r).wait_recv()
            self.snd[r - 1] = (self.own[r] + self.sib[r]).astype(self.snd.dtype)
            self._ici(r).start()

    def finish(self):
        self._local(0).wait()
        self._d2d(0).wait_recv()
        acc = self.own[0] + self.sib[0]
        for r in (1, 2, 3):
            self._ici(r).wait_recv()
            acc = acc + self.rcv[r - 1].astype(F32)
        for r in range(4):
            self._d2d(r).wait_send()
        for r in (1, 2, 3):
            self._ici(r).wait_send()
        return acc


SMALL_ROWS = 48


def _adamw_math(g, w, m, v):
    nm = ADAM_B1 * m + (1.0 - ADAM_B1) * g
    nv = ADAM_B2 * v + (1.0 - ADAM_B2) * (g * g)
    m_hat = nm / (1.0 - ADAM_B1 ** ADAM_STEP)
    v_hat = nv / (1.0 - ADAM_B2 ** ADAM_STEP)
    return -ADAM_LR * (m_hat / (jnp.sqrt(v_hat) + ADAM_EPS) + ADAM_WD * w), nm, nv


def _adamw(g_in, w_in_wmv, r_small, r_wg, g_out, params):
    n_par = len(params)
    rows, width = g_in.shape
    parts = 3
    pr = rows // parts
    assert pr * parts == rows

    def body(gin_ref, w_hbm, m_hbm, v_hbm, rsm_ref, rwg_ref, gout_ref, *refs):
        ins, outs = refs[:3 * n_par], refs[3 * n_par:7 * n_par + 1]
        big_outs = refs[7 * n_par + 1:7 * n_par + 5]
        bufs, stage, sems = refs[7 * n_par + 5:]

        def load(k, i, src):
            return pltpu.make_async_copy(src.at[pl.ds(pr * k, pr), 0, :], bufs.at[k, i], sems.at[7 * k + i])

        def store(k, i, dst):
            return pltpu.make_async_copy(stage.at[k, i], dst.at[pl.ds(pr * k, pr), 0, :], sems.at[7 * k + 3 + i])

        for k in range(parts):
            for i, src in enumerate((w_hbm, m_hbm, v_hbm)):
                load(k, i, src).start()

        g = rsm_ref[0]
        gwg = rwg_ref[0]
        for j in range(1, N_DEV):
            g = g + rsm_ref[j]
            gwg = gwg + rwg_ref[j]
        outs[4 * n_par][...] = g[40:41]
        grads = [gwg,
                 jnp.concatenate([g[r:r + 1] for r in range(0, 8)], axis=1),
                 jnp.concatenate([g[r:r + 1] for r in range(8, 16)], axis=1),
                 jnp.concatenate([g[16:17], g[17:18]], axis=1),
                 g[24:25],
                 g[32:33, 0:SWA_Q_HEADS],
                 gout_ref[...]]
        for p, gp in enumerate(grads):
            w_ref, m_ref, v_ref = ins[3 * p:3 * p + 3]
            outs[4 * p][...] = gp
            outs[4 * p + 1][...], outs[4 * p + 2][...], outs[4 * p + 3][...] = _adamw_math(
                gp, w_ref[...], m_ref[...], v_ref[...])

        for k in range(parts):
            for i, src in enumerate((w_hbm, m_hbm, v_hbm)):
                load(k, i, src).wait()
            gk = gin_ref[pr * k:pr * (k + 1), :]
            stage[k, 0] = gk
            stage[k, 1], stage[k, 2], stage[k, 3] = _adamw_math(gk, bufs[k, 0], bufs[k, 1], bufs[k, 2])
            for i, dst in enumerate(big_outs):
                store(k, i, dst).start()
        for k in range(parts):
            for i, dst in enumerate(big_outs):
                store(k, i, dst).wait()

    vmem = pl.BlockSpec(memory_space=pltpu.VMEM)
    hbm = pl.BlockSpec(memory_space=pl.ANY)
    flat = [t for wmv in params for t in wmv]
    res = pl.pallas_call(
        body, name="adamw",
        in_specs=[vmem, hbm, hbm, hbm] + [vmem] * (3 + len(flat)),
        out_specs=[vmem] * (4 * n_par + 1) + [hbm] * 4,
        out_shape=[jax.ShapeDtypeStruct(wmv[0].shape, F32) for wmv in params for _ in range(4)]
        + [jax.ShapeDtypeStruct((1, 128), F32)] + [jax.ShapeDtypeStruct((rows, 1, width), F32)] * 4,
        scratch_shapes=[pltpu.VMEM((parts, 3, pr, width), F32), pltpu.VMEM((parts, 4, pr, width), F32),
                        pltpu.SemaphoreType.DMA((7 * parts,))],
        compiler_params=_cparams(),
    )(g_in, *w_in_wmv, r_small, r_wg, g_out, *flat)
    return res[:4 * n_par + 1], res[4 * n_par + 1:]


def kernel(x, positions, w_in, gla_w_gate_up, gla_b_gate, attn_sinks, gla_norm_w, w_out, ln_g, ln_b, loss_target, m_w_in, m_gla_w_gate_up, m_gla_b_gate, m_attn_sinks, m_gla_norm_w, m_w_out, m_ln_g, m_ln_b, v_w_in, v_gla_w_gate_up, v_gla_b_gate, v_attn_sinks, v_gla_norm_w, v_w_out, v_ln_g, v_ln_b):
    grad_x, g_in, g_out, r_wg, r_small = _local_step(
        x[0], positions[0], _shard_view(w_in), gla_w_gate_up[0], gla_b_gate, attn_sinks[0], gla_norm_w, w_out[0],
        ln_g, ln_b, loss_target[0])

    vec, upd_in = _adamw(g_in, (_shard_view(w_in), _shard_view(m_w_in), _shard_view(v_w_in)), r_small, r_wg, g_out, [
        (gla_w_gate_up[0], m_gla_w_gate_up[0], v_gla_w_gate_up[0]), (ln_g, m_ln_g, v_ln_g), (ln_b, m_ln_b, v_ln_b),
        (gla_b_gate, m_gla_b_gate, v_gla_b_gate), (gla_norm_w, m_gla_norm_w, v_gla_norm_w),
        (attn_sinks, m_attn_sinks, v_attn_sinks), (w_out[0], m_w_out[0], v_w_out[0])])
    upd_in = [jnp.transpose(t, (1, 2, 0)) for t in upd_in]

    outs = [vec[28][0, 0], grad_x[None]]
    for kind in range(4):
        u_wg, u_ln_g, u_ln_b, u_bg, u_nw, u_sinks, u_out = (vec[4 * p + kind] for p in range(7))
        outs += [upd_in[kind], u_wg[None], u_bg, u_sinks, u_nw, u_out[None], u_ln_g, u_ln_b]
    return tuple(outs)
```

```python
import jax
import jax.numpy as jnp
from jax import lax
from jax.experimental import pallas as pl
from jax.experimental.pallas import tpu as pltpu

F32 = jnp.float32
MXU_DTYPE = jnp.bfloat16

N_DEV = 8
D_MODEL = 1024
SWA_Q_HEADS = 8
SWA_KV_HEADS = 2
SWA_GROUP = 4
SWA_HEAD_DIM = 64
BLOCK = 128
ROPE_THETA = 500000.0
ROT_DIM = 16
GLA_HEADS = 4
GLA_DK = 64
GLA_DV = 128
GLA_RANK = 16
GLA_TAU = 16.0
GLA_CHUNK = 64
D_IN_PROJ = 2832
D_IN_SHARD = D_IN_PROJ // N_DEV
D_OUT_SHARD = D_MODEL // N_DEV
OFF = (0, 512, 640, 768, 1280, 1536, 1792, 2304, 2816, 2832)
EPS = 1e-5
ALPHA = 2.0 ** 0.25
SWA_SCALE = SWA_HEAD_DIM ** -0.5
GLA_SCALE = GLA_DK ** -0.5
ADAM_LR = 0.001
ADAM_B1 = 0.9
ADAM_B2 = 0.999
ADAM_EPS = 1e-08
ADAM_WD = 0.01
ADAM_STEP = 10
VMEM_LIMIT = 56 * 1024 * 1024

_NT = (((1,), (1,)), ((), ()))
_TN = (((0,), (0,)), ((), ()))


def _mm(a, b):
    return jnp.dot(a, b, preferred_element_type=F32)


def _mm_nt(a, b):
    return lax.dot_general(a, b, _NT, preferred_element_type=F32)


def _mm_tn(a, b):
    return lax.dot_general(a, b, _TN, preferred_element_type=F32)


def _sigmoid(t):
    return 1.0 / (1.0 + jnp.exp(-t))


def _cparams(**kw):
    return pltpu.CompilerParams(vmem_limit_bytes=VMEM_LIMIT, **kw)


def _full(shape):
    return pl.BlockSpec(shape, lambda *_: (0,) * len(shape))


def _rows(tile, width):
    return pl.BlockSpec((tile, width), lambda i: (i, 0))


def _rope_angles(positions):
    half = ROT_DIM // 2
    inv_freq = ROPE_THETA ** (-jnp.arange(half, dtype=F32) / half)
    ang = positions.astype(F32)[:, None] * inv_freq[None, :]
    return jnp.concatenate([jnp.cos(ang), jnp.sin(ang)], axis=1)


def _split3_parts(t):
    hi = t.astype(MXU_DTYPE)
    r1 = t - hi.astype(F32)
    mid = r1.astype(MXU_DTYPE)
    return hi, mid, (r1 - mid.astype(F32)).astype(MXU_DTYPE)


def _rope_tables(cs):
    half = ROT_DIM // 2
    i = lax.broadcasted_iota(jnp.int32, (2 * half, 3 * 128), 0)
    lane = lax.broadcasted_iota(jnp.int32, (2 * half, 3 * 128), 1)
    table, pos = _idiv(lane, 128), lane & (SWA_HEAD_DIM - 1)
    is_c = (table == 0) & (pos < ROT_DIM) & ((pos & (half - 1)) == i)
    is_s1 = (table == 1) & (pos < half) & (pos + half == i)
    is_s2 = (table == 2) & (pos >= half) & (pos < ROT_DIM) & (pos == i)
    sel = jnp.where(is_c | is_s2, 1.0, jnp.where(is_s1, -1.0, 0.0)).astype(MXU_DTYPE)
    hi, mid, lo = _split3_parts(cs)
    t = (_mm(hi, sel) + _mm(mid, sel)) + _mm(lo, sel)
    pos1 = lax.broadcasted_iota(jnp.int32, (1, 128), 1) & (SWA_HEAD_DIM - 1)
    return t[:, 0:128] + jnp.where(pos1 >= ROT_DIM, 1.0, 0.0), t[:, 128:256], t[:, 256:384]


def _rope(t, c, s1, s2):
    return t * c + pltpu.roll(t, 120, 1) * s1 + pltpu.roll(t, 8, 1) * s2


def _rope_t(g, c, s1, s2):
    return g * c + pltpu.roll(g * s1, 8, 1) + pltpu.roll(g * s2, 120, 1)


def _in_proj(x, w_in_t, wg_s, b_gate, cos_sin, w_out_s):
    s = x.shape[0]
    ts = min(512, s)
    nsteps = s // ts
    forward_step, far_step = min(3, nsteps - 1), min(5, nsteps - 1)
    widths = [OFF[i + 1] - OFF[i] for i in range(9)]

    def body(x_ref, win_hbm, wgs_ref, bg_ref, cs_ref, wos_ref,
             qa_ref, ka_ref, va_ref, ga_ref, qb_ref, kb_ref, vb_ref, gb_ref, rb_ref, la_ref, oms_ref,
             c_ref, s1_ref, s2_ref, x16_ref, w_ref, wg_ref, wout_ref,
             win_all, wg_all, wout_all, stage, stage_sem, *sems):
        xb = x_ref[...].astype(MXU_DTYPE)
        x16_ref[...] = xb
        c, s1, s2 = _rope_tables(cs_ref[...])
        c_ref[...], s1_ref[...], s2_ref[...] = c, s1, s2
        i0 = pl.program_id(0)
        gather = _BlockGather(wout_all, *sems[0:2])

        @pl.when(i0 == 0)
        def _():
            ka_ref[0:BLOCK, :] = jnp.zeros((BLOCK, 128), ka_ref.dtype)
            va_ref[0:BLOCK, :] = jnp.zeros((BLOCK, 128), va_ref.dtype)
            first = (_BlockGather(win_all, *sems[2:4]), _BlockGather(wg_all, *sems[4:6]))
            load = pltpu.make_async_copy(win_hbm.at[:, 0, :], stage, stage_sem)
            load.start()
            wout_all[gather.me] = wos_ref[...].astype(wout_all.dtype)
            wg_all[gather.me] = wgs_ref[...].astype(wg_all.dtype)
            load.wait()
            win_all[gather.me] = stage[...].astype(win_all.dtype)
            for stage_of in ("start", "forward", "forward_far", "finish"):
                for g in first:
                    getattr(g, stage_of)()
            gather.start()
            for j in range(N_DEV):
                w_ref[D_IN_SHARD * j:D_IN_SHARD * (j + 1), :] = win_all[j]
                wg_ref[:, 32 * j:32 * (j + 1)] = wg_all[j]

        @pl.when(i0 == forward_step)
        def _():
            gather.forward()

        @pl.when(i0 == far_step)
        def _():
            gather.forward_far()

        @pl.when(i0 == nsteps - 1)
        def _():
            gather.finish()
            for j in range(N_DEV):
                wout_ref[D_OUT_SHARD * j:D_OUT_SHARD * (j + 1), :] = wout_all[j]

        kv_rows = pl.ds(pl.multiple_of(BLOCK + i0 * ts, BLOCK), ts)

        def cols(i):
            return _mm_nt(xb, w_ref[OFF[i]:OFF[i + 1], :])

        qa = cols(0)
        for i in range(4):
            qa_ref[:, 128 * i:128 * (i + 1)] = _rope(qa[:, 128 * i:128 * (i + 1)], c, s1, s2).astype(qa_ref.dtype)
        kv = _mm_nt(xb, w_ref[OFF[1]:OFF[3], :])
        ka_ref[kv_rows, :] = _rope(kv[:, 0:128], c, s1, s2).astype(ka_ref.dtype)
        va_ref[kv_rows, :] = kv[:, 128:256].astype(va_ref.dtype)
        ga_ref[...] = cols(3)
        qb_ref[...] = cols(4)
        kb_ref[...] = cols(5)
        vb_ref[...] = cols(6).astype(vb_ref.dtype)
        gb_ref[...] = cols(7)
        rb = cols(8)
        rb_ref[...] = rb
        logit = _mm(rb.astype(MXU_DTYPE), wg_ref[...]) + bg_ref[...]
        e = jnp.exp(-jnp.abs(logit))
        la_ref[...] = (jnp.minimum(logit, 0.0) - jnp.log(1.0 + e)) / GLA_TAU
        oms_ref[...] = jnp.where(logit >= 0.0, e, 1.0) / (1.0 + e)

    out_shape = [jax.ShapeDtypeStruct((s + BLOCK if i in (1, 2) else s, w), MXU_DTYPE if i in (0, 1, 2, 6) else F32)
                 for i, w in enumerate(widths)]
    out_shape += [jax.ShapeDtypeStruct((s, 256), F32)] * 2 + [jax.ShapeDtypeStruct((s, 128), F32)] * 3
    out_shape += [jax.ShapeDtypeStruct((s, D_MODEL), MXU_DTYPE)]
    out_shape += [jax.ShapeDtypeStruct((D_IN_PROJ, D_MODEL), MXU_DTYPE), jax.ShapeDtypeStruct((GLA_RANK, 256), MXU_DTYPE),
                  jax.ShapeDtypeStruct((D_MODEL, D_MODEL), MXU_DTYPE)]
    return pl.pallas_call(
        body, name="in_proj", grid=(nsteps,),
        in_specs=[_rows(ts, D_MODEL), pl.BlockSpec(memory_space=pl.ANY), _full((GLA_RANK, 32)), _full((1, 256)),
                  _rows(ts, ROT_DIM), _full((D_OUT_SHARD, D_MODEL))],
        out_specs=[_full((s + BLOCK, w)) if i in (1, 2) else _rows(ts, w) for i, w in enumerate(widths)]
        + [_rows(ts, 256)] * 2 + [_rows(ts, 128)] * 3 + [_rows(ts, D_MODEL)]
        + [_full((D_IN_PROJ, D_MODEL)), _full((GLA_RANK, 256)), _full((D_MODEL, D_MODEL))],
        out_shape=out_shape,
        scratch_shapes=[pltpu.VMEM((N_DEV, D_IN_SHARD, D_MODEL), MXU_DTYPE), pltpu.VMEM((N_DEV, GLA_RANK, 32), MXU_DTYPE),
                        pltpu.VMEM((N_DEV, D_OUT_SHARD, D_MODEL), MXU_DTYPE),
                        pltpu.VMEM((D_IN_SHARD, D_MODEL), F32), pltpu.SemaphoreType.DMA]
        + 3 * _BlockGather.scratch(),
        compiler_params=_cparams(dimension_semantics=("arbitrary",)),
    )(x, w_in_t, wg_s, b_gate, cos_sin, w_out_s)


SWA_ROWS = SWA_GROUP * BLOCK


SWA_BIAS_SHAPE = (2, 2 * BLOCK, SWA_ROWS)


def _swa_bias_fill(bias_ref):
    shape = SWA_BIAS_SHAPE[1:]
    ki = lax.broadcasted_iota(jnp.int32, shape, 0)
    qi = lax.broadcasted_iota(jnp.int32, shape, 1) & (BLOCK - 1)
    dist = qi + BLOCK - ki
    ok = (dist >= 0) & (dist < BLOCK)
    bias_ref[0] = jnp.where(ok & (ki >= BLOCK), 0.0, -jnp.inf).astype(F32)
    bias_ref[1] = jnp.where(ok, 0.0, -jnp.inf).astype(F32)


SWA_SUB = 8


def _swa_bias_of(bias_ref, n, b):
    return bias_ref[jnp.minimum(n, 1)] if b == 0 else bias_ref[1]


def _swa_dup(t, j):
    t = t.astype(F32)
    low = lax.broadcasted_iota(jnp.int32, t.shape, 1) < SWA_HEAD_DIM
    keep = low if j == 0 else jnp.logical_not(low)
    return jnp.where(keep, t, pltpu.roll(t, SWA_HEAD_DIM, 1)).astype(MXU_DTYPE)


def _swa_stack(t, j):
    low = lax.broadcasted_iota(jnp.int32, (BLOCK, 128), 1) < SWA_HEAD_DIM
    zero = jnp.zeros((BLOCK, 128), t.dtype)
    blocks = []
    for p in (2 * j, 2 * j + 1):
        tp = t[:, 128 * p:128 * (p + 1)]
        blocks += [jnp.where(low, tp, zero), jnp.where(low, zero, tp)]
    return jnp.concatenate(blocks, axis=0)


def _swa_unstack(t):
    low = lax.broadcasted_iota(jnp.int32, (BLOCK, 128), 1) < SWA_HEAD_DIM
    return [jnp.where(low, t[2 * BLOCK * i:2 * BLOCK * i + BLOCK], t[2 * BLOCK * i + BLOCK:2 * BLOCK * (i + 1)])
            for i in range(2)]


def _swa_sink_row(sink_ref, j):
    lane = lax.broadcasted_iota(jnp.int32, (1, SWA_ROWS), 1)
    row = jnp.full((1, SWA_ROWS), sink_ref[SWA_GROUP * j], F32)
    for r in range(1, SWA_GROUP):
        row = jnp.where(lane >= BLOCK * r, sink_ref[SWA_GROUP * j + r], row)
    return row


def _split3(t):
    return jnp.concatenate(_split3_parts(t), axis=1)


def _row_sums_as_row(t):
    ones = jnp.ones((8, 3 * t.shape[1]), MXU_DTYPE)
    return _mm_nt(ones, _split3(t))[0:1, :]


def _swa_probs_t(qs, kd, bias_t, sink):
    sc = _mm_nt(kd, qs) + bias_t
    m = jnp.maximum(jnp.max(sc, axis=0, keepdims=True), sink)
    p = jnp.exp(sc - m)
    ps = jnp.exp(sink - m)
    rinv = 1.0 / (jnp.sum(p, axis=0, keepdims=True) + ps)
    return p * rinv, ps * rinv


def _swa_fwd(sinks, qa, k_pad, v_pad, ga):
    s = qa.shape[0]
    sub = min(SWA_SUB, s // BLOCK)
    tq = sub * BLOCK

    def body(sink_ref, qa_ref, ga_ref, k_ref, v_ref, attn_ref, cat_ref, bias_ref):
        n = pl.program_id(0)

        @pl.when(n == 0)
        def _():
            _swa_bias_fill(bias_ref)

        for b in range(sub):
            rows = slice(BLOCK * b, BLOCK * (b + 1))
            start = pl.multiple_of((n * sub + b) * BLOCK, BLOCK)
            kw = k_ref[pl.ds(start, 2 * BLOCK), :]
            vw = v_ref[pl.ds(start, 2 * BLOCK), :]
            bias_t = _swa_bias_of(bias_ref, n, b)
            q = qa_ref[rows, :] * SWA_SCALE
            g = ga_ref[rows, :]
            silu = g * _sigmoid(g)
            for j in range(SWA_KV_HEADS):
                qs = _swa_stack(q, j).astype(MXU_DTYPE)
                probs, _ = _swa_probs_t(qs, _swa_dup(kw, j), bias_t, _swa_sink_row(sink_ref, j))
                pairs = _swa_unstack(_mm_tn(probs.astype(MXU_DTYPE), _swa_dup(vw, j)))
                for i in range(2):
                    lanes = slice(128 * (2 * j + i), 128 * (2 * j + i + 1))
                    attn_ref[rows, lanes] = pairs[i]
                    cat_ref[rows, lanes] = (pairs[i] * silu[:, lanes]).astype(cat_ref.dtype)

    return pl.pallas_call(
        body, name="swa_fwd", grid=(s // tq,),
        in_specs=[pl.BlockSpec(memory_space=pltpu.SMEM), _rows(tq, 512), _rows(tq, 512),
                  _full((s + BLOCK, 128)), _full((s + BLOCK, 128))],
        out_specs=[_rows(tq, 512), _rows(tq, 512)],
        out_shape=[jax.ShapeDtypeStruct((s, 512), F32), jax.ShapeDtypeStruct((s, 512), MXU_DTYPE)],
        scratch_shapes=[pltpu.VMEM(SWA_BIAS_SHAPE, F32)],
        compiler_params=_cparams(dimension_semantics=("arbitrary",)),
    )(sinks, qa, ga, k_pad, v_pad)


GLA_KW = GLA_HEADS * GLA_DK
GLA_VW = GLA_HEADS * GLA_DV


def _idiv(t, d):
    return t >> (d.bit_length() - 1)


def _chunk_cumsum(t, lower):
    n, w = t.shape
    r = lax.broadcasted_iota(jnp.int32, (n, n), 0)
    c = lax.broadcasted_iota(jnp.int32, (n, n), 1)
    tri = ((_idiv(r, GLA_CHUNK) == _idiv(c, GLA_CHUNK)) & ((r >= c) if lower else (r <= c))).astype(MXU_DTYPE)
    parts = _mm(tri, _split3(t))
    return (parts[:, :w] + parts[:, w:2 * w]) + parts[:, 2 * w:]


def _chunk_last(t):
    n = t.shape[0]
    return jnp.concatenate(
        [jnp.broadcast_to(t[c + GLA_CHUNK - 1:c + GLA_CHUNK, :], (GLA_CHUNK, t.shape[1]))
         for c in range(0, n, GLA_CHUNK)], axis=0)


def _head_stack(t, width):
    head = _idiv(lax.broadcasted_iota(jnp.int32, t.shape, 1), width)
    zero = jnp.zeros_like(t)
    return jnp.concatenate([jnp.where(head == h, t, zero) for h in range(GLA_HEADS)], axis=0)


def _heads_to_rows(t):
    return jnp.concatenate([t[:, GLA_DV * h:GLA_DV * (h + 1)] for h in range(GLA_HEADS)], axis=0)


def _rows_to_heads(t):
    return jnp.concatenate([t[GLA_CHUNK * h:GLA_CHUNK * (h + 1)] for h in range(GLA_HEADS)], axis=1)


def _state_by_head(t):
    srow = _idiv(lax.broadcasted_iota(jnp.int32, (GLA_VW, GLA_KW), 0), GLA_DV)
    slane = _idiv(lax.broadcasted_iota(jnp.int32, (GLA_VW, GLA_KW), 1), GLA_DK)
    return jnp.where(srow == slane, jnp.concatenate([t] * GLA_HEADS, axis=0), jnp.zeros((GLA_VW, GLA_KW), t.dtype))


def _gla_masks():
    row = lax.broadcasted_iota(jnp.int32, (GLA_CHUNK, GLA_KW), 0)
    pos = lax.broadcasted_iota(jnp.int32, (GLA_CHUNK, GLA_KW), 1) & (GLA_CHUNK - 1)
    return pos <= row, pos >= row


def _gla_fwd(qb, kb, vb, la, gb, norm_w):
    s = qb.shape[0]
    tb = min(256, s)
    ch = tb // GLA_CHUNK

    def body(qb_ref, kb_ref, vb_ref, la_ref, gb_ref, nw_ref, o_ref, cat_ref, sp_ref, st_ref):
        @pl.when(pl.program_id(0) == 0)
        def _():
            st_ref[...] = jnp.zeros_like(st_ref)

        causal, _ = _gla_masks()
        nw = nw_ref[...]
        b = _chunk_cumsum(la_ref[...], True)
        bl = _chunk_last(b)
        k = kb_ref[...]
        qd = ((qb_ref[...] * GLA_SCALE) * jnp.exp(b)).astype(MXU_DTYPE)
        ki = (k * jnp.exp(-b)).astype(MXU_DTYPE)
        ke = (k * jnp.exp(bl - b)).astype(MXU_DTYPE)
        dec = jnp.exp(bl)
        v = vb_ref[...].astype(MXU_DTYPE)
        g = gb_ref[...]
        silu = g * _sigmoid(g)
        for ci in range(ch):
            rows = slice(GLA_CHUNK * ci, GLA_CHUNK * (ci + 1))
            qds, kis, kes = (_head_stack(t[rows], GLA_DK) for t in (qd, ki, ke))
            a = jnp.where(causal, _mm_nt(qd[rows], kis), 0.0).astype(MXU_DTYPE)
            st = st_ref[...]
            sp_ref[ci] = st
            o = _mm(a, _head_stack(v[rows], GLA_DV)) + _rows_to_heads(_mm_nt(qds, st.astype(MXU_DTYPE)))
            st_ref[...] = st * dec[rows][0:1] + _mm_tn(_heads_to_rows(v[rows]), kes)
            o_ref[rows, :] = o
            for h in range(GLA_HEADS):
                lv = slice(GLA_DV * h, GLA_DV * (h + 1))
                oh = o[:, lv]
                r = lax.rsqrt(jnp.mean(oh * oh, axis=1, keepdims=True) + EPS)
                cat_ref[rows, lv] = (oh * r * nw * silu[rows, lv]).astype(cat_ref.dtype)

    return pl.pallas_call(
        body, name="gla_fwd", grid=(s // tb,),
        in_specs=[_rows(tb, 256), _rows(tb, 256), _rows(tb, 512), _rows(tb, 256), _rows(tb, 512), _full((1, 128))],
        out_specs=[_rows(tb, 512), _rows(tb, 512), pl.BlockSpec((ch, GLA_DV, 256), lambda i: (i, 0, 0))],
        out_shape=[jax.ShapeDtypeStruct((s, 512), F32), jax.ShapeDtypeStruct((s, 512), MXU_DTYPE),
                   jax.ShapeDtypeStruct((s // GLA_CHUNK, GLA_DV, 256), F32)],
        scratch_shapes=[pltpu.VMEM((GLA_DV, GLA_KW), F32)],
        compiler_params=_cparams(dimension_semantics=("arbitrary",)),
    )(qb, kb, vb, la, gb, norm_w)


def _out_ln_loss(cat_a, cat_b, w_out, x, target, ln_g, ln_b):
    s = x.shape[0]
    ts = min(512, s)
    halves = 2 if ts % 32 == 0 else 1
    th = ts // halves

    def body(ca_ref, cb_ref, w_ref, x_ref, t_ref, g_ref, b_ref,
             loss_ref, gx_ref, da_ref, db_ref, gw_ref, gln_ref):
        @pl.when(pl.program_id(0) == 0)
        def _():
            loss_ref[...] = jnp.zeros_like(loss_ref)
            gw_ref[...] = jnp.zeros_like(gw_ref)
            gln_ref[...] = jnp.zeros_like(gln_ref)

        g = g_ref[...]
        dh16s = []
        for k in range(halves):
            rows = slice(th * k, th * (k + 1))
            mix = _mm(ca_ref[rows, :], w_ref[0:512, :]) + _mm(cb_ref[rows, :], w_ref[512:1024, :])
            h = ALPHA * x_ref[rows, :] + mix
            mu = jnp.mean(h, axis=1, keepdims=True)
            hc = h - mu
            rstd = lax.rsqrt(jnp.mean(hc * hc, axis=1, keepdims=True) + EPS)
            xhat = hc * rstd
            err = xhat * g + b_ref[...] - t_ref[rows, :]
            loss_ref[...] += 0.5 * jnp.sum(jnp.mean(err * err, axis=1, keepdims=True))
            dy = err * (1.0 / D_MODEL)
            gln_ref[0:1, :] += jnp.sum(dy * xhat, axis=0, keepdims=True)
            gln_ref[1:2, :] += jnp.sum(dy, axis=0, keepdims=True)
            dxh = dy * g
            dh = rstd * (dxh - jnp.mean(dxh, axis=1, keepdims=True)
                         - xhat * jnp.mean(dxh * xhat, axis=1, keepdims=True))
            gx_ref[rows, :] = ALPHA * dh
            dh16s.append(dh.astype(MXU_DTYPE))
        for k in range(halves):
            rows = slice(th * k, th * (k + 1))
            da_ref[rows, :] = _mm_nt(dh16s[k], w_ref[0:512, :])
            db_ref[rows, :] = _mm_nt(dh16s[k], w_ref[512:1024, :])
        dh16 = jnp.concatenate(dh16s, axis=0)
        gw_ref[0:512, :] += _mm_tn(ca_ref[...], dh16)
        gw_ref[512:1024, :] += _mm_tn(cb_ref[...], dh16)

    return pl.pallas_call(
        body, name="out_ln_loss", grid=(s // ts,),
        in_specs=[_rows(ts, 512), _rows(ts, 512), _full((D_MODEL, D_MODEL)), _rows(ts, D_MODEL), _rows(ts, D_MODEL),
                  _full((1, D_MODEL)), _full((1, D_MODEL))],
        out_specs=[_full((1, 128)), _rows(ts, D_MODEL), _rows(ts, 512), _rows(ts, 512),
                   _full((D_MODEL, D_MODEL)), _full((2, D_MODEL))],
        out_shape=[jax.ShapeDtypeStruct((1, 128), F32), jax.ShapeDtypeStruct((s, D_MODEL), F32),
                   jax.ShapeDtypeStruct((s, 512), F32), jax.ShapeDtypeStruct((s, 512), F32),
                   jax.ShapeDtypeStruct((D_MODEL, D_MODEL), F32), jax.ShapeDtypeStruct((2, D_MODEL), F32)],
        compiler_params=_cparams(dimension_semantics=("arbitrary",)),
    )(cat_a, cat_b, w_out, x, target, ln_g, ln_b)


def _swa_bwd(sinks, qa, k_pad, v_pad, attn, ga, d_cat_a, rope):
    s = qa.shape[0]
    sub = min(SWA_SUB, s // BLOCK)
    tq = sub * BLOCK
    nsteps = s // tq

    def body(sink_ref, qa_ref, ga_ref, at_ref, dc_ref, c_ref, s1_ref, s2_ref, k_ref, v_ref,
             dq_ref, dg_ref, dk_out, dv_out, ds_ref, dk_ref, dv_ref, bias_ref):
        n = pl.program_id(0)

        @pl.when(n == 0)
        def _():
            _swa_bias_fill(bias_ref)
            dk_ref[...] = jnp.zeros_like(dk_ref)
            dv_ref[...] = jnp.zeros_like(dv_ref)
            ds_ref[...] = jnp.zeros_like(ds_ref)

        low = lax.broadcasted_iota(jnp.int32, (2 * BLOCK, 128), 1) < SWA_HEAD_DIM
        for b in range(sub):
            rows = slice(BLOCK * b, BLOCK * (b + 1))
            start = pl.multiple_of((n * sub + b) * BLOCK, BLOCK)
            kw = k_ref[pl.ds(start, 2 * BLOCK), :]
            vw = v_ref[pl.ds(start, 2 * BLOCK), :]
            bias_t = _swa_bias_of(bias_ref, n, b)
            q = qa_ref[rows, :] * SWA_SCALE
            g = ga_ref[rows, :]
            sg = _sigmoid(g)
            o = at_ref[rows, :]
            dc = dc_ref[rows, :]
            do = dc * (g * sg)
            dg_ref[rows, :] = (dc * o * (sg * (1.0 + g * (1.0 - sg)))).astype(dg_ref.dtype)
            od = do * o
            c, s1, s2 = c_ref[rows, :], s1_ref[rows, :], s2_ref[rows, :]
            dk, dv = [], []
            for j in range(SWA_KV_HEADS):
                kd, vd = _swa_dup(kw, j), _swa_dup(vw, j)
                qs = _swa_stack(q, j).astype(MXU_DTYPE)
                dos = _swa_stack(do, j).astype(MXU_DTYPE)
                probs, psink = _swa_probs_t(qs, kd, bias_t, _swa_sink_row(sink_ref, j))
                delta = _row_sums_as_row(_swa_stack(od, j))
                dsc = (probs * (_mm_nt(vd, dos) - delta)).astype(MXU_DTYPE)
                dsink = psink * delta
                for r in range(SWA_GROUP):
                    h = SWA_GROUP * j + r
                    ds_ref[h:h + 1, :] += jnp.zeros((1, 128), F32) - jnp.sum(dsink[:, BLOCK * r:BLOCK * (r + 1)])
                dq = _swa_unstack(_mm_tn(dsc, kd))
                for i in range(2):
                    lanes = slice(128 * (2 * j + i), 128 * (2 * j + i + 1))
                    dq_ref[rows, lanes] = _rope_t(dq[i] * SWA_SCALE, c, s1, s2).astype(dq_ref.dtype)
                dkj = _mm(dsc, qs)
                dvj = _mm(probs.astype(MXU_DTYPE), dos)
                dk.append(dkj + pltpu.roll(dkj, SWA_HEAD_DIM, 1))
                dv.append(dvj + pltpu.roll(dvj, SWA_HEAD_DIM, 1))
            dk_ref[pl.ds(start, 2 * BLOCK), :] += jnp.where(low, dk[0], dk[1])
            dv_ref[pl.ds(start, 2 * BLOCK), :] += jnp.where(low, dv[0], dv[1])

        @pl.when(n == nsteps - 1)
        def _():
            dk_out[...] = dk_ref[BLOCK:, :]
            dv_out[...] = dv_ref[BLOCK:, :]

    return pl.pallas_call(
        body, name="swa_bwd", grid=(nsteps,),
        in_specs=[pl.BlockSpec(memory_space=pltpu.SMEM)] + [_rows(tq, 512)] * 4 + [_rows(tq, 128)] * 3
        + [_full((s + BLOCK, 128))] * 2,
        out_specs=[_rows(tq, 512), _rows(tq, 512), _full((s, 128)), _full((s, 128)), _full((SWA_Q_HEADS, 128))],
        out_shape=[jax.ShapeDtypeStruct((s, 512), MXU_DTYPE), jax.ShapeDtypeStruct((s, 512), MXU_DTYPE),
                   jax.ShapeDtypeStruct((s, 128), F32), jax.ShapeDtypeStruct((s, 128), F32),
                   jax.ShapeDtypeStruct((SWA_Q_HEADS, 128), F32)],
        scratch_shapes=[pltpu.VMEM((s + BLOCK, 128), F32)] * 2 + [pltpu.VMEM(SWA_BIAS_SHAPE, F32)],
        compiler_params=_cparams(dimension_semantics=("arbitrary",)),
    )(sinks, qa, ga, attn, d_cat_a, *rope, k_pad, v_pad)


def _gla_bwd(qb, kb, vb, la, oms, gb, o, sprev, d_cat_b, rb, wg, norm_w):
    s = qb.shape[0]
    tb = min(512, s)
    ch = tb // GLA_CHUNK
    nb = s // tb

    def body(qb_ref, kb_ref, vb_ref, la_ref, oms_ref, gb_ref, o_ref, sp_ref, dc_ref, rb_ref, wg_ref, nw_ref,
             dq_ref, dk_ref, dv_ref, dg_ref, dr_ref, gwg_ref, gbg_ref, gnw_ref, dst_ref):
        @pl.when(pl.program_id(0) == 0)
        def _():
            dst_ref[...] = jnp.zeros_like(dst_ref)
            gwg_ref[...] = jnp.zeros_like(gwg_ref)
            gbg_ref[...] = jnp.zeros_like(gbg_ref)
            gnw_ref[...] = jnp.zeros_like(gnw_ref)

        causal, causal_t = _gla_masks()
        nw = nw_ref[...]
        b = _chunk_cumsum(la_ref[...], True)
        bl = _chunk_last(b)
        eb, enb, ee, dec = jnp.exp(b), jnp.exp(-b), jnp.exp(bl - b), jnp.exp(bl)
        k = kb_ref[...]
        qd = (qb_ref[...] * GLA_SCALE) * eb
        ki = k * enb
        ke = k * ee
        qd16, ki16, ke16 = qd.astype(MXU_DTYPE), ki.astype(MXU_DTYPE), ke.astype(MXU_DTYPE)
        v16 = vb_ref[...].astype(MXU_DTYPE)

        g = gb_ref[...]
        sg = _sigmoid(g)
        silu = g * sg
        dsilu = sg * (1.0 + g * (1.0 - sg))
        gnw = jnp.zeros((1, GLA_DV), F32)
        do = []
        for h in range(GLA_HEADS):
            lv = slice(GLA_DV * h, GLA_DV * (h + 1))
            oh = o_ref[:, lv]
            dch = dc_ref[:, lv]
            r = lax.rsqrt(jnp.mean(oh * oh, axis=1, keepdims=True) + EPS)
            d_on = dch * silu[:, lv]
            dg_ref[:, lv] = (dch * (oh * r * nw) * dsilu[:, lv]).astype(dg_ref.dtype)
            gnw += jnp.sum(d_on * oh * r, axis=0, keepdims=True)
            u = d_on * nw
            do.append(r * u - oh * (r * r * r) * jnp.mean(u * oh, axis=1, keepdims=True))
        gnw_ref[...] += gnw
        do16 = jnp.concatenate(do, axis=1).astype(MXU_DTYPE)

        db, dbl = [None] * ch, [None] * ch
        for ci in reversed(range(ch)):
            rows = slice(GLA_CHUNK * ci, GLA_CHUNK * (ci + 1))
            qds, kis, kes = (_head_stack(t[rows], GLA_DK) for t in (qd16, ki16, ke16))
            vs, dos = _head_stack(v16[rows], GLA_DV), _head_stack(do16[rows], GLA_DV)
            a = jnp.where(causal, _mm_nt(qd16[rows], kis), 0.0).astype(MXU_DTYPE)
            at = jnp.where(causal_t, _mm_nt(ki16[rows], qds), 0.0).astype(MXU_DTYPE)
            da = jnp.where(causal, _mm_nt(do16[rows], vs), 0.0).astype(MXU_DTYPE)
            dat = jnp.where(causal_t, _mm_nt(v16[rows], dos), 0.0).astype(MXU_DTYPE)
            st = sp_ref[ci]
            dst = dst_ref[...]
            dst16 = dst.astype(MXU_DTYPE)
            dv = _mm(at, dos) + _rows_to_heads(_mm_nt(kes, dst16))
            dqd = _mm(da, kis) + _mm(do16[rows], _state_by_head(st.astype(MXU_DTYPE)))
            dki = _mm(dat, qds)
            dke = _mm(v16[rows], _state_by_head(dst16))
            ddec = jnp.sum(dst * st, axis=0, keepdims=True)
            decc = dec[rows][0:1]
            dst_ref[...] = _mm_tn(_heads_to_rows(do16[rows]), qds) + dst * decc
            dq_ref[rows, :] = (dqd * eb[rows] * GLA_SCALE).astype(dq_ref.dtype)
            dk_ref[rows, :] = (dki * enb[rows] + dke * ee[rows]).astype(dk_ref.dtype)
            dv_ref[rows, :] = dv.astype(dv_ref.dtype)
            dke_ke = dke * ke[rows]
            db[ci] = dqd * qd[rows] - dki * ki[rows] - dke_ke
            dbl[ci] = jnp.broadcast_to(jnp.sum(dke_ke, axis=0, keepdims=True) + ddec * decc, (GLA_CHUNK, GLA_KW))

        dla = _chunk_cumsum(jnp.concatenate(db, axis=0), False) + jnp.concatenate(dbl, axis=0)
        dlogit = dla * oms_ref[...] * (1.0 / GLA_TAU)
        dl16 = dlogit.astype(MXU_DTYPE)
        gbg_ref[...] += jnp.sum(dlogit, axis=0, keepdims=True)
        gwg = _mm_tn(rb_ref[...].astype(MXU_DTYPE), dl16)
        for j in range(N_DEV):
            gwg_ref[j] += gwg[:, 32 * j:32 * (j + 1)]
        dr_ref[...] = _mm_nt(dl16, wg_ref[...]).astype(dr_ref.dtype)

    def rev(width):
        return pl.BlockSpec((tb, width), lambda i: (nb - 1 - i, 0))

    return pl.pallas_call(
        body, name="gla_bwd", grid=(nb,),
        in_specs=[rev(256), rev(256), rev(512), rev(256), rev(256), rev(512), rev(512),
                  pl.BlockSpec((ch, GLA_DV, 256), lambda i: (nb - 1 - i, 0, 0)), rev(512), rev(GLA_RANK),
                  _full((GLA_RANK, 256)), _full((1, 128))],
        out_specs=[rev(256), rev(256), rev(512), rev(512), rev(GLA_RANK),
                   _full((N_DEV, GLA_RANK, 32)), _full((1, 256)), _full((1, 128))],
        out_shape=[jax.ShapeDtypeStruct((s, 256), MXU_DTYPE), jax.ShapeDtypeStruct((s, 256), MXU_DTYPE),
                   jax.ShapeDtypeStruct((s, 512), MXU_DTYPE), jax.ShapeDtypeStruct((s, 512), MXU_DTYPE),
                   jax.ShapeDtypeStruct((s, GLA_RANK), MXU_DTYPE), jax.ShapeDtypeStruct((N_DEV, GLA_RANK, 32), F32),
                   jax.ShapeDtypeStruct((1, 256), F32), jax.ShapeDtypeStruct((1, 128), F32)],
        scratch_shapes=[pltpu.VMEM((GLA_DV, GLA_KW), F32)],
        compiler_params=_cparams(dimension_semantics=("arbitrary",)),
    )(qb, kb, vb, la, oms, gb, o, sprev, d_cat_b, rb, wg, norm_w)


def _dproj_tiles(piece_refs, rope_refs, members=(0, 1, 3, 4, 5, 6, 7, 8)):
    for i in members:
        if i == 1:
            dk = _rope_t(piece_refs[1][...], *(r[...] for r in rope_refs))
            yield OFF[1], OFF[3], jnp.concatenate([dk, piece_refs[2][...]], axis=1).astype(MXU_DTYPE)
        else:
            yield OFF[i], OFF[i + 1], piece_refs[i][...].astype(MXU_DTYPE)


def _in_proj_bwd_x(gx0, pieces, w_in, rope):
    s = gx0.shape[0]
    ts = min(512, s)
    widths = [OFF[i + 1] - OFF[i] for i in range(9)]

    def body(gx0_ref, *refs):
        w_ref, gx_ref = refs[12:]
        acc = gx0_ref[...]
        for lo, hi, t16 in _dproj_tiles(refs[:9], refs[9:12]):
            acc += _mm(t16, w_ref[lo:hi, :])
        gx_ref[...] = acc

    return pl.pallas_call(
        body, name="in_proj_bwd_x", grid=(s // ts,),
        in_specs=[_rows(ts, D_MODEL)] + [_rows(ts, w) for w in widths] + [_rows(ts, 128)] * 3
        + [_full((D_IN_PROJ, D_MODEL))],
        out_specs=_rows(ts, D_MODEL),
        out_shape=jax.ShapeDtypeStruct((s, D_MODEL), F32),
        compiler_params=_cparams(dimension_semantics=("arbitrary",)),
    )(gx0, *pieces, *rope, w_in)


GW_PASSES = 2
GW_EVENT_STEPS = (0, 2, 4, 6)


def _in_proj_bwd_w(x, pieces, rope, parts_wg, g_ln, g_bg, g_nw, g_sinks, loss, parts_w_out):
    s = x.shape[0]
    ts = min(1024, s)
    nt = s // ts
    n_pass = GW_PASSES
    n_steps = n_pass * nt
    cw = D_MODEL // n_pass
    n_chips = N_DEV // 2
    blk = (D_IN_SHARD, cw)
    out_blk = parts_w_out.shape[1:]
    w_out_forward_step = min(1, n_steps - 1)
    w_out_finish_step = max(n_steps - 2, w_out_forward_step)

    def body(x_hbm, *refs):
        piece_refs, rope_refs = refs[:9], refs[9:12]
        (pwg_ref, gln_ref, gbg_ref, gnw_ref, gsk_ref, loss_ref, pout_ref,
         gin_ref, rwg_ref, rsm_ref, gout_ref) = refs[12:23]
        (acc_ref, stage_ref, sib_ref, snd_ref, rcv_ref, sm_ref, xs_ref,
         d2d_send, d2d_recv, ici_send, ici_recv, out_sems, sm_send, sm_recv, sm_loc, x_sems) = refs[23:39]
        w_out_sum = _OwnerSum(pout_ref, *refs[39:])
        p, t = pl.program_id(0), pl.program_id(1)
        step = p * nt + t
        tile = pl.ds(pl.multiple_of(t * ts, ts), ts)

        def x_load(k, q):
            rows = pl.ds(pl.multiple_of(k * ts, ts), ts)
            return pltpu.make_async_copy(x_hbm.at[rows, pl.ds(q * cw, cw)], xs_ref.at[q, rows, :],
                                         x_sems.at[q * nt + k])
        x_, y_, c = _mesh_pos()
        me, mychip, sibling = 4 * x_ + 2 * y_ + c, 2 * x_ + y_, (x_, y_, 1 - c)
        small_dsts = (rwg_ref, rsm_ref)

        def small_src(a, block):
            return pwg_ref.at[block] if a == 0 else sm_ref

        def small_copy(k, a, src_block, dst_block, peer):
            i = 2 * (k - 1) + a
            return pltpu.make_async_remote_copy(
                src_ref=small_src(a, src_block), dst_ref=small_dsts[a].at[dst_block], send_sem=sm_send.at[i],
                recv_sem=sm_recv.at[i], device_id=peer, device_id_type=pl.DeviceIdType.MESH)

        def small_local(a):
            return pltpu.make_async_copy(small_src(a, me), small_dsts[a].at[me], sm_loc.at[a])

        @pl.when(step == 0)
        def _():
            for q in range(n_pass):
                for k in range(nt):
                    x_load(k, q).start()
            w_out_sum.start()
            acc_ref[...] = jnp.zeros_like(acc_ref)
            sm_ref[...] = jnp.zeros_like(sm_ref)
            for r in range(D_MODEL // 128):
                sm_ref[r:r + 1, :] = gln_ref[0:1, 128 * r:128 * (r + 1)]
                sm_ref[8 + r:9 + r, :] = gln_ref[1:2, 128 * r:128 * (r + 1)]
            for r in range(2):
                sm_ref[16 + r:17 + r, :] = gbg_ref[0:1, 128 * r:128 * (r + 1)]
            sm_ref[24:25, :] = gnw_ref[...]
            diag = (lax.broadcasted_iota(jnp.int32, gsk_ref.shape, 0)
                    == lax.broadcasted_iota(jnp.int32, gsk_ref.shape, 1))
            sm_ref[32:33, :] = jnp.sum(jnp.where(diag, gsk_ref[...], 0.0), axis=0, keepdims=True)
            sm_ref[40:41, :] = loss_ref[...]
            for a in range(2):
                small_local(a).start()
            for k in range(1, N_DEV):
                peer, pidx = _peer(k, x_, y_, c)
                for a in range(2):
                    small_copy(k, a, pidx, me, peer).start()

        for q in range(n_pass):
            @pl.when(p == q)
            def _(q=q):
                x_load(t, q).wait()

        xb = xs_ref[p, tile, :]
        for lo, hi, t16 in _dproj_tiles(piece_refs, rope_refs):
            acc_ref[p, lo:hi, :] += _mm_tn(t16, xb)

        @pl.when(step == w_out_forward_step)
        def _():
            w_out_sum.forward()

        @pl.when(step == w_out_finish_step)
        def _():
            gout_ref[...] = w_out_sum.finish()

        def block_rows(q, j):
            return acc_ref[q, D_IN_SHARD * j:D_IN_SHARD * (j + 1), :]

        def d2d(q):
            return pltpu.make_async_remote_copy(
                src_ref=stage_ref, dst_ref=sib_ref.at[q], send_sem=d2d_send.at[q],
                recv_sem=d2d_recv.at[q], device_id=sibling, device_id_type=pl.DeviceIdType.MESH)

        def ici(q, slot, owner):
            i = 3 * q + slot
            return pltpu.make_async_remote_copy(
                src_ref=snd_ref.at[q, slot], dst_ref=rcv_ref.at[q, slot], send_sem=ici_send.at[i],
                recv_sem=ici_recv.at[i], device_id=owner, device_id_type=pl.DeviceIdType.MESH)

        def out_copy(q):
            return pltpu.make_async_copy(sib_ref.at[q, mychip], gin_ref.at[:, pl.ds(q * cw, cw)], out_sems.at[q])

        first = (jnp.where(c == 0, 1 - x_, x_), jnp.where(c == 0, y_, 1 - y_))
        second = (jnp.where(c == 0, x_, 1 - x_), jnp.where(c == 0, 1 - y_, y_))

        def chip_of(pos):
            return 2 * pos[0] + pos[1]

        def to_sibling(q):
            if q >= 1:
                d2d(q - 1).wait_send()
            for cc in range(2):
                @pl.when(c == cc)
                def _(cc=cc):
                    for k in range(n_chips):
                        stage_ref[k] = block_rows(q, 2 * k + 1 - cc)
            d2d(q).start()

        def chip_sums_leave(q):
            d2d(q).wait_recv()
            for cc in range(2):
                @pl.when(c == cc)
                def _(cc=cc):
                    for k in range(n_chips):
                        sib_ref[q, k] = block_rows(q, 2 * k + cc) + sib_ref[q, k]
            snd_ref[q, 2] = sib_ref[q, chip_of((1 - x_, 1 - y_))].astype(snd_ref.dtype)
            ici(q, 2, (*first, c)).start()
            snd_ref[q, 0] = sib_ref[q, chip_of(first)].astype(snd_ref.dtype)
            ici(q, 0, (*first, c)).start()

        def combined_sum_leaves(q):
            ici(q, 2, sibling).wait_recv()
            snd_ref[q, 1] = (sib_ref[q, chip_of(second)] + rcv_ref[q, 2].astype(F32)).astype(snd_ref.dtype)
            ici(q, 1, (*second, c)).start()

        def owner_total(q):
            total = sib_ref[q, mychip]
            for slot in range(2):
                ici(q, slot, sibling).wait_recv()
                total = total + rcv_ref[q, slot].astype(F32)
            sib_ref[q, mychip] = total
            out_copy(q).start()

        stages = (to_sibling, chip_sums_leave, combined_sum_leaves, owner_total)
        events = sorted((min((q + 1) * nt - 1 + GW_EVENT_STEPS[i], n_steps - 1), i > 0, 2 * q + i, -q, q, i)
                        for q in range(n_pass) for i in range(len(stages)))
        for at_step, _, _, _, q, i in events:
            @pl.when(step == at_step)
            def _(q=q, i=i):
                stages[i](q)

        @pl.when(step == n_steps - 1)
        def _():
            for k in range(1, N_DEV):
                peer, pidx = _peer(k, x_, y_, c)
                for a in range(2):
                    small_copy(k, a, me, pidx, peer).wait_recv()
            for k in range(1, N_DEV):
                peer, pidx = _peer(k, x_, y_, c)
                for a in range(2):
                    small_copy(k, a, pidx, me, peer).wait_send()
            for a in range(2):
                small_local(a).wait()
            d2d(n_pass - 1).wait_send()
            for q in range(n_pass):
                for slot in range(3):
                    ici(q, slot, sibling).wait_send()
            for q in range(n_pass):
                out_copy(q).wait()

    widths = [OFF[i + 1] - OFF[i] for i in range(9)]
    hbm = pl.BlockSpec(memory_space=pl.ANY)
    vmem = pl.BlockSpec(memory_space=pltpu.VMEM)

    def token_tile(width):
        return pl.BlockSpec((ts, width), lambda p, t: (t, 0))

    return pl.pallas_call(
        body, name="in_proj_bwd_w", grid=(n_pass, nt),
        in_specs=[hbm] + [token_tile(w) for w in widths] + [token_tile(128)] * 3 + [hbm] + [vmem] * 5 + [hbm],
        out_specs=[hbm, hbm, hbm, vmem],
        out_shape=[jax.ShapeDtypeStruct((D_IN_SHARD, D_MODEL), F32),
                   jax.ShapeDtypeStruct((N_DEV,) + parts_wg.shape[1:], F32),
                   jax.ShapeDtypeStruct((N_DEV, SMALL_ROWS, 128), F32), jax.ShapeDtypeStruct(out_blk, F32)],
        scratch_shapes=[pltpu.VMEM((n_pass, D_IN_PROJ, cw), F32), pltpu.VMEM((n_chips,) + blk, F32),
                        pltpu.VMEM((n_pass, n_chips) + blk, F32), pltpu.VMEM((n_pass, 3) + blk, MXU_DTYPE),
                        pltpu.VMEM((n_pass, 3) + blk, MXU_DTYPE),
                        pltpu.VMEM((SMALL_ROWS, 128), F32), pltpu.VMEM((n_pass, s, cw), MXU_DTYPE),
                        pltpu.SemaphoreType.DMA((n_pass,)), pltpu.SemaphoreType.DMA((n_pass,)),
                        pltpu.SemaphoreType.DMA((3 * n_pass,)), pltpu.SemaphoreType.DMA((3 * n_pass,)),
                        pltpu.SemaphoreType.DMA((n_pass,)),
                        pltpu.SemaphoreType.DMA((2 * (N_DEV - 1),)), pltpu.SemaphoreType.DMA((2 * (N_DEV - 1),)),
                        pltpu.SemaphoreType.DMA((2,)), pltpu.SemaphoreType.DMA((n_pass * nt,))]
        + _OwnerSum.scratch(out_blk),
        compiler_params=_cparams(dimension_semantics=("arbitrary", "arbitrary")),
    )(x, *pieces, *rope, parts_wg, g_ln, g_bg, g_nw, g_sinks, loss, parts_w_out)


def _local_step(x, positions, w_in_t, wg_s, b_gate, sinks, norm_w, w_out_s, ln_g, ln_b, target):
    qa, k_pad, v_pad, ga, qb, kb, vb, gb, rb, la, oms, *rope, x16, w_in, wg, w_out = _in_proj(
        x, w_in_t, wg_s, b_gate, _rope_angles(positions), w_out_s)
    attn, cat_a = _swa_fwd(sinks, qa, k_pad, v_pad, ga)
    o, cat_b, sprev = _gla_fwd(qb, kb, vb, la, gb, norm_w)
    loss, gx0, d_cat_a, d_cat_b, g_w_out, g_ln = _out_ln_loss(cat_a, cat_b, w_out, x, target, ln_g, ln_b)
    parts_w_out = g_w_out.reshape(N_DEV, D_OUT_SHARD, D_MODEL)
    dqa, dga, dka, dva, g_sinks = _swa_bwd(sinks, qa, k_pad, v_pad, attn, ga, d_cat_a, rope)
    dqb, dkb, dvb, dgb, drb, parts_wg, g_bg, g_nw = _gla_bwd(
        qb, kb, vb, la, oms, gb, o, sprev, d_cat_b, rb, wg, norm_w)
    pieces = (dqa, dka, dva, dga, dqb, dkb, dvb, dgb, drb)
    grad_x = _in_proj_bwd_x(gx0, pieces, w_in, rope)
    g_in, r_wg, r_small, g_out = _in_proj_bwd_w(
        x16, pieces, rope, parts_wg, g_ln, g_bg, g_nw, g_sinks, loss, parts_w_out)
    return grad_x, g_in, g_out, r_wg, r_small


def _mesh_pos():
    return lax.axis_index("x"), lax.axis_index("y"), lax.axis_index("c")


def _peer(k, x, y, c):
    px = (1 - x) if k & 4 else x
    py = (1 - y) if k & 2 else y
    pc = (1 - c) if k & 1 else c
    return (px, py, pc), 4 * px + 2 * py + pc


def _other_chips(x, y):
    return [(1 - x, y), (x, 1 - y), (1 - x, 1 - y)]


def _shard_view(t):
    return jnp.transpose(t, (2, 0, 1))


class _BlockGather:
    def __init__(self, slots, send_sems, recv_sems):
        self.slots, self.send_sems, self.recv_sems = slots, send_sems, recv_sems
        x, y, c = _mesh_pos()
        self.c, self.me, self.sibling = c, 4 * x + 2 * y + c, (x, y, 1 - c)
        first = (jnp.where(c == 0, 1 - x, x), jnp.where(c == 0, y, 1 - y))
        second = (jnp.where(c == 0, x, 1 - x), jnp.where(c == 0, 1 - y, y))
        self.chips = [first, second, (1 - x, 1 - y)]

    @staticmethod
    def scratch():
        return [pltpu.SemaphoreType.DMA((N_DEV - 1,)), pltpu.SemaphoreType.DMA((N_DEV - 1,))]

    def _copy(self, k, block, to):
        return pltpu.make_async_remote_copy(
            src_ref=self.slots.at[block], dst_ref=self.slots.at[block], send_sem=self.send_sems.at[k],
            recv_sem=self.recv_sems.at[k], device_id=to, device_id_type=pl.DeviceIdType.MESH)

    def _block(self, j, c):
        cx, cy = self.chips[j]
        return 4 * cx + 2 * cy + c

    def _dev(self, j):
        return (*self.chips[j], self.c)

    def start(self):
        self._copy(1, self.me, self._dev(0)).start()
        self._copy(2, self.me, self._dev(1)).start()
        self._copy(0, self.me, self.sibling).start()

    def forward(self):
        self._copy(1, self._block(0, self.c), self.sibling).wait_recv()
        self._copy(3, self._block(0, self.c), self._dev(1)).start()
        self._copy(4, self._block(0, self.c), self.sibling).start()
        self._copy(2, self._block(1, self.c), self.sibling).wait_recv()
        self._copy(5, self._block(1, self.c), self.sibling).start()

    def forward_far(self):
        self._copy(3, self._block(2, self.c), self.sibling).wait_recv()
        self._copy(6, self._block(2, self.c), self.sibling).start()

    def finish(self):
        for k in (0, 4, 5, 6):
            self._copy(k, self.me, self.sibling).wait_recv()
        for k in range(N_DEV - 1):
            self._copy(k, self.me, self.sibling).wait_send()


class _OwnerSum:
    def __init__(self, parts, own, sib, snd, rcv, loc_sems, d2d_send, d2d_recv, ici_send, ici_recv):
        self.parts, self.own, self.sib, self.snd, self.rcv = parts, own, sib, snd, rcv
        self.sems = (loc_sems, d2d_send, d2d_recv, ici_send, ici_recv)
        x, y, c = _mesh_pos()
        self.c, self.sibling = c, (x, y, 1 - c)
        self.chips = [(x, y)] + _other_chips(x, y)

    @staticmethod
    def scratch(block):
        return [pltpu.VMEM((4,) + block, F32), pltpu.VMEM((4,) + block, F32),
                pltpu.VMEM((3,) + block, MXU_DTYPE), pltpu.VMEM((3,) + block, MXU_DTYPE),
                pltpu.SemaphoreType.DMA((4,)), pltpu.SemaphoreType.DMA((4,)), pltpu.SemaphoreType.DMA((4,)),
                pltpu.SemaphoreType.DMA((3,)), pltpu.SemaphoreType.DMA((3,))]

    def _local(self, r):
        cx, cy = self.chips[r]
        return pltpu.make_async_copy(self.parts.at[4 * cx + 2 * cy + self.c], self.own.at[r], self.sems[0].at[r])

    def _d2d(self, r):
        cx, cy = self.chips[r]
        return pltpu.make_async_remote_copy(
            src_ref=self.parts.at[4 * cx + 2 * cy + (1 - self.c)], dst_ref=self.sib.at[r], send_sem=self.sems[1].at[r],
            recv_sem=self.sems[2].at[r], device_id=self.sibling, device_id_type=pl.DeviceIdType.MESH)

    def _ici(self, r):
        cx, cy = self.chips[r]
        return pltpu.make_async_remote_copy(
            src_ref=self.snd.at[r - 1], dst_ref=self.rcv.at[r - 1], send_sem=self.sems[3].at[r - 1],
            recv_sem=self.sems[4].at[r - 1], device_id=(cx, cy, self.c), device_id_type=pl.DeviceIdType.MESH)

    def start(self):
        for r in (1, 2, 3, 0):
            self._local(r).start()
            self._d2d(r).start()

    def forward(self):
        for r in (1, 2, 3):
            self._local(r).wait()
            self._d2d(r).wait_recv()
            self.snd[r - 1] = (self.own[r] + self.sib[r]).astype(self.snd.dtype)
            self._ici(r).start()

    def finish(self):
        self._local(0).wait()
        self._d2d(0).wait_recv()
        acc = self.own[0] + self.sib[0]
        for r in (1, 2, 3):
            self._ici(r).wait_recv()
            acc = acc + self.rcv[r - 1].astype(F32)
        for r in range(4):
            self._d2d(r).wait_send()
        for r in (1, 2, 3):
            self._ici(r).wait_send()
        return acc


SMALL_ROWS = 48


def _adamw_math(g, w, m, v):
    nm = ADAM_B1 * m + (1.0 - ADAM_B1) * g
    nv = ADAM_B2 * v + (1.0 - ADAM_B2) * (g * g)
    m_hat = nm / (1.0 - ADAM_B1 ** ADAM_STEP)
    v_hat = nv / (1.0 - ADAM_B2 ** ADAM_STEP)
    return -ADAM_LR * (m_hat / (jnp.sqrt(v_hat) + ADAM_EPS) + ADAM_WD * w), nm, nv


def _adamw(g_in, w_in_wmv, r_small, r_wg, g_out, params):
    n_par = len(params)
    rows, width = g_in.shape
    parts = 3
    pr = rows // parts
    assert pr * parts == rows

    def body(gin_ref, w_hbm, m_hbm, v_hbm, rsm_ref, rwg_ref, gout_ref, *refs):
        ins, outs = refs[:3 * n_par], refs[3 * n_par:7 * n_par + 1]
        big_outs = refs[7 * n_par + 1:7 * n_par + 5]
        bufs, stage, sems = refs[7 * n_par + 5:]

        def load(k, i, src):
            return pltpu.make_async_copy(src.at[pl.ds(pr * k, pr), 0, :], bufs.at[k, i], sems.at[7 * k + i])

        def store(k, i, dst):
            return pltpu.make_async_copy(stage.at[k, i], dst.at[pl.ds(pr * k, pr), 0, :], sems.at[7 * k + 3 + i])

        for k in range(parts):
            for i, src in enumerate((w_hbm, m_hbm, v_hbm)):
                load(k, i, src).start()

        g = rsm_ref[0]
        gwg = rwg_ref[0]
        for j in range(1, N_DEV):
            g = g + rsm_ref[j]
            gwg = gwg + rwg_ref[j]
        outs[4 * n_par][...] = g[40:41]
        grads = [gwg,
                 jnp.concatenate([g[r:r + 1] for r in range(0, 8)], axis=1),
                 jnp.concatenate([g[r:r + 1] for r in range(8, 16)], axis=1),
                 jnp.concatenate([g[16:17], g[17:18]], axis=1),
                 g[24:25],
                 g[32:33, 0:SWA_Q_HEADS],
                 gout_ref[...]]
        for p, gp in enumerate(grads):
            w_ref, m_ref, v_ref = ins[3 * p:3 * p + 3]
            outs[4 * p][...] = gp
            outs[4 * p + 1][...], outs[4 * p + 2][...], outs[4 * p + 3][...] = _adamw_math(
                gp, w_ref[...], m_ref[...], v_ref[...])

        for k in range(parts):
            for i, src in enumerate((w_hbm, m_hbm, v_hbm)):
                load(k, i, src).wait()
            gk = gin_ref[pr * k:pr * (k + 1), :]
            stage[k, 0] = gk
            stage[k, 1], stage[k, 2], stage[k, 3] = _adamw_math(gk, bufs[k, 0], bufs[k, 1], bufs[k, 2])
            for i, dst in enumerate(big_outs):
                store(k, i, dst).start()
        for k in range(parts):
            for i, dst in enumerate(big_outs):
                store(k, i, dst).wait()

    vmem = pl.BlockSpec(memory_space=pltpu.VMEM)
    hbm = pl.BlockSpec(memory_space=pl.ANY)
    flat = [t for wmv in params for t in wmv]
    res = pl.pallas_call(
        body, name="adamw",
        in_specs=[vmem, hbm, hbm, hbm] + [vmem] * (3 + len(flat)),
        out_specs=[vmem] * (4 * n_par + 1) + [hbm] * 4,
        out_shape=[jax.ShapeDtypeStruct(wmv[0].shape, F32) for wmv in params for _ in range(4)]
        + [jax.ShapeDtypeStruct((1, 128), F32)] + [jax.ShapeDtypeStruct((rows, 1, width), F32)] * 4,
        scratch_shapes=[pltpu.VMEM((parts, 3, pr, width), F32), pltpu.VMEM((parts, 4, pr, width), F32),
                        pltpu.SemaphoreType.DMA((7 * parts,))],
        compiler_params=_cparams(),
    )(g_in, *w_in_wmv, r_small, r_wg, g_out, *flat)
    return res[:4 * n_par + 1], res[4 * n_par + 1:]


def kernel(x, positions, w_in, gla_w_gate_up, gla_b_gate, attn_sinks, gla_norm_w, w_out, ln_g, ln_b, loss_target, m_w_in, m_gla_w_gate_up, m_gla_b_gate, m_attn_sinks, m_gla_norm_w, m_w_out, m_ln_g, m_ln_b, v_w_in, v_gla_w_gate_up, v_gla_b_gate, v_attn_sinks, v_gla_norm_w, v_w_out, v_ln_g, v_ln_b):
    grad_x, g_in, g_out, r_wg, r_small = _local_step(
        x[0], positions[0], _shard_view(w_in), gla_w_gate_up[0], gla_b_gate, attn_sinks[0], gla_norm_w, w_out[0],
        ln_g, ln_b, loss_target[0])

    vec, upd_in = _adamw(g_in, (_shard_view(w_in), _shard_view(m_w_in), _shard_view(v_w_in)), r_small, r_wg, g_out, [
        (gla_w_gate_up[0], m_gla_w_gate_up[0], v_gla_w_gate_up[0]), (ln_g, m_ln_g, v_ln_g), (ln_b, m_ln_b, v_ln_b),
        (gla_b_gate, m_gla_b_gate, v_gla_b_gate), (gla_norm_w, m_gla_norm_w, v_gla_norm_w),
        (attn_sinks, m_attn_sinks, v_attn_sinks), (w_out[0], m_w_out[0], v_w_out[0])])
    upd_in = [jnp.transpose(t, (1, 2, 0)) for t in upd_in]

    outs = [vec[28][0, 0], grad_x[None]]
    for kind in range(4):
        u_wg, u_ln_g, u_ln_b, u_bg, u_nw, u_sinks, u_out = (vec[4 * p + kind] for p in range(7))
        outs += [upd_in[kind], u_wg[None], u_bg, u_sinks, u_nw, u_out[None], u_ln_g, u_ln_b]
    return tuple(outs)
```

```python
import jax
import jax.numpy as jnp
from jax import lax
from jax.experimental import pallas as pl
from jax.experimental.pallas import tpu as pltpu

F32 = jnp.float32
MXU_DTYPE = jnp.bfloat16

N_DEV = 8
D_MODEL = 1024
SWA_Q_HEADS = 8
SWA_KV_HEADS = 2
SWA_GROUP = 4
SWA_HEAD_DIM = 64
BLOCK = 128
ROPE_THETA = 500000.0
ROT_DIM = 16
GLA_HEADS = 4
GLA_DK = 64
GLA_DV = 128
GLA_RANK = 16
GLA_TAU = 16.0
GLA_CHUNK = 64
D_IN_PROJ = 2832
D_IN_SHARD = D_IN_PROJ // N_DEV
D_OUT_SHARD = D_MODEL // N_DEV
OFF = (0, 512, 640, 768, 1280, 1536, 1792, 2304, 2816, 2832)
EPS = 1e-5
ALPHA = 2.0 ** 0.25
SWA_SCALE = SWA_HEAD_DIM ** -0.5
GLA_SCALE = GLA_DK ** -0.5
ADAM_LR = 0.001
ADAM_B1 = 0.9
ADAM_B2 = 0.999
ADAM_EPS = 1e-08
ADAM_WD = 0.01
ADAM_STEP = 10
VMEM_LIMIT = 56 * 1024 * 1024

_NT = (((1,), (1,)), ((), ()))
_TN = (((0,), (0,)), ((), ()))


def _mm(a, b):
    return jnp.dot(a, b, preferred_element_type=F32)


def _mm_nt(a, b):
    return lax.dot_general(a, b, _NT, preferred_element_type=F32)


def _mm_tn(a, b):
    return lax.dot_general(a, b, _TN, preferred_element_type=F32)


def _sigmoid(t):
    return 1.0 / (1.0 + jnp.exp(-t))


def _cparams(**kw):
    return pltpu.CompilerParams(vmem_limit_bytes=VMEM_LIMIT, **kw)


def _full(shape):
    return pl.BlockSpec(shape, lambda *_: (0,) * len(shape))


def _rows(tile, width):
    return pl.BlockSpec((tile, width), lambda i: (i, 0))


def _rope_angles(positions):
    half = ROT_DIM // 2
    inv_freq = ROPE_THETA ** (-jnp.arange(half, dtype=F32) / half)
    ang = positions.astype(F32)[None, :] * inv_freq[:, None]
    return jnp.concatenate([jnp.cos(ang), jnp.sin(ang)], axis=0)


def _split3_parts(t):
    hi = t.astype(MXU_DTYPE)
    r1 = t - hi.astype(F32)
    mid = r1.astype(MXU_DTYPE)
    return hi, mid, (r1 - mid.astype(F32)).astype(MXU_DTYPE)


def _rope_tables(cs):
    half = ROT_DIM // 2
    i = lax.broadcasted_iota(jnp.int32, (2 * half, 3 * 128), 0)
    lane = lax.broadcasted_iota(jnp.int32, (2 * half, 3 * 128), 1)
    table, pos = _idiv(lane, 128), lane & (SWA_HEAD_DIM - 1)
    is_c = (table == 0) & (pos < ROT_DIM) & ((pos & (half - 1)) == i)
    is_s1 = (table == 1) & (pos < half) & (pos + half == i)
    is_s2 = (table == 2) & (pos >= half) & (pos < ROT_DIM) & (pos == i)
    sel = jnp.where(is_c | is_s2, 1.0, jnp.where(is_s1, -1.0, 0.0)).astype(MXU_DTYPE)
    hi, mid, lo = _split3_parts(cs)
    t = (_mm_tn(hi, sel) + _mm_tn(mid, sel)) + _mm_tn(lo, sel)
    pos1 = lax.broadcasted_iota(jnp.int32, (1, 128), 1) & (SWA_HEAD_DIM - 1)
    return t[:, 0:128] + jnp.where(pos1 >= ROT_DIM, 1.0, 0.0), t[:, 128:256], t[:, 256:384]


def _rope(t, c, s1, s2):
    return t * c + pltpu.roll(t, 120, 1) * s1 + pltpu.roll(t, 8, 1) * s2


def _rope_t(g, c, s1, s2):
    return g * c + pltpu.roll(g * s1, 8, 1) + pltpu.roll(g * s2, 120, 1)


def _in_proj(x, w_in_t, wg_s, b_gate, cos_sin, w_out_s):
    s = x.shape[0]
    ts = min(512, s)
    nsteps = s // ts
    forward_step, far_step = min(3, nsteps - 1), min(5, nsteps - 1)
    widths = [OFF[i + 1] - OFF[i] for i in range(9)]

    def body(x_ref, win_hbm, wgs_ref, bg_ref, cs_ref, wos_ref,
             qa_ref, ka_ref, va_ref, ga_ref, qb_ref, kb_ref, vb_ref, gb_ref, rb_ref, la_ref, oms_ref,
             c_ref, s1_ref, s2_ref, x16_ref, w_ref, wg_ref, wout_ref,
             win_all, wg_all, wout_all, stage, stage_sem, *sems):
        xb = x_ref[...].astype(MXU_DTYPE)
        x16_ref[...] = xb
        c, s1, s2 = _rope_tables(cs_ref[...])
        c_ref[...], s1_ref[...], s2_ref[...] = c, s1, s2
        i0 = pl.program_id(0)
        gather = _BlockGather(wout_all, *sems[0:2])

        @pl.when(i0 == 0)
        def _():
            ka_ref[0:BLOCK, :] = jnp.zeros((BLOCK, 128), ka_ref.dtype)
            va_ref[0:BLOCK, :] = jnp.zeros((BLOCK, 128), va_ref.dtype)
            first = (_BlockGather(win_all, *sems[2:4]), _BlockGather(wg_all, *sems[4:6]))
            load = pltpu.make_async_copy(win_hbm.at[:, 0, :], stage, stage_sem)
            load.start()
            wout_all[gather.me] = wos_ref[...].astype(wout_all.dtype)
            wg_all[gather.me] = wgs_ref[...].astype(wg_all.dtype)
            load.wait()
            win_all[gather.me] = stage[...].astype(win_all.dtype)
            for stage_of in ("start", "forward", "forward_far", "finish"):
                for g in first:
                    getattr(g, stage_of)()
            gather.start()
            for j in range(N_DEV):
                w_ref[D_IN_SHARD * j:D_IN_SHARD * (j + 1), :] = win_all[j]
                wg_ref[:, 32 * j:32 * (j + 1)] = wg_all[j]

        @pl.when(i0 == forward_step)
        def _():
            gather.forward()

        @pl.when(i0 == far_step)
        def _():
            gather.forward_far()

        @pl.when(i0 == nsteps - 1)
        def _():
            gather.finish()
            for j in range(N_DEV):
                wout_ref[D_OUT_SHARD * j:D_OUT_SHARD * (j + 1), :] = wout_all[j]

        kv_rows = pl.ds(pl.multiple_of(BLOCK + i0 * ts, BLOCK), ts)

        def cols(i):
            return _mm_nt(xb, w_ref[OFF[i]:OFF[i + 1], :])

        qa = cols(0)
        for i in range(4):
            qa_ref[:, 128 * i:128 * (i + 1)] = _rope(qa[:, 128 * i:128 * (i + 1)], c, s1, s2).astype(qa_ref.dtype)
        kv = _mm_nt(xb, w_ref[OFF[1]:OFF[3], :])
        ka_ref[kv_rows, :] = _rope(kv[:, 0:128], c, s1, s2).astype(ka_ref.dtype)
        va_ref[kv_rows, :] = kv[:, 128:256].astype(va_ref.dtype)
        ga_ref[...] = cols(3)
        qb_ref[...] = cols(4)
        kb_ref[...] = cols(5)
        vb_ref[...] = cols(6).astype(vb_ref.dtype)
        gb_ref[...] = cols(7)
        rb = cols(8)
        rb_ref[...] = rb
        logit = _mm(rb.astype(MXU_DTYPE), wg_ref[...]) + bg_ref[...]
        e = jnp.exp(-jnp.abs(logit))
        la_ref[...] = (jnp.minimum(logit, 0.0) - jnp.log(1.0 + e)) / GLA_TAU
        oms_ref[...] = jnp.where(logit >= 0.0, e, 1.0) / (1.0 + e)

    out_shape = [jax.ShapeDtypeStruct((s + BLOCK if i in (1, 2) else s, w), MXU_DTYPE if i in (0, 1, 2, 6) else F32)
                 for i, w in enumerate(widths)]
    out_shape += [jax.ShapeDtypeStruct((s, 256), F32)] * 2 + [jax.ShapeDtypeStruct((s, 128), F32)] * 3
    out_shape += [jax.ShapeDtypeStruct((s, D_MODEL), MXU_DTYPE)]
    out_shape += [jax.ShapeDtypeStruct((D_IN_PROJ, D_MODEL), MXU_DTYPE), jax.ShapeDtypeStruct((GLA_RANK, 256), MXU_DTYPE),
                  jax.ShapeDtypeStruct((D_MODEL, D_MODEL), MXU_DTYPE)]
    return pl.pallas_call(
        body, name="in_proj", grid=(nsteps,),
        in_specs=[_rows(ts, D_MODEL), pl.BlockSpec(memory_space=pl.ANY), _full((GLA_RANK, 32)), _full((1, 256)),
                  pl.BlockSpec((ROT_DIM, ts), lambda i: (0, i)), _full((D_OUT_SHARD, D_MODEL))],
        out_specs=[_full((s + BLOCK, w)) if i in (1, 2) else _rows(ts, w) for i, w in enumerate(widths)]
        + [_rows(ts, 256)] * 2 + [_rows(ts, 128)] * 3 + [_rows(ts, D_MODEL)]
        + [_full((D_IN_PROJ, D_MODEL)), _full((GLA_RANK, 256)), _full((D_MODEL, D_MODEL))],
        out_shape=out_shape,
        scratch_shapes=[pltpu.VMEM((N_DEV, D_IN_SHARD, D_MODEL), MXU_DTYPE), pltpu.VMEM((N_DEV, GLA_RANK, 32), MXU_DTYPE),
                        pltpu.VMEM((N_DEV, D_OUT_SHARD, D_MODEL), MXU_DTYPE),
                        pltpu.VMEM((D_IN_SHARD, D_MODEL), F32), pltpu.SemaphoreType.DMA]
        + 3 * _BlockGather.scratch(),
        compiler_params=_cparams(dimension_semantics=("arbitrary",)),
    )(x, w_in_t, wg_s, b_gate, cos_sin, w_out_s)


SWA_ROWS = SWA_GROUP * BLOCK


SWA_BIAS_SHAPE = (2, 2 * BLOCK, SWA_ROWS)


def _swa_bias_fill(bias_ref):
    shape = SWA_BIAS_SHAPE[1:]
    ki = lax.broadcasted_iota(jnp.int32, shape, 0)
    qi = lax.broadcasted_iota(jnp.int32, shape, 1) & (BLOCK - 1)
    dist = qi + BLOCK - ki
    ok = (dist >= 0) & (dist < BLOCK)
    bias_ref[0] = jnp.where(ok & (ki >= BLOCK), 0.0, -jnp.inf).astype(F32)
    bias_ref[1] = jnp.where(ok, 0.0, -jnp.inf).astype(F32)


SWA_SUB = 8


def _swa_bias_of(bias_ref, n, b):
    return bias_ref[jnp.minimum(n, 1)] if b == 0 else bias_ref[1]


def _swa_dup(t, j):
    t = t.astype(F32)
    low = lax.broadcasted_iota(jnp.int32, t.shape, 1) < SWA_HEAD_DIM
    keep = low if j == 0 else jnp.logical_not(low)
    return jnp.where(keep, t, pltpu.roll(t, SWA_HEAD_DIM, 1)).astype(MXU_DTYPE)


def _swa_stack(t, j):
    low = lax.broadcasted_iota(jnp.int32, (BLOCK, 128), 1) < SWA_HEAD_DIM
    zero = jnp.zeros((BLOCK, 128), t.dtype)
    blocks = []
    for p in (2 * j, 2 * j + 1):
        tp = t[:, 128 * p:128 * (p + 1)]
        blocks += [jnp.where(low, tp, zero), jnp.where(low, zero, tp)]
    return jnp.concatenate(blocks, axis=0)


def _swa_unstack(t):
    low = lax.broadcasted_iota(jnp.int32, (BLOCK, 128), 1) < SWA_HEAD_DIM
    return [jnp.where(low, t[2 * BLOCK * i:2 * BLOCK * i + BLOCK], t[2 * BLOCK * i + BLOCK:2 * BLOCK * (i + 1)])
            for i in range(2)]


def _swa_sink_row(sink_ref, j):
    lane = lax.broadcasted_iota(jnp.int32, (1, SWA_ROWS), 1)
    row = jnp.full((1, SWA_ROWS), sink_ref[SWA_GROUP * j], F32)
    for r in range(1, SWA_GROUP):
        row = jnp.where(lane >= BLOCK * r, sink_ref[SWA_GROUP * j + r], row)
    return row


def _split3(t):
    return jnp.concatenate(_split3_parts(t), axis=1)


def _row_sums_as_row(t):
    ones = jnp.ones((8, 3 * t.shape[1]), MXU_DTYPE)
    return _mm_nt(ones, _split3(t))[0:1, :]


def _swa_probs_t(qs, kd, bias_t, sink):
    sc = _mm_nt(kd, qs) + bias_t
    m = jnp.maximum(jnp.max(sc, axis=0, keepdims=True), sink)
    p = jnp.exp(sc - m)
    ps = jnp.exp(sink - m)
    rinv = 1.0 / (jnp.sum(p, axis=0, keepdims=True) + ps)
    return p * rinv, ps * rinv


def _swa_fwd(sinks, qa, k_pad, v_pad, ga):
    s = qa.shape[0]
    sub = min(SWA_SUB, s // BLOCK)
    tq = sub * BLOCK

    def body(sink_ref, qa_ref, ga_ref, k_ref, v_ref, attn_ref, cat_ref, bias_ref):
        n = pl.program_id(0)

        @pl.when(n == 0)
        def _():
            _swa_bias_fill(bias_ref)

        for b in range(sub):
            rows = slice(BLOCK * b, BLOCK * (b + 1))
            start = pl.multiple_of((n * sub + b) * BLOCK, BLOCK)
            kw = k_ref[pl.ds(start, 2 * BLOCK), :]
            vw = v_ref[pl.ds(start, 2 * BLOCK), :]
            bias_t = _swa_bias_of(bias_ref, n, b)
            q = qa_ref[rows, :] * SWA_SCALE
            g = ga_ref[rows, :]
            silu = g * _sigmoid(g)
            for j in range(SWA_KV_HEADS):
                qs = _swa_stack(q, j).astype(MXU_DTYPE)
                probs, _ = _swa_probs_t(qs, _swa_dup(kw, j), bias_t, _swa_sink_row(sink_ref, j))
                pairs = _swa_unstack(_mm_tn(probs.astype(MXU_DTYPE), _swa_dup(vw, j)))
                for i in range(2):
                    lanes = slice(128 * (2 * j + i), 128 * (2 * j + i + 1))
                    attn_ref[rows, lanes] = pairs[i]
                    cat_ref[rows, lanes] = (pairs[i] * silu[:, lanes]).astype(cat_ref.dtype)

    return pl.pallas_call(
        body, name="swa_fwd", grid=(s // tq,),
        in_specs=[pl.BlockSpec(memory_space=pltpu.SMEM), _rows(tq, 512), _rows(tq, 512),
                  _full((s + BLOCK, 128)), _full((s + BLOCK, 128))],
        out_specs=[_rows(tq, 512), _rows(tq, 512)],
        out_shape=[jax.ShapeDtypeStruct((s, 512), F32), jax.ShapeDtypeStruct((s, 512), MXU_DTYPE)],
        scratch_shapes=[pltpu.VMEM(SWA_BIAS_SHAPE, F32)],
        compiler_params=_cparams(dimension_semantics=("arbitrary",)),
    )(sinks, qa, ga, k_pad, v_pad)


GLA_KW = GLA_HEADS * GLA_DK
GLA_VW = GLA_HEADS * GLA_DV


def _idiv(t, d):
    return t >> (d.bit_length() - 1)


def _chunk_cumsum(t, lower):
    n, w = t.shape
    r = lax.broadcasted_iota(jnp.int32, (n, n), 0)
    c = lax.broadcasted_iota(jnp.int32, (n, n), 1)
    tri = ((_idiv(r, GLA_CHUNK) == _idiv(c, GLA_CHUNK)) & ((r >= c) if lower else (r <= c))).astype(MXU_DTYPE)
    parts = _mm(tri, _split3(t))
    return (parts[:, :w] + parts[:, w:2 * w]) + parts[:, 2 * w:]


def _chunk_last(t):
    n = t.shape[0]
    return jnp.concatenate(
        [jnp.broadcast_to(t[c + GLA_CHUNK - 1:c + GLA_CHUNK, :], (GLA_CHUNK, t.shape[1]))
         for c in range(0, n, GLA_CHUNK)], axis=0)


def _head_stack(t, width):
    head = _idiv(lax.broadcasted_iota(jnp.int32, t.shape, 1), width)
    zero = jnp.zeros_like(t)
    return jnp.concatenate([jnp.where(head == h, t, zero) for h in range(GLA_HEADS)], axis=0)


def _heads_to_rows(t):
    return jnp.concatenate([t[:, GLA_DV * h:GLA_DV * (h + 1)] for h in range(GLA_HEADS)], axis=0)


def _rows_to_heads(t):
    return jnp.concatenate([t[GLA_CHUNK * h:GLA_CHUNK * (h + 1)] for h in range(GLA_HEADS)], axis=1)


def _state_by_head(t):
    srow = _idiv(lax.broadcasted_iota(jnp.int32, (GLA_VW, GLA_KW), 0), GLA_DV)
    slane = _idiv(lax.broadcasted_iota(jnp.int32, (GLA_VW, GLA_KW), 1), GLA_DK)
    return jnp.where(srow == slane, jnp.concatenate([t] * GLA_HEADS, axis=0), jnp.zeros((GLA_VW, GLA_KW), t.dtype))


def _gla_masks():
    row = lax.broadcasted_iota(jnp.int32, (GLA_CHUNK, GLA_KW), 0)
    pos = lax.broadcasted_iota(jnp.int32, (GLA_CHUNK, GLA_KW), 1) & (GLA_CHUNK - 1)
    return pos <= row, pos >= row


def _gla_fwd(qb, kb, vb, la, gb, norm_w):
    s = qb.shape[0]
    tb = min(256, s)
    ch = tb // GLA_CHUNK

    def body(qb_ref, kb_ref, vb_ref, la_ref, gb_ref, nw_ref, o_ref, cat_ref, sp_ref, st_ref):
        @pl.when(pl.program_id(0) == 0)
        def _():
            st_ref[...] = jnp.zeros_like(st_ref)

        causal, _ = _gla_masks()
        nw = nw_ref[...]
        b = _chunk_cumsum(la_ref[...], True)
        bl = _chunk_last(b)
        k = kb_ref[...]
        qd = ((qb_ref[...] * GLA_SCALE) * jnp.exp(b)).astype(MXU_DTYPE)
        ki = (k * jnp.exp(-b)).astype(MXU_DTYPE)
        ke = (k * jnp.exp(bl - b)).astype(MXU_DTYPE)
        dec = jnp.exp(bl)
        v = vb_ref[...].astype(MXU_DTYPE)
        g = gb_ref[...]
        silu = g * _sigmoid(g)
        for ci in range(ch):
            rows = slice(GLA_CHUNK * ci, GLA_CHUNK * (ci + 1))
            qds, kis, kes = (_head_stack(t[rows], GLA_DK) for t in (qd, ki, ke))
            a = jnp.where(causal, _mm_nt(qd[rows], kis), 0.0).astype(MXU_DTYPE)
            st = st_ref[...]
            sp_ref[ci] = st
            o = _mm(a, _head_stack(v[rows], GLA_DV)) + _rows_to_heads(_mm_nt(qds, st.astype(MXU_DTYPE)))
            st_ref[...] = st * dec[rows][0:1] + _mm_tn(_heads_to_rows(v[rows]), kes)
            o_ref[rows, :] = o
            for h in range(GLA_HEADS):
                lv = slice(GLA_DV * h, GLA_DV * (h + 1))
                oh = o[:, lv]
                r = lax.rsqrt(jnp.mean(oh * oh, axis=1, keepdims=True) + EPS)
                cat_ref[rows, lv] = (oh * r * nw * silu[rows, lv]).astype(cat_ref.dtype)

    return pl.pallas_call(
        body, name="gla_fwd", grid=(s // tb,),
        in_specs=[_rows(tb, 256), _rows(tb, 256), _rows(tb, 512), _rows(tb, 256), _rows(tb, 512), _full((1, 128))],
        out_specs=[_rows(tb, 512), _rows(tb, 512), pl.BlockSpec((ch, GLA_DV, 256), lambda i: (i, 0, 0))],
        out_shape=[jax.ShapeDtypeStruct((s, 512), F32), jax.ShapeDtypeStruct((s, 512), MXU_DTYPE),
                   jax.ShapeDtypeStruct((s // GLA_CHUNK, GLA_DV, 256), F32)],
        scratch_shapes=[pltpu.VMEM((GLA_DV, GLA_KW), F32)],
        compiler_params=_cparams(dimension_semantics=("arbitrary",)),
    )(qb, kb, vb, la, gb, norm_w)


def _out_ln_loss(cat_a, cat_b, w_out, x, target, ln_g, ln_b):
    s = x.shape[0]
    ts = min(512, s)
    halves = 2 if ts % 32 == 0 else 1
    th = ts // halves

    def body(ca_ref, cb_ref, w_ref, x_ref, t_ref, g_ref, b_ref,
             loss_ref, gx_ref, da_ref, db_ref, gw_ref, gln_ref):
        @pl.when(pl.program_id(0) == 0)
        def _():
            loss_ref[...] = jnp.zeros_like(loss_ref)
            gw_ref[...] = jnp.zeros_like(gw_ref)
            gln_ref[...] = jnp.zeros_like(gln_ref)

        g = g_ref[...]
        dh16s = []
        for k in range(halves):
            rows = slice(th * k, th * (k + 1))
            mix = _mm(ca_ref[rows, :], w_ref[0:512, :]) + _mm(cb_ref[rows, :], w_ref[512:1024, :])
            h = ALPHA * x_ref[rows, :] + mix
            mu = jnp.mean(h, axis=1, keepdims=True)
            hc = h - mu
            rstd = lax.rsqrt(jnp.mean(hc * hc, axis=1, keepdims=True) + EPS)
            xhat = hc * rstd
            err = xhat * g + b_ref[...] - t_ref[rows, :]
            loss_ref[...] += 0.5 * jnp.sum(jnp.mean(err * err, axis=1, keepdims=True))
            dy = err * (1.0 / D_MODEL)
            gln_ref[0:1, :] += jnp.sum(dy * xhat, axis=0, keepdims=True)
            gln_ref[1:2, :] += jnp.sum(dy, axis=0, keepdims=True)
            dxh = dy * g
            dh = rstd * (dxh - jnp.mean(dxh, axis=1, keepdims=True)
                         - xhat * jnp.mean(dxh * xhat, axis=1, keepdims=True))
            gx_ref[rows, :] = ALPHA * dh
            dh16s.append(dh.astype(MXU_DTYPE))
        for k in range(halves):
            rows = slice(th * k, th * (k + 1))
            da_ref[rows, :] = _mm_nt(dh16s[k], w_ref[0:512, :])
            db_ref[rows, :] = _mm_nt(dh16s[k], w_ref[512:1024, :])
        dh16 = jnp.concatenate(dh16s, axis=0)
        gw_ref[0:512, :] += _mm_tn(ca_ref[...], dh16)
        gw_ref[512:1024, :] += _mm_tn(cb_ref[...], dh16)

    return pl.pallas_call(
        body, name="out_ln_loss", grid=(s // ts,),
        in_specs=[_rows(ts, 512), _rows(ts, 512), _full((D_MODEL, D_MODEL)), _rows(ts, D_MODEL), _rows(ts, D_MODEL),
                  _full((1, D_MODEL)), _full((1, D_MODEL))],
        out_specs=[_full((1, 128)), _rows(ts, D_MODEL), _rows(ts, 512), _rows(ts, 512),
                   _full((D_MODEL, D_MODEL)), _full((2, D_MODEL))],
        out_shape=[jax.ShapeDtypeStruct((1, 128), F32), jax.ShapeDtypeStruct((s, D_MODEL), F32),
                   jax.ShapeDtypeStruct((s, 512), F32), jax.ShapeDtypeStruct((s, 512), F32),
                   jax.ShapeDtypeStruct((D_MODEL, D_MODEL), F32), jax.ShapeDtypeStruct((2, D_MODEL), F32)],
        compiler_params=_cparams(dimension_semantics=("arbitrary",)),
    )(cat_a, cat_b, w_out, x, target, ln_g, ln_b)


def _swa_bwd(sinks, qa, k_pad, v_pad, attn, ga, d_cat_a, rope):
    s = qa.shape[0]
    sub = min(SWA_SUB, s // BLOCK)
    tq = sub * BLOCK
    nsteps = s // tq

    def body(sink_ref, qa_ref, ga_ref, at_ref, dc_ref, c_ref, s1_ref, s2_ref, k_ref, v_ref,
             dq_ref, dg_ref, dk_out, dv_out, ds_ref, dk_ref, dv_ref, bias_ref):
        n = pl.program_id(0)

        @pl.when(n == 0)
        def _():
            _swa_bias_fill(bias_ref)
            dk_ref[...] = jnp.zeros_like(dk_ref)
            dv_ref[...] = jnp.zeros_like(dv_ref)
            ds_ref[...] = jnp.zeros_like(ds_ref)

        low = lax.broadcasted_iota(jnp.int32, (2 * BLOCK, 128), 1) < SWA_HEAD_DIM
        for b in range(sub):
            rows = slice(BLOCK * b, BLOCK * (b + 1))
            start = pl.multiple_of((n * sub + b) * BLOCK, BLOCK)
            kw = k_ref[pl.ds(start, 2 * BLOCK), :]
            vw = v_ref[pl.ds(start, 2 * BLOCK), :]
            bias_t = _swa_bias_of(bias_ref, n, b)
            q = qa_ref[rows, :] * SWA_SCALE
            g = ga_ref[rows, :]
            sg = _sigmoid(g)
            o = at_ref[rows, :]
            dc = dc_ref[rows, :]
            do = dc * (g * sg)
            dg_ref[rows, :] = (dc * o * (sg * (1.0 + g * (1.0 - sg)))).astype(dg_ref.dtype)
            od = do * o
            c, s1, s2 = c_ref[rows, :], s1_ref[rows, :], s2_ref[rows, :]
            dk, dv = [], []
            for j in range(SWA_KV_HEADS):
                kd, vd = _swa_dup(kw, j), _swa_dup(vw, j)
                qs = _swa_stack(q, j).astype(MXU_DTYPE)
                dos = _swa_stack(do, j).astype(MXU_DTYPE)
                probs, psink = _swa_probs_t(qs, kd, bias_t, _swa_sink_row(sink_ref, j))
                delta = _row_sums_as_row(_swa_stack(od, j))
                dsc = (probs * (_mm_nt(vd, dos) - delta)).astype(MXU_DTYPE)
                dsink = psink * delta
                for r in range(SWA_GROUP):
                    h = SWA_GROUP * j + r
                    ds_ref[h:h + 1, :] += jnp.zeros((1, 128), F32) - jnp.sum(dsink[:, BLOCK * r:BLOCK * (r + 1)])
                dq = _swa_unstack(_mm_tn(dsc, kd))
                for i in range(2):
                    lanes = slice(128 * (2 * j + i), 128 * (2 * j + i + 1))
                    dq_ref[rows, lanes] = _rope_t(dq[i] * SWA_SCALE, c, s1, s2).astype(dq_ref.dtype)
                dkj = _mm(dsc, qs)
                dvj = _mm(probs.astype(MXU_DTYPE), dos)
                dk.append(dkj + pltpu.roll(dkj, SWA_HEAD_DIM, 1))
                dv.append(dvj + pltpu.roll(dvj, SWA_HEAD_DIM, 1))
            dk_ref[pl.ds(start, 2 * BLOCK), :] += jnp.where(low, dk[0], dk[1])
            dv_ref[pl.ds(start, 2 * BLOCK), :] += jnp.where(low, dv[0], dv[1])

        @pl.when(n == nsteps - 1)
        def _():
            dk_out[...] = dk_ref[BLOCK:, :]
            dv_out[...] = dv_ref[BLOCK:, :]

    return pl.pallas_call(
        body, name="swa_bwd", grid=(nsteps,),
        in_specs=[pl.BlockSpec(memory_space=pltpu.SMEM)] + [_rows(tq, 512)] * 4 + [_rows(tq, 128)] * 3
        + [_full((s + BLOCK, 128))] * 2,
        out_specs=[_rows(tq, 512), _rows(tq, 512), _full((s, 128)), _full((s, 128)), _full((SWA_Q_HEADS, 128))],
        out_shape=[jax.ShapeDtypeStruct((s, 512), MXU_DTYPE), jax.ShapeDtypeStruct((s, 512), MXU_DTYPE),
                   jax.ShapeDtypeStruct((s, 128), F32), jax.ShapeDtypeStruct((s, 128), F32),
                   jax.ShapeDtypeStruct((SWA_Q_HEADS, 128), F32)],
        scratch_shapes=[pltpu.VMEM((s + BLOCK, 128), F32)] * 2 + [pltpu.VMEM(SWA_BIAS_SHAPE, F32)],
        compiler_params=_cparams(dimension_semantics=("arbitrary",)),
    )(sinks, qa, ga, attn, d_cat_a, *rope, k_pad, v_pad)


def _gla_bwd(qb, kb, vb, la, oms, gb, o, sprev, d_cat_b, rb, wg, norm_w):
    s = qb.shape[0]
    tb = min(512, s)
    ch = tb // GLA_CHUNK
    nb = s // tb

    def body(qb_ref, kb_ref, vb_ref, la_ref, oms_ref, gb_ref, o_ref, sp_ref, dc_ref, rb_ref, wg_ref, nw_ref,
             dq_ref, dk_ref, dv_ref, dg_ref, dr_ref, gwg_ref, gbg_ref, gnw_ref, dst_ref):
        @pl.when(pl.program_id(0) == 0)
        def _():
            dst_ref[...] = jnp.zeros_like(dst_ref)
            gwg_ref[...] = jnp.zeros_like(gwg_ref)
            gbg_ref[...] = jnp.zeros_like(gbg_ref)
            gnw_ref[...] = jnp.zeros_like(gnw_ref)

        causal, causal_t = _gla_masks()
        nw = nw_ref[...]
        b = _chunk_cumsum(la_ref[...], True)
        bl = _chunk_last(b)
        eb, enb, ee, dec = jnp.exp(b), jnp.exp(-b), jnp.exp(bl - b), jnp.exp(bl)
        k = kb_ref[...]
        qd = (qb_ref[...] * GLA_SCALE) * eb
        ki = k * enb
        ke = k * ee
        qd16, ki16, ke16 = qd.astype(MXU_DTYPE), ki.astype(MXU_DTYPE), ke.astype(MXU_DTYPE)
        v16 = vb_ref[...].astype(MXU_DTYPE)

        g = gb_ref[...]
        sg = _sigmoid(g)
        silu = g * sg
        dsilu = sg * (1.0 + g * (1.0 - sg))
        gnw = jnp.zeros((1, GLA_DV), F32)
        do = []
        for h in range(GLA_HEADS):
            lv = slice(GLA_DV * h, GLA_DV * (h + 1))
            oh = o_ref[:, lv]
            dch = dc_ref[:, lv]
            r = lax.rsqrt(jnp.mean(oh * oh, axis=1, keepdims=True) + EPS)
            d_on = dch * silu[:, lv]
            dg_ref[:, lv] = (dch * (oh * r * nw) * dsilu[:, lv]).astype(dg_ref.dtype)
            gnw += jnp.sum(d_on * oh * r, axis=0, keepdims=True)
            u = d_on * nw
            do.append(r * u - oh * (r * r * r) * jnp.mean(u * oh, axis=1, keepdims=True))
        gnw_ref[...] += gnw
        do16 = jnp.concatenate(do, axis=1).astype(MXU_DTYPE)

        db, dbl = [None] * ch, [None] * ch
        for ci in reversed(range(ch)):
            rows = slice(GLA_CHUNK * ci, GLA_CHUNK * (ci + 1))
            qds, kis, kes = (_head_stack(t[rows], GLA_DK) for t in (qd16, ki16, ke16))
            vs, dos = _head_stack(v16[rows], GLA_DV), _head_stack(do16[rows], GLA_DV)
            a = jnp.where(causal, _mm_nt(qd16[rows], kis), 0.0).astype(MXU_DTYPE)
            at = jnp.where(causal_t, _mm_nt(ki16[rows], qds), 0.0).astype(MXU_DTYPE)
            da = jnp.where(causal, _mm_nt(do16[rows], vs), 0.0).astype(MXU_DTYPE)
            dat = jnp.where(causal_t, _mm_nt(v16[rows], dos), 0.0).astype(MXU_DTYPE)
            st = sp_ref[ci]
            dst = dst_ref[...]
            dst16 = dst.astype(MXU_DTYPE)
            dv = _mm(at, dos) + _rows_to_heads(_mm_nt(kes, dst16))
            dqd = _mm(da, kis) + _mm(do16[rows], _state_by_head(st.astype(MXU_DTYPE)))
            dki = _mm(dat, qds)
            dke = _mm(v16[rows], _state_by_head(dst16))
            ddec = jnp.sum(dst * st, axis=0, keepdims=True)
            decc = dec[rows][0:1]
            dst_ref[...] = _mm_tn(_heads_to_rows(do16[rows]), qds) + dst * decc
            dq_ref[rows, :] = (dqd * eb[rows] * GLA_SCALE).astype(dq_ref.dtype)
            dk_ref[rows, :] = (dki * enb[rows] + dke * ee[rows]).astype(dk_ref.dtype)
            dv_ref[rows, :] = dv.astype(dv_ref.dtype)
            dke_ke = dke * ke[rows]
            db[ci] = dqd * qd[rows] - dki * ki[rows] - dke_ke
            dbl[ci] = jnp.broadcast_to(jnp.sum(dke_ke, axis=0, keepdims=True) + ddec * decc, (GLA_CHUNK, GLA_KW))

        dla = _chunk_cumsum(jnp.concatenate(db, axis=0), False) + jnp.concatenate(dbl, axis=0)
        dlogit = dla * oms_ref[...] * (1.0 / GLA_TAU)
        dl16 = dlogit.astype(MXU_DTYPE)
        gbg_ref[...] += jnp.sum(dlogit, axis=0, keepdims=True)
        gwg = _mm_tn(rb_ref[...].astype(MXU_DTYPE), dl16)
        for j in range(N_DEV):
            gwg_ref[j] += gwg[:, 32 * j:32 * (j + 1)]
        dr_ref[...] = _mm_nt(dl16, wg_ref[...]).astype(dr_ref.dtype)

    def rev(width):
        return pl.BlockSpec((tb, width), lambda i: (nb - 1 - i, 0))

    return pl.pallas_call(
        body, name="gla_bwd", grid=(nb,),
        in_specs=[rev(256), rev(256), rev(512), rev(256), rev(256), rev(512), rev(512),
                  pl.BlockSpec((ch, GLA_DV, 256), lambda i: (nb - 1 - i, 0, 0)), rev(512), rev(GLA_RANK),
                  _full((GLA_RANK, 256)), _full((1, 128))],
        out_specs=[rev(256), rev(256), rev(512), rev(512), rev(GLA_RANK),
                   _full((N_DEV, GLA_RANK, 32)), _full((1, 256)), _full((1, 128))],
        out_shape=[jax.ShapeDtypeStruct((s, 256), MXU_DTYPE), jax.ShapeDtypeStruct((s, 256), MXU_DTYPE),
                   jax.ShapeDtypeStruct((s, 512), MXU_DTYPE), jax.ShapeDtypeStruct((s, 512), MXU_DTYPE),
                   jax.ShapeDtypeStruct((s, GLA_RANK), MXU_DTYPE), jax.ShapeDtypeStruct((N_DEV, GLA_RANK, 32), F32),
                   jax.ShapeDtypeStruct((1, 256), F32), jax.ShapeDtypeStruct((1, 128), F32)],
        scratch_shapes=[pltpu.VMEM((GLA_DV, GLA_KW), F32)],
        compiler_params=_cparams(dimension_semantics=("arbitrary",)),
    )(qb, kb, vb, la, oms, gb, o, sprev, d_cat_b, rb, wg, norm_w)


def _dproj_tiles(piece_refs, rope_refs, members=(0, 1, 3, 4, 5, 6, 7, 8)):
    for i in members:
        if i == 1:
            dk = _rope_t(piece_refs[1][...], *(r[...] for r in rope_refs))
            yield OFF[1], OFF[3], jnp.concatenate([dk, piece_refs[2][...]], axis=1).astype(MXU_DTYPE)
        else:
            yield OFF[i], OFF[i + 1], piece_refs[i][...].astype(MXU_DTYPE)


def _in_proj_bwd_x(gx0, pieces, w_in, rope):
    s = gx0.shape[0]
    ts = min(512, s)
    widths = [OFF[i + 1] - OFF[i] for i in range(9)]

    def body(gx0_ref, *refs):
        w_ref, gx_ref = refs[12:]
        acc = gx0_ref[...]
        for lo, hi, t16 in _dproj_tiles(refs[:9], refs[9:12]):
            acc += _mm(t16, w_ref[lo:hi, :])
        gx_ref[...] = acc

    return pl.pallas_call(
        body, name="in_proj_bwd_x", grid=(s // ts,),
        in_specs=[_rows(ts, D_MODEL)] + [_rows(ts, w) for w in widths] + [_rows(ts, 128)] * 3
        + [_full((D_IN_PROJ, D_MODEL))],
        out_specs=_rows(ts, D_MODEL),
        out_shape=jax.ShapeDtypeStruct((s, D_MODEL), F32),
        compiler_params=_cparams(dimension_semantics=("arbitrary",)),
    )(gx0, *pieces, *rope, w_in)


GW_PASSES = 2
GW_EVENT_STEPS = (0, 2, 4, 6)


def _in_proj_bwd_w(x, pieces, rope, parts_wg, g_ln, g_bg, g_nw, g_sinks, loss, parts_w_out):
    s = x.shape[0]
    ts = min(1024, s)
    nt = s // ts
    n_pass = GW_PASSES
    n_steps = n_pass * nt
    cw = D_MODEL // n_pass
    n_chips = N_DEV // 2
    blk = (D_IN_SHARD, cw)
    out_blk = parts_w_out.shape[1:]
    w_out_forward_step = min(1, n_steps - 1)
    w_out_finish_step = max(n_steps - 2, w_out_forward_step)

    def body(x_hbm, *refs):
        piece_refs, rope_refs = refs[:9], refs[9:12]
        (pwg_ref, gln_ref, gbg_ref, gnw_ref, gsk_ref, loss_ref, pout_ref,
         gin_ref, rwg_ref, rsm_ref, gout_ref) = refs[12:23]
        (acc_ref, stage_ref, sib_ref, snd_ref, rcv_ref, sm_ref, xs_ref,
         d2d_send, d2d_recv, ici_send, ici_recv, out_sems, sm_send, sm_recv, sm_loc, x_sems) = refs[23:39]
        w_out_sum = _OwnerSum(pout_ref, *refs[39:])
        p, t = pl.program_id(0), pl.program_id(1)
        step = p * nt + t
        tile = pl.ds(pl.multiple_of(t * ts, ts), ts)

        def x_load(k, q):
            rows = pl.ds(pl.multiple_of(k * ts, ts), ts)
            return pltpu.make_async_copy(x_hbm.at[rows, pl.ds(q * cw, cw)], xs_ref.at[q, rows, :],
                                         x_sems.at[q * nt + k])
        x_, y_, c = _mesh_pos()
        me, mychip, sibling = 4 * x_ + 2 * y_ + c, 2 * x_ + y_, (x_, y_, 1 - c)
        small_dsts = (rwg_ref, rsm_ref)

        def small_src(a, block):
            return pwg_ref.at[block] if a == 0 else sm_ref

        def small_copy(k, a, src_block, dst_block, peer):
            i = 2 * (k - 1) + a
            return pltpu.make_async_remote_copy(
                src_ref=small_src(a, src_block), dst_ref=small_dsts[a].at[dst_block], send_sem=sm_send.at[i],
                recv_sem=sm_recv.at[i], device_id=peer, device_id_type=pl.DeviceIdType.MESH)

        def small_local(a):
            return pltpu.make_async_copy(small_src(a, me), small_dsts[a].at[me], sm_loc.at[a])

        @pl.when(step == 0)
        def _():
            for q in range(n_pass):
                for k in range(nt):
                    x_load(k, q).start()
            w_out_sum.start()
            acc_ref[...] = jnp.zeros_like(acc_ref)
            sm_ref[...] = jnp.zeros_like(sm_ref)
            for r in range(D_MODEL // 128):
                sm_ref[r:r + 1, :] = gln_ref[0:1, 128 * r:128 * (r + 1)]
                sm_ref[8 + r:9 + r, :] = gln_ref[1:2, 128 * r:128 * (r + 1)]
            for r in range(2):
                sm_ref[16 + r:17 + r, :] = gbg_ref[0:1, 128 * r:128 * (r + 1)]
            sm_ref[24:25, :] = gnw_ref[...]
            diag = (lax.broadcasted_iota(jnp.int32, gsk_ref.shape, 0)
                    == lax.broadcasted_iota(jnp.int32, gsk_ref.shape, 1))
            sm_ref[32:33, :] = jnp.sum(jnp.where(diag, gsk_ref[...], 0.0), axis=0, keepdims=True)
            sm_ref[40:41, :] = loss_ref[...]
            for a in range(2):
                small_local(a).start()
            for k in range(1, N_DEV):
                peer, pidx = _peer(k, x_, y_, c)
                for a in range(2):
                    small_copy(k, a, pidx, me, peer).start()

        for q in range(n_pass):
            @pl.when(p == q)
            def _(q=q):
                x_load(t, q).wait()

        xb = xs_ref[p, tile, :]
        for lo, hi, t16 in _dproj_tiles(piece_refs, rope_refs):
            acc_ref[p, lo:hi, :] += _mm_tn(t16, xb)

        @pl.when(step == w_out_forward_step)
        def _():
            w_out_sum.forward()

        @pl.when(step == w_out_finish_step)
        def _():
            gout_ref[...] = w_out_sum.finish()

        def block_rows(q, j):
            return acc_ref[q, D_IN_SHARD * j:D_IN_SHARD * (j + 1), :]

        def d2d(q):
            return pltpu.make_async_remote_copy(
                src_ref=stage_ref, dst_ref=sib_ref.at[q], send_sem=d2d_send.at[q],
                recv_sem=d2d_recv.at[q], device_id=sibling, device_id_type=pl.DeviceIdType.MESH)

        def ici(q, slot, owner):
            i = 3 * q + slot
            return pltpu.make_async_remote_copy(
                src_ref=snd_ref.at[q, slot], dst_ref=rcv_ref.at[q, slot], send_sem=ici_send.at[i],
                recv_sem=ici_recv.at[i], device_id=owner, device_id_type=pl.DeviceIdType.MESH)

        def out_copy(q):
            return pltpu.make_async_copy(sib_ref.at[q, mychip], gin_ref.at[:, pl.ds(q * cw, cw)], out_sems.at[q])

        first = (jnp.where(c == 0, 1 - x_, x_), jnp.where(c == 0, y_, 1 - y_))
        second = (jnp.where(c == 0, x_, 1 - x_), jnp.where(c == 0, 1 - y_, y_))

        def chip_of(pos):
            return 2 * pos[0] + pos[1]

        def to_sibling(q):
            if q >= 1:
                d2d(q - 1).wait_send()
            for cc in range(2):
                @pl.when(c == cc)
                def _(cc=cc):
                    for k in range(n_chips):
                        stage_ref[k] = block_rows(q, 2 * k + 1 - cc)
            d2d(q).start()

        def chip_sums_leave(q):
            d2d(q).wait_recv()
            for cc in range(2):
                @pl.when(c == cc)
                def _(cc=cc):
                    for k in range(n_chips):
                        sib_ref[q, k] = block_rows(q, 2 * k + cc) + sib_ref[q, k]
            snd_ref[q, 2] = sib_ref[q, chip_of((1 - x_, 1 - y_))].astype(snd_ref.dtype)
            ici(q, 2, (*first, c)).start()
            snd_ref[q, 0] = sib_ref[q, chip_of(first)].astype(snd_ref.dtype)
            ici(q, 0, (*first, c)).start()

        def combined_sum_leaves(q):
            ici(q, 2, sibling).wait_recv()
            snd_ref[q, 1] = (sib_ref[q, chip_of(second)] + rcv_ref[q, 2].astype(F32)).astype(snd_ref.dtype)
            ici(q, 1, (*second, c)).start()

        def owner_total(q):
            total = sib_ref[q, mychip]
            for slot in range(2):
                ici(q, slot, sibling).wait_recv()
                total = total + rcv_ref[q, slot].astype(F32)
            sib_ref[q, mychip] = total
            out_copy(q).start()

        stages = (to_sibling, chip_sums_leave, combined_sum_leaves, owner_total)
        events = sorted((min((q + 1) * nt - 1 + GW_EVENT_STEPS[i], n_steps - 1), i > 0, 2 * q + i, -q, q, i)
                        for q in range(n_pass) for i in range(len(stages)))
        for at_step, _, _, _, q, i in events:
            @pl.when(step == at_step)
            def _(q=q, i=i):
                stages[i](q)

        @pl.when(step == n_steps - 1)
        def _():
            for k in range(1, N_DEV):
                peer, pidx = _peer(k, x_, y_, c)
                for a in range(2):
                    small_copy(k, a, me, pidx, peer).wait_recv()
            for k in range(1, N_DEV):
                peer, pidx = _peer(k, x_, y_, c)
                for a in range(2):
                    small_copy(k, a, pidx, me, peer).wait_send()
            for a in range(2):
                small_local(a).wait()
            d2d(n_pass - 1).wait_send()
            for q in range(n_pass):
                for slot in range(3):
                    ici(q, slot, sibling).wait_send()
            for q in range(n_pass):
                out_copy(q).wait()

    widths = [OFF[i + 1] - OFF[i] for i in range(9)]
    hbm = pl.BlockSpec(memory_space=pl.ANY)
    vmem = pl.BlockSpec(memory_space=pltpu.VMEM)

    def token_tile(width):
        return pl.BlockSpec((ts, width), lambda p, t: (t, 0))

    return pl.pallas_call(
        body, name="in_proj_bwd_w", grid=(n_pass, nt),
        in_specs=[hbm] + [token_tile(w) for w in widths] + [token_tile(128)] * 3 + [hbm] + [vmem] * 5 + [hbm],
        out_specs=[hbm, hbm, hbm, vmem],
        out_shape=[jax.ShapeDtypeStruct((D_IN_SHARD, D_MODEL), F32),
                   jax.ShapeDtypeStruct((N_DEV,) + parts_wg.shape[1:], F32),
                   jax.ShapeDtypeStruct((N_DEV, SMALL_ROWS, 128), F32), jax.ShapeDtypeStruct(out_blk, F32)],
        scratch_shapes=[pltpu.VMEM((n_pass, D_IN_PROJ, cw), F32), pltpu.VMEM((n_chips,) + blk, F32),
                        pltpu.VMEM((n_pass, n_chips) + blk, F32), pltpu.VMEM((n_pass, 3) + blk, MXU_DTYPE),
                        pltpu.VMEM((n_pass, 3) + blk, MXU_DTYPE),
                        pltpu.VMEM((SMALL_ROWS, 128), F32), pltpu.VMEM((n_pass, s, cw), MXU_DTYPE),
                        pltpu.SemaphoreType.DMA((n_pass,)), pltpu.SemaphoreType.DMA((n_pass,)),
                        pltpu.SemaphoreType.DMA((3 * n_pass,)), pltpu.SemaphoreType.DMA((3 * n_pass,)),
                        pltpu.SemaphoreType.DMA((n_pass,)),
                        pltpu.SemaphoreType.DMA((2 * (N_DEV - 1),)), pltpu.SemaphoreType.DMA((2 * (N_DEV - 1),)),
                        pltpu.SemaphoreType.DMA((2,)), pltpu.SemaphoreType.DMA((n_pass * nt,))]
        + _OwnerSum.scratch(out_blk),
        compiler_params=_cparams(dimension_semantics=("arbitrary", "arbitrary")),
    )(x, *pieces, *rope, parts_wg, g_ln, g_bg, g_nw, g_sinks, loss, parts_w_out)


def _local_step(x, positions, w_in_t, wg_s, b_gate, sinks, norm_w, w_out_s, ln_g, ln_b, target):
    qa, k_pad, v_pad, ga, qb, kb, vb, gb, rb, la, oms, *rope, x16, w_in, wg, w_out = _in_proj(
        x, w_in_t, wg_s, b_gate, _rope_angles(positions), w_out_s)
    attn, cat_a = _swa_fwd(sinks, qa, k_pad, v_pad, ga)
    o, cat_b, sprev = _gla_fwd(qb, kb, vb, la, gb, norm_w)
    loss, gx0, d_cat_a, d_cat_b, g_w_out, g_ln = _out_ln_loss(cat_a, cat_b, w_out, x, target, ln_g, ln_b)
    parts_w_out = g_w_out.reshape(N_DEV, D_OUT_SHARD, D_MODEL)
    dqa, dga, dka, dva, g_sinks = _swa_bwd(sinks, qa, k_pad, v_pad, attn, ga, d_cat_a, rope)
    dqb, dkb, dvb, dgb, drb, parts_wg, g_bg, g_nw = _gla_bwd(
        qb, kb, vb, la, oms, gb, o, sprev, d_cat_b, rb, wg, norm_w)
    pieces = (dqa, dka, dva, dga, dqb, dkb, dvb, dgb, drb)
    grad_x = _in_proj_bwd_x(gx0, pieces, w_in, rope)
    g_in, r_wg, r_small, g_out = _in_proj_bwd_w(
        x16, pieces, rope, parts_wg, g_ln, g_bg, g_nw, g_sinks, loss, parts_w_out)
    return grad_x, g_in, g_out, r_wg, r_small


def _mesh_pos():
    return lax.axis_index("x"), lax.axis_index("y"), lax.axis_index("c")


def _peer(k, x, y, c):
    px = (1 - x) if k & 4 else x
    py = (1 - y) if k & 2 else y
    pc = (1 - c) if k & 1 else c
    return (px, py, pc), 4 * px + 2 * py + pc


def _other_chips(x, y):
    return [(1 - x, y), (x, 1 - y), (1 - x, 1 - y)]


def _shard_view(t):
    return jnp.transpose(t, (2, 0, 1))


class _BlockGather:
    def __init__(self, slots, send_sems, recv_sems):
        self.slots, self.send_sems, self.recv_sems = slots, send_sems, recv_sems
        x, y, c = _mesh_pos()
        self.c, self.me, self.sibling = c, 4 * x + 2 * y + c, (x, y, 1 - c)
        first = (jnp.where(c == 0, 1 - x, x), jnp.where(c == 0, y, 1 - y))
        second = (jnp.where(c == 0, x, 1 - x), jnp.where(c == 0, 1 - y, y))
        self.chips = [first, second, (1 - x, 1 - y)]

    @staticmethod
    def scratch():
        return [pltpu.SemaphoreType.DMA((N_DEV - 1,)), pltpu.SemaphoreType.DMA((N_DEV - 1,))]

    def _copy(self, k, block, to):
        return pltpu.make_async_remote_copy(
            src_ref=self.slots.at[block], dst_ref=self.slots.at[block], send_sem=self.send_sems.at[k],
            recv_sem=self.recv_sems.at[k], device_id=to, device_id_type=pl.DeviceIdType.MESH)

    def _block(self, j, c):
        cx, cy = self.chips[j]
        return 4 * cx + 2 * cy + c

    def _dev(self, j):
        return (*self.chips[j], self.c)

    def start(self):
        self._copy(1, self.me, self._dev(0)).start()
        self._copy(2, self.me, self._dev(1)).start()
        self._copy(0, self.me, self.sibling).start()

    def forward(self):
        self._copy(1, self._block(0, self.c), self.sibling).wait_recv()
        self._copy(3, self._block(0, self.c), self._dev(1)).start()
        self._copy(4, self._block(0, self.c), self.sibling).start()
        self._copy(2, self._block(1, self.c), self.sibling).wait_recv()
        self._copy(5, self._block(1, self.c), self.sibling).start()

    def forward_far(self):
        self._copy(3, self._block(2, self.c), self.sibling).wait_recv()
        self._copy(6, self._block(2, self.c), self.sibling).start()

    def finish(self):
        for k in (0, 4, 5, 6):
            self._copy(k, self.me, self.sibling).wait_recv()
        for k in range(N_DEV - 1):
            self._copy(k, self.me, self.sibling).wait_send()


class _OwnerSum:
    def __init__(self, parts, own, sib, snd, rcv, loc_sems, d2d_send, d2d_recv, ici_send, ici_recv):
        self.parts, self.own, self.sib, self.snd, self.rcv = parts, own, sib, snd, rcv
        self.sems = (loc_sems, d2d_send, d2d_recv, ici_send, ici_recv)
        x, y, c = _mesh_pos()
        self.c, self.sibling = c, (x, y, 1 - c)
        self.chips = [(x, y)] + _other_chips(x, y)

    @staticmethod
    def scratch(block):
        return [pltpu.VMEM((4,) + block, F32), pltpu.VMEM((4,) + block, F32),
                pltpu.VMEM((3,) + block, MXU_DTYPE), pltpu.VMEM((3,) + block, MXU_DTYPE),
                pltpu.SemaphoreType.DMA((4,)), pltpu.SemaphoreType.DMA((4,)), pltpu.SemaphoreType.DMA((4,)),
                pltpu.SemaphoreType.DMA((3,)), pltpu.SemaphoreType.DMA((3,))]

    def _local(self, r):
        cx, cy = self.chips[r]
        return pltpu.make_async_copy(self.parts.at[4 * cx + 2 * cy + self.c], self.own.at[r], self.sems[0].at[r])

    def _d2d(self, r):
        cx, cy = self.chips[r]
        return pltpu.make_async_remote_copy(
            src_ref=self.parts.at[4 * cx + 2 * cy + (1 - self.c)], dst_ref=self.sib.at[r], send_sem=self.sems[1].at[r],
            recv_sem=self.sems[2].at[r], device_id=self.sibling, device_id_type=pl.DeviceIdType.MESH)

    def _ici(self, r):
        cx, cy = self.chips[r]
        return pltpu.make_async_remote_copy(
            src_ref=self.snd.at[r - 1], dst_ref=self.rcv.at[r - 1], send_sem=self.sems[3].at[r - 1],
            recv_sem=self.sems[4].at[r - 1], device_id=(cx, cy, self.c), device_id_type=pl.DeviceIdType.MESH)

    def start(self):
        for r in (1, 2, 3, 0):
            self._local(r).start()
            self._d2d(r).start()

    def forward(self):
        for r in (1, 2, 3):
            self._local(r).wait()
            self._d2d(r).wait_recv()
            self.snd[r - 1] = (self.own[r] + self.sib[r]).astype(self.snd.dtype)
            self._ici(r).start()

    def finish(self):
        self._local(0).wait()
        self._d2d(0).wait_recv()
        acc = self.own[0] + self.sib[0]
        for r in (1, 2, 3):
            self._ici(r).wait_recv()
            acc = acc + self.rcv[r - 1].astype(F32)
        for r in range(4):
            self._d2d(r).wait_send()
        for r in (1, 2, 3):
            self._ici(r).wait_send()
        return acc


SMALL_ROWS = 48


def _adamw_math(g, w, m, v):
    nm = ADAM_B1 * m + (1.0 - ADAM_B1) * g
    nv = ADAM_B2 * v + (1.0 - ADAM_B2) * (g * g)
    m_hat = nm / (1.0 - ADAM_B1 ** ADAM_STEP)
    v_hat = nv / (1.0 - ADAM_B2 ** ADAM_STEP)
    return -ADAM_LR * (m_hat / (jnp.sqrt(v_hat) + ADAM_EPS) + ADAM_WD * w), nm, nv


def _adamw(g_in, w_in_wmv, r_small, r_wg, g_out, params):
    n_par = len(params)
    rows, width = g_in.shape
    parts = 3
    pr = rows // parts
    assert pr * parts == rows

    def body(gin_ref, w_hbm, m_hbm, v_hbm, rsm_ref, rwg_ref, gout_ref, *refs):
        ins, outs = refs[:3 * n_par], refs[3 * n_par:7 * n_par + 1]
        big_outs = refs[7 * n_par + 1:7 * n_par + 5]
        bufs, stage, sems = refs[7 * n_par + 5:]

        def load(k, i, src):
            return pltpu.make_async_copy(src.at[pl.ds(pr * k, pr), 0, :], bufs.at[k, i], sems.at[7 * k + i])

        def store(k, i, dst):
            return pltpu.make_async_copy(stage.at[k, i], dst.at[pl.ds(pr * k, pr), 0, :], sems.at[7 * k + 3 + i])

        for k in range(parts):
            for i, src in enumerate((w_hbm, m_hbm, v_hbm)):
                load(k, i, src).start()

        g = rsm_ref[0]
        gwg = rwg_ref[0]
        for j in range(1, N_DEV):
            g = g + rsm_ref[j]
            gwg = gwg + rwg_ref[j]
        outs[4 * n_par][...] = g[40:41]
        grads = [gwg,
                 jnp.concatenate([g[r:r + 1] for r in range(0, 8)], axis=1),
                 jnp.concatenate([g[r:r + 1] for r in range(8, 16)], axis=1),
                 jnp.concatenate([g[16:17], g[17:18]], axis=1),
                 g[24:25],
                 g[32:33, 0:SWA_Q_HEADS],
                 gout_ref[...]]
        for p, gp in enumerate(grads):
            w_ref, m_ref, v_ref = ins[3 * p:3 * p + 3]
            outs[4 * p][...] = gp
            outs[4 * p + 1][...], outs[4 * p + 2][...], outs[4 * p + 3][...] = _adamw_math(
                gp, w_ref[...], m_ref[...], v_ref[...])

        for k in range(parts):
            for i, src in enumerate((w_hbm, m_hbm, v_hbm)):
                load(k, i, src).wait()
            gk = gin_ref[pr * k:pr * (k + 1), :]
            stage[k, 0] = gk
            stage[k, 1], stage[k, 2], stage[k, 3] = _adamw_math(gk, bufs[k, 0], bufs[k, 1], bufs[k, 2])
            for i, dst in enumerate(big_outs):
                store(k, i, dst).start()
        for k in range(parts):
            for i, dst in enumerate(big_outs):
                store(k, i, dst).wait()

    vmem = pl.BlockSpec(memory_space=pltpu.VMEM)
    hbm = pl.BlockSpec(memory_space=pl.ANY)
    flat = [t for wmv in params for t in wmv]
    res = pl.pallas_call(
        body, name="adamw",
        in_specs=[vmem, hbm, hbm, hbm] + [vmem] * (3 + len(flat)),
        out_specs=[vmem] * (4 * n_par + 1) + [hbm] * 4,
        out_shape=[jax.ShapeDtypeStruct(wmv[0].shape, F32) for wmv in params for _ in range(4)]
        + [jax.ShapeDtypeStruct((1, 128), F32)] + [jax.ShapeDtypeStruct((rows, 1, width), F32)] * 4,
        scratch_shapes=[pltpu.VMEM((parts, 3, pr, width), F32), pltpu.VMEM((parts, 4, pr, width), F32),
                        pltpu.SemaphoreType.DMA((7 * parts,))],
        compiler_params=_cparams(),
    )(g_in, *w_in_wmv, r_small, r_wg, g_out, *flat)
    return res[:4 * n_par + 1], res[4 * n_par + 1:]


def kernel(x, positions, w_in, gla_w_gate_up, gla_b_gate, attn_sinks, gla_norm_w, w_out, ln_g, ln_b, loss_target, m_w_in, m_gla_w_gate_up, m_gla_b_gate, m_attn_sinks, m_gla_norm_w, m_w_out, m_ln_g, m_ln_b, v_w_in, v_gla_w_gate_up, v_gla_b_gate, v_attn_sinks, v_gla_norm_w, v_w_out, v_ln_g, v_ln_b):
    grad_x, g_in, g_out, r_wg, r_small = _local_step(
        x[0], positions[0], _shard_view(w_in), gla_w_gate_up[0], gla_b_gate, attn_sinks[0], gla_norm_w, w_out[0],
        ln_g, ln_b, loss_target[0])

    vec, upd_in = _adamw(g_in, (_shard_view(w_in), _shard_view(m_w_in), _shard_view(v_w_in)), r_small, r_wg, g_out, [
        (gla_w_gate_up[0], m_gla_w_gate_up[0], v_gla_w_gate_up[0]), (ln_g, m_ln_g, v_ln_g), (ln_b, m_ln_b, v_ln_b),
        (gla_b_gate, m_gla_b_gate, v_gla_b_gate), (gla_norm_w, m_gla_norm_w, v_gla_norm_w),
        (attn_sinks, m_attn_sinks, v_attn_sinks), (w_out[0], m_w_out[0], v_w_out[0])])
    upd_in = [jnp.transpose(t, (1, 2, 0)) for t in upd_in]

    outs = [vec[28][0, 0], grad_x[None]]
    for kind in range(4):
        u_wg, u_ln_g, u_ln_b, u_bg, u_nw, u_sinks, u_out = (vec[4 * p + kind] for p in range(7))
        outs += [upd_in[kind], u_wg[None], u_bg, u_sinks, u_nw, u_out[None], u_ln_g, u_ln_b]
    return tuple(outs)
```

```python
import jax
import jax.numpy as jnp
from jax import lax
from jax.experimental import pallas as pl
from jax.experimental.pallas import tpu as pltpu

F32 = jnp.float32
MXU_DTYPE = jnp.bfloat16

N_DEV = 8
D_MODEL = 1024
SWA_Q_HEADS = 8
SWA_KV_HEADS = 2
SWA_GROUP = 4
SWA_HEAD_DIM = 64
BLOCK = 128
ROPE_THETA = 500000.0
ROT_DIM = 16
GLA_HEADS = 4
GLA_DK = 64
GLA_DV = 128
GLA_RANK = 16
GLA_TAU = 16.0
GLA_CHUNK = 64
D_IN_PROJ = 2832
D_IN_SHARD = D_IN_PROJ // N_DEV
D_OUT_SHARD = D_MODEL // N_DEV
OFF = (0, 512, 640, 768, 1280, 1536, 1792, 2304, 2816, 2832)
EPS = 1e-5
ALPHA = 2.0 ** 0.25
SWA_SCALE = SWA_HEAD_DIM ** -0.5
GLA_SCALE = GLA_DK ** -0.5
ADAM_LR = 0.001
ADAM_B1 = 0.9
ADAM_B2 = 0.999
ADAM_EPS = 1e-08
ADAM_WD = 0.01
ADAM_STEP = 10
VMEM_LIMIT = 56 * 1024 * 1024

_NT = (((1,), (1,)), ((), ()))
_TN = (((0,), (0,)), ((), ()))


def _mm(a, b):
    return jnp.dot(a, b, preferred_element_type=F32)


def _mm_nt(a, b):
    return lax.dot_general(a, b, _NT, preferred_element_type=F32)


def _mm_tn(a, b):
    return lax.dot_general(a, b, _TN, preferred_element_type=F32)


def _sigmoid(t):
    return 1.0 / (1.0 + jnp.exp(-t))


def _cparams(**kw):
    return pltpu.CompilerParams(vmem_limit_bytes=VMEM_LIMIT, **kw)


def _full(shape):
    return pl.BlockSpec(shape, lambda *_: (0,) * len(shape))


def _rows(tile, width):
    return pl.BlockSpec((tile, width), lambda i: (i, 0))


def _rope_angles(positions):
    half = ROT_DIM // 2
    inv_freq = ROPE_THETA ** (-jnp.arange(half, dtype=F32) / half)
    ang = positions.astype(F32)[None, :] * inv_freq[:, None]
    return jnp.concatenate([jnp.cos(ang), jnp.sin(ang)], axis=0)


def _split3_parts(t):
    hi = t.astype(MXU_DTYPE)
    r1 = t - hi.astype(F32)
    mid = r1.astype(MXU_DTYPE)
    return hi, mid, (r1 - mid.astype(F32)).astype(MXU_DTYPE)


def _rope_tables(cs):
    half = ROT_DIM // 2
    i = lax.broadcasted_iota(jnp.int32, (2 * half, 3 * 128), 0)
    lane = lax.broadcasted_iota(jnp.int32, (2 * half, 3 * 128), 1)
    table, pos = _idiv(lane, 128), lane & (SWA_HEAD_DIM - 1)
    is_c = (table == 0) & (pos < ROT_DIM) & ((pos & (half - 1)) == i)
    is_s1 = (table == 1) & (pos < half) & (pos + half == i)
    is_s2 = (table == 2) & (pos >= half) & (pos < ROT_DIM) & (pos == i)
    sel = jnp.where(is_c | is_s2, 1.0, jnp.where(is_s1, -1.0, 0.0)).astype(MXU_DTYPE)
    hi, mid, lo = _split3_parts(cs)
    t = (_mm_tn(hi, sel) + _mm_tn(mid, sel)) + _mm_tn(lo, sel)
    pos1 = lax.broadcasted_iota(jnp.int32, (1, 128), 1) & (SWA_HEAD_DIM - 1)
    return t[:, 0:128] + jnp.where(pos1 >= ROT_DIM, 1.0, 0.0), t[:, 128:256], t[:, 256:384]


def _rope(t, c, s1, s2):
    return t * c + pltpu.roll(t, 120, 1) * s1 + pltpu.roll(t, 8, 1) * s2


def _rope_t(g, c, s1, s2):
    return g * c + pltpu.roll(g * s1, 8, 1) + pltpu.roll(g * s2, 120, 1)


def _in_proj(x, w_in_t, wg_s, b_gate, cos_sin, w_out_s):
    s = x.shape[0]
    ts = min(512, s)
    nsteps = s // ts
    forward_step, far_step = min(3, nsteps - 1), min(5, nsteps - 1)
    widths = [OFF[i + 1] - OFF[i] for i in range(9)]

    def body(x_ref, win_hbm, wgs_ref, bg_ref, cs_ref, wos_ref,
             qa_ref, ka_ref, va_ref, ga_ref, qb_ref, kb_ref, vb_ref, gb_ref, rb_ref, la_ref, oms_ref,
             c_ref, s1_ref, s2_ref, x16_ref, w_ref, wg_ref, wout_ref,
             win_all, wg_all, wout_all, stage, stage_sem, *sems):
        xb = x_ref[...].astype(MXU_DTYPE)
        x16_ref[...] = xb
        c, s1, s2 = _rope_tables(cs_ref[...])
        c_ref[...], s1_ref[...], s2_ref[...] = c, s1, s2
        i0 = pl.program_id(0)
        gather = _BlockGather(wout_all, *sems[0:2])

        @pl.when(i0 == 0)
        def _():
            ka_ref[0:BLOCK, :] = jnp.zeros((BLOCK, 128), ka_ref.dtype)
            va_ref[0:BLOCK, :] = jnp.zeros((BLOCK, 128), va_ref.dtype)
            first = (_BlockGather(win_all, *sems[2:4]), _BlockGather(wg_all, *sems[4:6]))
            load = pltpu.make_async_copy(win_hbm.at[:, 0, :], stage, stage_sem)
            load.start()
            wout_all[gather.me] = wos_ref[...].astype(wout_all.dtype)
            wg_all[gather.me] = wgs_ref[...].astype(wg_all.dtype)
            load.wait()
            win_all[gather.me] = stage[...].astype(win_all.dtype)
            for stage_of in ("start", "forward", "forward_far", "finish"):
                for g in first:
                    getattr(g, stage_of)()
            gather.start()
            for j in range(N_DEV):
                w_ref[D_IN_SHARD * j:D_IN_SHARD * (j + 1), :] = win_all[j]
                wg_ref[:, 32 * j:32 * (j + 1)] = wg_all[j]

        @pl.when(i0 == forward_step)
        def _():
            gather.forward()

        @pl.when(i0 == far_step)
        def _():
            gather.forward_far()

        @pl.when(i0 == nsteps - 1)
        def _():
            gather.finish()
            for j in range(N_DEV):
                wout_ref[D_OUT_SHARD * j:D_OUT_SHARD * (j + 1), :] = wout_all[j]

        kv_rows = pl.ds(pl.multiple_of(BLOCK + i0 * ts, BLOCK), ts)

        def cols(i):
            return _mm_nt(xb, w_ref[OFF[i]:OFF[i + 1], :])

        qa = cols(0)
        for i in range(4):
            qa_ref[:, 128 * i:128 * (i + 1)] = _rope(qa[:, 128 * i:128 * (i + 1)], c, s1, s2).astype(qa_ref.dtype)
        kv = _mm_nt(xb, w_ref[OFF[1]:OFF[3], :])
        ka_ref[kv_rows, :] = _rope(kv[:, 0:128], c, s1, s2).astype(ka_ref.dtype)
        va_ref[kv_rows, :] = kv[:, 128:256].astype(va_ref.dtype)
        ga_ref[...] = cols(3)
        qb_ref[...] = cols(4)
        kb_ref[...] = cols(5)
        vb_ref[...] = cols(6).astype(vb_ref.dtype)
        gb_ref[...] = cols(7)
        rb = cols(8)
        rb_ref[...] = rb
        logit = _mm(rb.astype(MXU_DTYPE), wg_ref[...]) + bg_ref[...]
        e = jnp.exp(-jnp.abs(logit))
        la_ref[...] = (jnp.minimum(logit, 0.0) - jnp.log(1.0 + e)) / GLA_TAU
        oms_ref[...] = jnp.where(logit >= 0.0, e, 1.0) / (1.0 + e)

    out_shape = [jax.ShapeDtypeStruct((s + BLOCK if i in (1, 2) else s, w), MXU_DTYPE if i in (0, 1, 2, 6) else F32)
                 for i, w in enumerate(widths)]
    out_shape += [jax.ShapeDtypeStruct((s, 256), F32)] * 2 + [jax.ShapeDtypeStruct((s, 128), F32)] * 3
    out_shape += [jax.ShapeDtypeStruct((s, D_MODEL), MXU_DTYPE)]
    out_shape += [jax.ShapeDtypeStruct((D_IN_PROJ, D_MODEL), MXU_DTYPE), jax.ShapeDtypeStruct((GLA_RANK, 256), MXU_DTYPE),
                  jax.ShapeDtypeStruct((D_MODEL, D_MODEL), MXU_DTYPE)]
    return pl.pallas_call(
        body, name="in_proj", grid=(nsteps,),
        in_specs=[_rows(ts, D_MODEL), pl.BlockSpec(memory_space=pl.ANY), _full((GLA_RANK, 32)), _full((1, 256)),
                  pl.BlockSpec((ROT_DIM, ts), lambda i: (0, i)), _full((D_OUT_SHARD, D_MODEL))],
        out_specs=[_full((s + BLOCK, w)) if i in (1, 2) else _rows(ts, w) for i, w in enumerate(widths)]
        + [_rows(ts, 256)] * 2 + [_rows(ts, 128)] * 3 + [_rows(ts, D_MODEL)]
        + [_full((D_IN_PROJ, D_MODEL)), _full((GLA_RANK, 256)), _full((D_MODEL, D_MODEL))],
        out_shape=out_shape,
        scratch_shapes=[pltpu.VMEM((N_DEV, D_IN_SHARD, D_MODEL), MXU_DTYPE), pltpu.VMEM((N_DEV, GLA_RANK, 32), MXU_DTYPE),
                        pltpu.VMEM((N_DEV, D_OUT_SHARD, D_MODEL), MXU_DTYPE),
                        pltpu.VMEM((D_IN_SHARD, D_MODEL), F32), pltpu.SemaphoreType.DMA]
        + 3 * _BlockGather.scratch(),
        compiler_params=_cparams(dimension_semantics=("arbitrary",)),
    )(x, w_in_t, wg_s, b_gate, cos_sin, w_out_s)


SWA_ROWS = SWA_GROUP * BLOCK


SWA_BIAS_SHAPE = (2, 2 * BLOCK, SWA_ROWS)


def _swa_bias_fill(bias_ref):
    shape = SWA_BIAS_SHAPE[1:]
    ki = lax.broadcasted_iota(jnp.int32, shape, 0)
    qi = lax.broadcasted_iota(jnp.int32, shape, 1) & (BLOCK - 1)
    dist = qi + BLOCK - ki
    ok = (dist >= 0) & (dist < BLOCK)
    bias_ref[0] = jnp.where(ok & (ki >= BLOCK), 0.0, -jnp.inf).astype(F32)
    bias_ref[1] = jnp.where(ok, 0.0, -jnp.inf).astype(F32)


SWA_SUB = 8


def _swa_bias_of(bias_ref, n, b):
    return bias_ref[jnp.minimum(n, 1)] if b == 0 else bias_ref[1]


def _swa_dup(t, j):
    t = t.astype(F32)
    low = lax.broadcasted_iota(jnp.int32, t.shape, 1) < SWA_HEAD_DIM
    keep = low if j == 0 else jnp.logical_not(low)
    return jnp.where(keep, t, pltpu.roll(t, SWA_HEAD_DIM, 1)).astype(MXU_DTYPE)


def _swa_stack(t, j):
    low = lax.broadcasted_iota(jnp.int32, (BLOCK, 128), 1) < SWA_HEAD_DIM
    zero = jnp.zeros((BLOCK, 128), t.dtype)
    blocks = []
    for p in (2 * j, 2 * j + 1):
        tp = t[:, 128 * p:128 * (p + 1)]
        blocks += [jnp.where(low, tp, zero), jnp.where(low, zero, tp)]
    return jnp.concatenate(blocks, axis=0)


def _swa_unstack(t):
    low = lax.broadcasted_iota(jnp.int32, (BLOCK, 128), 1) < SWA_HEAD_DIM
    return [jnp.where(low, t[2 * BLOCK * i:2 * BLOCK * i + BLOCK], t[2 * BLOCK * i + BLOCK:2 * BLOCK * (i + 1)])
            for i in range(2)]


def _swa_sink_row(sink_ref, j):
    lane = lax.broadcasted_iota(jnp.int32, (1, SWA_ROWS), 1)
    row = jnp.full((1, SWA_ROWS), sink_ref[SWA_GROUP * j], F32)
    for r in range(1, SWA_GROUP):
        row = jnp.where(lane >= BLOCK * r, sink_ref[SWA_GROUP * j + r], row)
    return row


def _split3(t):
    return jnp.concatenate(_split3_parts(t), axis=1)


def _row_sums_as_row(t):
    ones = jnp.ones((8, 3 * t.shape[1]), MXU_DTYPE)
    return _mm_nt(ones, _split3(t))[0:1, :]


def _swa_probs_t(qs, kd, bias_t, sink):
    sc = _mm_nt(kd, qs) + bias_t
    m = jnp.maximum(jnp.max(sc, axis=0, keepdims=True), sink)
    p = jnp.exp(sc - m)
    ps = jnp.exp(sink - m)
    rinv = 1.0 / (jnp.sum(p, axis=0, keepdims=True) + ps)
    return p * rinv, ps * rinv


def _swa_fwd(sinks, qa, k_pad, v_pad, ga):
    s = qa.shape[0]
    sub = min(SWA_SUB, s // BLOCK)
    tq = sub * BLOCK

    def body(sink_ref, qa_ref, ga_ref, k_ref, v_ref, attn_ref, cat_ref, bias_ref):
        n = pl.program_id(0)

        @pl.when(n == 0)
        def _():
            _swa_bias_fill(bias_ref)

        for b in range(sub):
            rows = slice(BLOCK * b, BLOCK * (b + 1))
            start = pl.multiple_of((n * sub + b) * BLOCK, BLOCK)
            kw = k_ref[pl.ds(start, 2 * BLOCK), :]
            vw = v_ref[pl.ds(start, 2 * BLOCK), :]
            bias_t = _swa_bias_of(bias_ref, n, b)
            q = qa_ref[rows, :] * SWA_SCALE
            g = ga_ref[rows, :]
            silu = g * _sigmoid(g)
            for j in range(SWA_KV_HEADS):
                qs = _swa_stack(q, j).astype(MXU_DTYPE)
                probs, _ = _swa_probs_t(qs, _swa_dup(kw, j), bias_t, _swa_sink_row(sink_ref, j))
                pairs = _swa_unstack(_mm_tn(probs.astype(MXU_DTYPE), _swa_dup(vw, j)))
                for i in range(2):
                    lanes = slice(128 * (2 * j + i), 128 * (2 * j + i + 1))
                    attn_ref[rows, lanes] = pairs[i]
                    cat_ref[rows, lanes] = (pairs[i] * silu[:, lanes]).astype(cat_ref.dtype)

    return pl.pallas_call(
        body, name="swa_fwd", grid=(s // tq,),
        in_specs=[pl.BlockSpec(memory_space=pltpu.SMEM), _rows(tq, 512), _rows(tq, 512),
                  _full((s + BLOCK, 128)), _full((s + BLOCK, 128))],
        out_specs=[_rows(tq, 512), _rows(tq, 512)],
        out_shape=[jax.ShapeDtypeStruct((s, 512), F32), jax.ShapeDtypeStruct((s, 512), MXU_DTYPE)],
        scratch_shapes=[pltpu.VMEM(SWA_BIAS_SHAPE, F32)],
        compiler_params=_cparams(dimension_semantics=("arbitrary",)),
    )(sinks, qa, ga, k_pad, v_pad)


GLA_KW = GLA_HEADS * GLA_DK
GLA_VW = GLA_HEADS * GLA_DV


def _idiv(t, d):
    return t >> (d.bit_length() - 1)


def _chunk_cumsum(t, lower):
    n, w = t.shape
    r = lax.broadcasted_iota(jnp.int32, (n, n), 0)
    c = lax.broadcasted_iota(jnp.int32, (n, n), 1)
    tri = ((_idiv(r, GLA_CHUNK) == _idiv(c, GLA_CHUNK)) & ((r >= c) if lower else (r <= c))).astype(MXU_DTYPE)
    parts = _mm(tri, _split3(t))
    return (parts[:, :w] + parts[:, w:2 * w]) + parts[:, 2 * w:]


def _chunk_last(t):
    n = t.shape[0]
    return jnp.concatenate(
        [jnp.broadcast_to(t[c + GLA_CHUNK - 1:c + GLA_CHUNK, :], (GLA_CHUNK, t.shape[1]))
         for c in range(0, n, GLA_CHUNK)], axis=0)


def _head_stack(t, width):
    head = _idiv(lax.broadcasted_iota(jnp.int32, t.shape, 1), width)
    zero = jnp.zeros_like(t)
    return jnp.concatenate([jnp.where(head == h, t, zero) for h in range(GLA_HEADS)], axis=0)


def _heads_to_rows(t):
    return jnp.concatenate([t[:, GLA_DV * h:GLA_DV * (h + 1)] for h in range(GLA_HEADS)], axis=0)


def _rows_to_heads(t):
    return jnp.concatenate([t[GLA_CHUNK * h:GLA_CHUNK * (h + 1)] for h in range(GLA_HEADS)], axis=1)


def _state_by_head(t):
    srow = _idiv(lax.broadcasted_iota(jnp.int32, (GLA_VW, GLA_KW), 0), GLA_DV)
    slane = _idiv(lax.broadcasted_iota(jnp.int32, (GLA_VW, GLA_KW), 1), GLA_DK)
    return jnp.where(srow == slane, jnp.concatenate([t] * GLA_HEADS, axis=0), jnp.zeros((GLA_VW, GLA_KW), t.dtype))


def _gla_masks():
    row = lax.broadcasted_iota(jnp.int32, (GLA_CHUNK, GLA_KW), 0)
    pos = lax.broadcasted_iota(jnp.int32, (GLA_CHUNK, GLA_KW), 1) & (GLA_CHUNK - 1)
    return pos <= row, pos >= row


def _gla_fwd(qb, kb, vb, la, gb, norm_w):
    s = qb.shape[0]
    tb = min(256, s)
    ch = tb // GLA_CHUNK

    def body(qb_ref, kb_ref, vb_ref, la_ref, gb_ref, nw_ref, o_ref, cat_ref, sp_ref, st_ref):
        @pl.when(pl.program_id(0) == 0)
        def _():
            st_ref[...] = jnp.zeros_like(st_ref)

        causal, _ = _gla_masks()
        nw = nw_ref[...]
        b = _chunk_cumsum(la_ref[...], True)
        bl = _chunk_last(b)
        k = kb_ref[...]
        qd = ((qb_ref[...] * GLA_SCALE) * jnp.exp(b)).astype(MXU_DTYPE)
        ki = (k * jnp.exp(-b)).astype(MXU_DTYPE)
        ke = (k * jnp.exp(bl - b)).astype(MXU_DTYPE)
        dec = jnp.exp(bl)
        v = vb_ref[...].astype(MXU_DTYPE)
        g = gb_ref[...]
        silu = g * _sigmoid(g)
        for ci in range(ch):
            rows = slice(GLA_CHUNK * ci, GLA_CHUNK * (ci + 1))
            qds, kis, kes = (_head_stack(t[rows], GLA_DK) for t in (qd, ki, ke))
            a = jnp.where(causal, _mm_nt(qd[rows], kis), 0.0).astype(MXU_DTYPE)
            st = st_ref[...]
            sp_ref[ci] = st
            o = _mm(a, _head_stack(v[rows], GLA_DV)) + _rows_to_heads(_mm_nt(qds, st.astype(MXU_DTYPE)))
            st_ref[...] = st * dec[rows][0:1] + _mm_tn(_heads_to_rows(v[rows]), kes)
            o_ref[rows, :] = o
            for h in range(GLA_HEADS):
                lv = slice(GLA_DV * h, GLA_DV * (h + 1))
                oh = o[:, lv]
                r = lax.rsqrt(jnp.mean(oh * oh, axis=1, keepdims=True) + EPS)
                cat_ref[rows, lv] = (oh * r * nw * silu[rows, lv]).astype(cat_ref.dtype)

    return pl.pallas_call(
        body, name="gla_fwd", grid=(s // tb,),
        in_specs=[_rows(tb, 256), _rows(tb, 256), _rows(tb, 512), _rows(tb, 256), _rows(tb, 512), _full((1, 128))],
        out_specs=[_rows(tb, 512), _rows(tb, 512), pl.BlockSpec((ch, GLA_DV, 256), lambda i: (i, 0, 0))],
        out_shape=[jax.ShapeDtypeStruct((s, 512), F32), jax.ShapeDtypeStruct((s, 512), MXU_DTYPE),
                   jax.ShapeDtypeStruct((s // GLA_CHUNK, GLA_DV, 256), F32)],
        scratch_shapes=[pltpu.VMEM((GLA_DV, GLA_KW), F32)],
        compiler_params=_cparams(dimension_semantics=("arbitrary",)),
    )(qb, kb, vb, la, gb, norm_w)


def _out_ln_loss(cat_a, cat_b, w_out, x, target, ln_g, ln_b):
    s = x.shape[0]
    ts = min(512, s)
    halves = 2 if ts % 32 == 0 else 1
    th = ts // halves

    def body(ca_ref, cb_ref, w_ref, x_ref, t_ref, g_ref, b_ref,
             loss_ref, gx_ref, da_ref, db_ref, gw_ref, gln_ref):
        @pl.when(pl.program_id(0) == 0)
        def _():
            loss_ref[...] = jnp.zeros_like(loss_ref)
            gw_ref[...] = jnp.zeros_like(gw_ref)
            gln_ref[...] = jnp.zeros_like(gln_ref)

        g = g_ref[...]
        dh16s = []
        for k in range(halves):
            rows = slice(th * k, th * (k + 1))
            mix = _mm(ca_ref[rows, :], w_ref[0:512, :]) + _mm(cb_ref[rows, :], w_ref[512:1024, :])
            h = ALPHA * x_ref[rows, :] + mix
            mu = jnp.mean(h, axis=1, keepdims=True)
            hc = h - mu
            rstd = lax.rsqrt(jnp.mean(hc * hc, axis=1, keepdims=True) + EPS)
            xhat = hc * rstd
            err = xhat * g + b_ref[...] - t_ref[rows, :]
            loss_ref[...] += 0.5 * jnp.sum(jnp.mean(err * err, axis=1, keepdims=True))
            dy = err * (1.0 / D_MODEL)
            gln_ref[0:1, :] += jnp.sum(dy * xhat, axis=0, keepdims=True)
            gln_ref[1:2, :] += jnp.sum(dy, axis=0, keepdims=True)
            dxh = dy * g
            dh = rstd * (dxh - jnp.mean(dxh, axis=1, keepdims=True)
                         - xhat * jnp.mean(dxh * xhat, axis=1, keepdims=True))
            gx_ref[rows, :] = ALPHA * dh
            dh16s.append(dh.astype(MXU_DTYPE))
        for k in range(halves):
            rows = slice(th * k, th * (k + 1))
            da_ref[rows, :] = _mm_nt(dh16s[k], w_ref[0:512, :])
            db_ref[rows, :] = _mm_nt(dh16s[k], w_ref[512:1024, :])
        dh16 = jnp.concatenate(dh16s, axis=0)
        gw_ref[0:512, :] += _mm_tn(ca_ref[...], dh16)
        gw_ref[512:1024, :] += _mm_tn(cb_ref[...], dh16)

    return pl.pallas_call(
        body, name="out_ln_loss", grid=(s // ts,),
        in_specs=[_rows(ts, 512), _rows(ts, 512), _full((D_MODEL, D_MODEL)), _rows(ts, D_MODEL), _rows(ts, D_MODEL),
                  _full((1, D_MODEL)), _full((1, D_MODEL))],
        out_specs=[_full((1, 128)), _rows(ts, D_MODEL), _rows(ts, 512), _rows(ts, 512),
                   _full((D_MODEL, D_MODEL)), _full((2, D_MODEL))],
        out_shape=[jax.ShapeDtypeStruct((1, 128), F32), jax.ShapeDtypeStruct((s, D_MODEL), F32),
                   jax.ShapeDtypeStruct((s, 512), F32), jax.ShapeDtypeStruct((s, 512), F32),
                   jax.ShapeDtypeStruct((D_MODEL, D_MODEL), F32), jax.ShapeDtypeStruct((2, D_MODEL), F32)],
        compiler_params=_cparams(dimension_semantics=("arbitrary",)),
    )(cat_a, cat_b, w_out, x, target, ln_g, ln_b)


def _swa_bwd(sinks, qa, k_pad, v_pad, attn, ga, d_cat_a, rope):
    s = qa.shape[0]
    sub = min(SWA_SUB, s // BLOCK)
    tq = sub * BLOCK
    nsteps = s // tq

    def body(sink_ref, qa_ref, ga_ref, at_ref, dc_ref, c_ref, s1_ref, s2_ref, k_ref, v_ref,
             dq_ref, dg_ref, dk_out, dv_out, ds_ref, dk_ref, dv_ref, bias_ref):
        n = pl.program_id(0)

        @pl.when(n == 0)
        def _():
            _swa_bias_fill(bias_ref)
            dk_ref[...] = jnp.zeros_like(dk_ref)
            dv_ref[...] = jnp.zeros_like(dv_ref)
            ds_ref[...] = jnp.zeros_like(ds_ref)

        low = lax.broadcasted_iota(jnp.int32, (2 * BLOCK, 128), 1) < SWA_HEAD_DIM
        for b in range(sub):
            rows = slice(BLOCK * b, BLOCK * (b + 1))
            start = pl.multiple_of((n * sub + b) * BLOCK, BLOCK)
            kw = k_ref[pl.ds(start, 2 * BLOCK), :]
            vw = v_ref[pl.ds(start, 2 * BLOCK), :]
            bias_t = _swa_bias_of(bias_ref, n, b)
            q = qa_ref[rows, :] * SWA_SCALE
            g = ga_ref[rows, :]
            sg = _sigmoid(g)
            o = at_ref[rows, :]
            dc = dc_ref[rows, :]
            do = dc * (g * sg)
            dg_ref[rows, :] = (dc * o * (sg * (1.0 + g * (1.0 - sg)))).astype(dg_ref.dtype)
            od = do * o
            c, s1, s2 = c_ref[rows, :], s1_ref[rows, :], s2_ref[rows, :]
            dk, dv = [], []
            for j in range(SWA_KV_HEADS):
                kd, vd = _swa_dup(kw, j), _swa_dup(vw, j)
                qs = _swa_stack(q, j).astype(MXU_DTYPE)
                dos = _swa_stack(do, j).astype(MXU_DTYPE)
                probs, psink = _swa_probs_t(qs, kd, bias_t, _swa_sink_row(sink_ref, j))
                delta = _row_sums_as_row(_swa_stack(od, j))
                dsc = (probs * (_mm_nt(vd, dos) - delta)).astype(MXU_DTYPE)
                dsink = psink * delta
                for r in range(SWA_GROUP):
                    h = SWA_GROUP * j + r
                    ds_ref[h:h + 1, :] += jnp.zeros((1, 128), F32) - jnp.sum(dsink[:, BLOCK * r:BLOCK * (r + 1)])
                dq = _swa_unstack(_mm_tn(dsc, kd))
                for i in range(2):
                    lanes = slice(128 * (2 * j + i), 128 * (2 * j + i + 1))
                    dq_ref[rows, lanes] = _rope_t(dq[i] * SWA_SCALE, c, s1, s2).astype(dq_ref.dtype)
                dkj = _mm(dsc, qs)
                dvj = _mm(probs.astype(MXU_DTYPE), dos)
                dk.append(dkj + pltpu.roll(dkj, SWA_HEAD_DIM, 1))
                dv.append(dvj + pltpu.roll(dvj, SWA_HEAD_DIM, 1))
            dk_ref[pl.ds(start, 2 * BLOCK), :] += jnp.where(low, dk[0], dk[1])
            dv_ref[pl.ds(start, 2 * BLOCK), :] += jnp.where(low, dv[0], dv[1])

        @pl.when(n == nsteps - 1)
        def _():
            dk_out[...] = dk_ref[BLOCK:, :]
            dv_out[...] = dv_ref[BLOCK:, :]

    return pl.pallas_call(
        body, name="swa_bwd", grid=(nsteps,),
        in_specs=[pl.BlockSpec(memory_space=pltpu.SMEM)] + [_rows(tq, 512)] * 4 + [_rows(tq, 128)] * 3
        + [_full((s + BLOCK, 128))] * 2,
        out_specs=[_rows(tq, 512), _rows(tq, 512), _full((s, 128)), _full((s, 128)), _full((SWA_Q_HEADS, 128))],
        out_shape=[jax.ShapeDtypeStruct((s, 512), MXU_DTYPE), jax.ShapeDtypeStruct((s, 512), MXU_DTYPE),
                   jax.ShapeDtypeStruct((s, 128), F32), jax.ShapeDtypeStruct((s, 128), F32),
                   jax.ShapeDtypeStruct((SWA_Q_HEADS, 128), F32)],
        scratch_shapes=[pltpu.VMEM((s + BLOCK, 128), F32)] * 2 + [pltpu.VMEM(SWA_BIAS_SHAPE, F32)],
        compiler_params=_cparams(dimension_semantics=("arbitrary",)),
    )(sinks, qa, ga, attn, d_cat_a, *rope, k_pad, v_pad)


def _gla_bwd(qb, kb, vb, la, oms, gb, o, sprev, d_cat_b, rb, wg, norm_w):
    s = qb.shape[0]
    tb = min(512, s)
    ch = tb // GLA_CHUNK
    nb = s // tb

    def body(qb_ref, kb_ref, vb_ref, la_ref, oms_ref, gb_ref, o_ref, sp_ref, dc_ref, rb_ref, wg_ref, nw_ref,
             dq_ref, dk_ref, dv_ref, dg_ref, dr_ref, gwg_ref, gbg_ref, gnw_ref, dst_ref):
        @pl.when(pl.program_id(0) == 0)
        def _():
            dst_ref[...] = jnp.zeros_like(dst_ref)
            gwg_ref[...] = jnp.zeros_like(gwg_ref)
            gbg_ref[...] = jnp.zeros_like(gbg_ref)
            gnw_ref[...] = jnp.zeros_like(gnw_ref)

        causal, causal_t = _gla_masks()
        nw = nw_ref[...]
        b = _chunk_cumsum(la_ref[...], True)
        bl = _chunk_last(b)
        eb, enb, ee, dec = jnp.exp(b), jnp.exp(-b), jnp.exp(bl - b), jnp.exp(bl)
        k = kb_ref[...]
        qd = (qb_ref[...] * GLA_SCALE) * eb
        ki = k * enb
        ke = k * ee
        qd16, ki16, ke16 = qd.astype(MXU_DTYPE), ki.astype(MXU_DTYPE), ke.astype(MXU_DTYPE)
        v16 = vb_ref[...].astype(MXU_DTYPE)

        g = gb_ref[...]
        sg = _sigmoid(g)
        silu = g * sg
        dsilu = sg * (1.0 + g * (1.0 - sg))
        gnw = jnp.zeros((1, GLA_DV), F32)
        do = []
        for h in range(GLA_HEADS):
            lv = slice(GLA_DV * h, GLA_DV * (h + 1))
            oh = o_ref[:, lv]
            dch = dc_ref[:, lv]
            r = lax.rsqrt(jnp.mean(oh * oh, axis=1, keepdims=True) + EPS)
            d_on = dch * silu[:, lv]
            dg_ref[:, lv] = (dch * (oh * r * nw) * dsilu[:, lv]).astype(dg_ref.dtype)
            gnw += jnp.sum(d_on * oh * r, axis=0, keepdims=True)
            u = d_on * nw
            do.append(r * u - oh * (r * r * r) * jnp.mean(u * oh, axis=1, keepdims=True))
        gnw_ref[...] += gnw
        do16 = jnp.concatenate(do, axis=1).astype(MXU_DTYPE)

        db, dbl = [None] * ch, [None] * ch
        for ci in reversed(range(ch)):
            rows = slice(GLA_CHUNK * ci, GLA_CHUNK * (ci + 1))
            qds, kis, kes = (_head_stack(t[rows], GLA_DK) for t in (qd16, ki16, ke16))
            vs, dos = _head_stack(v16[rows], GLA_DV), _head_stack(do16[rows], GLA_DV)
            a = jnp.where(causal, _mm_nt(qd16[rows], kis), 0.0).astype(MXU_DTYPE)
            at = jnp.where(causal_t, _mm_nt(ki16[rows], qds), 0.0).astype(MXU_DTYPE)
            da = jnp.where(causal, _mm_nt(do16[rows], vs), 0.0).astype(MXU_DTYPE)
            dat = jnp.where(causal_t, _mm_nt(v16[rows], dos), 0.0).astype(MXU_DTYPE)
            st = sp_ref[ci]
            dst = dst_ref[...]
            dst16 = dst.astype(MXU_DTYPE)
            dv = _mm(at, dos) + _rows_to_heads(_mm_nt(kes, dst16))
            dqd = _mm(da, kis) + _mm(do16[rows], _state_by_head(st.astype(MXU_DTYPE)))
            dki = _mm(dat, qds)
            dke = _mm(v16[rows], _state_by_head(dst16))
            ddec = jnp.sum(dst * st, axis=0, keepdims=True)
            decc = dec[rows][0:1]
            dst_ref[...] = _mm_tn(_heads_to_rows(do16[rows]), qds) + dst * decc
            dq_ref[rows, :] = (dqd * eb[rows] * GLA_SCALE).astype(dq_ref.dtype)
            dk_ref[rows, :] = (dki * enb[rows] + dke * ee[rows]).astype(dk_ref.dtype)
            dv_ref[rows, :] = dv.astype(dv_ref.dtype)
            dke_ke = dke * ke[rows]
            db[ci] = dqd * qd[rows] - dki * ki[rows] - dke_ke
            dbl[ci] = jnp.broadcast_to(jnp.sum(dke_ke, axis=0, keepdims=True) + ddec * decc, (GLA_CHUNK, GLA_KW))

        dla = _chunk_cumsum(jnp.concatenate(db, axis=0), False) + jnp.concatenate(dbl, axis=0)
        dlogit = dla * oms_ref[...] * (1.0 / GLA_TAU)
        dl16 = dlogit.astype(MXU_DTYPE)
        gbg_ref[...] += jnp.sum(dlogit, axis=0, keepdims=True)
        gwg = _mm_tn(rb_ref[...].astype(MXU_DTYPE), dl16)
        for j in range(N_DEV):
            gwg_ref[j] += gwg[:, 32 * j:32 * (j + 1)]
        dr_ref[...] = _mm_nt(dl16, wg_ref[...]).astype(dr_ref.dtype)

    def rev(width):
        return pl.BlockSpec((tb, width), lambda i: (nb - 1 - i, 0))

    return pl.pallas_call(
        body, name="gla_bwd", grid=(nb,),
        in_specs=[rev(256), rev(256), rev(512), rev(256), rev(256), rev(512), rev(512),
                  pl.BlockSpec((ch, GLA_DV, 256), lambda i: (nb - 1 - i, 0, 0)), rev(512), rev(GLA_RANK),
                  _full((GLA_RANK, 256)), _full((1, 128))],
        out_specs=[rev(256), rev(256), rev(512), rev(512), rev(GLA_RANK),
                   _full((N_DEV, GLA_RANK, 32)), _full((1, 256)), _full((1, 128))],
        out_shape=[jax.ShapeDtypeStruct((s, 256), MXU_DTYPE), jax.ShapeDtypeStruct((s, 256), MXU_DTYPE),
                   jax.ShapeDtypeStruct((s, 512), MXU_DTYPE), jax.ShapeDtypeStruct((s, 512), MXU_DTYPE),
                   jax.ShapeDtypeStruct((s, GLA_RANK), MXU_DTYPE), jax.ShapeDtypeStruct((N_DEV, GLA_RANK, 32), F32),
                   jax.ShapeDtypeStruct((1, 256), F32), jax.ShapeDtypeStruct((1, 128), F32)],
        scratch_shapes=[pltpu.VMEM((GLA_DV, GLA_KW), F32)],
        compiler_params=_cparams(dimension_semantics=("arbitrary",)),
    )(qb, kb, vb, la, oms, gb, o, sprev, d_cat_b, rb, wg, norm_w)


def _dproj_tiles(piece_refs, rope_refs):
    for i in (0, 1, 3, 4, 5, 6, 7, 8):
        if i == 1:
            dk = _rope_t(piece_refs[1][...], *(r[...] for r in rope_refs))
            yield OFF[1], OFF[3], jnp.concatenate([dk, piece_refs[2][...]], axis=1).astype(MXU_DTYPE)
        else:
            yield OFF[i], OFF[i + 1], piece_refs[i][...].astype(MXU_DTYPE)


def _in_proj_bwd_x(gx0, pieces, w_in, rope):
    s = gx0.shape[0]
    ts = min(512, s)
    widths = [OFF[i + 1] - OFF[i] for i in range(9)]

    def body(gx0_ref, *refs):
        w_ref, gx_ref = refs[12:]
        acc = gx0_ref[...]
        for lo, hi, t16 in _dproj_tiles(refs[:9], refs[9:12]):
            acc += _mm(t16, w_ref[lo:hi, :])
        gx_ref[...] = acc

    return pl.pallas_call(
        body, name="in_proj_bwd_x", grid=(s // ts,),
        in_specs=[_rows(ts, D_MODEL)] + [_rows(ts, w) for w in widths] + [_rows(ts, 128)] * 3
        + [_full((D_IN_PROJ, D_MODEL))],
        out_specs=_rows(ts, D_MODEL),
        out_shape=jax.ShapeDtypeStruct((s, D_MODEL), F32),
        compiler_params=_cparams(dimension_semantics=("arbitrary",)),
    )(gx0, *pieces, *rope, w_in)


GW_PASSES = 2
GW_EVENT_STEPS = ((0, 4, 10, 14), (0, 2, 5, 7))


def _in_proj_bwd_w(x, pieces, rope, parts_wg, g_ln, g_bg, g_nw, g_sinks, loss, parts_w_out, gx0, w_in):
    s = x.shape[0]
    ts = min(512, s)
    nt = s // ts
    n_pass = GW_PASSES
    n_steps = (n_pass + 1) * nt
    cw = D_MODEL // n_pass
    n_chips = N_DEV // 2
    blk = (D_IN_SHARD, cw)
    out_blk = parts_w_out.shape[1:]
    w_out_forward_step = min(2, n_steps - 1)
    w_out_finish_step = max(min(14, n_steps - 2), w_out_forward_step)

    def body(x_ref, *refs):
        piece_refs, rope_refs = refs[:9], refs[9:12]
        (pwg_ref, gln_ref, gbg_ref, gnw_ref, gsk_ref, loss_ref, pout_ref, gx0_ref, win_ref,
         gin_ref, rwg_ref, rsm_ref, gout_ref, gx_ref) = refs[12:26]
        (acc_ref, stage_ref, sib_ref, snd_ref, rcv_ref, sm_ref,
         d2d_send, d2d_recv, ici_send, ici_recv, out_sems, sm_send, sm_recv, sm_loc) = refs[26:40]
        w_out_sum = _OwnerSum(pout_ref, *refs[40:])
        p, t = pl.program_id(0), pl.program_id(1)
        step = p * nt + t
        x_, y_, c = _mesh_pos()
        me, mychip, sibling = 4 * x_ + 2 * y_ + c, 2 * x_ + y_, (x_, y_, 1 - c)
        small_dsts = (rwg_ref, rsm_ref)

        def small_src(a, block):
            return pwg_ref.at[block] if a == 0 else sm_ref

        def small_copy(k, a, src_block, dst_block, peer):
            i = 2 * (k - 1) + a
            return pltpu.make_async_remote_copy(
                src_ref=small_src(a, src_block), dst_ref=small_dsts[a].at[dst_block], send_sem=sm_send.at[i],
                recv_sem=sm_recv.at[i], device_id=peer, device_id_type=pl.DeviceIdType.MESH)

        def small_local(a):
            return pltpu.make_async_copy(small_src(a, me), small_dsts[a].at[me], sm_loc.at[a])

        @pl.when(step == 0)
        def _():
            w_out_sum.start()
            acc_ref[...] = jnp.zeros_like(acc_ref)
            sm_ref[...] = jnp.zeros_like(sm_ref)
            for r in range(D_MODEL // 128):
                sm_ref[r:r + 1, :] = gln_ref[0:1, 128 * r:128 * (r + 1)]
                sm_ref[8 + r:9 + r, :] = gln_ref[1:2, 128 * r:128 * (r + 1)]
            for r in range(2):
                sm_ref[16 + r:17 + r, :] = gbg_ref[0:1, 128 * r:128 * (r + 1)]
            sm_ref[24:25, :] = gnw_ref[...]
            diag = (lax.broadcasted_iota(jnp.int32, gsk_ref.shape, 0)
                    == lax.broadcasted_iota(jnp.int32, gsk_ref.shape, 1))
            sm_ref[32:33, :] = jnp.sum(jnp.where(diag, gsk_ref[...], 0.0), axis=0, keepdims=True)
            sm_ref[40:41, :] = loss_ref[...]
            for a in range(2):
                small_local(a).start()
            for k in range(1, N_DEV):
                peer, pidx = _peer(k, x_, y_, c)
                for a in range(2):
                    small_copy(k, a, pidx, me, peer).start()

        @pl.when(p < n_pass)
        def _():
            xb = x_ref[...]
            for lo, hi, t16 in _dproj_tiles(piece_refs, rope_refs):
                acc_ref[p, lo:hi, :] += _mm_tn(t16, xb)

        @pl.when(p == n_pass)
        def _():
            gx = gx0_ref[...]
            for lo, hi, t16 in _dproj_tiles(piece_refs, rope_refs):
                gx += _mm(t16, win_ref[lo:hi, :])
            gx_ref[...] = gx

        @pl.when(step == w_out_forward_step)
        def _():
            w_out_sum.forward()

        @pl.when(step == w_out_finish_step)
        def _():
            gout_ref[...] = w_out_sum.finish()

        def block_rows(q, j):
            return acc_ref[q, D_IN_SHARD * j:D_IN_SHARD * (j + 1), :]

        def d2d(q):
            return pltpu.make_async_remote_copy(
                src_ref=stage_ref, dst_ref=sib_ref.at[q], send_sem=d2d_send.at[q],
                recv_sem=d2d_recv.at[q], device_id=sibling, device_id_type=pl.DeviceIdType.MESH)

        def ici(q, slot, owner):
            i = 3 * q + slot
            return pltpu.make_async_remote_copy(
                src_ref=snd_ref.at[q, slot], dst_ref=rcv_ref.at[q, slot], send_sem=ici_send.at[i],
                recv_sem=ici_recv.at[i], device_id=owner, device_id_type=pl.DeviceIdType.MESH)

        def out_copy(q):
            return pltpu.make_async_copy(sib_ref.at[q, mychip], gin_ref.at[:, pl.ds(q * cw, cw)], out_sems.at[q])

        first = (jnp.where(c == 0, 1 - x_, x_), jnp.where(c == 0, y_, 1 - y_))
        second = (jnp.where(c == 0, x_, 1 - x_), jnp.where(c == 0, 1 - y_, y_))

        def chip_of(pos):
            return 2 * pos[0] + pos[1]

        def to_sibling(q):
            if q >= 1:
                d2d(q - 1).wait_send()
            for cc in range(2):
                @pl.when(c == cc)
                def _(cc=cc):
                    for k in range(n_chips):
                        stage_ref[k] = block_rows(q, 2 * k + 1 - cc)
            d2d(q).start()

        def chip_sums_leave(q):
            d2d(q).wait_recv()
            for cc in range(2):
                @pl.when(c == cc)
                def _(cc=cc):
                    for k in range(n_chips):
                        sib_ref[q, k] = block_rows(q, 2 * k + cc) + sib_ref[q, k]
            snd_ref[q, 2] = sib_ref[q, chip_of((1 - x_, 1 - y_))].astype(snd_ref.dtype)
            ici(q, 2, (*first, c)).start()
            snd_ref[q, 0] = sib_ref[q, chip_of(first)].astype(snd_ref.dtype)
            ici(q, 0, (*first, c)).start()

        def combined_sum_leaves(q):
            ici(q, 2, sibling).wait_recv()
            snd_ref[q, 1] = (sib_ref[q, chip_of(second)] + rcv_ref[q, 2].astype(F32)).astype(snd_ref.dtype)
            ici(q, 1, (*second, c)).start()

        def owner_total(q):
            total = sib_ref[q, mychip]
            for slot in range(2):
                ici(q, slot, sibling).wait_recv()
                total = total + rcv_ref[q, slot].astype(F32)
            sib_ref[q, mychip] = total
            out_copy(q).start()

        stages = (to_sibling, chip_sums_leave, combined_sum_leaves, owner_total)
        events = sorted((min((q + 1) * nt - 1 + GW_EVENT_STEPS[q][i], n_steps - 1), i > 0, 2 * q + i, -q, q, i)
                        for q in range(n_pass) for i in range(len(stages)))
        for at_step, _, _, _, q, i in events:
            @pl.when(step == at_step)
            def _(q=q, i=i):
                stages[i](q)

        @pl.when(step == n_steps - 1)
        def _():
            for k in range(1, N_DEV):
                peer, pidx = _peer(k, x_, y_, c)
                for a in range(2):
                    small_copy(k, a, me, pidx, peer).wait_recv()
            for k in range(1, N_DEV):
                peer, pidx = _peer(k, x_, y_, c)
                for a in range(2):
                    small_copy(k, a, pidx, me, peer).wait_send()
            for a in range(2):
                small_local(a).wait()
            d2d(n_pass - 1).wait_send()
            for q in range(n_pass):
                for slot in range(3):
                    ici(q, slot, sibling).wait_send()
            for q in range(n_pass):
                out_copy(q).wait()

    widths = [OFF[i + 1] - OFF[i] for i in range(9)]
    hbm = pl.BlockSpec(memory_space=pl.ANY)
    vmem = pl.BlockSpec(memory_space=pltpu.VMEM)

    def token_tile(width):
        return pl.BlockSpec((ts, width), lambda p, t: (t, 0))

    x_spec = pl.BlockSpec((ts, cw), lambda p, t: (jnp.where(p < n_pass, t, nt - 1), jnp.minimum(p, n_pass - 1)))
    gx_spec = pl.BlockSpec((ts, D_MODEL), lambda p, t: (jnp.where(p == n_pass, t, 0), 0))
    return pl.pallas_call(
        body, name="in_proj_bwd_w", grid=(n_pass + 1, nt),
        in_specs=[x_spec] + [token_tile(w) for w in widths] + [token_tile(128)] * 3 + [hbm] + [vmem] * 5
        + [hbm, gx_spec, vmem],
        out_specs=[hbm, hbm, hbm, vmem, gx_spec],
        out_shape=[jax.ShapeDtypeStruct((D_IN_SHARD, D_MODEL), F32),
                   jax.ShapeDtypeStruct((N_DEV,) + parts_wg.shape[1:], F32),
                   jax.ShapeDtypeStruct((N_DEV, SMALL_ROWS, 128), F32), jax.ShapeDtypeStruct(out_blk, F32),
                   jax.ShapeDtypeStruct((s, D_MODEL), F32)],
        scratch_shapes=[pltpu.VMEM((n_pass, D_IN_PROJ, cw), F32), pltpu.VMEM((n_chips,) + blk, F32),
                        pltpu.VMEM((n_pass, n_chips) + blk, F32), pltpu.VMEM((n_pass, 3) + blk, MXU_DTYPE),
                        pltpu.VMEM((n_pass, 3) + blk, MXU_DTYPE),
                        pltpu.VMEM((SMALL_ROWS, 128), F32),
                        pltpu.SemaphoreType.DMA((n_pass,)), pltpu.SemaphoreType.DMA((n_pass,)),
                        pltpu.SemaphoreType.DMA((3 * n_pass,)), pltpu.SemaphoreType.DMA((3 * n_pass,)),
                        pltpu.SemaphoreType.DMA((n_pass,)),
                        pltpu.SemaphoreType.DMA((2 * (N_DEV - 1),)), pltpu.SemaphoreType.DMA((2 * (N_DEV - 1),)),
                        pltpu.SemaphoreType.DMA((2,))]
        + _OwnerSum.scratch(out_blk),
        compiler_params=_cparams(dimension_semantics=("arbitrary", "arbitrary")),
    )(x, *pieces, *rope, parts_wg, g_ln, g_bg, g_nw, g_sinks, loss, parts_w_out, gx0, w_in)


def _local_step(x, positions, w_in_t, wg_s, b_gate, sinks, norm_w, w_out_s, ln_g, ln_b, target):
    qa, k_pad, v_pad, ga, qb, kb, vb, gb, rb, la, oms, *rope, x16, w_in, wg, w_out = _in_proj(
        x, w_in_t, wg_s, b_gate, _rope_angles(positions), w_out_s)
    attn, cat_a = _swa_fwd(sinks, qa, k_pad, v_pad, ga)
    o, cat_b, sprev = _gla_fwd(qb, kb, vb, la, gb, norm_w)
    loss, gx0, d_cat_a, d_cat_b, g_w_out, g_ln = _out_ln_loss(cat_a, cat_b, w_out, x, target, ln_g, ln_b)
    parts_w_out = g_w_out.reshape(N_DEV, D_OUT_SHARD, D_MODEL)
    dqa, dga, dka, dva, g_sinks = _swa_bwd(sinks, qa, k_pad, v_pad, attn, ga, d_cat_a, rope)
    dqb, dkb, dvb, dgb, drb, parts_wg, g_bg, g_nw = _gla_bwd(
        qb, kb, vb, la, oms, gb, o, sprev, d_cat_b, rb, wg, norm_w)
    pieces = (dqa, dka, dva, dga, dqb, dkb, dvb, dgb, drb)
    g_in, r_wg, r_small, g_out, grad_x = _in_proj_bwd_w(
        x16, pieces, rope, parts_wg, g_ln, g_bg, g_nw, g_sinks, loss, parts_w_out, gx0, w_in)
    return grad_x, g_in, g_out, r_wg, r_small


def _mesh_pos():
    return lax.axis_index("x"), lax.axis_index("y"), lax.axis_index("c")


def _peer(k, x, y, c):
    px = (1 - x) if k & 4 else x
    py = (1 - y) if k & 2 else y
    pc = (1 - c) if k & 1 else c
    return (px, py, pc), 4 * px + 2 * py + pc


def _other_chips(x, y):
    return [(1 - x, y), (x, 1 - y), (1 - x, 1 - y)]


def _shard_view(t):
    return jnp.transpose(t, (2, 0, 1))


class _BlockGather:
    def __init__(self, slots, send_sems, recv_sems):
        self.slots, self.send_sems, self.recv_sems = slots, send_sems, recv_sems
        x, y, c = _mesh_pos()
        self.c, self.me, self.sibling = c, 4 * x + 2 * y + c, (x, y, 1 - c)
        first = (jnp.where(c == 0, 1 - x, x), jnp.where(c == 0, y, 1 - y))
        second = (jnp.where(c == 0, x, 1 - x), jnp.where(c == 0, 1 - y, y))
        self.chips = [first, second, (1 - x, 1 - y)]

    @staticmethod
    def scratch():
        return [pltpu.SemaphoreType.DMA((N_DEV - 1,)), pltpu.SemaphoreType.DMA((N_DEV - 1,))]

    def _copy(self, k, block, to):
        return pltpu.make_async_remote_copy(
            src_ref=self.slots.at[block], dst_ref=self.slots.at[block], send_sem=self.send_sems.at[k],
            recv_sem=self.recv_sems.at[k], device_id=to, device_id_type=pl.DeviceIdType.MESH)

    def _block(self, j, c):
        cx, cy = self.chips[j]
        return 4 * cx + 2 * cy + c

    def _dev(self, j):
        return (*self.chips[j], self.c)

    def start(self):
        self._copy(1, self.me, self._dev(0)).start()
        self._copy(2, self.me, self._dev(1)).start()
        self._copy(0, self.me, self.sibling).start()

    def forward(self):
        self._copy(1, self._block(0, self.c), self.sibling).wait_recv()
        self._copy(3, self._block(0, self.c), self._dev(1)).start()
        self._copy(4, self._block(0, self.c), self.sibling).start()
        self._copy(2, self._block(1, self.c), self.sibling).wait_recv()
        self._copy(5, self._block(1, self.c), self.sibling).start()

    def forward_far(self):
        self._copy(3, self._block(2, self.c), self.sibling).wait_recv()
        self._copy(6, self._block(2, self.c), self.sibling).start()

    def finish(self):
        for k in (0, 4, 5, 6):
            self._copy(k, self.me, self.sibling).wait_recv()
        for k in range(N_DEV - 1):
            self._copy(k, self.me, self.sibling).wait_send()


class _OwnerSum:
    def __init__(self, parts, own, sib, snd, rcv, loc_sems, d2d_send, d2d_recv, ici_send, ici_recv):
        self.parts, self.own, self.sib, self.snd, self.rcv = parts, own, sib, snd, rcv
        self.sems = (loc_sems, d2d_send, d2d_recv, ici_send, ici_recv)
        x, y, c = _mesh_pos()
        self.c, self.sibling = c, (x, y, 1 - c)
        self.chips = [(x, y)] + _other_chips(x, y)

    @staticmethod
    def scratch(block):
        return [pltpu.VMEM((4,) + block, F32), pltpu.VMEM((4,) + block, F32),
                pltpu.VMEM((3,) + block, MXU_DTYPE), pltpu.VMEM((3,) + block, MXU_DTYPE),
                pltpu.SemaphoreType.DMA((4,)), pltpu.SemaphoreType.DMA((4,)), pltpu.SemaphoreType.DMA((4,)),
                pltpu.SemaphoreType.DMA((3,)), pltpu.SemaphoreType.DMA((3,))]

    def _local(self, r):
        cx, cy = self.chips[r]
        return pltpu.make_async_copy(self.parts.at[4 * cx + 2 * cy + self.c], self.own.at[r], self.sems[0].at[r])

    def _d2d(self, r):
        cx, cy = self.chips[r]
        return pltpu.make_async_remote_copy(
            src_ref=self.parts.at[4 * cx + 2 * cy + (1 - self.c)], dst_ref=self.sib.at[r], send_sem=self.sems[1].at[r],
            recv_sem=self.sems[2].at[r], device_id=self.sibling, device_id_type=pl.DeviceIdType.MESH)

    def _ici(self, r):
        cx, cy = self.chips[r]
        return pltpu.make_async_remote_copy(
            src_ref=self.snd.at[r - 1], dst_ref=self.rcv.at[r - 1], send_sem=self.sems[3].at[r - 1],
            recv_sem=self.sems[4].at[r - 1], device_id=(cx, cy, self.c), device_id_type=pl.DeviceIdType.MESH)

    def start(self):
        for r in (1, 2, 3, 0):
            self._local(r).start()
            self._d2d(r).start()

    def forward(self):
        for r in (1, 2, 3):
            self._local(r).wait()
            self._d2d(r).wait_recv()
            self.snd[r - 1] = (self.own[r] + self.sib[r]).astype(self.snd.dtype)
            self._ici(r).start()

    def finish(self):
        self._local(0).wait()
        self._d2d(0).wait_recv()
        acc = self.own[0] + self.sib[0]
        for r in (1, 2, 3):
            self._ici(r).wait_recv()
            acc = acc + self.rcv[r - 1].astype(F32)
        for r in range(4):
            self._d2d(r).wait_send()
        for r in (1, 2, 3):
            self._ici(r).wait_send()
        return acc


SMALL_ROWS = 48


def _adamw_math(g, w, m, v):
    nm = ADAM_B1 * m + (1.0 - ADAM_B1) * g
    nv = ADAM_B2 * v + (1.0 - ADAM_B2) * (g * g)
    m_hat = nm / (1.0 - ADAM_B1 ** ADAM_STEP)
    v_hat = nv / (1.0 - ADAM_B2 ** ADAM_STEP)
    return -ADAM_LR * (m_hat / (jnp.sqrt(v_hat) + ADAM_EPS) + ADAM_WD * w), nm, nv


def _adamw(g_in, w_in_wmv, r_small, r_wg, g_out, params):
    n_par = len(params)
    rows, width = g_in.shape
    parts = 3
    pr = rows // parts
    assert pr * parts == rows

    def body(gin_ref, w_hbm, m_hbm, v_hbm, rsm_ref, rwg_ref, gout_ref, *refs):
        ins, outs = refs[:3 * n_par], refs[3 * n_par:7 * n_par + 1]
        big_outs = refs[7 * n_par + 1:7 * n_par + 5]
        bufs, stage, sems = refs[7 * n_par + 5:]

        def load(k, i, src):
            return pltpu.make_async_copy(src.at[pl.ds(pr * k, pr), 0, :], bufs.at[k, i], sems.at[7 * k + i])

        def store(k, i, dst):
            return pltpu.make_async_copy(stage.at[k, i], dst.at[pl.ds(pr * k, pr), 0, :], sems.at[7 * k + 3 + i])

        for k in range(parts):
            for i, src in enumerate((w_hbm, m_hbm, v_hbm)):
                load(k, i, src).start()

        g = rsm_ref[0]
        gwg = rwg_ref[0]
        for j in range(1, N_DEV):
            g = g + rsm_ref[j]
            gwg = gwg + rwg_ref[j]
        outs[4 * n_par][...] = g[40:41]
        grads = [gwg,
                 jnp.concatenate([g[r:r + 1] for r in range(0, 8)], axis=1),
                 jnp.concatenate([g[r:r + 1] for r in range(8, 16)], axis=1),
                 jnp.concatenate([g[16:17], g[17:18]], axis=1),
                 g[24:25],
                 g[32:33, 0:SWA_Q_HEADS],
                 gout_ref[...]]
        for p, gp in enumerate(grads):
            w_ref, m_ref, v_ref = ins[3 * p:3 * p + 3]
            outs[4 * p][...] = gp
            outs[4 * p + 1][...], outs[4 * p + 2][...], outs[4 * p + 3][...] = _adamw_math(
                gp, w_ref[...], m_ref[...], v_ref[...])

        for k in range(parts):
            for i, src in enumerate((w_hbm, m_hbm, v_hbm)):
                load(k, i, src).wait()
            gk = gin_ref[pr * k:pr * (k + 1), :]
            stage[k, 0] = gk
            stage[k, 1], stage[k, 2], stage[k, 3] = _adamw_math(gk, bufs[k, 0], bufs[k, 1], bufs[k, 2])
            for i, dst in enumerate(big_outs):
                store(k, i, dst).start()
        for k in range(parts):
            for i, dst in enumerate(big_outs):
                store(k, i, dst).wait()

    vmem = pl.BlockSpec(memory_space=pltpu.VMEM)
    hbm = pl.BlockSpec(memory_space=pl.ANY)
    flat = [t for wmv in params for t in wmv]
    res = pl.pallas_call(
        body, name="adamw",
        in_specs=[vmem, hbm, hbm, hbm] + [vmem] * (3 + len(flat)),
        out_specs=[vmem] * (4 * n_par + 1) + [hbm] * 4,
        out_shape=[jax.ShapeDtypeStruct(wmv[0].shape, F32) for wmv in params for _ in range(4)]
        + [jax.ShapeDtypeStruct((1, 128), F32)] + [jax.ShapeDtypeStruct((rows, 1, width), F32)] * 4,
        scratch_shapes=[pltpu.VMEM((parts, 3, pr, width), F32), pltpu.VMEM((parts, 4, pr, width), F32),
                        pltpu.SemaphoreType.DMA((7 * parts,))],
        compiler_params=_cparams(),
    )(g_in, *w_in_wmv, r_small, r_wg, g_out, *flat)
    return res[:4 * n_par + 1], res[4 * n_par + 1:]


def kernel(x, positions, w_in, gla_w_gate_up, gla_b_gate, attn_sinks, gla_norm_w, w_out, ln_g, ln_b, loss_target, m_w_in, m_gla_w_gate_up, m_gla_b_gate, m_attn_sinks, m_gla_norm_w, m_w_out, m_ln_g, m_ln_b, v_w_in, v_gla_w_gate_up, v_gla_b_gate, v_attn_sinks, v_gla_norm_w, v_w_out, v_ln_g, v_ln_b):
    grad_x, g_in, g_out, r_wg, r_small = _local_step(
        x[0], positions[0], _shard_view(w_in), gla_w_gate_up[0], gla_b_gate, attn_sinks[0], gla_norm_w, w_out[0],
        ln_g, ln_b, loss_target[0])

    vec, upd_in = _adamw(g_in, (_shard_view(w_in), _shard_view(m_w_in), _shard_view(v_w_in)), r_small, r_wg, g_out, [
        (gla_w_gate_up[0], m_gla_w_gate_up[0], v_gla_w_gate_up[0]), (ln_g, m_ln_g, v_ln_g), (ln_b, m_ln_b, v_ln_b),
        (gla_b_gate, m_gla_b_gate, v_gla_b_gate), (gla_norm_w, m_gla_norm_w, v_gla_norm_w),
        (attn_sinks, m_attn_sinks, v_attn_sinks), (w_out[0], m_w_out[0], v_w_out[0])])
    upd_in = [jnp.transpose(t, (1, 2, 0)) for t in upd_in]

    outs = [vec[28][0, 0], grad_x[None]]
    for kind in range(4):
        u_wg, u_ln_g, u_ln_b, u_bg, u_nw, u_sinks, u_out = (vec[4 * p + kind] for p in range(7))
        outs += [upd_in[kind], u_wg[None], u_bg, u_sinks, u_nw, u_out[None], u_ln_g, u_ln_b]
    return tuple(outs)
```
